```python
import math
import jax, jax.numpy as jnp
from jax import lax
import numpy as np

D_MODEL = 1024
BATCH = 8
SEQ = 2048
DEPTH = 4

N_EVEN = (DEPTH + 1) // 2
N_ODD = DEPTH // 2
EPS = 1e-6
NEG_INF = -1e30
Q_BLOCK = 128
FOX_WIDTH = D_MODEL // 2
FOX_HEAD_DIM = 64
FOX_HEADS = FOX_WIDTH // FOX_HEAD_DIM
SC_WIDTH = D_MODEL - FOX_WIDTH
SC_GROUPS = SC_WIDTH // 64
SC_K = 3
AB_IN = 3 * FOX_WIDTH + FOX_HEADS + 3 * SC_WIDTH
LRU_WIDTH = D_MODEL
LRU_BW = 256
LRU_BLOCKS = LRU_WIDTH // LRU_BW
RG_CONV_K = 4
RG_C = 8.0
RG_MIN_RAD = 0.9
RG_MAX_RAD = 0.999
MEM_LEN = 256
MEM_HEADS = 4
MEM_HEAD_DIM = D_MODEL // MEM_HEADS
D_FF = ((8 * D_MODEL // 3 + 255) // 256) * 256

kernel_name = "fox_shortconv_rglru_sandwich_hybrid"


def rmsnorm(x, g):
    x32 = x.astype(jnp.float32)
    y = x32 * lax.rsqrt(jnp.mean(x32 * x32, axis=-1, keepdims=True) + EPS)
    return y.astype(x.dtype) * g


def causal_depthwise_conv(x, w):
    k_width, ch = w.shape
    return lax.conv_general_dilated(
        x, w[:, None, :], window_strides=(1,), padding=[(k_width - 1, 0)],
        dimension_numbers=("NWC", "WIO", "NWC"), feature_group_count=ch)


def forgetting_attention(q, k, v, log_f):
    seq = q.shape[1]
    scale = q.shape[-1] ** -0.5
    cum = jnp.cumsum(log_f, axis=1).transpose(0, 2, 1)
    outs = []
    for blk in range(seq // Q_BLOCK):
        lo, hi = blk * Q_BLOCK, (blk + 1) * Q_BLOCK
        s = jnp.einsum("bqhd,bkhd->bhqk", q[:, lo:hi], k[:, :hi],
                       preferred_element_type=jnp.float32) * scale
        s = s + cum[:, :, lo:hi, None] - cum[:, :, None, :hi]
        causal = (lo + jnp.arange(Q_BLOCK))[:, None] >= jnp.arange(hi)[None, :]
        s = jnp.where(causal, s, NEG_INF)
        p = jax.nn.softmax(s, axis=-1).astype(v.dtype)
        outs.append(jnp.einsum("bhqk,bkhd->bqhd", p, v[:, :hi]))
    return jnp.concatenate(outs, axis=1)


def fox_shortconv_mixer(h, w_in, b_f, conv_w, w_out):
    bsz, seq, _ = h.shape
    proj = h @ w_in
    i1 = FOX_WIDTH
    i2 = 2 * FOX_WIDTH
    i3 = 3 * FOX_WIDTH
    i4 = i3 + FOX_HEADS
    i5 = i4 + SC_WIDTH
    i6 = i5 + SC_WIDTH
    q, k, v, f_logit, b_gate, c_gate, u = jnp.split(proj, [i1, i2, i3, i4, i5, i6], axis=-1)
    heads = lambda t: t.reshape(bsz, seq, FOX_HEADS, FOX_HEAD_DIM)
    log_f = jax.nn.log_sigmoid((f_logit + b_f).astype(jnp.float32))
    y_a = forgetting_attention(heads(q), heads(k), heads(v), log_f).reshape(bsz, seq, FOX_WIDTH)
    y_b = b_gate * causal_depthwise_conv(c_gate * u, conv_w)
    return jnp.concatenate([y_a, y_b], axis=-1) @ w_out


def _lru_combine(c1, c2):
    a1, b1 = c1
    a2, b2 = c2
    return a1 * a2, a2 * b1 + b2


def rglru_mixer(h, w_in, conv_w, conv_b, w_a, b_a, w_i, b_i, lam, w_out):
    bsz, seq, _ = h.shape
    gate, u = jnp.split(h @ w_in, 2, axis=-1)
    u = causal_depthwise_conv(u, conv_w) + conv_b
    ub = u.reshape(bsz, seq, LRU_BLOCKS, LRU_BW)
    r = jax.nn.sigmoid(jnp.einsum("bsnc,ncd->bsnd", ub, w_a) + b_a).reshape(bsz, seq, LRU_WIDTH)
    i = jax.nn.sigmoid(jnp.einsum("bsnc,ncd->bsnd", ub, w_i) + b_i).reshape(bsz, seq, LRU_WIDTH)
    log_a = -RG_C * r.astype(jnp.float32) * jax.nn.softplus(-lam.astype(jnp.float32))
    a = jnp.exp(log_a)
    b = jnp.sqrt(-jnp.expm1(2.0 * log_a)) * (i * u).astype(jnp.float32)
    _, hs = lax.associative_scan(_lru_combine, (a, b), axis=1)
    y = jax.nn.gelu(gate) * hs.astype(h.dtype)
    return y @ w_out


def memory_cross_attention(h, m, w_q, w_kv, w_o):
    bsz, seq, _ = h.shape
    mlen = m.shape[1]
    q = (h @ w_q).reshape(bsz, seq, MEM_HEADS, MEM_HEAD_DIM)
    k, v = jnp.split(m @ w_kv, 2, axis=-1)
    k = k.reshape(bsz, mlen, MEM_HEADS, MEM_HEAD_DIM)
    v = v.reshape(bsz, mlen, MEM_HEADS, MEM_HEAD_DIM)
    s = jnp.einsum("bqhd,bkhd->bhqk", q, k, preferred_element_type=jnp.float32) * (MEM_HEAD_DIM ** -0.5)
    p = jax.nn.softmax(s, axis=-1).astype(v.dtype)
    o = jnp.einsum("bhqk,bkhd->bqhd", p, v).reshape(bsz, seq, D_MODEL)
    return o @ w_o


def swiglu(h, w_gu, w_down):
    g, u = jnp.split(h @ w_gu, 2, axis=-1)
    return (jax.nn.silu(g) * u) @ w_down


def _fwd_setup_inputs(seed: int = 0) -> dict:
    key = jax.random.key(seed)
    ks = iter(jax.random.split(key, 40))
    dense = lambda shape, fan_in: jax.random.normal(next(ks), shape, jnp.float32) * (fan_in ** -0.5)
    gain = lambda shape: 1.0 + 0.02 * jax.random.normal(next(ks), shape, jnp.float32)
    small = lambda shape: 0.01 * jax.random.normal(next(ks), shape, jnp.float32)
    x = jax.random.normal(next(ks), (BATCH, SEQ, D_MODEL), jnp.float32)
    mem = jax.random.normal(next(ks), (BATCH, MEM_LEN, D_MODEL), jnp.float32)
    rad = jax.random.uniform(next(ks), (N_ODD, LRU_WIDTH), jnp.float32, RG_MIN_RAD, RG_MAX_RAD)
    c_lam = -jnp.log(jnp.expm1(-jnp.log(rad) / RG_C))
    return {
        "x": x,
        "mem": mem,
        "g_mix_pre": gain((DEPTH, D_MODEL)),
        "g_mix_post": gain((DEPTH, D_MODEL)),
        "g_cross_pre": gain((DEPTH, D_MODEL)),
        "g_mem": gain((DEPTH, D_MODEL)),
        "g_cross_post": gain((DEPTH, D_MODEL)),
        "g_ffn_pre": gain((DEPTH, D_MODEL)),
        "g_ffn_post": gain((DEPTH, D_MODEL)),
        "w_xq": dense((DEPTH, D_MODEL, D_MODEL), D_MODEL),
        "w_xkv": dense((DEPTH, D_MODEL, 2 * D_MODEL), D_MODEL),
        "w_xo": dense((DEPTH, D_MODEL, D_MODEL), D_MODEL),
        "w_ffn_gu": dense((DEPTH, D_MODEL, 2 * D_FF), D_MODEL),
        "w_ffn_down": dense((DEPTH, D_FF, D_MODEL), D_FF),
        "ab_w_in": dense((N_EVEN, D_MODEL, AB_IN), D_MODEL),
        "ab_b_f": jax.random.uniform(next(ks), (N_EVEN, FOX_HEADS), jnp.float32, 2.0, 5.0),
        "ab_conv_w": dense((N_EVEN, SC_K, SC_WIDTH), SC_K),
        "ab_w_out": dense((N_EVEN, D_MODEL, D_MODEL), D_MODEL),
        "c_w_in": dense((N_ODD, D_MODEL, 2 * LRU_WIDTH), D_MODEL),
        "c_conv_w": dense((N_ODD, RG_CONV_K, LRU_WIDTH), RG_CONV_K),
        "c_conv_b": small((N_ODD, LRU_WIDTH)),
        "c_w_a": dense((N_ODD, LRU_BLOCKS, LRU_BW, LRU_BW), LRU_BW),
        "c_b_a": small((N_ODD, LRU_BLOCKS, LRU_BW)),
        "c_w_i": dense((N_ODD, LRU_BLOCKS, LRU_BW, LRU_BW), LRU_BW),
        "c_b_i": small((N_ODD, LRU_BLOCKS, LRU_BW)),
        "c_lam": c_lam,
        "c_w_out": dense((N_ODD, LRU_WIDTH, D_MODEL), LRU_WIDTH),
    }


def _fwd_reference(x, mem, g_mix_pre, g_mix_post, g_cross_pre, g_mem, g_cross_post,
              g_ffn_pre, g_ffn_post, w_xq, w_xkv, w_xo, w_ffn_gu, w_ffn_down,
              ab_w_in, ab_b_f, ab_conv_w, ab_w_out,
              c_w_in, c_conv_w, c_conv_b, c_w_a, c_b_a, c_w_i, c_b_i, c_lam, c_w_out):
    for layer in range(DEPTH):
        h = rmsnorm(x, g_mix_pre[layer])
        if layer % 2 == 0:
            e = layer // 2
            y = fox_shortconv_mixer(h, ab_w_in[e], ab_b_f[e], ab_conv_w[e], ab_w_out[e])
        else:
            o = layer // 2
            y = rglru_mixer(h, c_w_in[o], c_conv_w[o], c_conv_b[o], c_w_a[o], c_b_a[o],
                            c_w_i[o], c_b_i[o], c_lam[o], c_w_out[o])
        x = x + rmsnorm(y, g_mix_post[layer])
        h = rmsnorm(x, g_cross_pre[layer])
        m = rmsnorm(mem, g_mem[layer])
        y = memory_cross_attention(h, m, w_xq[layer], w_xkv[layer], w_xo[layer])
        x = x + rmsnorm(y, g_cross_post[layer])
        h = rmsnorm(x, g_ffn_pre[layer])
        y = swiglu(h, w_ffn_gu[layer], w_ffn_down[layer])
        x = x + rmsnorm(y, g_ffn_post[layer])
    return x


import jax as _jax
import jax.numpy as _jnp

TWIN_FORMAT = 'train_step'
FWD_PARAMS = ['x', 'mem', 'g_mix_pre', 'g_mix_post', 'g_cross_pre', 'g_mem', 'g_cross_post', 'g_ffn_pre', 'g_ffn_post', 'w_xq', 'w_xkv', 'w_xo', 'w_ffn_gu', 'w_ffn_down', 'ab_w_in', 'ab_b_f', 'ab_conv_w', 'ab_w_out', 'c_w_in', 'c_conv_w', 'c_conv_b', 'c_w_a', 'c_b_a', 'c_w_i', 'c_b_i', 'c_lam', 'c_w_out']
TWIN_WEIGHTS = ['g_mix_pre', 'g_mix_post', 'g_cross_pre', 'g_mem', 'g_cross_post', 'g_ffn_pre', 'g_ffn_post', 'w_xq', 'w_xkv', 'w_xo', 'w_ffn_gu', 'w_ffn_down', 'ab_w_in', 'ab_b_f', 'ab_conv_w', 'ab_w_out', 'c_w_in', 'c_conv_w', 'c_conv_b', 'c_w_a', 'c_b_a', 'c_w_i', 'c_b_i', 'c_lam', 'c_w_out']
TWIN_DIFF_INPUT = 'x'
TWIN_INPUTS = ['x', 'mem', 'g_mix_pre', 'g_mix_post', 'g_cross_pre', 'g_mem', 'g_cross_post', 'g_ffn_pre', 'g_ffn_post', 'w_xq', 'w_xkv', 'w_xo', 'w_ffn_gu', 'w_ffn_down', 'ab_w_in', 'ab_b_f', 'ab_conv_w', 'ab_w_out', 'c_w_in', 'c_conv_w', 'c_conv_b', 'c_w_a', 'c_b_a', 'c_w_i', 'c_b_i', 'c_lam', 'c_w_out', 'loss_target', 'm_g_mix_pre', 'm_g_mix_post', 'm_g_cross_pre', 'm_g_mem', 'm_g_cross_post', 'm_g_ffn_pre', 'm_g_ffn_post', 'm_w_xq', 'm_w_xkv', 'm_w_xo', 'm_w_ffn_gu', 'm_w_ffn_down', 'm_ab_w_in', 'm_ab_b_f', 'm_ab_conv_w', 'm_ab_w_out', 'm_c_w_in', 'm_c_conv_w', 'm_c_conv_b', 'm_c_w_a', 'm_c_b_a', 'm_c_w_i', 'm_c_b_i', 'm_c_lam', 'm_c_w_out', 'v_g_mix_pre', 'v_g_mix_post', 'v_g_cross_pre', 'v_g_mem', 'v_g_cross_post', 'v_g_ffn_pre', 'v_g_ffn_post', 'v_w_xq', 'v_w_xkv', 'v_w_xo', 'v_w_ffn_gu', 'v_w_ffn_down', 'v_ab_w_in', 'v_ab_b_f', 'v_ab_conv_w', 'v_ab_w_out', 'v_c_w_in', 'v_c_conv_w', 'v_c_conv_b', 'v_c_w_a', 'v_c_b_a', 'v_c_w_i', 'v_c_b_i', 'v_c_lam', 'v_c_w_out']
TWIN_OUTPUTS = ['loss', 'grad_x', 'grad_g_mix_pre', 'grad_g_mix_post', 'grad_g_cross_pre', 'grad_g_mem', 'grad_g_cross_post', 'grad_g_ffn_pre', 'grad_g_ffn_post', 'grad_w_xq', 'grad_w_xkv', 'grad_w_xo', 'grad_w_ffn_gu', 'grad_w_ffn_down', 'grad_ab_w_in', 'grad_ab_b_f', 'grad_ab_conv_w', 'grad_ab_w_out', 'grad_c_w_in', 'grad_c_conv_w', 'grad_c_conv_b', 'grad_c_w_a', 'grad_c_b_a', 'grad_c_w_i', 'grad_c_b_i', 'grad_c_lam', 'grad_c_w_out', 'delta_g_mix_pre', 'delta_g_mix_post', 'delta_g_cross_pre', 'delta_g_mem', 'delta_g_cross_post', 'delta_g_ffn_pre', 'delta_g_ffn_post', 'delta_w_xq', 'delta_w_xkv', 'delta_w_xo', 'delta_w_ffn_gu', 'delta_w_ffn_down', 'delta_ab_w_in', 'delta_ab_b_f', 'delta_ab_conv_w', 'delta_ab_w_out', 'delta_c_w_in', 'delta_c_conv_w', 'delta_c_conv_b', 'delta_c_w_a', 'delta_c_b_a', 'delta_c_w_i', 'delta_c_b_i', 'delta_c_lam', 'delta_c_w_out', 'new_m_g_mix_pre', 'new_m_g_mix_post', 'new_m_g_cross_pre', 'new_m_g_mem', 'new_m_g_cross_post', 'new_m_g_ffn_pre', 'new_m_g_ffn_post', 'new_m_w_xq', 'new_m_w_xkv', 'new_m_w_xo', 'new_m_w_ffn_gu', 'new_m_w_ffn_down', 'new_m_ab_w_in', 'new_m_ab_b_f', 'new_m_ab_conv_w', 'new_m_ab_w_out', 'new_m_c_w_in', 'new_m_c_conv_w', 'new_m_c_conv_b', 'new_m_c_w_a', 'new_m_c_b_a', 'new_m_c_w_i', 'new_m_c_b_i', 'new_m_c_lam', 'new_m_c_w_out', 'new_v_g_mix_pre', 'new_v_g_mix_post', 'new_v_g_cross_pre', 'new_v_g_mem', 'new_v_g_cross_post', 'new_v_g_ffn_pre', 'new_v_g_ffn_post', 'new_v_w_xq', 'new_v_w_xkv', 'new_v_w_xo', 'new_v_w_ffn_gu', 'new_v_w_ffn_down', 'new_v_ab_w_in', 'new_v_ab_b_f', 'new_v_ab_conv_w', 'new_v_ab_w_out', 'new_v_c_w_in', 'new_v_c_conv_w', 'new_v_c_conv_b', 'new_v_c_w_a', 'new_v_c_b_a', 'new_v_c_w_i', 'new_v_c_b_i', 'new_v_c_lam', 'new_v_c_w_out']
TWIN_LEAF_KINDS = {'loss': 'loss', 'grad_x': 'grad_x', 'grad_g_mix_pre': 'grad_w', 'grad_g_mix_post': 'grad_w', 'grad_g_cross_pre': 'grad_w', 'grad_g_mem': 'grad_w', 'grad_g_cross_post': 'grad_w', 'grad_g_ffn_pre': 'grad_w', 'grad_g_ffn_post': 'grad_w', 'grad_w_xq': 'grad_w', 'grad_w_xkv': 'grad_w', 'grad_w_xo': 'grad_w', 'grad_w_ffn_gu': 'grad_w', 'grad_w_ffn_down': 'grad_w', 'grad_ab_w_in': 'grad_w', 'grad_ab_b_f': 'grad_w', 'grad_ab_conv_w': 'grad_w', 'grad_ab_w_out': 'grad_w', 'grad_c_w_in': 'grad_w', 'grad_c_conv_w': 'grad_w', 'grad_c_conv_b': 'grad_w', 'grad_c_w_a': 'grad_w', 'grad_c_b_a': 'grad_w', 'grad_c_w_i': 'grad_w', 'grad_c_b_i': 'grad_w', 'grad_c_lam': 'grad_w', 'grad_c_w_out': 'grad_w', 'delta_g_mix_pre': 'delta_w', 'delta_g_mix_post': 'delta_w', 'delta_g_cross_pre': 'delta_w', 'delta_g_mem': 'delta_w', 'delta_g_cross_post': 'delta_w', 'delta_g_ffn_pre': 'delta_w', 'delta_g_ffn_post': 'delta_w', 'delta_w_xq': 'delta_w', 'delta_w_xkv': 'delta_w', 'delta_w_xo': 'delta_w', 'delta_w_ffn_gu': 'delta_w', 'delta_w_ffn_down': 'delta_w', 'delta_ab_w_in': 'delta_w', 'delta_ab_b_f': 'delta_w', 'delta_ab_conv_w': 'delta_w', 'delta_ab_w_out': 'delta_w', 'delta_c_w_in': 'delta_w', 'delta_c_conv_w': 'delta_w', 'delta_c_conv_b': 'delta_w', 'delta_c_w_a': 'delta_w', 'delta_c_b_a': 'delta_w', 'delta_c_w_i': 'delta_w', 'delta_c_b_i': 'delta_w', 'delta_c_lam': 'delta_w', 'delta_c_w_out': 'delta_w', 'new_m_g_mix_pre': 'new_m', 'new_m_g_mix_post': 'new_m', 'new_m_g_cross_pre': 'new_m', 'new_m_g_mem': 'new_m', 'new_m_g_cross_post': 'new_m', 'new_m_g_ffn_pre': 'new_m', 'new_m_g_ffn_post': 'new_m', 'new_m_w_xq': 'new_m', 'new_m_w_xkv': 'new_m', 'new_m_w_xo': 'new_m', 'new_m_w_ffn_gu': 'new_m', 'new_m_w_ffn_down': 'new_m', 'new_m_ab_w_in': 'new_m', 'new_m_ab_b_f': 'new_m', 'new_m_ab_conv_w': 'new_m', 'new_m_ab_w_out': 'new_m', 'new_m_c_w_in': 'new_m', 'new_m_c_conv_w': 'new_m', 'new_m_c_conv_b': 'new_m', 'new_m_c_w_a': 'new_m', 'new_m_c_b_a': 'new_m', 'new_m_c_w_i': 'new_m', 'new_m_c_b_i': 'new_m', 'new_m_c_lam': 'new_m', 'new_m_c_w_out': 'new_m', 'new_v_g_mix_pre': 'new_v', 'new_v_g_mix_post': 'new_v', 'new_v_g_cross_pre': 'new_v', 'new_v_g_mem': 'new_v', 'new_v_g_cross_post': 'new_v', 'new_v_g_ffn_pre': 'new_v', 'new_v_g_ffn_post': 'new_v', 'new_v_w_xq': 'new_v', 'new_v_w_xkv': 'new_v', 'new_v_w_xo': 'new_v', 'new_v_w_ffn_gu': 'new_v', 'new_v_w_ffn_down': 'new_v', 'new_v_ab_w_in': 'new_v', 'new_v_ab_b_f': 'new_v', 'new_v_ab_conv_w': 'new_v', 'new_v_ab_w_out': 'new_v', 'new_v_c_w_in': 'new_v', 'new_v_c_conv_w': 'new_v', 'new_v_c_conv_b': 'new_v', 'new_v_c_w_a': 'new_v', 'new_v_c_b_a': 'new_v', 'new_v_c_w_i': 'new_v', 'new_v_c_b_i': 'new_v', 'new_v_c_lam': 'new_v', 'new_v_c_w_out': 'new_v'}


def _forward(args):
    return _fwd_reference(*[args[k] for k in FWD_PARAMS])


def _output_shape():
    out = _jax.eval_shape(lambda: _forward(_fwd_setup_inputs(0)))
    return out.shape, out.dtype

N_MICROBATCH = 1
ADAM_LR = 0.001
ADAM_B1 = 0.9
ADAM_B2 = 0.999
ADAM_EPS = 1e-08
ADAM_WD = 0.01
ADAM_STEP = 10
PER_EXAMPLE_BATCH_AXIS = {'x': 0, 'mem': 0, 'loss_target': 0}
SHARED_INPUTS = []
_WEIGHT_DTYPES = {'g_mix_pre': _jnp.float32, 'g_mix_post': _jnp.float32, 'g_cross_pre': _jnp.float32, 'g_mem': _jnp.float32, 'g_cross_post': _jnp.float32, 'g_ffn_pre': _jnp.float32, 'g_ffn_post': _jnp.float32, 'w_xq': _jnp.float32, 'w_xkv': _jnp.float32, 'w_xo': _jnp.float32, 'w_ffn_gu': _jnp.float32, 'w_ffn_down': _jnp.float32, 'ab_w_in': _jnp.float32, 'ab_b_f': _jnp.float32, 'ab_conv_w': _jnp.float32, 'ab_w_out': _jnp.float32, 'c_w_in': _jnp.float32, 'c_conv_w': _jnp.float32, 'c_conv_b': _jnp.float32, 'c_w_a': _jnp.float32, 'c_b_a': _jnp.float32, 'c_w_i': _jnp.float32, 'c_b_i': _jnp.float32, 'c_lam': _jnp.float32, 'c_w_out': _jnp.float32}
MOMENT_SCALE = {'g_mix_pre': 5.268842e+00, 'g_mix_post': 1.673664e+01, 'g_cross_pre': 2.596095e+00, 'g_mem': 9.721508e+00, 'g_cross_post': 1.853185e+01, 'g_ffn_pre': 3.324903e+00, 'g_ffn_post': 1.608378e+01, 'w_xq': 2.624274e+00, 'w_xkv': 6.687725e+00, 'w_xo': 9.372546e+00, 'w_ffn_gu': 1.355568e+00, 'w_ffn_down': 2.652733e+00, 'ab_w_in': 1.923712e+00, 'ab_b_f': 2.414531e+00, 'ab_conv_w': 1.888714e+00, 'ab_w_out': 2.948075e+00, 'c_w_in': 5.176312e+00, 'c_conv_w': 6.330701e+00, 'c_conv_b': 1.954935e+01, 'c_w_a': 4.018768e-01, 'c_b_a': 8.071409e-01, 'c_w_i': 9.567567e-01, 'c_b_i': 2.661854e+00, 'c_lam': 1.910657e+00, 'c_w_out': 6.538436e+00}


def _to_microbatches(a, axis):
    t = _jnp.moveaxis(a, axis, 0)
    t = t.reshape((N_MICROBATCH, t.shape[0] // N_MICROBATCH) + t.shape[1:])
    return _jnp.moveaxis(t, 1, axis + 1)


def setup_inputs(seed: int = 0) -> dict:
    inp = _fwd_setup_inputs(seed)
    key = _jax.random.fold_in(_jax.random.key(seed), 7919)
    shape, _ = _output_shape()
    out = dict(inp)
    out["loss_target"] = _jax.random.normal(_jax.random.fold_in(key, 0), shape, _jnp.float32)
    for i, name in enumerate(TWIN_WEIGHTS):
        w = inp[name].astype(_jnp.float32)
        if MOMENT_SCALE is None:
            s = _jnp.sqrt(_jnp.mean(_jnp.square(w)) + 1e-30)
        else:
            s = MOMENT_SCALE[name]
        km, kv = _jax.random.split(_jax.random.fold_in(key, i + 1))
        out[name] = w
        out["m_" + name] = s * _jax.random.normal(km, w.shape, _jnp.float32)
        out["v_" + name] = (s * s) * _jax.random.uniform(kv, w.shape, _jnp.float32, 0.5, 1.5)
    if N_MICROBATCH > 1:
        for name, axis in PER_EXAMPLE_BATCH_AXIS.items():
            out[name] = _to_microbatches(out[name], axis)
    return {'x': out['x'], 'mem': out['mem'], 'g_mix_pre': out['g_mix_pre'], 'g_mix_post': out['g_mix_post'], 'g_cross_pre': out['g_cross_pre'], 'g_mem': out['g_mem'], 'g_cross_post': out['g_cross_post'], 'g_ffn_pre': out['g_ffn_pre'], 'g_ffn_post': out['g_ffn_post'], 'w_xq': out['w_xq'], 'w_xkv': out['w_xkv'], 'w_xo': out['w_xo'], 'w_ffn_gu': out['w_ffn_gu'], 'w_ffn_down': out['w_ffn_down'], 'ab_w_in': out['ab_w_in'], 'ab_b_f': out['ab_b_f'], 'ab_conv_w': out['ab_conv_w'], 'ab_w_out': out['ab_w_out'], 'c_w_in': out['c_w_in'], 'c_conv_w': out['c_conv_w'], 'c_conv_b': out['c_conv_b'], 'c_w_a': out['c_w_a'], 'c_b_a': out['c_b_a'], 'c_w_i': out['c_w_i'], 'c_b_i': out['c_b_i'], 'c_lam': out['c_lam'], 'c_w_out': out['c_w_out'], 'loss_target': out['loss_target'], 'm_g_mix_pre': out['m_g_mix_pre'], 'm_g_mix_post': out['m_g_mix_post'], 'm_g_cross_pre': out['m_g_cross_pre'], 'm_g_mem': out['m_g_mem'], 'm_g_cross_post': out['m_g_cross_post'], 'm_g_ffn_pre': out['m_g_ffn_pre'], 'm_g_ffn_post': out['m_g_ffn_post'], 'm_w_xq': out['m_w_xq'], 'm_w_xkv': out['m_w_xkv'], 'm_w_xo': out['m_w_xo'], 'm_w_ffn_gu': out['m_w_ffn_gu'], 'm_w_ffn_down': out['m_w_ffn_down'], 'm_ab_w_in': out['m_ab_w_in'], 'm_ab_b_f': out['m_ab_b_f'], 'm_ab_conv_w': out['m_ab_conv_w'], 'm_ab_w_out': out['m_ab_w_out'], 'm_c_w_in': out['m_c_w_in'], 'm_c_conv_w': out['m_c_conv_w'], 'm_c_conv_b': out['m_c_conv_b'], 'm_c_w_a': out['m_c_w_a'], 'm_c_b_a': out['m_c_b_a'], 'm_c_w_i': out['m_c_w_i'], 'm_c_b_i': out['m_c_b_i'], 'm_c_lam': out['m_c_lam'], 'm_c_w_out': out['m_c_w_out'], 'v_g_mix_pre': out['v_g_mix_pre'], 'v_g_mix_post': out['v_g_mix_post'], 'v_g_cross_pre': out['v_g_cross_pre'], 'v_g_mem': out['v_g_mem'], 'v_g_cross_post': out['v_g_cross_post'], 'v_g_ffn_pre': out['v_g_ffn_pre'], 'v_g_ffn_post': out['v_g_ffn_post'], 'v_w_xq': out['v_w_xq'], 'v_w_xkv': out['v_w_xkv'], 'v_w_xo': out['v_w_xo'], 'v_w_ffn_gu': out['v_w_ffn_gu'], 'v_w_ffn_down': out['v_w_ffn_down'], 'v_ab_w_in': out['v_ab_w_in'], 'v_ab_b_f': out['v_ab_b_f'], 'v_ab_conv_w': out['v_ab_conv_w'], 'v_ab_w_out': out['v_ab_w_out'], 'v_c_w_in': out['v_c_w_in'], 'v_c_conv_w': out['v_c_conv_w'], 'v_c_conv_b': out['v_c_conv_b'], 'v_c_w_a': out['v_c_w_a'], 'v_c_b_a': out['v_c_b_a'], 'v_c_w_i': out['v_c_w_i'], 'v_c_b_i': out['v_c_b_i'], 'v_c_lam': out['v_c_lam'], 'v_c_w_out': out['v_c_w_out']}


def _loss(weights, diff, rest, loss_target):
    with _jax.named_scope("forward"):
        args = {**rest, TWIN_DIFF_INPUT: diff, **{k: w.astype(_WEIGHT_DTYPES[k]) for k, w in weights.items()}}
        y = _forward(args)
    with _jax.named_scope("loss_head"):
        err = _jnp.square(y.astype(_jnp.float32) - loss_target)
        return 0.5 * _jnp.sum(_jnp.mean(err, axis=-1)) if err.ndim else 0.5 * err


def _adamw(w, g, m, v):
    m = ADAM_B1 * m + (1.0 - ADAM_B1) * g
    v = ADAM_B2 * v + (1.0 - ADAM_B2) * _jnp.square(g)
    m_hat = m / (1.0 - ADAM_B1 ** ADAM_STEP)
    v_hat = v / (1.0 - ADAM_B2 ** ADAM_STEP)
    delta = -ADAM_LR * (m_hat / (_jnp.sqrt(v_hat) + ADAM_EPS) + ADAM_WD * w)
    return delta, m, v


def reference(x, mem, g_mix_pre, g_mix_post, g_cross_pre, g_mem, g_cross_post, g_ffn_pre, g_ffn_post, w_xq, w_xkv, w_xo, w_ffn_gu, w_ffn_down, ab_w_in, ab_b_f, ab_conv_w, ab_w_out, c_w_in, c_conv_w, c_conv_b, c_w_a, c_b_a, c_w_i, c_b_i, c_lam, c_w_out, loss_target, m_g_mix_pre, m_g_mix_post, m_g_cross_pre, m_g_mem, m_g_cross_post, m_g_ffn_pre, m_g_ffn_post, m_w_xq, m_w_xkv, m_w_xo, m_w_ffn_gu, m_w_ffn_down, m_ab_w_in, m_ab_b_f, m_ab_conv_w, m_ab_w_out, m_c_w_in, m_c_conv_w, m_c_conv_b, m_c_w_a, m_c_b_a, m_c_w_i, m_c_b_i, m_c_lam, m_c_w_out, v_g_mix_pre, v_g_mix_post, v_g_cross_pre, v_g_mem, v_g_cross_post, v_g_ffn_pre, v_g_ffn_post, v_w_xq, v_w_xkv, v_w_xo, v_w_ffn_gu, v_w_ffn_down, v_ab_w_in, v_ab_b_f, v_ab_conv_w, v_ab_w_out, v_c_w_in, v_c_conv_w, v_c_conv_b, v_c_w_a, v_c_b_a, v_c_w_i, v_c_b_i, v_c_lam, v_c_w_out):
    given = dict(x=x, mem=mem, g_mix_pre=g_mix_pre, g_mix_post=g_mix_post, g_cross_pre=g_cross_pre, g_mem=g_mem, g_cross_post=g_cross_post, g_ffn_pre=g_ffn_pre, g_ffn_post=g_ffn_post, w_xq=w_xq, w_xkv=w_xkv, w_xo=w_xo, w_ffn_gu=w_ffn_gu, w_ffn_down=w_ffn_down, ab_w_in=ab_w_in, ab_b_f=ab_b_f, ab_conv_w=ab_conv_w, ab_w_out=ab_w_out, c_w_in=c_w_in, c_conv_w=c_conv_w, c_conv_b=c_conv_b, c_w_a=c_w_a, c_b_a=c_b_a, c_w_i=c_w_i, c_b_i=c_b_i, c_lam=c_lam, c_w_out=c_w_out, loss_target=loss_target, m_g_mix_pre=m_g_mix_pre, m_g_mix_post=m_g_mix_post, m_g_cross_pre=m_g_cross_pre, m_g_mem=m_g_mem, m_g_cross_post=m_g_cross_post, m_g_ffn_pre=m_g_ffn_pre, m_g_ffn_post=m_g_ffn_post, m_w_xq=m_w_xq, m_w_xkv=m_w_xkv, m_w_xo=m_w_xo, m_w_ffn_gu=m_w_ffn_gu, m_w_ffn_down=m_w_ffn_down, m_ab_w_in=m_ab_w_in, m_ab_b_f=m_ab_b_f, m_ab_conv_w=m_ab_conv_w, m_ab_w_out=m_ab_w_out, m_c_w_in=m_c_w_in, m_c_conv_w=m_c_conv_w, m_c_conv_b=m_c_conv_b, m_c_w_a=m_c_w_a, m_c_b_a=m_c_b_a, m_c_w_i=m_c_w_i, m_c_b_i=m_c_b_i, m_c_lam=m_c_lam, m_c_w_out=m_c_w_out, v_g_mix_pre=v_g_mix_pre, v_g_mix_post=v_g_mix_post, v_g_cross_pre=v_g_cross_pre, v_g_mem=v_g_mem, v_g_cross_post=v_g_cross_post, v_g_ffn_pre=v_g_ffn_pre, v_g_ffn_post=v_g_ffn_post, v_w_xq=v_w_xq, v_w_xkv=v_w_xkv, v_w_xo=v_w_xo, v_w_ffn_gu=v_w_ffn_gu, v_w_ffn_down=v_w_ffn_down, v_ab_w_in=v_ab_w_in, v_ab_b_f=v_ab_b_f, v_ab_conv_w=v_ab_conv_w, v_ab_w_out=v_ab_w_out, v_c_w_in=v_c_w_in, v_c_conv_w=v_c_conv_w, v_c_conv_b=v_c_conv_b, v_c_w_a=v_c_w_a, v_c_b_a=v_c_b_a, v_c_w_i=v_c_w_i, v_c_b_i=v_c_b_i, v_c_lam=v_c_lam, v_c_w_out=v_c_w_out)
    weights = {n: given[n] for n in TWIN_WEIGHTS}
    shared = {n: given[n] for n in SHARED_INPUTS}
    per_example = {n: given[n] for n in ['x', 'mem']}
    grad_fn = _jax.value_and_grad(_loss, argnums=(0, 1))

    def one_microbatch(ex, loss_target):
        ex = dict(ex)
        diff = ex.pop(TWIN_DIFF_INPUT)
        return grad_fn(weights, diff, {**shared, **ex}, loss_target)

    if N_MICROBATCH == 1:
        loss, (grad_w, grad_x) = one_microbatch(per_example, given["loss_target"])
    else:
        def body(carry, xs):
            loss_sum, grad_sum = carry
            l_k, (gw_k, gx_k) = one_microbatch(xs[0], xs[1])
            with _jax.named_scope("update"):
                return (loss_sum + l_k, _jax.tree.map(_jnp.add, grad_sum, gw_k)), gx_k

        init = (_jnp.zeros((), _jnp.float32), _jax.tree.map(_jnp.zeros_like, weights))
        (loss, grad_w), grad_x = _jax.lax.scan(body, init, (per_example, given["loss_target"]))
    with _jax.named_scope("update"):
        delta_w, new_m, new_v = {}, {}, {}
        for n in TWIN_WEIGHTS:
            delta_w[n], new_m[n], new_v[n] = _adamw(weights[n], grad_w[n], given["m_" + n], given["v_" + n])
    return (loss, grad_x, *[grad_w[n] for n in TWIN_WEIGHTS], *[delta_w[n] for n in TWIN_WEIGHTS],
            *[new_m[n] for n in TWIN_WEIGHTS], *[new_v[n] for n in TWIN_WEIGHTS])
```

```python
import functools
import math

import jax
import jax.numpy as jnp
from jax import lax
from jax.experimental import pallas as pl
from jax.experimental.pallas import tpu as pltpu

F32 = jnp.float32
BF16 = jnp.bfloat16
BS = pl.BlockSpec
ANY = pl.BlockSpec(memory_space=pl.ANY)
MESH = pl.DeviceIdType.MESH

DM = 1024
DEPTH = 4
EPS = 1e-6
NEG = -1e30
FOX_W = 512
FOX_HD = 64
FOX_H = 8
SC_W = 512
SC_K = 3
AB_IN = 3 * FOX_W + FOX_H + 3 * SC_W
AB_PAD = 3200
LRU_BW = 256
LRU_NB = 4
RG_K = 4
RG_C = 8.0
MEM_H = 4
MEM_HD = 256
D_FF = 2816
NDEV = 8
FFB = 2 * D_FF // NDEV
ADAM_LR, ADAM_B1, ADAM_B2, ADAM_EPS, ADAM_WD, ADAM_STEP = 0.001, 0.9, 0.999, 1e-08, 0.01, 10

LANE = 128
VMEM_LIMIT = 48 * 1024 * 1024


def _params(ngrid):
    return pltpu.CompilerParams(dimension_semantics=("arbitrary",) * ngrid, vmem_limit_bytes=VMEM_LIMIT)


def _tile(n, t):
    return t if n % t == 0 else n


def _mm(name, a, b, *, grid, a_spec, b_spec, o_spec, out_shape, dn, out_dtype=F32):
    nred = grid[-1]
    ngrid = len(grid)

    def kern(a_ref, b_ref, o_ref, *scratch):
        p = lax.dot_general(a_ref[...].astype(BF16), b_ref[...].astype(BF16), (dn, ((), ())),
                            preferred_element_type=F32)
        if nred == 1:
            o_ref[...] = p.astype(o_ref.dtype)
            return
        acc = scratch[0] if scratch else o_ref
        r = pl.program_id(ngrid - 1)

        @pl.when(r == 0)
        def _():
            acc[...] = p

        @pl.when(r > 0)
        def _():
            acc[...] += p

        if scratch:
            @pl.when(r == nred - 1)
            def _():
                o_ref[...] = acc[...].astype(o_ref.dtype)

    blk = tuple(d for d in o_spec.block_shape if d is not None)
    scratch = [pltpu.VMEM(blk, F32)] if (nred > 1 and out_dtype != F32) else []
    return pl.pallas_call(kern, name=name, grid=grid, in_specs=[a_spec, b_spec], out_specs=o_spec,
                          out_shape=jax.ShapeDtypeStruct(out_shape, out_dtype), scratch_shapes=scratch,
                          compiler_params=_params(ngrid))(a, b)


NN = ((1,), (0,))
NT = ((1,), (1,))
TN = ((0,), (0,))


def _mm_nn(name, a, w, out_dtype=F32, tn=None):
    m, k = a.shape
    n = w.shape[1]
    tm = _tile(m, 512)
    tn = n if tn is None else tn
    return _mm(name, a, w, grid=(m // tm, n // tn, 1), a_spec=BS((tm, k), lambda i, j, r: (i, 0)),
               b_spec=BS((k, tn), lambda i, j, r: (0, j)), o_spec=BS((tm, tn), lambda i, j, r: (i, j)),
               out_shape=(m, n), dn=NN, out_dtype=out_dtype)


def _mm_nt(name, a, w, out_dtype=F32, tn=None):
    m, n = a.shape
    k = w.shape[0]
    tm = _tile(m, 512)
    tn = n if tn is None else tn
    return _mm(name, a, w, grid=(m // tm, n // tn), a_spec=BS((tm, tn), lambda i, r: (i, r)),
               b_spec=BS((k, tn), lambda i, r: (0, r)), o_spec=BS((tm, k), lambda i, r: (i, 0)),
               out_shape=(m, k), dn=NT, out_dtype=out_dtype)


def _mm_tn(name, a, b, tn=None):
    m, k = a.shape
    n = b.shape[1]
    tm = _tile(m, 512)
    tn = n if tn is None else tn
    return _mm(name, a, b, grid=(n // tn, m // tm), a_spec=BS((tm, k), lambda j, r: (r, 0)),
               b_spec=BS((tm, tn), lambda j, r: (r, j)), o_spec=BS((k, tn), lambda j, r: (0, j)),
               out_shape=(k, n), dn=TN)


def _bmm_nn(name, a, w, out_dtype=F32):
    m, k = a.shape
    g, _, n = w.shape
    tm = _tile(m, 512)
    return _mm(name, a, w, grid=(g, m // tm, 1), a_spec=BS((tm, k), lambda q, i, r: (i, 0)),
               b_spec=BS((None, k, n), lambda q, i, r: (q, 0, 0)), o_spec=BS((None, tm, n), lambda q, i, r: (q, i, 0)),
               out_shape=(g, m, n), dn=NN, out_dtype=out_dtype)


def _bmm_tn(name, a, b):
    m, k = a.shape
    g, _, n = b.shape
    tm = _tile(m, 512)
    return _mm(name, a, b, grid=(g, m // tm), a_spec=BS((tm, k), lambda q, r: (r, 0)),
               b_spec=BS((None, tm, n), lambda q, r: (q, r, 0)), o_spec=BS((None, k, n), lambda q, r: (q, 0, 0)),
               out_shape=(g, k, n), dn=TN)


def _bmm_nt_sum(name, a, w):
    g, m, n = a.shape
    k = w.shape[1]
    tm = _tile(m, 512)
    return _mm(name, a, w, grid=(m // tm, g), a_spec=BS((None, tm, n), lambda i, q: (q, i, 0)),
               b_spec=BS((None, k, n), lambda i, q: (q, 0, 0)), o_spec=BS((tm, k), lambda i, q: (i, 0)),
               out_shape=(m, k), dn=NT)


def _bmm_nn_sum(name, a, w):
    g, m, k = a.shape
    n = w.shape[2]
    tm = _tile(m, 512)
    return _mm(name, a, w, grid=(m // tm, g), a_spec=BS((None, tm, k), lambda i, q: (q, i, 0)),
               b_spec=BS((None, k, n), lambda i, q: (q, 0, 0)), o_spec=BS((tm, n), lambda i, q: (i, 0)),
               out_shape=(m, n), dn=NN)


def _bbmm_tn(name, a, b):
    g, m, k = a.shape
    n = b.shape[2]
    tm = _tile(m, 512)
    return _mm(name, a, b, grid=(g, m // tm), a_spec=BS((None, tm, k), lambda q, r: (q, r, 0)),
               b_spec=BS((None, tm, n), lambda q, r: (q, r, 0)), o_spec=BS((None, k, n), lambda q, r: (q, 0, 0)),
               out_shape=(g, k, n), dn=TN)


def _rstd(x):
    return lax.rsqrt(jnp.mean(x * x, axis=-1, keepdims=True) + EPS)


def _norm_fwd(x, g):
    rows = x.shape[0]
    tm = _tile(rows, 512)

    def kern(x_ref, g_ref, h_ref):
        xv = x_ref[...]
        h_ref[...] = ((xv * _rstd(xv)) * g_ref[...]).astype(BF16)

    return pl.pallas_call(kern, name="norm_fwd", grid=(rows // tm,),
                          in_specs=[BS((tm, DM), lambda i: (i, 0)), BS((1, DM), lambda i: (0, 0))],
                          out_specs=BS((tm, DM), lambda i: (i, 0)),
                          out_shape=jax.ShapeDtypeStruct((rows, DM), BF16), compiler_params=_params(1))(x, g)


def _norm_res(x, y, g):
    rows = x.shape[0]
    tm = _tile(rows, 512)

    def kern(x_ref, y_ref, g_ref, o_ref):
        yv = y_ref[...]
        o_ref[...] = x_ref[...] + (yv * _rstd(yv)) * g_ref[...]

    row = BS((tm, DM), lambda i: (i, 0))
    return pl.pallas_call(kern, name="norm_res", grid=(rows // tm,),
                          in_specs=[row, row, BS((1, DM), lambda i: (0, 0))], out_specs=row,
                          out_shape=jax.ShapeDtypeStruct((rows, DM), F32), compiler_params=_params(1))(x, y, g)


def _norm_bwd(z, dout, g, resid, out_dtype):
    rows = z.shape[0]
    tm = _tile(rows, 512)
    has_res = resid is not None

    def kern(*refs):
        if has_res:
            z_ref, d_ref, g_ref, r_ref, dz_ref, dg_ref = refs
        else:
            z_ref, d_ref, g_ref, dz_ref, dg_ref = refs
        zv = z_ref[...]
        dv = d_ref[...].astype(F32)
        r = _rstd(zv)
        zh = zv * r
        dzh = dv * g_ref[...]
        dz = r * (dzh - zh * jnp.mean(dzh * zh, axis=-1, keepdims=True))
        if has_res:
            dz = dz + r_ref[...]
        dz_ref[...] = dz.astype(dz_ref.dtype)
        part = jnp.sum(dv * zh, axis=0, keepdims=True)

        @pl.when(pl.program_id(0) == 0)
        def _():
            dg_ref[...] = part

        @pl.when(pl.program_id(0) > 0)
        def _():
            dg_ref[...] += part

    row = BS((tm, DM), lambda i: (i, 0))
    vec = BS((1, DM), lambda i: (0, 0))
    ins = [row, row, vec] + ([row] if has_res else [])
    args = (z, dout, g) + ((resid,) if has_res else ())
    return pl.pallas_call(kern, name="norm_bwd_res" if has_res else "norm_bwd", grid=(rows // tm,), in_specs=ins,
                          out_specs=[row, vec],
                          out_shape=[jax.ShapeDtypeStruct((rows, DM), out_dtype), jax.ShapeDtypeStruct((1, DM), F32)],
                          compiler_params=_params(1))(*args)


def _ffn_up(h, wgu4):
    s = h.shape[0]
    tm = _tile(s, 512)

    def kern(h_ref, w_ref, gu_ref, a_ref):
        hv = h_ref[...]
        gate = jnp.dot(hv, w_ref[0], preferred_element_type=F32)
        up = jnp.dot(hv, w_ref[1], preferred_element_type=F32)
        gu_ref[0] = gate
        gu_ref[1] = up
        a_ref[...] = (gate * jax.nn.sigmoid(gate) * up).astype(BF16)

    return pl.pallas_call(
        kern, name="ffn_up", grid=(4, s // tm),
        in_specs=[BS((tm, DM), lambda j, i: (i, 0)), BS((2, None, DM, FFB), lambda j, i: (0, j, 0, 0))],
        out_specs=[BS((2, None, tm, FFB), lambda j, i: (0, j, i, 0)), BS((None, tm, FFB), lambda j, i: (j, i, 0))],
        out_shape=[jax.ShapeDtypeStruct((2, 4, s, FFB), F32), jax.ShapeDtypeStruct((4, s, FFB), BF16)],
        compiler_params=_params(2))(h, wgu4)


def _ffn_da(dy, wd4, gu):
    s = dy.shape[0]
    tm = _tile(s, 512)

    def kern(dy_ref, w_ref, gu_ref, o_ref):
        da = lax.dot_general(dy_ref[...], w_ref[...], (NT, ((), ())), preferred_element_type=F32)
        gate = gu_ref[0]
        up = gu_ref[1]
        sg = jax.nn.sigmoid(gate)
        o_ref[0] = (da * up * (sg * (1.0 + gate * (1.0 - sg)))).astype(BF16)
        o_ref[1] = (da * (gate * sg)).astype(BF16)

    blk = BS((2, None, tm, FFB), lambda j, i: (0, j, i, 0))
    return pl.pallas_call(
        kern, name="ffn_da", grid=(4, s // tm),
        in_specs=[BS((tm, DM), lambda j, i: (i, 0)), BS((None, FFB, DM), lambda j, i: (j, 0, 0)), blk],
        out_specs=blk, out_shape=jax.ShapeDtypeStruct((2, 4, s, FFB), BF16), compiler_params=_params(2))(dy, wd4, gu)


def _ffn_fwd(x, gpre, gpost, wgu, wd):
    h = _norm_fwd(x, gpre)
    gu, a = _ffn_up(h, wgu.reshape(2, 4, DM, FFB))
    y = _bmm_nn_sum("ffn_down", a, wd.reshape(4, FFB, DM))
    return _norm_res(x, y, gpost), (x, h, gu, a, y)


def _ffn_bwd(dxo, saved, gpre, gpost, wgu, wd):
    x, h, gu, a, y = saved
    s = x.shape[0]
    dy, dgpost = _norm_bwd(y, dxo, gpost, None, BF16)
    dgu = _ffn_da(dy, wd.reshape(4, FFB, DM), gu).reshape(8, s, FFB)
    dwd = _bmm_tn_a3("ffn_dwd", a, dy)
    dwgu = _bmm_tn("ffn_dwgu", h, dgu)
    dh = _bmm_nt_sum("ffn_dh", dgu, wgu)
    dx, dgpre = _norm_bwd(x, dh, gpre, dxo, F32)
    return dx, dgpre, dgpost, dwgu, dwd.reshape(D_FF, DM)


def _bmm_tn_a3(name, a, b):
    g, m, k = a.shape
    n = b.shape[1]
    tm = _tile(m, 512)
    return _mm(name, a, b, grid=(g, m // tm), a_spec=BS((None, tm, k), lambda q, r: (q, r, 0)),
               b_spec=BS((tm, n), lambda q, r: (r, 0)), o_spec=BS((None, k, n), lambda q, r: (q, 0, 0)),
               out_shape=(g, k, n), dn=TN)


def _softmax_rows(s):
    m = jnp.max(s, axis=-1, keepdims=True)
    p = jnp.exp(s - m)
    return p / jnp.sum(p, axis=-1, keepdims=True)


def _xattn_fwd_call(h, wq, kv):
    s = h.shape[0]
    mlen = kv.shape[1]
    tm = _tile(s, 512)
    scale = MEM_HD ** -0.5

    def kern(h_ref, w_ref, k_ref, v_ref, q_ref, o_ref):
        q = jnp.dot(h_ref[...], w_ref[...], preferred_element_type=F32).astype(BF16)
        q_ref[...] = q
        sc = lax.dot_general(q, k_ref[...], (NT, ((), ())), preferred_element_type=F32) * scale
        p = _softmax_rows(sc)
        o_ref[...] = jnp.dot(p.astype(BF16), v_ref[...], preferred_element_type=F32).astype(BF16)

    blk = BS((tm, MEM_HD), lambda i, hd: (i, hd))
    return pl.pallas_call(
        kern, name="xattn_fwd", grid=(s // tm, MEM_H),
        in_specs=[BS((tm, DM), lambda i, hd: (i, 0)), BS((DM, MEM_HD), lambda i, hd: (0, hd)),
                  BS((None, mlen, MEM_HD), lambda i, hd: (hd, 0, 0)),
                  BS((None, mlen, MEM_HD), lambda i, hd: (MEM_H + hd, 0, 0))],
        out_specs=[blk, blk],
        out_shape=[jax.ShapeDtypeStruct((s, DM), BF16), jax.ShapeDtypeStruct((s, DM), BF16)],
        compiler_params=_params(2))(h, wq, kv, kv)


def _xattn_bwd_call(q, kv, do):
    s = q.shape[0]
    mlen = kv.shape[1]
    tm = _tile(s, 512)
    scale = MEM_HD ** -0.5

    def kern(q_ref, k_ref, v_ref, do_ref, dq_ref, dkv_ref):
        qv, kvv, vv, dov = q_ref[...], k_ref[...], v_ref[...], do_ref[...]
        sc = lax.dot_general(qv, kvv, (NT, ((), ())), preferred_element_type=F32) * scale
        p = _softmax_rows(sc)
        dp = lax.dot_general(dov, vv, (NT, ((), ())), preferred_element_type=F32)
        ds = (p * (dp - jnp.sum(dp * p, axis=-1, keepdims=True)) * scale).astype(BF16)
        dq_ref[...] = jnp.dot(ds, kvv, preferred_element_type=F32).astype(BF16)
        dk = lax.dot_general(ds, qv, (TN, ((), ())), preferred_element_type=F32)
        dv = lax.dot_general(p.astype(BF16), dov, (TN, ((), ())), preferred_element_type=F32)

        @pl.when(pl.program_id(1) == 0)
        def _():
            dkv_ref[0] = dk
            dkv_ref[1] = dv

        @pl.when(pl.program_id(1) > 0)
        def _():
            dkv_ref[0] += dk
            dkv_ref[1] += dv

    blk = BS((tm, MEM_HD), lambda hd, i: (i, hd))
    return pl.pallas_call(
        kern, name="xattn_bwd", grid=(MEM_H, s // tm),
        in_specs=[blk, BS((None, mlen, MEM_HD), lambda hd, i: (hd, 0, 0)),
                  BS((None, mlen, MEM_HD), lambda hd, i: (MEM_H + hd, 0, 0)), blk],
        out_specs=[blk, BS((2, None, mlen, MEM_HD), lambda hd, i: (0, hd, 0, 0))],
        out_shape=[jax.ShapeDtypeStruct((s, DM), BF16), jax.ShapeDtypeStruct((2, MEM_H, mlen, MEM_HD), F32)],
        compiler_params=_params(2))(q, kv, kv, do)


def _cross_fwd(x, mem, gpre, gmem, gpost, wq, wkv, wo):
    h = _norm_fwd(x, gpre)
    mn = _norm_fwd(mem, gmem)
    kv = _bmm_nn("xattn_kv", mn, wkv, BF16)
    q, o = _xattn_fwd_call(h, wq, kv)
    y = _mm_nn("xattn_out", o, wo)
    return _norm_res(x, y, gpost), (x, h, mn, kv, q, o, y)


def _cross_bwd(dxo, saved, mem, gpre, gmem, gpost, wq, wkv, wo):
    x, h, mn, kv, q, o, y = saved
    mlen = mem.shape[0]
    dy, dgpost = _norm_bwd(y, dxo, gpost, None, BF16)
    do = _mm_nt("xattn_do", dy, wo, BF16)
    dwo = _mm_tn("xattn_dwo", o, dy)
    dq, dkv = _xattn_bwd_call(q, kv, do)
    dwq = _mm_tn("xattn_dwq", h, dq)
    dh = _mm_nt("xattn_dh", dq, wq)
    dkv8 = dkv.reshape(8, mlen, MEM_HD)
    dwkv = _bmm_tn("xattn_dwkv", mn, dkv8)
    dmn = _bmm_nt_sum("xattn_dmn", dkv8, wkv)
    _, dgmem = _norm_bwd(mem, dmn, gmem, None, BF16)
    dx, dgpre = _norm_bwd(x, dh, gpre, dxo, F32)
    return dx, dgpre, dgmem, dgpost, dwq, dwkv, dwo


def _log_sigmoid(z):
    return jnp.minimum(z, 0.0) - jnp.log1p(jnp.exp(-jnp.abs(z)))


def _lane_scan_steps():
    return (1, 2, 4, 8, 16, 32, 64)


def _fox_cum(frow, bfb):
    s = frow.shape[1]

    def kern(f_ref, b_ref, o_ref):
        lane = lax.broadcasted_iota(jnp.int32, (FOX_H, LANE), 1)
        carry = jnp.zeros((FOX_H, 1), F32)
        for c in range(s // LANE):
            sl = slice(c * LANE, (c + 1) * LANE)
            lf = _log_sigmoid(f_ref[:, sl] + b_ref[...])
            v = lf
            for d in _lane_scan_steps():
                v = v + jnp.where(lane >= d, pltpu.roll(v, d, 1), 0.0)
            o_ref[:, sl] = v + carry
            carry = carry + jnp.sum(lf, axis=1, keepdims=True)

    return pl.pallas_call(kern, name="fox_cum", out_shape=jax.ShapeDtypeStruct((FOX_H, s), F32),
                          compiler_params=pltpu.CompilerParams(vmem_limit_bytes=VMEM_LIMIT))(frow, bfb)


def _fox_dlogf(dcq, dck, frow, bfb):
    s = frow.shape[1]

    def kern(q_ref, d_ref, f_ref, b_ref, df_ref, db_ref):
        lane = lax.broadcasted_iota(jnp.int32, (FOX_H, LANE), 1)
        carry = jnp.zeros((FOX_H, 1), F32)
        dbf = jnp.zeros((FOX_H, 1), F32)
        for c in reversed(range(s // LANE)):
            sl = slice(c * LANE, (c + 1) * LANE)
            dc = q_ref[:, sl] - d_ref[:, sl]
            v = dc
            for d in _lane_scan_steps():
                v = v + jnp.where(lane < LANE - d, pltpu.roll(v, LANE - d, 1), 0.0)
            v = v + carry
            carry = carry + jnp.sum(dc, axis=1, keepdims=True)
            df = v * jax.nn.sigmoid(-(f_ref[:, sl] + b_ref[...]))
            df_ref[:, sl] = df
            dbf = dbf + jnp.sum(df, axis=1, keepdims=True)
        db_ref[...] = jnp.broadcast_to(dbf, (FOX_H, LANE))

    return pl.pallas_call(kern, name="fox_dlogf",
                          out_shape=[jax.ShapeDtypeStruct((FOX_H, s), F32), jax.ShapeDtypeStruct((FOX_H, LANE), F32)],
                          compiler_params=pltpu.CompilerParams(vmem_limit_bytes=VMEM_LIMIT))(dcq, dck, frow, bfb)


Q_COL, K_COL, V_COL = 0, FOX_W // LANE, 2 * FOX_W // LANE
B_COL, C_COL, U_COL = 12, 16, 20


def _fox_logits(qm, kb, cc, cr, causal, scale, reps):
    sc = lax.dot_general(qm, kb, (NT, ((), ())), preferred_element_type=F32) * scale
    sc = sc + jnp.tile(cc, (1, reps)) - cr
    return jnp.where(causal, sc, NEG)


def _fox_fwd_call(proj, cumc, cumr):
    s = proj.shape[0]
    tq = _tile(s, 256)
    nq = s // tq
    reps = tq // LANE
    scale = FOX_HD ** -0.5

    def kern(q_ref, k_ref, v_ref, cc_ref, cr_ref, o_ref, lse_ref, m_s, l_s, acc_s):
        i = pl.program_id(1)
        j = pl.program_id(2)
        lane = lax.broadcasted_iota(jnp.int32, (tq, LANE), 1)

        @pl.when(j == 0)
        def _():
            m_s[...] = jnp.full(m_s.shape, NEG, F32)
            l_s[...] = jnp.zeros(l_s.shape, F32)
            acc_s[...] = jnp.zeros(acc_s.shape, F32)

        @pl.when(j <= i)
        def _():
            qv = q_ref[...]
            kb = k_ref[...].astype(BF16)
            vb = v_ref[...].astype(BF16)
            causal = (i * tq + lax.broadcasted_iota(jnp.int32, (tq, tq), 0)
                      >= j * tq + lax.broadcasted_iota(jnp.int32, (tq, tq), 1))
            for hh in range(2):
                sel = (lane < FOX_HD) if hh == 0 else (lane >= FOX_HD)
                qm = jnp.where(sel, qv, 0.0).astype(BF16)
                sc = _fox_logits(qm, kb, cc_ref[hh], cr_ref[hh:hh + 1, :], causal, scale, reps)
                m_prev = m_s[hh]
                m_new = jnp.maximum(m_prev, jnp.max(sc, axis=-1, keepdims=True))
                alpha = jnp.exp(m_prev - m_new)
                p = jnp.exp(sc - m_new)
                l_s[hh] = alpha * l_s[hh] + jnp.sum(p, axis=-1, keepdims=True)
                acc_s[hh] = alpha * acc_s[hh] + jnp.dot(p.astype(BF16), vb, preferred_element_type=F32)
                m_s[hh] = m_new

        @pl.when(j == i)
        def _():
            o_ref[...] = jnp.where(lane < FOX_HD, acc_s[0] / l_s[0], acc_s[1] / l_s[1])
            for hh in range(2):
                lse_ref[hh] = jnp.broadcast_to(m_s[hh] + jnp.log(l_s[hh]), (tq, LANE))

    kvi = lambda hp, i, j: jnp.minimum(j, i)
    return pl.pallas_call(
        kern, name="fox_fwd", grid=(4, nq, nq),
        in_specs=[BS((tq, LANE), lambda hp, i, j: (i, Q_COL + hp)),
                  BS((tq, LANE), lambda hp, i, j: (kvi(hp, i, j), K_COL + hp)),
                  BS((tq, LANE), lambda hp, i, j: (kvi(hp, i, j), V_COL + hp)),
                  BS((2, tq, LANE), lambda hp, i, j: (hp, i, 0)),
                  BS((None, 2, tq), lambda hp, i, j: (hp, 0, kvi(hp, i, j)))],
        out_specs=[BS((tq, LANE), lambda hp, i, j: (i, hp)), BS((2, tq, LANE), lambda hp, i, j: (hp, i, 0))],
        out_shape=[jax.ShapeDtypeStruct((s, FOX_W), F32), jax.ShapeDtypeStruct((FOX_H, s, LANE), F32)],
        scratch_shapes=[pltpu.VMEM((2, tq, 1), F32), pltpu.VMEM((2, tq, 1), F32), pltpu.VMEM((2, tq, LANE), F32)],
        compiler_params=_params(3))(proj, proj, proj, cumc, cumr)


ROWSUM_M = 16


def _fox_bwd_call(proj, o, lse, dcat, cumc, cumr):
    s = proj.shape[0]
    tq = _tile(s, 256)
    nq = s // tq
    reps = tq // LANE
    scale = FOX_HD ** -0.5

    def kern(q_ref, k_ref, v_ref, do_ref, o_ref, lse_ref, cc_ref, cr_ref, dq_ref, dk_ref, dv_ref, dck_ref, dcq_ref):
        j = pl.program_id(1)
        i = pl.program_id(2)
        lane = lax.broadcasted_iota(jnp.int32, (tq, LANE), 1)
        ones = jnp.ones((ROWSUM_M, tq), BF16)

        @pl.when((j == 0) & (i == 0))
        def _():
            dq_ref[...] = jnp.zeros(dq_ref.shape, F32)
            dcq_ref[...] = jnp.zeros(dcq_ref.shape, F32)

        @pl.when(i == j)
        def _():
            dk_ref[...] = jnp.zeros(dk_ref.shape, F32)
            dv_ref[...] = jnp.zeros(dv_ref.shape, F32)
            dck_ref[...] = jnp.zeros(dck_ref.shape, F32)

        @pl.when(i >= j)
        def _():
            qv = q_ref[...]
            dov = do_ref[...]
            ov = o_ref[...]
            kb = k_ref[...].astype(BF16)
            vb = v_ref[...].astype(BF16)
            causal = (i * tq + lax.broadcasted_iota(jnp.int32, (tq, tq), 0)
                      >= j * tq + lax.broadcasted_iota(jnp.int32, (tq, tq), 1))
            dq_t = jnp.zeros((tq, LANE), F32)
            dk_t = jnp.zeros((tq, LANE), F32)
            dv_t = jnp.zeros((tq, LANE), F32)
            for hh in range(2):
                sel = (lane < FOX_HD) if hh == 0 else (lane >= FOX_HD)
                qm = jnp.where(sel, qv, 0.0).astype(BF16)
                dom32 = jnp.where(sel, dov, 0.0)
                dom = dom32.astype(BF16)
                sc = _fox_logits(qm, kb, cc_ref[hh], cr_ref[hh:hh + 1, :], causal, scale, reps)
                p = jnp.exp(sc - jnp.tile(lse_ref[hh], (1, reps)))
                dp = lax.dot_general(dom, vb, (NT, ((), ())), preferred_element_type=F32)
                delta = jnp.sum(dom32 * ov, axis=-1, keepdims=True)
                ds = p * (dp - delta)
                dsb = ds.astype(BF16)
                dq_t = jnp.where(sel, jnp.dot(dsb, kb, preferred_element_type=F32) * scale, dq_t)
                dk_t = dk_t + lax.dot_general(dsb, qm, (TN, ((), ())), preferred_element_type=F32) * scale
                dv_t = dv_t + lax.dot_general(p.astype(BF16), dom, (TN, ((), ())), preferred_element_type=F32)
                dck_ref[hh] += jnp.sum(ds, axis=0, keepdims=True)
                ds_lo = (ds - dsb.astype(F32)).astype(BF16)
                dcq_ref[hh, i] += (lax.dot_general(ones, dsb, (NT, ((), ())), preferred_element_type=F32)
                                   + lax.dot_general(ones, ds_lo, (NT, ((), ())), preferred_element_type=F32))
            rows =pl.ds(pl.multiple_of(i * tq, tq), tq)
            dq_ref[rows, :] += dq_t
            dk_ref[...] += dk_t
            dv_ref[...] += dv_t

    qi = lambda hp, j, i: jnp.maximum(i, j)
    return pl.pallas_call(
        kern, name="fox_bwd", grid=(4, nq, nq),
        in_specs=[BS((tq, LANE), lambda hp, j, i: (qi(hp, j, i), Q_COL + hp)),
                  BS((tq, LANE), lambda hp, j, i: (j, K_COL + hp)),
                  BS((tq, LANE), lambda hp, j, i: (j, V_COL + hp)),
                  BS((tq, LANE), lambda hp, j, i: (qi(hp, j, i), hp)),
                  BS((tq, LANE), lambda hp, j, i: (qi(hp, j, i), hp)),
                  BS((2, tq, LANE), lambda hp, j, i: (hp, qi(hp, j, i), 0)),
                  BS((2, tq, LANE), lambda hp, j, i: (hp, qi(hp, j, i), 0)),
                  BS((None, 2, tq), lambda hp, j, i: (hp, 0, j))],
        out_specs=[BS((s, LANE), lambda hp, j, i: (0, hp)), BS((tq, LANE), lambda hp, j, i: (j, hp)),
                   BS((tq, LANE), lambda hp, j, i: (j, hp)), BS((2, 1, tq), lambda hp, j, i: (hp, 0, j)),
                   BS((2, nq, ROWSUM_M, tq), lambda hp, j, i: (hp, 0, 0, 0))],
        out_shape=[jax.ShapeDtypeStruct((s, FOX_W), F32), jax.ShapeDtypeStruct((s, FOX_W), F32),
                   jax.ShapeDtypeStruct((s, FOX_W), F32), jax.ShapeDtypeStruct((FOX_H, 1, s), F32),
                   jax.ShapeDtypeStruct((FOX_H, nq, ROWSUM_M, tq), F32)],
        compiler_params=_params(3))(proj, proj, proj, dcat, o, lse, cumc, cumr)


def _shift_down(v, d, row):
    return jnp.where(row >= d, pltpu.roll(v, d, 0), 0.0)


def _shift_up(v, d, row, n):
    return jnp.where(row < n - d, pltpu.roll(v, n - d, 0), 0.0)


def _sconv_fwd(proj, convw):
    s = proj.shape[0]

    def kern(b_ref, c_ref, u_ref, w_ref, y_ref):
        row = lax.broadcasted_iota(jnp.int32, (s, LANE), 0)
        z = c_ref[...] * u_ref[...]
        conv = w_ref[2:3, :] * z + w_ref[1:2, :] * _shift_down(z, 1, row) + w_ref[0:1, :] * _shift_down(z, 2, row)
        y_ref[...] = (b_ref[...] * conv).astype(BF16)

    col = lambda base: BS((s, LANE), lambda cb: (0, base + cb))
    return pl.pallas_call(kern, name="sconv_fwd", grid=(SC_W // LANE,),
                          in_specs=[col(B_COL), col(C_COL), col(U_COL), BS((SC_K, LANE), lambda cb: (0, cb))],
                          out_specs=BS((s, LANE), lambda cb: (0, cb)),
                          out_shape=jax.ShapeDtypeStruct((s, SC_W), BF16), compiler_params=_params(1))(proj, proj, proj, convw)


def _sconv_bwd(proj, convw, dcat):
    s = proj.shape[0]

    def kern(b_ref, c_ref, u_ref, w_ref, dy_ref, db_ref, dc_ref, du_ref, dw_ref):
        row = lax.broadcasted_iota(jnp.int32, (s, LANE), 0)
        cv, uv, dyv = c_ref[...], u_ref[...], dy_ref[...]
        z = cv * uv
        z1 = _shift_down(z, 1, row)
        z2 = _shift_down(z, 2, row)
        conv = w_ref[2:3, :] * z + w_ref[1:2, :] * z1 + w_ref[0:1, :] * z2
        db_ref[...] = dyv * conv
        dcv = dyv * b_ref[...]
        dz = w_ref[2:3, :] * dcv + w_ref[1:2, :] * _shift_up(dcv, 1, row, s) + w_ref[0:1, :] * _shift_up(dcv, 2, row, s)
        dc_ref[...] = dz * uv
        du_ref[...] = dz * cv
        dw_ref[0:1, :] = jnp.sum(dcv * z2, axis=0, keepdims=True)
        dw_ref[1:2, :] = jnp.sum(dcv * z1, axis=0, keepdims=True)
        dw_ref[2:3, :] = jnp.sum(dcv * z, axis=0, keepdims=True)

    col = lambda base: BS((s, LANE), lambda cb: (0, base + cb))
    out = BS((s, LANE), lambda cb: (0, cb))
    wspec = BS((SC_K, LANE), lambda cb: (0, cb))
    act = jax.ShapeDtypeStruct((s, SC_W), F32)
    return pl.pallas_call(kern, name="sconv_bwd", grid=(SC_W // LANE,),
                          in_specs=[col(B_COL), col(C_COL), col(U_COL), wspec, col(FOX_W // LANE)],
                          out_specs=[out, out, out, wspec],
                          out_shape=[act, act, act, jax.ShapeDtypeStruct((SC_K, SC_W), F32)],
                          compiler_params=_params(1))(proj, proj, proj, convw, dcat)


def _fox_layer_fwd(x, gpre, gpost, wall, bfb, convw, wout):
    s = x.shape[0]
    h = _norm_fwd(x, gpre)
    proj = _mm_nn("fox_proj", h, wall, tn=AB_PAD // 5)
    frow = proj[:, 3 * FOX_W + 3 * SC_W:3 * FOX_W + 3 * SC_W + FOX_H].T
    cumr = _fox_cum(frow, bfb)
    cumc = jnp.broadcast_to(cumr[:, :, None], (FOX_H, s, LANE))
    cumr4 = cumr.reshape(4, 2, s)
    o, lse = _fox_fwd_call(proj, cumc, cumr4)
    yb = _sconv_fwd(proj, convw)
    cat = jnp.concatenate([o.astype(BF16), yb], axis=1)
    y = _mm_nn("fox_out", cat, wout)
    return _norm_res(x, y, gpost), (x, h, proj, frow, cumc, cumr4, o, lse, cat, y)


def _fox_layer_bwd(dxo, saved, gpre, gpost, wall, bfb, convw, wout):
    x, h, proj, frow, cumc, cumr4, o, lse, cat, y = saved
    s = x.shape[0]
    dy, dgpost = _norm_bwd(y, dxo, gpost, None, BF16)
    dcat = _mm_nt("fox_dcat", dy, wout)
    dwout = _mm_tn("fox_dwout", cat, dy)
    db, dc, du, dconvw = _sconv_bwd(proj, convw, dcat)
    dq, dk, dv, dck, dcq = _fox_bwd_call(proj, o, lse, dcat, cumc, cumr4)
    dfrow, dbf = _fox_dlogf(dcq[:, :, 0, :].reshape(FOX_H, s), dck.reshape(FOX_H, s), frow, bfb)
    dfcol = jnp.pad(dfrow.T, ((0, 0), (0, LANE - FOX_H)))
    dproj = jnp.concatenate([dq, dk, dv, db, dc, du, dfcol], axis=1).astype(BF16)
    dwall = _mm_tn("fox_dwall", h, dproj, tn=AB_PAD // 5)
    dh = _mm_nt("fox_dh", dproj, wall, tn=AB_PAD // 5)
    dx, dgpre = _norm_bwd(x, dh, gpre, dxo, F32)
    return dx, dgpre, dgpost, dwall, dbf[:, 0], dconvw, dwout


def _ab_pack(w):
    nf = 3 * FOX_W
    return jnp.concatenate([w[:, :nf], w[:, nf + FOX_H:], w[:, nf:nf + FOX_H],
                            jnp.zeros((w.shape[0], AB_PAD - AB_IN), w.dtype)], axis=1)


def _ab_unpack(w):
    nf = 3 * FOX_W
    nbcu = 3 * SC_W
    return jnp.concatenate([w[:, :nf], w[:, nf + nbcu:nf + nbcu + FOX_H], w[:, nf:nf + nbcu]], axis=1)


NCH = DM // LANE
CH_PER_BLK = LRU_BW // LANE


def _chunk_spec(s, lead=0):
    return BS((None, s, LANE), lambda ch: (lead + ch // CH_PER_BLK, 0, ch % CH_PER_BLK))


def _vec_chunk(rows):
    return BS((rows, LANE), lambda ch: (0, ch))


def _neg_expm1(x):
    series = -x * (1.0 + x * (1 / 2) * (1.0 + x * (1 / 3) * (1.0 + x * (1 / 4) * (1.0 + x * (1 / 5) * (
        1.0 + x * (1 / 6) * (1.0 + x * (1 / 7)))))))
    return jnp.where(x > -0.25, series, 1.0 - jnp.exp(x))


def _softplus(z):
    return jnp.maximum(z, 0.0) + jnp.log1p(jnp.exp(-jnp.abs(z)))


GELU_C = math.sqrt(2.0 / math.pi)
GELU_A = 0.044715


def _gelu(x):
    return 0.5 * x * (1.0 + jnp.tanh(GELU_C * (x + GELU_A * x * x * x)))


def _gelu_grad(x):
    t = jnp.tanh(GELU_C * (x + GELU_A * x * x * x))
    return 0.5 * (1.0 + t) + 0.5 * x * (1.0 - t * t) * GELU_C * (1.0 + 3.0 * GELU_A * x * x)


def _lru_conv_fwd(gu, convw, convb):
    s = gu.shape[1]

    def kern(x_ref, w_ref, b_ref, u_ref):
        row = lax.broadcasted_iota(jnp.int32, (s, LANE), 0)
        xv = x_ref[...]
        u_ref[...] = (b_ref[...] + w_ref[3:4, :] * xv + w_ref[2:3, :] * _shift_down(xv, 1, row)
                      + w_ref[1:2, :] * _shift_down(xv, 2, row) + w_ref[0:1, :] * _shift_down(xv, 3, row))

    return pl.pallas_call(kern, name="lru_conv_fwd", grid=(NCH,),
                          in_specs=[_chunk_spec(s, LRU_NB), _vec_chunk(RG_K), _vec_chunk(1)], out_specs=_chunk_spec(s),
                          out_shape=jax.ShapeDtypeStruct((LRU_NB, s, LRU_BW), F32), compiler_params=_params(1))(gu, convw, convb)


def _lru_conv_bwd(dud, dug, gu, convw):
    s = gu.shape[1]

    def kern(d1_ref, d2_ref, x_ref, w_ref, dx_ref, dw_ref, db_ref):
        row = lax.broadcasted_iota(jnp.int32, (s, LANE), 0)
        du = d1_ref[...] + d2_ref[...]
        xv = x_ref[...]
        dx_ref[...] = (w_ref[3:4, :] * du + w_ref[2:3, :] * _shift_up(du, 1, row, s) + w_ref[1:2, :] * _shift_up(du, 2, row, s)
                       + w_ref[0:1, :] * _shift_up(du, 3, row, s)).astype(BF16)
        dw_ref[3:4, :] = jnp.sum(du * xv, axis=0, keepdims=True)
        for k in range(1, RG_K):
            dw_ref[3 - k:4 - k, :] = jnp.sum(du * _shift_down(xv, k, row), axis=0, keepdims=True)
        db_ref[...] = jnp.sum(du, axis=0, keepdims=True)

    return pl.pallas_call(kern, name="lru_conv_bwd", grid=(NCH,),
                          in_specs=[_chunk_spec(s), _chunk_spec(s), _chunk_spec(s, LRU_NB), _vec_chunk(RG_K)],
                          out_specs=[_chunk_spec(s), _vec_chunk(RG_K), _vec_chunk(1)],
                          out_shape=[jax.ShapeDtypeStruct((LRU_NB, s, LRU_BW), BF16),
                                     jax.ShapeDtypeStruct((RG_K, DM), F32), jax.ShapeDtypeStruct((1, DM), F32)],
                          compiler_params=_params(1))(dud, dug, gu, convw)


def _lru_gates(z_ref, bai_ref, lam_ref, uv):
    r = jax.nn.sigmoid(z_ref[0] + bai_ref[0:1, :])
    ig = jax.nn.sigmoid(z_ref[1] + bai_ref[1:2, :])
    sp = _softplus(-lam_ref[...])
    la = -RG_C * r * sp
    a = jnp.exp(la)
    sq = jnp.sqrt(_neg_expm1(2.0 * la))
    return r, ig, sp, a, sq


def _scan_steps(n):
    d, out = 1, []
    while d < n:
        out.append(d)
        d *= 2
    return out


def _lru_scan_fwd(z, bai, lam, u, gu):
    s = u.shape[1]
    zspec = BS((2, None, s, LANE), lambda ch: (0, ch // CH_PER_BLK, 0, ch % CH_PER_BLK))

    def kern(z_ref, bai_ref, lam_ref, u_ref, g_ref, hs_ref, y_ref):
        row = lax.broadcasted_iota(jnp.int32, (s, LANE), 0)
        uv = u_ref[...]
        _, ig, _, a, sq = _lru_gates(z_ref, bai_ref, lam_ref, uv)
        b = sq * (ig * uv)
        for d in _scan_steps(s):
            a_sh = jnp.where(row >= d, pltpu.roll(a, d, 0), 1.0)
            b = a * _shift_down(b, d, row) + b
            a = a * a_sh
        hs_ref[...] = b
        y_ref[...] = (_gelu(g_ref[...]) * b).astype(BF16)

    return pl.pallas_call(kern, name="lru_scan_fwd", grid=(NCH,),
                          in_specs=[zspec, _vec_chunk(2), _vec_chunk(1), _chunk_spec(s), _chunk_spec(s)],
                          out_specs=[_chunk_spec(s), BS((s, LANE), lambda ch: (0, ch))],
                          out_shape=[jax.ShapeDtypeStruct((LRU_NB, s, LRU_BW), F32), jax.ShapeDtypeStruct((s, DM), BF16)],
                          compiler_params=_params(1))(z, bai, lam, u, gu)


def _lru_scan_bwd(dyp, z, bai, lam, u, gu, hs):
    s = u.shape[1]
    zspec = BS((2, None, s, LANE), lambda ch: (0, ch // CH_PER_BLK, 0, ch % CH_PER_BLK))

    def kern(dy_ref, z_ref, bai_ref, lam_ref, u_ref, g_ref, hs_ref, dg_ref, dz_ref, du_ref, dbai_ref, dlam_ref):
        row = lax.broadcasted_iota(jnp.int32, (s, LANE), 0)
        uv, gv, hv, dyv = u_ref[...], g_ref[...], hs_ref[...], dy_ref[...]
        r, ig, sp, a, sq = _lru_gates(z_ref, bai_ref, lam_ref, uv)
        dg_ref[...] = (dyv * hv * _gelu_grad(gv)).astype(BF16)
        g = dyv * _gelu(gv)
        an = _shift_up(a, 1, row, s)
        for d in _scan_steps(s):
            an_sh = jnp.where(row < s - d, pltpu.roll(an, s - d, 0), 1.0)
            g = an * _shift_up(g, d, row, s) + g
            an = an * an_sh
        da = g * _shift_down(hv, 1, row)
        dsq = g * (ig * uv)
        di = g * sq * uv
        du_ref[...] = g * sq * ig
        dla = da * a - dsq * (a * a / sq)
        dzr = dla * (-RG_C * sp) * r * (1.0 - r)
        dzi = di * ig * (1.0 - ig)
        dz_ref[0] = dzr.astype(BF16)
        dz_ref[1] = dzi.astype(BF16)
        dbai_ref[0:1, :] = jnp.sum(dzr, axis=0, keepdims=True)
        dbai_ref[1:2, :] = jnp.sum(dzi, axis=0, keepdims=True)
        dlam_ref[...] = jnp.sum(dla * r, axis=0, keepdims=True) * (RG_C * jax.nn.sigmoid(-lam_ref[...]))

    return pl.pallas_call(
        kern, name="lru_scan_bwd", grid=(NCH,),
        in_specs=[BS((s, LANE), lambda ch: (0, ch)), zspec, _vec_chunk(2), _vec_chunk(1), _chunk_spec(s), _chunk_spec(s),
                  _chunk_spec(s)],
        out_specs=[_chunk_spec(s), zspec, _chunk_spec(s), _vec_chunk(2), _vec_chunk(1)],
        out_shape=[jax.ShapeDtypeStruct((LRU_NB, s, LRU_BW), BF16), jax.ShapeDtypeStruct((2, LRU_NB, s, LRU_BW), BF16),
                   jax.ShapeDtypeStruct((LRU_NB, s, LRU_BW), F32), jax.ShapeDtypeStruct((2, DM), F32),
                   jax.ShapeDtypeStruct((1, DM), F32)],
        compiler_params=_params(1))(dyp, z, bai, lam, u, gu, hs)


def _lru_layer_fwd(x, gpre, gpost, win, convw, convb, wai, bai, lam, wout):
    s = x.shape[0]
    tm = _tile(s, 512)
    h = _norm_fwd(x, gpre)
    gu = _bmm_nn("lru_in", h, win)
    u = _lru_conv_fwd(gu, convw, convb)
    z = _mm("lru_gate", u, wai, grid=(2, LRU_NB, s // tm, 1),
            a_spec=BS((None, tm, LRU_BW), lambda k, n, i, r: (n, i, 0)),
            b_spec=BS((None, None, LRU_BW, LRU_BW), lambda k, n, i, r: (k, n, 0, 0)),
            o_spec=BS((None, None, tm, LRU_BW), lambda k, n, i, r: (k, n, i, 0)),
            out_shape=(2, LRU_NB, s, LRU_BW), dn=NN)
    hs, yp = _lru_scan_fwd(z, bai, lam, u, gu)
    y = _mm_nn("lru_out", yp, wout)
    return _norm_res(x, y, gpost), (x, h, gu, u, z, hs, yp, y)


def _lru_layer_bwd(dxo, saved, gpre, gpost, win, convw, convb, wai, bai, lam, wout):
    x, h, gu, u, z, hs, yp, y = saved
    s = x.shape[0]
    tm = _tile(s, 512)
    dy, dgpost = _norm_bwd(y, dxo, gpost, None, BF16)
    dyp = _mm_nt("lru_dyp", dy, wout)
    dwout = _mm_tn("lru_dwout", yp, dy)
    dgate, dz, dud, dbai, dlam = _lru_scan_bwd(dyp, z, bai, lam, u, gu, hs)
    dwai = _mm("lru_dwai", u, dz, grid=(2, LRU_NB, s // tm),
               a_spec=BS((None, tm, LRU_BW), lambda k, n, r: (n, r, 0)),
               b_spec=BS((None, None, tm, LRU_BW), lambda k, n, r: (k, n, r, 0)),
               o_spec=BS((None, None, LRU_BW, LRU_BW), lambda k, n, r: (k, n, 0, 0)),
               out_shape=(2, LRU_NB, LRU_BW, LRU_BW), dn=TN)
    dug = _mm("lru_dug", dz, wai, grid=(LRU_NB, s // tm, 2),
              a_spec=BS((None, None, tm, LRU_BW), lambda n, i, k: (k, n, i, 0)),
              b_spec=BS((None, None, LRU_BW, LRU_BW), lambda n, i, k: (k, n, 0, 0)),
              o_spec=BS((None, tm, LRU_BW), lambda n, i, k: (n, i, 0)),
              out_shape=(LRU_NB, s, LRU_BW), dn=NT)
    duraw, dconvw, dconvb = _lru_conv_bwd(dud, dug, gu, convw)
    dgu = jnp.concatenate([dgate, duraw], axis=0)
    dwin = _bmm_tn("lru_dwin", h, dgu)
    dh = _bmm_nt_sum("lru_dh", dgu, win)
    dx, dgpre = _norm_bwd(x, dh, gpre, dxo, F32)
    return dx, dgpre, dgpost, dwin, dconvw, dconvb, dwai, dbai, dlam, dwout


CHIP_FLIPS = ((1, 0), (0, 1), (1, 1))


def _place():
    return lax.axis_index("x"), lax.axis_index("y"), lax.axis_index("c")


def _flip(v, f):
    return 1 - v if f else v


def _comm_params():
    return pltpu.CompilerParams(vmem_limit_bytes=VMEM_LIMIT)


def _all_gather(shards):
    n = len(shards)

    def body(*refs):
        ins, outs = refs[:n], refs[n:2 * n]
        send_sems, recv_sems, local_sems = refs[2 * n:]
        x, y, c = _place()
        me, sibling = (x, y, c), (x, y, 1 - c)
        chips = [(_flip(x, fx), _flip(y, fy)) for fx, fy in CHIP_FLIPS]

        def slot(t, p):
            return outs[t].at[:, 4 * p[0] + 2 * p[1] + p[2]]

        def copy(t, k, block, to, src=None):
            return pltpu.make_async_remote_copy(
                src_ref=slot(t, block) if src is None else src, dst_ref=slot(t, block),
                send_sem=send_sems.at[7 * t + k], recv_sem=recv_sems.at[7 * t + k], device_id=to, device_id_type=MESH)

        mine = [pltpu.make_async_copy(ins[t], slot(t, me), local_sems.at[t]) for t in range(n)]
        first = []
        for t in range(n):
            first.append(copy(t, 0, me, sibling, src=ins[t]))
            first += [copy(t, 1 + j, me, (*chip, c), src=ins[t]) for j, chip in enumerate(chips)]
        for cp in mine + first:
            cp.start()
        passed = []
        for j, chip in enumerate(chips):
            for t in range(n):
                copy(t, 1 + j, (*chip, c), me).wait_recv()
                fwd = copy(t, 4 + j, (*chip, c), sibling)
                fwd.start()
                passed.append(fwd)
        for t in range(n):
            copy(t, 0, sibling, me).wait_recv()
            for j, chip in enumerate(chips):
                copy(t, 4 + j, (*chip, 1 - c), me).wait_recv()
        for cp in first + passed:
            cp.wait_send()
        for cp in mine:
            cp.wait()

    outs = [jax.ShapeDtypeStruct((s.shape[0], NDEV) + s.shape[1:], s.dtype) for s in shards]
    return pl.pallas_call(body, name="all_gather", in_specs=[ANY] * n, out_specs=[ANY] * n, out_shape=outs,
                          scratch_shapes=[pltpu.SemaphoreType.DMA((7 * n,)), pltpu.SemaphoreType.DMA((7 * n,)),
                                          pltpu.SemaphoreType.DMA((n,))],
                          compiler_params=_comm_params())(*shards)


def _small_gather(v):
    def body(v_ref, o_ref, send_sems, recv_sems, local_sem):
        x, y, c = _place()
        mine = 4 * x + 2 * y + c
        local = pltpu.make_async_copy(v_ref, o_ref.at[mine], local_sem)
        local.start()
        sends = []
        for k in range(1, NDEV):
            fx, fy, fc = (k >> 2) & 1, (k >> 1) & 1, k & 1
            sends.append(pltpu.make_async_remote_copy(
                src_ref=v_ref, dst_ref=o_ref.at[mine], send_sem=send_sems.at[k - 1], recv_sem=recv_sems.at[k - 1],
                device_id=(_flip(x, fx), _flip(y, fy), _flip(c, fc)), device_id_type=MESH))
        for cp in sends:
            cp.start()
        for k in range(1, NDEV):
            fx, fy, fc = (k >> 2) & 1, (k >> 1) & 1, k & 1
            src = 4 * _flip(x, fx) + 2 * _flip(y, fy) + _flip(c, fc)
            pltpu.make_async_remote_copy(src_ref=v_ref, dst_ref=o_ref.at[src], send_sem=send_sems.at[k - 1],
                                         recv_sem=recv_sems.at[k - 1], device_id=(x, y, c), device_id_type=MESH).wait_recv()
        for cp in sends:
            cp.wait_send()
        local.wait()

    return pl.pallas_call(body, name="small_gather", in_specs=[ANY], out_specs=ANY,
                          out_shape=jax.ShapeDtypeStruct((NDEV,) + v.shape, v.dtype),
                          scratch_shapes=[pltpu.SemaphoreType.DMA((NDEV - 1,)), pltpu.SemaphoreType.DMA((NDEV - 1,)),
                                          pltpu.SemaphoreType.DMA],
                          compiler_params=_comm_params())(v)


REL_CHIPS = ((0, 0),) + CHIP_FLIPS


def _rs_d2d(grads):
    n = len(grads)

    def body(*refs):
        ins, owns, gots = refs[:n], refs[n:2 * n], refs[2 * n:3 * n]
        send_sems, recv_sems, local_sems = refs[3 * n:]
        x, y, c = _place()
        copies = []
        for t in range(n):
            for f, (fx, fy) in enumerate(REL_CHIPS):
                px, py = _flip(x, fx), _flip(y, fy)
                copies.append(pltpu.make_async_copy(ins[t].at[px, py, c], owns[t].at[f], local_sems.at[4 * t + f]))
                copies.append(pltpu.make_async_remote_copy(
                    src_ref=ins[t].at[px, py, 1 - c], dst_ref=gots[t].at[f], send_sem=send_sems.at[4 * t + f],
                    recv_sem=recv_sems.at[4 * t + f], device_id=(x, y, 1 - c), device_id_type=MESH))
        for cp in copies:
            cp.start()
        for cp in copies:
            cp.wait()

    g5 = [g.reshape((2, 2, 2) + g.shape[1:]) for g in grads]
    out = [jax.ShapeDtypeStruct((4,) + g.shape[1:], F32) for g in grads]
    res = pl.pallas_call(body, name="rs_d2d", in_specs=[ANY] * n, out_specs=[ANY] * (2 * n), out_shape=out + out,
                         scratch_shapes=[pltpu.SemaphoreType.DMA((4 * n,)), pltpu.SemaphoreType.DMA((4 * n,)),
                                         pltpu.SemaphoreType.DMA((4 * n,))],
                         compiler_params=_comm_params())(*g5)
    return res[:n], res[n:]


def _rs_ici(parts):
    n = len(parts)

    def body(*refs):
        ins, outs = refs[:n], refs[n:2 * n]
        send_sems, recv_sems = refs[2 * n:]
        x, y, c = _place()
        copies = []
        for t in range(n):
            for f, (fx, fy) in enumerate(CHIP_FLIPS):
                copies.append(pltpu.make_async_remote_copy(
                    src_ref=ins[t].at[f], dst_ref=outs[t].at[f], send_sem=send_sems.at[3 * t + f],
                    recv_sem=recv_sems.at[3 * t + f], device_id=(_flip(x, fx), _flip(y, fy), c), device_id_type=MESH))
        for cp in copies:
            cp.start()
        for cp in copies:
            cp.wait()

    out = [jax.ShapeDtypeStruct(p.shape, p.dtype) for p in parts]
    return pl.pallas_call(body, name="rs_ici", in_specs=[ANY] * n, out_specs=[ANY] * n, out_shape=out,
                          scratch_shapes=[pltpu.SemaphoreType.DMA((3 * n,)), pltpu.SemaphoreType.DMA((3 * n,))],
                          compiler_params=_comm_params())(*parts)


def _row_tile(rows):
    for t in (256, 128, 64, 32, 16, 8):
        if rows % t == 0:
            return t
    return rows


def _rs_chip_sum(own, got):
    _, a, b = own.shape
    ta = _row_tile(a)

    def kern(o_ref, g_ref, p_ref):
        p_ref[...] = (o_ref[...] + g_ref[...]).astype(BF16)

    src = BS((None, ta, b), lambda f, i: (f + 1, i, 0))
    return pl.pallas_call(kern, name="rs_chip_sum", grid=(3, a // ta), in_specs=[src, src],
                          out_specs=BS((None, ta, b), lambda f, i: (f, i, 0)),
                          out_shape=jax.ShapeDtypeStruct((3, a, b), BF16), compiler_params=_params(2))(own, got)


def _rs_final_sum(own, got, recv):
    _, a, b = own.shape
    ta = _row_tile(a)

    def kern(o_ref, g_ref, r_ref, s_ref):
        acc = o_ref[...] + g_ref[...]
        for f in range(3):
            acc = acc + r_ref[f].astype(F32)
        s_ref[...] = acc

    mine = BS((None, ta, b), lambda i: (0, i, 0))
    return pl.pallas_call(kern, name="rs_final_sum", grid=(a // ta,),
                          in_specs=[mine, mine, BS((3, ta, b), lambda i: (0, i, 0))],
                          out_specs=BS((ta, b), lambda i: (i, 0)),
                          out_shape=jax.ShapeDtypeStruct((a, b), F32), compiler_params=_params(1))(own, got, recv)


def _reduce_scatter(grads):
    owns, gots = _rs_d2d(grads)
    parts = [_rs_chip_sum(o, g) for o, g in zip(owns, gots)]
    recvs = _rs_ici(parts)
    return [_rs_final_sum(o, g, r) for o, g, r in zip(owns, gots, recvs)]


def _sum_devices(v):
    _, r, _ = v.shape

    def kern(v_ref, o_ref):
        acc = v_ref[0]
        for d in range(1, NDEV):
            acc = acc + v_ref[d]
        o_ref[...] = acc

    return pl.pallas_call(kern, name="sum_devices", out_shape=jax.ShapeDtypeStruct((r, LANE), F32),
                          compiler_params=_comm_params())(v)


def _loss_head(xf, target):
    s = xf.shape[0]
    tm = _tile(s, 512)

    def kern(x_ref, t_ref, dx_ref, l_ref):
        err = x_ref[...] - t_ref[...]
        dx_ref[...] = err * (1.0 / DM)
        part = jnp.broadcast_to(0.5 * jnp.sum(jnp.mean(err * err, axis=-1, keepdims=True), axis=0, keepdims=True), (8, LANE))

        @pl.when(pl.program_id(0) == 0)
        def _():
            l_ref[...] = part

        @pl.when(pl.program_id(0) > 0)
        def _():
            l_ref[...] += part

    row = BS((tm, DM), lambda i: (i, 0))
    return pl.pallas_call(kern, name="loss_head", grid=(s // tm,), in_specs=[row, row],
                          out_specs=[row, BS((8, LANE), lambda i: (0, 0))],
                          out_shape=[jax.ShapeDtypeStruct((s, DM), F32), jax.ShapeDtypeStruct((8, LANE), F32)],
                          compiler_params=_params(1))(xf, target)


def _adamw(w, g, m, v):
    rows, cols = w.shape
    tr = _row_tile(rows)

    def kern(w_ref, g_ref, m_ref, v_ref, d_ref, nm_ref, nv_ref):
        gv = g_ref[...]
        nm = ADAM_B1 * m_ref[...] + (1.0 - ADAM_B1) * gv
        nv = ADAM_B2 * v_ref[...] + (1.0 - ADAM_B2) * (gv * gv)
        m_hat = nm / (1.0 - ADAM_B1 ** ADAM_STEP)
        v_hat = nv / (1.0 - ADAM_B2 ** ADAM_STEP)
        d_ref[...] = -ADAM_LR * (m_hat / (jnp.sqrt(v_hat) + ADAM_EPS) + ADAM_WD * w_ref[...])
        nm_ref[...] = nm
        nv_ref[...] = nv

    blk = BS((tr, cols), lambda i: (i, 0))
    shp = jax.ShapeDtypeStruct((rows, cols), F32)
    return pl.pallas_call(kern, name="adamw", grid=(rows // tr,), in_specs=[blk] * 4, out_specs=[blk] * 3,
                          out_shape=[shp] * 3, compiler_params=_params(1))(w, g, m, v)


def _adamw_nd(w, g, m, v):
    shape = w.shape
    two = (math.prod(shape[:-1]), shape[-1])
    return tuple(o.reshape(shape) for o in _adamw(w.reshape(two), g.reshape(two), m.reshape(two), v.reshape(two)))


def _pack_small(parts):
    flat = jnp.concatenate([p.reshape(-1) for p in parts])
    pad = (-flat.shape[0]) % (8 * LANE)
    return jnp.pad(flat, (0, pad)).reshape(-1, LANE)


def _unpack_small(packed, shapes, lead=()):
    flat = packed.reshape(lead + (-1,))
    out, off = [], 0
    for shp in shapes:
        n = math.prod(shp)
        out.append(flat[..., off:off + n].reshape(lead + tuple(shp)))
        off += n
    return out


def _blocks_of_columns(w):
    k, n = w.shape
    return w.reshape(k, NDEV, n // NDEV).transpose(1, 0, 2)


def _columns_of_blocks(wb):
    n, k, c = wb.shape
    return wb.transpose(1, 0, 2).reshape(k, n * c)


WEIGHT_NAMES = ('g_mix_pre', 'g_mix_post', 'g_cross_pre', 'g_mem', 'g_cross_post', 'g_ffn_pre', 'g_ffn_post', 'w_xq',
                'w_xkv', 'w_xo', 'w_ffn_gu', 'w_ffn_down', 'ab_w_in', 'ab_b_f', 'ab_conv_w', 'ab_w_out', 'c_w_in',
                'c_conv_w', 'c_conv_b', 'c_w_a', 'c_b_a', 'c_w_i', 'c_b_i', 'c_lam', 'c_w_out')
BIG = ('w_xq', 'w_xkv', 'w_xo', 'w_ffn_gu', 'w_ffn_down', 'ab_w_in', 'ab_w_out', 'c_w_in', 'c_w_a', 'c_w_i', 'c_w_out')
SMALL_SHARDED = ('ab_conv_w', 'c_conv_w', 'c_conv_b', 'c_b_a', 'c_b_i', 'c_lam')
REPLICATED = ('g_mix_pre', 'g_mix_post', 'g_cross_pre', 'g_mem', 'g_cross_post', 'g_ffn_pre', 'g_ffn_post', 'ab_b_f')


def _small_full(name, gathered):
    nd = gathered.ndim
    return jnp.moveaxis(gathered, 0, nd - 2).reshape(gathered.shape[1:-1] + (NDEV * gathered.shape[-1],))


def _small_shard(full, dev):
    c = full.shape[-1] // NDEV
    return lax.dynamic_slice_in_dim(full, dev * c, c, axis=full.ndim - 1)


def kernel(x, mem, g_mix_pre, g_mix_post, g_cross_pre, g_mem, g_cross_post, g_ffn_pre, g_ffn_post, w_xq, w_xkv, w_xo, w_ffn_gu, w_ffn_down, ab_w_in, ab_b_f, ab_conv_w, ab_w_out, c_w_in, c_conv_w, c_conv_b, c_w_a, c_b_a, c_w_i, c_b_i, c_lam, c_w_out, loss_target, m_g_mix_pre, m_g_mix_post, m_g_cross_pre, m_g_mem, m_g_cross_post, m_g_ffn_pre, m_g_ffn_post, m_w_xq, m_w_xkv, m_w_xo, m_w_ffn_gu, m_w_ffn_down, m_ab_w_in, m_ab_b_f, m_ab_conv_w, m_ab_w_out, m_c_w_in, m_c_conv_w, m_c_conv_b, m_c_w_a, m_c_b_a, m_c_w_i, m_c_b_i, m_c_lam, m_c_w_out, v_g_mix_pre, v_g_mix_post, v_g_cross_pre, v_g_mem, v_g_cross_post, v_g_ffn_pre, v_g_ffn_post, v_w_xq, v_w_xkv, v_w_xo, v_w_ffn_gu, v_w_ffn_down, v_ab_w_in, v_ab_b_f, v_ab_conv_w, v_ab_w_out, v_c_w_in, v_c_conv_w, v_c_conv_b, v_c_w_a, v_c_b_a, v_c_w_i, v_c_b_i, v_c_lam, v_c_w_out):
    args = locals()
    w = {n: args[n] for n in WEIGHT_NAMES}
    mom = {n: args["m_" + n] for n in WEIGHT_NAMES}
    var = {n: args["v_" + n] for n in WEIGHT_NAMES}
    dev = 4 * lax.axis_index("x") + 2 * lax.axis_index("y") + lax.axis_index("c")
    xs, mems, target = x[0], mem[0], loss_target[0]
    n_even, n_odd = (DEPTH + 1) // 2, DEPTH // 2

    def shard3(name):
        s = w[name].astype(BF16)
        return s.reshape(s.shape[0], -1, s.shape[-1]) if s.ndim == 4 else s
    full = dict(zip(BIG, _all_gather([shard3(n) for n in BIG])))
    small_shapes = [w[n].shape for n in SMALL_SHARDED]
    gathered_small = _unpack_small(_small_gather(_pack_small([w[n] for n in SMALL_SHARDED])), small_shapes, (NDEV,))
    small = {n: _small_full(n, g) for n, g in zip(SMALL_SHARDED, gathered_small)}

    wq = full['w_xq'].reshape(DEPTH, DM, DM)
    wo = full['w_xo'].reshape(DEPTH, DM, DM)
    wkv = full['w_xkv']
    wgu = full['w_ffn_gu']
    wd = full['w_ffn_down'].reshape(DEPTH, D_FF, DM)
    ab_wall = [_ab_pack(_columns_of_blocks(full['ab_w_in'][e])) for e in range(n_even)]
    ab_wout = full['ab_w_out'].reshape(n_even, DM, DM)
    ab_bfb = jnp.broadcast_to(ab_b_f[:, :, None], (n_even, FOX_H, LANE))
    c_win = full['c_w_in']
    c_wout = full['c_w_out'].reshape(n_odd, DM, DM)

    def gate_w(name):
        g = full[name].reshape(n_odd, NDEV, LRU_NB, LRU_BW // NDEV, LRU_BW)
        return g.transpose(0, 2, 1, 3, 4).reshape(n_odd, LRU_NB, LRU_BW, LRU_BW)
    c_wai = jnp.stack([gate_w('c_w_a'), gate_w('c_w_i')], axis=1)
    c_bai = jnp.stack([small['c_b_a'].reshape(n_odd, DM), small['c_b_i'].reshape(n_odd, DM)], axis=1)
    row = lambda a, l: a[l][None]

    def mixer_args(l):
        if l % 2 == 0:
            e = l // 2
            return (row(g_mix_pre, l), row(g_mix_post, l), ab_wall[e], ab_bfb[e], small['ab_conv_w'][e], ab_wout[e])
        o = l // 2
        return (row(g_mix_pre, l), row(g_mix_post, l), c_win[o], small['c_conv_w'][o], row(small['c_conv_b'], o),
                c_wai[o], c_bai[o], row(small['c_lam'], o), c_wout[o])

    def cross_args(l):
        return (row(g_cross_pre, l), row(g_mem, l), row(g_cross_post, l), wq[l], wkv[l], wo[l])

    def ffn_args(l):
        return (row(g_ffn_pre, l), row(g_ffn_post, l), wgu[l], wd[l])

    saved = []
    h = xs
    for l in range(DEPTH):
        h, s_mix = (_fox_layer_fwd if l % 2 == 0 else _lru_layer_fwd)(h, *mixer_args(l))
        h, s_cross = _cross_fwd(h, mems, *cross_args(l))
        h, s_ffn = _ffn_fwd(h, *ffn_args(l))
        saved.append((s_mix, s_cross, s_ffn))
    dx, loss_rep = _loss_head(h, target)
    loss = lax.psum(loss_rep[0, 0], ("x", "y", "c"))

    grads = {n: [None] * w[n].shape[0] for n in BIG}
    partial = {n: [None] * w[n].shape[0] for n in REPLICATED + SMALL_SHARDED}
    for l in reversed(range(DEPTH)):
        s_mix, s_cross, s_ffn = saved[l]
        dx, partial['g_ffn_pre'][l], partial['g_ffn_post'][l], dwgu, dwd = _ffn_bwd(dx, s_ffn, *ffn_args(l))
        (dx, partial['g_cross_pre'][l], partial['g_mem'][l], partial['g_cross_post'][l], dwq, dwkv, dwo) = _cross_bwd(
            dx, s_cross, mems, *cross_args(l))
        layer = {'w_xq': (l, dwq.reshape(NDEV, DM // NDEV, DM)), 'w_xkv': (l, dwkv), 'w_xo': (l, dwo.reshape(NDEV, DM // NDEV, DM)),
                 'w_ffn_gu': (l, dwgu), 'w_ffn_down': (l, dwd.reshape(NDEV, D_FF // NDEV, DM))}
        if l % 2 == 0:
            e = l // 2
            (dx, partial['g_mix_pre'][l], partial['g_mix_post'][l], dwall, partial['ab_b_f'][e], partial['ab_conv_w'][e],
             dwout) = _fox_layer_bwd(dx, s_mix, *mixer_args(l))
            layer['ab_w_in'] = (e, _blocks_of_columns(_ab_unpack(dwall)))
            layer['ab_w_out'] = (e, dwout.reshape(NDEV, DM // NDEV, DM))
        else:
            o = l // 2
            (dx, partial['g_mix_pre'][l], partial['g_mix_post'][l], dwin, partial['c_conv_w'][o], dconvb, dwai, dbai, dlam,
             dwout) = _lru_layer_bwd(dx, s_mix, *mixer_args(l))
            partial['c_conv_b'][o], partial['c_lam'][o] = dconvb[0], dlam[0]
            partial['c_b_a'][o], partial['c_b_i'][o] = dbai[0].reshape(LRU_NB, LRU_BW), dbai[1].reshape(LRU_NB, LRU_BW)
            rows = LRU_BW // NDEV
            by_dev = lambda d: d.reshape(LRU_NB, NDEV, rows, LRU_BW).transpose(1, 0, 2, 3).reshape(NDEV, LRU_NB * rows, LRU_BW)
            layer['c_w_in'] = (o, dwin)
            layer['c_w_a'] = (o, by_dev(dwai[0]))
            layer['c_w_i'] = (o, by_dev(dwai[1]))
            layer['c_w_out'] = (o, dwout.reshape(NDEV, DM // NDEV, DM))
        names = list(layer)
        for n, g in zip(names, _reduce_scatter([layer[n][1] for n in names])):
            grads[n][layer[n][0]] = g

    small_names = REPLICATED + SMALL_SHARDED
    small_parts = [jnp.stack([p.reshape(w[n].shape[1:] if n in REPLICATED else small[n].shape[1:]) for p in partial[n]])
                   for n in small_names]
    reduced = _unpack_small(_sum_devices(_small_gather(_pack_small(small_parts))), [p.shape for p in small_parts])
    grad = {}
    for n, g in zip(small_names, reduced):
        grad[n] = g if n in REPLICATED else _small_shard(g, dev)
    for n in BIG:
        grad[n] = jnp.stack(grads[n]).reshape(w[n].shape)

    delta, new_m, new_v = {}, {}, {}
    for n in BIG:
        delta[n], new_m[n], new_v[n] = _adamw_nd(w[n], grad[n], mom[n], var[n])
    shapes = [w[n].shape for n in small_names]
    packed = [_pack_small([t[n] for n in small_names]) for t in (w, grad, mom, var)]
    for res, out in zip(_adamw(*packed), (delta, new_m, new_v)):
        for n, val in zip(small_names, _unpack_small(res, shapes)):
            out[n] = val

    return (loss, dx[None], *[grad[n] for n in WEIGHT_NAMES], *[delta[n] for n in WEIGHT_NAMES],
            *[new_m[n] for n in WEIGHT_NAMES], *[new_v[n] for n in WEIGHT_NAMES])
```

```python
import functools
import math

import jax
import jax.numpy as jnp
from jax import lax
from jax.experimental import pallas as pl
from jax.experimental.pallas import tpu as pltpu

F32 = jnp.float32
BF16 = jnp.bfloat16
BS = pl.BlockSpec
ANY = pl.BlockSpec(memory_space=pl.ANY)
MESH = pl.DeviceIdType.MESH

DM = 1024
DEPTH = 4
EPS = 1e-6
NEG = -1e30
FOX_W = 512
FOX_HD = 64
FOX_H = 8
SC_W = 512
SC_K = 3
AB_IN = 3 * FOX_W + FOX_H + 3 * SC_W
AB_PAD = 3200
LRU_BW = 256
LRU_NB = 4
RG_K = 4
RG_C = 8.0
MEM_H = 4
MEM_HD = 256
D_FF = 2816
NDEV = 8
FFB = 2 * D_FF // NDEV
ADAM_LR, ADAM_B1, ADAM_B2, ADAM_EPS, ADAM_WD, ADAM_STEP = 0.001, 0.9, 0.999, 1e-08, 0.01, 10

LANE = 128
VMEM_LIMIT = 48 * 1024 * 1024


def _params(ngrid):
    return pltpu.CompilerParams(dimension_semantics=("arbitrary",) * ngrid, vmem_limit_bytes=VMEM_LIMIT)


def _tile(n, t):
    return t if n % t == 0 else n


def _mm(name, a, b, *, grid, a_spec, b_spec, o_spec, out_shape, dn, out_dtype=F32):
    nred = grid[-1]
    ngrid = len(grid)

    def kern(a_ref, b_ref, o_ref, *scratch):
        p = lax.dot_general(a_ref[...].astype(BF16), b_ref[...].astype(BF16), (dn, ((), ())),
                            preferred_element_type=F32)
        if nred == 1:
            o_ref[...] = p.astype(o_ref.dtype)
            return
        acc = scratch[0] if scratch else o_ref
        r = pl.program_id(ngrid - 1)

        @pl.when(r == 0)
        def _():
            acc[...] = p

        @pl.when(r > 0)
        def _():
            acc[...] += p

        if scratch:
            @pl.when(r == nred - 1)
            def _():
                o_ref[...] = acc[...].astype(o_ref.dtype)

    blk = tuple(d for d in o_spec.block_shape if d is not None)
    scratch = [pltpu.VMEM(blk, F32)] if (nred > 1 and out_dtype != F32) else []
    return pl.pallas_call(kern, name=name, grid=grid, in_specs=[a_spec, b_spec], out_specs=o_spec,
                          out_shape=jax.ShapeDtypeStruct(out_shape, out_dtype), scratch_shapes=scratch,
                          compiler_params=_params(ngrid))(a, b)


NN = ((1,), (0,))
NT = ((1,), (1,))
TN = ((0,), (0,))


def _mm_nn(name, a, w, out_dtype=F32, tn=None):
    m, k = a.shape
    n = w.shape[1]
    tm = _tile(m, 512)
    tn = n if tn is None else tn
    return _mm(name, a, w, grid=(m // tm, n // tn, 1), a_spec=BS((tm, k), lambda i, j, r: (i, 0)),
               b_spec=BS((k, tn), lambda i, j, r: (0, j)), o_spec=BS((tm, tn), lambda i, j, r: (i, j)),
               out_shape=(m, n), dn=NN, out_dtype=out_dtype)


def _mm_nt(name, a, w, out_dtype=F32, tn=None):
    m, n = a.shape
    k = w.shape[0]
    tm = _tile(m, 512)
    tn = n if tn is None else tn
    return _mm(name, a, w, grid=(m // tm, n // tn), a_spec=BS((tm, tn), lambda i, r: (i, r)),
               b_spec=BS((k, tn), lambda i, r: (0, r)), o_spec=BS((tm, k), lambda i, r: (i, 0)),
               out_shape=(m, k), dn=NT, out_dtype=out_dtype)


def _mm_tn(name, a, b, tn=None):
    m, k = a.shape
    n = b.shape[1]
    tm = _tile(m, 512)
    tn = n if tn is None else tn
    return _mm(name, a, b, grid=(n // tn, m // tm), a_spec=BS((tm, k), lambda j, r: (r, 0)),
               b_spec=BS((tm, tn), lambda j, r: (r, j)), o_spec=BS((k, tn), lambda j, r: (0, j)),
               out_shape=(k, n), dn=TN)


def _bmm_nn(name, a, w, out_dtype=F32):
    m, k = a.shape
    g, _, n = w.shape
    tm = _tile(m, 512)
    return _mm(name, a, w, grid=(g, m // tm, 1), a_spec=BS((tm, k), lambda q, i, r: (i, 0)),
               b_spec=BS((None, k, n), lambda q, i, r: (q, 0, 0)), o_spec=BS((None, tm, n), lambda q, i, r: (q, i, 0)),
               out_shape=(g, m, n), dn=NN, out_dtype=out_dtype)


def _bmm_tn(name, a, b):
    m, k = a.shape
    g, _, n = b.shape
    tm = _tile(m, 512)
    return _mm(name, a, b, grid=(g, m // tm), a_spec=BS((tm, k), lambda q, r: (r, 0)),
               b_spec=BS((None, tm, n), lambda q, r: (q, r, 0)), o_spec=BS((None, k, n), lambda q, r: (q, 0, 0)),
               out_shape=(g, k, n), dn=TN)


def _bmm_nt_sum(name, a, w):
    g, m, n = a.shape
    k = w.shape[1]
    tm = _tile(m, 512)
    return _mm(name, a, w, grid=(m // tm, g), a_spec=BS((None, tm, n), lambda i, q: (q, i, 0)),
               b_spec=BS((None, k, n), lambda i, q: (q, 0, 0)), o_spec=BS((tm, k), lambda i, q: (i, 0)),
               out_shape=(m, k), dn=NT)


def _bmm_nn_sum(name, a, w):
    g, m, k = a.shape
    n = w.shape[2]
    tm = _tile(m, 512)
    return _mm(name, a, w, grid=(m // tm, g), a_spec=BS((None, tm, k), lambda i, q: (q, i, 0)),
               b_spec=BS((None, k, n), lambda i, q: (q, 0, 0)), o_spec=BS((tm, n), lambda i, q: (i, 0)),
               out_shape=(m, n), dn=NN)


def _bbmm_tn(name, a, b):
    g, m, k = a.shape
    n = b.shape[2]
    tm = _tile(m, 512)
    return _mm(name, a, b, grid=(g, m // tm), a_spec=BS((None, tm, k), lambda q, r: (q, r, 0)),
               b_spec=BS((None, tm, n), lambda q, r: (q, r, 0)), o_spec=BS((None, k, n), lambda q, r: (q, 0, 0)),
               out_shape=(g, k, n), dn=TN)


def _rstd(x):
    return lax.rsqrt(jnp.mean(x * x, axis=-1, keepdims=True) + EPS)


def _norm_fwd(x, g):
    rows = x.shape[0]
    tm = _tile(rows, 512)

    def kern(x_ref, g_ref, h_ref):
        xv = x_ref[...]
        h_ref[...] = ((xv * _rstd(xv)) * g_ref[...]).astype(BF16)

    return pl.pallas_call(kern, name="norm_fwd", grid=(rows // tm,),
                          in_specs=[BS((tm, DM), lambda i: (i, 0)), BS((1, DM), lambda i: (0, 0))],
                          out_specs=BS((tm, DM), lambda i: (i, 0)),
                          out_shape=jax.ShapeDtypeStruct((rows, DM), BF16), compiler_params=_params(1))(x, g)


def _norm_res(x, y, g):
    rows = x.shape[0]
    tm = _tile(rows, 512)

    def kern(x_ref, y_ref, g_ref, o_ref):
        yv = y_ref[...]
        o_ref[...] = x_ref[...] + (yv * _rstd(yv)) * g_ref[...]

    row = BS((tm, DM), lambda i: (i, 0))
    return pl.pallas_call(kern, name="norm_res", grid=(rows // tm,),
                          in_specs=[row, row, BS((1, DM), lambda i: (0, 0))], out_specs=row,
                          out_shape=jax.ShapeDtypeStruct((rows, DM), F32), compiler_params=_params(1))(x, y, g)


def _norm_bwd(z, dout, g, resid, out_dtype):
    rows = z.shape[0]
    tm = _tile(rows, 512)
    has_res = resid is not None

    def kern(*refs):
        if has_res:
            z_ref, d_ref, g_ref, r_ref, dz_ref, dg_ref = refs
        else:
            z_ref, d_ref, g_ref, dz_ref, dg_ref = refs
        zv = z_ref[...]
        dv = d_ref[...].astype(F32)
        r = _rstd(zv)
        zh = zv * r
        dzh = dv * g_ref[...]
        dz = r * (dzh - zh * jnp.mean(dzh * zh, axis=-1, keepdims=True))
        if has_res:
            dz = dz + r_ref[...]
        dz_ref[...] = dz.astype(dz_ref.dtype)
        part = jnp.sum(dv * zh, axis=0, keepdims=True)

        @pl.when(pl.program_id(0) == 0)
        def _():
            dg_ref[...] = part

        @pl.when(pl.program_id(0) > 0)
        def _():
            dg_ref[...] += part

    row = BS((tm, DM), lambda i: (i, 0))
    vec = BS((1, DM), lambda i: (0, 0))
    ins = [row, row, vec] + ([row] if has_res else [])
    args = (z, dout, g) + ((resid,) if has_res else ())
    return pl.pallas_call(kern, name="norm_bwd_res" if has_res else "norm_bwd", grid=(rows // tm,), in_specs=ins,
                          out_specs=[row, vec],
                          out_shape=[jax.ShapeDtypeStruct((rows, DM), out_dtype), jax.ShapeDtypeStruct((1, DM), F32)],
                          compiler_params=_params(1))(*args)


def _ffn_up(h, wgu4):
    s = h.shape[0]
    tm = _tile(s, 512)

    def kern(h_ref, w_ref, gu_ref, a_ref):
        hv = h_ref[...]
        gate = jnp.dot(hv, w_ref[0], preferred_element_type=F32)
        up = jnp.dot(hv, w_ref[1], preferred_element_type=F32)
        gu_ref[0] = gate
        gu_ref[1] = up
        a_ref[...] = (gate * jax.nn.sigmoid(gate) * up).astype(BF16)

    return pl.pallas_call(
        kern, name="ffn_up", grid=(4, s // tm),
        in_specs=[BS((tm, DM), lambda j, i: (i, 0)), BS((2, None, DM, FFB), lambda j, i: (0, j, 0, 0))],
        out_specs=[BS((2, None, tm, FFB), lambda j, i: (0, j, i, 0)), BS((None, tm, FFB), lambda j, i: (j, i, 0))],
        out_shape=[jax.ShapeDtypeStruct((2, 4, s, FFB), F32), jax.ShapeDtypeStruct((4, s, FFB), BF16)],
        compiler_params=_params(2))(h, wgu4)


def _ffn_da(dy, wd4, gu):
    s = dy.shape[0]
    tm = _tile(s, 512)

    def kern(dy_ref, w_ref, gu_ref, o_ref):
        da = lax.dot_general(dy_ref[...], w_ref[...], (NT, ((), ())), preferred_element_type=F32)
        gate = gu_ref[0]
        up = gu_ref[1]
        sg = jax.nn.sigmoid(gate)
        o_ref[0] = (da * up * (sg * (1.0 + gate * (1.0 - sg)))).astype(BF16)
        o_ref[1] = (da * (gate * sg)).astype(BF16)

    blk = BS((2, None, tm, FFB), lambda j, i: (0, j, i, 0))
    return pl.pallas_call(
        kern, name="ffn_da", grid=(4, s // tm),
        in_specs=[BS((tm, DM), lambda j, i: (i, 0)), BS((None, FFB, DM), lambda j, i: (j, 0, 0)), blk],
        out_specs=blk, out_shape=jax.ShapeDtypeStruct((2, 4, s, FFB), BF16), compiler_params=_params(2))(dy, wd4, gu)


def _ffn_fwd(x, gpre, gpost, wgu, wd):
    h = _norm_fwd(x, gpre)
    gu, a = _ffn_up(h, wgu.reshape(2, 4, DM, FFB))
    y = _bmm_nn_sum("ffn_down", a, wd.reshape(4, FFB, DM))
    return _norm_res(x, y, gpost), (x, h, gu, a, y)


def _ffn_bwd(dxo, saved, gpre, gpost, wgu, wd):
    x, h, gu, a, y = saved
    s = x.shape[0]
    dy, dgpost = _norm_bwd(y, dxo, gpost, None, BF16)
    dgu = _ffn_da(dy, wd.reshape(4, FFB, DM), gu).reshape(8, s, FFB)
    dwd = _bmm_tn_a3("ffn_dwd", a, dy)
    dwgu = _bmm_tn("ffn_dwgu", h, dgu)
    dh = _bmm_nt_sum("ffn_dh", dgu, wgu)
    dx, dgpre = _norm_bwd(x, dh, gpre, dxo, F32)
    return dx, dgpre, dgpost, dwgu, dwd.reshape(D_FF, DM)


def _bmm_tn_a3(name, a, b):
    g, m, k = a.shape
    n = b.shape[1]
    tm = _tile(m, 512)
    return _mm(name, a, b, grid=(g, m // tm), a_spec=BS((None, tm, k), lambda q, r: (q, r, 0)),
               b_spec=BS((tm, n), lambda q, r: (r, 0)), o_spec=BS((None, k, n), lambda q, r: (q, 0, 0)),
               out_shape=(g, k, n), dn=TN)


def _softmax_rows(s):
    m = jnp.max(s, axis=-1, keepdims=True)
    p = jnp.exp(s - m)
    return p / jnp.sum(p, axis=-1, keepdims=True)


def _xattn_fwd_call(h, wq, kv):
    s = h.shape[0]
    mlen = kv.shape[1]
    tm = _tile(s, 512)
    scale = MEM_HD ** -0.5

    def kern(h_ref, w_ref, k_ref, v_ref, q_ref, o_ref):
        q = jnp.dot(h_ref[...], w_ref[...], preferred_element_type=F32).astype(BF16)
        q_ref[...] = q
        sc = lax.dot_general(q, k_ref[...], (NT, ((), ())), preferred_element_type=F32) * scale
        p = _softmax_rows(sc)
        o_ref[...] = jnp.dot(p.astype(BF16), v_ref[...], preferred_element_type=F32).astype(BF16)

    blk = BS((tm, MEM_HD), lambda i, hd: (i, hd))
    return pl.pallas_call(
        kern, name="xattn_fwd", grid=(s // tm, MEM_H),
        in_specs=[BS((tm, DM), lambda i, hd: (i, 0)), BS((DM, MEM_HD), lambda i, hd: (0, hd)),
                  BS((None, mlen, MEM_HD), lambda i, hd: (hd, 0, 0)),
                  BS((None, mlen, MEM_HD), lambda i, hd: (MEM_H + hd, 0, 0))],
        out_specs=[blk, blk],
        out_shape=[jax.ShapeDtypeStruct((s, DM), BF16), jax.ShapeDtypeStruct((s, DM), BF16)],
        compiler_params=_params(2))(h, wq, kv, kv)


def _xattn_bwd_call(q, kv, do):
    s = q.shape[0]
    mlen = kv.shape[1]
    tm = _tile(s, 512)
    scale = MEM_HD ** -0.5

    def kern(q_ref, k_ref, v_ref, do_ref, dq_ref, dkv_ref):
        qv, kvv, vv, dov = q_ref[...], k_ref[...], v_ref[...], do_ref[...]
        sc = lax.dot_general(qv, kvv, (NT, ((), ())), preferred_element_type=F32) * scale
        p = _softmax_rows(sc)
        dp = lax.dot_general(dov, vv, (NT, ((), ())), preferred_element_type=F32)
        ds = (p * (dp - jnp.sum(dp * p, axis=-1, keepdims=True)) * scale).astype(BF16)
        dq_ref[...] = jnp.dot(ds, kvv, preferred_element_type=F32).astype(BF16)
        dk = lax.dot_general(ds, qv, (TN, ((), ())), preferred_element_type=F32)
        dv = lax.dot_general(p.astype(BF16), dov, (TN, ((), ())), preferred_element_type=F32)

        @pl.when(pl.program_id(1) == 0)
        def _():
            dkv_ref[0] = dk
            dkv_ref[1] = dv

        @pl.when(pl.program_id(1) > 0)
        def _():
            dkv_ref[0] += dk
            dkv_ref[1] += dv

    blk = BS((tm, MEM_HD), lambda hd, i: (i, hd))
    return pl.pallas_call(
        kern, name="xattn_bwd", grid=(MEM_H, s // tm),
        in_specs=[blk, BS((None, mlen, MEM_HD), lambda hd, i: (hd, 0, 0)),
                  BS((None, mlen, MEM_HD), lambda hd, i: (MEM_H + hd, 0, 0)), blk],
        out_specs=[blk, BS((2, None, mlen, MEM_HD), lambda hd, i: (0, hd, 0, 0))],
        out_shape=[jax.ShapeDtypeStruct((s, DM), BF16), jax.ShapeDtypeStruct((2, MEM_H, mlen, MEM_HD), F32)],
        compiler_params=_params(2))(q, kv, kv, do)


def _cross_fwd(x, mem, gpre, gmem, gpost, wq, wkv, wo):
    h = _norm_fwd(x, gpre)
    mn = _norm_fwd(mem, gmem)
    kv = _bmm_nn("xattn_kv", mn, wkv, BF16)
    q, o = _xattn_fwd_call(h, wq, kv)
    y = _mm_nn("xattn_out", o, wo)
    return _norm_res(x, y, gpost), (x, h, mn, kv, q, o, y)


def _cross_bwd(dxo, saved, mem, gpre, gmem, gpost, wq, wkv, wo):
    x, h, mn, kv, q, o, y = saved
    mlen = mem.shape[0]
    dy, dgpost = _norm_bwd(y, dxo, gpost, None, BF16)
    do = _mm_nt("xattn_do", dy, wo, BF16)
    dwo = _mm_tn("xattn_dwo", o, dy)
    dq, dkv = _xattn_bwd_call(q, kv, do)
    dwq = _mm_tn("xattn_dwq", h, dq)
    dh = _mm_nt("xattn_dh", dq, wq)
    dkv8 = dkv.reshape(8, mlen, MEM_HD)
    dwkv = _bmm_tn("xattn_dwkv", mn, dkv8)
    dmn = _bmm_nt_sum("xattn_dmn", dkv8, wkv)
    _, dgmem = _norm_bwd(mem, dmn, gmem, None, BF16)
    dx, dgpre = _norm_bwd(x, dh, gpre, dxo, F32)
    return dx, dgpre, dgmem, dgpost, dwq, dwkv, dwo


def _log_sigmoid(z):
    return jnp.minimum(z, 0.0) - jnp.log1p(jnp.exp(-jnp.abs(z)))


def _lane_scan_steps():
    return (1, 2, 4, 8, 16, 32, 64)


def _fox_cum(frow, bfb):
    s = frow.shape[1]

    def kern(f_ref, b_ref, o_ref):
        lane = lax.broadcasted_iota(jnp.int32, (FOX_H, LANE), 1)
        carry = jnp.zeros((FOX_H, 1), F32)
        for c in range(s // LANE):
            sl = slice(c * LANE, (c + 1) * LANE)
            lf = _log_sigmoid(f_ref[:, sl] + b_ref[...])
            v = lf
            for d in _lane_scan_steps():
                v = v + jnp.where(lane >= d, pltpu.roll(v, d, 1), 0.0)
            o_ref[:, sl] = v + carry
            carry = carry + jnp.sum(lf, axis=1, keepdims=True)

    return pl.pallas_call(kern, name="fox_cum", out_shape=jax.ShapeDtypeStruct((FOX_H, s), F32),
                          compiler_params=pltpu.CompilerParams(vmem_limit_bytes=VMEM_LIMIT))(frow, bfb)


def _fox_dlogf(dcq, dck, frow, bfb):
    s = frow.shape[1]

    def kern(q_ref, d_ref, f_ref, b_ref, df_ref, db_ref):
        lane = lax.broadcasted_iota(jnp.int32, (FOX_H, LANE), 1)
        carry = jnp.zeros((FOX_H, 1), F32)
        dbf = jnp.zeros((FOX_H, 1), F32)
        for c in reversed(range(s // LANE)):
            sl = slice(c * LANE, (c + 1) * LANE)
            dc = q_ref[:, sl] - d_ref[:, sl]
            v = dc
            for d in _lane_scan_steps():
                v = v + jnp.where(lane < LANE - d, pltpu.roll(v, LANE - d, 1), 0.0)
            v = v + carry
            carry = carry + jnp.sum(dc, axis=1, keepdims=True)
            df = v * jax.nn.sigmoid(-(f_ref[:, sl] + b_ref[...]))
            df_ref[:, sl] = df
            dbf = dbf + jnp.sum(df, axis=1, keepdims=True)
        db_ref[...] = jnp.broadcast_to(dbf, (FOX_H, LANE))

    return pl.pallas_call(kern, name="fox_dlogf",
                          out_shape=[jax.ShapeDtypeStruct((FOX_H, s), F32), jax.ShapeDtypeStruct((FOX_H, LANE), F32)],
                          compiler_params=pltpu.CompilerParams(vmem_limit_bytes=VMEM_LIMIT))(dcq, dck, frow, bfb)


FOX_TQ = 512
Q_COL, K_COL, V_COL = 0, FOX_W // LANE, 2 * FOX_W // LANE
B_COL, C_COL, U_COL = 12, 16, 20


def _fox_logits(qm, kb, cc, cr, causal, scale, reps):
    sc = lax.dot_general(qm, kb, (NT, ((), ())), preferred_element_type=F32) * scale
    sc = sc + jnp.tile(cc, (1, reps)) - cr
    return jnp.where(causal, sc, NEG)


def _fox_fwd_call(proj, cumc, cumr):
    s = proj.shape[0]
    tq = _tile(s, FOX_TQ)
    nq = s // tq
    reps = tq // LANE
    scale = FOX_HD ** -0.5

    def kern(q_ref, k_ref, v_ref, cc_ref, cr_ref, o_ref, lse_ref, m_s, l_s, acc_s):
        i = pl.program_id(1)
        j = pl.program_id(2)
        lane = lax.broadcasted_iota(jnp.int32, (tq, LANE), 1)

        @pl.when(j == 0)
        def _():
            m_s[...] = jnp.full(m_s.shape, NEG, F32)
            l_s[...] = jnp.zeros(l_s.shape, F32)
            acc_s[...] = jnp.zeros(acc_s.shape, F32)

        @pl.when(j <= i)
        def _():
            qv = q_ref[...]
            kb = k_ref[...].astype(BF16)
            vb = v_ref[...].astype(BF16)
            causal = (i * tq + lax.broadcasted_iota(jnp.int32, (tq, tq), 0)
                      >= j * tq + lax.broadcasted_iota(jnp.int32, (tq, tq), 1))
            for hh in range(2):
                sel = (lane < FOX_HD) if hh == 0 else (lane >= FOX_HD)
                qm = jnp.where(sel, qv, 0.0).astype(BF16)
                sc = _fox_logits(qm, kb, cc_ref[hh], cr_ref[hh:hh + 1, :], causal, scale, reps)
                m_prev = m_s[hh]
                m_new = jnp.maximum(m_prev, jnp.max(sc, axis=-1, keepdims=True))
                alpha = jnp.exp(m_prev - m_new)
                p = jnp.exp(sc - m_new)
                l_s[hh] = alpha * l_s[hh] + jnp.sum(p, axis=-1, keepdims=True)
                acc_s[hh] = alpha * acc_s[hh] + jnp.dot(p.astype(BF16), vb, preferred_element_type=F32)
                m_s[hh] = m_new

        @pl.when(j == i)
        def _():
            o_ref[...] = jnp.where(lane < FOX_HD, acc_s[0] / l_s[0], acc_s[1] / l_s[1])
            for hh in range(2):
                lse_ref[hh] = jnp.broadcast_to(m_s[hh] + jnp.log(l_s[hh]), (tq, LANE))

    kvi = lambda hp, i, j: jnp.minimum(j, i)
    return pl.pallas_call(
        kern, name="fox_fwd", grid=(4, nq, nq),
        in_specs=[BS((tq, LANE), lambda hp, i, j: (i, Q_COL + hp)),
                  BS((tq, LANE), lambda hp, i, j: (kvi(hp, i, j), K_COL + hp)),
                  BS((tq, LANE), lambda hp, i, j: (kvi(hp, i, j), V_COL + hp)),
                  BS((2, tq, LANE), lambda hp, i, j: (hp, i, 0)),
                  BS((None, 2, tq), lambda hp, i, j: (hp, 0, kvi(hp, i, j)))],
        out_specs=[BS((tq, LANE), lambda hp, i, j: (i, hp)), BS((2, tq, LANE), lambda hp, i, j: (hp, i, 0))],
        out_shape=[jax.ShapeDtypeStruct((s, FOX_W), F32), jax.ShapeDtypeStruct((FOX_H, s, LANE), F32)],
        scratch_shapes=[pltpu.VMEM((2, tq, 1), F32), pltpu.VMEM((2, tq, 1), F32), pltpu.VMEM((2, tq, LANE), F32)],
        compiler_params=_params(3))(proj, proj, proj, cumc, cumr)


ROWSUM_M = 16


def _fox_bwd_call(proj, o, lse, dcat, cumc, cumr):
    s = proj.shape[0]
    tq = _tile(s, FOX_TQ)
    nq = s // tq
    reps = tq // LANE
    scale = FOX_HD ** -0.5

    def kern(q_ref, k_ref, v_ref, do_ref, o_ref, lse_ref, cc_ref, cr_ref, dq_ref, dk_ref, dv_ref, dck_ref, dcq_ref):
        j = pl.program_id(1)
        i = pl.program_id(2)
        lane = lax.broadcasted_iota(jnp.int32, (tq, LANE), 1)
        ones = jnp.ones((ROWSUM_M, tq), BF16)

        @pl.when((j == 0) & (i == 0))
        def _():
            dq_ref[...] = jnp.zeros(dq_ref.shape, F32)
            dcq_ref[...] = jnp.zeros(dcq_ref.shape, F32)

        @pl.when(i == j)
        def _():
            dk_ref[...] = jnp.zeros(dk_ref.shape, F32)
            dv_ref[...] = jnp.zeros(dv_ref.shape, F32)
            dck_ref[...] = jnp.zeros(dck_ref.shape, F32)

        @pl.when(i >= j)
        def _():
            qv = q_ref[...]
            dov = do_ref[...]
            ov = o_ref[...]
            kb = k_ref[...].astype(BF16)
            vb = v_ref[...].astype(BF16)
            causal = (i * tq + lax.broadcasted_iota(jnp.int32, (tq, tq), 0)
                      >= j * tq + lax.broadcasted_iota(jnp.int32, (tq, tq), 1))
            dq_t = jnp.zeros((tq, LANE), F32)
            dk_t = jnp.zeros((tq, LANE), F32)
            dv_t = jnp.zeros((tq, LANE), F32)
            for hh in range(2):
                sel = (lane < FOX_HD) if hh == 0 else (lane >= FOX_HD)
                qm = jnp.where(sel, qv, 0.0).astype(BF16)
                dom32 = jnp.where(sel, dov, 0.0)
                dom = dom32.astype(BF16)
                sc = _fox_logits(qm, kb, cc_ref[hh], cr_ref[hh:hh + 1, :], causal, scale, reps)
                p = jnp.exp(sc - jnp.tile(lse_ref[hh], (1, reps)))
                dp = lax.dot_general(dom, vb, (NT, ((), ())), preferred_element_type=F32)
                delta = jnp.sum(dom32 * ov, axis=-1, keepdims=True)
                ds = p * (dp - delta)
                dsb = ds.astype(BF16)
                dq_t = jnp.where(sel, jnp.dot(dsb, kb, preferred_element_type=F32) * scale, dq_t)
                dk_t = dk_t + lax.dot_general(dsb, qm, (TN, ((), ())), preferred_element_type=F32) * scale
                dv_t = dv_t + lax.dot_general(p.astype(BF16), dom, (TN, ((), ())), preferred_element_type=F32)
                dck_ref[hh] += jnp.sum(ds, axis=0, keepdims=True)
                ds_lo = (ds - dsb.astype(F32)).astype(BF16)
                dcq_ref[hh, i] += (lax.dot_general(ones, dsb, (NT, ((), ())), preferred_element_type=F32)
                                   + lax.dot_general(ones, ds_lo, (NT, ((), ())), preferred_element_type=F32))
            rows =pl.ds(pl.multiple_of(i * tq, tq), tq)
            dq_ref[rows, :] += dq_t
            dk_ref[...] += dk_t
            dv_ref[...] += dv_t

    qi = lambda hp, j, i: jnp.maximum(i, j)
    return pl.pallas_call(
        kern, name="fox_bwd", grid=(4, nq, nq),
        in_specs=[BS((tq, LANE), lambda hp, j, i: (qi(hp, j, i), Q_COL + hp)),
                  BS((tq, LANE), lambda hp, j, i: (j, K_COL + hp)),
                  BS((tq, LANE), lambda hp, j, i: (j, V_COL + hp)),
                  BS((tq, LANE), lambda hp, j, i: (qi(hp, j, i), hp)),
                  BS((tq, LANE), lambda hp, j, i: (qi(hp, j, i), hp)),
                  BS((2, tq, LANE), lambda hp, j, i: (hp, qi(hp, j, i), 0)),
                  BS((2, tq, LANE), lambda hp, j, i: (hp, qi(hp, j, i), 0)),
                  BS((None, 2, tq), lambda hp, j, i: (hp, 0, j))],
        out_specs=[BS((s, LANE), lambda hp, j, i: (0, hp)), BS((tq, LANE), lambda hp, j, i: (j, hp)),
                   BS((tq, LANE), lambda hp, j, i: (j, hp)), BS((2, 1, tq), lambda hp, j, i: (hp, 0, j)),
                   BS((2, nq, ROWSUM_M, tq), lambda hp, j, i: (hp, 0, 0, 0))],
        out_shape=[jax.ShapeDtypeStruct((s, FOX_W), F32), jax.ShapeDtypeStruct((s, FOX_W), F32),
                   jax.ShapeDtypeStruct((s, FOX_W), F32), jax.ShapeDtypeStruct((FOX_H, 1, s), F32),
                   jax.ShapeDtypeStruct((FOX_H, nq, ROWSUM_M, tq), F32)],
        compiler_params=_params(3))(proj, proj, proj, dcat, o, lse, cumc, cumr)


def _shift_down(v, d, row):
    return jnp.where(row >= d, pltpu.roll(v, d, 0), 0.0)


def _shift_up(v, d, row, n):
    return jnp.where(row < n - d, pltpu.roll(v, n - d, 0), 0.0)


def _sconv_fwd(proj, convw):
    s = proj.shape[0]

    def kern(b_ref, c_ref, u_ref, w_ref, y_ref):
        row = lax.broadcasted_iota(jnp.int32, (s, LANE), 0)
        z = c_ref[...] * u_ref[...]
        conv = w_ref[2:3, :] * z + w_ref[1:2, :] * _shift_down(z, 1, row) + w_ref[0:1, :] * _shift_down(z, 2, row)
        y_ref[...] = (b_ref[...] * conv).astype(BF16)

    col = lambda base: BS((s, LANE), lambda cb: (0, base + cb))
    return pl.pallas_call(kern, name="sconv_fwd", grid=(SC_W // LANE,),
                          in_specs=[col(B_COL), col(C_COL), col(U_COL), BS((SC_K, LANE), lambda cb: (0, cb))],
                          out_specs=BS((s, LANE), lambda cb: (0, cb)),
                          out_shape=jax.ShapeDtypeStruct((s, SC_W), BF16), compiler_params=_params(1))(proj, proj, proj, convw)


def _sconv_bwd(proj, convw, dcat):
    s = proj.shape[0]

    def kern(b_ref, c_ref, u_ref, w_ref, dy_ref, db_ref, dc_ref, du_ref, dw_ref):
        row = lax.broadcasted_iota(jnp.int32, (s, LANE), 0)
        cv, uv, dyv = c_ref[...], u_ref[...], dy_ref[...]
        z = cv * uv
        z1 = _shift_down(z, 1, row)
        z2 = _shift_down(z, 2, row)
        conv = w_ref[2:3, :] * z + w_ref[1:2, :] * z1 + w_ref[0:1, :] * z2
        db_ref[...] = dyv * conv
        dcv = dyv * b_ref[...]
        dz = w_ref[2:3, :] * dcv + w_ref[1:2, :] * _shift_up(dcv, 1, row, s) + w_ref[0:1, :] * _shift_up(dcv, 2, row, s)
        dc_ref[...] = dz * uv
        du_ref[...] = dz * cv
        dw_ref[0:1, :] = jnp.sum(dcv * z2, axis=0, keepdims=True)
        dw_ref[1:2, :] = jnp.sum(dcv * z1, axis=0, keepdims=True)
        dw_ref[2:3, :] = jnp.sum(dcv * z, axis=0, keepdims=True)

    col = lambda base: BS((s, LANE), lambda cb: (0, base + cb))
    out = BS((s, LANE), lambda cb: (0, cb))
    wspec = BS((SC_K, LANE), lambda cb: (0, cb))
    act = jax.ShapeDtypeStruct((s, SC_W), F32)
    return pl.pallas_call(kern, name="sconv_bwd", grid=(SC_W // LANE,),
                          in_specs=[col(B_COL), col(C_COL), col(U_COL), wspec, col(FOX_W // LANE)],
                          out_specs=[out, out, out, wspec],
                          out_shape=[act, act, act, jax.ShapeDtypeStruct((SC_K, SC_W), F32)],
                          compiler_params=_params(1))(proj, proj, proj, convw, dcat)


def _fox_layer_fwd(x, gpre, gpost, wall, bfb, convw, wout):
    s = x.shape[0]
    h = _norm_fwd(x, gpre)
    proj = _mm_nn("fox_proj", h, wall, tn=AB_PAD // 5)
    frow = proj[:, 3 * FOX_W + 3 * SC_W:3 * FOX_W + 3 * SC_W + FOX_H].T
    cumr = _fox_cum(frow, bfb)
    cumc = jnp.broadcast_to(cumr[:, :, None], (FOX_H, s, LANE))
    cumr4 = cumr.reshape(4, 2, s)
    o, lse = _fox_fwd_call(proj, cumc, cumr4)
    yb = _sconv_fwd(proj, convw)
    cat = jnp.concatenate([o.astype(BF16), yb], axis=1)
    y = _mm_nn("fox_out", cat, wout)
    return _norm_res(x, y, gpost), (x, h, proj, frow, cumc, cumr4, o, lse, cat, y)


def _fox_layer_bwd(dxo, saved, gpre, gpost, wall, bfb, convw, wout):
    x, h, proj, frow, cumc, cumr4, o, lse, cat, y = saved
    s = x.shape[0]
    dy, dgpost = _norm_bwd(y, dxo, gpost, None, BF16)
    dcat = _mm_nt("fox_dcat", dy, wout)
    dwout = _mm_tn("fox_dwout", cat, dy)
    db, dc, du, dconvw = _sconv_bwd(proj, convw, dcat)
    dq, dk, dv, dck, dcq = _fox_bwd_call(proj, o, lse, dcat, cumc, cumr4)
    dfrow, dbf = _fox_dlogf(dcq[:, :, 0, :].reshape(FOX_H, s), dck.reshape(FOX_H, s), frow, bfb)
    dfcol = jnp.pad(dfrow.T, ((0, 0), (0, LANE - FOX_H)))
    dproj = jnp.concatenate([dq, dk, dv, db, dc, du, dfcol], axis=1).astype(BF16)
    dwall = _mm_tn("fox_dwall", h, dproj, tn=AB_PAD // 5)
    dh = _mm_nt("fox_dh", dproj, wall, tn=AB_PAD // 5)
    dx, dgpre = _norm_bwd(x, dh, gpre, dxo, F32)
    return dx, dgpre, dgpost, dwall, dbf[:, 0], dconvw, dwout


def _ab_pack(w):
    nf = 3 * FOX_W
    return jnp.concatenate([w[:, :nf], w[:, nf + FOX_H:], w[:, nf:nf + FOX_H],
                            jnp.zeros((w.shape[0], AB_PAD - AB_IN), w.dtype)], axis=1)


def _ab_unpack(w):
    nf = 3 * FOX_W
    nbcu = 3 * SC_W
    return jnp.concatenate([w[:, :nf], w[:, nf + nbcu:nf + nbcu + FOX_H], w[:, nf:nf + nbcu]], axis=1)


NCH = DM // LANE
CH_PER_BLK = LRU_BW // LANE


def _chunk_spec(s, lead=0):
    return BS((None, s, LANE), lambda ch: (lead + ch // CH_PER_BLK, 0, ch % CH_PER_BLK))


def _vec_chunk(rows):
    return BS((rows, LANE), lambda ch: (0, ch))


def _neg_expm1(x):
    series = -x * (1.0 + x * (1 / 2) * (1.0 + x * (1 / 3) * (1.0 + x * (1 / 4) * (1.0 + x * (1 / 5) * (
        1.0 + x * (1 / 6) * (1.0 + x * (1 / 7)))))))
    return jnp.where(x > -0.25, series, 1.0 - jnp.exp(x))


def _softplus(z):
    return jnp.maximum(z, 0.0) + jnp.log1p(jnp.exp(-jnp.abs(z)))


GELU_C = math.sqrt(2.0 / math.pi)
GELU_A = 0.044715


def _gelu(x):
    return 0.5 * x * (1.0 + jnp.tanh(GELU_C * (x + GELU_A * x * x * x)))


def _gelu_grad(x):
    t = jnp.tanh(GELU_C * (x + GELU_A * x * x * x))
    return 0.5 * (1.0 + t) + 0.5 * x * (1.0 - t * t) * GELU_C * (1.0 + 3.0 * GELU_A * x * x)


def _lru_conv_fwd(gu, convw, convb):
    s = gu.shape[1]

    def kern(x_ref, w_ref, b_ref, u_ref):
        row = lax.broadcasted_iota(jnp.int32, (s, LANE), 0)
        xv = x_ref[...]
        u_ref[...] = (b_ref[...] + w_ref[3:4, :] * xv + w_ref[2:3, :] * _shift_down(xv, 1, row)
                      + w_ref[1:2, :] * _shift_down(xv, 2, row) + w_ref[0:1, :] * _shift_down(xv, 3, row))

    return pl.pallas_call(kern, name="lru_conv_fwd", grid=(NCH,),
                          in_specs=[_chunk_spec(s, LRU_NB), _vec_chunk(RG_K), _vec_chunk(1)], out_specs=_chunk_spec(s),
                          out_shape=jax.ShapeDtypeStruct((LRU_NB, s, LRU_BW), F32), compiler_params=_params(1))(gu, convw, convb)


def _lru_conv_bwd(dud, dug, gu, convw):
    s = gu.shape[1]

    def kern(d1_ref, d2_ref, x_ref, w_ref, dx_ref, dw_ref, db_ref):
        row = lax.broadcasted_iota(jnp.int32, (s, LANE), 0)
        du = d1_ref[...] + d2_ref[...]
        xv = x_ref[...]
        dx_ref[...] = (w_ref[3:4, :] * du + w_ref[2:3, :] * _shift_up(du, 1, row, s) + w_ref[1:2, :] * _shift_up(du, 2, row, s)
                       + w_ref[0:1, :] * _shift_up(du, 3, row, s)).astype(BF16)
        dw_ref[3:4, :] = jnp.sum(du * xv, axis=0, keepdims=True)
        for k in range(1, RG_K):
            dw_ref[3 - k:4 - k, :] = jnp.sum(du * _shift_down(xv, k, row), axis=0, keepdims=True)
        db_ref[...] = jnp.sum(du, axis=0, keepdims=True)

    return pl.pallas_call(kern, name="lru_conv_bwd", grid=(NCH,),
                          in_specs=[_chunk_spec(s), _chunk_spec(s), _chunk_spec(s, LRU_NB), _vec_chunk(RG_K)],
                          out_specs=[_chunk_spec(s), _vec_chunk(RG_K), _vec_chunk(1)],
                          out_shape=[jax.ShapeDtypeStruct((LRU_NB, s, LRU_BW), BF16),
                                     jax.ShapeDtypeStruct((RG_K, DM), F32), jax.ShapeDtypeStruct((1, DM), F32)],
                          compiler_params=_params(1))(dud, dug, gu, convw)


def _lru_gates(z_ref, bai_ref, lam_ref, uv):
    r = jax.nn.sigmoid(z_ref[0] + bai_ref[0:1, :])
    ig = jax.nn.sigmoid(z_ref[1] + bai_ref[1:2, :])
    sp = _softplus(-lam_ref[...])
    la = -RG_C * r * sp
    a = jnp.exp(la)
    sq = jnp.sqrt(_neg_expm1(2.0 * la))
    return r, ig, sp, a, sq


def _scan_steps(n):
    d, out = 1, []
    while d < n:
        out.append(d)
        d *= 2
    return out


def _lru_scan_fwd(z, bai, lam, u, gu):
    s = u.shape[1]
    zspec = BS((2, None, s, LANE), lambda ch: (0, ch // CH_PER_BLK, 0, ch % CH_PER_BLK))

    def kern(z_ref, bai_ref, lam_ref, u_ref, g_ref, hs_ref, y_ref):
        row = lax.broadcasted_iota(jnp.int32, (s, LANE), 0)
        uv = u_ref[...]
        _, ig, _, a, sq = _lru_gates(z_ref, bai_ref, lam_ref, uv)
        b = sq * (ig * uv)
        for d in _scan_steps(s):
            a_sh = jnp.where(row >= d, pltpu.roll(a, d, 0), 1.0)
            b = a * _shift_down(b, d, row) + b
            a = a * a_sh
        hs_ref[...] = b
        y_ref[...] = (_gelu(g_ref[...]) * b).astype(BF16)

    return pl.pallas_call(kern, name="lru_scan_fwd", grid=(NCH,),
                          in_specs=[zspec, _vec_chunk(2), _vec_chunk(1), _chunk_spec(s), _chunk_spec(s)],
                          out_specs=[_chunk_spec(s), BS((s, LANE), lambda ch: (0, ch))],
                          out_shape=[jax.ShapeDtypeStruct((LRU_NB, s, LRU_BW), F32), jax.ShapeDtypeStruct((s, DM), BF16)],
                          compiler_params=_params(1))(z, bai, lam, u, gu)


def _lru_scan_bwd(dyp, z, bai, lam, u, gu, hs):
    s = u.shape[1]
    zspec = BS((2, None, s, LANE), lambda ch: (0, ch // CH_PER_BLK, 0, ch % CH_PER_BLK))

    def kern(dy_ref, z_ref, bai_ref, lam_ref, u_ref, g_ref, hs_ref, dg_ref, dz_ref, du_ref, dbai_ref, dlam_ref):
        row = lax.broadcasted_iota(jnp.int32, (s, LANE), 0)
        uv, gv, hv, dyv = u_ref[...], g_ref[...], hs_ref[...], dy_ref[...]
        r, ig, sp, a, sq = _lru_gates(z_ref, bai_ref, lam_ref, uv)
        dg_ref[...] = (dyv * hv * _gelu_grad(gv)).astype(BF16)
        g = dyv * _gelu(gv)
        an = _shift_up(a, 1, row, s)
        for d in _scan_steps(s):
            an_sh = jnp.where(row < s - d, pltpu.roll(an, s - d, 0), 1.0)
            g = an * _shift_up(g, d, row, s) + g
            an = an * an_sh
        da = g * _shift_down(hv, 1, row)
        dsq = g * (ig * uv)
        di = g * sq * uv
        du_ref[...] = g * sq * ig
        dla = da * a - dsq * (a * a / sq)
        dzr = dla * (-RG_C * sp) * r * (1.0 - r)
        dzi = di * ig * (1.0 - ig)
        dz_ref[0] = dzr.astype(BF16)
        dz_ref[1] = dzi.astype(BF16)
        dbai_ref[0:1, :] = jnp.sum(dzr, axis=0, keepdims=True)
        dbai_ref[1:2, :] = jnp.sum(dzi, axis=0, keepdims=True)
        dlam_ref[...] = jnp.sum(dla * r, axis=0, keepdims=True) * (RG_C * jax.nn.sigmoid(-lam_ref[...]))

    return pl.pallas_call(
        kern, name="lru_scan_bwd", grid=(NCH,),
        in_specs=[BS((s, LANE), lambda ch: (0, ch)), zspec, _vec_chunk(2), _vec_chunk(1), _chunk_spec(s), _chunk_spec(s),
                  _chunk_spec(s)],
        out_specs=[_chunk_spec(s), zspec, _chunk_spec(s), _vec_chunk(2), _vec_chunk(1)],
        out_shape=[jax.ShapeDtypeStruct((LRU_NB, s, LRU_BW), BF16), jax.ShapeDtypeStruct((2, LRU_NB, s, LRU_BW), BF16),
                   jax.ShapeDtypeStruct((LRU_NB, s, LRU_BW), F32), jax.ShapeDtypeStruct((2, DM), F32),
                   jax.ShapeDtypeStruct((1, DM), F32)],
        compiler_params=_params(1))(dyp, z, bai, lam, u, gu, hs)


def _lru_layer_fwd(x, gpre, gpost, win, convw, convb, wai, bai, lam, wout):
    s = x.shape[0]
    tm = _tile(s, 512)
    h = _norm_fwd(x, gpre)
    gu = _bmm_nn("lru_in", h, win)
    u = _lru_conv_fwd(gu, convw, convb)
    z = _mm("lru_gate", u, wai, grid=(2, LRU_NB, s // tm, 1),
            a_spec=BS((None, tm, LRU_BW), lambda k, n, i, r: (n, i, 0)),
            b_spec=BS((None, None, LRU_BW, LRU_BW), lambda k, n, i, r: (k, n, 0, 0)),
            o_spec=BS((None, None, tm, LRU_BW), lambda k, n, i, r: (k, n, i, 0)),
            out_shape=(2, LRU_NB, s, LRU_BW), dn=NN)
    hs, yp = _lru_scan_fwd(z, bai, lam, u, gu)
    y = _mm_nn("lru_out", yp, wout)
    return _norm_res(x, y, gpost), (x, h, gu, u, z, hs, yp, y)


def _lru_layer_bwd(dxo, saved, gpre, gpost, win, convw, convb, wai, bai, lam, wout):
    x, h, gu, u, z, hs, yp, y = saved
    s = x.shape[0]
    tm = _tile(s, 512)
    dy, dgpost = _norm_bwd(y, dxo, gpost, None, BF16)
    dyp = _mm_nt("lru_dyp", dy, wout)
    dwout = _mm_tn("lru_dwout", yp, dy)
    dgate, dz, dud, dbai, dlam = _lru_scan_bwd(dyp, z, bai, lam, u, gu, hs)
    dwai = _mm("lru_dwai", u, dz, grid=(2, LRU_NB, s // tm),
               a_spec=BS((None, tm, LRU_BW), lambda k, n, r: (n, r, 0)),
               b_spec=BS((None, None, tm, LRU_BW), lambda k, n, r: (k, n, r, 0)),
               o_spec=BS((None, None, LRU_BW, LRU_BW), lambda k, n, r: (k, n, 0, 0)),
               out_shape=(2, LRU_NB, LRU_BW, LRU_BW), dn=TN)
    dug = _mm("lru_dug", dz, wai, grid=(LRU_NB, s // tm, 2),
              a_spec=BS((None, None, tm, LRU_BW), lambda n, i, k: (k, n, i, 0)),
              b_spec=BS((None, None, LRU_BW, LRU_BW), lambda n, i, k: (k, n, 0, 0)),
              o_spec=BS((None, tm, LRU_BW), lambda n, i, k: (n, i, 0)),
              out_shape=(LRU_NB, s, LRU_BW), dn=NT)
    duraw, dconvw, dconvb = _lru_conv_bwd(dud, dug, gu, convw)
    dgu = jnp.concatenate([dgate, duraw], axis=0)
    dwin = _bmm_tn("lru_dwin", h, dgu)
    dh = _bmm_nt_sum("lru_dh", dgu, win)
    dx, dgpre = _norm_bwd(x, dh, gpre, dxo, F32)
    return dx, dgpre, dgpost, dwin, dconvw, dconvb, dwai, dbai, dlam, dwout


CHIP_FLIPS = ((1, 0), (0, 1), (1, 1))


def _place():
    return lax.axis_index("x"), lax.axis_index("y"), lax.axis_index("c")


def _flip(v, f):
    return 1 - v if f else v


def _comm_params():
    return pltpu.CompilerParams(vmem_limit_bytes=VMEM_LIMIT)


def _all_gather(shards):
    n = len(shards)

    def body(*refs):
        ins, outs, stage = refs[:n], refs[n:2 * n], refs[2 * n:3 * n]
        send_sems, recv_sems, local_sems = refs[3 * n:]
        x, y, c = _place()
        me, sibling = (x, y, c), (x, y, 1 - c)
        chips = [(_flip(x, fx), _flip(y, fy)) for fx, fy in CHIP_FLIPS]

        def slot(t, p):
            return outs[t].at[:, 4 * p[0] + 2 * p[1] + p[2]]

        def copy(t, k, block, to, src=None):
            return pltpu.make_async_remote_copy(
                src_ref=slot(t, block) if src is None else src, dst_ref=slot(t, block),
                send_sem=send_sems.at[7 * t + k], recv_sem=recv_sems.at[7 * t + k], device_id=to, device_id_type=MESH)

        first = []
        for t in range(n):
            first.append(copy(t, 0, me, sibling, src=ins[t]))
            first += [copy(t, 1 + j, me, (*chip, c), src=ins[t]) for j, chip in enumerate(chips)]
        for cp in first:
            cp.start()
        load = [pltpu.make_async_copy(ins[t], stage[t], local_sems.at[t]) for t in range(n)]
        mine = [pltpu.make_async_copy(stage[t], slot(t, me), local_sems.at[t]) for t in range(n)]
        for cp in load:
            cp.start()
        for t in range(n):
            load[t].wait()
            mine[t].start()
        passed = []
        for j, chip in enumerate(chips):
            for t in range(n):
                copy(t, 1 + j, (*chip, c), me).wait_recv()
                fwd = copy(t, 4 + j, (*chip, c), sibling)
                fwd.start()
                passed.append(fwd)
        for t in range(n):
            copy(t, 0, sibling, me).wait_recv()
            for j, chip in enumerate(chips):
                copy(t, 4 + j, (*chip, 1 - c), me).wait_recv()
        for cp in first + passed:
            cp.wait_send()
        for cp in mine:
            cp.wait()

    outs = [jax.ShapeDtypeStruct((s.shape[0], NDEV) + s.shape[1:], s.dtype) for s in shards]
    return pl.pallas_call(body, name="all_gather", in_specs=[ANY] * n, out_specs=[ANY] * n, out_shape=outs,
                          scratch_shapes=[pltpu.VMEM(s.shape, s.dtype) for s in shards]
                          + [pltpu.SemaphoreType.DMA((7 * n,)), pltpu.SemaphoreType.DMA((7 * n,)),
                             pltpu.SemaphoreType.DMA((n,))],
                          compiler_params=_comm_params())(*shards)


def _small_gather(v):
    def body(v_ref, o_ref, send_sems, recv_sems, local_sem):
        x, y, c = _place()
        mine = 4 * x + 2 * y + c
        local = pltpu.make_async_copy(v_ref, o_ref.at[mine], local_sem)
        local.start()
        sends = []
        for k in range(1, NDEV):
            fx, fy, fc = (k >> 2) & 1, (k >> 1) & 1, k & 1
            sends.append(pltpu.make_async_remote_copy(
                src_ref=v_ref, dst_ref=o_ref.at[mine], send_sem=send_sems.at[k - 1], recv_sem=recv_sems.at[k - 1],
                device_id=(_flip(x, fx), _flip(y, fy), _flip(c, fc)), device_id_type=MESH))
        for cp in sends:
            cp.start()
        for k in range(1, NDEV):
            fx, fy, fc = (k >> 2) & 1, (k >> 1) & 1, k & 1
            src = 4 * _flip(x, fx) + 2 * _flip(y, fy) + _flip(c, fc)
            pltpu.make_async_remote_copy(src_ref=v_ref, dst_ref=o_ref.at[src], send_sem=send_sems.at[k - 1],
                                         recv_sem=recv_sems.at[k - 1], device_id=(x, y, c), device_id_type=MESH).wait_recv()
        for cp in sends:
            cp.wait_send()
        local.wait()

    return pl.pallas_call(body, name="small_gather", in_specs=[ANY], out_specs=ANY,
                          out_shape=jax.ShapeDtypeStruct((NDEV,) + v.shape, v.dtype),
                          scratch_shapes=[pltpu.SemaphoreType.DMA((NDEV - 1,)), pltpu.SemaphoreType.DMA((NDEV - 1,)),
                                          pltpu.SemaphoreType.DMA],
                          compiler_params=_comm_params())(v)


REL_CHIPS = ((0, 0),) + CHIP_FLIPS


def _rs_d2d(g5s):
    n = len(g5s)

    def body(*refs):
        ins, gots = refs[:n], refs[n:2 * n]
        send_sems, recv_sems = refs[2 * n:]
        x, y, c = _place()
        copies = []
        for t in range(n):
            for f, (fx, fy) in enumerate(REL_CHIPS):
                copies.append(pltpu.make_async_remote_copy(
                    src_ref=ins[t].at[_flip(x, fx), _flip(y, fy), 1 - c], dst_ref=gots[t].at[f],
                    send_sem=send_sems.at[4 * t + f], recv_sem=recv_sems.at[4 * t + f], device_id=(x, y, 1 - c),
                    device_id_type=MESH))
        for cp in copies:
            cp.start()
        for cp in copies:
            cp.wait()

    out = [jax.ShapeDtypeStruct((4,) + g.shape[3:], F32) for g in g5s]
    return pl.pallas_call(body, name="rs_d2d", in_specs=[ANY] * n, out_specs=[ANY] * n, out_shape=out,
                          scratch_shapes=[pltpu.SemaphoreType.DMA((4 * n,)), pltpu.SemaphoreType.DMA((4 * n,))],
                          compiler_params=_comm_params())(*g5s)


def _rs_ici(parts):
    n = len(parts)

    def body(*refs):
        ins, outs = refs[:n], refs[n:2 * n]
        send_sems, recv_sems = refs[2 * n:]
        x, y, c = _place()
        copies = []
        for t in range(n):
            for f, (fx, fy) in enumerate(CHIP_FLIPS):
                copies.append(pltpu.make_async_remote_copy(
                    src_ref=ins[t].at[f], dst_ref=outs[t].at[f], send_sem=send_sems.at[3 * t + f],
                    recv_sem=recv_sems.at[3 * t + f], device_id=(_flip(x, fx), _flip(y, fy), c), device_id_type=MESH))
        for cp in copies:
            cp.start()
        for cp in copies:
            cp.wait()

    out = [jax.ShapeDtypeStruct(p.shape, p.dtype) for p in parts]
    return pl.pallas_call(body, name="rs_ici", in_specs=[ANY] * n, out_specs=[ANY] * n, out_shape=out,
                          scratch_shapes=[pltpu.SemaphoreType.DMA((3 * n,)), pltpu.SemaphoreType.DMA((3 * n,))],
                          compiler_params=_comm_params())(*parts)


def _row_tile(rows):
    for t in (256, 128, 64, 32, 16, 8):
        if rows % t == 0:
            return t
    return rows


def _rs_chip_sum(pos, g5, got):
    a, b = g5.shape[3:]
    ta = _row_tile(a)

    def kern(pos_ref, o_ref, g_ref, p_ref):
        p_ref[...] = (o_ref[...] + g_ref[...]).astype(BF16)

    def mine(f, i, pos_ref):
        return (pos_ref[0] ^ ((f + 1) & 1), pos_ref[1] ^ ((f + 1) >> 1), pos_ref[2], i, 0)

    spec = pltpu.PrefetchScalarGridSpec(
        num_scalar_prefetch=1, grid=(3, a // ta),
        in_specs=[BS((None, None, None, ta, b), mine), BS((None, ta, b), lambda f, i, pos_ref: (f + 1, i, 0))],
        out_specs=BS((None, ta, b), lambda f, i, pos_ref: (f, i, 0)))
    return pl.pallas_call(kern, name="rs_chip_sum", grid_spec=spec, out_shape=jax.ShapeDtypeStruct((3, a, b), BF16),
                          compiler_params=_params(2))(pos, g5, got)


def _rs_final_sum(pos, g5, got, recv):
    a, b = g5.shape[3:]
    ta = _row_tile(a)

    def kern(pos_ref, o_ref, g_ref, r_ref, s_ref):
        acc = o_ref[...] + g_ref[...]
        for f in range(3):
            acc = acc + r_ref[f].astype(F32)
        s_ref[...] = acc

    spec = pltpu.PrefetchScalarGridSpec(
        num_scalar_prefetch=1, grid=(a // ta,),
        in_specs=[BS((None, None, None, ta, b), lambda i, pos_ref: (pos_ref[0], pos_ref[1], pos_ref[2], i, 0)),
                  BS((None, ta, b), lambda i, pos_ref: (0, i, 0)), BS((3, ta, b), lambda i, pos_ref: (0, i, 0))],
        out_specs=BS((ta, b), lambda i, pos_ref: (i, 0)))
    return pl.pallas_call(kern, name="rs_final_sum", grid_spec=spec, out_shape=jax.ShapeDtypeStruct((a, b), F32),
                          compiler_params=_params(1))(pos, g5, got, recv)


def _reduce_scatter(grads, pos):
    g5s = [g.reshape((2, 2, 2) + g.shape[1:]) for g in grads]
    gots = _rs_d2d(g5s)
    parts = [_rs_chip_sum(pos, g, got) for g, got in zip(g5s, gots)]
    recvs = _rs_ici(parts)
    return [_rs_final_sum(pos, g, got, r) for g, got, r in zip(g5s, gots, recvs)]


def _sum_devices(v):
    _, r, _ = v.shape

    def kern(v_ref, o_ref):
        acc = v_ref[0]
        for d in range(1, NDEV):
            acc = acc + v_ref[d]
        o_ref[...] = acc

    return pl.pallas_call(kern, name="sum_devices", out_shape=jax.ShapeDtypeStruct((r, LANE), F32),
                          compiler_params=_comm_params())(v)


def _loss_head(xf, target):
    s = xf.shape[0]
    tm = _tile(s, 512)

    def kern(x_ref, t_ref, dx_ref, l_ref):
        err = x_ref[...] - t_ref[...]
        dx_ref[...] = err * (1.0 / DM)
        part = jnp.broadcast_to(0.5 * jnp.sum(jnp.mean(err * err, axis=-1, keepdims=True), axis=0, keepdims=True), (8, LANE))

        @pl.when(pl.program_id(0) == 0)
        def _():
            l_ref[...] = part

        @pl.when(pl.program_id(0) > 0)
        def _():
            l_ref[...] += part

    row = BS((tm, DM), lambda i: (i, 0))
    return pl.pallas_call(kern, name="loss_head", grid=(s // tm,), in_specs=[row, row],
                          out_specs=[row, BS((8, LANE), lambda i: (0, 0))],
                          out_shape=[jax.ShapeDtypeStruct((s, DM), F32), jax.ShapeDtypeStruct((8, LANE), F32)],
                          compiler_params=_params(1))(xf, target)


def _adamw(w, g, m, v):
    rows, cols = w.shape
    tr = _row_tile(rows)

    def kern(w_ref, g_ref, m_ref, v_ref, d_ref, nm_ref, nv_ref):
        gv = g_ref[...]
        nm = ADAM_B1 * m_ref[...] + (1.0 - ADAM_B1) * gv
        nv = ADAM_B2 * v_ref[...] + (1.0 - ADAM_B2) * (gv * gv)
        m_hat = nm / (1.0 - ADAM_B1 ** ADAM_STEP)
        v_hat = nv / (1.0 - ADAM_B2 ** ADAM_STEP)
        d_ref[...] = -ADAM_LR * (m_hat / (jnp.sqrt(v_hat) + ADAM_EPS) + ADAM_WD * w_ref[...])
        nm_ref[...] = nm
        nv_ref[...] = nv

    blk = BS((tr, cols), lambda i: (i, 0))
    shp = jax.ShapeDtypeStruct((rows, cols), F32)
    return pl.pallas_call(kern, name="adamw", grid=(rows // tr,), in_specs=[blk] * 4, out_specs=[blk] * 3,
                          out_shape=[shp] * 3, compiler_params=_params(1))(w, g, m, v)


def _adamw_nd(w, g, m, v):
    shape = w.shape
    two = (math.prod(shape[:-1]), shape[-1])
    return tuple(o.reshape(shape) for o in _adamw(w.reshape(two), g.reshape(two), m.reshape(two), v.reshape(two)))


def _pack_small(parts):
    flat = jnp.concatenate([p.reshape(-1) for p in parts])
    pad = (-flat.shape[0]) % (8 * LANE)
    return jnp.pad(flat, (0, pad)).reshape(-1, LANE)


def _unpack_small(packed, shapes, lead=()):
    flat = packed.reshape(lead + (-1,))
    out, off = [], 0
    for shp in shapes:
        n = math.prod(shp)
        out.append(flat[..., off:off + n].reshape(lead + tuple(shp)))
        off += n
    return out


def _blocks_of_columns(w):
    k, n = w.shape
    return w.reshape(k, NDEV, n // NDEV).transpose(1, 0, 2)


def _columns_of_blocks(wb):
    n, k, c = wb.shape
    return wb.transpose(1, 0, 2).reshape(k, n * c)


WEIGHT_NAMES = ('g_mix_pre', 'g_mix_post', 'g_cross_pre', 'g_mem', 'g_cross_post', 'g_ffn_pre', 'g_ffn_post', 'w_xq',
                'w_xkv', 'w_xo', 'w_ffn_gu', 'w_ffn_down', 'ab_w_in', 'ab_b_f', 'ab_conv_w', 'ab_w_out', 'c_w_in',
                'c_conv_w', 'c_conv_b', 'c_w_a', 'c_b_a', 'c_w_i', 'c_b_i', 'c_lam', 'c_w_out')
BIG = ('w_xq', 'w_xkv', 'w_xo', 'w_ffn_gu', 'w_ffn_down', 'ab_w_in', 'ab_w_out', 'c_w_in', 'c_w_a', 'c_w_i', 'c_w_out')
SMALL_SHARDED = ('ab_conv_w', 'c_conv_w', 'c_conv_b', 'c_b_a', 'c_b_i', 'c_lam')
REPLICATED = ('g_mix_pre', 'g_mix_post', 'g_cross_pre', 'g_mem', 'g_cross_post', 'g_ffn_pre', 'g_ffn_post', 'ab_b_f')


def _small_full(name, gathered):
    nd = gathered.ndim
    return jnp.moveaxis(gathered, 0, nd - 2).reshape(gathered.shape[1:-1] + (NDEV * gathered.shape[-1],))


def _small_shard(full, dev):
    c = full.shape[-1] // NDEV
    return lax.dynamic_slice_in_dim(full, dev * c, c, axis=full.ndim - 1)


def kernel(x, mem, g_mix_pre, g_mix_post, g_cross_pre, g_mem, g_cross_post, g_ffn_pre, g_ffn_post, w_xq, w_xkv, w_xo, w_ffn_gu, w_ffn_down, ab_w_in, ab_b_f, ab_conv_w, ab_w_out, c_w_in, c_conv_w, c_conv_b, c_w_a, c_b_a, c_w_i, c_b_i, c_lam, c_w_out, loss_target, m_g_mix_pre, m_g_mix_post, m_g_cross_pre, m_g_mem, m_g_cross_post, m_g_ffn_pre, m_g_ffn_post, m_w_xq, m_w_xkv, m_w_xo, m_w_ffn_gu, m_w_ffn_down, m_ab_w_in, m_ab_b_f, m_ab_conv_w, m_ab_w_out, m_c_w_in, m_c_conv_w, m_c_conv_b, m_c_w_a, m_c_b_a, m_c_w_i, m_c_b_i, m_c_lam, m_c_w_out, v_g_mix_pre, v_g_mix_post, v_g_cross_pre, v_g_mem, v_g_cross_post, v_g_ffn_pre, v_g_ffn_post, v_w_xq, v_w_xkv, v_w_xo, v_w_ffn_gu, v_w_ffn_down, v_ab_w_in, v_ab_b_f, v_ab_conv_w, v_ab_w_out, v_c_w_in, v_c_conv_w, v_c_conv_b, v_c_w_a, v_c_b_a, v_c_w_i, v_c_b_i, v_c_lam, v_c_w_out):
    args = locals()
    w = {n: args[n] for n in WEIGHT_NAMES}
    mom = {n: args["m_" + n] for n in WEIGHT_NAMES}
    var = {n: args["v_" + n] for n in WEIGHT_NAMES}
    pos = jnp.stack([lax.axis_index("x"), lax.axis_index("y"), lax.axis_index("c")]).astype(jnp.int32)
    dev = 4 * pos[0] + 2 * pos[1] + pos[2]
    xs, mems, target = x[0], mem[0], loss_target[0]
    n_even, n_odd = (DEPTH + 1) // 2, DEPTH // 2

    def shard3(name):
        s = w[name].astype(BF16)
        return s.reshape(s.shape[0], -1, s.shape[-1]) if s.ndim == 4 else s
    full = dict(zip(BIG, _all_gather([shard3(n) for n in BIG])))
    small_shapes = [w[n].shape for n in SMALL_SHARDED]
    gathered_small = _unpack_small(_small_gather(_pack_small([w[n] for n in SMALL_SHARDED])), small_shapes, (NDEV,))
    small = {n: _small_full(n, g) for n, g in zip(SMALL_SHARDED, gathered_small)}

    wq = full['w_xq'].reshape(DEPTH, DM, DM)
    wo = full['w_xo'].reshape(DEPTH, DM, DM)
    wkv = full['w_xkv']
    wgu = full['w_ffn_gu']
    wd = full['w_ffn_down'].reshape(DEPTH, D_FF, DM)
    ab_wall = [_ab_pack(_columns_of_blocks(full['ab_w_in'][e])) for e in range(n_even)]
    ab_wout = full['ab_w_out'].reshape(n_even, DM, DM)
    ab_bfb = jnp.broadcast_to(ab_b_f[:, :, None], (n_even, FOX_H, LANE))
    c_win = full['c_w_in']
    c_wout = full['c_w_out'].reshape(n_odd, DM, DM)

    def gate_w(name):
        g = full[name].reshape(n_odd, NDEV, LRU_NB, LRU_BW // NDEV, LRU_BW)
        return g.transpose(0, 2, 1, 3, 4).reshape(n_odd, LRU_NB, LRU_BW, LRU_BW)
    c_wai = jnp.stack([gate_w('c_w_a'), gate_w('c_w_i')], axis=1)
    c_bai = jnp.stack([small['c_b_a'].reshape(n_odd, DM), small['c_b_i'].reshape(n_odd, DM)], axis=1)
    row = lambda a, l: a[l][None]

    def mixer_args(l):
        if l % 2 == 0:
            e = l // 2
            return (row(g_mix_pre, l), row(g_mix_post, l), ab_wall[e], ab_bfb[e], small['ab_conv_w'][e], ab_wout[e])
        o = l // 2
        return (row(g_mix_pre, l), row(g_mix_post, l), c_win[o], small['c_conv_w'][o], row(small['c_conv_b'], o),
                c_wai[o], c_bai[o], row(small['c_lam'], o), c_wout[o])

    def cross_args(l):
        return (row(g_cross_pre, l), row(g_mem, l), row(g_cross_post, l), wq[l], wkv[l], wo[l])

    def ffn_args(l):
        return (row(g_ffn_pre, l), row(g_ffn_post, l), wgu[l], wd[l])

    saved = []
    h = xs
    for l in range(DEPTH):
        h, s_mix = (_fox_layer_fwd if l % 2 == 0 else _lru_layer_fwd)(h, *mixer_args(l))
        h, s_cross = _cross_fwd(h, mems, *cross_args(l))
        h, s_ffn = _ffn_fwd(h, *ffn_args(l))
        saved.append((s_mix, s_cross, s_ffn))
    dx, loss_rep = _loss_head(h, target)
    loss = lax.psum(loss_rep[0, 0], ("x", "y", "c"))

    grads = {n: [None] * w[n].shape[0] for n in BIG}
    partial = {n: [None] * w[n].shape[0] for n in REPLICATED + SMALL_SHARDED}
    for l in reversed(range(DEPTH)):
        s_mix, s_cross, s_ffn = saved[l]
        dx, partial['g_ffn_pre'][l], partial['g_ffn_post'][l], dwgu, dwd = _ffn_bwd(dx, s_ffn, *ffn_args(l))
        (dx, partial['g_cross_pre'][l], partial['g_mem'][l], partial['g_cross_post'][l], dwq, dwkv, dwo) = _cross_bwd(
            dx, s_cross, mems, *cross_args(l))
        layer = {'w_xq': (l, dwq.reshape(NDEV, DM // NDEV, DM)), 'w_xkv': (l, dwkv), 'w_xo': (l, dwo.reshape(NDEV, DM // NDEV, DM)),
                 'w_ffn_gu': (l, dwgu), 'w_ffn_down': (l, dwd.reshape(NDEV, D_FF // NDEV, DM))}
        if l % 2 == 0:
            e = l // 2
            (dx, partial['g_mix_pre'][l], partial['g_mix_post'][l], dwall, partial['ab_b_f'][e], partial['ab_conv_w'][e],
             dwout) = _fox_layer_bwd(dx, s_mix, *mixer_args(l))
            layer['ab_w_in'] = (e, _blocks_of_columns(_ab_unpack(dwall)))
            layer['ab_w_out'] = (e, dwout.reshape(NDEV, DM // NDEV, DM))
        else:
            o = l // 2
            (dx, partial['g_mix_pre'][l], partial['g_mix_post'][l], dwin, partial['c_conv_w'][o], dconvb, dwai, dbai, dlam,
             dwout) = _lru_layer_bwd(dx, s_mix, *mixer_args(l))
            partial['c_conv_b'][o], partial['c_lam'][o] = dconvb[0], dlam[0]
            partial['c_b_a'][o], partial['c_b_i'][o] = dbai[0].reshape(LRU_NB, LRU_BW), dbai[1].reshape(LRU_NB, LRU_BW)
            rows = LRU_BW // NDEV
            by_dev = lambda d: d.reshape(LRU_NB, NDEV, rows, LRU_BW).transpose(1, 0, 2, 3).reshape(NDEV, LRU_NB * rows, LRU_BW)
            layer['c_w_in'] = (o, dwin)
            layer['c_w_a'] = (o, by_dev(dwai[0]))
            layer['c_w_i'] = (o, by_dev(dwai[1]))
            layer['c_w_out'] = (o, dwout.reshape(NDEV, DM // NDEV, DM))
        names = list(layer)
        for n, g in zip(names, _reduce_scatter([layer[n][1] for n in names], pos)):
            grads[n][layer[n][0]] = g

    small_names = REPLICATED + SMALL_SHARDED
    small_parts = [jnp.stack([p.reshape(w[n].shape[1:] if n in REPLICATED else small[n].shape[1:]) for p in partial[n]])
                   for n in small_names]
    reduced = _unpack_small(_sum_devices(_small_gather(_pack_small(small_parts))), [p.shape for p in small_parts])
    grad = {}
    for n, g in zip(small_names, reduced):
        grad[n] = g if n in REPLICATED else _small_shard(g, dev)
    for n in BIG:
        grad[n] = jnp.stack(grads[n]).reshape(w[n].shape)

    delta, new_m, new_v = {}, {}, {}
    for n in BIG:
        delta[n], new_m[n], new_v[n] = _adamw_nd(w[n], grad[n], mom[n], var[n])
    shapes = [w[n].shape for n in small_names]
    packed = [_pack_small([t[n] for n in small_names]) for t in (w, grad, mom, var)]
    for res, out in zip(_adamw(*packed), (delta, new_m, new_v)):
        for n, val in zip(small_names, _unpack_small(res, shapes)):
            out[n] = val

    return (loss, dx[None], *[grad[n] for n in WEIGHT_NAMES], *[delta[n] for n in WEIGHT_NAMES],
            *[new_m[n] for n in WEIGHT_NAMES], *[new_v[n] for n in WEIGHT_NAMES])
```

```python
import functools
import math

import jax
import jax.numpy as jnp
from jax import lax
from jax.experimental import pallas as pl
from jax.experimental.pallas import tpu as pltpu

F32 = jnp.float32
BF16 = jnp.bfloat16
BS = pl.BlockSpec
ANY = pl.BlockSpec(memory_space=pl.ANY)
MESH = pl.DeviceIdType.MESH

DM = 1024
DEPTH = 4
EPS = 1e-6
NEG = -1e30
FOX_W = 512
FOX_HD = 64
FOX_H = 8
SC_W = 512
SC_K = 3
AB_IN = 3 * FOX_W + FOX_H + 3 * SC_W
AB_PAD = 3200
LRU_BW = 256
LRU_NB = 4
RG_K = 4
RG_C = 8.0
MEM_H = 4
MEM_HD = 256
D_FF = 2816
NDEV = 8
FFB = 2 * D_FF // NDEV
ADAM_LR, ADAM_B1, ADAM_B2, ADAM_EPS, ADAM_WD, ADAM_STEP = 0.001, 0.9, 0.999, 1e-08, 0.01, 10

LANE = 128
VMEM_LIMIT = 48 * 1024 * 1024


def _params(ngrid):
    return pltpu.CompilerParams(dimension_semantics=("arbitrary",) * ngrid, vmem_limit_bytes=VMEM_LIMIT)


def _tile(n, t):
    return t if n % t == 0 else n


def _mm(name, a, b, *, grid, a_spec, b_spec, o_spec, out_shape, dn, out_dtype=F32):
    nred = grid[-1]
    ngrid = len(grid)

    def kern(a_ref, b_ref, o_ref, *scratch):
        p = lax.dot_general(a_ref[...].astype(BF16), b_ref[...].astype(BF16), (dn, ((), ())),
                            preferred_element_type=F32)
        if nred == 1:
            o_ref[...] = p.astype(o_ref.dtype)
            return
        acc = scratch[0] if scratch else o_ref
        r = pl.program_id(ngrid - 1)

        @pl.when(r == 0)
        def _():
            acc[...] = p

        @pl.when(r > 0)
        def _():
            acc[...] += p

        if scratch:
            @pl.when(r == nred - 1)
            def _():
                o_ref[...] = acc[...].astype(o_ref.dtype)

    blk = tuple(d for d in o_spec.block_shape if d is not None)
    scratch = [pltpu.VMEM(blk, F32)] if (nred > 1 and out_dtype != F32) else []
    return pl.pallas_call(kern, name=name, grid=grid, in_specs=[a_spec, b_spec], out_specs=o_spec,
                          out_shape=jax.ShapeDtypeStruct(out_shape, out_dtype), scratch_shapes=scratch,
                          compiler_params=_params(ngrid))(a, b)


NN = ((1,), (0,))
NT = ((1,), (1,))
TN = ((0,), (0,))


def _mm_nn(name, a, w, out_dtype=F32, tn=None):
    m, k = a.shape
    n = w.shape[1]
    tm = _tile(m, 512)
    tn = n if tn is None else tn
    return _mm(name, a, w, grid=(m // tm, n // tn, 1), a_spec=BS((tm, k), lambda i, j, r: (i, 0)),
               b_spec=BS((k, tn), lambda i, j, r: (0, j)), o_spec=BS((tm, tn), lambda i, j, r: (i, j)),
               out_shape=(m, n), dn=NN, out_dtype=out_dtype)


def _mm_nt(name, a, w, out_dtype=F32, tn=None):
    m, n = a.shape
    k = w.shape[0]
    tm = _tile(m, 512)
    tn = n if tn is None else tn
    return _mm(name, a, w, grid=(m // tm, n // tn), a_spec=BS((tm, tn), lambda i, r: (i, r)),
               b_spec=BS((k, tn), lambda i, r: (0, r)), o_spec=BS((tm, k), lambda i, r: (i, 0)),
               out_shape=(m, k), dn=NT, out_dtype=out_dtype)


def _mm_tn(name, a, b, tn=None):
    m, k = a.shape
    n = b.shape[1]
    tm = _tile(m, 512)
    tn = n if tn is None else tn
    return _mm(name, a, b, grid=(n // tn, m // tm), a_spec=BS((tm, k), lambda j, r: (r, 0)),
               b_spec=BS((tm, tn), lambda j, r: (r, j)), o_spec=BS((k, tn), lambda j, r: (0, j)),
               out_shape=(k, n), dn=TN)


def _bmm_nn(name, a, w, out_dtype=F32):
    m, k = a.shape
    g, _, n = w.shape
    tm = _tile(m, 512)
    return _mm(name, a, w, grid=(g, m // tm, 1), a_spec=BS((tm, k), lambda q, i, r: (i, 0)),
               b_spec=BS((None, k, n), lambda q, i, r: (q, 0, 0)), o_spec=BS((None, tm, n), lambda q, i, r: (q, i, 0)),
               out_shape=(g, m, n), dn=NN, out_dtype=out_dtype)


def _bmm_tn(name, a, b):
    m, k = a.shape
    g, _, n = b.shape
    tm = _tile(m, 512)
    return _mm(name, a, b, grid=(g, m // tm), a_spec=BS((tm, k), lambda q, r: (r, 0)),
               b_spec=BS((None, tm, n), lambda q, r: (q, r, 0)), o_spec=BS((None, k, n), lambda q, r: (q, 0, 0)),
               out_shape=(g, k, n), dn=TN)


def _bmm_nt_sum(name, a, w):
    g, m, n = a.shape
    k = w.shape[1]
    tm = _tile(m, 512)
    return _mm(name, a, w, grid=(m // tm, g), a_spec=BS((None, tm, n), lambda i, q: (q, i, 0)),
               b_spec=BS((None, k, n), lambda i, q: (q, 0, 0)), o_spec=BS((tm, k), lambda i, q: (i, 0)),
               out_shape=(m, k), dn=NT)


def _bmm_nn_sum(name, a, w):
    g, m, k = a.shape
    n = w.shape[2]
    tm = _tile(m, 512)
    return _mm(name, a, w, grid=(m // tm, g), a_spec=BS((None, tm, k), lambda i, q: (q, i, 0)),
               b_spec=BS((None, k, n), lambda i, q: (q, 0, 0)), o_spec=BS((tm, n), lambda i, q: (i, 0)),
               out_shape=(m, n), dn=NN)


def _bbmm_tn(name, a, b):
    g, m, k = a.shape
    n = b.shape[2]
    tm = _tile(m, 512)
    return _mm(name, a, b, grid=(g, m // tm), a_spec=BS((None, tm, k), lambda q, r: (q, r, 0)),
               b_spec=BS((None, tm, n), lambda q, r: (q, r, 0)), o_spec=BS((None, k, n), lambda q, r: (q, 0, 0)),
               out_shape=(g, k, n), dn=TN)


def _rstd(x):
    return lax.rsqrt(jnp.mean(x * x, axis=-1, keepdims=True) + EPS)


def _norm_fwd(x, g):
    rows = x.shape[0]
    tm = _tile(rows, 512)

    def kern(x_ref, g_ref, h_ref):
        xv = x_ref[...]
        h_ref[...] = ((xv * _rstd(xv)) * g_ref[...]).astype(BF16)

    return pl.pallas_call(kern, name="norm_fwd", grid=(rows // tm,),
                          in_specs=[BS((tm, DM), lambda i: (i, 0)), BS((1, DM), lambda i: (0, 0))],
                          out_specs=BS((tm, DM), lambda i: (i, 0)),
                          out_shape=jax.ShapeDtypeStruct((rows, DM), BF16), compiler_params=_params(1))(x, g)


def _norm_res(x, y, g):
    rows = x.shape[0]
    tm = _tile(rows, 512)

    def kern(x_ref, y_ref, g_ref, o_ref):
        yv = y_ref[...]
        o_ref[...] = x_ref[...] + (yv * _rstd(yv)) * g_ref[...]

    row = BS((tm, DM), lambda i: (i, 0))
    return pl.pallas_call(kern, name="norm_res", grid=(rows // tm,),
                          in_specs=[row, row, BS((1, DM), lambda i: (0, 0))], out_specs=row,
                          out_shape=jax.ShapeDtypeStruct((rows, DM), F32), compiler_params=_params(1))(x, y, g)


def _norm_bwd(z, dout, g, resid, out_dtype):
    rows = z.shape[0]
    tm = _tile(rows, 512)
    has_res = resid is not None

    def kern(*refs):
        if has_res:
            z_ref, d_ref, g_ref, r_ref, dz_ref, dg_ref = refs
        else:
            z_ref, d_ref, g_ref, dz_ref, dg_ref = refs
        zv = z_ref[...]
        dv = d_ref[...].astype(F32)
        r = _rstd(zv)
        zh = zv * r
        dzh = dv * g_ref[...]
        dz = r * (dzh - zh * jnp.mean(dzh * zh, axis=-1, keepdims=True))
        if has_res:
            dz = dz + r_ref[...]
        dz_ref[...] = dz.astype(dz_ref.dtype)
        part = jnp.sum(dv * zh, axis=0, keepdims=True)

        @pl.when(pl.program_id(0) == 0)
        def _():
            dg_ref[...] = part

        @pl.when(pl.program_id(0) > 0)
        def _():
            dg_ref[...] += part

    row = BS((tm, DM), lambda i: (i, 0))
    vec = BS((1, DM), lambda i: (0, 0))
    ins = [row, row, vec] + ([row] if has_res else [])
    args = (z, dout, g) + ((resid,) if has_res else ())
    return pl.pallas_call(kern, name="norm_bwd_res" if has_res else "norm_bwd", grid=(rows // tm,), in_specs=ins,
                          out_specs=[row, vec],
                          out_shape=[jax.ShapeDtypeStruct((rows, DM), out_dtype), jax.ShapeDtypeStruct((1, DM), F32)],
                          compiler_params=_params(1))(*args)


def _ffn_up(h, wgu4):
    s = h.shape[0]
    tm = _tile(s, 512)

    def kern(h_ref, w_ref, gu_ref, a_ref):
        hv = h_ref[...]
        gate = jnp.dot(hv, w_ref[0], preferred_element_type=F32)
        up = jnp.dot(hv, w_ref[1], preferred_element_type=F32)
        gu_ref[0] = gate
        gu_ref[1] = up
        a_ref[...] = (gate * jax.nn.sigmoid(gate) * up).astype(BF16)

    return pl.pallas_call(
        kern, name="ffn_up", grid=(4, s // tm),
        in_specs=[BS((tm, DM), lambda j, i: (i, 0)), BS((2, None, DM, FFB), lambda j, i: (0, j, 0, 0))],
        out_specs=[BS((2, None, tm, FFB), lambda j, i: (0, j, i, 0)), BS((None, tm, FFB), lambda j, i: (j, i, 0))],
        out_shape=[jax.ShapeDtypeStruct((2, 4, s, FFB), F32), jax.ShapeDtypeStruct((4, s, FFB), BF16)],
        compiler_params=_params(2))(h, wgu4)


def _ffn_da(dy, wd4, gu):
    s = dy.shape[0]
    tm = _tile(s, 512)

    def kern(dy_ref, w_ref, gu_ref, o_ref):
        da = lax.dot_general(dy_ref[...], w_ref[...], (NT, ((), ())), preferred_element_type=F32)
        gate = gu_ref[0]
        up = gu_ref[1]
        sg = jax.nn.sigmoid(gate)
        o_ref[0] = (da * up * (sg * (1.0 + gate * (1.0 - sg)))).astype(BF16)
        o_ref[1] = (da * (gate * sg)).astype(BF16)

    blk = BS((2, None, tm, FFB), lambda j, i: (0, j, i, 0))
    return pl.pallas_call(
        kern, name="ffn_da", grid=(4, s // tm),
        in_specs=[BS((tm, DM), lambda j, i: (i, 0)), BS((None, FFB, DM), lambda j, i: (j, 0, 0)), blk],
        out_specs=blk, out_shape=jax.ShapeDtypeStruct((2, 4, s, FFB), BF16), compiler_params=_params(2))(dy, wd4, gu)


def _ffn_fwd(x, gpre, gpost, wgu, wd):
    h = _norm_fwd(x, gpre)
    gu, a = _ffn_up(h, wgu.reshape(2, 4, DM, FFB))
    y = _bmm_nn_sum("ffn_down", a, wd.reshape(4, FFB, DM))
    return _norm_res(x, y, gpost), (x, h, gu, a, y)


def _ffn_bwd(dxo, saved, gpre, gpost, wgu, wd):
    x, h, gu, a, y = saved
    s = x.shape[0]
    dy, dgpost = _norm_bwd(y, dxo, gpost, None, BF16)
    dgu = _ffn_da(dy, wd.reshape(4, FFB, DM), gu).reshape(8, s, FFB)
    dwd = _bmm_tn_a3("ffn_dwd", a, dy)
    dwgu = _bmm_tn("ffn_dwgu", h, dgu)
    dh = _bmm_nt_sum("ffn_dh", dgu, wgu)
    dx, dgpre = _norm_bwd(x, dh, gpre, dxo, F32)
    return dx, dgpre, dgpost, dwgu, dwd.reshape(D_FF, DM)


def _bmm_tn_a3(name, a, b):
    g, m, k = a.shape
    n = b.shape[1]
    tm = _tile(m, 512)
    return _mm(name, a, b, grid=(g, m // tm), a_spec=BS((None, tm, k), lambda q, r: (q, r, 0)),
               b_spec=BS((tm, n), lambda q, r: (r, 0)), o_spec=BS((None, k, n), lambda q, r: (q, 0, 0)),
               out_shape=(g, k, n), dn=TN)


def _softmax_rows(s):
    m = jnp.max(s, axis=-1, keepdims=True)
    p = jnp.exp(s - m)
    return p / jnp.sum(p, axis=-1, keepdims=True)


def _xattn_fwd_call(h, wq, kv):
    s = h.shape[0]
    mlen = kv.shape[1]
    tm = _tile(s, 512)
    scale = MEM_HD ** -0.5

    def kern(h_ref, w_ref, k_ref, v_ref, q_ref, o_ref):
        q = jnp.dot(h_ref[...], w_ref[...], preferred_element_type=F32).astype(BF16)
        q_ref[...] = q
        sc = lax.dot_general(q, k_ref[...], (NT, ((), ())), preferred_element_type=F32) * scale
        p = _softmax_rows(sc)
        o_ref[...] = jnp.dot(p.astype(BF16), v_ref[...], preferred_element_type=F32).astype(BF16)

    blk = BS((tm, MEM_HD), lambda i, hd: (i, hd))
    return pl.pallas_call(
        kern, name="xattn_fwd", grid=(s // tm, MEM_H),
        in_specs=[BS((tm, DM), lambda i, hd: (i, 0)), BS((DM, MEM_HD), lambda i, hd: (0, hd)),
                  BS((None, mlen, MEM_HD), lambda i, hd: (hd, 0, 0)),
                  BS((None, mlen, MEM_HD), lambda i, hd: (MEM_H + hd, 0, 0))],
        out_specs=[blk, blk],
        out_shape=[jax.ShapeDtypeStruct((s, DM), BF16), jax.ShapeDtypeStruct((s, DM), BF16)],
        compiler_params=_params(2))(h, wq, kv, kv)


def _xattn_bwd_call(q, kv, do):
    s = q.shape[0]
    mlen = kv.shape[1]
    tm = _tile(s, 512)
    scale = MEM_HD ** -0.5

    def kern(q_ref, k_ref, v_ref, do_ref, dq_ref, dkv_ref):
        qv, kvv, vv, dov = q_ref[...], k_ref[...], v_ref[...], do_ref[...]
        sc = lax.dot_general(qv, kvv, (NT, ((), ())), preferred_element_type=F32) * scale
        p = _softmax_rows(sc)
        dp = lax.dot_general(dov, vv, (NT, ((), ())), preferred_element_type=F32)
        ds = (p * (dp - jnp.sum(dp * p, axis=-1, keepdims=True)) * scale).astype(BF16)
        dq_ref[...] = jnp.dot(ds, kvv, preferred_element_type=F32).astype(BF16)
        dk = lax.dot_general(ds, qv, (TN, ((), ())), preferred_element_type=F32)
        dv = lax.dot_general(p.astype(BF16), dov, (TN, ((), ())), preferred_element_type=F32)

        @pl.when(pl.program_id(1) == 0)
        def _():
            dkv_ref[0] = dk
            dkv_ref[1] = dv

        @pl.when(pl.program_id(1) > 0)
        def _():
            dkv_ref[0] += dk
            dkv_ref[1] += dv

    blk = BS((tm, MEM_HD), lambda hd, i: (i, hd))
    return pl.pallas_call(
        kern, name="xattn_bwd", grid=(MEM_H, s // tm),
        in_specs=[blk, BS((None, mlen, MEM_HD), lambda hd, i: (hd, 0, 0)),
                  BS((None, mlen, MEM_HD), lambda hd, i: (MEM_H + hd, 0, 0)), blk],
        out_specs=[blk, BS((2, None, mlen, MEM_HD), lambda hd, i: (0, hd, 0, 0))],
        out_shape=[jax.ShapeDtypeStruct((s, DM), BF16), jax.ShapeDtypeStruct((2, MEM_H, mlen, MEM_HD), F32)],
        compiler_params=_params(2))(q, kv, kv, do)


def _cross_fwd(x, mem, gpre, gmem, gpost, wq, wkv, wo):
    h = _norm_fwd(x, gpre)
    mn = _norm_fwd(mem, gmem)
    kv = _bmm_nn("xattn_kv", mn, wkv, BF16)
    q, o = _xattn_fwd_call(h, wq, kv)
    y = _mm_nn("xattn_out", o, wo)
    return _norm_res(x, y, gpost), (x, h, mn, kv, q, o, y)


def _cross_bwd(dxo, saved, mem, gpre, gmem, gpost, wq, wkv, wo):
    x, h, mn, kv, q, o, y = saved
    mlen = mem.shape[0]
    dy, dgpost = _norm_bwd(y, dxo, gpost, None, BF16)
    do = _mm_nt("xattn_do", dy, wo, BF16)
    dwo = _mm_tn("xattn_dwo", o, dy)
    dq, dkv = _xattn_bwd_call(q, kv, do)
    dwq = _mm_tn("xattn_dwq", h, dq)
    dh = _mm_nt("xattn_dh", dq, wq)
    dkv8 = dkv.reshape(8, mlen, MEM_HD)
    dwkv = _bmm_tn("xattn_dwkv", mn, dkv8)
    dmn = _bmm_nt_sum("xattn_dmn", dkv8, wkv)
    _, dgmem = _norm_bwd(mem, dmn, gmem, None, BF16)
    dx, dgpre = _norm_bwd(x, dh, gpre, dxo, F32)
    return dx, dgpre, dgmem, dgpost, dwq, dwkv, dwo


def _log_sigmoid(z):
    return jnp.minimum(z, 0.0) - jnp.log1p(jnp.exp(-jnp.abs(z)))


def _lane_scan_steps():
    return (1, 2, 4, 8, 16, 32, 64)


def _fox_cum(frow, bfb):
    s = frow.shape[1]

    def kern(f_ref, b_ref, o_ref):
        lane = lax.broadcasted_iota(jnp.int32, (FOX_H, LANE), 1)
        carry = jnp.zeros((FOX_H, 1), F32)
        for c in range(s // LANE):
            sl = slice(c * LANE, (c + 1) * LANE)
            lf = _log_sigmoid(f_ref[:, sl] + b_ref[...])
            v = lf
            for d in _lane_scan_steps():
                v = v + jnp.where(lane >= d, pltpu.roll(v, d, 1), 0.0)
            o_ref[:, sl] = v + carry
            carry = carry + jnp.sum(lf, axis=1, keepdims=True)

    return pl.pallas_call(kern, name="fox_cum", out_shape=jax.ShapeDtypeStruct((FOX_H, s), F32),
                          compiler_params=pltpu.CompilerParams(vmem_limit_bytes=VMEM_LIMIT))(frow, bfb)


def _fox_dlogf(dcq, dck, frow, bfb):
    s = frow.shape[1]

    def kern(q_ref, d_ref, f_ref, b_ref, df_ref, db_ref):
        lane = lax.broadcasted_iota(jnp.int32, (FOX_H, LANE), 1)
        carry = jnp.zeros((FOX_H, 1), F32)
        dbf = jnp.zeros((FOX_H, 1), F32)
        for c in reversed(range(s // LANE)):
            sl = slice(c * LANE, (c + 1) * LANE)
            dc = q_ref[:, sl] - d_ref[:, sl]
            v = dc
            for d in _lane_scan_steps():
                v = v + jnp.where(lane < LANE - d, pltpu.roll(v, LANE - d, 1), 0.0)
            v = v + carry
            carry = carry + jnp.sum(dc, axis=1, keepdims=True)
            df = v * jax.nn.sigmoid(-(f_ref[:, sl] + b_ref[...]))
            df_ref[:, sl] = df
            dbf = dbf + jnp.sum(df, axis=1, keepdims=True)
        db_ref[...] = jnp.broadcast_to(dbf, (FOX_H, LANE))

    return pl.pallas_call(kern, name="fox_dlogf",
                          out_shape=[jax.ShapeDtypeStruct((FOX_H, s), F32), jax.ShapeDtypeStruct((FOX_H, LANE), F32)],
                          compiler_params=pltpu.CompilerParams(vmem_limit_bytes=VMEM_LIMIT))(dcq, dck, frow, bfb)


FOX_TQ = 512
Q_COL, K_COL, V_COL = 0, FOX_W // LANE, 2 * FOX_W // LANE
B_COL, C_COL, U_COL = 12, 16, 20


def _fox_logits(qm, kb, cc, cr, causal, scale, reps):
    sc = lax.dot_general(qm, kb, (NT, ((), ())), preferred_element_type=F32) * scale
    sc = sc + jnp.tile(cc, (1, reps)) - cr
    return jnp.where(causal, sc, NEG)


def _fox_fwd_call(proj, cumc, cumr):
    s = proj.shape[0]
    tq = _tile(s, FOX_TQ)
    nq = s // tq
    reps = tq // LANE
    scale = FOX_HD ** -0.5

    def kern(q_ref, k_ref, v_ref, cc_ref, cr_ref, o_ref, lse_ref, m_s, l_s, acc_s):
        i = pl.program_id(1)
        j = pl.program_id(2)
        lane = lax.broadcasted_iota(jnp.int32, (tq, LANE), 1)

        @pl.when(j == 0)
        def _():
            m_s[...] = jnp.full(m_s.shape, NEG, F32)
            l_s[...] = jnp.zeros(l_s.shape, F32)
            acc_s[...] = jnp.zeros(acc_s.shape, F32)

        @pl.when(j <= i)
        def _():
            qv = q_ref[...]
            kb = k_ref[...].astype(BF16)
            vb = v_ref[...].astype(BF16)
            causal = (i * tq + lax.broadcasted_iota(jnp.int32, (tq, tq), 0)
                      >= j * tq + lax.broadcasted_iota(jnp.int32, (tq, tq), 1))
            for hh in range(2):
                sel = (lane < FOX_HD) if hh == 0 else (lane >= FOX_HD)
                qm = jnp.where(sel, qv, 0.0).astype(BF16)
                sc = _fox_logits(qm, kb, cc_ref[hh], cr_ref[hh:hh + 1, :], causal, scale, reps)
                m_prev = m_s[hh]
                m_new = jnp.maximum(m_prev, jnp.max(sc, axis=-1, keepdims=True))
                alpha = jnp.exp(m_prev - m_new)
                p = jnp.exp(sc - m_new)
                l_s[hh] = alpha * l_s[hh] + jnp.sum(p, axis=-1, keepdims=True)
                acc_s[hh] = alpha * acc_s[hh] + jnp.dot(p.astype(BF16), vb, preferred_element_type=F32)
                m_s[hh] = m_new

        @pl.when(j == i)
        def _():
            o_ref[...] = jnp.where(lane < FOX_HD, acc_s[0] / l_s[0], acc_s[1] / l_s[1])
            for hh in range(2):
                lse_ref[hh] = jnp.broadcast_to(m_s[hh] + jnp.log(l_s[hh]), (tq, LANE))

    kvi = lambda hp, i, j: jnp.minimum(j, i)
    return pl.pallas_call(
        kern, name="fox_fwd", grid=(4, nq, nq),
        in_specs=[BS((tq, LANE), lambda hp, i, j: (i, Q_COL + hp)),
                  BS((tq, LANE), lambda hp, i, j: (kvi(hp, i, j), K_COL + hp)),
                  BS((tq, LANE), lambda hp, i, j: (kvi(hp, i, j), V_COL + hp)),
                  BS((2, tq, LANE), lambda hp, i, j: (hp, i, 0)),
                  BS((None, 2, tq), lambda hp, i, j: (hp, 0, kvi(hp, i, j)))],
        out_specs=[BS((tq, LANE), lambda hp, i, j: (i, hp)), BS((2, tq, LANE), lambda hp, i, j: (hp, i, 0))],
        out_shape=[jax.ShapeDtypeStruct((s, FOX_W), F32), jax.ShapeDtypeStruct((FOX_H, s, LANE), F32)],
        scratch_shapes=[pltpu.VMEM((2, tq, 1), F32), pltpu.VMEM((2, tq, 1), F32), pltpu.VMEM((2, tq, LANE), F32)],
        compiler_params=_params(3))(proj, proj, proj, cumc, cumr)


ROWSUM_M = 16


def _fox_bwd_call(proj, o, lse, dcat, cumc, cumr):
    s = proj.shape[0]
    tq = _tile(s, FOX_TQ)
    nq = s // tq
    reps = tq // LANE
    scale = FOX_HD ** -0.5

    def kern(q_ref, k_ref, v_ref, do_ref, o_ref, lse_ref, cc_ref, cr_ref, dq_ref, dk_ref, dv_ref, dck_ref, dcq_ref):
        j = pl.program_id(1)
        i = pl.program_id(2)
        lane = lax.broadcasted_iota(jnp.int32, (tq, LANE), 1)
        ones = jnp.ones((ROWSUM_M, tq), BF16)

        @pl.when((j == 0) & (i == 0))
        def _():
            dq_ref[...] = jnp.zeros(dq_ref.shape, F32)
            dcq_ref[...] = jnp.zeros(dcq_ref.shape, F32)

        @pl.when(i == j)
        def _():
            dk_ref[...] = jnp.zeros(dk_ref.shape, F32)
            dv_ref[...] = jnp.zeros(dv_ref.shape, F32)
            dck_ref[...] = jnp.zeros(dck_ref.shape, F32)

        @pl.when(i >= j)
        def _():
            qv = q_ref[...]
            dov = do_ref[...]
            ov = o_ref[...]
            kb = k_ref[...].astype(BF16)
            vb = v_ref[...].astype(BF16)
            causal = (i * tq + lax.broadcasted_iota(jnp.int32, (tq, tq), 0)
                      >= j * tq + lax.broadcasted_iota(jnp.int32, (tq, tq), 1))
            dq_t = jnp.zeros((tq, LANE), F32)
            dk_t = jnp.zeros((tq, LANE), F32)
            dv_t = jnp.zeros((tq, LANE), F32)
            for hh in range(2):
                sel = (lane < FOX_HD) if hh == 0 else (lane >= FOX_HD)
                qm = jnp.where(sel, qv, 0.0).astype(BF16)
                dom32 = jnp.where(sel, dov, 0.0)
                dom = dom32.astype(BF16)
                sc = _fox_logits(qm, kb, cc_ref[hh], cr_ref[hh:hh + 1, :], causal, scale, reps)
                p = jnp.exp(sc - jnp.tile(lse_ref[hh], (1, reps)))
                dp = lax.dot_general(dom, vb, (NT, ((), ())), preferred_element_type=F32)
                delta = jnp.sum(dom32 * ov, axis=-1, keepdims=True)
                ds = p * (dp - delta)
                dsb = ds.astype(BF16)
                dq_t = jnp.where(sel, jnp.dot(dsb, kb, preferred_element_type=F32) * scale, dq_t)
                dk_t = dk_t + lax.dot_general(dsb, qm, (TN, ((), ())), preferred_element_type=F32) * scale
                dv_t = dv_t + lax.dot_general(p.astype(BF16), dom, (TN, ((), ())), preferred_element_type=F32)
                dck_ref[hh] += jnp.sum(ds, axis=0, keepdims=True)
                ds_lo = (ds - dsb.astype(F32)).astype(BF16)
                dcq_ref[hh, i] += (lax.dot_general(ones, dsb, (NT, ((), ())), preferred_element_type=F32)
                                   + lax.dot_general(ones, ds_lo, (NT, ((), ())), preferred_element_type=F32))
            rows =pl.ds(pl.multiple_of(i * tq, tq), tq)
            dq_ref[rows, :] += dq_t
            dk_ref[...] += dk_t
            dv_ref[...] += dv_t

    qi = lambda hp, j, i: jnp.maximum(i, j)
    return pl.pallas_call(
        kern, name="fox_bwd", grid=(4, nq, nq),
        in_specs=[BS((tq, LANE), lambda hp, j, i: (qi(hp, j, i), Q_COL + hp)),
                  BS((tq, LANE), lambda hp, j, i: (j, K_COL + hp)),
                  BS((tq, LANE), lambda hp, j, i: (j, V_COL + hp)),
                  BS((tq, LANE), lambda hp, j, i: (qi(hp, j, i), hp)),
                  BS((tq, LANE), lambda hp, j, i: (qi(hp, j, i), hp)),
                  BS((2, tq, LANE), lambda hp, j, i: (hp, qi(hp, j, i), 0)),
                  BS((2, tq, LANE), lambda hp, j, i: (hp, qi(hp, j, i), 0)),
                  BS((None, 2, tq), lambda hp, j, i: (hp, 0, j))],
        out_specs=[BS((s, LANE), lambda hp, j, i: (0, hp)), BS((tq, LANE), lambda hp, j, i: (j, hp)),
                   BS((tq, LANE), lambda hp, j, i: (j, hp)), BS((2, 1, tq), lambda hp, j, i: (hp, 0, j)),
                   BS((2, nq, ROWSUM_M, tq), lambda hp, j, i: (hp, 0, 0, 0))],
        out_shape=[jax.ShapeDtypeStruct((s, FOX_W), F32), jax.ShapeDtypeStruct((s, FOX_W), F32),
                   jax.ShapeDtypeStruct((s, FOX_W), F32), jax.ShapeDtypeStruct((FOX_H, 1, s), F32),
                   jax.ShapeDtypeStruct((FOX_H, nq, ROWSUM_M, tq), F32)],
        compiler_params=_params(3))(proj, proj, proj, dcat, o, lse, cumc, cumr)


def _shift_down(v, d, row):
    return jnp.where(row >= d, pltpu.roll(v, d, 0), 0.0)


def _shift_up(v, d, row, n):
    return jnp.where(row < n - d, pltpu.roll(v, n - d, 0), 0.0)


def _sconv_fwd(proj, convw):
    s = proj.shape[0]

    def kern(b_ref, c_ref, u_ref, w_ref, y_ref):
        row = lax.broadcasted_iota(jnp.int32, (s, LANE), 0)
        z = c_ref[...] * u_ref[...]
        conv = w_ref[2:3, :] * z + w_ref[1:2, :] * _shift_down(z, 1, row) + w_ref[0:1, :] * _shift_down(z, 2, row)
        y_ref[...] = (b_ref[...] * conv).astype(BF16)

    col = lambda base: BS((s, LANE), lambda cb: (0, base + cb))
    return pl.pallas_call(kern, name="sconv_fwd", grid=(SC_W // LANE,),
                          in_specs=[col(B_COL), col(C_COL), col(U_COL), BS((SC_K, LANE), lambda cb: (0, cb))],
                          out_specs=BS((s, LANE), lambda cb: (0, cb)),
                          out_shape=jax.ShapeDtypeStruct((s, SC_W), BF16), compiler_params=_params(1))(proj, proj, proj, convw)


def _sconv_bwd(proj, convw, dcat):
    s = proj.shape[0]

    def kern(b_ref, c_ref, u_ref, w_ref, dy_ref, db_ref, dc_ref, du_ref, dw_ref):
        row = lax.broadcasted_iota(jnp.int32, (s, LANE), 0)
        cv, uv, dyv = c_ref[...], u_ref[...], dy_ref[...]
        z = cv * uv
        z1 = _shift_down(z, 1, row)
        z2 = _shift_down(z, 2, row)
        conv = w_ref[2:3, :] * z + w_ref[1:2, :] * z1 + w_ref[0:1, :] * z2
        db_ref[...] = dyv * conv
        dcv = dyv * b_ref[...]
        dz = w_ref[2:3, :] * dcv + w_ref[1:2, :] * _shift_up(dcv, 1, row, s) + w_ref[0:1, :] * _shift_up(dcv, 2, row, s)
        dc_ref[...] = dz * uv
        du_ref[...] = dz * cv
        dw_ref[0:1, :] = jnp.sum(dcv * z2, axis=0, keepdims=True)
        dw_ref[1:2, :] = jnp.sum(dcv * z1, axis=0, keepdims=True)
        dw_ref[2:3, :] = jnp.sum(dcv * z, axis=0, keepdims=True)

    col = lambda base: BS((s, LANE), lambda cb: (0, base + cb))
    out = BS((s, LANE), lambda cb: (0, cb))
    wspec = BS((SC_K, LANE), lambda cb: (0, cb))
    act = jax.ShapeDtypeStruct((s, SC_W), F32)
    return pl.pallas_call(kern, name="sconv_bwd", grid=(SC_W // LANE,),
                          in_specs=[col(B_COL), col(C_COL), col(U_COL), wspec, col(FOX_W // LANE)],
                          out_specs=[out, out, out, wspec],
                          out_shape=[act, act, act, jax.ShapeDtypeStruct((SC_K, SC_W), F32)],
                          compiler_params=_params(1))(proj, proj, proj, convw, dcat)


def _fox_layer_fwd(x, gpre, gpost, wall, bfb, convw, wout):
    s = x.shape[0]
    h = _norm_fwd(x, gpre)
    proj = _mm_nn("fox_proj", h, wall, tn=AB_PAD // 5)
    frow = proj[:, 3 * FOX_W + 3 * SC_W:3 * FOX_W + 3 * SC_W + FOX_H].T
    cumr = _fox_cum(frow, bfb)
    cumc = jnp.broadcast_to(cumr[:, :, None], (FOX_H, s, LANE))
    cumr4 = cumr.reshape(4, 2, s)
    o, lse = _fox_fwd_call(proj, cumc, cumr4)
    yb = _sconv_fwd(proj, convw)
    cat = jnp.concatenate([o.astype(BF16), yb], axis=1)
    y = _mm_nn("fox_out", cat, wout)
    return _norm_res(x, y, gpost), (x, h, proj, frow, cumc, cumr4, o, lse, cat, y)


def _fox_layer_bwd(dxo, saved, gpre, gpost, wall, bfb, convw, wout):
    x, h, proj, frow, cumc, cumr4, o, lse, cat, y = saved
    s = x.shape[0]
    dy, dgpost = _norm_bwd(y, dxo, gpost, None, BF16)
    dcat = _mm_nt("fox_dcat", dy, wout)
    dwout = _mm_tn("fox_dwout", cat, dy)
    db, dc, du, dconvw = _sconv_bwd(proj, convw, dcat)
    dq, dk, dv, dck, dcq = _fox_bwd_call(proj, o, lse, dcat, cumc, cumr4)
    dfrow, dbf = _fox_dlogf(dcq[:, :, 0, :].reshape(FOX_H, s), dck.reshape(FOX_H, s), frow, bfb)
    dfcol = jnp.pad(dfrow.T, ((0, 0), (0, LANE - FOX_H)))
    dproj = jnp.concatenate([dq, dk, dv, db, dc, du, dfcol], axis=1).astype(BF16)
    dwall = _mm_tn("fox_dwall", h, dproj, tn=AB_PAD // 5)
    dh = _mm_nt("fox_dh", dproj, wall, tn=AB_PAD // 5)
    dx, dgpre = _norm_bwd(x, dh, gpre, dxo, F32)
    return dx, dgpre, dgpost, dwall, dbf[:, 0], dconvw, dwout


def _ab_pack(w):
    nf = 3 * FOX_W
    return jnp.concatenate([w[:, :nf], w[:, nf + FOX_H:], w[:, nf:nf + FOX_H],
                            jnp.zeros((w.shape[0], AB_PAD - AB_IN), w.dtype)], axis=1)


def _ab_unpack(w):
    nf = 3 * FOX_W
    nbcu = 3 * SC_W
    return jnp.concatenate([w[:, :nf], w[:, nf + nbcu:nf + nbcu + FOX_H], w[:, nf:nf + nbcu]], axis=1)


NCH = DM // LANE
CH_PER_BLK = LRU_BW // LANE


def _chunk_spec(s, lead=0):
    return BS((None, s, LANE), lambda ch: (lead + ch // CH_PER_BLK, 0, ch % CH_PER_BLK))


def _vec_chunk(rows):
    return BS((rows, LANE), lambda ch: (0, ch))


def _neg_expm1(x):
    series = -x * (1.0 + x * (1 / 2) * (1.0 + x * (1 / 3) * (1.0 + x * (1 / 4) * (1.0 + x * (1 / 5) * (
        1.0 + x * (1 / 6) * (1.0 + x * (1 / 7)))))))
    return jnp.where(x > -0.25, series, 1.0 - jnp.exp(x))


def _softplus(z):
    return jnp.maximum(z, 0.0) + jnp.log1p(jnp.exp(-jnp.abs(z)))


GELU_C = math.sqrt(2.0 / math.pi)
GELU_A = 0.044715


def _gelu(x):
    return 0.5 * x * (1.0 + jnp.tanh(GELU_C * (x + GELU_A * x * x * x)))


def _gelu_grad(x):
    t = jnp.tanh(GELU_C * (x + GELU_A * x * x * x))
    return 0.5 * (1.0 + t) + 0.5 * x * (1.0 - t * t) * GELU_C * (1.0 + 3.0 * GELU_A * x * x)


def _lru_conv_fwd(gu, convw, convb):
    s = gu.shape[1]

    def kern(x_ref, w_ref, b_ref, u_ref):
        row = lax.broadcasted_iota(jnp.int32, (s, LANE), 0)
        xv = x_ref[...]
        u_ref[...] = (b_ref[...] + w_ref[3:4, :] * xv + w_ref[2:3, :] * _shift_down(xv, 1, row)
                      + w_ref[1:2, :] * _shift_down(xv, 2, row) + w_ref[0:1, :] * _shift_down(xv, 3, row))

    return pl.pallas_call(kern, name="lru_conv_fwd", grid=(NCH,),
                          in_specs=[_chunk_spec(s, LRU_NB), _vec_chunk(RG_K), _vec_chunk(1)], out_specs=_chunk_spec(s),
                          out_shape=jax.ShapeDtypeStruct((LRU_NB, s, LRU_BW), F32), compiler_params=_params(1))(gu, convw, convb)


def _lru_conv_bwd(dud, dug, gu, convw):
    s = gu.shape[1]

    def kern(d1_ref, d2_ref, x_ref, w_ref, dx_ref, dw_ref, db_ref):
        row = lax.broadcasted_iota(jnp.int32, (s, LANE), 0)
        du = d1_ref[...] + d2_ref[...]
        xv = x_ref[...]
        dx_ref[...] = (w_ref[3:4, :] * du + w_ref[2:3, :] * _shift_up(du, 1, row, s) + w_ref[1:2, :] * _shift_up(du, 2, row, s)
                       + w_ref[0:1, :] * _shift_up(du, 3, row, s)).astype(BF16)
        dw_ref[3:4, :] = jnp.sum(du * xv, axis=0, keepdims=True)
        for k in range(1, RG_K):
            dw_ref[3 - k:4 - k, :] = jnp.sum(du * _shift_down(xv, k, row), axis=0, keepdims=True)
        db_ref[...] = jnp.sum(du, axis=0, keepdims=True)

    return pl.pallas_call(kern, name="lru_conv_bwd", grid=(NCH,),
                          in_specs=[_chunk_spec(s), _chunk_spec(s), _chunk_spec(s, LRU_NB), _vec_chunk(RG_K)],
                          out_specs=[_chunk_spec(s), _vec_chunk(RG_K), _vec_chunk(1)],
                          out_shape=[jax.ShapeDtypeStruct((LRU_NB, s, LRU_BW), BF16),
                                     jax.ShapeDtypeStruct((RG_K, DM), F32), jax.ShapeDtypeStruct((1, DM), F32)],
                          compiler_params=_params(1))(dud, dug, gu, convw)


def _lru_gates(z_ref, bai_ref, lam_ref, uv):
    r = jax.nn.sigmoid(z_ref[0] + bai_ref[0:1, :])
    ig = jax.nn.sigmoid(z_ref[1] + bai_ref[1:2, :])
    sp = _softplus(-lam_ref[...])
    la = -RG_C * r * sp
    a = jnp.exp(la)
    sq = jnp.sqrt(_neg_expm1(2.0 * la))
    return r, ig, sp, a, sq


def _scan_steps(n):
    d, out = 1, []
    while d < n:
        out.append(d)
        d *= 2
    return out


def _lru_scan_fwd(z, bai, lam, u, gu):
    s = u.shape[1]
    zspec = BS((2, None, s, LANE), lambda ch: (0, ch // CH_PER_BLK, 0, ch % CH_PER_BLK))

    def kern(z_ref, bai_ref, lam_ref, u_ref, g_ref, hs_ref, y_ref):
        row = lax.broadcasted_iota(jnp.int32, (s, LANE), 0)
        uv = u_ref[...]
        _, ig, _, a, sq = _lru_gates(z_ref, bai_ref, lam_ref, uv)
        b = sq * (ig * uv)
        for d in _scan_steps(s):
            a_sh = jnp.where(row >= d, pltpu.roll(a, d, 0), 1.0)
            b = a * _shift_down(b, d, row) + b
            a = a * a_sh
        hs_ref[...] = b
        y_ref[...] = (_gelu(g_ref[...]) * b).astype(BF16)

    return pl.pallas_call(kern, name="lru_scan_fwd", grid=(NCH,),
                          in_specs=[zspec, _vec_chunk(2), _vec_chunk(1), _chunk_spec(s), _chunk_spec(s)],
                          out_specs=[_chunk_spec(s), BS((s, LANE), lambda ch: (0, ch))],
                          out_shape=[jax.ShapeDtypeStruct((LRU_NB, s, LRU_BW), F32), jax.ShapeDtypeStruct((s, DM), BF16)],
                          compiler_params=_params(1))(z, bai, lam, u, gu)


def _lru_scan_bwd(dyp, z, bai, lam, u, gu, hs):
    s = u.shape[1]
    zspec = BS((2, None, s, LANE), lambda ch: (0, ch // CH_PER_BLK, 0, ch % CH_PER_BLK))

    def kern(dy_ref, z_ref, bai_ref, lam_ref, u_ref, g_ref, hs_ref, dg_ref, dz_ref, du_ref, dbai_ref, dlam_ref):
        row = lax.broadcasted_iota(jnp.int32, (s, LANE), 0)
        uv, gv, hv, dyv = u_ref[...], g_ref[...], hs_ref[...], dy_ref[...]
        r, ig, sp, a, sq = _lru_gates(z_ref, bai_ref, lam_ref, uv)
        dg_ref[...] = (dyv * hv * _gelu_grad(gv)).astype(BF16)
        g = dyv * _gelu(gv)
        an = _shift_up(a, 1, row, s)
        for d in _scan_steps(s):
            an_sh = jnp.where(row < s - d, pltpu.roll(an, s - d, 0), 1.0)
            g = an * _shift_up(g, d, row, s) + g
            an = an * an_sh
        da = g * _shift_down(hv, 1, row)
        dsq = g * (ig * uv)
        di = g * sq * uv
        du_ref[...] = g * sq * ig
        dla = da * a - dsq * (a * a / sq)
        dzr = dla * (-RG_C * sp) * r * (1.0 - r)
        dzi = di * ig * (1.0 - ig)
        dz_ref[0] = dzr.astype(BF16)
        dz_ref[1] = dzi.astype(BF16)
        dbai_ref[0:1, :] = jnp.sum(dzr, axis=0, keepdims=True)
        dbai_ref[1:2, :] = jnp.sum(dzi, axis=0, keepdims=True)
        dlam_ref[...] = jnp.sum(dla * r, axis=0, keepdims=True) * (RG_C * jax.nn.sigmoid(-lam_ref[...]))

    return pl.pallas_call(
        kern, name="lru_scan_bwd", grid=(NCH,),
        in_specs=[BS((s, LANE), lambda ch: (0, ch)), zspec, _vec_chunk(2), _vec_chunk(1), _chunk_spec(s), _chunk_spec(s),
                  _chunk_spec(s)],
        out_specs=[_chunk_spec(s), zspec, _chunk_spec(s), _vec_chunk(2), _vec_chunk(1)],
        out_shape=[jax.ShapeDtypeStruct((LRU_NB, s, LRU_BW), BF16), jax.ShapeDtypeStruct((2, LRU_NB, s, LRU_BW), BF16),
                   jax.ShapeDtypeStruct((LRU_NB, s, LRU_BW), F32), jax.ShapeDtypeStruct((2, DM), F32),
                   jax.ShapeDtypeStruct((1, DM), F32)],
        compiler_params=_params(1))(dyp, z, bai, lam, u, gu, hs)


def _lru_layer_fwd(x, gpre, gpost, win, convw, convb, wai, bai, lam, wout):
    s = x.shape[0]
    tm = _tile(s, 512)
    h = _norm_fwd(x, gpre)
    gu = _bmm_nn("lru_in", h, win)
    u = _lru_conv_fwd(gu, convw, convb)
    z = _mm("lru_gate", u, wai, grid=(2, LRU_NB, s // tm, 1),
            a_spec=BS((None, tm, LRU_BW), lambda k, n, i, r: (n, i, 0)),
            b_spec=BS((None, None, LRU_BW, LRU_BW), lambda k, n, i, r: (k, n, 0, 0)),
            o_spec=BS((None, None, tm, LRU_BW), lambda k, n, i, r: (k, n, i, 0)),
            out_shape=(2, LRU_NB, s, LRU_BW), dn=NN)
    hs, yp = _lru_scan_fwd(z, bai, lam, u, gu)
    y = _mm_nn("lru_out", yp, wout)
    return _norm_res(x, y, gpost), (x, h, gu, u, z, hs, yp, y)


def _lru_layer_bwd(dxo, saved, gpre, gpost, win, convw, convb, wai, bai, lam, wout):
    x, h, gu, u, z, hs, yp, y = saved
    s = x.shape[0]
    tm = _tile(s, 512)
    dy, dgpost = _norm_bwd(y, dxo, gpost, None, BF16)
    dyp = _mm_nt("lru_dyp", dy, wout)
    dwout = _mm_tn("lru_dwout", yp, dy)
    dgate, dz, dud, dbai, dlam = _lru_scan_bwd(dyp, z, bai, lam, u, gu, hs)
    dwai = _mm("lru_dwai", u, dz, grid=(2, LRU_NB, s // tm),
               a_spec=BS((None, tm, LRU_BW), lambda k, n, r: (n, r, 0)),
               b_spec=BS((None, None, tm, LRU_BW), lambda k, n, r: (k, n, r, 0)),
               o_spec=BS((None, None, LRU_BW, LRU_BW), lambda k, n, r: (k, n, 0, 0)),
               out_shape=(2, LRU_NB, LRU_BW, LRU_BW), dn=TN)
    dug = _mm("lru_dug", dz, wai, grid=(LRU_NB, s // tm, 2),
              a_spec=BS((None, None, tm, LRU_BW), lambda n, i, k: (k, n, i, 0)),
              b_spec=BS((None, None, LRU_BW, LRU_BW), lambda n, i, k: (k, n, 0, 0)),
              o_spec=BS((None, tm, LRU_BW), lambda n, i, k: (n, i, 0)),
              out_shape=(LRU_NB, s, LRU_BW), dn=NT)
    duraw, dconvw, dconvb = _lru_conv_bwd(dud, dug, gu, convw)
    dgu = jnp.concatenate([dgate, duraw], axis=0)
    dwin = _bmm_tn("lru_dwin", h, dgu)
    dh = _bmm_nt_sum("lru_dh", dgu, win)
    dx, dgpre = _norm_bwd(x, dh, gpre, dxo, F32)
    return dx, dgpre, dgpost, dwin, dconvw, dconvb, dwai, dbai, dlam, dwout


CHIP_FLIPS = ((1, 0), (0, 1), (1, 1))


def _place():
    return lax.axis_index("x"), lax.axis_index("y"), lax.axis_index("c")


def _flip(v, f):
    return 1 - v if f else v


def _comm_params():
    return pltpu.CompilerParams(vmem_limit_bytes=VMEM_LIMIT)


def _all_gather(shards):
    n = len(shards)

    def body(*refs):
        ins, outs, stage = refs[:n], refs[n:2 * n], refs[2 * n:3 * n]
        send_sems, recv_sems, local_sems = refs[3 * n:]
        x, y, c = _place()
        me, sibling = (x, y, c), (x, y, 1 - c)
        chips = [(_flip(x, fx), _flip(y, fy)) for fx, fy in CHIP_FLIPS]

        def slot(t, p):
            return outs[t].at[:, 4 * p[0] + 2 * p[1] + p[2]]

        def copy(t, k, block, to, src=None):
            return pltpu.make_async_remote_copy(
                src_ref=slot(t, block) if src is None else src, dst_ref=slot(t, block),
                send_sem=send_sems.at[7 * t + k], recv_sem=recv_sems.at[7 * t + k], device_id=to, device_id_type=MESH)

        first = []
        for t in range(n):
            first.append(copy(t, 0, me, sibling, src=ins[t]))
            first += [copy(t, 1 + j, me, (*chip, c), src=ins[t]) for j, chip in enumerate(chips)]
        for cp in first:
            cp.start()
        load = [pltpu.make_async_copy(ins[t], stage[t], local_sems.at[t]) for t in range(n)]
        mine = [pltpu.make_async_copy(stage[t], slot(t, me), local_sems.at[t]) for t in range(n)]
        for cp in load:
            cp.start()
        for t in range(n):
            load[t].wait()
            mine[t].start()
        passed = []
        for j, chip in enumerate(chips):
            for t in range(n):
                copy(t, 1 + j, (*chip, c), me).wait_recv()
                fwd = copy(t, 4 + j, (*chip, c), sibling)
                fwd.start()
                passed.append(fwd)
        for t in range(n):
            copy(t, 0, sibling, me).wait_recv()
            for j, chip in enumerate(chips):
                copy(t, 4 + j, (*chip, 1 - c), me).wait_recv()
        for cp in first + passed:
            cp.wait_send()
        for cp in mine:
            cp.wait()

    outs = [jax.ShapeDtypeStruct((s.shape[0], NDEV) + s.shape[1:], s.dtype) for s in shards]
    return pl.pallas_call(body, name="all_gather", in_specs=[ANY] * n, out_specs=[ANY] * n, out_shape=outs,
                          scratch_shapes=[pltpu.VMEM(s.shape, s.dtype) for s in shards]
                          + [pltpu.SemaphoreType.DMA((7 * n,)), pltpu.SemaphoreType.DMA((7 * n,)),
                             pltpu.SemaphoreType.DMA((n,))],
                          compiler_params=_comm_params())(*shards)


def _small_gather(v):
    def body(v_ref, o_ref, send_sems, recv_sems, local_sem):
        x, y, c = _place()
        mine = 4 * x + 2 * y + c
        local = pltpu.make_async_copy(v_ref, o_ref.at[mine], local_sem)
        local.start()
        sends = []
        for k in range(1, NDEV):
            fx, fy, fc = (k >> 2) & 1, (k >> 1) & 1, k & 1
            sends.append(pltpu.make_async_remote_copy(
                src_ref=v_ref, dst_ref=o_ref.at[mine], send_sem=send_sems.at[k - 1], recv_sem=recv_sems.at[k - 1],
                device_id=(_flip(x, fx), _flip(y, fy), _flip(c, fc)), device_id_type=MESH))
        for cp in sends:
            cp.start()
        for k in range(1, NDEV):
            fx, fy, fc = (k >> 2) & 1, (k >> 1) & 1, k & 1
            src = 4 * _flip(x, fx) + 2 * _flip(y, fy) + _flip(c, fc)
            pltpu.make_async_remote_copy(src_ref=v_ref, dst_ref=o_ref.at[src], send_sem=send_sems.at[k - 1],
                                         recv_sem=recv_sems.at[k - 1], device_id=(x, y, c), device_id_type=MESH).wait_recv()
        for cp in sends:
            cp.wait_send()
        local.wait()

    return pl.pallas_call(body, name="small_gather", in_specs=[ANY], out_specs=ANY,
                          out_shape=jax.ShapeDtypeStruct((NDEV,) + v.shape, v.dtype),
                          scratch_shapes=[pltpu.SemaphoreType.DMA((NDEV - 1,)), pltpu.SemaphoreType.DMA((NDEV - 1,)),
                                          pltpu.SemaphoreType.DMA],
                          compiler_params=_comm_params())(v)


REL_CHIPS = ((0, 0),) + CHIP_FLIPS


def _rs_d2d(g5s):
    n = len(g5s)

    def body(*refs):
        ins, gots = refs[:n], refs[n:2 * n]
        send_sems, recv_sems = refs[2 * n:]
        x, y, c = _place()
        copies = []
        for t in range(n):
            for f, (fx, fy) in enumerate(REL_CHIPS):
                copies.append(pltpu.make_async_remote_copy(
                    src_ref=ins[t].at[_flip(x, fx), _flip(y, fy), 1 - c], dst_ref=gots[t].at[f],
                    send_sem=send_sems.at[4 * t + f], recv_sem=recv_sems.at[4 * t + f], device_id=(x, y, 1 - c),
                    device_id_type=MESH))
        for cp in copies:
            cp.start()
        for cp in copies:
            cp.wait()

    out = [jax.ShapeDtypeStruct((4,) + g.shape[3:], F32) for g in g5s]
    return pl.pallas_call(body, name="rs_d2d", in_specs=[ANY] * n, out_specs=[ANY] * n, out_shape=out,
                          scratch_shapes=[pltpu.SemaphoreType.DMA((4 * n,)), pltpu.SemaphoreType.DMA((4 * n,))],
                          compiler_params=_comm_params())(*g5s)


def _rs_ici(parts):
    n = len(parts)

    def body(*refs):
        ins, outs = refs[:n], refs[n:2 * n]
        send_sems, recv_sems = refs[2 * n:]
        x, y, c = _place()
        copies = []
        for t in range(n):
            for f, (fx, fy) in enumerate(CHIP_FLIPS):
                copies.append(pltpu.make_async_remote_copy(
                    src_ref=ins[t].at[f], dst_ref=outs[t].at[f], send_sem=send_sems.at[3 * t + f],
                    recv_sem=recv_sems.at[3 * t + f], device_id=(_flip(x, fx), _flip(y, fy), c), device_id_type=MESH))
        for cp in copies:
            cp.start()
        for cp in copies:
            cp.wait()

    out = [jax.ShapeDtypeStruct(p.shape, p.dtype) for p in parts]
    return pl.pallas_call(body, name="rs_ici", in_specs=[ANY] * n, out_specs=[ANY] * n, out_shape=out,
                          scratch_shapes=[pltpu.SemaphoreType.DMA((3 * n,)), pltpu.SemaphoreType.DMA((3 * n,))],
                          compiler_params=_comm_params())(*parts)


HBM = pl.BlockSpec(memory_space=pltpu.HBM)
SEM = pl.BlockSpec(memory_space=pltpu.SEMAPHORE)
EFFECT = pltpu.SideEffectType.DATAFLOW_SIDE_EFFECTING


def _in_hbm(a):
    return pltpu.with_memory_space_constraint(a, pltpu.HBM)


def _rs_ici_copies(ins, lands, send_sems, recv_sems):
    x, y, c = _place()
    return [pltpu.make_async_remote_copy(
        src_ref=ins[t].at[f], dst_ref=lands[t].at[f], send_sem=send_sems.at[3 * t + f], recv_sem=recv_sems.at[3 * t + f],
        device_id=(_flip(x, fx), _flip(y, fy), c), device_id_type=MESH)
        for t in range(len(ins)) for f, (fx, fy) in enumerate(CHIP_FLIPS)]


def _rs_ici_start(parts, name):
    n = len(parts)

    def body(*refs):
        ins, lands = refs[:n], refs[n:2 * n]
        send_sems, recv_sems = refs[2 * n], refs[2 * n + 1]
        token = refs[-1]
        for cp in _rs_ici_copies(ins, lands, send_sems, recv_sems):
            cp.start()
        token[...] = jnp.zeros(token.shape, token.dtype)

    thru = [pltpu.HBM(p.shape, p.dtype) for p in parts]
    res = pl.pallas_call(
        body, name=name, in_specs=[HBM] * (2 * n),
        out_shape=(pltpu.SemaphoreType.DMA((3 * n,)), pltpu.SemaphoreType.DMA((3 * n,)), *thru, *thru,
                   jax.ShapeDtypeStruct((8, LANE), F32)),
        out_specs=(SEM, SEM, *([HBM] * (2 * n)), pl.BlockSpec(memory_space=pltpu.VMEM)),
        input_output_aliases={i: 2 + i for i in range(2 * n)},
        compiler_params=pltpu.CompilerParams(has_side_effects=EFFECT),
    )(*[_in_hbm(p) for p in parts], *[_in_hbm(lax.empty(p.shape, p.dtype)) for p in parts])
    return res[:-1], res[-1]


def _rs_ici_wait(state, after, name):
    n = (len(state) - 2) // 2

    def body(*refs):
        send_sems, recv_sems = refs[0], refs[1]
        ins, lands = refs[2:2 + n], refs[2 + n:2 + 2 * n]
        for cp in _rs_ici_copies(ins, lands, send_sems, recv_sems):
            cp.wait_send()
            cp.wait_recv()

    thru = [pltpu.HBM(s.shape, s.dtype) for s in state[2:]]
    res = pl.pallas_call(
        body, name=name, in_specs=[SEM, SEM] + [HBM] * (2 * n) + [ANY], out_shape=tuple(thru),
        out_specs=tuple([HBM] * (2 * n)), input_output_aliases={2 + i: i for i in range(2 * n)},
        compiler_params=pltpu.CompilerParams(has_side_effects=EFFECT),
    )(*state, after)
    return list(res[n:])


def _row_tile(rows):
    for t in (256, 128, 64, 32, 16, 8):
        if rows % t == 0:
            return t
    return rows


def _rs_chip_sum(pos, g5, got):
    a, b = g5.shape[3:]
    ta = _row_tile(a)

    def kern(pos_ref, o_ref, g_ref, p_ref):
        p_ref[...] = (o_ref[...] + g_ref[...]).astype(BF16)

    def mine(f, i, pos_ref):
        return (pos_ref[0] ^ ((f + 1) & 1), pos_ref[1] ^ ((f + 1) >> 1), pos_ref[2], i, 0)

    spec = pltpu.PrefetchScalarGridSpec(
        num_scalar_prefetch=1, grid=(3, a // ta),
        in_specs=[BS((None, None, None, ta, b), mine), BS((None, ta, b), lambda f, i, pos_ref: (f + 1, i, 0))],
        out_specs=BS((None, ta, b), lambda f, i, pos_ref: (f, i, 0)))
    return pl.pallas_call(kern, name="rs_chip_sum", grid_spec=spec, out_shape=jax.ShapeDtypeStruct((3, a, b), BF16),
                          compiler_params=_params(2))(pos, g5, got)


def _rs_final_sum(pos, g5, got, recv):
    a, b = g5.shape[3:]
    ta = _row_tile(a)

    def kern(pos_ref, o_ref, g_ref, r_ref, s_ref):
        acc = o_ref[...] + g_ref[...]
        for f in range(3):
            acc = acc + r_ref[f].astype(F32)
        s_ref[...] = acc

    spec = pltpu.PrefetchScalarGridSpec(
        num_scalar_prefetch=1, grid=(a // ta,),
        in_specs=[BS((None, None, None, ta, b), lambda i, pos_ref: (pos_ref[0], pos_ref[1], pos_ref[2], i, 0)),
                  BS((None, ta, b), lambda i, pos_ref: (0, i, 0)), BS((3, ta, b), lambda i, pos_ref: (0, i, 0))],
        out_specs=BS((ta, b), lambda i, pos_ref: (i, 0)))
    return pl.pallas_call(kern, name="rs_final_sum", grid_spec=spec, out_shape=jax.ShapeDtypeStruct((a, b), F32),
                          compiler_params=_params(1))(pos, g5, got, recv)


def _reduce_scatter(grads, pos):
    g5s = [g.reshape((2, 2, 2) + g.shape[1:]) for g in grads]
    gots = _rs_d2d(g5s)
    parts = [_rs_chip_sum(pos, g, got) for g, got in zip(g5s, gots)]
    recvs = _rs_ici(parts)
    return [_rs_final_sum(pos, g, got, r) for g, got, r in zip(g5s, gots, recvs)]


def _rs_begin(grads, pos, tag):
    g5s = [g.reshape((2, 2, 2) + g.shape[1:]) for g in grads]
    gots = _rs_d2d(g5s)
    parts = [_rs_chip_sum(pos, g, got) for g, got in zip(g5s, gots)]
    state, token = _rs_ici_start(parts, "rs_ici_start_" + tag)
    return (g5s, gots, state, tag), token


def _rs_end(pending, after, pos):
    g5s, gots, state, tag = pending
    recvs = _rs_ici_wait(state, after, "rs_ici_wait_" + tag)
    return [_rs_final_sum(pos, g, got, r) for g, got, r in zip(g5s, gots, recvs)]


def _sum_devices(v):
    _, r, _ = v.shape

    def kern(v_ref, o_ref):
        acc = v_ref[0]
        for d in range(1, NDEV):
            acc = acc + v_ref[d]
        o_ref[...] = acc

    return pl.pallas_call(kern, name="sum_devices", out_shape=jax.ShapeDtypeStruct((r, LANE), F32),
                          compiler_params=_comm_params())(v)


def _loss_head(xf, target):
    s = xf.shape[0]
    tm = _tile(s, 512)

    def kern(x_ref, t_ref, dx_ref, l_ref):
        err = x_ref[...] - t_ref[...]
        dx_ref[...] = err * (1.0 / DM)
        part = jnp.broadcast_to(0.5 * jnp.sum(jnp.mean(err * err, axis=-1, keepdims=True), axis=0, keepdims=True), (8, LANE))

        @pl.when(pl.program_id(0) == 0)
        def _():
            l_ref[...] = part

        @pl.when(pl.program_id(0) > 0)
        def _():
            l_ref[...] += part

    row = BS((tm, DM), lambda i: (i, 0))
    return pl.pallas_call(kern, name="loss_head", grid=(s // tm,), in_specs=[row, row],
                          out_specs=[row, BS((8, LANE), lambda i: (0, 0))],
                          out_shape=[jax.ShapeDtypeStruct((s, DM), F32), jax.ShapeDtypeStruct((8, LANE), F32)],
                          compiler_params=_params(1))(xf, target)


def _adamw(w, g, m, v):
    rows, cols = w.shape
    tr = _row_tile(rows)

    def kern(w_ref, g_ref, m_ref, v_ref, d_ref, nm_ref, nv_ref):
        gv = g_ref[...]
        nm = ADAM_B1 * m_ref[...] + (1.0 - ADAM_B1) * gv
        nv = ADAM_B2 * v_ref[...] + (1.0 - ADAM_B2) * (gv * gv)
        m_hat = nm / (1.0 - ADAM_B1 ** ADAM_STEP)
        v_hat = nv / (1.0 - ADAM_B2 ** ADAM_STEP)
        d_ref[...] = -ADAM_LR * (m_hat / (jnp.sqrt(v_hat) + ADAM_EPS) + ADAM_WD * w_ref[...])
        nm_ref[...] = nm
        nv_ref[...] = nv

    blk = BS((tr, cols), lambda i: (i, 0))
    shp = jax.ShapeDtypeStruct((rows, cols), F32)
    return pl.pallas_call(kern, name="adamw", grid=(rows // tr,), in_specs=[blk] * 4, out_specs=[blk] * 3,
                          out_shape=[shp] * 3, compiler_params=_params(1))(w, g, m, v)


def _adamw_nd(w, g, m, v):
    shape = w.shape
    two = (math.prod(shape[:-1]), shape[-1])
    return tuple(o.reshape(shape) for o in _adamw(w.reshape(two), g.reshape(two), m.reshape(two), v.reshape(two)))


def _pack_small(parts):
    flat = jnp.concatenate([p.reshape(-1) for p in parts])
    pad = (-flat.shape[0]) % (8 * LANE)
    return jnp.pad(flat, (0, pad)).reshape(-1, LANE)


def _unpack_small(packed, shapes, lead=()):
    flat = packed.reshape(lead + (-1,))
    out, off = [], 0
    for shp in shapes:
        n = math.prod(shp)
        out.append(flat[..., off:off + n].reshape(lead + tuple(shp)))
        off += n
    return out


def _blocks_of_columns(w):
    k, n = w.shape
    return w.reshape(k, NDEV, n // NDEV).transpose(1, 0, 2)


def _columns_of_blocks(wb):
    n, k, c = wb.shape
    return wb.transpose(1, 0, 2).reshape(k, n * c)


WEIGHT_NAMES = ('g_mix_pre', 'g_mix_post', 'g_cross_pre', 'g_mem', 'g_cross_post', 'g_ffn_pre', 'g_ffn_post', 'w_xq',
                'w_xkv', 'w_xo', 'w_ffn_gu', 'w_ffn_down', 'ab_w_in', 'ab_b_f', 'ab_conv_w', 'ab_w_out', 'c_w_in',
                'c_conv_w', 'c_conv_b', 'c_w_a', 'c_b_a', 'c_w_i', 'c_b_i', 'c_lam', 'c_w_out')
BIG = ('w_xq', 'w_xkv', 'w_xo', 'w_ffn_gu', 'w_ffn_down', 'ab_w_in', 'ab_w_out', 'c_w_in', 'c_w_a', 'c_w_i', 'c_w_out')
SMALL_SHARDED = ('ab_conv_w', 'c_conv_w', 'c_conv_b', 'c_b_a', 'c_b_i', 'c_lam')
REPLICATED = ('g_mix_pre', 'g_mix_post', 'g_cross_pre', 'g_mem', 'g_cross_post', 'g_ffn_pre', 'g_ffn_post', 'ab_b_f')


def _small_full(name, gathered):
    nd = gathered.ndim
    return jnp.moveaxis(gathered, 0, nd - 2).reshape(gathered.shape[1:-1] + (NDEV * gathered.shape[-1],))


def _small_shard(full, dev):
    c = full.shape[-1] // NDEV
    return lax.dynamic_slice_in_dim(full, dev * c, c, axis=full.ndim - 1)


def kernel(x, mem, g_mix_pre, g_mix_post, g_cross_pre, g_mem, g_cross_post, g_ffn_pre, g_ffn_post, w_xq, w_xkv, w_xo, w_ffn_gu, w_ffn_down, ab_w_in, ab_b_f, ab_conv_w, ab_w_out, c_w_in, c_conv_w, c_conv_b, c_w_a, c_b_a, c_w_i, c_b_i, c_lam, c_w_out, loss_target, m_g_mix_pre, m_g_mix_post, m_g_cross_pre, m_g_mem, m_g_cross_post, m_g_ffn_pre, m_g_ffn_post, m_w_xq, m_w_xkv, m_w_xo, m_w_ffn_gu, m_w_ffn_down, m_ab_w_in, m_ab_b_f, m_ab_conv_w, m_ab_w_out, m_c_w_in, m_c_conv_w, m_c_conv_b, m_c_w_a, m_c_b_a, m_c_w_i, m_c_b_i, m_c_lam, m_c_w_out, v_g_mix_pre, v_g_mix_post, v_g_cross_pre, v_g_mem, v_g_cross_post, v_g_ffn_pre, v_g_ffn_post, v_w_xq, v_w_xkv, v_w_xo, v_w_ffn_gu, v_w_ffn_down, v_ab_w_in, v_ab_b_f, v_ab_conv_w, v_ab_w_out, v_c_w_in, v_c_conv_w, v_c_conv_b, v_c_w_a, v_c_b_a, v_c_w_i, v_c_b_i, v_c_lam, v_c_w_out):
    args = locals()
    w = {n: args[n] for n in WEIGHT_NAMES}
    mom = {n: args["m_" + n] for n in WEIGHT_NAMES}
    var = {n: args["v_" + n] for n in WEIGHT_NAMES}
    pos = jnp.stack([lax.axis_index("x"), lax.axis_index("y"), lax.axis_index("c")]).astype(jnp.int32)
    dev = 4 * pos[0] + 2 * pos[1] + pos[2]
    xs, mems, target = x[0], mem[0], loss_target[0]
    n_even, n_odd = (DEPTH + 1) // 2, DEPTH // 2

    def shard3(name):
        s = w[name].astype(BF16)
        return s.reshape(s.shape[0], -1, s.shape[-1]) if s.ndim == 4 else s
    full = dict(zip(BIG, _all_gather([shard3(n) for n in BIG])))
    small_shapes = [w[n].shape for n in SMALL_SHARDED]
    gathered_small = _unpack_small(_small_gather(_pack_small([w[n] for n in SMALL_SHARDED])), small_shapes, (NDEV,))
    small = {n: _small_full(n, g) for n, g in zip(SMALL_SHARDED, gathered_small)}

    wq = full['w_xq'].reshape(DEPTH, DM, DM)
    wo = full['w_xo'].reshape(DEPTH, DM, DM)
    wkv = full['w_xkv']
    wgu = full['w_ffn_gu']
    wd = full['w_ffn_down'].reshape(DEPTH, D_FF, DM)
    ab_wall = [_ab_pack(_columns_of_blocks(full['ab_w_in'][e])) for e in range(n_even)]
    ab_wout = full['ab_w_out'].reshape(n_even, DM, DM)
    ab_bfb = jnp.broadcast_to(ab_b_f[:, :, None], (n_even, FOX_H, LANE))
    c_win = full['c_w_in']
    c_wout = full['c_w_out'].reshape(n_odd, DM, DM)

    def gate_w(name):
        g = full[name].reshape(n_odd, NDEV, LRU_NB, LRU_BW // NDEV, LRU_BW)
        return g.transpose(0, 2, 1, 3, 4).reshape(n_odd, LRU_NB, LRU_BW, LRU_BW)
    c_wai = jnp.stack([gate_w('c_w_a'), gate_w('c_w_i')], axis=1)
    c_bai = jnp.stack([small['c_b_a'].reshape(n_odd, DM), small['c_b_i'].reshape(n_odd, DM)], axis=1)
    row = lambda a, l: a[l][None]

    def mixer_args(l):
        if l % 2 == 0:
            e = l // 2
            return (row(g_mix_pre, l), row(g_mix_post, l), ab_wall[e], ab_bfb[e], small['ab_conv_w'][e], ab_wout[e])
        o = l // 2
        return (row(g_mix_pre, l), row(g_mix_post, l), c_win[o], small['c_conv_w'][o], row(small['c_conv_b'], o),
                c_wai[o], c_bai[o], row(small['c_lam'], o), c_wout[o])

    def cross_args(l):
        return (row(g_cross_pre, l), row(g_mem, l), row(g_cross_post, l), wq[l], wkv[l], wo[l])

    def ffn_args(l):
        return (row(g_ffn_pre, l), row(g_ffn_post, l), wgu[l], wd[l])

    saved = []
    h = xs
    for l in range(DEPTH):
        h, s_mix = (_fox_layer_fwd if l % 2 == 0 else _lru_layer_fwd)(h, *mixer_args(l))
        h, s_cross = _cross_fwd(h, mems, *cross_args(l))
        h, s_ffn = _ffn_fwd(h, *ffn_args(l))
        saved.append((s_mix, s_cross, s_ffn))
    dx, loss_rep = _loss_head(h, target)
    loss = lax.psum(loss_rep[0, 0], ("x", "y", "c"))

    grads = {n: [None] * w[n].shape[0] for n in BIG}
    partial = {n: [None] * w[n].shape[0] for n in REPLICATED + SMALL_SHARDED}
    pending = None

    def finish(pending, after):
        state, names, where = pending
        for n, g in zip(names, _rs_end(state, after, pos)):
            grads[n][where[n]] = g

    for l in reversed(range(DEPTH)):
        s_mix, s_cross, s_ffn = saved[l]
        if pending is not None:
            dx, _ = lax.optimization_barrier((dx, token))
        dx, partial['g_ffn_pre'][l], partial['g_ffn_post'][l], dwgu, dwd = _ffn_bwd(dx, s_ffn, *ffn_args(l))
        (dx, partial['g_cross_pre'][l], partial['g_mem'][l], partial['g_cross_post'][l], dwq, dwkv, dwo) = _cross_bwd(
            dx, s_cross, mems, *cross_args(l))
        layer = {'w_xq': (l, dwq.reshape(NDEV, DM // NDEV, DM)), 'w_xkv': (l, dwkv), 'w_xo': (l, dwo.reshape(NDEV, DM // NDEV, DM)),
                 'w_ffn_gu': (l, dwgu), 'w_ffn_down': (l, dwd.reshape(NDEV, D_FF // NDEV, DM))}
        if l % 2 == 0:
            e = l // 2
            (dx, partial['g_mix_pre'][l], partial['g_mix_post'][l], dwall, partial['ab_b_f'][e], partial['ab_conv_w'][e],
             dwout) = _fox_layer_bwd(dx, s_mix, *mixer_args(l))
            layer['ab_w_in'] = (e, _blocks_of_columns(_ab_unpack(dwall)))
            layer['ab_w_out'] = (e, dwout.reshape(NDEV, DM // NDEV, DM))
        else:
            o = l // 2
            (dx, partial['g_mix_pre'][l], partial['g_mix_post'][l], dwin, partial['c_conv_w'][o], dconvb, dwai, dbai, dlam,
             dwout) = _lru_layer_bwd(dx, s_mix, *mixer_args(l))
            partial['c_conv_b'][o], partial['c_lam'][o] = dconvb[0], dlam[0]
            partial['c_b_a'][o], partial['c_b_i'][o] = dbai[0].reshape(LRU_NB, LRU_BW), dbai[1].reshape(LRU_NB, LRU_BW)
            rows = LRU_BW // NDEV
            by_dev = lambda d: d.reshape(LRU_NB, NDEV, rows, LRU_BW).transpose(1, 0, 2, 3).reshape(NDEV, LRU_NB * rows, LRU_BW)
            layer['c_w_in'] = (o, dwin)
            layer['c_w_a'] = (o, by_dev(dwai[0]))
            layer['c_w_i'] = (o, by_dev(dwai[1]))
            layer['c_w_out'] = (o, dwout.reshape(NDEV, DM // NDEV, DM))
        if pending is not None:
            finish(pending, dx)
        names = list(layer)
        state, token = _rs_begin([layer[n][1] for n in names], pos, str(l))
        pending = (state, names, {n: layer[n][0] for n in names})
    finish(pending, dx)

    small_names = REPLICATED + SMALL_SHARDED
    small_parts = [jnp.stack([p.reshape(w[n].shape[1:] if n in REPLICATED else small[n].shape[1:]) for p in partial[n]])
                   for n in small_names]
    reduced = _unpack_small(_sum_devices(_small_gather(_pack_small(small_parts))), [p.shape for p in small_parts])
    grad = {}
    for n, g in zip(small_names, reduced):
        grad[n] = g if n in REPLICATED else _small_shard(g, dev)
    for n in BIG:
        grad[n] = jnp.stack(grads[n]).reshape(w[n].shape)

    delta, new_m, new_v = {}, {}, {}
    for n in BIG:
        delta[n], new_m[n], new_v[n] = _adamw_nd(w[n], grad[n], mom[n], var[n])
    shapes = [w[n].shape for n in small_names]
    packed = [_pack_small([t[n] for n in small_names]) for t in (w, grad, mom, var)]
    for res, out in zip(_adamw(*packed), (delta, new_m, new_v)):
        for n, val in zip(small_names, _unpack_small(res, shapes)):
            out[n] = val

    return (loss, dx[None], *[grad[n] for n in WEIGHT_NAMES], *[delta[n] for n in WEIGHT_NAMES],
            *[new_m[n] for n in WEIGHT_NAMES], *[new_v[n] for n in WEIGHT_NAMES])
```

```python
import functools
import math

import jax
import jax.numpy as jnp
from jax import lax
from jax.experimental import pallas as pl
from jax.experimental.pallas import tpu as pltpu

F32 = jnp.float32
BF16 = jnp.bfloat16
BS = pl.BlockSpec
ANY = pl.BlockSpec(memory_space=pl.ANY)
MESH = pl.DeviceIdType.MESH

DM = 1024
DEPTH = 4
EPS = 1e-6
NEG = -1e30
FOX_W = 512
FOX_HD = 64
FOX_H = 8
SC_W = 512
SC_K = 3
AB_IN = 3 * FOX_W + FOX_H + 3 * SC_W
AB_PAD = 3200
LRU_BW = 256
LRU_NB = 4
RG_K = 4
RG_C = 8.0
MEM_H = 4
MEM_HD = 256
D_FF = 2816
NDEV = 8
FFB = 2 * D_FF // NDEV
ADAM_LR, ADAM_B1, ADAM_B2, ADAM_EPS, ADAM_WD, ADAM_STEP = 0.001, 0.9, 0.999, 1e-08, 0.01, 10

LANE = 128
VMEM_LIMIT = 48 * 1024 * 1024


def _params(ngrid):
    return pltpu.CompilerParams(dimension_semantics=("arbitrary",) * ngrid, vmem_limit_bytes=VMEM_LIMIT)


def _tile(n, t):
    return t if n % t == 0 else n


def _mm(name, a, b, *, grid, a_spec, b_spec, o_spec, out_shape, dn, out_dtype=F32):
    nred = grid[-1]
    ngrid = len(grid)

    def kern(a_ref, b_ref, o_ref, *scratch):
        p = lax.dot_general(a_ref[...].astype(BF16), b_ref[...].astype(BF16), (dn, ((), ())),
                            preferred_element_type=F32)
        if nred == 1:
            o_ref[...] = p.astype(o_ref.dtype)
            return
        acc = scratch[0] if scratch else o_ref
        r = pl.program_id(ngrid - 1)

        @pl.when(r == 0)
        def _():
            acc[...] = p

        @pl.when(r > 0)
        def _():
            acc[...] += p

        if scratch:
            @pl.when(r == nred - 1)
            def _():
                o_ref[...] = acc[...].astype(o_ref.dtype)

    blk = tuple(d for d in o_spec.block_shape if d is not None)
    scratch = [pltpu.VMEM(blk, F32)] if (nred > 1 and out_dtype != F32) else []
    return pl.pallas_call(kern, name=name, grid=grid, in_specs=[a_spec, b_spec], out_specs=o_spec,
                          out_shape=jax.ShapeDtypeStruct(out_shape, out_dtype), scratch_shapes=scratch,
                          compiler_params=_params(ngrid))(a, b)


NN = ((1,), (0,))
NT = ((1,), (1,))
TN = ((0,), (0,))


def _mm_nn(name, a, w, out_dtype=F32, tn=None):
    m, k = a.shape
    n = w.shape[1]
    tm = _tile(m, 512)
    tn = n if tn is None else tn
    return _mm(name, a, w, grid=(m // tm, n // tn, 1), a_spec=BS((tm, k), lambda i, j, r: (i, 0)),
               b_spec=BS((k, tn), lambda i, j, r: (0, j)), o_spec=BS((tm, tn), lambda i, j, r: (i, j)),
               out_shape=(m, n), dn=NN, out_dtype=out_dtype)


def _mm_nt(name, a, w, out_dtype=F32, tn=None):
    m, n = a.shape
    k = w.shape[0]
    tm = _tile(m, 512)
    tn = n if tn is None else tn
    return _mm(name, a, w, grid=(m // tm, n // tn), a_spec=BS((tm, tn), lambda i, r: (i, r)),
               b_spec=BS((k, tn), lambda i, r: (0, r)), o_spec=BS((tm, k), lambda i, r: (i, 0)),
               out_shape=(m, k), dn=NT, out_dtype=out_dtype)


def _mm_tn(name, a, b, tn=None):
    m, k = a.shape
    n = b.shape[1]
    tm = _tile(m, 512)
    tn = n if tn is None else tn
    return _mm(name, a, b, grid=(n // tn, m // tm), a_spec=BS((tm, k), lambda j, r: (r, 0)),
               b_spec=BS((tm, tn), lambda j, r: (r, j)), o_spec=BS((k, tn), lambda j, r: (0, j)),
               out_shape=(k, n), dn=TN)


def _bmm_nn(name, a, w, out_dtype=F32):
    m, k = a.shape
    g, _, n = w.shape
    tm = _tile(m, 512)
    return _mm(name, a, w, grid=(g, m // tm, 1), a_spec=BS((tm, k), lambda q, i, r: (i, 0)),
               b_spec=BS((None, k, n), lambda q, i, r: (q, 0, 0)), o_spec=BS((None, tm, n), lambda q, i, r: (q, i, 0)),
               out_shape=(g, m, n), dn=NN, out_dtype=out_dtype)


def _bmm_tn(name, a, b):
    m, k = a.shape
    g, _, n = b.shape
    tm = _tile(m, 512)
    return _mm(name, a, b, grid=(g, m // tm), a_spec=BS((tm, k), lambda q, r: (r, 0)),
               b_spec=BS((None, tm, n), lambda q, r: (q, r, 0)), o_spec=BS((None, k, n), lambda q, r: (q, 0, 0)),
               out_shape=(g, k, n), dn=TN)


def _bmm_nt_sum(name, a, w):
    g, m, n = a.shape
    k = w.shape[1]
    tm = _tile(m, 512)
    return _mm(name, a, w, grid=(m // tm, g), a_spec=BS((None, tm, n), lambda i, q: (q, i, 0)),
               b_spec=BS((None, k, n), lambda i, q: (q, 0, 0)), o_spec=BS((tm, k), lambda i, q: (i, 0)),
               out_shape=(m, k), dn=NT)


def _bmm_nn_sum(name, a, w):
    g, m, k = a.shape
    n = w.shape[2]
    tm = _tile(m, 512)
    return _mm(name, a, w, grid=(m // tm, g), a_spec=BS((None, tm, k), lambda i, q: (q, i, 0)),
               b_spec=BS((None, k, n), lambda i, q: (q, 0, 0)), o_spec=BS((tm, n), lambda i, q: (i, 0)),
               out_shape=(m, n), dn=NN)


def _bbmm_tn(name, a, b):
    g, m, k = a.shape
    n = b.shape[2]
    tm = _tile(m, 512)
    return _mm(name, a, b, grid=(g, m // tm), a_spec=BS((None, tm, k), lambda q, r: (q, r, 0)),
               b_spec=BS((None, tm, n), lambda q, r: (q, r, 0)), o_spec=BS((None, k, n), lambda q, r: (q, 0, 0)),
               out_shape=(g, k, n), dn=TN)


def _rstd(x):
    return lax.rsqrt(jnp.mean(x * x, axis=-1, keepdims=True) + EPS)


def _norm_fwd(x, g, after=None):
    rows = x.shape[0]
    tm = _tile(rows, 512)

    def kern(x_ref, g_ref, *rest):
        xv = x_ref[...]
        rest[-1][...] = ((xv * _rstd(xv)) * g_ref[...]).astype(BF16)

    extra = () if after is None else (after,)
    return pl.pallas_call(kern, name="norm_fwd", grid=(rows // tm,),
                          in_specs=[BS((tm, DM), lambda i: (i, 0)), BS((1, DM), lambda i: (0, 0))] + [ANY] * len(extra),
                          out_specs=BS((tm, DM), lambda i: (i, 0)),
                          out_shape=jax.ShapeDtypeStruct((rows, DM), BF16), compiler_params=_params(1))(x, g, *extra)


def _norm_res(x, y, g):
    rows = x.shape[0]
    tm = _tile(rows, 512)

    def kern(x_ref, y_ref, g_ref, o_ref):
        yv = y_ref[...]
        o_ref[...] = x_ref[...] + (yv * _rstd(yv)) * g_ref[...]

    row = BS((tm, DM), lambda i: (i, 0))
    return pl.pallas_call(kern, name="norm_res", grid=(rows // tm,),
                          in_specs=[row, row, BS((1, DM), lambda i: (0, 0))], out_specs=row,
                          out_shape=jax.ShapeDtypeStruct((rows, DM), F32), compiler_params=_params(1))(x, y, g)


def _norm_bwd(z, dout, g, resid, out_dtype, after=None):
    rows = z.shape[0]
    tm = _tile(rows, 512)
    has_res = resid is not None

    def kern(*refs):
        z_ref, d_ref, g_ref = refs[:3]
        r_ref = refs[3] if has_res else None
        dz_ref, dg_ref = refs[-2:]
        zv = z_ref[...]
        dv = d_ref[...].astype(F32)
        r = _rstd(zv)
        zh = zv * r
        dzh = dv * g_ref[...]
        dz = r * (dzh - zh * jnp.mean(dzh * zh, axis=-1, keepdims=True))
        if has_res:
            dz = dz + r_ref[...]
        dz_ref[...] = dz.astype(dz_ref.dtype)
        part = jnp.sum(dv * zh, axis=0, keepdims=True)

        @pl.when(pl.program_id(0) == 0)
        def _():
            dg_ref[...] = part

        @pl.when(pl.program_id(0) > 0)
        def _():
            dg_ref[...] += part

    row = BS((tm, DM), lambda i: (i, 0))
    vec = BS((1, DM), lambda i: (0, 0))
    ins = [row, row, vec] + ([row] if has_res else []) + ([ANY] if after is not None else [])
    args = (z, dout, g) + ((resid,) if has_res else ()) + ((after,) if after is not None else ())
    return pl.pallas_call(kern, name="norm_bwd_res" if has_res else "norm_bwd", grid=(rows // tm,), in_specs=ins,
                          out_specs=[row, vec],
                          out_shape=[jax.ShapeDtypeStruct((rows, DM), out_dtype), jax.ShapeDtypeStruct((1, DM), F32)],
                          compiler_params=_params(1))(*args)


def _ffn_up(h, wgu4):
    s = h.shape[0]
    tm = _tile(s, 512)

    def kern(h_ref, w_ref, gu_ref, a_ref):
        hv = h_ref[...]
        gate = jnp.dot(hv, w_ref[0], preferred_element_type=F32)
        up = jnp.dot(hv, w_ref[1], preferred_element_type=F32)
        gu_ref[0] = gate
        gu_ref[1] = up
        a_ref[...] = (gate * jax.nn.sigmoid(gate) * up).astype(BF16)

    return pl.pallas_call(
        kern, name="ffn_up", grid=(4, s // tm),
        in_specs=[BS((tm, DM), lambda j, i: (i, 0)), BS((2, None, DM, FFB), lambda j, i: (0, j, 0, 0))],
        out_specs=[BS((2, None, tm, FFB), lambda j, i: (0, j, i, 0)), BS((None, tm, FFB), lambda j, i: (j, i, 0))],
        out_shape=[jax.ShapeDtypeStruct((2, 4, s, FFB), F32), jax.ShapeDtypeStruct((4, s, FFB), BF16)],
        compiler_params=_params(2))(h, wgu4)


def _ffn_da(dy, wd4, gu):
    s = dy.shape[0]
    tm = _tile(s, 512)

    def kern(dy_ref, w_ref, gu_ref, o_ref):
        da = lax.dot_general(dy_ref[...], w_ref[...], (NT, ((), ())), preferred_element_type=F32)
        gate = gu_ref[0]
        up = gu_ref[1]
        sg = jax.nn.sigmoid(gate)
        o_ref[0] = (da * up * (sg * (1.0 + gate * (1.0 - sg)))).astype(BF16)
        o_ref[1] = (da * (gate * sg)).astype(BF16)

    blk = BS((2, None, tm, FFB), lambda j, i: (0, j, i, 0))
    return pl.pallas_call(
        kern, name="ffn_da", grid=(4, s // tm),
        in_specs=[BS((tm, DM), lambda j, i: (i, 0)), BS((None, FFB, DM), lambda j, i: (j, 0, 0)), blk],
        out_specs=blk, out_shape=jax.ShapeDtypeStruct((2, 4, s, FFB), BF16), compiler_params=_params(2))(dy, wd4, gu)


def _ffn_fwd(x, gpre, gpost, wgu, wd):
    h = _norm_fwd(x, gpre)
    gu, a = _ffn_up(h, wgu.reshape(2, 4, DM, FFB))
    y = _bmm_nn_sum("ffn_down", a, wd.reshape(4, FFB, DM))
    return _norm_res(x, y, gpost), (x, h, gu, a, y)


def _ffn_bwd(dxo, saved, gpre, gpost, wgu, wd, after=None):
    x, h, gu, a, y = saved
    s = x.shape[0]
    dy, dgpost = _norm_bwd(y, dxo, gpost, None, BF16, after)
    dgu = _ffn_da(dy, wd.reshape(4, FFB, DM), gu).reshape(8, s, FFB)
    dwd = _bmm_tn_a3("ffn_dwd", a, dy)
    dwgu = _bmm_tn("ffn_dwgu", h, dgu)
    dh = _bmm_nt_sum("ffn_dh", dgu, wgu)
    dx, dgpre = _norm_bwd(x, dh, gpre, dxo, F32)
    return dx, dgpre, dgpost, dwgu, dwd.reshape(D_FF, DM)


def _bmm_tn_a3(name, a, b):
    g, m, k = a.shape
    n = b.shape[1]
    tm = _tile(m, 512)
    return _mm(name, a, b, grid=(g, m // tm), a_spec=BS((None, tm, k), lambda q, r: (q, r, 0)),
               b_spec=BS((tm, n), lambda q, r: (r, 0)), o_spec=BS((None, k, n), lambda q, r: (q, 0, 0)),
               out_shape=(g, k, n), dn=TN)


def _softmax_rows(s):
    m = jnp.max(s, axis=-1, keepdims=True)
    p = jnp.exp(s - m)
    return p / jnp.sum(p, axis=-1, keepdims=True)


def _xattn_fwd_call(h, wq, kv):
    s = h.shape[0]
    mlen = kv.shape[1]
    tm = _tile(s, 512)
    scale = MEM_HD ** -0.5

    def kern(h_ref, w_ref, k_ref, v_ref, q_ref, o_ref):
        q = jnp.dot(h_ref[...], w_ref[...], preferred_element_type=F32).astype(BF16)
        q_ref[...] = q
        sc = lax.dot_general(q, k_ref[...], (NT, ((), ())), preferred_element_type=F32) * scale
        p = _softmax_rows(sc)
        o_ref[...] = jnp.dot(p.astype(BF16), v_ref[...], preferred_element_type=F32).astype(BF16)

    blk = BS((tm, MEM_HD), lambda i, hd: (i, hd))
    return pl.pallas_call(
        kern, name="xattn_fwd", grid=(s // tm, MEM_H),
        in_specs=[BS((tm, DM), lambda i, hd: (i, 0)), BS((DM, MEM_HD), lambda i, hd: (0, hd)),
                  BS((None, mlen, MEM_HD), lambda i, hd: (hd, 0, 0)),
                  BS((None, mlen, MEM_HD), lambda i, hd: (MEM_H + hd, 0, 0))],
        out_specs=[blk, blk],
        out_shape=[jax.ShapeDtypeStruct((s, DM), BF16), jax.ShapeDtypeStruct((s, DM), BF16)],
        compiler_params=_params(2))(h, wq, kv, kv)


def _xattn_bwd_call(q, kv, do):
    s = q.shape[0]
    mlen = kv.shape[1]
    tm = _tile(s, 512)
    scale = MEM_HD ** -0.5

    def kern(q_ref, k_ref, v_ref, do_ref, dq_ref, dkv_ref):
        qv, kvv, vv, dov = q_ref[...], k_ref[...], v_ref[...], do_ref[...]
        sc = lax.dot_general(qv, kvv, (NT, ((), ())), preferred_element_type=F32) * scale
        p = _softmax_rows(sc)
        dp = lax.dot_general(dov, vv, (NT, ((), ())), preferred_element_type=F32)
        ds = (p * (dp - jnp.sum(dp * p, axis=-1, keepdims=True)) * scale).astype(BF16)
        dq_ref[...] = jnp.dot(ds, kvv, preferred_element_type=F32).astype(BF16)
        dk = lax.dot_general(ds, qv, (TN, ((), ())), preferred_element_type=F32)
        dv = lax.dot_general(p.astype(BF16), dov, (TN, ((), ())), preferred_element_type=F32)

        @pl.when(pl.program_id(1) == 0)
        def _():
            dkv_ref[0] = dk
            dkv_ref[1] = dv

        @pl.when(pl.program_id(1) > 0)
        def _():
            dkv_ref[0] += dk
            dkv_ref[1] += dv

    blk = BS((tm, MEM_HD), lambda hd, i: (i, hd))
    return pl.pallas_call(
        kern, name="xattn_bwd", grid=(MEM_H, s // tm),
        in_specs=[blk, BS((None, mlen, MEM_HD), lambda hd, i: (hd, 0, 0)),
                  BS((None, mlen, MEM_HD), lambda hd, i: (MEM_H + hd, 0, 0)), blk],
        out_specs=[blk, BS((2, None, mlen, MEM_HD), lambda hd, i: (0, hd, 0, 0))],
        out_shape=[jax.ShapeDtypeStruct((s, DM), BF16), jax.ShapeDtypeStruct((2, MEM_H, mlen, MEM_HD), F32)],
        compiler_params=_params(2))(q, kv, kv, do)


def _cross_fwd(x, mem, gpre, gmem, gpost, wq, wkv, wo):
    h = _norm_fwd(x, gpre)
    mn = _norm_fwd(mem, gmem)
    kv = _bmm_nn("xattn_kv", mn, wkv, BF16)
    q, o = _xattn_fwd_call(h, wq, kv)
    y = _mm_nn("xattn_out", o, wo)
    return _norm_res(x, y, gpost), (x, h, mn, kv, q, o, y)


def _cross_bwd(dxo, saved, mem, gpre, gmem, gpost, wq, wkv, wo):
    x, h, mn, kv, q, o, y = saved
    mlen = mem.shape[0]
    dy, dgpost = _norm_bwd(y, dxo, gpost, None, BF16)
    do = _mm_nt("xattn_do", dy, wo, BF16)
    dwo = _mm_tn("xattn_dwo", o, dy)
    dq, dkv = _xattn_bwd_call(q, kv, do)
    dwq = _mm_tn("xattn_dwq", h, dq)
    dh = _mm_nt("xattn_dh", dq, wq)
    dkv8 = dkv.reshape(8, mlen, MEM_HD)
    dwkv = _bmm_tn("xattn_dwkv", mn, dkv8)
    dmn = _bmm_nt_sum("xattn_dmn", dkv8, wkv)
    _, dgmem = _norm_bwd(mem, dmn, gmem, None, BF16)
    dx, dgpre = _norm_bwd(x, dh, gpre, dxo, F32)
    return dx, dgpre, dgmem, dgpost, dwq, dwkv, dwo


def _log_sigmoid(z):
    return jnp.minimum(z, 0.0) - jnp.log1p(jnp.exp(-jnp.abs(z)))


def _lane_scan_steps():
    return (1, 2, 4, 8, 16, 32, 64)


def _fox_cum(frow, bfb):
    s = frow.shape[1]

    def kern(f_ref, b_ref, o_ref):
        lane = lax.broadcasted_iota(jnp.int32, (FOX_H, LANE), 1)
        carry = jnp.zeros((FOX_H, 1), F32)
        for c in range(s // LANE):
            sl = slice(c * LANE, (c + 1) * LANE)
            lf = _log_sigmoid(f_ref[:, sl] + b_ref[...])
            v = lf
            for d in _lane_scan_steps():
                v = v + jnp.where(lane >= d, pltpu.roll(v, d, 1), 0.0)
            o_ref[:, sl] = v + carry
            carry = carry + jnp.sum(lf, axis=1, keepdims=True)

    return pl.pallas_call(kern, name="fox_cum", out_shape=jax.ShapeDtypeStruct((FOX_H, s), F32),
                          compiler_params=pltpu.CompilerParams(vmem_limit_bytes=VMEM_LIMIT))(frow, bfb)


def _fox_dlogf(dcq, dck, frow, bfb):
    s = frow.shape[1]

    def kern(q_ref, d_ref, f_ref, b_ref, df_ref, db_ref):
        lane = lax.broadcasted_iota(jnp.int32, (FOX_H, LANE), 1)
        carry = jnp.zeros((FOX_H, 1), F32)
        dbf = jnp.zeros((FOX_H, 1), F32)
        for c in reversed(range(s // LANE)):
            sl = slice(c * LANE, (c + 1) * LANE)
            dc = q_ref[:, sl] - d_ref[:, sl]
            v = dc
            for d in _lane_scan_steps():
                v = v + jnp.where(lane < LANE - d, pltpu.roll(v, LANE - d, 1), 0.0)
            v = v + carry
            carry = carry + jnp.sum(dc, axis=1, keepdims=True)
            df = v * jax.nn.sigmoid(-(f_ref[:, sl] + b_ref[...]))
            df_ref[:, sl] = df
            dbf = dbf + jnp.sum(df, axis=1, keepdims=True)
        db_ref[...] = jnp.broadcast_to(dbf, (FOX_H, LANE))

    return pl.pallas_call(kern, name="fox_dlogf",
                          out_shape=[jax.ShapeDtypeStruct((FOX_H, s), F32), jax.ShapeDtypeStruct((FOX_H, LANE), F32)],
                          compiler_params=pltpu.CompilerParams(vmem_limit_bytes=VMEM_LIMIT))(dcq, dck, frow, bfb)


FOX_TQ = 512
Q_COL, K_COL, V_COL = 0, FOX_W // LANE, 2 * FOX_W // LANE
B_COL, C_COL, U_COL = 12, 16, 20


def _fox_logits(qm, kb, cc, cr, causal, scale, reps):
    sc = lax.dot_general(qm, kb, (NT, ((), ())), preferred_element_type=F32) * scale
    sc = sc + jnp.tile(cc, (1, reps)) - cr
    return jnp.where(causal, sc, NEG)


def _fox_fwd_call(proj, cumc, cumr):
    s = proj.shape[0]
    tq = _tile(s, FOX_TQ)
    nq = s // tq
    reps = tq // LANE
    scale = FOX_HD ** -0.5

    def kern(q_ref, k_ref, v_ref, cc_ref, cr_ref, o_ref, lse_ref, m_s, l_s, acc_s):
        i = pl.program_id(1)
        j = pl.program_id(2)
        lane = lax.broadcasted_iota(jnp.int32, (tq, LANE), 1)

        @pl.when(j == 0)
        def _():
            m_s[...] = jnp.full(m_s.shape, NEG, F32)
            l_s[...] = jnp.zeros(l_s.shape, F32)
            acc_s[...] = jnp.zeros(acc_s.shape, F32)

        @pl.when(j <= i)
        def _():
            qv = q_ref[...]
            kb = k_ref[...].astype(BF16)
            vb = v_ref[...].astype(BF16)
            causal = (i * tq + lax.broadcasted_iota(jnp.int32, (tq, tq), 0)
                      >= j * tq + lax.broadcasted_iota(jnp.int32, (tq, tq), 1))
            for hh in range(2):
                sel = (lane < FOX_HD) if hh == 0 else (lane >= FOX_HD)
                qm = jnp.where(sel, qv, 0.0).astype(BF16)
                sc = _fox_logits(qm, kb, cc_ref[hh], cr_ref[hh:hh + 1, :], causal, scale, reps)
                m_prev = m_s[hh]
                m_new = jnp.maximum(m_prev, jnp.max(sc, axis=-1, keepdims=True))
                alpha = jnp.exp(m_prev - m_new)
                p = jnp.exp(sc - m_new)
                l_s[hh] = alpha * l_s[hh] + jnp.sum(p, axis=-1, keepdims=True)
                acc_s[hh] = alpha * acc_s[hh] + jnp.dot(p.astype(BF16), vb, preferred_element_type=F32)
                m_s[hh] = m_new

        @pl.when(j == i)
        def _():
            o_ref[...] = jnp.where(lane < FOX_HD, acc_s[0] / l_s[0], acc_s[1] / l_s[1])
            for hh in range(2):
                lse_ref[hh] = jnp.broadcast_to(m_s[hh] + jnp.log(l_s[hh]), (tq, LANE))

    kvi = lambda hp, i, j: jnp.minimum(j, i)
    return pl.pallas_call(
        kern, name="fox_fwd", grid=(4, nq, nq),
        in_specs=[BS((tq, LANE), lambda hp, i, j: (i, Q_COL + hp)),
                  BS((tq, LANE), lambda hp, i, j: (kvi(hp, i, j), K_COL + hp)),
                  BS((tq, LANE), lambda hp, i, j: (kvi(hp, i, j), V_COL + hp)),
                  BS((2, tq, LANE), lambda hp, i, j: (hp, i, 0)),
                  BS((None, 2, tq), lambda hp, i, j: (hp, 0, kvi(hp, i, j)))],
        out_specs=[BS((tq, LANE), lambda hp, i, j: (i, hp)), BS((2, tq, LANE), lambda hp, i, j: (hp, i, 0))],
        out_shape=[jax.ShapeDtypeStruct((s, FOX_W), F32), jax.ShapeDtypeStruct((FOX_H, s, LANE), F32)],
        scratch_shapes=[pltpu.VMEM((2, tq, 1), F32), pltpu.VMEM((2, tq, 1), F32), pltpu.VMEM((2, tq, LANE), F32)],
        compiler_params=_params(3))(proj, proj, proj, cumc, cumr)


ROWSUM_M = 16


def _fox_bwd_call(proj, o, lse, dcat, cumc, cumr):
    s = proj.shape[0]
    tq = _tile(s, FOX_TQ)
    nq = s // tq
    reps = tq // LANE
    scale = FOX_HD ** -0.5

    def kern(q_ref, k_ref, v_ref, do_ref, o_ref, lse_ref, cc_ref, cr_ref, dq_ref, dk_ref, dv_ref, dck_ref, dcq_ref):
        j = pl.program_id(1)
        i = pl.program_id(2)
        lane = lax.broadcasted_iota(jnp.int32, (tq, LANE), 1)
        ones = jnp.ones((ROWSUM_M, tq), BF16)

        @pl.when((j == 0) & (i == 0))
        def _():
            dq_ref[...] = jnp.zeros(dq_ref.shape, F32)
            dcq_ref[...] = jnp.zeros(dcq_ref.shape, F32)

        @pl.when(i == j)
        def _():
            dk_ref[...] = jnp.zeros(dk_ref.shape, F32)
            dv_ref[...] = jnp.zeros(dv_ref.shape, F32)
            dck_ref[...] = jnp.zeros(dck_ref.shape, F32)

        @pl.when(i >= j)
        def _():
            qv = q_ref[...]
            dov = do_ref[...]
            ov = o_ref[...]
            kb = k_ref[...].astype(BF16)
            vb = v_ref[...].astype(BF16)
            causal = (i * tq + lax.broadcasted_iota(jnp.int32, (tq, tq), 0)
                      >= j * tq + lax.broadcasted_iota(jnp.int32, (tq, tq), 1))
            dq_t = jnp.zeros((tq, LANE), F32)
            dk_t = jnp.zeros((tq, LANE), F32)
            dv_t = jnp.zeros((tq, LANE), F32)
            for hh in range(2):
                sel = (lane < FOX_HD) if hh == 0 else (lane >= FOX_HD)
                qm = jnp.where(sel, qv, 0.0).astype(BF16)
                dom32 = jnp.where(sel, dov, 0.0)
                dom = dom32.astype(BF16)
                sc = _fox_logits(qm, kb, cc_ref[hh], cr_ref[hh:hh + 1, :], causal, scale, reps)
                p = jnp.exp(sc - jnp.tile(lse_ref[hh], (1, reps)))
                dp = lax.dot_general(dom, vb, (NT, ((), ())), preferred_element_type=F32)
                delta = jnp.sum(dom32 * ov, axis=-1, keepdims=True)
                ds = p * (dp - delta)
                dsb = ds.astype(BF16)
                dq_t = jnp.where(sel, jnp.dot(dsb, kb, preferred_element_type=F32) * scale, dq_t)
                dk_t = dk_t + lax.dot_general(dsb, qm, (TN, ((), ())), preferred_element_type=F32) * scale
                dv_t = dv_t + lax.dot_general(p.astype(BF16), dom, (TN, ((), ())), preferred_element_type=F32)
                dck_ref[hh] += jnp.sum(ds, axis=0, keepdims=True)
                ds_lo = (ds - dsb.astype(F32)).astype(BF16)
                dcq_ref[hh, i] += (lax.dot_general(ones, dsb, (NT, ((), ())), preferred_element_type=F32)
                                   + lax.dot_general(ones, ds_lo, (NT, ((), ())), preferred_element_type=F32))
            rows =pl.ds(pl.multiple_of(i * tq, tq), tq)
            dq_ref[rows, :] += dq_t
            dk_ref[...] += dk_t
            dv_ref[...] += dv_t

    qi = lambda hp, j, i: jnp.maximum(i, j)
    return pl.pallas_call(
        kern, name="fox_bwd", grid=(4, nq, nq),
        in_specs=[BS((tq, LANE), lambda hp, j, i: (qi(hp, j, i), Q_COL + hp)),
                  BS((tq, LANE), lambda hp, j, i: (j, K_COL + hp)),
                  BS((tq, LANE), lambda hp, j, i: (j, V_COL + hp)),
                  BS((tq, LANE), lambda hp, j, i: (qi(hp, j, i), hp)),
                  BS((tq, LANE), lambda hp, j, i: (qi(hp, j, i), hp)),
                  BS((2, tq, LANE), lambda hp, j, i: (hp, qi(hp, j, i), 0)),
                  BS((2, tq, LANE), lambda hp, j, i: (hp, qi(hp, j, i), 0)),
                  BS((None, 2, tq), lambda hp, j, i: (hp, 0, j))],
        out_specs=[BS((s, LANE), lambda hp, j, i: (0, hp)), BS((tq, LANE), lambda hp, j, i: (j, hp)),
                   BS((tq, LANE), lambda hp, j, i: (j, hp)), BS((2, 1, tq), lambda hp, j, i: (hp, 0, j)),
                   BS((2, nq, ROWSUM_M, tq), lambda hp, j, i: (hp, 0, 0, 0))],
        out_shape=[jax.ShapeDtypeStruct((s, FOX_W), F32), jax.ShapeDtypeStruct((s, FOX_W), F32),
                   jax.ShapeDtypeStruct((s, FOX_W), F32), jax.ShapeDtypeStruct((FOX_H, 1, s), F32),
                   jax.ShapeDtypeStruct((FOX_H, nq, ROWSUM_M, tq), F32)],
        compiler_params=_params(3))(proj, proj, proj, dcat, o, lse, cumc, cumr)


def _shift_down(v, d, row):
    return jnp.where(row >= d, pltpu.roll(v, d, 0), 0.0)


def _shift_up(v, d, row, n):
    return jnp.where(row < n - d, pltpu.roll(v, n - d, 0), 0.0)


def _sconv_fwd(proj, convw):
    s = proj.shape[0]

    def kern(b_ref, c_ref, u_ref, w_ref, y_ref):
        row = lax.broadcasted_iota(jnp.int32, (s, LANE), 0)
        z = c_ref[...] * u_ref[...]
        conv = w_ref[2:3, :] * z + w_ref[1:2, :] * _shift_down(z, 1, row) + w_ref[0:1, :] * _shift_down(z, 2, row)
        y_ref[...] = (b_ref[...] * conv).astype(BF16)

    col = lambda base: BS((s, LANE), lambda cb: (0, base + cb))
    return pl.pallas_call(kern, name="sconv_fwd", grid=(SC_W // LANE,),
                          in_specs=[col(B_COL), col(C_COL), col(U_COL), BS((SC_K, LANE), lambda cb: (0, cb))],
                          out_specs=BS((s, LANE), lambda cb: (0, cb)),
                          out_shape=jax.ShapeDtypeStruct((s, SC_W), BF16), compiler_params=_params(1))(proj, proj, proj, convw)


def _sconv_bwd(proj, convw, dcat):
    s = proj.shape[0]

    def kern(b_ref, c_ref, u_ref, w_ref, dy_ref, db_ref, dc_ref, du_ref, dw_ref):
        row = lax.broadcasted_iota(jnp.int32, (s, LANE), 0)
        cv, uv, dyv = c_ref[...], u_ref[...], dy_ref[...]
        z = cv * uv
        z1 = _shift_down(z, 1, row)
        z2 = _shift_down(z, 2, row)
        conv = w_ref[2:3, :] * z + w_ref[1:2, :] * z1 + w_ref[0:1, :] * z2
        db_ref[...] = dyv * conv
        dcv = dyv * b_ref[...]
        dz = w_ref[2:3, :] * dcv + w_ref[1:2, :] * _shift_up(dcv, 1, row, s) + w_ref[0:1, :] * _shift_up(dcv, 2, row, s)
        dc_ref[...] = dz * uv
        du_ref[...] = dz * cv
        dw_ref[0:1, :] = jnp.sum(dcv * z2, axis=0, keepdims=True)
        dw_ref[1:2, :] = jnp.sum(dcv * z1, axis=0, keepdims=True)
        dw_ref[2:3, :] = jnp.sum(dcv * z, axis=0, keepdims=True)

    col = lambda base: BS((s, LANE), lambda cb: (0, base + cb))
    out = BS((s, LANE), lambda cb: (0, cb))
    wspec = BS((SC_K, LANE), lambda cb: (0, cb))
    act = jax.ShapeDtypeStruct((s, SC_W), F32)
    return pl.pallas_call(kern, name="sconv_bwd", grid=(SC_W // LANE,),
                          in_specs=[col(B_COL), col(C_COL), col(U_COL), wspec, col(FOX_W // LANE)],
                          out_specs=[out, out, out, wspec],
                          out_shape=[act, act, act, jax.ShapeDtypeStruct((SC_K, SC_W), F32)],
                          compiler_params=_params(1))(proj, proj, proj, convw, dcat)


def _fox_layer_fwd(x, gpre, gpost, wall, bfb, convw, wout, after=None):
    s = x.shape[0]
    h = _norm_fwd(x, gpre, after)
    proj = _mm_nn("fox_proj", h, wall, tn=AB_PAD // 5)
    frow = proj[:, 3 * FOX_W + 3 * SC_W:3 * FOX_W + 3 * SC_W + FOX_H].T
    cumr = _fox_cum(frow, bfb)
    cumc = jnp.broadcast_to(cumr[:, :, None], (FOX_H, s, LANE))
    cumr4 = cumr.reshape(4, 2, s)
    o, lse = _fox_fwd_call(proj, cumc, cumr4)
    yb = _sconv_fwd(proj, convw)
    cat = jnp.concatenate([o.astype(BF16), yb], axis=1)
    y = _mm_nn("fox_out", cat, wout)
    return _norm_res(x, y, gpost), (x, h, proj, frow, cumc, cumr4, o, lse, cat, y)


def _fox_layer_bwd(dxo, saved, gpre, gpost, wall, bfb, convw, wout):
    x, h, proj, frow, cumc, cumr4, o, lse, cat, y = saved
    s = x.shape[0]
    dy, dgpost = _norm_bwd(y, dxo, gpost, None, BF16)
    dcat = _mm_nt("fox_dcat", dy, wout)
    dwout = _mm_tn("fox_dwout", cat, dy)
    db, dc, du, dconvw = _sconv_bwd(proj, convw, dcat)
    dq, dk, dv, dck, dcq = _fox_bwd_call(proj, o, lse, dcat, cumc, cumr4)
    dfrow, dbf = _fox_dlogf(dcq[:, :, 0, :].reshape(FOX_H, s), dck.reshape(FOX_H, s), frow, bfb)
    dfcol = jnp.pad(dfrow.T, ((0, 0), (0, LANE - FOX_H)))
    dproj = jnp.concatenate([dq, dk, dv, db, dc, du, dfcol], axis=1).astype(BF16)
    dwall = _mm_tn("fox_dwall", h, dproj, tn=AB_PAD // 5)
    dh = _mm_nt("fox_dh", dproj, wall, tn=AB_PAD // 5)
    dx, dgpre = _norm_bwd(x, dh, gpre, dxo, F32)
    return dx, dgpre, dgpost, dwall, dbf[:, 0], dconvw, dwout


def _ab_pack(w):
    nf = 3 * FOX_W
    return jnp.concatenate([w[:, :nf], w[:, nf + FOX_H:], w[:, nf:nf + FOX_H],
                            jnp.zeros((w.shape[0], AB_PAD - AB_IN), w.dtype)], axis=1)


def _ab_unpack(w):
    nf = 3 * FOX_W
    nbcu = 3 * SC_W
    return jnp.concatenate([w[:, :nf], w[:, nf + nbcu:nf + nbcu + FOX_H], w[:, nf:nf + nbcu]], axis=1)


NCH = DM // LANE
CH_PER_BLK = LRU_BW // LANE


def _chunk_spec(s, lead=0):
    return BS((None, s, LANE), lambda ch: (lead + ch // CH_PER_BLK, 0, ch % CH_PER_BLK))


def _vec_chunk(rows):
    return BS((rows, LANE), lambda ch: (0, ch))


def _neg_expm1(x):
    series = -x * (1.0 + x * (1 / 2) * (1.0 + x * (1 / 3) * (1.0 + x * (1 / 4) * (1.0 + x * (1 / 5) * (
        1.0 + x * (1 / 6) * (1.0 + x * (1 / 7)))))))
    return jnp.where(x > -0.25, series, 1.0 - jnp.exp(x))


def _softplus(z):
    return jnp.maximum(z, 0.0) + jnp.log1p(jnp.exp(-jnp.abs(z)))


GELU_C = math.sqrt(2.0 / math.pi)
GELU_A = 0.044715


def _gelu(x):
    return 0.5 * x * (1.0 + jnp.tanh(GELU_C * (x + GELU_A * x * x * x)))


def _gelu_grad(x):
    t = jnp.tanh(GELU_C * (x + GELU_A * x * x * x))
    return 0.5 * (1.0 + t) + 0.5 * x * (1.0 - t * t) * GELU_C * (1.0 + 3.0 * GELU_A * x * x)


def _lru_conv_fwd(gu, convw, convb):
    s = gu.shape[1]

    def kern(x_ref, w_ref, b_ref, u_ref):
        row = lax.broadcasted_iota(jnp.int32, (s, LANE), 0)
        xv = x_ref[...]
        u_ref[...] = (b_ref[...] + w_ref[3:4, :] * xv + w_ref[2:3, :] * _shift_down(xv, 1, row)
                      + w_ref[1:2, :] * _shift_down(xv, 2, row) + w_ref[0:1, :] * _shift_down(xv, 3, row))

    return pl.pallas_call(kern, name="lru_conv_fwd", grid=(NCH,),
                          in_specs=[_chunk_spec(s, LRU_NB), _vec_chunk(RG_K), _vec_chunk(1)], out_specs=_chunk_spec(s),
                          out_shape=jax.ShapeDtypeStruct((LRU_NB, s, LRU_BW), F32), compiler_params=_params(1))(gu, convw, convb)


def _lru_conv_bwd(dud, dug, gu, convw):
    s = gu.shape[1]

    def kern(d1_ref, d2_ref, x_ref, w_ref, dx_ref, dw_ref, db_ref):
        row = lax.broadcasted_iota(jnp.int32, (s, LANE), 0)
        du = d1_ref[...] + d2_ref[...]
        xv = x_ref[...]
        dx_ref[...] = (w_ref[3:4, :] * du + w_ref[2:3, :] * _shift_up(du, 1, row, s) + w_ref[1:2, :] * _shift_up(du, 2, row, s)
                       + w_ref[0:1, :] * _shift_up(du, 3, row, s)).astype(BF16)
        dw_ref[3:4, :] = jnp.sum(du * xv, axis=0, keepdims=True)
        for k in range(1, RG_K):
            dw_ref[3 - k:4 - k, :] = jnp.sum(du * _shift_down(xv, k, row), axis=0, keepdims=True)
        db_ref[...] = jnp.sum(du, axis=0, keepdims=True)

    return pl.pallas_call(kern, name="lru_conv_bwd", grid=(NCH,),
                          in_specs=[_chunk_spec(s), _chunk_spec(s), _chunk_spec(s, LRU_NB), _vec_chunk(RG_K)],
                          out_specs=[_chunk_spec(s), _vec_chunk(RG_K), _vec_chunk(1)],
                          out_shape=[jax.ShapeDtypeStruct((LRU_NB, s, LRU_BW), BF16),
                                     jax.ShapeDtypeStruct((RG_K, DM), F32), jax.ShapeDtypeStruct((1, DM), F32)],
                          compiler_params=_params(1))(dud, dug, gu, convw)


def _lru_gates(z_ref, bai_ref, lam_ref, uv):
    r = jax.nn.sigmoid(z_ref[0] + bai_ref[0:1, :])
    ig = jax.nn.sigmoid(z_ref[1] + bai_ref[1:2, :])
    sp = _softplus(-lam_ref[...])
    la = -RG_C * r * sp
    a = jnp.exp(la)
    sq = jnp.sqrt(_neg_expm1(2.0 * la))
    return r, ig, sp, a, sq


def _scan_steps(n):
    d, out = 1, []
    while d < n:
        out.append(d)
        d *= 2
    return out


def _lru_scan_fwd(z, bai, lam, u, gu):
    s = u.shape[1]
    zspec = BS((2, None, s, LANE), lambda ch: (0, ch // CH_PER_BLK, 0, ch % CH_PER_BLK))

    def kern(z_ref, bai_ref, lam_ref, u_ref, g_ref, hs_ref, y_ref):
        row = lax.broadcasted_iota(jnp.int32, (s, LANE), 0)
        uv = u_ref[...]
        _, ig, _, a, sq = _lru_gates(z_ref, bai_ref, lam_ref, uv)
        b = sq * (ig * uv)
        for d in _scan_steps(s):
            a_sh = jnp.where(row >= d, pltpu.roll(a, d, 0), 1.0)
            b = a * _shift_down(b, d, row) + b
            a = a * a_sh
        hs_ref[...] = b
        y_ref[...] = (_gelu(g_ref[...]) * b).astype(BF16)

    return pl.pallas_call(kern, name="lru_scan_fwd", grid=(NCH,),
                          in_specs=[zspec, _vec_chunk(2), _vec_chunk(1), _chunk_spec(s), _chunk_spec(s)],
                          out_specs=[_chunk_spec(s), BS((s, LANE), lambda ch: (0, ch))],
                          out_shape=[jax.ShapeDtypeStruct((LRU_NB, s, LRU_BW), F32), jax.ShapeDtypeStruct((s, DM), BF16)],
                          compiler_params=_params(1))(z, bai, lam, u, gu)


def _lru_scan_bwd(dyp, z, bai, lam, u, gu, hs):
    s = u.shape[1]
    zspec = BS((2, None, s, LANE), lambda ch: (0, ch // CH_PER_BLK, 0, ch % CH_PER_BLK))

    def kern(dy_ref, z_ref, bai_ref, lam_ref, u_ref, g_ref, hs_ref, dg_ref, dz_ref, du_ref, dbai_ref, dlam_ref):
        row = lax.broadcasted_iota(jnp.int32, (s, LANE), 0)
        uv, gv, hv, dyv = u_ref[...], g_ref[...], hs_ref[...], dy_ref[...]
        r, ig, sp, a, sq = _lru_gates(z_ref, bai_ref, lam_ref, uv)
        dg_ref[...] = (dyv * hv * _gelu_grad(gv)).astype(BF16)
        g = dyv * _gelu(gv)
        an = _shift_up(a, 1, row, s)
        for d in _scan_steps(s):
            an_sh = jnp.where(row < s - d, pltpu.roll(an, s - d, 0), 1.0)
            g = an * _shift_up(g, d, row, s) + g
            an = an * an_sh
        da = g * _shift_down(hv, 1, row)
        dsq = g * (ig * uv)
        di = g * sq * uv
        du_ref[...] = g * sq * ig
        dla = da * a - dsq * (a * a / sq)
        dzr = dla * (-RG_C * sp) * r * (1.0 - r)
        dzi = di * ig * (1.0 - ig)
        dz_ref[0] = dzr.astype(BF16)
        dz_ref[1] = dzi.astype(BF16)
        dbai_ref[0:1, :] = jnp.sum(dzr, axis=0, keepdims=True)
        dbai_ref[1:2, :] = jnp.sum(dzi, axis=0, keepdims=True)
        dlam_ref[...] = jnp.sum(dla * r, axis=0, keepdims=True) * (RG_C * jax.nn.sigmoid(-lam_ref[...]))

    return pl.pallas_call(
        kern, name="lru_scan_bwd", grid=(NCH,),
        in_specs=[BS((s, LANE), lambda ch: (0, ch)), zspec, _vec_chunk(2), _vec_chunk(1), _chunk_spec(s), _chunk_spec(s),
                  _chunk_spec(s)],
        out_specs=[_chunk_spec(s), zspec, _chunk_spec(s), _vec_chunk(2), _vec_chunk(1)],
        out_shape=[jax.ShapeDtypeStruct((LRU_NB, s, LRU_BW), BF16), jax.ShapeDtypeStruct((2, LRU_NB, s, LRU_BW), BF16),
                   jax.ShapeDtypeStruct((LRU_NB, s, LRU_BW), F32), jax.ShapeDtypeStruct((2, DM), F32),
                   jax.ShapeDtypeStruct((1, DM), F32)],
        compiler_params=_params(1))(dyp, z, bai, lam, u, gu, hs)


def _lru_layer_fwd(x, gpre, gpost, win, convw, convb, wai, bai, lam, wout, after=None):
    s = x.shape[0]
    tm = _tile(s, 512)
    h = _norm_fwd(x, gpre, after)
    gu = _bmm_nn("lru_in", h, win)
    u = _lru_conv_fwd(gu, convw, convb)
    z = _mm("lru_gate", u, wai, grid=(2, LRU_NB, s // tm, 1),
            a_spec=BS((None, tm, LRU_BW), lambda k, n, i, r: (n, i, 0)),
            b_spec=BS((None, None, LRU_BW, LRU_BW), lambda k, n, i, r: (k, n, 0, 0)),
            o_spec=BS((None, None, tm, LRU_BW), lambda k, n, i, r: (k, n, i, 0)),
            out_shape=(2, LRU_NB, s, LRU_BW), dn=NN)
    hs, yp = _lru_scan_fwd(z, bai, lam, u, gu)
    y = _mm_nn("lru_out", yp, wout)
    return _norm_res(x, y, gpost), (x, h, gu, u, z, hs, yp, y)


def _lru_layer_bwd(dxo, saved, gpre, gpost, win, convw, convb, wai, bai, lam, wout):
    x, h, gu, u, z, hs, yp, y = saved
    s = x.shape[0]
    tm = _tile(s, 512)
    dy, dgpost = _norm_bwd(y, dxo, gpost, None, BF16)
    dyp = _mm_nt("lru_dyp", dy, wout)
    dwout = _mm_tn("lru_dwout", yp, dy)
    dgate, dz, dud, dbai, dlam = _lru_scan_bwd(dyp, z, bai, lam, u, gu, hs)
    dwai = _mm("lru_dwai", u, dz, grid=(2, LRU_NB, s // tm),
               a_spec=BS((None, tm, LRU_BW), lambda k, n, r: (n, r, 0)),
               b_spec=BS((None, None, tm, LRU_BW), lambda k, n, r: (k, n, r, 0)),
               o_spec=BS((None, None, LRU_BW, LRU_BW), lambda k, n, r: (k, n, 0, 0)),
               out_shape=(2, LRU_NB, LRU_BW, LRU_BW), dn=TN)
    dug = _mm("lru_dug", dz, wai, grid=(LRU_NB, s // tm, 2),
              a_spec=BS((None, None, tm, LRU_BW), lambda n, i, k: (k, n, i, 0)),
              b_spec=BS((None, None, LRU_BW, LRU_BW), lambda n, i, k: (k, n, 0, 0)),
              o_spec=BS((None, tm, LRU_BW), lambda n, i, k: (n, i, 0)),
              out_shape=(LRU_NB, s, LRU_BW), dn=NT)
    duraw, dconvw, dconvb = _lru_conv_bwd(dud, dug, gu, convw)
    dgu = jnp.concatenate([dgate, duraw], axis=0)
    dwin = _bmm_tn("lru_dwin", h, dgu)
    dh = _bmm_nt_sum("lru_dh", dgu, win)
    dx, dgpre = _norm_bwd(x, dh, gpre, dxo, F32)
    return dx, dgpre, dgpost, dwin, dconvw, dconvb, dwai, dbai, dlam, dwout


CHIP_FLIPS = ((1, 0), (0, 1), (1, 1))


def _place():
    return lax.axis_index("x"), lax.axis_index("y"), lax.axis_index("c")


def _flip(v, f):
    return 1 - v if f else v


def _comm_params():
    return pltpu.CompilerParams(vmem_limit_bytes=VMEM_LIMIT)


def _all_gather(shards):
    n = len(shards)

    def body(*refs):
        ins, outs, stage = refs[:n], refs[n:2 * n], refs[2 * n:3 * n]
        send_sems, recv_sems, local_sems = refs[3 * n:]
        x, y, c = _place()
        me, sibling = (x, y, c), (x, y, 1 - c)
        chips = [(_flip(x, fx), _flip(y, fy)) for fx, fy in CHIP_FLIPS]

        def slot(t, p):
            return outs[t].at[:, 4 * p[0] + 2 * p[1] + p[2]]

        def copy(t, k, block, to, src=None):
            return pltpu.make_async_remote_copy(
                src_ref=slot(t, block) if src is None else src, dst_ref=slot(t, block),
                send_sem=send_sems.at[7 * t + k], recv_sem=recv_sems.at[7 * t + k], device_id=to, device_id_type=MESH)

        first = []
        for t in range(n):
            first.append(copy(t, 0, me, sibling, src=ins[t]))
            first += [copy(t, 1 + j, me, (*chip, c), src=ins[t]) for j, chip in enumerate(chips)]
        for cp in first:
            cp.start()
        load = [pltpu.make_async_copy(ins[t], stage[t], local_sems.at[t]) for t in range(n)]
        mine = [pltpu.make_async_copy(stage[t], slot(t, me), local_sems.at[t]) for t in range(n)]
        for cp in load:
            cp.start()
        for t in range(n):
            load[t].wait()
            mine[t].start()
        passed = []
        for j, chip in enumerate(chips):
            for t in range(n):
                copy(t, 1 + j, (*chip, c), me).wait_recv()
                fwd = copy(t, 4 + j, (*chip, c), sibling)
                fwd.start()
                passed.append(fwd)
        for t in range(n):
            copy(t, 0, sibling, me).wait_recv()
            for j, chip in enumerate(chips):
                copy(t, 4 + j, (*chip, 1 - c), me).wait_recv()
        for cp in first + passed:
            cp.wait_send()
        for cp in mine:
            cp.wait()

    outs = [jax.ShapeDtypeStruct((s.shape[0], NDEV) + s.shape[1:], s.dtype) for s in shards]
    return pl.pallas_call(body, name="all_gather", in_specs=[ANY] * n, out_specs=[ANY] * n, out_shape=outs,
                          scratch_shapes=[pltpu.VMEM(s.shape, s.dtype) for s in shards]
                          + [pltpu.SemaphoreType.DMA((7 * n,)), pltpu.SemaphoreType.DMA((7 * n,)),
                             pltpu.SemaphoreType.DMA((n,))],
                          compiler_params=_comm_params())(*shards)


def _small_gather(v):
    def body(v_ref, o_ref, send_sems, recv_sems, local_sem):
        x, y, c = _place()
        mine = 4 * x + 2 * y + c
        local = pltpu.make_async_copy(v_ref, o_ref.at[mine], local_sem)
        local.start()
        sends = []
        for k in range(1, NDEV):
            fx, fy, fc = (k >> 2) & 1, (k >> 1) & 1, k & 1
            sends.append(pltpu.make_async_remote_copy(
                src_ref=v_ref, dst_ref=o_ref.at[mine], send_sem=send_sems.at[k - 1], recv_sem=recv_sems.at[k - 1],
                device_id=(_flip(x, fx), _flip(y, fy), _flip(c, fc)), device_id_type=MESH))
        for cp in sends:
            cp.start()
        for k in range(1, NDEV):
            fx, fy, fc = (k >> 2) & 1, (k >> 1) & 1, k & 1
            src = 4 * _flip(x, fx) + 2 * _flip(y, fy) + _flip(c, fc)
            pltpu.make_async_remote_copy(src_ref=v_ref, dst_ref=o_ref.at[src], send_sem=send_sems.at[k - 1],
                                         recv_sem=recv_sems.at[k - 1], device_id=(x, y, c), device_id_type=MESH).wait_recv()
        for cp in sends:
            cp.wait_send()
        local.wait()

    return pl.pallas_call(body, name="small_gather", in_specs=[ANY], out_specs=ANY,
                          out_shape=jax.ShapeDtypeStruct((NDEV,) + v.shape, v.dtype),
                          scratch_shapes=[pltpu.SemaphoreType.DMA((NDEV - 1,)), pltpu.SemaphoreType.DMA((NDEV - 1,)),
                                          pltpu.SemaphoreType.DMA],
                          compiler_params=_comm_params())(v)


REL_CHIPS = ((0, 0),) + CHIP_FLIPS


def _rs_d2d(g5s):
    n = len(g5s)

    def body(*refs):
        ins, gots = refs[:n], refs[n:2 * n]
        send_sems, recv_sems = refs[2 * n:]
        x, y, c = _place()
        copies = []
        for t in range(n):
            for f, (fx, fy) in enumerate(REL_CHIPS):
                copies.append(pltpu.make_async_remote_copy(
                    src_ref=ins[t].at[_flip(x, fx), _flip(y, fy), 1 - c], dst_ref=gots[t].at[f],
                    send_sem=send_sems.at[4 * t + f], recv_sem=recv_sems.at[4 * t + f], device_id=(x, y, 1 - c),
                    device_id_type=MESH))
        for cp in copies:
            cp.start()
        for cp in copies:
            cp.wait()

    out = [jax.ShapeDtypeStruct((4,) + g.shape[3:], F32) for g in g5s]
    return pl.pallas_call(body, name="rs_d2d", in_specs=[ANY] * n, out_specs=[ANY] * n, out_shape=out,
                          scratch_shapes=[pltpu.SemaphoreType.DMA((4 * n,)), pltpu.SemaphoreType.DMA((4 * n,))],
                          compiler_params=_comm_params())(*g5s)


def _rs_ici(parts):
    n = len(parts)

    def body(*refs):
        ins, outs = refs[:n], refs[n:2 * n]
        send_sems, recv_sems = refs[2 * n:]
        x, y, c = _place()
        copies = []
        for t in range(n):
            for f, (fx, fy) in enumerate(CHIP_FLIPS):
                copies.append(pltpu.make_async_remote_copy(
                    src_ref=ins[t].at[f], dst_ref=outs[t].at[f], send_sem=send_sems.at[3 * t + f],
                    recv_sem=recv_sems.at[3 * t + f], device_id=(_flip(x, fx), _flip(y, fy), c), device_id_type=MESH))
        for cp in copies:
            cp.start()
        for cp in copies:
            cp.wait()

    out = [jax.ShapeDtypeStruct(p.shape, p.dtype) for p in parts]
    return pl.pallas_call(body, name="rs_ici", in_specs=[ANY] * n, out_specs=[ANY] * n, out_shape=out,
                          scratch_shapes=[pltpu.SemaphoreType.DMA((3 * n,)), pltpu.SemaphoreType.DMA((3 * n,))],
                          compiler_params=_comm_params())(*parts)


HBM = pl.BlockSpec(memory_space=pltpu.HBM)
SEM = pl.BlockSpec(memory_space=pltpu.SEMAPHORE)
EFFECT = pltpu.SideEffectType.DATAFLOW_SIDE_EFFECTING


def _in_hbm(a):
    return pltpu.with_memory_space_constraint(a, pltpu.HBM)


def _rs_ici_copies(ins, lands, send_sems, recv_sems):
    x, y, c = _place()
    return [pltpu.make_async_remote_copy(
        src_ref=ins[t].at[f], dst_ref=lands[t].at[f], send_sem=send_sems.at[3 * t + f], recv_sem=recv_sems.at[3 * t + f],
        device_id=(_flip(x, fx), _flip(y, fy), c), device_id_type=MESH)
        for t in range(len(ins)) for f, (fx, fy) in enumerate(CHIP_FLIPS)]


def _rs_ici_start(parts, name):
    n = len(parts)

    def body(*refs):
        ins, lands = refs[:n], refs[n:2 * n]
        send_sems, recv_sems = refs[2 * n], refs[2 * n + 1]
        token = refs[-1]
        for cp in _rs_ici_copies(ins, lands, send_sems, recv_sems):
            cp.start()
        token[...] = jnp.zeros(token.shape, token.dtype)

    thru = [pltpu.HBM(p.shape, p.dtype) for p in parts]
    res = pl.pallas_call(
        body, name=name, in_specs=[HBM] * (2 * n),
        out_shape=(pltpu.SemaphoreType.DMA((3 * n,)), pltpu.SemaphoreType.DMA((3 * n,)), *thru, *thru,
                   jax.ShapeDtypeStruct((8, LANE), F32)),
        out_specs=(SEM, SEM, *([HBM] * (2 * n)), pl.BlockSpec(memory_space=pltpu.VMEM)),
        input_output_aliases={i: 2 + i for i in range(2 * n)},
        compiler_params=pltpu.CompilerParams(has_side_effects=EFFECT),
    )(*[_in_hbm(p) for p in parts], *[_in_hbm(lax.empty(p.shape, p.dtype)) for p in parts])
    return res[:-1], res[-1]


def _rs_ici_wait(state, after, name):
    n = (len(state) - 2) // 2

    def body(*refs):
        send_sems, recv_sems = refs[0], refs[1]
        ins, lands = refs[2:2 + n], refs[2 + n:2 + 2 * n]
        for cp in _rs_ici_copies(ins, lands, send_sems, recv_sems):
            cp.wait_send()
            cp.wait_recv()

    thru = [pltpu.HBM(s.shape, s.dtype) for s in state[2:]]
    res = pl.pallas_call(
        body, name=name, in_specs=[SEM, SEM] + [HBM] * (2 * n) + [ANY], out_shape=tuple(thru),
        out_specs=tuple([HBM] * (2 * n)), input_output_aliases={2 + i: i for i in range(2 * n)},
        compiler_params=pltpu.CompilerParams(has_side_effects=EFFECT),
    )(*state, after)
    return list(res[n:])


def _ag_copies(shards, lands, send_sems, recv_sems):
    x, y, c = _place()
    mine = 4 * x + 2 * y + c
    peers = [(x, y, 1 - c)] + [(_flip(x, fx), _flip(y, fy), c) for fx, fy in CHIP_FLIPS]
    return [pltpu.make_async_remote_copy(
        src_ref=shards[t], dst_ref=lands[t].at[mine], send_sem=send_sems.at[4 * t + k], recv_sem=recv_sems.at[4 * t + k],
        device_id=peer, device_id_type=MESH) for t in range(len(shards)) for k, peer in enumerate(peers)]


def _ag_start(shards, after, name):
    n = len(shards)

    def body(*refs):
        ins, lands = refs[:n], refs[n:2 * n]
        send_sems, recv_sems = refs[2 * n + 1], refs[2 * n + 2]
        token = refs[-1]
        for cp in _ag_copies(ins, lands, send_sems, recv_sems):
            cp.start()
        token[...] = jnp.zeros(token.shape, token.dtype)

    thru = [pltpu.HBM(s.shape, s.dtype) for s in shards]
    land = [pltpu.HBM((NDEV,) + s.shape, s.dtype) for s in shards]
    res = pl.pallas_call(
        body, name=name, in_specs=[HBM] * (2 * n) + [ANY],
        out_shape=(pltpu.SemaphoreType.DMA((4 * n,)), pltpu.SemaphoreType.DMA((4 * n,)), *thru, *land,
                   jax.ShapeDtypeStruct((8, LANE), F32)),
        out_specs=(SEM, SEM, *([HBM] * (2 * n)), pl.BlockSpec(memory_space=pltpu.VMEM)),
        input_output_aliases={i: 2 + i for i in range(2 * n)},
        compiler_params=pltpu.CompilerParams(has_side_effects=EFFECT),
    )(*[_in_hbm(s) for s in shards], *[_in_hbm(lax.empty((NDEV,) + s.shape, s.dtype)) for s in shards], after)
    return res[:-1], res[-1]


def _ag_wait(state, after, name):
    n = (len(state) - 2) // 2

    def body(*refs):
        send_sems, recv_sems = refs[0], refs[1]
        ins, lands = refs[2:2 + n], refs[2 + n:2 + 2 * n]
        for cp in _ag_copies(ins, lands, send_sems, recv_sems):
            cp.wait_send()
            cp.wait_recv()

    thru = [pltpu.HBM(s.shape, s.dtype) for s in state[2:]]
    res = pl.pallas_call(
        body, name=name, in_specs=[SEM, SEM] + [HBM] * (2 * n) + [ANY], out_shape=tuple(thru),
        out_specs=tuple([HBM] * (2 * n)), input_output_aliases={2 + i: i for i in range(2 * n)},
        compiler_params=pltpu.CompilerParams(has_side_effects=EFFECT),
    )(*state, after)
    return list(res[:n]), list(res[n:])


def _ag_finish(shards, lands):
    n = len(shards)

    def body(*refs):
        ins, outs, stage = refs[:n], refs[2 * n:3 * n], refs[3 * n:4 * n]
        send_sems, recv_sems, local_sems = refs[4 * n:]
        x, y, c = _place()
        chips = [(_flip(x, fx), _flip(y, fy)) for fx, fy in CHIP_FLIPS]

        def passing(t, j, core, to):
            blk = outs[t].at[4 * chips[j][0] + 2 * chips[j][1] + core]
            return pltpu.make_async_remote_copy(src_ref=blk, dst_ref=blk, send_sem=send_sems.at[3 * t + j],
                                                recv_sem=recv_sems.at[3 * t + j], device_id=to, device_id_type=MESH)

        sends = [passing(t, j, c, (x, y, 1 - c)) for t in range(n) for j in range(3)]
        for cp in sends:
            cp.start()
        load = [pltpu.make_async_copy(ins[t], stage[t], local_sems.at[t]) for t in range(n)]
        mine = [pltpu.make_async_copy(stage[t], outs[t].at[4 * x + 2 * y + c], local_sems.at[t]) for t in range(n)]
        for cp in load:
            cp.start()
        for t in range(n):
            load[t].wait()
            mine[t].start()
        for t in range(n):
            for j in range(3):
                passing(t, j, 1 - c, (x, y, c)).wait_recv()
        for cp in sends:
            cp.wait_send()
        for cp in mine:
            cp.wait()

    return pl.pallas_call(
        body, name="ag_finish", in_specs=[ANY] * (2 * n), out_specs=[ANY] * n,
        out_shape=[jax.ShapeDtypeStruct(l.shape, l.dtype) for l in lands],
        input_output_aliases={n + i: i for i in range(n)},
        scratch_shapes=[pltpu.VMEM(s.shape, s.dtype) for s in shards]
        + [pltpu.SemaphoreType.DMA((3 * n,)), pltpu.SemaphoreType.DMA((3 * n,)), pltpu.SemaphoreType.DMA((n,))],
        compiler_params=_comm_params())(*shards, *lands)


def _row_tile(rows):
    for t in (256, 128, 64, 32, 16, 8):
        if rows % t == 0:
            return t
    return rows


def _rs_chip_sum(pos, g5, got):
    a, b = g5.shape[3:]
    ta = _row_tile(a)

    def kern(pos_ref, o_ref, g_ref, p_ref):
        p_ref[...] = (o_ref[...] + g_ref[...]).astype(BF16)

    def mine(f, i, pos_ref):
        return (pos_ref[0] ^ ((f + 1) & 1), pos_ref[1] ^ ((f + 1) >> 1), pos_ref[2], i, 0)

    spec = pltpu.PrefetchScalarGridSpec(
        num_scalar_prefetch=1, grid=(3, a // ta),
        in_specs=[BS((None, None, None, ta, b), mine), BS((None, ta, b), lambda f, i, pos_ref: (f + 1, i, 0))],
        out_specs=BS((None, ta, b), lambda f, i, pos_ref: (f, i, 0)))
    return pl.pallas_call(kern, name="rs_chip_sum", grid_spec=spec, out_shape=jax.ShapeDtypeStruct((3, a, b), BF16),
                          compiler_params=_params(2))(pos, g5, got)


def _rs_final_sum(pos, g5, got, recv):
    a, b = g5.shape[3:]
    ta = _row_tile(a)

    def kern(pos_ref, o_ref, g_ref, r_ref, s_ref):
        acc = o_ref[...] + g_ref[...]
        for f in range(3):
            acc = acc + r_ref[f].astype(F32)
        s_ref[...] = acc

    spec = pltpu.PrefetchScalarGridSpec(
        num_scalar_prefetch=1, grid=(a // ta,),
        in_specs=[BS((None, None, None, ta, b), lambda i, pos_ref: (pos_ref[0], pos_ref[1], pos_ref[2], i, 0)),
                  BS((None, ta, b), lambda i, pos_ref: (0, i, 0)), BS((3, ta, b), lambda i, pos_ref: (0, i, 0))],
        out_specs=BS((ta, b), lambda i, pos_ref: (i, 0)))
    return pl.pallas_call(kern, name="rs_final_sum", grid_spec=spec, out_shape=jax.ShapeDtypeStruct((a, b), F32),
                          compiler_params=_params(1))(pos, g5, got, recv)


def _reduce_scatter(grads, pos):
    g5s = [g.reshape((2, 2, 2) + g.shape[1:]) for g in grads]
    gots = _rs_d2d(g5s)
    parts = [_rs_chip_sum(pos, g, got) for g, got in zip(g5s, gots)]
    recvs = _rs_ici(parts)
    return [_rs_final_sum(pos, g, got, r) for g, got, r in zip(g5s, gots, recvs)]


def _rs_begin(grads, pos, tag):
    g5s = [g.reshape((2, 2, 2) + g.shape[1:]) for g in grads]
    gots = _rs_d2d(g5s)
    parts = [_rs_chip_sum(pos, g, got) for g, got in zip(g5s, gots)]
    state, token = _rs_ici_start(parts, "rs_ici_start_" + tag)
    return (g5s, gots, state, tag), token


def _rs_end(pending, after, pos):
    g5s, gots, state, tag = pending
    recvs = _rs_ici_wait(state, after, "rs_ici_wait_" + tag)
    return [_rs_final_sum(pos, g, got, r) for g, got, r in zip(g5s, gots, recvs)]


def _sum_devices(v):
    _, r, _ = v.shape

    def kern(v_ref, o_ref):
        acc = v_ref[0]
        for d in range(1, NDEV):
            acc = acc + v_ref[d]
        o_ref[...] = acc

    return pl.pallas_call(kern, name="sum_devices", out_shape=jax.ShapeDtypeStruct((r, LANE), F32),
                          compiler_params=_comm_params())(v)


def _loss_head(xf, target):
    s = xf.shape[0]
    tm = _tile(s, 512)

    def kern(x_ref, t_ref, dx_ref, l_ref):
        err = x_ref[...] - t_ref[...]
        dx_ref[...] = err * (1.0 / DM)
        part = jnp.broadcast_to(0.5 * jnp.sum(jnp.mean(err * err, axis=-1, keepdims=True), axis=0, keepdims=True), (8, LANE))

        @pl.when(pl.program_id(0) == 0)
        def _():
            l_ref[...] = part

        @pl.when(pl.program_id(0) > 0)
        def _():
            l_ref[...] += part

    row = BS((tm, DM), lambda i: (i, 0))
    return pl.pallas_call(kern, name="loss_head", grid=(s // tm,), in_specs=[row, row],
                          out_specs=[row, BS((8, LANE), lambda i: (0, 0))],
                          out_shape=[jax.ShapeDtypeStruct((s, DM), F32), jax.ShapeDtypeStruct((8, LANE), F32)],
                          compiler_params=_params(1))(xf, target)


def _adamw(w, g, m, v):
    rows, cols = w.shape
    tr = _row_tile(rows)

    def kern(w_ref, g_ref, m_ref, v_ref, d_ref, nm_ref, nv_ref):
        gv = g_ref[...]
        nm = ADAM_B1 * m_ref[...] + (1.0 - ADAM_B1) * gv
        nv = ADAM_B2 * v_ref[...] + (1.0 - ADAM_B2) * (gv * gv)
        m_hat = nm / (1.0 - ADAM_B1 ** ADAM_STEP)
        v_hat = nv / (1.0 - ADAM_B2 ** ADAM_STEP)
        d_ref[...] = -ADAM_LR * (m_hat / (jnp.sqrt(v_hat) + ADAM_EPS) + ADAM_WD * w_ref[...])
        nm_ref[...] = nm
        nv_ref[...] = nv

    blk = BS((tr, cols), lambda i: (i, 0))
    shp = jax.ShapeDtypeStruct((rows, cols), F32)
    return pl.pallas_call(kern, name="adamw", grid=(rows // tr,), in_specs=[blk] * 4, out_specs=[blk] * 3,
                          out_shape=[shp] * 3, compiler_params=_params(1))(w, g, m, v)


def _adamw_nd(w, g, m, v):
    shape = w.shape
    two = (math.prod(shape[:-1]), shape[-1])
    return tuple(o.reshape(shape) for o in _adamw(w.reshape(two), g.reshape(two), m.reshape(two), v.reshape(two)))


def _pack_small(parts):
    flat = jnp.concatenate([p.reshape(-1) for p in parts])
    pad = (-flat.shape[0]) % (8 * LANE)
    return jnp.pad(flat, (0, pad)).reshape(-1, LANE)


def _unpack_small(packed, shapes, lead=()):
    flat = packed.reshape(lead + (-1,))
    out, off = [], 0
    for shp in shapes:
        n = math.prod(shp)
        out.append(flat[..., off:off + n].reshape(lead + tuple(shp)))
        off += n
    return out


def _blocks_of_columns(w):
    k, n = w.shape
    return w.reshape(k, NDEV, n // NDEV).transpose(1, 0, 2)


def _columns_of_blocks(wb):
    n, k, c = wb.shape
    return wb.transpose(1, 0, 2).reshape(k, n * c)


WEIGHT_NAMES = ('g_mix_pre', 'g_mix_post', 'g_cross_pre', 'g_mem', 'g_cross_post', 'g_ffn_pre', 'g_ffn_post', 'w_xq',
                'w_xkv', 'w_xo', 'w_ffn_gu', 'w_ffn_down', 'ab_w_in', 'ab_b_f', 'ab_conv_w', 'ab_w_out', 'c_w_in',
                'c_conv_w', 'c_conv_b', 'c_w_a', 'c_b_a', 'c_w_i', 'c_b_i', 'c_lam', 'c_w_out')
BIG = ('w_xq', 'w_xkv', 'w_xo', 'w_ffn_gu', 'w_ffn_down', 'ab_w_in', 'ab_w_out', 'c_w_in', 'c_w_a', 'c_w_i', 'c_w_out')
SMALL_SHARDED = ('ab_conv_w', 'c_conv_w', 'c_conv_b', 'c_b_a', 'c_b_i', 'c_lam')
REPLICATED = ('g_mix_pre', 'g_mix_post', 'g_cross_pre', 'g_mem', 'g_cross_post', 'g_ffn_pre', 'g_ffn_post', 'ab_b_f')


def _small_full(name, gathered):
    nd = gathered.ndim
    return jnp.moveaxis(gathered, 0, nd - 2).reshape(gathered.shape[1:-1] + (NDEV * gathered.shape[-1],))


def _small_shard(full, dev):
    c = full.shape[-1] // NDEV
    return lax.dynamic_slice_in_dim(full, dev * c, c, axis=full.ndim - 1)


def kernel(x, mem, g_mix_pre, g_mix_post, g_cross_pre, g_mem, g_cross_post, g_ffn_pre, g_ffn_post, w_xq, w_xkv, w_xo, w_ffn_gu, w_ffn_down, ab_w_in, ab_b_f, ab_conv_w, ab_w_out, c_w_in, c_conv_w, c_conv_b, c_w_a, c_b_a, c_w_i, c_b_i, c_lam, c_w_out, loss_target, m_g_mix_pre, m_g_mix_post, m_g_cross_pre, m_g_mem, m_g_cross_post, m_g_ffn_pre, m_g_ffn_post, m_w_xq, m_w_xkv, m_w_xo, m_w_ffn_gu, m_w_ffn_down, m_ab_w_in, m_ab_b_f, m_ab_conv_w, m_ab_w_out, m_c_w_in, m_c_conv_w, m_c_conv_b, m_c_w_a, m_c_b_a, m_c_w_i, m_c_b_i, m_c_lam, m_c_w_out, v_g_mix_pre, v_g_mix_post, v_g_cross_pre, v_g_mem, v_g_cross_post, v_g_ffn_pre, v_g_ffn_post, v_w_xq, v_w_xkv, v_w_xo, v_w_ffn_gu, v_w_ffn_down, v_ab_w_in, v_ab_b_f, v_ab_conv_w, v_ab_w_out, v_c_w_in, v_c_conv_w, v_c_conv_b, v_c_w_a, v_c_b_a, v_c_w_i, v_c_b_i, v_c_lam, v_c_w_out):
    args = locals()
    w = {n: args[n] for n in WEIGHT_NAMES}
    mom = {n: args["m_" + n] for n in WEIGHT_NAMES}
    var = {n: args["v_" + n] for n in WEIGHT_NAMES}
    pos = jnp.stack([lax.axis_index("x"), lax.axis_index("y"), lax.axis_index("c")]).astype(jnp.int32)
    dev = 4 * pos[0] + 2 * pos[1] + pos[2]
    xs, mems, target = x[0], mem[0], loss_target[0]
    n_even, n_odd = (DEPTH + 1) // 2, DEPTH // 2

    small_shapes = [w[n].shape for n in SMALL_SHARDED]
    gathered_small = _unpack_small(_small_gather(_pack_small([w[n] for n in SMALL_SHARDED])), small_shapes, (NDEV,))
    small = {n: _small_full(n, g) for n, g in zip(SMALL_SHARDED, gathered_small)}
    ab_bfb = jnp.broadcast_to(ab_b_f[:, :, None], (n_even, FOX_H, LANE))
    c_bai = jnp.stack([small['c_b_a'].reshape(n_odd, DM), small['c_b_i'].reshape(n_odd, DM)], axis=1)
    row = lambda a, l: a[l][None]

    def layer_names(l):
        mixer = ('ab_w_in', 'ab_w_out') if l % 2 == 0 else ('c_w_in', 'c_w_a', 'c_w_i', 'c_w_out')
        return ('w_xq', 'w_xkv', 'w_xo', 'w_ffn_gu', 'w_ffn_down') + mixer

    def layer_shards(l):
        out = []
        for n in layer_names(l):
            s = w[n][l if w[n].shape[0] == DEPTH else l // 2].astype(BF16)
            out.append(s.reshape(-1, s.shape[-1]))
        return out

    def layer_weights(l, full):
        gate_w = lambda g: g.reshape(NDEV, LRU_NB, LRU_BW // NDEV, LRU_BW).transpose(1, 0, 2, 3).reshape(
            LRU_NB, LRU_BW, LRU_BW)
        cross = (row(g_cross_pre, l), row(g_mem, l), row(g_cross_post, l), full['w_xq'].reshape(DM, DM), full['w_xkv'],
                 full['w_xo'].reshape(DM, DM))
        ffn = (row(g_ffn_pre, l), row(g_ffn_post, l), full['w_ffn_gu'], full['w_ffn_down'].reshape(D_FF, DM))
        if l % 2 == 0:
            e = l // 2
            mixer = (row(g_mix_pre, l), row(g_mix_post, l), _ab_pack(_columns_of_blocks(full['ab_w_in'])), ab_bfb[e],
                     small['ab_conv_w'][e], full['ab_w_out'].reshape(DM, DM))
        else:
            o = l // 2
            mixer = (row(g_mix_pre, l), row(g_mix_post, l), full['c_w_in'], small['c_conv_w'][o],
                     row(small['c_conv_b'], o), jnp.stack([gate_w(full['c_w_a']), gate_w(full['c_w_i'])]), c_bai[o],
                     row(small['c_lam'], o), full['c_w_out'].reshape(DM, DM))
        return mixer, cross, ffn

    saved, weights = [], []
    h = xs
    state, _ = _ag_start(layer_shards(0), xs, "ag_start_0")
    for l in range(DEPTH):
        shards, lands = _ag_wait(state, h, "ag_wait_%d" % l)
        full = _ag_finish(shards, lands)
        weights.append(layer_weights(l, dict(zip(layer_names(l), full))))
        token = None
        if l + 1 < DEPTH:
            state, token = _ag_start(layer_shards(l + 1), full[0], "ag_start_%d" % (l + 1))
        mixer, cross, ffn = weights[l]
        h, s_mix = (_fox_layer_fwd if l % 2 == 0 else _lru_layer_fwd)(h, *mixer, after=token)
        h, s_cross = _cross_fwd(h, mems, *cross)
        h, s_ffn = _ffn_fwd(h, *ffn)
        saved.append((s_mix, s_cross, s_ffn))
    mixer_args = lambda l: weights[l][0]
    cross_args = lambda l: weights[l][1]
    ffn_args = lambda l: weights[l][2]
    dx, loss_rep = _loss_head(h, target)
    loss = lax.psum(loss_rep[0, 0], ("x", "y", "c"))

    grads = {n: [None] * w[n].shape[0] for n in BIG}
    partial = {n: [None] * w[n].shape[0] for n in REPLICATED + SMALL_SHARDED}
    pending = None

    def finish(pending, after):
        state, names, where = pending
        for n, g in zip(names, _rs_end(state, after, pos)):
            grads[n][where[n]] = g

    for l in reversed(range(DEPTH)):
        s_mix, s_cross, s_ffn = saved[l]
        dx, partial['g_ffn_pre'][l], partial['g_ffn_post'][l], dwgu, dwd = _ffn_bwd(
            dx, s_ffn, *ffn_args(l), after=None if pending is None else token)
        (dx, partial['g_cross_pre'][l], partial['g_mem'][l], partial['g_cross_post'][l], dwq, dwkv, dwo) = _cross_bwd(
            dx, s_cross, mems, *cross_args(l))
        layer = {'w_xq': (l, dwq.reshape(NDEV, DM // NDEV, DM)), 'w_xkv': (l, dwkv), 'w_xo': (l, dwo.reshape(NDEV, DM // NDEV, DM)),
                 'w_ffn_gu': (l, dwgu), 'w_ffn_down': (l, dwd.reshape(NDEV, D_FF // NDEV, DM))}
        if l % 2 == 0:
            e = l // 2
            (dx, partial['g_mix_pre'][l], partial['g_mix_post'][l], dwall, partial['ab_b_f'][e], partial['ab_conv_w'][e],
             dwout) = _fox_layer_bwd(dx, s_mix, *mixer_args(l))
            layer['ab_w_in'] = (e, _blocks_of_columns(_ab_unpack(dwall)))
            layer['ab_w_out'] = (e, dwout.reshape(NDEV, DM // NDEV, DM))
        else:
            o = l // 2
            (dx, partial['g_mix_pre'][l], partial['g_mix_post'][l], dwin, partial['c_conv_w'][o], dconvb, dwai, dbai, dlam,
             dwout) = _lru_layer_bwd(dx, s_mix, *mixer_args(l))
            partial['c_conv_b'][o], partial['c_lam'][o] = dconvb[0], dlam[0]
            partial['c_b_a'][o], partial['c_b_i'][o] = dbai[0].reshape(LRU_NB, LRU_BW), dbai[1].reshape(LRU_NB, LRU_BW)
            rows = LRU_BW // NDEV
            by_dev = lambda d: d.reshape(LRU_NB, NDEV, rows, LRU_BW).transpose(1, 0, 2, 3).reshape(NDEV, LRU_NB * rows, LRU_BW)
            layer['c_w_in'] = (o, dwin)
            layer['c_w_a'] = (o, by_dev(dwai[0]))
            layer['c_w_i'] = (o, by_dev(dwai[1]))
            layer['c_w_out'] = (o, dwout.reshape(NDEV, DM // NDEV, DM))
        if pending is not None:
            finish(pending, dx)
        names = list(layer)
        state, token = _rs_begin([layer[n][1] for n in names], pos, str(l))
        pending = (state, names, {n: layer[n][0] for n in names})
    finish(pending, dx)

    small_names = REPLICATED + SMALL_SHARDED
    small_parts = [jnp.stack([p.reshape(w[n].shape[1:] if n in REPLICATED else small[n].shape[1:]) for p in partial[n]])
                   for n in small_names]
    reduced = _unpack_small(_sum_devices(_small_gather(_pack_small(small_parts))), [p.shape for p in small_parts])
    grad = {}
    for n, g in zip(small_names, reduced):
        grad[n] = g if n in REPLICATED else _small_shard(g, dev)
    for n in BIG:
        grad[n] = jnp.stack(grads[n]).reshape(w[n].shape)

    delta, new_m, new_v = {}, {}, {}
    for n in BIG:
        delta[n], new_m[n], new_v[n] = _adamw_nd(w[n], grad[n], mom[n], var[n])
    shapes = [w[n].shape for n in small_names]
    packed = [_pack_small([t[n] for n in small_names]) for t in (w, grad, mom, var)]
    for res, out in zip(_adamw(*packed), (delta, new_m, new_v)):
        for n, val in zip(small_names, _unpack_small(res, shapes)):
            out[n] = val

    return (loss, dx[None], *[grad[n] for n in WEIGHT_NAMES], *[delta[n] for n in WEIGHT_NAMES],
            *[new_m[n] for n in WEIGHT_NAMES], *[new_v[n] for n in WEIGHT_NAMES])
```

```python
import functools
import math

import jax
import jax.numpy as jnp
from jax import lax
from jax.experimental import pallas as pl
from jax.experimental.pallas import tpu as pltpu

F32 = jnp.float32
BF16 = jnp.bfloat16
BS = pl.BlockSpec
ANY = pl.BlockSpec(memory_space=pl.ANY)
MESH = pl.DeviceIdType.MESH

DM = 1024
DEPTH = 4
EPS = 1e-6
NEG = -1e30
FOX_W = 512
FOX_HD = 64
FOX_H = 8
SC_W = 512
SC_K = 3
AB_IN = 3 * FOX_W + FOX_H + 3 * SC_W
AB_PAD = 3200
LRU_BW = 256
LRU_NB = 4
RG_K = 4
RG_C = 8.0
MEM_H = 4
MEM_HD = 256
D_FF = 2816
NDEV = 8
FFB = 2 * D_FF // NDEV
ADAM_LR, ADAM_B1, ADAM_B2, ADAM_EPS, ADAM_WD, ADAM_STEP = 0.001, 0.9, 0.999, 1e-08, 0.01, 10

LANE = 128
VMEM_LIMIT = 48 * 1024 * 1024


def _params(ngrid):
    return pltpu.CompilerParams(dimension_semantics=("arbitrary",) * ngrid, vmem_limit_bytes=VMEM_LIMIT)


TK_RED = 2048
TM_SUM = 1024


def _tile(n, t):
    return t if n % t == 0 else n


def _mm(name, a, b, *, grid, a_spec, b_spec, o_spec, out_shape, dn, out_dtype=F32):
    nred = grid[-1]
    ngrid = len(grid)

    def kern(a_ref, b_ref, o_ref, *scratch):
        p = lax.dot_general(a_ref[...].astype(BF16), b_ref[...].astype(BF16), (dn, ((), ())),
                            preferred_element_type=F32)
        if nred == 1:
            o_ref[...] = p.astype(o_ref.dtype)
            return
        acc = scratch[0] if scratch else o_ref
        r = pl.program_id(ngrid - 1)

        @pl.when(r == 0)
        def _():
            acc[...] = p

        @pl.when(r > 0)
        def _():
            acc[...] += p

        if scratch:
            @pl.when(r == nred - 1)
            def _():
                o_ref[...] = acc[...].astype(o_ref.dtype)

    blk = tuple(d for d in o_spec.block_shape if d is not None)
    scratch = [pltpu.VMEM(blk, F32)] if (nred > 1 and out_dtype != F32) else []
    return pl.pallas_call(kern, name=name, grid=grid, in_specs=[a_spec, b_spec], out_specs=o_spec,
                          out_shape=jax.ShapeDtypeStruct(out_shape, out_dtype), scratch_shapes=scratch,
                          compiler_params=_params(ngrid))(a, b)


NN = ((1,), (0,))
NT = ((1,), (1,))
TN = ((0,), (0,))


def _mm_nn(name, a, w, out_dtype=F32, tn=None):
    m, k = a.shape
    n = w.shape[1]
    tm = _tile(m, 512)
    tn = n if tn is None else tn
    return _mm(name, a, w, grid=(m // tm, n // tn, 1), a_spec=BS((tm, k), lambda i, j, r: (i, 0)),
               b_spec=BS((k, tn), lambda i, j, r: (0, j)), o_spec=BS((tm, tn), lambda i, j, r: (i, j)),
               out_shape=(m, n), dn=NN, out_dtype=out_dtype)


def _mm_nt(name, a, w, out_dtype=F32, tn=None):
    m, n = a.shape
    k = w.shape[0]
    tm = _tile(m, 512)
    tn = n if tn is None else tn
    return _mm(name, a, w, grid=(m // tm, n // tn), a_spec=BS((tm, tn), lambda i, r: (i, r)),
               b_spec=BS((k, tn), lambda i, r: (0, r)), o_spec=BS((tm, k), lambda i, r: (i, 0)),
               out_shape=(m, k), dn=NT, out_dtype=out_dtype)


def _mm_tn(name, a, b, tn=None):
    m, k = a.shape
    n = b.shape[1]
    tm = _tile(m, TK_RED)
    tn = n if tn is None else tn
    return _mm(name, a, b, grid=(n // tn, m // tm), a_spec=BS((tm, k), lambda j, r: (r, 0)),
               b_spec=BS((tm, tn), lambda j, r: (r, j)), o_spec=BS((k, tn), lambda j, r: (0, j)),
               out_shape=(k, n), dn=TN)


def _bmm_nn(name, a, w, out_dtype=F32):
    m, k = a.shape
    g, _, n = w.shape
    tm = _tile(m, 512)
    return _mm(name, a, w, grid=(g, m // tm, 1), a_spec=BS((tm, k), lambda q, i, r: (i, 0)),
               b_spec=BS((None, k, n), lambda q, i, r: (q, 0, 0)), o_spec=BS((None, tm, n), lambda q, i, r: (q, i, 0)),
               out_shape=(g, m, n), dn=NN, out_dtype=out_dtype)


def _bmm_tn(name, a, b):
    m, k = a.shape
    g, _, n = b.shape
    tm = _tile(m, TK_RED)
    return _mm(name, a, b, grid=(g, m // tm), a_spec=BS((tm, k), lambda q, r: (r, 0)),
               b_spec=BS((None, tm, n), lambda q, r: (q, r, 0)), o_spec=BS((None, k, n), lambda q, r: (q, 0, 0)),
               out_shape=(g, k, n), dn=TN)


def _bmm_nt_sum(name, a, w):
    g, m, n = a.shape
    k = w.shape[1]
    tm = _tile(m, TM_SUM)
    return _mm(name, a, w, grid=(m // tm, g), a_spec=BS((None, tm, n), lambda i, q: (q, i, 0)),
               b_spec=BS((None, k, n), lambda i, q: (q, 0, 0)), o_spec=BS((tm, k), lambda i, q: (i, 0)),
               out_shape=(m, k), dn=NT)


def _bmm_nn_sum(name, a, w):
    g, m, k = a.shape
    n = w.shape[2]
    tm = _tile(m, TM_SUM)
    return _mm(name, a, w, grid=(m // tm, g), a_spec=BS((None, tm, k), lambda i, q: (q, i, 0)),
               b_spec=BS((None, k, n), lambda i, q: (q, 0, 0)), o_spec=BS((tm, n), lambda i, q: (i, 0)),
               out_shape=(m, n), dn=NN)


def _bbmm_tn(name, a, b):
    g, m, k = a.shape
    n = b.shape[2]
    tm = _tile(m, TK_RED)
    return _mm(name, a, b, grid=(g, m // tm), a_spec=BS((None, tm, k), lambda q, r: (q, r, 0)),
               b_spec=BS((None, tm, n), lambda q, r: (q, r, 0)), o_spec=BS((None, k, n), lambda q, r: (q, 0, 0)),
               out_shape=(g, k, n), dn=TN)


def _rstd(x):
    return lax.rsqrt(jnp.mean(x * x, axis=-1, keepdims=True) + EPS)


def _norm_fwd(x, g, after=None):
    rows = x.shape[0]
    tm = _tile(rows, 512)

    def kern(x_ref, g_ref, *rest):
        xv = x_ref[...]
        rest[-1][...] = ((xv * _rstd(xv)) * g_ref[...]).astype(BF16)

    extra = () if after is None else (after,)
    return pl.pallas_call(kern, name="norm_fwd", grid=(rows // tm,),
                          in_specs=[BS((tm, DM), lambda i: (i, 0)), BS((1, DM), lambda i: (0, 0))] + [ANY] * len(extra),
                          out_specs=BS((tm, DM), lambda i: (i, 0)),
                          out_shape=jax.ShapeDtypeStruct((rows, DM), BF16), compiler_params=_params(1))(x, g, *extra)


def _norm_res(x, y, g):
    rows = x.shape[0]
    tm = _tile(rows, 512)

    def kern(x_ref, y_ref, g_ref, o_ref):
        yv = y_ref[...]
        o_ref[...] = x_ref[...] + (yv * _rstd(yv)) * g_ref[...]

    row = BS((tm, DM), lambda i: (i, 0))
    return pl.pallas_call(kern, name="norm_res", grid=(rows // tm,),
                          in_specs=[row, row, BS((1, DM), lambda i: (0, 0))], out_specs=row,
                          out_shape=jax.ShapeDtypeStruct((rows, DM), F32), compiler_params=_params(1))(x, y, g)


def _norm_bwd(z, dout, g, resid, out_dtype, after=None):
    rows = z.shape[0]
    tm = _tile(rows, 512)
    has_res = resid is not None

    def kern(*refs):
        z_ref, d_ref, g_ref = refs[:3]
        r_ref = refs[3] if has_res else None
        dz_ref, dg_ref = refs[-2:]
        zv = z_ref[...]
        dv = d_ref[...].astype(F32)
        r = _rstd(zv)
        zh = zv * r
        dzh = dv * g_ref[...]
        dz = r * (dzh - zh * jnp.mean(dzh * zh, axis=-1, keepdims=True))
        if has_res:
            dz = dz + r_ref[...]
        dz_ref[...] = dz.astype(dz_ref.dtype)
        part = jnp.sum(dv * zh, axis=0, keepdims=True)

        @pl.when(pl.program_id(0) == 0)
        def _():
            dg_ref[...] = part

        @pl.when(pl.program_id(0) > 0)
        def _():
            dg_ref[...] += part

    row = BS((tm, DM), lambda i: (i, 0))
    vec = BS((1, DM), lambda i: (0, 0))
    ins = [row, row, vec] + ([row] if has_res else []) + ([ANY] if after is not None else [])
    args = (z, dout, g) + ((resid,) if has_res else ()) + ((after,) if after is not None else ())
    return pl.pallas_call(kern, name="norm_bwd_res" if has_res else "norm_bwd", grid=(rows // tm,), in_specs=ins,
                          out_specs=[row, vec],
                          out_shape=[jax.ShapeDtypeStruct((rows, DM), out_dtype), jax.ShapeDtypeStruct((1, DM), F32)],
                          compiler_params=_params(1))(*args)


def _ffn_up(h, wgu4):
    s = h.shape[0]
    tm = _tile(s, 512)

    def kern(h_ref, w_ref, gu_ref, a_ref):
        hv = h_ref[...]
        gate = jnp.dot(hv, w_ref[0], preferred_element_type=F32)
        up = jnp.dot(hv, w_ref[1], preferred_element_type=F32)
        gu_ref[0] = gate
        gu_ref[1] = up
        a_ref[...] = (gate * jax.nn.sigmoid(gate) * up).astype(BF16)

    return pl.pallas_call(
        kern, name="ffn_up", grid=(4, s // tm),
        in_specs=[BS((tm, DM), lambda j, i: (i, 0)), BS((2, None, DM, FFB), lambda j, i: (0, j, 0, 0))],
        out_specs=[BS((2, None, tm, FFB), lambda j, i: (0, j, i, 0)), BS((None, tm, FFB), lambda j, i: (j, i, 0))],
        out_shape=[jax.ShapeDtypeStruct((2, 4, s, FFB), F32), jax.ShapeDtypeStruct((4, s, FFB), BF16)],
        compiler_params=_params(2))(h, wgu4)


def _ffn_da(dy, wd4, gu):
    s = dy.shape[0]
    tm = _tile(s, 512)

    def kern(dy_ref, w_ref, gu_ref, o_ref):
        da = lax.dot_general(dy_ref[...], w_ref[...], (NT, ((), ())), preferred_element_type=F32)
        gate = gu_ref[0]
        up = gu_ref[1]
        sg = jax.nn.sigmoid(gate)
        o_ref[0] = (da * up * (sg * (1.0 + gate * (1.0 - sg)))).astype(BF16)
        o_ref[1] = (da * (gate * sg)).astype(BF16)

    blk = BS((2, None, tm, FFB), lambda j, i: (0, j, i, 0))
    return pl.pallas_call(
        kern, name="ffn_da", grid=(4, s // tm),
        in_specs=[BS((tm, DM), lambda j, i: (i, 0)), BS((None, FFB, DM), lambda j, i: (j, 0, 0)), blk],
        out_specs=blk, out_shape=jax.ShapeDtypeStruct((2, 4, s, FFB), BF16), compiler_params=_params(2))(dy, wd4, gu)


def _ffn_fwd(x, gpre, gpost, wgu, wd):
    h = _norm_fwd(x, gpre)
    gu, a = _ffn_up(h, wgu.reshape(2, 4, DM, FFB))
    y = _bmm_nn_sum("ffn_down", a, wd.reshape(4, FFB, DM))
    return _norm_res(x, y, gpost), (x, h, gu, a, y)


def _ffn_bwd(dxo, saved, gpre, gpost, wgu, wd, after=None):
    x, h, gu, a, y = saved
    s = x.shape[0]
    dy, dgpost = _norm_bwd(y, dxo, gpost, None, BF16, after)
    dgu = _ffn_da(dy, wd.reshape(4, FFB, DM), gu).reshape(8, s, FFB)
    dwd = _bmm_tn_a3("ffn_dwd", a, dy)
    dwgu = _bmm_tn("ffn_dwgu", h, dgu)
    dh = _bmm_nt_sum("ffn_dh", dgu, wgu)
    dx, dgpre = _norm_bwd(x, dh, gpre, dxo, F32)
    return dx, dgpre, dgpost, dwgu, dwd.reshape(D_FF, DM)


def _bmm_tn_a3(name, a, b):
    g, m, k = a.shape
    n = b.shape[1]
    tm = _tile(m, TK_RED)
    return _mm(name, a, b, grid=(g, m // tm), a_spec=BS((None, tm, k), lambda q, r: (q, r, 0)),
               b_spec=BS((tm, n), lambda q, r: (r, 0)), o_spec=BS((None, k, n), lambda q, r: (q, 0, 0)),
               out_shape=(g, k, n), dn=TN)


def _softmax_rows(s):
    m = jnp.max(s, axis=-1, keepdims=True)
    p = jnp.exp(s - m)
    return p / jnp.sum(p, axis=-1, keepdims=True)


def _xattn_fwd_call(h, wq, kv):
    s = h.shape[0]
    mlen = kv.shape[1]
    tm = _tile(s, 512)
    scale = MEM_HD ** -0.5

    def kern(h_ref, w_ref, k_ref, v_ref, q_ref, o_ref):
        q = jnp.dot(h_ref[...], w_ref[...], preferred_element_type=F32).astype(BF16)
        q_ref[...] = q
        sc = lax.dot_general(q, k_ref[...], (NT, ((), ())), preferred_element_type=F32) * scale
        p = _softmax_rows(sc)
        o_ref[...] = jnp.dot(p.astype(BF16), v_ref[...], preferred_element_type=F32).astype(BF16)

    blk = BS((tm, MEM_HD), lambda i, hd: (i, hd))
    return pl.pallas_call(
        kern, name="xattn_fwd", grid=(s // tm, MEM_H),
        in_specs=[BS((tm, DM), lambda i, hd: (i, 0)), BS((DM, MEM_HD), lambda i, hd: (0, hd)),
                  BS((None, mlen, MEM_HD), lambda i, hd: (hd, 0, 0)),
                  BS((None, mlen, MEM_HD), lambda i, hd: (MEM_H + hd, 0, 0))],
        out_specs=[blk, blk],
        out_shape=[jax.ShapeDtypeStruct((s, DM), BF16), jax.ShapeDtypeStruct((s, DM), BF16)],
        compiler_params=_params(2))(h, wq, kv, kv)


def _xattn_bwd_call(q, kv, do):
    s = q.shape[0]
    mlen = kv.shape[1]
    tm = _tile(s, 512)
    scale = MEM_HD ** -0.5

    def kern(q_ref, k_ref, v_ref, do_ref, dq_ref, dkv_ref):
        qv, kvv, vv, dov = q_ref[...], k_ref[...], v_ref[...], do_ref[...]
        sc = lax.dot_general(qv, kvv, (NT, ((), ())), preferred_element_type=F32) * scale
        p = _softmax_rows(sc)
        dp = lax.dot_general(dov, vv, (NT, ((), ())), preferred_element_type=F32)
        ds = (p * (dp - jnp.sum(dp * p, axis=-1, keepdims=True)) * scale).astype(BF16)
        dq_ref[...] = jnp.dot(ds, kvv, preferred_element_type=F32).astype(BF16)
        dk = lax.dot_general(ds, qv, (TN, ((), ())), preferred_element_type=F32)
        dv = lax.dot_general(p.astype(BF16), dov, (TN, ((), ())), preferred_element_type=F32)

        @pl.when(pl.program_id(1) == 0)
        def _():
            dkv_ref[0] = dk
            dkv_ref[1] = dv

        @pl.when(pl.program_id(1) > 0)
        def _():
            dkv_ref[0] += dk
            dkv_ref[1] += dv

    blk = BS((tm, MEM_HD), lambda hd, i: (i, hd))
    return pl.pallas_call(
        kern, name="xattn_bwd", grid=(MEM_H, s // tm),
        in_specs=[blk, BS((None, mlen, MEM_HD), lambda hd, i: (hd, 0, 0)),
                  BS((None, mlen, MEM_HD), lambda hd, i: (MEM_H + hd, 0, 0)), blk],
        out_specs=[blk, BS((2, None, mlen, MEM_HD), lambda hd, i: (0, hd, 0, 0))],
        out_shape=[jax.ShapeDtypeStruct((s, DM), BF16), jax.ShapeDtypeStruct((2, MEM_H, mlen, MEM_HD), F32)],
        compiler_params=_params(2))(q, kv, kv, do)


def _cross_fwd(x, mem, gpre, gmem, gpost, wq, wkv, wo, after=None):
    h = _norm_fwd(x, gpre, after)
    mn = _norm_fwd(mem, gmem)
    kv = _bmm_nn("xattn_kv", mn, wkv, BF16)
    q, o = _xattn_fwd_call(h, wq, kv)
    y = _mm_nn("xattn_out", o, wo)
    return _norm_res(x, y, gpost), (x, h, mn, kv, q, o, y)


def _cross_bwd(dxo, saved, mem, gpre, gmem, gpost, wq, wkv, wo, after=None):
    x, h, mn, kv, q, o, y = saved
    mlen = mem.shape[0]
    dy, dgpost = _norm_bwd(y, dxo, gpost, None, BF16, after)
    do = _mm_nt("xattn_do", dy, wo, BF16)
    dwo = _mm_tn("xattn_dwo", o, dy)
    dq, dkv = _xattn_bwd_call(q, kv, do)
    dwq = _mm_tn("xattn_dwq", h, dq)
    dh = _mm_nt("xattn_dh", dq, wq)
    dkv8 = dkv.reshape(8, mlen, MEM_HD)
    dwkv = _bmm_tn("xattn_dwkv", mn, dkv8)
    dmn = _bmm_nt_sum("xattn_dmn", dkv8, wkv)
    _, dgmem = _norm_bwd(mem, dmn, gmem, None, BF16)
    dx, dgpre = _norm_bwd(x, dh, gpre, dxo, F32)
    return dx, dgpre, dgmem, dgpost, dwq, dwkv, dwo


def _log_sigmoid(z):
    return jnp.minimum(z, 0.0) - jnp.log1p(jnp.exp(-jnp.abs(z)))


def _lane_scan_steps():
    return (1, 2, 4, 8, 16, 32, 64)


def _fox_cum(frow, bfb):
    s = frow.shape[1]

    def kern(f_ref, b_ref, o_ref):
        lane = lax.broadcasted_iota(jnp.int32, (FOX_H, LANE), 1)
        carry = jnp.zeros((FOX_H, 1), F32)
        for c in range(s // LANE):
            sl = slice(c * LANE, (c + 1) * LANE)
            lf = _log_sigmoid(f_ref[:, sl] + b_ref[...])
            v = lf
            for d in _lane_scan_steps():
                v = v + jnp.where(lane >= d, pltpu.roll(v, d, 1), 0.0)
            o_ref[:, sl] = v + carry
            carry = carry + jnp.sum(lf, axis=1, keepdims=True)

    return pl.pallas_call(kern, name="fox_cum", out_shape=jax.ShapeDtypeStruct((FOX_H, s), F32),
                          compiler_params=pltpu.CompilerParams(vmem_limit_bytes=VMEM_LIMIT))(frow, bfb)


def _fox_dlogf(dcq, dck, frow, bfb):
    s = frow.shape[1]

    def kern(q_ref, d_ref, f_ref, b_ref, df_ref, db_ref):
        lane = lax.broadcasted_iota(jnp.int32, (FOX_H, LANE), 1)
        carry = jnp.zeros((FOX_H, 1), F32)
        dbf = jnp.zeros((FOX_H, 1), F32)
        for c in reversed(range(s // LANE)):
            sl = slice(c * LANE, (c + 1) * LANE)
            dc = q_ref[:, sl] - d_ref[:, sl]
            v = dc
            for d in _lane_scan_steps():
                v = v + jnp.where(lane < LANE - d, pltpu.roll(v, LANE - d, 1), 0.0)
            v = v + carry
            carry = carry + jnp.sum(dc, axis=1, keepdims=True)
            df = v * jax.nn.sigmoid(-(f_ref[:, sl] + b_ref[...]))
            df_ref[:, sl] = df
            dbf = dbf + jnp.sum(df, axis=1, keepdims=True)
        db_ref[...] = jnp.broadcast_to(dbf, (FOX_H, LANE))

    return pl.pallas_call(kern, name="fox_dlogf",
                          out_shape=[jax.ShapeDtypeStruct((FOX_H, s), F32), jax.ShapeDtypeStruct((FOX_H, LANE), F32)],
                          compiler_params=pltpu.CompilerParams(vmem_limit_bytes=VMEM_LIMIT))(dcq, dck, frow, bfb)


FOX_TQ = 512
Q_COL, K_COL, V_COL = 0, FOX_W // LANE, 2 * FOX_W // LANE
B_COL, C_COL, U_COL = 12, 16, 20


def _fox_logits(qm, kb, cc, cr, causal, scale, reps):
    sc = lax.dot_general(qm, kb, (NT, ((), ())), preferred_element_type=F32) * scale
    sc = sc + jnp.tile(cc, (1, reps)) - cr
    return jnp.where(causal, sc, NEG)


def _fox_fwd_call(proj, cumc, cumr):
    s = proj.shape[0]
    tq = _tile(s, FOX_TQ)
    nq = s // tq
    reps = tq // LANE
    scale = FOX_HD ** -0.5

    def kern(q_ref, k_ref, v_ref, cc_ref, cr_ref, o_ref, lse_ref, m_s, l_s, acc_s):
        i = pl.program_id(1)
        j = pl.program_id(2)
        lane = lax.broadcasted_iota(jnp.int32, (tq, LANE), 1)

        @pl.when(j == 0)
        def _():
            m_s[...] = jnp.full(m_s.shape, NEG, F32)
            l_s[...] = jnp.zeros(l_s.shape, F32)
            acc_s[...] = jnp.zeros(acc_s.shape, F32)

        @pl.when(j <= i)
        def _():
            qv = q_ref[...]
            kb = k_ref[...].astype(BF16)
            vb = v_ref[...].astype(BF16)
            causal = (i * tq + lax.broadcasted_iota(jnp.int32, (tq, tq), 0)
                      >= j * tq + lax.broadcasted_iota(jnp.int32, (tq, tq), 1))
            for hh in range(2):
                sel = (lane < FOX_HD) if hh == 0 else (lane >= FOX_HD)
                qm = jnp.where(sel, qv, 0.0).astype(BF16)
                sc = _fox_logits(qm, kb, cc_ref[hh], cr_ref[hh:hh + 1, :], causal, scale, reps)
                m_prev = m_s[hh]
                m_new = jnp.maximum(m_prev, jnp.max(sc, axis=-1, keepdims=True))
                alpha = jnp.exp(m_prev - m_new)
                p = jnp.exp(sc - m_new)
                l_s[hh] = alpha * l_s[hh] + jnp.sum(p, axis=-1, keepdims=True)
                acc_s[hh] = alpha * acc_s[hh] + jnp.dot(p.astype(BF16), vb, preferred_element_type=F32)
                m_s[hh] = m_new

        @pl.when(j == i)
        def _():
            o_ref[...] = jnp.where(lane < FOX_HD, acc_s[0] / l_s[0], acc_s[1] / l_s[1])
            for hh in range(2):
                lse_ref[hh] = jnp.broadcast_to(m_s[hh] + jnp.log(l_s[hh]), (tq, LANE))

    kvi = lambda hp, i, j: jnp.minimum(j, i)
    return pl.pallas_call(
        kern, name="fox_fwd", grid=(4, nq, nq),
        in_specs=[BS((tq, LANE), lambda hp, i, j: (i, Q_COL + hp)),
                  BS((tq, LANE), lambda hp, i, j: (kvi(hp, i, j), K_COL + hp)),
                  BS((tq, LANE), lambda hp, i, j: (kvi(hp, i, j), V_COL + hp)),
                  BS((2, tq, LANE), lambda hp, i, j: (hp, i, 0)),
                  BS((None, 2, tq), lambda hp, i, j: (hp, 0, kvi(hp, i, j)))],
        out_specs=[BS((tq, LANE), lambda hp, i, j: (i, hp)), BS((2, tq, LANE), lambda hp, i, j: (hp, i, 0))],
        out_shape=[jax.ShapeDtypeStruct((s, FOX_W), F32), jax.ShapeDtypeStruct((FOX_H, s, LANE), F32)],
        scratch_shapes=[pltpu.VMEM((2, tq, 1), F32), pltpu.VMEM((2, tq, 1), F32), pltpu.VMEM((2, tq, LANE), F32)],
        compiler_params=_params(3))(proj, proj, proj, cumc, cumr)


ROWSUM_M = 16


def _fox_bwd_call(proj, o, lse, dcat, cumc, cumr):
    s = proj.shape[0]
    tq = _tile(s, FOX_TQ)
    nq = s // tq
    reps = tq // LANE
    scale = FOX_HD ** -0.5

    def kern(q_ref, k_ref, v_ref, do_ref, o_ref, lse_ref, cc_ref, cr_ref, dq_ref, dk_ref, dv_ref, dck_ref, dcq_ref):
        j = pl.program_id(1)
        i = pl.program_id(2)
        lane = lax.broadcasted_iota(jnp.int32, (tq, LANE), 1)
        ones = jnp.ones((ROWSUM_M, tq), BF16)

        @pl.when((j == 0) & (i == 0))
        def _():
            dq_ref[...] = jnp.zeros(dq_ref.shape, F32)
            dcq_ref[...] = jnp.zeros(dcq_ref.shape, F32)

        @pl.when(i == j)
        def _():
            dk_ref[...] = jnp.zeros(dk_ref.shape, F32)
            dv_ref[...] = jnp.zeros(dv_ref.shape, F32)
            dck_ref[...] = jnp.zeros(dck_ref.shape, F32)

        @pl.when(i >= j)
        def _():
            qv = q_ref[...]
            dov = do_ref[...]
            ov = o_ref[...]
            kb = k_ref[...].astype(BF16)
            vb = v_ref[...].astype(BF16)
            causal = (i * tq + lax.broadcasted_iota(jnp.int32, (tq, tq), 0)
                      >= j * tq + lax.broadcasted_iota(jnp.int32, (tq, tq), 1))
            dq_t = jnp.zeros((tq, LANE), F32)
            dk_t = jnp.zeros((tq, LANE), F32)
            dv_t = jnp.zeros((tq, LANE), F32)
            for hh in range(2):
                sel = (lane < FOX_HD) if hh == 0 else (lane >= FOX_HD)
                qm = jnp.where(sel, qv, 0.0).astype(BF16)
                dom32 = jnp.where(sel, dov, 0.0)
                dom = dom32.astype(BF16)
                sc = _fox_logits(qm, kb, cc_ref[hh], cr_ref[hh:hh + 1, :], causal, scale, reps)
                p = jnp.exp(sc - jnp.tile(lse_ref[hh], (1, reps)))
                dp = lax.dot_general(dom, vb, (NT, ((), ())), preferred_element_type=F32)
                delta = jnp.sum(dom32 * ov, axis=-1, keepdims=True)
                ds = p * (dp - delta)
                dsb = ds.astype(BF16)
                dq_t = jnp.where(sel, jnp.dot(dsb, kb, preferred_element_type=F32) * scale, dq_t)
                dk_t = dk_t + lax.dot_general(dsb, qm, (TN, ((), ())), preferred_element_type=F32) * scale
                dv_t = dv_t + lax.dot_general(p.astype(BF16), dom, (TN, ((), ())), preferred_element_type=F32)
                dck_ref[hh] += jnp.sum(ds, axis=0, keepdims=True)
                ds_lo = (ds - dsb.astype(F32)).astype(BF16)
                dcq_ref[hh, i] += (lax.dot_general(ones, dsb, (NT, ((), ())), preferred_element_type=F32)
                                   + lax.dot_general(ones, ds_lo, (NT, ((), ())), preferred_element_type=F32))
            rows =pl.ds(pl.multiple_of(i * tq, tq), tq)
            dq_ref[rows, :] += dq_t
            dk_ref[...] += dk_t
            dv_ref[...] += dv_t

    qi = lambda hp, j, i: jnp.maximum(i, j)
    return pl.pallas_call(
        kern, name="fox_bwd", grid=(4, nq, nq),
        in_specs=[BS((tq, LANE), lambda hp, j, i: (qi(hp, j, i), Q_COL + hp)),
                  BS((tq, LANE), lambda hp, j, i: (j, K_COL + hp)),
                  BS((tq, LANE), lambda hp, j, i: (j, V_COL + hp)),
                  BS((tq, LANE), lambda hp, j, i: (qi(hp, j, i), hp)),
                  BS((tq, LANE), lambda hp, j, i: (qi(hp, j, i), hp)),
                  BS((2, tq, LANE), lambda hp, j, i: (hp, qi(hp, j, i), 0)),
                  BS((2, tq, LANE), lambda hp, j, i: (hp, qi(hp, j, i), 0)),
                  BS((None, 2, tq), lambda hp, j, i: (hp, 0, j))],
        out_specs=[BS((s, LANE), lambda hp, j, i: (0, hp)), BS((tq, LANE), lambda hp, j, i: (j, hp)),
                   BS((tq, LANE), lambda hp, j, i: (j, hp)), BS((2, 1, tq), lambda hp, j, i: (hp, 0, j)),
                   BS((2, nq, ROWSUM_M, tq), lambda hp, j, i: (hp, 0, 0, 0))],
        out_shape=[jax.ShapeDtypeStruct((s, FOX_W), F32), jax.ShapeDtypeStruct((s, FOX_W), F32),
                   jax.ShapeDtypeStruct((s, FOX_W), F32), jax.ShapeDtypeStruct((FOX_H, 1, s), F32),
                   jax.ShapeDtypeStruct((FOX_H, nq, ROWSUM_M, tq), F32)],
        compiler_params=_params(3))(proj, proj, proj, dcat, o, lse, cumc, cumr)


def _shift_down(v, d, row):
    return jnp.where(row >= d, pltpu.roll(v, d, 0), 0.0)


def _shift_up(v, d, row, n):
    return jnp.where(row < n - d, pltpu.roll(v, n - d, 0), 0.0)


def _sconv_fwd(proj, convw):
    s = proj.shape[0]

    def kern(b_ref, c_ref, u_ref, w_ref, y_ref):
        row = lax.broadcasted_iota(jnp.int32, (s, LANE), 0)
        z = c_ref[...] * u_ref[...]
        conv = w_ref[2:3, :] * z + w_ref[1:2, :] * _shift_down(z, 1, row) + w_ref[0:1, :] * _shift_down(z, 2, row)
        y_ref[...] = (b_ref[...] * conv).astype(BF16)

    col = lambda base: BS((s, LANE), lambda cb: (0, base + cb))
    return pl.pallas_call(kern, name="sconv_fwd", grid=(SC_W // LANE,),
                          in_specs=[col(B_COL), col(C_COL), col(U_COL), BS((SC_K, LANE), lambda cb: (0, cb))],
                          out_specs=BS((s, LANE), lambda cb: (0, cb)),
                          out_shape=jax.ShapeDtypeStruct((s, SC_W), BF16), compiler_params=_params(1))(proj, proj, proj, convw)


def _sconv_bwd(proj, convw, dcat):
    s = proj.shape[0]

    def kern(b_ref, c_ref, u_ref, w_ref, dy_ref, db_ref, dc_ref, du_ref, dw_ref):
        row = lax.broadcasted_iota(jnp.int32, (s, LANE), 0)
        cv, uv, dyv = c_ref[...], u_ref[...], dy_ref[...]
        z = cv * uv
        z1 = _shift_down(z, 1, row)
        z2 = _shift_down(z, 2, row)
        conv = w_ref[2:3, :] * z + w_ref[1:2, :] * z1 + w_ref[0:1, :] * z2
        db_ref[...] = dyv * conv
        dcv = dyv * b_ref[...]
        dz = w_ref[2:3, :] * dcv + w_ref[1:2, :] * _shift_up(dcv, 1, row, s) + w_ref[0:1, :] * _shift_up(dcv, 2, row, s)
        dc_ref[...] = dz * uv
        du_ref[...] = dz * cv
        dw_ref[0:1, :] = jnp.sum(dcv * z2, axis=0, keepdims=True)
        dw_ref[1:2, :] = jnp.sum(dcv * z1, axis=0, keepdims=True)
        dw_ref[2:3, :] = jnp.sum(dcv * z, axis=0, keepdims=True)

    col = lambda base: BS((s, LANE), lambda cb: (0, base + cb))
    out = BS((s, LANE), lambda cb: (0, cb))
    wspec = BS((SC_K, LANE), lambda cb: (0, cb))
    act = jax.ShapeDtypeStruct((s, SC_W), F32)
    return pl.pallas_call(kern, name="sconv_bwd", grid=(SC_W // LANE,),
                          in_specs=[col(B_COL), col(C_COL), col(U_COL), wspec, col(FOX_W // LANE)],
                          out_specs=[out, out, out, wspec],
                          out_shape=[act, act, act, jax.ShapeDtypeStruct((SC_K, SC_W), F32)],
                          compiler_params=_params(1))(proj, proj, proj, convw, dcat)


def _fox_layer_fwd(x, gpre, gpost, wall, bfb, convw, wout, after=None):
    s = x.shape[0]
    h = _norm_fwd(x, gpre, after)
    proj = _mm_nn("fox_proj", h, wall, tn=AB_PAD // 5)
    frow = proj[:, 3 * FOX_W + 3 * SC_W:3 * FOX_W + 3 * SC_W + FOX_H].T
    cumr = _fox_cum(frow, bfb)
    cumc = jnp.broadcast_to(cumr[:, :, None], (FOX_H, s, LANE))
    cumr4 = cumr.reshape(4, 2, s)
    o, lse = _fox_fwd_call(proj, cumc, cumr4)
    yb = _sconv_fwd(proj, convw)
    cat = jnp.concatenate([o.astype(BF16), yb], axis=1)
    y = _mm_nn("fox_out", cat, wout)
    return _norm_res(x, y, gpost), (x, h, proj, frow, cumc, cumr4, o, lse, cat, y)


def _fox_layer_bwd(dxo, saved, gpre, gpost, wall, bfb, convw, wout, after=None):
    x, h, proj, frow, cumc, cumr4, o, lse, cat, y = saved
    s = x.shape[0]
    dy, dgpost = _norm_bwd(y, dxo, gpost, None, BF16, after)
    dcat = _mm_nt("fox_dcat", dy, wout)
    dwout = _mm_tn("fox_dwout", cat, dy)
    db, dc, du, dconvw = _sconv_bwd(proj, convw, dcat)
    dq, dk, dv, dck, dcq = _fox_bwd_call(proj, o, lse, dcat, cumc, cumr4)
    dfrow, dbf = _fox_dlogf(dcq[:, :, 0, :].reshape(FOX_H, s), dck.reshape(FOX_H, s), frow, bfb)
    dfcol = jnp.pad(dfrow.T, ((0, 0), (0, LANE - FOX_H)))
    dproj = jnp.concatenate([dq, dk, dv, db, dc, du, dfcol], axis=1).astype(BF16)
    dwall = _mm_tn("fox_dwall", h, dproj, tn=AB_PAD // 5)
    dh = _mm_nt("fox_dh", dproj, wall, tn=AB_PAD // 5)
    dx, dgpre = _norm_bwd(x, dh, gpre, dxo, F32)
    return dx, dgpre, dgpost, dwall, dbf[:, 0], dconvw, dwout


def _ab_pack(w):
    nf = 3 * FOX_W
    return jnp.concatenate([w[:, :nf], w[:, nf + FOX_H:], w[:, nf:nf + FOX_H],
                            jnp.zeros((w.shape[0], AB_PAD - AB_IN), w.dtype)], axis=1)


def _ab_unpack(w):
    nf = 3 * FOX_W
    nbcu = 3 * SC_W
    return jnp.concatenate([w[:, :nf], w[:, nf + nbcu:nf + nbcu + FOX_H], w[:, nf:nf + nbcu]], axis=1)


NCH = DM // LANE
CH_PER_BLK = LRU_BW // LANE


def _chunk_spec(s, lead=0):
    return BS((None, s, LANE), lambda ch: (lead + ch // CH_PER_BLK, 0, ch % CH_PER_BLK))


def _vec_chunk(rows):
    return BS((rows, LANE), lambda ch: (0, ch))


def _neg_expm1(x):
    series = -x * (1.0 + x * (1 / 2) * (1.0 + x * (1 / 3) * (1.0 + x * (1 / 4) * (1.0 + x * (1 / 5) * (
        1.0 + x * (1 / 6) * (1.0 + x * (1 / 7)))))))
    return jnp.where(x > -0.25, series, 1.0 - jnp.exp(x))


def _softplus(z):
    return jnp.maximum(z, 0.0) + jnp.log1p(jnp.exp(-jnp.abs(z)))


GELU_C = math.sqrt(2.0 / math.pi)
GELU_A = 0.044715


def _gelu(x):
    return 0.5 * x * (1.0 + jnp.tanh(GELU_C * (x + GELU_A * x * x * x)))


def _gelu_grad(x):
    t = jnp.tanh(GELU_C * (x + GELU_A * x * x * x))
    return 0.5 * (1.0 + t) + 0.5 * x * (1.0 - t * t) * GELU_C * (1.0 + 3.0 * GELU_A * x * x)


def _lru_conv_fwd(gu, convw, convb):
    s = gu.shape[1]

    def kern(x_ref, w_ref, b_ref, u_ref):
        row = lax.broadcasted_iota(jnp.int32, (s, LANE), 0)
        xv = x_ref[...]
        u_ref[...] = (b_ref[...] + w_ref[3:4, :] * xv + w_ref[2:3, :] * _shift_down(xv, 1, row)
                      + w_ref[1:2, :] * _shift_down(xv, 2, row) + w_ref[0:1, :] * _shift_down(xv, 3, row))

    return pl.pallas_call(kern, name="lru_conv_fwd", grid=(NCH,),
                          in_specs=[_chunk_spec(s, LRU_NB), _vec_chunk(RG_K), _vec_chunk(1)], out_specs=_chunk_spec(s),
                          out_shape=jax.ShapeDtypeStruct((LRU_NB, s, LRU_BW), F32), compiler_params=_params(1))(gu, convw, convb)


def _lru_conv_bwd(dud, dug, gu, convw):
    s = gu.shape[1]

    def kern(d1_ref, d2_ref, x_ref, w_ref, dx_ref, dw_ref, db_ref):
        row = lax.broadcasted_iota(jnp.int32, (s, LANE), 0)
        du = d1_ref[...] + d2_ref[...]
        xv = x_ref[...]
        dx_ref[...] = (w_ref[3:4, :] * du + w_ref[2:3, :] * _shift_up(du, 1, row, s) + w_ref[1:2, :] * _shift_up(du, 2, row, s)
                       + w_ref[0:1, :] * _shift_up(du, 3, row, s)).astype(BF16)
        dw_ref[3:4, :] = jnp.sum(du * xv, axis=0, keepdims=True)
        for k in range(1, RG_K):
            dw_ref[3 - k:4 - k, :] = jnp.sum(du * _shift_down(xv, k, row), axis=0, keepdims=True)
        db_ref[...] = jnp.sum(du, axis=0, keepdims=True)

    return pl.pallas_call(kern, name="lru_conv_bwd", grid=(NCH,),
                          in_specs=[_chunk_spec(s), _chunk_spec(s), _chunk_spec(s, LRU_NB), _vec_chunk(RG_K)],
                          out_specs=[_chunk_spec(s), _vec_chunk(RG_K), _vec_chunk(1)],
                          out_shape=[jax.ShapeDtypeStruct((LRU_NB, s, LRU_BW), BF16),
                                     jax.ShapeDtypeStruct((RG_K, DM), F32), jax.ShapeDtypeStruct((1, DM), F32)],
                          compiler_params=_params(1))(dud, dug, gu, convw)


def _lru_gates(z_ref, bai_ref, lam_ref, uv):
    r = jax.nn.sigmoid(z_ref[0] + bai_ref[0:1, :])
    ig = jax.nn.sigmoid(z_ref[1] + bai_ref[1:2, :])
    sp = _softplus(-lam_ref[...])
    la = -RG_C * r * sp
    a = jnp.exp(la)
    sq = jnp.sqrt(_neg_expm1(2.0 * la))
    return r, ig, sp, a, sq


def _scan_steps(n):
    d, out = 1, []
    while d < n:
        out.append(d)
        d *= 2
    return out


def _lru_scan_fwd(z, bai, lam, u, gu):
    s = u.shape[1]
    zspec = BS((2, None, s, LANE), lambda ch: (0, ch // CH_PER_BLK, 0, ch % CH_PER_BLK))

    def kern(z_ref, bai_ref, lam_ref, u_ref, g_ref, hs_ref, y_ref):
        row = lax.broadcasted_iota(jnp.int32, (s, LANE), 0)
        uv = u_ref[...]
        _, ig, _, a, sq = _lru_gates(z_ref, bai_ref, lam_ref, uv)
        b = sq * (ig * uv)
        for d in _scan_steps(s):
            a_sh = jnp.where(row >= d, pltpu.roll(a, d, 0), 1.0)
            b = a * _shift_down(b, d, row) + b
            a = a * a_sh
        hs_ref[...] = b
        y_ref[...] = (_gelu(g_ref[...]) * b).astype(BF16)

    return pl.pallas_call(kern, name="lru_scan_fwd", grid=(NCH,),
                          in_specs=[zspec, _vec_chunk(2), _vec_chunk(1), _chunk_spec(s), _chunk_spec(s)],
                          out_specs=[_chunk_spec(s), BS((s, LANE), lambda ch: (0, ch))],
                          out_shape=[jax.ShapeDtypeStruct((LRU_NB, s, LRU_BW), F32), jax.ShapeDtypeStruct((s, DM), BF16)],
                          compiler_params=_params(1))(z, bai, lam, u, gu)


def _lru_scan_bwd(dyp, z, bai, lam, u, gu, hs):
    s = u.shape[1]
    zspec = BS((2, None, s, LANE), lambda ch: (0, ch // CH_PER_BLK, 0, ch % CH_PER_BLK))

    def kern(dy_ref, z_ref, bai_ref, lam_ref, u_ref, g_ref, hs_ref, dg_ref, dz_ref, du_ref, dbai_ref, dlam_ref):
        row = lax.broadcasted_iota(jnp.int32, (s, LANE), 0)
        uv, gv, hv, dyv = u_ref[...], g_ref[...], hs_ref[...], dy_ref[...]
        r, ig, sp, a, sq = _lru_gates(z_ref, bai_ref, lam_ref, uv)
        dg_ref[...] = (dyv * hv * _gelu_grad(gv)).astype(BF16)
        g = dyv * _gelu(gv)
        an = _shift_up(a, 1, row, s)
        for d in _scan_steps(s):
            an_sh = jnp.where(row < s - d, pltpu.roll(an, s - d, 0), 1.0)
            g = an * _shift_up(g, d, row, s) + g
            an = an * an_sh
        da = g * _shift_down(hv, 1, row)
        dsq = g * (ig * uv)
        di = g * sq * uv
        du_ref[...] = g * sq * ig
        dla = da * a - dsq * (a * a / sq)
        dzr = dla * (-RG_C * sp) * r * (1.0 - r)
        dzi = di * ig * (1.0 - ig)
        dz_ref[0] = dzr.astype(BF16)
        dz_ref[1] = dzi.astype(BF16)
        dbai_ref[0:1, :] = jnp.sum(dzr, axis=0, keepdims=True)
        dbai_ref[1:2, :] = jnp.sum(dzi, axis=0, keepdims=True)
        dlam_ref[...] = jnp.sum(dla * r, axis=0, keepdims=True) * (RG_C * jax.nn.sigmoid(-lam_ref[...]))

    return pl.pallas_call(
        kern, name="lru_scan_bwd", grid=(NCH,),
        in_specs=[BS((s, LANE), lambda ch: (0, ch)), zspec, _vec_chunk(2), _vec_chunk(1), _chunk_spec(s), _chunk_spec(s),
                  _chunk_spec(s)],
        out_specs=[_chunk_spec(s), zspec, _chunk_spec(s), _vec_chunk(2), _vec_chunk(1)],
        out_shape=[jax.ShapeDtypeStruct((LRU_NB, s, LRU_BW), BF16), jax.ShapeDtypeStruct((2, LRU_NB, s, LRU_BW), BF16),
                   jax.ShapeDtypeStruct((LRU_NB, s, LRU_BW), F32), jax.ShapeDtypeStruct((2, DM), F32),
                   jax.ShapeDtypeStruct((1, DM), F32)],
        compiler_params=_params(1))(dyp, z, bai, lam, u, gu, hs)


def _lru_layer_fwd(x, gpre, gpost, win, convw, convb, wai, bai, lam, wout, after=None):
    s = x.shape[0]
    tm = _tile(s, 512)
    h = _norm_fwd(x, gpre, after)
    gu = _bmm_nn("lru_in", h, win)
    u = _lru_conv_fwd(gu, convw, convb)
    z = _mm("lru_gate", u, wai, grid=(2, LRU_NB, s // tm, 1),
            a_spec=BS((None, tm, LRU_BW), lambda k, n, i, r: (n, i, 0)),
            b_spec=BS((None, None, LRU_BW, LRU_BW), lambda k, n, i, r: (k, n, 0, 0)),
            o_spec=BS((None, None, tm, LRU_BW), lambda k, n, i, r: (k, n, i, 0)),
            out_shape=(2, LRU_NB, s, LRU_BW), dn=NN)
    hs, yp = _lru_scan_fwd(z, bai, lam, u, gu)
    y = _mm_nn("lru_out", yp, wout)
    return _norm_res(x, y, gpost), (x, h, gu, u, z, hs, yp, y)


def _lru_layer_bwd(dxo, saved, gpre, gpost, win, convw, convb, wai, bai, lam, wout, after=None):
    x, h, gu, u, z, hs, yp, y = saved
    s = x.shape[0]
    tm = _tile(s, 512)
    dy, dgpost = _norm_bwd(y, dxo, gpost, None, BF16, after)
    dyp = _mm_nt("lru_dyp", dy, wout)
    dwout = _mm_tn("lru_dwout", yp, dy)
    dgate, dz, dud, dbai, dlam = _lru_scan_bwd(dyp, z, bai, lam, u, gu, hs)
    dwai = _mm("lru_dwai", u, dz, grid=(2, LRU_NB, s // tm),
               a_spec=BS((None, tm, LRU_BW), lambda k, n, r: (n, r, 0)),
               b_spec=BS((None, None, tm, LRU_BW), lambda k, n, r: (k, n, r, 0)),
               o_spec=BS((None, None, LRU_BW, LRU_BW), lambda k, n, r: (k, n, 0, 0)),
               out_shape=(2, LRU_NB, LRU_BW, LRU_BW), dn=TN)
    dug = _mm("lru_dug", dz, wai, grid=(LRU_NB, s // tm, 2),
              a_spec=BS((None, None, tm, LRU_BW), lambda n, i, k: (k, n, i, 0)),
              b_spec=BS((None, None, LRU_BW, LRU_BW), lambda n, i, k: (k, n, 0, 0)),
              o_spec=BS((None, tm, LRU_BW), lambda n, i, k: (n, i, 0)),
              out_shape=(LRU_NB, s, LRU_BW), dn=NT)
    duraw, dconvw, dconvb = _lru_conv_bwd(dud, dug, gu, convw)
    dgu = jnp.concatenate([dgate, duraw], axis=0)
    dwin = _bmm_tn("lru_dwin", h, dgu)
    dh = _bmm_nt_sum("lru_dh", dgu, win)
    dx, dgpre = _norm_bwd(x, dh, gpre, dxo, F32)
    return dx, dgpre, dgpost, dwin, dconvw, dconvb, dwai, dbai, dlam, dwout


CHIP_FLIPS = ((1, 0), (0, 1), (1, 1))


def _place():
    return lax.axis_index("x"), lax.axis_index("y"), lax.axis_index("c")


def _flip(v, f):
    return 1 - v if f else v


def _comm_params():
    return pltpu.CompilerParams(vmem_limit_bytes=VMEM_LIMIT)


def _all_gather(shards):
    n = len(shards)

    def body(*refs):
        ins, outs, stage = refs[:n], refs[n:2 * n], refs[2 * n:3 * n]
        send_sems, recv_sems, local_sems = refs[3 * n:]
        x, y, c = _place()
        me, sibling = (x, y, c), (x, y, 1 - c)
        chips = [(_flip(x, fx), _flip(y, fy)) for fx, fy in CHIP_FLIPS]

        def slot(t, p):
            return outs[t].at[:, 4 * p[0] + 2 * p[1] + p[2]]

        def copy(t, k, block, to, src=None):
            return pltpu.make_async_remote_copy(
                src_ref=slot(t, block) if src is None else src, dst_ref=slot(t, block),
                send_sem=send_sems.at[7 * t + k], recv_sem=recv_sems.at[7 * t + k], device_id=to, device_id_type=MESH)

        first = []
        for t in range(n):
            first.append(copy(t, 0, me, sibling, src=ins[t]))
            first += [copy(t, 1 + j, me, (*chip, c), src=ins[t]) for j, chip in enumerate(chips)]
        for cp in first:
            cp.start()
        load = [pltpu.make_async_copy(ins[t], stage[t], local_sems.at[t]) for t in range(n)]
        mine = [pltpu.make_async_copy(stage[t], slot(t, me), local_sems.at[t]) for t in range(n)]
        for cp in load:
            cp.start()
        for t in range(n):
            load[t].wait()
            mine[t].start()
        passed = []
        for j, chip in enumerate(chips):
            for t in range(n):
                copy(t, 1 + j, (*chip, c), me).wait_recv()
                fwd = copy(t, 4 + j, (*chip, c), sibling)
                fwd.start()
                passed.append(fwd)
        for t in range(n):
            copy(t, 0, sibling, me).wait_recv()
            for j, chip in enumerate(chips):
                copy(t, 4 + j, (*chip, 1 - c), me).wait_recv()
        for cp in first + passed:
            cp.wait_send()
        for cp in mine:
            cp.wait()

    outs = [jax.ShapeDtypeStruct((s.shape[0], NDEV) + s.shape[1:], s.dtype) for s in shards]
    return pl.pallas_call(body, name="all_gather", in_specs=[ANY] * n, out_specs=[ANY] * n, out_shape=outs,
                          scratch_shapes=[pltpu.VMEM(s.shape, s.dtype) for s in shards]
                          + [pltpu.SemaphoreType.DMA((7 * n,)), pltpu.SemaphoreType.DMA((7 * n,)),
                             pltpu.SemaphoreType.DMA((n,))],
                          compiler_params=_comm_params())(*shards)


def _small_gather(v):
    def body(v_ref, o_ref, send_sems, recv_sems, local_sem):
        x, y, c = _place()
        mine = 4 * x + 2 * y + c
        local = pltpu.make_async_copy(v_ref, o_ref.at[mine], local_sem)
        local.start()
        sends = []
        for k in range(1, NDEV):
            fx, fy, fc = (k >> 2) & 1, (k >> 1) & 1, k & 1
            sends.append(pltpu.make_async_remote_copy(
                src_ref=v_ref, dst_ref=o_ref.at[mine], send_sem=send_sems.at[k - 1], recv_sem=recv_sems.at[k - 1],
                device_id=(_flip(x, fx), _flip(y, fy), _flip(c, fc)), device_id_type=MESH))
        for cp in sends:
            cp.start()
        for k in range(1, NDEV):
            fx, fy, fc = (k >> 2) & 1, (k >> 1) & 1, k & 1
            src = 4 * _flip(x, fx) + 2 * _flip(y, fy) + _flip(c, fc)
            pltpu.make_async_remote_copy(src_ref=v_ref, dst_ref=o_ref.at[src], send_sem=send_sems.at[k - 1],
                                         recv_sem=recv_sems.at[k - 1], device_id=(x, y, c), device_id_type=MESH).wait_recv()
        for cp in sends:
            cp.wait_send()
        local.wait()

    return pl.pallas_call(body, name="small_gather", in_specs=[ANY], out_specs=ANY,
                          out_shape=jax.ShapeDtypeStruct((NDEV,) + v.shape, v.dtype),
                          scratch_shapes=[pltpu.SemaphoreType.DMA((NDEV - 1,)), pltpu.SemaphoreType.DMA((NDEV - 1,)),
                                          pltpu.SemaphoreType.DMA],
                          compiler_params=_comm_params())(v)


REL_CHIPS = ((0, 0),) + CHIP_FLIPS


def _rs_d2d(g5s):
    n = len(g5s)

    def body(*refs):
        ins, gots = refs[:n], refs[n:2 * n]
        send_sems, recv_sems = refs[2 * n:]
        x, y, c = _place()
        copies = []
        for t in range(n):
            for f, (fx, fy) in enumerate(REL_CHIPS):
                copies.append(pltpu.make_async_remote_copy(
                    src_ref=ins[t].at[_flip(x, fx), _flip(y, fy), 1 - c], dst_ref=gots[t].at[f],
                    send_sem=send_sems.at[4 * t + f], recv_sem=recv_sems.at[4 * t + f], device_id=(x, y, 1 - c),
                    device_id_type=MESH))
        for cp in copies:
            cp.start()
        for cp in copies:
            cp.wait()

    out = [jax.ShapeDtypeStruct((4,) + g.shape[3:], F32) for g in g5s]
    return pl.pallas_call(body, name="rs_d2d", in_specs=[ANY] * n, out_specs=[ANY] * n, out_shape=out,
                          scratch_shapes=[pltpu.SemaphoreType.DMA((4 * n,)), pltpu.SemaphoreType.DMA((4 * n,))],
                          compiler_params=_comm_params())(*g5s)


def _rs_ici(parts):
    n = len(parts)

    def body(*refs):
        ins, outs = refs[:n], refs[n:2 * n]
        send_sems, recv_sems = refs[2 * n:]
        x, y, c = _place()
        copies = []
        for t in range(n):
            for f, (fx, fy) in enumerate(CHIP_FLIPS):
                copies.append(pltpu.make_async_remote_copy(
                    src_ref=ins[t].at[f], dst_ref=outs[t].at[f], send_sem=send_sems.at[3 * t + f],
                    recv_sem=recv_sems.at[3 * t + f], device_id=(_flip(x, fx), _flip(y, fy), c), device_id_type=MESH))
        for cp in copies:
            cp.start()
        for cp in copies:
            cp.wait()

    out = [jax.ShapeDtypeStruct(p.shape, p.dtype) for p in parts]
    return pl.pallas_call(body, name="rs_ici", in_specs=[ANY] * n, out_specs=[ANY] * n, out_shape=out,
                          scratch_shapes=[pltpu.SemaphoreType.DMA((3 * n,)), pltpu.SemaphoreType.DMA((3 * n,))],
                          compiler_params=_comm_params())(*parts)


HBM = pl.BlockSpec(memory_space=pltpu.HBM)
SEM = pl.BlockSpec(memory_space=pltpu.SEMAPHORE)
EFFECT = pltpu.SideEffectType.DATAFLOW_SIDE_EFFECTING


def _in_hbm(a):
    return pltpu.with_memory_space_constraint(a, pltpu.HBM)


def _rs_ici_copies(ins, lands, send_sems, recv_sems):
    x, y, c = _place()
    return [pltpu.make_async_remote_copy(
        src_ref=ins[t].at[f], dst_ref=lands[t].at[f], send_sem=send_sems.at[3 * t + f], recv_sem=recv_sems.at[3 * t + f],
        device_id=(_flip(x, fx), _flip(y, fy), c), device_id_type=MESH)
        for t in range(len(ins)) for f, (fx, fy) in enumerate(CHIP_FLIPS)]


def _rs_ici_start(parts, name):
    n = len(parts)

    def body(*refs):
        ins, lands = refs[:n], refs[n:2 * n]
        send_sems, recv_sems = refs[2 * n], refs[2 * n + 1]
        token = refs[-1]
        for cp in _rs_ici_copies(ins, lands, send_sems, recv_sems):
            cp.start()
        token[...] = jnp.zeros(token.shape, token.dtype)

    thru = [pltpu.HBM(p.shape, p.dtype) for p in parts]
    res = pl.pallas_call(
        body, name=name, in_specs=[HBM] * (2 * n),
        out_shape=(pltpu.SemaphoreType.DMA((3 * n,)), pltpu.SemaphoreType.DMA((3 * n,)), *thru, *thru,
                   jax.ShapeDtypeStruct((8, LANE), F32)),
        out_specs=(SEM, SEM, *([HBM] * (2 * n)), pl.BlockSpec(memory_space=pltpu.VMEM)),
        input_output_aliases={i: 2 + i for i in range(2 * n)},
        compiler_params=pltpu.CompilerParams(has_side_effects=EFFECT),
    )(*[_in_hbm(p) for p in parts], *[_in_hbm(lax.empty(p.shape, p.dtype)) for p in parts])
    return res[:-1], res[-1]


def _rs_ici_wait(state, after, name):
    n = (len(state) - 2) // 2

    def body(*refs):
        send_sems, recv_sems = refs[0], refs[1]
        ins, lands = refs[2:2 + n], refs[2 + n:2 + 2 * n]
        for cp in _rs_ici_copies(ins, lands, send_sems, recv_sems):
            cp.wait_send()
            cp.wait_recv()

    thru = [pltpu.HBM(s.shape, s.dtype) for s in state[2:]]
    res = pl.pallas_call(
        body, name=name, in_specs=[SEM, SEM] + [HBM] * (2 * n) + [ANY], out_shape=tuple(thru),
        out_specs=tuple([HBM] * (2 * n)), input_output_aliases={2 + i: i for i in range(2 * n)},
        compiler_params=pltpu.CompilerParams(has_side_effects=EFFECT),
    )(*state, after)
    return list(res[n:])


def _ag_copies(shards, lands, send_sems, recv_sems):
    x, y, c = _place()
    mine = 4 * x + 2 * y + c
    peers = [(x, y, 1 - c)] + [(_flip(x, fx), _flip(y, fy), c) for fx, fy in CHIP_FLIPS]
    return [pltpu.make_async_remote_copy(
        src_ref=shards[t], dst_ref=lands[t].at[mine], send_sem=send_sems.at[4 * t + k], recv_sem=recv_sems.at[4 * t + k],
        device_id=peer, device_id_type=MESH) for t in range(len(shards)) for k, peer in enumerate(peers)]


def _ag_start(shards, after, name):
    n = len(shards)

    def body(*refs):
        ins, lands = refs[:n], refs[n:2 * n]
        send_sems, recv_sems = refs[2 * n + 1], refs[2 * n + 2]
        token = refs[-1]
        for cp in _ag_copies(ins, lands, send_sems, recv_sems):
            cp.start()
        token[...] = jnp.zeros(token.shape, token.dtype)

    thru = [pltpu.HBM(s.shape, s.dtype) for s in shards]
    land = [pltpu.HBM((NDEV,) + s.shape, s.dtype) for s in shards]
    res = pl.pallas_call(
        body, name=name, in_specs=[HBM] * (2 * n) + [ANY],
        out_shape=(pltpu.SemaphoreType.DMA((4 * n,)), pltpu.SemaphoreType.DMA((4 * n,)), *thru, *land,
                   jax.ShapeDtypeStruct((8, LANE), F32)),
        out_specs=(SEM, SEM, *([HBM] * (2 * n)), pl.BlockSpec(memory_space=pltpu.VMEM)),
        input_output_aliases={i: 2 + i for i in range(2 * n)},
        compiler_params=pltpu.CompilerParams(has_side_effects=EFFECT),
    )(*[_in_hbm(s) for s in shards], *[_in_hbm(lax.empty((NDEV,) + s.shape, s.dtype)) for s in shards], after)
    return res[:-1], res[-1]


def _ag_wait(state, after, name):
    n = (len(state) - 2) // 2

    def body(*refs):
        send_sems, recv_sems = refs[0], refs[1]
        ins, lands = refs[2:2 + n], refs[2 + n:2 + 2 * n]
        for cp in _ag_copies(ins, lands, send_sems, recv_sems):
            cp.wait_send()
            cp.wait_recv()

    thru = [pltpu.HBM(s.shape, s.dtype) for s in state[2:]]
    res = pl.pallas_call(
        body, name=name, in_specs=[SEM, SEM] + [HBM] * (2 * n) + [ANY], out_shape=tuple(thru),
        out_specs=tuple([HBM] * (2 * n)), input_output_aliases={2 + i: i for i in range(2 * n)},
        compiler_params=pltpu.CompilerParams(has_side_effects=EFFECT),
    )(*state, after)
    return list(res[:n]), list(res[n:])


def _ag_finish(shards, lands):
    n = len(shards)

    def body(*refs):
        ins, outs, stage = refs[:n], refs[2 * n:3 * n], refs[3 * n:4 * n]
        send_sems, recv_sems, local_sems = refs[4 * n:]
        x, y, c = _place()
        chips = [(_flip(x, fx), _flip(y, fy)) for fx, fy in CHIP_FLIPS]

        def passing(t, j, core, to):
            blk = outs[t].at[4 * chips[j][0] + 2 * chips[j][1] + core]
            return pltpu.make_async_remote_copy(src_ref=blk, dst_ref=blk, send_sem=send_sems.at[3 * t + j],
                                                recv_sem=recv_sems.at[3 * t + j], device_id=to, device_id_type=MESH)

        sends = [passing(t, j, c, (x, y, 1 - c)) for t in range(n) for j in range(3)]
        for cp in sends:
            cp.start()
        load = [pltpu.make_async_copy(ins[t], stage[t], local_sems.at[t]) for t in range(n)]
        mine = [pltpu.make_async_copy(stage[t], outs[t].at[4 * x + 2 * y + c], local_sems.at[t]) for t in range(n)]
        for cp in load:
            cp.start()
        for t in range(n):
            load[t].wait()
            mine[t].start()
        for t in range(n):
            for j in range(3):
                passing(t, j, 1 - c, (x, y, c)).wait_recv()
        for cp in sends:
            cp.wait_send()
        for cp in mine:
            cp.wait()

    return pl.pallas_call(
        body, name="ag_finish", in_specs=[ANY] * (2 * n), out_specs=[ANY] * n,
        out_shape=[jax.ShapeDtypeStruct(l.shape, l.dtype) for l in lands],
        input_output_aliases={n + i: i for i in range(n)},
        scratch_shapes=[pltpu.VMEM(s.shape, s.dtype) for s in shards]
        + [pltpu.SemaphoreType.DMA((3 * n,)), pltpu.SemaphoreType.DMA((3 * n,)), pltpu.SemaphoreType.DMA((n,))],
        compiler_params=_comm_params())(*shards, *lands)


def _row_tile(rows):
    for t in (256, 128, 64, 32, 16, 8):
        if rows % t == 0:
            return t
    return rows


def _rs_chip_sum(pos, g5, got):
    a, b = g5.shape[3:]
    ta = _row_tile(a)

    def kern(pos_ref, o_ref, g_ref, p_ref):
        p_ref[...] = (o_ref[...] + g_ref[...]).astype(BF16)

    def mine(f, i, pos_ref):
        return (pos_ref[0] ^ ((f + 1) & 1), pos_ref[1] ^ ((f + 1) >> 1), pos_ref[2], i, 0)

    spec = pltpu.PrefetchScalarGridSpec(
        num_scalar_prefetch=1, grid=(3, a // ta),
        in_specs=[BS((None, None, None, ta, b), mine), BS((None, ta, b), lambda f, i, pos_ref: (f + 1, i, 0))],
        out_specs=BS((None, ta, b), lambda f, i, pos_ref: (f, i, 0)))
    return pl.pallas_call(kern, name="rs_chip_sum", grid_spec=spec, out_shape=jax.ShapeDtypeStruct((3, a, b), BF16),
                          compiler_params=_params(2))(pos, g5, got)


def _rs_final_sum(pos, g5, got, recv):
    a, b = g5.shape[3:]
    ta = _row_tile(a)

    def kern(pos_ref, o_ref, g_ref, r_ref, s_ref):
        acc = o_ref[...] + g_ref[...]
        for f in range(3):
            acc = acc + r_ref[f].astype(F32)
        s_ref[...] = acc

    spec = pltpu.PrefetchScalarGridSpec(
        num_scalar_prefetch=1, grid=(a // ta,),
        in_specs=[BS((None, None, None, ta, b), lambda i, pos_ref: (pos_ref[0], pos_ref[1], pos_ref[2], i, 0)),
                  BS((None, ta, b), lambda i, pos_ref: (0, i, 0)), BS((3, ta, b), lambda i, pos_ref: (0, i, 0))],
        out_specs=BS((ta, b), lambda i, pos_ref: (i, 0)))
    return pl.pallas_call(kern, name="rs_final_sum", grid_spec=spec, out_shape=jax.ShapeDtypeStruct((a, b), F32),
                          compiler_params=_params(1))(pos, g5, got, recv)


def _reduce_scatter(grads, pos):
    g5s = [g.reshape((2, 2, 2) + g.shape[1:]) for g in grads]
    gots = _rs_d2d(g5s)
    parts = [_rs_chip_sum(pos, g, got) for g, got in zip(g5s, gots)]
    recvs = _rs_ici(parts)
    return [_rs_final_sum(pos, g, got, r) for g, got, r in zip(g5s, gots, recvs)]


def _rs_d2d_copies(ins, lands, send_sems, recv_sems):
    x, y, c = _place()
    return [pltpu.make_async_remote_copy(
        src_ref=ins[t].at[_flip(x, fx), _flip(y, fy), 1 - c], dst_ref=lands[t].at[f], send_sem=send_sems.at[4 * t + f],
        recv_sem=recv_sems.at[4 * t + f], device_id=(x, y, 1 - c), device_id_type=MESH)
        for t in range(len(ins)) for f, (fx, fy) in enumerate(REL_CHIPS)]


def _rs_d2d_start(g5s, name):
    n = len(g5s)

    def body(*refs):
        ins, lands = refs[:n], refs[n:2 * n]
        for cp in _rs_d2d_copies(ins, lands, refs[2 * n], refs[2 * n + 1]):
            cp.start()
        refs[-1][...] = jnp.zeros(refs[-1].shape, F32)

    thru = [pltpu.HBM(g.shape, g.dtype) for g in g5s]
    land = [pltpu.HBM((4,) + g.shape[3:], F32) for g in g5s]
    res = pl.pallas_call(
        body, name=name, in_specs=[HBM] * (2 * n),
        out_shape=(pltpu.SemaphoreType.DMA((4 * n,)), pltpu.SemaphoreType.DMA((4 * n,)), *thru, *land,
                   jax.ShapeDtypeStruct((8, LANE), F32)),
        out_specs=(SEM, SEM, *([HBM] * (2 * n)), pl.BlockSpec(memory_space=pltpu.VMEM)),
        input_output_aliases={i: 2 + i for i in range(2 * n)},
        compiler_params=pltpu.CompilerParams(has_side_effects=EFFECT),
    )(*[_in_hbm(g) for g in g5s], *[_in_hbm(lax.empty((4,) + g.shape[3:], F32)) for g in g5s])
    return res[:-1], res[-1]


def _rs_d2d_wait(state, after, name):
    n = (len(state) - 2) // 2

    def body(*refs):
        ins, lands = refs[2:2 + n], refs[2 + n:2 + 2 * n]
        for cp in _rs_d2d_copies(ins, lands, refs[0], refs[1]):
            cp.wait_send()
            cp.wait_recv()

    thru = [pltpu.HBM(s.shape, s.dtype) for s in state[2:]]
    res = pl.pallas_call(
        body, name=name, in_specs=[SEM, SEM] + [HBM] * (2 * n) + [ANY], out_shape=tuple(thru),
        out_specs=tuple([HBM] * (2 * n)), input_output_aliases={2 + i: i for i in range(2 * n)},
        compiler_params=pltpu.CompilerParams(has_side_effects=EFFECT),
    )(*state, after)
    return list(res[:n]), list(res[n:])


def _as_g5(grads):
    return [g.reshape((2, 2, 2) + g.shape[1:]) for g in grads]


def _rs_mid(g5s, gots, pos, tag):
    parts = [_rs_chip_sum(pos, g, got) for g, got in zip(g5s, gots)]
    state, token = _rs_ici_start(parts, "rs_ici_start_" + tag)
    return (g5s, gots, state, tag), token


def _rs_begin(grads, pos, tag):
    g5s = _as_g5(grads)
    return _rs_mid(g5s, _rs_d2d(g5s), pos, tag)


def _rs_end(pending, after, pos):
    g5s, gots, state, tag = pending
    recvs = _rs_ici_wait(state, after, "rs_ici_wait_" + tag)
    return [_rs_final_sum(pos, g, got, r) for g, got, r in zip(g5s, gots, recvs)]


def _sum_devices(v):
    _, r, _ = v.shape

    def kern(v_ref, o_ref):
        acc = v_ref[0]
        for d in range(1, NDEV):
            acc = acc + v_ref[d]
        o_ref[...] = acc

    return pl.pallas_call(kern, name="sum_devices", out_shape=jax.ShapeDtypeStruct((r, LANE), F32),
                          compiler_params=_comm_params())(v)


def _loss_head(xf, target):
    s = xf.shape[0]
    tm = _tile(s, 512)

    def kern(x_ref, t_ref, dx_ref, l_ref):
        err = x_ref[...] - t_ref[...]
        dx_ref[...] = err * (1.0 / DM)
        part = jnp.broadcast_to(0.5 * jnp.sum(jnp.mean(err * err, axis=-1, keepdims=True), axis=0, keepdims=True), (8, LANE))

        @pl.when(pl.program_id(0) == 0)
        def _():
            l_ref[...] = part

        @pl.when(pl.program_id(0) > 0)
        def _():
            l_ref[...] += part

    row = BS((tm, DM), lambda i: (i, 0))
    return pl.pallas_call(kern, name="loss_head", grid=(s // tm,), in_specs=[row, row],
                          out_specs=[row, BS((8, LANE), lambda i: (0, 0))],
                          out_shape=[jax.ShapeDtypeStruct((s, DM), F32), jax.ShapeDtypeStruct((8, LANE), F32)],
                          compiler_params=_params(1))(xf, target)


def _adamw(w, g, m, v, after=None):
    rows, cols = w.shape
    tr = _row_tile(rows)
    extra = () if after is None else (after,)

    def kern(w_ref, g_ref, m_ref, v_ref, *rest):
        d_ref, nm_ref, nv_ref = rest[-3:]
        gv = g_ref[...]
        nm = ADAM_B1 * m_ref[...] + (1.0 - ADAM_B1) * gv
        nv = ADAM_B2 * v_ref[...] + (1.0 - ADAM_B2) * (gv * gv)
        m_hat = nm / (1.0 - ADAM_B1 ** ADAM_STEP)
        v_hat = nv / (1.0 - ADAM_B2 ** ADAM_STEP)
        d_ref[...] = -ADAM_LR * (m_hat / (jnp.sqrt(v_hat) + ADAM_EPS) + ADAM_WD * w_ref[...])
        nm_ref[...] = nm
        nv_ref[...] = nv

    blk = BS((tr, cols), lambda i: (i, 0))
    shp = jax.ShapeDtypeStruct((rows, cols), F32)
    return pl.pallas_call(kern, name="adamw", grid=(rows // tr,), in_specs=[blk] * 4 + [ANY] * len(extra),
                          out_specs=[blk] * 3, out_shape=[shp] * 3, compiler_params=_params(1))(w, g, m, v, *extra)


def _adamw_nd(w, g, m, v, after=None):
    shape = w.shape
    two = (math.prod(shape[:-1]), shape[-1])
    return tuple(o.reshape(shape)
                 for o in _adamw(w.reshape(two), g.reshape(two), m.reshape(two), v.reshape(two), after))


def _pack_small(parts):
    flat = jnp.concatenate([p.reshape(-1) for p in parts])
    pad = (-flat.shape[0]) % (8 * LANE)
    return jnp.pad(flat, (0, pad)).reshape(-1, LANE)


def _unpack_small(packed, shapes, lead=()):
    flat = packed.reshape(lead + (-1,))
    out, off = [], 0
    for shp in shapes:
        n = math.prod(shp)
        out.append(flat[..., off:off + n].reshape(lead + tuple(shp)))
        off += n
    return out


def _blocks_of_columns(w):
    k, n = w.shape
    return w.reshape(k, NDEV, n // NDEV).transpose(1, 0, 2)


def _columns_of_blocks(wb):
    n, k, c = wb.shape
    return wb.transpose(1, 0, 2).reshape(k, n * c)


WEIGHT_NAMES = ('g_mix_pre', 'g_mix_post', 'g_cross_pre', 'g_mem', 'g_cross_post', 'g_ffn_pre', 'g_ffn_post', 'w_xq',
                'w_xkv', 'w_xo', 'w_ffn_gu', 'w_ffn_down', 'ab_w_in', 'ab_b_f', 'ab_conv_w', 'ab_w_out', 'c_w_in',
                'c_conv_w', 'c_conv_b', 'c_w_a', 'c_b_a', 'c_w_i', 'c_b_i', 'c_lam', 'c_w_out')
BIG = ('w_xq', 'w_xkv', 'w_xo', 'w_ffn_gu', 'w_ffn_down', 'ab_w_in', 'ab_w_out', 'c_w_in', 'c_w_a', 'c_w_i', 'c_w_out')
SMALL_SHARDED = ('ab_conv_w', 'c_conv_w', 'c_conv_b', 'c_b_a', 'c_b_i', 'c_lam')
REPLICATED = ('g_mix_pre', 'g_mix_post', 'g_cross_pre', 'g_mem', 'g_cross_post', 'g_ffn_pre', 'g_ffn_post', 'ab_b_f')


def _small_full(name, gathered):
    nd = gathered.ndim
    return jnp.moveaxis(gathered, 0, nd - 2).reshape(gathered.shape[1:-1] + (NDEV * gathered.shape[-1],))


def _small_shard(full, dev):
    c = full.shape[-1] // NDEV
    return lax.dynamic_slice_in_dim(full, dev * c, c, axis=full.ndim - 1)


def kernel(x, mem, g_mix_pre, g_mix_post, g_cross_pre, g_mem, g_cross_post, g_ffn_pre, g_ffn_post, w_xq, w_xkv, w_xo, w_ffn_gu, w_ffn_down, ab_w_in, ab_b_f, ab_conv_w, ab_w_out, c_w_in, c_conv_w, c_conv_b, c_w_a, c_b_a, c_w_i, c_b_i, c_lam, c_w_out, loss_target, m_g_mix_pre, m_g_mix_post, m_g_cross_pre, m_g_mem, m_g_cross_post, m_g_ffn_pre, m_g_ffn_post, m_w_xq, m_w_xkv, m_w_xo, m_w_ffn_gu, m_w_ffn_down, m_ab_w_in, m_ab_b_f, m_ab_conv_w, m_ab_w_out, m_c_w_in, m_c_conv_w, m_c_conv_b, m_c_w_a, m_c_b_a, m_c_w_i, m_c_b_i, m_c_lam, m_c_w_out, v_g_mix_pre, v_g_mix_post, v_g_cross_pre, v_g_mem, v_g_cross_post, v_g_ffn_pre, v_g_ffn_post, v_w_xq, v_w_xkv, v_w_xo, v_w_ffn_gu, v_w_ffn_down, v_ab_w_in, v_ab_b_f, v_ab_conv_w, v_ab_w_out, v_c_w_in, v_c_conv_w, v_c_conv_b, v_c_w_a, v_c_b_a, v_c_w_i, v_c_b_i, v_c_lam, v_c_w_out):
    args = locals()
    w = {n: args[n] for n in WEIGHT_NAMES}
    mom = {n: args["m_" + n] for n in WEIGHT_NAMES}
    var = {n: args["v_" + n] for n in WEIGHT_NAMES}
    pos = jnp.stack([lax.axis_index("x"), lax.axis_index("y"), lax.axis_index("c")]).astype(jnp.int32)
    dev = 4 * pos[0] + 2 * pos[1] + pos[2]
    xs, mems, target = x[0], mem[0], loss_target[0]
    n_even, n_odd = (DEPTH + 1) // 2, DEPTH // 2

    small_shapes = [w[n].shape for n in SMALL_SHARDED]
    gathered_small = _unpack_small(_small_gather(_pack_small([w[n] for n in SMALL_SHARDED])), small_shapes, (NDEV,))
    small = {n: _small_full(n, g) for n, g in zip(SMALL_SHARDED, gathered_small)}
    ab_bfb = jnp.broadcast_to(ab_b_f[:, :, None], (n_even, FOX_H, LANE))
    c_bai = jnp.stack([small['c_b_a'].reshape(n_odd, DM), small['c_b_i'].reshape(n_odd, DM)], axis=1)
    row = lambda a, l: a[l][None]

    REST = ('w_xq', 'w_xkv', 'w_xo', 'w_ffn_gu', 'w_ffn_down')

    def mixer_names(l):
        return ('ab_w_in', 'ab_w_out') if l % 2 == 0 else ('c_w_in', 'c_w_a', 'c_w_i', 'c_w_out')

    def shards_of(l, names):
        out = []
        for n in names:
            s = w[n][l if w[n].shape[0] == DEPTH else l // 2].astype(BF16)
            out.append(s.reshape(-1, s.shape[-1]))
        return out

    def mixer_weights(l, full):
        if l % 2 == 0:
            e = l // 2
            return (row(g_mix_pre, l), row(g_mix_post, l), _ab_pack(_columns_of_blocks(full['ab_w_in'])), ab_bfb[e],
                    small['ab_conv_w'][e], full['ab_w_out'].reshape(DM, DM))
        o = l // 2
        gate_w = lambda g: g.reshape(NDEV, LRU_NB, LRU_BW // NDEV, LRU_BW).transpose(1, 0, 2, 3).reshape(
            LRU_NB, LRU_BW, LRU_BW)
        return (row(g_mix_pre, l), row(g_mix_post, l), full['c_w_in'], small['c_conv_w'][o], row(small['c_conv_b'], o),
                jnp.stack([gate_w(full['c_w_a']), gate_w(full['c_w_i'])]), c_bai[o], row(small['c_lam'], o),
                full['c_w_out'].reshape(DM, DM))

    def rest_weights(l, full):
        cross = (row(g_cross_pre, l), row(g_mem, l), row(g_cross_post, l), full['w_xq'].reshape(DM, DM), full['w_xkv'],
                 full['w_xo'].reshape(DM, DM))
        ffn = (row(g_ffn_pre, l), row(g_ffn_post, l), full['w_ffn_gu'], full['w_ffn_down'].reshape(D_FF, DM))
        return cross, ffn

    def gathered(state, names, after, tag):
        shards, lands = _ag_wait(state, after, "ag_wait_" + tag)
        full = _ag_finish(shards, lands)
        return dict(zip(names, full)), full[0]

    saved, weights = [], []
    h = xs
    st_m, _ = _ag_start(shards_of(0, mixer_names(0)), xs, "ag_start_0m")
    st_r, token = _ag_start(shards_of(0, REST), st_m[2], "ag_start_0r")
    full_m, _ = gathered(st_m, mixer_names(0), xs, "0m")
    for l in range(DEPTH):
        if l > 0:
            full, done = gathered(state, mixer_names(l) + REST, h, str(l))
            full_m = full_r = full
            token = None
            if l + 1 < DEPTH:
                state, token = _ag_start(shards_of(l + 1, mixer_names(l + 1) + REST), done, "ag_start_%d" % (l + 1))
        mixer = mixer_weights(l, full_m)
        h, s_mix = (_fox_layer_fwd if l % 2 == 0 else _lru_layer_fwd)(h, *mixer, after=token)
        token = None
        if l == 0:
            full_r, done = gathered(st_r, REST, h, "0r")
            state, token = _ag_start(shards_of(1, mixer_names(1) + REST), done, "ag_start_1")
        cross, ffn = rest_weights(l, full_r)
        h, s_cross = _cross_fwd(h, mems, *cross, after=token)
        h, s_ffn = _ffn_fwd(h, *ffn)
        saved.append((s_mix, s_cross, s_ffn))
        weights.append((mixer, cross, ffn))
    mixer_args = lambda l: weights[l][0]
    cross_args = lambda l: weights[l][1]
    ffn_args = lambda l: weights[l][2]
    dx, loss_rep = _loss_head(h, target)
    loss = lax.psum(loss_rep[0, 0], ("x", "y", "c"))

    grads = {n: [None] * w[n].shape[0] for n in BIG}
    partial = {n: [None] * w[n].shape[0] for n in REPLICATED + SMALL_SHARDED}
    def finish(pending, after):
        state, names, where = pending
        for n, g in zip(names, _rs_end(state, after, pos)):
            grads[n][where[n]] = g

    def unit(layer, names):
        return [layer[n][1] for n in names], names, {n: layer[n][0] for n in names}

    d2d = ici = None
    token = None
    for l in reversed(range(DEPTH)):
        s_mix, s_cross, s_ffn = saved[l]
        dx, partial['g_ffn_pre'][l], partial['g_ffn_post'][l], dwgu, dwd = _ffn_bwd(dx, s_ffn, *ffn_args(l), after=token)
        token = None
        if d2d is not None:
            g5s, gots = _rs_d2d_wait(d2d[0], dx, "rs_d2d_wait_%d" % (l + 1))
            state, token = _rs_mid(g5s, gots, pos, str(l + 1))
            ici, d2d = (state,) + d2d[1:], None
        (dx, partial['g_cross_pre'][l], partial['g_mem'][l], partial['g_cross_post'][l], dwq, dwkv, dwo) = _cross_bwd(
            dx, s_cross, mems, *cross_args(l), after=token)
        token = None
        layer = {'w_xq': (l, dwq.reshape(NDEV, DM // NDEV, DM)), 'w_xkv': (l, dwkv), 'w_xo': (l, dwo.reshape(NDEV, DM // NDEV, DM)),
                 'w_ffn_gu': (l, dwgu), 'w_ffn_down': (l, dwd.reshape(NDEV, D_FF // NDEV, DM))}
        if l == 0:
            gs, names, where = unit(layer, REST)
            state, token = _rs_begin(gs, pos, "0r")
            ici_rest = (state, names, where)
        if l % 2 == 0:
            e = l // 2
            (dx, partial['g_mix_pre'][l], partial['g_mix_post'][l], dwall, partial['ab_b_f'][e], partial['ab_conv_w'][e],
             dwout) = _fox_layer_bwd(dx, s_mix, *mixer_args(l), after=token)
            layer['ab_w_in'] = (e, _blocks_of_columns(_ab_unpack(dwall)))
            layer['ab_w_out'] = (e, dwout.reshape(NDEV, DM // NDEV, DM))
        else:
            o = l // 2
            (dx, partial['g_mix_pre'][l], partial['g_mix_post'][l], dwin, partial['c_conv_w'][o], dconvb, dwai, dbai, dlam,
             dwout) = _lru_layer_bwd(dx, s_mix, *mixer_args(l), after=token)
            partial['c_conv_b'][o], partial['c_lam'][o] = dconvb[0], dlam[0]
            partial['c_b_a'][o], partial['c_b_i'][o] = dbai[0].reshape(LRU_NB, LRU_BW), dbai[1].reshape(LRU_NB, LRU_BW)
            rows = LRU_BW // NDEV
            by_dev = lambda d: d.reshape(LRU_NB, NDEV, rows, LRU_BW).transpose(1, 0, 2, 3).reshape(NDEV, LRU_NB * rows, LRU_BW)
            layer['c_w_in'] = (o, dwin)
            layer['c_w_a'] = (o, by_dev(dwai[0]))
            layer['c_w_i'] = (o, by_dev(dwai[1]))
            layer['c_w_out'] = (o, dwout.reshape(NDEV, DM // NDEV, DM))
        token = None
        if ici is not None:
            finish(ici, dx)
            ici = None
        if l > 0:
            gs, names, where = unit(layer, list(layer))
            state, token = _rs_d2d_start(_as_g5(gs), "rs_d2d_start_%d" % l)
            d2d = (state, names, where)
    gs, names, where = unit(layer, mixer_names(0))
    state, token = _rs_begin(gs, pos, "0m")
    ici_mixer = (state, names, where)
    finish(ici_rest, dx)

    small_names = REPLICATED + SMALL_SHARDED
    small_parts = [jnp.stack([p.reshape(w[n].shape[1:] if n in REPLICATED else small[n].shape[1:]) for p in partial[n]])
                   for n in small_names]
    reduced = _unpack_small(_sum_devices(_small_gather(_pack_small(small_parts))), [p.shape for p in small_parts])
    grad = {}
    for n, g in zip(small_names, reduced):
        grad[n] = g if n in REPLICATED else _small_shard(g, dev)

    delta, new_m, new_v = {}, {}, {}
    last = mixer_names(0)
    for n in BIG:
        if n not in last:
            grad[n] = jnp.stack(grads[n]).reshape(w[n].shape)
            delta[n], new_m[n], new_v[n] = _adamw_nd(w[n], grad[n], mom[n], var[n], token)
            token = delta[n]
    shapes = [w[n].shape for n in small_names]
    packed = [_pack_small([t[n] for n in small_names]) for t in (w, grad, mom, var)]
    res_small = _adamw(*packed, after=token)
    for res, out in zip(res_small, (delta, new_m, new_v)):
        for n, val in zip(small_names, _unpack_small(res, shapes)):
            out[n] = val
    finish(ici_mixer, res_small[0])
    for n in last:
        grad[n] = jnp.stack(grads[n]).reshape(w[n].shape)
        delta[n], new_m[n], new_v[n] = _adamw_nd(w[n], grad[n], mom[n], var[n])

    return (loss, dx[None], *[grad[n] for n in WEIGHT_NAMES], *[delta[n] for n in WEIGHT_NAMES],
            *[new_m[n] for n in WEIGHT_NAMES], *[new_v[n] for n in WEIGHT_NAMES])
```

```python
import functools
import math

import jax
import jax.numpy as jnp
from jax import lax
from jax.experimental import pallas as pl
from jax.experimental.pallas import tpu as pltpu

F32 = jnp.float32
BF16 = jnp.bfloat16
BS = pl.BlockSpec
ANY = pl.BlockSpec(memory_space=pl.ANY)
MESH = pl.DeviceIdType.MESH

DM = 1024
DEPTH = 4
EPS = 1e-6
NEG = -1e30
FOX_W = 512
FOX_HD = 64
FOX_H = 8
SC_W = 512
SC_K = 3
AB_IN = 3 * FOX_W + FOX_H + 3 * SC_W
AB_PAD = 3200
LRU_BW = 256
LRU_NB = 4
RG_K = 4
RG_C = 8.0
MEM_H = 4
MEM_HD = 256
D_FF = 2816
NDEV = 8
FFB = 2 * D_FF // NDEV
ADAM_LR, ADAM_B1, ADAM_B2, ADAM_EPS, ADAM_WD, ADAM_STEP = 0.001, 0.9, 0.999, 1e-08, 0.01, 10

LANE = 128
VMEM_LIMIT = 48 * 1024 * 1024


def _params(ngrid):
    return pltpu.CompilerParams(dimension_semantics=("arbitrary",) * ngrid, vmem_limit_bytes=VMEM_LIMIT)


TK_RED = 2048
TM_SUM = 1024


def _tile(n, t):
    return t if n % t == 0 else n


def _mm(name, a, b, *, grid, a_spec, b_spec, o_spec, out_shape, dn, out_dtype=F32):
    nred = grid[-1]
    ngrid = len(grid)

    def kern(a_ref, b_ref, o_ref, *scratch):
        p = lax.dot_general(a_ref[...].astype(BF16), b_ref[...].astype(BF16), (dn, ((), ())),
                            preferred_element_type=F32)
        if nred == 1:
            o_ref[...] = p.astype(o_ref.dtype)
            return
        acc = scratch[0] if scratch else o_ref
        r = pl.program_id(ngrid - 1)

        @pl.when(r == 0)
        def _():
            acc[...] = p

        @pl.when(r > 0)
        def _():
            acc[...] += p

        if scratch:
            @pl.when(r == nred - 1)
            def _():
                o_ref[...] = acc[...].astype(o_ref.dtype)

    blk = tuple(d for d in o_spec.block_shape if d is not None)
    scratch = [pltpu.VMEM(blk, F32)] if (nred > 1 and out_dtype != F32) else []
    return pl.pallas_call(kern, name=name, grid=grid, in_specs=[a_spec, b_spec], out_specs=o_spec,
                          out_shape=jax.ShapeDtypeStruct(out_shape, out_dtype), scratch_shapes=scratch,
                          compiler_params=_params(ngrid))(a, b)


NN = ((1,), (0,))
NT = ((1,), (1,))
TN = ((0,), (0,))


def _mm_nn(name, a, w, out_dtype=F32, tn=None):
    m, k = a.shape
    n = w.shape[1]
    tm = _tile(m, 512)
    tn = n if tn is None else tn
    return _mm(name, a, w, grid=(m // tm, n // tn, 1), a_spec=BS((tm, k), lambda i, j, r: (i, 0)),
               b_spec=BS((k, tn), lambda i, j, r: (0, j)), o_spec=BS((tm, tn), lambda i, j, r: (i, j)),
               out_shape=(m, n), dn=NN, out_dtype=out_dtype)


def _mm_nt(name, a, w, out_dtype=F32, tn=None):
    m, n = a.shape
    k = w.shape[0]
    tm = _tile(m, 512)
    tn = n if tn is None else tn
    return _mm(name, a, w, grid=(m // tm, n // tn), a_spec=BS((tm, tn), lambda i, r: (i, r)),
               b_spec=BS((k, tn), lambda i, r: (0, r)), o_spec=BS((tm, k), lambda i, r: (i, 0)),
               out_shape=(m, k), dn=NT, out_dtype=out_dtype)


def _mm_tn(name, a, b, tn=None):
    m, k = a.shape
    n = b.shape[1]
    tm = _tile(m, TK_RED)
    tn = n if tn is None else tn
    return _mm(name, a, b, grid=(n // tn, m // tm), a_spec=BS((tm, k), lambda j, r: (r, 0)),
               b_spec=BS((tm, tn), lambda j, r: (r, j)), o_spec=BS((k, tn), lambda j, r: (0, j)),
               out_shape=(k, n), dn=TN)


def _bmm_nn(name, a, w, out_dtype=F32):
    m, k = a.shape
    g, _, n = w.shape
    tm = _tile(m, 512)
    return _mm(name, a, w, grid=(g, m // tm, 1), a_spec=BS((tm, k), lambda q, i, r: (i, 0)),
               b_spec=BS((None, k, n), lambda q, i, r: (q, 0, 0)), o_spec=BS((None, tm, n), lambda q, i, r: (q, i, 0)),
               out_shape=(g, m, n), dn=NN, out_dtype=out_dtype)


def _bmm_tn(name, a, b):
    m, k = a.shape
    g, _, n = b.shape
    tm = _tile(m, TK_RED)
    return _mm(name, a, b, grid=(g, m // tm), a_spec=BS((tm, k), lambda q, r: (r, 0)),
               b_spec=BS((None, tm, n), lambda q, r: (q, r, 0)), o_spec=BS((None, k, n), lambda q, r: (q, 0, 0)),
               out_shape=(g, k, n), dn=TN)


def _bmm_nt_sum(name, a, w):
    g, m, n = a.shape
    k = w.shape[1]
    tm = _tile(m, TM_SUM)
    return _mm(name, a, w, grid=(m // tm, g), a_spec=BS((None, tm, n), lambda i, q: (q, i, 0)),
               b_spec=BS((None, k, n), lambda i, q: (q, 0, 0)), o_spec=BS((tm, k), lambda i, q: (i, 0)),
               out_shape=(m, k), dn=NT)


def _bmm_nn_sum(name, a, w):
    g, m, k = a.shape
    n = w.shape[2]
    tm = _tile(m, TM_SUM)
    return _mm(name, a, w, grid=(m // tm, g), a_spec=BS((None, tm, k), lambda i, q: (q, i, 0)),
               b_spec=BS((None, k, n), lambda i, q: (q, 0, 0)), o_spec=BS((tm, n), lambda i, q: (i, 0)),
               out_shape=(m, n), dn=NN)


def _bbmm_tn(name, a, b):
    g, m, k = a.shape
    n = b.shape[2]
    tm = _tile(m, TK_RED)
    return _mm(name, a, b, grid=(g, m // tm), a_spec=BS((None, tm, k), lambda q, r: (q, r, 0)),
               b_spec=BS((None, tm, n), lambda q, r: (q, r, 0)), o_spec=BS((None, k, n), lambda q, r: (q, 0, 0)),
               out_shape=(g, k, n), dn=TN)


def _rstd(x):
    return lax.rsqrt(jnp.mean(x * x, axis=-1, keepdims=True) + EPS)


def _norm_fwd(x, g, after=None):
    rows = x.shape[0]
    tm = _tile(rows, 512)

    def kern(x_ref, g_ref, *rest):
        xv = x_ref[...]
        rest[-1][...] = ((xv * _rstd(xv)) * g_ref[...]).astype(BF16)

    extra = () if after is None else (after,)
    return pl.pallas_call(kern, name="norm_fwd", grid=(rows // tm,),
                          in_specs=[BS((tm, DM), lambda i: (i, 0)), BS((1, DM), lambda i: (0, 0))] + [ANY] * len(extra),
                          out_specs=BS((tm, DM), lambda i: (i, 0)),
                          out_shape=jax.ShapeDtypeStruct((rows, DM), BF16), compiler_params=_params(1))(x, g, *extra)


def _norm_res(x, y, g):
    rows = x.shape[0]
    tm = _tile(rows, 512)

    def kern(x_ref, y_ref, g_ref, o_ref):
        yv = y_ref[...]
        o_ref[...] = x_ref[...] + (yv * _rstd(yv)) * g_ref[...]

    row = BS((tm, DM), lambda i: (i, 0))
    return pl.pallas_call(kern, name="norm_res", grid=(rows // tm,),
                          in_specs=[row, row, BS((1, DM), lambda i: (0, 0))], out_specs=row,
                          out_shape=jax.ShapeDtypeStruct((rows, DM), F32), compiler_params=_params(1))(x, y, g)


def _norm_bwd(z, dout, g, resid, out_dtype, after=None):
    rows = z.shape[0]
    tm = _tile(rows, 512)
    has_res = resid is not None

    def kern(*refs):
        z_ref, d_ref, g_ref = refs[:3]
        r_ref = refs[3] if has_res else None
        dz_ref, dg_ref = refs[-2:]
        zv = z_ref[...]
        dv = d_ref[...].astype(F32)
        r = _rstd(zv)
        zh = zv * r
        dzh = dv * g_ref[...]
        dz = r * (dzh - zh * jnp.mean(dzh * zh, axis=-1, keepdims=True))
        if has_res:
            dz = dz + r_ref[...]
        dz_ref[...] = dz.astype(dz_ref.dtype)
        part = jnp.sum(dv * zh, axis=0, keepdims=True)

        @pl.when(pl.program_id(0) == 0)
        def _():
            dg_ref[...] = part

        @pl.when(pl.program_id(0) > 0)
        def _():
            dg_ref[...] += part

    row = BS((tm, DM), lambda i: (i, 0))
    vec = BS((1, DM), lambda i: (0, 0))
    ins = [row, row, vec] + ([row] if has_res else []) + ([ANY] if after is not None else [])
    args = (z, dout, g) + ((resid,) if has_res else ()) + ((after,) if after is not None else ())
    return pl.pallas_call(kern, name="norm_bwd_res" if has_res else "norm_bwd", grid=(rows // tm,), in_specs=ins,
                          out_specs=[row, vec],
                          out_shape=[jax.ShapeDtypeStruct((rows, DM), out_dtype), jax.ShapeDtypeStruct((1, DM), F32)],
                          compiler_params=_params(1))(*args)


def _ffn_up(h, wgu4):
    s = h.shape[0]
    tm = _tile(s, 512)

    def kern(h_ref, w_ref, gu_ref, a_ref):
        hv = h_ref[...]
        gate = jnp.dot(hv, w_ref[0], preferred_element_type=F32)
        up = jnp.dot(hv, w_ref[1], preferred_element_type=F32)
        gu_ref[0] = gate
        gu_ref[1] = up
        a_ref[...] = (gate * jax.nn.sigmoid(gate) * up).astype(BF16)

    return pl.pallas_call(
        kern, name="ffn_up", grid=(4, s // tm),
        in_specs=[BS((tm, DM), lambda j, i: (i, 0)), BS((2, None, DM, FFB), lambda j, i: (0, j, 0, 0))],
        out_specs=[BS((2, None, tm, FFB), lambda j, i: (0, j, i, 0)), BS((None, tm, FFB), lambda j, i: (j, i, 0))],
        out_shape=[jax.ShapeDtypeStruct((2, 4, s, FFB), F32), jax.ShapeDtypeStruct((4, s, FFB), BF16)],
        compiler_params=_params(2))(h, wgu4)


def _ffn_da(dy, wd4, gu):
    s = dy.shape[0]
    tm = _tile(s, 512)

    def kern(dy_ref, w_ref, gu_ref, o_ref):
        da = lax.dot_general(dy_ref[...], w_ref[...], (NT, ((), ())), preferred_element_type=F32)
        gate = gu_ref[0]
        up = gu_ref[1]
        sg = jax.nn.sigmoid(gate)
        o_ref[0] = (da * up * (sg * (1.0 + gate * (1.0 - sg)))).astype(BF16)
        o_ref[1] = (da * (gate * sg)).astype(BF16)

    blk = BS((2, None, tm, FFB), lambda j, i: (0, j, i, 0))
    return pl.pallas_call(
        kern, name="ffn_da", grid=(4, s // tm),
        in_specs=[BS((tm, DM), lambda j, i: (i, 0)), BS((None, FFB, DM), lambda j, i: (j, 0, 0)), blk],
        out_specs=blk, out_shape=jax.ShapeDtypeStruct((2, 4, s, FFB), BF16), compiler_params=_params(2))(dy, wd4, gu)


def _ffn_fwd(x, gpre, gpost, wgu, wd):
    h = _norm_fwd(x, gpre)
    gu, a = _ffn_up(h, wgu.reshape(2, 4, DM, FFB))
    y = _bmm_nn_sum("ffn_down", a, wd.reshape(4, FFB, DM))
    return _norm_res(x, y, gpost), (x, h, gu, a, y)


def _ffn_bwd(dxo, saved, gpre, gpost, wgu, wd, after=None):
    x, h, gu, a, y = saved
    s = x.shape[0]
    dy, dgpost = _norm_bwd(y, dxo, gpost, None, BF16, after)
    dgu = _ffn_da(dy, wd.reshape(4, FFB, DM), gu).reshape(8, s, FFB)
    dwd = _bmm_tn_a3("ffn_dwd", a, dy)
    dwgu = _bmm_tn("ffn_dwgu", h, dgu)
    dh = _bmm_nt_sum("ffn_dh", dgu, wgu)
    dx, dgpre = _norm_bwd(x, dh, gpre, dxo, F32)
    return dx, dgpre, dgpost, dwgu, dwd.reshape(D_FF, DM)


def _bmm_tn_a3(name, a, b):
    g, m, k = a.shape
    n = b.shape[1]
    tm = _tile(m, TK_RED)
    return _mm(name, a, b, grid=(g, m // tm), a_spec=BS((None, tm, k), lambda q, r: (q, r, 0)),
               b_spec=BS((tm, n), lambda q, r: (r, 0)), o_spec=BS((None, k, n), lambda q, r: (q, 0, 0)),
               out_shape=(g, k, n), dn=TN)


def _softmax_rows(s):
    m = jnp.max(s, axis=-1, keepdims=True)
    p = jnp.exp(s - m)
    return p / jnp.sum(p, axis=-1, keepdims=True)


def _xattn_fwd_call(h, wq, kv):
    s = h.shape[0]
    mlen = kv.shape[1]
    tm = _tile(s, 512)
    scale = MEM_HD ** -0.5

    def kern(h_ref, w_ref, k_ref, v_ref, q_ref, o_ref):
        q = jnp.dot(h_ref[...], w_ref[...], preferred_element_type=F32).astype(BF16)
        q_ref[...] = q
        sc = lax.dot_general(q, k_ref[...], (NT, ((), ())), preferred_element_type=F32) * scale
        p = _softmax_rows(sc)
        o_ref[...] = jnp.dot(p.astype(BF16), v_ref[...], preferred_element_type=F32).astype(BF16)

    blk = BS((tm, MEM_HD), lambda i, hd: (i, hd))
    return pl.pallas_call(
        kern, name="xattn_fwd", grid=(s // tm, MEM_H),
        in_specs=[BS((tm, DM), lambda i, hd: (i, 0)), BS((DM, MEM_HD), lambda i, hd: (0, hd)),
                  BS((None, mlen, MEM_HD), lambda i, hd: (hd, 0, 0)),
                  BS((None, mlen, MEM_HD), lambda i, hd: (MEM_H + hd, 0, 0))],
        out_specs=[blk, blk],
        out_shape=[jax.ShapeDtypeStruct((s, DM), BF16), jax.ShapeDtypeStruct((s, DM), BF16)],
        compiler_params=_params(2))(h, wq, kv, kv)


def _xattn_bwd_call(q, kv, do):
    s = q.shape[0]
    mlen = kv.shape[1]
    tm = _tile(s, 512)
    scale = MEM_HD ** -0.5

    def kern(q_ref, k_ref, v_ref, do_ref, dq_ref, dkv_ref):
        qv, kvv, vv, dov = q_ref[...], k_ref[...], v_ref[...], do_ref[...]
        sc = lax.dot_general(qv, kvv, (NT, ((), ())), preferred_element_type=F32) * scale
        p = _softmax_rows(sc)
        dp = lax.dot_general(dov, vv, (NT, ((), ())), preferred_element_type=F32)
        ds = (p * (dp - jnp.sum(dp * p, axis=-1, keepdims=True)) * scale).astype(BF16)
        dq_ref[...] = jnp.dot(ds, kvv, preferred_element_type=F32).astype(BF16)
        dk = lax.dot_general(ds, qv, (TN, ((), ())), preferred_element_type=F32)
        dv = lax.dot_general(p.astype(BF16), dov, (TN, ((), ())), preferred_element_type=F32)

        @pl.when(pl.program_id(1) == 0)
        def _():
            dkv_ref[0] = dk
            dkv_ref[1] = dv

        @pl.when(pl.program_id(1) > 0)
        def _():
            dkv_ref[0] += dk
            dkv_ref[1] += dv

    blk = BS((tm, MEM_HD), lambda hd, i: (i, hd))
    return pl.pallas_call(
        kern, name="xattn_bwd", grid=(MEM_H, s // tm),
        in_specs=[blk, BS((None, mlen, MEM_HD), lambda hd, i: (hd, 0, 0)),
                  BS((None, mlen, MEM_HD), lambda hd, i: (MEM_H + hd, 0, 0)), blk],
        out_specs=[blk, BS((2, None, mlen, MEM_HD), lambda hd, i: (0, hd, 0, 0))],
        out_shape=[jax.ShapeDtypeStruct((s, DM), BF16), jax.ShapeDtypeStruct((2, MEM_H, mlen, MEM_HD), F32)],
        compiler_params=_params(2))(q, kv, kv, do)


def _cross_fwd(x, mem, gpre, gmem, gpost, wq, wkv, wo, after=None):
    h = _norm_fwd(x, gpre, after)
    mn = _norm_fwd(mem, gmem)
    kv = _bmm_nn("xattn_kv", mn, wkv, BF16)
    q, o = _xattn_fwd_call(h, wq, kv)
    y = _mm_nn("xattn_out", o, wo)
    return _norm_res(x, y, gpost), (x, h, mn, kv, q, o, y)


def _cross_bwd(dxo, saved, mem, gpre, gmem, gpost, wq, wkv, wo, after=None):
    x, h, mn, kv, q, o, y = saved
    mlen = mem.shape[0]
    dy, dgpost = _norm_bwd(y, dxo, gpost, None, BF16, after)
    do = _mm_nt("xattn_do", dy, wo, BF16)
    dwo = _mm_tn("xattn_dwo", o, dy)
    dq, dkv = _xattn_bwd_call(q, kv, do)
    dwq = _mm_tn("xattn_dwq", h, dq)
    dh = _mm_nt("xattn_dh", dq, wq)
    dkv8 = dkv.reshape(8, mlen, MEM_HD)
    dwkv = _bmm_tn("xattn_dwkv", mn, dkv8)
    dmn = _bmm_nt_sum("xattn_dmn", dkv8, wkv)
    _, dgmem = _norm_bwd(mem, dmn, gmem, None, BF16)
    dx, dgpre = _norm_bwd(x, dh, gpre, dxo, F32)
    return dx, dgpre, dgmem, dgpost, dwq, dwkv, dwo


def _log_sigmoid(z):
    return jnp.minimum(z, 0.0) - jnp.log1p(jnp.exp(-jnp.abs(z)))


def _lane_scan_steps():
    return (1, 2, 4, 8, 16, 32, 64)


def _fox_cum(frow, bfb):
    s = frow.shape[1]

    def kern(f_ref, b_ref, o_ref):
        lane = lax.broadcasted_iota(jnp.int32, (FOX_H, LANE), 1)
        carry = jnp.zeros((FOX_H, 1), F32)
        for c in range(s // LANE):
            sl = slice(c * LANE, (c + 1) * LANE)
            lf = _log_sigmoid(f_ref[:, sl] + b_ref[...])
            v = lf
            for d in _lane_scan_steps():
                v = v + jnp.where(lane >= d, pltpu.roll(v, d, 1), 0.0)
            o_ref[:, sl] = v + carry
            carry = carry + jnp.sum(lf, axis=1, keepdims=True)

    return pl.pallas_call(kern, name="fox_cum", out_shape=jax.ShapeDtypeStruct((FOX_H, s), F32),
                          compiler_params=pltpu.CompilerParams(vmem_limit_bytes=VMEM_LIMIT))(frow, bfb)


def _fox_dlogf(dcq, dck, frow, bfb):
    s = frow.shape[1]

    def kern(q_ref, d_ref, f_ref, b_ref, df_ref, db_ref):
        lane = lax.broadcasted_iota(jnp.int32, (FOX_H, LANE), 1)
        carry = jnp.zeros((FOX_H, 1), F32)
        dbf = jnp.zeros((FOX_H, 1), F32)
        for c in reversed(range(s // LANE)):
            sl = slice(c * LANE, (c + 1) * LANE)
            dc = q_ref[:, sl] - d_ref[:, sl]
            v = dc
            for d in _lane_scan_steps():
                v = v + jnp.where(lane < LANE - d, pltpu.roll(v, LANE - d, 1), 0.0)
            v = v + carry
            carry = carry + jnp.sum(dc, axis=1, keepdims=True)
            df = v * jax.nn.sigmoid(-(f_ref[:, sl] + b_ref[...]))
            df_ref[:, sl] = df
            dbf = dbf + jnp.sum(df, axis=1, keepdims=True)
        db_ref[...] = jnp.broadcast_to(dbf, (FOX_H, LANE))

    return pl.pallas_call(kern, name="fox_dlogf",
                          out_shape=[jax.ShapeDtypeStruct((FOX_H, s), F32), jax.ShapeDtypeStruct((FOX_H, LANE), F32)],
                          compiler_params=pltpu.CompilerParams(vmem_limit_bytes=VMEM_LIMIT))(dcq, dck, frow, bfb)


FOX_TQ = 512
Q_COL, K_COL, V_COL = 0, FOX_W // LANE, 2 * FOX_W // LANE
B_COL, C_COL, U_COL = 12, 16, 20


def _fox_logits(qm, kb, cc, cr, causal, scale, reps):
    sc = lax.dot_general(qm, kb, (NT, ((), ())), preferred_element_type=F32) * scale
    sc = sc + jnp.tile(cc, (1, reps)) - cr
    return jnp.where(causal, sc, NEG)


def _fox_fwd_call(proj, cumc, cumr):
    s = proj.shape[0]
    tq = _tile(s, FOX_TQ)
    nq = s // tq
    reps = tq // LANE
    scale = FOX_HD ** -0.5

    def kern(q_ref, k_ref, v_ref, cc_ref, cr_ref, o_ref, lse_ref, m_s, l_s, acc_s):
        i = pl.program_id(1)
        j = pl.program_id(2)
        lane = lax.broadcasted_iota(jnp.int32, (tq, LANE), 1)

        @pl.when(j == 0)
        def _():
            m_s[...] = jnp.full(m_s.shape, NEG, F32)
            l_s[...] = jnp.zeros(l_s.shape, F32)
            acc_s[...] = jnp.zeros(acc_s.shape, F32)

        @pl.when(j <= i)
        def _():
            qv = q_ref[...]
            kb = k_ref[...].astype(BF16)
            vb = v_ref[...].astype(BF16)
            causal = (i * tq + lax.broadcasted_iota(jnp.int32, (tq, tq), 0)
                      >= j * tq + lax.broadcasted_iota(jnp.int32, (tq, tq), 1))
            for hh in range(2):
                sel = (lane < FOX_HD) if hh == 0 else (lane >= FOX_HD)
                qm = jnp.where(sel, qv, 0.0).astype(BF16)
                sc = _fox_logits(qm, kb, cc_ref[hh], cr_ref[hh:hh + 1, :], causal, scale, reps)
                m_prev = m_s[hh]
                m_new = jnp.maximum(m_prev, jnp.max(sc, axis=-1, keepdims=True))
                alpha = jnp.exp(m_prev - m_new)
                p = jnp.exp(sc - m_new)
                l_s[hh] = alpha * l_s[hh] + jnp.sum(p, axis=-1, keepdims=True)
                acc_s[hh] = alpha * acc_s[hh] + jnp.dot(p.astype(BF16), vb, preferred_element_type=F32)
                m_s[hh] = m_new

        @pl.when(j == i)
        def _():
            o_ref[...] = jnp.where(lane < FOX_HD, acc_s[0] / l_s[0], acc_s[1] / l_s[1])
            for hh in range(2):
                lse_ref[hh] = jnp.broadcast_to(m_s[hh] + jnp.log(l_s[hh]), (tq, LANE))

    kvi = lambda hp, i, j: jnp.minimum(j, i)
    return pl.pallas_call(
        kern, name="fox_fwd", grid=(4, nq, nq),
        in_specs=[BS((tq, LANE), lambda hp, i, j: (i, Q_COL + hp)),
                  BS((tq, LANE), lambda hp, i, j: (kvi(hp, i, j), K_COL + hp)),
                  BS((tq, LANE), lambda hp, i, j: (kvi(hp, i, j), V_COL + hp)),
                  BS((2, tq, LANE), lambda hp, i, j: (hp, i, 0)),
                  BS((None, 2, tq), lambda hp, i, j: (hp, 0, kvi(hp, i, j)))],
        out_specs=[BS((tq, LANE), lambda hp, i, j: (i, hp)), BS((2, tq, LANE), lambda hp, i, j: (hp, i, 0))],
        out_shape=[jax.ShapeDtypeStruct((s, FOX_W), F32), jax.ShapeDtypeStruct((FOX_H, s, LANE), F32)],
        scratch_shapes=[pltpu.VMEM((2, tq, 1), F32), pltpu.VMEM((2, tq, 1), F32), pltpu.VMEM((2, tq, LANE), F32)],
        compiler_params=_params(3))(proj, proj, proj, cumc, cumr)


ROWSUM_M = 16


def _fox_bwd_call(proj, o, lse, dcat, cumc, cumr):
    s = proj.shape[0]
    tq = _tile(s, FOX_TQ)
    nq = s // tq
    reps = tq // LANE
    scale = FOX_HD ** -0.5

    def kern(q_ref, k_ref, v_ref, do_ref, o_ref, lse_ref, cc_ref, cr_ref, dq_ref, dk_ref, dv_ref, dck_ref, dcq_ref):
        j = pl.program_id(1)
        i = pl.program_id(2)
        lane = lax.broadcasted_iota(jnp.int32, (tq, LANE), 1)
        ones = jnp.ones((ROWSUM_M, tq), BF16)

        @pl.when((j == 0) & (i == 0))
        def _():
            dq_ref[...] = jnp.zeros(dq_ref.shape, F32)
            dcq_ref[...] = jnp.zeros(dcq_ref.shape, F32)

        @pl.when(i == j)
        def _():
            dk_ref[...] = jnp.zeros(dk_ref.shape, F32)
            dv_ref[...] = jnp.zeros(dv_ref.shape, F32)
            dck_ref[...] = jnp.zeros(dck_ref.shape, F32)

        @pl.when(i >= j)
        def _():
            qv = q_ref[...]
            dov = do_ref[...]
            ov = o_ref[...]
            kb = k_ref[...].astype(BF16)
            vb = v_ref[...].astype(BF16)
            causal = (i * tq + lax.broadcasted_iota(jnp.int32, (tq, tq), 0)
                      >= j * tq + lax.broadcasted_iota(jnp.int32, (tq, tq), 1))
            dq_t = jnp.zeros((tq, LANE), F32)
            dk_t = jnp.zeros((tq, LANE), F32)
            dv_t = jnp.zeros((tq, LANE), F32)
            for hh in range(2):
                sel = (lane < FOX_HD) if hh == 0 else (lane >= FOX_HD)
                qm = jnp.where(sel, qv, 0.0).astype(BF16)
                dom32 = jnp.where(sel, dov, 0.0)
                dom = dom32.astype(BF16)
                sc = _fox_logits(qm, kb, cc_ref[hh], cr_ref[hh:hh + 1, :], causal, scale, reps)
                p = jnp.exp(sc - jnp.tile(lse_ref[hh], (1, reps)))
                dp = lax.dot_general(dom, vb, (NT, ((), ())), preferred_element_type=F32)
                delta = jnp.sum(dom32 * ov, axis=-1, keepdims=True)
                ds = p * (dp - delta)
                dsb = ds.astype(BF16)
                dq_t = jnp.where(sel, jnp.dot(dsb, kb, preferred_element_type=F32) * scale, dq_t)
                dk_t = dk_t + lax.dot_general(dsb, qm, (TN, ((), ())), preferred_element_type=F32) * scale
                dv_t = dv_t + lax.dot_general(p.astype(BF16), dom, (TN, ((), ())), preferred_element_type=F32)
                dck_ref[hh] += jnp.sum(ds, axis=0, keepdims=True)
                ds_lo = (ds - dsb.astype(F32)).astype(BF16)
                dcq_ref[hh, i] += (lax.dot_general(ones, dsb, (NT, ((), ())), preferred_element_type=F32)
                                   + lax.dot_general(ones, ds_lo, (NT, ((), ())), preferred_element_type=F32))
            rows =pl.ds(pl.multiple_of(i * tq, tq), tq)
            dq_ref[rows, :] += dq_t
            dk_ref[...] += dk_t
            dv_ref[...] += dv_t

    qi = lambda hp, j, i: jnp.maximum(i, j)
    return pl.pallas_call(
        kern, name="fox_bwd", grid=(4, nq, nq),
        in_specs=[BS((tq, LANE), lambda hp, j, i: (qi(hp, j, i), Q_COL + hp)),
                  BS((tq, LANE), lambda hp, j, i: (j, K_COL + hp)),
                  BS((tq, LANE), lambda hp, j, i: (j, V_COL + hp)),
                  BS((tq, LANE), lambda hp, j, i: (qi(hp, j, i), hp)),
                  BS((tq, LANE), lambda hp, j, i: (qi(hp, j, i), hp)),
                  BS((2, tq, LANE), lambda hp, j, i: (hp, qi(hp, j, i), 0)),
                  BS((2, tq, LANE), lambda hp, j, i: (hp, qi(hp, j, i), 0)),
                  BS((None, 2, tq), lambda hp, j, i: (hp, 0, j))],
        out_specs=[BS((s, LANE), lambda hp, j, i: (0, hp)), BS((tq, LANE), lambda hp, j, i: (j, hp)),
                   BS((tq, LANE), lambda hp, j, i: (j, hp)), BS((2, 1, tq), lambda hp, j, i: (hp, 0, j)),
                   BS((2, nq, ROWSUM_M, tq), lambda hp, j, i: (hp, 0, 0, 0))],
        out_shape=[jax.ShapeDtypeStruct((s, FOX_W), F32), jax.ShapeDtypeStruct((s, FOX_W), F32),
                   jax.ShapeDtypeStruct((s, FOX_W), F32), jax.ShapeDtypeStruct((FOX_H, 1, s), F32),
                   jax.ShapeDtypeStruct((FOX_H, nq, ROWSUM_M, tq), F32)],
        compiler_params=_params(3))(proj, proj, proj, dcat, o, lse, cumc, cumr)


def _shift_down(v, d, row):
    return jnp.where(row >= d, pltpu.roll(v, d, 0), 0.0)


def _shift_up(v, d, row, n):
    return jnp.where(row < n - d, pltpu.roll(v, n - d, 0), 0.0)


def _sconv_fwd(proj, convw):
    s = proj.shape[0]

    def kern(b_ref, c_ref, u_ref, w_ref, y_ref):
        row = lax.broadcasted_iota(jnp.int32, (s, LANE), 0)
        z = c_ref[...] * u_ref[...]
        conv = w_ref[2:3, :] * z + w_ref[1:2, :] * _shift_down(z, 1, row) + w_ref[0:1, :] * _shift_down(z, 2, row)
        y_ref[...] = (b_ref[...] * conv).astype(BF16)

    col = lambda base: BS((s, LANE), lambda cb: (0, base + cb))
    return pl.pallas_call(kern, name="sconv_fwd", grid=(SC_W // LANE,),
                          in_specs=[col(B_COL), col(C_COL), col(U_COL), BS((SC_K, LANE), lambda cb: (0, cb))],
                          out_specs=BS((s, LANE), lambda cb: (0, cb)),
                          out_shape=jax.ShapeDtypeStruct((s, SC_W), BF16), compiler_params=_params(1))(proj, proj, proj, convw)


def _sconv_bwd(proj, convw, dcat):
    s = proj.shape[0]

    def kern(b_ref, c_ref, u_ref, w_ref, dy_ref, db_ref, dc_ref, du_ref, dw_ref):
        row = lax.broadcasted_iota(jnp.int32, (s, LANE), 0)
        cv, uv, dyv = c_ref[...], u_ref[...], dy_ref[...]
        z = cv * uv
        z1 = _shift_down(z, 1, row)
        z2 = _shift_down(z, 2, row)
        conv = w_ref[2:3, :] * z + w_ref[1:2, :] * z1 + w_ref[0:1, :] * z2
        db_ref[...] = dyv * conv
        dcv = dyv * b_ref[...]
        dz = w_ref[2:3, :] * dcv + w_ref[1:2, :] * _shift_up(dcv, 1, row, s) + w_ref[0:1, :] * _shift_up(dcv, 2, row, s)
        dc_ref[...] = dz * uv
        du_ref[...] = dz * cv
        dw_ref[0:1, :] = jnp.sum(dcv * z2, axis=0, keepdims=True)
        dw_ref[1:2, :] = jnp.sum(dcv * z1, axis=0, keepdims=True)
        dw_ref[2:3, :] = jnp.sum(dcv * z, axis=0, keepdims=True)

    col = lambda base: BS((s, LANE), lambda cb: (0, base + cb))
    out = BS((s, LANE), lambda cb: (0, cb))
    wspec = BS((SC_K, LANE), lambda cb: (0, cb))
    act = jax.ShapeDtypeStruct((s, SC_W), F32)
    return pl.pallas_call(kern, name="sconv_bwd", grid=(SC_W // LANE,),
                          in_specs=[col(B_COL), col(C_COL), col(U_COL), wspec, col(FOX_W // LANE)],
                          out_specs=[out, out, out, wspec],
                          out_shape=[act, act, act, jax.ShapeDtypeStruct((SC_K, SC_W), F32)],
                          compiler_params=_params(1))(proj, proj, proj, convw, dcat)


def _fox_layer_fwd(x, gpre, gpost, wall, bfb, convw, wout, after=None):
    s = x.shape[0]
    h = _norm_fwd(x, gpre, after)
    proj = _mm_nn("fox_proj", h, wall, tn=AB_PAD // 5)
    frow = proj[:, 3 * FOX_W + 3 * SC_W:3 * FOX_W + 3 * SC_W + FOX_H].T
    cumr = _fox_cum(frow, bfb)
    cumc = jnp.broadcast_to(cumr[:, :, None], (FOX_H, s, LANE))
    cumr4 = cumr.reshape(4, 2, s)
    o, lse = _fox_fwd_call(proj, cumc, cumr4)
    yb = _sconv_fwd(proj, convw)
    cat = jnp.concatenate([o.astype(BF16), yb], axis=1)
    y = _mm_nn("fox_out", cat, wout)
    return _norm_res(x, y, gpost), (x, h, proj, frow, cumc, cumr4, o, lse, cat, y)


def _fox_layer_bwd(dxo, saved, gpre, gpost, wall, bfb, convw, wout, after=None):
    x, h, proj, frow, cumc, cumr4, o, lse, cat, y = saved
    s = x.shape[0]
    dy, dgpost = _norm_bwd(y, dxo, gpost, None, BF16, after)
    dcat = _mm_nt("fox_dcat", dy, wout)
    dwout = _mm_tn("fox_dwout", cat, dy)
    db, dc, du, dconvw = _sconv_bwd(proj, convw, dcat)
    dq, dk, dv, dck, dcq = _fox_bwd_call(proj, o, lse, dcat, cumc, cumr4)
    dfrow, dbf = _fox_dlogf(dcq[:, :, 0, :].reshape(FOX_H, s), dck.reshape(FOX_H, s), frow, bfb)
    dfcol = jnp.pad(dfrow.T, ((0, 0), (0, LANE - FOX_H)))
    dproj = jnp.concatenate([dq, dk, dv, db, dc, du, dfcol], axis=1).astype(BF16)
    dwall = _mm_tn("fox_dwall", h, dproj, tn=AB_PAD // 5)
    dh = _mm_nt("fox_dh", dproj, wall, tn=AB_PAD // 5)
    dx, dgpre = _norm_bwd(x, dh, gpre, dxo, F32)
    return dx, dgpre, dgpost, dwall, dbf[:, 0], dconvw, dwout


def _ab_pack(w):
    nf = 3 * FOX_W
    return jnp.concatenate([w[:, :nf], w[:, nf + FOX_H:], w[:, nf:nf + FOX_H],
                            jnp.zeros((w.shape[0], AB_PAD - AB_IN), w.dtype)], axis=1)


def _ab_unpack(w):
    nf = 3 * FOX_W
    nbcu = 3 * SC_W
    return jnp.concatenate([w[:, :nf], w[:, nf + nbcu:nf + nbcu + FOX_H], w[:, nf:nf + nbcu]], axis=1)


NCH = DM // LANE
CH_PER_BLK = LRU_BW // LANE


def _chunk_spec(s, lead=0):
    return BS((None, s, LANE), lambda ch: (lead + ch // CH_PER_BLK, 0, ch % CH_PER_BLK))


def _vec_chunk(rows):
    return BS((rows, LANE), lambda ch: (0, ch))


def _neg_expm1(x):
    series = -x * (1.0 + x * (1 / 2) * (1.0 + x * (1 / 3) * (1.0 + x * (1 / 4) * (1.0 + x * (1 / 5) * (
        1.0 + x * (1 / 6) * (1.0 + x * (1 / 7)))))))
    return jnp.where(x > -0.25, series, 1.0 - jnp.exp(x))


def _softplus(z):
    return jnp.maximum(z, 0.0) + jnp.log1p(jnp.exp(-jnp.abs(z)))


GELU_C = math.sqrt(2.0 / math.pi)
GELU_A = 0.044715


def _gelu(x):
    return 0.5 * x * (1.0 + jnp.tanh(GELU_C * (x + GELU_A * x * x * x)))


def _gelu_grad(x):
    t = jnp.tanh(GELU_C * (x + GELU_A * x * x * x))
    return 0.5 * (1.0 + t) + 0.5 * x * (1.0 - t * t) * GELU_C * (1.0 + 3.0 * GELU_A * x * x)


def _lru_conv_fwd(gu, convw, convb):
    s = gu.shape[1]

    def kern(x_ref, w_ref, b_ref, u_ref):
        row = lax.broadcasted_iota(jnp.int32, (s, LANE), 0)
        xv = x_ref[...]
        u_ref[...] = (b_ref[...] + w_ref[3:4, :] * xv + w_ref[2:3, :] * _shift_down(xv, 1, row)
                      + w_ref[1:2, :] * _shift_down(xv, 2, row) + w_ref[0:1, :] * _shift_down(xv, 3, row))

    return pl.pallas_call(kern, name="lru_conv_fwd", grid=(NCH,),
                          in_specs=[_chunk_spec(s, LRU_NB), _vec_chunk(RG_K), _vec_chunk(1)], out_specs=_chunk_spec(s),
                          out_shape=jax.ShapeDtypeStruct((LRU_NB, s, LRU_BW), F32), compiler_params=_params(1))(gu, convw, convb)


def _lru_conv_bwd(dud, dug, gu, convw):
    s = gu.shape[1]

    def kern(d1_ref, d2_ref, x_ref, w_ref, dx_ref, dw_ref, db_ref):
        row = lax.broadcasted_iota(jnp.int32, (s, LANE), 0)
        du = d1_ref[...] + d2_ref[...]
        xv = x_ref[...]
        dx_ref[...] = (w_ref[3:4, :] * du + w_ref[2:3, :] * _shift_up(du, 1, row, s) + w_ref[1:2, :] * _shift_up(du, 2, row, s)
                       + w_ref[0:1, :] * _shift_up(du, 3, row, s)).astype(BF16)
        dw_ref[3:4, :] = jnp.sum(du * xv, axis=0, keepdims=True)
        for k in range(1, RG_K):
            dw_ref[3 - k:4 - k, :] = jnp.sum(du * _shift_down(xv, k, row), axis=0, keepdims=True)
        db_ref[...] = jnp.sum(du, axis=0, keepdims=True)

    return pl.pallas_call(kern, name="lru_conv_bwd", grid=(NCH,),
                          in_specs=[_chunk_spec(s), _chunk_spec(s), _chunk_spec(s, LRU_NB), _vec_chunk(RG_K)],
                          out_specs=[_chunk_spec(s), _vec_chunk(RG_K), _vec_chunk(1)],
                          out_shape=[jax.ShapeDtypeStruct((LRU_NB, s, LRU_BW), BF16),
                                     jax.ShapeDtypeStruct((RG_K, DM), F32), jax.ShapeDtypeStruct((1, DM), F32)],
                          compiler_params=_params(1))(dud, dug, gu, convw)


def _lru_gates(z_ref, bai_ref, lam_ref, uv):
    r = jax.nn.sigmoid(z_ref[0] + bai_ref[0:1, :])
    ig = jax.nn.sigmoid(z_ref[1] + bai_ref[1:2, :])
    sp = _softplus(-lam_ref[...])
    la = -RG_C * r * sp
    a = jnp.exp(la)
    sq = jnp.sqrt(_neg_expm1(2.0 * la))
    return r, ig, sp, a, sq


def _scan_steps(n):
    d, out = 1, []
    while d < n:
        out.append(d)
        d *= 2
    return out


def _lru_scan_fwd(z, bai, lam, u, gu):
    s = u.shape[1]
    zspec = BS((2, None, s, LANE), lambda ch: (0, ch // CH_PER_BLK, 0, ch % CH_PER_BLK))

    def kern(z_ref, bai_ref, lam_ref, u_ref, g_ref, hs_ref, y_ref):
        row = lax.broadcasted_iota(jnp.int32, (s, LANE), 0)
        uv = u_ref[...]
        _, ig, _, a, sq = _lru_gates(z_ref, bai_ref, lam_ref, uv)
        b = sq * (ig * uv)
        for d in _scan_steps(s):
            a_sh = jnp.where(row >= d, pltpu.roll(a, d, 0), 1.0)
            b = a * _shift_down(b, d, row) + b
            a = a * a_sh
        hs_ref[...] = b
        y_ref[...] = (_gelu(g_ref[...]) * b).astype(BF16)

    return pl.pallas_call(kern, name="lru_scan_fwd", grid=(NCH,),
                          in_specs=[zspec, _vec_chunk(2), _vec_chunk(1), _chunk_spec(s), _chunk_spec(s)],
                          out_specs=[_chunk_spec(s), BS((s, LANE), lambda ch: (0, ch))],
                          out_shape=[jax.ShapeDtypeStruct((LRU_NB, s, LRU_BW), F32), jax.ShapeDtypeStruct((s, DM), BF16)],
                          compiler_params=_params(1))(z, bai, lam, u, gu)


def _lru_scan_bwd(dyp, z, bai, lam, u, gu, hs):
    s = u.shape[1]
    zspec = BS((2, None, s, LANE), lambda ch: (0, ch // CH_PER_BLK, 0, ch % CH_PER_BLK))

    def kern(dy_ref, z_ref, bai_ref, lam_ref, u_ref, g_ref, hs_ref, dg_ref, dz_ref, du_ref, dbai_ref, dlam_ref):
        row = lax.broadcasted_iota(jnp.int32, (s, LANE), 0)
        uv, gv, hv, dyv = u_ref[...], g_ref[...], hs_ref[...], dy_ref[...]
        r, ig, sp, a, sq = _lru_gates(z_ref, bai_ref, lam_ref, uv)
        dg_ref[...] = (dyv * hv * _gelu_grad(gv)).astype(BF16)
        g = dyv * _gelu(gv)
        an = _shift_up(a, 1, row, s)
        for d in _scan_steps(s):
            an_sh = jnp.where(row < s - d, pltpu.roll(an, s - d, 0), 1.0)
            g = an * _shift_up(g, d, row, s) + g
            an = an * an_sh
        da = g * _shift_down(hv, 1, row)
        dsq = g * (ig * uv)
        di = g * sq * uv
        du_ref[...] = g * sq * ig
        dla = da * a - dsq * (a * a / sq)
        dzr = dla * (-RG_C * sp) * r * (1.0 - r)
        dzi = di * ig * (1.0 - ig)
        dz_ref[0] = dzr.astype(BF16)
        dz_ref[1] = dzi.astype(BF16)
        dbai_ref[0:1, :] = jnp.sum(dzr, axis=0, keepdims=True)
        dbai_ref[1:2, :] = jnp.sum(dzi, axis=0, keepdims=True)
        dlam_ref[...] = jnp.sum(dla * r, axis=0, keepdims=True) * (RG_C * jax.nn.sigmoid(-lam_ref[...]))

    return pl.pallas_call(
        kern, name="lru_scan_bwd", grid=(NCH,),
        in_specs=[BS((s, LANE), lambda ch: (0, ch)), zspec, _vec_chunk(2), _vec_chunk(1), _chunk_spec(s), _chunk_spec(s),
                  _chunk_spec(s)],
        out_specs=[_chunk_spec(s), zspec, _chunk_spec(s), _vec_chunk(2), _vec_chunk(1)],
        out_shape=[jax.ShapeDtypeStruct((LRU_NB, s, LRU_BW), BF16), jax.ShapeDtypeStruct((2, LRU_NB, s, LRU_BW), BF16),
                   jax.ShapeDtypeStruct((LRU_NB, s, LRU_BW), F32), jax.ShapeDtypeStruct((2, DM), F32),
                   jax.ShapeDtypeStruct((1, DM), F32)],
        compiler_params=_params(1))(dyp, z, bai, lam, u, gu, hs)


def _lru_layer_fwd(x, gpre, gpost, win, convw, convb, wai, bai, lam, wout, after=None):
    s = x.shape[0]
    tm = _tile(s, 512)
    h = _norm_fwd(x, gpre, after)
    gu = _bmm_nn("lru_in", h, win)
    u = _lru_conv_fwd(gu, convw, convb)
    z = _mm("lru_gate", u, wai, grid=(2, LRU_NB, s // tm, 1),
            a_spec=BS((None, tm, LRU_BW), lambda k, n, i, r: (n, i, 0)),
            b_spec=BS((None, None, LRU_BW, LRU_BW), lambda k, n, i, r: (k, n, 0, 0)),
            o_spec=BS((None, None, tm, LRU_BW), lambda k, n, i, r: (k, n, i, 0)),
            out_shape=(2, LRU_NB, s, LRU_BW), dn=NN)
    hs, yp = _lru_scan_fwd(z, bai, lam, u, gu)
    y = _mm_nn("lru_out", yp, wout)
    return _norm_res(x, y, gpost), (x, h, gu, u, z, hs, yp, y)


def _lru_layer_bwd(dxo, saved, gpre, gpost, win, convw, convb, wai, bai, lam, wout, after=None):
    x, h, gu, u, z, hs, yp, y = saved
    s = x.shape[0]
    tm = _tile(s, 512)
    dy, dgpost = _norm_bwd(y, dxo, gpost, None, BF16, after)
    dyp = _mm_nt("lru_dyp", dy, wout)
    dwout = _mm_tn("lru_dwout", yp, dy)
    dgate, dz, dud, dbai, dlam = _lru_scan_bwd(dyp, z, bai, lam, u, gu, hs)
    dwai = _mm("lru_dwai", u, dz, grid=(2, LRU_NB, s // tm),
               a_spec=BS((None, tm, LRU_BW), lambda k, n, r: (n, r, 0)),
               b_spec=BS((None, None, tm, LRU_BW), lambda k, n, r: (k, n, r, 0)),
               o_spec=BS((None, None, LRU_BW, LRU_BW), lambda k, n, r: (k, n, 0, 0)),
               out_shape=(2, LRU_NB, LRU_BW, LRU_BW), dn=TN)
    dug = _mm("lru_dug", dz, wai, grid=(LRU_NB, s // tm, 2),
              a_spec=BS((None, None, tm, LRU_BW), lambda n, i, k: (k, n, i, 0)),
              b_spec=BS((None, None, LRU_BW, LRU_BW), lambda n, i, k: (k, n, 0, 0)),
              o_spec=BS((None, tm, LRU_BW), lambda n, i, k: (n, i, 0)),
              out_shape=(LRU_NB, s, LRU_BW), dn=NT)
    duraw, dconvw, dconvb = _lru_conv_bwd(dud, dug, gu, convw)
    dgu = jnp.concatenate([dgate, duraw], axis=0)
    dwin = _bmm_tn("lru_dwin", h, dgu)
    dh = _bmm_nt_sum("lru_dh", dgu, win)
    dx, dgpre = _norm_bwd(x, dh, gpre, dxo, F32)
    return dx, dgpre, dgpost, dwin, dconvw, dconvb, dwai, dbai, dlam, dwout


CHIP_FLIPS = ((1, 0), (0, 1), (1, 1))


def _place():
    return lax.axis_index("x"), lax.axis_index("y"), lax.axis_index("c")


def _flip(v, f):
    return 1 - v if f else v


def _comm_params():
    return pltpu.CompilerParams(vmem_limit_bytes=VMEM_LIMIT)


def _all_gather(shards):
    n = len(shards)

    def body(*refs):
        ins, outs, stage = refs[:n], refs[n:2 * n], refs[2 * n:3 * n]
        send_sems, recv_sems, local_sems = refs[3 * n:]
        x, y, c = _place()
        me, sibling = (x, y, c), (x, y, 1 - c)
        chips = [(_flip(x, fx), _flip(y, fy)) for fx, fy in CHIP_FLIPS]

        def slot(t, p):
            return outs[t].at[:, 4 * p[0] + 2 * p[1] + p[2]]

        def copy(t, k, block, to, src=None):
            return pltpu.make_async_remote_copy(
                src_ref=slot(t, block) if src is None else src, dst_ref=slot(t, block),
                send_sem=send_sems.at[7 * t + k], recv_sem=recv_sems.at[7 * t + k], device_id=to, device_id_type=MESH)

        first = []
        for t in range(n):
            first.append(copy(t, 0, me, sibling, src=ins[t]))
            first += [copy(t, 1 + j, me, (*chip, c), src=ins[t]) for j, chip in enumerate(chips)]
        for cp in first:
            cp.start()
        load = [pltpu.make_async_copy(ins[t], stage[t], local_sems.at[t]) for t in range(n)]
        mine = [pltpu.make_async_copy(stage[t], slot(t, me), local_sems.at[t]) for t in range(n)]
        for cp in load:
            cp.start()
        for t in range(n):
            load[t].wait()
            mine[t].start()
        passed = []
        for j, chip in enumerate(chips):
            for t in range(n):
                copy(t, 1 + j, (*chip, c), me).wait_recv()
                fwd = copy(t, 4 + j, (*chip, c), sibling)
                fwd.start()
                passed.append(fwd)
        for t in range(n):
            copy(t, 0, sibling, me).wait_recv()
            for j, chip in enumerate(chips):
                copy(t, 4 + j, (*chip, 1 - c), me).wait_recv()
        for cp in first + passed:
            cp.wait_send()
        for cp in mine:
            cp.wait()

    outs = [jax.ShapeDtypeStruct((s.shape[0], NDEV) + s.shape[1:], s.dtype) for s in shards]
    return pl.pallas_call(body, name="all_gather", in_specs=[ANY] * n, out_specs=[ANY] * n, out_shape=outs,
                          scratch_shapes=[pltpu.VMEM(s.shape, s.dtype) for s in shards]
                          + [pltpu.SemaphoreType.DMA((7 * n,)), pltpu.SemaphoreType.DMA((7 * n,)),
                             pltpu.SemaphoreType.DMA((n,))],
                          compiler_params=_comm_params())(*shards)


def _small_gather(v):
    def body(v_ref, o_ref, send_sems, recv_sems, local_sem):
        x, y, c = _place()
        mine = 4 * x + 2 * y + c
        local = pltpu.make_async_copy(v_ref, o_ref.at[mine], local_sem)
        local.start()
        sends = []
        for k in range(1, NDEV):
            fx, fy, fc = (k >> 2) & 1, (k >> 1) & 1, k & 1
            sends.append(pltpu.make_async_remote_copy(
                src_ref=v_ref, dst_ref=o_ref.at[mine], send_sem=send_sems.at[k - 1], recv_sem=recv_sems.at[k - 1],
                device_id=(_flip(x, fx), _flip(y, fy), _flip(c, fc)), device_id_type=MESH))
        for cp in sends:
            cp.start()
        for k in range(1, NDEV):
            fx, fy, fc = (k >> 2) & 1, (k >> 1) & 1, k & 1
            src = 4 * _flip(x, fx) + 2 * _flip(y, fy) + _flip(c, fc)
            pltpu.make_async_remote_copy(src_ref=v_ref, dst_ref=o_ref.at[src], send_sem=send_sems.at[k - 1],
                                         recv_sem=recv_sems.at[k - 1], device_id=(x, y, c), device_id_type=MESH).wait_recv()
        for cp in sends:
            cp.wait_send()
        local.wait()

    return pl.pallas_call(body, name="small_gather", in_specs=[ANY], out_specs=ANY,
                          out_shape=jax.ShapeDtypeStruct((NDEV,) + v.shape, v.dtype),
                          scratch_shapes=[pltpu.SemaphoreType.DMA((NDEV - 1,)), pltpu.SemaphoreType.DMA((NDEV - 1,)),
                                          pltpu.SemaphoreType.DMA],
                          compiler_params=_comm_params())(v)


REL_CHIPS = ((0, 0),) + CHIP_FLIPS


def _rs_d2d(g5s, after=None):
    n = len(g5s)
    extra = () if after is None else (after,)

    def body(*refs):
        ins, gots = refs[:n], refs[n + len(extra):2 * n + len(extra)]
        send_sems, recv_sems = refs[2 * n + len(extra):]
        x, y, c = _place()
        copies = []
        for t in range(n):
            for f, (fx, fy) in enumerate(REL_CHIPS):
                copies.append(pltpu.make_async_remote_copy(
                    src_ref=ins[t].at[_flip(x, fx), _flip(y, fy), 1 - c], dst_ref=gots[t].at[f],
                    send_sem=send_sems.at[4 * t + f], recv_sem=recv_sems.at[4 * t + f], device_id=(x, y, 1 - c),
                    device_id_type=MESH))
        for cp in copies:
            cp.start()
        for cp in copies:
            cp.wait()

    out = [jax.ShapeDtypeStruct((4,) + g.shape[3:], F32) for g in g5s]
    return pl.pallas_call(body, name="rs_d2d", in_specs=[ANY] * (n + len(extra)), out_specs=[ANY] * n, out_shape=out,
                          scratch_shapes=[pltpu.SemaphoreType.DMA((4 * n,)), pltpu.SemaphoreType.DMA((4 * n,))],
                          compiler_params=_comm_params())(*g5s, *extra)


def _rs_ici(parts):
    n = len(parts)

    def body(*refs):
        ins, outs = refs[:n], refs[n:2 * n]
        send_sems, recv_sems = refs[2 * n:]
        x, y, c = _place()
        copies = []
        for t in range(n):
            for f, (fx, fy) in enumerate(CHIP_FLIPS):
                copies.append(pltpu.make_async_remote_copy(
                    src_ref=ins[t].at[f], dst_ref=outs[t].at[f], send_sem=send_sems.at[3 * t + f],
                    recv_sem=recv_sems.at[3 * t + f], device_id=(_flip(x, fx), _flip(y, fy), c), device_id_type=MESH))
        for cp in copies:
            cp.start()
        for cp in copies:
            cp.wait()

    out = [jax.ShapeDtypeStruct(p.shape, p.dtype) for p in parts]
    return pl.pallas_call(body, name="rs_ici", in_specs=[ANY] * n, out_specs=[ANY] * n, out_shape=out,
                          scratch_shapes=[pltpu.SemaphoreType.DMA((3 * n,)), pltpu.SemaphoreType.DMA((3 * n,))],
                          compiler_params=_comm_params())(*parts)


HBM = pl.BlockSpec(memory_space=pltpu.HBM)
SEM = pl.BlockSpec(memory_space=pltpu.SEMAPHORE)
EFFECT = pltpu.SideEffectType.DATAFLOW_SIDE_EFFECTING


def _in_hbm(a):
    return pltpu.with_memory_space_constraint(a, pltpu.HBM)


def _rs_ici_copies(ins, lands, send_sems, recv_sems):
    x, y, c = _place()
    return [pltpu.make_async_remote_copy(
        src_ref=ins[t].at[f], dst_ref=lands[t].at[f], send_sem=send_sems.at[3 * t + f], recv_sem=recv_sems.at[3 * t + f],
        device_id=(_flip(x, fx), _flip(y, fy), c), device_id_type=MESH)
        for t in range(len(ins)) for f, (fx, fy) in enumerate(CHIP_FLIPS)]


def _rs_ici_start(parts, name):
    n = len(parts)

    def body(*refs):
        ins, lands = refs[:n], refs[n:2 * n]
        send_sems, recv_sems = refs[2 * n], refs[2 * n + 1]
        token = refs[-1]
        for cp in _rs_ici_copies(ins, lands, send_sems, recv_sems):
            cp.start()
        token[...] = jnp.zeros(token.shape, token.dtype)

    thru = [pltpu.HBM(p.shape, p.dtype) for p in parts]
    res = pl.pallas_call(
        body, name=name, in_specs=[HBM] * (2 * n),
        out_shape=(pltpu.SemaphoreType.DMA((3 * n,)), pltpu.SemaphoreType.DMA((3 * n,)), *thru, *thru,
                   jax.ShapeDtypeStruct((8, LANE), F32)),
        out_specs=(SEM, SEM, *([HBM] * (2 * n)), pl.BlockSpec(memory_space=pltpu.VMEM)),
        input_output_aliases={i: 2 + i for i in range(2 * n)},
        compiler_params=pltpu.CompilerParams(has_side_effects=EFFECT),
    )(*[_in_hbm(p) for p in parts], *[_in_hbm(lax.empty(p.shape, p.dtype)) for p in parts])
    return res[:-1], res[-1]


def _rs_ici_wait(state, after, name):
    n = (len(state) - 2) // 2

    def body(*refs):
        send_sems, recv_sems = refs[0], refs[1]
        ins, lands = refs[2:2 + n], refs[2 + n:2 + 2 * n]
        for cp in _rs_ici_copies(ins, lands, send_sems, recv_sems):
            cp.wait_send()
            cp.wait_recv()

    thru = [pltpu.HBM(s.shape, s.dtype) for s in state[2:]]
    res = pl.pallas_call(
        body, name=name, in_specs=[SEM, SEM] + [HBM] * (2 * n) + [ANY], out_shape=tuple(thru),
        out_specs=tuple([HBM] * (2 * n)), input_output_aliases={2 + i: i for i in range(2 * n)},
        compiler_params=pltpu.CompilerParams(has_side_effects=EFFECT),
    )(*state, after)
    return list(res[n:])


def _ag_copies(shards, lands, send_sems, recv_sems):
    x, y, c = _place()
    mine = 4 * x + 2 * y + c
    peers = [(x, y, 1 - c)] + [(_flip(x, fx), _flip(y, fy), c) for fx, fy in CHIP_FLIPS]
    return [pltpu.make_async_remote_copy(
        src_ref=shards[t], dst_ref=lands[t].at[mine], send_sem=send_sems.at[4 * t + k], recv_sem=recv_sems.at[4 * t + k],
        device_id=peer, device_id_type=MESH) for t in range(len(shards)) for k, peer in enumerate(peers)]


def _ag_start(shards, after, name):
    n = len(shards)

    def body(*refs):
        ins, lands = refs[:n], refs[n:2 * n]
        send_sems, recv_sems = refs[2 * n + 1], refs[2 * n + 2]
        token = refs[-1]
        for cp in _ag_copies(ins, lands, send_sems, recv_sems):
            cp.start()
        token[...] = jnp.zeros(token.shape, token.dtype)

    thru = [pltpu.HBM(s.shape, s.dtype) for s in shards]
    land = [pltpu.HBM((NDEV,) + s.shape, s.dtype) for s in shards]
    res = pl.pallas_call(
        body, name=name, in_specs=[HBM] * (2 * n) + [ANY],
        out_shape=(pltpu.SemaphoreType.DMA((4 * n,)), pltpu.SemaphoreType.DMA((4 * n,)), *thru, *land,
                   jax.ShapeDtypeStruct((8, LANE), F32)),
        out_specs=(SEM, SEM, *([HBM] * (2 * n)), pl.BlockSpec(memory_space=pltpu.VMEM)),
        input_output_aliases={i: 2 + i for i in range(2 * n)},
        compiler_params=pltpu.CompilerParams(has_side_effects=EFFECT),
    )(*[_in_hbm(s) for s in shards], *[_in_hbm(lax.empty((NDEV,) + s.shape, s.dtype)) for s in shards], after)
    return res[:-1], res[-1]


def _ag_wait(state, after, name):
    n = (len(state) - 2) // 2

    def body(*refs):
        send_sems, recv_sems = refs[0], refs[1]
        ins, lands = refs[2:2 + n], refs[2 + n:2 + 2 * n]
        for cp in _ag_copies(ins, lands, send_sems, recv_sems):
            cp.wait_send()
            cp.wait_recv()

    thru = [pltpu.HBM(s.shape, s.dtype) for s in state[2:]]
    res = pl.pallas_call(
        body, name=name, in_specs=[SEM, SEM] + [HBM] * (2 * n) + [ANY], out_shape=tuple(thru),
        out_specs=tuple([HBM] * (2 * n)), input_output_aliases={2 + i: i for i in range(2 * n)},
        compiler_params=pltpu.CompilerParams(has_side_effects=EFFECT),
    )(*state, after)
    return list(res[:n]), list(res[n:])


def _ag_finish(shards, lands):
    n = len(shards)

    def body(*refs):
        ins, outs, stage = refs[:n], refs[2 * n:3 * n], refs[3 * n:4 * n]
        send_sems, recv_sems, local_sems = refs[4 * n:]
        x, y, c = _place()
        chips = [(_flip(x, fx), _flip(y, fy)) for fx, fy in CHIP_FLIPS]

        def passing(t, j, core, to):
            blk = outs[t].at[4 * chips[j][0] + 2 * chips[j][1] + core]
            return pltpu.make_async_remote_copy(src_ref=blk, dst_ref=blk, send_sem=send_sems.at[3 * t + j],
                                                recv_sem=recv_sems.at[3 * t + j], device_id=to, device_id_type=MESH)

        sends = [passing(t, j, c, (x, y, 1 - c)) for t in range(n) for j in range(3)]
        for cp in sends:
            cp.start()
        load = [pltpu.make_async_copy(ins[t], stage[t], local_sems.at[t]) for t in range(n)]
        mine = [pltpu.make_async_copy(stage[t], outs[t].at[4 * x + 2 * y + c], local_sems.at[t]) for t in range(n)]
        for cp in load:
            cp.start()
        for t in range(n):
            load[t].wait()
            mine[t].start()
        for t in range(n):
            for j in range(3):
                passing(t, j, 1 - c, (x, y, c)).wait_recv()
        for cp in sends:
            cp.wait_send()
        for cp in mine:
            cp.wait()

    return pl.pallas_call(
        body, name="ag_finish", in_specs=[ANY] * (2 * n), out_specs=[ANY] * n,
        out_shape=[jax.ShapeDtypeStruct(l.shape, l.dtype) for l in lands],
        input_output_aliases={n + i: i for i in range(n)},
        scratch_shapes=[pltpu.VMEM(s.shape, s.dtype) for s in shards]
        + [pltpu.SemaphoreType.DMA((3 * n,)), pltpu.SemaphoreType.DMA((3 * n,)), pltpu.SemaphoreType.DMA((n,))],
        compiler_params=_comm_params())(*shards, *lands)


def _row_tile(rows, largest=256):
    for t in (1024, 512, 256, 128, 64, 32, 16, 8):
        if t > largest:
            continue
        if rows % t == 0:
            return t
    return rows


def _rs_chip_sum(pos, g5, got):
    a, b = g5.shape[3:]
    ta = _row_tile(a, 1024)

    def kern(pos_ref, o_ref, g_ref, p_ref):
        p_ref[...] = (o_ref[...] + g_ref[...]).astype(BF16)

    def mine(f, i, pos_ref):
        return (pos_ref[0] ^ ((f + 1) & 1), pos_ref[1] ^ ((f + 1) >> 1), pos_ref[2], i, 0)

    spec = pltpu.PrefetchScalarGridSpec(
        num_scalar_prefetch=1, grid=(3, a // ta),
        in_specs=[BS((None, None, None, ta, b), mine), BS((None, ta, b), lambda f, i, pos_ref: (f + 1, i, 0))],
        out_specs=BS((None, ta, b), lambda f, i, pos_ref: (f, i, 0)))
    return pl.pallas_call(kern, name="rs_chip_sum", grid_spec=spec, out_shape=jax.ShapeDtypeStruct((3, a, b), BF16),
                          compiler_params=_params(2))(pos, g5, got)


def _rs_final_sum(pos, g5, got, recv):
    a, b = g5.shape[3:]
    ta = _row_tile(a, 1024)

    def kern(pos_ref, o_ref, g_ref, r_ref, s_ref):
        acc = o_ref[...] + g_ref[...]
        for f in range(3):
            acc = acc + r_ref[f].astype(F32)
        s_ref[...] = acc

    spec = pltpu.PrefetchScalarGridSpec(
        num_scalar_prefetch=1, grid=(a // ta,),
        in_specs=[BS((None, None, None, ta, b), lambda i, pos_ref: (pos_ref[0], pos_ref[1], pos_ref[2], i, 0)),
                  BS((None, ta, b), lambda i, pos_ref: (0, i, 0)), BS((3, ta, b), lambda i, pos_ref: (0, i, 0))],
        out_specs=BS((ta, b), lambda i, pos_ref: (i, 0)))
    return pl.pallas_call(kern, name="rs_final_sum", grid_spec=spec, out_shape=jax.ShapeDtypeStruct((a, b), F32),
                          compiler_params=_params(1))(pos, g5, got, recv)


def _reduce_scatter(grads, pos):
    g5s = [g.reshape((2, 2, 2) + g.shape[1:]) for g in grads]
    gots = _rs_d2d(g5s)
    parts = [_rs_chip_sum(pos, g, got) for g, got in zip(g5s, gots)]
    recvs = _rs_ici(parts)
    return [_rs_final_sum(pos, g, got, r) for g, got, r in zip(g5s, gots, recvs)]


def _rs_d2d_copies(ins, lands, send_sems, recv_sems):
    x, y, c = _place()
    return [pltpu.make_async_remote_copy(
        src_ref=ins[t].at[_flip(x, fx), _flip(y, fy), 1 - c], dst_ref=lands[t].at[f], send_sem=send_sems.at[4 * t + f],
        recv_sem=recv_sems.at[4 * t + f], device_id=(x, y, 1 - c), device_id_type=MESH)
        for t in range(len(ins)) for f, (fx, fy) in enumerate(REL_CHIPS)]


def _rs_d2d_start(g5s, name):
    n = len(g5s)

    def body(*refs):
        ins, lands = refs[:n], refs[n:2 * n]
        for cp in _rs_d2d_copies(ins, lands, refs[2 * n], refs[2 * n + 1]):
            cp.start()
        refs[-1][...] = jnp.zeros(refs[-1].shape, F32)

    thru = [pltpu.HBM(g.shape, g.dtype) for g in g5s]
    land = [pltpu.HBM((4,) + g.shape[3:], F32) for g in g5s]
    res = pl.pallas_call(
        body, name=name, in_specs=[HBM] * (2 * n),
        out_shape=(pltpu.SemaphoreType.DMA((4 * n,)), pltpu.SemaphoreType.DMA((4 * n,)), *thru, *land,
                   jax.ShapeDtypeStruct((8, LANE), F32)),
        out_specs=(SEM, SEM, *([HBM] * (2 * n)), pl.BlockSpec(memory_space=pltpu.VMEM)),
        input_output_aliases={i: 2 + i for i in range(2 * n)},
        compiler_params=pltpu.CompilerParams(has_side_effects=EFFECT),
    )(*[_in_hbm(g) for g in g5s], *[_in_hbm(lax.empty((4,) + g.shape[3:], F32)) for g in g5s])
    return res[:-1], res[-1]


def _rs_d2d_wait(state, after, name):
    n = (len(state) - 2) // 2

    def body(*refs):
        ins, lands = refs[2:2 + n], refs[2 + n:2 + 2 * n]
        for cp in _rs_d2d_copies(ins, lands, refs[0], refs[1]):
            cp.wait_send()
            cp.wait_recv()

    thru = [pltpu.HBM(s.shape, s.dtype) for s in state[2:]]
    res = pl.pallas_call(
        body, name=name, in_specs=[SEM, SEM] + [HBM] * (2 * n) + [ANY], out_shape=tuple(thru),
        out_specs=tuple([HBM] * (2 * n)), input_output_aliases={2 + i: i for i in range(2 * n)},
        compiler_params=pltpu.CompilerParams(has_side_effects=EFFECT),
    )(*state, after)
    return list(res[:n]), list(res[n:])


def _as_g5(grads):
    return [g.reshape((2, 2, 2) + g.shape[1:]) for g in grads]


def _rs_mid(g5s, gots, pos, tag):
    parts = [_rs_chip_sum(pos, g, got) for g, got in zip(g5s, gots)]
    state, token = _rs_ici_start(parts, "rs_ici_start_" + tag)
    return (g5s, gots, state, tag), token


def _rs_begin(grads, pos, tag, after=None):
    g5s = _as_g5(grads)
    return _rs_mid(g5s, _rs_d2d(g5s, after), pos, tag)


def _rs_end(pending, after, pos):
    g5s, gots, state, tag = pending
    recvs = _rs_ici_wait(state, after, "rs_ici_wait_" + tag)
    return [_rs_final_sum(pos, g, got, r) for g, got, r in zip(g5s, gots, recvs)]


def _sum_devices(v):
    _, r, _ = v.shape

    def kern(v_ref, o_ref):
        acc = v_ref[0]
        for d in range(1, NDEV):
            acc = acc + v_ref[d]
        o_ref[...] = acc

    return pl.pallas_call(kern, name="sum_devices", out_shape=jax.ShapeDtypeStruct((r, LANE), F32),
                          compiler_params=_comm_params())(v)


def _loss_head(xf, target):
    s = xf.shape[0]
    tm = _tile(s, 512)

    def kern(x_ref, t_ref, dx_ref, l_ref):
        err = x_ref[...] - t_ref[...]
        dx_ref[...] = err * (1.0 / DM)
        part = jnp.broadcast_to(0.5 * jnp.sum(jnp.mean(err * err, axis=-1, keepdims=True), axis=0, keepdims=True), (8, LANE))

        @pl.when(pl.program_id(0) == 0)
        def _():
            l_ref[...] = part

        @pl.when(pl.program_id(0) > 0)
        def _():
            l_ref[...] += part

    row = BS((tm, DM), lambda i: (i, 0))
    return pl.pallas_call(kern, name="loss_head", grid=(s // tm,), in_specs=[row, row],
                          out_specs=[row, BS((8, LANE), lambda i: (0, 0))],
                          out_shape=[jax.ShapeDtypeStruct((s, DM), F32), jax.ShapeDtypeStruct((8, LANE), F32)],
                          compiler_params=_params(1))(xf, target)


def _adamw(w, g, m, v, after=None):
    rows, cols = w.shape
    tr = _row_tile(rows)
    extra = () if after is None else (after,)

    def kern(w_ref, g_ref, m_ref, v_ref, *rest):
        d_ref, nm_ref, nv_ref = rest[-3:]
        gv = g_ref[...]
        nm = ADAM_B1 * m_ref[...] + (1.0 - ADAM_B1) * gv
        nv = ADAM_B2 * v_ref[...] + (1.0 - ADAM_B2) * (gv * gv)
        m_hat = nm / (1.0 - ADAM_B1 ** ADAM_STEP)
        v_hat = nv / (1.0 - ADAM_B2 ** ADAM_STEP)
        d_ref[...] = -ADAM_LR * (m_hat / (jnp.sqrt(v_hat) + ADAM_EPS) + ADAM_WD * w_ref[...])
        nm_ref[...] = nm
        nv_ref[...] = nv

    blk = BS((tr, cols), lambda i: (i, 0))
    shp = jax.ShapeDtypeStruct((rows, cols), F32)
    return pl.pallas_call(kern, name="adamw", grid=(rows // tr,), in_specs=[blk] * 4 + [ANY] * len(extra),
                          out_specs=[blk] * 3, out_shape=[shp] * 3, compiler_params=_params(1))(w, g, m, v, *extra)


def _adamw_nd(w, g, m, v, after=None):
    shape = w.shape
    two = (math.prod(shape[:-1]), shape[-1])
    return tuple(o.reshape(shape)
                 for o in _adamw(w.reshape(two), g.reshape(two), m.reshape(two), v.reshape(two), after))


def _pack_small(parts):
    flat = jnp.concatenate([p.reshape(-1) for p in parts])
    pad = (-flat.shape[0]) % (8 * LANE)
    return jnp.pad(flat, (0, pad)).reshape(-1, LANE)


def _unpack_small(packed, shapes, lead=()):
    flat = packed.reshape(lead + (-1,))
    out, off = [], 0
    for shp in shapes:
        n = math.prod(shp)
        out.append(flat[..., off:off + n].reshape(lead + tuple(shp)))
        off += n
    return out


def _blocks_of_columns(w):
    k, n = w.shape
    return w.reshape(k, NDEV, n // NDEV).transpose(1, 0, 2)


def _columns_of_blocks(wb):
    n, k, c = wb.shape
    return wb.transpose(1, 0, 2).reshape(k, n * c)


WEIGHT_NAMES = ('g_mix_pre', 'g_mix_post', 'g_cross_pre', 'g_mem', 'g_cross_post', 'g_ffn_pre', 'g_ffn_post', 'w_xq',
                'w_xkv', 'w_xo', 'w_ffn_gu', 'w_ffn_down', 'ab_w_in', 'ab_b_f', 'ab_conv_w', 'ab_w_out', 'c_w_in',
                'c_conv_w', 'c_conv_b', 'c_w_a', 'c_b_a', 'c_w_i', 'c_b_i', 'c_lam', 'c_w_out')
BIG = ('w_xq', 'w_xkv', 'w_xo', 'w_ffn_gu', 'w_ffn_down', 'ab_w_in', 'ab_w_out', 'c_w_in', 'c_w_a', 'c_w_i', 'c_w_out')
SMALL_SHARDED = ('ab_conv_w', 'c_conv_w', 'c_conv_b', 'c_b_a', 'c_b_i', 'c_lam')
REPLICATED = ('g_mix_pre', 'g_mix_post', 'g_cross_pre', 'g_mem', 'g_cross_post', 'g_ffn_pre', 'g_ffn_post', 'ab_b_f')


def _small_full(name, gathered):
    nd = gathered.ndim
    return jnp.moveaxis(gathered, 0, nd - 2).reshape(gathered.shape[1:-1] + (NDEV * gathered.shape[-1],))


def _small_shard(full, dev):
    c = full.shape[-1] // NDEV
    return lax.dynamic_slice_in_dim(full, dev * c, c, axis=full.ndim - 1)


def kernel(x, mem, g_mix_pre, g_mix_post, g_cross_pre, g_mem, g_cross_post, g_ffn_pre, g_ffn_post, w_xq, w_xkv, w_xo, w_ffn_gu, w_ffn_down, ab_w_in, ab_b_f, ab_conv_w, ab_w_out, c_w_in, c_conv_w, c_conv_b, c_w_a, c_b_a, c_w_i, c_b_i, c_lam, c_w_out, loss_target, m_g_mix_pre, m_g_mix_post, m_g_cross_pre, m_g_mem, m_g_cross_post, m_g_ffn_pre, m_g_ffn_post, m_w_xq, m_w_xkv, m_w_xo, m_w_ffn_gu, m_w_ffn_down, m_ab_w_in, m_ab_b_f, m_ab_conv_w, m_ab_w_out, m_c_w_in, m_c_conv_w, m_c_conv_b, m_c_w_a, m_c_b_a, m_c_w_i, m_c_b_i, m_c_lam, m_c_w_out, v_g_mix_pre, v_g_mix_post, v_g_cross_pre, v_g_mem, v_g_cross_post, v_g_ffn_pre, v_g_ffn_post, v_w_xq, v_w_xkv, v_w_xo, v_w_ffn_gu, v_w_ffn_down, v_ab_w_in, v_ab_b_f, v_ab_conv_w, v_ab_w_out, v_c_w_in, v_c_conv_w, v_c_conv_b, v_c_w_a, v_c_b_a, v_c_w_i, v_c_b_i, v_c_lam, v_c_w_out):
    args = locals()
    w = {n: args[n] for n in WEIGHT_NAMES}
    mom = {n: args["m_" + n] for n in WEIGHT_NAMES}
    var = {n: args["v_" + n] for n in WEIGHT_NAMES}
    pos = jnp.stack([lax.axis_index("x"), lax.axis_index("y"), lax.axis_index("c")]).astype(jnp.int32)
    dev = 4 * pos[0] + 2 * pos[1] + pos[2]
    xs, mems, target = x[0], mem[0], loss_target[0]
    n_even, n_odd = (DEPTH + 1) // 2, DEPTH // 2

    small_shapes = [w[n].shape for n in SMALL_SHARDED]
    small_w_all = _small_gather(_pack_small([w[n] for n in SMALL_SHARDED]))
    gathered_small = _unpack_small(small_w_all, small_shapes, (NDEV,))
    small = {n: _small_full(n, g) for n, g in zip(SMALL_SHARDED, gathered_small)}
    ab_bfb = jnp.broadcast_to(ab_b_f[:, :, None], (n_even, FOX_H, LANE))
    c_bai = jnp.stack([small['c_b_a'].reshape(n_odd, DM), small['c_b_i'].reshape(n_odd, DM)], axis=1)
    row = lambda a, l: a[l][None]

    REST = ('w_xq', 'w_xkv', 'w_xo', 'w_ffn_gu', 'w_ffn_down')

    def mixer_names(l):
        return ('ab_w_in', 'ab_w_out') if l % 2 == 0 else ('c_w_in', 'c_w_a', 'c_w_i', 'c_w_out')

    def shards_of(l, names):
        out = []
        for n in names:
            s = w[n][l if w[n].shape[0] == DEPTH else l // 2].astype(BF16)
            out.append(s.reshape(-1, s.shape[-1]))
        return out

    def mixer_weights(l, full):
        if l % 2 == 0:
            e = l // 2
            return (row(g_mix_pre, l), row(g_mix_post, l), _ab_pack(_columns_of_blocks(full['ab_w_in'])), ab_bfb[e],
                    small['ab_conv_w'][e], full['ab_w_out'].reshape(DM, DM))
        o = l // 2
        gate_w = lambda g: g.reshape(NDEV, LRU_NB, LRU_BW // NDEV, LRU_BW).transpose(1, 0, 2, 3).reshape(
            LRU_NB, LRU_BW, LRU_BW)
        return (row(g_mix_pre, l), row(g_mix_post, l), full['c_w_in'], small['c_conv_w'][o], row(small['c_conv_b'], o),
                jnp.stack([gate_w(full['c_w_a']), gate_w(full['c_w_i'])]), c_bai[o], row(small['c_lam'], o),
                full['c_w_out'].reshape(DM, DM))

    def rest_weights(l, full):
        cross = (row(g_cross_pre, l), row(g_mem, l), row(g_cross_post, l), full['w_xq'].reshape(DM, DM), full['w_xkv'],
                 full['w_xo'].reshape(DM, DM))
        ffn = (row(g_ffn_pre, l), row(g_ffn_post, l), full['w_ffn_gu'], full['w_ffn_down'].reshape(D_FF, DM))
        return cross, ffn

    def gathered(state, names, after, tag):
        shards, lands = _ag_wait(state, after, "ag_wait_" + tag)
        full = _ag_finish(shards, lands)
        return dict(zip(names, full)), full[0]

    saved, weights = [], []
    h = xs
    names_of = lambda l: mixer_names(l) + REST
    states = {}
    st_m, _ = _ag_start(shards_of(0, mixer_names(0)), small_w_all, "ag_start_0m")
    st_r, _ = _ag_start(shards_of(0, REST), st_m[2], "ag_start_0r")
    states[1], token = _ag_start(shards_of(1, names_of(1)), st_r[2], "ag_start_1")
    full_m, _ = gathered(st_m, mixer_names(0), xs, "0m")
    for l in range(DEPTH):
        if l > 0:
            full, done = gathered(states[l], names_of(l), h, str(l))
            full_m = full_r = full
            token = None
            if l + 2 < DEPTH:
                states[l + 2], token = _ag_start(shards_of(l + 2, names_of(l + 2)), done, "ag_start_%d" % (l + 2))
        mixer = mixer_weights(l, full_m)
        h, s_mix = (_fox_layer_fwd if l % 2 == 0 else _lru_layer_fwd)(h, *mixer, after=token)
        token = None
        if l == 0:
            full_r, done = gathered(st_r, REST, h, "0r")
            states[2], token = _ag_start(shards_of(2, names_of(2)), done, "ag_start_2")
        cross, ffn = rest_weights(l, full_r)
        h, s_cross = _cross_fwd(h, mems, *cross, after=token)
        h, s_ffn = _ffn_fwd(h, *ffn)
        saved.append((s_mix, s_cross, s_ffn))
        weights.append((mixer, cross, ffn))
    mixer_args = lambda l: weights[l][0]
    cross_args = lambda l: weights[l][1]
    ffn_args = lambda l: weights[l][2]
    dx, loss_rep = _loss_head(h, target)
    loss = lax.psum(loss_rep[0, 0], ("x", "y", "c"))

    grads = {n: [None] * w[n].shape[0] for n in BIG}
    partial = {n: [None] * w[n].shape[0] for n in REPLICATED + SMALL_SHARDED}
    def finish(pending, after):
        state, names, where = pending
        for n, g in zip(names, _rs_end(state, after, pos)):
            grads[n][where[n]] = g

    def unit(layer, names):
        return [layer[n][1] for n in names], names, {n: layer[n][0] for n in names}

    d2d = ici = None
    token = None
    for l in reversed(range(DEPTH)):
        s_mix, s_cross, s_ffn = saved[l]
        dx, partial['g_ffn_pre'][l], partial['g_ffn_post'][l], dwgu, dwd = _ffn_bwd(dx, s_ffn, *ffn_args(l), after=token)
        token = None
        if d2d is not None:
            g5s, gots = _rs_d2d_wait(d2d[0], dx, "rs_d2d_wait_%d" % (l + 1))
            state, token = _rs_mid(g5s, gots, pos, str(l + 1))
            ici, d2d = (state,) + d2d[1:], None
        (dx, partial['g_cross_pre'][l], partial['g_mem'][l], partial['g_cross_post'][l], dwq, dwkv, dwo) = _cross_bwd(
            dx, s_cross, mems, *cross_args(l), after=token)
        token = None
        layer = {'w_xq': (l, dwq.reshape(NDEV, DM // NDEV, DM)), 'w_xkv': (l, dwkv), 'w_xo': (l, dwo.reshape(NDEV, DM // NDEV, DM)),
                 'w_ffn_gu': (l, dwgu), 'w_ffn_down': (l, dwd.reshape(NDEV, D_FF // NDEV, DM))}
        if l == 0:
            gs, names, where = unit(layer, REST)
            state, token = _rs_begin(gs, pos, "0r")
            ici_rest = (state, names, where)
        if l % 2 == 0:
            e = l // 2
            (dx, partial['g_mix_pre'][l], partial['g_mix_post'][l], dwall, partial['ab_b_f'][e], partial['ab_conv_w'][e],
             dwout) = _fox_layer_bwd(dx, s_mix, *mixer_args(l), after=token)
            layer['ab_w_in'] = (e, _blocks_of_columns(_ab_unpack(dwall)))
            layer['ab_w_out'] = (e, dwout.reshape(NDEV, DM // NDEV, DM))
        else:
            o = l // 2
            (dx, partial['g_mix_pre'][l], partial['g_mix_post'][l], dwin, partial['c_conv_w'][o], dconvb, dwai, dbai, dlam,
             dwout) = _lru_layer_bwd(dx, s_mix, *mixer_args(l), after=token)
            partial['c_conv_b'][o], partial['c_lam'][o] = dconvb[0], dlam[0]
            partial['c_b_a'][o], partial['c_b_i'][o] = dbai[0].reshape(LRU_NB, LRU_BW), dbai[1].reshape(LRU_NB, LRU_BW)
            rows = LRU_BW // NDEV
            by_dev = lambda d: d.reshape(LRU_NB, NDEV, rows, LRU_BW).transpose(1, 0, 2, 3).reshape(NDEV, LRU_NB * rows, LRU_BW)
            layer['c_w_in'] = (o, dwin)
            layer['c_w_a'] = (o, by_dev(dwai[0]))
            layer['c_w_i'] = (o, by_dev(dwai[1]))
            layer['c_w_out'] = (o, dwout.reshape(NDEV, DM // NDEV, DM))
        token = None
        if ici is not None:
            finish(ici, dx)
            ici = None
        if l > 0:
            gs, names, where = unit(layer, list(layer))
            state, token = _rs_d2d_start(_as_g5(gs), "rs_d2d_start_%d" % l)
            d2d = (state, names, where)
    small_names = REPLICATED + SMALL_SHARDED
    small_parts = [jnp.stack([p.reshape(w[n].shape[1:] if n in REPLICATED else small[n].shape[1:]) for p in partial[n]])
                   for n in small_names]
    small_all = _small_gather(_pack_small(small_parts))
    reduced = _unpack_small(_sum_devices(small_all), [p.shape for p in small_parts])
    grad = {}
    for n, g in zip(small_names, reduced):
        grad[n] = g if n in REPLICATED else _small_shard(g, dev)

    gs, names, where = unit(layer, mixer_names(0))
    state, token = _rs_begin(gs, pos, "0m", after=small_all)
    ici_mixer = (state, names, where)
    finish(ici_rest, dx)

    delta, new_m, new_v = {}, {}, {}
    last = mixer_names(0)
    for n in BIG:
        if n not in last:
            grad[n] = jnp.stack(grads[n]).reshape(w[n].shape)
            delta[n], new_m[n], new_v[n] = _adamw_nd(w[n], grad[n], mom[n], var[n], token)
            token = delta[n]
    shapes = [w[n].shape for n in small_names]
    packed = [_pack_small([t[n] for n in small_names]) for t in (w, grad, mom, var)]
    res_small = _adamw(*packed, after=token)
    for res, out in zip(res_small, (delta, new_m, new_v)):
        for n, val in zip(small_names, _unpack_small(res, shapes)):
            out[n] = val
    finish(ici_mixer, res_small[0])
    for n in last:
        grad[n] = jnp.stack(grads[n]).reshape(w[n].shape)
        delta[n], new_m[n], new_v[n] = _adamw_nd(w[n], grad[n], mom[n], var[n])

    return (loss, dx[None], *[grad[n] for n in WEIGHT_NAMES], *[delta[n] for n in WEIGHT_NAMES],
            *[new_m[n] for n in WEIGHT_NAMES], *[new_v[n] for n in WEIGHT_NAMES])
```

```python
import functools
import math

import jax
import jax.numpy as jnp
from jax import lax
from jax.experimental import pallas as pl
from jax.experimental.pallas import tpu as pltpu

F32 = jnp.float32
BF16 = jnp.bfloat16
BS = pl.BlockSpec
ANY = pl.BlockSpec(memory_space=pl.ANY)
MESH = pl.DeviceIdType.MESH

DM = 1024
DEPTH = 4
EPS = 1e-6
NEG = -1e30
FOX_W = 512
FOX_HD = 64
FOX_H = 8
SC_W = 512
SC_K = 3
AB_IN = 3 * FOX_W + FOX_H + 3 * SC_W
AB_PAD = 3200
LRU_BW = 256
LRU_NB = 4
RG_K = 4
RG_C = 8.0
MEM_H = 4
MEM_HD = 256
D_FF = 2816
NDEV = 8
FFB = 2 * D_FF // NDEV
ADAM_LR, ADAM_B1, ADAM_B2, ADAM_EPS, ADAM_WD, ADAM_STEP = 0.001, 0.9, 0.999, 1e-08, 0.01, 10

LANE = 128
VMEM_LIMIT = 48 * 1024 * 1024


def _params(ngrid):
    return pltpu.CompilerParams(dimension_semantics=("arbitrary",) * ngrid, vmem_limit_bytes=VMEM_LIMIT)


def _call(kern, **kwargs):
    pin = lambda s: pltpu.HBM(s.shape, s.dtype)
    out = kwargs.pop("out_shape")
    kwargs["out_shape"] = [pin(s) for s in out] if isinstance(out, (list, tuple)) else pin(out)
    call = pl.pallas_call(kern, **kwargs)

    def run(*operands):
        return call(*[pltpu.with_memory_space_constraint(o, pltpu.HBM) if o.ndim >= 2 else o for o in operands])

    return run


TK_RED = 2048
TM_SUM = 1024


def _tile(n, t):
    return t if n % t == 0 else n


def _mm(name, a, b, *, grid, a_spec, b_spec, o_spec, out_shape, dn, out_dtype=F32):
    nred = grid[-1]
    ngrid = len(grid)

    def kern(a_ref, b_ref, o_ref, *scratch):
        p = lax.dot_general(a_ref[...].astype(BF16), b_ref[...].astype(BF16), (dn, ((), ())),
                            preferred_element_type=F32)
        if nred == 1:
            o_ref[...] = p.astype(o_ref.dtype)
            return
        acc = scratch[0] if scratch else o_ref
        r = pl.program_id(ngrid - 1)

        @pl.when(r == 0)
        def _():
            acc[...] = p

        @pl.when(r > 0)
        def _():
            acc[...] += p

        if scratch:
            @pl.when(r == nred - 1)
            def _():
                o_ref[...] = acc[...].astype(o_ref.dtype)

    blk = tuple(d for d in o_spec.block_shape if d is not None)
    scratch = [pltpu.VMEM(blk, F32)] if (nred > 1 and out_dtype != F32) else []
    return _call(kern, name=name, grid=grid, in_specs=[a_spec, b_spec], out_specs=o_spec,
                          out_shape=jax.ShapeDtypeStruct(out_shape, out_dtype), scratch_shapes=scratch,
                          compiler_params=_params(ngrid))(a, b)


NN = ((1,), (0,))
NT = ((1,), (1,))
TN = ((0,), (0,))


def _mm_nn(name, a, w, out_dtype=F32, tn=None):
    m, k = a.shape
    n = w.shape[1]
    tm = _tile(m, 512)
    tn = n if tn is None else tn
    return _mm(name, a, w, grid=(m // tm, n // tn, 1), a_spec=BS((tm, k), lambda i, j, r: (i, 0)),
               b_spec=BS((k, tn), lambda i, j, r: (0, j)), o_spec=BS((tm, tn), lambda i, j, r: (i, j)),
               out_shape=(m, n), dn=NN, out_dtype=out_dtype)


def _mm_nt(name, a, w, out_dtype=F32, tn=None):
    m, n = a.shape
    k = w.shape[0]
    tm = _tile(m, 512)
    tn = n if tn is None else tn
    return _mm(name, a, w, grid=(m // tm, n // tn), a_spec=BS((tm, tn), lambda i, r: (i, r)),
               b_spec=BS((k, tn), lambda i, r: (0, r)), o_spec=BS((tm, k), lambda i, r: (i, 0)),
               out_shape=(m, k), dn=NT, out_dtype=out_dtype)


def _mm_tn(name, a, b, tn=None):
    m, k = a.shape
    n = b.shape[1]
    tm = _tile(m, TK_RED)
    tn = n if tn is None else tn
    return _mm(name, a, b, grid=(n // tn, m // tm), a_spec=BS((tm, k), lambda j, r: (r, 0)),
               b_spec=BS((tm, tn), lambda j, r: (r, j)), o_spec=BS((k, tn), lambda j, r: (0, j)),
               out_shape=(k, n), dn=TN)


def _bmm_nn(name, a, w, out_dtype=F32):
    m, k = a.shape
    g, _, n = w.shape
    tm = _tile(m, 512)
    return _mm(name, a, w, grid=(g, m // tm, 1), a_spec=BS((tm, k), lambda q, i, r: (i, 0)),
               b_spec=BS((None, k, n), lambda q, i, r: (q, 0, 0)), o_spec=BS((None, tm, n), lambda q, i, r: (q, i, 0)),
               out_shape=(g, m, n), dn=NN, out_dtype=out_dtype)


def _bmm_tn(name, a, b):
    m, k = a.shape
    g, _, n = b.shape
    tm = _tile(m, TK_RED)
    return _mm(name, a, b, grid=(g, m // tm), a_spec=BS((tm, k), lambda q, r: (r, 0)),
               b_spec=BS((None, tm, n), lambda q, r: (q, r, 0)), o_spec=BS((None, k, n), lambda q, r: (q, 0, 0)),
               out_shape=(g, k, n), dn=TN)


def _bmm_nt_sum(name, a, w):
    g, m, n = a.shape
    k = w.shape[1]
    tm = _tile(m, TM_SUM)
    return _mm(name, a, w, grid=(m // tm, g), a_spec=BS((None, tm, n), lambda i, q: (q, i, 0)),
               b_spec=BS((None, k, n), lambda i, q: (q, 0, 0)), o_spec=BS((tm, k), lambda i, q: (i, 0)),
               out_shape=(m, k), dn=NT)


def _bmm_nn_sum(name, a, w):
    g, m, k = a.shape
    n = w.shape[2]
    tm = _tile(m, TM_SUM)
    return _mm(name, a, w, grid=(m // tm, g), a_spec=BS((None, tm, k), lambda i, q: (q, i, 0)),
               b_spec=BS((None, k, n), lambda i, q: (q, 0, 0)), o_spec=BS((tm, n), lambda i, q: (i, 0)),
               out_shape=(m, n), dn=NN)


def _bbmm_tn(name, a, b):
    g, m, k = a.shape
    n = b.shape[2]
    tm = _tile(m, TK_RED)
    return _mm(name, a, b, grid=(g, m // tm), a_spec=BS((None, tm, k), lambda q, r: (q, r, 0)),
               b_spec=BS((None, tm, n), lambda q, r: (q, r, 0)), o_spec=BS((None, k, n), lambda q, r: (q, 0, 0)),
               out_shape=(g, k, n), dn=TN)


def _rstd(x):
    return lax.rsqrt(jnp.mean(x * x, axis=-1, keepdims=True) + EPS)


def _norm_fwd(x, g, after=None):
    rows = x.shape[0]
    tm = _tile(rows, 512)

    def kern(x_ref, g_ref, *rest):
        xv = x_ref[...]
        rest[-1][...] = ((xv * _rstd(xv)) * g_ref[...]).astype(BF16)

    extra = () if after is None else (after,)
    return _call(kern, name="norm_fwd", grid=(rows // tm,),
                          in_specs=[BS((tm, DM), lambda i: (i, 0)), BS((1, DM), lambda i: (0, 0))] + [ANY] * len(extra),
                          out_specs=BS((tm, DM), lambda i: (i, 0)),
                          out_shape=jax.ShapeDtypeStruct((rows, DM), BF16), compiler_params=_params(1))(x, g, *extra)


def _norm_res(x, y, g):
    rows = x.shape[0]
    tm = _tile(rows, 512)

    def kern(x_ref, y_ref, g_ref, o_ref):
        yv = y_ref[...]
        o_ref[...] = x_ref[...] + (yv * _rstd(yv)) * g_ref[...]

    row = BS((tm, DM), lambda i: (i, 0))
    return _call(kern, name="norm_res", grid=(rows // tm,),
                          in_specs=[row, row, BS((1, DM), lambda i: (0, 0))], out_specs=row,
                          out_shape=jax.ShapeDtypeStruct((rows, DM), F32), compiler_params=_params(1))(x, y, g)


def _norm_bwd(z, dout, g, resid, out_dtype, after=None):
    rows = z.shape[0]
    tm = _tile(rows, 512)
    has_res = resid is not None

    def kern(*refs):
        z_ref, d_ref, g_ref = refs[:3]
        r_ref = refs[3] if has_res else None
        dz_ref, dg_ref = refs[-2:]
        zv = z_ref[...]
        dv = d_ref[...].astype(F32)
        r = _rstd(zv)
        zh = zv * r
        dzh = dv * g_ref[...]
        dz = r * (dzh - zh * jnp.mean(dzh * zh, axis=-1, keepdims=True))
        if has_res:
            dz = dz + r_ref[...]
        dz_ref[...] = dz.astype(dz_ref.dtype)
        part = jnp.sum(dv * zh, axis=0, keepdims=True)

        @pl.when(pl.program_id(0) == 0)
        def _():
            dg_ref[...] = part

        @pl.when(pl.program_id(0) > 0)
        def _():
            dg_ref[...] += part

    row = BS((tm, DM), lambda i: (i, 0))
    vec = BS((1, DM), lambda i: (0, 0))
    ins = [row, row, vec] + ([row] if has_res else []) + ([ANY] if after is not None else [])
    args = (z, dout, g) + ((resid,) if has_res else ()) + ((after,) if after is not None else ())
    return _call(kern, name="norm_bwd_res" if has_res else "norm_bwd", grid=(rows // tm,), in_specs=ins,
                          out_specs=[row, vec],
                          out_shape=[jax.ShapeDtypeStruct((rows, DM), out_dtype), jax.ShapeDtypeStruct((1, DM), F32)],
                          compiler_params=_params(1))(*args)


def _ffn_up(h, wgu4):
    s = h.shape[0]
    tm = _tile(s, 512)

    def kern(h_ref, w_ref, gu_ref, a_ref):
        hv = h_ref[...]
        gate = jnp.dot(hv, w_ref[0], preferred_element_type=F32)
        up = jnp.dot(hv, w_ref[1], preferred_element_type=F32)
        gu_ref[0] = gate
        gu_ref[1] = up
        a_ref[...] = (gate * jax.nn.sigmoid(gate) * up).astype(BF16)

    return _call(
        kern, name="ffn_up", grid=(4, s // tm),
        in_specs=[BS((tm, DM), lambda j, i: (i, 0)), BS((2, None, DM, FFB), lambda j, i: (0, j, 0, 0))],
        out_specs=[BS((2, None, tm, FFB), lambda j, i: (0, j, i, 0)), BS((None, tm, FFB), lambda j, i: (j, i, 0))],
        out_shape=[jax.ShapeDtypeStruct((2, 4, s, FFB), F32), jax.ShapeDtypeStruct((4, s, FFB), BF16)],
        compiler_params=_params(2))(h, wgu4)


def _ffn_da(dy, wd4, gu):
    s = dy.shape[0]
    tm = _tile(s, 512)

    def kern(dy_ref, w_ref, gu_ref, o_ref):
        da = lax.dot_general(dy_ref[...], w_ref[...], (NT, ((), ())), preferred_element_type=F32)
        gate = gu_ref[0]
        up = gu_ref[1]
        sg = jax.nn.sigmoid(gate)
        o_ref[0] = (da * up * (sg * (1.0 + gate * (1.0 - sg)))).astype(BF16)
        o_ref[1] = (da * (gate * sg)).astype(BF16)

    blk = BS((2, None, tm, FFB), lambda j, i: (0, j, i, 0))
    return _call(
        kern, name="ffn_da", grid=(4, s // tm),
        in_specs=[BS((tm, DM), lambda j, i: (i, 0)), BS((None, FFB, DM), lambda j, i: (j, 0, 0)), blk],
        out_specs=blk, out_shape=jax.ShapeDtypeStruct((2, 4, s, FFB), BF16), compiler_params=_params(2))(dy, wd4, gu)


def _ffn_fwd(x, gpre, gpost, wgu, wd):
    h = _norm_fwd(x, gpre)
    gu, a = _ffn_up(h, wgu.reshape(2, 4, DM, FFB))
    y = _bmm_nn_sum("ffn_down", a, wd.reshape(4, FFB, DM))
    return _norm_res(x, y, gpost), (x, h, gu, a, y)


def _ffn_bwd(dxo, saved, gpre, gpost, wgu, wd, after=None):
    x, h, gu, a, y = saved
    s = x.shape[0]
    dy, dgpost = _norm_bwd(y, dxo, gpost, None, BF16, after)
    dgu = _ffn_da(dy, wd.reshape(4, FFB, DM), gu).reshape(8, s, FFB)
    dwd = _bmm_tn_a3("ffn_dwd", a, dy)
    dwgu = _bmm_tn("ffn_dwgu", h, dgu)
    dh = _bmm_nt_sum("ffn_dh", dgu, wgu)
    dx, dgpre = _norm_bwd(x, dh, gpre, dxo, F32)
    return dx, dgpre, dgpost, dwgu, dwd.reshape(D_FF, DM)


def _bmm_tn_a3(name, a, b):
    g, m, k = a.shape
    n = b.shape[1]
    tm = _tile(m, TK_RED)
    return _mm(name, a, b, grid=(g, m // tm), a_spec=BS((None, tm, k), lambda q, r: (q, r, 0)),
               b_spec=BS((tm, n), lambda q, r: (r, 0)), o_spec=BS((None, k, n), lambda q, r: (q, 0, 0)),
               out_shape=(g, k, n), dn=TN)


def _softmax_rows(s):
    m = jnp.max(s, axis=-1, keepdims=True)
    p = jnp.exp(s - m)
    return p / jnp.sum(p, axis=-1, keepdims=True)


def _xattn_fwd_call(h, wq, kv):
    s = h.shape[0]
    mlen = kv.shape[1]
    tm = _tile(s, 512)
    scale = MEM_HD ** -0.5

    def kern(h_ref, w_ref, k_ref, v_ref, q_ref, o_ref):
        q = jnp.dot(h_ref[...], w_ref[...], preferred_element_type=F32).astype(BF16)
        q_ref[...] = q
        sc = lax.dot_general(q, k_ref[...], (NT, ((), ())), preferred_element_type=F32) * scale
        p = _softmax_rows(sc)
        o_ref[...] = jnp.dot(p.astype(BF16), v_ref[...], preferred_element_type=F32).astype(BF16)

    blk = BS((tm, MEM_HD), lambda i, hd: (i, hd))
    return _call(
        kern, name="xattn_fwd", grid=(s // tm, MEM_H),
        in_specs=[BS((tm, DM), lambda i, hd: (i, 0)), BS((DM, MEM_HD), lambda i, hd: (0, hd)),
                  BS((None, mlen, MEM_HD), lambda i, hd: (hd, 0, 0)),
                  BS((None, mlen, MEM_HD), lambda i, hd: (MEM_H + hd, 0, 0))],
        out_specs=[blk, blk],
        out_shape=[jax.ShapeDtypeStruct((s, DM), BF16), jax.ShapeDtypeStruct((s, DM), BF16)],
        compiler_params=_params(2))(h, wq, kv, kv)


def _xattn_bwd_call(q, kv, do):
    s = q.shape[0]
    mlen = kv.shape[1]
    tm = _tile(s, 512)
    scale = MEM_HD ** -0.5

    def kern(q_ref, k_ref, v_ref, do_ref, dq_ref, dkv_ref):
        qv, kvv, vv, dov = q_ref[...], k_ref[...], v_ref[...], do_ref[...]
        sc = lax.dot_general(qv, kvv, (NT, ((), ())), preferred_element_type=F32) * scale
        p = _softmax_rows(sc)
        dp = lax.dot_general(dov, vv, (NT, ((), ())), preferred_element_type=F32)
        ds = (p * (dp - jnp.sum(dp * p, axis=-1, keepdims=True)) * scale).astype(BF16)
        dq_ref[...] = jnp.dot(ds, kvv, preferred_element_type=F32).astype(BF16)
        dk = lax.dot_general(ds, qv, (TN, ((), ())), preferred_element_type=F32)
        dv = lax.dot_general(p.astype(BF16), dov, (TN, ((), ())), preferred_element_type=F32)

        @pl.when(pl.program_id(1) == 0)
        def _():
            dkv_ref[0] = dk
            dkv_ref[1] = dv

        @pl.when(pl.program_id(1) > 0)
        def _():
            dkv_ref[0] += dk
            dkv_ref[1] += dv

    blk = BS((tm, MEM_HD), lambda hd, i: (i, hd))
    return _call(
        kern, name="xattn_bwd", grid=(MEM_H, s // tm),
        in_specs=[blk, BS((None, mlen, MEM_HD), lambda hd, i: (hd, 0, 0)),
                  BS((None, mlen, MEM_HD), lambda hd, i: (MEM_H + hd, 0, 0)), blk],
        out_specs=[blk, BS((2, None, mlen, MEM_HD), lambda hd, i: (0, hd, 0, 0))],
        out_shape=[jax.ShapeDtypeStruct((s, DM), BF16), jax.ShapeDtypeStruct((2, MEM_H, mlen, MEM_HD), F32)],
        compiler_params=_params(2))(q, kv, kv, do)


def _cross_fwd(x, mem, gpre, gmem, gpost, wq, wkv, wo, after=None):
    h = _norm_fwd(x, gpre, after)
    mn = _norm_fwd(mem, gmem)
    kv = _bmm_nn("xattn_kv", mn, wkv, BF16)
    q, o = _xattn_fwd_call(h, wq, kv)
    y = _mm_nn("xattn_out", o, wo)
    return _norm_res(x, y, gpost), (x, h, mn, kv, q, o, y)


def _cross_bwd(dxo, saved, mem, gpre, gmem, gpost, wq, wkv, wo, after=None):
    x, h, mn, kv, q, o, y = saved
    mlen = mem.shape[0]
    dy, dgpost = _norm_bwd(y, dxo, gpost, None, BF16, after)
    do = _mm_nt("xattn_do", dy, wo, BF16)
    dwo = _mm_tn("xattn_dwo", o, dy)
    dq, dkv = _xattn_bwd_call(q, kv, do)
    dwq = _mm_tn("xattn_dwq", h, dq)
    dh = _mm_nt("xattn_dh", dq, wq)
    dkv8 = dkv.reshape(8, mlen, MEM_HD)
    dwkv = _bmm_tn("xattn_dwkv", mn, dkv8)
    dmn = _bmm_nt_sum("xattn_dmn", dkv8, wkv)
    _, dgmem = _norm_bwd(mem, dmn, gmem, None, BF16)
    dx, dgpre = _norm_bwd(x, dh, gpre, dxo, F32)
    return dx, dgpre, dgmem, dgpost, dwq, dwkv, dwo


def _log_sigmoid(z):
    return jnp.minimum(z, 0.0) - jnp.log1p(jnp.exp(-jnp.abs(z)))


def _lane_scan_steps():
    return (1, 2, 4, 8, 16, 32, 64)


def _fox_cum(frow, bfb):
    s = frow.shape[1]

    def kern(f_ref, b_ref, o_ref):
        lane = lax.broadcasted_iota(jnp.int32, (FOX_H, LANE), 1)
        carry = jnp.zeros((FOX_H, 1), F32)
        for c in range(s // LANE):
            sl = slice(c * LANE, (c + 1) * LANE)
            lf = _log_sigmoid(f_ref[:, sl] + b_ref[...])
            v = lf
            for d in _lane_scan_steps():
                v = v + jnp.where(lane >= d, pltpu.roll(v, d, 1), 0.0)
            o_ref[:, sl] = v + carry
            carry = carry + jnp.sum(lf, axis=1, keepdims=True)

    return _call(kern, name="fox_cum", out_shape=jax.ShapeDtypeStruct((FOX_H, s), F32),
                          compiler_params=pltpu.CompilerParams(vmem_limit_bytes=VMEM_LIMIT))(frow, bfb)


def _fox_dlogf(dcq, dck, frow, bfb):
    s = frow.shape[1]

    def kern(q_ref, d_ref, f_ref, b_ref, df_ref, db_ref):
        lane = lax.broadcasted_iota(jnp.int32, (FOX_H, LANE), 1)
        carry = jnp.zeros((FOX_H, 1), F32)
        dbf = jnp.zeros((FOX_H, 1), F32)
        for c in reversed(range(s // LANE)):
            sl = slice(c * LANE, (c + 1) * LANE)
            dc = q_ref[:, sl] - d_ref[:, sl]
            v = dc
            for d in _lane_scan_steps():
                v = v + jnp.where(lane < LANE - d, pltpu.roll(v, LANE - d, 1), 0.0)
            v = v + carry
            carry = carry + jnp.sum(dc, axis=1, keepdims=True)
            df = v * jax.nn.sigmoid(-(f_ref[:, sl] + b_ref[...]))
            df_ref[:, sl] = df
            dbf = dbf + jnp.sum(df, axis=1, keepdims=True)
        db_ref[...] = jnp.broadcast_to(dbf, (FOX_H, LANE))

    return _call(kern, name="fox_dlogf",
                          out_shape=[jax.ShapeDtypeStruct((FOX_H, s), F32), jax.ShapeDtypeStruct((FOX_H, LANE), F32)],
                          compiler_params=pltpu.CompilerParams(vmem_limit_bytes=VMEM_LIMIT))(dcq, dck, frow, bfb)


FOX_TQ = 512
Q_COL, K_COL, V_COL = 0, FOX_W // LANE, 2 * FOX_W // LANE
B_COL, C_COL, U_COL = 12, 16, 20


def _fox_logits(qm, kb, cc, cr, causal, scale, reps):
    sc = lax.dot_general(qm, kb, (NT, ((), ())), preferred_element_type=F32) * scale
    sc = sc + jnp.tile(cc, (1, reps)) - cr
    return jnp.where(causal, sc, NEG)


def _fox_fwd_call(proj, cumc, cumr):
    s = proj.shape[0]
    tq = _tile(s, FOX_TQ)
    nq = s // tq
    reps = tq // LANE
    scale = FOX_HD ** -0.5

    def kern(q_ref, k_ref, v_ref, cc_ref, cr_ref, o_ref, lse_ref, m_s, l_s, acc_s):
        i = pl.program_id(1)
        j = pl.program_id(2)
        lane = lax.broadcasted_iota(jnp.int32, (tq, LANE), 1)

        @pl.when(j == 0)
        def _():
            m_s[...] = jnp.full(m_s.shape, NEG, F32)
            l_s[...] = jnp.zeros(l_s.shape, F32)
            acc_s[...] = jnp.zeros(acc_s.shape, F32)

        @pl.when(j <= i)
        def _():
            qv = q_ref[...]
            kb = k_ref[...].astype(BF16)
            vb = v_ref[...].astype(BF16)
            causal = (i * tq + lax.broadcasted_iota(jnp.int32, (tq, tq), 0)
                      >= j * tq + lax.broadcasted_iota(jnp.int32, (tq, tq), 1))
            for hh in range(2):
                sel = (lane < FOX_HD) if hh == 0 else (lane >= FOX_HD)
                qm = jnp.where(sel, qv, 0.0).astype(BF16)
                sc = _fox_logits(qm, kb, cc_ref[hh], cr_ref[hh:hh + 1, :], causal, scale, reps)
                m_prev = m_s[hh]
                m_new = jnp.maximum(m_prev, jnp.max(sc, axis=-1, keepdims=True))
                alpha = jnp.exp(m_prev - m_new)
                p = jnp.exp(sc - m_new)
                l_s[hh] = alpha * l_s[hh] + jnp.sum(p, axis=-1, keepdims=True)
                acc_s[hh] = alpha * acc_s[hh] + jnp.dot(p.astype(BF16), vb, preferred_element_type=F32)
                m_s[hh] = m_new

        @pl.when(j == i)
        def _():
            o_ref[...] = jnp.where(lane < FOX_HD, acc_s[0] / l_s[0], acc_s[1] / l_s[1])
            for hh in range(2):
                lse_ref[hh] = jnp.broadcast_to(m_s[hh] + jnp.log(l_s[hh]), (tq, LANE))

    kvi = lambda hp, i, j: jnp.minimum(j, i)
    return _call(
        kern, name="fox_fwd", grid=(4, nq, nq),
        in_specs=[BS((tq, LANE), lambda hp, i, j: (i, Q_COL + hp)),
                  BS((tq, LANE), lambda hp, i, j: (kvi(hp, i, j), K_COL + hp)),
                  BS((tq, LANE), lambda hp, i, j: (kvi(hp, i, j), V_COL + hp)),
                  BS((2, tq, LANE), lambda hp, i, j: (hp, i, 0)),
                  BS((None, 2, tq), lambda hp, i, j: (hp, 0, kvi(hp, i, j)))],
        out_specs=[BS((tq, LANE), lambda hp, i, j: (i, hp)), BS((2, tq, LANE), lambda hp, i, j: (hp, i, 0))],
        out_shape=[jax.ShapeDtypeStruct((s, FOX_W), F32), jax.ShapeDtypeStruct((FOX_H, s, LANE), F32)],
        scratch_shapes=[pltpu.VMEM((2, tq, 1), F32), pltpu.VMEM((2, tq, 1), F32), pltpu.VMEM((2, tq, LANE), F32)],
        compiler_params=_params(3))(proj, proj, proj, cumc, cumr)


ROWSUM_M = 16


def _fox_bwd_call(proj, o, lse, dcat, cumc, cumr):
    s = proj.shape[0]
    tq = _tile(s, FOX_TQ)
    nq = s // tq
    reps = tq // LANE
    scale = FOX_HD ** -0.5

    def kern(q_ref, k_ref, v_ref, do_ref, o_ref, lse_ref, cc_ref, cr_ref, dq_ref, dk_ref, dv_ref, dck_ref, dcq_ref):
        j = pl.program_id(1)
        i = pl.program_id(2)
        lane = lax.broadcasted_iota(jnp.int32, (tq, LANE), 1)
        ones = jnp.ones((ROWSUM_M, tq), BF16)

        @pl.when((j == 0) & (i == 0))
        def _():
            dq_ref[...] = jnp.zeros(dq_ref.shape, F32)
            dcq_ref[...] = jnp.zeros(dcq_ref.shape, F32)

        @pl.when(i == j)
        def _():
            dk_ref[...] = jnp.zeros(dk_ref.shape, F32)
            dv_ref[...] = jnp.zeros(dv_ref.shape, F32)
            dck_ref[...] = jnp.zeros(dck_ref.shape, F32)

        @pl.when(i >= j)
        def _():
            qv = q_ref[...]
            dov = do_ref[...]
            ov = o_ref[...]
            kb = k_ref[...].astype(BF16)
            vb = v_ref[...].astype(BF16)
            causal = (i * tq + lax.broadcasted_iota(jnp.int32, (tq, tq), 0)
                      >= j * tq + lax.broadcasted_iota(jnp.int32, (tq, tq), 1))
            dq_t = jnp.zeros((tq, LANE), F32)
            dk_t = jnp.zeros((tq, LANE), F32)
            dv_t = jnp.zeros((tq, LANE), F32)
            for hh in range(2):
                sel = (lane < FOX_HD) if hh == 0 else (lane >= FOX_HD)
                qm = jnp.where(sel, qv, 0.0).astype(BF16)
                dom32 = jnp.where(sel, dov, 0.0)
                dom = dom32.astype(BF16)
                sc = _fox_logits(qm, kb, cc_ref[hh], cr_ref[hh:hh + 1, :], causal, scale, reps)
                p = jnp.exp(sc - jnp.tile(lse_ref[hh], (1, reps)))
                dp = lax.dot_general(dom, vb, (NT, ((), ())), preferred_element_type=F32)
                delta = jnp.sum(dom32 * ov, axis=-1, keepdims=True)
                ds = p * (dp - delta)
                dsb = ds.astype(BF16)
                dq_t = jnp.where(sel, jnp.dot(dsb, kb, preferred_element_type=F32) * scale, dq_t)
                dk_t = dk_t + lax.dot_general(dsb, qm, (TN, ((), ())), preferred_element_type=F32) * scale
                dv_t = dv_t + lax.dot_general(p.astype(BF16), dom, (TN, ((), ())), preferred_element_type=F32)
                dck_ref[hh] += jnp.sum(ds, axis=0, keepdims=True)
                ds_lo = (ds - dsb.astype(F32)).astype(BF16)
                dcq_ref[hh, i] += (lax.dot_general(ones, dsb, (NT, ((), ())), preferred_element_type=F32)
                                   + lax.dot_general(ones, ds_lo, (NT, ((), ())), preferred_element_type=F32))
            rows =pl.ds(pl.multiple_of(i * tq, tq), tq)
            dq_ref[rows, :] += dq_t
            dk_ref[...] += dk_t
            dv_ref[...] += dv_t

    qi = lambda hp, j, i: jnp.maximum(i, j)
    return _call(
        kern, name="fox_bwd", grid=(4, nq, nq),
        in_specs=[BS((tq, LANE), lambda hp, j, i: (qi(hp, j, i), Q_COL + hp)),
                  BS((tq, LANE), lambda hp, j, i: (j, K_COL + hp)),
                  BS((tq, LANE), lambda hp, j, i: (j, V_COL + hp)),
                  BS((tq, LANE), lambda hp, j, i: (qi(hp, j, i), hp)),
                  BS((tq, LANE), lambda hp, j, i: (qi(hp, j, i), hp)),
                  BS((2, tq, LANE), lambda hp, j, i: (hp, qi(hp, j, i), 0)),
                  BS((2, tq, LANE), lambda hp, j, i: (hp, qi(hp, j, i), 0)),
                  BS((None, 2, tq), lambda hp, j, i: (hp, 0, j))],
        out_specs=[BS((s, LANE), lambda hp, j, i: (0, hp)), BS((tq, LANE), lambda hp, j, i: (j, hp)),
                   BS((tq, LANE), lambda hp, j, i: (j, hp)), BS((2, 1, tq), lambda hp, j, i: (hp, 0, j)),
                   BS((2, nq, ROWSUM_M, tq), lambda hp, j, i: (hp, 0, 0, 0))],
        out_shape=[jax.ShapeDtypeStruct((s, FOX_W), F32), jax.ShapeDtypeStruct((s, FOX_W), F32),
                   jax.ShapeDtypeStruct((s, FOX_W), F32), jax.ShapeDtypeStruct((FOX_H, 1, s), F32),
                   jax.ShapeDtypeStruct((FOX_H, nq, ROWSUM_M, tq), F32)],
        compiler_params=_params(3))(proj, proj, proj, dcat, o, lse, cumc, cumr)


def _shift_down(v, d, row):
    return jnp.where(row >= d, pltpu.roll(v, d, 0), 0.0)


def _shift_up(v, d, row, n):
    return jnp.where(row < n - d, pltpu.roll(v, n - d, 0), 0.0)


def _sconv_fwd(proj, convw):
    s = proj.shape[0]

    def kern(b_ref, c_ref, u_ref, w_ref, y_ref):
        row = lax.broadcasted_iota(jnp.int32, (s, LANE), 0)
        z = c_ref[...] * u_ref[...]
        conv = w_ref[2:3, :] * z + w_ref[1:2, :] * _shift_down(z, 1, row) + w_ref[0:1, :] * _shift_down(z, 2, row)
        y_ref[...] = (b_ref[...] * conv).astype(BF16)

    col = lambda base: BS((s, LANE), lambda cb: (0, base + cb))
    return _call(kern, name="sconv_fwd", grid=(SC_W // LANE,),
                          in_specs=[col(B_COL), col(C_COL), col(U_COL), BS((SC_K, LANE), lambda cb: (0, cb))],
                          out_specs=BS((s, LANE), lambda cb: (0, cb)),
                          out_shape=jax.ShapeDtypeStruct((s, SC_W), BF16), compiler_params=_params(1))(proj, proj, proj, convw)


def _sconv_bwd(proj, convw, dcat):
    s = proj.shape[0]

    def kern(b_ref, c_ref, u_ref, w_ref, dy_ref, db_ref, dc_ref, du_ref, dw_ref):
        row = lax.broadcasted_iota(jnp.int32, (s, LANE), 0)
        cv, uv, dyv = c_ref[...], u_ref[...], dy_ref[...]
        z = cv * uv
        z1 = _shift_down(z, 1, row)
        z2 = _shift_down(z, 2, row)
        conv = w_ref[2:3, :] * z + w_ref[1:2, :] * z1 + w_ref[0:1, :] * z2
        db_ref[...] = dyv * conv
        dcv = dyv * b_ref[...]
        dz = w_ref[2:3, :] * dcv + w_ref[1:2, :] * _shift_up(dcv, 1, row, s) + w_ref[0:1, :] * _shift_up(dcv, 2, row, s)
        dc_ref[...] = dz * uv
        du_ref[...] = dz * cv
        dw_ref[0:1, :] = jnp.sum(dcv * z2, axis=0, keepdims=True)
        dw_ref[1:2, :] = jnp.sum(dcv * z1, axis=0, keepdims=True)
        dw_ref[2:3, :] = jnp.sum(dcv * z, axis=0, keepdims=True)

    col = lambda base: BS((s, LANE), lambda cb: (0, base + cb))
    out = BS((s, LANE), lambda cb: (0, cb))
    wspec = BS((SC_K, LANE), lambda cb: (0, cb))
    act = jax.ShapeDtypeStruct((s, SC_W), F32)
    return _call(kern, name="sconv_bwd", grid=(SC_W // LANE,),
                          in_specs=[col(B_COL), col(C_COL), col(U_COL), wspec, col(FOX_W // LANE)],
                          out_specs=[out, out, out, wspec],
                          out_shape=[act, act, act, jax.ShapeDtypeStruct((SC_K, SC_W), F32)],
                          compiler_params=_params(1))(proj, proj, proj, convw, dcat)


def _fox_layer_fwd(x, gpre, gpost, wall, bfb, convw, wout, after=None):
    s = x.shape[0]
    h = _norm_fwd(x, gpre, after)
    proj = _mm_nn("fox_proj", h, wall, tn=AB_PAD // 5)
    frow = proj[:, 3 * FOX_W + 3 * SC_W:3 * FOX_W + 3 * SC_W + FOX_H].T
    cumr = _fox_cum(frow, bfb)
    cumc = jnp.broadcast_to(cumr[:, :, None], (FOX_H, s, LANE))
    cumr4 = cumr.reshape(4, 2, s)
    o, lse = _fox_fwd_call(proj, cumc, cumr4)
    yb = _sconv_fwd(proj, convw)
    cat = jnp.concatenate([o.astype(BF16), yb], axis=1)
    y = _mm_nn("fox_out", cat, wout)
    return _norm_res(x, y, gpost), (x, h, proj, frow, cumc, cumr4, o, lse, cat, y)


def _fox_layer_bwd(dxo, saved, gpre, gpost, wall, bfb, convw, wout, after=None):
    x, h, proj, frow, cumc, cumr4, o, lse, cat, y = saved
    s = x.shape[0]
    dy, dgpost = _norm_bwd(y, dxo, gpost, None, BF16, after)
    dcat = _mm_nt("fox_dcat", dy, wout)
    dwout = _mm_tn("fox_dwout", cat, dy)
    db, dc, du, dconvw = _sconv_bwd(proj, convw, dcat)
    dq, dk, dv, dck, dcq = _fox_bwd_call(proj, o, lse, dcat, cumc, cumr4)
    dfrow, dbf = _fox_dlogf(dcq[:, :, 0, :].reshape(FOX_H, s), dck.reshape(FOX_H, s), frow, bfb)
    dfcol = jnp.pad(dfrow.T, ((0, 0), (0, LANE - FOX_H)))
    dproj = jnp.concatenate([dq, dk, dv, db, dc, du, dfcol], axis=1).astype(BF16)
    dwall = _mm_tn("fox_dwall", h, dproj, tn=AB_PAD // 5)
    dh = _mm_nt("fox_dh", dproj, wall, tn=AB_PAD // 5)
    dx, dgpre = _norm_bwd(x, dh, gpre, dxo, F32)
    return dx, dgpre, dgpost, dwall, dbf[:, 0], dconvw, dwout


def _ab_pack(w):
    nf = 3 * FOX_W
    return jnp.concatenate([w[:, :nf], w[:, nf + FOX_H:], w[:, nf:nf + FOX_H],
                            jnp.zeros((w.shape[0], AB_PAD - AB_IN), w.dtype)], axis=1)


def _ab_unpack(w):
    nf = 3 * FOX_W
    nbcu = 3 * SC_W
    return jnp.concatenate([w[:, :nf], w[:, nf + nbcu:nf + nbcu + FOX_H], w[:, nf:nf + nbcu]], axis=1)


NCH = DM // LANE
CH_PER_BLK = LRU_BW // LANE


def _chunk_spec(s, lead=0):
    return BS((None, s, LANE), lambda ch: (lead + ch // CH_PER_BLK, 0, ch % CH_PER_BLK))


def _vec_chunk(rows):
    return BS((rows, LANE), lambda ch: (0, ch))


def _neg_expm1(x):
    series = -x * (1.0 + x * (1 / 2) * (1.0 + x * (1 / 3) * (1.0 + x * (1 / 4) * (1.0 + x * (1 / 5) * (
        1.0 + x * (1 / 6) * (1.0 + x * (1 / 7)))))))
    return jnp.where(x > -0.25, series, 1.0 - jnp.exp(x))


def _softplus(z):
    return jnp.maximum(z, 0.0) + jnp.log1p(jnp.exp(-jnp.abs(z)))


GELU_C = math.sqrt(2.0 / math.pi)
GELU_A = 0.044715


def _gelu(x):
    return 0.5 * x * (1.0 + jnp.tanh(GELU_C * (x + GELU_A * x * x * x)))


def _gelu_grad(x):
    t = jnp.tanh(GELU_C * (x + GELU_A * x * x * x))
    return 0.5 * (1.0 + t) + 0.5 * x * (1.0 - t * t) * GELU_C * (1.0 + 3.0 * GELU_A * x * x)


def _lru_conv_fwd(gu, convw, convb):
    s = gu.shape[1]

    def kern(x_ref, w_ref, b_ref, u_ref):
        row = lax.broadcasted_iota(jnp.int32, (s, LANE), 0)
        xv = x_ref[...]
        u_ref[...] = (b_ref[...] + w_ref[3:4, :] * xv + w_ref[2:3, :] * _shift_down(xv, 1, row)
                      + w_ref[1:2, :] * _shift_down(xv, 2, row) + w_ref[0:1, :] * _shift_down(xv, 3, row))

    return _call(kern, name="lru_conv_fwd", grid=(NCH,),
                          in_specs=[_chunk_spec(s, LRU_NB), _vec_chunk(RG_K), _vec_chunk(1)], out_specs=_chunk_spec(s),
                          out_shape=jax.ShapeDtypeStruct((LRU_NB, s, LRU_BW), F32), compiler_params=_params(1))(gu, convw, convb)


def _lru_conv_bwd(dud, dug, gu, convw):
    s = gu.shape[1]

    def kern(d1_ref, d2_ref, x_ref, w_ref, dx_ref, dw_ref, db_ref):
        row = lax.broadcasted_iota(jnp.int32, (s, LANE), 0)
        du = d1_ref[...] + d2_ref[...]
        xv = x_ref[...]
        dx_ref[...] = (w_ref[3:4, :] * du + w_ref[2:3, :] * _shift_up(du, 1, row, s) + w_ref[1:2, :] * _shift_up(du, 2, row, s)
                       + w_ref[0:1, :] * _shift_up(du, 3, row, s)).astype(BF16)
        dw_ref[3:4, :] = jnp.sum(du * xv, axis=0, keepdims=True)
        for k in range(1, RG_K):
            dw_ref[3 - k:4 - k, :] = jnp.sum(du * _shift_down(xv, k, row), axis=0, keepdims=True)
        db_ref[...] = jnp.sum(du, axis=0, keepdims=True)

    return _call(kern, name="lru_conv_bwd", grid=(NCH,),
                          in_specs=[_chunk_spec(s), _chunk_spec(s), _chunk_spec(s, LRU_NB), _vec_chunk(RG_K)],
                          out_specs=[_chunk_spec(s), _vec_chunk(RG_K), _vec_chunk(1)],
                          out_shape=[jax.ShapeDtypeStruct((LRU_NB, s, LRU_BW), BF16),
                                     jax.ShapeDtypeStruct((RG_K, DM), F32), jax.ShapeDtypeStruct((1, DM), F32)],
                          compiler_params=_params(1))(dud, dug, gu, convw)


def _lru_gates(z_ref, bai_ref, lam_ref, uv):
    r = jax.nn.sigmoid(z_ref[0] + bai_ref[0:1, :])
    ig = jax.nn.sigmoid(z_ref[1] + bai_ref[1:2, :])
    sp = _softplus(-lam_ref[...])
    la = -RG_C * r * sp
    a = jnp.exp(la)
    sq = jnp.sqrt(_neg_expm1(2.0 * la))
    return r, ig, sp, a, sq


def _scan_steps(n):
    d, out = 1, []
    while d < n:
        out.append(d)
        d *= 2
    return out


def _lru_scan_fwd(z, bai, lam, u, gu):
    s = u.shape[1]
    zspec = BS((2, None, s, LANE), lambda ch: (0, ch // CH_PER_BLK, 0, ch % CH_PER_BLK))

    def kern(z_ref, bai_ref, lam_ref, u_ref, g_ref, hs_ref, y_ref):
        row = lax.broadcasted_iota(jnp.int32, (s, LANE), 0)
        uv = u_ref[...]
        _, ig, _, a, sq = _lru_gates(z_ref, bai_ref, lam_ref, uv)
        b = sq * (ig * uv)
        for d in _scan_steps(s):
            a_sh = jnp.where(row >= d, pltpu.roll(a, d, 0), 1.0)
            b = a * _shift_down(b, d, row) + b
            a = a * a_sh
        hs_ref[...] = b
        y_ref[...] = (_gelu(g_ref[...]) * b).astype(BF16)

    return _call(kern, name="lru_scan_fwd", grid=(NCH,),
                          in_specs=[zspec, _vec_chunk(2), _vec_chunk(1), _chunk_spec(s), _chunk_spec(s)],
                          out_specs=[_chunk_spec(s), BS((s, LANE), lambda ch: (0, ch))],
                          out_shape=[jax.ShapeDtypeStruct((LRU_NB, s, LRU_BW), F32), jax.ShapeDtypeStruct((s, DM), BF16)],
                          compiler_params=_params(1))(z, bai, lam, u, gu)


def _lru_scan_bwd(dyp, z, bai, lam, u, gu, hs):
    s = u.shape[1]
    zspec = BS((2, None, s, LANE), lambda ch: (0, ch // CH_PER_BLK, 0, ch % CH_PER_BLK))

    def kern(dy_ref, z_ref, bai_ref, lam_ref, u_ref, g_ref, hs_ref, dg_ref, dz_ref, du_ref, dbai_ref, dlam_ref):
        row = lax.broadcasted_iota(jnp.int32, (s, LANE), 0)
        uv, gv, hv, dyv = u_ref[...], g_ref[...], hs_ref[...], dy_ref[...]
        r, ig, sp, a, sq = _lru_gates(z_ref, bai_ref, lam_ref, uv)
        dg_ref[...] = (dyv * hv * _gelu_grad(gv)).astype(BF16)
        g = dyv * _gelu(gv)
        an = _shift_up(a, 1, row, s)
        for d in _scan_steps(s):
            an_sh = jnp.where(row < s - d, pltpu.roll(an, s - d, 0), 1.0)
            g = an * _shift_up(g, d, row, s) + g
            an = an * an_sh
        da = g * _shift_down(hv, 1, row)
        dsq = g * (ig * uv)
        di = g * sq * uv
        du_ref[...] = g * sq * ig
        dla = da * a - dsq * (a * a / sq)
        dzr = dla * (-RG_C * sp) * r * (1.0 - r)
        dzi = di * ig * (1.0 - ig)
        dz_ref[0] = dzr.astype(BF16)
        dz_ref[1] = dzi.astype(BF16)
        dbai_ref[0:1, :] = jnp.sum(dzr, axis=0, keepdims=True)
        dbai_ref[1:2, :] = jnp.sum(dzi, axis=0, keepdims=True)
        dlam_ref[...] = jnp.sum(dla * r, axis=0, keepdims=True) * (RG_C * jax.nn.sigmoid(-lam_ref[...]))

    return _call(
        kern, name="lru_scan_bwd", grid=(NCH,),
        in_specs=[BS((s, LANE), lambda ch: (0, ch)), zspec, _vec_chunk(2), _vec_chunk(1), _chunk_spec(s), _chunk_spec(s),
                  _chunk_spec(s)],
        out_specs=[_chunk_spec(s), zspec, _chunk_spec(s), _vec_chunk(2), _vec_chunk(1)],
        out_shape=[jax.ShapeDtypeStruct((LRU_NB, s, LRU_BW), BF16), jax.ShapeDtypeStruct((2, LRU_NB, s, LRU_BW), BF16),
                   jax.ShapeDtypeStruct((LRU_NB, s, LRU_BW), F32), jax.ShapeDtypeStruct((2, DM), F32),
                   jax.ShapeDtypeStruct((1, DM), F32)],
        compiler_params=_params(1))(dyp, z, bai, lam, u, gu, hs)


def _lru_layer_fwd(x, gpre, gpost, win, convw, convb, wai, bai, lam, wout, after=None):
    s = x.shape[0]
    tm = _tile(s, 512)
    h = _norm_fwd(x, gpre, after)
    gu = _bmm_nn("lru_in", h, win)
    u = _lru_conv_fwd(gu, convw, convb)
    z = _mm("lru_gate", u, wai, grid=(2, LRU_NB, s // tm, 1),
            a_spec=BS((None, tm, LRU_BW), lambda k, n, i, r: (n, i, 0)),
            b_spec=BS((None, None, LRU_BW, LRU_BW), lambda k, n, i, r: (k, n, 0, 0)),
            o_spec=BS((None, None, tm, LRU_BW), lambda k, n, i, r: (k, n, i, 0)),
            out_shape=(2, LRU_NB, s, LRU_BW), dn=NN)
    hs, yp = _lru_scan_fwd(z, bai, lam, u, gu)
    y = _mm_nn("lru_out", yp, wout)
    return _norm_res(x, y, gpost), (x, h, gu, u, z, hs, yp, y)


def _lru_layer_bwd(dxo, saved, gpre, gpost, win, convw, convb, wai, bai, lam, wout, after=None):
    x, h, gu, u, z, hs, yp, y = saved
    s = x.shape[0]
    tm = _tile(s, 512)
    dy, dgpost = _norm_bwd(y, dxo, gpost, None, BF16, after)
    dyp = _mm_nt("lru_dyp", dy, wout)
    dwout = _mm_tn("lru_dwout", yp, dy)
    dgate, dz, dud, dbai, dlam = _lru_scan_bwd(dyp, z, bai, lam, u, gu, hs)
    dwai = _mm("lru_dwai", u, dz, grid=(2, LRU_NB, s // tm),
               a_spec=BS((None, tm, LRU_BW), lambda k, n, r: (n, r, 0)),
               b_spec=BS((None, None, tm, LRU_BW), lambda k, n, r: (k, n, r, 0)),
               o_spec=BS((None, None, LRU_BW, LRU_BW), lambda k, n, r: (k, n, 0, 0)),
               out_shape=(2, LRU_NB, LRU_BW, LRU_BW), dn=TN)
    dug = _mm("lru_dug", dz, wai, grid=(LRU_NB, s // tm, 2),
              a_spec=BS((None, None, tm, LRU_BW), lambda n, i, k: (k, n, i, 0)),
              b_spec=BS((None, None, LRU_BW, LRU_BW), lambda n, i, k: (k, n, 0, 0)),
              o_spec=BS((None, tm, LRU_BW), lambda n, i, k: (n, i, 0)),
              out_shape=(LRU_NB, s, LRU_BW), dn=NT)
    duraw, dconvw, dconvb = _lru_conv_bwd(dud, dug, gu, convw)
    dgu = jnp.concatenate([dgate, duraw], axis=0)
    dwin = _bmm_tn("lru_dwin", h, dgu)
    dh = _bmm_nt_sum("lru_dh", dgu, win)
    dx, dgpre = _norm_bwd(x, dh, gpre, dxo, F32)
    return dx, dgpre, dgpost, dwin, dconvw, dconvb, dwai, dbai, dlam, dwout


CHIP_FLIPS = ((1, 0), (0, 1), (1, 1))


def _place():
    return lax.axis_index("x"), lax.axis_index("y"), lax.axis_index("c")


def _flip(v, f):
    return 1 - v if f else v


def _comm_params():
    return pltpu.CompilerParams(vmem_limit_bytes=VMEM_LIMIT)


def _all_gather(shards):
    n = len(shards)

    def body(*refs):
        ins, outs, stage = refs[:n], refs[n:2 * n], refs[2 * n:3 * n]
        send_sems, recv_sems, local_sems = refs[3 * n:]
        x, y, c = _place()
        me, sibling = (x, y, c), (x, y, 1 - c)
        chips = [(_flip(x, fx), _flip(y, fy)) for fx, fy in CHIP_FLIPS]

        def slot(t, p):
            return outs[t].at[:, 4 * p[0] + 2 * p[1] + p[2]]

        def copy(t, k, block, to, src=None):
            return pltpu.make_async_remote_copy(
                src_ref=slot(t, block) if src is None else src, dst_ref=slot(t, block),
                send_sem=send_sems.at[7 * t + k], recv_sem=recv_sems.at[7 * t + k], device_id=to, device_id_type=MESH)

        first = []
        for t in range(n):
            first.append(copy(t, 0, me, sibling, src=ins[t]))
            first += [copy(t, 1 + j, me, (*chip, c), src=ins[t]) for j, chip in enumerate(chips)]
        for cp in first:
            cp.start()
        load = [pltpu.make_async_copy(ins[t], stage[t], local_sems.at[t]) for t in range(n)]
        mine = [pltpu.make_async_copy(stage[t], slot(t, me), local_sems.at[t]) for t in range(n)]
        for cp in load:
            cp.start()
        for t in range(n):
            load[t].wait()
            mine[t].start()
        passed = []
        for j, chip in enumerate(chips):
            for t in range(n):
                copy(t, 1 + j, (*chip, c), me).wait_recv()
                fwd = copy(t, 4 + j, (*chip, c), sibling)
                fwd.start()
                passed.append(fwd)
        for t in range(n):
            copy(t, 0, sibling, me).wait_recv()
            for j, chip in enumerate(chips):
                copy(t, 4 + j, (*chip, 1 - c), me).wait_recv()
        for cp in first + passed:
            cp.wait_send()
        for cp in mine:
            cp.wait()

    outs = [jax.ShapeDtypeStruct((s.shape[0], NDEV) + s.shape[1:], s.dtype) for s in shards]
    return pl.pallas_call(body, name="all_gather", in_specs=[ANY] * n, out_specs=[ANY] * n, out_shape=outs,
                          scratch_shapes=[pltpu.VMEM(s.shape, s.dtype) for s in shards]
                          + [pltpu.SemaphoreType.DMA((7 * n,)), pltpu.SemaphoreType.DMA((7 * n,)),
                             pltpu.SemaphoreType.DMA((n,))],
                          compiler_params=_comm_params())(*shards)


def _small_gather(v):
    def body(v_ref, o_ref, send_sems, recv_sems, local_sem):
        x, y, c = _place()
        mine = 4 * x + 2 * y + c
        local = pltpu.make_async_copy(v_ref, o_ref.at[mine], local_sem)
        local.start()
        sends = []
        for k in range(1, NDEV):
            fx, fy, fc = (k >> 2) & 1, (k >> 1) & 1, k & 1
            sends.append(pltpu.make_async_remote_copy(
                src_ref=v_ref, dst_ref=o_ref.at[mine], send_sem=send_sems.at[k - 1], recv_sem=recv_sems.at[k - 1],
                device_id=(_flip(x, fx), _flip(y, fy), _flip(c, fc)), device_id_type=MESH))
        for cp in sends:
            cp.start()
        for k in range(1, NDEV):
            fx, fy, fc = (k >> 2) & 1, (k >> 1) & 1, k & 1
            src = 4 * _flip(x, fx) + 2 * _flip(y, fy) + _flip(c, fc)
            pltpu.make_async_remote_copy(src_ref=v_ref, dst_ref=o_ref.at[src], send_sem=send_sems.at[k - 1],
                                         recv_sem=recv_sems.at[k - 1], device_id=(x, y, c), device_id_type=MESH).wait_recv()
        for cp in sends:
            cp.wait_send()
        local.wait()

    return pl.pallas_call(body, name="small_gather", in_specs=[ANY], out_specs=ANY,
                          out_shape=jax.ShapeDtypeStruct((NDEV,) + v.shape, v.dtype),
                          scratch_shapes=[pltpu.SemaphoreType.DMA((NDEV - 1,)), pltpu.SemaphoreType.DMA((NDEV - 1,)),
                                          pltpu.SemaphoreType.DMA],
                          compiler_params=_comm_params())(v)


REL_CHIPS = ((0, 0),) + CHIP_FLIPS


def _rs_d2d(g5s, after=None):
    n = len(g5s)
    extra = () if after is None else (after,)

    def body(*refs):
        ins, gots = refs[:n], refs[n + len(extra):2 * n + len(extra)]
        send_sems, recv_sems = refs[2 * n + len(extra):]
        x, y, c = _place()
        copies = []
        for t in range(n):
            for f, (fx, fy) in enumerate(REL_CHIPS):
                copies.append(pltpu.make_async_remote_copy(
                    src_ref=ins[t].at[_flip(x, fx), _flip(y, fy), 1 - c], dst_ref=gots[t].at[f],
                    send_sem=send_sems.at[4 * t + f], recv_sem=recv_sems.at[4 * t + f], device_id=(x, y, 1 - c),
                    device_id_type=MESH))
        for cp in copies:
            cp.start()
        for cp in copies:
            cp.wait()

    out = [jax.ShapeDtypeStruct((4,) + g.shape[3:], F32) for g in g5s]
    return pl.pallas_call(body, name="rs_d2d", in_specs=[ANY] * (n + len(extra)), out_specs=[ANY] * n, out_shape=out,
                          scratch_shapes=[pltpu.SemaphoreType.DMA((4 * n,)), pltpu.SemaphoreType.DMA((4 * n,))],
                          compiler_params=_comm_params())(*g5s, *extra)


def _rs_ici(parts):
    n = len(parts)

    def body(*refs):
        ins, outs = refs[:n], refs[n:2 * n]
        send_sems, recv_sems = refs[2 * n:]
        x, y, c = _place()
        copies = []
        for t in range(n):
            for f, (fx, fy) in enumerate(CHIP_FLIPS):
                copies.append(pltpu.make_async_remote_copy(
                    src_ref=ins[t].at[f], dst_ref=outs[t].at[f], send_sem=send_sems.at[3 * t + f],
                    recv_sem=recv_sems.at[3 * t + f], device_id=(_flip(x, fx), _flip(y, fy), c), device_id_type=MESH))
        for cp in copies:
            cp.start()
        for cp in copies:
            cp.wait()

    out = [jax.ShapeDtypeStruct(p.shape, p.dtype) for p in parts]
    return pl.pallas_call(body, name="rs_ici", in_specs=[ANY] * n, out_specs=[ANY] * n, out_shape=out,
                          scratch_shapes=[pltpu.SemaphoreType.DMA((3 * n,)), pltpu.SemaphoreType.DMA((3 * n,))],
                          compiler_params=_comm_params())(*parts)


HBM = pl.BlockSpec(memory_space=pltpu.HBM)
SEM = pl.BlockSpec(memory_space=pltpu.SEMAPHORE)
EFFECT = pltpu.SideEffectType.DATAFLOW_SIDE_EFFECTING


def _in_hbm(a):
    return pltpu.with_memory_space_constraint(a, pltpu.HBM)


def _rs_ici_copies(ins, lands, send_sems, recv_sems):
    x, y, c = _place()
    return [pltpu.make_async_remote_copy(
        src_ref=ins[t].at[f], dst_ref=lands[t].at[f], send_sem=send_sems.at[3 * t + f], recv_sem=recv_sems.at[3 * t + f],
        device_id=(_flip(x, fx), _flip(y, fy), c), device_id_type=MESH)
        for t in range(len(ins)) for f, (fx, fy) in enumerate(CHIP_FLIPS)]


def _rs_ici_start(parts, name):
    n = len(parts)

    def body(*refs):
        ins, lands = refs[:n], refs[n:2 * n]
        send_sems, recv_sems = refs[2 * n], refs[2 * n + 1]
        token = refs[-1]
        for cp in _rs_ici_copies(ins, lands, send_sems, recv_sems):
            cp.start()
        token[...] = jnp.zeros(token.shape, token.dtype)

    thru = [pltpu.HBM(p.shape, p.dtype) for p in parts]
    res = pl.pallas_call(
        body, name=name, in_specs=[HBM] * (2 * n),
        out_shape=(pltpu.SemaphoreType.DMA((3 * n,)), pltpu.SemaphoreType.DMA((3 * n,)), *thru, *thru,
                   jax.ShapeDtypeStruct((8, LANE), F32)),
        out_specs=(SEM, SEM, *([HBM] * (2 * n)), pl.BlockSpec(memory_space=pltpu.VMEM)),
        input_output_aliases={i: 2 + i for i in range(2 * n)},
        compiler_params=pltpu.CompilerParams(has_side_effects=EFFECT),
    )(*[_in_hbm(p) for p in parts], *[_in_hbm(lax.empty(p.shape, p.dtype)) for p in parts])
    return res[:-1], res[-1]


def _rs_ici_wait(state, after, name):
    n = (len(state) - 2) // 2

    def body(*refs):
        send_sems, recv_sems = refs[0], refs[1]
        ins, lands = refs[2:2 + n], refs[2 + n:2 + 2 * n]
        for cp in _rs_ici_copies(ins, lands, send_sems, recv_sems):
            cp.wait_send()
            cp.wait_recv()

    thru = [pltpu.HBM(s.shape, s.dtype) for s in state[2:]]
    res = pl.pallas_call(
        body, name=name, in_specs=[SEM, SEM] + [HBM] * (2 * n) + [ANY], out_shape=tuple(thru),
        out_specs=tuple([HBM] * (2 * n)), input_output_aliases={2 + i: i for i in range(2 * n)},
        compiler_params=pltpu.CompilerParams(has_side_effects=EFFECT),
    )(*state, after)
    return list(res[n:])


def _ag_copies(shards, lands, send_sems, recv_sems):
    x, y, c = _place()
    mine = 4 * x + 2 * y + c
    peers = [(x, y, 1 - c)] + [(_flip(x, fx), _flip(y, fy), c) for fx, fy in CHIP_FLIPS]
    return [pltpu.make_async_remote_copy(
        src_ref=shards[t], dst_ref=lands[t].at[mine], send_sem=send_sems.at[4 * t + k], recv_sem=recv_sems.at[4 * t + k],
        device_id=peer, device_id_type=MESH) for t in range(len(shards)) for k, peer in enumerate(peers)]


def _ag_start(shards, after, name):
    n = len(shards)

    def body(*refs):
        ins, lands = refs[:n], refs[n:2 * n]
        send_sems, recv_sems = refs[2 * n + 1], refs[2 * n + 2]
        token = refs[-1]
        for cp in _ag_copies(ins, lands, send_sems, recv_sems):
            cp.start()
        token[...] = jnp.zeros(token.shape, token.dtype)

    thru = [pltpu.HBM(s.shape, s.dtype) for s in shards]
    land = [pltpu.HBM((NDEV,) + s.shape, s.dtype) for s in shards]
    res = pl.pallas_call(
        body, name=name, in_specs=[HBM] * (2 * n) + [ANY],
        out_shape=(pltpu.SemaphoreType.DMA((4 * n,)), pltpu.SemaphoreType.DMA((4 * n,)), *thru, *land,
                   jax.ShapeDtypeStruct((8, LANE), F32)),
        out_specs=(SEM, SEM, *([HBM] * (2 * n)), pl.BlockSpec(memory_space=pltpu.VMEM)),
        input_output_aliases={i: 2 + i for i in range(2 * n)},
        compiler_params=pltpu.CompilerParams(has_side_effects=EFFECT),
    )(*[_in_hbm(s) for s in shards], *[_in_hbm(lax.empty((NDEV,) + s.shape, s.dtype)) for s in shards], after)
    return res[:-1], res[-1]


def _ag_wait(state, after, name):
    n = (len(state) - 2) // 2

    def body(*refs):
        send_sems, recv_sems = refs[0], refs[1]
        ins, lands = refs[2:2 + n], refs[2 + n:2 + 2 * n]
        for cp in _ag_copies(ins, lands, send_sems, recv_sems):
            cp.wait_send()
            cp.wait_recv()

    thru = [pltpu.HBM(s.shape, s.dtype) for s in state[2:]]
    res = pl.pallas_call(
        body, name=name, in_specs=[SEM, SEM] + [HBM] * (2 * n) + [ANY], out_shape=tuple(thru),
        out_specs=tuple([HBM] * (2 * n)), input_output_aliases={2 + i: i for i in range(2 * n)},
        compiler_params=pltpu.CompilerParams(has_side_effects=EFFECT),
    )(*state, after)
    return list(res[:n]), list(res[n:])


def _ag_finish(shards, lands):
    n = len(shards)

    def body(*refs):
        ins, outs, stage = refs[:n], refs[2 * n:3 * n], refs[3 * n:4 * n]
        send_sems, recv_sems, local_sems = refs[4 * n:]
        x, y, c = _place()
        chips = [(_flip(x, fx), _flip(y, fy)) for fx, fy in CHIP_FLIPS]

        def passing(t, j, core, to):
            blk = outs[t].at[4 * chips[j][0] + 2 * chips[j][1] + core]
            return pltpu.make_async_remote_copy(src_ref=blk, dst_ref=blk, send_sem=send_sems.at[3 * t + j],
                                                recv_sem=recv_sems.at[3 * t + j], device_id=to, device_id_type=MESH)

        sends = [passing(t, j, c, (x, y, 1 - c)) for t in range(n) for j in range(3)]
        for cp in sends:
            cp.start()
        load = [pltpu.make_async_copy(ins[t], stage[t], local_sems.at[t]) for t in range(n)]
        mine = [pltpu.make_async_copy(stage[t], outs[t].at[4 * x + 2 * y + c], local_sems.at[t]) for t in range(n)]
        for cp in load:
            cp.start()
        for t in range(n):
            load[t].wait()
            mine[t].start()
        for t in range(n):
            for j in range(3):
                passing(t, j, 1 - c, (x, y, c)).wait_recv()
        for cp in sends:
            cp.wait_send()
        for cp in mine:
            cp.wait()

    return pl.pallas_call(
        body, name="ag_finish", in_specs=[ANY] * (2 * n), out_specs=[ANY] * n,
        out_shape=[jax.ShapeDtypeStruct(l.shape, l.dtype) for l in lands],
        input_output_aliases={n + i: i for i in range(n)},
        scratch_shapes=[pltpu.VMEM(s.shape, s.dtype) for s in shards]
        + [pltpu.SemaphoreType.DMA((3 * n,)), pltpu.SemaphoreType.DMA((3 * n,)), pltpu.SemaphoreType.DMA((n,))],
        compiler_params=_comm_params())(*shards, *lands)


def _row_tile(rows, largest=256):
    for t in (1024, 512, 256, 128, 64, 32, 16, 8):
        if t > largest:
            continue
        if rows % t == 0:
            return t
    return rows


def _rs_chip_sum(pos, g5, got):
    a, b = g5.shape[3:]
    ta = _row_tile(a, 1024)

    def kern(pos_ref, o_ref, g_ref, p_ref):
        p_ref[...] = (o_ref[...] + g_ref[...]).astype(BF16)

    def mine(f, i, pos_ref):
        return (pos_ref[0] ^ ((f + 1) & 1), pos_ref[1] ^ ((f + 1) >> 1), pos_ref[2], i, 0)

    spec = pltpu.PrefetchScalarGridSpec(
        num_scalar_prefetch=1, grid=(3, a // ta),
        in_specs=[BS((None, None, None, ta, b), mine), BS((None, ta, b), lambda f, i, pos_ref: (f + 1, i, 0))],
        out_specs=BS((None, ta, b), lambda f, i, pos_ref: (f, i, 0)))
    return _call(kern, name="rs_chip_sum", grid_spec=spec, out_shape=jax.ShapeDtypeStruct((3, a, b), BF16),
                          compiler_params=_params(2))(pos, g5, got)


def _rs_final_sum(pos, g5, got, recv):
    a, b = g5.shape[3:]
    ta = _row_tile(a, 1024)

    def kern(pos_ref, o_ref, g_ref, r_ref, s_ref):
        acc = o_ref[...] + g_ref[...]
        for f in range(3):
            acc = acc + r_ref[f].astype(F32)
        s_ref[...] = acc

    spec = pltpu.PrefetchScalarGridSpec(
        num_scalar_prefetch=1, grid=(a // ta,),
        in_specs=[BS((None, None, None, ta, b), lambda i, pos_ref: (pos_ref[0], pos_ref[1], pos_ref[2], i, 0)),
                  BS((None, ta, b), lambda i, pos_ref: (0, i, 0)), BS((3, ta, b), lambda i, pos_ref: (0, i, 0))],
        out_specs=BS((ta, b), lambda i, pos_ref: (i, 0)))
    return _call(kern, name="rs_final_sum", grid_spec=spec, out_shape=jax.ShapeDtypeStruct((a, b), F32),
                          compiler_params=_params(1))(pos, g5, got, recv)


def _reduce_scatter(grads, pos):
    g5s = [g.reshape((2, 2, 2) + g.shape[1:]) for g in grads]
    gots = _rs_d2d(g5s)
    parts = [_rs_chip_sum(pos, g, got) for g, got in zip(g5s, gots)]
    recvs = _rs_ici(parts)
    return [_rs_final_sum(pos, g, got, r) for g, got, r in zip(g5s, gots, recvs)]


def _rs_d2d_copies(ins, lands, send_sems, recv_sems):
    x, y, c = _place()
    return [pltpu.make_async_remote_copy(
        src_ref=ins[t].at[_flip(x, fx), _flip(y, fy), 1 - c], dst_ref=lands[t].at[f], send_sem=send_sems.at[4 * t + f],
        recv_sem=recv_sems.at[4 * t + f], device_id=(x, y, 1 - c), device_id_type=MESH)
        for t in range(len(ins)) for f, (fx, fy) in enumerate(REL_CHIPS)]


def _rs_d2d_start(g5s, name):
    n = len(g5s)

    def body(*refs):
        ins, lands = refs[:n], refs[n:2 * n]
        for cp in _rs_d2d_copies(ins, lands, refs[2 * n], refs[2 * n + 1]):
            cp.start()
        refs[-1][...] = jnp.zeros(refs[-1].shape, F32)

    thru = [pltpu.HBM(g.shape, g.dtype) for g in g5s]
    land = [pltpu.HBM((4,) + g.shape[3:], F32) for g in g5s]
    res = pl.pallas_call(
        body, name=name, in_specs=[HBM] * (2 * n),
        out_shape=(pltpu.SemaphoreType.DMA((4 * n,)), pltpu.SemaphoreType.DMA((4 * n,)), *thru, *land,
                   jax.ShapeDtypeStruct((8, LANE), F32)),
        out_specs=(SEM, SEM, *([HBM] * (2 * n)), pl.BlockSpec(memory_space=pltpu.VMEM)),
        input_output_aliases={i: 2 + i for i in range(2 * n)},
        compiler_params=pltpu.CompilerParams(has_side_effects=EFFECT),
    )(*[_in_hbm(g) for g in g5s], *[_in_hbm(lax.empty((4,) + g.shape[3:], F32)) for g in g5s])
    return res[:-1], res[-1]


def _rs_d2d_wait(state, after, name):
    n = (len(state) - 2) // 2

    def body(*refs):
        ins, lands = refs[2:2 + n], refs[2 + n:2 + 2 * n]
        for cp in _rs_d2d_copies(ins, lands, refs[0], refs[1]):
            cp.wait_send()
            cp.wait_recv()

    thru = [pltpu.HBM(s.shape, s.dtype) for s in state[2:]]
    res = pl.pallas_call(
        body, name=name, in_specs=[SEM, SEM] + [HBM] * (2 * n) + [ANY], out_shape=tuple(thru),
        out_specs=tuple([HBM] * (2 * n)), input_output_aliases={2 + i: i for i in range(2 * n)},
        compiler_params=pltpu.CompilerParams(has_side_effects=EFFECT),
    )(*state, after)
    return list(res[:n]), list(res[n:])


def _as_g5(grads):
    return [g.reshape((2, 2, 2) + g.shape[1:]) for g in grads]


def _rs_mid(g5s, gots, pos, tag):
    parts = [_rs_chip_sum(pos, g, got) for g, got in zip(g5s, gots)]
    state, token = _rs_ici_start(parts, "rs_ici_start_" + tag)
    return (g5s, gots, state, tag), token


def _rs_begin(grads, pos, tag, after=None):
    g5s = _as_g5(grads)
    return _rs_mid(g5s, _rs_d2d(g5s, after), pos, tag)


def _rs_end(pending, after, pos):
    g5s, gots, state, tag = pending
    recvs = _rs_ici_wait(state, after, "rs_ici_wait_" + tag)
    return [_rs_final_sum(pos, g, got, r) for g, got, r in zip(g5s, gots, recvs)]


def _sum_devices(v):
    _, r, _ = v.shape

    def kern(v_ref, o_ref):
        acc = v_ref[0]
        for d in range(1, NDEV):
            acc = acc + v_ref[d]
        o_ref[...] = acc

    return _call(kern, name="sum_devices", out_shape=jax.ShapeDtypeStruct((r, LANE), F32),
                          compiler_params=_comm_params())(v)


def _loss_head(xf, target):
    s = xf.shape[0]
    tm = _tile(s, 512)

    def kern(x_ref, t_ref, dx_ref, l_ref):
        err = x_ref[...] - t_ref[...]
        dx_ref[...] = err * (1.0 / DM)
        part = jnp.broadcast_to(0.5 * jnp.sum(jnp.mean(err * err, axis=-1, keepdims=True), axis=0, keepdims=True), (8, LANE))

        @pl.when(pl.program_id(0) == 0)
        def _():
            l_ref[...] = part

        @pl.when(pl.program_id(0) > 0)
        def _():
            l_ref[...] += part

    row = BS((tm, DM), lambda i: (i, 0))
    return _call(kern, name="loss_head", grid=(s // tm,), in_specs=[row, row],
                          out_specs=[row, BS((8, LANE), lambda i: (0, 0))],
                          out_shape=[jax.ShapeDtypeStruct((s, DM), F32), jax.ShapeDtypeStruct((8, LANE), F32)],
                          compiler_params=_params(1))(xf, target)


def _adamw(w, g, m, v, after=None):
    rows, cols = w.shape
    tr = _row_tile(rows)
    extra = () if after is None else (after,)

    def kern(w_ref, g_ref, m_ref, v_ref, *rest):
        d_ref, nm_ref, nv_ref = rest[-3:]
        gv = g_ref[...]
        nm = ADAM_B1 * m_ref[...] + (1.0 - ADAM_B1) * gv
        nv = ADAM_B2 * v_ref[...] + (1.0 - ADAM_B2) * (gv * gv)
        m_hat = nm / (1.0 - ADAM_B1 ** ADAM_STEP)
        v_hat = nv / (1.0 - ADAM_B2 ** ADAM_STEP)
        d_ref[...] = -ADAM_LR * (m_hat / (jnp.sqrt(v_hat) + ADAM_EPS) + ADAM_WD * w_ref[...])
        nm_ref[...] = nm
        nv_ref[...] = nv

    blk = BS((tr, cols), lambda i: (i, 0))
    shp = jax.ShapeDtypeStruct((rows, cols), F32)
    return _call(kern, name="adamw", grid=(rows // tr,), in_specs=[blk] * 4 + [ANY] * len(extra),
                          out_specs=[blk] * 3, out_shape=[shp] * 3, compiler_params=_params(1))(w, g, m, v, *extra)


def _adamw_nd(w, g, m, v, after=None):
    shape = w.shape
    two = (math.prod(shape[:-1]), shape[-1])
    return tuple(o.reshape(shape)
                 for o in _adamw(w.reshape(two), g.reshape(two), m.reshape(two), v.reshape(two), after))


def _pack_small(parts):
    flat = jnp.concatenate([p.reshape(-1) for p in parts])
    pad = (-flat.shape[0]) % (8 * LANE)
    return jnp.pad(flat, (0, pad)).reshape(-1, LANE)


def _unpack_small(packed, shapes, lead=()):
    flat = packed.reshape(lead + (-1,))
    out, off = [], 0
    for shp in shapes:
        n = math.prod(shp)
        out.append(flat[..., off:off + n].reshape(lead + tuple(shp)))
        off += n
    return out


def _blocks_of_columns(w):
    k, n = w.shape
    return w.reshape(k, NDEV, n // NDEV).transpose(1, 0, 2)


def _columns_of_blocks(wb):
    n, k, c = wb.shape
    return wb.transpose(1, 0, 2).reshape(k, n * c)


WEIGHT_NAMES = ('g_mix_pre', 'g_mix_post', 'g_cross_pre', 'g_mem', 'g_cross_post', 'g_ffn_pre', 'g_ffn_post', 'w_xq',
                'w_xkv', 'w_xo', 'w_ffn_gu', 'w_ffn_down', 'ab_w_in', 'ab_b_f', 'ab_conv_w', 'ab_w_out', 'c_w_in',
                'c_conv_w', 'c_conv_b', 'c_w_a', 'c_b_a', 'c_w_i', 'c_b_i', 'c_lam', 'c_w_out')
BIG = ('w_xq', 'w_xkv', 'w_xo', 'w_ffn_gu', 'w_ffn_down', 'ab_w_in', 'ab_w_out', 'c_w_in', 'c_w_a', 'c_w_i', 'c_w_out')
SMALL_SHARDED = ('ab_conv_w', 'c_conv_w', 'c_conv_b', 'c_b_a', 'c_b_i', 'c_lam')
REPLICATED = ('g_mix_pre', 'g_mix_post', 'g_cross_pre', 'g_mem', 'g_cross_post', 'g_ffn_pre', 'g_ffn_post', 'ab_b_f')


def _small_full(name, gathered):
    nd = gathered.ndim
    return jnp.moveaxis(gathered, 0, nd - 2).reshape(gathered.shape[1:-1] + (NDEV * gathered.shape[-1],))


def _small_shard(full, dev):
    c = full.shape[-1] // NDEV
    return lax.dynamic_slice_in_dim(full, dev * c, c, axis=full.ndim - 1)


def kernel(x, mem, g_mix_pre, g_mix_post, g_cross_pre, g_mem, g_cross_post, g_ffn_pre, g_ffn_post, w_xq, w_xkv, w_xo, w_ffn_gu, w_ffn_down, ab_w_in, ab_b_f, ab_conv_w, ab_w_out, c_w_in, c_conv_w, c_conv_b, c_w_a, c_b_a, c_w_i, c_b_i, c_lam, c_w_out, loss_target, m_g_mix_pre, m_g_mix_post, m_g_cross_pre, m_g_mem, m_g_cross_post, m_g_ffn_pre, m_g_ffn_post, m_w_xq, m_w_xkv, m_w_xo, m_w_ffn_gu, m_w_ffn_down, m_ab_w_in, m_ab_b_f, m_ab_conv_w, m_ab_w_out, m_c_w_in, m_c_conv_w, m_c_conv_b, m_c_w_a, m_c_b_a, m_c_w_i, m_c_b_i, m_c_lam, m_c_w_out, v_g_mix_pre, v_g_mix_post, v_g_cross_pre, v_g_mem, v_g_cross_post, v_g_ffn_pre, v_g_ffn_post, v_w_xq, v_w_xkv, v_w_xo, v_w_ffn_gu, v_w_ffn_down, v_ab_w_in, v_ab_b_f, v_ab_conv_w, v_ab_w_out, v_c_w_in, v_c_conv_w, v_c_conv_b, v_c_w_a, v_c_b_a, v_c_w_i, v_c_b_i, v_c_lam, v_c_w_out):
    args = locals()
    w = {n: args[n] for n in WEIGHT_NAMES}
    mom = {n: args["m_" + n] for n in WEIGHT_NAMES}
    var = {n: args["v_" + n] for n in WEIGHT_NAMES}
    pos = jnp.stack([lax.axis_index("x"), lax.axis_index("y"), lax.axis_index("c")]).astype(jnp.int32)
    dev = 4 * pos[0] + 2 * pos[1] + pos[2]
    xs, mems, target = x[0], mem[0], loss_target[0]
    n_even, n_odd = (DEPTH + 1) // 2, DEPTH // 2

    small_shapes = [w[n].shape for n in SMALL_SHARDED]
    small_w_all = _small_gather(_pack_small([w[n] for n in SMALL_SHARDED]))
    gathered_small = _unpack_small(small_w_all, small_shapes, (NDEV,))
    small = {n: _small_full(n, g) for n, g in zip(SMALL_SHARDED, gathered_small)}
    ab_bfb = jnp.broadcast_to(ab_b_f[:, :, None], (n_even, FOX_H, LANE))
    c_bai = jnp.stack([small['c_b_a'].reshape(n_odd, DM), small['c_b_i'].reshape(n_odd, DM)], axis=1)
    row = lambda a, l: a[l][None]

    REST = ('w_xq', 'w_xkv', 'w_xo', 'w_ffn_gu', 'w_ffn_down')

    def mixer_names(l):
        return ('ab_w_in', 'ab_w_out') if l % 2 == 0 else ('c_w_in', 'c_w_a', 'c_w_i', 'c_w_out')

    def shards_of(l, names):
        out = []
        for n in names:
            s = w[n][l if w[n].shape[0] == DEPTH else l // 2].astype(BF16)
            out.append(s.reshape(-1, s.shape[-1]))
        return out

    def mixer_weights(l, full):
        if l % 2 == 0:
            e = l // 2
            return (row(g_mix_pre, l), row(g_mix_post, l), _ab_pack(_columns_of_blocks(full['ab_w_in'])), ab_bfb[e],
                    small['ab_conv_w'][e], full['ab_w_out'].reshape(DM, DM))
        o = l // 2
        gate_w = lambda g: g.reshape(NDEV, LRU_NB, LRU_BW // NDEV, LRU_BW).transpose(1, 0, 2, 3).reshape(
            LRU_NB, LRU_BW, LRU_BW)
        return (row(g_mix_pre, l), row(g_mix_post, l), full['c_w_in'], small['c_conv_w'][o], row(small['c_conv_b'], o),
                jnp.stack([gate_w(full['c_w_a']), gate_w(full['c_w_i'])]), c_bai[o], row(small['c_lam'], o),
                full['c_w_out'].reshape(DM, DM))

    def rest_weights(l, full):
        cross = (row(g_cross_pre, l), row(g_mem, l), row(g_cross_post, l), full['w_xq'].reshape(DM, DM), full['w_xkv'],
                 full['w_xo'].reshape(DM, DM))
        ffn = (row(g_ffn_pre, l), row(g_ffn_post, l), full['w_ffn_gu'], full['w_ffn_down'].reshape(D_FF, DM))
        return cross, ffn

    def gathered(state, names, after, tag):
        shards, lands = _ag_wait(state, after, "ag_wait_" + tag)
        full = _ag_finish(shards, lands)
        return dict(zip(names, full)), full[0]

    saved, weights = [], []
    h = xs
    names_of = lambda l: mixer_names(l) + REST
    states = {}
    st_m, _ = _ag_start(shards_of(0, mixer_names(0)), small_w_all, "ag_start_0m")
    st_r, _ = _ag_start(shards_of(0, REST), st_m[2], "ag_start_0r")
    states[1], token = _ag_start(shards_of(1, names_of(1)), st_r[2], "ag_start_1")
    full_m, _ = gathered(st_m, mixer_names(0), xs, "0m")
    for l in range(DEPTH):
        if l > 0:
            full, done = gathered(states[l], names_of(l), h, str(l))
            full_m = full_r = full
            token = None
            if l + 2 < DEPTH:
                states[l + 2], token = _ag_start(shards_of(l + 2, names_of(l + 2)), done, "ag_start_%d" % (l + 2))
        mixer = mixer_weights(l, full_m)
        h, s_mix = (_fox_layer_fwd if l % 2 == 0 else _lru_layer_fwd)(h, *mixer, after=token)
        token = None
        if l == 0:
            full_r, done = gathered(st_r, REST, h, "0r")
            states[2], token = _ag_start(shards_of(2, names_of(2)), done, "ag_start_2")
        cross, ffn = rest_weights(l, full_r)
        h, s_cross = _cross_fwd(h, mems, *cross, after=token)
        h, s_ffn = _ffn_fwd(h, *ffn)
        saved.append((s_mix, s_cross, s_ffn))
        weights.append((mixer, cross, ffn))
    mixer_args = lambda l: weights[l][0]
    cross_args = lambda l: weights[l][1]
    ffn_args = lambda l: weights[l][2]
    dx, loss_rep = _loss_head(h, target)
    loss = lax.psum(loss_rep[0, 0], ("x", "y", "c"))

    grads = {n: [None] * w[n].shape[0] for n in BIG}
    partial = {n: [None] * w[n].shape[0] for n in REPLICATED + SMALL_SHARDED}
    def finish(pending, after):
        state, names, where = pending
        for n, g in zip(names, _rs_end(state, after, pos)):
            grads[n][where[n]] = g

    def unit(layer, names):
        return [layer[n][1] for n in names], names, {n: layer[n][0] for n in names}

    d2d = ici = None
    token = None
    for l in reversed(range(DEPTH)):
        s_mix, s_cross, s_ffn = saved[l]
        dx, partial['g_ffn_pre'][l], partial['g_ffn_post'][l], dwgu, dwd = _ffn_bwd(dx, s_ffn, *ffn_args(l), after=token)
        token = None
        if d2d is not None:
            g5s, gots = _rs_d2d_wait(d2d[0], dx, "rs_d2d_wait_%d" % (l + 1))
            state, token = _rs_mid(g5s, gots, pos, str(l + 1))
            ici, d2d = (state,) + d2d[1:], None
        (dx, partial['g_cross_pre'][l], partial['g_mem'][l], partial['g_cross_post'][l], dwq, dwkv, dwo) = _cross_bwd(
            dx, s_cross, mems, *cross_args(l), after=token)
        token = None
        layer = {'w_xq': (l, dwq.reshape(NDEV, DM // NDEV, DM)), 'w_xkv': (l, dwkv), 'w_xo': (l, dwo.reshape(NDEV, DM // NDEV, DM)),
                 'w_ffn_gu': (l, dwgu), 'w_ffn_down': (l, dwd.reshape(NDEV, D_FF // NDEV, DM))}
        if l == 0:
            gs, names, where = unit(layer, REST)
            state, token = _rs_begin(gs, pos, "0r")
            ici_rest = (state, names, where)
        if l % 2 == 0:
            e = l // 2
            (dx, partial['g_mix_pre'][l], partial['g_mix_post'][l], dwall, partial['ab_b_f'][e], partial['ab_conv_w'][e],
             dwout) = _fox_layer_bwd(dx, s_mix, *mixer_args(l), after=token)
            layer['ab_w_in'] = (e, _blocks_of_columns(_ab_unpack(dwall)))
            layer['ab_w_out'] = (e, dwout.reshape(NDEV, DM // NDEV, DM))
        else:
            o = l // 2
            (dx, partial['g_mix_pre'][l], partial['g_mix_post'][l], dwin, partial['c_conv_w'][o], dconvb, dwai, dbai, dlam,
             dwout) = _lru_layer_bwd(dx, s_mix, *mixer_args(l), after=token)
            partial['c_conv_b'][o], partial['c_lam'][o] = dconvb[0], dlam[0]
            partial['c_b_a'][o], partial['c_b_i'][o] = dbai[0].reshape(LRU_NB, LRU_BW), dbai[1].reshape(LRU_NB, LRU_BW)
            rows = LRU_BW // NDEV
            by_dev = lambda d: d.reshape(LRU_NB, NDEV, rows, LRU_BW).transpose(1, 0, 2, 3).reshape(NDEV, LRU_NB * rows, LRU_BW)
            layer['c_w_in'] = (o, dwin)
            layer['c_w_a'] = (o, by_dev(dwai[0]))
            layer['c_w_i'] = (o, by_dev(dwai[1]))
            layer['c_w_out'] = (o, dwout.reshape(NDEV, DM // NDEV, DM))
        token = None
        if ici is not None:
            finish(ici, dx)
            ici = None
        if l > 0:
            gs, names, where = unit(layer, list(layer))
            state, token = _rs_d2d_start(_as_g5(gs), "rs_d2d_start_%d" % l)
            d2d = (state, names, where)
    small_names = REPLICATED + SMALL_SHARDED
    small_parts = [jnp.stack([p.reshape(w[n].shape[1:] if n in REPLICATED else small[n].shape[1:]) for p in partial[n]])
                   for n in small_names]
    small_all = _small_gather(_pack_small(small_parts))
    reduced = _unpack_small(_sum_devices(small_all), [p.shape for p in small_parts])
    grad = {}
    for n, g in zip(small_names, reduced):
        grad[n] = g if n in REPLICATED else _small_shard(g, dev)

    gs, names, where = unit(layer, mixer_names(0))
    state, token = _rs_begin(gs, pos, "0m", after=small_all)
    ici_mixer = (state, names, where)
    finish(ici_rest, dx)

    delta, new_m, new_v = {}, {}, {}
    last = mixer_names(0)
    for n in BIG:
        if n not in last:
            grad[n] = jnp.stack(grads[n]).reshape(w[n].shape)
            delta[n], new_m[n], new_v[n] = _adamw_nd(w[n], grad[n], mom[n], var[n], token)
            token = delta[n]
    shapes = [w[n].shape for n in small_names]
    packed = [_pack_small([t[n] for n in small_names]) for t in (w, grad, mom, var)]
    res_small = _adamw(*packed, after=token)
    for res, out in zip(res_small, (delta, new_m, new_v)):
        for n, val in zip(small_names, _unpack_small(res, shapes)):
            out[n] = val
    finish(ici_mixer, res_small[0])
    for n in last:
        grad[n] = jnp.stack(grads[n]).reshape(w[n].shape)
        delta[n], new_m[n], new_v[n] = _adamw_nd(w[n], grad[n], mom[n], var[n])

    return (loss, dx[None], *[grad[n] for n in WEIGHT_NAMES], *[delta[n] for n in WEIGHT_NAMES],
            *[new_m[n] for n in WEIGHT_NAMES], *[new_v[n] for n in WEIGHT_NAMES])
```

```python
import functools
import math

import jax
import jax.numpy as jnp
from jax import lax
from jax.experimental import pallas as pl
from jax.experimental.pallas import tpu as pltpu

F32 = jnp.float32
BF16 = jnp.bfloat16
BS = pl.BlockSpec
ANY = pl.BlockSpec(memory_space=pl.ANY)
MESH = pl.DeviceIdType.MESH

DM = 1024
DEPTH = 4
EPS = 1e-6
NEG = -1e30
FOX_W = 512
FOX_HD = 64
FOX_H = 8
SC_W = 512
SC_K = 3
AB_IN = 3 * FOX_W + FOX_H + 3 * SC_W
AB_PAD = 3200
LRU_BW = 256
LRU_NB = 4
RG_K = 4
RG_C = 8.0
MEM_H = 4
MEM_HD = 256
D_FF = 2816
NDEV = 8
FFB = 2 * D_FF // NDEV
ADAM_LR, ADAM_B1, ADAM_B2, ADAM_EPS, ADAM_WD, ADAM_STEP = 0.001, 0.9, 0.999, 1e-08, 0.01, 10

LANE = 128
VMEM_LIMIT = 48 * 1024 * 1024


def _params(ngrid):
    return pltpu.CompilerParams(dimension_semantics=("arbitrary",) * ngrid, vmem_limit_bytes=VMEM_LIMIT)


def _call(kern, **kwargs):
    return pl.pallas_call(kern, **kwargs)


TK_RED = 2048
TM_SUM = 1024


def _tile(n, t):
    return t if n % t == 0 else n


def _mm(name, a, b, *, grid, a_spec, b_spec, o_spec, out_shape, dn, out_dtype=F32):
    nred = grid[-1]
    ngrid = len(grid)

    def kern(a_ref, b_ref, o_ref, *scratch):
        p = lax.dot_general(a_ref[...].astype(BF16), b_ref[...].astype(BF16), (dn, ((), ())),
                            preferred_element_type=F32)
        if nred == 1:
            o_ref[...] = p.astype(o_ref.dtype)
            return
        acc = scratch[0] if scratch else o_ref
        r = pl.program_id(ngrid - 1)

        @pl.when(r == 0)
        def _():
            acc[...] = p

        @pl.when(r > 0)
        def _():
            acc[...] += p

        if scratch:
            @pl.when(r == nred - 1)
            def _():
                o_ref[...] = acc[...].astype(o_ref.dtype)

    blk = tuple(d for d in o_spec.block_shape if d is not None)
    scratch = [pltpu.VMEM(blk, F32)] if (nred > 1 and out_dtype != F32) else []
    return _call(kern, name=name, grid=grid, in_specs=[a_spec, b_spec], out_specs=o_spec,
                          out_shape=jax.ShapeDtypeStruct(out_shape, out_dtype), scratch_shapes=scratch,
                          compiler_params=_params(ngrid))(a, b)


NN = ((1,), (0,))
NT = ((1,), (1,))
TN = ((0,), (0,))


def _mm_nn(name, a, w, out_dtype=F32, tn=None):
    m, k = a.shape
    n = w.shape[1]
    tm = _tile(m, 512)
    tn = n if tn is None else tn
    return _mm(name, a, w, grid=(m // tm, n // tn, 1), a_spec=BS((tm, k), lambda i, j, r: (i, 0)),
               b_spec=BS((k, tn), lambda i, j, r: (0, j)), o_spec=BS((tm, tn), lambda i, j, r: (i, j)),
               out_shape=(m, n), dn=NN, out_dtype=out_dtype)


def _mm_nt(name, a, w, out_dtype=F32, tn=None):
    m, n = a.shape
    k = w.shape[0]
    tm = _tile(m, 512)
    tn = n if tn is None else tn
    return _mm(name, a, w, grid=(m // tm, n // tn), a_spec=BS((tm, tn), lambda i, r: (i, r)),
               b_spec=BS((k, tn), lambda i, r: (0, r)), o_spec=BS((tm, k), lambda i, r: (i, 0)),
               out_shape=(m, k), dn=NT, out_dtype=out_dtype)


def _mm_tn(name, a, b, tn=None):
    m, k = a.shape
    n = b.shape[1]
    tm = _tile(m, TK_RED)
    tn = n if tn is None else tn
    return _mm(name, a, b, grid=(n // tn, m // tm), a_spec=BS((tm, k), lambda j, r: (r, 0)),
               b_spec=BS((tm, tn), lambda j, r: (r, j)), o_spec=BS((k, tn), lambda j, r: (0, j)),
               out_shape=(k, n), dn=TN)


def _bmm_nn(name, a, w, out_dtype=F32):
    m, k = a.shape
    g, _, n = w.shape
    tm = _tile(m, 512)
    return _mm(name, a, w, grid=(g, m // tm, 1), a_spec=BS((tm, k), lambda q, i, r: (i, 0)),
               b_spec=BS((None, k, n), lambda q, i, r: (q, 0, 0)), o_spec=BS((None, tm, n), lambda q, i, r: (q, i, 0)),
               out_shape=(g, m, n), dn=NN, out_dtype=out_dtype)


def _bmm_tn(name, a, b):
    m, k = a.shape
    g, _, n = b.shape
    tm = _tile(m, TK_RED)
    return _mm(name, a, b, grid=(g, m // tm), a_spec=BS((tm, k), lambda q, r: (r, 0)),
               b_spec=BS((None, tm, n), lambda q, r: (q, r, 0)), o_spec=BS((None, k, n), lambda q, r: (q, 0, 0)),
               out_shape=(g, k, n), dn=TN)


def _bmm_nt_sum(name, a, w):
    g, m, n = a.shape
    k = w.shape[1]
    tm = _tile(m, TM_SUM)
    return _mm(name, a, w, grid=(m // tm, g), a_spec=BS((None, tm, n), lambda i, q: (q, i, 0)),
               b_spec=BS((None, k, n), lambda i, q: (q, 0, 0)), o_spec=BS((tm, k), lambda i, q: (i, 0)),
               out_shape=(m, k), dn=NT)


def _bmm_nn_sum(name, a, w):
    g, m, k = a.shape
    n = w.shape[2]
    tm = _tile(m, TM_SUM)
    return _mm(name, a, w, grid=(m // tm, g), a_spec=BS((None, tm, k), lambda i, q: (q, i, 0)),
               b_spec=BS((None, k, n), lambda i, q: (q, 0, 0)), o_spec=BS((tm, n), lambda i, q: (i, 0)),
               out_shape=(m, n), dn=NN)


def _bbmm_tn(name, a, b):
    g, m, k = a.shape
    n = b.shape[2]
    tm = _tile(m, TK_RED)
    return _mm(name, a, b, grid=(g, m // tm), a_spec=BS((None, tm, k), lambda q, r: (q, r, 0)),
               b_spec=BS((None, tm, n), lambda q, r: (q, r, 0)), o_spec=BS((None, k, n), lambda q, r: (q, 0, 0)),
               out_shape=(g, k, n), dn=TN)


def _rstd(x):
    return lax.rsqrt(jnp.mean(x * x, axis=-1, keepdims=True) + EPS)


def _norm_fwd(x, g, after=None):
    rows = x.shape[0]
    tm = _tile(rows, 512)

    def kern(x_ref, g_ref, *rest):
        xv = x_ref[...]
        rest[-1][...] = ((xv * _rstd(xv)) * g_ref[...]).astype(BF16)

    extra = () if after is None else (after,)
    return _call(kern, name="norm_fwd", grid=(rows // tm,),
                          in_specs=[BS((tm, DM), lambda i: (i, 0)), BS((1, DM), lambda i: (0, 0))] + [ANY] * len(extra),
                          out_specs=BS((tm, DM), lambda i: (i, 0)),
                          out_shape=jax.ShapeDtypeStruct((rows, DM), BF16), compiler_params=_params(1))(x, g, *extra)


def _norm_res(x, y, g):
    rows = x.shape[0]
    tm = _tile(rows, 512)

    def kern(x_ref, y_ref, g_ref, o_ref):
        yv = y_ref[...]
        o_ref[...] = x_ref[...] + (yv * _rstd(yv)) * g_ref[...]

    row = BS((tm, DM), lambda i: (i, 0))
    return _call(kern, name="norm_res", grid=(rows // tm,),
                          in_specs=[row, row, BS((1, DM), lambda i: (0, 0))], out_specs=row,
                          out_shape=jax.ShapeDtypeStruct((rows, DM), F32), compiler_params=_params(1))(x, y, g)


def _norm_bwd(z, dout, g, resid, out_dtype, after=None):
    rows = z.shape[0]
    tm = _tile(rows, 512)
    has_res = resid is not None

    def kern(*refs):
        z_ref, d_ref, g_ref = refs[:3]
        r_ref = refs[3] if has_res else None
        dz_ref, dg_ref = refs[-2:]
        zv = z_ref[...]
        dv = d_ref[...].astype(F32)
        r = _rstd(zv)
        zh = zv * r
        dzh = dv * g_ref[...]
        dz = r * (dzh - zh * jnp.mean(dzh * zh, axis=-1, keepdims=True))
        if has_res:
            dz = dz + r_ref[...]
        dz_ref[...] = dz.astype(dz_ref.dtype)
        part = jnp.sum(dv * zh, axis=0, keepdims=True)

        @pl.when(pl.program_id(0) == 0)
        def _():
            dg_ref[...] = part

        @pl.when(pl.program_id(0) > 0)
        def _():
            dg_ref[...] += part

    row = BS((tm, DM), lambda i: (i, 0))
    vec = BS((1, DM), lambda i: (0, 0))
    ins = [row, row, vec] + ([row] if has_res else []) + ([ANY] if after is not None else [])
    args = (z, dout, g) + ((resid,) if has_res else ()) + ((after,) if after is not None else ())
    return _call(kern, name="norm_bwd_res" if has_res else "norm_bwd", grid=(rows // tm,), in_specs=ins,
                          out_specs=[row, vec],
                          out_shape=[jax.ShapeDtypeStruct((rows, DM), out_dtype), jax.ShapeDtypeStruct((1, DM), F32)],
                          compiler_params=_params(1))(*args)


def _ffn_up(h, wgu4):
    s = h.shape[0]
    tm = _tile(s, 512)

    def kern(h_ref, w_ref, gu_ref, a_ref):
        hv = h_ref[...]
        gate = jnp.dot(hv, w_ref[0], preferred_element_type=F32)
        up = jnp.dot(hv, w_ref[1], preferred_element_type=F32)
        gu_ref[0] = gate.astype(BF16)
        gu_ref[1] = up.astype(BF16)
        a_ref[...] = (gate * jax.nn.sigmoid(gate) * up).astype(BF16)

    return _call(
        kern, name="ffn_up", grid=(4, s // tm),
        in_specs=[BS((tm, DM), lambda j, i: (i, 0)), BS((2, None, DM, FFB), lambda j, i: (0, j, 0, 0))],
        out_specs=[BS((2, None, tm, FFB), lambda j, i: (0, j, i, 0)), BS((None, tm, FFB), lambda j, i: (j, i, 0))],
        out_shape=[jax.ShapeDtypeStruct((2, 4, s, FFB), BF16), jax.ShapeDtypeStruct((4, s, FFB), BF16)],
        compiler_params=_params(2))(h, wgu4)


def _ffn_da(dy, wd4, gu):
    s = dy.shape[0]
    tm = _tile(s, 512)

    def kern(dy_ref, w_ref, gu_ref, o_ref):
        da = lax.dot_general(dy_ref[...], w_ref[...], (NT, ((), ())), preferred_element_type=F32)
        gate = gu_ref[0].astype(F32)
        up = gu_ref[1].astype(F32)
        sg = jax.nn.sigmoid(gate)
        o_ref[0] = (da * up * (sg * (1.0 + gate * (1.0 - sg)))).astype(BF16)
        o_ref[1] = (da * (gate * sg)).astype(BF16)

    blk = BS((2, None, tm, FFB), lambda j, i: (0, j, i, 0))
    return _call(
        kern, name="ffn_da", grid=(4, s // tm),
        in_specs=[BS((tm, DM), lambda j, i: (i, 0)), BS((None, FFB, DM), lambda j, i: (j, 0, 0)), blk],
        out_specs=blk, out_shape=jax.ShapeDtypeStruct((2, 4, s, FFB), BF16), compiler_params=_params(2))(dy, wd4, gu)


def _ffn_fwd(x, gpre, gpost, wgu, wd):
    h = _norm_fwd(x, gpre)
    gu, a = _ffn_up(h, wgu.reshape(2, 4, DM, FFB))
    y = _bmm_nn_sum("ffn_down", a, wd.reshape(4, FFB, DM))
    return _norm_res(x, y, gpost), (x, h, gu, a, y)


def _ffn_bwd(dxo, saved, gpre, gpost, wgu, wd, after=None):
    x, h, gu, a, y = saved
    s = x.shape[0]
    dy, dgpost = _norm_bwd(y, dxo, gpost, None, BF16, after)
    dgu = _ffn_da(dy, wd.reshape(4, FFB, DM), gu).reshape(8, s, FFB)
    dwd = _bmm_tn_a3("ffn_dwd", a, dy)
    dwgu = _bmm_tn("ffn_dwgu", h, dgu)
    dh = _bmm_nt_sum("ffn_dh", dgu, wgu)
    dx, dgpre = _norm_bwd(x, dh, gpre, dxo, F32)
    return dx, dgpre, dgpost, dwgu, dwd.reshape(D_FF, DM)


def _bmm_tn_a3(name, a, b):
    g, m, k = a.shape
    n = b.shape[1]
    tm = _tile(m, TK_RED)
    return _mm(name, a, b, grid=(g, m // tm), a_spec=BS((None, tm, k), lambda q, r: (q, r, 0)),
               b_spec=BS((tm, n), lambda q, r: (r, 0)), o_spec=BS((None, k, n), lambda q, r: (q, 0, 0)),
               out_shape=(g, k, n), dn=TN)


def _softmax_rows(s):
    m = jnp.max(s, axis=-1, keepdims=True)
    p = jnp.exp(s - m)
    return p / jnp.sum(p, axis=-1, keepdims=True)


def _xattn_fwd_call(h, wq, kv):
    s = h.shape[0]
    mlen = kv.shape[1]
    tm = _tile(s, 512)
    scale = MEM_HD ** -0.5

    def kern(h_ref, w_ref, k_ref, v_ref, q_ref, o_ref):
        q = jnp.dot(h_ref[...], w_ref[...], preferred_element_type=F32).astype(BF16)
        q_ref[...] = q
        sc = lax.dot_general(q, k_ref[...], (NT, ((), ())), preferred_element_type=F32) * scale
        p = _softmax_rows(sc)
        o_ref[...] = jnp.dot(p.astype(BF16), v_ref[...], preferred_element_type=F32).astype(BF16)

    blk = BS((tm, MEM_HD), lambda i, hd: (i, hd))
    return _call(
        kern, name="xattn_fwd", grid=(s // tm, MEM_H),
        in_specs=[BS((tm, DM), lambda i, hd: (i, 0)), BS((DM, MEM_HD), lambda i, hd: (0, hd)),
                  BS((None, mlen, MEM_HD), lambda i, hd: (hd, 0, 0)),
                  BS((None, mlen, MEM_HD), lambda i, hd: (MEM_H + hd, 0, 0))],
        out_specs=[blk, blk],
        out_shape=[jax.ShapeDtypeStruct((s, DM), BF16), jax.ShapeDtypeStruct((s, DM), BF16)],
        compiler_params=_params(2))(h, wq, kv, kv)


def _xattn_bwd_call(q, kv, do):
    s = q.shape[0]
    mlen = kv.shape[1]
    tm = _tile(s, 512)
    scale = MEM_HD ** -0.5

    def kern(q_ref, k_ref, v_ref, do_ref, dq_ref, dkv_ref):
        qv, kvv, vv, dov = q_ref[...], k_ref[...], v_ref[...], do_ref[...]
        sc = lax.dot_general(qv, kvv, (NT, ((), ())), preferred_element_type=F32) * scale
        p = _softmax_rows(sc)
        dp = lax.dot_general(dov, vv, (NT, ((), ())), preferred_element_type=F32)
        ds = (p * (dp - jnp.sum(dp * p, axis=-1, keepdims=True)) * scale).astype(BF16)
        dq_ref[...] = jnp.dot(ds, kvv, preferred_element_type=F32).astype(BF16)
        dk = lax.dot_general(ds, qv, (TN, ((), ())), preferred_element_type=F32)
        dv = lax.dot_general(p.astype(BF16), dov, (TN, ((), ())), preferred_element_type=F32)

        @pl.when(pl.program_id(1) == 0)
        def _():
            dkv_ref[0] = dk
            dkv_ref[1] = dv

        @pl.when(pl.program_id(1) > 0)
        def _():
            dkv_ref[0] += dk
            dkv_ref[1] += dv

    blk = BS((tm, MEM_HD), lambda hd, i: (i, hd))
    return _call(
        kern, name="xattn_bwd", grid=(MEM_H, s // tm),
        in_specs=[blk, BS((None, mlen, MEM_HD), lambda hd, i: (hd, 0, 0)),
                  BS((None, mlen, MEM_HD), lambda hd, i: (MEM_H + hd, 0, 0)), blk],
        out_specs=[blk, BS((2, None, mlen, MEM_HD), lambda hd, i: (0, hd, 0, 0))],
        out_shape=[jax.ShapeDtypeStruct((s, DM), BF16), jax.ShapeDtypeStruct((2, MEM_H, mlen, MEM_HD), F32)],
        compiler_params=_params(2))(q, kv, kv, do)


def _cross_fwd(x, mem, gpre, gmem, gpost, wq, wkv, wo, after=None):
    h = _norm_fwd(x, gpre, after)
    mn = _norm_fwd(mem, gmem)
    kv = _bmm_nn("xattn_kv", mn, wkv, BF16)
    q, o = _xattn_fwd_call(h, wq, kv)
    y = _mm_nn("xattn_out", o, wo)
    return _norm_res(x, y, gpost), (x, h, mn, kv, q, o, y)


def _cross_bwd(dxo, saved, mem, gpre, gmem, gpost, wq, wkv, wo, after=None):
    x, h, mn, kv, q, o, y = saved
    mlen = mem.shape[0]
    dy, dgpost = _norm_bwd(y, dxo, gpost, None, BF16, after)
    do = _mm_nt("xattn_do", dy, wo, BF16)
    dwo = _mm_tn("xattn_dwo", o, dy)
    dq, dkv = _xattn_bwd_call(q, kv, do)
    dwq = _mm_tn("xattn_dwq", h, dq)
    dh = _mm_nt("xattn_dh", dq, wq)
    dkv8 = dkv.reshape(8, mlen, MEM_HD)
    dwkv = _bmm_tn("xattn_dwkv", mn, dkv8)
    dmn = _bmm_nt_sum("xattn_dmn", dkv8, wkv)
    _, dgmem = _norm_bwd(mem, dmn, gmem, None, BF16)
    dx, dgpre = _norm_bwd(x, dh, gpre, dxo, F32)
    return dx, dgpre, dgmem, dgpost, dwq, dwkv, dwo


def _log_sigmoid(z):
    return jnp.minimum(z, 0.0) - jnp.log1p(jnp.exp(-jnp.abs(z)))


def _lane_scan_steps():
    return (1, 2, 4, 8, 16, 32, 64)


def _fox_cum(frow, bfb):
    s = frow.shape[1]

    def kern(f_ref, b_ref, o_ref):
        lane = lax.broadcasted_iota(jnp.int32, (FOX_H, LANE), 1)
        carry = jnp.zeros((FOX_H, 1), F32)
        for c in range(s // LANE):
            sl = slice(c * LANE, (c + 1) * LANE)
            lf = _log_sigmoid(f_ref[:, sl] + b_ref[...])
            v = lf
            for d in _lane_scan_steps():
                v = v + jnp.where(lane >= d, pltpu.roll(v, d, 1), 0.0)
            o_ref[:, sl] = v + carry
            carry = carry + jnp.sum(lf, axis=1, keepdims=True)

    return _call(kern, name="fox_cum", out_shape=jax.ShapeDtypeStruct((FOX_H, s), F32),
                          compiler_params=pltpu.CompilerParams(vmem_limit_bytes=VMEM_LIMIT))(frow, bfb)


def _fox_dlogf(dcq, dck, frow, bfb):
    s = frow.shape[1]

    def kern(q_ref, d_ref, f_ref, b_ref, df_ref, db_ref):
        lane = lax.broadcasted_iota(jnp.int32, (FOX_H, LANE), 1)
        carry = jnp.zeros((FOX_H, 1), F32)
        dbf = jnp.zeros((FOX_H, 1), F32)
        for c in reversed(range(s // LANE)):
            sl = slice(c * LANE, (c + 1) * LANE)
            dc = q_ref[:, sl] - d_ref[:, sl]
            v = dc
            for d in _lane_scan_steps():
                v = v + jnp.where(lane < LANE - d, pltpu.roll(v, LANE - d, 1), 0.0)
            v = v + carry
            carry = carry + jnp.sum(dc, axis=1, keepdims=True)
            df = v * jax.nn.sigmoid(-(f_ref[:, sl] + b_ref[...]))
            df_ref[:, sl] = df
            dbf = dbf + jnp.sum(df, axis=1, keepdims=True)
        db_ref[...] = jnp.broadcast_to(dbf, (FOX_H, LANE))

    return _call(kern, name="fox_dlogf",
                          out_shape=[jax.ShapeDtypeStruct((FOX_H, s), F32), jax.ShapeDtypeStruct((FOX_H, LANE), F32)],
                          compiler_params=pltpu.CompilerParams(vmem_limit_bytes=VMEM_LIMIT))(dcq, dck, frow, bfb)


FOX_TQ = 512
Q_COL, K_COL, V_COL = 0, FOX_W // LANE, 2 * FOX_W // LANE
B_COL, C_COL, U_COL = 12, 16, 20


def _bf16_terms(c):
    hi = c.astype(BF16).astype(F32)
    mid = (c - hi).astype(BF16).astype(F32)
    return hi, mid, (c - hi - mid).astype(BF16).astype(F32)


def _fox_operands(qv, kv, cq, ck, lane, hh, scale):
    sel = (lane < FOX_HD) if hh == 0 else (lane >= FOX_HD)
    b0 = FOX_HD if hh == 0 else 0
    qa = jnp.where(sel, qv * scale, 0.0)
    ka = jnp.where(sel, kv, 0.0)
    for n, (tq_, tk_) in enumerate(zip(_bf16_terms(cq), _bf16_terms(ck))):
        qa = jnp.where(lane == b0 + n, tq_, jnp.where(lane == b0 + 3 + n, 1.0, qa))
        ka = jnp.where(lane == b0 + n, 1.0, jnp.where(lane == b0 + 3 + n, -tk_, ka))
    return sel, qa.astype(BF16), ka.astype(BF16)


def _fox_logits(qa, ka, causal):
    sc = lax.dot_general(qa, ka, (NT, ((), ())), preferred_element_type=F32)
    return sc if causal is None else jnp.where(causal, sc, NEG)


def _fox_prep(proj, cumc):
    s = proj.shape[0]
    tp = _tile(s, 512)
    scale = FOX_HD ** -0.5

    def kern(q_ref, k_ref, c_ref, qa_ref, ka_ref):
        lane = lax.broadcasted_iota(jnp.int32, (tp, LANE), 1)
        for hh in range(2):
            _, qa_ref[hh], ka_ref[hh] = _fox_operands(q_ref[...], k_ref[...], c_ref[hh], c_ref[hh], lane, hh, scale)

    pair = BS((2, tp, LANE), lambda hp, i: (hp, i, 0))
    shp = jax.ShapeDtypeStruct((FOX_H, s, LANE), BF16)
    return _call(kern, name="fox_prep", grid=(4, s // tp),
                 in_specs=[BS((tp, LANE), lambda hp, i: (i, Q_COL + hp)), BS((tp, LANE), lambda hp, i: (i, K_COL + hp)), pair],
                 out_specs=[pair, pair], out_shape=[shp, shp], compiler_params=_params(2))(proj, proj, cumc)


def _fox_fwd_call(proj, qa, ka):
    s = proj.shape[0]
    tq = _tile(s, FOX_TQ)
    nq = s // tq
    reps = tq // LANE
    scale = FOX_HD ** -0.5

    def kern(qa_ref, ka_ref, v_ref, o_ref, lse_ref, m_s, l_s, acc_s):
        i = pl.program_id(1)
        j = pl.program_id(2)
        lane = lax.broadcasted_iota(jnp.int32, (tq, LANE), 1)

        @pl.when(j == 0)
        def _():
            m_s[...] = jnp.full(m_s.shape, NEG, F32)
            l_s[...] = jnp.zeros(l_s.shape, F32)
            acc_s[...] = jnp.zeros(acc_s.shape, F32)

        def step(diagonal):
            vb = v_ref[...].astype(BF16)
            causal = (lax.broadcasted_iota(jnp.int32, (tq, tq), 0) >= lax.broadcasted_iota(jnp.int32, (tq, tq), 1)
                      if diagonal else None)
            for hh in range(2):
                sc = _fox_logits(qa_ref[hh], ka_ref[hh], causal)
                m_prev = m_s[hh]
                m_new = jnp.maximum(m_prev, jnp.max(sc, axis=-1, keepdims=True))
                alpha = jnp.exp(m_prev - m_new)
                p = jnp.exp(sc - m_new)
                l_s[hh] = alpha * l_s[hh] + jnp.sum(p, axis=-1, keepdims=True)
                acc_s[hh] = alpha * acc_s[hh] + jnp.dot(p.astype(BF16), vb, preferred_element_type=F32)
                m_s[hh] = m_new

        @pl.when(j < i)
        def _():
            step(False)

        @pl.when(j == i)
        def _():
            step(True)
            o_ref[...] = jnp.where(lane < FOX_HD, acc_s[0] / l_s[0], acc_s[1] / l_s[1])
            for hh in range(2):
                lse_ref[hh] = jnp.broadcast_to(m_s[hh] + jnp.log(l_s[hh]), (tq, LANE))

    kvi = lambda hp, i, j: jnp.minimum(j, i)
    return _call(
        kern, name="fox_fwd", grid=(4, nq, nq),
        in_specs=[BS((2, tq, LANE), lambda hp, i, j: (hp, i, 0)),
                  BS((2, tq, LANE), lambda hp, i, j: (hp, kvi(hp, i, j), 0)),
                  BS((tq, LANE), lambda hp, i, j: (kvi(hp, i, j), V_COL + hp))],
        out_specs=[BS((tq, LANE), lambda hp, i, j: (i, hp)), BS((2, tq, LANE), lambda hp, i, j: (hp, i, 0))],
        out_shape=[jax.ShapeDtypeStruct((s, FOX_W), F32), jax.ShapeDtypeStruct((FOX_H, s, LANE), F32)],
        scratch_shapes=[pltpu.VMEM((2, tq, 1), F32), pltpu.VMEM((2, tq, 1), F32), pltpu.VMEM((2, tq, LANE), F32)],
        compiler_params=_params(3))(qa, ka, proj)


ROWSUM_M = 16


def _fox_bwd_call(proj, o, lse, dcat, qa, ka):
    s = proj.shape[0]
    tq = _tile(s, FOX_TQ)
    nq = s // tq
    reps = tq // LANE
    scale = FOX_HD ** -0.5

    def kern(qa_ref, ka_ref, v_ref, do_ref, o_ref, lse_ref, dq_ref, dk_ref, dv_ref, dck_ref, dcq_ref):
        j = pl.program_id(1)
        i = pl.program_id(2)
        lane = lax.broadcasted_iota(jnp.int32, (tq, LANE), 1)
        ones = jnp.ones((ROWSUM_M, tq), BF16)

        @pl.when((j == 0) & (i == 0))
        def _():
            dq_ref[...] = jnp.zeros(dq_ref.shape, F32)
            dcq_ref[...] = jnp.zeros(dcq_ref.shape, F32)

        @pl.when(i == j)
        def _():
            dk_ref[...] = jnp.zeros(dk_ref.shape, F32)
            dv_ref[...] = jnp.zeros(dv_ref.shape, F32)
            dck_ref[...] = jnp.zeros(dck_ref.shape, F32)

        def step(diagonal):
            dov = do_ref[...]
            ov = o_ref[...]
            vb = v_ref[...].astype(BF16)
            causal = (lax.broadcasted_iota(jnp.int32, (tq, tq), 0) >= lax.broadcasted_iota(jnp.int32, (tq, tq), 1)
                      if diagonal else None)
            dq_t = jnp.zeros((tq, LANE), F32)
            dk_t = jnp.zeros((tq, LANE), F32)
            dv_t = jnp.zeros((tq, LANE), F32)
            for hh in range(2):
                sel = (lane < FOX_HD) if hh == 0 else (lane >= FOX_HD)
                qa, ka = qa_ref[hh], ka_ref[hh]
                dom32 = jnp.where(sel, dov, 0.0)
                dom = dom32.astype(BF16)
                sc = _fox_logits(qa, ka, causal)
                p = jnp.exp(sc - jnp.tile(lse_ref[hh], (1, reps)))
                dp = lax.dot_general(dom, vb, (NT, ((), ())), preferred_element_type=F32)
                delta = jnp.sum(dom32 * ov, axis=-1, keepdims=True)
                ds = p * (dp - delta)
                dsb = ds.astype(BF16)
                dq_t = jnp.where(sel, jnp.dot(dsb, ka, preferred_element_type=F32) * scale, dq_t)
                dk_t = jnp.where(sel, lax.dot_general(dsb, qa, (TN, ((), ())), preferred_element_type=F32), dk_t)
                dv_t = dv_t + lax.dot_general(p.astype(BF16), dom, (TN, ((), ())), preferred_element_type=F32)
                dck_ref[hh] += jnp.sum(ds, axis=0, keepdims=True)
                ds_lo = (ds - dsb.astype(F32)).astype(BF16)
                dcq_ref[hh, i] += (lax.dot_general(ones, dsb, (NT, ((), ())), preferred_element_type=F32)
                                   + lax.dot_general(ones, ds_lo, (NT, ((), ())), preferred_element_type=F32))
            rows = pl.ds(pl.multiple_of(i * tq, tq), tq)
            dq_ref[rows, :] += dq_t
            dk_ref[...] += dk_t
            dv_ref[...] += dv_t

        @pl.when(i > j)
        def _():
            step(False)

        @pl.when(i == j)
        def _():
            step(True)

    qi = lambda hp, j, i: jnp.maximum(i, j)
    return _call(
        kern, name="fox_bwd", grid=(4, nq, nq),
        in_specs=[BS((2, tq, LANE), lambda hp, j, i: (hp, qi(hp, j, i), 0)),
                  BS((2, tq, LANE), lambda hp, j, i: (hp, j, 0)),
                  BS((tq, LANE), lambda hp, j, i: (j, V_COL + hp)),
                  BS((tq, LANE), lambda hp, j, i: (qi(hp, j, i), hp)),
                  BS((tq, LANE), lambda hp, j, i: (qi(hp, j, i), hp)),
                  BS((2, tq, LANE), lambda hp, j, i: (hp, qi(hp, j, i), 0))],
        out_specs=[BS((s, LANE), lambda hp, j, i: (0, hp)), BS((tq, LANE), lambda hp, j, i: (j, hp)),
                   BS((tq, LANE), lambda hp, j, i: (j, hp)), BS((2, 1, tq), lambda hp, j, i: (hp, 0, j)),
                   BS((2, nq, ROWSUM_M, tq), lambda hp, j, i: (hp, 0, 0, 0))],
        out_shape=[jax.ShapeDtypeStruct((s, FOX_W), F32), jax.ShapeDtypeStruct((s, FOX_W), F32),
                   jax.ShapeDtypeStruct((s, FOX_W), F32), jax.ShapeDtypeStruct((FOX_H, 1, s), F32),
                   jax.ShapeDtypeStruct((FOX_H, nq, ROWSUM_M, tq), F32)],
        compiler_params=_params(3))(qa, ka, proj, dcat, o, lse)


def _shift_down(v, d, row):
    return jnp.where(row >= d, pltpu.roll(v, d, 0), 0.0)


def _shift_up(v, d, row, n):
    return jnp.where(row < n - d, pltpu.roll(v, n - d, 0), 0.0)


def _sconv_fwd(proj, convw):
    s = proj.shape[0]

    def kern(b_ref, c_ref, u_ref, w_ref, y_ref):
        row = lax.broadcasted_iota(jnp.int32, (s, LANE), 0)
        z = c_ref[...] * u_ref[...]
        conv = w_ref[2:3, :] * z + w_ref[1:2, :] * _shift_down(z, 1, row) + w_ref[0:1, :] * _shift_down(z, 2, row)
        y_ref[...] = (b_ref[...] * conv).astype(BF16)

    col = lambda base: BS((s, LANE), lambda cb: (0, base + cb))
    return _call(kern, name="sconv_fwd", grid=(SC_W // LANE,),
                          in_specs=[col(B_COL), col(C_COL), col(U_COL), BS((SC_K, LANE), lambda cb: (0, cb))],
                          out_specs=BS((s, LANE), lambda cb: (0, cb)),
                          out_shape=jax.ShapeDtypeStruct((s, SC_W), BF16), compiler_params=_params(1))(proj, proj, proj, convw)


def _sconv_bwd(proj, convw, dcat):
    s = proj.shape[0]

    def kern(b_ref, c_ref, u_ref, w_ref, dy_ref, db_ref, dc_ref, du_ref, dw_ref):
        row = lax.broadcasted_iota(jnp.int32, (s, LANE), 0)
        cv, uv, dyv = c_ref[...], u_ref[...], dy_ref[...]
        z = cv * uv
        z1 = _shift_down(z, 1, row)
        z2 = _shift_down(z, 2, row)
        conv = w_ref[2:3, :] * z + w_ref[1:2, :] * z1 + w_ref[0:1, :] * z2
        db_ref[...] = dyv * conv
        dcv = dyv * b_ref[...]
        dz = w_ref[2:3, :] * dcv + w_ref[1:2, :] * _shift_up(dcv, 1, row, s) + w_ref[0:1, :] * _shift_up(dcv, 2, row, s)
        dc_ref[...] = dz * uv
        du_ref[...] = dz * cv
        dw_ref[0:1, :] = jnp.sum(dcv * z2, axis=0, keepdims=True)
        dw_ref[1:2, :] = jnp.sum(dcv * z1, axis=0, keepdims=True)
        dw_ref[2:3, :] = jnp.sum(dcv * z, axis=0, keepdims=True)

    col = lambda base: BS((s, LANE), lambda cb: (0, base + cb))
    out = BS((s, LANE), lambda cb: (0, cb))
    wspec = BS((SC_K, LANE), lambda cb: (0, cb))
    act = jax.ShapeDtypeStruct((s, SC_W), F32)
    return _call(kern, name="sconv_bwd", grid=(SC_W // LANE,),
                          in_specs=[col(B_COL), col(C_COL), col(U_COL), wspec, col(FOX_W // LANE)],
                          out_specs=[out, out, out, wspec],
                          out_shape=[act, act, act, jax.ShapeDtypeStruct((SC_K, SC_W), F32)],
                          compiler_params=_params(1))(proj, proj, proj, convw, dcat)


def _fox_layer_fwd(x, gpre, gpost, wall, bfb, convw, wout, after=None):
    s = x.shape[0]
    h = _norm_fwd(x, gpre, after)
    proj = _mm_nn("fox_proj", h, wall, tn=AB_PAD // 5)
    frow = proj[:, 3 * FOX_W + 3 * SC_W:3 * FOX_W + 3 * SC_W + FOX_H].T
    cumr = _fox_cum(frow, bfb)
    qa, ka = _fox_prep(proj, jnp.broadcast_to(cumr[:, :, None], (FOX_H, s, LANE)))
    o, lse = _fox_fwd_call(proj, qa, ka)
    yb = _sconv_fwd(proj, convw)
    cat = jnp.concatenate([o.astype(BF16), yb], axis=1)
    y = _mm_nn("fox_out", cat, wout)
    return _norm_res(x, y, gpost), (x, h, proj, frow, qa, ka, o, lse, cat, y)


def _fox_layer_bwd(dxo, saved, gpre, gpost, wall, bfb, convw, wout, after=None):
    x, h, proj, frow, qa, ka, o, lse, cat, y = saved
    s = x.shape[0]
    dy, dgpost = _norm_bwd(y, dxo, gpost, None, BF16, after)
    dcat = _mm_nt("fox_dcat", dy, wout)
    dwout = _mm_tn("fox_dwout", cat, dy)
    db, dc, du, dconvw = _sconv_bwd(proj, convw, dcat)
    dq, dk, dv, dck, dcq = _fox_bwd_call(proj, o, lse, dcat, qa, ka)
    dfrow, dbf = _fox_dlogf(dcq[:, :, 0, :].reshape(FOX_H, s), dck.reshape(FOX_H, s), frow, bfb)
    dfcol = jnp.pad(dfrow.T, ((0, 0), (0, LANE - FOX_H)))
    dproj = jnp.concatenate([dq, dk, dv, db, dc, du, dfcol], axis=1).astype(BF16)
    dwall = _mm_tn("fox_dwall", h, dproj, tn=AB_PAD // 5)
    dh = _mm_nt("fox_dh", dproj, wall, tn=AB_PAD // 5)
    dx, dgpre = _norm_bwd(x, dh, gpre, dxo, F32)
    return dx, dgpre, dgpost, dwall, dbf[:, 0], dconvw, dwout


def _ab_pack(w):
    nf = 3 * FOX_W
    return jnp.concatenate([w[:, :nf], w[:, nf + FOX_H:], w[:, nf:nf + FOX_H],
                            jnp.zeros((w.shape[0], AB_PAD - AB_IN), w.dtype)], axis=1)


def _ab_unpack(w):
    nf = 3 * FOX_W
    nbcu = 3 * SC_W
    return jnp.concatenate([w[:, :nf], w[:, nf + nbcu:nf + nbcu + FOX_H], w[:, nf:nf + nbcu]], axis=1)


NCH = DM // LANE
CH_PER_BLK = LRU_BW // LANE


def _chunk_spec(s, lead=0):
    return BS((None, s, LANE), lambda ch: (lead + ch // CH_PER_BLK, 0, ch % CH_PER_BLK))


def _vec_chunk(rows):
    return BS((rows, LANE), lambda ch: (0, ch))


def _neg_expm1(x):
    series = -x * (1.0 + x * (1 / 2) * (1.0 + x * (1 / 3) * (1.0 + x * (1 / 4) * (1.0 + x * (1 / 5) * (
        1.0 + x * (1 / 6) * (1.0 + x * (1 / 7)))))))
    return jnp.where(x > -0.25, series, 1.0 - jnp.exp(x))


def _softplus(z):
    return jnp.maximum(z, 0.0) + jnp.log1p(jnp.exp(-jnp.abs(z)))


GELU_C = math.sqrt(2.0 / math.pi)
GELU_A = 0.044715


def _gelu(x):
    return 0.5 * x * (1.0 + jnp.tanh(GELU_C * (x + GELU_A * x * x * x)))


def _gelu_grad(x):
    t = jnp.tanh(GELU_C * (x + GELU_A * x * x * x))
    return 0.5 * (1.0 + t) + 0.5 * x * (1.0 - t * t) * GELU_C * (1.0 + 3.0 * GELU_A * x * x)


def _lru_conv_fwd(gu, convw, convb):
    s = gu.shape[1]

    def kern(x_ref, w_ref, b_ref, u_ref):
        row = lax.broadcasted_iota(jnp.int32, (s, LANE), 0)
        xv = x_ref[...]
        u_ref[...] = (b_ref[...] + w_ref[3:4, :] * xv + w_ref[2:3, :] * _shift_down(xv, 1, row)
                      + w_ref[1:2, :] * _shift_down(xv, 2, row) + w_ref[0:1, :] * _shift_down(xv, 3, row))

    return _call(kern, name="lru_conv_fwd", grid=(NCH,),
                          in_specs=[_chunk_spec(s, LRU_NB), _vec_chunk(RG_K), _vec_chunk(1)], out_specs=_chunk_spec(s),
                          out_shape=jax.ShapeDtypeStruct((LRU_NB, s, LRU_BW), F32), compiler_params=_params(1))(gu, convw, convb)


def _lru_conv_bwd(dud, dug, gu, convw):
    s = gu.shape[1]

    def kern(d1_ref, d2_ref, x_ref, w_ref, dx_ref, dw_ref, db_ref):
        row = lax.broadcasted_iota(jnp.int32, (s, LANE), 0)
        du = d1_ref[...] + d2_ref[...]
        xv = x_ref[...]
        dx_ref[...] = (w_ref[3:4, :] * du + w_ref[2:3, :] * _shift_up(du, 1, row, s) + w_ref[1:2, :] * _shift_up(du, 2, row, s)
                       + w_ref[0:1, :] * _shift_up(du, 3, row, s)).astype(BF16)
        dw_ref[3:4, :] = jnp.sum(du * xv, axis=0, keepdims=True)
        for k in range(1, RG_K):
            dw_ref[3 - k:4 - k, :] = jnp.sum(du * _shift_down(xv, k, row), axis=0, keepdims=True)
        db_ref[...] = jnp.sum(du, axis=0, keepdims=True)

    return _call(kern, name="lru_conv_bwd", grid=(NCH,),
                          in_specs=[_chunk_spec(s), _chunk_spec(s), _chunk_spec(s, LRU_NB), _vec_chunk(RG_K)],
                          out_specs=[_chunk_spec(s), _vec_chunk(RG_K), _vec_chunk(1)],
                          out_shape=[jax.ShapeDtypeStruct((LRU_NB, s, LRU_BW), BF16),
                                     jax.ShapeDtypeStruct((RG_K, DM), F32), jax.ShapeDtypeStruct((1, DM), F32)],
                          compiler_params=_params(1))(dud, dug, gu, convw)


def _lru_gates(z_ref, bai_ref, lam_ref, uv):
    r = jax.nn.sigmoid(z_ref[0] + bai_ref[0:1, :])
    ig = jax.nn.sigmoid(z_ref[1] + bai_ref[1:2, :])
    sp = _softplus(-lam_ref[...])
    la = -RG_C * r * sp
    a = jnp.exp(la)
    sq = jnp.sqrt(_neg_expm1(2.0 * la))
    return r, ig, sp, a, sq


def _scan_steps(n):
    d, out = 1, []
    while d < n:
        out.append(d)
        d *= 2
    return out


def _lru_scan_fwd(z, bai, lam, u, gu):
    s = u.shape[1]
    zspec = BS((2, None, s, LANE), lambda ch: (0, ch // CH_PER_BLK, 0, ch % CH_PER_BLK))

    def kern(z_ref, bai_ref, lam_ref, u_ref, g_ref, hs_ref, y_ref):
        row = lax.broadcasted_iota(jnp.int32, (s, LANE), 0)
        uv = u_ref[...]
        _, ig, _, a, sq = _lru_gates(z_ref, bai_ref, lam_ref, uv)
        b = sq * (ig * uv)
        for d in _scan_steps(s):
            a_sh = jnp.where(row >= d, pltpu.roll(a, d, 0), 1.0)
            b = a * _shift_down(b, d, row) + b
            a = a * a_sh
        hs_ref[...] = b
        y_ref[...] = (_gelu(g_ref[...]) * b).astype(BF16)

    return _call(kern, name="lru_scan_fwd", grid=(NCH,),
                          in_specs=[zspec, _vec_chunk(2), _vec_chunk(1), _chunk_spec(s), _chunk_spec(s)],
                          out_specs=[_chunk_spec(s), BS((s, LANE), lambda ch: (0, ch))],
                          out_shape=[jax.ShapeDtypeStruct((LRU_NB, s, LRU_BW), F32), jax.ShapeDtypeStruct((s, DM), BF16)],
                          compiler_params=_params(1))(z, bai, lam, u, gu)


def _lru_scan_bwd(dyp, z, bai, lam, u, gu, hs):
    s = u.shape[1]
    zspec = BS((2, None, s, LANE), lambda ch: (0, ch // CH_PER_BLK, 0, ch % CH_PER_BLK))

    def kern(dy_ref, z_ref, bai_ref, lam_ref, u_ref, g_ref, hs_ref, dg_ref, dz_ref, du_ref, dbai_ref, dlam_ref):
        row = lax.broadcasted_iota(jnp.int32, (s, LANE), 0)
        uv, gv, hv, dyv = u_ref[...], g_ref[...], hs_ref[...], dy_ref[...]
        r, ig, sp, a, sq = _lru_gates(z_ref, bai_ref, lam_ref, uv)
        dg_ref[...] = (dyv * hv * _gelu_grad(gv)).astype(BF16)
        g = dyv * _gelu(gv)
        an = _shift_up(a, 1, row, s)
        for d in _scan_steps(s):
            an_sh = jnp.where(row < s - d, pltpu.roll(an, s - d, 0), 1.0)
            g = an * _shift_up(g, d, row, s) + g
            an = an * an_sh
        da = g * _shift_down(hv, 1, row)
        dsq = g * (ig * uv)
        di = g * sq * uv
        du_ref[...] = g * sq * ig
        dla = da * a - dsq * (a * a / sq)
        dzr = dla * (-RG_C * sp) * r * (1.0 - r)
        dzi = di * ig * (1.0 - ig)
        dz_ref[0] = dzr.astype(BF16)
        dz_ref[1] = dzi.astype(BF16)
        dbai_ref[0:1, :] = jnp.sum(dzr, axis=0, keepdims=True)
        dbai_ref[1:2, :] = jnp.sum(dzi, axis=0, keepdims=True)
        dlam_ref[...] = jnp.sum(dla * r, axis=0, keepdims=True) * (RG_C * jax.nn.sigmoid(-lam_ref[...]))

    return _call(
        kern, name="lru_scan_bwd", grid=(NCH,),
        in_specs=[BS((s, LANE), lambda ch: (0, ch)), zspec, _vec_chunk(2), _vec_chunk(1), _chunk_spec(s), _chunk_spec(s),
                  _chunk_spec(s)],
        out_specs=[_chunk_spec(s), zspec, _chunk_spec(s), _vec_chunk(2), _vec_chunk(1)],
        out_shape=[jax.ShapeDtypeStruct((LRU_NB, s, LRU_BW), BF16), jax.ShapeDtypeStruct((2, LRU_NB, s, LRU_BW), BF16),
                   jax.ShapeDtypeStruct((LRU_NB, s, LRU_BW), F32), jax.ShapeDtypeStruct((2, DM), F32),
                   jax.ShapeDtypeStruct((1, DM), F32)],
        compiler_params=_params(1))(dyp, z, bai, lam, u, gu, hs)


def _lru_layer_fwd(x, gpre, gpost, win, convw, convb, wai, bai, lam, wout, after=None):
    s = x.shape[0]
    tm = _tile(s, 512)
    h = _norm_fwd(x, gpre, after)
    gu = _bmm_nn("lru_in", h, win)
    u = _lru_conv_fwd(gu, convw, convb)
    z = _mm("lru_gate", u, wai, grid=(2, LRU_NB, s // tm, 1),
            a_spec=BS((None, tm, LRU_BW), lambda k, n, i, r: (n, i, 0)),
            b_spec=BS((None, None, LRU_BW, LRU_BW), lambda k, n, i, r: (k, n, 0, 0)),
            o_spec=BS((None, None, tm, LRU_BW), lambda k, n, i, r: (k, n, i, 0)),
            out_shape=(2, LRU_NB, s, LRU_BW), dn=NN)
    hs, yp = _lru_scan_fwd(z, bai, lam, u, gu)
    y = _mm_nn("lru_out", yp, wout)
    return _norm_res(x, y, gpost), (x, h, gu, u, z, hs, yp, y)


def _lru_layer_bwd(dxo, saved, gpre, gpost, win, convw, convb, wai, bai, lam, wout, after=None):
    x, h, gu, u, z, hs, yp, y = saved
    s = x.shape[0]
    tm = _tile(s, 512)
    dy, dgpost = _norm_bwd(y, dxo, gpost, None, BF16, after)
    dyp = _mm_nt("lru_dyp", dy, wout)
    dwout = _mm_tn("lru_dwout", yp, dy)
    dgate, dz, dud, dbai, dlam = _lru_scan_bwd(dyp, z, bai, lam, u, gu, hs)
    dwai = _mm("lru_dwai", u, dz, grid=(2, LRU_NB, s // tm),
               a_spec=BS((None, tm, LRU_BW), lambda k, n, r: (n, r, 0)),
               b_spec=BS((None, None, tm, LRU_BW), lambda k, n, r: (k, n, r, 0)),
               o_spec=BS((None, None, LRU_BW, LRU_BW), lambda k, n, r: (k, n, 0, 0)),
               out_shape=(2, LRU_NB, LRU_BW, LRU_BW), dn=TN)
    dug = _mm("lru_dug", dz, wai, grid=(LRU_NB, s // tm, 2),
              a_spec=BS((None, None, tm, LRU_BW), lambda n, i, k: (k, n, i, 0)),
              b_spec=BS((None, None, LRU_BW, LRU_BW), lambda n, i, k: (k, n, 0, 0)),
              o_spec=BS((None, tm, LRU_BW), lambda n, i, k: (n, i, 0)),
              out_shape=(LRU_NB, s, LRU_BW), dn=NT)
    duraw, dconvw, dconvb = _lru_conv_bwd(dud, dug, gu, convw)
    dgu = jnp.concatenate([dgate, duraw], axis=0)
    dwin = _bmm_tn("lru_dwin", h, dgu)
    dh = _bmm_nt_sum("lru_dh", dgu, win)
    dx, dgpre = _norm_bwd(x, dh, gpre, dxo, F32)
    return dx, dgpre, dgpost, dwin, dconvw, dconvb, dwai, dbai, dlam, dwout


CHIP_FLIPS = ((1, 0), (0, 1), (1, 1))


def _place():
    return lax.axis_index("x"), lax.axis_index("y"), lax.axis_index("c")


def _flip(v, f):
    return 1 - v if f else v


def _comm_params():
    return pltpu.CompilerParams(vmem_limit_bytes=VMEM_LIMIT)


def _all_gather(shards):
    n = len(shards)

    def body(*refs):
        ins, outs, stage = refs[:n], refs[n:2 * n], refs[2 * n:3 * n]
        send_sems, recv_sems, local_sems = refs[3 * n:]
        x, y, c = _place()
        me, sibling = (x, y, c), (x, y, 1 - c)
        chips = [(_flip(x, fx), _flip(y, fy)) for fx, fy in CHIP_FLIPS]

        def slot(t, p):
            return outs[t].at[:, 4 * p[0] + 2 * p[1] + p[2]]

        def copy(t, k, block, to, src=None):
            return pltpu.make_async_remote_copy(
                src_ref=slot(t, block) if src is None else src, dst_ref=slot(t, block),
                send_sem=send_sems.at[7 * t + k], recv_sem=recv_sems.at[7 * t + k], device_id=to, device_id_type=MESH)

        first = []
        for t in range(n):
            first.append(copy(t, 0, me, sibling, src=ins[t]))
            first += [copy(t, 1 + j, me, (*chip, c), src=ins[t]) for j, chip in enumerate(chips)]
        for cp in first:
            cp.start()
        load = [pltpu.make_async_copy(ins[t], stage[t], local_sems.at[t]) for t in range(n)]
        mine = [pltpu.make_async_copy(stage[t], slot(t, me), local_sems.at[t]) for t in range(n)]
        for cp in load:
            cp.start()
        for t in range(n):
            load[t].wait()
            mine[t].start()
        passed = []
        for j, chip in enumerate(chips):
            for t in range(n):
                copy(t, 1 + j, (*chip, c), me).wait_recv()
                fwd = copy(t, 4 + j, (*chip, c), sibling)
                fwd.start()
                passed.append(fwd)
        for t in range(n):
            copy(t, 0, sibling, me).wait_recv()
            for j, chip in enumerate(chips):
                copy(t, 4 + j, (*chip, 1 - c), me).wait_recv()
        for cp in first + passed:
            cp.wait_send()
        for cp in mine:
            cp.wait()

    outs = [jax.ShapeDtypeStruct((s.shape[0], NDEV) + s.shape[1:], s.dtype) for s in shards]
    return pl.pallas_call(body, name="all_gather", in_specs=[ANY] * n, out_specs=[ANY] * n, out_shape=outs,
                          scratch_shapes=[pltpu.VMEM(s.shape, s.dtype) for s in shards]
                          + [pltpu.SemaphoreType.DMA((7 * n,)), pltpu.SemaphoreType.DMA((7 * n,)),
                             pltpu.SemaphoreType.DMA((n,))],
                          compiler_params=_comm_params())(*shards)


def _small_gather(v):
    def body(v_ref, o_ref, send_sems, recv_sems, local_sem):
        x, y, c = _place()
        mine = 4 * x + 2 * y + c
        local = pltpu.make_async_copy(v_ref, o_ref.at[mine], local_sem)
        local.start()
        sends = []
        for k in range(1, NDEV):
            fx, fy, fc = (k >> 2) & 1, (k >> 1) & 1, k & 1
            sends.append(pltpu.make_async_remote_copy(
                src_ref=v_ref, dst_ref=o_ref.at[mine], send_sem=send_sems.at[k - 1], recv_sem=recv_sems.at[k - 1],
                device_id=(_flip(x, fx), _flip(y, fy), _flip(c, fc)), device_id_type=MESH))
        for cp in sends:
            cp.start()
        for k in range(1, NDEV):
            fx, fy, fc = (k >> 2) & 1, (k >> 1) & 1, k & 1
            src = 4 * _flip(x, fx) + 2 * _flip(y, fy) + _flip(c, fc)
            pltpu.make_async_remote_copy(src_ref=v_ref, dst_ref=o_ref.at[src], send_sem=send_sems.at[k - 1],
                                         recv_sem=recv_sems.at[k - 1], device_id=(x, y, c), device_id_type=MESH).wait_recv()
        for cp in sends:
            cp.wait_send()
        local.wait()

    return pl.pallas_call(body, name="small_gather", in_specs=[ANY], out_specs=ANY,
                          out_shape=jax.ShapeDtypeStruct((NDEV,) + v.shape, v.dtype),
                          scratch_shapes=[pltpu.SemaphoreType.DMA((NDEV - 1,)), pltpu.SemaphoreType.DMA((NDEV - 1,)),
                                          pltpu.SemaphoreType.DMA],
                          compiler_params=_comm_params())(v)


REL_CHIPS = ((0, 0),) + CHIP_FLIPS


def _rs_d2d(g5s, after=None):
    n = len(g5s)
    extra = () if after is None else (after,)

    def body(*refs):
        ins, gots = refs[:n], refs[n + len(extra):2 * n + len(extra)]
        send_sems, recv_sems = refs[2 * n + len(extra):]
        x, y, c = _place()
        copies = []
        for t in range(n):
            for f, (fx, fy) in enumerate(REL_CHIPS):
                copies.append(pltpu.make_async_remote_copy(
                    src_ref=ins[t].at[_flip(x, fx), _flip(y, fy), 1 - c], dst_ref=gots[t].at[f],
                    send_sem=send_sems.at[4 * t + f], recv_sem=recv_sems.at[4 * t + f], device_id=(x, y, 1 - c),
                    device_id_type=MESH))
        for cp in copies:
            cp.start()
        for cp in copies:
            cp.wait()

    out = [jax.ShapeDtypeStruct((4,) + g.shape[3:], F32) for g in g5s]
    return pl.pallas_call(body, name="rs_d2d", in_specs=[ANY] * (n + len(extra)), out_specs=[ANY] * n, out_shape=out,
                          scratch_shapes=[pltpu.SemaphoreType.DMA((4 * n,)), pltpu.SemaphoreType.DMA((4 * n,))],
                          compiler_params=_comm_params())(*g5s, *extra)


def _rs_ici(parts):
    n = len(parts)

    def body(*refs):
        ins, outs = refs[:n], refs[n:2 * n]
        send_sems, recv_sems = refs[2 * n:]
        x, y, c = _place()
        copies = []
        for t in range(n):
            for f, (fx, fy) in enumerate(CHIP_FLIPS):
                copies.append(pltpu.make_async_remote_copy(
                    src_ref=ins[t].at[f], dst_ref=outs[t].at[f], send_sem=send_sems.at[3 * t + f],
                    recv_sem=recv_sems.at[3 * t + f], device_id=(_flip(x, fx), _flip(y, fy), c), device_id_type=MESH))
        for cp in copies:
            cp.start()
        for cp in copies:
            cp.wait()

    out = [jax.ShapeDtypeStruct(p.shape, p.dtype) for p in parts]
    return pl.pallas_call(body, name="rs_ici", in_specs=[ANY] * n, out_specs=[ANY] * n, out_shape=out,
                          scratch_shapes=[pltpu.SemaphoreType.DMA((3 * n,)), pltpu.SemaphoreType.DMA((3 * n,))],
                          compiler_params=_comm_params())(*parts)


HBM = pl.BlockSpec(memory_space=pltpu.HBM)
SEM = pl.BlockSpec(memory_space=pltpu.SEMAPHORE)
EFFECT = pltpu.SideEffectType.DATAFLOW_SIDE_EFFECTING


def _in_hbm(a):
    return pltpu.with_memory_space_constraint(a, pltpu.HBM)


def _rs_ici_copies(ins, lands, send_sems, recv_sems):
    x, y, c = _place()
    return [pltpu.make_async_remote_copy(
        src_ref=ins[t].at[f], dst_ref=lands[t].at[f], send_sem=send_sems.at[3 * t + f], recv_sem=recv_sems.at[3 * t + f],
        device_id=(_flip(x, fx), _flip(y, fy), c), device_id_type=MESH)
        for t in range(len(ins)) for f, (fx, fy) in enumerate(CHIP_FLIPS)]


def _rs_ici_start(parts, name):
    n = len(parts)

    def body(*refs):
        ins, lands = refs[:n], refs[n:2 * n]
        send_sems, recv_sems = refs[2 * n], refs[2 * n + 1]
        token = refs[-1]
        for cp in _rs_ici_copies(ins, lands, send_sems, recv_sems):
            cp.start()
        token[...] = jnp.zeros(token.shape, token.dtype)

    thru = [pltpu.HBM(p.shape, p.dtype) for p in parts]
    res = pl.pallas_call(
        body, name=name, in_specs=[HBM] * (2 * n),
        out_shape=(pltpu.SemaphoreType.DMA((3 * n,)), pltpu.SemaphoreType.DMA((3 * n,)), *thru, *thru,
                   jax.ShapeDtypeStruct((8, LANE), F32)),
        out_specs=(SEM, SEM, *([HBM] * (2 * n)), pl.BlockSpec(memory_space=pltpu.VMEM)),
        input_output_aliases={i: 2 + i for i in range(2 * n)},
        compiler_params=pltpu.CompilerParams(has_side_effects=EFFECT),
    )(*[_in_hbm(p) for p in parts], *[_in_hbm(lax.empty(p.shape, p.dtype)) for p in parts])
    return res[:-1], res[-1]


def _rs_ici_wait(state, after, name):
    n = (len(state) - 2) // 2

    def body(*refs):
        send_sems, recv_sems = refs[0], refs[1]
        ins, lands = refs[2:2 + n], refs[2 + n:2 + 2 * n]
        for cp in _rs_ici_copies(ins, lands, send_sems, recv_sems):
            cp.wait_send()
            cp.wait_recv()

    thru = [pltpu.HBM(s.shape, s.dtype) for s in state[2:]]
    res = pl.pallas_call(
        body, name=name, in_specs=[SEM, SEM] + [HBM] * (2 * n) + [ANY], out_shape=tuple(thru),
        out_specs=tuple([HBM] * (2 * n)), input_output_aliases={2 + i: i for i in range(2 * n)},
        compiler_params=pltpu.CompilerParams(has_side_effects=EFFECT),
    )(*state, after)
    return list(res[n:])


def _ag_copies(shards, lands, send_sems, recv_sems):
    x, y, c = _place()
    mine = 4 * x + 2 * y + c
    peers = [(x, y, 1 - c)] + [(_flip(x, fx), _flip(y, fy), c) for fx, fy in CHIP_FLIPS]
    return [pltpu.make_async_remote_copy(
        src_ref=shards[t], dst_ref=lands[t].at[mine], send_sem=send_sems.at[4 * t + k], recv_sem=recv_sems.at[4 * t + k],
        device_id=peer, device_id_type=MESH) for t in range(len(shards)) for k, peer in enumerate(peers)]


def _ag_start(shards, after, name):
    n = len(shards)

    def body(*refs):
        ins, lands = refs[:n], refs[n:2 * n]
        send_sems, recv_sems = refs[2 * n + 1], refs[2 * n + 2]
        token = refs[-1]
        for cp in _ag_copies(ins, lands, send_sems, recv_sems):
            cp.start()
        token[...] = jnp.zeros(token.shape, token.dtype)

    thru = [pltpu.HBM(s.shape, s.dtype) for s in shards]
    land = [pltpu.HBM((NDEV,) + s.shape, s.dtype) for s in shards]
    res = pl.pallas_call(
        body, name=name, in_specs=[HBM] * (2 * n) + [ANY],
        out_shape=(pltpu.SemaphoreType.DMA((4 * n,)), pltpu.SemaphoreType.DMA((4 * n,)), *thru, *land,
                   jax.ShapeDtypeStruct((8, LANE), F32)),
        out_specs=(SEM, SEM, *([HBM] * (2 * n)), pl.BlockSpec(memory_space=pltpu.VMEM)),
        input_output_aliases={i: 2 + i for i in range(2 * n)},
        compiler_params=pltpu.CompilerParams(has_side_effects=EFFECT),
    )(*[_in_hbm(s) for s in shards], *[_in_hbm(lax.empty((NDEV,) + s.shape, s.dtype)) for s in shards], after)
    return res[:-1], res[-1]


def _ag_wait(state, after, name):
    n = (len(state) - 2) // 2

    def body(*refs):
        send_sems, recv_sems = refs[0], refs[1]
        ins, lands = refs[2:2 + n], refs[2 + n:2 + 2 * n]
        for cp in _ag_copies(ins, lands, send_sems, recv_sems):
            cp.wait_send()
            cp.wait_recv()

    thru = [pltpu.HBM(s.shape, s.dtype) for s in state[2:]]
    res = pl.pallas_call(
        body, name=name, in_specs=[SEM, SEM] + [HBM] * (2 * n) + [ANY], out_shape=tuple(thru),
        out_specs=tuple([HBM] * (2 * n)), input_output_aliases={2 + i: i for i in range(2 * n)},
        compiler_params=pltpu.CompilerParams(has_side_effects=EFFECT),
    )(*state, after)
    return list(res[:n]), list(res[n:])


def _ag_finish(shards, lands):
    n = len(shards)

    def body(*refs):
        ins, outs, stage = refs[:n], refs[2 * n:3 * n], refs[3 * n:4 * n]
        send_sems, recv_sems, local_sems = refs[4 * n:]
        x, y, c = _place()
        chips = [(_flip(x, fx), _flip(y, fy)) for fx, fy in CHIP_FLIPS]

        def passing(t, j, core, to):
            blk = outs[t].at[4 * chips[j][0] + 2 * chips[j][1] + core]
            return pltpu.make_async_remote_copy(src_ref=blk, dst_ref=blk, send_sem=send_sems.at[3 * t + j],
                                                recv_sem=recv_sems.at[3 * t + j], device_id=to, device_id_type=MESH)

        sends = [passing(t, j, c, (x, y, 1 - c)) for t in range(n) for j in range(3)]
        for cp in sends:
            cp.start()
        load = [pltpu.make_async_copy(ins[t], stage[t], local_sems.at[t]) for t in range(n)]
        mine = [pltpu.make_async_copy(stage[t], outs[t].at[4 * x + 2 * y + c], local_sems.at[t]) for t in range(n)]
        for cp in load:
            cp.start()
        for t in range(n):
            load[t].wait()
            mine[t].start()
        for t in range(n):
            for j in range(3):
                passing(t, j, 1 - c, (x, y, c)).wait_recv()
        for cp in sends:
            cp.wait_send()
        for cp in mine:
            cp.wait()

    return pl.pallas_call(
        body, name="ag_finish", in_specs=[ANY] * (2 * n), out_specs=[ANY] * n,
        out_shape=[jax.ShapeDtypeStruct(l.shape, l.dtype) for l in lands],
        input_output_aliases={n + i: i for i in range(n)},
        scratch_shapes=[pltpu.VMEM(s.shape, s.dtype) for s in shards]
        + [pltpu.SemaphoreType.DMA((3 * n,)), pltpu.SemaphoreType.DMA((3 * n,)), pltpu.SemaphoreType.DMA((n,))],
        compiler_params=_comm_params())(*shards, *lands)


def _row_tile(rows, largest=256):
    for t in (1024, 512, 256, 128, 64, 32, 16, 8):
        if t > largest:
            continue
        if rows % t == 0:
            return t
    return rows


def _rs_chip_sum(pos, g5, got):
    a, b = g5.shape[3:]
    ta = _row_tile(a, 1024)

    def kern(pos_ref, o_ref, g_ref, p_ref):
        p_ref[...] = (o_ref[...] + g_ref[...]).astype(BF16)

    def mine(f, i, pos_ref):
        return (pos_ref[0] ^ ((f + 1) & 1), pos_ref[1] ^ ((f + 1) >> 1), pos_ref[2], i, 0)

    spec = pltpu.PrefetchScalarGridSpec(
        num_scalar_prefetch=1, grid=(3, a // ta),
        in_specs=[BS((None, None, None, ta, b), mine), BS((None, ta, b), lambda f, i, pos_ref: (f + 1, i, 0))],
        out_specs=BS((None, ta, b), lambda f, i, pos_ref: (f, i, 0)))
    return _call(kern, name="rs_chip_sum", grid_spec=spec, out_shape=jax.ShapeDtypeStruct((3, a, b), BF16),
                          compiler_params=_params(2))(pos, g5, got)


def _rs_final_sum(pos, g5, got, recv):
    a, b = g5.shape[3:]
    ta = _row_tile(a, 1024)

    def kern(pos_ref, o_ref, g_ref, r_ref, s_ref):
        acc = o_ref[...] + g_ref[...]
        for f in range(3):
            acc = acc + r_ref[f].astype(F32)
        s_ref[...] = acc

    spec = pltpu.PrefetchScalarGridSpec(
        num_scalar_prefetch=1, grid=(a // ta,),
        in_specs=[BS((None, None, None, ta, b), lambda i, pos_ref: (pos_ref[0], pos_ref[1], pos_ref[2], i, 0)),
                  BS((None, ta, b), lambda i, pos_ref: (0, i, 0)), BS((3, ta, b), lambda i, pos_ref: (0, i, 0))],
        out_specs=BS((ta, b), lambda i, pos_ref: (i, 0)))
    return _call(kern, name="rs_final_sum", grid_spec=spec, out_shape=jax.ShapeDtypeStruct((a, b), F32),
                          compiler_params=_params(1))(pos, g5, got, recv)


def _reduce_scatter(grads, pos):
    g5s = [g.reshape((2, 2, 2) + g.shape[1:]) for g in grads]
    gots = _rs_d2d(g5s)
    parts = [_rs_chip_sum(pos, g, got) for g, got in zip(g5s, gots)]
    recvs = _rs_ici(parts)
    return [_rs_final_sum(pos, g, got, r) for g, got, r in zip(g5s, gots, recvs)]


def _rs_d2d_copies(ins, lands, send_sems, recv_sems):
    x, y, c = _place()
    return [pltpu.make_async_remote_copy(
        src_ref=ins[t].at[_flip(x, fx), _flip(y, fy), 1 - c], dst_ref=lands[t].at[f], send_sem=send_sems.at[4 * t + f],
        recv_sem=recv_sems.at[4 * t + f], device_id=(x, y, 1 - c), device_id_type=MESH)
        for t in range(len(ins)) for f, (fx, fy) in enumerate(REL_CHIPS)]


def _rs_d2d_start(g5s, name):
    n = len(g5s)

    def body(*refs):
        ins, lands = refs[:n], refs[n:2 * n]
        for cp in _rs_d2d_copies(ins, lands, refs[2 * n], refs[2 * n + 1]):
            cp.start()
        refs[-1][...] = jnp.zeros(refs[-1].shape, F32)

    thru = [pltpu.HBM(g.shape, g.dtype) for g in g5s]
    land = [pltpu.HBM((4,) + g.shape[3:], F32) for g in g5s]
    res = pl.pallas_call(
        body, name=name, in_specs=[HBM] * (2 * n),
        out_shape=(pltpu.SemaphoreType.DMA((4 * n,)), pltpu.SemaphoreType.DMA((4 * n,)), *thru, *land,
                   jax.ShapeDtypeStruct((8, LANE), F32)),
        out_specs=(SEM, SEM, *([HBM] * (2 * n)), pl.BlockSpec(memory_space=pltpu.VMEM)),
        input_output_aliases={i: 2 + i for i in range(2 * n)},
        compiler_params=pltpu.CompilerParams(has_side_effects=EFFECT),
    )(*[_in_hbm(g) for g in g5s], *[_in_hbm(lax.empty((4,) + g.shape[3:], F32)) for g in g5s])
    return res[:-1], res[-1]


def _rs_d2d_wait(state, after, name):
    n = (len(state) - 2) // 2

    def body(*refs):
        ins, lands = refs[2:2 + n], refs[2 + n:2 + 2 * n]
        for cp in _rs_d2d_copies(ins, lands, refs[0], refs[1]):
            cp.wait_send()
            cp.wait_recv()

    thru = [pltpu.HBM(s.shape, s.dtype) for s in state[2:]]
    res = pl.pallas_call(
        body, name=name, in_specs=[SEM, SEM] + [HBM] * (2 * n) + [ANY], out_shape=tuple(thru),
        out_specs=tuple([HBM] * (2 * n)), input_output_aliases={2 + i: i for i in range(2 * n)},
        compiler_params=pltpu.CompilerParams(has_side_effects=EFFECT),
    )(*state, after)
    return list(res[:n]), list(res[n:])


def _as_g5(grads):
    return [g.reshape((2, 2, 2) + g.shape[1:]) for g in grads]


def _rs_mid(g5s, gots, pos, tag):
    parts = [_rs_chip_sum(pos, g, got) for g, got in zip(g5s, gots)]
    state, token = _rs_ici_start(parts, "rs_ici_start_" + tag)
    return (g5s, gots, state, tag), token


def _rs_begin(grads, pos, tag, after=None):
    g5s = _as_g5(grads)
    return _rs_mid(g5s, _rs_d2d(g5s, after), pos, tag)


def _rs_end(pending, after, pos):
    g5s, gots, state, tag = pending
    recvs = _rs_ici_wait(state, after, "rs_ici_wait_" + tag)
    return [_rs_final_sum(pos, g, got, r) for g, got, r in zip(g5s, gots, recvs)]


def _sum_devices(v):
    _, r, _ = v.shape

    def kern(v_ref, o_ref):
        acc = v_ref[0]
        for d in range(1, NDEV):
            acc = acc + v_ref[d]
        o_ref[...] = acc

    return _call(kern, name="sum_devices", out_shape=jax.ShapeDtypeStruct((r, LANE), F32),
                          compiler_params=_comm_params())(v)


def _loss_head(xf, target):
    s = xf.shape[0]
    tm = _tile(s, 512)

    def kern(x_ref, t_ref, dx_ref, l_ref):
        err = x_ref[...] - t_ref[...]
        dx_ref[...] = err * (1.0 / DM)
        part = jnp.broadcast_to(0.5 * jnp.sum(jnp.mean(err * err, axis=-1, keepdims=True), axis=0, keepdims=True), (8, LANE))

        @pl.when(pl.program_id(0) == 0)
        def _():
            l_ref[...] = part

        @pl.when(pl.program_id(0) > 0)
        def _():
            l_ref[...] += part

    row = BS((tm, DM), lambda i: (i, 0))
    return _call(kern, name="loss_head", grid=(s // tm,), in_specs=[row, row],
                          out_specs=[row, BS((8, LANE), lambda i: (0, 0))],
                          out_shape=[jax.ShapeDtypeStruct((s, DM), F32), jax.ShapeDtypeStruct((8, LANE), F32)],
                          compiler_params=_params(1))(xf, target)


def _adamw(w, g, m, v, after=None):
    rows, cols = w.shape
    tr = _row_tile(rows)
    extra = () if after is None else (after,)

    def kern(w_ref, g_ref, m_ref, v_ref, *rest):
        d_ref, nm_ref, nv_ref = rest[-3:]
        gv = g_ref[...]
        nm = ADAM_B1 * m_ref[...] + (1.0 - ADAM_B1) * gv
        nv = ADAM_B2 * v_ref[...] + (1.0 - ADAM_B2) * (gv * gv)
        m_hat = nm / (1.0 - ADAM_B1 ** ADAM_STEP)
        v_hat = nv / (1.0 - ADAM_B2 ** ADAM_STEP)
        d_ref[...] = -ADAM_LR * (m_hat / (jnp.sqrt(v_hat) + ADAM_EPS) + ADAM_WD * w_ref[...])
        nm_ref[...] = nm
        nv_ref[...] = nv

    blk = BS((tr, cols), lambda i: (i, 0))
    shp = jax.ShapeDtypeStruct((rows, cols), F32)
    return _call(kern, name="adamw", grid=(rows // tr,), in_specs=[blk] * 4 + [ANY] * len(extra),
                          out_specs=[blk] * 3, out_shape=[shp] * 3, compiler_params=_params(1))(w, g, m, v, *extra)


def _adamw_nd(w, g, m, v, after=None):
    shape = w.shape
    two = (math.prod(shape[:-1]), shape[-1])
    return tuple(o.reshape(shape)
                 for o in _adamw(w.reshape(two), g.reshape(two), m.reshape(two), v.reshape(two), after))


def _pack_small(parts):
    flat = jnp.concatenate([p.reshape(-1) for p in parts])
    pad = (-flat.shape[0]) % (8 * LANE)
    return jnp.pad(flat, (0, pad)).reshape(-1, LANE)


def _unpack_small(packed, shapes, lead=()):
    flat = packed.reshape(lead + (-1,))
    out, off = [], 0
    for shp in shapes:
        n = math.prod(shp)
        out.append(flat[..., off:off + n].reshape(lead + tuple(shp)))
        off += n
    return out


def _blocks_of_columns(w):
    k, n = w.shape
    return w.reshape(k, NDEV, n // NDEV).transpose(1, 0, 2)


def _columns_of_blocks(wb):
    n, k, c = wb.shape
    return wb.transpose(1, 0, 2).reshape(k, n * c)


WEIGHT_NAMES = ('g_mix_pre', 'g_mix_post', 'g_cross_pre', 'g_mem', 'g_cross_post', 'g_ffn_pre', 'g_ffn_post', 'w_xq',
                'w_xkv', 'w_xo', 'w_ffn_gu', 'w_ffn_down', 'ab_w_in', 'ab_b_f', 'ab_conv_w', 'ab_w_out', 'c_w_in',
                'c_conv_w', 'c_conv_b', 'c_w_a', 'c_b_a', 'c_w_i', 'c_b_i', 'c_lam', 'c_w_out')
BIG = ('w_xq', 'w_xkv', 'w_xo', 'w_ffn_gu', 'w_ffn_down', 'ab_w_in', 'ab_w_out', 'c_w_in', 'c_w_a', 'c_w_i', 'c_w_out')
SMALL_SHARDED = ('ab_conv_w', 'c_conv_w', 'c_conv_b', 'c_b_a', 'c_b_i', 'c_lam')
REPLICATED = ('g_mix_pre', 'g_mix_post', 'g_cross_pre', 'g_mem', 'g_cross_post', 'g_ffn_pre', 'g_ffn_post', 'ab_b_f')


def _small_full(name, gathered):
    nd = gathered.ndim
    return jnp.moveaxis(gathered, 0, nd - 2).reshape(gathered.shape[1:-1] + (NDEV * gathered.shape[-1],))


def _small_shard(full, dev):
    c = full.shape[-1] // NDEV
    return lax.dynamic_slice_in_dim(full, dev * c, c, axis=full.ndim - 1)


def kernel(x, mem, g_mix_pre, g_mix_post, g_cross_pre, g_mem, g_cross_post, g_ffn_pre, g_ffn_post, w_xq, w_xkv, w_xo, w_ffn_gu, w_ffn_down, ab_w_in, ab_b_f, ab_conv_w, ab_w_out, c_w_in, c_conv_w, c_conv_b, c_w_a, c_b_a, c_w_i, c_b_i, c_lam, c_w_out, loss_target, m_g_mix_pre, m_g_mix_post, m_g_cross_pre, m_g_mem, m_g_cross_post, m_g_ffn_pre, m_g_ffn_post, m_w_xq, m_w_xkv, m_w_xo, m_w_ffn_gu, m_w_ffn_down, m_ab_w_in, m_ab_b_f, m_ab_conv_w, m_ab_w_out, m_c_w_in, m_c_conv_w, m_c_conv_b, m_c_w_a, m_c_b_a, m_c_w_i, m_c_b_i, m_c_lam, m_c_w_out, v_g_mix_pre, v_g_mix_post, v_g_cross_pre, v_g_mem, v_g_cross_post, v_g_ffn_pre, v_g_ffn_post, v_w_xq, v_w_xkv, v_w_xo, v_w_ffn_gu, v_w_ffn_down, v_ab_w_in, v_ab_b_f, v_ab_conv_w, v_ab_w_out, v_c_w_in, v_c_conv_w, v_c_conv_b, v_c_w_a, v_c_b_a, v_c_w_i, v_c_b_i, v_c_lam, v_c_w_out):
    args = locals()
    w = {n: args[n] for n in WEIGHT_NAMES}
    mom = {n: args["m_" + n] for n in WEIGHT_NAMES}
    var = {n: args["v_" + n] for n in WEIGHT_NAMES}
    pos = jnp.stack([lax.axis_index("x"), lax.axis_index("y"), lax.axis_index("c")]).astype(jnp.int32)
    dev = 4 * pos[0] + 2 * pos[1] + pos[2]
    xs, mems, target = x[0], mem[0], loss_target[0]
    n_even, n_odd = (DEPTH + 1) // 2, DEPTH // 2

    small_shapes = [w[n].shape for n in SMALL_SHARDED]
    small_w_all = _small_gather(_pack_small([w[n] for n in SMALL_SHARDED]))
    gathered_small = _unpack_small(small_w_all, small_shapes, (NDEV,))
    small = {n: _small_full(n, g) for n, g in zip(SMALL_SHARDED, gathered_small)}
    ab_bfb = jnp.broadcast_to(ab_b_f[:, :, None], (n_even, FOX_H, LANE))
    c_bai = jnp.stack([small['c_b_a'].reshape(n_odd, DM), small['c_b_i'].reshape(n_odd, DM)], axis=1)
    row = lambda a, l: a[l][None]

    REST = ('w_xq', 'w_xkv', 'w_xo', 'w_ffn_gu', 'w_ffn_down')

    def mixer_names(l):
        return ('ab_w_in', 'ab_w_out') if l % 2 == 0 else ('c_w_in', 'c_w_a', 'c_w_i', 'c_w_out')

    def shards_of(l, names):
        out = []
        for n in names:
            s = w[n][l if w[n].shape[0] == DEPTH else l // 2].astype(BF16)
            out.append(s.reshape(-1, s.shape[-1]))
        return out

    def mixer_weights(l, full):
        if l % 2 == 0:
            e = l // 2
            return (row(g_mix_pre, l), row(g_mix_post, l), _ab_pack(_columns_of_blocks(full['ab_w_in'])), ab_bfb[e],
                    small['ab_conv_w'][e], full['ab_w_out'].reshape(DM, DM))
        o = l // 2
        gate_w = lambda g: g.reshape(NDEV, LRU_NB, LRU_BW // NDEV, LRU_BW).transpose(1, 0, 2, 3).reshape(
            LRU_NB, LRU_BW, LRU_BW)
        return (row(g_mix_pre, l), row(g_mix_post, l), full['c_w_in'], small['c_conv_w'][o], row(small['c_conv_b'], o),
                jnp.stack([gate_w(full['c_w_a']), gate_w(full['c_w_i'])]), c_bai[o], row(small['c_lam'], o),
                full['c_w_out'].reshape(DM, DM))

    def rest_weights(l, full):
        cross = (row(g_cross_pre, l), row(g_mem, l), row(g_cross_post, l), full['w_xq'].reshape(DM, DM), full['w_xkv'],
                 full['w_xo'].reshape(DM, DM))
        ffn = (row(g_ffn_pre, l), row(g_ffn_post, l), full['w_ffn_gu'], full['w_ffn_down'].reshape(D_FF, DM))
        return cross, ffn

    def gathered(state, names, after, tag):
        shards, lands = _ag_wait(state, after, "ag_wait_" + tag)
        full = _ag_finish(shards, lands)
        return dict(zip(names, full)), full[0]

    saved, weights = [], []
    h = xs
    names_of = lambda l: mixer_names(l) + REST
    states = {}
    st_m, _ = _ag_start(shards_of(0, mixer_names(0)), small_w_all, "ag_start_0m")
    st_r, _ = _ag_start(shards_of(0, REST), st_m[2], "ag_start_0r")
    states[1], token = _ag_start(shards_of(1, names_of(1)), st_r[2], "ag_start_1")
    full_m, _ = gathered(st_m, mixer_names(0), xs, "0m")
    for l in range(DEPTH):
        if l > 0:
            full, done = gathered(states[l], names_of(l), h, str(l))
            full_m = full_r = full
            token = None
            if l + 2 < DEPTH:
                states[l + 2], token = _ag_start(shards_of(l + 2, names_of(l + 2)), done, "ag_start_%d" % (l + 2))
        mixer = mixer_weights(l, full_m)
        h, s_mix = (_fox_layer_fwd if l % 2 == 0 else _lru_layer_fwd)(h, *mixer, after=token)
        token = None
        if l == 0:
            full_r, done = gathered(st_r, REST, h, "0r")
            states[2], token = _ag_start(shards_of(2, names_of(2)), done, "ag_start_2")
        cross, ffn = rest_weights(l, full_r)
        h, s_cross = _cross_fwd(h, mems, *cross, after=token)
        h, s_ffn = _ffn_fwd(h, *ffn)
        saved.append((s_mix, s_cross, s_ffn))
        weights.append((mixer, cross, ffn))
    mixer_args = lambda l: weights[l][0]
    cross_args = lambda l: weights[l][1]
    ffn_args = lambda l: weights[l][2]
    dx, loss_rep = _loss_head(h, target)
    loss = lax.psum(loss_rep[0, 0], ("x", "y", "c"))

    grads = {n: [None] * w[n].shape[0] for n in BIG}
    partial = {n: [None] * w[n].shape[0] for n in REPLICATED + SMALL_SHARDED}
    def finish(pending, after):
        state, names, where = pending
        for n, g in zip(names, _rs_end(state, after, pos)):
            grads[n][where[n]] = g

    def unit(layer, names):
        return [layer[n][1] for n in names], names, {n: layer[n][0] for n in names}

    d2d = ici = None
    token = None
    for l in reversed(range(DEPTH)):
        s_mix, s_cross, s_ffn = saved[l]
        dx, partial['g_ffn_pre'][l], partial['g_ffn_post'][l], dwgu, dwd = _ffn_bwd(dx, s_ffn, *ffn_args(l), after=token)
        token = None
        if d2d is not None:
            g5s, gots = _rs_d2d_wait(d2d[0], dx, "rs_d2d_wait_%d" % (l + 1))
            state, token = _rs_mid(g5s, gots, pos, str(l + 1))
            ici, d2d = (state,) + d2d[1:], None
        (dx, partial['g_cross_pre'][l], partial['g_mem'][l], partial['g_cross_post'][l], dwq, dwkv, dwo) = _cross_bwd(
            dx, s_cross, mems, *cross_args(l), after=token)
        token = None
        layer = {'w_xq': (l, dwq.reshape(NDEV, DM // NDEV, DM)), 'w_xkv': (l, dwkv), 'w_xo': (l, dwo.reshape(NDEV, DM // NDEV, DM)),
                 'w_ffn_gu': (l, dwgu), 'w_ffn_down': (l, dwd.reshape(NDEV, D_FF // NDEV, DM))}
        if l == 0:
            gs, names, where = unit(layer, REST)
            state, token = _rs_begin(gs, pos, "0r")
            ici_rest = (state, names, where)
        if l % 2 == 0:
            e = l // 2
            (dx, partial['g_mix_pre'][l], partial['g_mix_post'][l], dwall, partial['ab_b_f'][e], partial['ab_conv_w'][e],
             dwout) = _fox_layer_bwd(dx, s_mix, *mixer_args(l), after=token)
            layer['ab_w_in'] = (e, _blocks_of_columns(_ab_unpack(dwall)))
            layer['ab_w_out'] = (e, dwout.reshape(NDEV, DM // NDEV, DM))
        else:
            o = l // 2
            (dx, partial['g_mix_pre'][l], partial['g_mix_post'][l], dwin, partial['c_conv_w'][o], dconvb, dwai, dbai, dlam,
             dwout) = _lru_layer_bwd(dx, s_mix, *mixer_args(l), after=token)
            partial['c_conv_b'][o], partial['c_lam'][o] = dconvb[0], dlam[0]
            partial['c_b_a'][o], partial['c_b_i'][o] = dbai[0].reshape(LRU_NB, LRU_BW), dbai[1].reshape(LRU_NB, LRU_BW)
            rows = LRU_BW // NDEV
            by_dev = lambda d: d.reshape(LRU_NB, NDEV, rows, LRU_BW).transpose(1, 0, 2, 3).reshape(NDEV, LRU_NB * rows, LRU_BW)
            layer['c_w_in'] = (o, dwin)
            layer['c_w_a'] = (o, by_dev(dwai[0]))
            layer['c_w_i'] = (o, by_dev(dwai[1]))
            layer['c_w_out'] = (o, dwout.reshape(NDEV, DM // NDEV, DM))
        token = None
        if ici is not None:
            finish(ici, dx)
            ici = None
        if l > 0:
            gs, names, where = unit(layer, list(layer))
            state, token = _rs_d2d_start(_as_g5(gs), "rs_d2d_start_%d" % l)
            d2d = (state, names, where)
    small_names = REPLICATED + SMALL_SHARDED
    small_parts = [jnp.stack([p.reshape(w[n].shape[1:] if n in REPLICATED else small[n].shape[1:]) for p in partial[n]])
                   for n in small_names]
    small_all = _small_gather(_pack_small(small_parts))
    reduced = _unpack_small(_sum_devices(small_all), [p.shape for p in small_parts])
    grad = {}
    for n, g in zip(small_names, reduced):
        grad[n] = g if n in REPLICATED else _small_shard(g, dev)

    gs, names, where = unit(layer, mixer_names(0))
    state, token = _rs_begin(gs, pos, "0m", after=small_all)
    ici_mixer = (state, names, where)
    finish(ici_rest, dx)

    delta, new_m, new_v = {}, {}, {}
    last = mixer_names(0)
    for n in BIG:
        if n not in last:
            grad[n] = jnp.stack(grads[n]).reshape(w[n].shape)
            delta[n], new_m[n], new_v[n] = _adamw_nd(w[n], grad[n], mom[n], var[n], token)
            token = delta[n]
    shapes = [w[n].shape for n in small_names]
    packed = [_pack_small([t[n] for n in small_names]) for t in (w, grad, mom, var)]
    res_small = _adamw(*packed, after=token)
    for res, out in zip(res_small, (delta, new_m, new_v)):
        for n, val in zip(small_names, _unpack_small(res, shapes)):
            out[n] = val
    finish(ici_mixer, res_small[0])
    for n in last:
        grad[n] = jnp.stack(grads[n]).reshape(w[n].shape)
        delta[n], new_m[n], new_v[n] = _adamw_nd(w[n], grad[n], mom[n], var[n])

    return (loss, dx[None], *[grad[n] for n in WEIGHT_NAMES], *[delta[n] for n in WEIGHT_NAMES],
            *[new_m[n] for n in WEIGHT_NAMES], *[new_v[n] for n in WEIGHT_NAMES])
```

```python
import functools
import math

import jax
import jax.numpy as jnp
from jax import lax
from jax.experimental import pallas as pl
from jax.experimental.pallas import tpu as pltpu

F32 = jnp.float32
BF16 = jnp.bfloat16
BS = pl.BlockSpec
ANY = pl.BlockSpec(memory_space=pl.ANY)
MESH = pl.DeviceIdType.MESH

DM = 1024
DEPTH = 4
EPS = 1e-6
NEG = -1e30
FOX_W = 512
FOX_HD = 64
FOX_H = 8
SC_W = 512
SC_K = 3
AB_IN = 3 * FOX_W + FOX_H + 3 * SC_W
AB_PAD = 3200
LRU_BW = 256
LRU_NB = 4
RG_K = 4
RG_C = 8.0
MEM_H = 4
MEM_HD = 256
D_FF = 2816
NDEV = 8
FFB = 2 * D_FF // NDEV
ADAM_LR, ADAM_B1, ADAM_B2, ADAM_EPS, ADAM_WD, ADAM_STEP = 0.001, 0.9, 0.999, 1e-08, 0.01, 10

LANE = 128
VMEM_LIMIT = 56 * 1024 * 1024


def _params(ngrid):
    return pltpu.CompilerParams(dimension_semantics=("arbitrary",) * ngrid, vmem_limit_bytes=VMEM_LIMIT)


def _call(kern, **kwargs):
    return pl.pallas_call(kern, **kwargs)


TK_RED = 2048
TM_SUM = 512


def _tile(n, t):
    return t if n % t == 0 else n


def _mm(name, a, b, *, grid, a_spec, b_spec, o_spec, out_shape, dn, out_dtype=F32):
    nred = grid[-1]
    ngrid = len(grid)

    def kern(a_ref, b_ref, o_ref, *scratch):
        p = lax.dot_general(a_ref[...].astype(BF16), b_ref[...].astype(BF16), (dn, ((), ())),
                            preferred_element_type=F32)
        if nred == 1:
            o_ref[...] = p.astype(o_ref.dtype)
            return
        acc = scratch[0] if scratch else o_ref
        r = pl.program_id(ngrid - 1)

        @pl.when(r == 0)
        def _():
            acc[...] = p

        @pl.when(r > 0)
        def _():
            acc[...] += p

        if scratch:
            @pl.when(r == nred - 1)
            def _():
                o_ref[...] = acc[...].astype(o_ref.dtype)

    blk = tuple(d for d in o_spec.block_shape if d is not None)
    scratch = [pltpu.VMEM(blk, F32)] if (nred > 1 and out_dtype != F32) else []
    return _call(kern, name=name, grid=grid, in_specs=[a_spec, b_spec], out_specs=o_spec,
                          out_shape=jax.ShapeDtypeStruct(out_shape, out_dtype), scratch_shapes=scratch,
                          compiler_params=_params(ngrid))(a, b)


NN = ((1,), (0,))
NT = ((1,), (1,))
TN = ((0,), (0,))


def _mm_nn(name, a, w, out_dtype=F32, tn=None):
    m, k = a.shape
    n = w.shape[1]
    tm = _tile(m, 512)
    tn = n if tn is None else tn
    return _mm(name, a, w, grid=(m // tm, n // tn, 1), a_spec=BS((tm, k), lambda i, j, r: (i, 0)),
               b_spec=BS((k, tn), lambda i, j, r: (0, j)), o_spec=BS((tm, tn), lambda i, j, r: (i, j)),
               out_shape=(m, n), dn=NN, out_dtype=out_dtype)


def _mm_nt(name, a, w, out_dtype=F32, tn=None):
    m, n = a.shape
    k = w.shape[0]
    tm = _tile(m, 512)
    tn = n if tn is None else tn
    return _mm(name, a, w, grid=(m // tm, n // tn), a_spec=BS((tm, tn), lambda i, r: (i, r)),
               b_spec=BS((k, tn), lambda i, r: (0, r)), o_spec=BS((tm, k), lambda i, r: (i, 0)),
               out_shape=(m, k), dn=NT, out_dtype=out_dtype)


def _mm_tn(name, a, b, tn=None):
    m, k = a.shape
    n = b.shape[1]
    tm = _tile(m, TK_RED)
    tn = n if tn is None else tn
    return _mm(name, a, b, grid=(n // tn, m // tm), a_spec=BS((tm, k), lambda j, r: (r, 0)),
               b_spec=BS((tm, tn), lambda j, r: (r, j)), o_spec=BS((k, tn), lambda j, r: (0, j)),
               out_shape=(k, n), dn=TN)


def _bmm_nn(name, a, w, out_dtype=F32):
    m, k = a.shape
    g, _, n = w.shape
    tm = _tile(m, 512)
    return _mm(name, a, w, grid=(g, m // tm, 1), a_spec=BS((tm, k), lambda q, i, r: (i, 0)),
               b_spec=BS((None, k, n), lambda q, i, r: (q, 0, 0)), o_spec=BS((None, tm, n), lambda q, i, r: (q, i, 0)),
               out_shape=(g, m, n), dn=NN, out_dtype=out_dtype)


def _bmm_tn(name, a, b):
    m, k = a.shape
    g, _, n = b.shape
    tm = _tile(m, TK_RED)
    return _mm(name, a, b, grid=(g, m // tm), a_spec=BS((tm, k), lambda q, r: (r, 0)),
               b_spec=BS((None, tm, n), lambda q, r: (q, r, 0)), o_spec=BS((None, k, n), lambda q, r: (q, 0, 0)),
               out_shape=(g, k, n), dn=TN)


def _block_sum(name, a, w, dn, out_cols):
    g, m, ac = a.shape
    tm = _tile(m, TM_SUM)

    def kern(a_ref, w_ref, o_ref):
        acc = None
        for q in range(g):
            p = lax.dot_general(a_ref[q].astype(BF16), w_ref[q].astype(BF16), (dn, ((), ())), preferred_element_type=F32)
            acc = p if acc is None else acc + p
        o_ref[...] = acc

    return _call(kern, name=name, grid=(m // tm,),
                 in_specs=[BS((g, tm, ac), lambda i: (0, i, 0)), BS(w.shape, lambda i: (0, 0, 0))],
                 out_specs=BS((tm, out_cols), lambda i: (i, 0)), out_shape=jax.ShapeDtypeStruct((m, out_cols), F32),
                 compiler_params=_params(1))(a, w)


def _bmm_nt_sum(name, a, w):
    return _block_sum(name, a, w, NT, w.shape[1])


def _bmm_nn_sum(name, a, w):
    return _block_sum(name, a, w, NN, w.shape[2])


def _bbmm_tn(name, a, b):
    g, m, k = a.shape
    n = b.shape[2]
    tm = _tile(m, TK_RED)
    return _mm(name, a, b, grid=(g, m // tm), a_spec=BS((None, tm, k), lambda q, r: (q, r, 0)),
               b_spec=BS((None, tm, n), lambda q, r: (q, r, 0)), o_spec=BS((None, k, n), lambda q, r: (q, 0, 0)),
               out_shape=(g, k, n), dn=TN)


def _rstd(x):
    return lax.rsqrt(jnp.mean(x * x, axis=-1, keepdims=True) + EPS)


def _norm_fwd(x, g, after=None):
    rows = x.shape[0]
    tm = _tile(rows, 512)

    def kern(x_ref, g_ref, *rest):
        xv = x_ref[...]
        rest[-1][...] = ((xv * _rstd(xv)) * g_ref[...]).astype(BF16)

    extra = () if after is None else (after,)
    return _call(kern, name="norm_fwd", grid=(rows // tm,),
                          in_specs=[BS((tm, DM), lambda i: (i, 0)), BS((1, DM), lambda i: (0, 0))] + [ANY] * len(extra),
                          out_specs=BS((tm, DM), lambda i: (i, 0)),
                          out_shape=jax.ShapeDtypeStruct((rows, DM), BF16), compiler_params=_params(1))(x, g, *extra)


def _norm_res(x, y, g):
    rows = x.shape[0]
    tm = _tile(rows, 512)

    def kern(x_ref, y_ref, g_ref, o_ref):
        yv = y_ref[...]
        o_ref[...] = x_ref[...] + (yv * _rstd(yv)) * g_ref[...]

    row = BS((tm, DM), lambda i: (i, 0))
    return _call(kern, name="norm_res", grid=(rows // tm,),
                          in_specs=[row, row, BS((1, DM), lambda i: (0, 0))], out_specs=row,
                          out_shape=jax.ShapeDtypeStruct((rows, DM), F32), compiler_params=_params(1))(x, y, g)


def _norm_bwd(z, dout, g, resid, out_dtype, after=None):
    rows = z.shape[0]
    tm = _tile(rows, 512)
    has_res = resid is not None

    def kern(*refs):
        z_ref, d_ref, g_ref = refs[:3]
        r_ref = refs[3] if has_res else None
        dz_ref, dg_ref = refs[-2:]
        zv = z_ref[...]
        dv = d_ref[...].astype(F32)
        r = _rstd(zv)
        zh = zv * r
        dzh = dv * g_ref[...]
        dz = r * (dzh - zh * jnp.mean(dzh * zh, axis=-1, keepdims=True))
        if has_res:
            dz = dz + r_ref[...]
        dz_ref[...] = dz.astype(dz_ref.dtype)
        part = jnp.sum(dv * zh, axis=0, keepdims=True)

        @pl.when(pl.program_id(0) == 0)
        def _():
            dg_ref[...] = part

        @pl.when(pl.program_id(0) > 0)
        def _():
            dg_ref[...] += part

    row = BS((tm, DM), lambda i: (i, 0))
    vec = BS((1, DM), lambda i: (0, 0))
    ins = [row, row, vec] + ([row] if has_res else []) + ([ANY] if after is not None else [])
    args = (z, dout, g) + ((resid,) if has_res else ()) + ((after,) if after is not None else ())
    return _call(kern, name="norm_bwd_res" if has_res else "norm_bwd", grid=(rows // tm,), in_specs=ins,
                          out_specs=[row, vec],
                          out_shape=[jax.ShapeDtypeStruct((rows, DM), out_dtype), jax.ShapeDtypeStruct((1, DM), F32)],
                          compiler_params=_params(1))(*args)


def _ffn_up(h, wgu4):
    s = h.shape[0]
    tm = _tile(s, 512)

    def kern(h_ref, w_ref, gu_ref, a_ref):
        hv = h_ref[...]
        gate = lax.dot_general(hv, w_ref[0], (NT, ((), ())), preferred_element_type=F32)
        up = lax.dot_general(hv, w_ref[1], (NT, ((), ())), preferred_element_type=F32)
        gu_ref[0] = gate.astype(BF16)
        gu_ref[1] = up.astype(BF16)
        a_ref[...] = (gate * jax.nn.sigmoid(gate) * up).astype(BF16)

    return _call(
        kern, name="ffn_up", grid=(4, s // tm),
        in_specs=[BS((tm, DM), lambda j, i: (i, 0)), BS((2, None, FFB, DM), lambda j, i: (0, j, 0, 0))],
        out_specs=[BS((2, None, tm, FFB), lambda j, i: (0, j, i, 0)), BS((None, tm, FFB), lambda j, i: (j, i, 0))],
        out_shape=[jax.ShapeDtypeStruct((2, 4, s, FFB), BF16), jax.ShapeDtypeStruct((4, s, FFB), BF16)],
        compiler_params=_params(2))(h, wgu4)


def _ffn_da(dy, wd4, gu):
    s = dy.shape[0]
    tm = _tile(s, 512)

    def kern(dy_ref, w_ref, gu_ref, o_ref):
        da = lax.dot_general(dy_ref[...], w_ref[...], (NT, ((), ())), preferred_element_type=F32)
        gate = gu_ref[0].astype(F32)
        up = gu_ref[1].astype(F32)
        sg = jax.nn.sigmoid(gate)
        o_ref[0] = (da * up * (sg * (1.0 + gate * (1.0 - sg)))).astype(BF16)
        o_ref[1] = (da * (gate * sg)).astype(BF16)

    blk = BS((2, None, tm, FFB), lambda j, i: (0, j, i, 0))
    return _call(
        kern, name="ffn_da", grid=(4, s // tm),
        in_specs=[BS((tm, DM), lambda j, i: (i, 0)), BS((None, FFB, DM), lambda j, i: (j, 0, 0)), blk],
        out_specs=blk, out_shape=jax.ShapeDtypeStruct((2, 4, s, FFB), BF16), compiler_params=_params(2))(dy, wd4, gu)


def _ffn_fwd(x, gpre, gpost, wgu, wd):
    h = _norm_fwd(x, gpre)
    gu, a = _ffn_up(h, wgu.reshape(2, 4, FFB, DM))
    y = _bmm_nn_sum("ffn_down", a, wd.reshape(4, FFB, DM))
    return _norm_res(x, y, gpost), (x, h, gu, a, y)


def _ffn_bwd(dxo, saved, gpre, gpost, wgu, wd, after=None):
    x, h, gu, a, y = saved
    s = x.shape[0]
    dy, dgpost = _norm_bwd(y, dxo, gpost, None, BF16, after)
    dgu = _ffn_da(dy, wd.reshape(4, FFB, DM), gu).reshape(8, s, FFB)
    dwd = _bmm_tn_a3("ffn_dwd", a, dy)
    dwgu = _bmm_tn_a3("ffn_dwgu", dgu, h)
    dh = _bmm_nn_sum("ffn_dh", dgu, wgu)
    dx, dgpre = _norm_bwd(x, dh, gpre, dxo, F32)
    return dx, dgpre, dgpost, dwgu, dwd.reshape(D_FF, DM)


def _bmm_tn_a3(name, a, b):
    g, m, k = a.shape
    n = b.shape[1]
    tm = _tile(m, TK_RED)
    return _mm(name, a, b, grid=(g, m // tm), a_spec=BS((None, tm, k), lambda q, r: (q, r, 0)),
               b_spec=BS((tm, n), lambda q, r: (r, 0)), o_spec=BS((None, k, n), lambda q, r: (q, 0, 0)),
               out_shape=(g, k, n), dn=TN)


def _softmax_rows(s):
    m = jnp.max(s, axis=-1, keepdims=True)
    p = jnp.exp(s - m)
    return p / jnp.sum(p, axis=-1, keepdims=True)


def _xattn_fwd_call(h, wq, kv):
    s = h.shape[0]
    mlen = kv.shape[1]
    tm = _tile(s, 512)
    scale = MEM_HD ** -0.5

    def kern(h_ref, w_ref, k_ref, v_ref, q_ref, o_ref):
        q = jnp.dot(h_ref[...], w_ref[...], preferred_element_type=F32).astype(BF16)
        q_ref[...] = q
        sc = lax.dot_general(q, k_ref[...], (NT, ((), ())), preferred_element_type=F32) * scale
        p = _softmax_rows(sc)
        o_ref[...] = jnp.dot(p.astype(BF16), v_ref[...], preferred_element_type=F32).astype(BF16)

    blk = BS((tm, MEM_HD), lambda i, hd: (i, hd))
    return _call(
        kern, name="xattn_fwd", grid=(s // tm, MEM_H),
        in_specs=[BS((tm, DM), lambda i, hd: (i, 0)), BS((DM, MEM_HD), lambda i, hd: (0, hd)),
                  BS((None, mlen, MEM_HD), lambda i, hd: (hd, 0, 0)),
                  BS((None, mlen, MEM_HD), lambda i, hd: (MEM_H + hd, 0, 0))],
        out_specs=[blk, blk],
        out_shape=[jax.ShapeDtypeStruct((s, DM), BF16), jax.ShapeDtypeStruct((s, DM), BF16)],
        compiler_params=_params(2))(h, wq, kv, kv)


def _xattn_bwd_call(q, kv, do):
    s = q.shape[0]
    mlen = kv.shape[1]
    tm = _tile(s, 512)
    scale = MEM_HD ** -0.5

    def kern(q_ref, k_ref, v_ref, do_ref, dq_ref, dkv_ref):
        qv, kvv, vv, dov = q_ref[...], k_ref[...], v_ref[...], do_ref[...]
        sc = lax.dot_general(qv, kvv, (NT, ((), ())), preferred_element_type=F32) * scale
        p = _softmax_rows(sc)
        dp = lax.dot_general(dov, vv, (NT, ((), ())), preferred_element_type=F32)
        ds = (p * (dp - jnp.sum(dp * p, axis=-1, keepdims=True)) * scale).astype(BF16)
        dq_ref[...] = jnp.dot(ds, kvv, preferred_element_type=F32).astype(BF16)
        dk = lax.dot_general(ds, qv, (TN, ((), ())), preferred_element_type=F32)
        dv = lax.dot_general(p.astype(BF16), dov, (TN, ((), ())), preferred_element_type=F32)

        @pl.when(pl.program_id(1) == 0)
        def _():
            dkv_ref[0] = dk
            dkv_ref[1] = dv

        @pl.when(pl.program_id(1) > 0)
        def _():
            dkv_ref[0] += dk
            dkv_ref[1] += dv

    blk = BS((tm, MEM_HD), lambda hd, i: (i, hd))
    return _call(
        kern, name="xattn_bwd", grid=(MEM_H, s // tm),
        in_specs=[blk, BS((None, mlen, MEM_HD), lambda hd, i: (hd, 0, 0)),
                  BS((None, mlen, MEM_HD), lambda hd, i: (MEM_H + hd, 0, 0)), blk],
        out_specs=[blk, BS((2, None, mlen, MEM_HD), lambda hd, i: (0, hd, 0, 0))],
        out_shape=[jax.ShapeDtypeStruct((s, DM), BF16), jax.ShapeDtypeStruct((2, MEM_H, mlen, MEM_HD), F32)],
        compiler_params=_params(2))(q, kv, kv, do)


def _cross_fwd(x, mem, gpre, gmem, gpost, wq, wkv, wo, after=None):
    h = _norm_fwd(x, gpre, after)
    mn = _norm_fwd(mem, gmem)
    kv = _bmm_nn("xattn_kv", mn, wkv, BF16)
    q, o = _xattn_fwd_call(h, wq, kv)
    y = _mm_nn("xattn_out", o, wo)
    return _norm_res(x, y, gpost), (x, h, mn, kv, q, o, y)


def _cross_bwd(dxo, saved, mem, gpre, gmem, gpost, wq, wkv, wo, after=None):
    x, h, mn, kv, q, o, y = saved
    mlen = mem.shape[0]
    dy, dgpost = _norm_bwd(y, dxo, gpost, None, BF16, after)
    do = _mm_nt("xattn_do", dy, wo, BF16)
    dwo = _mm_tn("xattn_dwo", o, dy)
    dq, dkv = _xattn_bwd_call(q, kv, do)
    dwq = _mm_tn("xattn_dwq", h, dq)
    dh = _mm_nt("xattn_dh", dq, wq)
    dkv8 = dkv.reshape(8, mlen, MEM_HD)
    dwkv = _bmm_tn("xattn_dwkv", mn, dkv8)
    dmn = _bmm_nt_sum("xattn_dmn", dkv8, wkv)
    _, dgmem = _norm_bwd(mem, dmn, gmem, None, BF16)
    dx, dgpre = _norm_bwd(x, dh, gpre, dxo, F32)
    return dx, dgpre, dgmem, dgpost, dwq, dwkv, dwo


def _log_sigmoid(z):
    return jnp.minimum(z, 0.0) - jnp.log1p(jnp.exp(-jnp.abs(z)))


def _lane_scan_steps():
    return (1, 2, 4, 8, 16, 32, 64)


def _fox_cum(frow, bfb):
    s = frow.shape[1]

    def kern(f_ref, b_ref, o_ref):
        lane = lax.broadcasted_iota(jnp.int32, (FOX_H, LANE), 1)
        carry = jnp.zeros((FOX_H, 1), F32)
        for c in range(s // LANE):
            sl = slice(c * LANE, (c + 1) * LANE)
            lf = _log_sigmoid(f_ref[:, sl] + b_ref[...])
            v = lf
            for d in _lane_scan_steps():
                v = v + jnp.where(lane >= d, pltpu.roll(v, d, 1), 0.0)
            o_ref[:, sl] = v + carry
            carry = carry + jnp.sum(lf, axis=1, keepdims=True)

    return _call(kern, name="fox_cum", out_shape=jax.ShapeDtypeStruct((FOX_H, s), F32),
                          compiler_params=pltpu.CompilerParams(vmem_limit_bytes=VMEM_LIMIT))(frow, bfb)


def _fox_dlogf(dcq, dck, frow, bfb):
    s = frow.shape[1]

    def kern(q_ref, d_ref, f_ref, b_ref, df_ref, db_ref):
        lane = lax.broadcasted_iota(jnp.int32, (FOX_H, LANE), 1)
        carry = jnp.zeros((FOX_H, 1), F32)
        dbf = jnp.zeros((FOX_H, 1), F32)
        for c in reversed(range(s // LANE)):
            sl = slice(c * LANE, (c + 1) * LANE)
            dc = q_ref[:, sl] - d_ref[:, sl]
            v = dc
            for d in _lane_scan_steps():
                v = v + jnp.where(lane < LANE - d, pltpu.roll(v, LANE - d, 1), 0.0)
            v = v + carry
            carry = carry + jnp.sum(dc, axis=1, keepdims=True)
            df = v * jax.nn.sigmoid(-(f_ref[:, sl] + b_ref[...]))
            df_ref[:, sl] = df
            dbf = dbf + jnp.sum(df, axis=1, keepdims=True)
        db_ref[...] = jnp.broadcast_to(dbf, (FOX_H, LANE))

    return _call(kern, name="fox_dlogf",
                          out_shape=[jax.ShapeDtypeStruct((FOX_H, s), F32), jax.ShapeDtypeStruct((FOX_H, LANE), F32)],
                          compiler_params=pltpu.CompilerParams(vmem_limit_bytes=VMEM_LIMIT))(dcq, dck, frow, bfb)


FOX_TQ = 512
Q_COL, K_COL, V_COL = 0, FOX_W // LANE, 2 * FOX_W // LANE
B_COL, C_COL, U_COL = 12, 16, 20


def _bf16_terms(c):
    hi = c.astype(BF16).astype(F32)
    mid = (c - hi).astype(BF16).astype(F32)
    return hi, mid, (c - hi - mid).astype(BF16).astype(F32)


def _fox_operands(qv, kv, cq, ck, lane, hh, scale):
    sel = (lane < FOX_HD) if hh == 0 else (lane >= FOX_HD)
    b0 = FOX_HD if hh == 0 else 0
    qa = jnp.where(sel, qv * scale, 0.0)
    ka = jnp.where(sel, kv, 0.0)
    for n, (tq_, tk_) in enumerate(zip(_bf16_terms(cq), _bf16_terms(ck))):
        qa = jnp.where(lane == b0 + n, tq_, jnp.where(lane == b0 + 3 + n, 1.0, qa))
        ka = jnp.where(lane == b0 + n, 1.0, jnp.where(lane == b0 + 3 + n, -tk_, ka))
    return sel, qa.astype(BF16), ka.astype(BF16)


def _fox_logits(qa, ka, causal):
    sc = lax.dot_general(qa, ka, (NT, ((), ())), preferred_element_type=F32)
    return sc if causal is None else jnp.where(causal, sc, NEG)


def _fox_prep(proj, cumc):
    s = proj.shape[0]
    tp = _tile(s, 512)
    scale = FOX_HD ** -0.5

    def kern(q_ref, k_ref, c_ref, qa_ref, ka_ref):
        lane = lax.broadcasted_iota(jnp.int32, (tp, LANE), 1)
        for hh in range(2):
            _, qa_ref[hh], ka_ref[hh] = _fox_operands(q_ref[...], k_ref[...], c_ref[hh], c_ref[hh], lane, hh, scale)

    pair = BS((2, tp, LANE), lambda hp, i: (hp, i, 0))
    shp = jax.ShapeDtypeStruct((FOX_H, s, LANE), BF16)
    return _call(kern, name="fox_prep", grid=(4, s // tp),
                 in_specs=[BS((tp, LANE), lambda hp, i: (i, Q_COL + hp)), BS((tp, LANE), lambda hp, i: (i, K_COL + hp)), pair],
                 out_specs=[pair, pair], out_shape=[shp, shp], compiler_params=_params(2))(proj, proj, cumc)


def _fox_fwd_call(proj, qa, ka):
    s = proj.shape[0]
    tq = _tile(s, FOX_TQ)
    nq = s // tq
    reps = tq // LANE
    scale = FOX_HD ** -0.5

    def kern(qa_ref, ka_ref, v_ref, o_ref, lse_ref, m_s, l_s, acc_s):
        i = pl.program_id(1)
        j = pl.program_id(2)
        lane = lax.broadcasted_iota(jnp.int32, (tq, LANE), 1)

        @pl.when(j == 0)
        def _():
            m_s[...] = jnp.full(m_s.shape, NEG, F32)
            l_s[...] = jnp.zeros(l_s.shape, F32)
            acc_s[...] = jnp.zeros(acc_s.shape, F32)

        def step(diagonal):
            vb = v_ref[...].astype(BF16)
            causal = (lax.broadcasted_iota(jnp.int32, (tq, tq), 0) >= lax.broadcasted_iota(jnp.int32, (tq, tq), 1)
                      if diagonal else None)
            for hh in range(2):
                sc = _fox_logits(qa_ref[hh], ka_ref[hh], causal)
                m_prev = m_s[hh]
                m_new = jnp.maximum(m_prev, jnp.max(sc, axis=-1, keepdims=True))
                alpha = jnp.exp(m_prev - m_new)
                p = jnp.exp(sc - m_new)
                l_s[hh] = alpha * l_s[hh] + jnp.sum(p, axis=-1, keepdims=True)
                acc_s[hh] = alpha * acc_s[hh] + jnp.dot(p.astype(BF16), vb, preferred_element_type=F32)
                m_s[hh] = m_new

        @pl.when(j < i)
        def _():
            step(False)

        @pl.when(j == i)
        def _():
            step(True)
            o_ref[...] = jnp.where(lane < FOX_HD, acc_s[0] / l_s[0], acc_s[1] / l_s[1])
            for hh in range(2):
                lse_ref[hh] = jnp.broadcast_to(m_s[hh] + jnp.log(l_s[hh]), (tq, LANE))

    kvi = lambda hp, i, j: jnp.minimum(j, i)
    return _call(
        kern, name="fox_fwd", grid=(4, nq, nq),
        in_specs=[BS((2, tq, LANE), lambda hp, i, j: (hp, i, 0)),
                  BS((2, tq, LANE), lambda hp, i, j: (hp, kvi(hp, i, j), 0)),
                  BS((tq, LANE), lambda hp, i, j: (kvi(hp, i, j), V_COL + hp))],
        out_specs=[BS((tq, LANE), lambda hp, i, j: (i, hp)), BS((2, tq, LANE), lambda hp, i, j: (hp, i, 0))],
        out_shape=[jax.ShapeDtypeStruct((s, FOX_W), F32), jax.ShapeDtypeStruct((FOX_H, s, LANE), F32)],
        scratch_shapes=[pltpu.VMEM((2, tq, 1), F32), pltpu.VMEM((2, tq, 1), F32), pltpu.VMEM((2, tq, LANE), F32)],
        compiler_params=_params(3))(qa, ka, proj)


ROWSUM_M = 16


def _fox_bwd_call(proj, o, lse, dcat, qa, ka):
    s = proj.shape[0]
    tq = _tile(s, FOX_TQ)
    nq = s // tq
    reps = tq // LANE
    scale = FOX_HD ** -0.5

    def kern(qa_ref, ka_ref, v_ref, do_ref, o_ref, lse_ref, dq_ref, dk_ref, dv_ref, dck_ref, dcq_ref):
        j = pl.program_id(1)
        i = pl.program_id(2)
        lane = lax.broadcasted_iota(jnp.int32, (tq, LANE), 1)
        ones = jnp.ones((ROWSUM_M, tq), BF16)

        @pl.when((j == 0) & (i == 0))
        def _():
            dq_ref[...] = jnp.zeros(dq_ref.shape, F32)
            dcq_ref[...] = jnp.zeros(dcq_ref.shape, F32)

        @pl.when(i == j)
        def _():
            dk_ref[...] = jnp.zeros(dk_ref.shape, F32)
            dv_ref[...] = jnp.zeros(dv_ref.shape, F32)
            dck_ref[...] = jnp.zeros(dck_ref.shape, F32)

        def step(diagonal):
            dov = do_ref[...]
            ov = o_ref[...]
            vb = v_ref[...].astype(BF16)
            causal = (lax.broadcasted_iota(jnp.int32, (tq, tq), 0) >= lax.broadcasted_iota(jnp.int32, (tq, tq), 1)
                      if diagonal else None)
            dq_t = jnp.zeros((tq, LANE), F32)
            dk_t = jnp.zeros((tq, LANE), F32)
            dv_t = jnp.zeros((tq, LANE), F32)
            for hh in range(2):
                sel = (lane < FOX_HD) if hh == 0 else (lane >= FOX_HD)
                qa, ka = qa_ref[hh], ka_ref[hh]
                dom32 = jnp.where(sel, dov, 0.0)
                dom = dom32.astype(BF16)
                sc = _fox_logits(qa, ka, causal)
                p = jnp.exp(sc - jnp.tile(lse_ref[hh], (1, reps)))
                dp = lax.dot_general(dom, vb, (NT, ((), ())), preferred_element_type=F32)
                delta = jnp.sum(dom32 * ov, axis=-1, keepdims=True)
                ds = p * (dp - delta)
                dsb = ds.astype(BF16)
                dq_t = jnp.where(sel, jnp.dot(dsb, ka, preferred_element_type=F32) * scale, dq_t)
                dk_t = jnp.where(sel, lax.dot_general(dsb, qa, (TN, ((), ())), preferred_element_type=F32), dk_t)
                dv_t = dv_t + lax.dot_general(p.astype(BF16), dom, (TN, ((), ())), preferred_element_type=F32)
                dck_ref[hh] += jnp.sum(ds, axis=0, keepdims=True)
                ds_lo = (ds - dsb.astype(F32)).astype(BF16)
                dcq_ref[hh, i] += (lax.dot_general(ones, dsb, (NT, ((), ())), preferred_element_type=F32)
                                   + lax.dot_general(ones, ds_lo, (NT, ((), ())), preferred_element_type=F32))
            rows = pl.ds(pl.multiple_of(i * tq, tq), tq)
            dq_ref[rows, :] += dq_t
            dk_ref[...] += dk_t
            dv_ref[...] += dv_t

        @pl.when(i > j)
        def _():
            step(False)

        @pl.when(i == j)
        def _():
            step(True)

    qi = lambda hp, j, i: jnp.maximum(i, j)
    return _call(
        kern, name="fox_bwd", grid=(4, nq, nq),
        in_specs=[BS((2, tq, LANE), lambda hp, j, i: (hp, qi(hp, j, i), 0)),
                  BS((2, tq, LANE), lambda hp, j, i: (hp, j, 0)),
                  BS((tq, LANE), lambda hp, j, i: (j, V_COL + hp)),
                  BS((tq, LANE), lambda hp, j, i: (qi(hp, j, i), hp)),
                  BS((tq, LANE), lambda hp, j, i: (qi(hp, j, i), hp)),
                  BS((2, tq, LANE), lambda hp, j, i: (hp, qi(hp, j, i), 0))],
        out_specs=[BS((s, LANE), lambda hp, j, i: (0, hp)), BS((tq, LANE), lambda hp, j, i: (j, hp)),
                   BS((tq, LANE), lambda hp, j, i: (j, hp)), BS((2, 1, tq), lambda hp, j, i: (hp, 0, j)),
                   BS((2, nq, ROWSUM_M, tq), lambda hp, j, i: (hp, 0, 0, 0))],
        out_shape=[jax.ShapeDtypeStruct((s, FOX_W), F32), jax.ShapeDtypeStruct((s, FOX_W), F32),
                   jax.ShapeDtypeStruct((s, FOX_W), F32), jax.ShapeDtypeStruct((FOX_H, 1, s), F32),
                   jax.ShapeDtypeStruct((FOX_H, nq, ROWSUM_M, tq), F32)],
        compiler_params=_params(3))(qa, ka, proj, dcat, o, lse)


def _shift_down(v, d, row):
    return jnp.where(row >= d, pltpu.roll(v, d, 0), 0.0)


def _shift_up(v, d, row, n):
    return jnp.where(row < n - d, pltpu.roll(v, n - d, 0), 0.0)


def _sconv_fwd(proj, convw):
    s = proj.shape[0]

    def kern(b_ref, c_ref, u_ref, w_ref, y_ref):
        row = lax.broadcasted_iota(jnp.int32, (s, LANE), 0)
        z = c_ref[...] * u_ref[...]
        conv = w_ref[2:3, :] * z + w_ref[1:2, :] * _shift_down(z, 1, row) + w_ref[0:1, :] * _shift_down(z, 2, row)
        y_ref[...] = (b_ref[...] * conv).astype(BF16)

    col = lambda base: BS((s, LANE), lambda cb: (0, base + cb))
    return _call(kern, name="sconv_fwd", grid=(SC_W // LANE,),
                          in_specs=[col(B_COL), col(C_COL), col(U_COL), BS((SC_K, LANE), lambda cb: (0, cb))],
                          out_specs=BS((s, LANE), lambda cb: (0, cb)),
                          out_shape=jax.ShapeDtypeStruct((s, SC_W), BF16), compiler_params=_params(1))(proj, proj, proj, convw)


def _sconv_bwd(proj, convw, dcat):
    s = proj.shape[0]

    def kern(b_ref, c_ref, u_ref, w_ref, dy_ref, db_ref, dc_ref, du_ref, dw_ref):
        row = lax.broadcasted_iota(jnp.int32, (s, LANE), 0)
        cv, uv, dyv = c_ref[...], u_ref[...], dy_ref[...]
        z = cv * uv
        z1 = _shift_down(z, 1, row)
        z2 = _shift_down(z, 2, row)
        conv = w_ref[2:3, :] * z + w_ref[1:2, :] * z1 + w_ref[0:1, :] * z2
        db_ref[...] = dyv * conv
        dcv = dyv * b_ref[...]
        dz = w_ref[2:3, :] * dcv + w_ref[1:2, :] * _shift_up(dcv, 1, row, s) + w_ref[0:1, :] * _shift_up(dcv, 2, row, s)
        dc_ref[...] = dz * uv
        du_ref[...] = dz * cv
        dw_ref[0:1, :] = jnp.sum(dcv * z2, axis=0, keepdims=True)
        dw_ref[1:2, :] = jnp.sum(dcv * z1, axis=0, keepdims=True)
        dw_ref[2:3, :] = jnp.sum(dcv * z, axis=0, keepdims=True)

    col = lambda base: BS((s, LANE), lambda cb: (0, base + cb))
    out = BS((s, LANE), lambda cb: (0, cb))
    wspec = BS((SC_K, LANE), lambda cb: (0, cb))
    act = jax.ShapeDtypeStruct((s, SC_W), F32)
    return _call(kern, name="sconv_bwd", grid=(SC_W // LANE,),
                          in_specs=[col(B_COL), col(C_COL), col(U_COL), wspec, col(FOX_W // LANE)],
                          out_specs=[out, out, out, wspec],
                          out_shape=[act, act, act, jax.ShapeDtypeStruct((SC_K, SC_W), F32)],
                          compiler_params=_params(1))(proj, proj, proj, convw, dcat)


def _fox_layer_fwd(x, gpre, gpost, wall, bfb, convw, wout, after=None):
    s = x.shape[0]
    h = _norm_fwd(x, gpre, after)
    proj = _mm_nn("fox_proj", h, wall, tn=AB_PAD // 5)
    frow = proj[:, 3 * FOX_W + 3 * SC_W:3 * FOX_W + 3 * SC_W + FOX_H].T
    cumr = _fox_cum(frow, bfb)
    qa, ka = _fox_prep(proj, jnp.broadcast_to(cumr[:, :, None], (FOX_H, s, LANE)))
    o, lse = _fox_fwd_call(proj, qa, ka)
    yb = _sconv_fwd(proj, convw)
    cat = jnp.concatenate([o.astype(BF16), yb], axis=1)
    y = _mm_nn("fox_out", cat, wout)
    return _norm_res(x, y, gpost), (x, h, proj, frow, qa, ka, o, lse, cat, y)


def _fox_layer_bwd(dxo, saved, gpre, gpost, wall, bfb, convw, wout, after=None):
    x, h, proj, frow, qa, ka, o, lse, cat, y = saved
    s = x.shape[0]
    dy, dgpost = _norm_bwd(y, dxo, gpost, None, BF16, after)
    dcat = _mm_nt("fox_dcat", dy, wout)
    dwout = _mm_tn("fox_dwout", cat, dy)
    db, dc, du, dconvw = _sconv_bwd(proj, convw, dcat)
    dq, dk, dv, dck, dcq = _fox_bwd_call(proj, o, lse, dcat, qa, ka)
    dfrow, dbf = _fox_dlogf(dcq[:, :, 0, :].reshape(FOX_H, s), dck.reshape(FOX_H, s), frow, bfb)
    dfcol = jnp.pad(dfrow.T, ((0, 0), (0, LANE - FOX_H)))
    dproj = jnp.concatenate([dq, dk, dv, db, dc, du, dfcol], axis=1).astype(BF16)
    dwall = _mm_tn("fox_dwall", h, dproj, tn=AB_PAD // 5)
    dh = _mm_nt("fox_dh", dproj, wall, tn=AB_PAD // 5)
    dx, dgpre = _norm_bwd(x, dh, gpre, dxo, F32)
    return dx, dgpre, dgpost, dwall, dbf[:, 0], dconvw, dwout


def _ab_pack(w):
    nf = 3 * FOX_W
    return jnp.concatenate([w[:, :nf], w[:, nf + FOX_H:], w[:, nf:nf + FOX_H],
                            jnp.zeros((w.shape[0], AB_PAD - AB_IN), w.dtype)], axis=1)


def _ab_unpack(w):
    nf = 3 * FOX_W
    nbcu = 3 * SC_W
    return jnp.concatenate([w[:, :nf], w[:, nf + nbcu:nf + nbcu + FOX_H], w[:, nf:nf + nbcu]], axis=1)


NCH = DM // LANE
CH_PER_BLK = LRU_BW // LANE


def _chunk_spec(s, lead=0):
    return BS((None, s, LANE), lambda ch: (lead + ch // CH_PER_BLK, 0, ch % CH_PER_BLK))


def _vec_chunk(rows):
    return BS((rows, LANE), lambda ch: (0, ch))


def _neg_expm1(x):
    series = -x * (1.0 + x * (1 / 2) * (1.0 + x * (1 / 3) * (1.0 + x * (1 / 4) * (1.0 + x * (1 / 5) * (
        1.0 + x * (1 / 6) * (1.0 + x * (1 / 7)))))))
    return jnp.where(x > -0.25, series, 1.0 - jnp.exp(x))


def _softplus(z):
    return jnp.maximum(z, 0.0) + jnp.log1p(jnp.exp(-jnp.abs(z)))


GELU_C = math.sqrt(2.0 / math.pi)
GELU_A = 0.044715


def _gelu(x):
    return 0.5 * x * (1.0 + jnp.tanh(GELU_C * (x + GELU_A * x * x * x)))


def _gelu_grad(x):
    t = jnp.tanh(GELU_C * (x + GELU_A * x * x * x))
    return 0.5 * (1.0 + t) + 0.5 * x * (1.0 - t * t) * GELU_C * (1.0 + 3.0 * GELU_A * x * x)


def _lru_conv_fwd(gu, convw, convb):
    s = gu.shape[1]

    def kern(x_ref, w_ref, b_ref, u_ref):
        row = lax.broadcasted_iota(jnp.int32, (s, LANE), 0)
        xv = x_ref[...]
        u_ref[...] = (b_ref[...] + w_ref[3:4, :] * xv + w_ref[2:3, :] * _shift_down(xv, 1, row)
                      + w_ref[1:2, :] * _shift_down(xv, 2, row) + w_ref[0:1, :] * _shift_down(xv, 3, row))

    return _call(kern, name="lru_conv_fwd", grid=(NCH,),
                          in_specs=[_chunk_spec(s, LRU_NB), _vec_chunk(RG_K), _vec_chunk(1)], out_specs=_chunk_spec(s),
                          out_shape=jax.ShapeDtypeStruct((LRU_NB, s, LRU_BW), F32), compiler_params=_params(1))(gu, convw, convb)


def _lru_conv_bwd(dud, dug, gu, convw):
    s = gu.shape[1]

    def kern(d1_ref, d2_ref, x_ref, w_ref, dx_ref, dw_ref, db_ref):
        row = lax.broadcasted_iota(jnp.int32, (s, LANE), 0)
        du = d1_ref[...] + d2_ref[...]
        xv = x_ref[...]
        dx_ref[...] = (w_ref[3:4, :] * du + w_ref[2:3, :] * _shift_up(du, 1, row, s) + w_ref[1:2, :] * _shift_up(du, 2, row, s)
                       + w_ref[0:1, :] * _shift_up(du, 3, row, s)).astype(BF16)
        dw_ref[3:4, :] = jnp.sum(du * xv, axis=0, keepdims=True)
        for k in range(1, RG_K):
            dw_ref[3 - k:4 - k, :] = jnp.sum(du * _shift_down(xv, k, row), axis=0, keepdims=True)
        db_ref[...] = jnp.sum(du, axis=0, keepdims=True)

    return _call(kern, name="lru_conv_bwd", grid=(NCH,),
                          in_specs=[_chunk_spec(s), _chunk_spec(s), _chunk_spec(s, LRU_NB), _vec_chunk(RG_K)],
                          out_specs=[_chunk_spec(s), _vec_chunk(RG_K), _vec_chunk(1)],
                          out_shape=[jax.ShapeDtypeStruct((LRU_NB, s, LRU_BW), BF16),
                                     jax.ShapeDtypeStruct((RG_K, DM), F32), jax.ShapeDtypeStruct((1, DM), F32)],
                          compiler_params=_params(1))(dud, dug, gu, convw)


def _lru_gates(z_ref, bai_ref, lam_ref, uv):
    r = jax.nn.sigmoid(z_ref[0] + bai_ref[0:1, :])
    ig = jax.nn.sigmoid(z_ref[1] + bai_ref[1:2, :])
    sp = _softplus(-lam_ref[...])
    la = -RG_C * r * sp
    a = jnp.exp(la)
    sq = jnp.sqrt(_neg_expm1(2.0 * la))
    return r, ig, sp, a, sq


def _scan_steps(n):
    d, out = 1, []
    while d < n:
        out.append(d)
        d *= 2
    return out


def _lru_scan_fwd(z, bai, lam, u, gu):
    s = u.shape[1]
    zspec = BS((2, None, s, LANE), lambda ch: (0, ch // CH_PER_BLK, 0, ch % CH_PER_BLK))

    def kern(z_ref, bai_ref, lam_ref, u_ref, g_ref, hs_ref, y_ref):
        row = lax.broadcasted_iota(jnp.int32, (s, LANE), 0)
        uv = u_ref[...]
        _, ig, _, a, sq = _lru_gates(z_ref, bai_ref, lam_ref, uv)
        b = sq * (ig * uv)
        for d in _scan_steps(s):
            a_sh = jnp.where(row >= d, pltpu.roll(a, d, 0), 1.0)
            b = a * _shift_down(b, d, row) + b
            a = a * a_sh
        hs_ref[...] = b
        y_ref[...] = (_gelu(g_ref[...]) * b).astype(BF16)

    return _call(kern, name="lru_scan_fwd", grid=(NCH,),
                          in_specs=[zspec, _vec_chunk(2), _vec_chunk(1), _chunk_spec(s), _chunk_spec(s)],
                          out_specs=[_chunk_spec(s), BS((s, LANE), lambda ch: (0, ch))],
                          out_shape=[jax.ShapeDtypeStruct((LRU_NB, s, LRU_BW), F32), jax.ShapeDtypeStruct((s, DM), BF16)],
                          compiler_params=_params(1))(z, bai, lam, u, gu)


def _lru_scan_bwd(dyp, z, bai, lam, u, gu, hs):
    s = u.shape[1]
    zspec = BS((2, None, s, LANE), lambda ch: (0, ch // CH_PER_BLK, 0, ch % CH_PER_BLK))

    def kern(dy_ref, z_ref, bai_ref, lam_ref, u_ref, g_ref, hs_ref, dg_ref, dz_ref, du_ref, dbai_ref, dlam_ref):
        row = lax.broadcasted_iota(jnp.int32, (s, LANE), 0)
        uv, gv, hv, dyv = u_ref[...], g_ref[...], hs_ref[...], dy_ref[...]
        r, ig, sp, a, sq = _lru_gates(z_ref, bai_ref, lam_ref, uv)
        dg_ref[...] = (dyv * hv * _gelu_grad(gv)).astype(BF16)
        g = dyv * _gelu(gv)
        an = _shift_up(a, 1, row, s)
        for d in _scan_steps(s):
            an_sh = jnp.where(row < s - d, pltpu.roll(an, s - d, 0), 1.0)
            g = an * _shift_up(g, d, row, s) + g
            an = an * an_sh
        da = g * _shift_down(hv, 1, row)
        dsq = g * (ig * uv)
        di = g * sq * uv
        du_ref[...] = g * sq * ig
        dla = da * a - dsq * (a * a / sq)
        dzr = dla * (-RG_C * sp) * r * (1.0 - r)
        dzi = di * ig * (1.0 - ig)
        dz_ref[0] = dzr.astype(BF16)
        dz_ref[1] = dzi.astype(BF16)
        dbai_ref[0:1, :] = jnp.sum(dzr, axis=0, keepdims=True)
        dbai_ref[1:2, :] = jnp.sum(dzi, axis=0, keepdims=True)
        dlam_ref[...] = jnp.sum(dla * r, axis=0, keepdims=True) * (RG_C * jax.nn.sigmoid(-lam_ref[...]))

    return _call(
        kern, name="lru_scan_bwd", grid=(NCH,),
        in_specs=[BS((s, LANE), lambda ch: (0, ch)), zspec, _vec_chunk(2), _vec_chunk(1), _chunk_spec(s), _chunk_spec(s),
                  _chunk_spec(s)],
        out_specs=[_chunk_spec(s), zspec, _chunk_spec(s), _vec_chunk(2), _vec_chunk(1)],
        out_shape=[jax.ShapeDtypeStruct((LRU_NB, s, LRU_BW), BF16), jax.ShapeDtypeStruct((2, LRU_NB, s, LRU_BW), BF16),
                   jax.ShapeDtypeStruct((LRU_NB, s, LRU_BW), F32), jax.ShapeDtypeStruct((2, DM), F32),
                   jax.ShapeDtypeStruct((1, DM), F32)],
        compiler_params=_params(1))(dyp, z, bai, lam, u, gu, hs)


def _lru_layer_fwd(x, gpre, gpost, win, convw, convb, wai, bai, lam, wout, after=None):
    s = x.shape[0]
    tm = _tile(s, 512)
    h = _norm_fwd(x, gpre, after)
    gu = _bmm_nn("lru_in", h, win)
    u = _lru_conv_fwd(gu, convw, convb)
    z = _mm("lru_gate", u, wai, grid=(2, LRU_NB, s // tm, 1),
            a_spec=BS((None, tm, LRU_BW), lambda k, n, i, r: (n, i, 0)),
            b_spec=BS((None, None, LRU_BW, LRU_BW), lambda k, n, i, r: (k, n, 0, 0)),
            o_spec=BS((None, None, tm, LRU_BW), lambda k, n, i, r: (k, n, i, 0)),
            out_shape=(2, LRU_NB, s, LRU_BW), dn=NN)
    hs, yp = _lru_scan_fwd(z, bai, lam, u, gu)
    y = _mm_nn("lru_out", yp, wout)
    return _norm_res(x, y, gpost), (x, h, gu, u, z, hs, yp, y)


def _lru_layer_bwd(dxo, saved, gpre, gpost, win, convw, convb, wai, bai, lam, wout, after=None):
    x, h, gu, u, z, hs, yp, y = saved
    s = x.shape[0]
    tm = _tile(s, 512)
    dy, dgpost = _norm_bwd(y, dxo, gpost, None, BF16, after)
    dyp = _mm_nt("lru_dyp", dy, wout)
    dwout = _mm_tn("lru_dwout", yp, dy)
    dgate, dz, dud, dbai, dlam = _lru_scan_bwd(dyp, z, bai, lam, u, gu, hs)
    dwai = _mm("lru_dwai", u, dz, grid=(2, LRU_NB, s // tm),
               a_spec=BS((None, tm, LRU_BW), lambda k, n, r: (n, r, 0)),
               b_spec=BS((None, None, tm, LRU_BW), lambda k, n, r: (k, n, r, 0)),
               o_spec=BS((None, None, LRU_BW, LRU_BW), lambda k, n, r: (k, n, 0, 0)),
               out_shape=(2, LRU_NB, LRU_BW, LRU_BW), dn=TN)
    dug = _mm("lru_dug", dz, wai, grid=(LRU_NB, s // tm, 2),
              a_spec=BS((None, None, tm, LRU_BW), lambda n, i, k: (k, n, i, 0)),
              b_spec=BS((None, None, LRU_BW, LRU_BW), lambda n, i, k: (k, n, 0, 0)),
              o_spec=BS((None, tm, LRU_BW), lambda n, i, k: (n, i, 0)),
              out_shape=(LRU_NB, s, LRU_BW), dn=NT)
    duraw, dconvw, dconvb = _lru_conv_bwd(dud, dug, gu, convw)
    dgu = jnp.concatenate([dgate, duraw], axis=0)
    dwin = _bmm_tn("lru_dwin", h, dgu)
    dh = _bmm_nt_sum("lru_dh", dgu, win)
    dx, dgpre = _norm_bwd(x, dh, gpre, dxo, F32)
    return dx, dgpre, dgpost, dwin, dconvw, dconvb, dwai, dbai, dlam, dwout


CHIP_FLIPS = ((1, 0), (0, 1), (1, 1))


def _place():
    return lax.axis_index("x"), lax.axis_index("y"), lax.axis_index("c")


def _flip(v, f):
    return 1 - v if f else v


def _comm_params():
    return pltpu.CompilerParams(vmem_limit_bytes=VMEM_LIMIT)


def _all_gather(shards):
    n = len(shards)

    def body(*refs):
        ins, outs, stage = refs[:n], refs[n:2 * n], refs[2 * n:3 * n]
        send_sems, recv_sems, local_sems = refs[3 * n:]
        x, y, c = _place()
        me, sibling = (x, y, c), (x, y, 1 - c)
        chips = [(_flip(x, fx), _flip(y, fy)) for fx, fy in CHIP_FLIPS]

        def slot(t, p):
            return outs[t].at[:, 4 * p[0] + 2 * p[1] + p[2]]

        def copy(t, k, block, to, src=None):
            return pltpu.make_async_remote_copy(
                src_ref=slot(t, block) if src is None else src, dst_ref=slot(t, block),
                send_sem=send_sems.at[7 * t + k], recv_sem=recv_sems.at[7 * t + k], device_id=to, device_id_type=MESH)

        first = []
        for t in range(n):
            first.append(copy(t, 0, me, sibling, src=ins[t]))
            first += [copy(t, 1 + j, me, (*chip, c), src=ins[t]) for j, chip in enumerate(chips)]
        for cp in first:
            cp.start()
        load = [pltpu.make_async_copy(ins[t], stage[t], local_sems.at[t]) for t in range(n)]
        mine = [pltpu.make_async_copy(stage[t], slot(t, me), local_sems.at[t]) for t in range(n)]
        for cp in load:
            cp.start()
        for t in range(n):
            load[t].wait()
            mine[t].start()
        passed = []
        for j, chip in enumerate(chips):
            for t in range(n):
                copy(t, 1 + j, (*chip, c), me).wait_recv()
                fwd = copy(t, 4 + j, (*chip, c), sibling)
                fwd.start()
                passed.append(fwd)
        for t in range(n):
            copy(t, 0, sibling, me).wait_recv()
            for j, chip in enumerate(chips):
                copy(t, 4 + j, (*chip, 1 - c), me).wait_recv()
        for cp in first + passed:
            cp.wait_send()
        for cp in mine:
            cp.wait()

    outs = [jax.ShapeDtypeStruct((s.shape[0], NDEV) + s.shape[1:], s.dtype) for s in shards]
    return pl.pallas_call(body, name="all_gather", in_specs=[ANY] * n, out_specs=[ANY] * n, out_shape=outs,
                          scratch_shapes=[pltpu.VMEM(s.shape, s.dtype) for s in shards]
                          + [pltpu.SemaphoreType.DMA((7 * n,)), pltpu.SemaphoreType.DMA((7 * n,)),
                             pltpu.SemaphoreType.DMA((n,))],
                          compiler_params=_comm_params())(*shards)


def _small_gather(v):
    def body(v_ref, o_ref, send_sems, recv_sems, local_sem):
        x, y, c = _place()
        mine = 4 * x + 2 * y + c
        local = pltpu.make_async_copy(v_ref, o_ref.at[mine], local_sem)
        local.start()
        sends = []
        for k in range(1, NDEV):
            fx, fy, fc = (k >> 2) & 1, (k >> 1) & 1, k & 1
            sends.append(pltpu.make_async_remote_copy(
                src_ref=v_ref, dst_ref=o_ref.at[mine], send_sem=send_sems.at[k - 1], recv_sem=recv_sems.at[k - 1],
                device_id=(_flip(x, fx), _flip(y, fy), _flip(c, fc)), device_id_type=MESH))
        for cp in sends:
            cp.start()
        for k in range(1, NDEV):
            fx, fy, fc = (k >> 2) & 1, (k >> 1) & 1, k & 1
            src = 4 * _flip(x, fx) + 2 * _flip(y, fy) + _flip(c, fc)
            pltpu.make_async_remote_copy(src_ref=v_ref, dst_ref=o_ref.at[src], send_sem=send_sems.at[k - 1],
                                         recv_sem=recv_sems.at[k - 1], device_id=(x, y, c), device_id_type=MESH).wait_recv()
        for cp in sends:
            cp.wait_send()
        local.wait()

    return pl.pallas_call(body, name="small_gather", in_specs=[ANY], out_specs=ANY,
                          out_shape=jax.ShapeDtypeStruct((NDEV,) + v.shape, v.dtype),
                          scratch_shapes=[pltpu.SemaphoreType.DMA((NDEV - 1,)), pltpu.SemaphoreType.DMA((NDEV - 1,)),
                                          pltpu.SemaphoreType.DMA],
                          compiler_params=_comm_params())(v)


REL_CHIPS = ((0, 0),) + CHIP_FLIPS


def _rs_d2d(g5s, after=None):
    n = len(g5s)
    extra = () if after is None else (after,)

    def body(*refs):
        ins, gots = refs[:n], refs[n + len(extra):2 * n + len(extra)]
        send_sems, recv_sems = refs[2 * n + len(extra):]
        x, y, c = _place()
        copies = []
        for t in range(n):
            for f, (fx, fy) in enumerate(REL_CHIPS):
                copies.append(pltpu.make_async_remote_copy(
                    src_ref=ins[t].at[_flip(x, fx), _flip(y, fy), 1 - c], dst_ref=gots[t].at[f],
                    send_sem=send_sems.at[4 * t + f], recv_sem=recv_sems.at[4 * t + f], device_id=(x, y, 1 - c),
                    device_id_type=MESH))
        for cp in copies:
            cp.start()
        for cp in copies:
            cp.wait()

    out = [jax.ShapeDtypeStruct((4,) + g.shape[3:], F32) for g in g5s]
    return pl.pallas_call(body, name="rs_d2d", in_specs=[ANY] * (n + len(extra)), out_specs=[ANY] * n, out_shape=out,
                          scratch_shapes=[pltpu.SemaphoreType.DMA((4 * n,)), pltpu.SemaphoreType.DMA((4 * n,))],
                          compiler_params=_comm_params())(*g5s, *extra)


def _rs_ici(parts):
    n = len(parts)

    def body(*refs):
        ins, outs = refs[:n], refs[n:2 * n]
        send_sems, recv_sems = refs[2 * n:]
        x, y, c = _place()
        copies = []
        for t in range(n):
            for f, (fx, fy) in enumerate(CHIP_FLIPS):
                copies.append(pltpu.make_async_remote_copy(
                    src_ref=ins[t].at[f], dst_ref=outs[t].at[f], send_sem=send_sems.at[3 * t + f],
                    recv_sem=recv_sems.at[3 * t + f], device_id=(_flip(x, fx), _flip(y, fy), c), device_id_type=MESH))
        for cp in copies:
            cp.start()
        for cp in copies:
            cp.wait()

    out = [jax.ShapeDtypeStruct(p.shape, p.dtype) for p in parts]
    return pl.pallas_call(body, name="rs_ici", in_specs=[ANY] * n, out_specs=[ANY] * n, out_shape=out,
                          scratch_shapes=[pltpu.SemaphoreType.DMA((3 * n,)), pltpu.SemaphoreType.DMA((3 * n,))],
                          compiler_params=_comm_params())(*parts)


HBM = pl.BlockSpec(memory_space=pltpu.HBM)
SEM = pl.BlockSpec(memory_space=pltpu.SEMAPHORE)
EFFECT = pltpu.SideEffectType.DATAFLOW_SIDE_EFFECTING


def _in_hbm(a):
    return pltpu.with_memory_space_constraint(a, pltpu.HBM)


def _rs_ici_copies(ins, lands, send_sems, recv_sems):
    x, y, c = _place()
    return [pltpu.make_async_remote_copy(
        src_ref=ins[t].at[f], dst_ref=lands[t].at[f], send_sem=send_sems.at[3 * t + f], recv_sem=recv_sems.at[3 * t + f],
        device_id=(_flip(x, fx), _flip(y, fy), c), device_id_type=MESH)
        for t in range(len(ins)) for f, (fx, fy) in enumerate(CHIP_FLIPS)]


def _rs_ici_start(parts, name):
    n = len(parts)

    def body(*refs):
        ins, lands = refs[:n], refs[n:2 * n]
        send_sems, recv_sems = refs[2 * n], refs[2 * n + 1]
        token = refs[-1]
        for cp in _rs_ici_copies(ins, lands, send_sems, recv_sems):
            cp.start()
        token[...] = jnp.zeros(token.shape, token.dtype)

    thru = [pltpu.HBM(p.shape, p.dtype) for p in parts]
    res = pl.pallas_call(
        body, name=name, in_specs=[HBM] * (2 * n),
        out_shape=(pltpu.SemaphoreType.DMA((3 * n,)), pltpu.SemaphoreType.DMA((3 * n,)), *thru, *thru,
                   jax.ShapeDtypeStruct((8, LANE), F32)),
        out_specs=(SEM, SEM, *([HBM] * (2 * n)), pl.BlockSpec(memory_space=pltpu.VMEM)),
        input_output_aliases={i: 2 + i for i in range(2 * n)},
        compiler_params=pltpu.CompilerParams(has_side_effects=EFFECT),
    )(*[_in_hbm(p) for p in parts], *[_in_hbm(lax.empty(p.shape, p.dtype)) for p in parts])
    return res[:-1], res[-1]


def _rs_ici_wait(state, after, name):
    n = (len(state) - 2) // 2

    def body(*refs):
        send_sems, recv_sems = refs[0], refs[1]
        ins, lands = refs[2:2 + n], refs[2 + n:2 + 2 * n]
        for cp in _rs_ici_copies(ins, lands, send_sems, recv_sems):
            cp.wait_send()
            cp.wait_recv()

    thru = [pltpu.HBM(s.shape, s.dtype) for s in state[2:]]
    res = pl.pallas_call(
        body, name=name, in_specs=[SEM, SEM] + [HBM] * (2 * n) + [ANY], out_shape=tuple(thru),
        out_specs=tuple([HBM] * (2 * n)), input_output_aliases={2 + i: i for i in range(2 * n)},
        compiler_params=pltpu.CompilerParams(has_side_effects=EFFECT),
    )(*state, after)
    return list(res[n:])


def _ag_copies(shards, lands, send_sems, recv_sems):
    x, y, c = _place()
    mine = 4 * x + 2 * y + c
    peers = [(x, y, 1 - c)] + [(_flip(x, fx), _flip(y, fy), c) for fx, fy in CHIP_FLIPS]
    return [pltpu.make_async_remote_copy(
        src_ref=shards[t], dst_ref=lands[t].at[mine], send_sem=send_sems.at[4 * t + k], recv_sem=recv_sems.at[4 * t + k],
        device_id=peer, device_id_type=MESH) for t in range(len(shards)) for k, peer in enumerate(peers)]


def _ag_start(shards, after, name):
    n = len(shards)

    def body(*refs):
        ins, lands = refs[:n], refs[n:2 * n]
        send_sems, recv_sems = refs[2 * n + 1], refs[2 * n + 2]
        token = refs[-1]
        for cp in _ag_copies(ins, lands, send_sems, recv_sems):
            cp.start()
        token[...] = jnp.zeros(token.shape, token.dtype)

    thru = [pltpu.HBM(s.shape, s.dtype) for s in shards]
    land = [pltpu.HBM((NDEV,) + s.shape, s.dtype) for s in shards]
    res = pl.pallas_call(
        body, name=name, in_specs=[HBM] * (2 * n) + [ANY],
        out_shape=(pltpu.SemaphoreType.DMA((4 * n,)), pltpu.SemaphoreType.DMA((4 * n,)), *thru, *land,
                   jax.ShapeDtypeStruct((8, LANE), F32)),
        out_specs=(SEM, SEM, *([HBM] * (2 * n)), pl.BlockSpec(memory_space=pltpu.VMEM)),
        input_output_aliases={i: 2 + i for i in range(2 * n)},
        compiler_params=pltpu.CompilerParams(has_side_effects=EFFECT),
    )(*[_in_hbm(s) for s in shards], *[_in_hbm(lax.empty((NDEV,) + s.shape, s.dtype)) for s in shards], after)
    return res[:-1], res[-1]


def _ag_wait(state, after, name):
    n = (len(state) - 2) // 2

    def body(*refs):
        send_sems, recv_sems = refs[0], refs[1]
        ins, lands = refs[2:2 + n], refs[2 + n:2 + 2 * n]
        for cp in _ag_copies(ins, lands, send_sems, recv_sems):
            cp.wait_send()
            cp.wait_recv()

    thru = [pltpu.HBM(s.shape, s.dtype) for s in state[2:]]
    res = pl.pallas_call(
        body, name=name, in_specs=[SEM, SEM] + [HBM] * (2 * n) + [ANY], out_shape=tuple(thru),
        out_specs=tuple([HBM] * (2 * n)), input_output_aliases={2 + i: i for i in range(2 * n)},
        compiler_params=pltpu.CompilerParams(has_side_effects=EFFECT),
    )(*state, after)
    return list(res[:n]), list(res[n:])


def _ag_finish(shards, lands):
    n = len(shards)

    def body(*refs):
        ins, outs, stage = refs[:n], refs[2 * n:3 * n], refs[3 * n:4 * n]
        send_sems, recv_sems, local_sems = refs[4 * n:]
        x, y, c = _place()
        chips = [(_flip(x, fx), _flip(y, fy)) for fx, fy in CHIP_FLIPS]

        def passing(t, j, core, to):
            blk = outs[t].at[4 * chips[j][0] + 2 * chips[j][1] + core]
            return pltpu.make_async_remote_copy(src_ref=blk, dst_ref=blk, send_sem=send_sems.at[3 * t + j],
                                                recv_sem=recv_sems.at[3 * t + j], device_id=to, device_id_type=MESH)

        sends = [passing(t, j, c, (x, y, 1 - c)) for t in range(n) for j in range(3)]
        for cp in sends:
            cp.start()
        load = [pltpu.make_async_copy(ins[t], stage[t], local_sems.at[t]) for t in range(n)]
        mine = [pltpu.make_async_copy(stage[t], outs[t].at[4 * x + 2 * y + c], local_sems.at[t]) for t in range(n)]
        for cp in load:
            cp.start()
        for t in range(n):
            load[t].wait()
            mine[t].start()
        for t in range(n):
            for j in range(3):
                passing(t, j, 1 - c, (x, y, c)).wait_recv()
        for cp in sends:
            cp.wait_send()
        for cp in mine:
            cp.wait()

    return pl.pallas_call(
        body, name="ag_finish", in_specs=[ANY] * (2 * n), out_specs=[ANY] * n,
        out_shape=[jax.ShapeDtypeStruct(l.shape, l.dtype) for l in lands],
        input_output_aliases={n + i: i for i in range(n)},
        scratch_shapes=[pltpu.VMEM(s.shape, s.dtype) for s in shards]
        + [pltpu.SemaphoreType.DMA((3 * n,)), pltpu.SemaphoreType.DMA((3 * n,)), pltpu.SemaphoreType.DMA((n,))],
        compiler_params=_comm_params())(*shards, *lands)


def _row_tile(rows, largest=256):
    for t in (1024, 512, 256, 128, 64, 32, 16, 8):
        if t > largest:
            continue
        if rows % t == 0:
            return t
    return rows


def _rs_chip_sum(pos, g5, got):
    a, b = g5.shape[3:]
    ta = _row_tile(a, 1024)

    def kern(pos_ref, o_ref, g_ref, p_ref):
        p_ref[...] = (o_ref[...] + g_ref[...]).astype(BF16)

    def mine(f, i, pos_ref):
        return (pos_ref[0] ^ ((f + 1) & 1), pos_ref[1] ^ ((f + 1) >> 1), pos_ref[2], i, 0)

    spec = pltpu.PrefetchScalarGridSpec(
        num_scalar_prefetch=1, grid=(3, a // ta),
        in_specs=[BS((None, None, None, ta, b), mine), BS((None, ta, b), lambda f, i, pos_ref: (f + 1, i, 0))],
        out_specs=BS((None, ta, b), lambda f, i, pos_ref: (f, i, 0)))
    return _call(kern, name="rs_chip_sum", grid_spec=spec, out_shape=jax.ShapeDtypeStruct((3, a, b), BF16),
                          compiler_params=_params(2))(pos, g5, got)


def _rs_final_sum(pos, g5, got, recv):
    a, b = g5.shape[3:]
    ta = _row_tile(a, 1024)

    def kern(pos_ref, o_ref, g_ref, r_ref, s_ref):
        acc = o_ref[...] + g_ref[...]
        for f in range(3):
            acc = acc + r_ref[f].astype(F32)
        s_ref[...] = acc

    spec = pltpu.PrefetchScalarGridSpec(
        num_scalar_prefetch=1, grid=(a // ta,),
        in_specs=[BS((None, None, None, ta, b), lambda i, pos_ref: (pos_ref[0], pos_ref[1], pos_ref[2], i, 0)),
                  BS((None, ta, b), lambda i, pos_ref: (0, i, 0)), BS((3, ta, b), lambda i, pos_ref: (0, i, 0))],
        out_specs=BS((ta, b), lambda i, pos_ref: (i, 0)))
    return _call(kern, name="rs_final_sum", grid_spec=spec, out_shape=jax.ShapeDtypeStruct((a, b), F32),
                          compiler_params=_params(1))(pos, g5, got, recv)


def _reduce_scatter(grads, pos):
    g5s = [g.reshape((2, 2, 2) + g.shape[1:]) for g in grads]
    gots = _rs_d2d(g5s)
    parts = [_rs_chip_sum(pos, g, got) for g, got in zip(g5s, gots)]
    recvs = _rs_ici(parts)
    return [_rs_final_sum(pos, g, got, r) for g, got, r in zip(g5s, gots, recvs)]


def _rs_d2d_copies(ins, lands, send_sems, recv_sems):
    x, y, c = _place()
    return [pltpu.make_async_remote_copy(
        src_ref=ins[t].at[_flip(x, fx), _flip(y, fy), 1 - c], dst_ref=lands[t].at[f], send_sem=send_sems.at[4 * t + f],
        recv_sem=recv_sems.at[4 * t + f], device_id=(x, y, 1 - c), device_id_type=MESH)
        for t in range(len(ins)) for f, (fx, fy) in enumerate(REL_CHIPS)]


def _rs_d2d_start(g5s, name):
    n = len(g5s)

    def body(*refs):
        ins, lands = refs[:n], refs[n:2 * n]
        for cp in _rs_d2d_copies(ins, lands, refs[2 * n], refs[2 * n + 1]):
            cp.start()
        refs[-1][...] = jnp.zeros(refs[-1].shape, F32)

    thru = [pltpu.HBM(g.shape, g.dtype) for g in g5s]
    land = [pltpu.HBM((4,) + g.shape[3:], F32) for g in g5s]
    res = pl.pallas_call(
        body, name=name, in_specs=[HBM] * (2 * n),
        out_shape=(pltpu.SemaphoreType.DMA((4 * n,)), pltpu.SemaphoreType.DMA((4 * n,)), *thru, *land,
                   jax.ShapeDtypeStruct((8, LANE), F32)),
        out_specs=(SEM, SEM, *([HBM] * (2 * n)), pl.BlockSpec(memory_space=pltpu.VMEM)),
        input_output_aliases={i: 2 + i for i in range(2 * n)},
        compiler_params=pltpu.CompilerParams(has_side_effects=EFFECT),
    )(*[_in_hbm(g) for g in g5s], *[_in_hbm(lax.empty((4,) + g.shape[3:], F32)) for g in g5s])
    return res[:-1], res[-1]


def _rs_d2d_wait(state, after, name):
    n = (len(state) - 2) // 2

    def body(*refs):
        ins, lands = refs[2:2 + n], refs[2 + n:2 + 2 * n]
        for cp in _rs_d2d_copies(ins, lands, refs[0], refs[1]):
            cp.wait_send()
            cp.wait_recv()

    thru = [pltpu.HBM(s.shape, s.dtype) for s in state[2:]]
    res = pl.pallas_call(
        body, name=name, in_specs=[SEM, SEM] + [HBM] * (2 * n) + [ANY], out_shape=tuple(thru),
        out_specs=tuple([HBM] * (2 * n)), input_output_aliases={2 + i: i for i in range(2 * n)},
        compiler_params=pltpu.CompilerParams(has_side_effects=EFFECT),
    )(*state, after)
    return list(res[:n]), list(res[n:])


def _as_g5(grads):
    return [g.reshape((2, 2, 2) + g.shape[1:]) for g in grads]


def _rs_mid(g5s, gots, pos, tag):
    parts = [_rs_chip_sum(pos, g, got) for g, got in zip(g5s, gots)]
    state, token = _rs_ici_start(parts, "rs_ici_start_" + tag)
    return (g5s, gots, state, tag), token


def _rs_begin(grads, pos, tag, after=None):
    g5s = _as_g5(grads)
    return _rs_mid(g5s, _rs_d2d(g5s, after), pos, tag)


def _rs_end(pending, after, pos):
    g5s, gots, state, tag = pending
    recvs = _rs_ici_wait(state, after, "rs_ici_wait_" + tag)
    return [_rs_final_sum(pos, g, got, r) for g, got, r in zip(g5s, gots, recvs)]


def _sum_devices(v):
    _, r, _ = v.shape

    def kern(v_ref, o_ref):
        acc = v_ref[0]
        for d in range(1, NDEV):
            acc = acc + v_ref[d]
        o_ref[...] = acc

    return _call(kern, name="sum_devices", out_shape=jax.ShapeDtypeStruct((r, LANE), F32),
                          compiler_params=_comm_params())(v)


def _loss_head(xf, target):
    s = xf.shape[0]
    tm = _tile(s, 512)

    def kern(x_ref, t_ref, dx_ref, l_ref):
        err = x_ref[...] - t_ref[...]
        dx_ref[...] = err * (1.0 / DM)
        part = jnp.broadcast_to(0.5 * jnp.sum(jnp.mean(err * err, axis=-1, keepdims=True), axis=0, keepdims=True), (8, LANE))

        @pl.when(pl.program_id(0) == 0)
        def _():
            l_ref[...] = part

        @pl.when(pl.program_id(0) > 0)
        def _():
            l_ref[...] += part

    row = BS((tm, DM), lambda i: (i, 0))
    return _call(kern, name="loss_head", grid=(s // tm,), in_specs=[row, row],
                          out_specs=[row, BS((8, LANE), lambda i: (0, 0))],
                          out_shape=[jax.ShapeDtypeStruct((s, DM), F32), jax.ShapeDtypeStruct((8, LANE), F32)],
                          compiler_params=_params(1))(xf, target)


def _adamw(w, g, m, v, after=None):
    rows, cols = w.shape
    tr = _row_tile(rows)
    extra = () if after is None else (after,)

    def kern(w_ref, g_ref, m_ref, v_ref, *rest):
        d_ref, nm_ref, nv_ref = rest[-3:]
        gv = g_ref[...]
        nm = ADAM_B1 * m_ref[...] + (1.0 - ADAM_B1) * gv
        nv = ADAM_B2 * v_ref[...] + (1.0 - ADAM_B2) * (gv * gv)
        m_hat = nm / (1.0 - ADAM_B1 ** ADAM_STEP)
        v_hat = nv / (1.0 - ADAM_B2 ** ADAM_STEP)
        d_ref[...] = -ADAM_LR * (m_hat / (jnp.sqrt(v_hat) + ADAM_EPS) + ADAM_WD * w_ref[...])
        nm_ref[...] = nm
        nv_ref[...] = nv

    blk = BS((tr, cols), lambda i: (i, 0))
    shp = jax.ShapeDtypeStruct((rows, cols), F32)
    return _call(kern, name="adamw", grid=(rows // tr,), in_specs=[blk] * 4 + [ANY] * len(extra),
                          out_specs=[blk] * 3, out_shape=[shp] * 3, compiler_params=_params(1))(w, g, m, v, *extra)


def _adamw_nd(w, g, m, v, after=None):
    shape = w.shape
    two = (math.prod(shape[:-1]), shape[-1])
    return tuple(o.reshape(shape)
                 for o in _adamw(w.reshape(two), g.reshape(two), m.reshape(two), v.reshape(two), after))


def _pack_small(parts):
    flat = jnp.concatenate([p.reshape(-1) for p in parts])
    pad = (-flat.shape[0]) % (8 * LANE)
    return jnp.pad(flat, (0, pad)).reshape(-1, LANE)


def _unpack_small(packed, shapes, lead=()):
    flat = packed.reshape(lead + (-1,))
    out, off = [], 0
    for shp in shapes:
        n = math.prod(shp)
        out.append(flat[..., off:off + n].reshape(lead + tuple(shp)))
        off += n
    return out


def _blocks_of_columns(w):
    k, n = w.shape
    return w.reshape(k, NDEV, n // NDEV).transpose(1, 0, 2)


def _columns_of_blocks(wb):
    n, k, c = wb.shape
    return wb.transpose(1, 0, 2).reshape(k, n * c)


WEIGHT_NAMES = ('g_mix_pre', 'g_mix_post', 'g_cross_pre', 'g_mem', 'g_cross_post', 'g_ffn_pre', 'g_ffn_post', 'w_xq',
                'w_xkv', 'w_xo', 'w_ffn_gu', 'w_ffn_down', 'ab_w_in', 'ab_b_f', 'ab_conv_w', 'ab_w_out', 'c_w_in',
                'c_conv_w', 'c_conv_b', 'c_w_a', 'c_b_a', 'c_w_i', 'c_b_i', 'c_lam', 'c_w_out')
BIG = ('w_xq', 'w_xkv', 'w_xo', 'w_ffn_gu', 'w_ffn_down', 'ab_w_in', 'ab_w_out', 'c_w_in', 'c_w_a', 'c_w_i', 'c_w_out')
SMALL_SHARDED = ('ab_conv_w', 'c_conv_w', 'c_conv_b', 'c_b_a', 'c_b_i', 'c_lam')
REPLICATED = ('g_mix_pre', 'g_mix_post', 'g_cross_pre', 'g_mem', 'g_cross_post', 'g_ffn_pre', 'g_ffn_post', 'ab_b_f')


def _small_full(name, gathered):
    nd = gathered.ndim
    return jnp.moveaxis(gathered, 0, nd - 2).reshape(gathered.shape[1:-1] + (NDEV * gathered.shape[-1],))


def _small_shard(full, dev):
    c = full.shape[-1] // NDEV
    return lax.dynamic_slice_in_dim(full, dev * c, c, axis=full.ndim - 1)


def kernel(x, mem, g_mix_pre, g_mix_post, g_cross_pre, g_mem, g_cross_post, g_ffn_pre, g_ffn_post, w_xq, w_xkv, w_xo, w_ffn_gu, w_ffn_down, ab_w_in, ab_b_f, ab_conv_w, ab_w_out, c_w_in, c_conv_w, c_conv_b, c_w_a, c_b_a, c_w_i, c_b_i, c_lam, c_w_out, loss_target, m_g_mix_pre, m_g_mix_post, m_g_cross_pre, m_g_mem, m_g_cross_post, m_g_ffn_pre, m_g_ffn_post, m_w_xq, m_w_xkv, m_w_xo, m_w_ffn_gu, m_w_ffn_down, m_ab_w_in, m_ab_b_f, m_ab_conv_w, m_ab_w_out, m_c_w_in, m_c_conv_w, m_c_conv_b, m_c_w_a, m_c_b_a, m_c_w_i, m_c_b_i, m_c_lam, m_c_w_out, v_g_mix_pre, v_g_mix_post, v_g_cross_pre, v_g_mem, v_g_cross_post, v_g_ffn_pre, v_g_ffn_post, v_w_xq, v_w_xkv, v_w_xo, v_w_ffn_gu, v_w_ffn_down, v_ab_w_in, v_ab_b_f, v_ab_conv_w, v_ab_w_out, v_c_w_in, v_c_conv_w, v_c_conv_b, v_c_w_a, v_c_b_a, v_c_w_i, v_c_b_i, v_c_lam, v_c_w_out):
    args = locals()
    w = {n: args[n] for n in WEIGHT_NAMES}
    mom = {n: args["m_" + n] for n in WEIGHT_NAMES}
    var = {n: args["v_" + n] for n in WEIGHT_NAMES}
    for t in (w, mom, var):
        t['w_ffn_gu'] = t['w_ffn_gu'].transpose(0, 2, 1)
    pos = jnp.stack([lax.axis_index("x"), lax.axis_index("y"), lax.axis_index("c")]).astype(jnp.int32)
    dev = 4 * pos[0] + 2 * pos[1] + pos[2]
    xs, mems, target = x[0], mem[0], loss_target[0]
    n_even, n_odd = (DEPTH + 1) // 2, DEPTH // 2

    small_shapes = [w[n].shape for n in SMALL_SHARDED]
    small_w_all = _small_gather(_pack_small([w[n] for n in SMALL_SHARDED]))
    gathered_small = _unpack_small(small_w_all, small_shapes, (NDEV,))
    small = {n: _small_full(n, g) for n, g in zip(SMALL_SHARDED, gathered_small)}
    ab_bfb = jnp.broadcast_to(ab_b_f[:, :, None], (n_even, FOX_H, LANE))
    c_bai = jnp.stack([small['c_b_a'].reshape(n_odd, DM), small['c_b_i'].reshape(n_odd, DM)], axis=1)
    row = lambda a, l: a[l][None]

    REST = ('w_xq', 'w_xkv', 'w_xo', 'w_ffn_gu', 'w_ffn_down')

    def mixer_names(l):
        return ('ab_w_in', 'ab_w_out') if l % 2 == 0 else ('c_w_in', 'c_w_a', 'c_w_i', 'c_w_out')

    def shards_of(l, names):
        out = []
        for n in names:
            s = w[n][l if w[n].shape[0] == DEPTH else l // 2].astype(BF16)
            out.append(s.reshape(-1, s.shape[-1]))
        return out

    def mixer_weights(l, full):
        if l % 2 == 0:
            e = l // 2
            return (row(g_mix_pre, l), row(g_mix_post, l), _ab_pack(_columns_of_blocks(full['ab_w_in'])), ab_bfb[e],
                    small['ab_conv_w'][e], full['ab_w_out'].reshape(DM, DM))
        o = l // 2
        gate_w = lambda g: g.reshape(NDEV, LRU_NB, LRU_BW // NDEV, LRU_BW).transpose(1, 0, 2, 3).reshape(
            LRU_NB, LRU_BW, LRU_BW)
        return (row(g_mix_pre, l), row(g_mix_post, l), full['c_w_in'], small['c_conv_w'][o], row(small['c_conv_b'], o),
                jnp.stack([gate_w(full['c_w_a']), gate_w(full['c_w_i'])]), c_bai[o], row(small['c_lam'], o),
                full['c_w_out'].reshape(DM, DM))

    def rest_weights(l, full):
        cross = (row(g_cross_pre, l), row(g_mem, l), row(g_cross_post, l), full['w_xq'].reshape(DM, DM), full['w_xkv'],
                 full['w_xo'].reshape(DM, DM))
        ffn = (row(g_ffn_pre, l), row(g_ffn_post, l), full['w_ffn_gu'], full['w_ffn_down'].reshape(D_FF, DM))
        return cross, ffn

    def gathered(state, names, after, tag):
        shards, lands = _ag_wait(state, after, "ag_wait_" + tag)
        full = _ag_finish(shards, lands)
        return dict(zip(names, full)), full[0]

    saved, weights = [], []
    h = xs
    names_of = lambda l: mixer_names(l) + REST
    states = {}
    st_m, _ = _ag_start(shards_of(0, mixer_names(0)), small_w_all, "ag_start_0m")
    st_r, _ = _ag_start(shards_of(0, REST), st_m[2], "ag_start_0r")
    states[1], token = _ag_start(shards_of(1, names_of(1)), st_r[2], "ag_start_1")
    full_m, _ = gathered(st_m, mixer_names(0), xs, "0m")
    for l in range(DEPTH):
        if l > 0:
            full, done = gathered(states[l], names_of(l), h, str(l))
            full_m = full_r = full
            token = None
            if l + 2 < DEPTH:
                states[l + 2], token = _ag_start(shards_of(l + 2, names_of(l + 2)), done, "ag_start_%d" % (l + 2))
        mixer = mixer_weights(l, full_m)
        h, s_mix = (_fox_layer_fwd if l % 2 == 0 else _lru_layer_fwd)(h, *mixer, after=token)
        token = None
        if l == 0:
            full_r, done = gathered(st_r, REST, h, "0r")
            states[2], token = _ag_start(shards_of(2, names_of(2)), done, "ag_start_2")
        cross, ffn = rest_weights(l, full_r)
        h, s_cross = _cross_fwd(h, mems, *cross, after=token)
        h, s_ffn = _ffn_fwd(h, *ffn)
        saved.append((s_mix, s_cross, s_ffn))
        weights.append((mixer, cross, ffn))
    mixer_args = lambda l: weights[l][0]
    cross_args = lambda l: weights[l][1]
    ffn_args = lambda l: weights[l][2]
    dx, loss_rep = _loss_head(h, target)
    loss = lax.psum(loss_rep[0, 0], ("x", "y", "c"))

    grads = {n: [None] * w[n].shape[0] for n in BIG}
    partial = {n: [None] * w[n].shape[0] for n in REPLICATED + SMALL_SHARDED}
    def finish(pending, after):
        state, names, where = pending
        for n, g in zip(names, _rs_end(state, after, pos)):
            grads[n][where[n]] = g

    def unit(layer, names):
        return [layer[n][1] for n in names], names, {n: layer[n][0] for n in names}

    d2d = ici = None
    token = None
    for l in reversed(range(DEPTH)):
        s_mix, s_cross, s_ffn = saved[l]
        dx, partial['g_ffn_pre'][l], partial['g_ffn_post'][l], dwgu, dwd = _ffn_bwd(dx, s_ffn, *ffn_args(l), after=token)
        token = None
        if d2d is not None:
            g5s, gots = _rs_d2d_wait(d2d[0], dx, "rs_d2d_wait_%d" % (l + 1))
            state, token = _rs_mid(g5s, gots, pos, str(l + 1))
            ici, d2d = (state,) + d2d[1:], None
        (dx, partial['g_cross_pre'][l], partial['g_mem'][l], partial['g_cross_post'][l], dwq, dwkv, dwo) = _cross_bwd(
            dx, s_cross, mems, *cross_args(l), after=token)
        token = None
        layer = {'w_xq': (l, dwq.reshape(NDEV, DM // NDEV, DM)), 'w_xkv': (l, dwkv), 'w_xo': (l, dwo.reshape(NDEV, DM // NDEV, DM)),
                 'w_ffn_gu': (l, dwgu), 'w_ffn_down': (l, dwd.reshape(NDEV, D_FF // NDEV, DM))}
        if l == 0:
            gs, names, where = unit(layer, REST)
            state, token = _rs_begin(gs, pos, "0r")
            ici_rest = (state, names, where)
        if l % 2 == 0:
            e = l // 2
            (dx, partial['g_mix_pre'][l], partial['g_mix_post'][l], dwall, partial['ab_b_f'][e], partial['ab_conv_w'][e],
             dwout) = _fox_layer_bwd(dx, s_mix, *mixer_args(l), after=token)
            layer['ab_w_in'] = (e, _blocks_of_columns(_ab_unpack(dwall)))
            layer['ab_w_out'] = (e, dwout.reshape(NDEV, DM // NDEV, DM))
        else:
            o = l // 2
            (dx, partial['g_mix_pre'][l], partial['g_mix_post'][l], dwin, partial['c_conv_w'][o], dconvb, dwai, dbai, dlam,
             dwout) = _lru_layer_bwd(dx, s_mix, *mixer_args(l), after=token)
            partial['c_conv_b'][o], partial['c_lam'][o] = dconvb[0], dlam[0]
            partial['c_b_a'][o], partial['c_b_i'][o] = dbai[0].reshape(LRU_NB, LRU_BW), dbai[1].reshape(LRU_NB, LRU_BW)
            rows = LRU_BW // NDEV
            by_dev = lambda d: d.reshape(LRU_NB, NDEV, rows, LRU_BW).transpose(1, 0, 2, 3).reshape(NDEV, LRU_NB * rows, LRU_BW)
            layer['c_w_in'] = (o, dwin)
            layer['c_w_a'] = (o, by_dev(dwai[0]))
            layer['c_w_i'] = (o, by_dev(dwai[1]))
            layer['c_w_out'] = (o, dwout.reshape(NDEV, DM // NDEV, DM))
        token = None
        if ici is not None:
            finish(ici, dx)
            ici = None
        if l > 0:
            gs, names, where = unit(layer, list(layer))
            state, token = _rs_d2d_start(_as_g5(gs), "rs_d2d_start_%d" % l)
            d2d = (state, names, where)
    small_names = REPLICATED + SMALL_SHARDED
    small_parts = [jnp.stack([p.reshape(w[n].shape[1:] if n in REPLICATED else small[n].shape[1:]) for p in partial[n]])
                   for n in small_names]
    small_all = _small_gather(_pack_small(small_parts))
    reduced = _unpack_small(_sum_devices(small_all), [p.shape for p in small_parts])
    grad = {}
    for n, g in zip(small_names, reduced):
        grad[n] = g if n in REPLICATED else _small_shard(g, dev)

    gs, names, where = unit(layer, mixer_names(0))
    state, token = _rs_begin(gs, pos, "0m", after=small_all)
    ici_mixer = (state, names, where)
    finish(ici_rest, dx)

    delta, new_m, new_v = {}, {}, {}
    last = mixer_names(0)
    for n in BIG:
        if n not in last:
            grad[n] = jnp.stack(grads[n]).reshape(w[n].shape)
            delta[n], new_m[n], new_v[n] = _adamw_nd(w[n], grad[n], mom[n], var[n], token)
            token = delta[n]
    shapes = [w[n].shape for n in small_names]
    packed = [_pack_small([t[n] for n in small_names]) for t in (w, grad, mom, var)]
    res_small = _adamw(*packed, after=token)
    for res, out in zip(res_small, (delta, new_m, new_v)):
        for n, val in zip(small_names, _unpack_small(res, shapes)):
            out[n] = val
    finish(ici_mixer, res_small[0])
    for n in last:
        grad[n] = jnp.stack(grads[n]).reshape(w[n].shape)
        delta[n], new_m[n], new_v[n] = _adamw_nd(w[n], grad[n], mom[n], var[n])

    for t in (grad, delta, new_m, new_v):
        t['w_ffn_gu'] = t['w_ffn_gu'].transpose(0, 2, 1)
    return (loss, dx[None], *[grad[n] for n in WEIGHT_NAMES], *[delta[n] for n in WEIGHT_NAMES],
            *[new_m[n] for n in WEIGHT_NAMES], *[new_v[n] for n in WEIGHT_NAMES])
```

```python
import functools
import math

import jax
import jax.numpy as jnp
from jax import lax
from jax.experimental import pallas as pl
from jax.experimental.pallas import tpu as pltpu

F32 = jnp.float32
BF16 = jnp.bfloat16
BS = pl.BlockSpec
ANY = pl.BlockSpec(memory_space=pl.ANY)
MESH = pl.DeviceIdType.MESH

DM = 1024
DEPTH = 4
EPS = 1e-6
NEG = -1e30
FOX_W = 512
FOX_HD = 64
FOX_H = 8
SC_W = 512
SC_K = 3
AB_IN = 3 * FOX_W + FOX_H + 3 * SC_W
AB_PAD = 3200
LRU_BW = 256
LRU_NB = 4
RG_K = 4
RG_C = 8.0
MEM_H = 4
MEM_HD = 256
D_FF = 2816
NDEV = 8
FFB = 2 * D_FF // NDEV
ADAM_LR, ADAM_B1, ADAM_B2, ADAM_EPS, ADAM_WD, ADAM_STEP = 0.001, 0.9, 0.999, 1e-08, 0.01, 10

LANE = 128
VMEM_LIMIT = 24 * 1024 * 1024
VMEM_BIG = 48 * 1024 * 1024


def _params(ngrid, vmem=None):
    return pltpu.CompilerParams(dimension_semantics=("arbitrary",) * ngrid, vmem_limit_bytes=vmem or VMEM_LIMIT)


def _call(kern, **kwargs):
    return pl.pallas_call(kern, **kwargs)


TK_RED = 2048
TM_SUM = 512


def _tile(n, t):
    return t if n % t == 0 else n


def _mm(name, a, b, *, grid, a_spec, b_spec, o_spec, out_shape, dn, out_dtype=F32):
    nred = grid[-1]
    ngrid = len(grid)

    def kern(a_ref, b_ref, o_ref, *scratch):
        p = lax.dot_general(a_ref[...].astype(BF16), b_ref[...].astype(BF16), (dn, ((), ())),
                            preferred_element_type=F32)
        if nred == 1:
            o_ref[...] = p.astype(o_ref.dtype)
            return
        acc = scratch[0] if scratch else o_ref
        r = pl.program_id(ngrid - 1)

        @pl.when(r == 0)
        def _():
            acc[...] = p

        @pl.when(r > 0)
        def _():
            acc[...] += p

        if scratch:
            @pl.when(r == nred - 1)
            def _():
                o_ref[...] = acc[...].astype(o_ref.dtype)

    blk = tuple(d for d in o_spec.block_shape if d is not None)
    scratch = [pltpu.VMEM(blk, F32)] if (nred > 1 and out_dtype != F32) else []
    return _call(kern, name=name, grid=grid, in_specs=[a_spec, b_spec], out_specs=o_spec,
                          out_shape=jax.ShapeDtypeStruct(out_shape, out_dtype), scratch_shapes=scratch,
                          compiler_params=_params(ngrid))(a, b)


NN = ((1,), (0,))
NT = ((1,), (1,))
TN = ((0,), (0,))


def _mm_nn(name, a, w, out_dtype=F32, tn=None):
    m, k = a.shape
    n = w.shape[1]
    tm = _tile(m, 512)
    tn = n if tn is None else tn
    return _mm(name, a, w, grid=(m // tm, n // tn, 1), a_spec=BS((tm, k), lambda i, j, r: (i, 0)),
               b_spec=BS((k, tn), lambda i, j, r: (0, j)), o_spec=BS((tm, tn), lambda i, j, r: (i, j)),
               out_shape=(m, n), dn=NN, out_dtype=out_dtype)


def _mm_nt(name, a, w, out_dtype=F32, tn=None):
    m, n = a.shape
    k = w.shape[0]
    tm = _tile(m, 512)
    tn = n if tn is None else tn
    return _mm(name, a, w, grid=(m // tm, n // tn), a_spec=BS((tm, tn), lambda i, r: (i, r)),
               b_spec=BS((k, tn), lambda i, r: (0, r)), o_spec=BS((tm, k), lambda i, r: (i, 0)),
               out_shape=(m, k), dn=NT, out_dtype=out_dtype)


def _mm_tn(name, a, b, tn=None):
    m, k = a.shape
    n = b.shape[1]
    tm = _tile(m, TK_RED)
    tn = n if tn is None else tn
    return _mm(name, a, b, grid=(n // tn, m // tm), a_spec=BS((tm, k), lambda j, r: (r, 0)),
               b_spec=BS((tm, tn), lambda j, r: (r, j)), o_spec=BS((k, tn), lambda j, r: (0, j)),
               out_shape=(k, n), dn=TN)


def _bmm_nn(name, a, w, out_dtype=F32):
    m, k = a.shape
    g, _, n = w.shape
    tm = _tile(m, 512)
    return _mm(name, a, w, grid=(g, m // tm, 1), a_spec=BS((tm, k), lambda q, i, r: (i, 0)),
               b_spec=BS((None, k, n), lambda q, i, r: (q, 0, 0)), o_spec=BS((None, tm, n), lambda q, i, r: (q, i, 0)),
               out_shape=(g, m, n), dn=NN, out_dtype=out_dtype)


def _bmm_tn(name, a, b):
    m, k = a.shape
    g, _, n = b.shape
    tm = _tile(m, TK_RED)
    return _mm(name, a, b, grid=(g, m // tm), a_spec=BS((tm, k), lambda q, r: (r, 0)),
               b_spec=BS((None, tm, n), lambda q, r: (q, r, 0)), o_spec=BS((None, k, n), lambda q, r: (q, 0, 0)),
               out_shape=(g, k, n), dn=TN)


def _block_sum(name, a, w, dn, out_cols):
    g, m, ac = a.shape
    tm = _tile(m, TM_SUM)

    def kern(a_ref, w_ref, o_ref):
        acc = None
        for q in range(g):
            p = lax.dot_general(a_ref[q].astype(BF16), w_ref[q].astype(BF16), (dn, ((), ())), preferred_element_type=F32)
            acc = p if acc is None else acc + p
        o_ref[...] = acc

    return _call(kern, name=name, grid=(m // tm,),
                 in_specs=[BS((g, tm, ac), lambda i: (0, i, 0)), BS(w.shape, lambda i: (0, 0, 0))],
                 out_specs=BS((tm, out_cols), lambda i: (i, 0)), out_shape=jax.ShapeDtypeStruct((m, out_cols), F32),
                 compiler_params=_params(1, VMEM_BIG))(a, w)


def _bmm_nt_sum(name, a, w):
    return _block_sum(name, a, w, NT, w.shape[1])


def _bmm_nn_sum(name, a, w):
    return _block_sum(name, a, w, NN, w.shape[2])


def _bbmm_tn(name, a, b):
    g, m, k = a.shape
    n = b.shape[2]
    tm = _tile(m, TK_RED)
    return _mm(name, a, b, grid=(g, m // tm), a_spec=BS((None, tm, k), lambda q, r: (q, r, 0)),
               b_spec=BS((None, tm, n), lambda q, r: (q, r, 0)), o_spec=BS((None, k, n), lambda q, r: (q, 0, 0)),
               out_shape=(g, k, n), dn=TN)


def _rstd(x):
    return lax.rsqrt(jnp.mean(x * x, axis=-1, keepdims=True) + EPS)


def _norm_fwd(x, g, after=None):
    rows = x.shape[0]
    tm = _tile(rows, 512)

    def kern(x_ref, g_ref, *rest):
        xv = x_ref[...]
        rest[-1][...] = ((xv * _rstd(xv)) * g_ref[...]).astype(BF16)

    extra = () if after is None else (after,)
    return _call(kern, name="norm_fwd", grid=(rows // tm,),
                          in_specs=[BS((tm, DM), lambda i: (i, 0)), BS((1, DM), lambda i: (0, 0))] + [ANY] * len(extra),
                          out_specs=BS((tm, DM), lambda i: (i, 0)),
                          out_shape=jax.ShapeDtypeStruct((rows, DM), BF16), compiler_params=_params(1))(x, g, *extra)


def _norm_res(x, y, g):
    rows = x.shape[0]
    tm = _tile(rows, 512)

    def kern(x_ref, y_ref, g_ref, o_ref):
        yv = y_ref[...]
        o_ref[...] = x_ref[...] + (yv * _rstd(yv)) * g_ref[...]

    row = BS((tm, DM), lambda i: (i, 0))
    return _call(kern, name="norm_res", grid=(rows // tm,),
                          in_specs=[row, row, BS((1, DM), lambda i: (0, 0))], out_specs=row,
                          out_shape=jax.ShapeDtypeStruct((rows, DM), F32), compiler_params=_params(1))(x, y, g)


def _norm_bwd(z, dout, g, resid, out_dtype, after=None):
    rows = z.shape[0]
    tm = _tile(rows, 512)
    has_res = resid is not None

    def kern(*refs):
        z_ref, d_ref, g_ref = refs[:3]
        r_ref = refs[3] if has_res else None
        dz_ref, dg_ref = refs[-2:]
        zv = z_ref[...]
        dv = d_ref[...].astype(F32)
        r = _rstd(zv)
        zh = zv * r
        dzh = dv * g_ref[...]
        dz = r * (dzh - zh * jnp.mean(dzh * zh, axis=-1, keepdims=True))
        if has_res:
            dz = dz + r_ref[...]
        dz_ref[...] = dz.astype(dz_ref.dtype)
        part = jnp.sum(dv * zh, axis=0, keepdims=True)

        @pl.when(pl.program_id(0) == 0)
        def _():
            dg_ref[...] = part

        @pl.when(pl.program_id(0) > 0)
        def _():
            dg_ref[...] += part

    row = BS((tm, DM), lambda i: (i, 0))
    vec = BS((1, DM), lambda i: (0, 0))
    ins = [row, row, vec] + ([row] if has_res else []) + ([ANY] if after is not None else [])
    args = (z, dout, g) + ((resid,) if has_res else ()) + ((after,) if after is not None else ())
    return _call(kern, name="norm_bwd_res" if has_res else "norm_bwd", grid=(rows // tm,), in_specs=ins,
                          out_specs=[row, vec],
                          out_shape=[jax.ShapeDtypeStruct((rows, DM), out_dtype), jax.ShapeDtypeStruct((1, DM), F32)],
                          compiler_params=_params(1))(*args)


def _ffn_up(h, wgu4):
    s = h.shape[0]
    tm = _tile(s, 512)

    def kern(h_ref, w_ref, gu_ref, a_ref):
        hv = h_ref[...]
        gate = lax.dot_general(hv, w_ref[0], (NT, ((), ())), preferred_element_type=F32)
        up = lax.dot_general(hv, w_ref[1], (NT, ((), ())), preferred_element_type=F32)
        gu_ref[0] = gate.astype(BF16)
        gu_ref[1] = up.astype(BF16)
        a_ref[...] = (gate * jax.nn.sigmoid(gate) * up).astype(BF16)

    return _call(
        kern, name="ffn_up", grid=(4, s // tm),
        in_specs=[BS((tm, DM), lambda j, i: (i, 0)), BS((2, None, FFB, DM), lambda j, i: (0, j, 0, 0))],
        out_specs=[BS((2, None, tm, FFB), lambda j, i: (0, j, i, 0)), BS((None, tm, FFB), lambda j, i: (j, i, 0))],
        out_shape=[jax.ShapeDtypeStruct((2, 4, s, FFB), BF16), jax.ShapeDtypeStruct((4, s, FFB), BF16)],
        compiler_params=_params(2))(h, wgu4)


def _ffn_da(dy, wd4, gu):
    s = dy.shape[0]
    tm = _tile(s, 512)

    def kern(dy_ref, w_ref, gu_ref, o_ref):
        da = lax.dot_general(dy_ref[...], w_ref[...], (NT, ((), ())), preferred_element_type=F32)
        gate = gu_ref[0].astype(F32)
        up = gu_ref[1].astype(F32)
        sg = jax.nn.sigmoid(gate)
        o_ref[0] = (da * up * (sg * (1.0 + gate * (1.0 - sg)))).astype(BF16)
        o_ref[1] = (da * (gate * sg)).astype(BF16)

    blk = BS((2, None, tm, FFB), lambda j, i: (0, j, i, 0))
    return _call(
        kern, name="ffn_da", grid=(4, s // tm),
        in_specs=[BS((tm, DM), lambda j, i: (i, 0)), BS((None, FFB, DM), lambda j, i: (j, 0, 0)), blk],
        out_specs=blk, out_shape=jax.ShapeDtypeStruct((2, 4, s, FFB), BF16), compiler_params=_params(2))(dy, wd4, gu)


def _ffn_fwd(x, gpre, gpost, wgu, wd):
    h = _norm_fwd(x, gpre)
    gu, a = _ffn_up(h, wgu.reshape(2, 4, FFB, DM))
    y = _bmm_nn_sum("ffn_down", a, wd.reshape(4, FFB, DM))
    return _norm_res(x, y, gpost), (x, h, gu, a, y)


def _ffn_bwd(dxo, saved, gpre, gpost, wgu, wd, after=None):
    x, h, gu, a, y = saved
    s = x.shape[0]
    dy, dgpost = _norm_bwd(y, dxo, gpost, None, BF16, after)
    dgu = _ffn_da(dy, wd.reshape(4, FFB, DM), gu).reshape(8, s, FFB)
    dwd = _bmm_tn_a3("ffn_dwd", a, dy)
    dwgu = _bmm_tn_a3("ffn_dwgu", dgu, h)
    dh = _bmm_nn_sum("ffn_dh", dgu, wgu)
    dx, dgpre = _norm_bwd(x, dh, gpre, dxo, F32)
    return dx, dgpre, dgpost, dwgu, dwd.reshape(D_FF, DM)


def _bmm_tn_a3(name, a, b):
    g, m, k = a.shape
    n = b.shape[1]
    tm = _tile(m, TK_RED)
    return _mm(name, a, b, grid=(g, m // tm), a_spec=BS((None, tm, k), lambda q, r: (q, r, 0)),
               b_spec=BS((tm, n), lambda q, r: (r, 0)), o_spec=BS((None, k, n), lambda q, r: (q, 0, 0)),
               out_shape=(g, k, n), dn=TN)


def _softmax_rows(s):
    m = jnp.max(s, axis=-1, keepdims=True)
    p = jnp.exp(s - m)
    return p / jnp.sum(p, axis=-1, keepdims=True)


def _xattn_fwd_call(h, wq, kv):
    s = h.shape[0]
    mlen = kv.shape[1]
    tm = _tile(s, 512)
    scale = MEM_HD ** -0.5

    def kern(h_ref, w_ref, k_ref, v_ref, q_ref, o_ref):
        q = jnp.dot(h_ref[...], w_ref[...], preferred_element_type=F32).astype(BF16)
        q_ref[...] = q
        sc = lax.dot_general(q, k_ref[...], (NT, ((), ())), preferred_element_type=F32) * scale
        p = _softmax_rows(sc)
        o_ref[...] = jnp.dot(p.astype(BF16), v_ref[...], preferred_element_type=F32).astype(BF16)

    blk = BS((tm, MEM_HD), lambda i, hd: (i, hd))
    return _call(
        kern, name="xattn_fwd", grid=(s // tm, MEM_H),
        in_specs=[BS((tm, DM), lambda i, hd: (i, 0)), BS((DM, MEM_HD), lambda i, hd: (0, hd)),
                  BS((None, mlen, MEM_HD), lambda i, hd: (hd, 0, 0)),
                  BS((None, mlen, MEM_HD), lambda i, hd: (MEM_H + hd, 0, 0))],
        out_specs=[blk, blk],
        out_shape=[jax.ShapeDtypeStruct((s, DM), BF16), jax.ShapeDtypeStruct((s, DM), BF16)],
        compiler_params=_params(2))(h, wq, kv, kv)


def _xattn_bwd_call(q, kv, do):
    s = q.shape[0]
    mlen = kv.shape[1]
    tm = _tile(s, 512)
    scale = MEM_HD ** -0.5

    def kern(q_ref, k_ref, v_ref, do_ref, dq_ref, dkv_ref):
        qv, kvv, vv, dov = q_ref[...], k_ref[...], v_ref[...], do_ref[...]
        sc = lax.dot_general(qv, kvv, (NT, ((), ())), preferred_element_type=F32) * scale
        p = _softmax_rows(sc)
        dp = lax.dot_general(dov, vv, (NT, ((), ())), preferred_element_type=F32)
        ds = (p * (dp - jnp.sum(dp * p, axis=-1, keepdims=True)) * scale).astype(BF16)
        dq_ref[...] = jnp.dot(ds, kvv, preferred_element_type=F32).astype(BF16)
        dk = lax.dot_general(ds, qv, (TN, ((), ())), preferred_element_type=F32)
        dv = lax.dot_general(p.astype(BF16), dov, (TN, ((), ())), preferred_element_type=F32)

        @pl.when(pl.program_id(1) == 0)
        def _():
            dkv_ref[0] = dk
            dkv_ref[1] = dv

        @pl.when(pl.program_id(1) > 0)
        def _():
            dkv_ref[0] += dk
            dkv_ref[1] += dv

    blk = BS((tm, MEM_HD), lambda hd, i: (i, hd))
    return _call(
        kern, name="xattn_bwd", grid=(MEM_H, s // tm),
        in_specs=[blk, BS((None, mlen, MEM_HD), lambda hd, i: (hd, 0, 0)),
                  BS((None, mlen, MEM_HD), lambda hd, i: (MEM_H + hd, 0, 0)), blk],
        out_specs=[blk, BS((2, None, mlen, MEM_HD), lambda hd, i: (0, hd, 0, 0))],
        out_shape=[jax.ShapeDtypeStruct((s, DM), BF16), jax.ShapeDtypeStruct((2, MEM_H, mlen, MEM_HD), F32)],
        compiler_params=_params(2))(q, kv, kv, do)


def _cross_fwd(x, mem, gpre, gmem, gpost, wq, wkv, wo, after=None):
    h = _norm_fwd(x, gpre, after)
    mn = _norm_fwd(mem, gmem)
    kv = _bmm_nn("xattn_kv", mn, wkv, BF16)
    q, o = _xattn_fwd_call(h, wq, kv)
    y = _mm_nn("xattn_out", o, wo)
    return _norm_res(x, y, gpost), (x, h, mn, kv, q, o, y)


def _cross_bwd(dxo, saved, mem, gpre, gmem, gpost, wq, wkv, wo, after=None):
    x, h, mn, kv, q, o, y = saved
    mlen = mem.shape[0]
    dy, dgpost = _norm_bwd(y, dxo, gpost, None, BF16, after)
    do = _mm_nt("xattn_do", dy, wo, BF16)
    dwo = _mm_tn("xattn_dwo", o, dy)
    dq, dkv = _xattn_bwd_call(q, kv, do)
    dwq = _mm_tn("xattn_dwq", h, dq)
    dh = _mm_nt("xattn_dh", dq, wq)
    dkv8 = dkv.reshape(8, mlen, MEM_HD)
    dwkv = _bmm_tn("xattn_dwkv", mn, dkv8)
    dmn = _bmm_nt_sum("xattn_dmn", dkv8, wkv)
    _, dgmem = _norm_bwd(mem, dmn, gmem, None, BF16)
    dx, dgpre = _norm_bwd(x, dh, gpre, dxo, F32)
    return dx, dgpre, dgmem, dgpost, dwq, dwkv, dwo


def _log_sigmoid(z):
    return jnp.minimum(z, 0.0) - jnp.log1p(jnp.exp(-jnp.abs(z)))


def _lane_scan_steps():
    return (1, 2, 4, 8, 16, 32, 64)


def _fox_cum(frow, bfb):
    s = frow.shape[1]

    def kern(f_ref, b_ref, o_ref):
        lane = lax.broadcasted_iota(jnp.int32, (FOX_H, LANE), 1)
        carry = jnp.zeros((FOX_H, 1), F32)
        for c in range(s // LANE):
            sl = slice(c * LANE, (c + 1) * LANE)
            lf = _log_sigmoid(f_ref[:, sl] + b_ref[...])
            v = lf
            for d in _lane_scan_steps():
                v = v + jnp.where(lane >= d, pltpu.roll(v, d, 1), 0.0)
            o_ref[:, sl] = v + carry
            carry = carry + jnp.sum(lf, axis=1, keepdims=True)

    return _call(kern, name="fox_cum", out_shape=jax.ShapeDtypeStruct((FOX_H, s), F32),
                          compiler_params=pltpu.CompilerParams(vmem_limit_bytes=VMEM_LIMIT))(frow, bfb)


def _fox_dlogf(dcq, dck, frow, bfb):
    s = frow.shape[1]

    def kern(q_ref, d_ref, f_ref, b_ref, df_ref, db_ref):
        lane = lax.broadcasted_iota(jnp.int32, (FOX_H, LANE), 1)
        carry = jnp.zeros((FOX_H, 1), F32)
        dbf = jnp.zeros((FOX_H, 1), F32)
        for c in reversed(range(s // LANE)):
            sl = slice(c * LANE, (c + 1) * LANE)
            dc = q_ref[:, sl] - d_ref[:, sl]
            v = dc
            for d in _lane_scan_steps():
                v = v + jnp.where(lane < LANE - d, pltpu.roll(v, LANE - d, 1), 0.0)
            v = v + carry
            carry = carry + jnp.sum(dc, axis=1, keepdims=True)
            df = v * jax.nn.sigmoid(-(f_ref[:, sl] + b_ref[...]))
            df_ref[:, sl] = df
            dbf = dbf + jnp.sum(df, axis=1, keepdims=True)
        db_ref[...] = jnp.broadcast_to(dbf, (FOX_H, LANE))

    return _call(kern, name="fox_dlogf",
                          out_shape=[jax.ShapeDtypeStruct((FOX_H, s), F32), jax.ShapeDtypeStruct((FOX_H, LANE), F32)],
                          compiler_params=pltpu.CompilerParams(vmem_limit_bytes=VMEM_LIMIT))(dcq, dck, frow, bfb)


FOX_TQ = 512
Q_COL, K_COL, V_COL = 0, FOX_W // LANE, 2 * FOX_W // LANE
B_COL, C_COL, U_COL = 12, 16, 20


def _bf16_terms(c):
    hi = c.astype(BF16).astype(F32)
    mid = (c - hi).astype(BF16).astype(F32)
    return hi, mid, (c - hi - mid).astype(BF16).astype(F32)


def _fox_operands(qv, kv, cq, ck, lane, hh, scale):
    sel = (lane < FOX_HD) if hh == 0 else (lane >= FOX_HD)
    b0 = FOX_HD if hh == 0 else 0
    qa = jnp.where(sel, qv * scale, 0.0)
    ka = jnp.where(sel, kv, 0.0)
    for n, (tq_, tk_) in enumerate(zip(_bf16_terms(cq), _bf16_terms(ck))):
        qa = jnp.where(lane == b0 + n, tq_, jnp.where(lane == b0 + 3 + n, 1.0, qa))
        ka = jnp.where(lane == b0 + n, 1.0, jnp.where(lane == b0 + 3 + n, -tk_, ka))
    return sel, qa.astype(BF16), ka.astype(BF16)


def _fox_logits(qa, ka, causal):
    sc = lax.dot_general(qa, ka, (NT, ((), ())), preferred_element_type=F32)
    return sc if causal is None else jnp.where(causal, sc, NEG)


def _fox_prep(proj, cumc):
    s = proj.shape[0]
    tp = _tile(s, 512)
    scale = FOX_HD ** -0.5

    def kern(q_ref, k_ref, c_ref, qa_ref, ka_ref):
        lane = lax.broadcasted_iota(jnp.int32, (tp, LANE), 1)
        for hh in range(2):
            _, qa_ref[hh], ka_ref[hh] = _fox_operands(q_ref[...], k_ref[...], c_ref[hh], c_ref[hh], lane, hh, scale)

    pair = BS((2, tp, LANE), lambda hp, i: (hp, i, 0))
    shp = jax.ShapeDtypeStruct((FOX_H, s, LANE), BF16)
    return _call(kern, name="fox_prep", grid=(4, s // tp),
                 in_specs=[BS((tp, LANE), lambda hp, i: (i, Q_COL + hp)), BS((tp, LANE), lambda hp, i: (i, K_COL + hp)), pair],
                 out_specs=[pair, pair], out_shape=[shp, shp], compiler_params=_params(2))(proj, proj, cumc)


def _fox_fwd_call(proj, qa, ka):
    s = proj.shape[0]
    tq = _tile(s, FOX_TQ)
    nq = s // tq
    reps = tq // LANE
    scale = FOX_HD ** -0.5

    def kern(qa_ref, ka_ref, v_ref, o_ref, lse_ref, m_s, l_s, acc_s):
        i = pl.program_id(1)
        j = pl.program_id(2)
        lane = lax.broadcasted_iota(jnp.int32, (tq, LANE), 1)

        @pl.when(j == 0)
        def _():
            m_s[...] = jnp.full(m_s.shape, NEG, F32)
            l_s[...] = jnp.zeros(l_s.shape, F32)
            acc_s[...] = jnp.zeros(acc_s.shape, F32)

        def step(diagonal):
            vb = v_ref[...].astype(BF16)
            causal = (lax.broadcasted_iota(jnp.int32, (tq, tq), 0) >= lax.broadcasted_iota(jnp.int32, (tq, tq), 1)
                      if diagonal else None)
            for hh in range(2):
                sc = _fox_logits(qa_ref[hh], ka_ref[hh], causal)
                m_prev = m_s[hh]
                m_new = jnp.maximum(m_prev, jnp.max(sc, axis=-1, keepdims=True))
                alpha = jnp.exp(m_prev - m_new)
                p = jnp.exp(sc - m_new)
                l_s[hh] = alpha * l_s[hh] + jnp.sum(p, axis=-1, keepdims=True)
                acc_s[hh] = alpha * acc_s[hh] + jnp.dot(p.astype(BF16), vb, preferred_element_type=F32)
                m_s[hh] = m_new

        @pl.when(j < i)
        def _():
            step(False)

        @pl.when(j == i)
        def _():
            step(True)
            o_ref[...] = jnp.where(lane < FOX_HD, acc_s[0] / l_s[0], acc_s[1] / l_s[1])
            for hh in range(2):
                lse_ref[hh] = jnp.broadcast_to(m_s[hh] + jnp.log(l_s[hh]), (tq, LANE))

    kvi = lambda hp, i, j: jnp.minimum(j, i)
    return _call(
        kern, name="fox_fwd", grid=(4, nq, nq),
        in_specs=[BS((2, tq, LANE), lambda hp, i, j: (hp, i, 0)),
                  BS((2, tq, LANE), lambda hp, i, j: (hp, kvi(hp, i, j), 0)),
                  BS((tq, LANE), lambda hp, i, j: (kvi(hp, i, j), V_COL + hp))],
        out_specs=[BS((tq, LANE), lambda hp, i, j: (i, hp)), BS((2, tq, LANE), lambda hp, i, j: (hp, i, 0))],
        out_shape=[jax.ShapeDtypeStruct((s, FOX_W), F32), jax.ShapeDtypeStruct((FOX_H, s, LANE), F32)],
        scratch_shapes=[pltpu.VMEM((2, tq, 1), F32), pltpu.VMEM((2, tq, 1), F32), pltpu.VMEM((2, tq, LANE), F32)],
        compiler_params=_params(3))(qa, ka, proj)


ROWSUM_M = 16


def _fox_bwd_call(proj, o, lse, dcat, qa, ka):
    s = proj.shape[0]
    tq = _tile(s, FOX_TQ)
    nq = s // tq
    reps = tq // LANE
    scale = FOX_HD ** -0.5

    def kern(qa_ref, ka_ref, v_ref, do_ref, o_ref, lse_ref, dq_ref, dk_ref, dv_ref, dck_ref, dcq_ref):
        j = pl.program_id(1)
        i = pl.program_id(2)
        lane = lax.broadcasted_iota(jnp.int32, (tq, LANE), 1)
        ones = jnp.ones((ROWSUM_M, tq), BF16)

        @pl.when((j == 0) & (i == 0))
        def _():
            dq_ref[...] = jnp.zeros(dq_ref.shape, F32)
            dcq_ref[...] = jnp.zeros(dcq_ref.shape, F32)

        @pl.when(i == j)
        def _():
            dk_ref[...] = jnp.zeros(dk_ref.shape, F32)
            dv_ref[...] = jnp.zeros(dv_ref.shape, F32)
            dck_ref[...] = jnp.zeros(dck_ref.shape, F32)

        def step(diagonal):
            dov = do_ref[...]
            ov = o_ref[...]
            vb = v_ref[...].astype(BF16)
            causal = (lax.broadcasted_iota(jnp.int32, (tq, tq), 0) >= lax.broadcasted_iota(jnp.int32, (tq, tq), 1)
                      if diagonal else None)
            dq_t = jnp.zeros((tq, LANE), F32)
            dk_t = jnp.zeros((tq, LANE), F32)
            dv_t = jnp.zeros((tq, LANE), F32)
            for hh in range(2):
                sel = (lane < FOX_HD) if hh == 0 else (lane >= FOX_HD)
                qa, ka = qa_ref[hh], ka_ref[hh]
                dom32 = jnp.where(sel, dov, 0.0)
                dom = dom32.astype(BF16)
                sc = _fox_logits(qa, ka, causal)
                p = jnp.exp(sc - jnp.tile(lse_ref[hh], (1, reps)))
                dp = lax.dot_general(dom, vb, (NT, ((), ())), preferred_element_type=F32)
                delta = jnp.sum(dom32 * ov, axis=-1, keepdims=True)
                ds = p * (dp - delta)
                dsb = ds.astype(BF16)
                dq_t = jnp.where(sel, jnp.dot(dsb, ka, preferred_element_type=F32) * scale, dq_t)
                dk_t = jnp.where(sel, lax.dot_general(dsb, qa, (TN, ((), ())), preferred_element_type=F32), dk_t)
                dv_t = dv_t + lax.dot_general(p.astype(BF16), dom, (TN, ((), ())), preferred_element_type=F32)
                dck_ref[hh] += jnp.sum(ds, axis=0, keepdims=True)
                ds_lo = (ds - dsb.astype(F32)).astype(BF16)
                dcq_ref[hh, i] += (lax.dot_general(ones, dsb, (NT, ((), ())), preferred_element_type=F32)
                                   + lax.dot_general(ones, ds_lo, (NT, ((), ())), preferred_element_type=F32))
            rows = pl.ds(pl.multiple_of(i * tq, tq), tq)
            dq_ref[rows, :] += dq_t
            dk_ref[...] += dk_t
            dv_ref[...] += dv_t

        @pl.when(i > j)
        def _():
            step(False)

        @pl.when(i == j)
        def _():
            step(True)

    qi = lambda hp, j, i: jnp.maximum(i, j)
    return _call(
        kern, name="fox_bwd", grid=(4, nq, nq),
        in_specs=[BS((2, tq, LANE), lambda hp, j, i: (hp, qi(hp, j, i), 0)),
                  BS((2, tq, LANE), lambda hp, j, i: (hp, j, 0)),
                  BS((tq, LANE), lambda hp, j, i: (j, V_COL + hp)),
                  BS((tq, LANE), lambda hp, j, i: (qi(hp, j, i), hp)),
                  BS((tq, LANE), lambda hp, j, i: (qi(hp, j, i), hp)),
                  BS((2, tq, LANE), lambda hp, j, i: (hp, qi(hp, j, i), 0))],
        out_specs=[BS((s, LANE), lambda hp, j, i: (0, hp)), BS((tq, LANE), lambda hp, j, i: (j, hp)),
                   BS((tq, LANE), lambda hp, j, i: (j, hp)), BS((2, 1, tq), lambda hp, j, i: (hp, 0, j)),
                   BS((2, nq, ROWSUM_M, tq), lambda hp, j, i: (hp, 0, 0, 0))],
        out_shape=[jax.ShapeDtypeStruct((s, FOX_W), F32), jax.ShapeDtypeStruct((s, FOX_W), F32),
                   jax.ShapeDtypeStruct((s, FOX_W), F32), jax.ShapeDtypeStruct((FOX_H, 1, s), F32),
                   jax.ShapeDtypeStruct((FOX_H, nq, ROWSUM_M, tq), F32)],
        compiler_params=_params(3))(qa, ka, proj, dcat, o, lse)


def _shift_down(v, d, row):
    return jnp.where(row >= d, pltpu.roll(v, d, 0), 0.0)


def _shift_up(v, d, row, n):
    return jnp.where(row < n - d, pltpu.roll(v, n - d, 0), 0.0)


def _sconv_fwd(proj, convw):
    s = proj.shape[0]

    def kern(b_ref, c_ref, u_ref, w_ref, y_ref):
        row = lax.broadcasted_iota(jnp.int32, (s, LANE), 0)
        z = c_ref[...] * u_ref[...]
        conv = w_ref[2:3, :] * z + w_ref[1:2, :] * _shift_down(z, 1, row) + w_ref[0:1, :] * _shift_down(z, 2, row)
        y_ref[...] = (b_ref[...] * conv).astype(BF16)

    col = lambda base: BS((s, LANE), lambda cb: (0, base + cb))
    return _call(kern, name="sconv_fwd", grid=(SC_W // LANE,),
                          in_specs=[col(B_COL), col(C_COL), col(U_COL), BS((SC_K, LANE), lambda cb: (0, cb))],
                          out_specs=BS((s, LANE), lambda cb: (0, cb)),
                          out_shape=jax.ShapeDtypeStruct((s, SC_W), BF16), compiler_params=_params(1))(proj, proj, proj, convw)


def _sconv_bwd(proj, convw, dcat):
    s = proj.shape[0]

    def kern(b_ref, c_ref, u_ref, w_ref, dy_ref, db_ref, dc_ref, du_ref, dw_ref):
        row = lax.broadcasted_iota(jnp.int32, (s, LANE), 0)
        cv, uv, dyv = c_ref[...], u_ref[...], dy_ref[...]
        z = cv * uv
        z1 = _shift_down(z, 1, row)
        z2 = _shift_down(z, 2, row)
        conv = w_ref[2:3, :] * z + w_ref[1:2, :] * z1 + w_ref[0:1, :] * z2
        db_ref[...] = dyv * conv
        dcv = dyv * b_ref[...]
        dz = w_ref[2:3, :] * dcv + w_ref[1:2, :] * _shift_up(dcv, 1, row, s) + w_ref[0:1, :] * _shift_up(dcv, 2, row, s)
        dc_ref[...] = dz * uv
        du_ref[...] = dz * cv
        dw_ref[0:1, :] = jnp.sum(dcv * z2, axis=0, keepdims=True)
        dw_ref[1:2, :] = jnp.sum(dcv * z1, axis=0, keepdims=True)
        dw_ref[2:3, :] = jnp.sum(dcv * z, axis=0, keepdims=True)

    col = lambda base: BS((s, LANE), lambda cb: (0, base + cb))
    out = BS((s, LANE), lambda cb: (0, cb))
    wspec = BS((SC_K, LANE), lambda cb: (0, cb))
    act = jax.ShapeDtypeStruct((s, SC_W), F32)
    return _call(kern, name="sconv_bwd", grid=(SC_W // LANE,),
                          in_specs=[col(B_COL), col(C_COL), col(U_COL), wspec, col(FOX_W // LANE)],
                          out_specs=[out, out, out, wspec],
                          out_shape=[act, act, act, jax.ShapeDtypeStruct((SC_K, SC_W), F32)],
                          compiler_params=_params(1))(proj, proj, proj, convw, dcat)


def _fox_layer_fwd(x, gpre, gpost, wall, bfb, convw, wout, after=None):
    s = x.shape[0]
    h = _norm_fwd(x, gpre, after)
    proj = _mm_nn("fox_proj", h, wall, tn=AB_PAD // 5)
    frow = proj[:, 3 * FOX_W + 3 * SC_W:3 * FOX_W + 3 * SC_W + FOX_H].T
    cumr = _fox_cum(frow, bfb)
    qa, ka = _fox_prep(proj, jnp.broadcast_to(cumr[:, :, None], (FOX_H, s, LANE)))
    o, lse = _fox_fwd_call(proj, qa, ka)
    yb = _sconv_fwd(proj, convw)
    cat = jnp.concatenate([o.astype(BF16), yb], axis=1)
    y = _mm_nn("fox_out", cat, wout)
    return _norm_res(x, y, gpost), (x, h, proj, frow, qa, ka, o, lse, cat, y)


def _fox_layer_bwd(dxo, saved, gpre, gpost, wall, bfb, convw, wout, after=None):
    x, h, proj, frow, qa, ka, o, lse, cat, y = saved
    s = x.shape[0]
    dy, dgpost = _norm_bwd(y, dxo, gpost, None, BF16, after)
    dcat = _mm_nt("fox_dcat", dy, wout)
    dwout = _mm_tn("fox_dwout", cat, dy)
    db, dc, du, dconvw = _sconv_bwd(proj, convw, dcat)
    dq, dk, dv, dck, dcq = _fox_bwd_call(proj, o, lse, dcat, qa, ka)
    dfrow, dbf = _fox_dlogf(dcq[:, :, 0, :].reshape(FOX_H, s), dck.reshape(FOX_H, s), frow, bfb)
    dfcol = jnp.pad(dfrow.T, ((0, 0), (0, LANE - FOX_H)))
    dproj = jnp.concatenate([dq, dk, dv, db, dc, du, dfcol], axis=1).astype(BF16)
    dwall = _mm_tn("fox_dwall", h, dproj, tn=AB_PAD // 5)
    dh = _mm_nt("fox_dh", dproj, wall, tn=AB_PAD // 5)
    dx, dgpre = _norm_bwd(x, dh, gpre, dxo, F32)
    return dx, dgpre, dgpost, dwall, dbf[:, 0], dconvw, dwout


def _ab_pack(w):
    nf = 3 * FOX_W
    return jnp.concatenate([w[:, :nf], w[:, nf + FOX_H:], w[:, nf:nf + FOX_H],
                            jnp.zeros((w.shape[0], AB_PAD - AB_IN), w.dtype)], axis=1)


def _ab_unpack(w):
    nf = 3 * FOX_W
    nbcu = 3 * SC_W
    return jnp.concatenate([w[:, :nf], w[:, nf + nbcu:nf + nbcu + FOX_H], w[:, nf:nf + nbcu]], axis=1)


NCH = DM // LANE
CH_PER_BLK = LRU_BW // LANE


def _chunk_spec(s, lead=0):
    return BS((None, s, LANE), lambda ch: (lead + ch // CH_PER_BLK, 0, ch % CH_PER_BLK))


def _vec_chunk(rows):
    return BS((rows, LANE), lambda ch: (0, ch))


def _neg_expm1(x):
    series = -x * (1.0 + x * (1 / 2) * (1.0 + x * (1 / 3) * (1.0 + x * (1 / 4) * (1.0 + x * (1 / 5) * (
        1.0 + x * (1 / 6) * (1.0 + x * (1 / 7)))))))
    return jnp.where(x > -0.25, series, 1.0 - jnp.exp(x))


def _softplus(z):
    return jnp.maximum(z, 0.0) + jnp.log1p(jnp.exp(-jnp.abs(z)))


GELU_C = math.sqrt(2.0 / math.pi)
GELU_A = 0.044715


def _gelu(x):
    return 0.5 * x * (1.0 + jnp.tanh(GELU_C * (x + GELU_A * x * x * x)))


def _gelu_grad(x):
    t = jnp.tanh(GELU_C * (x + GELU_A * x * x * x))
    return 0.5 * (1.0 + t) + 0.5 * x * (1.0 - t * t) * GELU_C * (1.0 + 3.0 * GELU_A * x * x)


def _lru_conv_fwd(gu, convw, convb):
    s = gu.shape[1]

    def kern(x_ref, w_ref, b_ref, u_ref):
        row = lax.broadcasted_iota(jnp.int32, (s, LANE), 0)
        xv = x_ref[...]
        u_ref[...] = (b_ref[...] + w_ref[3:4, :] * xv + w_ref[2:3, :] * _shift_down(xv, 1, row)
                      + w_ref[1:2, :] * _shift_down(xv, 2, row) + w_ref[0:1, :] * _shift_down(xv, 3, row))

    return _call(kern, name="lru_conv_fwd", grid=(NCH,),
                          in_specs=[_chunk_spec(s, LRU_NB), _vec_chunk(RG_K), _vec_chunk(1)], out_specs=_chunk_spec(s),
                          out_shape=jax.ShapeDtypeStruct((LRU_NB, s, LRU_BW), F32), compiler_params=_params(1))(gu, convw, convb)


def _lru_conv_bwd(dud, dug, gu, convw):
    s = gu.shape[1]

    def kern(d1_ref, d2_ref, x_ref, w_ref, dx_ref, dw_ref, db_ref):
        row = lax.broadcasted_iota(jnp.int32, (s, LANE), 0)
        du = d1_ref[...] + d2_ref[...]
        xv = x_ref[...]
        dx_ref[...] = (w_ref[3:4, :] * du + w_ref[2:3, :] * _shift_up(du, 1, row, s) + w_ref[1:2, :] * _shift_up(du, 2, row, s)
                       + w_ref[0:1, :] * _shift_up(du, 3, row, s)).astype(BF16)
        dw_ref[3:4, :] = jnp.sum(du * xv, axis=0, keepdims=True)
        for k in range(1, RG_K):
            dw_ref[3 - k:4 - k, :] = jnp.sum(du * _shift_down(xv, k, row), axis=0, keepdims=True)
        db_ref[...] = jnp.sum(du, axis=0, keepdims=True)

    return _call(kern, name="lru_conv_bwd", grid=(NCH,),
                          in_specs=[_chunk_spec(s), _chunk_spec(s), _chunk_spec(s, LRU_NB), _vec_chunk(RG_K)],
                          out_specs=[_chunk_spec(s), _vec_chunk(RG_K), _vec_chunk(1)],
                          out_shape=[jax.ShapeDtypeStruct((LRU_NB, s, LRU_BW), BF16),
                                     jax.ShapeDtypeStruct((RG_K, DM), F32), jax.ShapeDtypeStruct((1, DM), F32)],
                          compiler_params=_params(1))(dud, dug, gu, convw)


def _lru_gates(z_ref, bai_ref, lam_ref, uv):
    r = jax.nn.sigmoid(z_ref[0] + bai_ref[0:1, :])
    ig = jax.nn.sigmoid(z_ref[1] + bai_ref[1:2, :])
    sp = _softplus(-lam_ref[...])
    la = -RG_C * r * sp
    a = jnp.exp(la)
    sq = jnp.sqrt(_neg_expm1(2.0 * la))
    return r, ig, sp, a, sq


def _scan_steps(n):
    d, out = 1, []
    while d < n:
        out.append(d)
        d *= 2
    return out


def _lru_scan_fwd(z, bai, lam, u, gu):
    s = u.shape[1]
    zspec = BS((2, None, s, LANE), lambda ch: (0, ch // CH_PER_BLK, 0, ch % CH_PER_BLK))

    def kern(z_ref, bai_ref, lam_ref, u_ref, g_ref, hs_ref, y_ref):
        row = lax.broadcasted_iota(jnp.int32, (s, LANE), 0)
        uv = u_ref[...]
        _, ig, _, a, sq = _lru_gates(z_ref, bai_ref, lam_ref, uv)
        b = sq * (ig * uv)
        for d in _scan_steps(s):
            a_sh = jnp.where(row >= d, pltpu.roll(a, d, 0), 1.0)
            b = a * _shift_down(b, d, row) + b
            a = a * a_sh
        hs_ref[...] = b
        y_ref[...] = (_gelu(g_ref[...]) * b).astype(BF16)

    return _call(kern, name="lru_scan_fwd", grid=(NCH,),
                          in_specs=[zspec, _vec_chunk(2), _vec_chunk(1), _chunk_spec(s), _chunk_spec(s)],
                          out_specs=[_chunk_spec(s), BS((s, LANE), lambda ch: (0, ch))],
                          out_shape=[jax.ShapeDtypeStruct((LRU_NB, s, LRU_BW), F32), jax.ShapeDtypeStruct((s, DM), BF16)],
                          compiler_params=_params(1, VMEM_BIG))(z, bai, lam, u, gu)


def _lru_scan_bwd(dyp, z, bai, lam, u, gu, hs):
    s = u.shape[1]
    zspec = BS((2, None, s, LANE), lambda ch: (0, ch // CH_PER_BLK, 0, ch % CH_PER_BLK))

    def kern(dy_ref, z_ref, bai_ref, lam_ref, u_ref, g_ref, hs_ref, dg_ref, dz_ref, du_ref, dbai_ref, dlam_ref):
        row = lax.broadcasted_iota(jnp.int32, (s, LANE), 0)
        uv, gv, hv, dyv = u_ref[...], g_ref[...], hs_ref[...], dy_ref[...]
        r, ig, sp, a, sq = _lru_gates(z_ref, bai_ref, lam_ref, uv)
        dg_ref[...] = (dyv * hv * _gelu_grad(gv)).astype(BF16)
        g = dyv * _gelu(gv)
        an = _shift_up(a, 1, row, s)
        for d in _scan_steps(s):
            an_sh = jnp.where(row < s - d, pltpu.roll(an, s - d, 0), 1.0)
            g = an * _shift_up(g, d, row, s) + g
            an = an * an_sh
        da = g * _shift_down(hv, 1, row)
        dsq = g * (ig * uv)
        di = g * sq * uv
        du_ref[...] = g * sq * ig
        dla = da * a - dsq * (a * a / sq)
        dzr = dla * (-RG_C * sp) * r * (1.0 - r)
        dzi = di * ig * (1.0 - ig)
        dz_ref[0] = dzr.astype(BF16)
        dz_ref[1] = dzi.astype(BF16)
        dbai_ref[0:1, :] = jnp.sum(dzr, axis=0, keepdims=True)
        dbai_ref[1:2, :] = jnp.sum(dzi, axis=0, keepdims=True)
        dlam_ref[...] = jnp.sum(dla * r, axis=0, keepdims=True) * (RG_C * jax.nn.sigmoid(-lam_ref[...]))

    return _call(
        kern, name="lru_scan_bwd", grid=(NCH,),
        in_specs=[BS((s, LANE), lambda ch: (0, ch)), zspec, _vec_chunk(2), _vec_chunk(1), _chunk_spec(s), _chunk_spec(s),
                  _chunk_spec(s)],
        out_specs=[_chunk_spec(s), zspec, _chunk_spec(s), _vec_chunk(2), _vec_chunk(1)],
        out_shape=[jax.ShapeDtypeStruct((LRU_NB, s, LRU_BW), BF16), jax.ShapeDtypeStruct((2, LRU_NB, s, LRU_BW), BF16),
                   jax.ShapeDtypeStruct((LRU_NB, s, LRU_BW), F32), jax.ShapeDtypeStruct((2, DM), F32),
                   jax.ShapeDtypeStruct((1, DM), F32)],
        compiler_params=_params(1, VMEM_BIG))(dyp, z, bai, lam, u, gu, hs)


def _lru_layer_fwd(x, gpre, gpost, win, convw, convb, wai, bai, lam, wout, after=None):
    s = x.shape[0]
    tm = _tile(s, 512)
    h = _norm_fwd(x, gpre, after)
    gu = _bmm_nn("lru_in", h, win)
    u = _lru_conv_fwd(gu, convw, convb)
    z = _mm("lru_gate", u, wai, grid=(2, LRU_NB, s // tm, 1),
            a_spec=BS((None, tm, LRU_BW), lambda k, n, i, r: (n, i, 0)),
            b_spec=BS((None, None, LRU_BW, LRU_BW), lambda k, n, i, r: (k, n, 0, 0)),
            o_spec=BS((None, None, tm, LRU_BW), lambda k, n, i, r: (k, n, i, 0)),
            out_shape=(2, LRU_NB, s, LRU_BW), dn=NN)
    hs, yp = _lru_scan_fwd(z, bai, lam, u, gu)
    y = _mm_nn("lru_out", yp, wout)
    return _norm_res(x, y, gpost), (x, h, gu, u, z, hs, yp, y)


def _lru_layer_bwd(dxo, saved, gpre, gpost, win, convw, convb, wai, bai, lam, wout, after=None):
    x, h, gu, u, z, hs, yp, y = saved
    s = x.shape[0]
    tm = _tile(s, 512)
    dy, dgpost = _norm_bwd(y, dxo, gpost, None, BF16, after)
    dyp = _mm_nt("lru_dyp", dy, wout)
    dwout = _mm_tn("lru_dwout", yp, dy)
    dgate, dz, dud, dbai, dlam = _lru_scan_bwd(dyp, z, bai, lam, u, gu, hs)
    dwai = _mm("lru_dwai", u, dz, grid=(2, LRU_NB, s // tm),
               a_spec=BS((None, tm, LRU_BW), lambda k, n, r: (n, r, 0)),
               b_spec=BS((None, None, tm, LRU_BW), lambda k, n, r: (k, n, r, 0)),
               o_spec=BS((None, None, LRU_BW, LRU_BW), lambda k, n, r: (k, n, 0, 0)),
               out_shape=(2, LRU_NB, LRU_BW, LRU_BW), dn=TN)
    dug = _mm("lru_dug", dz, wai, grid=(LRU_NB, s // tm, 2),
              a_spec=BS((None, None, tm, LRU_BW), lambda n, i, k: (k, n, i, 0)),
              b_spec=BS((None, None, LRU_BW, LRU_BW), lambda n, i, k: (k, n, 0, 0)),
              o_spec=BS((None, tm, LRU_BW), lambda n, i, k: (n, i, 0)),
              out_shape=(LRU_NB, s, LRU_BW), dn=NT)
    duraw, dconvw, dconvb = _lru_conv_bwd(dud, dug, gu, convw)
    dgu = jnp.concatenate([dgate, duraw], axis=0)
    dwin = _bmm_tn("lru_dwin", h, dgu)
    dh = _bmm_nt_sum("lru_dh", dgu, win)
    dx, dgpre = _norm_bwd(x, dh, gpre, dxo, F32)
    return dx, dgpre, dgpost, dwin, dconvw, dconvb, dwai, dbai, dlam, dwout


CHIP_FLIPS = ((1, 0), (0, 1), (1, 1))


def _place():
    return lax.axis_index("x"), lax.axis_index("y"), lax.axis_index("c")


def _flip(v, f):
    return 1 - v if f else v


def _comm_params():
    return pltpu.CompilerParams(vmem_limit_bytes=VMEM_LIMIT)


def _all_gather(shards):
    n = len(shards)

    def body(*refs):
        ins, outs, stage = refs[:n], refs[n:2 * n], refs[2 * n:3 * n]
        send_sems, recv_sems, local_sems = refs[3 * n:]
        x, y, c = _place()
        me, sibling = (x, y, c), (x, y, 1 - c)
        chips = [(_flip(x, fx), _flip(y, fy)) for fx, fy in CHIP_FLIPS]

        def slot(t, p):
            return outs[t].at[:, 4 * p[0] + 2 * p[1] + p[2]]

        def copy(t, k, block, to, src=None):
            return pltpu.make_async_remote_copy(
                src_ref=slot(t, block) if src is None else src, dst_ref=slot(t, block),
                send_sem=send_sems.at[7 * t + k], recv_sem=recv_sems.at[7 * t + k], device_id=to, device_id_type=MESH)

        first = []
        for t in range(n):
            first.append(copy(t, 0, me, sibling, src=ins[t]))
            first += [copy(t, 1 + j, me, (*chip, c), src=ins[t]) for j, chip in enumerate(chips)]
        for cp in first:
            cp.start()
        load = [pltpu.make_async_copy(ins[t], stage[t], local_sems.at[t]) for t in range(n)]
        mine = [pltpu.make_async_copy(stage[t], slot(t, me), local_sems.at[t]) for t in range(n)]
        for cp in load:
            cp.start()
        for t in range(n):
            load[t].wait()
            mine[t].start()
        passed = []
        for j, chip in enumerate(chips):
            for t in range(n):
                copy(t, 1 + j, (*chip, c), me).wait_recv()
                fwd = copy(t, 4 + j, (*chip, c), sibling)
                fwd.start()
                passed.append(fwd)
        for t in range(n):
            copy(t, 0, sibling, me).wait_recv()
            for j, chip in enumerate(chips):
                copy(t, 4 + j, (*chip, 1 - c), me).wait_recv()
        for cp in first + passed:
            cp.wait_send()
        for cp in mine:
            cp.wait()

    outs = [jax.ShapeDtypeStruct((s.shape[0], NDEV) + s.shape[1:], s.dtype) for s in shards]
    return pl.pallas_call(body, name="all_gather", in_specs=[ANY] * n, out_specs=[ANY] * n, out_shape=outs,
                          scratch_shapes=[pltpu.VMEM(s.shape, s.dtype) for s in shards]
                          + [pltpu.SemaphoreType.DMA((7 * n,)), pltpu.SemaphoreType.DMA((7 * n,)),
                             pltpu.SemaphoreType.DMA((n,))],
                          compiler_params=_comm_params())(*shards)


def _small_gather(v):
    def body(v_ref, o_ref, send_sems, recv_sems, local_sem):
        x, y, c = _place()
        mine = 4 * x + 2 * y + c
        local = pltpu.make_async_copy(v_ref, o_ref.at[mine], local_sem)
        local.start()
        sends = []
        for k in range(1, NDEV):
            fx, fy, fc = (k >> 2) & 1, (k >> 1) & 1, k & 1
            sends.append(pltpu.make_async_remote_copy(
                src_ref=v_ref, dst_ref=o_ref.at[mine], send_sem=send_sems.at[k - 1], recv_sem=recv_sems.at[k - 1],
                device_id=(_flip(x, fx), _flip(y, fy), _flip(c, fc)), device_id_type=MESH))
        for cp in sends:
            cp.start()
        for k in range(1, NDEV):
            fx, fy, fc = (k >> 2) & 1, (k >> 1) & 1, k & 1
            src = 4 * _flip(x, fx) + 2 * _flip(y, fy) + _flip(c, fc)
            pltpu.make_async_remote_copy(src_ref=v_ref, dst_ref=o_ref.at[src], send_sem=send_sems.at[k - 1],
                                         recv_sem=recv_sems.at[k - 1], device_id=(x, y, c), device_id_type=MESH).wait_recv()
        for cp in sends:
            cp.wait_send()
        local.wait()

    return pl.pallas_call(body, name="small_gather", in_specs=[ANY], out_specs=ANY,
                          out_shape=jax.ShapeDtypeStruct((NDEV,) + v.shape, v.dtype),
                          scratch_shapes=[pltpu.SemaphoreType.DMA((NDEV - 1,)), pltpu.SemaphoreType.DMA((NDEV - 1,)),
                                          pltpu.SemaphoreType.DMA],
                          compiler_params=_comm_params())(v)


REL_CHIPS = ((0, 0),) + CHIP_FLIPS


def _rs_d2d(g5s, after=None):
    n = len(g5s)
    extra = () if after is None else (after,)

    def body(*refs):
        ins, gots = refs[:n], refs[n + len(extra):2 * n + len(extra)]
        send_sems, recv_sems = refs[2 * n + len(extra):]
        x, y, c = _place()
        copies = []
        for t in range(n):
            for f, (fx, fy) in enumerate(REL_CHIPS):
                copies.append(pltpu.make_async_remote_copy(
                    src_ref=ins[t].at[_flip(x, fx), _flip(y, fy), 1 - c], dst_ref=gots[t].at[f],
                    send_sem=send_sems.at[4 * t + f], recv_sem=recv_sems.at[4 * t + f], device_id=(x, y, 1 - c),
                    device_id_type=MESH))
        for cp in copies:
            cp.start()
        for cp in copies:
            cp.wait()

    out = [jax.ShapeDtypeStruct((4,) + g.shape[3:], F32) for g in g5s]
    return pl.pallas_call(body, name="rs_d2d", in_specs=[ANY] * (n + len(extra)), out_specs=[ANY] * n, out_shape=out,
                          scratch_shapes=[pltpu.SemaphoreType.DMA((4 * n,)), pltpu.SemaphoreType.DMA((4 * n,))],
                          compiler_params=_comm_params())(*g5s, *extra)


def _rs_ici(parts):
    n = len(parts)

    def body(*refs):
        ins, outs = refs[:n], refs[n:2 * n]
        send_sems, recv_sems = refs[2 * n:]
        x, y, c = _place()
        copies = []
        for t in range(n):
            for f, (fx, fy) in enumerate(CHIP_FLIPS):
                copies.append(pltpu.make_async_remote_copy(
                    src_ref=ins[t].at[f], dst_ref=outs[t].at[f], send_sem=send_sems.at[3 * t + f],
                    recv_sem=recv_sems.at[3 * t + f], device_id=(_flip(x, fx), _flip(y, fy), c), device_id_type=MESH))
        for cp in copies:
            cp.start()
        for cp in copies:
            cp.wait()

    out = [jax.ShapeDtypeStruct(p.shape, p.dtype) for p in parts]
    return pl.pallas_call(body, name="rs_ici", in_specs=[ANY] * n, out_specs=[ANY] * n, out_shape=out,
                          scratch_shapes=[pltpu.SemaphoreType.DMA((3 * n,)), pltpu.SemaphoreType.DMA((3 * n,))],
                          compiler_params=_comm_params())(*parts)


HBM = pl.BlockSpec(memory_space=pltpu.HBM)
SEM = pl.BlockSpec(memory_space=pltpu.SEMAPHORE)
EFFECT = pltpu.SideEffectType.DATAFLOW_SIDE_EFFECTING


def _in_hbm(a):
    return pltpu.with_memory_space_constraint(a, pltpu.HBM)


def _rs_ici_copies(ins, lands, send_sems, recv_sems):
    x, y, c = _place()
    return [pltpu.make_async_remote_copy(
        src_ref=ins[t].at[f], dst_ref=lands[t].at[f], send_sem=send_sems.at[3 * t + f], recv_sem=recv_sems.at[3 * t + f],
        device_id=(_flip(x, fx), _flip(y, fy), c), device_id_type=MESH)
        for t in range(len(ins)) for f, (fx, fy) in enumerate(CHIP_FLIPS)]


def _rs_ici_start(parts, name):
    n = len(parts)

    def body(*refs):
        ins, lands = refs[:n], refs[n:2 * n]
        send_sems, recv_sems = refs[2 * n], refs[2 * n + 1]
        token = refs[-1]
        for cp in _rs_ici_copies(ins, lands, send_sems, recv_sems):
            cp.start()
        token[...] = jnp.zeros(token.shape, token.dtype)

    thru = [pltpu.HBM(p.shape, p.dtype) for p in parts]
    res = pl.pallas_call(
        body, name=name, in_specs=[HBM] * (2 * n),
        out_shape=(pltpu.SemaphoreType.DMA((3 * n,)), pltpu.SemaphoreType.DMA((3 * n,)), *thru, *thru,
                   jax.ShapeDtypeStruct((8, LANE), F32)),
        out_specs=(SEM, SEM, *([HBM] * (2 * n)), pl.BlockSpec(memory_space=pltpu.VMEM)),
        input_output_aliases={i: 2 + i for i in range(2 * n)},
        compiler_params=pltpu.CompilerParams(has_side_effects=EFFECT),
    )(*[_in_hbm(p) for p in parts], *[_in_hbm(lax.empty(p.shape, p.dtype)) for p in parts])
    return res[:-1], res[-1]


def _rs_ici_wait(state, after, name):
    n = (len(state) - 2) // 2

    def body(*refs):
        send_sems, recv_sems = refs[0], refs[1]
        ins, lands = refs[2:2 + n], refs[2 + n:2 + 2 * n]
        for cp in _rs_ici_copies(ins, lands, send_sems, recv_sems):
            cp.wait_send()
            cp.wait_recv()

    thru = [pltpu.HBM(s.shape, s.dtype) for s in state[2:]]
    res = pl.pallas_call(
        body, name=name, in_specs=[SEM, SEM] + [HBM] * (2 * n) + [ANY], out_shape=tuple(thru),
        out_specs=tuple([HBM] * (2 * n)), input_output_aliases={2 + i: i for i in range(2 * n)},
        compiler_params=pltpu.CompilerParams(has_side_effects=EFFECT),
    )(*state, after)
    return list(res[n:])


def _ag_copies(shards, lands, send_sems, recv_sems):
    x, y, c = _place()
    mine = 4 * x + 2 * y + c
    peers = [(x, y, 1 - c)] + [(_flip(x, fx), _flip(y, fy), c) for fx, fy in CHIP_FLIPS]
    return [pltpu.make_async_remote_copy(
        src_ref=shards[t], dst_ref=lands[t].at[mine], send_sem=send_sems.at[4 * t + k], recv_sem=recv_sems.at[4 * t + k],
        device_id=peer, device_id_type=MESH) for t in range(len(shards)) for k, peer in enumerate(peers)]


def _ag_start(shards, after, name):
    n = len(shards)

    def body(*refs):
        ins, lands = refs[:n], refs[n:2 * n]
        send_sems, recv_sems = refs[2 * n + 1], refs[2 * n + 2]
        token = refs[-1]
        for cp in _ag_copies(ins, lands, send_sems, recv_sems):
            cp.start()
        token[...] = jnp.zeros(token.shape, token.dtype)

    thru = [pltpu.HBM(s.shape, s.dtype) for s in shards]
    land = [pltpu.HBM((NDEV,) + s.shape, s.dtype) for s in shards]
    res = pl.pallas_call(
        body, name=name, in_specs=[HBM] * (2 * n) + [ANY],
        out_shape=(pltpu.SemaphoreType.DMA((4 * n,)), pltpu.SemaphoreType.DMA((4 * n,)), *thru, *land,
                   jax.ShapeDtypeStruct((8, LANE), F32)),
        out_specs=(SEM, SEM, *([HBM] * (2 * n)), pl.BlockSpec(memory_space=pltpu.VMEM)),
        input_output_aliases={i: 2 + i for i in range(2 * n)},
        compiler_params=pltpu.CompilerParams(has_side_effects=EFFECT),
    )(*[_in_hbm(s) for s in shards], *[_in_hbm(lax.empty((NDEV,) + s.shape, s.dtype)) for s in shards], after)
    return res[:-1], res[-1]


def _ag_wait(state, after, name):
    n = (len(state) - 2) // 2

    def body(*refs):
        send_sems, recv_sems = refs[0], refs[1]
        ins, lands = refs[2:2 + n], refs[2 + n:2 + 2 * n]
        for cp in _ag_copies(ins, lands, send_sems, recv_sems):
            cp.wait_send()
            cp.wait_recv()

    thru = [pltpu.HBM(s.shape, s.dtype) for s in state[2:]]
    res = pl.pallas_call(
        body, name=name, in_specs=[SEM, SEM] + [HBM] * (2 * n) + [ANY], out_shape=tuple(thru),
        out_specs=tuple([HBM] * (2 * n)), input_output_aliases={2 + i: i for i in range(2 * n)},
        compiler_params=pltpu.CompilerParams(has_side_effects=EFFECT),
    )(*state, after)
    return list(res[:n]), list(res[n:])


def _ag_finish(shards, lands):
    n = len(shards)

    def body(*refs):
        ins, outs, stage = refs[:n], refs[2 * n:3 * n], refs[3 * n:4 * n]
        send_sems, recv_sems, local_sems = refs[4 * n:]
        x, y, c = _place()
        chips = [(_flip(x, fx), _flip(y, fy)) for fx, fy in CHIP_FLIPS]

        def passing(t, j, core, to):
            blk = outs[t].at[4 * chips[j][0] + 2 * chips[j][1] + core]
            return pltpu.make_async_remote_copy(src_ref=blk, dst_ref=blk, send_sem=send_sems.at[3 * t + j],
                                                recv_sem=recv_sems.at[3 * t + j], device_id=to, device_id_type=MESH)

        sends = [passing(t, j, c, (x, y, 1 - c)) for t in range(n) for j in range(3)]
        for cp in sends:
            cp.start()
        load = [pltpu.make_async_copy(ins[t], stage[t], local_sems.at[t]) for t in range(n)]
        mine = [pltpu.make_async_copy(stage[t], outs[t].at[4 * x + 2 * y + c], local_sems.at[t]) for t in range(n)]
        for cp in load:
            cp.start()
        for t in range(n):
            load[t].wait()
            mine[t].start()
        for t in range(n):
            for j in range(3):
                passing(t, j, 1 - c, (x, y, c)).wait_recv()
        for cp in sends:
            cp.wait_send()
        for cp in mine:
            cp.wait()

    return pl.pallas_call(
        body, name="ag_finish", in_specs=[ANY] * (2 * n), out_specs=[ANY] * n,
        out_shape=[jax.ShapeDtypeStruct(l.shape, l.dtype) for l in lands],
        input_output_aliases={n + i: i for i in range(n)},
        scratch_shapes=[pltpu.VMEM(s.shape, s.dtype) for s in shards]
        + [pltpu.SemaphoreType.DMA((3 * n,)), pltpu.SemaphoreType.DMA((3 * n,)), pltpu.SemaphoreType.DMA((n,))],
        compiler_params=_comm_params())(*shards, *lands)


def _row_tile(rows, largest=256):
    for t in (1024, 512, 256, 128, 64, 32, 16, 8):
        if t > largest:
            continue
        if rows % t == 0:
            return t
    return rows


def _rs_chip_sum(pos, g5, got):
    a, b = g5.shape[3:]
    ta = _row_tile(a, 1024)

    def kern(pos_ref, o_ref, g_ref, p_ref):
        p_ref[...] = (o_ref[...] + g_ref[...]).astype(BF16)

    def mine(f, i, pos_ref):
        return (pos_ref[0] ^ ((f + 1) & 1), pos_ref[1] ^ ((f + 1) >> 1), pos_ref[2], i, 0)

    spec = pltpu.PrefetchScalarGridSpec(
        num_scalar_prefetch=1, grid=(3, a // ta),
        in_specs=[BS((None, None, None, ta, b), mine), BS((None, ta, b), lambda f, i, pos_ref: (f + 1, i, 0))],
        out_specs=BS((None, ta, b), lambda f, i, pos_ref: (f, i, 0)))
    return _call(kern, name="rs_chip_sum", grid_spec=spec, out_shape=jax.ShapeDtypeStruct((3, a, b), BF16),
                          compiler_params=_params(2))(pos, g5, got)


def _rs_final_sum(pos, g5, got, recv):
    a, b = g5.shape[3:]
    ta = _row_tile(a, 1024)

    def kern(pos_ref, o_ref, g_ref, r_ref, s_ref):
        acc = o_ref[...] + g_ref[...]
        for f in range(3):
            acc = acc + r_ref[f].astype(F32)
        s_ref[...] = acc

    spec = pltpu.PrefetchScalarGridSpec(
        num_scalar_prefetch=1, grid=(a // ta,),
        in_specs=[BS((None, None, None, ta, b), lambda i, pos_ref: (pos_ref[0], pos_ref[1], pos_ref[2], i, 0)),
                  BS((None, ta, b), lambda i, pos_ref: (0, i, 0)), BS((3, ta, b), lambda i, pos_ref: (0, i, 0))],
        out_specs=BS((ta, b), lambda i, pos_ref: (i, 0)))
    return _call(kern, name="rs_final_sum", grid_spec=spec, out_shape=jax.ShapeDtypeStruct((a, b), F32),
                          compiler_params=_params(1))(pos, g5, got, recv)


def _reduce_scatter(grads, pos):
    g5s = [g.reshape((2, 2, 2) + g.shape[1:]) for g in grads]
    gots = _rs_d2d(g5s)
    parts = [_rs_chip_sum(pos, g, got) for g, got in zip(g5s, gots)]
    recvs = _rs_ici(parts)
    return [_rs_final_sum(pos, g, got, r) for g, got, r in zip(g5s, gots, recvs)]


def _rs_d2d_copies(ins, lands, send_sems, recv_sems):
    x, y, c = _place()
    return [pltpu.make_async_remote_copy(
        src_ref=ins[t].at[_flip(x, fx), _flip(y, fy), 1 - c], dst_ref=lands[t].at[f], send_sem=send_sems.at[4 * t + f],
        recv_sem=recv_sems.at[4 * t + f], device_id=(x, y, 1 - c), device_id_type=MESH)
        for t in range(len(ins)) for f, (fx, fy) in enumerate(REL_CHIPS)]


def _rs_d2d_start(g5s, name):
    n = len(g5s)

    def body(*refs):
        ins, lands = refs[:n], refs[n:2 * n]
        for cp in _rs_d2d_copies(ins, lands, refs[2 * n], refs[2 * n + 1]):
            cp.start()
        refs[-1][...] = jnp.zeros(refs[-1].shape, F32)

    thru = [pltpu.HBM(g.shape, g.dtype) for g in g5s]
    land = [pltpu.HBM((4,) + g.shape[3:], F32) for g in g5s]
    res = pl.pallas_call(
        body, name=name, in_specs=[HBM] * (2 * n),
        out_shape=(pltpu.SemaphoreType.DMA((4 * n,)), pltpu.SemaphoreType.DMA((4 * n,)), *thru, *land,
                   jax.ShapeDtypeStruct((8, LANE), F32)),
        out_specs=(SEM, SEM, *([HBM] * (2 * n)), pl.BlockSpec(memory_space=pltpu.VMEM)),
        input_output_aliases={i: 2 + i for i in range(2 * n)},
        compiler_params=pltpu.CompilerParams(has_side_effects=EFFECT),
    )(*[_in_hbm(g) for g in g5s], *[_in_hbm(lax.empty((4,) + g.shape[3:], F32)) for g in g5s])
    return res[:-1], res[-1]


def _rs_d2d_wait(state, after, name):
    n = (len(state) - 2) // 2

    def body(*refs):
        ins, lands = refs[2:2 + n], refs[2 + n:2 + 2 * n]
        for cp in _rs_d2d_copies(ins, lands, refs[0], refs[1]):
            cp.wait_send()
            cp.wait_recv()

    thru = [pltpu.HBM(s.shape, s.dtype) for s in state[2:]]
    res = pl.pallas_call(
        body, name=name, in_specs=[SEM, SEM] + [HBM] * (2 * n) + [ANY], out_shape=tuple(thru),
        out_specs=tuple([HBM] * (2 * n)), input_output_aliases={2 + i: i for i in range(2 * n)},
        compiler_params=pltpu.CompilerParams(has_side_effects=EFFECT),
    )(*state, after)
    return list(res[:n]), list(res[n:])


def _as_g5(grads):
    return [g.reshape((2, 2, 2) + g.shape[1:]) for g in grads]


def _rs_mid(g5s, gots, pos, tag):
    parts = [_rs_chip_sum(pos, g, got) for g, got in zip(g5s, gots)]
    state, token = _rs_ici_start(parts, "rs_ici_start_" + tag)
    return (g5s, gots, state, tag), token


def _rs_begin(grads, pos, tag, after=None):
    g5s = _as_g5(grads)
    return _rs_mid(g5s, _rs_d2d(g5s, after), pos, tag)


def _rs_end(pending, after, pos):
    g5s, gots, state, tag = pending
    recvs = _rs_ici_wait(state, after, "rs_ici_wait_" + tag)
    return [_rs_final_sum(pos, g, got, r) for g, got, r in zip(g5s, gots, recvs)]


def _sum_devices(v):
    _, r, _ = v.shape

    def kern(v_ref, o_ref):
        acc = v_ref[0]
        for d in range(1, NDEV):
            acc = acc + v_ref[d]
        o_ref[...] = acc

    return _call(kern, name="sum_devices", out_shape=jax.ShapeDtypeStruct((r, LANE), F32),
                          compiler_params=_comm_params())(v)


def _loss_head(xf, target):
    s = xf.shape[0]
    tm = _tile(s, 512)

    def kern(x_ref, t_ref, dx_ref, l_ref):
        err = x_ref[...] - t_ref[...]
        dx_ref[...] = err * (1.0 / DM)
        part = jnp.broadcast_to(0.5 * jnp.sum(jnp.mean(err * err, axis=-1, keepdims=True), axis=0, keepdims=True), (8, LANE))

        @pl.when(pl.program_id(0) == 0)
        def _():
            l_ref[...] = part

        @pl.when(pl.program_id(0) > 0)
        def _():
            l_ref[...] += part

    row = BS((tm, DM), lambda i: (i, 0))
    return _call(kern, name="loss_head", grid=(s // tm,), in_specs=[row, row],
                          out_specs=[row, BS((8, LANE), lambda i: (0, 0))],
                          out_shape=[jax.ShapeDtypeStruct((s, DM), F32), jax.ShapeDtypeStruct((8, LANE), F32)],
                          compiler_params=_params(1))(xf, target)


def _adamw(w, g, m, v, after=None):
    rows, cols = w.shape
    tr = _row_tile(rows)
    extra = () if after is None else (after,)

    def kern(w_ref, g_ref, m_ref, v_ref, *rest):
        d_ref, nm_ref, nv_ref = rest[-3:]
        gv = g_ref[...]
        nm = ADAM_B1 * m_ref[...] + (1.0 - ADAM_B1) * gv
        nv = ADAM_B2 * v_ref[...] + (1.0 - ADAM_B2) * (gv * gv)
        m_hat = nm / (1.0 - ADAM_B1 ** ADAM_STEP)
        v_hat = nv / (1.0 - ADAM_B2 ** ADAM_STEP)
        d_ref[...] = -ADAM_LR * (m_hat / (jnp.sqrt(v_hat) + ADAM_EPS) + ADAM_WD * w_ref[...])
        nm_ref[...] = nm
        nv_ref[...] = nv

    blk = BS((tr, cols), lambda i: (i, 0))
    shp = jax.ShapeDtypeStruct((rows, cols), F32)
    return _call(kern, name="adamw", grid=(rows // tr,), in_specs=[blk] * 4 + [ANY] * len(extra),
                          out_specs=[blk] * 3, out_shape=[shp] * 3, compiler_params=_params(1))(w, g, m, v, *extra)


def _adamw_nd(w, g, m, v, after=None):
    shape = w.shape
    two = (math.prod(shape[:-1]), shape[-1])
    return tuple(o.reshape(shape)
                 for o in _adamw(w.reshape(two), g.reshape(two), m.reshape(two), v.reshape(two), after))


def _pack_small(parts):
    flat = jnp.concatenate([p.reshape(-1) for p in parts])
    pad = (-flat.shape[0]) % (8 * LANE)
    return jnp.pad(flat, (0, pad)).reshape(-1, LANE)


def _unpack_small(packed, shapes, lead=()):
    flat = packed.reshape(lead + (-1,))
    out, off = [], 0
    for shp in shapes:
        n = math.prod(shp)
        out.append(flat[..., off:off + n].reshape(lead + tuple(shp)))
        off += n
    return out


def _blocks_of_columns(w):
    k, n = w.shape
    return w.reshape(k, NDEV, n // NDEV).transpose(1, 0, 2)


def _columns_of_blocks(wb):
    n, k, c = wb.shape
    return wb.transpose(1, 0, 2).reshape(k, n * c)


WEIGHT_NAMES = ('g_mix_pre', 'g_mix_post', 'g_cross_pre', 'g_mem', 'g_cross_post', 'g_ffn_pre', 'g_ffn_post', 'w_xq',
                'w_xkv', 'w_xo', 'w_ffn_gu', 'w_ffn_down', 'ab_w_in', 'ab_b_f', 'ab_conv_w', 'ab_w_out', 'c_w_in',
                'c_conv_w', 'c_conv_b', 'c_w_a', 'c_b_a', 'c_w_i', 'c_b_i', 'c_lam', 'c_w_out')
BIG = ('w_xq', 'w_xkv', 'w_xo', 'w_ffn_gu', 'w_ffn_down', 'ab_w_in', 'ab_w_out', 'c_w_in', 'c_w_a', 'c_w_i', 'c_w_out')
SMALL_SHARDED = ('ab_conv_w', 'c_conv_w', 'c_conv_b', 'c_b_a', 'c_b_i', 'c_lam')
REPLICATED = ('g_mix_pre', 'g_mix_post', 'g_cross_pre', 'g_mem', 'g_cross_post', 'g_ffn_pre', 'g_ffn_post', 'ab_b_f')


def _small_full(name, gathered):
    nd = gathered.ndim
    return jnp.moveaxis(gathered, 0, nd - 2).reshape(gathered.shape[1:-1] + (NDEV * gathered.shape[-1],))


def _small_shard(full, dev):
    c = full.shape[-1] // NDEV
    return lax.dynamic_slice_in_dim(full, dev * c, c, axis=full.ndim - 1)


def kernel(x, mem, g_mix_pre, g_mix_post, g_cross_pre, g_mem, g_cross_post, g_ffn_pre, g_ffn_post, w_xq, w_xkv, w_xo, w_ffn_gu, w_ffn_down, ab_w_in, ab_b_f, ab_conv_w, ab_w_out, c_w_in, c_conv_w, c_conv_b, c_w_a, c_b_a, c_w_i, c_b_i, c_lam, c_w_out, loss_target, m_g_mix_pre, m_g_mix_post, m_g_cross_pre, m_g_mem, m_g_cross_post, m_g_ffn_pre, m_g_ffn_post, m_w_xq, m_w_xkv, m_w_xo, m_w_ffn_gu, m_w_ffn_down, m_ab_w_in, m_ab_b_f, m_ab_conv_w, m_ab_w_out, m_c_w_in, m_c_conv_w, m_c_conv_b, m_c_w_a, m_c_b_a, m_c_w_i, m_c_b_i, m_c_lam, m_c_w_out, v_g_mix_pre, v_g_mix_post, v_g_cross_pre, v_g_mem, v_g_cross_post, v_g_ffn_pre, v_g_ffn_post, v_w_xq, v_w_xkv, v_w_xo, v_w_ffn_gu, v_w_ffn_down, v_ab_w_in, v_ab_b_f, v_ab_conv_w, v_ab_w_out, v_c_w_in, v_c_conv_w, v_c_conv_b, v_c_w_a, v_c_b_a, v_c_w_i, v_c_b_i, v_c_lam, v_c_w_out):
    args = locals()
    w = {n: args[n] for n in WEIGHT_NAMES}
    mom = {n: args["m_" + n] for n in WEIGHT_NAMES}
    var = {n: args["v_" + n] for n in WEIGHT_NAMES}
    for t in (w, mom, var):
        t['w_ffn_gu'] = t['w_ffn_gu'].transpose(0, 2, 1)
    pos = jnp.stack([lax.axis_index("x"), lax.axis_index("y"), lax.axis_index("c")]).astype(jnp.int32)
    dev = 4 * pos[0] + 2 * pos[1] + pos[2]
    xs, mems, target = x[0], mem[0], loss_target[0]
    n_even, n_odd = (DEPTH + 1) // 2, DEPTH // 2

    small_shapes = [w[n].shape for n in SMALL_SHARDED]
    small_w_all = _small_gather(_pack_small([w[n] for n in SMALL_SHARDED]))
    gathered_small = _unpack_small(small_w_all, small_shapes, (NDEV,))
    small = {n: _small_full(n, g) for n, g in zip(SMALL_SHARDED, gathered_small)}
    ab_bfb = jnp.broadcast_to(ab_b_f[:, :, None], (n_even, FOX_H, LANE))
    c_bai = jnp.stack([small['c_b_a'].reshape(n_odd, DM), small['c_b_i'].reshape(n_odd, DM)], axis=1)
    row = lambda a, l: a[l][None]

    REST = ('w_xq', 'w_xkv', 'w_xo', 'w_ffn_gu', 'w_ffn_down')

    def mixer_names(l):
        return ('ab_w_in', 'ab_w_out') if l % 2 == 0 else ('c_w_in', 'c_w_a', 'c_w_i', 'c_w_out')

    def shards_of(l, names):
        out = []
        for n in names:
            s = w[n][l if w[n].shape[0] == DEPTH else l // 2].astype(BF16)
            out.append(s.reshape(-1, s.shape[-1]))
        return out

    def mixer_weights(l, full):
        if l % 2 == 0:
            e = l // 2
            return (row(g_mix_pre, l), row(g_mix_post, l), _ab_pack(_columns_of_blocks(full['ab_w_in'])), ab_bfb[e],
                    small['ab_conv_w'][e], full['ab_w_out'].reshape(DM, DM))
        o = l // 2
        gate_w = lambda g: g.reshape(NDEV, LRU_NB, LRU_BW // NDEV, LRU_BW).transpose(1, 0, 2, 3).reshape(
            LRU_NB, LRU_BW, LRU_BW)
        return (row(g_mix_pre, l), row(g_mix_post, l), full['c_w_in'], small['c_conv_w'][o], row(small['c_conv_b'], o),
                jnp.stack([gate_w(full['c_w_a']), gate_w(full['c_w_i'])]), c_bai[o], row(small['c_lam'], o),
                full['c_w_out'].reshape(DM, DM))

    def rest_weights(l, full):
        cross = (row(g_cross_pre, l), row(g_mem, l), row(g_cross_post, l), full['w_xq'].reshape(DM, DM), full['w_xkv'],
                 full['w_xo'].reshape(DM, DM))
        ffn = (row(g_ffn_pre, l), row(g_ffn_post, l), full['w_ffn_gu'], full['w_ffn_down'].reshape(D_FF, DM))
        return cross, ffn

    def gathered(state, names, after, tag):
        shards, lands = _ag_wait(state, after, "ag_wait_" + tag)
        full = _ag_finish(shards, lands)
        return dict(zip(names, full)), full[0]

    saved, weights = [], []
    h = xs
    names_of = lambda l: mixer_names(l) + REST
    states = {}
    st_m, _ = _ag_start(shards_of(0, mixer_names(0)), small_w_all, "ag_start_0m")
    st_r, _ = _ag_start(shards_of(0, REST), st_m[2], "ag_start_0r")
    states[1], token = _ag_start(shards_of(1, names_of(1)), st_r[2], "ag_start_1")
    full_m, _ = gathered(st_m, mixer_names(0), xs, "0m")
    for l in range(DEPTH):
        if l > 0:
            full, done = gathered(states[l], names_of(l), h, str(l))
            full_m = full_r = full
            token = None
            if l + 2 < DEPTH:
                states[l + 2], token = _ag_start(shards_of(l + 2, names_of(l + 2)), done, "ag_start_%d" % (l + 2))
        mixer = mixer_weights(l, full_m)
        h, s_mix = (_fox_layer_fwd if l % 2 == 0 else _lru_layer_fwd)(h, *mixer, after=token)
        token = None
        if l == 0:
            full_r, done = gathered(st_r, REST, h, "0r")
            states[2], token = _ag_start(shards_of(2, names_of(2)), done, "ag_start_2")
        cross, ffn = rest_weights(l, full_r)
        h, s_cross = _cross_fwd(h, mems, *cross, after=token)
        h, s_ffn = _ffn_fwd(h, *ffn)
        saved.append((s_mix, s_cross, s_ffn))
        weights.append((mixer, cross, ffn))
    mixer_args = lambda l: weights[l][0]
    cross_args = lambda l: weights[l][1]
    ffn_args = lambda l: weights[l][2]
    dx, loss_rep = _loss_head(h, target)
    loss = lax.psum(loss_rep[0, 0], ("x", "y", "c"))

    grads = {n: [None] * w[n].shape[0] for n in BIG}
    partial = {n: [None] * w[n].shape[0] for n in REPLICATED + SMALL_SHARDED}
    def finish(pending, after):
        state, names, where = pending
        for n, g in zip(names, _rs_end(state, after, pos)):
            grads[n][where[n]] = g

    def unit(layer, names):
        return [layer[n][1] for n in names], names, {n: layer[n][0] for n in names}

    d2d = ici = None
    token = None
    for l in reversed(range(DEPTH)):
        s_mix, s_cross, s_ffn = saved[l]
        dx, partial['g_ffn_pre'][l], partial['g_ffn_post'][l], dwgu, dwd = _ffn_bwd(dx, s_ffn, *ffn_args(l), after=token)
        token = None
        if d2d is not None:
            g5s, gots = _rs_d2d_wait(d2d[0], dx, "rs_d2d_wait_%d" % (l + 1))
            state, token = _rs_mid(g5s, gots, pos, str(l + 1))
            ici, d2d = (state,) + d2d[1:], None
        (dx, partial['g_cross_pre'][l], partial['g_mem'][l], partial['g_cross_post'][l], dwq, dwkv, dwo) = _cross_bwd(
            dx, s_cross, mems, *cross_args(l), after=token)
        token = None
        layer = {'w_xq': (l, dwq.reshape(NDEV, DM // NDEV, DM)), 'w_xkv': (l, dwkv), 'w_xo': (l, dwo.reshape(NDEV, DM // NDEV, DM)),
                 'w_ffn_gu': (l, dwgu), 'w_ffn_down': (l, dwd.reshape(NDEV, D_FF // NDEV, DM))}
        if l == 0:
            gs, names, where = unit(layer, REST)
            state, token = _rs_begin(gs, pos, "0r")
            ici_rest = (state, names, where)
        if l % 2 == 0:
            e = l // 2
            (dx, partial['g_mix_pre'][l], partial['g_mix_post'][l], dwall, partial['ab_b_f'][e], partial['ab_conv_w'][e],
             dwout) = _fox_layer_bwd(dx, s_mix, *mixer_args(l), after=token)
            layer['ab_w_in'] = (e, _blocks_of_columns(_ab_unpack(dwall)))
            layer['ab_w_out'] = (e, dwout.reshape(NDEV, DM // NDEV, DM))
        else:
            o = l // 2
            (dx, partial['g_mix_pre'][l], partial['g_mix_post'][l], dwin, partial['c_conv_w'][o], dconvb, dwai, dbai, dlam,
             dwout) = _lru_layer_bwd(dx, s_mix, *mixer_args(l), after=token)
            partial['c_conv_b'][o], partial['c_lam'][o] = dconvb[0], dlam[0]
            partial['c_b_a'][o], partial['c_b_i'][o] = dbai[0].reshape(LRU_NB, LRU_BW), dbai[1].reshape(LRU_NB, LRU_BW)
            rows = LRU_BW // NDEV
            by_dev = lambda d: d.reshape(LRU_NB, NDEV, rows, LRU_BW).transpose(1, 0, 2, 3).reshape(NDEV, LRU_NB * rows, LRU_BW)
            layer['c_w_in'] = (o, dwin)
            layer['c_w_a'] = (o, by_dev(dwai[0]))
            layer['c_w_i'] = (o, by_dev(dwai[1]))
            layer['c_w_out'] = (o, dwout.reshape(NDEV, DM // NDEV, DM))
        token = None
        if ici is not None:
            finish(ici, dx)
            ici = None
        if l > 0:
            gs, names, where = unit(layer, list(layer))
            state, token = _rs_d2d_start(_as_g5(gs), "rs_d2d_start_%d" % l)
            d2d = (state, names, where)
    small_names = REPLICATED + SMALL_SHARDED
    small_parts = [jnp.stack([p.reshape(w[n].shape[1:] if n in REPLICATED else small[n].shape[1:]) for p in partial[n]])
                   for n in small_names]
    small_all = _small_gather(_pack_small(small_parts))
    reduced = _unpack_small(_sum_devices(small_all), [p.shape for p in small_parts])
    grad = {}
    for n, g in zip(small_names, reduced):
        grad[n] = g if n in REPLICATED else _small_shard(g, dev)

    gs, names, where = unit(layer, mixer_names(0))
    state, token = _rs_begin(gs, pos, "0m", after=small_all)
    ici_mixer = (state, names, where)
    finish(ici_rest, dx)

    delta, new_m, new_v = {}, {}, {}
    last = mixer_names(0)
    for n in BIG:
        if n not in last:
            grad[n] = jnp.stack(grads[n]).reshape(w[n].shape)
            delta[n], new_m[n], new_v[n] = _adamw_nd(w[n], grad[n], mom[n], var[n], token)
            token = delta[n]
    shapes = [w[n].shape for n in small_names]
    packed = [_pack_small([t[n] for n in small_names]) for t in (w, grad, mom, var)]
    res_small = _adamw(*packed, after=token)
    for res, out in zip(res_small, (delta, new_m, new_v)):
        for n, val in zip(small_names, _unpack_small(res, shapes)):
            out[n] = val
    finish(ici_mixer, res_small[0])
    for n in last:
        grad[n] = jnp.stack(grads[n]).reshape(w[n].shape)
        delta[n], new_m[n], new_v[n] = _adamw_nd(w[n], grad[n], mom[n], var[n])

    for t in (grad, delta, new_m, new_v):
        t['w_ffn_gu'] = t['w_ffn_gu'].transpose(0, 2, 1)
    return (loss, dx[None], *[grad[n] for n in WEIGHT_NAMES], *[delta[n] for n in WEIGHT_NAMES],
            *[new_m[n] for n in WEIGHT_NAMES], *[new_v[n] for n in WEIGHT_NAMES])
```

```python
import math

import jax
import jax.numpy as jnp
from jax import lax
from jax.experimental import pallas as pl
from jax.experimental.pallas import tpu as pltpu

F32 = jnp.float32
BF16 = jnp.bfloat16
BS = pl.BlockSpec
ANY = pl.BlockSpec(memory_space=pl.ANY)
MESH = pl.DeviceIdType.MESH

DM = 1024
DEPTH = 4
EPS = 1e-6
NEG = -1e30
FOX_W = 512
FOX_HD = 64
FOX_H = 8
SC_W = 512
SC_K = 3
AB_IN = 3 * FOX_W + FOX_H + 3 * SC_W
AB_PAD = 3200
LRU_BW = 256
LRU_NB = 4
RG_K = 4
RG_C = 8.0
MEM_H = 4
MEM_HD = 256
D_FF = 2816
NDEV = 8
FFB = 2 * D_FF // NDEV
ADAM_LR, ADAM_B1, ADAM_B2, ADAM_EPS, ADAM_WD, ADAM_STEP = 0.001, 0.9, 0.999, 1e-08, 0.01, 10

LANE = 128
VMEM_LIMIT = 24 * 1024 * 1024
VMEM_BIG = 48 * 1024 * 1024


def _params(ngrid, vmem=None):
    return pltpu.CompilerParams(dimension_semantics=("arbitrary",) * ngrid, vmem_limit_bytes=vmem or VMEM_LIMIT)


def _call(kern, **kwargs):
    return pl.pallas_call(kern, **kwargs)


TK_RED = 2048
TM_SUM = 512


def _tile(n, t):
    return t if n % t == 0 else n


def _mm(name, a, b, *, grid, a_spec, b_spec, o_spec, out_shape, dn, out_dtype=F32, vmem=None):
    nred = grid[-1]
    ngrid = len(grid)

    def kern(a_ref, b_ref, o_ref, *scratch):
        p = lax.dot_general(a_ref[...].astype(BF16), b_ref[...].astype(BF16), (dn, ((), ())),
                            preferred_element_type=F32)
        if nred == 1:
            o_ref[...] = p.astype(o_ref.dtype)
            return
        acc = scratch[0] if scratch else o_ref
        r = pl.program_id(ngrid - 1)

        @pl.when(r == 0)
        def _():
            acc[...] = p

        @pl.when(r > 0)
        def _():
            acc[...] += p

        if scratch:
            @pl.when(r == nred - 1)
            def _():
                o_ref[...] = acc[...].astype(o_ref.dtype)

    blk = tuple(d for d in o_spec.block_shape if d is not None)
    scratch = [pltpu.VMEM(blk, F32)] if (nred > 1 and out_dtype != F32) else []
    return _call(kern, name=name, grid=grid, in_specs=[a_spec, b_spec], out_specs=o_spec,
                          out_shape=jax.ShapeDtypeStruct(out_shape, out_dtype), scratch_shapes=scratch,
                          compiler_params=_params(ngrid, vmem))(a, b)


NN = ((1,), (0,))
NT = ((1,), (1,))
TN = ((0,), (0,))


def _mm_nn(name, a, w, out_dtype=F32, tn=None, vmem=None):
    m, k = a.shape
    n = w.shape[1]
    tm = _tile(m, 512)
    tn = n if tn is None else tn
    return _mm(name, a, w, grid=(m // tm, n // tn, 1), a_spec=BS((tm, k), lambda i, j, r: (i, 0)),
               b_spec=BS((k, tn), lambda i, j, r: (0, j)), o_spec=BS((tm, tn), lambda i, j, r: (i, j)),
               out_shape=(m, n), dn=NN, out_dtype=out_dtype, vmem=vmem)


def _mm_nt_cols(name, a, wt, tn):
    m, k = a.shape
    n = wt.shape[0]
    tm = _tile(m, 512)
    return _mm(name, a, wt, grid=(m // tm, n // tn, 1), a_spec=BS((tm, k), lambda i, j, r: (i, 0)),
               b_spec=BS((tn, k), lambda i, j, r: (j, 0)), o_spec=BS((tm, tn), lambda i, j, r: (i, j)),
               out_shape=(m, n), dn=NT)


def _mm_tn_rows(name, a, b, tk):
    m, k = a.shape
    n = b.shape[1]
    tm = _tile(m, TK_RED)
    return _mm(name, a, b, grid=(k // tk, m // tm), a_spec=BS((tm, tk), lambda j, r: (r, j)),
               b_spec=BS((tm, n), lambda j, r: (r, 0)), o_spec=BS((tk, n), lambda j, r: (j, 0)),
               out_shape=(k, n), dn=TN)


def _mm_nt(name, a, w, out_dtype=F32, tn=None):
    m, n = a.shape
    k = w.shape[0]
    tm = _tile(m, 512)
    tn = n if tn is None else tn
    return _mm(name, a, w, grid=(m // tm, n // tn), a_spec=BS((tm, tn), lambda i, r: (i, r)),
               b_spec=BS((k, tn), lambda i, r: (0, r)), o_spec=BS((tm, k), lambda i, r: (i, 0)),
               out_shape=(m, k), dn=NT, out_dtype=out_dtype)


def _mm_tn(name, a, b, tn=None):
    m, k = a.shape
    n = b.shape[1]
    tm = _tile(m, TK_RED)
    tn = n if tn is None else tn
    return _mm(name, a, b, grid=(n // tn, m // tm), a_spec=BS((tm, k), lambda j, r: (r, 0)),
               b_spec=BS((tm, tn), lambda j, r: (r, j)), o_spec=BS((k, tn), lambda j, r: (0, j)),
               out_shape=(k, n), dn=TN)


def _bmm_nn(name, a, w, out_dtype=F32):
    m, k = a.shape
    g, _, n = w.shape
    tm = _tile(m, 512)
    return _mm(name, a, w, grid=(g, m // tm, 1), a_spec=BS((tm, k), lambda q, i, r: (i, 0)),
               b_spec=BS((None, k, n), lambda q, i, r: (q, 0, 0)), o_spec=BS((None, tm, n), lambda q, i, r: (q, i, 0)),
               out_shape=(g, m, n), dn=NN, out_dtype=out_dtype)


def _bmm_tn(name, a, b):
    m, k = a.shape
    g, _, n = b.shape
    tm = _tile(m, TK_RED)
    return _mm(name, a, b, grid=(g, m // tm), a_spec=BS((tm, k), lambda q, r: (r, 0)),
               b_spec=BS((None, tm, n), lambda q, r: (q, r, 0)), o_spec=BS((None, k, n), lambda q, r: (q, 0, 0)),
               out_shape=(g, k, n), dn=TN)


def _block_sum(name, a, w, dn, out_cols):
    g, m, ac = a.shape
    tm = _tile(m, TM_SUM)

    def kern(a_ref, w_ref, o_ref):
        acc = None
        for q in range(g):
            p = lax.dot_general(a_ref[q].astype(BF16), w_ref[q].astype(BF16), (dn, ((), ())), preferred_element_type=F32)
            acc = p if acc is None else acc + p
        o_ref[...] = acc

    return _call(kern, name=name, grid=(m // tm,),
                 in_specs=[BS((g, tm, ac), lambda i: (0, i, 0)), BS(w.shape, lambda i: (0, 0, 0))],
                 out_specs=BS((tm, out_cols), lambda i: (i, 0)), out_shape=jax.ShapeDtypeStruct((m, out_cols), F32),
                 compiler_params=_params(1, VMEM_BIG))(a, w)


def _bmm_nt_sum(name, a, w):
    return _block_sum(name, a, w, NT, w.shape[1])


def _bmm_nn_sum(name, a, w):
    return _block_sum(name, a, w, NN, w.shape[2])


def _rstd(x):
    return lax.rsqrt(jnp.mean(x * x, axis=-1, keepdims=True) + EPS)


def _norm_fwd(x, g, after=None):
    rows = x.shape[0]
    tm = _tile(rows, 512)

    def kern(x_ref, g_ref, *rest):
        xv = x_ref[...]
        rest[-1][...] = ((xv * _rstd(xv)) * g_ref[...]).astype(BF16)

    extra = () if after is None else (after,)
    return _call(kern, name="norm_fwd", grid=(rows // tm,),
                          in_specs=[BS((tm, DM), lambda i: (i, 0)), BS((1, DM), lambda i: (0, 0))] + [ANY] * len(extra),
                          out_specs=BS((tm, DM), lambda i: (i, 0)),
                          out_shape=jax.ShapeDtypeStruct((rows, DM), BF16), compiler_params=_params(1))(x, g, *extra)


def _norm_res(x, y, g):
    rows = x.shape[0]
    tm = _tile(rows, 512)

    def kern(x_ref, y_ref, g_ref, o_ref):
        yv = y_ref[...]
        o_ref[...] = x_ref[...] + (yv * _rstd(yv)) * g_ref[...]

    row = BS((tm, DM), lambda i: (i, 0))
    return _call(kern, name="norm_res", grid=(rows // tm,),
                          in_specs=[row, row, BS((1, DM), lambda i: (0, 0))], out_specs=row,
                          out_shape=jax.ShapeDtypeStruct((rows, DM), F32), compiler_params=_params(1))(x, y, g)


def _norm_bwd(z, dout, g, resid, out_dtype, after=None):
    rows = z.shape[0]
    tm = _tile(rows, 512)
    has_res = resid is not None

    def kern(*refs):
        z_ref, d_ref, g_ref = refs[:3]
        r_ref = refs[3] if has_res else None
        dz_ref, dg_ref = refs[-2:]
        zv = z_ref[...]
        dv = d_ref[...].astype(F32)
        r = _rstd(zv)
        zh = zv * r
        dzh = dv * g_ref[...]
        dz = r * (dzh - zh * jnp.mean(dzh * zh, axis=-1, keepdims=True))
        if has_res:
            dz = dz + r_ref[...]
        dz_ref[...] = dz.astype(dz_ref.dtype)
        part = jnp.sum(dv * zh, axis=0, keepdims=True)

        @pl.when(pl.program_id(0) == 0)
        def _():
            dg_ref[...] = part

        @pl.when(pl.program_id(0) > 0)
        def _():
            dg_ref[...] += part

    row = BS((tm, DM), lambda i: (i, 0))
    vec = BS((1, DM), lambda i: (0, 0))
    ins = [row, row, vec] + ([row] if has_res else []) + ([ANY] if after is not None else [])
    args = (z, dout, g) + ((resid,) if has_res else ()) + ((after,) if after is not None else ())
    return _call(kern, name="norm_bwd_res" if has_res else "norm_bwd", grid=(rows // tm,), in_specs=ins,
                          out_specs=[row, vec],
                          out_shape=[jax.ShapeDtypeStruct((rows, DM), out_dtype), jax.ShapeDtypeStruct((1, DM), F32)],
                          compiler_params=_params(1))(*args)


def _ffn_up(h, wgu4):
    s = h.shape[0]
    tm = _tile(s, 512)

    def kern(h_ref, w_ref, gu_ref, a_ref):
        hv = h_ref[...]
        gate = lax.dot_general(hv, w_ref[0], (NT, ((), ())), preferred_element_type=F32)
        up = lax.dot_general(hv, w_ref[1], (NT, ((), ())), preferred_element_type=F32)
        gu_ref[0] = gate.astype(BF16)
        gu_ref[1] = up.astype(BF16)
        a_ref[...] = (gate * jax.nn.sigmoid(gate) * up).astype(BF16)

    return _call(
        kern, name="ffn_up", grid=(4, s // tm),
        in_specs=[BS((tm, DM), lambda j, i: (i, 0)), BS((2, None, FFB, DM), lambda j, i: (0, j, 0, 0))],
        out_specs=[BS((2, None, tm, FFB), lambda j, i: (0, j, i, 0)), BS((None, tm, FFB), lambda j, i: (j, i, 0))],
        out_shape=[jax.ShapeDtypeStruct((2, 4, s, FFB), BF16), jax.ShapeDtypeStruct((4, s, FFB), BF16)],
        compiler_params=_params(2))(h, wgu4)


def _ffn_da(dy, wd4, gu):
    s = dy.shape[0]
    tm = _tile(s, 512)

    def kern(dy_ref, w_ref, gu_ref, o_ref):
        da = lax.dot_general(dy_ref[...], w_ref[...], (NT, ((), ())), preferred_element_type=F32)
        gate = gu_ref[0].astype(F32)
        up = gu_ref[1].astype(F32)
        sg = jax.nn.sigmoid(gate)
        o_ref[0] = (da * up * (sg * (1.0 + gate * (1.0 - sg)))).astype(BF16)
        o_ref[1] = (da * (gate * sg)).astype(BF16)

    blk = BS((2, None, tm, FFB), lambda j, i: (0, j, i, 0))
    return _call(
        kern, name="ffn_da", grid=(4, s // tm),
        in_specs=[BS((tm, DM), lambda j, i: (i, 0)), BS((None, FFB, DM), lambda j, i: (j, 0, 0)), blk],
        out_specs=blk, out_shape=jax.ShapeDtypeStruct((2, 4, s, FFB), BF16), compiler_params=_params(2))(dy, wd4, gu)


def _ffn_fwd(x, gpre, gpost, wgu, wd):
    h = _norm_fwd(x, gpre)
    gu, a = _ffn_up(h, wgu.reshape(2, 4, FFB, DM))
    y = _bmm_nn_sum("ffn_down", a, wd.reshape(4, FFB, DM))
    return _norm_res(x, y, gpost), (x, h, gu, a, y)


def _ffn_bwd(dxo, saved, gpre, gpost, wgu, wd, after=None):
    x, h, gu, a, y = saved
    s = x.shape[0]
    dy, dgpost = _norm_bwd(y, dxo, gpost, None, BF16, after)
    dgu = _ffn_da(dy, wd.reshape(4, FFB, DM), gu).reshape(8, s, FFB)
    dwd = _bmm_tn_a3("ffn_dwd", a, dy)
    dwgu = _bmm_tn_a3("ffn_dwgu", dgu, h)
    dh = _bmm_nn_sum("ffn_dh", dgu, wgu)
    dx, dgpre = _norm_bwd(x, dh, gpre, dxo, F32)
    return dx, dgpre, dgpost, dwgu, dwd.reshape(D_FF, DM)


def _bmm_tn_a3(name, a, b):
    g, m, k = a.shape
    n = b.shape[1]
    tm = _tile(m, TK_RED)
    return _mm(name, a, b, grid=(g, m // tm), a_spec=BS((None, tm, k), lambda q, r: (q, r, 0)),
               b_spec=BS((tm, n), lambda q, r: (r, 0)), o_spec=BS((None, k, n), lambda q, r: (q, 0, 0)),
               out_shape=(g, k, n), dn=TN)


def _softmax_rows(s):
    m = jnp.max(s, axis=-1, keepdims=True)
    p = jnp.exp(s - m)
    return p / jnp.sum(p, axis=-1, keepdims=True)


def _xattn_fwd_call(h, wq, kv):
    s = h.shape[0]
    mlen = kv.shape[1]
    tm = _tile(s, 512)
    scale = MEM_HD ** -0.5

    def kern(h_ref, w_ref, k_ref, v_ref, q_ref, o_ref):
        q = jnp.dot(h_ref[...], w_ref[...], preferred_element_type=F32).astype(BF16)
        q_ref[...] = q
        sc = lax.dot_general(q, k_ref[...], (NT, ((), ())), preferred_element_type=F32) * scale
        p = _softmax_rows(sc)
        o_ref[...] = jnp.dot(p.astype(BF16), v_ref[...], preferred_element_type=F32).astype(BF16)

    blk = BS((tm, MEM_HD), lambda i, hd: (i, hd))
    return _call(
        kern, name="xattn_fwd", grid=(s // tm, MEM_H),
        in_specs=[BS((tm, DM), lambda i, hd: (i, 0)), BS((DM, MEM_HD), lambda i, hd: (0, hd)),
                  BS((None, mlen, MEM_HD), lambda i, hd: (hd, 0, 0)),
                  BS((None, mlen, MEM_HD), lambda i, hd: (MEM_H + hd, 0, 0))],
        out_specs=[blk, blk],
        out_shape=[jax.ShapeDtypeStruct((s, DM), BF16), jax.ShapeDtypeStruct((s, DM), BF16)],
        compiler_params=_params(2))(h, wq, kv, kv)


def _xattn_bwd_call(q, kv, do):
    s = q.shape[0]
    mlen = kv.shape[1]
    tm = _tile(s, 512)
    scale = MEM_HD ** -0.5

    def kern(q_ref, k_ref, v_ref, do_ref, dq_ref, dkv_ref):
        qv, kvv, vv, dov = q_ref[...], k_ref[...], v_ref[...], do_ref[...]
        sc = lax.dot_general(qv, kvv, (NT, ((), ())), preferred_element_type=F32) * scale
        p = _softmax_rows(sc)
        dp = lax.dot_general(dov, vv, (NT, ((), ())), preferred_element_type=F32)
        ds = (p * (dp - jnp.sum(dp * p, axis=-1, keepdims=True)) * scale).astype(BF16)
        dq_ref[...] = jnp.dot(ds, kvv, preferred_element_type=F32).astype(BF16)
        dk = lax.dot_general(ds, qv, (TN, ((), ())), preferred_element_type=F32)
        dv = lax.dot_general(p.astype(BF16), dov, (TN, ((), ())), preferred_element_type=F32)

        @pl.when(pl.program_id(1) == 0)
        def _():
            dkv_ref[0] = dk
            dkv_ref[1] = dv

        @pl.when(pl.program_id(1) > 0)
        def _():
            dkv_ref[0] += dk
            dkv_ref[1] += dv

    blk = BS((tm, MEM_HD), lambda hd, i: (i, hd))
    return _call(
        kern, name="xattn_bwd", grid=(MEM_H, s // tm),
        in_specs=[blk, BS((None, mlen, MEM_HD), lambda hd, i: (hd, 0, 0)),
                  BS((None, mlen, MEM_HD), lambda hd, i: (MEM_H + hd, 0, 0)), blk],
        out_specs=[blk, BS((2, None, mlen, MEM_HD), lambda hd, i: (0, hd, 0, 0))],
        out_shape=[jax.ShapeDtypeStruct((s, DM), BF16), jax.ShapeDtypeStruct((2, MEM_H, mlen, MEM_HD), F32)],
        compiler_params=_params(2))(q, kv, kv, do)


def _cross_fwd(x, mem, gpre, gmem, gpost, wq, wkv, wo, after=None):
    h = _norm_fwd(x, gpre, after)
    mn = _norm_fwd(mem, gmem)
    kv = _bmm_nn("xattn_kv", mn, wkv, BF16)
    q, o = _xattn_fwd_call(h, wq, kv)
    y = _mm_nn("xattn_out", o, wo)
    return _norm_res(x, y, gpost), (x, h, mn, kv, q, o, y)


def _cross_bwd(dxo, saved, mem, gpre, gmem, gpost, wq, wkv, wo, after=None):
    x, h, mn, kv, q, o, y = saved
    mlen = mem.shape[0]
    dy, dgpost = _norm_bwd(y, dxo, gpost, None, BF16, after)
    do = _mm_nt("xattn_do", dy, wo, BF16)
    dwo = _mm_tn("xattn_dwo", o, dy)
    dq, dkv = _xattn_bwd_call(q, kv, do)
    dwq = _mm_tn("xattn_dwq", h, dq)
    dh = _mm_nt("xattn_dh", dq, wq)
    dkv8 = dkv.reshape(8, mlen, MEM_HD)
    dwkv = _bmm_tn("xattn_dwkv", mn, dkv8)
    dmn = _bmm_nt_sum("xattn_dmn", dkv8, wkv)
    _, dgmem = _norm_bwd(mem, dmn, gmem, None, BF16)
    dx, dgpre = _norm_bwd(x, dh, gpre, dxo, F32)
    return dx, dgpre, dgmem, dgpost, dwq, dwkv, dwo


def _log_sigmoid(z):
    return jnp.minimum(z, 0.0) - jnp.log1p(jnp.exp(-jnp.abs(z)))


def _lane_scan_steps():
    return (1, 2, 4, 8, 16, 32, 64)


def _fox_cum(frow, bfb):
    s = frow.shape[1]

    def kern(f_ref, b_ref, o_ref):
        lane = lax.broadcasted_iota(jnp.int32, (FOX_H, LANE), 1)
        carry = jnp.zeros((FOX_H, 1), F32)
        for c in range(s // LANE):
            sl = slice(c * LANE, (c + 1) * LANE)
            lf = _log_sigmoid(f_ref[:, sl] + b_ref[...])
            v = lf
            for d in _lane_scan_steps():
                v = v + jnp.where(lane >= d, pltpu.roll(v, d, 1), 0.0)
            o_ref[:, sl] = v + carry
            carry = carry + jnp.sum(lf, axis=1, keepdims=True)

    return _call(kern, name="fox_cum", out_shape=jax.ShapeDtypeStruct((FOX_H, s), F32),
                          compiler_params=pltpu.CompilerParams(vmem_limit_bytes=VMEM_LIMIT))(frow, bfb)


def _fox_dlogf(dcq, dck, frow, bfb):
    s = frow.shape[1]

    def kern(q_ref, d_ref, f_ref, b_ref, df_ref, db_ref):
        lane = lax.broadcasted_iota(jnp.int32, (FOX_H, LANE), 1)
        carry = jnp.zeros((FOX_H, 1), F32)
        dbf = jnp.zeros((FOX_H, 1), F32)
        for c in reversed(range(s // LANE)):
            sl = slice(c * LANE, (c + 1) * LANE)
            dc = q_ref[:, sl] - d_ref[:, sl]
            v = dc
            for d in _lane_scan_steps():
                v = v + jnp.where(lane < LANE - d, pltpu.roll(v, LANE - d, 1), 0.0)
            v = v + carry
            carry = carry + jnp.sum(dc, axis=1, keepdims=True)
            df = v * jax.nn.sigmoid(-(f_ref[:, sl] + b_ref[...]))
            df_ref[:, sl] = df
            dbf = dbf + jnp.sum(df, axis=1, keepdims=True)
        db_ref[...] = jnp.broadcast_to(dbf, (FOX_H, LANE))

    return _call(kern, name="fox_dlogf",
                          out_shape=[jax.ShapeDtypeStruct((FOX_H, s), F32), jax.ShapeDtypeStruct((FOX_H, LANE), F32)],
                          compiler_params=pltpu.CompilerParams(vmem_limit_bytes=VMEM_LIMIT))(dcq, dck, frow, bfb)


FOX_TQ = 512
Q_COL, K_COL, V_COL = 0, FOX_W // LANE, 2 * FOX_W // LANE
B_COL, C_COL, U_COL = 12, 16, 20


def _bf16_terms(c):
    hi = c.astype(BF16).astype(F32)
    mid = (c - hi).astype(BF16).astype(F32)
    return hi, mid, (c - hi - mid).astype(BF16).astype(F32)


def _fox_operands(qv, kv, cq, ck, lane, hh, scale):
    sel = (lane < FOX_HD) if hh == 0 else (lane >= FOX_HD)
    b0 = FOX_HD if hh == 0 else 0
    qa = jnp.where(sel, qv * scale, 0.0)
    ka = jnp.where(sel, kv, 0.0)
    for n, (tq_, tk_) in enumerate(zip(_bf16_terms(cq), _bf16_terms(ck))):
        qa = jnp.where(lane == b0 + n, tq_, jnp.where(lane == b0 + 3 + n, 1.0, qa))
        ka = jnp.where(lane == b0 + n, 1.0, jnp.where(lane == b0 + 3 + n, -tk_, ka))
    return sel, qa.astype(BF16), ka.astype(BF16)


def _fox_logits(qa, ka, causal):
    sc = lax.dot_general(qa, ka, (NT, ((), ())), preferred_element_type=F32)
    return sc if causal is None else jnp.where(causal, sc, NEG)


def _fox_prep(proj, cumc):
    s = proj.shape[0]
    tp = _tile(s, 512)
    scale = FOX_HD ** -0.5

    def kern(q_ref, k_ref, c_ref, qa_ref, ka_ref):
        lane = lax.broadcasted_iota(jnp.int32, (tp, LANE), 1)
        for hh in range(2):
            _, qa_ref[hh], ka_ref[hh] = _fox_operands(q_ref[...], k_ref[...], c_ref[hh], c_ref[hh], lane, hh, scale)

    pair = BS((2, tp, LANE), lambda hp, i: (hp, i, 0))
    shp = jax.ShapeDtypeStruct((FOX_H, s, LANE), BF16)
    return _call(kern, name="fox_prep", grid=(4, s // tp),
                 in_specs=[BS((tp, LANE), lambda hp, i: (i, Q_COL + hp)), BS((tp, LANE), lambda hp, i: (i, K_COL + hp)), pair],
                 out_specs=[pair, pair], out_shape=[shp, shp], compiler_params=_params(2))(proj, proj, cumc)


def _fox_fwd_call(proj, qa, ka):
    s = proj.shape[0]
    tq = _tile(s, FOX_TQ)
    nq = s // tq

    def kern(qa_ref, ka_ref, v_ref, o_ref, lse_ref, m_s, l_s, acc_s):
        i = pl.program_id(1)
        j = pl.program_id(2)
        lane = lax.broadcasted_iota(jnp.int32, (tq, LANE), 1)

        @pl.when(j == 0)
        def _():
            m_s[...] = jnp.full(m_s.shape, NEG, F32)
            l_s[...] = jnp.zeros(l_s.shape, F32)
            acc_s[...] = jnp.zeros(acc_s.shape, F32)

        def step(diagonal):
            vb = v_ref[...].astype(BF16)
            causal = (lax.broadcasted_iota(jnp.int32, (tq, tq), 0) >= lax.broadcasted_iota(jnp.int32, (tq, tq), 1)
                      if diagonal else None)
            for hh in range(2):
                sc = _fox_logits(qa_ref[hh], ka_ref[hh], causal)
                m_prev = m_s[hh]
                m_new = jnp.maximum(m_prev, jnp.max(sc, axis=-1, keepdims=True))
                alpha = jnp.exp(m_prev - m_new)
                p = jnp.exp(sc - m_new)
                l_s[hh] = alpha * l_s[hh] + jnp.sum(p, axis=-1, keepdims=True)
                acc_s[hh] = alpha * acc_s[hh] + jnp.dot(p.astype(BF16), vb, preferred_element_type=F32)
                m_s[hh] = m_new

        @pl.when(j < i)
        def _():
            step(False)

        @pl.when(j == i)
        def _():
            step(True)
            o_ref[...] = jnp.where(lane < FOX_HD, acc_s[0] / l_s[0], acc_s[1] / l_s[1])
            for hh in range(2):
                lse_ref[hh] = jnp.broadcast_to(m_s[hh] + jnp.log(l_s[hh]), (tq, LANE))

    kvi = lambda hp, i, j: jnp.minimum(j, i)
    return _call(
        kern, name="fox_fwd", grid=(4, nq, nq),
        in_specs=[BS((2, tq, LANE), lambda hp, i, j: (hp, i, 0)),
                  BS((2, tq, LANE), lambda hp, i, j: (hp, kvi(hp, i, j), 0)),
                  BS((tq, LANE), lambda hp, i, j: (kvi(hp, i, j), V_COL + hp))],
        out_specs=[BS((tq, LANE), lambda hp, i, j: (i, hp)), BS((2, tq, LANE), lambda hp, i, j: (hp, i, 0))],
        out_shape=[jax.ShapeDtypeStruct((s, FOX_W), F32), jax.ShapeDtypeStruct((FOX_H, s, LANE), F32)],
        scratch_shapes=[pltpu.VMEM((2, tq, 1), F32), pltpu.VMEM((2, tq, 1), F32), pltpu.VMEM((2, tq, LANE), F32)],
        compiler_params=_params(3))(qa, ka, proj)


ROWSUM_M = 16


def _fox_bwd_call(proj, o, lse, dcat, qa, ka):
    s = proj.shape[0]
    tq = _tile(s, FOX_TQ)
    nq = s // tq
    reps = tq // LANE
    scale = FOX_HD ** -0.5

    def kern(qa_ref, ka_ref, v_ref, do_ref, o_ref, lse_ref, dq_ref, dk_ref, dv_ref, dck_ref, dcq_ref):
        j = pl.program_id(1)
        i = pl.program_id(2)
        lane = lax.broadcasted_iota(jnp.int32, (tq, LANE), 1)
        ones = jnp.ones((ROWSUM_M, tq), BF16)

        @pl.when((j == 0) & (i == 0))
        def _():
            dq_ref[...] = jnp.zeros(dq_ref.shape, F32)
            dcq_ref[...] = jnp.zeros(dcq_ref.shape, F32)

        @pl.when(i == j)
        def _():
            dk_ref[...] = jnp.zeros(dk_ref.shape, F32)
            dv_ref[...] = jnp.zeros(dv_ref.shape, F32)
            dck_ref[...] = jnp.zeros(dck_ref.shape, F32)

        def step(diagonal):
            dov = do_ref[...]
            ov = o_ref[...]
            vb = v_ref[...].astype(BF16)
            causal = (lax.broadcasted_iota(jnp.int32, (tq, tq), 0) >= lax.broadcasted_iota(jnp.int32, (tq, tq), 1)
                      if diagonal else None)
            dq_t = jnp.zeros((tq, LANE), F32)
            dk_t = jnp.zeros((tq, LANE), F32)
            dv_t = jnp.zeros((tq, LANE), F32)
            for hh in range(2):
                sel = (lane < FOX_HD) if hh == 0 else (lane >= FOX_HD)
                qa, ka = qa_ref[hh], ka_ref[hh]
                dom32 = jnp.where(sel, dov, 0.0)
                dom = dom32.astype(BF16)
                sc = _fox_logits(qa, ka, causal)
                p = jnp.exp(sc - jnp.tile(lse_ref[hh], (1, reps)))
                dp = lax.dot_general(dom, vb, (NT, ((), ())), preferred_element_type=F32)
                delta = jnp.sum(dom32 * ov, axis=-1, keepdims=True)
                ds = p * (dp - delta)
                dsb = ds.astype(BF16)
                dq_t = jnp.where(sel, jnp.dot(dsb, ka, preferred_element_type=F32) * scale, dq_t)
                dk_t = jnp.where(sel, lax.dot_general(dsb, qa, (TN, ((), ())), preferred_element_type=F32), dk_t)
                dv_t = dv_t + lax.dot_general(p.astype(BF16), dom, (TN, ((), ())), preferred_element_type=F32)
                dck_ref[hh] += jnp.sum(ds, axis=0, keepdims=True)
                ds_lo = (ds - dsb.astype(F32)).astype(BF16)
                dcq_ref[hh, i] += (lax.dot_general(ones, dsb, (NT, ((), ())), preferred_element_type=F32)
                                   + lax.dot_general(ones, ds_lo, (NT, ((), ())), preferred_element_type=F32))
            rows = pl.ds(pl.multiple_of(i * tq, tq), tq)
            dq_ref[rows, :] += dq_t
            dk_ref[...] += dk_t
            dv_ref[...] += dv_t

        @pl.when(i > j)
        def _():
            step(False)

        @pl.when(i == j)
        def _():
            step(True)

    qi = lambda hp, j, i: jnp.maximum(i, j)
    return _call(
        kern, name="fox_bwd", grid=(4, nq, nq),
        in_specs=[BS((2, tq, LANE), lambda hp, j, i: (hp, qi(hp, j, i), 0)),
                  BS((2, tq, LANE), lambda hp, j, i: (hp, j, 0)),
                  BS((tq, LANE), lambda hp, j, i: (j, V_COL + hp)),
                  BS((tq, LANE), lambda hp, j, i: (qi(hp, j, i), hp)),
                  BS((tq, LANE), lambda hp, j, i: (qi(hp, j, i), hp)),
                  BS((2, tq, LANE), lambda hp, j, i: (hp, qi(hp, j, i), 0))],
        out_specs=[BS((s, LANE), lambda hp, j, i: (0, hp)), BS((tq, LANE), lambda hp, j, i: (j, hp)),
                   BS((tq, LANE), lambda hp, j, i: (j, hp)), BS((2, 1, tq), lambda hp, j, i: (hp, 0, j)),
                   BS((2, nq, ROWSUM_M, tq), lambda hp, j, i: (hp, 0, 0, 0))],
        out_shape=[jax.ShapeDtypeStruct((s, FOX_W), F32), jax.ShapeDtypeStruct((s, FOX_W), F32),
                   jax.ShapeDtypeStruct((s, FOX_W), F32), jax.ShapeDtypeStruct((FOX_H, 1, s), F32),
                   jax.ShapeDtypeStruct((FOX_H, nq, ROWSUM_M, tq), F32)],
        compiler_params=_params(3))(qa, ka, proj, dcat, o, lse)


def _shift_down(v, d, row):
    return jnp.where(row >= d, pltpu.roll(v, d, 0), 0.0)


def _shift_up(v, d, row, n):
    return jnp.where(row < n - d, pltpu.roll(v, n - d, 0), 0.0)


def _sconv_fwd(proj, convw):
    s = proj.shape[0]

    def kern(b_ref, c_ref, u_ref, w_ref, y_ref):
        row = lax.broadcasted_iota(jnp.int32, (s, LANE), 0)
        z = c_ref[...] * u_ref[...]
        conv = w_ref[2:3, :] * z + w_ref[1:2, :] * _shift_down(z, 1, row) + w_ref[0:1, :] * _shift_down(z, 2, row)
        y_ref[...] = (b_ref[...] * conv).astype(BF16)

    col = lambda base: BS((s, LANE), lambda cb: (0, base + cb))
    return _call(kern, name="sconv_fwd", grid=(SC_W // LANE,),
                          in_specs=[col(B_COL), col(C_COL), col(U_COL), BS((SC_K, LANE), lambda cb: (0, cb))],
                          out_specs=BS((s, LANE), lambda cb: (0, cb)),
                          out_shape=jax.ShapeDtypeStruct((s, SC_W), BF16), compiler_params=_params(1))(proj, proj, proj, convw)


def _sconv_bwd(proj, convw, dcat):
    s = proj.shape[0]

    def kern(b_ref, c_ref, u_ref, w_ref, dy_ref, db_ref, dc_ref, du_ref, dw_ref):
        row = lax.broadcasted_iota(jnp.int32, (s, LANE), 0)
        cv, uv, dyv = c_ref[...], u_ref[...], dy_ref[...]
        z = cv * uv
        z1 = _shift_down(z, 1, row)
        z2 = _shift_down(z, 2, row)
        conv = w_ref[2:3, :] * z + w_ref[1:2, :] * z1 + w_ref[0:1, :] * z2
        db_ref[...] = dyv * conv
        dcv = dyv * b_ref[...]
        dz = w_ref[2:3, :] * dcv + w_ref[1:2, :] * _shift_up(dcv, 1, row, s) + w_ref[0:1, :] * _shift_up(dcv, 2, row, s)
        dc_ref[...] = dz * uv
        du_ref[...] = dz * cv
        dw_ref[0:1, :] = jnp.sum(dcv * z2, axis=0, keepdims=True)
        dw_ref[1:2, :] = jnp.sum(dcv * z1, axis=0, keepdims=True)
        dw_ref[2:3, :] = jnp.sum(dcv * z, axis=0, keepdims=True)

    col = lambda base: BS((s, LANE), lambda cb: (0, base + cb))
    out = BS((s, LANE), lambda cb: (0, cb))
    wspec = BS((SC_K, LANE), lambda cb: (0, cb))
    act = jax.ShapeDtypeStruct((s, SC_W), F32)
    return _call(kern, name="sconv_bwd", grid=(SC_W // LANE,),
                          in_specs=[col(B_COL), col(C_COL), col(U_COL), wspec, col(FOX_W // LANE)],
                          out_specs=[out, out, out, wspec],
                          out_shape=[act, act, act, jax.ShapeDtypeStruct((SC_K, SC_W), F32)],
                          compiler_params=_params(1))(proj, proj, proj, convw, dcat)


def _fox_layer_fwd(x, gpre, gpost, wall, bfb, convw, wout, after=None):
    s = x.shape[0]
    h = _norm_fwd(x, gpre, after)
    proj = _mm_nt_cols("fox_proj", h, wall, AB_PAD // 5)
    frow = proj[:, 3 * FOX_W + 3 * SC_W:3 * FOX_W + 3 * SC_W + FOX_H].T
    cumr = _fox_cum(frow, bfb)
    qa, ka = _fox_prep(proj, jnp.broadcast_to(cumr[:, :, None], (FOX_H, s, LANE)))
    o, lse = _fox_fwd_call(proj, qa, ka)
    yb = _sconv_fwd(proj, convw)
    cat = jnp.concatenate([o.astype(BF16), yb], axis=1)
    y = _mm_nn("fox_out", cat, wout)
    return _norm_res(x, y, gpost), (x, h, proj, frow, qa, ka, o, lse, cat, y)


def _fox_layer_bwd(dxo, saved, gpre, gpost, wall, bfb, convw, wout, after=None):
    x, h, proj, frow, qa, ka, o, lse, cat, y = saved
    s = x.shape[0]
    dy, dgpost = _norm_bwd(y, dxo, gpost, None, BF16, after)
    dcat = _mm_nt("fox_dcat", dy, wout)
    dwout = _mm_tn("fox_dwout", cat, dy)
    db, dc, du, dconvw = _sconv_bwd(proj, convw, dcat)
    dq, dk, dv, dck, dcq = _fox_bwd_call(proj, o, lse, dcat, qa, ka)
    dfrow, dbf = _fox_dlogf(dcq[:, :, 0, :].reshape(FOX_H, s), dck.reshape(FOX_H, s), frow, bfb)
    dfcol = jnp.pad(dfrow.T, ((0, 0), (0, LANE - FOX_H)))
    dproj = jnp.concatenate([dq, dk, dv, db, dc, du, dfcol], axis=1).astype(BF16)
    dwall = _mm_tn_rows("fox_dwall", dproj, h, AB_PAD // 5)
    dh = _mm_nn("fox_dh", dproj, wall, vmem=VMEM_BIG)
    dx, dgpre = _norm_bwd(x, dh, gpre, dxo, F32)
    return dx, dgpre, dgpost, dwall, dbf[:, 0], dconvw, dwout


def _ab_pack(wt):
    nf = 3 * FOX_W
    return jnp.concatenate([wt[:nf], wt[nf + FOX_H:], wt[nf:nf + FOX_H],
                            jnp.zeros((AB_PAD - AB_IN, wt.shape[1]), wt.dtype)], axis=0)


def _ab_unpack(wt):
    nf = 3 * FOX_W
    nbcu = 3 * SC_W
    return jnp.concatenate([wt[:nf], wt[nf + nbcu:nf + nbcu + FOX_H], wt[nf:nf + nbcu]], axis=0)


NCH = DM // LANE
CH_PER_BLK = LRU_BW // LANE


def _chunk_spec(s, lead=0):
    return BS((None, s, LANE), lambda ch: (lead + ch // CH_PER_BLK, 0, ch % CH_PER_BLK))


def _vec_chunk(rows):
    return BS((rows, LANE), lambda ch: (0, ch))


def _neg_expm1(x):
    series = -x * (1.0 + x * (1 / 2) * (1.0 + x * (1 / 3) * (1.0 + x * (1 / 4) * (1.0 + x * (1 / 5) * (
        1.0 + x * (1 / 6) * (1.0 + x * (1 / 7)))))))
    return jnp.where(x > -0.25, series, 1.0 - jnp.exp(x))


def _softplus(z):
    return jnp.maximum(z, 0.0) + jnp.log1p(jnp.exp(-jnp.abs(z)))


GELU_C = math.sqrt(2.0 / math.pi)
GELU_A = 0.044715


def _gelu(x):
    return 0.5 * x * (1.0 + jnp.tanh(GELU_C * (x + GELU_A * x * x * x)))


def _gelu_grad(x):
    t = jnp.tanh(GELU_C * (x + GELU_A * x * x * x))
    return 0.5 * (1.0 + t) + 0.5 * x * (1.0 - t * t) * GELU_C * (1.0 + 3.0 * GELU_A * x * x)


def _lru_conv_fwd(gu, convw, convb):
    s = gu.shape[1]

    def kern(x_ref, w_ref, b_ref, u_ref):
        row = lax.broadcasted_iota(jnp.int32, (s, LANE), 0)
        xv = x_ref[...]
        u_ref[...] = (b_ref[...] + w_ref[3:4, :] * xv + w_ref[2:3, :] * _shift_down(xv, 1, row)
                      + w_ref[1:2, :] * _shift_down(xv, 2, row) + w_ref[0:1, :] * _shift_down(xv, 3, row))

    return _call(kern, name="lru_conv_fwd", grid=(NCH,),
                          in_specs=[_chunk_spec(s, LRU_NB), _vec_chunk(RG_K), _vec_chunk(1)], out_specs=_chunk_spec(s),
                          out_shape=jax.ShapeDtypeStruct((LRU_NB, s, LRU_BW), F32), compiler_params=_params(1))(gu, convw, convb)


def _lru_conv_bwd(dud, dug, gu, convw):
    s = gu.shape[1]

    def kern(d1_ref, d2_ref, x_ref, w_ref, dx_ref, dw_ref, db_ref):
        row = lax.broadcasted_iota(jnp.int32, (s, LANE), 0)
        du = d1_ref[...] + d2_ref[...]
        xv = x_ref[...]
        dx_ref[...] = (w_ref[3:4, :] * du + w_ref[2:3, :] * _shift_up(du, 1, row, s) + w_ref[1:2, :] * _shift_up(du, 2, row, s)
                       + w_ref[0:1, :] * _shift_up(du, 3, row, s)).astype(BF16)
        dw_ref[3:4, :] = jnp.sum(du * xv, axis=0, keepdims=True)
        for k in range(1, RG_K):
            dw_ref[3 - k:4 - k, :] = jnp.sum(du * _shift_down(xv, k, row), axis=0, keepdims=True)
        db_ref[...] = jnp.sum(du, axis=0, keepdims=True)

    return _call(kern, name="lru_conv_bwd", grid=(NCH,),
                          in_specs=[_chunk_spec(s), _chunk_spec(s), _chunk_spec(s, LRU_NB), _vec_chunk(RG_K)],
                          out_specs=[_chunk_spec(s), _vec_chunk(RG_K), _vec_chunk(1)],
                          out_shape=[jax.ShapeDtypeStruct((LRU_NB, s, LRU_BW), BF16),
                                     jax.ShapeDtypeStruct((RG_K, DM), F32), jax.ShapeDtypeStruct((1, DM), F32)],
                          compiler_params=_params(1))(dud, dug, gu, convw)


def _lru_gates(z_ref, bai_ref, lam_ref, uv):
    r = jax.nn.sigmoid(z_ref[0] + bai_ref[0:1, :])
    ig = jax.nn.sigmoid(z_ref[1] + bai_ref[1:2, :])
    sp = _softplus(-lam_ref[...])
    la = -RG_C * r * sp
    a = jnp.exp(la)
    sq = jnp.sqrt(_neg_expm1(2.0 * la))
    return r, ig, sp, a, sq


def _scan_steps(n):
    d, out = 1, []
    while d < n:
        out.append(d)
        d *= 2
    return out


def _lru_scan_fwd(z, bai, lam, u, gu):
    s = u.shape[1]
    zspec = BS((2, None, s, LANE), lambda ch: (0, ch // CH_PER_BLK, 0, ch % CH_PER_BLK))

    def kern(z_ref, bai_ref, lam_ref, u_ref, g_ref, hs_ref, y_ref):
        row = lax.broadcasted_iota(jnp.int32, (s, LANE), 0)
        uv = u_ref[...]
        _, ig, _, a, sq = _lru_gates(z_ref, bai_ref, lam_ref, uv)
        b = sq * (ig * uv)
        for d in _scan_steps(s):
            a_sh = jnp.where(row >= d, pltpu.roll(a, d, 0), 1.0)
            b = a * _shift_down(b, d, row) + b
            a = a * a_sh
        hs_ref[...] = b
        y_ref[...] = (_gelu(g_ref[...]) * b).astype(BF16)

    return _call(kern, name="lru_scan_fwd", grid=(NCH,),
                          in_specs=[zspec, _vec_chunk(2), _vec_chunk(1), _chunk_spec(s), _chunk_spec(s)],
                          out_specs=[_chunk_spec(s), BS((s, LANE), lambda ch: (0, ch))],
                          out_shape=[jax.ShapeDtypeStruct((LRU_NB, s, LRU_BW), F32), jax.ShapeDtypeStruct((s, DM), BF16)],
                          compiler_params=_params(1, VMEM_BIG))(z, bai, lam, u, gu)


def _lru_scan_bwd(dyp, z, bai, lam, u, gu, hs):
    s = u.shape[1]
    zspec = BS((2, None, s, LANE), lambda ch: (0, ch // CH_PER_BLK, 0, ch % CH_PER_BLK))

    def kern(dy_ref, z_ref, bai_ref, lam_ref, u_ref, g_ref, hs_ref, dg_ref, dz_ref, du_ref, dbai_ref, dlam_ref):
        row = lax.broadcasted_iota(jnp.int32, (s, LANE), 0)
        uv, gv, hv, dyv = u_ref[...], g_ref[...], hs_ref[...], dy_ref[...]
        r, ig, sp, a, sq = _lru_gates(z_ref, bai_ref, lam_ref, uv)
        dg_ref[...] = (dyv * hv * _gelu_grad(gv)).astype(BF16)
        g = dyv * _gelu(gv)
        an = _shift_up(a, 1, row, s)
        for d in _scan_steps(s):
            an_sh = jnp.where(row < s - d, pltpu.roll(an, s - d, 0), 1.0)
            g = an * _shift_up(g, d, row, s) + g
            an = an * an_sh
        da = g * _shift_down(hv, 1, row)
        dsq = g * (ig * uv)
        di = g * sq * uv
        du_ref[...] = g * sq * ig
        dla = da * a - dsq * (a * a / sq)
        dzr = dla * (-RG_C * sp) * r * (1.0 - r)
        dzi = di * ig * (1.0 - ig)
        dz_ref[0] = dzr.astype(BF16)
        dz_ref[1] = dzi.astype(BF16)
        dbai_ref[0:1, :] = jnp.sum(dzr, axis=0, keepdims=True)
        dbai_ref[1:2, :] = jnp.sum(dzi, axis=0, keepdims=True)
        dlam_ref[...] = jnp.sum(dla * r, axis=0, keepdims=True) * (RG_C * jax.nn.sigmoid(-lam_ref[...]))

    return _call(
        kern, name="lru_scan_bwd", grid=(NCH,),
        in_specs=[BS((s, LANE), lambda ch: (0, ch)), zspec, _vec_chunk(2), _vec_chunk(1), _chunk_spec(s), _chunk_spec(s),
                  _chunk_spec(s)],
        out_specs=[_chunk_spec(s), zspec, _chunk_spec(s), _vec_chunk(2), _vec_chunk(1)],
        out_shape=[jax.ShapeDtypeStruct((LRU_NB, s, LRU_BW), BF16), jax.ShapeDtypeStruct((2, LRU_NB, s, LRU_BW), BF16),
                   jax.ShapeDtypeStruct((LRU_NB, s, LRU_BW), F32), jax.ShapeDtypeStruct((2, DM), F32),
                   jax.ShapeDtypeStruct((1, DM), F32)],
        compiler_params=_params(1, VMEM_BIG))(dyp, z, bai, lam, u, gu, hs)


def _lru_layer_fwd(x, gpre, gpost, win, convw, convb, wai, bai, lam, wout, after=None):
    s = x.shape[0]
    tm = _tile(s, 512)
    h = _norm_fwd(x, gpre, after)
    gu = _bmm_nn("lru_in", h, win)
    u = _lru_conv_fwd(gu, convw, convb)
    z = _mm("lru_gate", u, wai, grid=(2, LRU_NB, s // tm, 1),
            a_spec=BS((None, tm, LRU_BW), lambda k, n, i, r: (n, i, 0)),
            b_spec=BS((None, None, LRU_BW, LRU_BW), lambda k, n, i, r: (k, n, 0, 0)),
            o_spec=BS((None, None, tm, LRU_BW), lambda k, n, i, r: (k, n, i, 0)),
            out_shape=(2, LRU_NB, s, LRU_BW), dn=NN)
    hs, yp = _lru_scan_fwd(z, bai, lam, u, gu)
    y = _mm_nn("lru_out", yp, wout)
    return _norm_res(x, y, gpost), (x, h, gu, u, z, hs, yp, y)


def _lru_layer_bwd(dxo, saved, gpre, gpost, win, convw, convb, wai, bai, lam, wout, after=None):
    x, h, gu, u, z, hs, yp, y = saved
    s = x.shape[0]
    tm = _tile(s, 512)
    dy, dgpost = _norm_bwd(y, dxo, gpost, None, BF16, after)
    dyp = _mm_nt("lru_dyp", dy, wout)
    dwout = _mm_tn("lru_dwout", yp, dy)
    dgate, dz, dud, dbai, dlam = _lru_scan_bwd(dyp, z, bai, lam, u, gu, hs)
    dwai = _mm("lru_dwai", u, dz, grid=(2, LRU_NB, s // tm),
               a_spec=BS((None, tm, LRU_BW), lambda k, n, r: (n, r, 0)),
               b_spec=BS((None, None, tm, LRU_BW), lambda k, n, r: (k, n, r, 0)),
               o_spec=BS((None, None, LRU_BW, LRU_BW), lambda k, n, r: (k, n, 0, 0)),
               out_shape=(2, LRU_NB, LRU_BW, LRU_BW), dn=TN)
    dug = _mm("lru_dug", dz, wai, grid=(LRU_NB, s // tm, 2),
              a_spec=BS((None, None, tm, LRU_BW), lambda n, i, k: (k, n, i, 0)),
              b_spec=BS((None, None, LRU_BW, LRU_BW), lambda n, i, k: (k, n, 0, 0)),
              o_spec=BS((None, tm, LRU_BW), lambda n, i, k: (n, i, 0)),
              out_shape=(LRU_NB, s, LRU_BW), dn=NT)
    duraw, dconvw, dconvb = _lru_conv_bwd(dud, dug, gu, convw)
    dgu = jnp.concatenate([dgate, duraw], axis=0)
    dwin = _bmm_tn("lru_dwin", h, dgu)
    dh = _bmm_nt_sum("lru_dh", dgu, win)
    dx, dgpre = _norm_bwd(x, dh, gpre, dxo, F32)
    return dx, dgpre, dgpost, dwin, dconvw, dconvb, dwai, dbai, dlam, dwout


CHIP_FLIPS = ((1, 0), (0, 1), (1, 1))


def _place():
    return lax.axis_index("x"), lax.axis_index("y"), lax.axis_index("c")


def _flip(v, f):
    return 1 - v if f else v


def _comm_params():
    return pltpu.CompilerParams(vmem_limit_bytes=VMEM_LIMIT)


def _small_gather(v):
    def body(v_ref, o_ref, send_sems, recv_sems, local_sem):
        x, y, c = _place()
        mine = 4 * x + 2 * y + c
        local = pltpu.make_async_copy(v_ref, o_ref.at[mine], local_sem)
        local.start()
        sends = []
        for k in range(1, NDEV):
            fx, fy, fc = (k >> 2) & 1, (k >> 1) & 1, k & 1
            sends.append(pltpu.make_async_remote_copy(
                src_ref=v_ref, dst_ref=o_ref.at[mine], send_sem=send_sems.at[k - 1], recv_sem=recv_sems.at[k - 1],
                device_id=(_flip(x, fx), _flip(y, fy), _flip(c, fc)), device_id_type=MESH))
        for cp in sends:
            cp.start()
        for k in range(1, NDEV):
            fx, fy, fc = (k >> 2) & 1, (k >> 1) & 1, k & 1
            src = 4 * _flip(x, fx) + 2 * _flip(y, fy) + _flip(c, fc)
            pltpu.make_async_remote_copy(src_ref=v_ref, dst_ref=o_ref.at[src], send_sem=send_sems.at[k - 1],
                                         recv_sem=recv_sems.at[k - 1], device_id=(x, y, c), device_id_type=MESH).wait_recv()
        for cp in sends:
            cp.wait_send()
        local.wait()

    return pl.pallas_call(body, name="small_gather", in_specs=[ANY], out_specs=ANY,
                          out_shape=jax.ShapeDtypeStruct((NDEV,) + v.shape, v.dtype),
                          scratch_shapes=[pltpu.SemaphoreType.DMA((NDEV - 1,)), pltpu.SemaphoreType.DMA((NDEV - 1,)),
                                          pltpu.SemaphoreType.DMA],
                          compiler_params=_comm_params())(v)


REL_CHIPS = ((0, 0),) + CHIP_FLIPS


def _rs_d2d(g5s, after=None):
    n = len(g5s)
    extra = () if after is None else (after,)

    def body(*refs):
        ins, gots = refs[:n], refs[n + len(extra):2 * n + len(extra)]
        send_sems, recv_sems = refs[2 * n + len(extra):]
        x, y, c = _place()
        copies = []
        for t in range(n):
            for f, (fx, fy) in enumerate(REL_CHIPS):
                copies.append(pltpu.make_async_remote_copy(
                    src_ref=ins[t].at[_flip(x, fx), _flip(y, fy), 1 - c], dst_ref=gots[t].at[f],
                    send_sem=send_sems.at[4 * t + f], recv_sem=recv_sems.at[4 * t + f], device_id=(x, y, 1 - c),
                    device_id_type=MESH))
        for cp in copies:
            cp.start()
        for cp in copies:
            cp.wait()

    out = [jax.ShapeDtypeStruct((4,) + g.shape[3:], F32) for g in g5s]
    return pl.pallas_call(body, name="rs_d2d", in_specs=[ANY] * (n + len(extra)), out_specs=[ANY] * n, out_shape=out,
                          scratch_shapes=[pltpu.SemaphoreType.DMA((4 * n,)), pltpu.SemaphoreType.DMA((4 * n,))],
                          compiler_params=_comm_params())(*g5s, *extra)


HBM = pl.BlockSpec(memory_space=pltpu.HBM)
SEM = pl.BlockSpec(memory_space=pltpu.SEMAPHORE)
EFFECT = pltpu.SideEffectType.DATAFLOW_SIDE_EFFECTING


def _in_hbm(a):
    return pltpu.with_memory_space_constraint(a, pltpu.HBM)


def _rs_ici_copies(ins, lands, send_sems, recv_sems):
    x, y, c = _place()
    return [pltpu.make_async_remote_copy(
        src_ref=ins[t].at[f], dst_ref=lands[t].at[f], send_sem=send_sems.at[3 * t + f], recv_sem=recv_sems.at[3 * t + f],
        device_id=(_flip(x, fx), _flip(y, fy), c), device_id_type=MESH)
        for t in range(len(ins)) for f, (fx, fy) in enumerate(CHIP_FLIPS)]


def _rs_ici_start(parts, name):
    n = len(parts)

    def body(*refs):
        ins, lands = refs[:n], refs[n:2 * n]
        send_sems, recv_sems = refs[2 * n], refs[2 * n + 1]
        token = refs[-1]
        for cp in _rs_ici_copies(ins, lands, send_sems, recv_sems):
            cp.start()
        token[...] = jnp.zeros(token.shape, token.dtype)

    thru = [pltpu.HBM(p.shape, p.dtype) for p in parts]
    res = pl.pallas_call(
        body, name=name, in_specs=[HBM] * (2 * n),
        out_shape=(pltpu.SemaphoreType.DMA((3 * n,)), pltpu.SemaphoreType.DMA((3 * n,)), *thru, *thru,
                   jax.ShapeDtypeStruct((8, LANE), F32)),
        out_specs=(SEM, SEM, *([HBM] * (2 * n)), pl.BlockSpec(memory_space=pltpu.VMEM)),
        input_output_aliases={i: 2 + i for i in range(2 * n)},
        compiler_params=pltpu.CompilerParams(has_side_effects=EFFECT),
    )(*[_in_hbm(p) for p in parts], *[_in_hbm(lax.empty(p.shape, p.dtype)) for p in parts])
    return res[:-1], res[-1]


def _rs_ici_wait(state, after, name):
    n = (len(state) - 2) // 2

    def body(*refs):
        send_sems, recv_sems = refs[0], refs[1]
        ins, lands = refs[2:2 + n], refs[2 + n:2 + 2 * n]
        for cp in _rs_ici_copies(ins, lands, send_sems, recv_sems):
            cp.wait_send()
            cp.wait_recv()

    thru = [pltpu.HBM(s.shape, s.dtype) for s in state[2:]]
    res = pl.pallas_call(
        body, name=name, in_specs=[SEM, SEM] + [HBM] * (2 * n) + [ANY], out_shape=tuple(thru),
        out_specs=tuple([HBM] * (2 * n)), input_output_aliases={2 + i: i for i in range(2 * n)},
        compiler_params=pltpu.CompilerParams(has_side_effects=EFFECT),
    )(*state, after)
    return list(res[n:])


def _ag_copies(shards, lands, send_sems, recv_sems):
    x, y, c = _place()
    mine = 4 * x + 2 * y + c
    peers = [(x, y, 1 - c)] + [(_flip(x, fx), _flip(y, fy), c) for fx, fy in CHIP_FLIPS]
    return [pltpu.make_async_remote_copy(
        src_ref=shards[t], dst_ref=lands[t].at[mine], send_sem=send_sems.at[4 * t + k], recv_sem=recv_sems.at[4 * t + k],
        device_id=peer, device_id_type=MESH) for t in range(len(shards)) for k, peer in enumerate(peers)]


def _ag_start(shards, after, name):
    n = len(shards)

    def body(*refs):
        ins, lands = refs[:n], refs[n:2 * n]
        send_sems, recv_sems = refs[2 * n + 1], refs[2 * n + 2]
        token = refs[-1]
        for cp in _ag_copies(ins, lands, send_sems, recv_sems):
            cp.start()
        token[...] = jnp.zeros(token.shape, token.dtype)

    thru = [pltpu.HBM(s.shape, s.dtype) for s in shards]
    land = [pltpu.HBM((NDEV,) + s.shape, s.dtype) for s in shards]
    res = pl.pallas_call(
        body, name=name, in_specs=[HBM] * (2 * n) + [ANY],
        out_shape=(pltpu.SemaphoreType.DMA((4 * n,)), pltpu.SemaphoreType.DMA((4 * n,)), *thru, *land,
                   jax.ShapeDtypeStruct((8, LANE), F32)),
        out_specs=(SEM, SEM, *([HBM] * (2 * n)), pl.BlockSpec(memory_space=pltpu.VMEM)),
        input_output_aliases={i: 2 + i for i in range(2 * n)},
        compiler_params=pltpu.CompilerParams(has_side_effects=EFFECT),
    )(*[_in_hbm(s) for s in shards], *[_in_hbm(lax.empty((NDEV,) + s.shape, s.dtype)) for s in shards], after)
    return res[:-1], res[-1]


def _ag_wait(state, after, name):
    n = (len(state) - 2) // 2

    def body(*refs):
        send_sems, recv_sems = refs[0], refs[1]
        ins, lands = refs[2:2 + n], refs[2 + n:2 + 2 * n]
        for cp in _ag_copies(ins, lands, send_sems, recv_sems):
            cp.wait_send()
            cp.wait_recv()

    thru = [pltpu.HBM(s.shape, s.dtype) for s in state[2:]]
    res = pl.pallas_call(
        body, name=name, in_specs=[SEM, SEM] + [HBM] * (2 * n) + [ANY], out_shape=tuple(thru),
        out_specs=tuple([HBM] * (2 * n)), input_output_aliases={2 + i: i for i in range(2 * n)},
        compiler_params=pltpu.CompilerParams(has_side_effects=EFFECT),
    )(*state, after)
    return list(res[:n]), list(res[n:])


def _ag_finish(shards, lands):
    n = len(shards)

    def body(*refs):
        ins, outs, stage = refs[:n], refs[2 * n:3 * n], refs[3 * n:4 * n]
        send_sems, recv_sems, local_sems = refs[4 * n:]
        x, y, c = _place()
        chips = [(_flip(x, fx), _flip(y, fy)) for fx, fy in CHIP_FLIPS]

        def passing(t, j, core, to):
            blk = outs[t].at[4 * chips[j][0] + 2 * chips[j][1] + core]
            return pltpu.make_async_remote_copy(src_ref=blk, dst_ref=blk, send_sem=send_sems.at[3 * t + j],
                                                recv_sem=recv_sems.at[3 * t + j], device_id=to, device_id_type=MESH)

        sends = [passing(t, j, c, (x, y, 1 - c)) for t in range(n) for j in range(3)]
        for cp in sends:
            cp.start()
        load = [pltpu.make_async_copy(ins[t], stage[t], local_sems.at[t]) for t in range(n)]
        mine = [pltpu.make_async_copy(stage[t], outs[t].at[4 * x + 2 * y + c], local_sems.at[t]) for t in range(n)]
        for cp in load:
            cp.start()
        for t in range(n):
            load[t].wait()
            mine[t].start()
        for t in range(n):
            for j in range(3):
                passing(t, j, 1 - c, (x, y, c)).wait_recv()
        for cp in sends:
            cp.wait_send()
        for cp in mine:
            cp.wait()

    return pl.pallas_call(
        body, name="ag_finish", in_specs=[ANY] * (2 * n), out_specs=[ANY] * n,
        out_shape=[jax.ShapeDtypeStruct(l.shape, l.dtype) for l in lands],
        input_output_aliases={n + i: i for i in range(n)},
        scratch_shapes=[pltpu.VMEM(s.shape, s.dtype) for s in shards]
        + [pltpu.SemaphoreType.DMA((3 * n,)), pltpu.SemaphoreType.DMA((3 * n,)), pltpu.SemaphoreType.DMA((n,))],
        compiler_params=_comm_params())(*shards, *lands)


def _row_tile(rows, largest=256):
    for t in (1024, 512, 256, 128, 64, 32, 16, 8):
        if t > largest:
            continue
        if rows % t == 0:
            return t
    return rows


def _rs_chip_sum(pos, g5, got):
    a, b = g5.shape[3:]
    ta = _row_tile(a, 1024)

    def kern(pos_ref, o_ref, g_ref, p_ref):
        p_ref[...] = (o_ref[...] + g_ref[...]).astype(BF16)

    def mine(f, i, pos_ref):
        return (pos_ref[0] ^ ((f + 1) & 1), pos_ref[1] ^ ((f + 1) >> 1), pos_ref[2], i, 0)

    spec = pltpu.PrefetchScalarGridSpec(
        num_scalar_prefetch=1, grid=(3, a // ta),
        in_specs=[BS((None, None, None, ta, b), mine), BS((None, ta, b), lambda f, i, pos_ref: (f + 1, i, 0))],
        out_specs=BS((None, ta, b), lambda f, i, pos_ref: (f, i, 0)))
    return _call(kern, name="rs_chip_sum", grid_spec=spec, out_shape=jax.ShapeDtypeStruct((3, a, b), BF16),
                          compiler_params=_params(2))(pos, g5, got)


def _rs_final_sum(pos, g5, got, recv):
    a, b = g5.shape[3:]
    ta = _row_tile(a, 1024)

    def kern(pos_ref, o_ref, g_ref, r_ref, s_ref):
        acc = o_ref[...] + g_ref[...]
        for f in range(3):
            acc = acc + r_ref[f].astype(F32)
        s_ref[...] = acc

    spec = pltpu.PrefetchScalarGridSpec(
        num_scalar_prefetch=1, grid=(a // ta,),
        in_specs=[BS((None, None, None, ta, b), lambda i, pos_ref: (pos_ref[0], pos_ref[1], pos_ref[2], i, 0)),
                  BS((None, ta, b), lambda i, pos_ref: (0, i, 0)), BS((3, ta, b), lambda i, pos_ref: (0, i, 0))],
        out_specs=BS((ta, b), lambda i, pos_ref: (i, 0)))
    return _call(kern, name="rs_final_sum", grid_spec=spec, out_shape=jax.ShapeDtypeStruct((a, b), F32),
                          compiler_params=_params(1))(pos, g5, got, recv)


def _rs_d2d_copies(ins, lands, send_sems, recv_sems):
    x, y, c = _place()
    return [pltpu.make_async_remote_copy(
        src_ref=ins[t].at[_flip(x, fx), _flip(y, fy), 1 - c], dst_ref=lands[t].at[f], send_sem=send_sems.at[4 * t + f],
        recv_sem=recv_sems.at[4 * t + f], device_id=(x, y, 1 - c), device_id_type=MESH)
        for t in range(len(ins)) for f, (fx, fy) in enumerate(REL_CHIPS)]


def _rs_d2d_start(g5s, name):
    n = len(g5s)

    def body(*refs):
        ins, lands = refs[:n], refs[n:2 * n]
        for cp in _rs_d2d_copies(ins, lands, refs[2 * n], refs[2 * n + 1]):
            cp.start()
        refs[-1][...] = jnp.zeros(refs[-1].shape, F32)

    thru = [pltpu.HBM(g.shape, g.dtype) for g in g5s]
    land = [pltpu.HBM((4,) + g.shape[3:], F32) for g in g5s]
    res = pl.pallas_call(
        body, name=name, in_specs=[HBM] * (2 * n),
        out_shape=(pltpu.SemaphoreType.DMA((4 * n,)), pltpu.SemaphoreType.DMA((4 * n,)), *thru, *land,
                   jax.ShapeDtypeStruct((8, LANE), F32)),
        out_specs=(SEM, SEM, *([HBM] * (2 * n)), pl.BlockSpec(memory_space=pltpu.VMEM)),
        input_output_aliases={i: 2 + i for i in range(2 * n)},
        compiler_params=pltpu.CompilerParams(has_side_effects=EFFECT),
    )(*[_in_hbm(g) for g in g5s], *[_in_hbm(lax.empty((4,) + g.shape[3:], F32)) for g in g5s])
    return res[:-1], res[-1]


def _rs_d2d_wait(state, after, name):
    n = (len(state) - 2) // 2

    def body(*refs):
        ins, lands = refs[2:2 + n], refs[2 + n:2 + 2 * n]
        for cp in _rs_d2d_copies(ins, lands, refs[0], refs[1]):
            cp.wait_send()
            cp.wait_recv()

    thru = [pltpu.HBM(s.shape, s.dtype) for s in state[2:]]
    res = pl.pallas_call(
        body, name=name, in_specs=[SEM, SEM] + [HBM] * (2 * n) + [ANY], out_shape=tuple(thru),
        out_specs=tuple([HBM] * (2 * n)), input_output_aliases={2 + i: i for i in range(2 * n)},
        compiler_params=pltpu.CompilerParams(has_side_effects=EFFECT),
    )(*state, after)
    return list(res[:n]), list(res[n:])


def _as_g5(grads):
    return [g.reshape((2, 2, 2) + g.shape[1:]) for g in grads]


def _rs_mid(g5s, gots, pos, tag):
    parts = [_rs_chip_sum(pos, g, got) for g, got in zip(g5s, gots)]
    state, token = _rs_ici_start(parts, "rs_ici_start_" + tag)
    return (g5s, gots, state, tag), token


def _rs_begin(grads, pos, tag, after=None):
    g5s = _as_g5(grads)
    return _rs_mid(g5s, _rs_d2d(g5s, after), pos, tag)


def _rs_end(pending, after, pos):
    g5s, gots, state, tag = pending
    recvs = _rs_ici_wait(state, after, "rs_ici_wait_" + tag)
    return [_rs_final_sum(pos, g, got, r) for g, got, r in zip(g5s, gots, recvs)]


def _sum_devices(v):
    _, r, _ = v.shape

    def kern(v_ref, o_ref):
        acc = v_ref[0]
        for d in range(1, NDEV):
            acc = acc + v_ref[d]
        o_ref[...] = acc

    return _call(kern, name="sum_devices", out_shape=jax.ShapeDtypeStruct((r, LANE), F32),
                          compiler_params=_comm_params())(v)


def _loss_head(xf, target):
    s = xf.shape[0]
    tm = _tile(s, 512)

    def kern(x_ref, t_ref, dx_ref, l_ref):
        err = x_ref[...] - t_ref[...]
        dx_ref[...] = err * (1.0 / DM)
        part = jnp.broadcast_to(0.5 * jnp.sum(jnp.mean(err * err, axis=-1, keepdims=True), axis=0, keepdims=True), (8, LANE))

        @pl.when(pl.program_id(0) == 0)
        def _():
            l_ref[...] = part

        @pl.when(pl.program_id(0) > 0)
        def _():
            l_ref[...] += part

    row = BS((tm, DM), lambda i: (i, 0))
    return _call(kern, name="loss_head", grid=(s // tm,), in_specs=[row, row],
                          out_specs=[row, BS((8, LANE), lambda i: (0, 0))],
                          out_shape=[jax.ShapeDtypeStruct((s, DM), F32), jax.ShapeDtypeStruct((8, LANE), F32)],
                          compiler_params=_params(1))(xf, target)


def _adamw(w, g, m, v, after=None):
    rows, cols = w.shape
    tr = _row_tile(rows)
    extra = () if after is None else (after,)

    def kern(w_ref, g_ref, m_ref, v_ref, *rest):
        d_ref, nm_ref, nv_ref = rest[-3:]
        gv = g_ref[...]
        nm = ADAM_B1 * m_ref[...] + (1.0 - ADAM_B1) * gv
        nv = ADAM_B2 * v_ref[...] + (1.0 - ADAM_B2) * (gv * gv)
        m_hat = nm / (1.0 - ADAM_B1 ** ADAM_STEP)
        v_hat = nv / (1.0 - ADAM_B2 ** ADAM_STEP)
        d_ref[...] = -ADAM_LR * (m_hat / (jnp.sqrt(v_hat) + ADAM_EPS) + ADAM_WD * w_ref[...])
        nm_ref[...] = nm
        nv_ref[...] = nv

    blk = BS((tr, cols), lambda i: (i, 0))
    shp = jax.ShapeDtypeStruct((rows, cols), F32)
    return _call(kern, name="adamw", grid=(rows // tr,), in_specs=[blk] * 4 + [ANY] * len(extra),
                          out_specs=[blk] * 3, out_shape=[shp] * 3, compiler_params=_params(1))(w, g, m, v, *extra)


def _adamw_nd(w, g, m, v, after=None):
    shape = w.shape
    two = (math.prod(shape[:-1]), shape[-1])
    return tuple(o.reshape(shape)
                 for o in _adamw(w.reshape(two), g.reshape(two), m.reshape(two), v.reshape(two), after))


def _pack_small(parts):
    flat = jnp.concatenate([p.reshape(-1) for p in parts])
    pad = (-flat.shape[0]) % (8 * LANE)
    return jnp.pad(flat, (0, pad)).reshape(-1, LANE)


def _unpack_small(packed, shapes, lead=()):
    flat = packed.reshape(lead + (-1,))
    out, off = [], 0
    for shp in shapes:
        n = math.prod(shp)
        out.append(flat[..., off:off + n].reshape(lead + tuple(shp)))
        off += n
    return out


WEIGHT_NAMES = ('g_mix_pre', 'g_mix_post', 'g_cross_pre', 'g_mem', 'g_cross_post', 'g_ffn_pre', 'g_ffn_post', 'w_xq',
                'w_xkv', 'w_xo', 'w_ffn_gu', 'w_ffn_down', 'ab_w_in', 'ab_b_f', 'ab_conv_w', 'ab_w_out', 'c_w_in',
                'c_conv_w', 'c_conv_b', 'c_w_a', 'c_b_a', 'c_w_i', 'c_b_i', 'c_lam', 'c_w_out')
BIG = ('w_xq', 'w_xkv', 'w_xo', 'w_ffn_gu', 'w_ffn_down', 'ab_w_in', 'ab_w_out', 'c_w_in', 'c_w_a', 'c_w_i', 'c_w_out')
SMALL_SHARDED = ('ab_conv_w', 'c_conv_w', 'c_conv_b', 'c_b_a', 'c_b_i', 'c_lam')
REPLICATED = ('g_mix_pre', 'g_mix_post', 'g_cross_pre', 'g_mem', 'g_cross_post', 'g_ffn_pre', 'g_ffn_post', 'ab_b_f')


def _small_full(name, gathered):
    nd = gathered.ndim
    return jnp.moveaxis(gathered, 0, nd - 2).reshape(gathered.shape[1:-1] + (NDEV * gathered.shape[-1],))


def _small_shard(full, dev):
    c = full.shape[-1] // NDEV
    return lax.dynamic_slice_in_dim(full, dev * c, c, axis=full.ndim - 1)


def kernel(x, mem, g_mix_pre, g_mix_post, g_cross_pre, g_mem, g_cross_post, g_ffn_pre, g_ffn_post, w_xq, w_xkv, w_xo, w_ffn_gu, w_ffn_down, ab_w_in, ab_b_f, ab_conv_w, ab_w_out, c_w_in, c_conv_w, c_conv_b, c_w_a, c_b_a, c_w_i, c_b_i, c_lam, c_w_out, loss_target, m_g_mix_pre, m_g_mix_post, m_g_cross_pre, m_g_mem, m_g_cross_post, m_g_ffn_pre, m_g_ffn_post, m_w_xq, m_w_xkv, m_w_xo, m_w_ffn_gu, m_w_ffn_down, m_ab_w_in, m_ab_b_f, m_ab_conv_w, m_ab_w_out, m_c_w_in, m_c_conv_w, m_c_conv_b, m_c_w_a, m_c_b_a, m_c_w_i, m_c_b_i, m_c_lam, m_c_w_out, v_g_mix_pre, v_g_mix_post, v_g_cross_pre, v_g_mem, v_g_cross_post, v_g_ffn_pre, v_g_ffn_post, v_w_xq, v_w_xkv, v_w_xo, v_w_ffn_gu, v_w_ffn_down, v_ab_w_in, v_ab_b_f, v_ab_conv_w, v_ab_w_out, v_c_w_in, v_c_conv_w, v_c_conv_b, v_c_w_a, v_c_b_a, v_c_w_i, v_c_b_i, v_c_lam, v_c_w_out):
    args = locals()
    w = {n: args[n] for n in WEIGHT_NAMES}
    mom = {n: args["m_" + n] for n in WEIGHT_NAMES}
    var = {n: args["v_" + n] for n in WEIGHT_NAMES}
    for t in (w, mom, var):
        t['w_ffn_gu'] = t['w_ffn_gu'].transpose(0, 2, 1)
    ab_t = [t['ab_w_in'].transpose(2, 0, 1) for t in (w, mom, var)]
    pos = jnp.stack([lax.axis_index("x"), lax.axis_index("y"), lax.axis_index("c")]).astype(jnp.int32)
    dev = 4 * pos[0] + 2 * pos[1] + pos[2]
    xs, mems, target = x[0], mem[0], loss_target[0]
    n_even, n_odd = (DEPTH + 1) // 2, DEPTH // 2

    small_shapes = [w[n].shape for n in SMALL_SHARDED]
    small_w_all = _small_gather(_pack_small([w[n] for n in SMALL_SHARDED]))
    gathered_small = _unpack_small(small_w_all, small_shapes, (NDEV,))
    small = {n: _small_full(n, g) for n, g in zip(SMALL_SHARDED, gathered_small)}
    ab_bfb = jnp.broadcast_to(ab_b_f[:, :, None], (n_even, FOX_H, LANE))
    c_bai = jnp.stack([small['c_b_a'].reshape(n_odd, DM), small['c_b_i'].reshape(n_odd, DM)], axis=1)
    row = lambda a, l: a[l][None]

    REST = ('w_xq', 'w_xkv', 'w_xo', 'w_ffn_gu', 'w_ffn_down')

    def mixer_names(l):
        return ('ab_w_in', 'ab_w_out') if l % 2 == 0 else ('c_w_in', 'c_w_a', 'c_w_i', 'c_w_out')

    def shards_of(l, names):
        out = []
        for n in names:
            if n == 'ab_w_in':
                s = ab_t[0][:, l // 2].astype(BF16)
            else:
                s = w[n][l if w[n].shape[0] == DEPTH else l // 2].astype(BF16)
            out.append(s.reshape(-1, s.shape[-1]))
        return out

    def mixer_weights(l, full):
        if l % 2 == 0:
            e = l // 2
            return (row(g_mix_pre, l), row(g_mix_post, l), _ab_pack(full['ab_w_in'].reshape(AB_IN, DM)), ab_bfb[e],
                    small['ab_conv_w'][e], full['ab_w_out'].reshape(DM, DM))
        o = l // 2
        gate_w = lambda g: g.reshape(NDEV, LRU_NB, LRU_BW // NDEV, LRU_BW).transpose(1, 0, 2, 3).reshape(
            LRU_NB, LRU_BW, LRU_BW)
        return (row(g_mix_pre, l), row(g_mix_post, l), full['c_w_in'], small['c_conv_w'][o], row(small['c_conv_b'], o),
                jnp.stack([gate_w(full['c_w_a']), gate_w(full['c_w_i'])]), c_bai[o], row(small['c_lam'], o),
                full['c_w_out'].reshape(DM, DM))

    def rest_weights(l, full):
        cross = (row(g_cross_pre, l), row(g_mem, l), row(g_cross_post, l), full['w_xq'].reshape(DM, DM), full['w_xkv'],
                 full['w_xo'].reshape(DM, DM))
        ffn = (row(g_ffn_pre, l), row(g_ffn_post, l), full['w_ffn_gu'], full['w_ffn_down'].reshape(D_FF, DM))
        return cross, ffn

    def gathered(state, names, after, tag):
        shards, lands = _ag_wait(state, after, "ag_wait_" + tag)
        full = _ag_finish(shards, lands)
        return dict(zip(names, full)), full[0]

    saved, weights = [], []
    h = xs
    names_of = lambda l: mixer_names(l) + REST
    states = {}
    st_m, _ = _ag_start(shards_of(0, mixer_names(0)), small_w_all, "ag_start_0m")
    st_r, _ = _ag_start(shards_of(0, REST), st_m[2], "ag_start_0r")
    states[1], token = _ag_start(shards_of(1, names_of(1)), st_r[2], "ag_start_1")
    full_m, _ = gathered(st_m, mixer_names(0), xs, "0m")
    for l in range(DEPTH):
        if l > 0:
            full, done = gathered(states[l], names_of(l), h, str(l))
            full_m = full_r = full
            token = None
            if l + 2 < DEPTH:
                states[l + 2], token = _ag_start(shards_of(l + 2, names_of(l + 2)), done, "ag_start_%d" % (l + 2))
        mixer = mixer_weights(l, full_m)
        h, s_mix = (_fox_layer_fwd if l % 2 == 0 else _lru_layer_fwd)(h, *mixer, after=token)
        token = None
        if l == 0:
            full_r, done = gathered(st_r, REST, h, "0r")
            states[2], token = _ag_start(shards_of(2, names_of(2)), done, "ag_start_2")
        cross, ffn = rest_weights(l, full_r)
        h, s_cross = _cross_fwd(h, mems, *cross, after=token)
        h, s_ffn = _ffn_fwd(h, *ffn)
        saved.append((s_mix, s_cross, s_ffn))
        weights.append((mixer, cross, ffn))
    mixer_args = lambda l: weights[l][0]
    cross_args = lambda l: weights[l][1]
    ffn_args = lambda l: weights[l][2]
    dx, loss_rep = _loss_head(h, target)
    loss = lax.psum(loss_rep[0, 0], ("x", "y", "c"))

    grads = {n: [None] * w[n].shape[0] for n in BIG}
    partial = {n: [None] * w[n].shape[0] for n in REPLICATED + SMALL_SHARDED}
    def finish(pending, after):
        state, names, where = pending
        for n, g in zip(names, _rs_end(state, after, pos)):
            grads[n][where[n]] = g

    def unit(layer, names):
        return [layer[n][1] for n in names], names, {n: layer[n][0] for n in names}

    d2d = ici = None
    token = None
    for l in reversed(range(DEPTH)):
        s_mix, s_cross, s_ffn = saved[l]
        dx, partial['g_ffn_pre'][l], partial['g_ffn_post'][l], dwgu, dwd = _ffn_bwd(dx, s_ffn, *ffn_args(l), after=token)
        token = None
        if d2d is not None:
            g5s, gots = _rs_d2d_wait(d2d[0], dx, "rs_d2d_wait_%d" % (l + 1))
            state, token = _rs_mid(g5s, gots, pos, str(l + 1))
            ici, d2d = (state,) + d2d[1:], None
        (dx, partial['g_cross_pre'][l], partial['g_mem'][l], partial['g_cross_post'][l], dwq, dwkv, dwo) = _cross_bwd(
            dx, s_cross, mems, *cross_args(l), after=token)
        token = None
        layer = {'w_xq': (l, dwq.reshape(NDEV, DM // NDEV, DM)), 'w_xkv': (l, dwkv), 'w_xo': (l, dwo.reshape(NDEV, DM // NDEV, DM)),
                 'w_ffn_gu': (l, dwgu), 'w_ffn_down': (l, dwd.reshape(NDEV, D_FF // NDEV, DM))}
        if l == 0:
            gs, names, where = unit(layer, REST)
            state, token = _rs_begin(gs, pos, "0r")
            ici_rest = (state, names, where)
        if l % 2 == 0:
            e = l // 2
            (dx, partial['g_mix_pre'][l], partial['g_mix_post'][l], dwall, partial['ab_b_f'][e], partial['ab_conv_w'][e],
             dwout) = _fox_layer_bwd(dx, s_mix, *mixer_args(l), after=token)
            layer['ab_w_in'] = (e, _ab_unpack(dwall).reshape(NDEV, AB_IN // NDEV, DM))
            layer['ab_w_out'] = (e, dwout.reshape(NDEV, DM // NDEV, DM))
        else:
            o = l // 2
            (dx, partial['g_mix_pre'][l], partial['g_mix_post'][l], dwin, partial['c_conv_w'][o], dconvb, dwai, dbai, dlam,
             dwout) = _lru_layer_bwd(dx, s_mix, *mixer_args(l), after=token)
            partial['c_conv_b'][o], partial['c_lam'][o] = dconvb[0], dlam[0]
            partial['c_b_a'][o], partial['c_b_i'][o] = dbai[0].reshape(LRU_NB, LRU_BW), dbai[1].reshape(LRU_NB, LRU_BW)
            rows = LRU_BW // NDEV
            by_dev = lambda d: d.reshape(LRU_NB, NDEV, rows, LRU_BW).transpose(1, 0, 2, 3).reshape(NDEV, LRU_NB * rows, LRU_BW)
            layer['c_w_in'] = (o, dwin)
            layer['c_w_a'] = (o, by_dev(dwai[0]))
            layer['c_w_i'] = (o, by_dev(dwai[1]))
            layer['c_w_out'] = (o, dwout.reshape(NDEV, DM // NDEV, DM))
        token = None
        if ici is not None:
            finish(ici, dx)
            ici = None
        if l > 0:
            gs, names, where = unit(layer, list(layer))
            state, token = _rs_d2d_start(_as_g5(gs), "rs_d2d_start_%d" % l)
            d2d = (state, names, where)
    small_names = REPLICATED + SMALL_SHARDED
    small_parts = [jnp.stack([p.reshape(w[n].shape[1:] if n in REPLICATED else small[n].shape[1:]) for p in partial[n]])
                   for n in small_names]
    small_all = _small_gather(_pack_small(small_parts))
    reduced = _unpack_small(_sum_devices(small_all), [p.shape for p in small_parts])
    grad = {}
    for n, g in zip(small_names, reduced):
        grad[n] = g if n in REPLICATED else _small_shard(g, dev)

    gs, names, where = unit(layer, mixer_names(0))
    state, token = _rs_begin(gs, pos, "0m", after=small_all)
    ici_mixer = (state, names, where)
    finish(ici_rest, dx)

    delta, new_m, new_v = {}, {}, {}
    last = mixer_names(0)
    for n in BIG:
        if n not in last:
            grad[n] = jnp.stack(grads[n]).reshape(w[n].shape)
            delta[n], new_m[n], new_v[n] = _adamw_nd(w[n], grad[n], mom[n], var[n], token)
            token = delta[n]
    shapes = [w[n].shape for n in small_names]
    packed = [_pack_small([t[n] for n in small_names]) for t in (w, grad, mom, var)]
    res_small = _adamw(*packed, after=token)
    for res, out in zip(res_small, (delta, new_m, new_v)):
        for n, val in zip(small_names, _unpack_small(res, shapes)):
            out[n] = val
    finish(ici_mixer, res_small[0])
    for n in last:
        if n == 'ab_w_in':
            g_t = jnp.stack(grads[n], axis=1)
            res = (g_t,) + _adamw_nd(ab_t[0], g_t, ab_t[1], ab_t[2])
            grad[n], delta[n], new_m[n], new_v[n] = (r.transpose(1, 2, 0) for r in res)
            continue
        grad[n] = jnp.stack(grads[n]).reshape(w[n].shape)
        delta[n], new_m[n], new_v[n] = _adamw_nd(w[n], grad[n], mom[n], var[n])

    for t in (grad, delta, new_m, new_v):
        t['w_ffn_gu'] = t['w_ffn_gu'].transpose(0, 2, 1)
    return (loss, dx[None], *[grad[n] for n in WEIGHT_NAMES], *[delta[n] for n in WEIGHT_NAMES],
            *[new_m[n] for n in WEIGHT_NAMES], *[new_v[n] for n in WEIGHT_NAMES])
```

```python
import math

import jax
import jax.numpy as jnp
from jax import lax
from jax.experimental import pallas as pl
from jax.experimental.pallas import tpu as pltpu

F32 = jnp.float32
BF16 = jnp.bfloat16
BS = pl.BlockSpec
ANY = pl.BlockSpec(memory_space=pl.ANY)
MESH = pl.DeviceIdType.MESH

DM = 1024
DEPTH = 4
EPS = 1e-6
NEG = -1e30
FOX_W = 512
FOX_HD = 64
FOX_H = 8
SC_W = 512
SC_K = 3
AB_IN = 3 * FOX_W + FOX_H + 3 * SC_W
AB_PAD = 3200
LRU_BW = 256
LRU_NB = 4
RG_K = 4
RG_C = 8.0
MEM_H = 4
MEM_HD = 256
D_FF = 2816
NDEV = 8
FFB = 2 * D_FF // NDEV
ADAM_LR, ADAM_B1, ADAM_B2, ADAM_EPS, ADAM_WD, ADAM_STEP = 0.001, 0.9, 0.999, 1e-08, 0.01, 10

LANE = 128
VMEM_LIMIT = 24 * 1024 * 1024
VMEM_BIG = 48 * 1024 * 1024


def _params(ngrid, vmem=None):
    return pltpu.CompilerParams(dimension_semantics=("arbitrary",) * ngrid, vmem_limit_bytes=vmem or VMEM_LIMIT)


def _weights(w):
    return pltpu.with_memory_space_constraint(w, pltpu.HBM)


def _call(kern, **kwargs):
    return pl.pallas_call(kern, **kwargs)


TK_RED = 2048
TM_SUM = 512


def _tile(n, t):
    return t if n % t == 0 else n


def _mm(name, a, b, *, grid, a_spec, b_spec, o_spec, out_shape, dn, out_dtype=F32, vmem=None):
    nred = grid[-1]
    ngrid = len(grid)

    def kern(a_ref, b_ref, o_ref, *scratch):
        p = lax.dot_general(a_ref[...].astype(BF16), b_ref[...].astype(BF16), (dn, ((), ())),
                            preferred_element_type=F32)
        if nred == 1:
            o_ref[...] = p.astype(o_ref.dtype)
            return
        acc = scratch[0] if scratch else o_ref
        r = pl.program_id(ngrid - 1)

        @pl.when(r == 0)
        def _():
            acc[...] = p

        @pl.when(r > 0)
        def _():
            acc[...] += p

        if scratch:
            @pl.when(r == nred - 1)
            def _():
                o_ref[...] = acc[...].astype(o_ref.dtype)

    blk = tuple(d for d in o_spec.block_shape if d is not None)
    scratch = [pltpu.VMEM(blk, F32)] if (nred > 1 and out_dtype != F32) else []
    return _call(kern, name=name, grid=grid, in_specs=[a_spec, b_spec], out_specs=o_spec,
                          out_shape=jax.ShapeDtypeStruct(out_shape, out_dtype), scratch_shapes=scratch,
                          compiler_params=_params(ngrid, vmem))(a, b if dn == TN else _weights(b))


NN = ((1,), (0,))
NT = ((1,), (1,))
TN = ((0,), (0,))


def _mm_nn(name, a, w, out_dtype=F32, tn=None, vmem=None):
    m, k = a.shape
    n = w.shape[1]
    tm = _tile(m, 512)
    tn = n if tn is None else tn
    return _mm(name, a, w, grid=(m // tm, n // tn, 1), a_spec=BS((tm, k), lambda i, j, r: (i, 0)),
               b_spec=BS((k, tn), lambda i, j, r: (0, j)), o_spec=BS((tm, tn), lambda i, j, r: (i, j)),
               out_shape=(m, n), dn=NN, out_dtype=out_dtype, vmem=vmem)


def _mm_nt_cols(name, a, wt, tn):
    m, k = a.shape
    n = wt.shape[0]
    tm = _tile(m, 512)
    return _mm(name, a, wt, grid=(m // tm, n // tn, 1), a_spec=BS((tm, k), lambda i, j, r: (i, 0)),
               b_spec=BS((tn, k), lambda i, j, r: (j, 0)), o_spec=BS((tm, tn), lambda i, j, r: (i, j)),
               out_shape=(m, n), dn=NT)


def _mm_tn_rows(name, a, b, tk):
    m, k = a.shape
    n = b.shape[1]
    tm = _tile(m, TK_RED)
    return _mm(name, a, b, grid=(k // tk, m // tm), a_spec=BS((tm, tk), lambda j, r: (r, j)),
               b_spec=BS((tm, n), lambda j, r: (r, 0)), o_spec=BS((tk, n), lambda j, r: (j, 0)),
               out_shape=(k, n), dn=TN)


def _mm_nt(name, a, w, out_dtype=F32, tn=None):
    m, n = a.shape
    k = w.shape[0]
    tm = _tile(m, 512)
    tn = n if tn is None else tn
    return _mm(name, a, w, grid=(m // tm, n // tn), a_spec=BS((tm, tn), lambda i, r: (i, r)),
               b_spec=BS((k, tn), lambda i, r: (0, r)), o_spec=BS((tm, k), lambda i, r: (i, 0)),
               out_shape=(m, k), dn=NT, out_dtype=out_dtype)


def _mm_tn(name, a, b, tn=None):
    m, k = a.shape
    n = b.shape[1]
    tm = _tile(m, TK_RED)
    tn = n if tn is None else tn
    return _mm(name, a, b, grid=(n // tn, m // tm), a_spec=BS((tm, k), lambda j, r: (r, 0)),
               b_spec=BS((tm, tn), lambda j, r: (r, j)), o_spec=BS((k, tn), lambda j, r: (0, j)),
               out_shape=(k, n), dn=TN)


def _bmm_nn(name, a, w, out_dtype=F32):
    m, k = a.shape
    g, _, n = w.shape
    tm = _tile(m, 512)
    return _mm(name, a, w, grid=(g, m // tm, 1), a_spec=BS((tm, k), lambda q, i, r: (i, 0)),
               b_spec=BS((None, k, n), lambda q, i, r: (q, 0, 0)), o_spec=BS((None, tm, n), lambda q, i, r: (q, i, 0)),
               out_shape=(g, m, n), dn=NN, out_dtype=out_dtype)


def _bmm_tn(name, a, b):
    m, k = a.shape
    g, _, n = b.shape
    tm = _tile(m, TK_RED)
    return _mm(name, a, b, grid=(g, m // tm), a_spec=BS((tm, k), lambda q, r: (r, 0)),
               b_spec=BS((None, tm, n), lambda q, r: (q, r, 0)), o_spec=BS((None, k, n), lambda q, r: (q, 0, 0)),
               out_shape=(g, k, n), dn=TN)


def _block_sum(name, a, w, dn, out_cols):
    g, m, ac = a.shape
    tm = _tile(m, TM_SUM)

    def kern(a_ref, w_ref, o_ref):
        acc = None
        for q in range(g):
            p = lax.dot_general(a_ref[q].astype(BF16), w_ref[q].astype(BF16), (dn, ((), ())), preferred_element_type=F32)
            acc = p if acc is None else acc + p
        o_ref[...] = acc

    return _call(kern, name=name, grid=(m // tm,),
                 in_specs=[BS((g, tm, ac), lambda i: (0, i, 0)), BS(w.shape, lambda i: (0, 0, 0))],
                 out_specs=BS((tm, out_cols), lambda i: (i, 0)), out_shape=jax.ShapeDtypeStruct((m, out_cols), F32),
                 compiler_params=_params(1, VMEM_BIG))(a, _weights(w))


def _bmm_nt_sum(name, a, w):
    return _block_sum(name, a, w, NT, w.shape[1])


def _bmm_nn_sum(name, a, w):
    return _block_sum(name, a, w, NN, w.shape[2])


def _rstd(x):
    return lax.rsqrt(jnp.mean(x * x, axis=-1, keepdims=True) + EPS)


def _norm_fwd(x, g, after=None):
    rows = x.shape[0]
    tm = _tile(rows, 512)

    def kern(x_ref, g_ref, *rest):
        xv = x_ref[...]
        rest[-1][...] = ((xv * _rstd(xv)) * g_ref[...]).astype(BF16)

    extra = () if after is None else (after,)
    return _call(kern, name="norm_fwd", grid=(rows // tm,),
                          in_specs=[BS((tm, DM), lambda i: (i, 0)), BS((1, DM), lambda i: (0, 0))] + [ANY] * len(extra),
                          out_specs=BS((tm, DM), lambda i: (i, 0)),
                          out_shape=jax.ShapeDtypeStruct((rows, DM), BF16), compiler_params=_params(1))(x, g, *extra)


def _norm_res(x, y, g):
    rows = x.shape[0]
    tm = _tile(rows, 512)

    def kern(x_ref, y_ref, g_ref, o_ref):
        yv = y_ref[...]
        o_ref[...] = x_ref[...] + (yv * _rstd(yv)) * g_ref[...]

    row = BS((tm, DM), lambda i: (i, 0))
    return _call(kern, name="norm_res", grid=(rows // tm,),
                          in_specs=[row, row, BS((1, DM), lambda i: (0, 0))], out_specs=row,
                          out_shape=jax.ShapeDtypeStruct((rows, DM), F32), compiler_params=_params(1))(x, y, g)


def _norm_bwd(z, dout, g, resid, out_dtype, after=None):
    rows = z.shape[0]
    tm = _tile(rows, 512)
    has_res = resid is not None

    def kern(*refs):
        z_ref, d_ref, g_ref = refs[:3]
        r_ref = refs[3] if has_res else None
        dz_ref, dg_ref = refs[-2:]
        zv = z_ref[...]
        dv = d_ref[...].astype(F32)
        r = _rstd(zv)
        zh = zv * r
        dzh = dv * g_ref[...]
        dz = r * (dzh - zh * jnp.mean(dzh * zh, axis=-1, keepdims=True))
        if has_res:
            dz = dz + r_ref[...]
        dz_ref[...] = dz.astype(dz_ref.dtype)
        part = jnp.sum(dv * zh, axis=0, keepdims=True)

        @pl.when(pl.program_id(0) == 0)
        def _():
            dg_ref[...] = part

        @pl.when(pl.program_id(0) > 0)
        def _():
            dg_ref[...] += part

    row = BS((tm, DM), lambda i: (i, 0))
    vec = BS((1, DM), lambda i: (0, 0))
    ins = [row, row, vec] + ([row] if has_res else []) + ([ANY] if after is not None else [])
    args = (z, dout, g) + ((resid,) if has_res else ()) + ((after,) if after is not None else ())
    return _call(kern, name="norm_bwd_res" if has_res else "norm_bwd", grid=(rows // tm,), in_specs=ins,
                          out_specs=[row, vec],
                          out_shape=[jax.ShapeDtypeStruct((rows, DM), out_dtype), jax.ShapeDtypeStruct((1, DM), F32)],
                          compiler_params=_params(1))(*args)


def _ffn_up(h, wgu4):
    s = h.shape[0]
    tm = _tile(s, 512)

    def kern(h_ref, w_ref, gu_ref, a_ref):
        hv = h_ref[...]
        gate = lax.dot_general(hv, w_ref[0], (NT, ((), ())), preferred_element_type=F32)
        up = lax.dot_general(hv, w_ref[1], (NT, ((), ())), preferred_element_type=F32)
        gu_ref[0] = gate.astype(BF16)
        gu_ref[1] = up.astype(BF16)
        a_ref[...] = (gate * jax.nn.sigmoid(gate) * up).astype(BF16)

    return _call(
        kern, name="ffn_up", grid=(4, s // tm),
        in_specs=[BS((tm, DM), lambda j, i: (i, 0)), BS((2, None, FFB, DM), lambda j, i: (0, j, 0, 0))],
        out_specs=[BS((2, None, tm, FFB), lambda j, i: (0, j, i, 0)), BS((None, tm, FFB), lambda j, i: (j, i, 0))],
        out_shape=[jax.ShapeDtypeStruct((2, 4, s, FFB), BF16), jax.ShapeDtypeStruct((4, s, FFB), BF16)],
        compiler_params=_params(2))(h, _weights(wgu4))


def _ffn_da(dy, wd4, gu):
    s = dy.shape[0]
    tm = _tile(s, 512)

    def kern(dy_ref, w_ref, gu_ref, o_ref):
        da = lax.dot_general(dy_ref[...], w_ref[...], (NT, ((), ())), preferred_element_type=F32)
        gate = gu_ref[0].astype(F32)
        up = gu_ref[1].astype(F32)
        sg = jax.nn.sigmoid(gate)
        o_ref[0] = (da * up * (sg * (1.0 + gate * (1.0 - sg)))).astype(BF16)
        o_ref[1] = (da * (gate * sg)).astype(BF16)

    blk = BS((2, None, tm, FFB), lambda j, i: (0, j, i, 0))
    return _call(
        kern, name="ffn_da", grid=(4, s // tm),
        in_specs=[BS((tm, DM), lambda j, i: (i, 0)), BS((None, FFB, DM), lambda j, i: (j, 0, 0)), blk],
        out_specs=blk, out_shape=jax.ShapeDtypeStruct((2, 4, s, FFB), BF16), compiler_params=_params(2))(dy, _weights(wd4), gu)


def _ffn_fwd(x, gpre, gpost, wgu, wd):
    h = _norm_fwd(x, gpre)
    gu, a = _ffn_up(h, wgu.reshape(2, 4, FFB, DM))
    y = _bmm_nn_sum("ffn_down", a, wd.reshape(4, FFB, DM))
    return _norm_res(x, y, gpost), (x, h, gu, a, y)


def _ffn_bwd(dxo, saved, gpre, gpost, wgu, wd, after=None):
    x, h, gu, a, y = saved
    s = x.shape[0]
    dy, dgpost = _norm_bwd(y, dxo, gpost, None, BF16, after)
    dgu = _ffn_da(dy, wd.reshape(4, FFB, DM), gu).reshape(8, s, FFB)
    dwd = _bmm_tn_a3("ffn_dwd", a, dy)
    dwgu = _bmm_tn_a3("ffn_dwgu", dgu, h)
    dh = _bmm_nn_sum("ffn_dh", dgu, wgu)
    dx, dgpre = _norm_bwd(x, dh, gpre, dxo, F32)
    return dx, dgpre, dgpost, dwgu, dwd.reshape(D_FF, DM)


def _bmm_tn_a3(name, a, b):
    g, m, k = a.shape
    n = b.shape[1]
    tm = _tile(m, TK_RED)
    return _mm(name, a, b, grid=(g, m // tm), a_spec=BS((None, tm, k), lambda q, r: (q, r, 0)),
               b_spec=BS((tm, n), lambda q, r: (r, 0)), o_spec=BS((None, k, n), lambda q, r: (q, 0, 0)),
               out_shape=(g, k, n), dn=TN)


def _softmax_rows(s):
    m = jnp.max(s, axis=-1, keepdims=True)
    p = jnp.exp(s - m)
    return p / jnp.sum(p, axis=-1, keepdims=True)


def _xattn_fwd_call(h, wq, kv):
    s = h.shape[0]
    mlen = kv.shape[1]
    tm = _tile(s, 512)
    scale = MEM_HD ** -0.5

    def kern(h_ref, w_ref, k_ref, v_ref, q_ref, o_ref):
        q = jnp.dot(h_ref[...], w_ref[...], preferred_element_type=F32).astype(BF16)
        q_ref[...] = q
        sc = lax.dot_general(q, k_ref[...], (NT, ((), ())), preferred_element_type=F32) * scale
        p = _softmax_rows(sc)
        o_ref[...] = jnp.dot(p.astype(BF16), v_ref[...], preferred_element_type=F32).astype(BF16)

    blk = BS((tm, MEM_HD), lambda i, hd: (i, hd))
    return _call(
        kern, name="xattn_fwd", grid=(s // tm, MEM_H),
        in_specs=[BS((tm, DM), lambda i, hd: (i, 0)), BS((DM, MEM_HD), lambda i, hd: (0, hd)),
                  BS((None, mlen, MEM_HD), lambda i, hd: (hd, 0, 0)),
                  BS((None, mlen, MEM_HD), lambda i, hd: (MEM_H + hd, 0, 0))],
        out_specs=[blk, blk],
        out_shape=[jax.ShapeDtypeStruct((s, DM), BF16), jax.ShapeDtypeStruct((s, DM), BF16)],
        compiler_params=_params(2))(h, _weights(wq), kv, kv)


def _xattn_bwd_call(q, kv, do):
    s = q.shape[0]
    mlen = kv.shape[1]
    tm = _tile(s, 512)
    scale = MEM_HD ** -0.5

    def kern(q_ref, k_ref, v_ref, do_ref, dq_ref, dkv_ref):
        qv, kvv, vv, dov = q_ref[...], k_ref[...], v_ref[...], do_ref[...]
        sc = lax.dot_general(qv, kvv, (NT, ((), ())), preferred_element_type=F32) * scale
        p = _softmax_rows(sc)
        dp = lax.dot_general(dov, vv, (NT, ((), ())), preferred_element_type=F32)
        ds = (p * (dp - jnp.sum(dp * p, axis=-1, keepdims=True)) * scale).astype(BF16)
        dq_ref[...] = jnp.dot(ds, kvv, preferred_element_type=F32).astype(BF16)
        dk = lax.dot_general(ds, qv, (TN, ((), ())), preferred_element_type=F32)
        dv = lax.dot_general(p.astype(BF16), dov, (TN, ((), ())), preferred_element_type=F32)

        @pl.when(pl.program_id(1) == 0)
        def _():
            dkv_ref[0] = dk
            dkv_ref[1] = dv

        @pl.when(pl.program_id(1) > 0)
        def _():
            dkv_ref[0] += dk
            dkv_ref[1] += dv

    blk = BS((tm, MEM_HD), lambda hd, i: (i, hd))
    return _call(
        kern, name="xattn_bwd", grid=(MEM_H, s // tm),
        in_specs=[blk, BS((None, mlen, MEM_HD), lambda hd, i: (hd, 0, 0)),
                  BS((None, mlen, MEM_HD), lambda hd, i: (MEM_H + hd, 0, 0)), blk],
        out_specs=[blk, BS((2, None, mlen, MEM_HD), lambda hd, i: (0, hd, 0, 0))],
        out_shape=[jax.ShapeDtypeStruct((s, DM), BF16), jax.ShapeDtypeStruct((2, MEM_H, mlen, MEM_HD), F32)],
        compiler_params=_params(2))(q, kv, kv, do)


def _cross_fwd(x, mem, gpre, gmem, gpost, wq, wkv, wo, after=None):
    h = _norm_fwd(x, gpre, after)
    mn = _norm_fwd(mem, gmem)
    kv = _bmm_nn("xattn_kv", mn, wkv, BF16)
    q, o = _xattn_fwd_call(h, wq, kv)
    y = _mm_nn("xattn_out", o, wo)
    return _norm_res(x, y, gpost), (x, h, mn, kv, q, o, y)


def _cross_bwd(dxo, saved, mem, gpre, gmem, gpost, wq, wkv, wo, after=None):
    x, h, mn, kv, q, o, y = saved
    mlen = mem.shape[0]
    dy, dgpost = _norm_bwd(y, dxo, gpost, None, BF16, after)
    do = _mm_nt("xattn_do", dy, wo, BF16)
    dwo = _mm_tn("xattn_dwo", o, dy)
    dq, dkv = _xattn_bwd_call(q, kv, do)
    dwq = _mm_tn("xattn_dwq", h, dq)
    dh = _mm_nt("xattn_dh", dq, wq)
    dkv8 = dkv.reshape(8, mlen, MEM_HD)
    dwkv = _bmm_tn("xattn_dwkv", mn, dkv8)
    dmn = _bmm_nt_sum("xattn_dmn", dkv8, wkv)
    _, dgmem = _norm_bwd(mem, dmn, gmem, None, BF16)
    dx, dgpre = _norm_bwd(x, dh, gpre, dxo, F32)
    return dx, dgpre, dgmem, dgpost, dwq, dwkv, dwo


def _log_sigmoid(z):
    return jnp.minimum(z, 0.0) - jnp.log1p(jnp.exp(-jnp.abs(z)))


def _lane_scan_steps():
    return (1, 2, 4, 8, 16, 32, 64)


def _fox_cum(frow, bfb):
    s = frow.shape[1]

    def kern(f_ref, b_ref, o_ref):
        lane = lax.broadcasted_iota(jnp.int32, (FOX_H, LANE), 1)
        carry = jnp.zeros((FOX_H, 1), F32)
        for c in range(s // LANE):
            sl = slice(c * LANE, (c + 1) * LANE)
            lf = _log_sigmoid(f_ref[:, sl] + b_ref[...])
            v = lf
            for d in _lane_scan_steps():
                v = v + jnp.where(lane >= d, pltpu.roll(v, d, 1), 0.0)
            o_ref[:, sl] = v + carry
            carry = carry + jnp.sum(lf, axis=1, keepdims=True)

    return _call(kern, name="fox_cum", out_shape=jax.ShapeDtypeStruct((FOX_H, s), F32),
                          compiler_params=pltpu.CompilerParams(vmem_limit_bytes=VMEM_LIMIT))(frow, bfb)


def _fox_dlogf(dcq, dck, frow, bfb):
    s = frow.shape[1]

    def kern(q_ref, d_ref, f_ref, b_ref, df_ref, db_ref):
        lane = lax.broadcasted_iota(jnp.int32, (FOX_H, LANE), 1)
        carry = jnp.zeros((FOX_H, 1), F32)
        dbf = jnp.zeros((FOX_H, 1), F32)
        for c in reversed(range(s // LANE)):
            sl = slice(c * LANE, (c + 1) * LANE)
            dc = q_ref[:, sl] - d_ref[:, sl]
            v = dc
            for d in _lane_scan_steps():
                v = v + jnp.where(lane < LANE - d, pltpu.roll(v, LANE - d, 1), 0.0)
            v = v + carry
            carry = carry + jnp.sum(dc, axis=1, keepdims=True)
            df = v * jax.nn.sigmoid(-(f_ref[:, sl] + b_ref[...]))
            df_ref[:, sl] = df
            dbf = dbf + jnp.sum(df, axis=1, keepdims=True)
        db_ref[...] = jnp.broadcast_to(dbf, (FOX_H, LANE))

    return _call(kern, name="fox_dlogf",
                          out_shape=[jax.ShapeDtypeStruct((FOX_H, s), F32), jax.ShapeDtypeStruct((FOX_H, LANE), F32)],
                          compiler_params=pltpu.CompilerParams(vmem_limit_bytes=VMEM_LIMIT))(dcq, dck, frow, bfb)


FOX_TQ = 512
Q_COL, K_COL, V_COL = 0, FOX_W // LANE, 2 * FOX_W // LANE
B_COL = 3 * FOX_W // LANE
C_COL = B_COL + SC_W // LANE
U_COL = C_COL + SC_W // LANE


def _bf16_terms(c):
    hi = c.astype(BF16).astype(F32)
    mid = (c - hi).astype(BF16).astype(F32)
    return hi, mid, (c - hi - mid).astype(BF16).astype(F32)


def _fox_operands(qv, kv, cq, ck, lane, hh, scale):
    sel = (lane < FOX_HD) if hh == 0 else (lane >= FOX_HD)
    b0 = FOX_HD if hh == 0 else 0
    qa = jnp.where(sel, qv * scale, 0.0)
    ka = jnp.where(sel, kv, 0.0)
    for n, (tq_, tk_) in enumerate(zip(_bf16_terms(cq), _bf16_terms(ck))):
        qa = jnp.where(lane == b0 + n, tq_, jnp.where(lane == b0 + 3 + n, 1.0, qa))
        ka = jnp.where(lane == b0 + n, 1.0, jnp.where(lane == b0 + 3 + n, -tk_, ka))
    return sel, qa.astype(BF16), ka.astype(BF16)


def _fox_logits(qa, ka, causal):
    sc = lax.dot_general(qa, ka, (NT, ((), ())), preferred_element_type=F32)
    return sc if causal is None else jnp.where(causal, sc, NEG)


def _fox_prep(proj, cumc):
    s = proj.shape[0]
    tp = _tile(s, 512)
    scale = FOX_HD ** -0.5

    def kern(q_ref, k_ref, c_ref, qa_ref, ka_ref):
        lane = lax.broadcasted_iota(jnp.int32, (tp, LANE), 1)
        for hh in range(2):
            _, qa_ref[hh], ka_ref[hh] = _fox_operands(q_ref[...], k_ref[...], c_ref[hh], c_ref[hh], lane, hh, scale)

    pair = BS((2, tp, LANE), lambda hp, i: (hp, i, 0))
    shp = jax.ShapeDtypeStruct((FOX_H, s, LANE), BF16)
    return _call(kern, name="fox_prep", grid=(4, s // tp),
                 in_specs=[BS((tp, LANE), lambda hp, i: (i, Q_COL + hp)), BS((tp, LANE), lambda hp, i: (i, K_COL + hp)), pair],
                 out_specs=[pair, pair], out_shape=[shp, shp], compiler_params=_params(2))(proj, proj, cumc)


def _fox_fwd_call(proj, qa, ka):
    s = proj.shape[0]
    tq = _tile(s, FOX_TQ)
    nq = s // tq

    def kern(qa_ref, ka_ref, v_ref, o_ref, lse_ref, m_s, l_s, acc_s):
        i = pl.program_id(1)
        j = pl.program_id(2)
        lane = lax.broadcasted_iota(jnp.int32, (tq, LANE), 1)

        @pl.when(j == 0)
        def _():
            m_s[...] = jnp.full(m_s.shape, NEG, F32)
            l_s[...] = jnp.zeros(l_s.shape, F32)
            acc_s[...] = jnp.zeros(acc_s.shape, F32)

        def step(diagonal):
            vb = v_ref[...].astype(BF16)
            causal = (lax.broadcasted_iota(jnp.int32, (tq, tq), 0) >= lax.broadcasted_iota(jnp.int32, (tq, tq), 1)
                      if diagonal else None)
            for hh in range(2):
                sc = _fox_logits(qa_ref[hh], ka_ref[hh], causal)
                m_prev = m_s[hh]
                m_new = jnp.maximum(m_prev, jnp.max(sc, axis=-1, keepdims=True))
                alpha = jnp.exp(m_prev - m_new)
                p = jnp.exp(sc - m_new)
                l_s[hh] = alpha * l_s[hh] + jnp.sum(p, axis=-1, keepdims=True)
                acc_s[hh] = alpha * acc_s[hh] + jnp.dot(p.astype(BF16), vb, preferred_element_type=F32)
                m_s[hh] = m_new

        @pl.when(j < i)
        def _():
            step(False)

        @pl.when(j == i)
        def _():
            step(True)
            o_ref[...] = jnp.where(lane < FOX_HD, acc_s[0] / l_s[0], acc_s[1] / l_s[1])
            for hh in range(2):
                lse_ref[hh] = jnp.broadcast_to(m_s[hh] + jnp.log(l_s[hh]), (tq, LANE))

    kvi = lambda hp, i, j: jnp.minimum(j, i)
    return _call(
        kern, name="fox_fwd", grid=(4, nq, nq),
        in_specs=[BS((2, tq, LANE), lambda hp, i, j: (hp, i, 0)),
                  BS((2, tq, LANE), lambda hp, i, j: (hp, kvi(hp, i, j), 0)),
                  BS((tq, LANE), lambda hp, i, j: (kvi(hp, i, j), V_COL + hp))],
        out_specs=[BS((tq, LANE), lambda hp, i, j: (i, hp)), BS((2, tq, LANE), lambda hp, i, j: (hp, i, 0))],
        out_shape=[jax.ShapeDtypeStruct((s, FOX_W), F32), jax.ShapeDtypeStruct((FOX_H, s, LANE), F32)],
        scratch_shapes=[pltpu.VMEM((2, tq, 1), F32), pltpu.VMEM((2, tq, 1), F32), pltpu.VMEM((2, tq, LANE), F32)],
        compiler_params=_params(3))(qa, ka, proj)


ROWSUM_M = 16


def _fox_bwd_call(proj, o, lse, dcat, qa, ka):
    s = proj.shape[0]
    tq = _tile(s, FOX_TQ)
    nq = s // tq
    reps = tq // LANE
    scale = FOX_HD ** -0.5

    def kern(qa_ref, ka_ref, v_ref, do_ref, o_ref, lse_ref, dq_ref, dk_ref, dv_ref, dck_ref, dcq_ref):
        j = pl.program_id(1)
        i = pl.program_id(2)
        lane = lax.broadcasted_iota(jnp.int32, (tq, LANE), 1)
        ones = jnp.ones((ROWSUM_M, tq), BF16)

        @pl.when((j == 0) & (i == 0))
        def _():
            dq_ref[...] = jnp.zeros(dq_ref.shape, F32)
            dcq_ref[...] = jnp.zeros(dcq_ref.shape, F32)

        @pl.when(i == j)
        def _():
            dk_ref[...] = jnp.zeros(dk_ref.shape, F32)
            dv_ref[...] = jnp.zeros(dv_ref.shape, F32)
            dck_ref[...] = jnp.zeros(dck_ref.shape, F32)

        def step(diagonal):
            dov = do_ref[...]
            ov = o_ref[...]
            vb = v_ref[...].astype(BF16)
            causal = (lax.broadcasted_iota(jnp.int32, (tq, tq), 0) >= lax.broadcasted_iota(jnp.int32, (tq, tq), 1)
                      if diagonal else None)
            dq_t = jnp.zeros((tq, LANE), F32)
            dk_t = jnp.zeros((tq, LANE), F32)
            dv_t = jnp.zeros((tq, LANE), F32)
            for hh in range(2):
                sel = (lane < FOX_HD) if hh == 0 else (lane >= FOX_HD)
                qa, ka = qa_ref[hh], ka_ref[hh]
                dom32 = jnp.where(sel, dov, 0.0)
                dom = dom32.astype(BF16)
                sc = _fox_logits(qa, ka, causal)
                p = jnp.exp(sc - jnp.tile(lse_ref[hh], (1, reps)))
                dp = lax.dot_general(dom, vb, (NT, ((), ())), preferred_element_type=F32)
                delta = jnp.sum(dom32 * ov, axis=-1, keepdims=True)
                ds = p * (dp - delta)
                dsb = ds.astype(BF16)
                dq_t = jnp.where(sel, jnp.dot(dsb, ka, preferred_element_type=F32) * scale, dq_t)
                dk_t = jnp.where(sel, lax.dot_general(dsb, qa, (TN, ((), ())), preferred_element_type=F32), dk_t)
                dv_t = dv_t + lax.dot_general(p.astype(BF16), dom, (TN, ((), ())), preferred_element_type=F32)
                dck_ref[hh] += jnp.sum(ds, axis=0, keepdims=True)
                ds_lo = (ds - dsb.astype(F32)).astype(BF16)
                dcq_ref[hh, i] += (lax.dot_general(ones, dsb, (NT, ((), ())), preferred_element_type=F32)
                                   + lax.dot_general(ones, ds_lo, (NT, ((), ())), preferred_element_type=F32))
            rows = pl.ds(pl.multiple_of(i * tq, tq), tq)
            dq_ref[rows, :] += dq_t
            dk_ref[...] += dk_t
            dv_ref[...] += dv_t

        @pl.when(i > j)
        def _():
            step(False)

        @pl.when(i == j)
        def _():
            step(True)

    qi = lambda hp, j, i: jnp.maximum(i, j)
    return _call(
        kern, name="fox_bwd", grid=(4, nq, nq),
        in_specs=[BS((2, tq, LANE), lambda hp, j, i: (hp, qi(hp, j, i), 0)),
                  BS((2, tq, LANE), lambda hp, j, i: (hp, j, 0)),
                  BS((tq, LANE), lambda hp, j, i: (j, V_COL + hp)),
                  BS((tq, LANE), lambda hp, j, i: (qi(hp, j, i), hp)),
                  BS((tq, LANE), lambda hp, j, i: (qi(hp, j, i), hp)),
                  BS((2, tq, LANE), lambda hp, j, i: (hp, qi(hp, j, i), 0))],
        out_specs=[BS((s, LANE), lambda hp, j, i: (0, hp)), BS((tq, LANE), lambda hp, j, i: (j, hp)),
                   BS((tq, LANE), lambda hp, j, i: (j, hp)), BS((2, 1, tq), lambda hp, j, i: (hp, 0, j)),
                   BS((2, nq, ROWSUM_M, tq), lambda hp, j, i: (hp, 0, 0, 0))],
        out_shape=[jax.ShapeDtypeStruct((s, FOX_W), F32), jax.ShapeDtypeStruct((s, FOX_W), F32),
                   jax.ShapeDtypeStruct((s, FOX_W), F32), jax.ShapeDtypeStruct((FOX_H, 1, s), F32),
                   jax.ShapeDtypeStruct((FOX_H, nq, ROWSUM_M, tq), F32)],
        compiler_params=_params(3))(qa, ka, proj, dcat, o, lse)


def _shift_down(v, d, row):
    return jnp.where(row >= d, pltpu.roll(v, d, 0), 0.0)


def _shift_up(v, d, row, n):
    return jnp.where(row < n - d, pltpu.roll(v, n - d, 0), 0.0)


def _sconv_fwd(proj, convw):
    s = proj.shape[0]

    def kern(b_ref, c_ref, u_ref, w_ref, y_ref):
        row = lax.broadcasted_iota(jnp.int32, (s, LANE), 0)
        z = c_ref[...] * u_ref[...]
        conv = w_ref[2:3, :] * z + w_ref[1:2, :] * _shift_down(z, 1, row) + w_ref[0:1, :] * _shift_down(z, 2, row)
        y_ref[...] = (b_ref[...] * conv).astype(BF16)

    col = lambda base: BS((s, LANE), lambda cb: (0, base + cb))
    return _call(kern, name="sconv_fwd", grid=(SC_W // LANE,),
                          in_specs=[col(B_COL), col(C_COL), col(U_COL), BS((SC_K, LANE), lambda cb: (0, cb))],
                          out_specs=BS((s, LANE), lambda cb: (0, cb)),
                          out_shape=jax.ShapeDtypeStruct((s, SC_W), BF16), compiler_params=_params(1))(proj, proj, proj, convw)


def _sconv_bwd(proj, convw, dcat):
    s = proj.shape[0]

    def kern(b_ref, c_ref, u_ref, w_ref, dy_ref, db_ref, dc_ref, du_ref, dw_ref):
        row = lax.broadcasted_iota(jnp.int32, (s, LANE), 0)
        cv, uv, dyv = c_ref[...], u_ref[...], dy_ref[...]
        z = cv * uv
        z1 = _shift_down(z, 1, row)
        z2 = _shift_down(z, 2, row)
        conv = w_ref[2:3, :] * z + w_ref[1:2, :] * z1 + w_ref[0:1, :] * z2
        db_ref[...] = dyv * conv
        dcv = dyv * b_ref[...]
        dz = w_ref[2:3, :] * dcv + w_ref[1:2, :] * _shift_up(dcv, 1, row, s) + w_ref[0:1, :] * _shift_up(dcv, 2, row, s)
        dc_ref[...] = dz * uv
        du_ref[...] = dz * cv
        dw_ref[0:1, :] = jnp.sum(dcv * z2, axis=0, keepdims=True)
        dw_ref[1:2, :] = jnp.sum(dcv * z1, axis=0, keepdims=True)
        dw_ref[2:3, :] = jnp.sum(dcv * z, axis=0, keepdims=True)

    col = lambda base: BS((s, LANE), lambda cb: (0, base + cb))
    out = BS((s, LANE), lambda cb: (0, cb))
    wspec = BS((SC_K, LANE), lambda cb: (0, cb))
    act = jax.ShapeDtypeStruct((s, SC_W), F32)
    return _call(kern, name="sconv_bwd", grid=(SC_W // LANE,),
                          in_specs=[col(B_COL), col(C_COL), col(U_COL), wspec, col(FOX_W // LANE)],
                          out_specs=[out, out, out, wspec],
                          out_shape=[act, act, act, jax.ShapeDtypeStruct((SC_K, SC_W), F32)],
                          compiler_params=_params(1))(proj, proj, proj, convw, dcat)


def _fox_layer_fwd(x, gpre, gpost, wall, bfb, convw, wout, after=None):
    s = x.shape[0]
    h = _norm_fwd(x, gpre, after)
    proj = _mm_nt_cols("fox_proj", h, wall, AB_PAD // 5)
    frow = proj[:, 3 * FOX_W + 3 * SC_W:3 * FOX_W + 3 * SC_W + FOX_H].T
    cumr = _fox_cum(frow, bfb)
    qa, ka = _fox_prep(proj, jnp.broadcast_to(cumr[:, :, None], (FOX_H, s, LANE)))
    o, lse = _fox_fwd_call(proj, qa, ka)
    yb = _sconv_fwd(proj, convw)
    cat = jnp.concatenate([o.astype(BF16), yb], axis=1)
    y = _mm_nn("fox_out", cat, wout)
    return _norm_res(x, y, gpost), (x, h, proj, frow, qa, ka, o, lse, cat, y)


def _fox_layer_bwd(dxo, saved, gpre, gpost, wall, bfb, convw, wout, after=None):
    x, h, proj, frow, qa, ka, o, lse, cat, y = saved
    s = x.shape[0]
    dy, dgpost = _norm_bwd(y, dxo, gpost, None, BF16, after)
    dcat = _mm_nt("fox_dcat", dy, wout)
    dwout = _mm_tn("fox_dwout", cat, dy)
    db, dc, du, dconvw = _sconv_bwd(proj, convw, dcat)
    dq, dk, dv, dck, dcq = _fox_bwd_call(proj, o, lse, dcat, qa, ka)
    dfrow, dbf = _fox_dlogf(dcq[:, :, 0, :].reshape(FOX_H, s), dck.reshape(FOX_H, s), frow, bfb)
    dfcol = jnp.pad(dfrow.T, ((0, 0), (0, LANE - FOX_H)))
    dproj = jnp.concatenate([dq, dk, dv, db, dc, du, dfcol], axis=1).astype(BF16)
    dwall = _mm_tn_rows("fox_dwall", dproj, h, AB_PAD // 5)
    dh = _mm_nn("fox_dh", dproj, wall, vmem=VMEM_BIG)
    dx, dgpre = _norm_bwd(x, dh, gpre, dxo, F32)
    return dx, dgpre, dgpost, dwall, dbf[:, 0], dconvw, dwout


def _ab_pack(wt):
    nf = 3 * FOX_W
    return jnp.concatenate([wt[:nf], wt[nf + FOX_H:], wt[nf:nf + FOX_H],
                            jnp.zeros((AB_PAD - AB_IN, wt.shape[1]), wt.dtype)], axis=0)


def _ab_unpack(wt):
    nf = 3 * FOX_W
    nbcu = 3 * SC_W
    return jnp.concatenate([wt[:nf], wt[nf + nbcu:nf + nbcu + FOX_H], wt[nf:nf + nbcu]], axis=0)


NCH = DM // LANE
CH_PER_BLK = LRU_BW // LANE


def _chunk_spec(s, lead=0):
    return BS((None, s, LANE), lambda ch: (lead + ch // CH_PER_BLK, 0, ch % CH_PER_BLK))


def _vec_chunk(rows):
    return BS((rows, LANE), lambda ch: (0, ch))


def _neg_expm1(x):
    series = -x * (1.0 + x * (1 / 2) * (1.0 + x * (1 / 3) * (1.0 + x * (1 / 4) * (1.0 + x * (1 / 5) * (
        1.0 + x * (1 / 6) * (1.0 + x * (1 / 7)))))))
    return jnp.where(x > -0.25, series, 1.0 - jnp.exp(x))


def _softplus(z):
    return jnp.maximum(z, 0.0) + jnp.log1p(jnp.exp(-jnp.abs(z)))


GELU_C = math.sqrt(2.0 / math.pi)
GELU_A = 0.044715


def _gelu(x):
    return 0.5 * x * (1.0 + jnp.tanh(GELU_C * (x + GELU_A * x * x * x)))


def _gelu_grad(x):
    t = jnp.tanh(GELU_C * (x + GELU_A * x * x * x))
    return 0.5 * (1.0 + t) + 0.5 * x * (1.0 - t * t) * GELU_C * (1.0 + 3.0 * GELU_A * x * x)


def _lru_conv_fwd(gu, convw, convb):
    s = gu.shape[1]

    def kern(x_ref, w_ref, b_ref, u_ref):
        row = lax.broadcasted_iota(jnp.int32, (s, LANE), 0)
        xv = x_ref[...]
        u_ref[...] = (b_ref[...] + w_ref[3:4, :] * xv + w_ref[2:3, :] * _shift_down(xv, 1, row)
                      + w_ref[1:2, :] * _shift_down(xv, 2, row) + w_ref[0:1, :] * _shift_down(xv, 3, row))

    return _call(kern, name="lru_conv_fwd", grid=(NCH,),
                          in_specs=[_chunk_spec(s, LRU_NB), _vec_chunk(RG_K), _vec_chunk(1)], out_specs=_chunk_spec(s),
                          out_shape=jax.ShapeDtypeStruct((LRU_NB, s, LRU_BW), F32), compiler_params=_params(1))(gu, convw, convb)


def _lru_conv_bwd(dud, dug, gu, convw):
    s = gu.shape[1]

    def kern(d1_ref, d2_ref, x_ref, w_ref, dx_ref, dw_ref, db_ref):
        row = lax.broadcasted_iota(jnp.int32, (s, LANE), 0)
        du = d1_ref[...] + d2_ref[...]
        xv = x_ref[...]
        dx_ref[...] = (w_ref[3:4, :] * du + w_ref[2:3, :] * _shift_up(du, 1, row, s) + w_ref[1:2, :] * _shift_up(du, 2, row, s)
                       + w_ref[0:1, :] * _shift_up(du, 3, row, s)).astype(BF16)
        dw_ref[3:4, :] = jnp.sum(du * xv, axis=0, keepdims=True)
        for k in range(1, RG_K):
            dw_ref[3 - k:4 - k, :] = jnp.sum(du * _shift_down(xv, k, row), axis=0, keepdims=True)
        db_ref[...] = jnp.sum(du, axis=0, keepdims=True)

    return _call(kern, name="lru_conv_bwd", grid=(NCH,),
                          in_specs=[_chunk_spec(s), _chunk_spec(s), _chunk_spec(s, LRU_NB), _vec_chunk(RG_K)],
                          out_specs=[_chunk_spec(s), _vec_chunk(RG_K), _vec_chunk(1)],
                          out_shape=[jax.ShapeDtypeStruct((LRU_NB, s, LRU_BW), BF16),
                                     jax.ShapeDtypeStruct((RG_K, DM), F32), jax.ShapeDtypeStruct((1, DM), F32)],
                          compiler_params=_params(1))(dud, dug, gu, convw)


def _lru_gates(z_ref, bai_ref, lam_ref, uv):
    r = jax.nn.sigmoid(z_ref[0] + bai_ref[0:1, :])
    ig = jax.nn.sigmoid(z_ref[1] + bai_ref[1:2, :])
    sp = _softplus(-lam_ref[...])
    la = -RG_C * r * sp
    a = jnp.exp(la)
    sq = jnp.sqrt(_neg_expm1(2.0 * la))
    return r, ig, sp, a, sq


def _scan_steps(n):
    d, out = 1, []
    while d < n:
        out.append(d)
        d *= 2
    return out


def _lru_scan_fwd(z, bai, lam, u, gu):
    s = u.shape[1]
    zspec = BS((2, None, s, LANE), lambda ch: (0, ch // CH_PER_BLK, 0, ch % CH_PER_BLK))

    def kern(z_ref, bai_ref, lam_ref, u_ref, g_ref, hs_ref, y_ref):
        row = lax.broadcasted_iota(jnp.int32, (s, LANE), 0)
        uv = u_ref[...]
        _, ig, _, a, sq = _lru_gates(z_ref, bai_ref, lam_ref, uv)
        b = sq * (ig * uv)
        for d in _scan_steps(s):
            a_sh = jnp.where(row >= d, pltpu.roll(a, d, 0), 1.0)
            b = a * _shift_down(b, d, row) + b
            a = a * a_sh
        hs_ref[...] = b
        y_ref[...] = (_gelu(g_ref[...]) * b).astype(BF16)

    return _call(kern, name="lru_scan_fwd", grid=(NCH,),
                          in_specs=[zspec, _vec_chunk(2), _vec_chunk(1), _chunk_spec(s), _chunk_spec(s)],
                          out_specs=[_chunk_spec(s), BS((s, LANE), lambda ch: (0, ch))],
                          out_shape=[jax.ShapeDtypeStruct((LRU_NB, s, LRU_BW), F32), jax.ShapeDtypeStruct((s, DM), BF16)],
                          compiler_params=_params(1, VMEM_BIG))(z, bai, lam, u, gu)


def _lru_scan_bwd(dyp, z, bai, lam, u, gu, hs):
    s = u.shape[1]
    zspec = BS((2, None, s, LANE), lambda ch: (0, ch // CH_PER_BLK, 0, ch % CH_PER_BLK))

    def kern(dy_ref, z_ref, bai_ref, lam_ref, u_ref, g_ref, hs_ref, dg_ref, dz_ref, du_ref, dbai_ref, dlam_ref):
        row = lax.broadcasted_iota(jnp.int32, (s, LANE), 0)
        uv, gv, hv, dyv = u_ref[...], g_ref[...], hs_ref[...], dy_ref[...]
        r, ig, sp, a, sq = _lru_gates(z_ref, bai_ref, lam_ref, uv)
        dg_ref[...] = (dyv * hv * _gelu_grad(gv)).astype(BF16)
        g = dyv * _gelu(gv)
        an = _shift_up(a, 1, row, s)
        for d in _scan_steps(s):
            an_sh = jnp.where(row < s - d, pltpu.roll(an, s - d, 0), 1.0)
            g = an * _shift_up(g, d, row, s) + g
            an = an * an_sh
        da = g * _shift_down(hv, 1, row)
        dsq = g * (ig * uv)
        di = g * sq * uv
        du_ref[...] = g * sq * ig
        dla = da * a - dsq * (a * a / sq)
        dzr = dla * (-RG_C * sp) * r * (1.0 - r)
        dzi = di * ig * (1.0 - ig)
        dz_ref[0] = dzr.astype(BF16)
        dz_ref[1] = dzi.astype(BF16)
        dbai_ref[0:1, :] = jnp.sum(dzr, axis=0, keepdims=True)
        dbai_ref[1:2, :] = jnp.sum(dzi, axis=0, keepdims=True)
        dlam_ref[...] = jnp.sum(dla * r, axis=0, keepdims=True) * (RG_C * jax.nn.sigmoid(-lam_ref[...]))

    return _call(
        kern, name="lru_scan_bwd", grid=(NCH,),
        in_specs=[BS((s, LANE), lambda ch: (0, ch)), zspec, _vec_chunk(2), _vec_chunk(1), _chunk_spec(s), _chunk_spec(s),
                  _chunk_spec(s)],
        out_specs=[_chunk_spec(s), zspec, _chunk_spec(s), _vec_chunk(2), _vec_chunk(1)],
        out_shape=[jax.ShapeDtypeStruct((LRU_NB, s, LRU_BW), BF16), jax.ShapeDtypeStruct((2, LRU_NB, s, LRU_BW), BF16),
                   jax.ShapeDtypeStruct((LRU_NB, s, LRU_BW), F32), jax.ShapeDtypeStruct((2, DM), F32),
                   jax.ShapeDtypeStruct((1, DM), F32)],
        compiler_params=_params(1, VMEM_BIG))(dyp, z, bai, lam, u, gu, hs)


def _lru_layer_fwd(x, gpre, gpost, win, convw, convb, wai, bai, lam, wout, after=None):
    s = x.shape[0]
    tm = _tile(s, 512)
    h = _norm_fwd(x, gpre, after)
    gu = _bmm_nn("lru_in", h, win)
    u = _lru_conv_fwd(gu, convw, convb)
    z = _mm("lru_gate", u, wai, grid=(2, LRU_NB, s // tm, 1),
            a_spec=BS((None, tm, LRU_BW), lambda k, n, i, r: (n, i, 0)),
            b_spec=BS((None, None, LRU_BW, LRU_BW), lambda k, n, i, r: (k, n, 0, 0)),
            o_spec=BS((None, None, tm, LRU_BW), lambda k, n, i, r: (k, n, i, 0)),
            out_shape=(2, LRU_NB, s, LRU_BW), dn=NN)
    hs, yp = _lru_scan_fwd(z, bai, lam, u, gu)
    y = _mm_nn("lru_out", yp, wout)
    return _norm_res(x, y, gpost), (x, h, gu, u, z, hs, yp, y)


def _lru_layer_bwd(dxo, saved, gpre, gpost, win, convw, convb, wai, bai, lam, wout, after=None):
    x, h, gu, u, z, hs, yp, y = saved
    s = x.shape[0]
    tm = _tile(s, 512)
    dy, dgpost = _norm_bwd(y, dxo, gpost, None, BF16, after)
    dyp = _mm_nt("lru_dyp", dy, wout)
    dwout = _mm_tn("lru_dwout", yp, dy)
    dgate, dz, dud, dbai, dlam = _lru_scan_bwd(dyp, z, bai, lam, u, gu, hs)
    dwai = _mm("lru_dwai", u, dz, grid=(2, LRU_NB, s // tm),
               a_spec=BS((None, tm, LRU_BW), lambda k, n, r: (n, r, 0)),
               b_spec=BS((None, None, tm, LRU_BW), lambda k, n, r: (k, n, r, 0)),
               o_spec=BS((None, None, LRU_BW, LRU_BW), lambda k, n, r: (k, n, 0, 0)),
               out_shape=(2, LRU_NB, LRU_BW, LRU_BW), dn=TN)
    dug = _mm("lru_dug", dz, wai, grid=(LRU_NB, s // tm, 2),
              a_spec=BS((None, None, tm, LRU_BW), lambda n, i, k: (k, n, i, 0)),
              b_spec=BS((None, None, LRU_BW, LRU_BW), lambda n, i, k: (k, n, 0, 0)),
              o_spec=BS((None, tm, LRU_BW), lambda n, i, k: (n, i, 0)),
              out_shape=(LRU_NB, s, LRU_BW), dn=NT)
    duraw, dconvw, dconvb = _lru_conv_bwd(dud, dug, gu, convw)
    dgu = jnp.concatenate([dgate, duraw], axis=0)
    dwin = _bmm_tn("lru_dwin", h, dgu)
    dh = _bmm_nt_sum("lru_dh", dgu, win)
    dx, dgpre = _norm_bwd(x, dh, gpre, dxo, F32)
    return dx, dgpre, dgpost, dwin, dconvw, dconvb, dwai, dbai, dlam, dwout


CHIP_FLIPS = ((1, 0), (0, 1), (1, 1))


def _place():
    return lax.axis_index("x"), lax.axis_index("y"), lax.axis_index("c")


def _flip(v, f):
    return 1 - v if f else v


def _comm_params():
    return pltpu.CompilerParams(vmem_limit_bytes=VMEM_LIMIT)


def _small_gather(v):
    def body(v_ref, o_ref, send_sems, recv_sems, local_sem):
        x, y, c = _place()
        mine = 4 * x + 2 * y + c
        local = pltpu.make_async_copy(v_ref, o_ref.at[mine], local_sem)
        local.start()
        sends = []
        for k in range(1, NDEV):
            fx, fy, fc = (k >> 2) & 1, (k >> 1) & 1, k & 1
            sends.append(pltpu.make_async_remote_copy(
                src_ref=v_ref, dst_ref=o_ref.at[mine], send_sem=send_sems.at[k - 1], recv_sem=recv_sems.at[k - 1],
                device_id=(_flip(x, fx), _flip(y, fy), _flip(c, fc)), device_id_type=MESH))
        for cp in sends:
            cp.start()
        for k in range(1, NDEV):
            fx, fy, fc = (k >> 2) & 1, (k >> 1) & 1, k & 1
            src = 4 * _flip(x, fx) + 2 * _flip(y, fy) + _flip(c, fc)
            pltpu.make_async_remote_copy(src_ref=v_ref, dst_ref=o_ref.at[src], send_sem=send_sems.at[k - 1],
                                         recv_sem=recv_sems.at[k - 1], device_id=(x, y, c), device_id_type=MESH).wait_recv()
        for cp in sends:
            cp.wait_send()
        local.wait()

    return pl.pallas_call(body, name="small_gather", in_specs=[ANY], out_specs=ANY,
                          out_shape=jax.ShapeDtypeStruct((NDEV,) + v.shape, v.dtype),
                          scratch_shapes=[pltpu.SemaphoreType.DMA((NDEV - 1,)), pltpu.SemaphoreType.DMA((NDEV - 1,)),
                                          pltpu.SemaphoreType.DMA],
                          compiler_params=_comm_params())(v)


REL_CHIPS = ((0, 0),) + CHIP_FLIPS


def _rs_d2d(g5s, after=None):
    n = len(g5s)
    extra = () if after is None else (after,)

    def body(*refs):
        ins, gots = refs[:n], refs[n + len(extra):2 * n + len(extra)]
        send_sems, recv_sems = refs[2 * n + len(extra):]
        x, y, c = _place()
        copies = []
        for t in range(n):
            for f, (fx, fy) in enumerate(REL_CHIPS):
                copies.append(pltpu.make_async_remote_copy(
                    src_ref=ins[t].at[_flip(x, fx), _flip(y, fy), 1 - c], dst_ref=gots[t].at[f],
                    send_sem=send_sems.at[4 * t + f], recv_sem=recv_sems.at[4 * t + f], device_id=(x, y, 1 - c),
                    device_id_type=MESH))
        for cp in copies:
            cp.start()
        for cp in copies:
            cp.wait()

    out = [jax.ShapeDtypeStruct((4,) + g.shape[3:], F32) for g in g5s]
    return pl.pallas_call(body, name="rs_d2d", in_specs=[ANY] * (n + len(extra)), out_specs=[ANY] * n, out_shape=out,
                          scratch_shapes=[pltpu.SemaphoreType.DMA((4 * n,)), pltpu.SemaphoreType.DMA((4 * n,))],
                          compiler_params=_comm_params())(*g5s, *extra)


HBM = pl.BlockSpec(memory_space=pltpu.HBM)
SEM = pl.BlockSpec(memory_space=pltpu.SEMAPHORE)
EFFECT = pltpu.SideEffectType.DATAFLOW_SIDE_EFFECTING


def _in_hbm(a):
    return pltpu.with_memory_space_constraint(a, pltpu.HBM)


def _rs_ici_copies(ins, lands, send_sems, recv_sems):
    x, y, c = _place()
    return [pltpu.make_async_remote_copy(
        src_ref=ins[t].at[f], dst_ref=lands[t].at[f], send_sem=send_sems.at[3 * t + f], recv_sem=recv_sems.at[3 * t + f],
        device_id=(_flip(x, fx), _flip(y, fy), c), device_id_type=MESH)
        for t in range(len(ins)) for f, (fx, fy) in enumerate(CHIP_FLIPS)]


def _rs_ici_start(parts, name):
    n = len(parts)

    def body(*refs):
        ins, lands = refs[:n], refs[n:2 * n]
        send_sems, recv_sems = refs[2 * n], refs[2 * n + 1]
        token = refs[-1]
        for cp in _rs_ici_copies(ins, lands, send_sems, recv_sems):
            cp.start()
        token[...] = jnp.zeros(token.shape, token.dtype)

    thru = [pltpu.HBM(p.shape, p.dtype) for p in parts]
    res = pl.pallas_call(
        body, name=name, in_specs=[HBM] * (2 * n),
        out_shape=(pltpu.SemaphoreType.DMA((3 * n,)), pltpu.SemaphoreType.DMA((3 * n,)), *thru, *thru,
                   jax.ShapeDtypeStruct((8, LANE), F32)),
        out_specs=(SEM, SEM, *([HBM] * (2 * n)), pl.BlockSpec(memory_space=pltpu.VMEM)),
        input_output_aliases={i: 2 + i for i in range(2 * n)},
        compiler_params=pltpu.CompilerParams(has_side_effects=EFFECT),
    )(*[_in_hbm(p) for p in parts], *[_in_hbm(lax.empty(p.shape, p.dtype)) for p in parts])
    return res[:-1], res[-1]


def _rs_ici_wait(state, after, name):
    n = (len(state) - 2) // 2

    def body(*refs):
        send_sems, recv_sems = refs[0], refs[1]
        ins, lands = refs[2:2 + n], refs[2 + n:2 + 2 * n]
        for cp in _rs_ici_copies(ins, lands, send_sems, recv_sems):
            cp.wait_send()
            cp.wait_recv()

    thru = [pltpu.HBM(s.shape, s.dtype) for s in state[2:]]
    res = pl.pallas_call(
        body, name=name, in_specs=[SEM, SEM] + [HBM] * (2 * n) + [ANY], out_shape=tuple(thru),
        out_specs=tuple([HBM] * (2 * n)), input_output_aliases={2 + i: i for i in range(2 * n)},
        compiler_params=pltpu.CompilerParams(has_side_effects=EFFECT),
    )(*state, after)
    return list(res[n:])


def _ag_copies(shards, lands, send_sems, recv_sems):
    x, y, c = _place()
    mine = 4 * x + 2 * y + c
    peers = [(x, y, 1 - c)] + [(_flip(x, fx), _flip(y, fy), c) for fx, fy in CHIP_FLIPS]
    return [pltpu.make_async_remote_copy(
        src_ref=shards[t], dst_ref=lands[t].at[mine], send_sem=send_sems.at[4 * t + k], recv_sem=recv_sems.at[4 * t + k],
        device_id=peer, device_id_type=MESH) for t in range(len(shards)) for k, peer in enumerate(peers)]


def _ag_start(shards, after, name):
    n = len(shards)

    def body(*refs):
        ins, lands = refs[:n], refs[n:2 * n]
        send_sems, recv_sems = refs[2 * n + 1], refs[2 * n + 2]
        token = refs[-1]
        for cp in _ag_copies(ins, lands, send_sems, recv_sems):
            cp.start()
        token[...] = jnp.zeros(token.shape, token.dtype)

    thru = [pltpu.HBM(s.shape, s.dtype) for s in shards]
    land = [pltpu.HBM((NDEV,) + s.shape, s.dtype) for s in shards]
    res = pl.pallas_call(
        body, name=name, in_specs=[HBM] * (2 * n) + [ANY],
        out_shape=(pltpu.SemaphoreType.DMA((4 * n,)), pltpu.SemaphoreType.DMA((4 * n,)), *thru, *land,
                   jax.ShapeDtypeStruct((8, LANE), F32)),
        out_specs=(SEM, SEM, *([HBM] * (2 * n)), pl.BlockSpec(memory_space=pltpu.VMEM)),
        input_output_aliases={i: 2 + i for i in range(2 * n)},
        compiler_params=pltpu.CompilerParams(has_side_effects=EFFECT),
    )(*[_in_hbm(s) for s in shards], *[_in_hbm(lax.empty((NDEV,) + s.shape, s.dtype)) for s in shards], after)
    return res[:-1], res[-1]


def _ag_wait(state, after, name):
    n = (len(state) - 2) // 2

    def body(*refs):
        send_sems, recv_sems = refs[0], refs[1]
        ins, lands = refs[2:2 + n], refs[2 + n:2 + 2 * n]
        for cp in _ag_copies(ins, lands, send_sems, recv_sems):
            cp.wait_send()
            cp.wait_recv()

    thru = [pltpu.HBM(s.shape, s.dtype) for s in state[2:]]
    res = pl.pallas_call(
        body, name=name, in_specs=[SEM, SEM] + [HBM] * (2 * n) + [ANY], out_shape=tuple(thru),
        out_specs=tuple([HBM] * (2 * n)), input_output_aliases={2 + i: i for i in range(2 * n)},
        compiler_params=pltpu.CompilerParams(has_side_effects=EFFECT),
    )(*state, after)
    return list(res[:n]), list(res[n:])


def _ag_finish(shards, lands):
    n = len(shards)

    def body(*refs):
        ins, outs, stage = refs[:n], refs[2 * n:3 * n], refs[3 * n:4 * n]
        send_sems, recv_sems, local_sems = refs[4 * n:]
        x, y, c = _place()
        chips = [(_flip(x, fx), _flip(y, fy)) for fx, fy in CHIP_FLIPS]

        def passing(t, j, core, to):
            blk = outs[t].at[4 * chips[j][0] + 2 * chips[j][1] + core]
            return pltpu.make_async_remote_copy(src_ref=blk, dst_ref=blk, send_sem=send_sems.at[3 * t + j],
                                                recv_sem=recv_sems.at[3 * t + j], device_id=to, device_id_type=MESH)

        sends = [passing(t, j, c, (x, y, 1 - c)) for t in range(n) for j in range(3)]
        for cp in sends:
            cp.start()
        load = [pltpu.make_async_copy(ins[t], stage[t], local_sems.at[t]) for t in range(n)]
        mine = [pltpu.make_async_copy(stage[t], outs[t].at[4 * x + 2 * y + c], local_sems.at[t]) for t in range(n)]
        for cp in load:
            cp.start()
        for t in range(n):
            load[t].wait()
            mine[t].start()
        for t in range(n):
            for j in range(3):
                passing(t, j, 1 - c, (x, y, c)).wait_recv()
        for cp in sends:
            cp.wait_send()
        for cp in mine:
            cp.wait()

    return pl.pallas_call(
        body, name="ag_finish", in_specs=[ANY] * (2 * n), out_specs=[ANY] * n,
        out_shape=[jax.ShapeDtypeStruct(l.shape, l.dtype) for l in lands],
        input_output_aliases={n + i: i for i in range(n)},
        scratch_shapes=[pltpu.VMEM(s.shape, s.dtype) for s in shards]
        + [pltpu.SemaphoreType.DMA((3 * n,)), pltpu.SemaphoreType.DMA((3 * n,)), pltpu.SemaphoreType.DMA((n,))],
        compiler_params=_comm_params())(*shards, *lands)


def _row_tile(rows, largest=256):
    for t in (1024, 512, 256, 128, 64, 32, 16, 8):
        if t > largest:
            continue
        if rows % t == 0:
            return t
    return rows


def _rs_chip_sum(pos, g5, got):
    a, b = g5.shape[3:]
    ta = _row_tile(a, 1024)

    def kern(pos_ref, o_ref, g_ref, p_ref):
        p_ref[...] = (o_ref[...] + g_ref[...]).astype(BF16)

    def mine(f, i, pos_ref):
        return (pos_ref[0] ^ ((f + 1) & 1), pos_ref[1] ^ ((f + 1) >> 1), pos_ref[2], i, 0)

    spec = pltpu.PrefetchScalarGridSpec(
        num_scalar_prefetch=1, grid=(3, a // ta),
        in_specs=[BS((None, None, None, ta, b), mine), BS((None, ta, b), lambda f, i, pos_ref: (f + 1, i, 0))],
        out_specs=BS((None, ta, b), lambda f, i, pos_ref: (f, i, 0)))
    return _call(kern, name="rs_chip_sum", grid_spec=spec, out_shape=jax.ShapeDtypeStruct((3, a, b), BF16),
                          compiler_params=_params(2))(pos, g5, got)


def _rs_final_sum(pos, g5, got, recv):
    a, b = g5.shape[3:]
    ta = _row_tile(a, 1024)

    def kern(pos_ref, o_ref, g_ref, r_ref, s_ref):
        acc = o_ref[...] + g_ref[...]
        for f in range(3):
            acc = acc + r_ref[f].astype(F32)
        s_ref[...] = acc

    spec = pltpu.PrefetchScalarGridSpec(
        num_scalar_prefetch=1, grid=(a // ta,),
        in_specs=[BS((None, None, None, ta, b), lambda i, pos_ref: (pos_ref[0], pos_ref[1], pos_ref[2], i, 0)),
                  BS((None, ta, b), lambda i, pos_ref: (0, i, 0)), BS((3, ta, b), lambda i, pos_ref: (0, i, 0))],
        out_specs=BS((ta, b), lambda i, pos_ref: (i, 0)))
    return _call(kern, name="rs_final_sum", grid_spec=spec, out_shape=jax.ShapeDtypeStruct((a, b), F32),
                          compiler_params=_params(1))(pos, g5, got, recv)


def _rs_d2d_copies(ins, lands, send_sems, recv_sems):
    x, y, c = _place()
    return [pltpu.make_async_remote_copy(
        src_ref=ins[t].at[_flip(x, fx), _flip(y, fy), 1 - c], dst_ref=lands[t].at[f], send_sem=send_sems.at[4 * t + f],
        recv_sem=recv_sems.at[4 * t + f], device_id=(x, y, 1 - c), device_id_type=MESH)
        for t in range(len(ins)) for f, (fx, fy) in enumerate(REL_CHIPS)]


def _rs_d2d_start(g5s, name):
    n = len(g5s)

    def body(*refs):
        ins, lands = refs[:n], refs[n:2 * n]
        for cp in _rs_d2d_copies(ins, lands, refs[2 * n], refs[2 * n + 1]):
            cp.start()
        refs[-1][...] = jnp.zeros(refs[-1].shape, F32)

    thru = [pltpu.HBM(g.shape, g.dtype) for g in g5s]
    land = [pltpu.HBM((4,) + g.shape[3:], F32) for g in g5s]
    res = pl.pallas_call(
        body, name=name, in_specs=[HBM] * (2 * n),
        out_shape=(pltpu.SemaphoreType.DMA((4 * n,)), pltpu.SemaphoreType.DMA((4 * n,)), *thru, *land,
                   jax.ShapeDtypeStruct((8, LANE), F32)),
        out_specs=(SEM, SEM, *([HBM] * (2 * n)), pl.BlockSpec(memory_space=pltpu.VMEM)),
        input_output_aliases={i: 2 + i for i in range(2 * n)},
        compiler_params=pltpu.CompilerParams(has_side_effects=EFFECT),
    )(*[_in_hbm(g) for g in g5s], *[_in_hbm(lax.empty((4,) + g.shape[3:], F32)) for g in g5s])
    return res[:-1], res[-1]


def _rs_d2d_wait(state, after, name):
    n = (len(state) - 2) // 2

    def body(*refs):
        ins, lands = refs[2:2 + n], refs[2 + n:2 + 2 * n]
        for cp in _rs_d2d_copies(ins, lands, refs[0], refs[1]):
            cp.wait_send()
            cp.wait_recv()

    thru = [pltpu.HBM(s.shape, s.dtype) for s in state[2:]]
    res = pl.pallas_call(
        body, name=name, in_specs=[SEM, SEM] + [HBM] * (2 * n) + [ANY], out_shape=tuple(thru),
        out_specs=tuple([HBM] * (2 * n)), input_output_aliases={2 + i: i for i in range(2 * n)},
        compiler_params=pltpu.CompilerParams(has_side_effects=EFFECT),
    )(*state, after)
    return list(res[:n]), list(res[n:])


def _as_g5(grads):
    return [g.reshape((2, 2, 2) + g.shape[1:]) for g in grads]


def _rs_mid(g5s, gots, pos, tag):
    parts = [_rs_chip_sum(pos, g, got) for g, got in zip(g5s, gots)]
    state, token = _rs_ici_start(parts, "rs_ici_start_" + tag)
    return (g5s, gots, state, tag), token


def _rs_begin(grads, pos, tag, after=None):
    g5s = _as_g5(grads)
    return _rs_mid(g5s, _rs_d2d(g5s, after), pos, tag)


def _rs_end(pending, after, pos):
    g5s, gots, state, tag = pending
    recvs = _rs_ici_wait(state, after, "rs_ici_wait_" + tag)
    return [_rs_final_sum(pos, g, got, r) for g, got, r in zip(g5s, gots, recvs)]


def _sum_devices(v):
    _, r, _ = v.shape

    def kern(v_ref, o_ref):
        acc = v_ref[0]
        for d in range(1, NDEV):
            acc = acc + v_ref[d]
        o_ref[...] = acc

    return _call(kern, name="sum_devices", out_shape=jax.ShapeDtypeStruct((r, LANE), F32),
                          compiler_params=_comm_params())(v)


def _loss_head(xf, target):
    s = xf.shape[0]
    tm = _tile(s, 512)

    def kern(x_ref, t_ref, dx_ref, l_ref):
        err = x_ref[...] - t_ref[...]
        dx_ref[...] = err * (1.0 / DM)
        part = jnp.broadcast_to(0.5 * jnp.sum(jnp.mean(err * err, axis=-1, keepdims=True), axis=0, keepdims=True), (8, LANE))

        @pl.when(pl.program_id(0) == 0)
        def _():
            l_ref[...] = part

        @pl.when(pl.program_id(0) > 0)
        def _():
            l_ref[...] += part

    row = BS((tm, DM), lambda i: (i, 0))
    return _call(kern, name="loss_head", grid=(s // tm,), in_specs=[row, row],
                          out_specs=[row, BS((8, LANE), lambda i: (0, 0))],
                          out_shape=[jax.ShapeDtypeStruct((s, DM), F32), jax.ShapeDtypeStruct((8, LANE), F32)],
                          compiler_params=_params(1))(xf, target)


def _adamw(w, g, m, v, after=None):
    rows, cols = w.shape
    tr = _row_tile(rows)
    extra = () if after is None else (after,)

    def kern(w_ref, g_ref, m_ref, v_ref, *rest):
        d_ref, nm_ref, nv_ref = rest[-3:]
        gv = g_ref[...]
        nm = ADAM_B1 * m_ref[...] + (1.0 - ADAM_B1) * gv
        nv = ADAM_B2 * v_ref[...] + (1.0 - ADAM_B2) * (gv * gv)
        m_hat = nm / (1.0 - ADAM_B1 ** ADAM_STEP)
        v_hat = nv / (1.0 - ADAM_B2 ** ADAM_STEP)
        d_ref[...] = -ADAM_LR * (m_hat / (jnp.sqrt(v_hat) + ADAM_EPS) + ADAM_WD * w_ref[...])
        nm_ref[...] = nm
        nv_ref[...] = nv

    blk = BS((tr, cols), lambda i: (i, 0))
    shp = jax.ShapeDtypeStruct((rows, cols), F32)
    return _call(kern, name="adamw", grid=(rows // tr,), in_specs=[blk] * 4 + [ANY] * len(extra),
                          out_specs=[blk] * 3, out_shape=[shp] * 3, compiler_params=_params(1))(w, g, m, v, *extra)


def _adamw_nd(w, g, m, v, after=None):
    shape = w.shape
    two = (math.prod(shape[:-1]), shape[-1])
    return tuple(o.reshape(shape)
                 for o in _adamw(w.reshape(two), g.reshape(two), m.reshape(two), v.reshape(two), after))


def _pack_small(parts):
    flat = jnp.concatenate([p.reshape(-1) for p in parts])
    pad = (-flat.shape[0]) % (8 * LANE)
    return jnp.pad(flat, (0, pad)).reshape(-1, LANE)


def _unpack_small(packed, shapes, lead=()):
    flat = packed.reshape(lead + (-1,))
    out, off = [], 0
    for shp in shapes:
        n = math.prod(shp)
        out.append(flat[..., off:off + n].reshape(lead + tuple(shp)))
        off += n
    return out


WEIGHT_NAMES = ('g_mix_pre', 'g_mix_post', 'g_cross_pre', 'g_mem', 'g_cross_post', 'g_ffn_pre', 'g_ffn_post', 'w_xq',
                'w_xkv', 'w_xo', 'w_ffn_gu', 'w_ffn_down', 'ab_w_in', 'ab_b_f', 'ab_conv_w', 'ab_w_out', 'c_w_in',
                'c_conv_w', 'c_conv_b', 'c_w_a', 'c_b_a', 'c_w_i', 'c_b_i', 'c_lam', 'c_w_out')
BIG = ('w_xq', 'w_xkv', 'w_xo', 'w_ffn_gu', 'w_ffn_down', 'ab_w_in', 'ab_w_out', 'c_w_in', 'c_w_a', 'c_w_i', 'c_w_out')
SMALL_SHARDED = ('ab_conv_w', 'c_conv_w', 'c_conv_b', 'c_b_a', 'c_b_i', 'c_lam')
REPLICATED = ('g_mix_pre', 'g_mix_post', 'g_cross_pre', 'g_mem', 'g_cross_post', 'g_ffn_pre', 'g_ffn_post', 'ab_b_f')


def _small_full(name, gathered):
    nd = gathered.ndim
    return jnp.moveaxis(gathered, 0, nd - 2).reshape(gathered.shape[1:-1] + (NDEV * gathered.shape[-1],))


def _small_shard(full, dev):
    c = full.shape[-1] // NDEV
    return lax.dynamic_slice_in_dim(full, dev * c, c, axis=full.ndim - 1)


def kernel(x, mem, g_mix_pre, g_mix_post, g_cross_pre, g_mem, g_cross_post, g_ffn_pre, g_ffn_post, w_xq, w_xkv, w_xo, w_ffn_gu, w_ffn_down, ab_w_in, ab_b_f, ab_conv_w, ab_w_out, c_w_in, c_conv_w, c_conv_b, c_w_a, c_b_a, c_w_i, c_b_i, c_lam, c_w_out, loss_target, m_g_mix_pre, m_g_mix_post, m_g_cross_pre, m_g_mem, m_g_cross_post, m_g_ffn_pre, m_g_ffn_post, m_w_xq, m_w_xkv, m_w_xo, m_w_ffn_gu, m_w_ffn_down, m_ab_w_in, m_ab_b_f, m_ab_conv_w, m_ab_w_out, m_c_w_in, m_c_conv_w, m_c_conv_b, m_c_w_a, m_c_b_a, m_c_w_i, m_c_b_i, m_c_lam, m_c_w_out, v_g_mix_pre, v_g_mix_post, v_g_cross_pre, v_g_mem, v_g_cross_post, v_g_ffn_pre, v_g_ffn_post, v_w_xq, v_w_xkv, v_w_xo, v_w_ffn_gu, v_w_ffn_down, v_ab_w_in, v_ab_b_f, v_ab_conv_w, v_ab_w_out, v_c_w_in, v_c_conv_w, v_c_conv_b, v_c_w_a, v_c_b_a, v_c_w_i, v_c_b_i, v_c_lam, v_c_w_out):
    args = locals()
    w = {n: args[n] for n in WEIGHT_NAMES}
    mom = {n: args["m_" + n] for n in WEIGHT_NAMES}
    var = {n: args["v_" + n] for n in WEIGHT_NAMES}
    for t in (w, mom, var):
        t['w_ffn_gu'] = t['w_ffn_gu'].transpose(0, 2, 1)
    ab_t = [t['ab_w_in'].transpose(2, 0, 1) for t in (w, mom, var)]
    pos = jnp.stack([lax.axis_index("x"), lax.axis_index("y"), lax.axis_index("c")]).astype(jnp.int32)
    dev = 4 * pos[0] + 2 * pos[1] + pos[2]
    xs, mems, target = x[0], mem[0], loss_target[0]
    n_even, n_odd = (DEPTH + 1) // 2, DEPTH // 2

    small_shapes = [w[n].shape for n in SMALL_SHARDED]
    small_w_all = _small_gather(_pack_small([w[n] for n in SMALL_SHARDED]))
    gathered_small = _unpack_small(small_w_all, small_shapes, (NDEV,))
    small = {n: _small_full(n, g) for n, g in zip(SMALL_SHARDED, gathered_small)}
    ab_bfb = jnp.broadcast_to(ab_b_f[:, :, None], (n_even, FOX_H, LANE))
    c_bai = jnp.stack([small['c_b_a'].reshape(n_odd, DM), small['c_b_i'].reshape(n_odd, DM)], axis=1)
    row = lambda a, l: a[l][None]

    REST = ('w_xq', 'w_xkv', 'w_xo', 'w_ffn_gu', 'w_ffn_down')

    def mixer_names(l):
        return ('ab_w_in', 'ab_w_out') if l % 2 == 0 else ('c_w_in', 'c_w_a', 'c_w_i', 'c_w_out')

    def shards_of(l, names):
        out = []
        for n in names:
            if n == 'ab_w_in':
                s = ab_t[0][:, l // 2].astype(BF16)
            else:
                s = w[n][l if w[n].shape[0] == DEPTH else l // 2].astype(BF16)
            out.append(s.reshape(-1, s.shape[-1]))
        return out

    def mixer_weights(l, full):
        if l % 2 == 0:
            e = l // 2
            return (row(g_mix_pre, l), row(g_mix_post, l), _ab_pack(full['ab_w_in'].reshape(AB_IN, DM)), ab_bfb[e],
                    small['ab_conv_w'][e], full['ab_w_out'].reshape(DM, DM))
        o = l // 2
        gate_w = lambda g: g.reshape(NDEV, LRU_NB, LRU_BW // NDEV, LRU_BW).transpose(1, 0, 2, 3).reshape(
            LRU_NB, LRU_BW, LRU_BW)
        return (row(g_mix_pre, l), row(g_mix_post, l), full['c_w_in'], small['c_conv_w'][o], row(small['c_conv_b'], o),
                jnp.stack([gate_w(full['c_w_a']), gate_w(full['c_w_i'])]), c_bai[o], row(small['c_lam'], o),
                full['c_w_out'].reshape(DM, DM))

    def rest_weights(l, full):
        cross = (row(g_cross_pre, l), row(g_mem, l), row(g_cross_post, l), full['w_xq'].reshape(DM, DM), full['w_xkv'],
                 full['w_xo'].reshape(DM, DM))
        ffn = (row(g_ffn_pre, l), row(g_ffn_post, l), full['w_ffn_gu'], full['w_ffn_down'].reshape(D_FF, DM))
        return cross, ffn

    def gathered(state, names, after, tag):
        shards, lands = _ag_wait(state, after, "ag_wait_" + tag)
        full = _ag_finish(shards, lands)
        return dict(zip(names, full)), full[0]

    saved, weights = [], []
    h = xs
    names_of = lambda l: mixer_names(l) + REST
    states = {}
    st_m, _ = _ag_start(shards_of(0, mixer_names(0)), small_w_all, "ag_start_0m")
    st_r, _ = _ag_start(shards_of(0, REST), st_m[2], "ag_start_0r")
    states[1], token = _ag_start(shards_of(1, names_of(1)), st_r[2], "ag_start_1")
    full_m, _ = gathered(st_m, mixer_names(0), xs, "0m")
    for l in range(DEPTH):
        if l > 0:
            full, done = gathered(states[l], names_of(l), h, str(l))
            full_m = full_r = full
            token = None
            if l + 2 < DEPTH:
                states[l + 2], token = _ag_start(shards_of(l + 2, names_of(l + 2)), done, "ag_start_%d" % (l + 2))
        mixer = mixer_weights(l, full_m)
        h, s_mix = (_fox_layer_fwd if l % 2 == 0 else _lru_layer_fwd)(h, *mixer, after=token)
        token = None
        if l == 0:
            full_r, done = gathered(st_r, REST, h, "0r")
            states[2], token = _ag_start(shards_of(2, names_of(2)), done, "ag_start_2")
        cross, ffn = rest_weights(l, full_r)
        h, s_cross = _cross_fwd(h, mems, *cross, after=token)
        h, s_ffn = _ffn_fwd(h, *ffn)
        saved.append((s_mix, s_cross, s_ffn))
        weights.append((mixer, cross, ffn))
    mixer_args = lambda l: weights[l][0]
    cross_args = lambda l: weights[l][1]
    ffn_args = lambda l: weights[l][2]
    dx, loss_rep = _loss_head(h, target)
    loss = lax.psum(loss_rep[0, 0], ("x", "y", "c"))

    grads = {n: [None] * w[n].shape[0] for n in BIG}
    partial = {n: [None] * w[n].shape[0] for n in REPLICATED + SMALL_SHARDED}
    def finish(pending, after):
        state, names, where = pending
        for n, g in zip(names, _rs_end(state, after, pos)):
            grads[n][where[n]] = g

    def unit(layer, names):
        return [layer[n][1] for n in names], names, {n: layer[n][0] for n in names}

    d2d = ici = None
    token = None
    for l in reversed(range(DEPTH)):
        s_mix, s_cross, s_ffn = saved[l]
        dx, partial['g_ffn_pre'][l], partial['g_ffn_post'][l], dwgu, dwd = _ffn_bwd(dx, s_ffn, *ffn_args(l), after=token)
        token = None
        if d2d is not None:
            g5s, gots = _rs_d2d_wait(d2d[0], dx, "rs_d2d_wait_%d" % (l + 1))
            state, token = _rs_mid(g5s, gots, pos, str(l + 1))
            ici, d2d = (state,) + d2d[1:], None
        (dx, partial['g_cross_pre'][l], partial['g_mem'][l], partial['g_cross_post'][l], dwq, dwkv, dwo) = _cross_bwd(
            dx, s_cross, mems, *cross_args(l), after=token)
        token = None
        layer = {'w_xq': (l, dwq.reshape(NDEV, DM // NDEV, DM)), 'w_xkv': (l, dwkv), 'w_xo': (l, dwo.reshape(NDEV, DM // NDEV, DM)),
                 'w_ffn_gu': (l, dwgu), 'w_ffn_down': (l, dwd.reshape(NDEV, D_FF // NDEV, DM))}
        if l == 0:
            gs, names, where = unit(layer, REST)
            state, token = _rs_begin(gs, pos, "0r")
            ici_rest = (state, names, where)
        if l % 2 == 0:
            e = l // 2
            (dx, partial['g_mix_pre'][l], partial['g_mix_post'][l], dwall, partial['ab_b_f'][e], partial['ab_conv_w'][e],
             dwout) = _fox_layer_bwd(dx, s_mix, *mixer_args(l), after=token)
            layer['ab_w_in'] = (e, _ab_unpack(dwall).reshape(NDEV, AB_IN // NDEV, DM))
            layer['ab_w_out'] = (e, dwout.reshape(NDEV, DM // NDEV, DM))
        else:
            o = l // 2
            (dx, partial['g_mix_pre'][l], partial['g_mix_post'][l], dwin, partial['c_conv_w'][o], dconvb, dwai, dbai, dlam,
             dwout) = _lru_layer_bwd(dx, s_mix, *mixer_args(l), after=token)
            partial['c_conv_b'][o], partial['c_lam'][o] = dconvb[0], dlam[0]
            partial['c_b_a'][o], partial['c_b_i'][o] = dbai[0].reshape(LRU_NB, LRU_BW), dbai[1].reshape(LRU_NB, LRU_BW)
            rows = LRU_BW // NDEV
            by_dev = lambda d: d.reshape(LRU_NB, NDEV, rows, LRU_BW).transpose(1, 0, 2, 3).reshape(NDEV, LRU_NB * rows, LRU_BW)
            layer['c_w_in'] = (o, dwin)
            layer['c_w_a'] = (o, by_dev(dwai[0]))
            layer['c_w_i'] = (o, by_dev(dwai[1]))
            layer['c_w_out'] = (o, dwout.reshape(NDEV, DM // NDEV, DM))
        token = None
        if ici is not None:
            finish(ici, dx)
            ici = None
        if l > 0:
            gs, names, where = unit(layer, list(layer))
            state, token = _rs_d2d_start(_as_g5(gs), "rs_d2d_start_%d" % l)
            d2d = (state, names, where)
    small_names = REPLICATED + SMALL_SHARDED
    small_parts = [jnp.stack([p.reshape(w[n].shape[1:] if n in REPLICATED else small[n].shape[1:]) for p in partial[n]])
                   for n in small_names]
    small_all = _small_gather(_pack_small(small_parts))
    reduced = _unpack_small(_sum_devices(small_all), [p.shape for p in small_parts])
    grad = {}
    for n, g in zip(small_names, reduced):
        grad[n] = g if n in REPLICATED else _small_shard(g, dev)

    gs, names, where = unit(layer, mixer_names(0))
    state, token = _rs_begin(gs, pos, "0m", after=small_all)
    ici_mixer = (state, names, where)
    finish(ici_rest, dx)

    delta, new_m, new_v = {}, {}, {}
    last = mixer_names(0)
    for n in BIG:
        if n not in last:
            grad[n] = jnp.stack(grads[n]).reshape(w[n].shape)
            delta[n], new_m[n], new_v[n] = _adamw_nd(w[n], grad[n], mom[n], var[n], token)
            token = delta[n]
    shapes = [w[n].shape for n in small_names]
    packed = [_pack_small([t[n] for n in small_names]) for t in (w, grad, mom, var)]
    res_small = _adamw(*packed, after=token)
    for res, out in zip(res_small, (delta, new_m, new_v)):
        for n, val in zip(small_names, _unpack_small(res, shapes)):
            out[n] = val
    finish(ici_mixer, res_small[0])
    for n in last:
        if n == 'ab_w_in':
            g_t = jnp.stack(grads[n], axis=1)
            res = (g_t,) + _adamw_nd(ab_t[0], g_t, ab_t[1], ab_t[2])
            grad[n], delta[n], new_m[n], new_v[n] = (r.transpose(1, 2, 0) for r in res)
            continue
        grad[n] = jnp.stack(grads[n]).reshape(w[n].shape)
        delta[n], new_m[n], new_v[n] = _adamw_nd(w[n], grad[n], mom[n], var[n])

    for t in (grad, delta, new_m, new_v):
        t['w_ffn_gu'] = t['w_ffn_gu'].transpose(0, 2, 1)
    return (loss, dx[None], *[grad[n] for n in WEIGHT_NAMES], *[delta[n] for n in WEIGHT_NAMES],
            *[new_m[n] for n in WEIGHT_NAMES], *[new_v[n] for n in WEIGHT_NAMES])
```

```python
import math

import jax
import jax.numpy as jnp
from jax import lax
from jax.experimental import pallas as pl
from jax.experimental.pallas import tpu as pltpu

F32 = jnp.float32
BF16 = jnp.bfloat16
BS = pl.BlockSpec
ANY = pl.BlockSpec(memory_space=pl.ANY)
MESH = pl.DeviceIdType.MESH

DM = 1024
DEPTH = 4
EPS = 1e-6
NEG = -1e30
FOX_W = 512
FOX_HD = 64
FOX_H = 8
SC_W = 512
SC_K = 3
AB_IN = 3 * FOX_W + FOX_H + 3 * SC_W
AB_PAD = 3200
LRU_BW = 256
LRU_NB = 4
RG_K = 4
RG_C = 8.0
MEM_H = 4
MEM_HD = 256
D_FF = 2816
NDEV = 8
FFB = 2 * D_FF // NDEV
ADAM_LR, ADAM_B1, ADAM_B2, ADAM_EPS, ADAM_WD, ADAM_STEP = 0.001, 0.9, 0.999, 1e-08, 0.01, 10

LANE = 128
VMEM_LIMIT = 16 * 1024 * 1024
VMEM_BIG = 48 * 1024 * 1024


def _params(ngrid, vmem=None):
    return pltpu.CompilerParams(dimension_semantics=("arbitrary",) * ngrid, vmem_limit_bytes=vmem or VMEM_LIMIT)


def _call(kern, **kwargs):
    return pl.pallas_call(kern, **kwargs)


TK_RED = 2048
TM_SUM = 512


def _tile(n, t):
    return t if n % t == 0 else n


def _mm(name, a, b, *, grid, a_spec, b_spec, o_spec, out_shape, dn, out_dtype=F32, vmem=None):
    nred = grid[-1]
    ngrid = len(grid)

    def kern(a_ref, b_ref, o_ref, *scratch):
        p = lax.dot_general(a_ref[...].astype(BF16), b_ref[...].astype(BF16), (dn, ((), ())),
                            preferred_element_type=F32)
        if nred == 1:
            o_ref[...] = p.astype(o_ref.dtype)
            return
        acc = scratch[0] if scratch else o_ref
        r = pl.program_id(ngrid - 1)

        @pl.when(r == 0)
        def _():
            acc[...] = p

        @pl.when(r > 0)
        def _():
            acc[...] += p

        if scratch:
            @pl.when(r == nred - 1)
            def _():
                o_ref[...] = acc[...].astype(o_ref.dtype)

    blk = tuple(d for d in o_spec.block_shape if d is not None)
    scratch = [pltpu.VMEM(blk, F32)] if (nred > 1 and out_dtype != F32) else []
    return _call(kern, name=name, grid=grid, in_specs=[a_spec, b_spec], out_specs=o_spec,
                          out_shape=jax.ShapeDtypeStruct(out_shape, out_dtype), scratch_shapes=scratch,
                          compiler_params=_params(ngrid, vmem))(a, b)


NN = ((1,), (0,))
NT = ((1,), (1,))
TN = ((0,), (0,))


def _mm_nn(name, a, w, out_dtype=F32, tn=None, vmem=None):
    m, k = a.shape
    n = w.shape[1]
    tm = _tile(m, 512)
    tn = n if tn is None else tn
    return _mm(name, a, w, grid=(m // tm, n // tn, 1), a_spec=BS((tm, k), lambda i, j, r: (i, 0)),
               b_spec=BS((k, tn), lambda i, j, r: (0, j)), o_spec=BS((tm, tn), lambda i, j, r: (i, j)),
               out_shape=(m, n), dn=NN, out_dtype=out_dtype, vmem=vmem)


def _mm_nt_cols(name, a, wt, tn):
    m, k = a.shape
    n = wt.shape[0]
    tm = _tile(m, 512)
    return _mm(name, a, wt, grid=(m // tm, n // tn, 1), a_spec=BS((tm, k), lambda i, j, r: (i, 0)),
               b_spec=BS((tn, k), lambda i, j, r: (j, 0)), o_spec=BS((tm, tn), lambda i, j, r: (i, j)),
               out_shape=(m, n), dn=NT)


def _mm_tn_rows(name, a, b, tk):
    m, k = a.shape
    n = b.shape[1]
    tm = _tile(m, TK_RED)
    return _mm(name, a, b, grid=(k // tk, m // tm), a_spec=BS((tm, tk), lambda j, r: (r, j)),
               b_spec=BS((tm, n), lambda j, r: (r, 0)), o_spec=BS((tk, n), lambda j, r: (j, 0)),
               out_shape=(k, n), dn=TN)


def _mm_nt(name, a, w, out_dtype=F32, tn=None):
    m, n = a.shape
    k = w.shape[0]
    tm = _tile(m, 512)
    tn = n if tn is None else tn
    return _mm(name, a, w, grid=(m // tm, n // tn), a_spec=BS((tm, tn), lambda i, r: (i, r)),
               b_spec=BS((k, tn), lambda i, r: (0, r)), o_spec=BS((tm, k), lambda i, r: (i, 0)),
               out_shape=(m, k), dn=NT, out_dtype=out_dtype)


def _mm_tn(name, a, b, tn=None):
    m, k = a.shape
    n = b.shape[1]
    tm = _tile(m, TK_RED)
    tn = n if tn is None else tn
    return _mm(name, a, b, grid=(n // tn, m // tm), a_spec=BS((tm, k), lambda j, r: (r, 0)),
               b_spec=BS((tm, tn), lambda j, r: (r, j)), o_spec=BS((k, tn), lambda j, r: (0, j)),
               out_shape=(k, n), dn=TN)


def _bmm_nn(name, a, w, out_dtype=F32):
    m, k = a.shape
    g, _, n = w.shape
    tm = _tile(m, 512)
    return _mm(name, a, w, grid=(g, m // tm, 1), a_spec=BS((tm, k), lambda q, i, r: (i, 0)),
               b_spec=BS((None, k, n), lambda q, i, r: (q, 0, 0)), o_spec=BS((None, tm, n), lambda q, i, r: (q, i, 0)),
               out_shape=(g, m, n), dn=NN, out_dtype=out_dtype)


def _bmm_tn(name, a, b):
    m, k = a.shape
    g, _, n = b.shape
    tm = _tile(m, TK_RED)
    return _mm(name, a, b, grid=(g, m // tm), a_spec=BS((tm, k), lambda q, r: (r, 0)),
               b_spec=BS((None, tm, n), lambda q, r: (q, r, 0)), o_spec=BS((None, k, n), lambda q, r: (q, 0, 0)),
               out_shape=(g, k, n), dn=TN)


def _block_sum(name, a, w, dn, out_cols):
    g, m, ac = a.shape
    tm = _tile(m, TM_SUM)

    def kern(a_ref, w_ref, o_ref):
        acc = None
        for q in range(g):
            p = lax.dot_general(a_ref[q].astype(BF16), w_ref[q].astype(BF16), (dn, ((), ())), preferred_element_type=F32)
            acc = p if acc is None else acc + p
        o_ref[...] = acc

    return _call(kern, name=name, grid=(m // tm,),
                 in_specs=[BS((g, tm, ac), lambda i: (0, i, 0)), BS(w.shape, lambda i: (0, 0, 0))],
                 out_specs=BS((tm, out_cols), lambda i: (i, 0)), out_shape=jax.ShapeDtypeStruct((m, out_cols), F32),
                 compiler_params=_params(1, VMEM_BIG))(a, w)


def _bmm_nt_sum(name, a, w):
    return _block_sum(name, a, w, NT, w.shape[1])


def _bmm_nn_sum(name, a, w):
    return _block_sum(name, a, w, NN, w.shape[2])


def _rstd(x):
    return lax.rsqrt(jnp.mean(x * x, axis=-1, keepdims=True) + EPS)


def _norm_fwd(x, g, after=None):
    rows = x.shape[0]
    tm = _tile(rows, 512)

    def kern(x_ref, g_ref, *rest):
        xv = x_ref[...]
        rest[-1][...] = ((xv * _rstd(xv)) * g_ref[...]).astype(BF16)

    extra = () if after is None else (after,)
    return _call(kern, name="norm_fwd", grid=(rows // tm,),
                          in_specs=[BS((tm, DM), lambda i: (i, 0)), BS((1, DM), lambda i: (0, 0))] + [ANY] * len(extra),
                          out_specs=BS((tm, DM), lambda i: (i, 0)),
                          out_shape=jax.ShapeDtypeStruct((rows, DM), BF16), compiler_params=_params(1))(x, g, *extra)


def _norm_res(x, y, g):
    rows = x.shape[0]
    tm = _tile(rows, 512)

    def kern(x_ref, y_ref, g_ref, o_ref):
        yv = y_ref[...]
        o_ref[...] = x_ref[...] + (yv * _rstd(yv)) * g_ref[...]

    row = BS((tm, DM), lambda i: (i, 0))
    return _call(kern, name="norm_res", grid=(rows // tm,),
                          in_specs=[row, row, BS((1, DM), lambda i: (0, 0))], out_specs=row,
                          out_shape=jax.ShapeDtypeStruct((rows, DM), F32), compiler_params=_params(1))(x, y, g)


def _norm_bwd(z, dout, g, resid, out_dtype, after=None):
    rows = z.shape[0]
    tm = _tile(rows, 512)
    has_res = resid is not None

    def kern(*refs):
        z_ref, d_ref, g_ref = refs[:3]
        r_ref = refs[3] if has_res else None
        dz_ref, dg_ref = refs[-2:]
        zv = z_ref[...]
        dv = d_ref[...].astype(F32)
        r = _rstd(zv)
        zh = zv * r
        dzh = dv * g_ref[...]
        dz = r * (dzh - zh * jnp.mean(dzh * zh, axis=-1, keepdims=True))
        if has_res:
            dz = dz + r_ref[...]
        dz_ref[...] = dz.astype(dz_ref.dtype)
        part = jnp.sum(dv * zh, axis=0, keepdims=True)

        @pl.when(pl.program_id(0) == 0)
        def _():
            dg_ref[...] = part

        @pl.when(pl.program_id(0) > 0)
        def _():
            dg_ref[...] += part

    row = BS((tm, DM), lambda i: (i, 0))
    vec = BS((1, DM), lambda i: (0, 0))
    ins = [row, row, vec] + ([row] if has_res else []) + ([ANY] if after is not None else [])
    args = (z, dout, g) + ((resid,) if has_res else ()) + ((after,) if after is not None else ())
    return _call(kern, name="norm_bwd_res" if has_res else "norm_bwd", grid=(rows // tm,), in_specs=ins,
                          out_specs=[row, vec],
                          out_shape=[jax.ShapeDtypeStruct((rows, DM), out_dtype), jax.ShapeDtypeStruct((1, DM), F32)],
                          compiler_params=_params(1))(*args)


def _ffn_up(h, wgu4):
    s = h.shape[0]
    tm = _tile(s, 512)

    def kern(h_ref, w_ref, gu_ref, a_ref):
        hv = h_ref[...]
        gate = lax.dot_general(hv, w_ref[0], (NT, ((), ())), preferred_element_type=F32)
        up = lax.dot_general(hv, w_ref[1], (NT, ((), ())), preferred_element_type=F32)
        gu_ref[0] = gate.astype(BF16)
        gu_ref[1] = up.astype(BF16)
        a_ref[...] = (gate * jax.nn.sigmoid(gate) * up).astype(BF16)

    return _call(
        kern, name="ffn_up", grid=(4, s // tm),
        in_specs=[BS((tm, DM), lambda j, i: (i, 0)), BS((2, None, FFB, DM), lambda j, i: (0, j, 0, 0))],
        out_specs=[BS((2, None, tm, FFB), lambda j, i: (0, j, i, 0)), BS((None, tm, FFB), lambda j, i: (j, i, 0))],
        out_shape=[jax.ShapeDtypeStruct((2, 4, s, FFB), BF16), jax.ShapeDtypeStruct((4, s, FFB), BF16)],
        compiler_params=_params(2))(h, wgu4)


def _ffn_da(dy, wd4, gu):
    s = dy.shape[0]
    tm = _tile(s, 512)

    def kern(dy_ref, w_ref, gu_ref, o_ref):
        da = lax.dot_general(dy_ref[...], w_ref[...], (NT, ((), ())), preferred_element_type=F32)
        gate = gu_ref[0].astype(F32)
        up = gu_ref[1].astype(F32)
        sg = jax.nn.sigmoid(gate)
        o_ref[0] = (da * up * (sg * (1.0 + gate * (1.0 - sg)))).astype(BF16)
        o_ref[1] = (da * (gate * sg)).astype(BF16)

    blk = BS((2, None, tm, FFB), lambda j, i: (0, j, i, 0))
    return _call(
        kern, name="ffn_da", grid=(4, s // tm),
        in_specs=[BS((tm, DM), lambda j, i: (i, 0)), BS((None, FFB, DM), lambda j, i: (j, 0, 0)), blk],
        out_specs=blk, out_shape=jax.ShapeDtypeStruct((2, 4, s, FFB), BF16), compiler_params=_params(2))(dy, wd4, gu)


def _ffn_fwd(x, gpre, gpost, wgu, wd):
    h = _norm_fwd(x, gpre)
    gu, a = _ffn_up(h, wgu.reshape(2, 4, FFB, DM))
    y = _bmm_nn_sum("ffn_down", a, wd.reshape(4, FFB, DM))
    return _norm_res(x, y, gpost), (x, h, gu, a, y)


def _ffn_bwd(dxo, saved, gpre, gpost, wgu, wd, after=None):
    x, h, gu, a, y = saved
    s = x.shape[0]
    dy, dgpost = _norm_bwd(y, dxo, gpost, None, BF16, after)
    dgu = _ffn_da(dy, wd.reshape(4, FFB, DM), gu).reshape(8, s, FFB)
    dwd = _bmm_tn_a3("ffn_dwd", a, dy)
    dwgu = _bmm_tn_a3("ffn_dwgu", dgu, h)
    dh = _bmm_nn_sum("ffn_dh", dgu, wgu)
    dx, dgpre = _norm_bwd(x, dh, gpre, dxo, F32)
    return dx, dgpre, dgpost, dwgu, dwd.reshape(D_FF, DM)


def _bmm_tn_a3(name, a, b):
    g, m, k = a.shape
    n = b.shape[1]
    tm = _tile(m, TK_RED)
    return _mm(name, a, b, grid=(g, m // tm), a_spec=BS((None, tm, k), lambda q, r: (q, r, 0)),
               b_spec=BS((tm, n), lambda q, r: (r, 0)), o_spec=BS((None, k, n), lambda q, r: (q, 0, 0)),
               out_shape=(g, k, n), dn=TN)


def _softmax_rows(s):
    m = jnp.max(s, axis=-1, keepdims=True)
    p = jnp.exp(s - m)
    return p / jnp.sum(p, axis=-1, keepdims=True)


def _xattn_fwd_call(h, wq, kv):
    s = h.shape[0]
    mlen = kv.shape[1]
    tm = _tile(s, 512)
    scale = MEM_HD ** -0.5

    def kern(h_ref, w_ref, k_ref, v_ref, q_ref, o_ref):
        q = jnp.dot(h_ref[...], w_ref[...], preferred_element_type=F32).astype(BF16)
        q_ref[...] = q
        sc = lax.dot_general(q, k_ref[...], (NT, ((), ())), preferred_element_type=F32) * scale
        p = _softmax_rows(sc)
        o_ref[...] = jnp.dot(p.astype(BF16), v_ref[...], preferred_element_type=F32).astype(BF16)

    blk = BS((tm, MEM_HD), lambda i, hd: (i, hd))
    return _call(
        kern, name="xattn_fwd", grid=(s // tm, MEM_H),
        in_specs=[BS((tm, DM), lambda i, hd: (i, 0)), BS((DM, MEM_HD), lambda i, hd: (0, hd)),
                  BS((None, mlen, MEM_HD), lambda i, hd: (hd, 0, 0)),
                  BS((None, mlen, MEM_HD), lambda i, hd: (MEM_H + hd, 0, 0))],
        out_specs=[blk, blk],
        out_shape=[jax.ShapeDtypeStruct((s, DM), BF16), jax.ShapeDtypeStruct((s, DM), BF16)],
        compiler_params=_params(2))(h, wq, kv, kv)


def _xattn_bwd_call(q, kv, do):
    s = q.shape[0]
    mlen = kv.shape[1]
    tm = _tile(s, 512)
    scale = MEM_HD ** -0.5

    def kern(q_ref, k_ref, v_ref, do_ref, dq_ref, dkv_ref):
        qv, kvv, vv, dov = q_ref[...], k_ref[...], v_ref[...], do_ref[...]
        sc = lax.dot_general(qv, kvv, (NT, ((), ())), preferred_element_type=F32) * scale
        p = _softmax_rows(sc)
        dp = lax.dot_general(dov, vv, (NT, ((), ())), preferred_element_type=F32)
        ds = (p * (dp - jnp.sum(dp * p, axis=-1, keepdims=True)) * scale).astype(BF16)
        dq_ref[...] = jnp.dot(ds, kvv, preferred_element_type=F32).astype(BF16)
        dk = lax.dot_general(ds, qv, (TN, ((), ())), preferred_element_type=F32)
        dv = lax.dot_general(p.astype(BF16), dov, (TN, ((), ())), preferred_element_type=F32)

        @pl.when(pl.program_id(1) == 0)
        def _():
            dkv_ref[0] = dk
            dkv_ref[1] = dv

        @pl.when(pl.program_id(1) > 0)
        def _():
            dkv_ref[0] += dk
            dkv_ref[1] += dv

    blk = BS((tm, MEM_HD), lambda hd, i: (i, hd))
    return _call(
        kern, name="xattn_bwd", grid=(MEM_H, s // tm),
        in_specs=[blk, BS((None, mlen, MEM_HD), lambda hd, i: (hd, 0, 0)),
                  BS((None, mlen, MEM_HD), lambda hd, i: (MEM_H + hd, 0, 0)), blk],
        out_specs=[blk, BS((2, None, mlen, MEM_HD), lambda hd, i: (0, hd, 0, 0))],
        out_shape=[jax.ShapeDtypeStruct((s, DM), BF16), jax.ShapeDtypeStruct((2, MEM_H, mlen, MEM_HD), F32)],
        compiler_params=_params(2))(q, kv, kv, do)


def _cross_fwd(x, mem, gpre, gmem, gpost, wq, wkv, wo, after=None):
    h = _norm_fwd(x, gpre, after)
    mn = _norm_fwd(mem, gmem)
    kv = _bmm_nn("xattn_kv", mn, wkv, BF16)
    q, o = _xattn_fwd_call(h, wq, kv)
    y = _mm_nn("xattn_out", o, wo)
    return _norm_res(x, y, gpost), (x, h, mn, kv, q, o, y)


def _cross_bwd(dxo, saved, mem, gpre, gmem, gpost, wq, wkv, wo, after=None):
    x, h, mn, kv, q, o, y = saved
    mlen = mem.shape[0]
    dy, dgpost = _norm_bwd(y, dxo, gpost, None, BF16, after)
    do = _mm_nt("xattn_do", dy, wo, BF16)
    dwo = _mm_tn("xattn_dwo", o, dy)
    dq, dkv = _xattn_bwd_call(q, kv, do)
    dwq = _mm_tn("xattn_dwq", h, dq)
    dh = _mm_nt("xattn_dh", dq, wq)
    dkv8 = dkv.reshape(8, mlen, MEM_HD)
    dwkv = _bmm_tn("xattn_dwkv", mn, dkv8)
    dmn = _bmm_nt_sum("xattn_dmn", dkv8, wkv)
    _, dgmem = _norm_bwd(mem, dmn, gmem, None, BF16)
    dx, dgpre = _norm_bwd(x, dh, gpre, dxo, F32)
    return dx, dgpre, dgmem, dgpost, dwq, dwkv, dwo


def _log_sigmoid(z):
    return jnp.minimum(z, 0.0) - jnp.log1p(jnp.exp(-jnp.abs(z)))


def _lane_scan_steps():
    return (1, 2, 4, 8, 16, 32, 64)


def _fox_cum(frow, bfb):
    s = frow.shape[1]

    def kern(f_ref, b_ref, o_ref):
        lane = lax.broadcasted_iota(jnp.int32, (FOX_H, LANE), 1)
        carry = jnp.zeros((FOX_H, 1), F32)
        for c in range(s // LANE):
            sl = slice(c * LANE, (c + 1) * LANE)
            lf = _log_sigmoid(f_ref[:, sl] + b_ref[...])
            v = lf
            for d in _lane_scan_steps():
                v = v + jnp.where(lane >= d, pltpu.roll(v, d, 1), 0.0)
            o_ref[:, sl] = v + carry
            carry = carry + jnp.sum(lf, axis=1, keepdims=True)

    return _call(kern, name="fox_cum", out_shape=jax.ShapeDtypeStruct((FOX_H, s), F32),
                          compiler_params=pltpu.CompilerParams(vmem_limit_bytes=VMEM_LIMIT))(frow, bfb)


def _fox_dlogf(dcq, dck, frow, bfb):
    s = frow.shape[1]

    def kern(q_ref, d_ref, f_ref, b_ref, df_ref, db_ref):
        lane = lax.broadcasted_iota(jnp.int32, (FOX_H, LANE), 1)
        carry = jnp.zeros((FOX_H, 1), F32)
        dbf = jnp.zeros((FOX_H, 1), F32)
        for c in reversed(range(s // LANE)):
            sl = slice(c * LANE, (c + 1) * LANE)
            dc = q_ref[:, sl] - d_ref[:, sl]
            v = dc
            for d in _lane_scan_steps():
                v = v + jnp.where(lane < LANE - d, pltpu.roll(v, LANE - d, 1), 0.0)
            v = v + carry
            carry = carry + jnp.sum(dc, axis=1, keepdims=True)
            df = v * jax.nn.sigmoid(-(f_ref[:, sl] + b_ref[...]))
            df_ref[:, sl] = df
            dbf = dbf + jnp.sum(df, axis=1, keepdims=True)
        db_ref[...] = jnp.broadcast_to(dbf, (FOX_H, LANE))

    return _call(kern, name="fox_dlogf",
                          out_shape=[jax.ShapeDtypeStruct((FOX_H, s), F32), jax.ShapeDtypeStruct((FOX_H, LANE), F32)],
                          compiler_params=pltpu.CompilerParams(vmem_limit_bytes=VMEM_LIMIT))(dcq, dck, frow, bfb)


FOX_TQ = 512
Q_COL, K_COL, V_COL = 0, FOX_W // LANE, 2 * FOX_W // LANE
B_COL = 3 * FOX_W // LANE
C_COL = B_COL + SC_W // LANE
U_COL = C_COL + SC_W // LANE


def _bf16_terms(c):
    hi = c.astype(BF16).astype(F32)
    mid = (c - hi).astype(BF16).astype(F32)
    return hi, mid, (c - hi - mid).astype(BF16).astype(F32)


def _fox_operands(qv, kv, cq, ck, lane, hh, scale):
    sel = (lane < FOX_HD) if hh == 0 else (lane >= FOX_HD)
    b0 = FOX_HD if hh == 0 else 0
    qa = jnp.where(sel, qv * scale, 0.0)
    ka = jnp.where(sel, kv, 0.0)
    for n, (tq_, tk_) in enumerate(zip(_bf16_terms(cq), _bf16_terms(ck))):
        qa = jnp.where(lane == b0 + n, tq_, jnp.where(lane == b0 + 3 + n, 1.0, qa))
        ka = jnp.where(lane == b0 + n, 1.0, jnp.where(lane == b0 + 3 + n, -tk_, ka))
    return sel, qa.astype(BF16), ka.astype(BF16)


def _fox_logits(qa, ka, causal):
    sc = lax.dot_general(qa, ka, (NT, ((), ())), preferred_element_type=F32)
    return sc if causal is None else jnp.where(causal, sc, NEG)


def _fox_prep(proj, cumc):
    s = proj.shape[0]
    tp = _tile(s, 512)
    scale = FOX_HD ** -0.5

    def kern(q_ref, k_ref, c_ref, qa_ref, ka_ref):
        lane = lax.broadcasted_iota(jnp.int32, (tp, LANE), 1)
        for hh in range(2):
            _, qa_ref[hh], ka_ref[hh] = _fox_operands(q_ref[...], k_ref[...], c_ref[hh], c_ref[hh], lane, hh, scale)

    pair = BS((2, tp, LANE), lambda hp, i: (hp, i, 0))
    shp = jax.ShapeDtypeStruct((FOX_H, s, LANE), BF16)
    return _call(kern, name="fox_prep", grid=(4, s // tp),
                 in_specs=[BS((tp, LANE), lambda hp, i: (i, Q_COL + hp)), BS((tp, LANE), lambda hp, i: (i, K_COL + hp)), pair],
                 out_specs=[pair, pair], out_shape=[shp, shp], compiler_params=_params(2))(proj, proj, cumc)


def _fox_fwd_call(proj, qa, ka):
    s = proj.shape[0]
    tq = _tile(s, FOX_TQ)
    nq = s // tq

    def kern(qa_ref, ka_ref, v_ref, o_ref, lse_ref, m_s, l_s, acc_s):
        i = pl.program_id(1)
        j = pl.program_id(2)
        lane = lax.broadcasted_iota(jnp.int32, (tq, LANE), 1)

        @pl.when(j == 0)
        def _():
            m_s[...] = jnp.full(m_s.shape, NEG, F32)
            l_s[...] = jnp.zeros(l_s.shape, F32)
            acc_s[...] = jnp.zeros(acc_s.shape, F32)

        def step(diagonal):
            vb = v_ref[...].astype(BF16)
            causal = (lax.broadcasted_iota(jnp.int32, (tq, tq), 0) >= lax.broadcasted_iota(jnp.int32, (tq, tq), 1)
                      if diagonal else None)
            for hh in range(2):
                sc = _fox_logits(qa_ref[hh], ka_ref[hh], causal)
                m_prev = m_s[hh]
                m_new = jnp.maximum(m_prev, jnp.max(sc, axis=-1, keepdims=True))
                alpha = jnp.exp(m_prev - m_new)
                p = jnp.exp(sc - m_new)
                l_s[hh] = alpha * l_s[hh] + jnp.sum(p, axis=-1, keepdims=True)
                acc_s[hh] = alpha * acc_s[hh] + jnp.dot(p.astype(BF16), vb, preferred_element_type=F32)
                m_s[hh] = m_new

        @pl.when(j < i)
        def _():
            step(False)

        @pl.when(j == i)
        def _():
            step(True)
            o_ref[...] = jnp.where(lane < FOX_HD, acc_s[0] / l_s[0], acc_s[1] / l_s[1])
            for hh in range(2):
                lse_ref[hh] = jnp.broadcast_to(m_s[hh] + jnp.log(l_s[hh]), (tq, LANE))

    kvi = lambda hp, i, j: jnp.minimum(j, i)
    return _call(
        kern, name="fox_fwd", grid=(4, nq, nq),
        in_specs=[BS((2, tq, LANE), lambda hp, i, j: (hp, i, 0)),
                  BS((2, tq, LANE), lambda hp, i, j: (hp, kvi(hp, i, j), 0)),
                  BS((tq, LANE), lambda hp, i, j: (kvi(hp, i, j), V_COL + hp))],
        out_specs=[BS((tq, LANE), lambda hp, i, j: (i, hp)), BS((2, tq, LANE), lambda hp, i, j: (hp, i, 0))],
        out_shape=[jax.ShapeDtypeStruct((s, FOX_W), F32), jax.ShapeDtypeStruct((FOX_H, s, LANE), F32)],
        scratch_shapes=[pltpu.VMEM((2, tq, 1), F32), pltpu.VMEM((2, tq, 1), F32), pltpu.VMEM((2, tq, LANE), F32)],
        compiler_params=_params(3))(qa, ka, proj)


ROWSUM_M = 16


def _fox_bwd_call(proj, o, lse, dcat, qa, ka):
    s = proj.shape[0]
    tq = _tile(s, FOX_TQ)
    nq = s // tq
    reps = tq // LANE
    scale = FOX_HD ** -0.5

    def kern(qa_ref, ka_ref, v_ref, do_ref, o_ref, lse_ref, dq_ref, dk_ref, dv_ref, dck_ref, dcq_ref):
        j = pl.program_id(1)
        i = pl.program_id(2)
        lane = lax.broadcasted_iota(jnp.int32, (tq, LANE), 1)
        ones = jnp.ones((ROWSUM_M, tq), BF16)

        @pl.when((j == 0) & (i == 0))
        def _():
            dq_ref[...] = jnp.zeros(dq_ref.shape, F32)
            dcq_ref[...] = jnp.zeros(dcq_ref.shape, F32)

        @pl.when(i == j)
        def _():
            dk_ref[...] = jnp.zeros(dk_ref.shape, F32)
            dv_ref[...] = jnp.zeros(dv_ref.shape, F32)
            dck_ref[...] = jnp.zeros(dck_ref.shape, F32)

        def step(diagonal):
            dov = do_ref[...]
            ov = o_ref[...]
            vb = v_ref[...].astype(BF16)
            causal = (lax.broadcasted_iota(jnp.int32, (tq, tq), 0) >= lax.broadcasted_iota(jnp.int32, (tq, tq), 1)
                      if diagonal else None)
            dq_t = jnp.zeros((tq, LANE), F32)
            dk_t = jnp.zeros((tq, LANE), F32)
            dv_t = jnp.zeros((tq, LANE), F32)
            for hh in range(2):
                sel = (lane < FOX_HD) if hh == 0 else (lane >= FOX_HD)
                qa, ka = qa_ref[hh], ka_ref[hh]
                dom32 = jnp.where(sel, dov, 0.0)
                dom = dom32.astype(BF16)
                sc = _fox_logits(qa, ka, causal)
                p = jnp.exp(sc - jnp.tile(lse_ref[hh], (1, reps)))
                dp = lax.dot_general(dom, vb, (NT, ((), ())), preferred_element_type=F32)
                delta = jnp.sum(dom32 * ov, axis=-1, keepdims=True)
                ds = p * (dp - delta)
                dsb = ds.astype(BF16)
                dq_t = jnp.where(sel, jnp.dot(dsb, ka, preferred_element_type=F32) * scale, dq_t)
                dk_t = jnp.where(sel, lax.dot_general(dsb, qa, (TN, ((), ())), preferred_element_type=F32), dk_t)
                dv_t = dv_t + lax.dot_general(p.astype(BF16), dom, (TN, ((), ())), preferred_element_type=F32)
                dck_ref[hh] += jnp.sum(ds, axis=0, keepdims=True)
                ds_lo = (ds - dsb.astype(F32)).astype(BF16)
                dcq_ref[hh, i] += (lax.dot_general(ones, dsb, (NT, ((), ())), preferred_element_type=F32)
                                   + lax.dot_general(ones, ds_lo, (NT, ((), ())), preferred_element_type=F32))
            rows = pl.ds(pl.multiple_of(i * tq, tq), tq)
            dq_ref[rows, :] += dq_t
            dk_ref[...] += dk_t
            dv_ref[...] += dv_t

        @pl.when(i > j)
        def _():
            step(False)

        @pl.when(i == j)
        def _():
            step(True)

    qi = lambda hp, j, i: jnp.maximum(i, j)
    return _call(
        kern, name="fox_bwd", grid=(4, nq, nq),
        in_specs=[BS((2, tq, LANE), lambda hp, j, i: (hp, qi(hp, j, i), 0)),
                  BS((2, tq, LANE), lambda hp, j, i: (hp, j, 0)),
                  BS((tq, LANE), lambda hp, j, i: (j, V_COL + hp)),
                  BS((tq, LANE), lambda hp, j, i: (qi(hp, j, i), hp)),
                  BS((tq, LANE), lambda hp, j, i: (qi(hp, j, i), hp)),
                  BS((2, tq, LANE), lambda hp, j, i: (hp, qi(hp, j, i), 0))],
        out_specs=[BS((s, LANE), lambda hp, j, i: (0, hp)), BS((tq, LANE), lambda hp, j, i: (j, hp)),
                   BS((tq, LANE), lambda hp, j, i: (j, hp)), BS((2, 1, tq), lambda hp, j, i: (hp, 0, j)),
                   BS((2, nq, ROWSUM_M, tq), lambda hp, j, i: (hp, 0, 0, 0))],
        out_shape=[jax.ShapeDtypeStruct((s, FOX_W), F32), jax.ShapeDtypeStruct((s, FOX_W), F32),
                   jax.ShapeDtypeStruct((s, FOX_W), F32), jax.ShapeDtypeStruct((FOX_H, 1, s), F32),
                   jax.ShapeDtypeStruct((FOX_H, nq, ROWSUM_M, tq), F32)],
        compiler_params=_params(3))(qa, ka, proj, dcat, o, lse)


def _shift_down(v, d, row):
    return jnp.where(row >= d, pltpu.roll(v, d, 0), 0.0)


def _shift_up(v, d, row, n):
    return jnp.where(row < n - d, pltpu.roll(v, n - d, 0), 0.0)


def _sconv_fwd(proj, convw):
    s = proj.shape[0]

    def kern(b_ref, c_ref, u_ref, w_ref, y_ref):
        row = lax.broadcasted_iota(jnp.int32, (s, LANE), 0)
        z = c_ref[...] * u_ref[...]
        conv = w_ref[2:3, :] * z + w_ref[1:2, :] * _shift_down(z, 1, row) + w_ref[0:1, :] * _shift_down(z, 2, row)
        y_ref[...] = (b_ref[...] * conv).astype(BF16)

    col = lambda base: BS((s, LANE), lambda cb: (0, base + cb))
    return _call(kern, name="sconv_fwd", grid=(SC_W // LANE,),
                          in_specs=[col(B_COL), col(C_COL), col(U_COL), BS((SC_K, LANE), lambda cb: (0, cb))],
                          out_specs=BS((s, LANE), lambda cb: (0, cb)),
                          out_shape=jax.ShapeDtypeStruct((s, SC_W), BF16), compiler_params=_params(1))(proj, proj, proj, convw)


def _sconv_bwd(proj, convw, dcat):
    s = proj.shape[0]

    def kern(b_ref, c_ref, u_ref, w_ref, dy_ref, db_ref, dc_ref, du_ref, dw_ref):
        row = lax.broadcasted_iota(jnp.int32, (s, LANE), 0)
        cv, uv, dyv = c_ref[...], u_ref[...], dy_ref[...]
        z = cv * uv
        z1 = _shift_down(z, 1, row)
        z2 = _shift_down(z, 2, row)
        conv = w_ref[2:3, :] * z + w_ref[1:2, :] * z1 + w_ref[0:1, :] * z2
        db_ref[...] = dyv * conv
        dcv = dyv * b_ref[...]
        dz = w_ref[2:3, :] * dcv + w_ref[1:2, :] * _shift_up(dcv, 1, row, s) + w_ref[0:1, :] * _shift_up(dcv, 2, row, s)
        dc_ref[...] = dz * uv
        du_ref[...] = dz * cv
        dw_ref[0:1, :] = jnp.sum(dcv * z2, axis=0, keepdims=True)
        dw_ref[1:2, :] = jnp.sum(dcv * z1, axis=0, keepdims=True)
        dw_ref[2:3, :] = jnp.sum(dcv * z, axis=0, keepdims=True)

    col = lambda base: BS((s, LANE), lambda cb: (0, base + cb))
    out = BS((s, LANE), lambda cb: (0, cb))
    wspec = BS((SC_K, LANE), lambda cb: (0, cb))
    act = jax.ShapeDtypeStruct((s, SC_W), F32)
    return _call(kern, name="sconv_bwd", grid=(SC_W // LANE,),
                          in_specs=[col(B_COL), col(C_COL), col(U_COL), wspec, col(FOX_W // LANE)],
                          out_specs=[out, out, out, wspec],
                          out_shape=[act, act, act, jax.ShapeDtypeStruct((SC_K, SC_W), F32)],
                          compiler_params=_params(1))(proj, proj, proj, convw, dcat)


def _fox_layer_fwd(x, gpre, gpost, wall, bfb, convw, wout, after=None):
    s = x.shape[0]
    h = _norm_fwd(x, gpre, after)
    proj = _mm_nt_cols("fox_proj", h, wall, AB_PAD // 5)
    frow = proj[:, 3 * FOX_W + 3 * SC_W:3 * FOX_W + 3 * SC_W + FOX_H].T
    cumr = _fox_cum(frow, bfb)
    qa, ka = _fox_prep(proj, jnp.broadcast_to(cumr[:, :, None], (FOX_H, s, LANE)))
    o, lse = _fox_fwd_call(proj, qa, ka)
    yb = _sconv_fwd(proj, convw)
    cat = jnp.concatenate([o.astype(BF16), yb], axis=1)
    y = _mm_nn("fox_out", cat, wout)
    return _norm_res(x, y, gpost), (x, h, proj, frow, qa, ka, o, lse, cat, y)


def _fox_layer_bwd(dxo, saved, gpre, gpost, wall, bfb, convw, wout, after=None):
    x, h, proj, frow, qa, ka, o, lse, cat, y = saved
    s = x.shape[0]
    dy, dgpost = _norm_bwd(y, dxo, gpost, None, BF16, after)
    dcat = _mm_nt("fox_dcat", dy, wout)
    dwout = _mm_tn("fox_dwout", cat, dy)
    db, dc, du, dconvw = _sconv_bwd(proj, convw, dcat)
    dq, dk, dv, dck, dcq = _fox_bwd_call(proj, o, lse, dcat, qa, ka)
    dfrow, dbf = _fox_dlogf(dcq[:, :, 0, :].reshape(FOX_H, s), dck.reshape(FOX_H, s), frow, bfb)
    dfcol = jnp.pad(dfrow.T, ((0, 0), (0, LANE - FOX_H)))
    dproj = jnp.concatenate([dq, dk, dv, db, dc, du, dfcol], axis=1).astype(BF16)
    dwall = _mm_tn_rows("fox_dwall", dproj, h, AB_PAD // 5)
    dh = _mm_nn("fox_dh", dproj, wall, vmem=VMEM_BIG)
    dx, dgpre = _norm_bwd(x, dh, gpre, dxo, F32)
    return dx, dgpre, dgpost, dwall, dbf[:, 0], dconvw, dwout


def _ab_pack(wt):
    nf = 3 * FOX_W
    return jnp.concatenate([wt[:nf], wt[nf + FOX_H:], wt[nf:nf + FOX_H],
                            jnp.zeros((AB_PAD - AB_IN, wt.shape[1]), wt.dtype)], axis=0)


def _ab_unpack(wt):
    nf = 3 * FOX_W
    nbcu = 3 * SC_W
    return jnp.concatenate([wt[:nf], wt[nf + nbcu:nf + nbcu + FOX_H], wt[nf:nf + nbcu]], axis=0)


NCH = DM // LANE
CH_PER_BLK = LRU_BW // LANE


def _chunk_spec(s, lead=0):
    return BS((None, s, LANE), lambda ch: (lead + ch // CH_PER_BLK, 0, ch % CH_PER_BLK))


def _vec_chunk(rows):
    return BS((rows, LANE), lambda ch: (0, ch))


def _neg_expm1(x):
    series = -x * (1.0 + x * (1 / 2) * (1.0 + x * (1 / 3) * (1.0 + x * (1 / 4) * (1.0 + x * (1 / 5) * (
        1.0 + x * (1 / 6) * (1.0 + x * (1 / 7)))))))
    return jnp.where(x > -0.25, series, 1.0 - jnp.exp(x))


def _softplus(z):
    return jnp.maximum(z, 0.0) + jnp.log1p(jnp.exp(-jnp.abs(z)))


GELU_C = math.sqrt(2.0 / math.pi)
GELU_A = 0.044715


def _gelu(x):
    return 0.5 * x * (1.0 + jnp.tanh(GELU_C * (x + GELU_A * x * x * x)))


def _gelu_grad(x):
    t = jnp.tanh(GELU_C * (x + GELU_A * x * x * x))
    return 0.5 * (1.0 + t) + 0.5 * x * (1.0 - t * t) * GELU_C * (1.0 + 3.0 * GELU_A * x * x)


def _lru_conv_fwd(gu, convw, convb):
    s = gu.shape[1]

    def kern(x_ref, w_ref, b_ref, u_ref):
        row = lax.broadcasted_iota(jnp.int32, (s, LANE), 0)
        xv = x_ref[...]
        u_ref[...] = (b_ref[...] + w_ref[3:4, :] * xv + w_ref[2:3, :] * _shift_down(xv, 1, row)
                      + w_ref[1:2, :] * _shift_down(xv, 2, row) + w_ref[0:1, :] * _shift_down(xv, 3, row))

    return _call(kern, name="lru_conv_fwd", grid=(NCH,),
                          in_specs=[_chunk_spec(s, LRU_NB), _vec_chunk(RG_K), _vec_chunk(1)], out_specs=_chunk_spec(s),
                          out_shape=jax.ShapeDtypeStruct((LRU_NB, s, LRU_BW), F32), compiler_params=_params(1))(gu, convw, convb)


def _lru_conv_bwd(dud, dug, gu, convw):
    s = gu.shape[1]

    def kern(d1_ref, d2_ref, x_ref, w_ref, dx_ref, dw_ref, db_ref):
        row = lax.broadcasted_iota(jnp.int32, (s, LANE), 0)
        du = d1_ref[...] + d2_ref[...]
        xv = x_ref[...]
        dx_ref[...] = (w_ref[3:4, :] * du + w_ref[2:3, :] * _shift_up(du, 1, row, s) + w_ref[1:2, :] * _shift_up(du, 2, row, s)
                       + w_ref[0:1, :] * _shift_up(du, 3, row, s)).astype(BF16)
        dw_ref[3:4, :] = jnp.sum(du * xv, axis=0, keepdims=True)
        for k in range(1, RG_K):
            dw_ref[3 - k:4 - k, :] = jnp.sum(du * _shift_down(xv, k, row), axis=0, keepdims=True)
        db_ref[...] = jnp.sum(du, axis=0, keepdims=True)

    return _call(kern, name="lru_conv_bwd", grid=(NCH,),
                          in_specs=[_chunk_spec(s), _chunk_spec(s), _chunk_spec(s, LRU_NB), _vec_chunk(RG_K)],
                          out_specs=[_chunk_spec(s), _vec_chunk(RG_K), _vec_chunk(1)],
                          out_shape=[jax.ShapeDtypeStruct((LRU_NB, s, LRU_BW), BF16),
                                     jax.ShapeDtypeStruct((RG_K, DM), F32), jax.ShapeDtypeStruct((1, DM), F32)],
                          compiler_params=_params(1))(dud, dug, gu, convw)


def _lru_gates(z_ref, bai_ref, lam_ref, uv):
    r = jax.nn.sigmoid(z_ref[0] + bai_ref[0:1, :])
    ig = jax.nn.sigmoid(z_ref[1] + bai_ref[1:2, :])
    sp = _softplus(-lam_ref[...])
    la = -RG_C * r * sp
    a = jnp.exp(la)
    sq = jnp.sqrt(_neg_expm1(2.0 * la))
    return r, ig, sp, a, sq


def _scan_steps(n):
    d, out = 1, []
    while d < n:
        out.append(d)
        d *= 2
    return out


def _lru_scan_fwd(z, bai, lam, u, gu):
    s = u.shape[1]
    zspec = BS((2, None, s, LANE), lambda ch: (0, ch // CH_PER_BLK, 0, ch % CH_PER_BLK))

    def kern(z_ref, bai_ref, lam_ref, u_ref, g_ref, hs_ref, y_ref):
        row = lax.broadcasted_iota(jnp.int32, (s, LANE), 0)
        uv = u_ref[...]
        _, ig, _, a, sq = _lru_gates(z_ref, bai_ref, lam_ref, uv)
        b = sq * (ig * uv)
        for d in _scan_steps(s):
            a_sh = jnp.where(row >= d, pltpu.roll(a, d, 0), 1.0)
            b = a * _shift_down(b, d, row) + b
            a = a * a_sh
        hs_ref[...] = b
        y_ref[...] = (_gelu(g_ref[...]) * b).astype(BF16)

    return _call(kern, name="lru_scan_fwd", grid=(NCH,),
                          in_specs=[zspec, _vec_chunk(2), _vec_chunk(1), _chunk_spec(s), _chunk_spec(s)],
                          out_specs=[_chunk_spec(s), BS((s, LANE), lambda ch: (0, ch))],
                          out_shape=[jax.ShapeDtypeStruct((LRU_NB, s, LRU_BW), F32), jax.ShapeDtypeStruct((s, DM), BF16)],
                          compiler_params=_params(1, VMEM_BIG))(z, bai, lam, u, gu)


def _lru_scan_bwd(dyp, z, bai, lam, u, gu, hs):
    s = u.shape[1]
    zspec = BS((2, None, s, LANE), lambda ch: (0, ch // CH_PER_BLK, 0, ch % CH_PER_BLK))

    def kern(dy_ref, z_ref, bai_ref, lam_ref, u_ref, g_ref, hs_ref, dg_ref, dz_ref, du_ref, dbai_ref, dlam_ref):
        row = lax.broadcasted_iota(jnp.int32, (s, LANE), 0)
        uv, gv, hv, dyv = u_ref[...], g_ref[...], hs_ref[...], dy_ref[...]
        r, ig, sp, a, sq = _lru_gates(z_ref, bai_ref, lam_ref, uv)
        dg_ref[...] = (dyv * hv * _gelu_grad(gv)).astype(BF16)
        g = dyv * _gelu(gv)
        an = _shift_up(a, 1, row, s)
        for d in _scan_steps(s):
            an_sh = jnp.where(row < s - d, pltpu.roll(an, s - d, 0), 1.0)
            g = an * _shift_up(g, d, row, s) + g
            an = an * an_sh
        da = g * _shift_down(hv, 1, row)
        dsq = g * (ig * uv)
        di = g * sq * uv
        du_ref[...] = g * sq * ig
        dla = da * a - dsq * (a * a / sq)
        dzr = dla * (-RG_C * sp) * r * (1.0 - r)
        dzi = di * ig * (1.0 - ig)
        dz_ref[0] = dzr.astype(BF16)
        dz_ref[1] = dzi.astype(BF16)
        dbai_ref[0:1, :] = jnp.sum(dzr, axis=0, keepdims=True)
        dbai_ref[1:2, :] = jnp.sum(dzi, axis=0, keepdims=True)
        dlam_ref[...] = jnp.sum(dla * r, axis=0, keepdims=True) * (RG_C * jax.nn.sigmoid(-lam_ref[...]))

    return _call(
        kern, name="lru_scan_bwd", grid=(NCH,),
        in_specs=[BS((s, LANE), lambda ch: (0, ch)), zspec, _vec_chunk(2), _vec_chunk(1), _chunk_spec(s), _chunk_spec(s),
                  _chunk_spec(s)],
        out_specs=[_chunk_spec(s), zspec, _chunk_spec(s), _vec_chunk(2), _vec_chunk(1)],
        out_shape=[jax.ShapeDtypeStruct((LRU_NB, s, LRU_BW), BF16), jax.ShapeDtypeStruct((2, LRU_NB, s, LRU_BW), BF16),
                   jax.ShapeDtypeStruct((LRU_NB, s, LRU_BW), F32), jax.ShapeDtypeStruct((2, DM), F32),
                   jax.ShapeDtypeStruct((1, DM), F32)],
        compiler_params=_params(1, VMEM_BIG))(dyp, z, bai, lam, u, gu, hs)


def _lru_layer_fwd(x, gpre, gpost, win, convw, convb, wai, bai, lam, wout, after=None):
    s = x.shape[0]
    tm = _tile(s, 512)
    h = _norm_fwd(x, gpre, after)
    gu = _bmm_nn("lru_in", h, win)
    u = _lru_conv_fwd(gu, convw, convb)
    z = _mm("lru_gate", u, wai, grid=(2, LRU_NB, s // tm, 1),
            a_spec=BS((None, tm, LRU_BW), lambda k, n, i, r: (n, i, 0)),
            b_spec=BS((None, None, LRU_BW, LRU_BW), lambda k, n, i, r: (k, n, 0, 0)),
            o_spec=BS((None, None, tm, LRU_BW), lambda k, n, i, r: (k, n, i, 0)),
            out_shape=(2, LRU_NB, s, LRU_BW), dn=NN)
    hs, yp = _lru_scan_fwd(z, bai, lam, u, gu)
    y = _mm_nn("lru_out", yp, wout)
    return _norm_res(x, y, gpost), (x, h, gu, u, z, hs, yp, y)


def _lru_layer_bwd(dxo, saved, gpre, gpost, win, convw, convb, wai, bai, lam, wout, after=None):
    x, h, gu, u, z, hs, yp, y = saved
    s = x.shape[0]
    tm = _tile(s, 512)
    dy, dgpost = _norm_bwd(y, dxo, gpost, None, BF16, after)
    dyp = _mm_nt("lru_dyp", dy, wout)
    dwout = _mm_tn("lru_dwout", yp, dy)
    dgate, dz, dud, dbai, dlam = _lru_scan_bwd(dyp, z, bai, lam, u, gu, hs)
    dwai = _mm("lru_dwai", u, dz, grid=(2, LRU_NB, s // tm),
               a_spec=BS((None, tm, LRU_BW), lambda k, n, r: (n, r, 0)),
               b_spec=BS((None, None, tm, LRU_BW), lambda k, n, r: (k, n, r, 0)),
               o_spec=BS((None, None, LRU_BW, LRU_BW), lambda k, n, r: (k, n, 0, 0)),
               out_shape=(2, LRU_NB, LRU_BW, LRU_BW), dn=TN)
    dug = _mm("lru_dug", dz, wai, grid=(LRU_NB, s // tm, 2),
              a_spec=BS((None, None, tm, LRU_BW), lambda n, i, k: (k, n, i, 0)),
              b_spec=BS((None, None, LRU_BW, LRU_BW), lambda n, i, k: (k, n, 0, 0)),
              o_spec=BS((None, tm, LRU_BW), lambda n, i, k: (n, i, 0)),
              out_shape=(LRU_NB, s, LRU_BW), dn=NT)
    duraw, dconvw, dconvb = _lru_conv_bwd(dud, dug, gu, convw)
    dgu = jnp.concatenate([dgate, duraw], axis=0)
    dwin = _bmm_tn("lru_dwin", h, dgu)
    dh = _bmm_nt_sum("lru_dh", dgu, win)
    dx, dgpre = _norm_bwd(x, dh, gpre, dxo, F32)
    return dx, dgpre, dgpost, dwin, dconvw, dconvb, dwai, dbai, dlam, dwout


CHIP_FLIPS = ((1, 0), (0, 1), (1, 1))


def _place():
    return lax.axis_index("x"), lax.axis_index("y"), lax.axis_index("c")


def _flip(v, f):
    return 1 - v if f else v


def _comm_params():
    return pltpu.CompilerParams(vmem_limit_bytes=VMEM_LIMIT)


def _small_gather(v):
    def body(v_ref, o_ref, send_sems, recv_sems, local_sem):
        x, y, c = _place()
        mine = 4 * x + 2 * y + c
        local = pltpu.make_async_copy(v_ref, o_ref.at[mine], local_sem)
        local.start()
        sends = []
        for k in range(1, NDEV):
            fx, fy, fc = (k >> 2) & 1, (k >> 1) & 1, k & 1
            sends.append(pltpu.make_async_remote_copy(
                src_ref=v_ref, dst_ref=o_ref.at[mine], send_sem=send_sems.at[k - 1], recv_sem=recv_sems.at[k - 1],
                device_id=(_flip(x, fx), _flip(y, fy), _flip(c, fc)), device_id_type=MESH))
        for cp in sends:
            cp.start()
        for k in range(1, NDEV):
            fx, fy, fc = (k >> 2) & 1, (k >> 1) & 1, k & 1
            src = 4 * _flip(x, fx) + 2 * _flip(y, fy) + _flip(c, fc)
            pltpu.make_async_remote_copy(src_ref=v_ref, dst_ref=o_ref.at[src], send_sem=send_sems.at[k - 1],
                                         recv_sem=recv_sems.at[k - 1], device_id=(x, y, c), device_id_type=MESH).wait_recv()
        for cp in sends:
            cp.wait_send()
        local.wait()

    return pl.pallas_call(body, name="small_gather", in_specs=[ANY], out_specs=ANY,
                          out_shape=jax.ShapeDtypeStruct((NDEV,) + v.shape, v.dtype),
                          scratch_shapes=[pltpu.SemaphoreType.DMA((NDEV - 1,)), pltpu.SemaphoreType.DMA((NDEV - 1,)),
                                          pltpu.SemaphoreType.DMA],
                          compiler_params=_comm_params())(v)


REL_CHIPS = ((0, 0),) + CHIP_FLIPS


def _rs_d2d(g5s, after=None):
    n = len(g5s)
    extra = () if after is None else (after,)

    def body(*refs):
        ins, gots = refs[:n], refs[n + len(extra):2 * n + len(extra)]
        send_sems, recv_sems = refs[2 * n + len(extra):]
        x, y, c = _place()
        copies = []
        for t in range(n):
            for f, (fx, fy) in enumerate(REL_CHIPS):
                copies.append(pltpu.make_async_remote_copy(
                    src_ref=ins[t].at[_flip(x, fx), _flip(y, fy), 1 - c], dst_ref=gots[t].at[f],
                    send_sem=send_sems.at[4 * t + f], recv_sem=recv_sems.at[4 * t + f], device_id=(x, y, 1 - c),
                    device_id_type=MESH))
        for cp in copies:
            cp.start()
        for cp in copies:
            cp.wait()

    out = [jax.ShapeDtypeStruct((4,) + g.shape[3:], F32) for g in g5s]
    return pl.pallas_call(body, name="rs_d2d", in_specs=[ANY] * (n + len(extra)), out_specs=[ANY] * n, out_shape=out,
                          scratch_shapes=[pltpu.SemaphoreType.DMA((4 * n,)), pltpu.SemaphoreType.DMA((4 * n,))],
                          compiler_params=_comm_params())(*g5s, *extra)


HBM = pl.BlockSpec(memory_space=pltpu.HBM)
SEM = pl.BlockSpec(memory_space=pltpu.SEMAPHORE)
EFFECT = pltpu.SideEffectType.DATAFLOW_SIDE_EFFECTING


def _in_hbm(a):
    return pltpu.with_memory_space_constraint(a, pltpu.HBM)


def _rs_ici_copies(ins, lands, send_sems, recv_sems):
    x, y, c = _place()
    return [pltpu.make_async_remote_copy(
        src_ref=ins[t].at[f], dst_ref=lands[t].at[f], send_sem=send_sems.at[3 * t + f], recv_sem=recv_sems.at[3 * t + f],
        device_id=(_flip(x, fx), _flip(y, fy), c), device_id_type=MESH)
        for t in range(len(ins)) for f, (fx, fy) in enumerate(CHIP_FLIPS)]


def _rs_ici_start(parts, name):
    n = len(parts)

    def body(*refs):
        ins, lands = refs[:n], refs[n:2 * n]
        send_sems, recv_sems = refs[2 * n], refs[2 * n + 1]
        token = refs[-1]
        for cp in _rs_ici_copies(ins, lands, send_sems, recv_sems):
            cp.start()
        token[...] = jnp.zeros(token.shape, token.dtype)

    thru = [pltpu.HBM(p.shape, p.dtype) for p in parts]
    res = pl.pallas_call(
        body, name=name, in_specs=[HBM] * (2 * n),
        out_shape=(pltpu.SemaphoreType.DMA((3 * n,)), pltpu.SemaphoreType.DMA((3 * n,)), *thru, *thru,
                   jax.ShapeDtypeStruct((8, LANE), F32)),
        out_specs=(SEM, SEM, *([HBM] * (2 * n)), pl.BlockSpec(memory_space=pltpu.VMEM)),
        input_output_aliases={i: 2 + i for i in range(2 * n)},
        compiler_params=pltpu.CompilerParams(has_side_effects=EFFECT),
    )(*[_in_hbm(p) for p in parts], *[_in_hbm(lax.empty(p.shape, p.dtype)) for p in parts])
    return res[:-1], res[-1]


def _rs_ici_wait(state, after, name):
    n = (len(state) - 2) // 2

    def body(*refs):
        send_sems, recv_sems = refs[0], refs[1]
        ins, lands = refs[2:2 + n], refs[2 + n:2 + 2 * n]
        for cp in _rs_ici_copies(ins, lands, send_sems, recv_sems):
            cp.wait_send()
            cp.wait_recv()

    thru = [pltpu.HBM(s.shape, s.dtype) for s in state[2:]]
    res = pl.pallas_call(
        body, name=name, in_specs=[SEM, SEM] + [HBM] * (2 * n) + [ANY], out_shape=tuple(thru),
        out_specs=tuple([HBM] * (2 * n)), input_output_aliases={2 + i: i for i in range(2 * n)},
        compiler_params=pltpu.CompilerParams(has_side_effects=EFFECT),
    )(*state, after)
    return list(res[n:])


def _ag_copies(shards, lands, send_sems, recv_sems):
    x, y, c = _place()
    mine = 4 * x + 2 * y + c
    peers = [(x, y, 1 - c)] + [(_flip(x, fx), _flip(y, fy), c) for fx, fy in CHIP_FLIPS]
    return [pltpu.make_async_remote_copy(
        src_ref=shards[t], dst_ref=lands[t].at[mine], send_sem=send_sems.at[4 * t + k], recv_sem=recv_sems.at[4 * t + k],
        device_id=peer, device_id_type=MESH) for t in range(len(shards)) for k, peer in enumerate(peers)]


def _ag_start(shards, after, name):
    n = len(shards)

    def body(*refs):
        ins, lands = refs[:n], refs[n:2 * n]
        send_sems, recv_sems = refs[2 * n + 1], refs[2 * n + 2]
        token = refs[-1]
        for cp in _ag_copies(ins, lands, send_sems, recv_sems):
            cp.start()
        token[...] = jnp.zeros(token.shape, token.dtype)

    thru = [pltpu.HBM(s.shape, s.dtype) for s in shards]
    land = [pltpu.HBM((NDEV,) + s.shape, s.dtype) for s in shards]
    res = pl.pallas_call(
        body, name=name, in_specs=[HBM] * (2 * n) + [ANY],
        out_shape=(pltpu.SemaphoreType.DMA((4 * n,)), pltpu.SemaphoreType.DMA((4 * n,)), *thru, *land,
                   jax.ShapeDtypeStruct((8, LANE), F32)),
        out_specs=(SEM, SEM, *([HBM] * (2 * n)), pl.BlockSpec(memory_space=pltpu.VMEM)),
        input_output_aliases={i: 2 + i for i in range(2 * n)},
        compiler_params=pltpu.CompilerParams(has_side_effects=EFFECT),
    )(*[_in_hbm(s) for s in shards], *[_in_hbm(lax.empty((NDEV,) + s.shape, s.dtype)) for s in shards], after)
    return res[:-1], res[-1]


def _ag_wait(state, after, name):
    n = (len(state) - 2) // 2

    def body(*refs):
        send_sems, recv_sems = refs[0], refs[1]
        ins, lands = refs[2:2 + n], refs[2 + n:2 + 2 * n]
        for cp in _ag_copies(ins, lands, send_sems, recv_sems):
            cp.wait_send()
            cp.wait_recv()

    thru = [pltpu.HBM(s.shape, s.dtype) for s in state[2:]]
    res = pl.pallas_call(
        body, name=name, in_specs=[SEM, SEM] + [HBM] * (2 * n) + [ANY], out_shape=tuple(thru),
        out_specs=tuple([HBM] * (2 * n)), input_output_aliases={2 + i: i for i in range(2 * n)},
        compiler_params=pltpu.CompilerParams(has_side_effects=EFFECT),
    )(*state, after)
    return list(res[:n]), list(res[n:])


def _ag_finish(shards, lands):
    n = len(shards)

    def body(*refs):
        ins, outs, stage = refs[:n], refs[2 * n:3 * n], refs[3 * n:4 * n]
        send_sems, recv_sems, local_sems = refs[4 * n:]
        x, y, c = _place()
        chips = [(_flip(x, fx), _flip(y, fy)) for fx, fy in CHIP_FLIPS]

        def passing(t, j, core, to):
            blk = outs[t].at[4 * chips[j][0] + 2 * chips[j][1] + core]
            return pltpu.make_async_remote_copy(src_ref=blk, dst_ref=blk, send_sem=send_sems.at[3 * t + j],
                                                recv_sem=recv_sems.at[3 * t + j], device_id=to, device_id_type=MESH)

        sends = [passing(t, j, c, (x, y, 1 - c)) for t in range(n) for j in range(3)]
        for cp in sends:
            cp.start()
        load = [pltpu.make_async_copy(ins[t], stage[t], local_sems.at[t]) for t in range(n)]
        mine = [pltpu.make_async_copy(stage[t], outs[t].at[4 * x + 2 * y + c], local_sems.at[t]) for t in range(n)]
        for cp in load:
            cp.start()
        for t in range(n):
            load[t].wait()
            mine[t].start()
        for t in range(n):
            for j in range(3):
                passing(t, j, 1 - c, (x, y, c)).wait_recv()
        for cp in sends:
            cp.wait_send()
        for cp in mine:
            cp.wait()

    return pl.pallas_call(
        body, name="ag_finish", in_specs=[ANY] * (2 * n), out_specs=[ANY] * n,
        out_shape=[jax.ShapeDtypeStruct(l.shape, l.dtype) for l in lands],
        input_output_aliases={n + i: i for i in range(n)},
        scratch_shapes=[pltpu.VMEM(s.shape, s.dtype) for s in shards]
        + [pltpu.SemaphoreType.DMA((3 * n,)), pltpu.SemaphoreType.DMA((3 * n,)), pltpu.SemaphoreType.DMA((n,))],
        compiler_params=_comm_params())(*shards, *lands)


def _row_tile(rows, largest=256):
    for t in (1024, 512, 256, 128, 64, 32, 16, 8):
        if t > largest:
            continue
        if rows % t == 0:
            return t
    return rows


def _rs_chip_sum(pos, g5, got):
    a, b = g5.shape[3:]
    ta = _row_tile(a, 1024)

    def kern(pos_ref, o_ref, g_ref, p_ref):
        p_ref[...] = (o_ref[...] + g_ref[...]).astype(BF16)

    def mine(f, i, pos_ref):
        return (pos_ref[0] ^ ((f + 1) & 1), pos_ref[1] ^ ((f + 1) >> 1), pos_ref[2], i, 0)

    spec = pltpu.PrefetchScalarGridSpec(
        num_scalar_prefetch=1, grid=(3, a // ta),
        in_specs=[BS((None, None, None, ta, b), mine), BS((None, ta, b), lambda f, i, pos_ref: (f + 1, i, 0))],
        out_specs=BS((None, ta, b), lambda f, i, pos_ref: (f, i, 0)))
    return _call(kern, name="rs_chip_sum", grid_spec=spec, out_shape=jax.ShapeDtypeStruct((3, a, b), BF16),
                          compiler_params=_params(2))(pos, g5, got)


def _rs_final_sum(pos, g5, got, recv):
    a, b = g5.shape[3:]
    ta = _row_tile(a, 1024)

    def kern(pos_ref, o_ref, g_ref, r_ref, s_ref):
        acc = o_ref[...] + g_ref[...]
        for f in range(3):
            acc = acc + r_ref[f].astype(F32)
        s_ref[...] = acc

    spec = pltpu.PrefetchScalarGridSpec(
        num_scalar_prefetch=1, grid=(a // ta,),
        in_specs=[BS((None, None, None, ta, b), lambda i, pos_ref: (pos_ref[0], pos_ref[1], pos_ref[2], i, 0)),
                  BS((None, ta, b), lambda i, pos_ref: (0, i, 0)), BS((3, ta, b), lambda i, pos_ref: (0, i, 0))],
        out_specs=BS((ta, b), lambda i, pos_ref: (i, 0)))
    return _call(kern, name="rs_final_sum", grid_spec=spec, out_shape=jax.ShapeDtypeStruct((a, b), F32),
                          compiler_params=_params(1))(pos, g5, got, recv)


def _rs_d2d_copies(ins, lands, send_sems, recv_sems):
    x, y, c = _place()
    return [pltpu.make_async_remote_copy(
        src_ref=ins[t].at[_flip(x, fx), _flip(y, fy), 1 - c], dst_ref=lands[t].at[f], send_sem=send_sems.at[4 * t + f],
        recv_sem=recv_sems.at[4 * t + f], device_id=(x, y, 1 - c), device_id_type=MESH)
        for t in range(len(ins)) for f, (fx, fy) in enumerate(REL_CHIPS)]


def _rs_d2d_start(g5s, name):
    n = len(g5s)

    def body(*refs):
        ins, lands = refs[:n], refs[n:2 * n]
        for cp in _rs_d2d_copies(ins, lands, refs[2 * n], refs[2 * n + 1]):
            cp.start()
        refs[-1][...] = jnp.zeros(refs[-1].shape, F32)

    thru = [pltpu.HBM(g.shape, g.dtype) for g in g5s]
    land = [pltpu.HBM((4,) + g.shape[3:], F32) for g in g5s]
    res = pl.pallas_call(
        body, name=name, in_specs=[HBM] * (2 * n),
        out_shape=(pltpu.SemaphoreType.DMA((4 * n,)), pltpu.SemaphoreType.DMA((4 * n,)), *thru, *land,
                   jax.ShapeDtypeStruct((8, LANE), F32)),
        out_specs=(SEM, SEM, *([HBM] * (2 * n)), pl.BlockSpec(memory_space=pltpu.VMEM)),
        input_output_aliases={i: 2 + i for i in range(2 * n)},
        compiler_params=pltpu.CompilerParams(has_side_effects=EFFECT),
    )(*[_in_hbm(g) for g in g5s], *[_in_hbm(lax.empty((4,) + g.shape[3:], F32)) for g in g5s])
    return res[:-1], res[-1]


def _rs_d2d_wait(state, after, name):
    n = (len(state) - 2) // 2

    def body(*refs):
        ins, lands = refs[2:2 + n], refs[2 + n:2 + 2 * n]
        for cp in _rs_d2d_copies(ins, lands, refs[0], refs[1]):
            cp.wait_send()
            cp.wait_recv()

    thru = [pltpu.HBM(s.shape, s.dtype) for s in state[2:]]
    res = pl.pallas_call(
        body, name=name, in_specs=[SEM, SEM] + [HBM] * (2 * n) + [ANY], out_shape=tuple(thru),
        out_specs=tuple([HBM] * (2 * n)), input_output_aliases={2 + i: i for i in range(2 * n)},
        compiler_params=pltpu.CompilerParams(has_side_effects=EFFECT),
    )(*state, after)
    return list(res[:n]), list(res[n:])


def _as_g5(grads):
    return [g.reshape((2, 2, 2) + g.shape[1:]) for g in grads]


def _rs_mid(g5s, gots, pos, tag):
    parts = [_rs_chip_sum(pos, g, got) for g, got in zip(g5s, gots)]
    state, token = _rs_ici_start(parts, "rs_ici_start_" + tag)
    return (g5s, gots, state, tag), token


def _rs_begin(grads, pos, tag, after=None):
    g5s = _as_g5(grads)
    return _rs_mid(g5s, _rs_d2d(g5s, after), pos, tag)


def _rs_end(pending, after, pos):
    g5s, gots, state, tag = pending
    recvs = _rs_ici_wait(state, after, "rs_ici_wait_" + tag)
    return [_rs_final_sum(pos, g, got, r) for g, got, r in zip(g5s, gots, recvs)]


def _sum_devices(v):
    _, r, _ = v.shape

    def kern(v_ref, o_ref):
        acc = v_ref[0]
        for d in range(1, NDEV):
            acc = acc + v_ref[d]
        o_ref[...] = acc

    return _call(kern, name="sum_devices", out_shape=jax.ShapeDtypeStruct((r, LANE), F32),
                          compiler_params=_comm_params())(v)


def _loss_head(xf, target):
    s = xf.shape[0]
    tm = _tile(s, 512)

    def kern(x_ref, t_ref, dx_ref, l_ref):
        err = x_ref[...] - t_ref[...]
        dx_ref[...] = err * (1.0 / DM)
        part = jnp.broadcast_to(0.5 * jnp.sum(jnp.mean(err * err, axis=-1, keepdims=True), axis=0, keepdims=True), (8, LANE))

        @pl.when(pl.program_id(0) == 0)
        def _():
            l_ref[...] = part

        @pl.when(pl.program_id(0) > 0)
        def _():
            l_ref[...] += part

    row = BS((tm, DM), lambda i: (i, 0))
    return _call(kern, name="loss_head", grid=(s // tm,), in_specs=[row, row],
                          out_specs=[row, BS((8, LANE), lambda i: (0, 0))],
                          out_shape=[jax.ShapeDtypeStruct((s, DM), F32), jax.ShapeDtypeStruct((8, LANE), F32)],
                          compiler_params=_params(1))(xf, target)


def _adamw(w, g, m, v, after=None):
    rows, cols = w.shape
    tr = _row_tile(rows)
    extra = () if after is None else (after,)

    def kern(w_ref, g_ref, m_ref, v_ref, *rest):
        d_ref, nm_ref, nv_ref = rest[-3:]
        gv = g_ref[...]
        nm = ADAM_B1 * m_ref[...] + (1.0 - ADAM_B1) * gv
        nv = ADAM_B2 * v_ref[...] + (1.0 - ADAM_B2) * (gv * gv)
        m_hat = nm / (1.0 - ADAM_B1 ** ADAM_STEP)
        v_hat = nv / (1.0 - ADAM_B2 ** ADAM_STEP)
        d_ref[...] = -ADAM_LR * (m_hat / (jnp.sqrt(v_hat) + ADAM_EPS) + ADAM_WD * w_ref[...])
        nm_ref[...] = nm
        nv_ref[...] = nv

    blk = BS((tr, cols), lambda i: (i, 0))
    shp = jax.ShapeDtypeStruct((rows, cols), F32)
    return _call(kern, name="adamw", grid=(rows // tr,), in_specs=[blk] * 4 + [ANY] * len(extra),
                          out_specs=[blk] * 3, out_shape=[shp] * 3, compiler_params=_params(1))(w, g, m, v, *extra)


def _adamw_nd(w, g, m, v, after=None):
    shape = w.shape
    two = (math.prod(shape[:-1]), shape[-1])
    return tuple(o.reshape(shape)
                 for o in _adamw(w.reshape(two), g.reshape(two), m.reshape(two), v.reshape(two), after))


def _pack_small(parts):
    flat = jnp.concatenate([p.reshape(-1) for p in parts])
    pad = (-flat.shape[0]) % (8 * LANE)
    return jnp.pad(flat, (0, pad)).reshape(-1, LANE)


def _unpack_small(packed, shapes, lead=()):
    flat = packed.reshape(lead + (-1,))
    out, off = [], 0
    for shp in shapes:
        n = math.prod(shp)
        out.append(flat[..., off:off + n].reshape(lead + tuple(shp)))
        off += n
    return out


WEIGHT_NAMES = ('g_mix_pre', 'g_mix_post', 'g_cross_pre', 'g_mem', 'g_cross_post', 'g_ffn_pre', 'g_ffn_post', 'w_xq',
                'w_xkv', 'w_xo', 'w_ffn_gu', 'w_ffn_down', 'ab_w_in', 'ab_b_f', 'ab_conv_w', 'ab_w_out', 'c_w_in',
                'c_conv_w', 'c_conv_b', 'c_w_a', 'c_b_a', 'c_w_i', 'c_b_i', 'c_lam', 'c_w_out')
BIG = ('w_xq', 'w_xkv', 'w_xo', 'w_ffn_gu', 'w_ffn_down', 'ab_w_in', 'ab_w_out', 'c_w_in', 'c_w_a', 'c_w_i', 'c_w_out')
SMALL_SHARDED = ('ab_conv_w', 'c_conv_w', 'c_conv_b', 'c_b_a', 'c_b_i', 'c_lam')
REPLICATED = ('g_mix_pre', 'g_mix_post', 'g_cross_pre', 'g_mem', 'g_cross_post', 'g_ffn_pre', 'g_ffn_post', 'ab_b_f')


def _small_full(name, gathered):
    nd = gathered.ndim
    return jnp.moveaxis(gathered, 0, nd - 2).reshape(gathered.shape[1:-1] + (NDEV * gathered.shape[-1],))


def _small_shard(full, dev):
    c = full.shape[-1] // NDEV
    return lax.dynamic_slice_in_dim(full, dev * c, c, axis=full.ndim - 1)


def kernel(x, mem, g_mix_pre, g_mix_post, g_cross_pre, g_mem, g_cross_post, g_ffn_pre, g_ffn_post, w_xq, w_xkv, w_xo, w_ffn_gu, w_ffn_down, ab_w_in, ab_b_f, ab_conv_w, ab_w_out, c_w_in, c_conv_w, c_conv_b, c_w_a, c_b_a, c_w_i, c_b_i, c_lam, c_w_out, loss_target, m_g_mix_pre, m_g_mix_post, m_g_cross_pre, m_g_mem, m_g_cross_post, m_g_ffn_pre, m_g_ffn_post, m_w_xq, m_w_xkv, m_w_xo, m_w_ffn_gu, m_w_ffn_down, m_ab_w_in, m_ab_b_f, m_ab_conv_w, m_ab_w_out, m_c_w_in, m_c_conv_w, m_c_conv_b, m_c_w_a, m_c_b_a, m_c_w_i, m_c_b_i, m_c_lam, m_c_w_out, v_g_mix_pre, v_g_mix_post, v_g_cross_pre, v_g_mem, v_g_cross_post, v_g_ffn_pre, v_g_ffn_post, v_w_xq, v_w_xkv, v_w_xo, v_w_ffn_gu, v_w_ffn_down, v_ab_w_in, v_ab_b_f, v_ab_conv_w, v_ab_w_out, v_c_w_in, v_c_conv_w, v_c_conv_b, v_c_w_a, v_c_b_a, v_c_w_i, v_c_b_i, v_c_lam, v_c_w_out):
    args = locals()
    w = {n: args[n] for n in WEIGHT_NAMES}
    mom = {n: args["m_" + n] for n in WEIGHT_NAMES}
    var = {n: args["v_" + n] for n in WEIGHT_NAMES}
    for t in (w, mom, var):
        t['w_ffn_gu'] = t['w_ffn_gu'].transpose(0, 2, 1)
    ab_t = [t['ab_w_in'].transpose(2, 0, 1) for t in (w, mom, var)]
    pos = jnp.stack([lax.axis_index("x"), lax.axis_index("y"), lax.axis_index("c")]).astype(jnp.int32)
    dev = 4 * pos[0] + 2 * pos[1] + pos[2]
    xs, mems, target = x[0], mem[0], loss_target[0]
    n_even, n_odd = (DEPTH + 1) // 2, DEPTH // 2

    small_shapes = [w[n].shape for n in SMALL_SHARDED]
    small_w_all = _small_gather(_pack_small([w[n] for n in SMALL_SHARDED]))
    gathered_small = _unpack_small(small_w_all, small_shapes, (NDEV,))
    small = {n: _small_full(n, g) for n, g in zip(SMALL_SHARDED, gathered_small)}
    ab_bfb = jnp.broadcast_to(ab_b_f[:, :, None], (n_even, FOX_H, LANE))
    c_bai = jnp.stack([small['c_b_a'].reshape(n_odd, DM), small['c_b_i'].reshape(n_odd, DM)], axis=1)
    row = lambda a, l: a[l][None]

    REST = ('w_xq', 'w_xkv', 'w_xo', 'w_ffn_gu', 'w_ffn_down')

    def mixer_names(l):
        return ('ab_w_in', 'ab_w_out') if l % 2 == 0 else ('c_w_in', 'c_w_a', 'c_w_i', 'c_w_out')

    def shards_of(l, names):
        out = []
        for n in names:
            if n == 'ab_w_in':
                s = ab_t[0][:, l // 2].astype(BF16)
            else:
                s = w[n][l if w[n].shape[0] == DEPTH else l // 2].astype(BF16)
            out.append(s.reshape(-1, s.shape[-1]))
        return out

    def mixer_weights(l, full):
        if l % 2 == 0:
            e = l // 2
            return (row(g_mix_pre, l), row(g_mix_post, l), _ab_pack(full['ab_w_in'].reshape(AB_IN, DM)), ab_bfb[e],
                    small['ab_conv_w'][e], full['ab_w_out'].reshape(DM, DM))
        o = l // 2
        gate_w = lambda g: g.reshape(NDEV, LRU_NB, LRU_BW // NDEV, LRU_BW).transpose(1, 0, 2, 3).reshape(
            LRU_NB, LRU_BW, LRU_BW)
        return (row(g_mix_pre, l), row(g_mix_post, l), full['c_w_in'], small['c_conv_w'][o], row(small['c_conv_b'], o),
                jnp.stack([gate_w(full['c_w_a']), gate_w(full['c_w_i'])]), c_bai[o], row(small['c_lam'], o),
                full['c_w_out'].reshape(DM, DM))

    def rest_weights(l, full):
        cross = (row(g_cross_pre, l), row(g_mem, l), row(g_cross_post, l), full['w_xq'].reshape(DM, DM), full['w_xkv'],
                 full['w_xo'].reshape(DM, DM))
        ffn = (row(g_ffn_pre, l), row(g_ffn_post, l), full['w_ffn_gu'], full['w_ffn_down'].reshape(D_FF, DM))
        return cross, ffn

    def gathered(state, names, after, tag):
        shards, lands = _ag_wait(state, after, "ag_wait_" + tag)
        full = _ag_finish(shards, lands)
        return dict(zip(names, full)), full[0]

    saved, weights = [], []
    h = xs
    names_of = lambda l: mixer_names(l) + REST
    states = {}
    st_m, _ = _ag_start(shards_of(0, mixer_names(0)), small_w_all, "ag_start_0m")
    st_r, _ = _ag_start(shards_of(0, REST), st_m[2], "ag_start_0r")
    states[1], token = _ag_start(shards_of(1, names_of(1)), st_r[2], "ag_start_1")
    full_m, _ = gathered(st_m, mixer_names(0), xs, "0m")
    for l in range(DEPTH):
        if l > 0:
            full, done = gathered(states[l], names_of(l), h, str(l))
            full_m = full_r = full
            token = None
            if l + 2 < DEPTH:
                states[l + 2], token = _ag_start(shards_of(l + 2, names_of(l + 2)), done, "ag_start_%d" % (l + 2))
        mixer = mixer_weights(l, full_m)
        h, s_mix = (_fox_layer_fwd if l % 2 == 0 else _lru_layer_fwd)(h, *mixer, after=token)
        token = None
        if l == 0:
            full_r, done = gathered(st_r, REST, h, "0r")
            states[2], token = _ag_start(shards_of(2, names_of(2)), done, "ag_start_2")
        cross, ffn = rest_weights(l, full_r)
        h, s_cross = _cross_fwd(h, mems, *cross, after=token)
        h, s_ffn = _ffn_fwd(h, *ffn)
        saved.append((s_mix, s_cross, s_ffn))
        weights.append((mixer, cross, ffn))
    mixer_args = lambda l: weights[l][0]
    cross_args = lambda l: weights[l][1]
    ffn_args = lambda l: weights[l][2]
    dx, loss_rep = _loss_head(h, target)
    loss = lax.psum(loss_rep[0, 0], ("x", "y", "c"))

    grads = {n: [None] * w[n].shape[0] for n in BIG}
    partial = {n: [None] * w[n].shape[0] for n in REPLICATED + SMALL_SHARDED}
    def finish(pending, after):
        state, names, where = pending
        for n, g in zip(names, _rs_end(state, after, pos)):
            grads[n][where[n]] = g

    def unit(layer, names):
        return [layer[n][1] for n in names], names, {n: layer[n][0] for n in names}

    d2d = ici = None
    token = None
    for l in reversed(range(DEPTH)):
        s_mix, s_cross, s_ffn = saved[l]
        dx, partial['g_ffn_pre'][l], partial['g_ffn_post'][l], dwgu, dwd = _ffn_bwd(dx, s_ffn, *ffn_args(l), after=token)
        token = None
        if d2d is not None:
            g5s, gots = _rs_d2d_wait(d2d[0], dx, "rs_d2d_wait_%d" % (l + 1))
            state, token = _rs_mid(g5s, gots, pos, str(l + 1))
            ici, d2d = (state,) + d2d[1:], None
        (dx, partial['g_cross_pre'][l], partial['g_mem'][l], partial['g_cross_post'][l], dwq, dwkv, dwo) = _cross_bwd(
            dx, s_cross, mems, *cross_args(l), after=token)
        token = None
        layer = {'w_xq': (l, dwq.reshape(NDEV, DM // NDEV, DM)), 'w_xkv': (l, dwkv), 'w_xo': (l, dwo.reshape(NDEV, DM // NDEV, DM)),
                 'w_ffn_gu': (l, dwgu), 'w_ffn_down': (l, dwd.reshape(NDEV, D_FF // NDEV, DM))}
        if l == 0:
            gs, names, where = unit(layer, REST)
            state, token = _rs_begin(gs, pos, "0r")
            ici_rest = (state, names, where)
        if l % 2 == 0:
            e = l // 2
            (dx, partial['g_mix_pre'][l], partial['g_mix_post'][l], dwall, partial['ab_b_f'][e], partial['ab_conv_w'][e],
             dwout) = _fox_layer_bwd(dx, s_mix, *mixer_args(l), after=token)
            layer['ab_w_in'] = (e, _ab_unpack(dwall).reshape(NDEV, AB_IN // NDEV, DM))
            layer['ab_w_out'] = (e, dwout.reshape(NDEV, DM // NDEV, DM))
        else:
            o = l // 2
            (dx, partial['g_mix_pre'][l], partial['g_mix_post'][l], dwin, partial['c_conv_w'][o], dconvb, dwai, dbai, dlam,
             dwout) = _lru_layer_bwd(dx, s_mix, *mixer_args(l), after=token)
            partial['c_conv_b'][o], partial['c_lam'][o] = dconvb[0], dlam[0]
            partial['c_b_a'][o], partial['c_b_i'][o] = dbai[0].reshape(LRU_NB, LRU_BW), dbai[1].reshape(LRU_NB, LRU_BW)
            rows = LRU_BW // NDEV
            by_dev = lambda d: d.reshape(LRU_NB, NDEV, rows, LRU_BW).transpose(1, 0, 2, 3).reshape(NDEV, LRU_NB * rows, LRU_BW)
            layer['c_w_in'] = (o, dwin)
            layer['c_w_a'] = (o, by_dev(dwai[0]))
            layer['c_w_i'] = (o, by_dev(dwai[1]))
            layer['c_w_out'] = (o, dwout.reshape(NDEV, DM // NDEV, DM))
        token = None
        if ici is not None:
            finish(ici, dx)
            ici = None
        if l > 0:
            gs, names, where = unit(layer, list(layer))
            state, token = _rs_d2d_start(_as_g5(gs), "rs_d2d_start_%d" % l)
            d2d = (state, names, where)
    small_names = REPLICATED + SMALL_SHARDED
    small_parts = [jnp.stack([p.reshape(w[n].shape[1:] if n in REPLICATED else small[n].shape[1:]) for p in partial[n]])
                   for n in small_names]
    small_all = _small_gather(_pack_small(small_parts))
    reduced = _unpack_small(_sum_devices(small_all), [p.shape for p in small_parts])
    grad = {}
    for n, g in zip(small_names, reduced):
        grad[n] = g if n in REPLICATED else _small_shard(g, dev)

    gs, names, where = unit(layer, mixer_names(0))
    state, token = _rs_begin(gs, pos, "0m", after=small_all)
    ici_mixer = (state, names, where)
    finish(ici_rest, dx)

    delta, new_m, new_v = {}, {}, {}
    last = mixer_names(0)
    for n in BIG:
        if n not in last:
            grad[n] = jnp.stack(grads[n]).reshape(w[n].shape)
            delta[n], new_m[n], new_v[n] = _adamw_nd(w[n], grad[n], mom[n], var[n], token)
            token = delta[n]
    shapes = [w[n].shape for n in small_names]
    packed = [_pack_small([t[n] for n in small_names]) for t in (w, grad, mom, var)]
    res_small = _adamw(*packed, after=token)
    for res, out in zip(res_small, (delta, new_m, new_v)):
        for n, val in zip(small_names, _unpack_small(res, shapes)):
            out[n] = val
    finish(ici_mixer, res_small[0])
    for n in last:
        if n == 'ab_w_in':
            g_t = jnp.stack(grads[n], axis=1)
            res = (g_t,) + _adamw_nd(ab_t[0], g_t, ab_t[1], ab_t[2])
            grad[n], delta[n], new_m[n], new_v[n] = (r.transpose(1, 2, 0) for r in res)
            continue
        grad[n] = jnp.stack(grads[n]).reshape(w[n].shape)
        delta[n], new_m[n], new_v[n] = _adamw_nd(w[n], grad[n], mom[n], var[n])

    for t in (grad, delta, new_m, new_v):
        t['w_ffn_gu'] = t['w_ffn_gu'].transpose(0, 2, 1)
    return (loss, dx[None], *[grad[n] for n in WEIGHT_NAMES], *[delta[n] for n in WEIGHT_NAMES],
            *[new_m[n] for n in WEIGHT_NAMES], *[new_v[n] for n in WEIGHT_NAMES])
```

```python
import math

import jax
import jax.numpy as jnp
from jax import lax
from jax.experimental import pallas as pl
from jax.experimental.pallas import tpu as pltpu

F32 = jnp.float32
BF16 = jnp.bfloat16
BS = pl.BlockSpec
ANY = pl.BlockSpec(memory_space=pl.ANY)
MESH = pl.DeviceIdType.MESH

DM = 1024
DEPTH = 4
EPS = 1e-6
NEG = -1e30
FOX_W = 512
FOX_HD = 64
FOX_H = 8
SC_W = 512
SC_K = 3
AB_IN = 3 * FOX_W + FOX_H + 3 * SC_W
AB_PAD = 3200
LRU_BW = 256
LRU_NB = 4
RG_K = 4
RG_C = 8.0
MEM_H = 4
MEM_HD = 256
D_FF = 2816
NDEV = 8
FFB = 2 * D_FF // NDEV
ADAM_LR, ADAM_B1, ADAM_B2, ADAM_EPS, ADAM_WD, ADAM_STEP = 0.001, 0.9, 0.999, 1e-08, 0.01, 10

LANE = 128
VMEM_LIMIT = 16 * 1024 * 1024
VMEM_BIG = 48 * 1024 * 1024


def _params(ngrid, vmem=None):
    return pltpu.CompilerParams(dimension_semantics=("arbitrary",) * ngrid, vmem_limit_bytes=vmem or VMEM_LIMIT)


def _call(kern, **kwargs):
    return pl.pallas_call(kern, **kwargs)


TK_RED = 2048
TM_SUM = 512


def _tile(n, t):
    return t if n % t == 0 else n


def _mm(name, a, b, *, grid, a_spec, b_spec, o_spec, out_shape, dn, out_dtype=F32, vmem=None):
    nred = grid[-1]
    ngrid = len(grid)

    def kern(a_ref, b_ref, o_ref, *scratch):
        p = lax.dot_general(a_ref[...].astype(BF16), b_ref[...].astype(BF16), (dn, ((), ())),
                            preferred_element_type=F32)
        if nred == 1:
            o_ref[...] = p.astype(o_ref.dtype)
            return
        acc = scratch[0] if scratch else o_ref
        r = pl.program_id(ngrid - 1)

        @pl.when(r == 0)
        def _():
            acc[...] = p

        @pl.when(r > 0)
        def _():
            acc[...] += p

        if scratch:
            @pl.when(r == nred - 1)
            def _():
                o_ref[...] = acc[...].astype(o_ref.dtype)

    blk = tuple(d for d in o_spec.block_shape if d is not None)
    scratch = [pltpu.VMEM(blk, F32)] if (nred > 1 and out_dtype != F32) else []
    a_blk = tuple(d for d in a_spec.block_shape if d is not None)
    cost = pl.CostEstimate(
        flops=2 * math.prod(grid) * math.prod(blk) * a_blk[dn[0][0]], transcendentals=0,
        bytes_accessed=a.size * a.dtype.itemsize + b.size * b.dtype.itemsize
        + math.prod(out_shape) * jnp.dtype(out_dtype).itemsize)
    return _call(kern, name=name, grid=grid, in_specs=[a_spec, b_spec], out_specs=o_spec,
                 out_shape=jax.ShapeDtypeStruct(out_shape, out_dtype), scratch_shapes=scratch, cost_estimate=cost,
                 compiler_params=_params(ngrid, vmem))(a, b)


NN = ((1,), (0,))
NT = ((1,), (1,))
TN = ((0,), (0,))


def _mm_nn(name, a, w, out_dtype=F32, tn=None, vmem=None):
    m, k = a.shape
    n = w.shape[1]
    tm = _tile(m, 512)
    tn = n if tn is None else tn
    return _mm(name, a, w, grid=(m // tm, n // tn, 1), a_spec=BS((tm, k), lambda i, j, r: (i, 0)),
               b_spec=BS((k, tn), lambda i, j, r: (0, j)), o_spec=BS((tm, tn), lambda i, j, r: (i, j)),
               out_shape=(m, n), dn=NN, out_dtype=out_dtype, vmem=vmem)


def _mm_nt_cols(name, a, wt, tn):
    m, k = a.shape
    n = wt.shape[0]
    tm = _tile(m, 512)
    return _mm(name, a, wt, grid=(m // tm, n // tn, 1), a_spec=BS((tm, k), lambda i, j, r: (i, 0)),
               b_spec=BS((tn, k), lambda i, j, r: (j, 0)), o_spec=BS((tm, tn), lambda i, j, r: (i, j)),
               out_shape=(m, n), dn=NT)


def _mm_tn_rows(name, a, b, tk):
    m, k = a.shape
    n = b.shape[1]
    tm = _tile(m, TK_RED)
    return _mm(name, a, b, grid=(k // tk, m // tm), a_spec=BS((tm, tk), lambda j, r: (r, j)),
               b_spec=BS((tm, n), lambda j, r: (r, 0)), o_spec=BS((tk, n), lambda j, r: (j, 0)),
               out_shape=(k, n), dn=TN)


def _mm_nt(name, a, w, out_dtype=F32, tn=None):
    m, n = a.shape
    k = w.shape[0]
    tm = _tile(m, 512)
    tn = n if tn is None else tn
    return _mm(name, a, w, grid=(m // tm, n // tn), a_spec=BS((tm, tn), lambda i, r: (i, r)),
               b_spec=BS((k, tn), lambda i, r: (0, r)), o_spec=BS((tm, k), lambda i, r: (i, 0)),
               out_shape=(m, k), dn=NT, out_dtype=out_dtype)


def _mm_tn(name, a, b, tn=None):
    m, k = a.shape
    n = b.shape[1]
    tm = _tile(m, TK_RED)
    tn = n if tn is None else tn
    return _mm(name, a, b, grid=(n // tn, m // tm), a_spec=BS((tm, k), lambda j, r: (r, 0)),
               b_spec=BS((tm, tn), lambda j, r: (r, j)), o_spec=BS((k, tn), lambda j, r: (0, j)),
               out_shape=(k, n), dn=TN)


def _bmm_nn(name, a, w, out_dtype=F32):
    m, k = a.shape
    g, _, n = w.shape
    tm = _tile(m, 512)
    return _mm(name, a, w, grid=(g, m // tm, 1), a_spec=BS((tm, k), lambda q, i, r: (i, 0)),
               b_spec=BS((None, k, n), lambda q, i, r: (q, 0, 0)), o_spec=BS((None, tm, n), lambda q, i, r: (q, i, 0)),
               out_shape=(g, m, n), dn=NN, out_dtype=out_dtype)


def _bmm_tn(name, a, b):
    m, k = a.shape
    g, _, n = b.shape
    tm = _tile(m, TK_RED)
    return _mm(name, a, b, grid=(g, m // tm), a_spec=BS((tm, k), lambda q, r: (r, 0)),
               b_spec=BS((None, tm, n), lambda q, r: (q, r, 0)), o_spec=BS((None, k, n), lambda q, r: (q, 0, 0)),
               out_shape=(g, k, n), dn=TN)


def _block_sum(name, a, w, dn, out_cols):
    g, m, ac = a.shape
    tm = _tile(m, TM_SUM)

    def kern(a_ref, w_ref, o_ref):
        acc = None
        for q in range(g):
            p = lax.dot_general(a_ref[q].astype(BF16), w_ref[q].astype(BF16), (dn, ((), ())), preferred_element_type=F32)
            acc = p if acc is None else acc + p
        o_ref[...] = acc

    cost = pl.CostEstimate(flops=2 * a.size * out_cols, transcendentals=0,
                           bytes_accessed=a.size * a.dtype.itemsize + w.size * w.dtype.itemsize + 4 * m * out_cols)
    return _call(kern, name=name, grid=(m // tm,),
                 in_specs=[BS((g, tm, ac), lambda i: (0, i, 0)), BS(w.shape, lambda i: (0, 0, 0))],
                 out_specs=BS((tm, out_cols), lambda i: (i, 0)), out_shape=jax.ShapeDtypeStruct((m, out_cols), F32),
                 cost_estimate=cost, compiler_params=_params(1, VMEM_BIG))(a, w)


def _bmm_nt_sum(name, a, w):
    return _block_sum(name, a, w, NT, w.shape[1])


def _bmm_nn_sum(name, a, w):
    return _block_sum(name, a, w, NN, w.shape[2])


def _rstd(x):
    return lax.rsqrt(jnp.mean(x * x, axis=-1, keepdims=True) + EPS)


def _norm_fwd(x, g, after=None):
    rows = x.shape[0]
    tm = _tile(rows, 512)

    def kern(x_ref, g_ref, *rest):
        xv = x_ref[...]
        rest[-1][...] = ((xv * _rstd(xv)) * g_ref[...]).astype(BF16)

    extra = () if after is None else (after,)
    return _call(kern, name="norm_fwd", grid=(rows // tm,),
                          in_specs=[BS((tm, DM), lambda i: (i, 0)), BS((1, DM), lambda i: (0, 0))] + [ANY] * len(extra),
                          out_specs=BS((tm, DM), lambda i: (i, 0)),
                          out_shape=jax.ShapeDtypeStruct((rows, DM), BF16), compiler_params=_params(1))(x, g, *extra)


def _norm_res(x, y, g):
    rows = x.shape[0]
    tm = _tile(rows, 512)

    def kern(x_ref, y_ref, g_ref, o_ref):
        yv = y_ref[...]
        o_ref[...] = x_ref[...] + (yv * _rstd(yv)) * g_ref[...]

    row = BS((tm, DM), lambda i: (i, 0))
    return _call(kern, name="norm_res", grid=(rows // tm,),
                          in_specs=[row, row, BS((1, DM), lambda i: (0, 0))], out_specs=row,
                          out_shape=jax.ShapeDtypeStruct((rows, DM), F32), compiler_params=_params(1))(x, y, g)


def _norm_bwd(z, dout, g, resid, out_dtype, after=None):
    rows = z.shape[0]
    tm = _tile(rows, 512)
    has_res = resid is not None

    def kern(*refs):
        z_ref, d_ref, g_ref = refs[:3]
        r_ref = refs[3] if has_res else None
        dz_ref, dg_ref = refs[-2:]
        zv = z_ref[...]
        dv = d_ref[...].astype(F32)
        r = _rstd(zv)
        zh = zv * r
        dzh = dv * g_ref[...]
        dz = r * (dzh - zh * jnp.mean(dzh * zh, axis=-1, keepdims=True))
        if has_res:
            dz = dz + r_ref[...]
        dz_ref[...] = dz.astype(dz_ref.dtype)
        part = jnp.sum(dv * zh, axis=0, keepdims=True)

        @pl.when(pl.program_id(0) == 0)
        def _():
            dg_ref[...] = part

        @pl.when(pl.program_id(0) > 0)
        def _():
            dg_ref[...] += part

    row = BS((tm, DM), lambda i: (i, 0))
    vec = BS((1, DM), lambda i: (0, 0))
    ins = [row, row, vec] + ([row] if has_res else []) + ([ANY] if after is not None else [])
    args = (z, dout, g) + ((resid,) if has_res else ()) + ((after,) if after is not None else ())
    return _call(kern, name="norm_bwd_res" if has_res else "norm_bwd", grid=(rows // tm,), in_specs=ins,
                          out_specs=[row, vec],
                          out_shape=[jax.ShapeDtypeStruct((rows, DM), out_dtype), jax.ShapeDtypeStruct((1, DM), F32)],
                          compiler_params=_params(1))(*args)


def _ffn_up(h, wgu4):
    s = h.shape[0]
    tm = _tile(s, 512)

    def kern(h_ref, w_ref, gu_ref, a_ref):
        hv = h_ref[...]
        gate = lax.dot_general(hv, w_ref[0], (NT, ((), ())), preferred_element_type=F32)
        up = lax.dot_general(hv, w_ref[1], (NT, ((), ())), preferred_element_type=F32)
        gu_ref[0] = gate.astype(BF16)
        gu_ref[1] = up.astype(BF16)
        a_ref[...] = (gate * jax.nn.sigmoid(gate) * up).astype(BF16)

    return _call(
        kern, name="ffn_up", grid=(4, s // tm),
        in_specs=[BS((tm, DM), lambda j, i: (i, 0)), BS((2, None, FFB, DM), lambda j, i: (0, j, 0, 0))],
        out_specs=[BS((2, None, tm, FFB), lambda j, i: (0, j, i, 0)), BS((None, tm, FFB), lambda j, i: (j, i, 0))],
        out_shape=[jax.ShapeDtypeStruct((2, 4, s, FFB), BF16), jax.ShapeDtypeStruct((4, s, FFB), BF16)],
        cost_estimate=pl.CostEstimate(flops=4 * s * DM * D_FF, transcendentals=s * D_FF,
                                      bytes_accessed=2 * (s * DM + 2 * D_FF * DM + 3 * s * D_FF)),
        compiler_params=_params(2))(h, wgu4)


def _ffn_da(dy, wd4, gu):
    s = dy.shape[0]
    tm = _tile(s, 512)

    def kern(dy_ref, w_ref, gu_ref, o_ref):
        da = lax.dot_general(dy_ref[...], w_ref[...], (NT, ((), ())), preferred_element_type=F32)
        gate = gu_ref[0].astype(F32)
        up = gu_ref[1].astype(F32)
        sg = jax.nn.sigmoid(gate)
        o_ref[0] = (da * up * (sg * (1.0 + gate * (1.0 - sg)))).astype(BF16)
        o_ref[1] = (da * (gate * sg)).astype(BF16)

    blk = BS((2, None, tm, FFB), lambda j, i: (0, j, i, 0))
    return _call(
        kern, name="ffn_da", grid=(4, s // tm),
        in_specs=[BS((tm, DM), lambda j, i: (i, 0)), BS((None, FFB, DM), lambda j, i: (j, 0, 0)), blk],
        out_specs=blk, out_shape=jax.ShapeDtypeStruct((2, 4, s, FFB), BF16),
        cost_estimate=pl.CostEstimate(flops=2 * s * DM * D_FF, transcendentals=s * D_FF,
                                      bytes_accessed=2 * (s * DM + D_FF * DM + 4 * s * D_FF)),
        compiler_params=_params(2))(dy, wd4, gu)


def _ffn_fwd(x, gpre, gpost, wgu, wd):
    h = _norm_fwd(x, gpre)
    gu, a = _ffn_up(h, wgu.reshape(2, 4, FFB, DM))
    y = _bmm_nn_sum("ffn_down", a, wd.reshape(4, FFB, DM))
    return _norm_res(x, y, gpost), (x, h, gu, a, y)


def _ffn_bwd(dxo, saved, gpre, gpost, wgu, wd, after=None):
    x, h, gu, a, y = saved
    s = x.shape[0]
    dy, dgpost = _norm_bwd(y, dxo, gpost, None, BF16, after)
    dgu = _ffn_da(dy, wd.reshape(4, FFB, DM), gu).reshape(8, s, FFB)
    dwd = _bmm_tn_a3("ffn_dwd", a, dy)
    dwgu = _bmm_tn_a3("ffn_dwgu", dgu, h)
    dh = _bmm_nn_sum("ffn_dh", dgu, wgu)
    dx, dgpre = _norm_bwd(x, dh, gpre, dxo, F32)
    return dx, dgpre, dgpost, dwgu, dwd.reshape(D_FF, DM)


def _bmm_tn_a3(name, a, b):
    g, m, k = a.shape
    n = b.shape[1]
    tm = _tile(m, TK_RED)
    return _mm(name, a, b, grid=(g, m // tm), a_spec=BS((None, tm, k), lambda q, r: (q, r, 0)),
               b_spec=BS((tm, n), lambda q, r: (r, 0)), o_spec=BS((None, k, n), lambda q, r: (q, 0, 0)),
               out_shape=(g, k, n), dn=TN)


def _softmax_rows(s):
    m = jnp.max(s, axis=-1, keepdims=True)
    p = jnp.exp(s - m)
    return p / jnp.sum(p, axis=-1, keepdims=True)


def _xattn_fwd_call(h, wq, kv):
    s = h.shape[0]
    mlen = kv.shape[1]
    tm = _tile(s, 512)
    scale = MEM_HD ** -0.5

    def kern(h_ref, w_ref, k_ref, v_ref, q_ref, o_ref):
        q = jnp.dot(h_ref[...], w_ref[...], preferred_element_type=F32).astype(BF16)
        q_ref[...] = q
        sc = lax.dot_general(q, k_ref[...], (NT, ((), ())), preferred_element_type=F32) * scale
        p = _softmax_rows(sc)
        o_ref[...] = jnp.dot(p.astype(BF16), v_ref[...], preferred_element_type=F32).astype(BF16)

    blk = BS((tm, MEM_HD), lambda i, hd: (i, hd))
    return _call(
        kern, name="xattn_fwd", grid=(s // tm, MEM_H),
        in_specs=[BS((tm, DM), lambda i, hd: (i, 0)), BS((DM, MEM_HD), lambda i, hd: (0, hd)),
                  BS((None, mlen, MEM_HD), lambda i, hd: (hd, 0, 0)),
                  BS((None, mlen, MEM_HD), lambda i, hd: (MEM_H + hd, 0, 0))],
        out_specs=[blk, blk],
        out_shape=[jax.ShapeDtypeStruct((s, DM), BF16), jax.ShapeDtypeStruct((s, DM), BF16)],
        compiler_params=_params(2))(h, wq, kv, kv)


def _xattn_bwd_call(q, kv, do):
    s = q.shape[0]
    mlen = kv.shape[1]
    tm = _tile(s, 512)
    scale = MEM_HD ** -0.5

    def kern(q_ref, k_ref, v_ref, do_ref, dq_ref, dkv_ref):
        qv, kvv, vv, dov = q_ref[...], k_ref[...], v_ref[...], do_ref[...]
        sc = lax.dot_general(qv, kvv, (NT, ((), ())), preferred_element_type=F32) * scale
        p = _softmax_rows(sc)
        dp = lax.dot_general(dov, vv, (NT, ((), ())), preferred_element_type=F32)
        ds = (p * (dp - jnp.sum(dp * p, axis=-1, keepdims=True)) * scale).astype(BF16)
        dq_ref[...] = jnp.dot(ds, kvv, preferred_element_type=F32).astype(BF16)
        dk = lax.dot_general(ds, qv, (TN, ((), ())), preferred_element_type=F32)
        dv = lax.dot_general(p.astype(BF16), dov, (TN, ((), ())), preferred_element_type=F32)

        @pl.when(pl.program_id(1) == 0)
        def _():
            dkv_ref[0] = dk
            dkv_ref[1] = dv

        @pl.when(pl.program_id(1) > 0)
        def _():
            dkv_ref[0] += dk
            dkv_ref[1] += dv

    blk = BS((tm, MEM_HD), lambda hd, i: (i, hd))
    return _call(
        kern, name="xattn_bwd", grid=(MEM_H, s // tm),
        in_specs=[blk, BS((None, mlen, MEM_HD), lambda hd, i: (hd, 0, 0)),
                  BS((None, mlen, MEM_HD), lambda hd, i: (MEM_H + hd, 0, 0)), blk],
        out_specs=[blk, BS((2, None, mlen, MEM_HD), lambda hd, i: (0, hd, 0, 0))],
        out_shape=[jax.ShapeDtypeStruct((s, DM), BF16), jax.ShapeDtypeStruct((2, MEM_H, mlen, MEM_HD), F32)],
        compiler_params=_params(2))(q, kv, kv, do)


def _cross_fwd(x, mem, gpre, gmem, gpost, wq, wkv, wo, after=None):
    h = _norm_fwd(x, gpre, after)
    mn = _norm_fwd(mem, gmem)
    kv = _bmm_nn("xattn_kv", mn, wkv, BF16)
    q, o = _xattn_fwd_call(h, wq, kv)
    y = _mm_nn("xattn_out", o, wo)
    return _norm_res(x, y, gpost), (x, h, mn, kv, q, o, y)


def _cross_bwd(dxo, saved, mem, gpre, gmem, gpost, wq, wkv, wo, after=None):
    x, h, mn, kv, q, o, y = saved
    mlen = mem.shape[0]
    dy, dgpost = _norm_bwd(y, dxo, gpost, None, BF16, after)
    do = _mm_nt("xattn_do", dy, wo, BF16)
    dwo = _mm_tn("xattn_dwo", o, dy)
    dq, dkv = _xattn_bwd_call(q, kv, do)
    dwq = _mm_tn("xattn_dwq", h, dq)
    dh = _mm_nt("xattn_dh", dq, wq)
    dkv8 = dkv.reshape(8, mlen, MEM_HD)
    dwkv = _bmm_tn("xattn_dwkv", mn, dkv8)
    dmn = _bmm_nt_sum("xattn_dmn", dkv8, wkv)
    _, dgmem = _norm_bwd(mem, dmn, gmem, None, BF16)
    dx, dgpre = _norm_bwd(x, dh, gpre, dxo, F32)
    return dx, dgpre, dgmem, dgpost, dwq, dwkv, dwo


def _log_sigmoid(z):
    return jnp.minimum(z, 0.0) - jnp.log1p(jnp.exp(-jnp.abs(z)))


def _lane_scan_steps():
    return (1, 2, 4, 8, 16, 32, 64)


def _fox_cum(frow, bfb):
    s = frow.shape[1]

    def kern(f_ref, b_ref, o_ref):
        lane = lax.broadcasted_iota(jnp.int32, (FOX_H, LANE), 1)
        carry = jnp.zeros((FOX_H, 1), F32)
        for c in range(s // LANE):
            sl = slice(c * LANE, (c + 1) * LANE)
            lf = _log_sigmoid(f_ref[:, sl] + b_ref[...])
            v = lf
            for d in _lane_scan_steps():
                v = v + jnp.where(lane >= d, pltpu.roll(v, d, 1), 0.0)
            o_ref[:, sl] = v + carry
            carry = carry + jnp.sum(lf, axis=1, keepdims=True)

    return _call(kern, name="fox_cum", out_shape=jax.ShapeDtypeStruct((FOX_H, s), F32),
                          compiler_params=pltpu.CompilerParams(vmem_limit_bytes=VMEM_LIMIT))(frow, bfb)


def _fox_dlogf(dcq, dck, frow, bfb):
    s = frow.shape[1]

    def kern(q_ref, d_ref, f_ref, b_ref, df_ref, db_ref):
        lane = lax.broadcasted_iota(jnp.int32, (FOX_H, LANE), 1)
        carry = jnp.zeros((FOX_H, 1), F32)
        dbf = jnp.zeros((FOX_H, 1), F32)
        for c in reversed(range(s // LANE)):
            sl = slice(c * LANE, (c + 1) * LANE)
            dc = q_ref[:, sl] - d_ref[:, sl]
            v = dc
            for d in _lane_scan_steps():
                v = v + jnp.where(lane < LANE - d, pltpu.roll(v, LANE - d, 1), 0.0)
            v = v + carry
            carry = carry + jnp.sum(dc, axis=1, keepdims=True)
            df = v * jax.nn.sigmoid(-(f_ref[:, sl] + b_ref[...]))
            df_ref[:, sl] = df
            dbf = dbf + jnp.sum(df, axis=1, keepdims=True)
        db_ref[...] = jnp.broadcast_to(dbf, (FOX_H, LANE))

    return _call(kern, name="fox_dlogf",
                          out_shape=[jax.ShapeDtypeStruct((FOX_H, s), F32), jax.ShapeDtypeStruct((FOX_H, LANE), F32)],
                          compiler_params=pltpu.CompilerParams(vmem_limit_bytes=VMEM_LIMIT))(dcq, dck, frow, bfb)


FOX_TQ = 512
Q_COL, K_COL, V_COL = 0, FOX_W // LANE, 2 * FOX_W // LANE
B_COL = 3 * FOX_W // LANE
C_COL = B_COL + SC_W // LANE
U_COL = C_COL + SC_W // LANE


def _bf16_terms(c):
    hi = c.astype(BF16).astype(F32)
    mid = (c - hi).astype(BF16).astype(F32)
    return hi, mid, (c - hi - mid).astype(BF16).astype(F32)


def _fox_operands(qv, kv, cq, ck, lane, hh, scale):
    sel = (lane < FOX_HD) if hh == 0 else (lane >= FOX_HD)
    b0 = FOX_HD if hh == 0 else 0
    qa = jnp.where(sel, qv * scale, 0.0)
    ka = jnp.where(sel, kv, 0.0)
    for n, (tq_, tk_) in enumerate(zip(_bf16_terms(cq), _bf16_terms(ck))):
        qa = jnp.where(lane == b0 + n, tq_, jnp.where(lane == b0 + 3 + n, 1.0, qa))
        ka = jnp.where(lane == b0 + n, 1.0, jnp.where(lane == b0 + 3 + n, -tk_, ka))
    return sel, qa.astype(BF16), ka.astype(BF16)


def _fox_logits(qa, ka, causal):
    sc = lax.dot_general(qa, ka, (NT, ((), ())), preferred_element_type=F32)
    return sc if causal is None else jnp.where(causal, sc, NEG)


def _fox_prep(proj, cumc):
    s = proj.shape[0]
    tp = _tile(s, 512)
    scale = FOX_HD ** -0.5

    def kern(q_ref, k_ref, c_ref, qa_ref, ka_ref):
        lane = lax.broadcasted_iota(jnp.int32, (tp, LANE), 1)
        for hh in range(2):
            _, qa_ref[hh], ka_ref[hh] = _fox_operands(q_ref[...], k_ref[...], c_ref[hh], c_ref[hh], lane, hh, scale)

    pair = BS((2, tp, LANE), lambda hp, i: (hp, i, 0))
    shp = jax.ShapeDtypeStruct((FOX_H, s, LANE), BF16)
    return _call(kern, name="fox_prep", grid=(4, s // tp),
                 in_specs=[BS((tp, LANE), lambda hp, i: (i, Q_COL + hp)), BS((tp, LANE), lambda hp, i: (i, K_COL + hp)), pair],
                 out_specs=[pair, pair], out_shape=[shp, shp], compiler_params=_params(2))(proj, proj, cumc)


def _fox_fwd_call(proj, qa, ka):
    s = proj.shape[0]
    tq = _tile(s, FOX_TQ)
    nq = s // tq

    def kern(qa_ref, ka_ref, v_ref, o_ref, lse_ref, m_s, l_s, acc_s):
        i = pl.program_id(1)
        j = pl.program_id(2)
        lane = lax.broadcasted_iota(jnp.int32, (tq, LANE), 1)

        @pl.when(j == 0)
        def _():
            m_s[...] = jnp.full(m_s.shape, NEG, F32)
            l_s[...] = jnp.zeros(l_s.shape, F32)
            acc_s[...] = jnp.zeros(acc_s.shape, F32)

        def step(diagonal):
            vb = v_ref[...].astype(BF16)
            causal = (lax.broadcasted_iota(jnp.int32, (tq, tq), 0) >= lax.broadcasted_iota(jnp.int32, (tq, tq), 1)
                      if diagonal else None)
            for hh in range(2):
                sc = _fox_logits(qa_ref[hh], ka_ref[hh], causal)
                m_prev = m_s[hh]
                m_new = jnp.maximum(m_prev, jnp.max(sc, axis=-1, keepdims=True))
                alpha = jnp.exp(m_prev - m_new)
                p = jnp.exp(sc - m_new)
                l_s[hh] = alpha * l_s[hh] + jnp.sum(p, axis=-1, keepdims=True)
                acc_s[hh] = alpha * acc_s[hh] + jnp.dot(p.astype(BF16), vb, preferred_element_type=F32)
                m_s[hh] = m_new

        @pl.when(j < i)
        def _():
            step(False)

        @pl.when(j == i)
        def _():
            step(True)
            o_ref[...] = jnp.where(lane < FOX_HD, acc_s[0] / l_s[0], acc_s[1] / l_s[1])
            for hh in range(2):
                lse_ref[hh] = jnp.broadcast_to(m_s[hh] + jnp.log(l_s[hh]), (tq, LANE))

    kvi = lambda hp, i, j: jnp.minimum(j, i)
    return _call(
        kern, name="fox_fwd", grid=(4, nq, nq),
        in_specs=[BS((2, tq, LANE), lambda hp, i, j: (hp, i, 0)),
                  BS((2, tq, LANE), lambda hp, i, j: (hp, kvi(hp, i, j), 0)),
                  BS((tq, LANE), lambda hp, i, j: (kvi(hp, i, j), V_COL + hp))],
        out_specs=[BS((tq, LANE), lambda hp, i, j: (i, hp)), BS((2, tq, LANE), lambda hp, i, j: (hp, i, 0))],
        out_shape=[jax.ShapeDtypeStruct((s, FOX_W), F32), jax.ShapeDtypeStruct((FOX_H, s, LANE), F32)],
        scratch_shapes=[pltpu.VMEM((2, tq, 1), F32), pltpu.VMEM((2, tq, 1), F32), pltpu.VMEM((2, tq, LANE), F32)],
        compiler_params=_params(3))(qa, ka, proj)


ROWSUM_M = 16


def _fox_bwd_call(proj, o, lse, dcat, qa, ka):
    s = proj.shape[0]
    tq = _tile(s, FOX_TQ)
    nq = s // tq
    reps = tq // LANE
    scale = FOX_HD ** -0.5

    def kern(qa_ref, ka_ref, v_ref, do_ref, o_ref, lse_ref, dq_ref, dk_ref, dv_ref, dck_ref, dcq_ref):
        j = pl.program_id(1)
        i = pl.program_id(2)
        lane = lax.broadcasted_iota(jnp.int32, (tq, LANE), 1)
        ones = jnp.ones((ROWSUM_M, tq), BF16)

        @pl.when((j == 0) & (i == 0))
        def _():
            dq_ref[...] = jnp.zeros(dq_ref.shape, F32)
            dcq_ref[...] = jnp.zeros(dcq_ref.shape, F32)

        @pl.when(i == j)
        def _():
            dk_ref[...] = jnp.zeros(dk_ref.shape, F32)
            dv_ref[...] = jnp.zeros(dv_ref.shape, F32)
            dck_ref[...] = jnp.zeros(dck_ref.shape, F32)

        def step(diagonal):
            dov = do_ref[...]
            ov = o_ref[...]
            vb = v_ref[...].astype(BF16)
            causal = (lax.broadcasted_iota(jnp.int32, (tq, tq), 0) >= lax.broadcasted_iota(jnp.int32, (tq, tq), 1)
                      if diagonal else None)
            dq_t = jnp.zeros((tq, LANE), F32)
            dk_t = jnp.zeros((tq, LANE), F32)
            dv_t = jnp.zeros((tq, LANE), F32)
            for hh in range(2):
                sel = (lane < FOX_HD) if hh == 0 else (lane >= FOX_HD)
                qa, ka = qa_ref[hh], ka_ref[hh]
                dom32 = jnp.where(sel, dov, 0.0)
                dom = dom32.astype(BF16)
                sc = _fox_logits(qa, ka, causal)
                p = jnp.exp(sc - jnp.tile(lse_ref[hh], (1, reps)))
                dp = lax.dot_general(dom, vb, (NT, ((), ())), preferred_element_type=F32)
                delta = jnp.sum(dom32 * ov, axis=-1, keepdims=True)
                ds = p * (dp - delta)
                dsb = ds.astype(BF16)
                dq_t = jnp.where(sel, jnp.dot(dsb, ka, preferred_element_type=F32) * scale, dq_t)
                dk_t = jnp.where(sel, lax.dot_general(dsb, qa, (TN, ((), ())), preferred_element_type=F32), dk_t)
                dv_t = dv_t + lax.dot_general(p.astype(BF16), dom, (TN, ((), ())), preferred_element_type=F32)
                dck_ref[hh] += jnp.sum(ds, axis=0, keepdims=True)
                ds_lo = (ds - dsb.astype(F32)).astype(BF16)
                dcq_ref[hh, i] += (lax.dot_general(ones, dsb, (NT, ((), ())), preferred_element_type=F32)
                                   + lax.dot_general(ones, ds_lo, (NT, ((), ())), preferred_element_type=F32))
            rows = pl.ds(pl.multiple_of(i * tq, tq), tq)
            dq_ref[rows, :] += dq_t
            dk_ref[...] += dk_t
            dv_ref[...] += dv_t

        @pl.when(i > j)
        def _():
            step(False)

        @pl.when(i == j)
        def _():
            step(True)

    qi = lambda hp, j, i: jnp.maximum(i, j)
    return _call(
        kern, name="fox_bwd", grid=(4, nq, nq),
        in_specs=[BS((2, tq, LANE), lambda hp, j, i: (hp, qi(hp, j, i), 0)),
                  BS((2, tq, LANE), lambda hp, j, i: (hp, j, 0)),
                  BS((tq, LANE), lambda hp, j, i: (j, V_COL + hp)),
                  BS((tq, LANE), lambda hp, j, i: (qi(hp, j, i), hp)),
                  BS((tq, LANE), lambda hp, j, i: (qi(hp, j, i), hp)),
                  BS((2, tq, LANE), lambda hp, j, i: (hp, qi(hp, j, i), 0))],
        out_specs=[BS((s, LANE), lambda hp, j, i: (0, hp)), BS((tq, LANE), lambda hp, j, i: (j, hp)),
                   BS((tq, LANE), lambda hp, j, i: (j, hp)), BS((2, 1, tq), lambda hp, j, i: (hp, 0, j)),
                   BS((2, nq, ROWSUM_M, tq), lambda hp, j, i: (hp, 0, 0, 0))],
        out_shape=[jax.ShapeDtypeStruct((s, FOX_W), F32), jax.ShapeDtypeStruct((s, FOX_W), F32),
                   jax.ShapeDtypeStruct((s, FOX_W), F32), jax.ShapeDtypeStruct((FOX_H, 1, s), F32),
                   jax.ShapeDtypeStruct((FOX_H, nq, ROWSUM_M, tq), F32)],
        compiler_params=_params(3))(qa, ka, proj, dcat, o, lse)


def _shift_down(v, d, row):
    return jnp.where(row >= d, pltpu.roll(v, d, 0), 0.0)


def _shift_up(v, d, row, n):
    return jnp.where(row < n - d, pltpu.roll(v, n - d, 0), 0.0)


def _sconv_fwd(proj, convw):
    s = proj.shape[0]

    def kern(b_ref, c_ref, u_ref, w_ref, y_ref):
        row = lax.broadcasted_iota(jnp.int32, (s, LANE), 0)
        z = c_ref[...] * u_ref[...]
        conv = w_ref[2:3, :] * z + w_ref[1:2, :] * _shift_down(z, 1, row) + w_ref[0:1, :] * _shift_down(z, 2, row)
        y_ref[...] = (b_ref[...] * conv).astype(BF16)

    col = lambda base: BS((s, LANE), lambda cb: (0, base + cb))
    return _call(kern, name="sconv_fwd", grid=(SC_W // LANE,),
                          in_specs=[col(B_COL), col(C_COL), col(U_COL), BS((SC_K, LANE), lambda cb: (0, cb))],
                          out_specs=BS((s, LANE), lambda cb: (0, cb)),
                          out_shape=jax.ShapeDtypeStruct((s, SC_W), BF16), compiler_params=_params(1))(proj, proj, proj, convw)


def _sconv_bwd(proj, convw, dcat):
    s = proj.shape[0]

    def kern(b_ref, c_ref, u_ref, w_ref, dy_ref, db_ref, dc_ref, du_ref, dw_ref):
        row = lax.broadcasted_iota(jnp.int32, (s, LANE), 0)
        cv, uv, dyv = c_ref[...], u_ref[...], dy_ref[...]
        z = cv * uv
        z1 = _shift_down(z, 1, row)
        z2 = _shift_down(z, 2, row)
        conv = w_ref[2:3, :] * z + w_ref[1:2, :] * z1 + w_ref[0:1, :] * z2
        db_ref[...] = dyv * conv
        dcv = dyv * b_ref[...]
        dz = w_ref[2:3, :] * dcv + w_ref[1:2, :] * _shift_up(dcv, 1, row, s) + w_ref[0:1, :] * _shift_up(dcv, 2, row, s)
        dc_ref[...] = dz * uv
        du_ref[...] = dz * cv
        dw_ref[0:1, :] = jnp.sum(dcv * z2, axis=0, keepdims=True)
        dw_ref[1:2, :] = jnp.sum(dcv * z1, axis=0, keepdims=True)
        dw_ref[2:3, :] = jnp.sum(dcv * z, axis=0, keepdims=True)

    col = lambda base: BS((s, LANE), lambda cb: (0, base + cb))
    out = BS((s, LANE), lambda cb: (0, cb))
    wspec = BS((SC_K, LANE), lambda cb: (0, cb))
    act = jax.ShapeDtypeStruct((s, SC_W), F32)
    return _call(kern, name="sconv_bwd", grid=(SC_W // LANE,),
                          in_specs=[col(B_COL), col(C_COL), col(U_COL), wspec, col(FOX_W // LANE)],
                          out_specs=[out, out, out, wspec],
                          out_shape=[act, act, act, jax.ShapeDtypeStruct((SC_K, SC_W), F32)],
                          compiler_params=_params(1))(proj, proj, proj, convw, dcat)


def _fox_layer_fwd(x, gpre, gpost, wall, bfb, convw, wout, after=None):
    s = x.shape[0]
    h = _norm_fwd(x, gpre, after)
    proj = _mm_nt_cols("fox_proj", h, wall, AB_PAD // 5)
    frow = proj[:, 3 * FOX_W + 3 * SC_W:3 * FOX_W + 3 * SC_W + FOX_H].T
    cumr = _fox_cum(frow, bfb)
    qa, ka = _fox_prep(proj, jnp.broadcast_to(cumr[:, :, None], (FOX_H, s, LANE)))
    o, lse = _fox_fwd_call(proj, qa, ka)
    yb = _sconv_fwd(proj, convw)
    cat = jnp.concatenate([o.astype(BF16), yb], axis=1)
    y = _mm_nn("fox_out", cat, wout)
    return _norm_res(x, y, gpost), (x, h, proj, frow, qa, ka, o, lse, cat, y)


def _fox_layer_bwd(dxo, saved, gpre, gpost, wall, bfb, convw, wout, after=None):
    x, h, proj, frow, qa, ka, o, lse, cat, y = saved
    s = x.shape[0]
    dy, dgpost = _norm_bwd(y, dxo, gpost, None, BF16, after)
    dcat = _mm_nt("fox_dcat", dy, wout)
    dwout = _mm_tn("fox_dwout", cat, dy)
    db, dc, du, dconvw = _sconv_bwd(proj, convw, dcat)
    dq, dk, dv, dck, dcq = _fox_bwd_call(proj, o, lse, dcat, qa, ka)
    dfrow, dbf = _fox_dlogf(dcq[:, :, 0, :].reshape(FOX_H, s), dck.reshape(FOX_H, s), frow, bfb)
    dfcol = jnp.pad(dfrow.T, ((0, 0), (0, LANE - FOX_H)))
    dproj = jnp.concatenate([dq, dk, dv, db, dc, du, dfcol], axis=1).astype(BF16)
    dwall = _mm_tn_rows("fox_dwall", dproj, h, AB_PAD // 5)
    dh = _mm_nn("fox_dh", dproj, wall, vmem=VMEM_BIG)
    dx, dgpre = _norm_bwd(x, dh, gpre, dxo, F32)
    return dx, dgpre, dgpost, dwall, dbf[:, 0], dconvw, dwout


def _ab_pack(wt):
    nf = 3 * FOX_W
    return jnp.concatenate([wt[:nf], wt[nf + FOX_H:], wt[nf:nf + FOX_H],
                            jnp.zeros((AB_PAD - AB_IN, wt.shape[1]), wt.dtype)], axis=0)


def _ab_unpack(wt):
    nf = 3 * FOX_W
    nbcu = 3 * SC_W
    return jnp.concatenate([wt[:nf], wt[nf + nbcu:nf + nbcu + FOX_H], wt[nf:nf + nbcu]], axis=0)


NCH = DM // LANE
CH_PER_BLK = LRU_BW // LANE


def _chunk_spec(s, lead=0):
    return BS((None, s, LANE), lambda ch: (lead + ch // CH_PER_BLK, 0, ch % CH_PER_BLK))


def _vec_chunk(rows):
    return BS((rows, LANE), lambda ch: (0, ch))


def _neg_expm1(x):
    series = -x * (1.0 + x * (1 / 2) * (1.0 + x * (1 / 3) * (1.0 + x * (1 / 4) * (1.0 + x * (1 / 5) * (
        1.0 + x * (1 / 6) * (1.0 + x * (1 / 7)))))))
    return jnp.where(x > -0.25, series, 1.0 - jnp.exp(x))


def _softplus(z):
    return jnp.maximum(z, 0.0) + jnp.log1p(jnp.exp(-jnp.abs(z)))


GELU_C = math.sqrt(2.0 / math.pi)
GELU_A = 0.044715


def _gelu(x):
    return 0.5 * x * (1.0 + jnp.tanh(GELU_C * (x + GELU_A * x * x * x)))


def _gelu_grad(x):
    t = jnp.tanh(GELU_C * (x + GELU_A * x * x * x))
    return 0.5 * (1.0 + t) + 0.5 * x * (1.0 - t * t) * GELU_C * (1.0 + 3.0 * GELU_A * x * x)


def _lru_conv_fwd(gu, convw, convb):
    s = gu.shape[1]

    def kern(x_ref, w_ref, b_ref, u_ref):
        row = lax.broadcasted_iota(jnp.int32, (s, LANE), 0)
        xv = x_ref[...]
        u_ref[...] = (b_ref[...] + w_ref[3:4, :] * xv + w_ref[2:3, :] * _shift_down(xv, 1, row)
                      + w_ref[1:2, :] * _shift_down(xv, 2, row) + w_ref[0:1, :] * _shift_down(xv, 3, row))

    return _call(kern, name="lru_conv_fwd", grid=(NCH,),
                          in_specs=[_chunk_spec(s, LRU_NB), _vec_chunk(RG_K), _vec_chunk(1)], out_specs=_chunk_spec(s),
                          out_shape=jax.ShapeDtypeStruct((LRU_NB, s, LRU_BW), F32), compiler_params=_params(1))(gu, convw, convb)


def _lru_conv_bwd(dud, dug, gu, convw):
    s = gu.shape[1]

    def kern(d1_ref, d2_ref, x_ref, w_ref, dx_ref, dw_ref, db_ref):
        row = lax.broadcasted_iota(jnp.int32, (s, LANE), 0)
        du = d1_ref[...] + d2_ref[...]
        xv = x_ref[...]
        dx_ref[...] = (w_ref[3:4, :] * du + w_ref[2:3, :] * _shift_up(du, 1, row, s) + w_ref[1:2, :] * _shift_up(du, 2, row, s)
                       + w_ref[0:1, :] * _shift_up(du, 3, row, s)).astype(BF16)
        dw_ref[3:4, :] = jnp.sum(du * xv, axis=0, keepdims=True)
        for k in range(1, RG_K):
            dw_ref[3 - k:4 - k, :] = jnp.sum(du * _shift_down(xv, k, row), axis=0, keepdims=True)
        db_ref[...] = jnp.sum(du, axis=0, keepdims=True)

    return _call(kern, name="lru_conv_bwd", grid=(NCH,),
                          in_specs=[_chunk_spec(s), _chunk_spec(s), _chunk_spec(s, LRU_NB), _vec_chunk(RG_K)],
                          out_specs=[_chunk_spec(s), _vec_chunk(RG_K), _vec_chunk(1)],
                          out_shape=[jax.ShapeDtypeStruct((LRU_NB, s, LRU_BW), BF16),
                                     jax.ShapeDtypeStruct((RG_K, DM), F32), jax.ShapeDtypeStruct((1, DM), F32)],
                          compiler_params=_params(1))(dud, dug, gu, convw)


def _lru_gates(z_ref, bai_ref, lam_ref, uv):
    r = jax.nn.sigmoid(z_ref[0] + bai_ref[0:1, :])
    ig = jax.nn.sigmoid(z_ref[1] + bai_ref[1:2, :])
    sp = _softplus(-lam_ref[...])
    la = -RG_C * r * sp
    a = jnp.exp(la)
    sq = jnp.sqrt(_neg_expm1(2.0 * la))
    return r, ig, sp, a, sq


def _scan_steps(n):
    d, out = 1, []
    while d < n:
        out.append(d)
        d *= 2
    return out


def _lru_scan_fwd(z, bai, lam, u, gu):
    s = u.shape[1]
    zspec = BS((2, None, s, LANE), lambda ch: (0, ch // CH_PER_BLK, 0, ch % CH_PER_BLK))

    def kern(z_ref, bai_ref, lam_ref, u_ref, g_ref, hs_ref, y_ref):
        row = lax.broadcasted_iota(jnp.int32, (s, LANE), 0)
        uv = u_ref[...]
        _, ig, _, a, sq = _lru_gates(z_ref, bai_ref, lam_ref, uv)
        b = sq * (ig * uv)
        for d in _scan_steps(s):
            a_sh = jnp.where(row >= d, pltpu.roll(a, d, 0), 1.0)
            b = a * _shift_down(b, d, row) + b
            a = a * a_sh
        hs_ref[...] = b
        y_ref[...] = (_gelu(g_ref[...]) * b).astype(BF16)

    return _call(kern, name="lru_scan_fwd", grid=(NCH,),
                          in_specs=[zspec, _vec_chunk(2), _vec_chunk(1), _chunk_spec(s), _chunk_spec(s)],
                          out_specs=[_chunk_spec(s), BS((s, LANE), lambda ch: (0, ch))],
                          out_shape=[jax.ShapeDtypeStruct((LRU_NB, s, LRU_BW), F32), jax.ShapeDtypeStruct((s, DM), BF16)],
                          compiler_params=_params(1, VMEM_BIG))(z, bai, lam, u, gu)


def _lru_scan_bwd(dyp, z, bai, lam, u, gu, hs):
    s = u.shape[1]
    zspec = BS((2, None, s, LANE), lambda ch: (0, ch // CH_PER_BLK, 0, ch % CH_PER_BLK))

    def kern(dy_ref, z_ref, bai_ref, lam_ref, u_ref, g_ref, hs_ref, dg_ref, dz_ref, du_ref, dbai_ref, dlam_ref):
        row = lax.broadcasted_iota(jnp.int32, (s, LANE), 0)
        uv, gv, hv, dyv = u_ref[...], g_ref[...], hs_ref[...], dy_ref[...]
        r, ig, sp, a, sq = _lru_gates(z_ref, bai_ref, lam_ref, uv)
        dg_ref[...] = (dyv * hv * _gelu_grad(gv)).astype(BF16)
        g = dyv * _gelu(gv)
        an = _shift_up(a, 1, row, s)
        for d in _scan_steps(s):
            an_sh = jnp.where(row < s - d, pltpu.roll(an, s - d, 0), 1.0)
            g = an * _shift_up(g, d, row, s) + g
            an = an * an_sh
        da = g * _shift_down(hv, 1, row)
        dsq = g * (ig * uv)
        di = g * sq * uv
        du_ref[...] = g * sq * ig
        dla = da * a - dsq * (a * a / sq)
        dzr = dla * (-RG_C * sp) * r * (1.0 - r)
        dzi = di * ig * (1.0 - ig)
        dz_ref[0] = dzr.astype(BF16)
        dz_ref[1] = dzi.astype(BF16)
        dbai_ref[0:1, :] = jnp.sum(dzr, axis=0, keepdims=True)
        dbai_ref[1:2, :] = jnp.sum(dzi, axis=0, keepdims=True)
        dlam_ref[...] = jnp.sum(dla * r, axis=0, keepdims=True) * (RG_C * jax.nn.sigmoid(-lam_ref[...]))

    return _call(
        kern, name="lru_scan_bwd", grid=(NCH,),
        in_specs=[BS((s, LANE), lambda ch: (0, ch)), zspec, _vec_chunk(2), _vec_chunk(1), _chunk_spec(s), _chunk_spec(s),
                  _chunk_spec(s)],
        out_specs=[_chunk_spec(s), zspec, _chunk_spec(s), _vec_chunk(2), _vec_chunk(1)],
        out_shape=[jax.ShapeDtypeStruct((LRU_NB, s, LRU_BW), BF16), jax.ShapeDtypeStruct((2, LRU_NB, s, LRU_BW), BF16),
                   jax.ShapeDtypeStruct((LRU_NB, s, LRU_BW), F32), jax.ShapeDtypeStruct((2, DM), F32),
                   jax.ShapeDtypeStruct((1, DM), F32)],
        compiler_params=_params(1, VMEM_BIG))(dyp, z, bai, lam, u, gu, hs)


def _lru_layer_fwd(x, gpre, gpost, win, convw, convb, wai, bai, lam, wout, after=None):
    s = x.shape[0]
    tm = _tile(s, 512)
    h = _norm_fwd(x, gpre, after)
    gu = _bmm_nn("lru_in", h, win)
    u = _lru_conv_fwd(gu, convw, convb)
    z = _mm("lru_gate", u, wai, grid=(2, LRU_NB, s // tm, 1),
            a_spec=BS((None, tm, LRU_BW), lambda k, n, i, r: (n, i, 0)),
            b_spec=BS((None, None, LRU_BW, LRU_BW), lambda k, n, i, r: (k, n, 0, 0)),
            o_spec=BS((None, None, tm, LRU_BW), lambda k, n, i, r: (k, n, i, 0)),
            out_shape=(2, LRU_NB, s, LRU_BW), dn=NN)
    hs, yp = _lru_scan_fwd(z, bai, lam, u, gu)
    y = _mm_nn("lru_out", yp, wout)
    return _norm_res(x, y, gpost), (x, h, gu, u, z, hs, yp, y)


def _lru_layer_bwd(dxo, saved, gpre, gpost, win, convw, convb, wai, bai, lam, wout, after=None):
    x, h, gu, u, z, hs, yp, y = saved
    s = x.shape[0]
    tm = _tile(s, 512)
    dy, dgpost = _norm_bwd(y, dxo, gpost, None, BF16, after)
    dyp = _mm_nt("lru_dyp", dy, wout)
    dwout = _mm_tn("lru_dwout", yp, dy)
    dgate, dz, dud, dbai, dlam = _lru_scan_bwd(dyp, z, bai, lam, u, gu, hs)
    dwai = _mm("lru_dwai", u, dz, grid=(2, LRU_NB, s // tm),
               a_spec=BS((None, tm, LRU_BW), lambda k, n, r: (n, r, 0)),
               b_spec=BS((None, None, tm, LRU_BW), lambda k, n, r: (k, n, r, 0)),
               o_spec=BS((None, None, LRU_BW, LRU_BW), lambda k, n, r: (k, n, 0, 0)),
               out_shape=(2, LRU_NB, LRU_BW, LRU_BW), dn=TN)
    dug = _mm("lru_dug", dz, wai, grid=(LRU_NB, s // tm, 2),
              a_spec=BS((None, None, tm, LRU_BW), lambda n, i, k: (k, n, i, 0)),
              b_spec=BS((None, None, LRU_BW, LRU_BW), lambda n, i, k: (k, n, 0, 0)),
              o_spec=BS((None, tm, LRU_BW), lambda n, i, k: (n, i, 0)),
              out_shape=(LRU_NB, s, LRU_BW), dn=NT)
    duraw, dconvw, dconvb = _lru_conv_bwd(dud, dug, gu, convw)
    dgu = jnp.concatenate([dgate, duraw], axis=0)
    dwin = _bmm_tn("lru_dwin", h, dgu)
    dh = _bmm_nt_sum("lru_dh", dgu, win)
    dx, dgpre = _norm_bwd(x, dh, gpre, dxo, F32)
    return dx, dgpre, dgpost, dwin, dconvw, dconvb, dwai, dbai, dlam, dwout


CHIP_FLIPS = ((1, 0), (0, 1), (1, 1))


def _place():
    return lax.axis_index("x"), lax.axis_index("y"), lax.axis_index("c")


def _flip(v, f):
    return 1 - v if f else v


def _comm_params():
    return pltpu.CompilerParams(vmem_limit_bytes=VMEM_LIMIT)


def _small_gather(v):
    def body(v_ref, o_ref, send_sems, recv_sems, local_sem):
        x, y, c = _place()
        mine = 4 * x + 2 * y + c
        local = pltpu.make_async_copy(v_ref, o_ref.at[mine], local_sem)
        local.start()
        sends = []
        for k in range(1, NDEV):
            fx, fy, fc = (k >> 2) & 1, (k >> 1) & 1, k & 1
            sends.append(pltpu.make_async_remote_copy(
                src_ref=v_ref, dst_ref=o_ref.at[mine], send_sem=send_sems.at[k - 1], recv_sem=recv_sems.at[k - 1],
                device_id=(_flip(x, fx), _flip(y, fy), _flip(c, fc)), device_id_type=MESH))
        for cp in sends:
            cp.start()
        for k in range(1, NDEV):
            fx, fy, fc = (k >> 2) & 1, (k >> 1) & 1, k & 1
            src = 4 * _flip(x, fx) + 2 * _flip(y, fy) + _flip(c, fc)
            pltpu.make_async_remote_copy(src_ref=v_ref, dst_ref=o_ref.at[src], send_sem=send_sems.at[k - 1],
                                         recv_sem=recv_sems.at[k - 1], device_id=(x, y, c), device_id_type=MESH).wait_recv()
        for cp in sends:
            cp.wait_send()
        local.wait()

    return pl.pallas_call(body, name="small_gather", in_specs=[ANY], out_specs=ANY,
                          out_shape=jax.ShapeDtypeStruct((NDEV,) + v.shape, v.dtype),
                          scratch_shapes=[pltpu.SemaphoreType.DMA((NDEV - 1,)), pltpu.SemaphoreType.DMA((NDEV - 1,)),
                                          pltpu.SemaphoreType.DMA],
                          compiler_params=_comm_params())(v)


REL_CHIPS = ((0, 0),) + CHIP_FLIPS


def _rs_d2d(g5s, after=None):
    n = len(g5s)
    extra = () if after is None else (after,)

    def body(*refs):
        ins, gots = refs[:n], refs[n + len(extra):2 * n + len(extra)]
        send_sems, recv_sems = refs[2 * n + len(extra):]
        x, y, c = _place()
        copies = []
        for t in range(n):
            for f, (fx, fy) in enumerate(REL_CHIPS):
                copies.append(pltpu.make_async_remote_copy(
                    src_ref=ins[t].at[_flip(x, fx), _flip(y, fy), 1 - c], dst_ref=gots[t].at[f],
                    send_sem=send_sems.at[4 * t + f], recv_sem=recv_sems.at[4 * t + f], device_id=(x, y, 1 - c),
                    device_id_type=MESH))
        for cp in copies:
            cp.start()
        for cp in copies:
            cp.wait()

    out = [jax.ShapeDtypeStruct((4,) + g.shape[3:], F32) for g in g5s]
    return pl.pallas_call(body, name="rs_d2d", in_specs=[ANY] * (n + len(extra)), out_specs=[ANY] * n, out_shape=out,
                          scratch_shapes=[pltpu.SemaphoreType.DMA((4 * n,)), pltpu.SemaphoreType.DMA((4 * n,))],
                          compiler_params=_comm_params())(*g5s, *extra)


HBM = pl.BlockSpec(memory_space=pltpu.HBM)
SEM = pl.BlockSpec(memory_space=pltpu.SEMAPHORE)
EFFECT = pltpu.SideEffectType.DATAFLOW_SIDE_EFFECTING


def _in_hbm(a):
    return pltpu.with_memory_space_constraint(a, pltpu.HBM)


def _rs_ici_copies(ins, lands, send_sems, recv_sems):
    x, y, c = _place()
    return [pltpu.make_async_remote_copy(
        src_ref=ins[t].at[f], dst_ref=lands[t].at[f], send_sem=send_sems.at[3 * t + f], recv_sem=recv_sems.at[3 * t + f],
        device_id=(_flip(x, fx), _flip(y, fy), c), device_id_type=MESH)
        for t in range(len(ins)) for f, (fx, fy) in enumerate(CHIP_FLIPS)]


def _rs_ici_start(parts, name):
    n = len(parts)

    def body(*refs):
        ins, lands = refs[:n], refs[n:2 * n]
        send_sems, recv_sems = refs[2 * n], refs[2 * n + 1]
        token = refs[-1]
        for cp in _rs_ici_copies(ins, lands, send_sems, recv_sems):
            cp.start()
        token[...] = jnp.zeros(token.shape, token.dtype)

    thru = [pltpu.HBM(p.shape, p.dtype) for p in parts]
    res = pl.pallas_call(
        body, name=name, in_specs=[HBM] * (2 * n),
        out_shape=(pltpu.SemaphoreType.DMA((3 * n,)), pltpu.SemaphoreType.DMA((3 * n,)), *thru, *thru,
                   jax.ShapeDtypeStruct((8, LANE), F32)),
        out_specs=(SEM, SEM, *([HBM] * (2 * n)), pl.BlockSpec(memory_space=pltpu.VMEM)),
        input_output_aliases={i: 2 + i for i in range(2 * n)},
        compiler_params=pltpu.CompilerParams(has_side_effects=EFFECT),
    )(*[_in_hbm(p) for p in parts], *[_in_hbm(lax.empty(p.shape, p.dtype)) for p in parts])
    return res[:-1], res[-1]


def _rs_ici_wait(state, after, name):
    n = (len(state) - 2) // 2

    def body(*refs):
        send_sems, recv_sems = refs[0], refs[1]
        ins, lands = refs[2:2 + n], refs[2 + n:2 + 2 * n]
        for cp in _rs_ici_copies(ins, lands, send_sems, recv_sems):
            cp.wait_send()
            cp.wait_recv()

    thru = [pltpu.HBM(s.shape, s.dtype) for s in state[2:]]
    res = pl.pallas_call(
        body, name=name, in_specs=[SEM, SEM] + [HBM] * (2 * n) + [ANY], out_shape=tuple(thru),
        out_specs=tuple([HBM] * (2 * n)), input_output_aliases={2 + i: i for i in range(2 * n)},
        compiler_params=pltpu.CompilerParams(has_side_effects=EFFECT),
    )(*state, after)
    return list(res[n:])


def _ag_copies(shards, lands, send_sems, recv_sems):
    x, y, c = _place()
    mine = 4 * x + 2 * y + c
    peers = [(x, y, 1 - c)] + [(_flip(x, fx), _flip(y, fy), c) for fx, fy in CHIP_FLIPS]
    return [pltpu.make_async_remote_copy(
        src_ref=shards[t], dst_ref=lands[t].at[mine], send_sem=send_sems.at[4 * t + k], recv_sem=recv_sems.at[4 * t + k],
        device_id=peer, device_id_type=MESH) for t in range(len(shards)) for k, peer in enumerate(peers)]


def _ag_start(shards, after, name):
    n = len(shards)

    def body(*refs):
        ins, lands = refs[:n], refs[n:2 * n]
        send_sems, recv_sems = refs[2 * n + 1], refs[2 * n + 2]
        token = refs[-1]
        for cp in _ag_copies(ins, lands, send_sems, recv_sems):
            cp.start()
        token[...] = jnp.zeros(token.shape, token.dtype)

    thru = [pltpu.HBM(s.shape, s.dtype) for s in shards]
    land = [pltpu.HBM((NDEV,) + s.shape, s.dtype) for s in shards]
    res = pl.pallas_call(
        body, name=name, in_specs=[HBM] * (2 * n) + [ANY],
        out_shape=(pltpu.SemaphoreType.DMA((4 * n,)), pltpu.SemaphoreType.DMA((4 * n,)), *thru, *land,
                   jax.ShapeDtypeStruct((8, LANE), F32)),
        out_specs=(SEM, SEM, *([HBM] * (2 * n)), pl.BlockSpec(memory_space=pltpu.VMEM)),
        input_output_aliases={i: 2 + i for i in range(2 * n)},
        compiler_params=pltpu.CompilerParams(has_side_effects=EFFECT),
    )(*[_in_hbm(s) for s in shards], *[_in_hbm(lax.empty((NDEV,) + s.shape, s.dtype)) for s in shards], after)
    return res[:-1], res[-1]


def _ag_wait(state, after, name):
    n = (len(state) - 2) // 2

    def body(*refs):
        send_sems, recv_sems = refs[0], refs[1]
        ins, lands = refs[2:2 + n], refs[2 + n:2 + 2 * n]
        for cp in _ag_copies(ins, lands, send_sems, recv_sems):
            cp.wait_send()
            cp.wait_recv()

    thru = [pltpu.HBM(s.shape, s.dtype) for s in state[2:]]
    res = pl.pallas_call(
        body, name=name, in_specs=[SEM, SEM] + [HBM] * (2 * n) + [ANY], out_shape=tuple(thru),
        out_specs=tuple([HBM] * (2 * n)), input_output_aliases={2 + i: i for i in range(2 * n)},
        compiler_params=pltpu.CompilerParams(has_side_effects=EFFECT),
    )(*state, after)
    return list(res[:n]), list(res[n:])


def _ag_finish(shards, lands):
    n = len(shards)

    def body(*refs):
        ins, outs, stage = refs[:n], refs[2 * n:3 * n], refs[3 * n:4 * n]
        send_sems, recv_sems, local_sems = refs[4 * n:]
        x, y, c = _place()
        chips = [(_flip(x, fx), _flip(y, fy)) for fx, fy in CHIP_FLIPS]

        def passing(t, j, core, to):
            blk = outs[t].at[4 * chips[j][0] + 2 * chips[j][1] + core]
            return pltpu.make_async_remote_copy(src_ref=blk, dst_ref=blk, send_sem=send_sems.at[3 * t + j],
                                                recv_sem=recv_sems.at[3 * t + j], device_id=to, device_id_type=MESH)

        sends = [passing(t, j, c, (x, y, 1 - c)) for t in range(n) for j in range(3)]
        for cp in sends:
            cp.start()
        load = [pltpu.make_async_copy(ins[t], stage[t], local_sems.at[t]) for t in range(n)]
        mine = [pltpu.make_async_copy(stage[t], outs[t].at[4 * x + 2 * y + c], local_sems.at[t]) for t in range(n)]
        for cp in load:
            cp.start()
        for t in range(n):
            load[t].wait()
            mine[t].start()
        for t in range(n):
            for j in range(3):
                passing(t, j, 1 - c, (x, y, c)).wait_recv()
        for cp in sends:
            cp.wait_send()
        for cp in mine:
            cp.wait()

    return pl.pallas_call(
        body, name="ag_finish", in_specs=[ANY] * (2 * n), out_specs=[ANY] * n,
        out_shape=[jax.ShapeDtypeStruct(l.shape, l.dtype) for l in lands],
        input_output_aliases={n + i: i for i in range(n)},
        scratch_shapes=[pltpu.VMEM(s.shape, s.dtype) for s in shards]
        + [pltpu.SemaphoreType.DMA((3 * n,)), pltpu.SemaphoreType.DMA((3 * n,)), pltpu.SemaphoreType.DMA((n,))],
        compiler_params=_comm_params())(*shards, *lands)


def _row_tile(rows, largest=256):
    for t in (1024, 512, 256, 128, 64, 32, 16, 8):
        if t > largest:
            continue
        if rows % t == 0:
            return t
    return rows


def _rs_chip_sum(pos, g5, got):
    a, b = g5.shape[3:]
    ta = _row_tile(a, 1024)

    def kern(pos_ref, o_ref, g_ref, p_ref):
        p_ref[...] = (o_ref[...] + g_ref[...]).astype(BF16)

    def mine(f, i, pos_ref):
        return (pos_ref[0] ^ ((f + 1) & 1), pos_ref[1] ^ ((f + 1) >> 1), pos_ref[2], i, 0)

    spec = pltpu.PrefetchScalarGridSpec(
        num_scalar_prefetch=1, grid=(3, a // ta),
        in_specs=[BS((None, None, None, ta, b), mine), BS((None, ta, b), lambda f, i, pos_ref: (f + 1, i, 0))],
        out_specs=BS((None, ta, b), lambda f, i, pos_ref: (f, i, 0)))
    return _call(kern, name="rs_chip_sum", grid_spec=spec, out_shape=jax.ShapeDtypeStruct((3, a, b), BF16),
                          compiler_params=_params(2))(pos, g5, got)


def _rs_final_sum(pos, g5, got, recv):
    a, b = g5.shape[3:]
    ta = _row_tile(a, 1024)

    def kern(pos_ref, o_ref, g_ref, r_ref, s_ref):
        acc = o_ref[...] + g_ref[...]
        for f in range(3):
            acc = acc + r_ref[f].astype(F32)
        s_ref[...] = acc

    spec = pltpu.PrefetchScalarGridSpec(
        num_scalar_prefetch=1, grid=(a // ta,),
        in_specs=[BS((None, None, None, ta, b), lambda i, pos_ref: (pos_ref[0], pos_ref[1], pos_ref[2], i, 0)),
                  BS((None, ta, b), lambda i, pos_ref: (0, i, 0)), BS((3, ta, b), lambda i, pos_ref: (0, i, 0))],
        out_specs=BS((ta, b), lambda i, pos_ref: (i, 0)))
    return _call(kern, name="rs_final_sum", grid_spec=spec, out_shape=jax.ShapeDtypeStruct((a, b), F32),
                          compiler_params=_params(1))(pos, g5, got, recv)


def _rs_d2d_copies(ins, lands, send_sems, recv_sems):
    x, y, c = _place()
    return [pltpu.make_async_remote_copy(
        src_ref=ins[t].at[_flip(x, fx), _flip(y, fy), 1 - c], dst_ref=lands[t].at[f], send_sem=send_sems.at[4 * t + f],
        recv_sem=recv_sems.at[4 * t + f], device_id=(x, y, 1 - c), device_id_type=MESH)
        for t in range(len(ins)) for f, (fx, fy) in enumerate(REL_CHIPS)]


def _rs_d2d_start(g5s, name):
    n = len(g5s)

    def body(*refs):
        ins, lands = refs[:n], refs[n:2 * n]
        for cp in _rs_d2d_copies(ins, lands, refs[2 * n], refs[2 * n + 1]):
            cp.start()
        refs[-1][...] = jnp.zeros(refs[-1].shape, F32)

    thru = [pltpu.HBM(g.shape, g.dtype) for g in g5s]
    land = [pltpu.HBM((4,) + g.shape[3:], F32) for g in g5s]
    res = pl.pallas_call(
        body, name=name, in_specs=[HBM] * (2 * n),
        out_shape=(pltpu.SemaphoreType.DMA((4 * n,)), pltpu.SemaphoreType.DMA((4 * n,)), *thru, *land,
                   jax.ShapeDtypeStruct((8, LANE), F32)),
        out_specs=(SEM, SEM, *([HBM] * (2 * n)), pl.BlockSpec(memory_space=pltpu.VMEM)),
        input_output_aliases={i: 2 + i for i in range(2 * n)},
        compiler_params=pltpu.CompilerParams(has_side_effects=EFFECT),
    )(*[_in_hbm(g) for g in g5s], *[_in_hbm(lax.empty((4,) + g.shape[3:], F32)) for g in g5s])
    return res[:-1], res[-1]


def _rs_d2d_wait(state, after, name):
    n = (len(state) - 2) // 2

    def body(*refs):
        ins, lands = refs[2:2 + n], refs[2 + n:2 + 2 * n]
        for cp in _rs_d2d_copies(ins, lands, refs[0], refs[1]):
            cp.wait_send()
            cp.wait_recv()

    thru = [pltpu.HBM(s.shape, s.dtype) for s in state[2:]]
    res = pl.pallas_call(
        body, name=name, in_specs=[SEM, SEM] + [HBM] * (2 * n) + [ANY], out_shape=tuple(thru),
        out_specs=tuple([HBM] * (2 * n)), input_output_aliases={2 + i: i for i in range(2 * n)},
        compiler_params=pltpu.CompilerParams(has_side_effects=EFFECT),
    )(*state, after)
    return list(res[:n]), list(res[n:])


def _as_g5(grads):
    return [g.reshape((2, 2, 2) + g.shape[1:]) for g in grads]


def _rs_mid(g5s, gots, pos, tag):
    parts = [_rs_chip_sum(pos, g, got) for g, got in zip(g5s, gots)]
    state, token = _rs_ici_start(parts, "rs_ici_start_" + tag)
    return (g5s, gots, state, tag), token


def _rs_begin(grads, pos, tag, after=None):
    g5s = _as_g5(grads)
    return _rs_mid(g5s, _rs_d2d(g5s, after), pos, tag)


def _rs_end(pending, after, pos):
    g5s, gots, state, tag = pending
    recvs = _rs_ici_wait(state, after, "rs_ici_wait_" + tag)
    return [_rs_final_sum(pos, g, got, r) for g, got, r in zip(g5s, gots, recvs)]


def _sum_devices(v):
    _, r, _ = v.shape

    def kern(v_ref, o_ref):
        acc = v_ref[0]
        for d in range(1, NDEV):
            acc = acc + v_ref[d]
        o_ref[...] = acc

    return _call(kern, name="sum_devices", out_shape=jax.ShapeDtypeStruct((r, LANE), F32),
                          compiler_params=_comm_params())(v)


def _loss_head(xf, target):
    s = xf.shape[0]
    tm = _tile(s, 512)

    def kern(x_ref, t_ref, dx_ref, l_ref):
        err = x_ref[...] - t_ref[...]
        dx_ref[...] = err * (1.0 / DM)
        part = jnp.broadcast_to(0.5 * jnp.sum(jnp.mean(err * err, axis=-1, keepdims=True), axis=0, keepdims=True), (8, LANE))

        @pl.when(pl.program_id(0) == 0)
        def _():
            l_ref[...] = part

        @pl.when(pl.program_id(0) > 0)
        def _():
            l_ref[...] += part

    row = BS((tm, DM), lambda i: (i, 0))
    return _call(kern, name="loss_head", grid=(s // tm,), in_specs=[row, row],
                          out_specs=[row, BS((8, LANE), lambda i: (0, 0))],
                          out_shape=[jax.ShapeDtypeStruct((s, DM), F32), jax.ShapeDtypeStruct((8, LANE), F32)],
                          compiler_params=_params(1))(xf, target)


def _adamw(w, g, m, v, after=None):
    rows, cols = w.shape
    tr = _row_tile(rows)
    extra = () if after is None else (after,)

    def kern(w_ref, g_ref, m_ref, v_ref, *rest):
        d_ref, nm_ref, nv_ref = rest[-3:]
        gv = g_ref[...]
        nm = ADAM_B1 * m_ref[...] + (1.0 - ADAM_B1) * gv
        nv = ADAM_B2 * v_ref[...] + (1.0 - ADAM_B2) * (gv * gv)
        m_hat = nm / (1.0 - ADAM_B1 ** ADAM_STEP)
        v_hat = nv / (1.0 - ADAM_B2 ** ADAM_STEP)
        d_ref[...] = -ADAM_LR * (m_hat / (jnp.sqrt(v_hat) + ADAM_EPS) + ADAM_WD * w_ref[...])
        nm_ref[...] = nm
        nv_ref[...] = nv

    blk = BS((tr, cols), lambda i: (i, 0))
    shp = jax.ShapeDtypeStruct((rows, cols), F32)
    return _call(kern, name="adamw", grid=(rows // tr,), in_specs=[blk] * 4 + [ANY] * len(extra),
                          out_specs=[blk] * 3, out_shape=[shp] * 3, compiler_params=_params(1))(w, g, m, v, *extra)


def _adamw_nd(w, g, m, v, after=None):
    shape = w.shape
    two = (math.prod(shape[:-1]), shape[-1])
    return tuple(o.reshape(shape)
                 for o in _adamw(w.reshape(two), g.reshape(two), m.reshape(two), v.reshape(two), after))


def _pack_small(parts):
    flat = jnp.concatenate([p.reshape(-1) for p in parts])
    pad = (-flat.shape[0]) % (8 * LANE)
    return jnp.pad(flat, (0, pad)).reshape(-1, LANE)


def _unpack_small(packed, shapes, lead=()):
    flat = packed.reshape(lead + (-1,))
    out, off = [], 0
    for shp in shapes:
        n = math.prod(shp)
        out.append(flat[..., off:off + n].reshape(lead + tuple(shp)))
        off += n
    return out


WEIGHT_NAMES = ('g_mix_pre', 'g_mix_post', 'g_cross_pre', 'g_mem', 'g_cross_post', 'g_ffn_pre', 'g_ffn_post', 'w_xq',
                'w_xkv', 'w_xo', 'w_ffn_gu', 'w_ffn_down', 'ab_w_in', 'ab_b_f', 'ab_conv_w', 'ab_w_out', 'c_w_in',
                'c_conv_w', 'c_conv_b', 'c_w_a', 'c_b_a', 'c_w_i', 'c_b_i', 'c_lam', 'c_w_out')
BIG = ('w_xq', 'w_xkv', 'w_xo', 'w_ffn_gu', 'w_ffn_down', 'ab_w_in', 'ab_w_out', 'c_w_in', 'c_w_a', 'c_w_i', 'c_w_out')
SMALL_SHARDED = ('ab_conv_w', 'c_conv_w', 'c_conv_b', 'c_b_a', 'c_b_i', 'c_lam')
REPLICATED = ('g_mix_pre', 'g_mix_post', 'g_cross_pre', 'g_mem', 'g_cross_post', 'g_ffn_pre', 'g_ffn_post', 'ab_b_f')


def _small_full(name, gathered):
    nd = gathered.ndim
    return jnp.moveaxis(gathered, 0, nd - 2).reshape(gathered.shape[1:-1] + (NDEV * gathered.shape[-1],))


def _small_shard(full, dev):
    c = full.shape[-1] // NDEV
    return lax.dynamic_slice_in_dim(full, dev * c, c, axis=full.ndim - 1)


def kernel(x, mem, g_mix_pre, g_mix_post, g_cross_pre, g_mem, g_cross_post, g_ffn_pre, g_ffn_post, w_xq, w_xkv, w_xo, w_ffn_gu, w_ffn_down, ab_w_in, ab_b_f, ab_conv_w, ab_w_out, c_w_in, c_conv_w, c_conv_b, c_w_a, c_b_a, c_w_i, c_b_i, c_lam, c_w_out, loss_target, m_g_mix_pre, m_g_mix_post, m_g_cross_pre, m_g_mem, m_g_cross_post, m_g_ffn_pre, m_g_ffn_post, m_w_xq, m_w_xkv, m_w_xo, m_w_ffn_gu, m_w_ffn_down, m_ab_w_in, m_ab_b_f, m_ab_conv_w, m_ab_w_out, m_c_w_in, m_c_conv_w, m_c_conv_b, m_c_w_a, m_c_b_a, m_c_w_i, m_c_b_i, m_c_lam, m_c_w_out, v_g_mix_pre, v_g_mix_post, v_g_cross_pre, v_g_mem, v_g_cross_post, v_g_ffn_pre, v_g_ffn_post, v_w_xq, v_w_xkv, v_w_xo, v_w_ffn_gu, v_w_ffn_down, v_ab_w_in, v_ab_b_f, v_ab_conv_w, v_ab_w_out, v_c_w_in, v_c_conv_w, v_c_conv_b, v_c_w_a, v_c_b_a, v_c_w_i, v_c_b_i, v_c_lam, v_c_w_out):
    args = locals()
    w = {n: args[n] for n in WEIGHT_NAMES}
    mom = {n: args["m_" + n] for n in WEIGHT_NAMES}
    var = {n: args["v_" + n] for n in WEIGHT_NAMES}
    for t in (w, mom, var):
        t['w_ffn_gu'] = t['w_ffn_gu'].transpose(0, 2, 1)
    ab_t = [t['ab_w_in'].transpose(2, 0, 1) for t in (w, mom, var)]
    pos = jnp.stack([lax.axis_index("x"), lax.axis_index("y"), lax.axis_index("c")]).astype(jnp.int32)
    dev = 4 * pos[0] + 2 * pos[1] + pos[2]
    xs, mems, target = x[0], mem[0], loss_target[0]
    n_even, n_odd = (DEPTH + 1) // 2, DEPTH // 2

    small_shapes = [w[n].shape for n in SMALL_SHARDED]
    small_w_all = _small_gather(_pack_small([w[n] for n in SMALL_SHARDED]))
    gathered_small = _unpack_small(small_w_all, small_shapes, (NDEV,))
    small = {n: _small_full(n, g) for n, g in zip(SMALL_SHARDED, gathered_small)}
    ab_bfb = jnp.broadcast_to(ab_b_f[:, :, None], (n_even, FOX_H, LANE))
    c_bai = jnp.stack([small['c_b_a'].reshape(n_odd, DM), small['c_b_i'].reshape(n_odd, DM)], axis=1)
    row = lambda a, l: a[l][None]

    REST = ('w_xq', 'w_xkv', 'w_xo', 'w_ffn_gu', 'w_ffn_down')

    def mixer_names(l):
        return ('ab_w_in', 'ab_w_out') if l % 2 == 0 else ('c_w_in', 'c_w_a', 'c_w_i', 'c_w_out')

    def shards_of(l, names):
        out = []
        for n in names:
            if n == 'ab_w_in':
                s = ab_t[0][:, l // 2].astype(BF16)
            else:
                s = w[n][l if w[n].shape[0] == DEPTH else l // 2].astype(BF16)
            out.append(s.reshape(-1, s.shape[-1]))
        return out

    def mixer_weights(l, full):
        if l % 2 == 0:
            e = l // 2
            return (row(g_mix_pre, l), row(g_mix_post, l), _ab_pack(full['ab_w_in'].reshape(AB_IN, DM)), ab_bfb[e],
                    small['ab_conv_w'][e], full['ab_w_out'].reshape(DM, DM))
        o = l // 2
        gate_w = lambda g: g.reshape(NDEV, LRU_NB, LRU_BW // NDEV, LRU_BW).transpose(1, 0, 2, 3).reshape(
            LRU_NB, LRU_BW, LRU_BW)
        return (row(g_mix_pre, l), row(g_mix_post, l), full['c_w_in'], small['c_conv_w'][o], row(small['c_conv_b'], o),
                jnp.stack([gate_w(full['c_w_a']), gate_w(full['c_w_i'])]), c_bai[o], row(small['c_lam'], o),
                full['c_w_out'].reshape(DM, DM))

    def rest_weights(l, full):
        cross = (row(g_cross_pre, l), row(g_mem, l), row(g_cross_post, l), full['w_xq'].reshape(DM, DM), full['w_xkv'],
                 full['w_xo'].reshape(DM, DM))
        ffn = (row(g_ffn_pre, l), row(g_ffn_post, l), full['w_ffn_gu'], full['w_ffn_down'].reshape(D_FF, DM))
        return cross, ffn

    def gathered(state, names, after, tag):
        shards, lands = _ag_wait(state, after, "ag_wait_" + tag)
        full = _ag_finish(shards, lands)
        return dict(zip(names, full)), full[0]

    saved, weights = [], []
    h = xs
    names_of = lambda l: mixer_names(l) + REST
    states = {}
    st_m, _ = _ag_start(shards_of(0, mixer_names(0)), small_w_all, "ag_start_0m")
    st_r, _ = _ag_start(shards_of(0, REST), st_m[2], "ag_start_0r")
    states[1], token = _ag_start(shards_of(1, names_of(1)), st_r[2], "ag_start_1")
    full_m, _ = gathered(st_m, mixer_names(0), xs, "0m")
    for l in range(DEPTH):
        if l > 0:
            full, done = gathered(states[l], names_of(l), h, str(l))
            full_m = full_r = full
            token = None
            if l + 2 < DEPTH:
                states[l + 2], token = _ag_start(shards_of(l + 2, names_of(l + 2)), done, "ag_start_%d" % (l + 2))
        mixer = mixer_weights(l, full_m)
        h, s_mix = (_fox_layer_fwd if l % 2 == 0 else _lru_layer_fwd)(h, *mixer, after=token)
        token = None
        if l == 0:
            full_r, done = gathered(st_r, REST, h, "0r")
            states[2], token = _ag_start(shards_of(2, names_of(2)), done, "ag_start_2")
        cross, ffn = rest_weights(l, full_r)
        h, s_cross = _cross_fwd(h, mems, *cross, after=token)
        h, s_ffn = _ffn_fwd(h, *ffn)
        saved.append((s_mix, s_cross, s_ffn))
        weights.append((mixer, cross, ffn))
    mixer_args = lambda l: weights[l][0]
    cross_args = lambda l: weights[l][1]
    ffn_args = lambda l: weights[l][2]
    dx, loss_rep = _loss_head(h, target)
    loss = lax.psum(loss_rep[0, 0], ("x", "y", "c"))

    grads = {n: [None] * w[n].shape[0] for n in BIG}
    partial = {n: [None] * w[n].shape[0] for n in REPLICATED + SMALL_SHARDED}
    def finish(pending, after):
        state, names, where = pending
        for n, g in zip(names, _rs_end(state, after, pos)):
            grads[n][where[n]] = g

    def unit(layer, names):
        return [layer[n][1] for n in names], names, {n: layer[n][0] for n in names}

    d2d = ici = None
    token = None
    for l in reversed(range(DEPTH)):
        s_mix, s_cross, s_ffn = saved[l]
        dx, partial['g_ffn_pre'][l], partial['g_ffn_post'][l], dwgu, dwd = _ffn_bwd(dx, s_ffn, *ffn_args(l), after=token)
        token = None
        if d2d is not None:
            g5s, gots = _rs_d2d_wait(d2d[0], dx, "rs_d2d_wait_%d" % (l + 1))
            state, token = _rs_mid(g5s, gots, pos, str(l + 1))
            ici, d2d = (state,) + d2d[1:], None
        (dx, partial['g_cross_pre'][l], partial['g_mem'][l], partial['g_cross_post'][l], dwq, dwkv, dwo) = _cross_bwd(
            dx, s_cross, mems, *cross_args(l), after=token)
        token = None
        layer = {'w_xq': (l, dwq.reshape(NDEV, DM // NDEV, DM)), 'w_xkv': (l, dwkv), 'w_xo': (l, dwo.reshape(NDEV, DM // NDEV, DM)),
                 'w_ffn_gu': (l, dwgu), 'w_ffn_down': (l, dwd.reshape(NDEV, D_FF // NDEV, DM))}
        if l == 0:
            gs, names, where = unit(layer, REST)
            state, token = _rs_begin(gs, pos, "0r")
            ici_rest = (state, names, where)
        if l % 2 == 0:
            e = l // 2
            (dx, partial['g_mix_pre'][l], partial['g_mix_post'][l], dwall, partial['ab_b_f'][e], partial['ab_conv_w'][e],
             dwout) = _fox_layer_bwd(dx, s_mix, *mixer_args(l), after=token)
            layer['ab_w_in'] = (e, _ab_unpack(dwall).reshape(NDEV, AB_IN // NDEV, DM))
            layer['ab_w_out'] = (e, dwout.reshape(NDEV, DM // NDEV, DM))
        else:
            o = l // 2
            (dx, partial['g_mix_pre'][l], partial['g_mix_post'][l], dwin, partial['c_conv_w'][o], dconvb, dwai, dbai, dlam,
             dwout) = _lru_layer_bwd(dx, s_mix, *mixer_args(l), after=token)
            partial['c_conv_b'][o], partial['c_lam'][o] = dconvb[0], dlam[0]
            partial['c_b_a'][o], partial['c_b_i'][o] = dbai[0].reshape(LRU_NB, LRU_BW), dbai[1].reshape(LRU_NB, LRU_BW)
            rows = LRU_BW // NDEV
            by_dev = lambda d: d.reshape(LRU_NB, NDEV, rows, LRU_BW).transpose(1, 0, 2, 3).reshape(NDEV, LRU_NB * rows, LRU_BW)
            layer['c_w_in'] = (o, dwin)
            layer['c_w_a'] = (o, by_dev(dwai[0]))
            layer['c_w_i'] = (o, by_dev(dwai[1]))
            layer['c_w_out'] = (o, dwout.reshape(NDEV, DM // NDEV, DM))
        token = None
        if ici is not None:
            finish(ici, dx)
            ici = None
        if l > 0:
            gs, names, where = unit(layer, list(layer))
            state, token = _rs_d2d_start(_as_g5(gs), "rs_d2d_start_%d" % l)
            d2d = (state, names, where)
    small_names = REPLICATED + SMALL_SHARDED
    small_parts = [jnp.stack([p.reshape(w[n].shape[1:] if n in REPLICATED else small[n].shape[1:]) for p in partial[n]])
                   for n in small_names]
    small_all = _small_gather(_pack_small(small_parts))
    reduced = _unpack_small(_sum_devices(small_all), [p.shape for p in small_parts])
    grad = {}
    for n, g in zip(small_names, reduced):
        grad[n] = g if n in REPLICATED else _small_shard(g, dev)

    gs, names, where = unit(layer, mixer_names(0))
    state, token = _rs_begin(gs, pos, "0m", after=small_all)
    ici_mixer = (state, names, where)
    finish(ici_rest, dx)

    delta, new_m, new_v = {}, {}, {}
    last = mixer_names(0)
    for n in BIG:
        if n not in last:
            grad[n] = jnp.stack(grads[n]).reshape(w[n].shape)
            delta[n], new_m[n], new_v[n] = _adamw_nd(w[n], grad[n], mom[n], var[n], token)
            token = delta[n]
    shapes = [w[n].shape for n in small_names]
    packed = [_pack_small([t[n] for n in small_names]) for t in (w, grad, mom, var)]
    res_small = _adamw(*packed, after=token)
    for res, out in zip(res_small, (delta, new_m, new_v)):
        for n, val in zip(small_names, _unpack_small(res, shapes)):
            out[n] = val
    finish(ici_mixer, res_small[0])
    for n in last:
        if n == 'ab_w_in':
            g_t = jnp.stack(grads[n], axis=1)
            res = (g_t,) + _adamw_nd(ab_t[0], g_t, ab_t[1], ab_t[2])
            grad[n], delta[n], new_m[n], new_v[n] = (r.transpose(1, 2, 0) for r in res)
            continue
        grad[n] = jnp.stack(grads[n]).reshape(w[n].shape)
        delta[n], new_m[n], new_v[n] = _adamw_nd(w[n], grad[n], mom[n], var[n])

    for t in (grad, delta, new_m, new_v):
        t['w_ffn_gu'] = t['w_ffn_gu'].transpose(0, 2, 1)
    return (loss, dx[None], *[grad[n] for n in WEIGHT_NAMES], *[delta[n] for n in WEIGHT_NAMES],
            *[new_m[n] for n in WEIGHT_NAMES], *[new_v[n] for n in WEIGHT_NAMES])
```

```python
import math

import jax
import jax.numpy as jnp
from jax import lax
from jax.experimental import pallas as pl
from jax.experimental.pallas import tpu as pltpu

F32 = jnp.float32
BF16 = jnp.bfloat16
BS = pl.BlockSpec
ANY = pl.BlockSpec(memory_space=pl.ANY)
MESH = pl.DeviceIdType.MESH

DM = 1024
DEPTH = 4
EPS = 1e-6
NEG = -1e30
FOX_W = 512
FOX_HD = 64
FOX_H = 8
SC_W = 512
SC_K = 3
AB_IN = 3 * FOX_W + FOX_H + 3 * SC_W
AB_PAD = 3200
LRU_BW = 256
LRU_NB = 4
RG_K = 4
RG_C = 8.0
MEM_H = 4
MEM_HD = 256
D_FF = 2816
NDEV = 8
FFB = 2 * D_FF // NDEV
ADAM_LR, ADAM_B1, ADAM_B2, ADAM_EPS, ADAM_WD, ADAM_STEP = 0.001, 0.9, 0.999, 1e-08, 0.01, 10

LANE = 128
VMEM_LIMIT = 16 * 1024 * 1024
VMEM_BIG = 40 * 1024 * 1024


def _params(ngrid, vmem=None):
    return pltpu.CompilerParams(dimension_semantics=("arbitrary",) * ngrid, vmem_limit_bytes=vmem or VMEM_LIMIT)


def _call(kern, **kwargs):
    return pl.pallas_call(kern, **kwargs)


TK_RED = 2048
TM_SUM = 512


def _tile(n, t):
    return t if n % t == 0 else n


def _mm(name, a, b, *, grid, a_spec, b_spec, o_spec, out_shape, dn, out_dtype=F32, vmem=None):
    nred = grid[-1]
    ngrid = len(grid)

    def kern(a_ref, b_ref, o_ref, *scratch):
        p = lax.dot_general(a_ref[...].astype(BF16), b_ref[...].astype(BF16), (dn, ((), ())),
                            preferred_element_type=F32)
        if nred == 1:
            o_ref[...] = p.astype(o_ref.dtype)
            return
        acc = scratch[0] if scratch else o_ref
        r = pl.program_id(ngrid - 1)

        @pl.when(r == 0)
        def _():
            acc[...] = p

        @pl.when(r > 0)
        def _():
            acc[...] += p

        if scratch:
            @pl.when(r == nred - 1)
            def _():
                o_ref[...] = acc[...].astype(o_ref.dtype)

    blk = tuple(d for d in o_spec.block_shape if d is not None)
    scratch = [pltpu.VMEM(blk, F32)] if (nred > 1 and out_dtype != F32) else []
    return _call(kern, name=name, grid=grid, in_specs=[a_spec, b_spec], out_specs=o_spec,
                          out_shape=jax.ShapeDtypeStruct(out_shape, out_dtype), scratch_shapes=scratch,
                          compiler_params=_params(ngrid, vmem))(a, b)


NN = ((1,), (0,))
NT = ((1,), (1,))
TN = ((0,), (0,))


def _mm_nn(name, a, w, out_dtype=F32, tn=None, vmem=None):
    m, k = a.shape
    n = w.shape[1]
    tm = _tile(m, 512)
    tn = n if tn is None else tn
    return _mm(name, a, w, grid=(m // tm, n // tn, 1), a_spec=BS((tm, k), lambda i, j, r: (i, 0)),
               b_spec=BS((k, tn), lambda i, j, r: (0, j)), o_spec=BS((tm, tn), lambda i, j, r: (i, j)),
               out_shape=(m, n), dn=NN, out_dtype=out_dtype, vmem=vmem)


def _mm_nt_cols(name, a, wt, tn):
    m, k = a.shape
    n = wt.shape[0]
    tm = _tile(m, 512)
    return _mm(name, a, wt, grid=(m // tm, n // tn, 1), a_spec=BS((tm, k), lambda i, j, r: (i, 0)),
               b_spec=BS((tn, k), lambda i, j, r: (j, 0)), o_spec=BS((tm, tn), lambda i, j, r: (i, j)),
               out_shape=(m, n), dn=NT)


def _mm_tn_rows(name, a, b, tk):
    m, k = a.shape
    n = b.shape[1]
    tm = _tile(m, TK_RED)
    return _mm(name, a, b, grid=(k // tk, m // tm), a_spec=BS((tm, tk), lambda j, r: (r, j)),
               b_spec=BS((tm, n), lambda j, r: (r, 0)), o_spec=BS((tk, n), lambda j, r: (j, 0)),
               out_shape=(k, n), dn=TN)


def _mm_nt(name, a, w, out_dtype=F32, tn=None):
    m, n = a.shape
    k = w.shape[0]
    tm = _tile(m, 512)
    tn = n if tn is None else tn
    return _mm(name, a, w, grid=(m // tm, n // tn), a_spec=BS((tm, tn), lambda i, r: (i, r)),
               b_spec=BS((k, tn), lambda i, r: (0, r)), o_spec=BS((tm, k), lambda i, r: (i, 0)),
               out_shape=(m, k), dn=NT, out_dtype=out_dtype)


def _mm_tn(name, a, b, tn=None):
    m, k = a.shape
    n = b.shape[1]
    tm = _tile(m, TK_RED)
    tn = n if tn is None else tn
    return _mm(name, a, b, grid=(n // tn, m // tm), a_spec=BS((tm, k), lambda j, r: (r, 0)),
               b_spec=BS((tm, tn), lambda j, r: (r, j)), o_spec=BS((k, tn), lambda j, r: (0, j)),
               out_shape=(k, n), dn=TN)


def _bmm_nn(name, a, w, out_dtype=F32):
    m, k = a.shape
    g, _, n = w.shape
    tm = _tile(m, 512)
    return _mm(name, a, w, grid=(g, m // tm, 1), a_spec=BS((tm, k), lambda q, i, r: (i, 0)),
               b_spec=BS((None, k, n), lambda q, i, r: (q, 0, 0)), o_spec=BS((None, tm, n), lambda q, i, r: (q, i, 0)),
               out_shape=(g, m, n), dn=NN, out_dtype=out_dtype)


def _bmm_tn(name, a, b):
    m, k = a.shape
    g, _, n = b.shape
    tm = _tile(m, TK_RED)
    return _mm(name, a, b, grid=(g, m // tm), a_spec=BS((tm, k), lambda q, r: (r, 0)),
               b_spec=BS((None, tm, n), lambda q, r: (q, r, 0)), o_spec=BS((None, k, n), lambda q, r: (q, 0, 0)),
               out_shape=(g, k, n), dn=TN)


def _block_sum(name, a, w, dn, out_cols):
    g, m, ac = a.shape
    tm = _tile(m, TM_SUM)

    def kern(a_ref, w_ref, o_ref):
        acc = None
        for q in range(g):
            p = lax.dot_general(a_ref[q].astype(BF16), w_ref[q].astype(BF16), (dn, ((), ())), preferred_element_type=F32)
            acc = p if acc is None else acc + p
        o_ref[...] = acc

    return _call(kern, name=name, grid=(m // tm,),
                 in_specs=[BS((g, tm, ac), lambda i: (0, i, 0)), BS(w.shape, lambda i: (0, 0, 0))],
                 out_specs=BS((tm, out_cols), lambda i: (i, 0)), out_shape=jax.ShapeDtypeStruct((m, out_cols), F32),
                 compiler_params=_params(1, VMEM_BIG))(a, w)


def _bmm_nt_sum(name, a, w):
    return _block_sum(name, a, w, NT, w.shape[1])


def _bmm_nn_sum(name, a, w):
    return _block_sum(name, a, w, NN, w.shape[2])


def _rstd(x):
    return lax.rsqrt(jnp.mean(x * x, axis=-1, keepdims=True) + EPS)


def _norm_fwd(x, g, after=None):
    rows = x.shape[0]
    tm = _tile(rows, 512)

    def kern(x_ref, g_ref, *rest):
        xv = x_ref[...]
        rest[-1][...] = ((xv * _rstd(xv)) * g_ref[...]).astype(BF16)

    extra = () if after is None else (after,)
    return _call(kern, name="norm_fwd", grid=(rows // tm,),
                          in_specs=[BS((tm, DM), lambda i: (i, 0)), BS((1, DM), lambda i: (0, 0))] + [ANY] * len(extra),
                          out_specs=BS((tm, DM), lambda i: (i, 0)),
                          out_shape=jax.ShapeDtypeStruct((rows, DM), BF16), compiler_params=_params(1))(x, g, *extra)


def _norm_res(x, y, g):
    rows = x.shape[0]
    tm = _tile(rows, 512)

    def kern(x_ref, y_ref, g_ref, o_ref):
        yv = y_ref[...]
        o_ref[...] = x_ref[...] + (yv * _rstd(yv)) * g_ref[...]

    row = BS((tm, DM), lambda i: (i, 0))
    return _call(kern, name="norm_res", grid=(rows // tm,),
                          in_specs=[row, row, BS((1, DM), lambda i: (0, 0))], out_specs=row,
                          out_shape=jax.ShapeDtypeStruct((rows, DM), F32), compiler_params=_params(1))(x, y, g)


def _norm_bwd(z, dout, g, resid, out_dtype, after=None):
    rows = z.shape[0]
    tm = _tile(rows, 512)
    has_res = resid is not None

    def kern(*refs):
        z_ref, d_ref, g_ref = refs[:3]
        r_ref = refs[3] if has_res else None
        dz_ref, dg_ref = refs[-2:]
        zv = z_ref[...]
        dv = d_ref[...].astype(F32)
        r = _rstd(zv)
        zh = zv * r
        dzh = dv * g_ref[...]
        dz = r * (dzh - zh * jnp.mean(dzh * zh, axis=-1, keepdims=True))
        if has_res:
            dz = dz + r_ref[...]
        dz_ref[...] = dz.astype(dz_ref.dtype)
        part = jnp.sum(dv * zh, axis=0, keepdims=True)

        @pl.when(pl.program_id(0) == 0)
        def _():
            dg_ref[...] = part

        @pl.when(pl.program_id(0) > 0)
        def _():
            dg_ref[...] += part

    row = BS((tm, DM), lambda i: (i, 0))
    vec = BS((1, DM), lambda i: (0, 0))
    ins = [row, row, vec] + ([row] if has_res else []) + ([ANY] if after is not None else [])
    args = (z, dout, g) + ((resid,) if has_res else ()) + ((after,) if after is not None else ())
    return _call(kern, name="norm_bwd_res" if has_res else "norm_bwd", grid=(rows // tm,), in_specs=ins,
                          out_specs=[row, vec],
                          out_shape=[jax.ShapeDtypeStruct((rows, DM), out_dtype), jax.ShapeDtypeStruct((1, DM), F32)],
                          compiler_params=_params(1))(*args)


def _ffn_up(h, wgu4):
    s = h.shape[0]
    tm = _tile(s, 512)

    def kern(h_ref, w_ref, gu_ref, a_ref):
        hv = h_ref[...]
        gate = lax.dot_general(hv, w_ref[0], (NT, ((), ())), preferred_element_type=F32)
        up = lax.dot_general(hv, w_ref[1], (NT, ((), ())), preferred_element_type=F32)
        gu_ref[0] = gate.astype(BF16)
        gu_ref[1] = up.astype(BF16)
        a_ref[...] = (gate * jax.nn.sigmoid(gate) * up).astype(BF16)

    return _call(
        kern, name="ffn_up", grid=(4, s // tm),
        in_specs=[BS((tm, DM), lambda j, i: (i, 0)), BS((2, None, FFB, DM), lambda j, i: (0, j, 0, 0))],
        out_specs=[BS((2, None, tm, FFB), lambda j, i: (0, j, i, 0)), BS((None, tm, FFB), lambda j, i: (j, i, 0))],
        out_shape=[jax.ShapeDtypeStruct((2, 4, s, FFB), BF16), jax.ShapeDtypeStruct((4, s, FFB), BF16)],
        compiler_params=_params(2))(h, wgu4)


def _ffn_da(dy, wd4, gu):
    s = dy.shape[0]
    tm = _tile(s, 512)

    def kern(dy_ref, w_ref, gu_ref, o_ref):
        da = lax.dot_general(dy_ref[...], w_ref[...], (NT, ((), ())), preferred_element_type=F32)
        gate = gu_ref[0].astype(F32)
        up = gu_ref[1].astype(F32)
        sg = jax.nn.sigmoid(gate)
        o_ref[0] = (da * up * (sg * (1.0 + gate * (1.0 - sg)))).astype(BF16)
        o_ref[1] = (da * (gate * sg)).astype(BF16)

    blk = BS((2, None, tm, FFB), lambda j, i: (0, j, i, 0))
    return _call(
        kern, name="ffn_da", grid=(4, s // tm),
        in_specs=[BS((tm, DM), lambda j, i: (i, 0)), BS((None, FFB, DM), lambda j, i: (j, 0, 0)), blk],
        out_specs=blk, out_shape=jax.ShapeDtypeStruct((2, 4, s, FFB), BF16), compiler_params=_params(2))(dy, wd4, gu)


def _ffn_fwd(x, gpre, gpost, wgu, wd):
    h = _norm_fwd(x, gpre)
    gu, a = _ffn_up(h, wgu.reshape(2, 4, FFB, DM))
    y = _bmm_nn_sum("ffn_down", a, wd.reshape(4, FFB, DM))
    return _norm_res(x, y, gpost), (x, h, gu, a, y)


def _ffn_bwd(dxo, saved, gpre, gpost, wgu, wd, after=None):
    x, h, gu, a, y = saved
    s = x.shape[0]
    dy, dgpost = _norm_bwd(y, dxo, gpost, None, BF16, after)
    dgu = _ffn_da(dy, wd.reshape(4, FFB, DM), gu).reshape(8, s, FFB)
    dwd = _bmm_tn_a3("ffn_dwd", a, dy)
    dwgu = _bmm_tn_a3("ffn_dwgu", dgu, h)
    dh = _bmm_nn_sum("ffn_dh", dgu, wgu)
    dx, dgpre = _norm_bwd(x, dh, gpre, dxo, F32)
    return dx, dgpre, dgpost, dwgu, dwd.reshape(D_FF, DM)


def _bmm_tn_a3(name, a, b):
    g, m, k = a.shape
    n = b.shape[1]
    tm = _tile(m, TK_RED)
    return _mm(name, a, b, grid=(g, m // tm), a_spec=BS((None, tm, k), lambda q, r: (q, r, 0)),
               b_spec=BS((tm, n), lambda q, r: (r, 0)), o_spec=BS((None, k, n), lambda q, r: (q, 0, 0)),
               out_shape=(g, k, n), dn=TN)


XATTN_TM = 1024


def _softmax_rows(s):
    m = jnp.max(s, axis=-1, keepdims=True)
    p = jnp.exp(s - m)
    return p / jnp.sum(p, axis=-1, keepdims=True)


def _xattn_fwd_call(h, wq, kv):
    s = h.shape[0]
    mlen = kv.shape[1]
    tm = _tile(s, XATTN_TM)
    scale = MEM_HD ** -0.5

    def kern(h_ref, w_ref, k_ref, v_ref, q_ref, o_ref):
        q = jnp.dot(h_ref[...], w_ref[...], preferred_element_type=F32).astype(BF16)
        q_ref[...] = q
        sc = lax.dot_general(q, k_ref[...], (NT, ((), ())), preferred_element_type=F32) * scale
        p = _softmax_rows(sc)
        o_ref[...] = jnp.dot(p.astype(BF16), v_ref[...], preferred_element_type=F32).astype(BF16)

    blk = BS((tm, MEM_HD), lambda i, hd: (i, hd))
    return _call(
        kern, name="xattn_fwd", grid=(s // tm, MEM_H),
        in_specs=[BS((tm, DM), lambda i, hd: (i, 0)), BS((DM, MEM_HD), lambda i, hd: (0, hd)),
                  BS((None, mlen, MEM_HD), lambda i, hd: (hd, 0, 0)),
                  BS((None, mlen, MEM_HD), lambda i, hd: (MEM_H + hd, 0, 0))],
        out_specs=[blk, blk],
        out_shape=[jax.ShapeDtypeStruct((s, DM), BF16), jax.ShapeDtypeStruct((s, DM), BF16)],
        compiler_params=_params(2))(h, wq, kv, kv)


def _xattn_bwd_call(q, kv, do):
    s = q.shape[0]
    mlen = kv.shape[1]
    tm = _tile(s, XATTN_TM)
    scale = MEM_HD ** -0.5

    def kern(q_ref, k_ref, v_ref, do_ref, dq_ref, dkv_ref):
        qv, kvv, vv, dov = q_ref[...], k_ref[...], v_ref[...], do_ref[...]
        sc = lax.dot_general(qv, kvv, (NT, ((), ())), preferred_element_type=F32) * scale
        p = _softmax_rows(sc)
        dp = lax.dot_general(dov, vv, (NT, ((), ())), preferred_element_type=F32)
        ds = (p * (dp - jnp.sum(dp * p, axis=-1, keepdims=True)) * scale).astype(BF16)
        dq_ref[...] = jnp.dot(ds, kvv, preferred_element_type=F32).astype(BF16)
        dk = lax.dot_general(ds, qv, (TN, ((), ())), preferred_element_type=F32)
        dv = lax.dot_general(p.astype(BF16), dov, (TN, ((), ())), preferred_element_type=F32)

        @pl.when(pl.program_id(1) == 0)
        def _():
            dkv_ref[0] = dk
            dkv_ref[1] = dv

        @pl.when(pl.program_id(1) > 0)
        def _():
            dkv_ref[0] += dk
            dkv_ref[1] += dv

    blk = BS((tm, MEM_HD), lambda hd, i: (i, hd))
    return _call(
        kern, name="xattn_bwd", grid=(MEM_H, s // tm),
        in_specs=[blk, BS((None, mlen, MEM_HD), lambda hd, i: (hd, 0, 0)),
                  BS((None, mlen, MEM_HD), lambda hd, i: (MEM_H + hd, 0, 0)), blk],
        out_specs=[blk, BS((2, None, mlen, MEM_HD), lambda hd, i: (0, hd, 0, 0))],
        out_shape=[jax.ShapeDtypeStruct((s, DM), BF16), jax.ShapeDtypeStruct((2, MEM_H, mlen, MEM_HD), F32)],
        compiler_params=_params(2))(q, kv, kv, do)


def _cross_fwd(x, mem, gpre, gmem, gpost, wq, wkv, wo, after=None):
    h = _norm_fwd(x, gpre, after)
    mn = _norm_fwd(mem, gmem)
    kv = _bmm_nn("xattn_kv", mn, wkv, BF16)
    q, o = _xattn_fwd_call(h, wq, kv)
    y = _mm_nn("xattn_out", o, wo)
    return _norm_res(x, y, gpost), (x, h, mn, kv, q, o, y)


def _cross_bwd(dxo, saved, mem, gpre, gmem, gpost, wq, wkv, wo, after=None):
    x, h, mn, kv, q, o, y = saved
    mlen = mem.shape[0]
    dy, dgpost = _norm_bwd(y, dxo, gpost, None, BF16, after)
    do = _mm_nt("xattn_do", dy, wo, BF16)
    dwo = _mm_tn("xattn_dwo", o, dy)
    dq, dkv = _xattn_bwd_call(q, kv, do)
    dwq = _mm_tn("xattn_dwq", h, dq)
    dh = _mm_nt("xattn_dh", dq, wq)
    dkv8 = dkv.reshape(8, mlen, MEM_HD)
    dwkv = _bmm_tn("xattn_dwkv", mn, dkv8)
    dmn = _bmm_nt_sum("xattn_dmn", dkv8, wkv)
    _, dgmem = _norm_bwd(mem, dmn, gmem, None, BF16)
    dx, dgpre = _norm_bwd(x, dh, gpre, dxo, F32)
    return dx, dgpre, dgmem, dgpost, dwq, dwkv, dwo


def _log_sigmoid(z):
    return jnp.minimum(z, 0.0) - jnp.log1p(jnp.exp(-jnp.abs(z)))


def _lane_scan_steps():
    return (1, 2, 4, 8, 16, 32, 64)


def _fox_cum(frow, bfb):
    s = frow.shape[1]

    def kern(f_ref, b_ref, o_ref):
        lane = lax.broadcasted_iota(jnp.int32, (FOX_H, LANE), 1)
        carry = jnp.zeros((FOX_H, 1), F32)
        for c in range(s // LANE):
            sl = slice(c * LANE, (c + 1) * LANE)
            lf = _log_sigmoid(f_ref[:, sl] + b_ref[...])
            v = lf
            for d in _lane_scan_steps():
                v = v + jnp.where(lane >= d, pltpu.roll(v, d, 1), 0.0)
            o_ref[:, sl] = v + carry
            carry = carry + jnp.sum(lf, axis=1, keepdims=True)

    return _call(kern, name="fox_cum", out_shape=jax.ShapeDtypeStruct((FOX_H, s), F32),
                          compiler_params=pltpu.CompilerParams(vmem_limit_bytes=VMEM_LIMIT))(frow, bfb)


def _fox_dlogf(dcq, dck, frow, bfb):
    s = frow.shape[1]

    def kern(q_ref, d_ref, f_ref, b_ref, df_ref, db_ref):
        lane = lax.broadcasted_iota(jnp.int32, (FOX_H, LANE), 1)
        carry = jnp.zeros((FOX_H, 1), F32)
        dbf = jnp.zeros((FOX_H, 1), F32)
        for c in reversed(range(s // LANE)):
            sl = slice(c * LANE, (c + 1) * LANE)
            dc = q_ref[:, sl] - d_ref[:, sl]
            v = dc
            for d in _lane_scan_steps():
                v = v + jnp.where(lane < LANE - d, pltpu.roll(v, LANE - d, 1), 0.0)
            v = v + carry
            carry = carry + jnp.sum(dc, axis=1, keepdims=True)
            df = v * jax.nn.sigmoid(-(f_ref[:, sl] + b_ref[...]))
            df_ref[:, sl] = df
            dbf = dbf + jnp.sum(df, axis=1, keepdims=True)
        db_ref[...] = jnp.broadcast_to(dbf, (FOX_H, LANE))

    return _call(kern, name="fox_dlogf",
                          out_shape=[jax.ShapeDtypeStruct((FOX_H, s), F32), jax.ShapeDtypeStruct((FOX_H, LANE), F32)],
                          compiler_params=pltpu.CompilerParams(vmem_limit_bytes=VMEM_LIMIT))(dcq, dck, frow, bfb)


FOX_TQ = 512
Q_COL, K_COL, V_COL = 0, FOX_W // LANE, 2 * FOX_W // LANE
B_COL = 3 * FOX_W // LANE
C_COL = B_COL + SC_W // LANE
U_COL = C_COL + SC_W // LANE


def _bf16_terms(c):
    hi = c.astype(BF16).astype(F32)
    mid = (c - hi).astype(BF16).astype(F32)
    return hi, mid, (c - hi - mid).astype(BF16).astype(F32)


def _fox_operands(qv, kv, cq, ck, lane, hh, scale):
    sel = (lane < FOX_HD) if hh == 0 else (lane >= FOX_HD)
    b0 = FOX_HD if hh == 0 else 0
    qa = jnp.where(sel, qv * scale, 0.0)
    ka = jnp.where(sel, kv, 0.0)
    for n, (tq_, tk_) in enumerate(zip(_bf16_terms(cq), _bf16_terms(ck))):
        qa = jnp.where(lane == b0 + n, tq_, jnp.where(lane == b0 + 3 + n, 1.0, qa))
        ka = jnp.where(lane == b0 + n, 1.0, jnp.where(lane == b0 + 3 + n, -tk_, ka))
    return sel, qa.astype(BF16), ka.astype(BF16)


def _fox_logits(qa, ka, causal):
    sc = lax.dot_general(qa, ka, (NT, ((), ())), preferred_element_type=F32)
    return sc if causal is None else jnp.where(causal, sc, NEG)


def _fox_prep(proj, cumc):
    s = proj.shape[0]
    tp = _tile(s, 512)
    scale = FOX_HD ** -0.5

    def kern(q_ref, k_ref, c_ref, qa_ref, ka_ref):
        lane = lax.broadcasted_iota(jnp.int32, (tp, LANE), 1)
        for hh in range(2):
            _, qa_ref[hh], ka_ref[hh] = _fox_operands(q_ref[...], k_ref[...], c_ref[hh], c_ref[hh], lane, hh, scale)

    pair = BS((2, tp, LANE), lambda hp, i: (hp, i, 0))
    shp = jax.ShapeDtypeStruct((FOX_H, s, LANE), BF16)
    return _call(kern, name="fox_prep", grid=(4, s // tp),
                 in_specs=[BS((tp, LANE), lambda hp, i: (i, Q_COL + hp)), BS((tp, LANE), lambda hp, i: (i, K_COL + hp)), pair],
                 out_specs=[pair, pair], out_shape=[shp, shp], compiler_params=_params(2))(proj, proj, cumc)


def _fox_fwd_call(proj, qa, ka):
    s = proj.shape[0]
    tq = _tile(s, FOX_TQ)
    nq = s // tq

    def kern(qa_ref, ka_ref, v_ref, o_ref, lse_ref, m_s, l_s, acc_s):
        i = pl.program_id(1)
        j = pl.program_id(2)
        lane = lax.broadcasted_iota(jnp.int32, (tq, LANE), 1)

        @pl.when(j == 0)
        def _():
            m_s[...] = jnp.full(m_s.shape, NEG, F32)
            l_s[...] = jnp.zeros(l_s.shape, F32)
            acc_s[...] = jnp.zeros(acc_s.shape, F32)

        def step(diagonal):
            vb = v_ref[...].astype(BF16)
            causal = (lax.broadcasted_iota(jnp.int32, (tq, tq), 0) >= lax.broadcasted_iota(jnp.int32, (tq, tq), 1)
                      if diagonal else None)
            for hh in range(2):
                sc = _fox_logits(qa_ref[hh], ka_ref[hh], causal)
                m_prev = m_s[hh]
                m_new = jnp.maximum(m_prev, jnp.max(sc, axis=-1, keepdims=True))
                alpha = jnp.exp(m_prev - m_new)
                p = jnp.exp(sc - m_new)
                l_s[hh] = alpha * l_s[hh] + jnp.sum(p, axis=-1, keepdims=True)
                acc_s[hh] = alpha * acc_s[hh] + jnp.dot(p.astype(BF16), vb, preferred_element_type=F32)
                m_s[hh] = m_new

        @pl.when(j < i)
        def _():
            step(False)

        @pl.when(j == i)
        def _():
            step(True)
            o_ref[...] = jnp.where(lane < FOX_HD, acc_s[0] / l_s[0], acc_s[1] / l_s[1])
            for hh in range(2):
                lse_ref[hh] = jnp.broadcast_to(m_s[hh] + jnp.log(l_s[hh]), (tq, LANE))

    kvi = lambda hp, i, j: jnp.minimum(j, i)
    return _call(
        kern, name="fox_fwd", grid=(4, nq, nq),
        in_specs=[BS((2, tq, LANE), lambda hp, i, j: (hp, i, 0)),
                  BS((2, tq, LANE), lambda hp, i, j: (hp, kvi(hp, i, j), 0)),
                  BS((tq, LANE), lambda hp, i, j: (kvi(hp, i, j), V_COL + hp))],
        out_specs=[BS((tq, LANE), lambda hp, i, j: (i, hp)), BS((2, tq, LANE), lambda hp, i, j: (hp, i, 0))],
        out_shape=[jax.ShapeDtypeStruct((s, FOX_W), F32), jax.ShapeDtypeStruct((FOX_H, s, LANE), F32)],
        scratch_shapes=[pltpu.VMEM((2, tq, 1), F32), pltpu.VMEM((2, tq, 1), F32), pltpu.VMEM((2, tq, LANE), F32)],
        compiler_params=_params(3))(qa, ka, proj)


ROWSUM_M = 16


def _fox_bwd_call(proj, o, lse, dcat, qa, ka):
    s = proj.shape[0]
    tq = _tile(s, FOX_TQ)
    nq = s // tq
    reps = tq // LANE
    scale = FOX_HD ** -0.5

    def kern(qa_ref, ka_ref, v_ref, do_ref, o_ref, lse_ref, dq_ref, dk_ref, dv_ref, dck_ref, dcq_ref):
        j = pl.program_id(1)
        i = pl.program_id(2)
        lane = lax.broadcasted_iota(jnp.int32, (tq, LANE), 1)
        ones = jnp.ones((ROWSUM_M, tq), BF16)

        @pl.when((j == 0) & (i == 0))
        def _():
            dq_ref[...] = jnp.zeros(dq_ref.shape, F32)
            dcq_ref[...] = jnp.zeros(dcq_ref.shape, F32)

        @pl.when(i == j)
        def _():
            dk_ref[...] = jnp.zeros(dk_ref.shape, F32)
            dv_ref[...] = jnp.zeros(dv_ref.shape, F32)
            dck_ref[...] = jnp.zeros(dck_ref.shape, F32)

        def step(diagonal):
            dov = do_ref[...]
            ov = o_ref[...]
            vb = v_ref[...].astype(BF16)
            causal = (lax.broadcasted_iota(jnp.int32, (tq, tq), 0) >= lax.broadcasted_iota(jnp.int32, (tq, tq), 1)
                      if diagonal else None)
            dq_t = jnp.zeros((tq, LANE), F32)
            dk_t = jnp.zeros((tq, LANE), F32)
            dv_t = jnp.zeros((tq, LANE), F32)
            for hh in range(2):
                sel = (lane < FOX_HD) if hh == 0 else (lane >= FOX_HD)
                qa, ka = qa_ref[hh], ka_ref[hh]
                dom32 = jnp.where(sel, dov, 0.0)
                dom = dom32.astype(BF16)
                sc = _fox_logits(qa, ka, causal)
                p = jnp.exp(sc - jnp.tile(lse_ref[hh], (1, reps)))
                dp = lax.dot_general(dom, vb, (NT, ((), ())), preferred_element_type=F32)
                delta = jnp.sum(dom32 * ov, axis=-1, keepdims=True)
                ds = p * (dp - delta)
                dsb = ds.astype(BF16)
                dq_t = jnp.where(sel, jnp.dot(dsb, ka, preferred_element_type=F32) * scale, dq_t)
                dk_t = jnp.where(sel, lax.dot_general(dsb, qa, (TN, ((), ())), preferred_element_type=F32), dk_t)
                dv_t = dv_t + lax.dot_general(p.astype(BF16), dom, (TN, ((), ())), preferred_element_type=F32)
                dck_ref[hh] += jnp.sum(ds, axis=0, keepdims=True)
                ds_lo = (ds - dsb.astype(F32)).astype(BF16)
                dcq_ref[hh, i] += (lax.dot_general(ones, dsb, (NT, ((), ())), preferred_element_type=F32)
                                   + lax.dot_general(ones, ds_lo, (NT, ((), ())), preferred_element_type=F32))
            rows = pl.ds(pl.multiple_of(i * tq, tq), tq)
            dq_ref[rows, :] += dq_t
            dk_ref[...] += dk_t
            dv_ref[...] += dv_t

        @pl.when(i > j)
        def _():
            step(False)

        @pl.when(i == j)
        def _():
            step(True)

    qi = lambda hp, j, i: jnp.maximum(i, j)
    return _call(
        kern, name="fox_bwd", grid=(4, nq, nq),
        in_specs=[BS((2, tq, LANE), lambda hp, j, i: (hp, qi(hp, j, i), 0)),
                  BS((2, tq, LANE), lambda hp, j, i: (hp, j, 0)),
                  BS((tq, LANE), lambda hp, j, i: (j, V_COL + hp)),
                  BS((tq, LANE), lambda hp, j, i: (qi(hp, j, i), hp)),
                  BS((tq, LANE), lambda hp, j, i: (qi(hp, j, i), hp)),
                  BS((2, tq, LANE), lambda hp, j, i: (hp, qi(hp, j, i), 0))],
        out_specs=[BS((s, LANE), lambda hp, j, i: (0, hp)), BS((tq, LANE), lambda hp, j, i: (j, hp)),
                   BS((tq, LANE), lambda hp, j, i: (j, hp)), BS((2, 1, tq), lambda hp, j, i: (hp, 0, j)),
                   BS((2, nq, ROWSUM_M, tq), lambda hp, j, i: (hp, 0, 0, 0))],
        out_shape=[jax.ShapeDtypeStruct((s, FOX_W), F32), jax.ShapeDtypeStruct((s, FOX_W), F32),
                   jax.ShapeDtypeStruct((s, FOX_W), F32), jax.ShapeDtypeStruct((FOX_H, 1, s), F32),
                   jax.ShapeDtypeStruct((FOX_H, nq, ROWSUM_M, tq), F32)],
        compiler_params=_params(3))(qa, ka, proj, dcat, o, lse)


def _shift_down(v, d, row):
    return jnp.where(row >= d, pltpu.roll(v, d, 0), 0.0)


def _shift_up(v, d, row, n):
    return jnp.where(row < n - d, pltpu.roll(v, n - d, 0), 0.0)


def _sconv_fwd(proj, convw):
    s = proj.shape[0]

    def kern(b_ref, c_ref, u_ref, w_ref, y_ref):
        row = lax.broadcasted_iota(jnp.int32, (s, LANE), 0)
        z = c_ref[...] * u_ref[...]
        conv = w_ref[2:3, :] * z + w_ref[1:2, :] * _shift_down(z, 1, row) + w_ref[0:1, :] * _shift_down(z, 2, row)
        y_ref[...] = (b_ref[...] * conv).astype(BF16)

    col = lambda base: BS((s, LANE), lambda cb: (0, base + cb))
    return _call(kern, name="sconv_fwd", grid=(SC_W // LANE,),
                          in_specs=[col(B_COL), col(C_COL), col(U_COL), BS((SC_K, LANE), lambda cb: (0, cb))],
                          out_specs=BS((s, LANE), lambda cb: (0, cb)),
                          out_shape=jax.ShapeDtypeStruct((s, SC_W), BF16), compiler_params=_params(1))(proj, proj, proj, convw)


def _sconv_bwd(proj, convw, dcat):
    s = proj.shape[0]

    def kern(b_ref, c_ref, u_ref, w_ref, dy_ref, db_ref, dc_ref, du_ref, dw_ref):
        row = lax.broadcasted_iota(jnp.int32, (s, LANE), 0)
        cv, uv, dyv = c_ref[...], u_ref[...], dy_ref[...]
        z = cv * uv
        z1 = _shift_down(z, 1, row)
        z2 = _shift_down(z, 2, row)
        conv = w_ref[2:3, :] * z + w_ref[1:2, :] * z1 + w_ref[0:1, :] * z2
        db_ref[...] = dyv * conv
        dcv = dyv * b_ref[...]
        dz = w_ref[2:3, :] * dcv + w_ref[1:2, :] * _shift_up(dcv, 1, row, s) + w_ref[0:1, :] * _shift_up(dcv, 2, row, s)
        dc_ref[...] = dz * uv
        du_ref[...] = dz * cv
        dw_ref[0:1, :] = jnp.sum(dcv * z2, axis=0, keepdims=True)
        dw_ref[1:2, :] = jnp.sum(dcv * z1, axis=0, keepdims=True)
        dw_ref[2:3, :] = jnp.sum(dcv * z, axis=0, keepdims=True)

    col = lambda base: BS((s, LANE), lambda cb: (0, base + cb))
    out = BS((s, LANE), lambda cb: (0, cb))
    wspec = BS((SC_K, LANE), lambda cb: (0, cb))
    act = jax.ShapeDtypeStruct((s, SC_W), F32)
    return _call(kern, name="sconv_bwd", grid=(SC_W // LANE,),
                          in_specs=[col(B_COL), col(C_COL), col(U_COL), wspec, col(FOX_W // LANE)],
                          out_specs=[out, out, out, wspec],
                          out_shape=[act, act, act, jax.ShapeDtypeStruct((SC_K, SC_W), F32)],
                          compiler_params=_params(1))(proj, proj, proj, convw, dcat)


def _fox_layer_fwd(x, gpre, gpost, wall, bfb, convw, wout, after=None):
    s = x.shape[0]
    h = _norm_fwd(x, gpre, after)
    proj = _mm_nt_cols("fox_proj", h, wall, AB_PAD // 5)
    frow = proj[:, 3 * FOX_W + 3 * SC_W:3 * FOX_W + 3 * SC_W + FOX_H].T
    cumr = _fox_cum(frow, bfb)
    qa, ka = _fox_prep(proj, jnp.broadcast_to(cumr[:, :, None], (FOX_H, s, LANE)))
    o, lse = _fox_fwd_call(proj, qa, ka)
    yb = _sconv_fwd(proj, convw)
    cat = jnp.concatenate([o.astype(BF16), yb], axis=1)
    y = _mm_nn("fox_out", cat, wout)
    return _norm_res(x, y, gpost), (x, h, proj, frow, qa, ka, o, lse, cat, y)


def _fox_layer_bwd(dxo, saved, gpre, gpost, wall, bfb, convw, wout, after=None):
    x, h, proj, frow, qa, ka, o, lse, cat, y = saved
    s = x.shape[0]
    dy, dgpost = _norm_bwd(y, dxo, gpost, None, BF16, after)
    dcat = _mm_nt("fox_dcat", dy, wout)
    dwout = _mm_tn("fox_dwout", cat, dy)
    db, dc, du, dconvw = _sconv_bwd(proj, convw, dcat)
    dq, dk, dv, dck, dcq = _fox_bwd_call(proj, o, lse, dcat, qa, ka)
    dfrow, dbf = _fox_dlogf(dcq[:, :, 0, :].reshape(FOX_H, s), dck.reshape(FOX_H, s), frow, bfb)
    dfcol = jnp.pad(dfrow.T, ((0, 0), (0, LANE - FOX_H)))
    dproj = jnp.concatenate([dq, dk, dv, db, dc, du, dfcol], axis=1).astype(BF16)
    dwall = _mm_tn_rows("fox_dwall", dproj, h, AB_PAD // 5)
    dh = _mm_nn("fox_dh", dproj, wall, vmem=VMEM_BIG)
    dx, dgpre = _norm_bwd(x, dh, gpre, dxo, F32)
    return dx, dgpre, dgpost, dwall, dbf[:, 0], dconvw, dwout


def _ab_pack(wt):
    nf = 3 * FOX_W
    return jnp.concatenate([wt[:nf], wt[nf + FOX_H:], wt[nf:nf + FOX_H],
                            jnp.zeros((AB_PAD - AB_IN, wt.shape[1]), wt.dtype)], axis=0)


def _ab_unpack(wt):
    nf = 3 * FOX_W
    nbcu = 3 * SC_W
    return jnp.concatenate([wt[:nf], wt[nf + nbcu:nf + nbcu + FOX_H], wt[nf:nf + nbcu]], axis=0)


NCH = DM // LANE
CH_PER_BLK = LRU_BW // LANE


def _chunk_spec(s, lead=0):
    return BS((None, s, LANE), lambda ch: (lead + ch // CH_PER_BLK, 0, ch % CH_PER_BLK))


def _vec_chunk(rows):
    return BS((rows, LANE), lambda ch: (0, ch))


def _neg_expm1(x):
    series = -x * (1.0 + x * (1 / 2) * (1.0 + x * (1 / 3) * (1.0 + x * (1 / 4) * (1.0 + x * (1 / 5) * (
        1.0 + x * (1 / 6) * (1.0 + x * (1 / 7)))))))
    return jnp.where(x > -0.25, series, 1.0 - jnp.exp(x))


def _softplus(z):
    return jnp.maximum(z, 0.0) + jnp.log1p(jnp.exp(-jnp.abs(z)))


GELU_C = math.sqrt(2.0 / math.pi)
GELU_A = 0.044715


def _gelu(x):
    return 0.5 * x * (1.0 + jnp.tanh(GELU_C * (x + GELU_A * x * x * x)))


def _gelu_grad(x):
    t = jnp.tanh(GELU_C * (x + GELU_A * x * x * x))
    return 0.5 * (1.0 + t) + 0.5 * x * (1.0 - t * t) * GELU_C * (1.0 + 3.0 * GELU_A * x * x)


def _lru_conv_fwd(gu, convw, convb):
    s = gu.shape[1]

    def kern(x_ref, w_ref, b_ref, u_ref):
        row = lax.broadcasted_iota(jnp.int32, (s, LANE), 0)
        xv = x_ref[...]
        u_ref[...] = (b_ref[...] + w_ref[3:4, :] * xv + w_ref[2:3, :] * _shift_down(xv, 1, row)
                      + w_ref[1:2, :] * _shift_down(xv, 2, row) + w_ref[0:1, :] * _shift_down(xv, 3, row))

    return _call(kern, name="lru_conv_fwd", grid=(NCH,),
                          in_specs=[_chunk_spec(s, LRU_NB), _vec_chunk(RG_K), _vec_chunk(1)], out_specs=_chunk_spec(s),
                          out_shape=jax.ShapeDtypeStruct((LRU_NB, s, LRU_BW), F32), compiler_params=_params(1))(gu, convw, convb)


def _lru_conv_bwd(dud, dug, gu, convw):
    s = gu.shape[1]

    def kern(d1_ref, d2_ref, x_ref, w_ref, dx_ref, dw_ref, db_ref):
        row = lax.broadcasted_iota(jnp.int32, (s, LANE), 0)
        du = d1_ref[...] + d2_ref[...]
        xv = x_ref[...]
        dx_ref[...] = (w_ref[3:4, :] * du + w_ref[2:3, :] * _shift_up(du, 1, row, s) + w_ref[1:2, :] * _shift_up(du, 2, row, s)
                       + w_ref[0:1, :] * _shift_up(du, 3, row, s)).astype(BF16)
        dw_ref[3:4, :] = jnp.sum(du * xv, axis=0, keepdims=True)
        for k in range(1, RG_K):
            dw_ref[3 - k:4 - k, :] = jnp.sum(du * _shift_down(xv, k, row), axis=0, keepdims=True)
        db_ref[...] = jnp.sum(du, axis=0, keepdims=True)

    return _call(kern, name="lru_conv_bwd", grid=(NCH,),
                          in_specs=[_chunk_spec(s), _chunk_spec(s), _chunk_spec(s, LRU_NB), _vec_chunk(RG_K)],
                          out_specs=[_chunk_spec(s), _vec_chunk(RG_K), _vec_chunk(1)],
                          out_shape=[jax.ShapeDtypeStruct((LRU_NB, s, LRU_BW), BF16),
                                     jax.ShapeDtypeStruct((RG_K, DM), F32), jax.ShapeDtypeStruct((1, DM), F32)],
                          compiler_params=_params(1))(dud, dug, gu, convw)


def _lru_gates(z_ref, bai_ref, lam_ref, uv):
    r = jax.nn.sigmoid(z_ref[0] + bai_ref[0:1, :])
    ig = jax.nn.sigmoid(z_ref[1] + bai_ref[1:2, :])
    sp = _softplus(-lam_ref[...])
    la = -RG_C * r * sp
    a = jnp.exp(la)
    sq = jnp.sqrt(_neg_expm1(2.0 * la))
    return r, ig, sp, a, sq


def _scan_steps(n):
    d, out = 1, []
    while d < n:
        out.append(d)
        d *= 2
    return out


def _lru_scan_fwd(z, bai, lam, u, gu):
    s = u.shape[1]
    zspec = BS((2, None, s, LANE), lambda ch: (0, ch // CH_PER_BLK, 0, ch % CH_PER_BLK))

    def kern(z_ref, bai_ref, lam_ref, u_ref, g_ref, hs_ref, y_ref):
        row = lax.broadcasted_iota(jnp.int32, (s, LANE), 0)
        uv = u_ref[...]
        _, ig, _, a, sq = _lru_gates(z_ref, bai_ref, lam_ref, uv)
        b = sq * (ig * uv)
        for d in _scan_steps(s):
            a_sh = jnp.where(row >= d, pltpu.roll(a, d, 0), 1.0)
            b = a * _shift_down(b, d, row) + b
            a = a * a_sh
        hs_ref[...] = b
        y_ref[...] = (_gelu(g_ref[...]) * b).astype(BF16)

    return _call(kern, name="lru_scan_fwd", grid=(NCH,),
                          in_specs=[zspec, _vec_chunk(2), _vec_chunk(1), _chunk_spec(s), _chunk_spec(s)],
                          out_specs=[_chunk_spec(s), BS((s, LANE), lambda ch: (0, ch))],
                          out_shape=[jax.ShapeDtypeStruct((LRU_NB, s, LRU_BW), F32), jax.ShapeDtypeStruct((s, DM), BF16)],
                          compiler_params=_params(1, VMEM_BIG))(z, bai, lam, u, gu)


def _lru_scan_bwd(dyp, z, bai, lam, u, gu, hs):
    s = u.shape[1]
    zspec = BS((2, None, s, LANE), lambda ch: (0, ch // CH_PER_BLK, 0, ch % CH_PER_BLK))

    def kern(dy_ref, z_ref, bai_ref, lam_ref, u_ref, g_ref, hs_ref, dg_ref, dz_ref, du_ref, dbai_ref, dlam_ref):
        row = lax.broadcasted_iota(jnp.int32, (s, LANE), 0)
        uv, gv, hv, dyv = u_ref[...], g_ref[...], hs_ref[...], dy_ref[...]
        r, ig, sp, a, sq = _lru_gates(z_ref, bai_ref, lam_ref, uv)
        dg_ref[...] = (dyv * hv * _gelu_grad(gv)).astype(BF16)
        g = dyv * _gelu(gv)
        an = _shift_up(a, 1, row, s)
        for d in _scan_steps(s):
            an_sh = jnp.where(row < s - d, pltpu.roll(an, s - d, 0), 1.0)
            g = an * _shift_up(g, d, row, s) + g
            an = an * an_sh
        da = g * _shift_down(hv, 1, row)
        dsq = g * (ig * uv)
        di = g * sq * uv
        du_ref[...] = g * sq * ig
        dla = da * a - dsq * (a * a / sq)
        dzr = dla * (-RG_C * sp) * r * (1.0 - r)
        dzi = di * ig * (1.0 - ig)
        dz_ref[0] = dzr.astype(BF16)
        dz_ref[1] = dzi.astype(BF16)
        dbai_ref[0:1, :] = jnp.sum(dzr, axis=0, keepdims=True)
        dbai_ref[1:2, :] = jnp.sum(dzi, axis=0, keepdims=True)
        dlam_ref[...] = jnp.sum(dla * r, axis=0, keepdims=True) * (RG_C * jax.nn.sigmoid(-lam_ref[...]))

    return _call(
        kern, name="lru_scan_bwd", grid=(NCH,),
        in_specs=[BS((s, LANE), lambda ch: (0, ch)), zspec, _vec_chunk(2), _vec_chunk(1), _chunk_spec(s), _chunk_spec(s),
                  _chunk_spec(s)],
        out_specs=[_chunk_spec(s), zspec, _chunk_spec(s), _vec_chunk(2), _vec_chunk(1)],
        out_shape=[jax.ShapeDtypeStruct((LRU_NB, s, LRU_BW), BF16), jax.ShapeDtypeStruct((2, LRU_NB, s, LRU_BW), BF16),
                   jax.ShapeDtypeStruct((LRU_NB, s, LRU_BW), F32), jax.ShapeDtypeStruct((2, DM), F32),
                   jax.ShapeDtypeStruct((1, DM), F32)],
        compiler_params=_params(1, VMEM_BIG))(dyp, z, bai, lam, u, gu, hs)


def _lru_layer_fwd(x, gpre, gpost, win, convw, convb, wai, bai, lam, wout, after=None):
    s = x.shape[0]
    tm = _tile(s, 512)
    h = _norm_fwd(x, gpre, after)
    gu = _bmm_nn("lru_in", h, win)
    u = _lru_conv_fwd(gu, convw, convb)
    z = _mm("lru_gate", u, wai, grid=(2, LRU_NB, s // tm, 1),
            a_spec=BS((None, tm, LRU_BW), lambda k, n, i, r: (n, i, 0)),
            b_spec=BS((None, None, LRU_BW, LRU_BW), lambda k, n, i, r: (k, n, 0, 0)),
            o_spec=BS((None, None, tm, LRU_BW), lambda k, n, i, r: (k, n, i, 0)),
            out_shape=(2, LRU_NB, s, LRU_BW), dn=NN)
    hs, yp = _lru_scan_fwd(z, bai, lam, u, gu)
    y = _mm_nn("lru_out", yp, wout)
    return _norm_res(x, y, gpost), (x, h, gu, u, z, hs, yp, y)


def _lru_layer_bwd(dxo, saved, gpre, gpost, win, convw, convb, wai, bai, lam, wout, after=None):
    x, h, gu, u, z, hs, yp, y = saved
    s = x.shape[0]
    tm = _tile(s, 512)
    dy, dgpost = _norm_bwd(y, dxo, gpost, None, BF16, after)
    dyp = _mm_nt("lru_dyp", dy, wout)
    dwout = _mm_tn("lru_dwout", yp, dy)
    dgate, dz, dud, dbai, dlam = _lru_scan_bwd(dyp, z, bai, lam, u, gu, hs)
    dwai = _mm("lru_dwai", u, dz, grid=(2, LRU_NB, s // tm),
               a_spec=BS((None, tm, LRU_BW), lambda k, n, r: (n, r, 0)),
               b_spec=BS((None, None, tm, LRU_BW), lambda k, n, r: (k, n, r, 0)),
               o_spec=BS((None, None, LRU_BW, LRU_BW), lambda k, n, r: (k, n, 0, 0)),
               out_shape=(2, LRU_NB, LRU_BW, LRU_BW), dn=TN)
    dug = _mm("lru_dug", dz, wai, grid=(LRU_NB, s // tm, 2),
              a_spec=BS((None, None, tm, LRU_BW), lambda n, i, k: (k, n, i, 0)),
              b_spec=BS((None, None, LRU_BW, LRU_BW), lambda n, i, k: (k, n, 0, 0)),
              o_spec=BS((None, tm, LRU_BW), lambda n, i, k: (n, i, 0)),
              out_shape=(LRU_NB, s, LRU_BW), dn=NT)
    duraw, dconvw, dconvb = _lru_conv_bwd(dud, dug, gu, convw)
    dgu = jnp.concatenate([dgate, duraw], axis=0)
    dwin = _bmm_tn("lru_dwin", h, dgu)
    dh = _bmm_nt_sum("lru_dh", dgu, win)
    dx, dgpre = _norm_bwd(x, dh, gpre, dxo, F32)
    return dx, dgpre, dgpost, dwin, dconvw, dconvb, dwai, dbai, dlam, dwout


CHIP_FLIPS = ((1, 0), (0, 1), (1, 1))


def _place():
    return lax.axis_index("x"), lax.axis_index("y"), lax.axis_index("c")


def _flip(v, f):
    return 1 - v if f else v


def _comm_params():
    return pltpu.CompilerParams(vmem_limit_bytes=VMEM_LIMIT)


def _small_gather(v):
    def body(v_ref, o_ref, send_sems, recv_sems, local_sem):
        x, y, c = _place()
        mine = 4 * x + 2 * y + c
        local = pltpu.make_async_copy(v_ref, o_ref.at[mine], local_sem)
        local.start()
        sends = []
        for k in range(1, NDEV):
            fx, fy, fc = (k >> 2) & 1, (k >> 1) & 1, k & 1
            sends.append(pltpu.make_async_remote_copy(
                src_ref=v_ref, dst_ref=o_ref.at[mine], send_sem=send_sems.at[k - 1], recv_sem=recv_sems.at[k - 1],
                device_id=(_flip(x, fx), _flip(y, fy), _flip(c, fc)), device_id_type=MESH))
        for cp in sends:
            cp.start()
        for k in range(1, NDEV):
            fx, fy, fc = (k >> 2) & 1, (k >> 1) & 1, k & 1
            src = 4 * _flip(x, fx) + 2 * _flip(y, fy) + _flip(c, fc)
            pltpu.make_async_remote_copy(src_ref=v_ref, dst_ref=o_ref.at[src], send_sem=send_sems.at[k - 1],
                                         recv_sem=recv_sems.at[k - 1], device_id=(x, y, c), device_id_type=MESH).wait_recv()
        for cp in sends:
            cp.wait_send()
        local.wait()

    return pl.pallas_call(body, name="small_gather", in_specs=[ANY], out_specs=ANY,
                          out_shape=jax.ShapeDtypeStruct((NDEV,) + v.shape, v.dtype),
                          scratch_shapes=[pltpu.SemaphoreType.DMA((NDEV - 1,)), pltpu.SemaphoreType.DMA((NDEV - 1,)),
                                          pltpu.SemaphoreType.DMA],
                          compiler_params=_comm_params())(v)


REL_CHIPS = ((0, 0),) + CHIP_FLIPS


def _rs_d2d(g5s, after=None):
    n = len(g5s)
    extra = () if after is None else (after,)

    def body(*refs):
        ins, gots = refs[:n], refs[n + len(extra):2 * n + len(extra)]
        send_sems, recv_sems = refs[2 * n + len(extra):]
        x, y, c = _place()
        copies = []
        for t in range(n):
            for f, (fx, fy) in enumerate(REL_CHIPS):
                copies.append(pltpu.make_async_remote_copy(
                    src_ref=ins[t].at[_flip(x, fx), _flip(y, fy), 1 - c], dst_ref=gots[t].at[f],
                    send_sem=send_sems.at[4 * t + f], recv_sem=recv_sems.at[4 * t + f], device_id=(x, y, 1 - c),
                    device_id_type=MESH))
        for cp in copies:
            cp.start()
        for cp in copies:
            cp.wait()

    out = [jax.ShapeDtypeStruct((4,) + g.shape[3:], F32) for g in g5s]
    return pl.pallas_call(body, name="rs_d2d", in_specs=[ANY] * (n + len(extra)), out_specs=[ANY] * n, out_shape=out,
                          scratch_shapes=[pltpu.SemaphoreType.DMA((4 * n,)), pltpu.SemaphoreType.DMA((4 * n,))],
                          compiler_params=_comm_params())(*g5s, *extra)


HBM = pl.BlockSpec(memory_space=pltpu.HBM)
SEM = pl.BlockSpec(memory_space=pltpu.SEMAPHORE)
EFFECT = pltpu.SideEffectType.DATAFLOW_SIDE_EFFECTING


def _in_hbm(a):
    return pltpu.with_memory_space_constraint(a, pltpu.HBM)


def _rs_ici_copies(ins, lands, send_sems, recv_sems):
    x, y, c = _place()
    return [pltpu.make_async_remote_copy(
        src_ref=ins[t].at[f], dst_ref=lands[t].at[f], send_sem=send_sems.at[3 * t + f], recv_sem=recv_sems.at[3 * t + f],
        device_id=(_flip(x, fx), _flip(y, fy), c), device_id_type=MESH)
        for t in range(len(ins)) for f, (fx, fy) in enumerate(CHIP_FLIPS)]


def _rs_ici_start(parts, name):
    n = len(parts)

    def body(*refs):
        ins, lands = refs[:n], refs[n:2 * n]
        send_sems, recv_sems = refs[2 * n], refs[2 * n + 1]
        token = refs[-1]
        for cp in _rs_ici_copies(ins, lands, send_sems, recv_sems):
            cp.start()
        token[...] = jnp.zeros(token.shape, token.dtype)

    thru = [pltpu.HBM(p.shape, p.dtype) for p in parts]
    res = pl.pallas_call(
        body, name=name, in_specs=[HBM] * (2 * n),
        out_shape=(pltpu.SemaphoreType.DMA((3 * n,)), pltpu.SemaphoreType.DMA((3 * n,)), *thru, *thru,
                   jax.ShapeDtypeStruct((8, LANE), F32)),
        out_specs=(SEM, SEM, *([HBM] * (2 * n)), pl.BlockSpec(memory_space=pltpu.VMEM)),
        input_output_aliases={i: 2 + i for i in range(2 * n)},
        compiler_params=pltpu.CompilerParams(has_side_effects=EFFECT),
    )(*[_in_hbm(p) for p in parts], *[_in_hbm(lax.empty(p.shape, p.dtype)) for p in parts])
    return res[:-1], res[-1]


def _rs_ici_wait(state, after, name):
    n = (len(state) - 2) // 2

    def body(*refs):
        send_sems, recv_sems = refs[0], refs[1]
        ins, lands = refs[2:2 + n], refs[2 + n:2 + 2 * n]
        for cp in _rs_ici_copies(ins, lands, send_sems, recv_sems):
            cp.wait_send()
            cp.wait_recv()

    thru = [pltpu.HBM(s.shape, s.dtype) for s in state[2:]]
    res = pl.pallas_call(
        body, name=name, in_specs=[SEM, SEM] + [HBM] * (2 * n) + [ANY], out_shape=tuple(thru),
        out_specs=tuple([HBM] * (2 * n)), input_output_aliases={2 + i: i for i in range(2 * n)},
        compiler_params=pltpu.CompilerParams(has_side_effects=EFFECT),
    )(*state, after)
    return list(res[n:])


def _ag_copies(shards, lands, send_sems, recv_sems):
    x, y, c = _place()
    mine = 4 * x + 2 * y + c
    peers = [(x, y, 1 - c)] + [(_flip(x, fx), _flip(y, fy), c) for fx, fy in CHIP_FLIPS]
    return [pltpu.make_async_remote_copy(
        src_ref=shards[t], dst_ref=lands[t].at[mine], send_sem=send_sems.at[4 * t + k], recv_sem=recv_sems.at[4 * t + k],
        device_id=peer, device_id_type=MESH) for t in range(len(shards)) for k, peer in enumerate(peers)]


def _ag_start(shards, after, name):
    n = len(shards)

    def body(*refs):
        ins, lands = refs[:n], refs[n:2 * n]
        send_sems, recv_sems = refs[2 * n + 1], refs[2 * n + 2]
        token = refs[-1]
        for cp in _ag_copies(ins, lands, send_sems, recv_sems):
            cp.start()
        token[...] = jnp.zeros(token.shape, token.dtype)

    thru = [pltpu.HBM(s.shape, s.dtype) for s in shards]
    land = [pltpu.HBM((NDEV,) + s.shape, s.dtype) for s in shards]
    res = pl.pallas_call(
        body, name=name, in_specs=[HBM] * (2 * n) + [ANY],
        out_shape=(pltpu.SemaphoreType.DMA((4 * n,)), pltpu.SemaphoreType.DMA((4 * n,)), *thru, *land,
                   jax.ShapeDtypeStruct((8, LANE), F32)),
        out_specs=(SEM, SEM, *([HBM] * (2 * n)), pl.BlockSpec(memory_space=pltpu.VMEM)),
        input_output_aliases={i: 2 + i for i in range(2 * n)},
        compiler_params=pltpu.CompilerParams(has_side_effects=EFFECT),
    )(*[_in_hbm(s) for s in shards], *[_in_hbm(lax.empty((NDEV,) + s.shape, s.dtype)) for s in shards], after)
    return res[:-1], res[-1]


def _ag_wait(state, after, name):
    n = (len(state) - 2) // 2

    def body(*refs):
        send_sems, recv_sems = refs[0], refs[1]
        ins, lands = refs[2:2 + n], refs[2 + n:2 + 2 * n]
        for cp in _ag_copies(ins, lands, send_sems, recv_sems):
            cp.wait_send()
            cp.wait_recv()

    thru = [pltpu.HBM(s.shape, s.dtype) for s in state[2:]]
    res = pl.pallas_call(
        body, name=name, in_specs=[SEM, SEM] + [HBM] * (2 * n) + [ANY], out_shape=tuple(thru),
        out_specs=tuple([HBM] * (2 * n)), input_output_aliases={2 + i: i for i in range(2 * n)},
        compiler_params=pltpu.CompilerParams(has_side_effects=EFFECT),
    )(*state, after)
    return list(res[:n]), list(res[n:])


def _ag_finish(shards, lands):
    n = len(shards)

    def body(*refs):
        ins, outs, stage = refs[:n], refs[2 * n:3 * n], refs[3 * n:4 * n]
        send_sems, recv_sems, local_sems = refs[4 * n:]
        x, y, c = _place()
        chips = [(_flip(x, fx), _flip(y, fy)) for fx, fy in CHIP_FLIPS]

        def passing(t, j, core, to):
            blk = outs[t].at[4 * chips[j][0] + 2 * chips[j][1] + core]
            return pltpu.make_async_remote_copy(src_ref=blk, dst_ref=blk, send_sem=send_sems.at[3 * t + j],
                                                recv_sem=recv_sems.at[3 * t + j], device_id=to, device_id_type=MESH)

        sends = [passing(t, j, c, (x, y, 1 - c)) for t in range(n) for j in range(3)]
        for cp in sends:
            cp.start()
        load = [pltpu.make_async_copy(ins[t], stage[t], local_sems.at[t]) for t in range(n)]
        mine = [pltpu.make_async_copy(stage[t], outs[t].at[4 * x + 2 * y + c], local_sems.at[t]) for t in range(n)]
        for cp in load:
            cp.start()
        for t in range(n):
            load[t].wait()
            mine[t].start()
        for t in range(n):
            for j in range(3):
                passing(t, j, 1 - c, (x, y, c)).wait_recv()
        for cp in sends:
            cp.wait_send()
        for cp in mine:
            cp.wait()

    return pl.pallas_call(
        body, name="ag_finish", in_specs=[ANY] * (2 * n), out_specs=[ANY] * n,
        out_shape=[jax.ShapeDtypeStruct(l.shape, l.dtype) for l in lands],
        input_output_aliases={n + i: i for i in range(n)},
        scratch_shapes=[pltpu.VMEM(s.shape, s.dtype) for s in shards]
        + [pltpu.SemaphoreType.DMA((3 * n,)), pltpu.SemaphoreType.DMA((3 * n,)), pltpu.SemaphoreType.DMA((n,))],
        compiler_params=_comm_params())(*shards, *lands)


def _row_tile(rows, largest=256):
    for t in (1024, 512, 256, 128, 64, 32, 16, 8):
        if t > largest:
            continue
        if rows % t == 0:
            return t
    return rows


def _rs_chip_sum(pos, g5, got):
    a, b = g5.shape[3:]
    ta = _row_tile(a, 1024)

    def kern(pos_ref, o_ref, g_ref, p_ref):
        p_ref[...] = (o_ref[...] + g_ref[...]).astype(BF16)

    def mine(f, i, pos_ref):
        return (pos_ref[0] ^ ((f + 1) & 1), pos_ref[1] ^ ((f + 1) >> 1), pos_ref[2], i, 0)

    spec = pltpu.PrefetchScalarGridSpec(
        num_scalar_prefetch=1, grid=(3, a // ta),
        in_specs=[BS((None, None, None, ta, b), mine), BS((None, ta, b), lambda f, i, pos_ref: (f + 1, i, 0))],
        out_specs=BS((None, ta, b), lambda f, i, pos_ref: (f, i, 0)))
    return _call(kern, name="rs_chip_sum", grid_spec=spec, out_shape=jax.ShapeDtypeStruct((3, a, b), BF16),
                          compiler_params=_params(2))(pos, g5, got)


def _rs_final_sum(pos, g5, got, recv):
    a, b = g5.shape[3:]
    ta = _row_tile(a, 1024)

    def kern(pos_ref, o_ref, g_ref, r_ref, s_ref):
        acc = o_ref[...] + g_ref[...]
        for f in range(3):
            acc = acc + r_ref[f].astype(F32)
        s_ref[...] = acc

    spec = pltpu.PrefetchScalarGridSpec(
        num_scalar_prefetch=1, grid=(a // ta,),
        in_specs=[BS((None, None, None, ta, b), lambda i, pos_ref: (pos_ref[0], pos_ref[1], pos_ref[2], i, 0)),
                  BS((None, ta, b), lambda i, pos_ref: (0, i, 0)), BS((3, ta, b), lambda i, pos_ref: (0, i, 0))],
        out_specs=BS((ta, b), lambda i, pos_ref: (i, 0)))
    return _call(kern, name="rs_final_sum", grid_spec=spec, out_shape=jax.ShapeDtypeStruct((a, b), F32),
                          compiler_params=_params(1))(pos, g5, got, recv)


def _rs_d2d_copies(ins, lands, send_sems, recv_sems):
    x, y, c = _place()
    return [pltpu.make_async_remote_copy(
        src_ref=ins[t].at[_flip(x, fx), _flip(y, fy), 1 - c], dst_ref=lands[t].at[f], send_sem=send_sems.at[4 * t + f],
        recv_sem=recv_sems.at[4 * t + f], device_id=(x, y, 1 - c), device_id_type=MESH)
        for t in range(len(ins)) for f, (fx, fy) in enumerate(REL_CHIPS)]


def _rs_d2d_start(g5s, name):
    n = len(g5s)

    def body(*refs):
        ins, lands = refs[:n], refs[n:2 * n]
        for cp in _rs_d2d_copies(ins, lands, refs[2 * n], refs[2 * n + 1]):
            cp.start()
        refs[-1][...] = jnp.zeros(refs[-1].shape, F32)

    thru = [pltpu.HBM(g.shape, g.dtype) for g in g5s]
    land = [pltpu.HBM((4,) + g.shape[3:], F32) for g in g5s]
    res = pl.pallas_call(
        body, name=name, in_specs=[HBM] * (2 * n),
        out_shape=(pltpu.SemaphoreType.DMA((4 * n,)), pltpu.SemaphoreType.DMA((4 * n,)), *thru, *land,
                   jax.ShapeDtypeStruct((8, LANE), F32)),
        out_specs=(SEM, SEM, *([HBM] * (2 * n)), pl.BlockSpec(memory_space=pltpu.VMEM)),
        input_output_aliases={i: 2 + i for i in range(2 * n)},
        compiler_params=pltpu.CompilerParams(has_side_effects=EFFECT),
    )(*[_in_hbm(g) for g in g5s], *[_in_hbm(lax.empty((4,) + g.shape[3:], F32)) for g in g5s])
    return res[:-1], res[-1]


def _rs_d2d_wait(state, after, name):
    n = (len(state) - 2) // 2

    def body(*refs):
        ins, lands = refs[2:2 + n], refs[2 + n:2 + 2 * n]
        for cp in _rs_d2d_copies(ins, lands, refs[0], refs[1]):
            cp.wait_send()
            cp.wait_recv()

    thru = [pltpu.HBM(s.shape, s.dtype) for s in state[2:]]
    res = pl.pallas_call(
        body, name=name, in_specs=[SEM, SEM] + [HBM] * (2 * n) + [ANY], out_shape=tuple(thru),
        out_specs=tuple([HBM] * (2 * n)), input_output_aliases={2 + i: i for i in range(2 * n)},
        compiler_params=pltpu.CompilerParams(has_side_effects=EFFECT),
    )(*state, after)
    return list(res[:n]), list(res[n:])


def _as_g5(grads):
    return [g.reshape((2, 2, 2) + g.shape[1:]) for g in grads]


def _rs_mid(g5s, gots, pos, tag):
    parts = [_rs_chip_sum(pos, g, got) for g, got in zip(g5s, gots)]
    state, token = _rs_ici_start(parts, "rs_ici_start_" + tag)
    return (g5s, gots, state, tag), token


def _rs_begin(grads, pos, tag, after=None):
    g5s = _as_g5(grads)
    return _rs_mid(g5s, _rs_d2d(g5s, after), pos, tag)


def _rs_end(pending, after, pos):
    g5s, gots, state, tag = pending
    recvs = _rs_ici_wait(state, after, "rs_ici_wait_" + tag)
    return [_rs_final_sum(pos, g, got, r) for g, got, r in zip(g5s, gots, recvs)]


def _sum_devices(v):
    _, r, _ = v.shape

    def kern(v_ref, o_ref):
        acc = v_ref[0]
        for d in range(1, NDEV):
            acc = acc + v_ref[d]
        o_ref[...] = acc

    return _call(kern, name="sum_devices", out_shape=jax.ShapeDtypeStruct((r, LANE), F32),
                          compiler_params=_comm_params())(v)


def _loss_head(xf, target):
    s = xf.shape[0]
    tm = _tile(s, 512)

    def kern(x_ref, t_ref, dx_ref, l_ref):
        err = x_ref[...] - t_ref[...]
        dx_ref[...] = err * (1.0 / DM)
        part = jnp.broadcast_to(0.5 * jnp.sum(jnp.mean(err * err, axis=-1, keepdims=True), axis=0, keepdims=True), (8, LANE))

        @pl.when(pl.program_id(0) == 0)
        def _():
            l_ref[...] = part

        @pl.when(pl.program_id(0) > 0)
        def _():
            l_ref[...] += part

    row = BS((tm, DM), lambda i: (i, 0))
    return _call(kern, name="loss_head", grid=(s // tm,), in_specs=[row, row],
                          out_specs=[row, BS((8, LANE), lambda i: (0, 0))],
                          out_shape=[jax.ShapeDtypeStruct((s, DM), F32), jax.ShapeDtypeStruct((8, LANE), F32)],
                          compiler_params=_params(1))(xf, target)


def _adamw(w, g, m, v, after=None):
    rows, cols = w.shape
    tr = _row_tile(rows)
    extra = () if after is None else (after,)

    def kern(w_ref, g_ref, m_ref, v_ref, *rest):
        d_ref, nm_ref, nv_ref = rest[-3:]
        gv = g_ref[...]
        nm = ADAM_B1 * m_ref[...] + (1.0 - ADAM_B1) * gv
        nv = ADAM_B2 * v_ref[...] + (1.0 - ADAM_B2) * (gv * gv)
        m_hat = nm / (1.0 - ADAM_B1 ** ADAM_STEP)
        v_hat = nv / (1.0 - ADAM_B2 ** ADAM_STEP)
        d_ref[...] = -ADAM_LR * (m_hat / (jnp.sqrt(v_hat) + ADAM_EPS) + ADAM_WD * w_ref[...])
        nm_ref[...] = nm
        nv_ref[...] = nv

    blk = BS((tr, cols), lambda i: (i, 0))
    shp = jax.ShapeDtypeStruct((rows, cols), F32)
    return _call(kern, name="adamw", grid=(rows // tr,), in_specs=[blk] * 4 + [ANY] * len(extra),
                          out_specs=[blk] * 3, out_shape=[shp] * 3, compiler_params=_params(1))(w, g, m, v, *extra)


def _adamw_nd(w, g, m, v, after=None):
    shape = w.shape
    two = (math.prod(shape[:-1]), shape[-1])
    return tuple(o.reshape(shape)
                 for o in _adamw(w.reshape(two), g.reshape(two), m.reshape(two), v.reshape(two), after))


def _pack_small(parts):
    flat = jnp.concatenate([p.reshape(-1) for p in parts])
    pad = (-flat.shape[0]) % (8 * LANE)
    return jnp.pad(flat, (0, pad)).reshape(-1, LANE)


def _unpack_small(packed, shapes, lead=()):
    flat = packed.reshape(lead + (-1,))
    out, off = [], 0
    for shp in shapes:
        n = math.prod(shp)
        out.append(flat[..., off:off + n].reshape(lead + tuple(shp)))
        off += n
    return out


WEIGHT_NAMES = ('g_mix_pre', 'g_mix_post', 'g_cross_pre', 'g_mem', 'g_cross_post', 'g_ffn_pre', 'g_ffn_post', 'w_xq',
                'w_xkv', 'w_xo', 'w_ffn_gu', 'w_ffn_down', 'ab_w_in', 'ab_b_f', 'ab_conv_w', 'ab_w_out', 'c_w_in',
                'c_conv_w', 'c_conv_b', 'c_w_a', 'c_b_a', 'c_w_i', 'c_b_i', 'c_lam', 'c_w_out')
BIG = ('w_xq', 'w_xkv', 'w_xo', 'w_ffn_gu', 'w_ffn_down', 'ab_w_in', 'ab_w_out', 'c_w_in', 'c_w_a', 'c_w_i', 'c_w_out')
SMALL_SHARDED = ('ab_conv_w', 'c_conv_w', 'c_conv_b', 'c_b_a', 'c_b_i', 'c_lam')
REPLICATED = ('g_mix_pre', 'g_mix_post', 'g_cross_pre', 'g_mem', 'g_cross_post', 'g_ffn_pre', 'g_ffn_post', 'ab_b_f')


def _small_full(name, gathered):
    nd = gathered.ndim
    return jnp.moveaxis(gathered, 0, nd - 2).reshape(gathered.shape[1:-1] + (NDEV * gathered.shape[-1],))


def _small_shard(full, dev):
    c = full.shape[-1] // NDEV
    return lax.dynamic_slice_in_dim(full, dev * c, c, axis=full.ndim - 1)


def kernel(x, mem, g_mix_pre, g_mix_post, g_cross_pre, g_mem, g_cross_post, g_ffn_pre, g_ffn_post, w_xq, w_xkv, w_xo, w_ffn_gu, w_ffn_down, ab_w_in, ab_b_f, ab_conv_w, ab_w_out, c_w_in, c_conv_w, c_conv_b, c_w_a, c_b_a, c_w_i, c_b_i, c_lam, c_w_out, loss_target, m_g_mix_pre, m_g_mix_post, m_g_cross_pre, m_g_mem, m_g_cross_post, m_g_ffn_pre, m_g_ffn_post, m_w_xq, m_w_xkv, m_w_xo, m_w_ffn_gu, m_w_ffn_down, m_ab_w_in, m_ab_b_f, m_ab_conv_w, m_ab_w_out, m_c_w_in, m_c_conv_w, m_c_conv_b, m_c_w_a, m_c_b_a, m_c_w_i, m_c_b_i, m_c_lam, m_c_w_out, v_g_mix_pre, v_g_mix_post, v_g_cross_pre, v_g_mem, v_g_cross_post, v_g_ffn_pre, v_g_ffn_post, v_w_xq, v_w_xkv, v_w_xo, v_w_ffn_gu, v_w_ffn_down, v_ab_w_in, v_ab_b_f, v_ab_conv_w, v_ab_w_out, v_c_w_in, v_c_conv_w, v_c_conv_b, v_c_w_a, v_c_b_a, v_c_w_i, v_c_b_i, v_c_lam, v_c_w_out):
    args = locals()
    w = {n: args[n] for n in WEIGHT_NAMES}
    mom = {n: args["m_" + n] for n in WEIGHT_NAMES}
    var = {n: args["v_" + n] for n in WEIGHT_NAMES}
    for t in (w, mom, var):
        t['w_ffn_gu'] = t['w_ffn_gu'].transpose(0, 2, 1)
    ab_t = [t['ab_w_in'].transpose(2, 0, 1) for t in (w, mom, var)]
    pos = jnp.stack([lax.axis_index("x"), lax.axis_index("y"), lax.axis_index("c")]).astype(jnp.int32)
    dev = 4 * pos[0] + 2 * pos[1] + pos[2]
    xs, mems, target = x[0], mem[0], loss_target[0]
    n_even, n_odd = (DEPTH + 1) // 2, DEPTH // 2

    small_shapes = [w[n].shape for n in SMALL_SHARDED]
    small_w_all = _small_gather(_pack_small([w[n] for n in SMALL_SHARDED]))
    gathered_small = _unpack_small(small_w_all, small_shapes, (NDEV,))
    small = {n: _small_full(n, g) for n, g in zip(SMALL_SHARDED, gathered_small)}
    ab_bfb = jnp.broadcast_to(ab_b_f[:, :, None], (n_even, FOX_H, LANE))
    c_bai = jnp.stack([small['c_b_a'].reshape(n_odd, DM), small['c_b_i'].reshape(n_odd, DM)], axis=1)
    row = lambda a, l: a[l][None]

    REST = ('w_xq', 'w_xkv', 'w_xo', 'w_ffn_gu', 'w_ffn_down')

    def mixer_names(l):
        return ('ab_w_in', 'ab_w_out') if l % 2 == 0 else ('c_w_in', 'c_w_a', 'c_w_i', 'c_w_out')

    def shards_of(l, names):
        out = []
        for n in names:
            if n == 'ab_w_in':
                s = ab_t[0][:, l // 2].astype(BF16)
            else:
                s = w[n][l if w[n].shape[0] == DEPTH else l // 2].astype(BF16)
            out.append(s.reshape(-1, s.shape[-1]))
        return out

    def mixer_weights(l, full):
        if l % 2 == 0:
            e = l // 2
            return (row(g_mix_pre, l), row(g_mix_post, l), _ab_pack(full['ab_w_in'].reshape(AB_IN, DM)), ab_bfb[e],
                    small['ab_conv_w'][e], full['ab_w_out'].reshape(DM, DM))
        o = l // 2
        gate_w = lambda g: g.reshape(NDEV, LRU_NB, LRU_BW // NDEV, LRU_BW).transpose(1, 0, 2, 3).reshape(
            LRU_NB, LRU_BW, LRU_BW)
        return (row(g_mix_pre, l), row(g_mix_post, l), full['c_w_in'], small['c_conv_w'][o], row(small['c_conv_b'], o),
                jnp.stack([gate_w(full['c_w_a']), gate_w(full['c_w_i'])]), c_bai[o], row(small['c_lam'], o),
                full['c_w_out'].reshape(DM, DM))

    def rest_weights(l, full):
        cross = (row(g_cross_pre, l), row(g_mem, l), row(g_cross_post, l), full['w_xq'].reshape(DM, DM), full['w_xkv'],
                 full['w_xo'].reshape(DM, DM))
        ffn = (row(g_ffn_pre, l), row(g_ffn_post, l), full['w_ffn_gu'], full['w_ffn_down'].reshape(D_FF, DM))
        return cross, ffn

    def gathered(state, names, after, tag):
        shards, lands = _ag_wait(state, after, "ag_wait_" + tag)
        full = _ag_finish(shards, lands)
        return dict(zip(names, full)), full[0]

    saved, weights = [], []
    h = xs
    names_of = lambda l: mixer_names(l) + REST
    states = {}
    st_m, _ = _ag_start(shards_of(0, mixer_names(0)), small_w_all, "ag_start_0m")
    st_r, _ = _ag_start(shards_of(0, REST), st_m[2], "ag_start_0r")
    states[1], token = _ag_start(shards_of(1, names_of(1)), st_r[2], "ag_start_1")
    full_m, _ = gathered(st_m, mixer_names(0), xs, "0m")
    for l in range(DEPTH):
        if l > 0:
            full, done = gathered(states[l], names_of(l), h, str(l))
            full_m = full_r = full
            token = None
            if l + 2 < DEPTH:
                states[l + 2], token = _ag_start(shards_of(l + 2, names_of(l + 2)), done, "ag_start_%d" % (l + 2))
        mixer = mixer_weights(l, full_m)
        h, s_mix = (_fox_layer_fwd if l % 2 == 0 else _lru_layer_fwd)(h, *mixer, after=token)
        token = None
        if l == 0:
            full_r, done = gathered(st_r, REST, h, "0r")
            states[2], token = _ag_start(shards_of(2, names_of(2)), done, "ag_start_2")
        cross, ffn = rest_weights(l, full_r)
        h, s_cross = _cross_fwd(h, mems, *cross, after=token)
        h, s_ffn = _ffn_fwd(h, *ffn)
        saved.append((s_mix, s_cross, s_ffn))
        weights.append((mixer, cross, ffn))
    mixer_args = lambda l: weights[l][0]
    cross_args = lambda l: weights[l][1]
    ffn_args = lambda l: weights[l][2]
    dx, loss_rep = _loss_head(h, target)
    loss = lax.psum(loss_rep[0, 0], ("x", "y", "c"))

    grads = {n: [None] * w[n].shape[0] for n in BIG}
    partial = {n: [None] * w[n].shape[0] for n in REPLICATED + SMALL_SHARDED}
    def finish(pending, after):
        state, names, where = pending
        for n, g in zip(names, _rs_end(state, after, pos)):
            grads[n][where[n]] = g

    def unit(layer, names):
        return [layer[n][1] for n in names], names, {n: layer[n][0] for n in names}

    d2d = ici = None
    token = None
    for l in reversed(range(DEPTH)):
        s_mix, s_cross, s_ffn = saved[l]
        dx, partial['g_ffn_pre'][l], partial['g_ffn_post'][l], dwgu, dwd = _ffn_bwd(dx, s_ffn, *ffn_args(l), after=token)
        token = None
        if d2d is not None:
            g5s, gots = _rs_d2d_wait(d2d[0], dx, "rs_d2d_wait_%d" % (l + 1))
            state, token = _rs_mid(g5s, gots, pos, str(l + 1))
            ici, d2d = (state,) + d2d[1:], None
        (dx, partial['g_cross_pre'][l], partial['g_mem'][l], partial['g_cross_post'][l], dwq, dwkv, dwo) = _cross_bwd(
            dx, s_cross, mems, *cross_args(l), after=token)
        token = None
        layer = {'w_xq': (l, dwq.reshape(NDEV, DM // NDEV, DM)), 'w_xkv': (l, dwkv), 'w_xo': (l, dwo.reshape(NDEV, DM // NDEV, DM)),
                 'w_ffn_gu': (l, dwgu), 'w_ffn_down': (l, dwd.reshape(NDEV, D_FF // NDEV, DM))}
        if l == 0:
            gs, names, where = unit(layer, REST)
            state, token = _rs_begin(gs, pos, "0r")
            ici_rest = (state, names, where)
        if l % 2 == 0:
            e = l // 2
            (dx, partial['g_mix_pre'][l], partial['g_mix_post'][l], dwall, partial['ab_b_f'][e], partial['ab_conv_w'][e],
             dwout) = _fox_layer_bwd(dx, s_mix, *mixer_args(l), after=token)
            layer['ab_w_in'] = (e, _ab_unpack(dwall).reshape(NDEV, AB_IN // NDEV, DM))
            layer['ab_w_out'] = (e, dwout.reshape(NDEV, DM // NDEV, DM))
        else:
            o = l // 2
            (dx, partial['g_mix_pre'][l], partial['g_mix_post'][l], dwin, partial['c_conv_w'][o], dconvb, dwai, dbai, dlam,
             dwout) = _lru_layer_bwd(dx, s_mix, *mixer_args(l), after=token)
            partial['c_conv_b'][o], partial['c_lam'][o] = dconvb[0], dlam[0]
            partial['c_b_a'][o], partial['c_b_i'][o] = dbai[0].reshape(LRU_NB, LRU_BW), dbai[1].reshape(LRU_NB, LRU_BW)
            rows = LRU_BW // NDEV
            by_dev = lambda d: d.reshape(LRU_NB, NDEV, rows, LRU_BW).transpose(1, 0, 2, 3).reshape(NDEV, LRU_NB * rows, LRU_BW)
            layer['c_w_in'] = (o, dwin)
            layer['c_w_a'] = (o, by_dev(dwai[0]))
            layer['c_w_i'] = (o, by_dev(dwai[1]))
            layer['c_w_out'] = (o, dwout.reshape(NDEV, DM // NDEV, DM))
        token = None
        if ici is not None:
            finish(ici, dx)
            ici = None
        if l > 0:
            gs, names, where = unit(layer, list(layer))
            state, token = _rs_d2d_start(_as_g5(gs), "rs_d2d_start_%d" % l)
            d2d = (state, names, where)
    small_names = REPLICATED + SMALL_SHARDED
    small_parts = [jnp.stack([p.reshape(w[n].shape[1:] if n in REPLICATED else small[n].shape[1:]) for p in partial[n]])
                   for n in small_names]
    small_all = _small_gather(_pack_small(small_parts))
    reduced = _unpack_small(_sum_devices(small_all), [p.shape for p in small_parts])
    grad = {}
    for n, g in zip(small_names, reduced):
        grad[n] = g if n in REPLICATED else _small_shard(g, dev)

    gs, names, where = unit(layer, mixer_names(0))
    state, token = _rs_begin(gs, pos, "0m", after=small_all)
    ici_mixer = (state, names, where)
    finish(ici_rest, dx)

    delta, new_m, new_v = {}, {}, {}
    last = mixer_names(0)
    for n in BIG:
        if n not in last:
            grad[n] = jnp.stack(grads[n]).reshape(w[n].shape)
            delta[n], new_m[n], new_v[n] = _adamw_nd(w[n], grad[n], mom[n], var[n], token)
            token = delta[n]
    shapes = [w[n].shape for n in small_names]
    packed = [_pack_small([t[n] for n in small_names]) for t in (w, grad, mom, var)]
    res_small = _adamw(*packed, after=token)
    for res, out in zip(res_small, (delta, new_m, new_v)):
        for n, val in zip(small_names, _unpack_small(res, shapes)):
            out[n] = val
    finish(ici_mixer, res_small[0])
    for n in last:
        if n == 'ab_w_in':
            g_t = jnp.stack(grads[n], axis=1)
            res = (g_t,) + _adamw_nd(ab_t[0], g_t, ab_t[1], ab_t[2])
            grad[n], delta[n], new_m[n], new_v[n] = (r.transpose(1, 2, 0) for r in res)
            continue
        grad[n] = jnp.stack(grads[n]).reshape(w[n].shape)
        delta[n], new_m[n], new_v[n] = _adamw_nd(w[n], grad[n], mom[n], var[n])

    for t in (grad, delta, new_m, new_v):
        t['w_ffn_gu'] = t['w_ffn_gu'].transpose(0, 2, 1)
    return (loss, dx[None], *[grad[n] for n in WEIGHT_NAMES], *[delta[n] for n in WEIGHT_NAMES],
            *[new_m[n] for n in WEIGHT_NAMES], *[new_v[n] for n in WEIGHT_NAMES])
```

```python
import math

import jax
import jax.numpy as jnp
from jax import lax
from jax.experimental import pallas as pl
from jax.experimental.pallas import tpu as pltpu

F32 = jnp.float32
BF16 = jnp.bfloat16
BS = pl.BlockSpec
ANY = pl.BlockSpec(memory_space=pl.ANY)
MESH = pl.DeviceIdType.MESH

DM = 1024
DEPTH = 4
EPS = 1e-6
NEG = -1e30
FOX_W = 512
FOX_HD = 64
FOX_H = 8
SC_W = 512
SC_K = 3
AB_IN = 3 * FOX_W + FOX_H + 3 * SC_W
AB_PAD = 3200
LRU_BW = 256
LRU_NB = 4
RG_K = 4
RG_C = 8.0
MEM_H = 4
MEM_HD = 256
D_FF = 2816
NDEV = 8
FFB = 2 * D_FF // NDEV
ADAM_LR, ADAM_B1, ADAM_B2, ADAM_EPS, ADAM_WD, ADAM_STEP = 0.001, 0.9, 0.999, 1e-08, 0.01, 10

LANE = 128
VMEM_LIMIT = 16 * 1024 * 1024
VMEM_MM = 32 * 1024 * 1024
VMEM_BIG = 40 * 1024 * 1024


def _params(ngrid, vmem=None):
    return pltpu.CompilerParams(dimension_semantics=("arbitrary",) * ngrid, vmem_limit_bytes=vmem or VMEM_LIMIT)


def _call(kern, **kwargs):
    return pl.pallas_call(kern, **kwargs)


TK_RED = 2048
TM_SUM = 512
MM_TM = 1024


def _tile(n, t):
    return t if n % t == 0 else n


def _mm(name, a, b, *, grid, a_spec, b_spec, o_spec, out_shape, dn, out_dtype=F32, vmem=None):
    nred = grid[-1]
    ngrid = len(grid)

    def kern(a_ref, b_ref, o_ref, *scratch):
        p = lax.dot_general(a_ref[...].astype(BF16), b_ref[...].astype(BF16), (dn, ((), ())),
                            preferred_element_type=F32)
        if nred == 1:
            o_ref[...] = p.astype(o_ref.dtype)
            return
        acc = scratch[0] if scratch else o_ref
        r = pl.program_id(ngrid - 1)

        @pl.when(r == 0)
        def _():
            acc[...] = p

        @pl.when(r > 0)
        def _():
            acc[...] += p

        if scratch:
            @pl.when(r == nred - 1)
            def _():
                o_ref[...] = acc[...].astype(o_ref.dtype)

    blk = tuple(d for d in o_spec.block_shape if d is not None)
    scratch = [pltpu.VMEM(blk, F32)] if (nred > 1 and out_dtype != F32) else []
    return _call(kern, name=name, grid=grid, in_specs=[a_spec, b_spec], out_specs=o_spec,
                          out_shape=jax.ShapeDtypeStruct(out_shape, out_dtype), scratch_shapes=scratch,
                          compiler_params=_params(ngrid, vmem or (VMEM_LIMIT if dn == TN else VMEM_MM)))(a, b)


NN = ((1,), (0,))
NT = ((1,), (1,))
TN = ((0,), (0,))


def _mm_nn(name, a, w, out_dtype=F32, tn=None, vmem=None):
    m, k = a.shape
    n = w.shape[1]
    tm = _tile(m, MM_TM)
    tn = n if tn is None else tn
    return _mm(name, a, w, grid=(m // tm, n // tn, 1), a_spec=BS((tm, k), lambda i, j, r: (i, 0)),
               b_spec=BS((k, tn), lambda i, j, r: (0, j)), o_spec=BS((tm, tn), lambda i, j, r: (i, j)),
               out_shape=(m, n), dn=NN, out_dtype=out_dtype, vmem=vmem)


def _mm_nt_cols(name, a, wt, tn):
    m, k = a.shape
    n = wt.shape[0]
    tm = _tile(m, MM_TM)
    return _mm(name, a, wt, grid=(m // tm, n // tn, 1), a_spec=BS((tm, k), lambda i, j, r: (i, 0)),
               b_spec=BS((tn, k), lambda i, j, r: (j, 0)), o_spec=BS((tm, tn), lambda i, j, r: (i, j)),
               out_shape=(m, n), dn=NT)


def _mm_tn_rows(name, a, b, tk):
    m, k = a.shape
    n = b.shape[1]
    tm = _tile(m, TK_RED)
    return _mm(name, a, b, grid=(k // tk, m // tm), a_spec=BS((tm, tk), lambda j, r: (r, j)),
               b_spec=BS((tm, n), lambda j, r: (r, 0)), o_spec=BS((tk, n), lambda j, r: (j, 0)),
               out_shape=(k, n), dn=TN)


def _mm_nt(name, a, w, out_dtype=F32, tn=None):
    m, n = a.shape
    k = w.shape[0]
    tm = _tile(m, MM_TM)
    tn = n if tn is None else tn
    return _mm(name, a, w, grid=(m // tm, n // tn), a_spec=BS((tm, tn), lambda i, r: (i, r)),
               b_spec=BS((k, tn), lambda i, r: (0, r)), o_spec=BS((tm, k), lambda i, r: (i, 0)),
               out_shape=(m, k), dn=NT, out_dtype=out_dtype)


def _mm_tn(name, a, b, tn=None):
    m, k = a.shape
    n = b.shape[1]
    tm = _tile(m, TK_RED)
    tn = n if tn is None else tn
    return _mm(name, a, b, grid=(n // tn, m // tm), a_spec=BS((tm, k), lambda j, r: (r, 0)),
               b_spec=BS((tm, tn), lambda j, r: (r, j)), o_spec=BS((k, tn), lambda j, r: (0, j)),
               out_shape=(k, n), dn=TN)


def _bmm_nn(name, a, w, out_dtype=F32):
    m, k = a.shape
    g, _, n = w.shape
    tm = _tile(m, MM_TM)
    return _mm(name, a, w, grid=(g, m // tm, 1), a_spec=BS((tm, k), lambda q, i, r: (i, 0)),
               b_spec=BS((None, k, n), lambda q, i, r: (q, 0, 0)), o_spec=BS((None, tm, n), lambda q, i, r: (q, i, 0)),
               out_shape=(g, m, n), dn=NN, out_dtype=out_dtype)


def _bmm_tn(name, a, b):
    m, k = a.shape
    g, _, n = b.shape
    tm = _tile(m, TK_RED)
    return _mm(name, a, b, grid=(g, m // tm), a_spec=BS((tm, k), lambda q, r: (r, 0)),
               b_spec=BS((None, tm, n), lambda q, r: (q, r, 0)), o_spec=BS((None, k, n), lambda q, r: (q, 0, 0)),
               out_shape=(g, k, n), dn=TN)


def _block_sum(name, a, w, dn, out_cols):
    g, m, ac = a.shape
    tm = _tile(m, TM_SUM)

    def kern(a_ref, w_ref, o_ref):
        acc = None
        for q in range(g):
            p = lax.dot_general(a_ref[q].astype(BF16), w_ref[q].astype(BF16), (dn, ((), ())), preferred_element_type=F32)
            acc = p if acc is None else acc + p
        o_ref[...] = acc

    return _call(kern, name=name, grid=(m // tm,),
                 in_specs=[BS((g, tm, ac), lambda i: (0, i, 0)), BS(w.shape, lambda i: (0, 0, 0))],
                 out_specs=BS((tm, out_cols), lambda i: (i, 0)), out_shape=jax.ShapeDtypeStruct((m, out_cols), F32),
                 compiler_params=_params(1, VMEM_BIG))(a, w)


def _bmm_nt_sum(name, a, w):
    return _block_sum(name, a, w, NT, w.shape[1])


def _bmm_nn_sum(name, a, w):
    return _block_sum(name, a, w, NN, w.shape[2])


def _rstd(x):
    return lax.rsqrt(jnp.mean(x * x, axis=-1, keepdims=True) + EPS)


def _norm_fwd(x, g, after=None):
    rows = x.shape[0]
    tm = _tile(rows, 512)

    def kern(x_ref, g_ref, *rest):
        xv = x_ref[...]
        rest[-1][...] = ((xv * _rstd(xv)) * g_ref[...]).astype(BF16)

    extra = () if after is None else (after,)
    return _call(kern, name="norm_fwd", grid=(rows // tm,),
                          in_specs=[BS((tm, DM), lambda i: (i, 0)), BS((1, DM), lambda i: (0, 0))] + [ANY] * len(extra),
                          out_specs=BS((tm, DM), lambda i: (i, 0)),
                          out_shape=jax.ShapeDtypeStruct((rows, DM), BF16), compiler_params=_params(1))(x, g, *extra)


def _norm_res(x, y, g):
    rows = x.shape[0]
    tm = _tile(rows, 512)

    def kern(x_ref, y_ref, g_ref, o_ref):
        yv = y_ref[...]
        o_ref[...] = x_ref[...] + (yv * _rstd(yv)) * g_ref[...]

    row = BS((tm, DM), lambda i: (i, 0))
    return _call(kern, name="norm_res", grid=(rows // tm,),
                          in_specs=[row, row, BS((1, DM), lambda i: (0, 0))], out_specs=row,
                          out_shape=jax.ShapeDtypeStruct((rows, DM), F32), compiler_params=_params(1))(x, y, g)


def _norm_bwd(z, dout, g, resid, out_dtype, after=None):
    rows = z.shape[0]
    tm = _tile(rows, 512)
    has_res = resid is not None

    def kern(*refs):
        z_ref, d_ref, g_ref = refs[:3]
        r_ref = refs[3] if has_res else None
        dz_ref, dg_ref = refs[-2:]
        zv = z_ref[...]
        dv = d_ref[...].astype(F32)
        r = _rstd(zv)
        zh = zv * r
        dzh = dv * g_ref[...]
        dz = r * (dzh - zh * jnp.mean(dzh * zh, axis=-1, keepdims=True))
        if has_res:
            dz = dz + r_ref[...]
        dz_ref[...] = dz.astype(dz_ref.dtype)
        part = jnp.sum(dv * zh, axis=0, keepdims=True)

        @pl.when(pl.program_id(0) == 0)
        def _():
            dg_ref[...] = part

        @pl.when(pl.program_id(0) > 0)
        def _():
            dg_ref[...] += part

    row = BS((tm, DM), lambda i: (i, 0))
    vec = BS((1, DM), lambda i: (0, 0))
    ins = [row, row, vec] + ([row] if has_res else []) + ([ANY] if after is not None else [])
    args = (z, dout, g) + ((resid,) if has_res else ()) + ((after,) if after is not None else ())
    return _call(kern, name="norm_bwd_res" if has_res else "norm_bwd", grid=(rows // tm,), in_specs=ins,
                          out_specs=[row, vec],
                          out_shape=[jax.ShapeDtypeStruct((rows, DM), out_dtype), jax.ShapeDtypeStruct((1, DM), F32)],
                          compiler_params=_params(1))(*args)


def _ffn_up(h, wgu4):
    s = h.shape[0]
    tm = _tile(s, MM_TM)

    def kern(h_ref, w_ref, gu_ref, a_ref):
        hv = h_ref[...]
        gate = lax.dot_general(hv, w_ref[0], (NT, ((), ())), preferred_element_type=F32)
        up = lax.dot_general(hv, w_ref[1], (NT, ((), ())), preferred_element_type=F32)
        gu_ref[0] = gate.astype(BF16)
        gu_ref[1] = up.astype(BF16)
        a_ref[...] = (gate * jax.nn.sigmoid(gate) * up).astype(BF16)

    return _call(
        kern, name="ffn_up", grid=(4, s // tm),
        in_specs=[BS((tm, DM), lambda j, i: (i, 0)), BS((2, None, FFB, DM), lambda j, i: (0, j, 0, 0))],
        out_specs=[BS((2, None, tm, FFB), lambda j, i: (0, j, i, 0)), BS((None, tm, FFB), lambda j, i: (j, i, 0))],
        out_shape=[jax.ShapeDtypeStruct((2, 4, s, FFB), BF16), jax.ShapeDtypeStruct((4, s, FFB), BF16)],
        compiler_params=_params(2, VMEM_BIG))(h, wgu4)


def _ffn_da(dy, wd4, gu):
    s = dy.shape[0]
    tm = _tile(s, MM_TM)

    def kern(dy_ref, w_ref, gu_ref, o_ref):
        da = lax.dot_general(dy_ref[...], w_ref[...], (NT, ((), ())), preferred_element_type=F32)
        gate = gu_ref[0].astype(F32)
        up = gu_ref[1].astype(F32)
        sg = jax.nn.sigmoid(gate)
        o_ref[0] = (da * up * (sg * (1.0 + gate * (1.0 - sg)))).astype(BF16)
        o_ref[1] = (da * (gate * sg)).astype(BF16)

    blk = BS((2, None, tm, FFB), lambda j, i: (0, j, i, 0))
    return _call(
        kern, name="ffn_da", grid=(4, s // tm),
        in_specs=[BS((tm, DM), lambda j, i: (i, 0)), BS((None, FFB, DM), lambda j, i: (j, 0, 0)), blk],
        out_specs=blk, out_shape=jax.ShapeDtypeStruct((2, 4, s, FFB), BF16), compiler_params=_params(2, VMEM_BIG))(dy, wd4, gu)


def _ffn_fwd(x, gpre, gpost, wgu, wd):
    h = _norm_fwd(x, gpre)
    gu, a = _ffn_up(h, wgu.reshape(2, 4, FFB, DM))
    y = _bmm_nn_sum("ffn_down", a, wd.reshape(4, FFB, DM))
    return _norm_res(x, y, gpost), (x, h, gu, a, y)


def _ffn_bwd(dxo, saved, gpre, gpost, wgu, wd, after=None):
    x, h, gu, a, y = saved
    s = x.shape[0]
    dy, dgpost = _norm_bwd(y, dxo, gpost, None, BF16, after)
    dgu = _ffn_da(dy, wd.reshape(4, FFB, DM), gu).reshape(8, s, FFB)
    dwd = _bmm_tn_a3("ffn_dwd", a, dy)
    dwgu = _bmm_tn_a3("ffn_dwgu", dgu, h)
    dh = _bmm_nn_sum("ffn_dh", dgu, wgu)
    dx, dgpre = _norm_bwd(x, dh, gpre, dxo, F32)
    return dx, dgpre, dgpost, dwgu, dwd.reshape(D_FF, DM)


def _bmm_tn_a3(name, a, b):
    g, m, k = a.shape
    n = b.shape[1]
    tm = _tile(m, TK_RED)
    return _mm(name, a, b, grid=(g, m // tm), a_spec=BS((None, tm, k), lambda q, r: (q, r, 0)),
               b_spec=BS((tm, n), lambda q, r: (r, 0)), o_spec=BS((None, k, n), lambda q, r: (q, 0, 0)),
               out_shape=(g, k, n), dn=TN)


XATTN_TM = 1024


def _softmax_rows(s):
    m = jnp.max(s, axis=-1, keepdims=True)
    p = jnp.exp(s - m)
    return p / jnp.sum(p, axis=-1, keepdims=True)


def _xattn_fwd_call(h, wq, kv):
    s = h.shape[0]
    mlen = kv.shape[1]
    tm = _tile(s, XATTN_TM)
    scale = MEM_HD ** -0.5

    def kern(h_ref, w_ref, k_ref, v_ref, q_ref, o_ref):
        q = jnp.dot(h_ref[...], w_ref[...], preferred_element_type=F32).astype(BF16)
        q_ref[...] = q
        sc = lax.dot_general(q, k_ref[...], (NT, ((), ())), preferred_element_type=F32) * scale
        p = _softmax_rows(sc)
        o_ref[...] = jnp.dot(p.astype(BF16), v_ref[...], preferred_element_type=F32).astype(BF16)

    blk = BS((tm, MEM_HD), lambda i, hd: (i, hd))
    return _call(
        kern, name="xattn_fwd", grid=(s // tm, MEM_H),
        in_specs=[BS((tm, DM), lambda i, hd: (i, 0)), BS((DM, MEM_HD), lambda i, hd: (0, hd)),
                  BS((None, mlen, MEM_HD), lambda i, hd: (hd, 0, 0)),
                  BS((None, mlen, MEM_HD), lambda i, hd: (MEM_H + hd, 0, 0))],
        out_specs=[blk, blk],
        out_shape=[jax.ShapeDtypeStruct((s, DM), BF16), jax.ShapeDtypeStruct((s, DM), BF16)],
        compiler_params=_params(2))(h, wq, kv, kv)


def _xattn_bwd_call(q, kv, do):
    s = q.shape[0]
    mlen = kv.shape[1]
    tm = _tile(s, XATTN_TM)
    scale = MEM_HD ** -0.5

    def kern(q_ref, k_ref, v_ref, do_ref, dq_ref, dkv_ref):
        qv, kvv, vv, dov = q_ref[...], k_ref[...], v_ref[...], do_ref[...]
        sc = lax.dot_general(qv, kvv, (NT, ((), ())), preferred_element_type=F32) * scale
        p = _softmax_rows(sc)
        dp = lax.dot_general(dov, vv, (NT, ((), ())), preferred_element_type=F32)
        ds = (p * (dp - jnp.sum(dp * p, axis=-1, keepdims=True)) * scale).astype(BF16)
        dq_ref[...] = jnp.dot(ds, kvv, preferred_element_type=F32).astype(BF16)
        dk = lax.dot_general(ds, qv, (TN, ((), ())), preferred_element_type=F32)
        dv = lax.dot_general(p.astype(BF16), dov, (TN, ((), ())), preferred_element_type=F32)

        @pl.when(pl.program_id(1) == 0)
        def _():
            dkv_ref[0] = dk
            dkv_ref[1] = dv

        @pl.when(pl.program_id(1) > 0)
        def _():
            dkv_ref[0] += dk
            dkv_ref[1] += dv

    blk = BS((tm, MEM_HD), lambda hd, i: (i, hd))
    return _call(
        kern, name="xattn_bwd", grid=(MEM_H, s // tm),
        in_specs=[blk, BS((None, mlen, MEM_HD), lambda hd, i: (hd, 0, 0)),
                  BS((None, mlen, MEM_HD), lambda hd, i: (MEM_H + hd, 0, 0)), blk],
        out_specs=[blk, BS((2, None, mlen, MEM_HD), lambda hd, i: (0, hd, 0, 0))],
        out_shape=[jax.ShapeDtypeStruct((s, DM), BF16), jax.ShapeDtypeStruct((2, MEM_H, mlen, MEM_HD), F32)],
        compiler_params=_params(2))(q, kv, kv, do)


def _cross_fwd(x, mem, gpre, gmem, gpost, wq, wkv, wo, after=None):
    h = _norm_fwd(x, gpre, after)
    mn = _norm_fwd(mem, gmem)
    kv = _bmm_nn("xattn_kv", mn, wkv, BF16)
    q, o = _xattn_fwd_call(h, wq, kv)
    y = _mm_nn("xattn_out", o, wo)
    return _norm_res(x, y, gpost), (x, h, mn, kv, q, o, y)


def _cross_bwd(dxo, saved, mem, gpre, gmem, gpost, wq, wkv, wo, after=None):
    x, h, mn, kv, q, o, y = saved
    mlen = mem.shape[0]
    dy, dgpost = _norm_bwd(y, dxo, gpost, None, BF16, after)
    do = _mm_nt("xattn_do", dy, wo, BF16)
    dwo = _mm_tn("xattn_dwo", o, dy)
    dq, dkv = _xattn_bwd_call(q, kv, do)
    dwq = _mm_tn("xattn_dwq", h, dq)
    dh = _mm_nt("xattn_dh", dq, wq)
    dkv8 = dkv.reshape(8, mlen, MEM_HD)
    dwkv = _bmm_tn("xattn_dwkv", mn, dkv8)
    dmn = _bmm_nt_sum("xattn_dmn", dkv8, wkv)
    _, dgmem = _norm_bwd(mem, dmn, gmem, None, BF16)
    dx, dgpre = _norm_bwd(x, dh, gpre, dxo, F32)
    return dx, dgpre, dgmem, dgpost, dwq, dwkv, dwo


def _log_sigmoid(z):
    return jnp.minimum(z, 0.0) - jnp.log1p(jnp.exp(-jnp.abs(z)))


def _lane_scan_steps():
    return (1, 2, 4, 8, 16, 32, 64)


def _fox_cum(frow, bfb):
    s = frow.shape[1]

    def kern(f_ref, b_ref, o_ref):
        lane = lax.broadcasted_iota(jnp.int32, (FOX_H, LANE), 1)
        carry = jnp.zeros((FOX_H, 1), F32)
        for c in range(s // LANE):
            sl = slice(c * LANE, (c + 1) * LANE)
            lf = _log_sigmoid(f_ref[:, sl] + b_ref[...])
            v = lf
            for d in _lane_scan_steps():
                v = v + jnp.where(lane >= d, pltpu.roll(v, d, 1), 0.0)
            o_ref[:, sl] = v + carry
            carry = carry + jnp.sum(lf, axis=1, keepdims=True)

    return _call(kern, name="fox_cum", out_shape=jax.ShapeDtypeStruct((FOX_H, s), F32),
                          compiler_params=pltpu.CompilerParams(vmem_limit_bytes=VMEM_LIMIT))(frow, bfb)


def _fox_dlogf(dcq, dck, frow, bfb):
    s = frow.shape[1]

    def kern(q_ref, d_ref, f_ref, b_ref, df_ref, db_ref):
        lane = lax.broadcasted_iota(jnp.int32, (FOX_H, LANE), 1)
        carry = jnp.zeros((FOX_H, 1), F32)
        dbf = jnp.zeros((FOX_H, 1), F32)
        for c in reversed(range(s // LANE)):
            sl = slice(c * LANE, (c + 1) * LANE)
            dc = q_ref[:, sl] - d_ref[:, sl]
            v = dc
            for d in _lane_scan_steps():
                v = v + jnp.where(lane < LANE - d, pltpu.roll(v, LANE - d, 1), 0.0)
            v = v + carry
            carry = carry + jnp.sum(dc, axis=1, keepdims=True)
            df = v * jax.nn.sigmoid(-(f_ref[:, sl] + b_ref[...]))
            df_ref[:, sl] = df
            dbf = dbf + jnp.sum(df, axis=1, keepdims=True)
        db_ref[...] = jnp.broadcast_to(dbf, (FOX_H, LANE))

    return _call(kern, name="fox_dlogf",
                          out_shape=[jax.ShapeDtypeStruct((FOX_H, s), F32), jax.ShapeDtypeStruct((FOX_H, LANE), F32)],
                          compiler_params=pltpu.CompilerParams(vmem_limit_bytes=VMEM_LIMIT))(dcq, dck, frow, bfb)


FOX_TQ = 512
Q_COL, K_COL, V_COL = 0, FOX_W // LANE, 2 * FOX_W // LANE
B_COL = 3 * FOX_W // LANE
C_COL = B_COL + SC_W // LANE
U_COL = C_COL + SC_W // LANE


def _bf16_terms(c):
    hi = c.astype(BF16).astype(F32)
    mid = (c - hi).astype(BF16).astype(F32)
    return hi, mid, (c - hi - mid).astype(BF16).astype(F32)


def _fox_operands(qv, kv, cq, ck, lane, hh, scale):
    sel = (lane < FOX_HD) if hh == 0 else (lane >= FOX_HD)
    b0 = FOX_HD if hh == 0 else 0
    qa = jnp.where(sel, qv * scale, 0.0)
    ka = jnp.where(sel, kv, 0.0)
    for n, (tq_, tk_) in enumerate(zip(_bf16_terms(cq), _bf16_terms(ck))):
        qa = jnp.where(lane == b0 + n, tq_, jnp.where(lane == b0 + 3 + n, 1.0, qa))
        ka = jnp.where(lane == b0 + n, 1.0, jnp.where(lane == b0 + 3 + n, -tk_, ka))
    return sel, qa.astype(BF16), ka.astype(BF16)


def _fox_logits(qa, ka, causal):
    sc = lax.dot_general(qa, ka, (NT, ((), ())), preferred_element_type=F32)
    return sc if causal is None else jnp.where(causal, sc, NEG)


def _fox_prep(proj, cumc):
    s = proj.shape[0]
    tp = _tile(s, 512)
    scale = FOX_HD ** -0.5

    def kern(q_ref, k_ref, c_ref, qa_ref, ka_ref):
        lane = lax.broadcasted_iota(jnp.int32, (tp, LANE), 1)
        for hh in range(2):
            _, qa_ref[hh], ka_ref[hh] = _fox_operands(q_ref[...], k_ref[...], c_ref[hh], c_ref[hh], lane, hh, scale)

    pair = BS((2, tp, LANE), lambda hp, i: (hp, i, 0))
    shp = jax.ShapeDtypeStruct((FOX_H, s, LANE), BF16)
    return _call(kern, name="fox_prep", grid=(4, s // tp),
                 in_specs=[BS((tp, LANE), lambda hp, i: (i, Q_COL + hp)), BS((tp, LANE), lambda hp, i: (i, K_COL + hp)), pair],
                 out_specs=[pair, pair], out_shape=[shp, shp], compiler_params=_params(2))(proj, proj, cumc)


def _fox_fwd_call(proj, qa, ka):
    s = proj.shape[0]
    tq = _tile(s, FOX_TQ)
    nq = s // tq

    def kern(qa_ref, ka_ref, v_ref, o_ref, lse_ref, m_s, l_s, acc_s):
        i = pl.program_id(1)
        j = pl.program_id(2)
        lane = lax.broadcasted_iota(jnp.int32, (tq, LANE), 1)

        @pl.when(j == 0)
        def _():
            m_s[...] = jnp.full(m_s.shape, NEG, F32)
            l_s[...] = jnp.zeros(l_s.shape, F32)
            acc_s[...] = jnp.zeros(acc_s.shape, F32)

        def step(diagonal):
            vb = v_ref[...].astype(BF16)
            causal = (lax.broadcasted_iota(jnp.int32, (tq, tq), 0) >= lax.broadcasted_iota(jnp.int32, (tq, tq), 1)
                      if diagonal else None)
            for hh in range(2):
                sc = _fox_logits(qa_ref[hh], ka_ref[hh], causal)
                m_prev = m_s[hh]
                m_new = jnp.maximum(m_prev, jnp.max(sc, axis=-1, keepdims=True))
                alpha = jnp.exp(m_prev - m_new)
                p = jnp.exp(sc - m_new)
                l_s[hh] = alpha * l_s[hh] + jnp.sum(p, axis=-1, keepdims=True)
                acc_s[hh] = alpha * acc_s[hh] + jnp.dot(p.astype(BF16), vb, preferred_element_type=F32)
                m_s[hh] = m_new

        @pl.when(j < i)
        def _():
            step(False)

        @pl.when(j == i)
        def _():
            step(True)
            o_ref[...] = jnp.where(lane < FOX_HD, acc_s[0] / l_s[0], acc_s[1] / l_s[1])
            for hh in range(2):
                lse_ref[hh] = jnp.broadcast_to(m_s[hh] + jnp.log(l_s[hh]), (tq, LANE))

    kvi = lambda hp, i, j: jnp.minimum(j, i)
    return _call(
        kern, name="fox_fwd", grid=(4, nq, nq),
        in_specs=[BS((2, tq, LANE), lambda hp, i, j: (hp, i, 0)),
                  BS((2, tq, LANE), lambda hp, i, j: (hp, kvi(hp, i, j), 0)),
                  BS((tq, LANE), lambda hp, i, j: (kvi(hp, i, j), V_COL + hp))],
        out_specs=[BS((tq, LANE), lambda hp, i, j: (i, hp)), BS((2, tq, LANE), lambda hp, i, j: (hp, i, 0))],
        out_shape=[jax.ShapeDtypeStruct((s, FOX_W), F32), jax.ShapeDtypeStruct((FOX_H, s, LANE), F32)],
        scratch_shapes=[pltpu.VMEM((2, tq, 1), F32), pltpu.VMEM((2, tq, 1), F32), pltpu.VMEM((2, tq, LANE), F32)],
        compiler_params=_params(3))(qa, ka, proj)


ROWSUM_M = 16


def _fox_bwd_call(proj, o, lse, dcat, qa, ka):
    s = proj.shape[0]
    tq = _tile(s, FOX_TQ)
    nq = s // tq
    reps = tq // LANE
    scale = FOX_HD ** -0.5

    def kern(qa_ref, ka_ref, v_ref, do_ref, o_ref, lse_ref, dq_ref, dk_ref, dv_ref, dck_ref, dcq_ref):
        j = pl.program_id(1)
        i = pl.program_id(2)
        lane = lax.broadcasted_iota(jnp.int32, (tq, LANE), 1)
        ones = jnp.ones((ROWSUM_M, tq), BF16)

        @pl.when((j == 0) & (i == 0))
        def _():
            dq_ref[...] = jnp.zeros(dq_ref.shape, F32)
            dcq_ref[...] = jnp.zeros(dcq_ref.shape, F32)

        @pl.when(i == j)
        def _():
            dk_ref[...] = jnp.zeros(dk_ref.shape, F32)
            dv_ref[...] = jnp.zeros(dv_ref.shape, F32)
            dck_ref[...] = jnp.zeros(dck_ref.shape, F32)

        def step(diagonal):
            dov = do_ref[...]
            ov = o_ref[...]
            vb = v_ref[...].astype(BF16)
            causal = (lax.broadcasted_iota(jnp.int32, (tq, tq), 0) >= lax.broadcasted_iota(jnp.int32, (tq, tq), 1)
                      if diagonal else None)
            dq_t = jnp.zeros((tq, LANE), F32)
            dk_t = jnp.zeros((tq, LANE), F32)
            dv_t = jnp.zeros((tq, LANE), F32)
            for hh in range(2):
                sel = (lane < FOX_HD) if hh == 0 else (lane >= FOX_HD)
                qa, ka = qa_ref[hh], ka_ref[hh]
                dom32 = jnp.where(sel, dov, 0.0)
                dom = dom32.astype(BF16)
                sc = _fox_logits(qa, ka, causal)
                p = jnp.exp(sc - jnp.tile(lse_ref[hh], (1, reps)))
                dp = lax.dot_general(dom, vb, (NT, ((), ())), preferred_element_type=F32)
                delta = jnp.sum(dom32 * ov, axis=-1, keepdims=True)
                ds = p * (dp - delta)
                dsb = ds.astype(BF16)
                dq_t = jnp.where(sel, jnp.dot(dsb, ka, preferred_element_type=F32) * scale, dq_t)
                dk_t = jnp.where(sel, lax.dot_general(dsb, qa, (TN, ((), ())), preferred_element_type=F32), dk_t)
                dv_t = dv_t + lax.dot_general(p.astype(BF16), dom, (TN, ((), ())), preferred_element_type=F32)
                dck_ref[hh] += jnp.sum(ds, axis=0, keepdims=True)
                ds_lo = (ds - dsb.astype(F32)).astype(BF16)
                dcq_ref[hh, i] += (lax.dot_general(ones, dsb, (NT, ((), ())), preferred_element_type=F32)
                                   + lax.dot_general(ones, ds_lo, (NT, ((), ())), preferred_element_type=F32))
            rows = pl.ds(pl.multiple_of(i * tq, tq), tq)
            dq_ref[rows, :] += dq_t
            dk_ref[...] += dk_t
            dv_ref[...] += dv_t

        @pl.when(i > j)
        def _():
            step(False)

        @pl.when(i == j)
        def _():
            step(True)

    qi = lambda hp, j, i: jnp.maximum(i, j)
    return _call(
        kern, name="fox_bwd", grid=(4, nq, nq),
        in_specs=[BS((2, tq, LANE), lambda hp, j, i: (hp, qi(hp, j, i), 0)),
                  BS((2, tq, LANE), lambda hp, j, i: (hp, j, 0)),
                  BS((tq, LANE), lambda hp, j, i: (j, V_COL + hp)),
                  BS((tq, LANE), lambda hp, j, i: (qi(hp, j, i), hp)),
                  BS((tq, LANE), lambda hp, j, i: (qi(hp, j, i), hp)),
                  BS((2, tq, LANE), lambda hp, j, i: (hp, qi(hp, j, i), 0))],
        out_specs=[BS((s, LANE), lambda hp, j, i: (0, hp)), BS((tq, LANE), lambda hp, j, i: (j, hp)),
                   BS((tq, LANE), lambda hp, j, i: (j, hp)), BS((2, 1, tq), lambda hp, j, i: (hp, 0, j)),
                   BS((2, nq, ROWSUM_M, tq), lambda hp, j, i: (hp, 0, 0, 0))],
        out_shape=[jax.ShapeDtypeStruct((s, FOX_W), F32), jax.ShapeDtypeStruct((s, FOX_W), F32),
                   jax.ShapeDtypeStruct((s, FOX_W), F32), jax.ShapeDtypeStruct((FOX_H, 1, s), F32),
                   jax.ShapeDtypeStruct((FOX_H, nq, ROWSUM_M, tq), F32)],
        compiler_params=_params(3))(qa, ka, proj, dcat, o, lse)


def _shift_down(v, d, row):
    return jnp.where(row >= d, pltpu.roll(v, d, 0), 0.0)


def _shift_up(v, d, row, n):
    return jnp.where(row < n - d, pltpu.roll(v, n - d, 0), 0.0)


def _sconv_fwd(proj, convw):
    s = proj.shape[0]

    def kern(b_ref, c_ref, u_ref, w_ref, y_ref):
        row = lax.broadcasted_iota(jnp.int32, (s, LANE), 0)
        z = c_ref[...] * u_ref[...]
        conv = w_ref[2:3, :] * z + w_ref[1:2, :] * _shift_down(z, 1, row) + w_ref[0:1, :] * _shift_down(z, 2, row)
        y_ref[...] = (b_ref[...] * conv).astype(BF16)

    col = lambda base: BS((s, LANE), lambda cb: (0, base + cb))
    return _call(kern, name="sconv_fwd", grid=(SC_W // LANE,),
                          in_specs=[col(B_COL), col(C_COL), col(U_COL), BS((SC_K, LANE), lambda cb: (0, cb))],
                          out_specs=BS((s, LANE), lambda cb: (0, cb)),
                          out_shape=jax.ShapeDtypeStruct((s, SC_W), BF16), compiler_params=_params(1))(proj, proj, proj, convw)


def _sconv_bwd(proj, convw, dcat):
    s = proj.shape[0]

    def kern(b_ref, c_ref, u_ref, w_ref, dy_ref, db_ref, dc_ref, du_ref, dw_ref):
        row = lax.broadcasted_iota(jnp.int32, (s, LANE), 0)
        cv, uv, dyv = c_ref[...], u_ref[...], dy_ref[...]
        z = cv * uv
        z1 = _shift_down(z, 1, row)
        z2 = _shift_down(z, 2, row)
        conv = w_ref[2:3, :] * z + w_ref[1:2, :] * z1 + w_ref[0:1, :] * z2
        db_ref[...] = dyv * conv
        dcv = dyv * b_ref[...]
        dz = w_ref[2:3, :] * dcv + w_ref[1:2, :] * _shift_up(dcv, 1, row, s) + w_ref[0:1, :] * _shift_up(dcv, 2, row, s)
        dc_ref[...] = dz * uv
        du_ref[...] = dz * cv
        dw_ref[0:1, :] = jnp.sum(dcv * z2, axis=0, keepdims=True)
        dw_ref[1:2, :] = jnp.sum(dcv * z1, axis=0, keepdims=True)
        dw_ref[2:3, :] = jnp.sum(dcv * z, axis=0, keepdims=True)

    col = lambda base: BS((s, LANE), lambda cb: (0, base + cb))
    out = BS((s, LANE), lambda cb: (0, cb))
    wspec = BS((SC_K, LANE), lambda cb: (0, cb))
    act = jax.ShapeDtypeStruct((s, SC_W), F32)
    return _call(kern, name="sconv_bwd", grid=(SC_W // LANE,),
                          in_specs=[col(B_COL), col(C_COL), col(U_COL), wspec, col(FOX_W // LANE)],
                          out_specs=[out, out, out, wspec],
                          out_shape=[act, act, act, jax.ShapeDtypeStruct((SC_K, SC_W), F32)],
                          compiler_params=_params(1))(proj, proj, proj, convw, dcat)


def _fox_layer_fwd(x, gpre, gpost, wall, bfb, convw, wout, after=None):
    s = x.shape[0]
    h = _norm_fwd(x, gpre, after)
    proj = _mm_nt_cols("fox_proj", h, wall, AB_PAD // 5)
    frow = proj[:, 3 * FOX_W + 3 * SC_W:3 * FOX_W + 3 * SC_W + FOX_H].T
    cumr = _fox_cum(frow, bfb)
    qa, ka = _fox_prep(proj, jnp.broadcast_to(cumr[:, :, None], (FOX_H, s, LANE)))
    o, lse = _fox_fwd_call(proj, qa, ka)
    yb = _sconv_fwd(proj, convw)
    cat = jnp.concatenate([o.astype(BF16), yb], axis=1)
    y = _mm_nn("fox_out", cat, wout)
    return _norm_res(x, y, gpost), (x, h, proj, frow, qa, ka, o, lse, cat, y)


def _fox_layer_bwd(dxo, saved, gpre, gpost, wall, bfb, convw, wout, after=None):
    x, h, proj, frow, qa, ka, o, lse, cat, y = saved
    s = x.shape[0]
    dy, dgpost = _norm_bwd(y, dxo, gpost, None, BF16, after)
    dcat = _mm_nt("fox_dcat", dy, wout)
    dwout = _mm_tn("fox_dwout", cat, dy)
    db, dc, du, dconvw = _sconv_bwd(proj, convw, dcat)
    dq, dk, dv, dck, dcq = _fox_bwd_call(proj, o, lse, dcat, qa, ka)
    dfrow, dbf = _fox_dlogf(dcq[:, :, 0, :].reshape(FOX_H, s), dck.reshape(FOX_H, s), frow, bfb)
    dfcol = jnp.pad(dfrow.T, ((0, 0), (0, LANE - FOX_H)))
    dproj = jnp.concatenate([dq, dk, dv, db, dc, du, dfcol], axis=1).astype(BF16)
    dwall = _mm_tn_rows("fox_dwall", dproj, h, AB_PAD // 5)
    dh = _mm_nn("fox_dh", dproj, wall, vmem=VMEM_BIG)
    dx, dgpre = _norm_bwd(x, dh, gpre, dxo, F32)
    return dx, dgpre, dgpost, dwall, dbf[:, 0], dconvw, dwout


def _ab_pack(wt):
    nf = 3 * FOX_W
    return jnp.concatenate([wt[:nf], wt[nf + FOX_H:], wt[nf:nf + FOX_H],
                            jnp.zeros((AB_PAD - AB_IN, wt.shape[1]), wt.dtype)], axis=0)


def _ab_unpack(wt):
    nf = 3 * FOX_W
    nbcu = 3 * SC_W
    return jnp.concatenate([wt[:nf], wt[nf + nbcu:nf + nbcu + FOX_H], wt[nf:nf + nbcu]], axis=0)


NCH = DM // LANE
CH_PER_BLK = LRU_BW // LANE


def _chunk_spec(s, lead=0):
    return BS((None, s, LANE), lambda ch: (lead + ch // CH_PER_BLK, 0, ch % CH_PER_BLK))


def _vec_chunk(rows):
    return BS((rows, LANE), lambda ch: (0, ch))


def _neg_expm1(x):
    series = -x * (1.0 + x * (1 / 2) * (1.0 + x * (1 / 3) * (1.0 + x * (1 / 4) * (1.0 + x * (1 / 5) * (
        1.0 + x * (1 / 6) * (1.0 + x * (1 / 7)))))))
    return jnp.where(x > -0.25, series, 1.0 - jnp.exp(x))


def _softplus(z):
    return jnp.maximum(z, 0.0) + jnp.log1p(jnp.exp(-jnp.abs(z)))


GELU_C = math.sqrt(2.0 / math.pi)
GELU_A = 0.044715


def _gelu(x):
    return 0.5 * x * (1.0 + jnp.tanh(GELU_C * (x + GELU_A * x * x * x)))


def _gelu_grad(x):
    t = jnp.tanh(GELU_C * (x + GELU_A * x * x * x))
    return 0.5 * (1.0 + t) + 0.5 * x * (1.0 - t * t) * GELU_C * (1.0 + 3.0 * GELU_A * x * x)


def _lru_conv_fwd(gu, convw, convb):
    s = gu.shape[1]

    def kern(x_ref, w_ref, b_ref, u_ref):
        row = lax.broadcasted_iota(jnp.int32, (s, LANE), 0)
        xv = x_ref[...]
        u_ref[...] = (b_ref[...] + w_ref[3:4, :] * xv + w_ref[2:3, :] * _shift_down(xv, 1, row)
                      + w_ref[1:2, :] * _shift_down(xv, 2, row) + w_ref[0:1, :] * _shift_down(xv, 3, row))

    return _call(kern, name="lru_conv_fwd", grid=(NCH,),
                          in_specs=[_chunk_spec(s, LRU_NB), _vec_chunk(RG_K), _vec_chunk(1)], out_specs=_chunk_spec(s),
                          out_shape=jax.ShapeDtypeStruct((LRU_NB, s, LRU_BW), F32), compiler_params=_params(1))(gu, convw, convb)


def _lru_conv_bwd(dud, dug, gu, convw):
    s = gu.shape[1]

    def kern(d1_ref, d2_ref, x_ref, w_ref, dx_ref, dw_ref, db_ref):
        row = lax.broadcasted_iota(jnp.int32, (s, LANE), 0)
        du = d1_ref[...] + d2_ref[...]
        xv = x_ref[...]
        dx_ref[...] = (w_ref[3:4, :] * du + w_ref[2:3, :] * _shift_up(du, 1, row, s) + w_ref[1:2, :] * _shift_up(du, 2, row, s)
                       + w_ref[0:1, :] * _shift_up(du, 3, row, s)).astype(BF16)
        dw_ref[3:4, :] = jnp.sum(du * xv, axis=0, keepdims=True)
        for k in range(1, RG_K):
            dw_ref[3 - k:4 - k, :] = jnp.sum(du * _shift_down(xv, k, row), axis=0, keepdims=True)
        db_ref[...] = jnp.sum(du, axis=0, keepdims=True)

    return _call(kern, name="lru_conv_bwd", grid=(NCH,),
                          in_specs=[_chunk_spec(s), _chunk_spec(s), _chunk_spec(s, LRU_NB), _vec_chunk(RG_K)],
                          out_specs=[_chunk_spec(s), _vec_chunk(RG_K), _vec_chunk(1)],
                          out_shape=[jax.ShapeDtypeStruct((LRU_NB, s, LRU_BW), BF16),
                                     jax.ShapeDtypeStruct((RG_K, DM), F32), jax.ShapeDtypeStruct((1, DM), F32)],
                          compiler_params=_params(1))(dud, dug, gu, convw)


def _lru_gates(z_ref, bai_ref, lam_ref, uv):
    r = jax.nn.sigmoid(z_ref[0] + bai_ref[0:1, :])
    ig = jax.nn.sigmoid(z_ref[1] + bai_ref[1:2, :])
    sp = _softplus(-lam_ref[...])
    la = -RG_C * r * sp
    a = jnp.exp(la)
    sq = jnp.sqrt(_neg_expm1(2.0 * la))
    return r, ig, sp, a, sq


def _scan_steps(n):
    d, out = 1, []
    while d < n:
        out.append(d)
        d *= 2
    return out


def _lru_scan_fwd(z, bai, lam, u, gu):
    s = u.shape[1]
    zspec = BS((2, None, s, LANE), lambda ch: (0, ch // CH_PER_BLK, 0, ch % CH_PER_BLK))

    def kern(z_ref, bai_ref, lam_ref, u_ref, g_ref, hs_ref, y_ref):
        row = lax.broadcasted_iota(jnp.int32, (s, LANE), 0)
        uv = u_ref[...]
        _, ig, _, a, sq = _lru_gates(z_ref, bai_ref, lam_ref, uv)
        b = sq * (ig * uv)
        for d in _scan_steps(s):
            a_sh = jnp.where(row >= d, pltpu.roll(a, d, 0), 1.0)
            b = a * _shift_down(b, d, row) + b
            a = a * a_sh
        hs_ref[...] = b
        y_ref[...] = (_gelu(g_ref[...]) * b).astype(BF16)

    return _call(kern, name="lru_scan_fwd", grid=(NCH,),
                          in_specs=[zspec, _vec_chunk(2), _vec_chunk(1), _chunk_spec(s), _chunk_spec(s)],
                          out_specs=[_chunk_spec(s), BS((s, LANE), lambda ch: (0, ch))],
                          out_shape=[jax.ShapeDtypeStruct((LRU_NB, s, LRU_BW), F32), jax.ShapeDtypeStruct((s, DM), BF16)],
                          compiler_params=_params(1, VMEM_BIG))(z, bai, lam, u, gu)


def _lru_scan_bwd(dyp, z, bai, lam, u, gu, hs):
    s = u.shape[1]
    zspec = BS((2, None, s, LANE), lambda ch: (0, ch // CH_PER_BLK, 0, ch % CH_PER_BLK))

    def kern(dy_ref, z_ref, bai_ref, lam_ref, u_ref, g_ref, hs_ref, dg_ref, dz_ref, du_ref, dbai_ref, dlam_ref):
        row = lax.broadcasted_iota(jnp.int32, (s, LANE), 0)
        uv, gv, hv, dyv = u_ref[...], g_ref[...], hs_ref[...], dy_ref[...]
        r, ig, sp, a, sq = _lru_gates(z_ref, bai_ref, lam_ref, uv)
        dg_ref[...] = (dyv * hv * _gelu_grad(gv)).astype(BF16)
        g = dyv * _gelu(gv)
        an = _shift_up(a, 1, row, s)
        for d in _scan_steps(s):
            an_sh = jnp.where(row < s - d, pltpu.roll(an, s - d, 0), 1.0)
            g = an * _shift_up(g, d, row, s) + g
            an = an * an_sh
        da = g * _shift_down(hv, 1, row)
        dsq = g * (ig * uv)
        di = g * sq * uv
        du_ref[...] = g * sq * ig
        dla = da * a - dsq * (a * a / sq)
        dzr = dla * (-RG_C * sp) * r * (1.0 - r)
        dzi = di * ig * (1.0 - ig)
        dz_ref[0] = dzr.astype(BF16)
        dz_ref[1] = dzi.astype(BF16)
        dbai_ref[0:1, :] = jnp.sum(dzr, axis=0, keepdims=True)
        dbai_ref[1:2, :] = jnp.sum(dzi, axis=0, keepdims=True)
        dlam_ref[...] = jnp.sum(dla * r, axis=0, keepdims=True) * (RG_C * jax.nn.sigmoid(-lam_ref[...]))

    return _call(
        kern, name="lru_scan_bwd", grid=(NCH,),
        in_specs=[BS((s, LANE), lambda ch: (0, ch)), zspec, _vec_chunk(2), _vec_chunk(1), _chunk_spec(s), _chunk_spec(s),
                  _chunk_spec(s)],
        out_specs=[_chunk_spec(s), zspec, _chunk_spec(s), _vec_chunk(2), _vec_chunk(1)],
        out_shape=[jax.ShapeDtypeStruct((LRU_NB, s, LRU_BW), BF16), jax.ShapeDtypeStruct((2, LRU_NB, s, LRU_BW), BF16),
                   jax.ShapeDtypeStruct((LRU_NB, s, LRU_BW), F32), jax.ShapeDtypeStruct((2, DM), F32),
                   jax.ShapeDtypeStruct((1, DM), F32)],
        compiler_params=_params(1, VMEM_BIG))(dyp, z, bai, lam, u, gu, hs)


def _lru_layer_fwd(x, gpre, gpost, win, convw, convb, wai, bai, lam, wout, after=None):
    s = x.shape[0]
    tm = _tile(s, 512)
    h = _norm_fwd(x, gpre, after)
    gu = _bmm_nn("lru_in", h, win)
    u = _lru_conv_fwd(gu, convw, convb)
    z = _mm("lru_gate", u, wai, grid=(2, LRU_NB, s // tm, 1),
            a_spec=BS((None, tm, LRU_BW), lambda k, n, i, r: (n, i, 0)),
            b_spec=BS((None, None, LRU_BW, LRU_BW), lambda k, n, i, r: (k, n, 0, 0)),
            o_spec=BS((None, None, tm, LRU_BW), lambda k, n, i, r: (k, n, i, 0)),
            out_shape=(2, LRU_NB, s, LRU_BW), dn=NN)
    hs, yp = _lru_scan_fwd(z, bai, lam, u, gu)
    y = _mm_nn("lru_out", yp, wout)
    return _norm_res(x, y, gpost), (x, h, gu, u, z, hs, yp, y)


def _lru_layer_bwd(dxo, saved, gpre, gpost, win, convw, convb, wai, bai, lam, wout, after=None):
    x, h, gu, u, z, hs, yp, y = saved
    s = x.shape[0]
    tm = _tile(s, 512)
    dy, dgpost = _norm_bwd(y, dxo, gpost, None, BF16, after)
    dyp = _mm_nt("lru_dyp", dy, wout)
    dwout = _mm_tn("lru_dwout", yp, dy)
    dgate, dz, dud, dbai, dlam = _lru_scan_bwd(dyp, z, bai, lam, u, gu, hs)
    dwai = _mm("lru_dwai", u, dz, grid=(2, LRU_NB, s // tm),
               a_spec=BS((None, tm, LRU_BW), lambda k, n, r: (n, r, 0)),
               b_spec=BS((None, None, tm, LRU_BW), lambda k, n, r: (k, n, r, 0)),
               o_spec=BS((None, None, LRU_BW, LRU_BW), lambda k, n, r: (k, n, 0, 0)),
               out_shape=(2, LRU_NB, LRU_BW, LRU_BW), dn=TN)
    dug = _mm("lru_dug", dz, wai, grid=(LRU_NB, s // tm, 2),
              a_spec=BS((None, None, tm, LRU_BW), lambda n, i, k: (k, n, i, 0)),
              b_spec=BS((None, None, LRU_BW, LRU_BW), lambda n, i, k: (k, n, 0, 0)),
              o_spec=BS((None, tm, LRU_BW), lambda n, i, k: (n, i, 0)),
              out_shape=(LRU_NB, s, LRU_BW), dn=NT)
    duraw, dconvw, dconvb = _lru_conv_bwd(dud, dug, gu, convw)
    dgu = jnp.concatenate([dgate, duraw], axis=0)
    dwin = _bmm_tn("lru_dwin", h, dgu)
    dh = _bmm_nt_sum("lru_dh", dgu, win)
    dx, dgpre = _norm_bwd(x, dh, gpre, dxo, F32)
    return dx, dgpre, dgpost, dwin, dconvw, dconvb, dwai, dbai, dlam, dwout


CHIP_FLIPS = ((1, 0), (0, 1), (1, 1))


def _place():
    return lax.axis_index("x"), lax.axis_index("y"), lax.axis_index("c")


def _flip(v, f):
    return 1 - v if f else v


def _comm_params():
    return pltpu.CompilerParams(vmem_limit_bytes=VMEM_LIMIT)


def _small_gather(v):
    def body(v_ref, o_ref, send_sems, recv_sems, local_sem):
        x, y, c = _place()
        mine = 4 * x + 2 * y + c
        local = pltpu.make_async_copy(v_ref, o_ref.at[mine], local_sem)
        local.start()
        sends = []
        for k in range(1, NDEV):
            fx, fy, fc = (k >> 2) & 1, (k >> 1) & 1, k & 1
            sends.append(pltpu.make_async_remote_copy(
                src_ref=v_ref, dst_ref=o_ref.at[mine], send_sem=send_sems.at[k - 1], recv_sem=recv_sems.at[k - 1],
                device_id=(_flip(x, fx), _flip(y, fy), _flip(c, fc)), device_id_type=MESH))
        for cp in sends:
            cp.start()
        for k in range(1, NDEV):
            fx, fy, fc = (k >> 2) & 1, (k >> 1) & 1, k & 1
            src = 4 * _flip(x, fx) + 2 * _flip(y, fy) + _flip(c, fc)
            pltpu.make_async_remote_copy(src_ref=v_ref, dst_ref=o_ref.at[src], send_sem=send_sems.at[k - 1],
                                         recv_sem=recv_sems.at[k - 1], device_id=(x, y, c), device_id_type=MESH).wait_recv()
        for cp in sends:
            cp.wait_send()
        local.wait()

    return pl.pallas_call(body, name="small_gather", in_specs=[ANY], out_specs=ANY,
                          out_shape=jax.ShapeDtypeStruct((NDEV,) + v.shape, v.dtype),
                          scratch_shapes=[pltpu.SemaphoreType.DMA((NDEV - 1,)), pltpu.SemaphoreType.DMA((NDEV - 1,)),
                                          pltpu.SemaphoreType.DMA],
                          compiler_params=_comm_params())(v)


REL_CHIPS = ((0, 0),) + CHIP_FLIPS


def _rs_d2d(g5s, after=None):
    n = len(g5s)
    extra = () if after is None else (after,)

    def body(*refs):
        ins, gots = refs[:n], refs[n + len(extra):2 * n + len(extra)]
        send_sems, recv_sems = refs[2 * n + len(extra):]
        x, y, c = _place()
        copies = []
        for t in range(n):
            for f, (fx, fy) in enumerate(REL_CHIPS):
                copies.append(pltpu.make_async_remote_copy(
                    src_ref=ins[t].at[_flip(x, fx), _flip(y, fy), 1 - c], dst_ref=gots[t].at[f],
                    send_sem=send_sems.at[4 * t + f], recv_sem=recv_sems.at[4 * t + f], device_id=(x, y, 1 - c),
                    device_id_type=MESH))
        for cp in copies:
            cp.start()
        for cp in copies:
            cp.wait()

    out = [jax.ShapeDtypeStruct((4,) + g.shape[3:], F32) for g in g5s]
    return pl.pallas_call(body, name="rs_d2d", in_specs=[ANY] * (n + len(extra)), out_specs=[ANY] * n, out_shape=out,
                          scratch_shapes=[pltpu.SemaphoreType.DMA((4 * n,)), pltpu.SemaphoreType.DMA((4 * n,))],
                          compiler_params=_comm_params())(*g5s, *extra)


HBM = pl.BlockSpec(memory_space=pltpu.HBM)
SEM = pl.BlockSpec(memory_space=pltpu.SEMAPHORE)
EFFECT = pltpu.SideEffectType.DATAFLOW_SIDE_EFFECTING


def _in_hbm(a):
    return pltpu.with_memory_space_constraint(a, pltpu.HBM)


def _rs_ici_copies(ins, lands, send_sems, recv_sems):
    x, y, c = _place()
    return [pltpu.make_async_remote_copy(
        src_ref=ins[t].at[f], dst_ref=lands[t].at[f], send_sem=send_sems.at[3 * t + f], recv_sem=recv_sems.at[3 * t + f],
        device_id=(_flip(x, fx), _flip(y, fy), c), device_id_type=MESH)
        for t in range(len(ins)) for f, (fx, fy) in enumerate(CHIP_FLIPS)]


def _rs_ici_start(parts, name):
    n = len(parts)

    def body(*refs):
        ins, lands = refs[:n], refs[n:2 * n]
        send_sems, recv_sems = refs[2 * n], refs[2 * n + 1]
        token = refs[-1]
        for cp in _rs_ici_copies(ins, lands, send_sems, recv_sems):
            cp.start()
        token[...] = jnp.zeros(token.shape, token.dtype)

    thru = [pltpu.HBM(p.shape, p.dtype) for p in parts]
    res = pl.pallas_call(
        body, name=name, in_specs=[HBM] * (2 * n),
        out_shape=(pltpu.SemaphoreType.DMA((3 * n,)), pltpu.SemaphoreType.DMA((3 * n,)), *thru, *thru,
                   jax.ShapeDtypeStruct((8, LANE), F32)),
        out_specs=(SEM, SEM, *([HBM] * (2 * n)), pl.BlockSpec(memory_space=pltpu.VMEM)),
        input_output_aliases={i: 2 + i for i in range(2 * n)},
        compiler_params=pltpu.CompilerParams(has_side_effects=EFFECT),
    )(*[_in_hbm(p) for p in parts], *[_in_hbm(lax.empty(p.shape, p.dtype)) for p in parts])
    return res[:-1], res[-1]


def _rs_ici_wait(state, after, name):
    n = (len(state) - 2) // 2

    def body(*refs):
        send_sems, recv_sems = refs[0], refs[1]
        ins, lands = refs[2:2 + n], refs[2 + n:2 + 2 * n]
        for cp in _rs_ici_copies(ins, lands, send_sems, recv_sems):
            cp.wait_send()
            cp.wait_recv()

    thru = [pltpu.HBM(s.shape, s.dtype) for s in state[2:]]
    res = pl.pallas_call(
        body, name=name, in_specs=[SEM, SEM] + [HBM] * (2 * n) + [ANY], out_shape=tuple(thru),
        out_specs=tuple([HBM] * (2 * n)), input_output_aliases={2 + i: i for i in range(2 * n)},
        compiler_params=pltpu.CompilerParams(has_side_effects=EFFECT),
    )(*state, after)
    return list(res[n:])


def _ag_copies(shards, lands, send_sems, recv_sems):
    x, y, c = _place()
    mine = 4 * x + 2 * y + c
    peers = [(x, y, 1 - c)] + [(_flip(x, fx), _flip(y, fy), c) for fx, fy in CHIP_FLIPS]
    return [pltpu.make_async_remote_copy(
        src_ref=shards[t], dst_ref=lands[t].at[mine], send_sem=send_sems.at[4 * t + k], recv_sem=recv_sems.at[4 * t + k],
        device_id=peer, device_id_type=MESH) for t in range(len(shards)) for k, peer in enumerate(peers)]


def _ag_start(shards, after, name):
    n = len(shards)

    def body(*refs):
        ins, lands = refs[:n], refs[n:2 * n]
        send_sems, recv_sems = refs[2 * n + 1], refs[2 * n + 2]
        token = refs[-1]
        for cp in _ag_copies(ins, lands, send_sems, recv_sems):
            cp.start()
        token[...] = jnp.zeros(token.shape, token.dtype)

    thru = [pltpu.HBM(s.shape, s.dtype) for s in shards]
    land = [pltpu.HBM((NDEV,) + s.shape, s.dtype) for s in shards]
    res = pl.pallas_call(
        body, name=name, in_specs=[HBM] * (2 * n) + [ANY],
        out_shape=(pltpu.SemaphoreType.DMA((4 * n,)), pltpu.SemaphoreType.DMA((4 * n,)), *thru, *land,
                   jax.ShapeDtypeStruct((8, LANE), F32)),
        out_specs=(SEM, SEM, *([HBM] * (2 * n)), pl.BlockSpec(memory_space=pltpu.VMEM)),
        input_output_aliases={i: 2 + i for i in range(2 * n)},
        compiler_params=pltpu.CompilerParams(has_side_effects=EFFECT),
    )(*[_in_hbm(s) for s in shards], *[_in_hbm(lax.empty((NDEV,) + s.shape, s.dtype)) for s in shards], after)
    return res[:-1], res[-1]


def _ag_wait(state, after, name):
    n = (len(state) - 2) // 2

    def body(*refs):
        send_sems, recv_sems = refs[0], refs[1]
        ins, lands = refs[2:2 + n], refs[2 + n:2 + 2 * n]
        for cp in _ag_copies(ins, lands, send_sems, recv_sems):
            cp.wait_send()
            cp.wait_recv()

    thru = [pltpu.HBM(s.shape, s.dtype) for s in state[2:]]
    res = pl.pallas_call(
        body, name=name, in_specs=[SEM, SEM] + [HBM] * (2 * n) + [ANY], out_shape=tuple(thru),
        out_specs=tuple([HBM] * (2 * n)), input_output_aliases={2 + i: i for i in range(2 * n)},
        compiler_params=pltpu.CompilerParams(has_side_effects=EFFECT),
    )(*state, after)
    return list(res[:n]), list(res[n:])


def _ag_finish(shards, lands):
    n = len(shards)

    def body(*refs):
        ins, outs, stage = refs[:n], refs[2 * n:3 * n], refs[3 * n:4 * n]
        send_sems, recv_sems, local_sems = refs[4 * n:]
        x, y, c = _place()
        chips = [(_flip(x, fx), _flip(y, fy)) for fx, fy in CHIP_FLIPS]

        def passing(t, j, core, to):
            blk = outs[t].at[4 * chips[j][0] + 2 * chips[j][1] + core]
            return pltpu.make_async_remote_copy(src_ref=blk, dst_ref=blk, send_sem=send_sems.at[3 * t + j],
                                                recv_sem=recv_sems.at[3 * t + j], device_id=to, device_id_type=MESH)

        sends = [passing(t, j, c, (x, y, 1 - c)) for t in range(n) for j in range(3)]
        for cp in sends:
            cp.start()
        load = [pltpu.make_async_copy(ins[t], stage[t], local_sems.at[t]) for t in range(n)]
        mine = [pltpu.make_async_copy(stage[t], outs[t].at[4 * x + 2 * y + c], local_sems.at[t]) for t in range(n)]
        for cp in load:
            cp.start()
        for t in range(n):
            load[t].wait()
            mine[t].start()
        for t in range(n):
            for j in range(3):
                passing(t, j, 1 - c, (x, y, c)).wait_recv()
        for cp in sends:
            cp.wait_send()
        for cp in mine:
            cp.wait()

    return pl.pallas_call(
        body, name="ag_finish", in_specs=[ANY] * (2 * n), out_specs=[ANY] * n,
        out_shape=[jax.ShapeDtypeStruct(l.shape, l.dtype) for l in lands],
        input_output_aliases={n + i: i for i in range(n)},
        scratch_shapes=[pltpu.VMEM(s.shape, s.dtype) for s in shards]
        + [pltpu.SemaphoreType.DMA((3 * n,)), pltpu.SemaphoreType.DMA((3 * n,)), pltpu.SemaphoreType.DMA((n,))],
        compiler_params=_comm_params())(*shards, *lands)


def _row_tile(rows, largest=256):
    for t in (1024, 512, 256, 128, 64, 32, 16, 8):
        if t > largest:
            continue
        if rows % t == 0:
            return t
    return rows


def _rs_chip_sum(pos, g5, got):
    a, b = g5.shape[3:]
    ta = _row_tile(a, 1024)

    def kern(pos_ref, o_ref, g_ref, p_ref):
        p_ref[...] = (o_ref[...] + g_ref[...]).astype(BF16)

    def mine(f, i, pos_ref):
        return (pos_ref[0] ^ ((f + 1) & 1), pos_ref[1] ^ ((f + 1) >> 1), pos_ref[2], i, 0)

    spec = pltpu.PrefetchScalarGridSpec(
        num_scalar_prefetch=1, grid=(3, a // ta),
        in_specs=[BS((None, None, None, ta, b), mine), BS((None, ta, b), lambda f, i, pos_ref: (f + 1, i, 0))],
        out_specs=BS((None, ta, b), lambda f, i, pos_ref: (f, i, 0)))
    return _call(kern, name="rs_chip_sum", grid_spec=spec, out_shape=jax.ShapeDtypeStruct((3, a, b), BF16),
                          compiler_params=_params(2))(pos, g5, got)


def _rs_final_sum(pos, g5, got, recv):
    a, b = g5.shape[3:]
    ta = _row_tile(a, 1024)

    def kern(pos_ref, o_ref, g_ref, r_ref, s_ref):
        acc = o_ref[...] + g_ref[...]
        for f in range(3):
            acc = acc + r_ref[f].astype(F32)
        s_ref[...] = acc

    spec = pltpu.PrefetchScalarGridSpec(
        num_scalar_prefetch=1, grid=(a // ta,),
        in_specs=[BS((None, None, None, ta, b), lambda i, pos_ref: (pos_ref[0], pos_ref[1], pos_ref[2], i, 0)),
                  BS((None, ta, b), lambda i, pos_ref: (0, i, 0)), BS((3, ta, b), lambda i, pos_ref: (0, i, 0))],
        out_specs=BS((ta, b), lambda i, pos_ref: (i, 0)))
    return _call(kern, name="rs_final_sum", grid_spec=spec, out_shape=jax.ShapeDtypeStruct((a, b), F32),
                          compiler_params=_params(1))(pos, g5, got, recv)


def _rs_d2d_copies(ins, lands, send_sems, recv_sems):
    x, y, c = _place()
    return [pltpu.make_async_remote_copy(
        src_ref=ins[t].at[_flip(x, fx), _flip(y, fy), 1 - c], dst_ref=lands[t].at[f], send_sem=send_sems.at[4 * t + f],
        recv_sem=recv_sems.at[4 * t + f], device_id=(x, y, 1 - c), device_id_type=MESH)
        for t in range(len(ins)) for f, (fx, fy) in enumerate(REL_CHIPS)]


def _rs_d2d_start(g5s, name):
    n = len(g5s)

    def body(*refs):
        ins, lands = refs[:n], refs[n:2 * n]
        for cp in _rs_d2d_copies(ins, lands, refs[2 * n], refs[2 * n + 1]):
            cp.start()
        refs[-1][...] = jnp.zeros(refs[-1].shape, F32)

    thru = [pltpu.HBM(g.shape, g.dtype) for g in g5s]
    land = [pltpu.HBM((4,) + g.shape[3:], F32) for g in g5s]
    res = pl.pallas_call(
        body, name=name, in_specs=[HBM] * (2 * n),
        out_shape=(pltpu.SemaphoreType.DMA((4 * n,)), pltpu.SemaphoreType.DMA((4 * n,)), *thru, *land,
                   jax.ShapeDtypeStruct((8, LANE), F32)),
        out_specs=(SEM, SEM, *([HBM] * (2 * n)), pl.BlockSpec(memory_space=pltpu.VMEM)),
        input_output_aliases={i: 2 + i for i in range(2 * n)},
        compiler_params=pltpu.CompilerParams(has_side_effects=EFFECT),
    )(*[_in_hbm(g) for g in g5s], *[_in_hbm(lax.empty((4,) + g.shape[3:], F32)) for g in g5s])
    return res[:-1], res[-1]


def _rs_d2d_wait(state, after, name):
    n = (len(state) - 2) // 2

    def body(*refs):
        ins, lands = refs[2:2 + n], refs[2 + n:2 + 2 * n]
        for cp in _rs_d2d_copies(ins, lands, refs[0], refs[1]):
            cp.wait_send()
            cp.wait_recv()

    thru = [pltpu.HBM(s.shape, s.dtype) for s in state[2:]]
    res = pl.pallas_call(
        body, name=name, in_specs=[SEM, SEM] + [HBM] * (2 * n) + [ANY], out_shape=tuple(thru),
        out_specs=tuple([HBM] * (2 * n)), input_output_aliases={2 + i: i for i in range(2 * n)},
        compiler_params=pltpu.CompilerParams(has_side_effects=EFFECT),
    )(*state, after)
    return list(res[:n]), list(res[n:])


def _as_g5(grads):
    return [g.reshape((2, 2, 2) + g.shape[1:]) for g in grads]


def _rs_mid(g5s, gots, pos, tag):
    parts = [_rs_chip_sum(pos, g, got) for g, got in zip(g5s, gots)]
    state, token = _rs_ici_start(parts, "rs_ici_start_" + tag)
    return (g5s, gots, state, tag), token


def _rs_begin(grads, pos, tag, after=None):
    g5s = _as_g5(grads)
    return _rs_mid(g5s, _rs_d2d(g5s, after), pos, tag)


def _rs_end(pending, after, pos):
    g5s, gots, state, tag = pending
    recvs = _rs_ici_wait(state, after, "rs_ici_wait_" + tag)
    return [_rs_final_sum(pos, g, got, r) for g, got, r in zip(g5s, gots, recvs)]


def _sum_devices(v):
    _, r, _ = v.shape

    def kern(v_ref, o_ref):
        acc = v_ref[0]
        for d in range(1, NDEV):
            acc = acc + v_ref[d]
        o_ref[...] = acc

    return _call(kern, name="sum_devices", out_shape=jax.ShapeDtypeStruct((r, LANE), F32),
                          compiler_params=_comm_params())(v)


def _loss_head(xf, target):
    s = xf.shape[0]
    tm = _tile(s, 512)

    def kern(x_ref, t_ref, dx_ref, l_ref):
        err = x_ref[...] - t_ref[...]
        dx_ref[...] = err * (1.0 / DM)
        part = jnp.broadcast_to(0.5 * jnp.sum(jnp.mean(err * err, axis=-1, keepdims=True), axis=0, keepdims=True), (8, LANE))

        @pl.when(pl.program_id(0) == 0)
        def _():
            l_ref[...] = part

        @pl.when(pl.program_id(0) > 0)
        def _():
            l_ref[...] += part

    row = BS((tm, DM), lambda i: (i, 0))
    return _call(kern, name="loss_head", grid=(s // tm,), in_specs=[row, row],
                          out_specs=[row, BS((8, LANE), lambda i: (0, 0))],
                          out_shape=[jax.ShapeDtypeStruct((s, DM), F32), jax.ShapeDtypeStruct((8, LANE), F32)],
                          compiler_params=_params(1))(xf, target)


def _adamw(w, g, m, v, after=None):
    rows, cols = w.shape
    tr = _row_tile(rows)
    extra = () if after is None else (after,)

    def kern(w_ref, g_ref, m_ref, v_ref, *rest):
        d_ref, nm_ref, nv_ref = rest[-3:]
        gv = g_ref[...]
        nm = ADAM_B1 * m_ref[...] + (1.0 - ADAM_B1) * gv
        nv = ADAM_B2 * v_ref[...] + (1.0 - ADAM_B2) * (gv * gv)
        m_hat = nm / (1.0 - ADAM_B1 ** ADAM_STEP)
        v_hat = nv / (1.0 - ADAM_B2 ** ADAM_STEP)
        d_ref[...] = -ADAM_LR * (m_hat / (jnp.sqrt(v_hat) + ADAM_EPS) + ADAM_WD * w_ref[...])
        nm_ref[...] = nm
        nv_ref[...] = nv

    blk = BS((tr, cols), lambda i: (i, 0))
    shp = jax.ShapeDtypeStruct((rows, cols), F32)
    return _call(kern, name="adamw", grid=(rows // tr,), in_specs=[blk] * 4 + [ANY] * len(extra),
                          out_specs=[blk] * 3, out_shape=[shp] * 3, compiler_params=_params(1))(w, g, m, v, *extra)


def _adamw_nd(w, g, m, v, after=None):
    shape = w.shape
    two = (math.prod(shape[:-1]), shape[-1])
    return tuple(o.reshape(shape)
                 for o in _adamw(w.reshape(two), g.reshape(two), m.reshape(two), v.reshape(two), after))


def _pack_small(parts):
    flat = jnp.concatenate([p.reshape(-1) for p in parts])
    pad = (-flat.shape[0]) % (8 * LANE)
    return jnp.pad(flat, (0, pad)).reshape(-1, LANE)


def _unpack_small(packed, shapes, lead=()):
    flat = packed.reshape(lead + (-1,))
    out, off = [], 0
    for shp in shapes:
        n = math.prod(shp)
        out.append(flat[..., off:off + n].reshape(lead + tuple(shp)))
        off += n
    return out


WEIGHT_NAMES = ('g_mix_pre', 'g_mix_post', 'g_cross_pre', 'g_mem', 'g_cross_post', 'g_ffn_pre', 'g_ffn_post', 'w_xq',
                'w_xkv', 'w_xo', 'w_ffn_gu', 'w_ffn_down', 'ab_w_in', 'ab_b_f', 'ab_conv_w', 'ab_w_out', 'c_w_in',
                'c_conv_w', 'c_conv_b', 'c_w_a', 'c_b_a', 'c_w_i', 'c_b_i', 'c_lam', 'c_w_out')
BIG = ('w_xq', 'w_xkv', 'w_xo', 'w_ffn_gu', 'w_ffn_down', 'ab_w_in', 'ab_w_out', 'c_w_in', 'c_w_a', 'c_w_i', 'c_w_out')
SMALL_SHARDED = ('ab_conv_w', 'c_conv_w', 'c_conv_b', 'c_b_a', 'c_b_i', 'c_lam')
REPLICATED = ('g_mix_pre', 'g_mix_post', 'g_cross_pre', 'g_mem', 'g_cross_post', 'g_ffn_pre', 'g_ffn_post', 'ab_b_f')


def _small_full(name, gathered):
    nd = gathered.ndim
    return jnp.moveaxis(gathered, 0, nd - 2).reshape(gathered.shape[1:-1] + (NDEV * gathered.shape[-1],))


def _small_shard(full, dev):
    c = full.shape[-1] // NDEV
    return lax.dynamic_slice_in_dim(full, dev * c, c, axis=full.ndim - 1)


def kernel(x, mem, g_mix_pre, g_mix_post, g_cross_pre, g_mem, g_cross_post, g_ffn_pre, g_ffn_post, w_xq, w_xkv, w_xo, w_ffn_gu, w_ffn_down, ab_w_in, ab_b_f, ab_conv_w, ab_w_out, c_w_in, c_conv_w, c_conv_b, c_w_a, c_b_a, c_w_i, c_b_i, c_lam, c_w_out, loss_target, m_g_mix_pre, m_g_mix_post, m_g_cross_pre, m_g_mem, m_g_cross_post, m_g_ffn_pre, m_g_ffn_post, m_w_xq, m_w_xkv, m_w_xo, m_w_ffn_gu, m_w_ffn_down, m_ab_w_in, m_ab_b_f, m_ab_conv_w, m_ab_w_out, m_c_w_in, m_c_conv_w, m_c_conv_b, m_c_w_a, m_c_b_a, m_c_w_i, m_c_b_i, m_c_lam, m_c_w_out, v_g_mix_pre, v_g_mix_post, v_g_cross_pre, v_g_mem, v_g_cross_post, v_g_ffn_pre, v_g_ffn_post, v_w_xq, v_w_xkv, v_w_xo, v_w_ffn_gu, v_w_ffn_down, v_ab_w_in, v_ab_b_f, v_ab_conv_w, v_ab_w_out, v_c_w_in, v_c_conv_w, v_c_conv_b, v_c_w_a, v_c_b_a, v_c_w_i, v_c_b_i, v_c_lam, v_c_w_out):
    args = locals()
    w = {n: args[n] for n in WEIGHT_NAMES}
    mom = {n: args["m_" + n] for n in WEIGHT_NAMES}
    var = {n: args["v_" + n] for n in WEIGHT_NAMES}
    for t in (w, mom, var):
        t['w_ffn_gu'] = t['w_ffn_gu'].transpose(0, 2, 1)
    ab_t = [t['ab_w_in'].transpose(2, 0, 1) for t in (w, mom, var)]
    pos = jnp.stack([lax.axis_index("x"), lax.axis_index("y"), lax.axis_index("c")]).astype(jnp.int32)
    dev = 4 * pos[0] + 2 * pos[1] + pos[2]
    xs, mems, target = x[0], mem[0], loss_target[0]
    n_even, n_odd = (DEPTH + 1) // 2, DEPTH // 2

    small_shapes = [w[n].shape for n in SMALL_SHARDED]
    small_w_all = _small_gather(_pack_small([w[n] for n in SMALL_SHARDED]))
    gathered_small = _unpack_small(small_w_all, small_shapes, (NDEV,))
    small = {n: _small_full(n, g) for n, g in zip(SMALL_SHARDED, gathered_small)}
    ab_bfb = jnp.broadcast_to(ab_b_f[:, :, None], (n_even, FOX_H, LANE))
    c_bai = jnp.stack([small['c_b_a'].reshape(n_odd, DM), small['c_b_i'].reshape(n_odd, DM)], axis=1)
    row = lambda a, l: a[l][None]

    REST = ('w_xq', 'w_xkv', 'w_xo', 'w_ffn_gu', 'w_ffn_down')

    def mixer_names(l):
        return ('ab_w_in', 'ab_w_out') if l % 2 == 0 else ('c_w_in', 'c_w_a', 'c_w_i', 'c_w_out')

    def shards_of(l, names):
        out = []
        for n in names:
            if n == 'ab_w_in':
                s = ab_t[0][:, l // 2].astype(BF16)
            else:
                s = w[n][l if w[n].shape[0] == DEPTH else l // 2].astype(BF16)
            out.append(s.reshape(-1, s.shape[-1]))
        return out

    def mixer_weights(l, full):
        if l % 2 == 0:
            e = l // 2
            return (row(g_mix_pre, l), row(g_mix_post, l), _ab_pack(full['ab_w_in'].reshape(AB_IN, DM)), ab_bfb[e],
                    small['ab_conv_w'][e], full['ab_w_out'].reshape(DM, DM))
        o = l // 2
        gate_w = lambda g: g.reshape(NDEV, LRU_NB, LRU_BW // NDEV, LRU_BW).transpose(1, 0, 2, 3).reshape(
            LRU_NB, LRU_BW, LRU_BW)
        return (row(g_mix_pre, l), row(g_mix_post, l), full['c_w_in'], small['c_conv_w'][o], row(small['c_conv_b'], o),
                jnp.stack([gate_w(full['c_w_a']), gate_w(full['c_w_i'])]), c_bai[o], row(small['c_lam'], o),
                full['c_w_out'].reshape(DM, DM))

    def rest_weights(l, full):
        cross = (row(g_cross_pre, l), row(g_mem, l), row(g_cross_post, l), full['w_xq'].reshape(DM, DM), full['w_xkv'],
                 full['w_xo'].reshape(DM, DM))
        ffn = (row(g_ffn_pre, l), row(g_ffn_post, l), full['w_ffn_gu'], full['w_ffn_down'].reshape(D_FF, DM))
        return cross, ffn

    def gathered(state, names, after, tag):
        shards, lands = _ag_wait(state, after, "ag_wait_" + tag)
        full = _ag_finish(shards, lands)
        return dict(zip(names, full)), full[0]

    saved, weights = [], []
    h = xs
    names_of = lambda l: mixer_names(l) + REST
    states = {}
    st_m, _ = _ag_start(shards_of(0, mixer_names(0)), small_w_all, "ag_start_0m")
    st_r, _ = _ag_start(shards_of(0, REST), st_m[2], "ag_start_0r")
    states[1], token = _ag_start(shards_of(1, names_of(1)), st_r[2], "ag_start_1")
    full_m, _ = gathered(st_m, mixer_names(0), xs, "0m")
    for l in range(DEPTH):
        if l > 0:
            full, done = gathered(states[l], names_of(l), h, str(l))
            full_m = full_r = full
            token = None
            if l + 2 < DEPTH:
                states[l + 2], token = _ag_start(shards_of(l + 2, names_of(l + 2)), done, "ag_start_%d" % (l + 2))
        mixer = mixer_weights(l, full_m)
        h, s_mix = (_fox_layer_fwd if l % 2 == 0 else _lru_layer_fwd)(h, *mixer, after=token)
        token = None
        if l == 0:
            full_r, done = gathered(st_r, REST, h, "0r")
            states[2], token = _ag_start(shards_of(2, names_of(2)), done, "ag_start_2")
        cross, ffn = rest_weights(l, full_r)
        h, s_cross = _cross_fwd(h, mems, *cross, after=token)
        h, s_ffn = _ffn_fwd(h, *ffn)
        saved.append((s_mix, s_cross, s_ffn))
        weights.append((mixer, cross, ffn))
    mixer_args = lambda l: weights[l][0]
    cross_args = lambda l: weights[l][1]
    ffn_args = lambda l: weights[l][2]
    dx, loss_rep = _loss_head(h, target)
    loss = lax.psum(loss_rep[0, 0], ("x", "y", "c"))

    grads = {n: [None] * w[n].shape[0] for n in BIG}
    partial = {n: [None] * w[n].shape[0] for n in REPLICATED + SMALL_SHARDED}
    def finish(pending, after):
        state, names, where = pending
        for n, g in zip(names, _rs_end(state, after, pos)):
            grads[n][where[n]] = g

    def unit(layer, names):
        return [layer[n][1] for n in names], names, {n: layer[n][0] for n in names}

    d2d = ici = None
    token = None
    for l in reversed(range(DEPTH)):
        s_mix, s_cross, s_ffn = saved[l]
        dx, partial['g_ffn_pre'][l], partial['g_ffn_post'][l], dwgu, dwd = _ffn_bwd(dx, s_ffn, *ffn_args(l), after=token)
        token = None
        if d2d is not None:
            g5s, gots = _rs_d2d_wait(d2d[0], dx, "rs_d2d_wait_%d" % (l + 1))
            state, token = _rs_mid(g5s, gots, pos, str(l + 1))
            ici, d2d = (state,) + d2d[1:], None
        (dx, partial['g_cross_pre'][l], partial['g_mem'][l], partial['g_cross_post'][l], dwq, dwkv, dwo) = _cross_bwd(
            dx, s_cross, mems, *cross_args(l), after=token)
        token = None
        layer = {'w_xq': (l, dwq.reshape(NDEV, DM // NDEV, DM)), 'w_xkv': (l, dwkv), 'w_xo': (l, dwo.reshape(NDEV, DM // NDEV, DM)),
                 'w_ffn_gu': (l, dwgu), 'w_ffn_down': (l, dwd.reshape(NDEV, D_FF // NDEV, DM))}
        if l == 0:
            gs, names, where = unit(layer, REST)
            state, token = _rs_begin(gs, pos, "0r")
            ici_rest = (state, names, where)
        if l % 2 == 0:
            e = l // 2
            (dx, partial['g_mix_pre'][l], partial['g_mix_post'][l], dwall, partial['ab_b_f'][e], partial['ab_conv_w'][e],
             dwout) = _fox_layer_bwd(dx, s_mix, *mixer_args(l), after=token)
            layer['ab_w_in'] = (e, _ab_unpack(dwall).reshape(NDEV, AB_IN // NDEV, DM))
            layer['ab_w_out'] = (e, dwout.reshape(NDEV, DM // NDEV, DM))
        else:
            o = l // 2
            (dx, partial['g_mix_pre'][l], partial['g_mix_post'][l], dwin, partial['c_conv_w'][o], dconvb, dwai, dbai, dlam,
             dwout) = _lru_layer_bwd(dx, s_mix, *mixer_args(l), after=token)
            partial['c_conv_b'][o], partial['c_lam'][o] = dconvb[0], dlam[0]
            partial['c_b_a'][o], partial['c_b_i'][o] = dbai[0].reshape(LRU_NB, LRU_BW), dbai[1].reshape(LRU_NB, LRU_BW)
            rows = LRU_BW // NDEV
            by_dev = lambda d: d.reshape(LRU_NB, NDEV, rows, LRU_BW).transpose(1, 0, 2, 3).reshape(NDEV, LRU_NB * rows, LRU_BW)
            layer['c_w_in'] = (o, dwin)
            layer['c_w_a'] = (o, by_dev(dwai[0]))
            layer['c_w_i'] = (o, by_dev(dwai[1]))
            layer['c_w_out'] = (o, dwout.reshape(NDEV, DM // NDEV, DM))
        token = None
        if ici is not None:
            finish(ici, dx)
            ici = None
        if l > 0:
            gs, names, where = unit(layer, list(layer))
            state, token = _rs_d2d_start(_as_g5(gs), "rs_d2d_start_%d" % l)
            d2d = (state, names, where)
    small_names = REPLICATED + SMALL_SHARDED
    small_parts = [jnp.stack([p.reshape(w[n].shape[1:] if n in REPLICATED else small[n].shape[1:]) for p in partial[n]])
                   for n in small_names]
    small_all = _small_gather(_pack_small(small_parts))
    reduced = _unpack_small(_sum_devices(small_all), [p.shape for p in small_parts])
    grad = {}
    for n, g in zip(small_names, reduced):
        grad[n] = g if n in REPLICATED else _small_shard(g, dev)

    gs, names, where = unit(layer, mixer_names(0))
    state, token = _rs_begin(gs, pos, "0m", after=small_all)
    ici_mixer = (state, names, where)
    finish(ici_rest, dx)

    delta, new_m, new_v = {}, {}, {}
    last = mixer_names(0)
    for n in BIG:
        if n not in last:
            grad[n] = jnp.stack(grads[n]).reshape(w[n].shape)
            delta[n], new_m[n], new_v[n] = _adamw_nd(w[n], grad[n], mom[n], var[n], token)
            token = delta[n]
    shapes = [w[n].shape for n in small_names]
    packed = [_pack_small([t[n] for n in small_names]) for t in (w, grad, mom, var)]
    res_small = _adamw(*packed, after=token)
    for res, out in zip(res_small, (delta, new_m, new_v)):
        for n, val in zip(small_names, _unpack_small(res, shapes)):
            out[n] = val
    finish(ici_mixer, res_small[0])
    for n in last:
        if n == 'ab_w_in':
            g_t = jnp.stack(grads[n], axis=1)
            res = (g_t,) + _adamw_nd(ab_t[0], g_t, ab_t[1], ab_t[2])
            grad[n], delta[n], new_m[n], new_v[n] = (r.transpose(1, 2, 0) for r in res)
            continue
        grad[n] = jnp.stack(grads[n]).reshape(w[n].shape)
        delta[n], new_m[n], new_v[n] = _adamw_nd(w[n], grad[n], mom[n], var[n])

    for t in (grad, delta, new_m, new_v):
        t['w_ffn_gu'] = t['w_ffn_gu'].transpose(0, 2, 1)
    return (loss, dx[None], *[grad[n] for n in WEIGHT_NAMES], *[delta[n] for n in WEIGHT_NAMES],
            *[new_m[n] for n in WEIGHT_NAMES], *[new_v[n] for n in WEIGHT_NAMES])
```

```python
import math

import jax
import jax.numpy as jnp
from jax import lax
from jax.experimental import pallas as pl
from jax.experimental.pallas import tpu as pltpu

F32 = jnp.float32
BF16 = jnp.bfloat16
BS = pl.BlockSpec
ANY = pl.BlockSpec(memory_space=pl.ANY)
MESH = pl.DeviceIdType.MESH

DM = 1024
DEPTH = 4
EPS = 1e-6
NEG = -1e30
FOX_W = 512
FOX_HD = 64
FOX_H = 8
SC_W = 512
SC_K = 3
AB_IN = 3 * FOX_W + FOX_H + 3 * SC_W
AB_PAD = 3200
LRU_BW = 256
LRU_NB = 4
RG_K = 4
RG_C = 8.0
MEM_H = 4
MEM_HD = 256
D_FF = 2816
NDEV = 8
FFB = 2 * D_FF // NDEV
ADAM_LR, ADAM_B1, ADAM_B2, ADAM_EPS, ADAM_WD, ADAM_STEP = 0.001, 0.9, 0.999, 1e-08, 0.01, 10

LANE = 128
VMEM_LIMIT = 16 * 1024 * 1024
VMEM_MM = 32 * 1024 * 1024
VMEM_BIG = 40 * 1024 * 1024


def _params(ngrid, vmem=None):
    return pltpu.CompilerParams(dimension_semantics=("arbitrary",) * ngrid, vmem_limit_bytes=vmem or VMEM_LIMIT)


def _call(kern, **kwargs):
    return pl.pallas_call(kern, **kwargs)


TK_RED = 2048
TM_SUM = 512
MM_TM = 1024


def _tile(n, t):
    return t if n % t == 0 else n


def _mm(name, a, b, *, grid, a_spec, b_spec, o_spec, out_shape, dn, out_dtype=F32, vmem=None):
    nred = grid[-1]
    ngrid = len(grid)

    def kern(a_ref, b_ref, o_ref, *scratch):
        p = lax.dot_general(a_ref[...].astype(BF16), b_ref[...].astype(BF16), (dn, ((), ())),
                            preferred_element_type=F32)
        if nred == 1:
            o_ref[...] = p.astype(o_ref.dtype)
            return
        acc = scratch[0] if scratch else o_ref
        r = pl.program_id(ngrid - 1)

        @pl.when(r == 0)
        def _():
            acc[...] = p

        @pl.when(r > 0)
        def _():
            acc[...] += p

        if scratch:
            @pl.when(r == nred - 1)
            def _():
                o_ref[...] = acc[...].astype(o_ref.dtype)

    blk = tuple(d for d in o_spec.block_shape if d is not None)
    scratch = [pltpu.VMEM(blk, F32)] if (nred > 1 and out_dtype != F32) else []
    return _call(kern, name=name, grid=grid, in_specs=[a_spec, b_spec], out_specs=o_spec,
                          out_shape=jax.ShapeDtypeStruct(out_shape, out_dtype), scratch_shapes=scratch,
                          compiler_params=_params(ngrid, vmem or (VMEM_LIMIT if dn == TN else VMEM_MM)))(a, b)


NN = ((1,), (0,))
NT = ((1,), (1,))
TN = ((0,), (0,))


def _mm_nn(name, a, w, out_dtype=F32, tn=None, vmem=None):
    m, k = a.shape
    n = w.shape[1]
    tm = _tile(m, MM_TM)
    tn = n if tn is None else tn
    return _mm(name, a, w, grid=(m // tm, n // tn, 1), a_spec=BS((tm, k), lambda i, j, r: (i, 0)),
               b_spec=BS((k, tn), lambda i, j, r: (0, j)), o_spec=BS((tm, tn), lambda i, j, r: (i, j)),
               out_shape=(m, n), dn=NN, out_dtype=out_dtype, vmem=vmem)


def _mm_nt_cols(name, a, wt, tn):
    m, k = a.shape
    n = wt.shape[0]
    tm = _tile(m, MM_TM)
    return _mm(name, a, wt, grid=(m // tm, n // tn, 1), a_spec=BS((tm, k), lambda i, j, r: (i, 0)),
               b_spec=BS((tn, k), lambda i, j, r: (j, 0)), o_spec=BS((tm, tn), lambda i, j, r: (i, j)),
               out_shape=(m, n), dn=NT)


def _mm_tn_rows(name, a, b, tk):
    m, k = a.shape
    n = b.shape[1]
    tm = _tile(m, TK_RED)
    return _mm(name, a, b, grid=(k // tk, m // tm), a_spec=BS((tm, tk), lambda j, r: (r, j)),
               b_spec=BS((tm, n), lambda j, r: (r, 0)), o_spec=BS((tk, n), lambda j, r: (j, 0)),
               out_shape=(k, n), dn=TN)


def _mm_nt(name, a, w, out_dtype=F32, tn=None):
    m, n = a.shape
    k = w.shape[0]
    tm = _tile(m, MM_TM)
    tn = n if tn is None else tn
    return _mm(name, a, w, grid=(m // tm, n // tn), a_spec=BS((tm, tn), lambda i, r: (i, r)),
               b_spec=BS((k, tn), lambda i, r: (0, r)), o_spec=BS((tm, k), lambda i, r: (i, 0)),
               out_shape=(m, k), dn=NT, out_dtype=out_dtype)


def _mm_tn(name, a, b, tn=None):
    m, k = a.shape
    n = b.shape[1]
    tm = _tile(m, TK_RED)
    tn = n if tn is None else tn
    return _mm(name, a, b, grid=(n // tn, m // tm), a_spec=BS((tm, k), lambda j, r: (r, 0)),
               b_spec=BS((tm, tn), lambda j, r: (r, j)), o_spec=BS((k, tn), lambda j, r: (0, j)),
               out_shape=(k, n), dn=TN)


def _bmm_nn(name, a, w, out_dtype=F32):
    m, k = a.shape
    g, _, n = w.shape
    tm = _tile(m, MM_TM)
    return _mm(name, a, w, grid=(g, m // tm, 1), a_spec=BS((tm, k), lambda q, i, r: (i, 0)),
               b_spec=BS((None, k, n), lambda q, i, r: (q, 0, 0)), o_spec=BS((None, tm, n), lambda q, i, r: (q, i, 0)),
               out_shape=(g, m, n), dn=NN, out_dtype=out_dtype)


def _bmm_tn(name, a, b):
    m, k = a.shape
    g, _, n = b.shape
    tm = _tile(m, TK_RED)
    return _mm(name, a, b, grid=(g, m // tm), a_spec=BS((tm, k), lambda q, r: (r, 0)),
               b_spec=BS((None, tm, n), lambda q, r: (q, r, 0)), o_spec=BS((None, k, n), lambda q, r: (q, 0, 0)),
               out_shape=(g, k, n), dn=TN)


def _block_sum(name, a, w, dn, out_cols):
    g, m, ac = a.shape
    tm = _tile(m, TM_SUM)

    def kern(a_ref, w_ref, o_ref):
        acc = None
        for q in range(g):
            p = lax.dot_general(a_ref[q].astype(BF16), w_ref[q].astype(BF16), (dn, ((), ())), preferred_element_type=F32)
            acc = p if acc is None else acc + p
        o_ref[...] = acc

    return _call(kern, name=name, grid=(m // tm,),
                 in_specs=[BS((g, tm, ac), lambda i: (0, i, 0)), BS(w.shape, lambda i: (0, 0, 0))],
                 out_specs=BS((tm, out_cols), lambda i: (i, 0)), out_shape=jax.ShapeDtypeStruct((m, out_cols), F32),
                 compiler_params=_params(1, VMEM_BIG))(a, w)


def _bmm_nt_sum(name, a, w):
    return _block_sum(name, a, w, NT, w.shape[1])


def _bmm_nn_sum(name, a, w):
    return _block_sum(name, a, w, NN, w.shape[2])


def _rstd(x):
    return lax.rsqrt(jnp.mean(x * x, axis=-1, keepdims=True) + EPS)


def _norm_fwd(x, g, after=None):
    rows = x.shape[0]
    tm = _tile(rows, 512)

    def kern(x_ref, g_ref, *rest):
        xv = x_ref[...]
        rest[-1][...] = ((xv * _rstd(xv)) * g_ref[...]).astype(BF16)

    extra = () if after is None else (after,)
    return _call(kern, name="norm_fwd", grid=(rows // tm,),
                          in_specs=[BS((tm, DM), lambda i: (i, 0)), BS((1, DM), lambda i: (0, 0))] + [ANY] * len(extra),
                          out_specs=BS((tm, DM), lambda i: (i, 0)),
                          out_shape=jax.ShapeDtypeStruct((rows, DM), BF16), compiler_params=_params(1))(x, g, *extra)


def _norm_res(x, y, g):
    rows = x.shape[0]
    tm = _tile(rows, 512)

    def kern(x_ref, y_ref, g_ref, o_ref):
        yv = y_ref[...]
        o_ref[...] = x_ref[...] + (yv * _rstd(yv)) * g_ref[...]

    row = BS((tm, DM), lambda i: (i, 0))
    return _call(kern, name="norm_res", grid=(rows // tm,),
                          in_specs=[row, row, BS((1, DM), lambda i: (0, 0))], out_specs=row,
                          out_shape=jax.ShapeDtypeStruct((rows, DM), F32), compiler_params=_params(1))(x, y, g)


def _norm_bwd(z, dout, g, resid, out_dtype, after=None):
    rows = z.shape[0]
    tm = _tile(rows, 512)
    has_res = resid is not None

    def kern(*refs):
        z_ref, d_ref, g_ref = refs[:3]
        r_ref = refs[3] if has_res else None
        dz_ref, dg_ref = refs[-2:]
        zv = z_ref[...]
        dv = d_ref[...].astype(F32)
        r = _rstd(zv)
        zh = zv * r
        dzh = dv * g_ref[...]
        dz = r * (dzh - zh * jnp.mean(dzh * zh, axis=-1, keepdims=True))
        if has_res:
            dz = dz + r_ref[...]
        dz_ref[...] = dz.astype(dz_ref.dtype)
        part = jnp.sum(dv * zh, axis=0, keepdims=True)

        @pl.when(pl.program_id(0) == 0)
        def _():
            dg_ref[...] = part

        @pl.when(pl.program_id(0) > 0)
        def _():
            dg_ref[...] += part

    row = BS((tm, DM), lambda i: (i, 0))
    vec = BS((1, DM), lambda i: (0, 0))
    ins = [row, row, vec] + ([row] if has_res else []) + ([ANY] if after is not None else [])
    args = (z, dout, g) + ((resid,) if has_res else ()) + ((after,) if after is not None else ())
    return _call(kern, name="norm_bwd_res" if has_res else "norm_bwd", grid=(rows // tm,), in_specs=ins,
                          out_specs=[row, vec],
                          out_shape=[jax.ShapeDtypeStruct((rows, DM), out_dtype), jax.ShapeDtypeStruct((1, DM), F32)],
                          compiler_params=_params(1))(*args)


def _ffn_up(h, wgu4):
    s = h.shape[0]
    tm = _tile(s, MM_TM)

    def kern(h_ref, w_ref, gu_ref, a_ref):
        hv = h_ref[...]
        gate = lax.dot_general(hv, w_ref[0], (NT, ((), ())), preferred_element_type=F32)
        up = lax.dot_general(hv, w_ref[1], (NT, ((), ())), preferred_element_type=F32)
        gu_ref[0] = gate.astype(BF16)
        gu_ref[1] = up.astype(BF16)
        a_ref[...] = (gate * jax.nn.sigmoid(gate) * up).astype(BF16)

    return _call(
        kern, name="ffn_up", grid=(4, s // tm),
        in_specs=[BS((tm, DM), lambda j, i: (i, 0)), BS((2, None, FFB, DM), lambda j, i: (0, j, 0, 0))],
        out_specs=[BS((2, None, tm, FFB), lambda j, i: (0, j, i, 0)), BS((None, tm, FFB), lambda j, i: (j, i, 0))],
        out_shape=[jax.ShapeDtypeStruct((2, 4, s, FFB), BF16), jax.ShapeDtypeStruct((4, s, FFB), BF16)],
        compiler_params=_params(2, VMEM_BIG))(h, wgu4)


def _ffn_da(dy, wd4, gu):
    s = dy.shape[0]
    tm = _tile(s, MM_TM)

    def kern(dy_ref, w_ref, gu_ref, o_ref):
        da = lax.dot_general(dy_ref[...], w_ref[...], (NT, ((), ())), preferred_element_type=F32)
        gate = gu_ref[0].astype(F32)
        up = gu_ref[1].astype(F32)
        sg = jax.nn.sigmoid(gate)
        o_ref[0] = (da * up * (sg * (1.0 + gate * (1.0 - sg)))).astype(BF16)
        o_ref[1] = (da * (gate * sg)).astype(BF16)

    blk = BS((2, None, tm, FFB), lambda j, i: (0, j, i, 0))
    return _call(
        kern, name="ffn_da", grid=(4, s // tm),
        in_specs=[BS((tm, DM), lambda j, i: (i, 0)), BS((None, FFB, DM), lambda j, i: (j, 0, 0)), blk],
        out_specs=blk, out_shape=jax.ShapeDtypeStruct((2, 4, s, FFB), BF16), compiler_params=_params(2, VMEM_BIG))(dy, wd4, gu)


def _ffn_fwd(x, gpre, gpost, wgu, wd):
    h = _norm_fwd(x, gpre)
    gu, a = _ffn_up(h, wgu.reshape(2, 4, FFB, DM))
    y = _bmm_nn_sum("ffn_down", a, wd.reshape(4, FFB, DM))
    return _norm_res(x, y, gpost), (x, h, gu, a, y)


def _ffn_bwd(dxo, saved, gpre, gpost, wgu, wd, after=None):
    x, h, gu, a, y = saved
    s = x.shape[0]
    dy, dgpost = _norm_bwd(y, dxo, gpost, None, BF16, after)
    dgu = _ffn_da(dy, wd.reshape(4, FFB, DM), gu).reshape(8, s, FFB)
    dwd = _bmm_tn_a3("ffn_dwd", a, dy)
    dwgu = _bmm_tn_a3("ffn_dwgu", dgu, h)
    dh = _bmm_nn_sum("ffn_dh", dgu, wgu)
    dx, dgpre = _norm_bwd(x, dh, gpre, dxo, F32)
    return dx, dgpre, dgpost, dwgu, dwd.reshape(D_FF, DM)


def _bmm_tn_a3(name, a, b):
    g, m, k = a.shape
    n = b.shape[1]
    tm = _tile(m, TK_RED)
    return _mm(name, a, b, grid=(g, m // tm), a_spec=BS((None, tm, k), lambda q, r: (q, r, 0)),
               b_spec=BS((tm, n), lambda q, r: (r, 0)), o_spec=BS((None, k, n), lambda q, r: (q, 0, 0)),
               out_shape=(g, k, n), dn=TN)


XATTN_TM = 2048


def _softmax_rows(s):
    m = jnp.max(s, axis=-1, keepdims=True)
    p = jnp.exp(s - m)
    return p / jnp.sum(p, axis=-1, keepdims=True)


def _xattn_fwd_call(h, wq, kv):
    s = h.shape[0]
    mlen = kv.shape[1]
    tm = _tile(s, XATTN_TM)
    scale = MEM_HD ** -0.5

    def kern(h_ref, w_ref, k_ref, v_ref, q_ref, o_ref):
        q = jnp.dot(h_ref[...], w_ref[...], preferred_element_type=F32).astype(BF16)
        q_ref[...] = q
        sc = lax.dot_general(q, k_ref[...], (NT, ((), ())), preferred_element_type=F32) * scale
        p = _softmax_rows(sc)
        o_ref[...] = jnp.dot(p.astype(BF16), v_ref[...], preferred_element_type=F32).astype(BF16)

    blk = BS((tm, MEM_HD), lambda i, hd: (i, hd))
    return _call(
        kern, name="xattn_fwd", grid=(s // tm, MEM_H),
        in_specs=[BS((tm, DM), lambda i, hd: (i, 0)), BS((DM, MEM_HD), lambda i, hd: (0, hd)),
                  BS((None, mlen, MEM_HD), lambda i, hd: (hd, 0, 0)),
                  BS((None, mlen, MEM_HD), lambda i, hd: (MEM_H + hd, 0, 0))],
        out_specs=[blk, blk],
        out_shape=[jax.ShapeDtypeStruct((s, DM), BF16), jax.ShapeDtypeStruct((s, DM), BF16)],
        compiler_params=_params(2, VMEM_MM))(h, wq, kv, kv)


def _xattn_bwd_call(q, kv, do):
    s = q.shape[0]
    mlen = kv.shape[1]
    tm = _tile(s, XATTN_TM)
    scale = MEM_HD ** -0.5

    def kern(q_ref, k_ref, v_ref, do_ref, dq_ref, dkv_ref):
        qv, kvv, vv, dov = q_ref[...], k_ref[...], v_ref[...], do_ref[...]
        sc = lax.dot_general(qv, kvv, (NT, ((), ())), preferred_element_type=F32) * scale
        p = _softmax_rows(sc)
        dp = lax.dot_general(dov, vv, (NT, ((), ())), preferred_element_type=F32)
        ds = (p * (dp - jnp.sum(dp * p, axis=-1, keepdims=True)) * scale).astype(BF16)
        dq_ref[...] = jnp.dot(ds, kvv, preferred_element_type=F32).astype(BF16)
        dk = lax.dot_general(ds, qv, (TN, ((), ())), preferred_element_type=F32)
        dv = lax.dot_general(p.astype(BF16), dov, (TN, ((), ())), preferred_element_type=F32)

        @pl.when(pl.program_id(1) == 0)
        def _():
            dkv_ref[0] = dk
            dkv_ref[1] = dv

        @pl.when(pl.program_id(1) > 0)
        def _():
            dkv_ref[0] += dk
            dkv_ref[1] += dv

    blk = BS((tm, MEM_HD), lambda hd, i: (i, hd))
    return _call(
        kern, name="xattn_bwd", grid=(MEM_H, s // tm),
        in_specs=[blk, BS((None, mlen, MEM_HD), lambda hd, i: (hd, 0, 0)),
                  BS((None, mlen, MEM_HD), lambda hd, i: (MEM_H + hd, 0, 0)), blk],
        out_specs=[blk, BS((2, None, mlen, MEM_HD), lambda hd, i: (0, hd, 0, 0))],
        out_shape=[jax.ShapeDtypeStruct((s, DM), BF16), jax.ShapeDtypeStruct((2, MEM_H, mlen, MEM_HD), F32)],
        compiler_params=_params(2, VMEM_MM))(q, kv, kv, do)


def _cross_fwd(x, mem, gpre, gmem, gpost, wq, wkv, wo, after=None):
    h = _norm_fwd(x, gpre, after)
    mn = _norm_fwd(mem, gmem)
    kv = _bmm_nn("xattn_kv", mn, wkv, BF16)
    q, o = _xattn_fwd_call(h, wq, kv)
    y = _mm_nn("xattn_out", o, wo)
    return _norm_res(x, y, gpost), (x, h, mn, kv, q, o, y)


def _cross_bwd(dxo, saved, mem, gpre, gmem, gpost, wq, wkv, wo, after=None):
    x, h, mn, kv, q, o, y = saved
    mlen = mem.shape[0]
    dy, dgpost = _norm_bwd(y, dxo, gpost, None, BF16, after)
    do = _mm_nt("xattn_do", dy, wo, BF16)
    dwo = _mm_tn("xattn_dwo", o, dy)
    dq, dkv = _xattn_bwd_call(q, kv, do)
    dwq = _mm_tn("xattn_dwq", h, dq)
    dh = _mm_nt("xattn_dh", dq, wq)
    dkv8 = dkv.reshape(8, mlen, MEM_HD)
    dwkv = _bmm_tn("xattn_dwkv", mn, dkv8)
    dmn = _bmm_nt_sum("xattn_dmn", dkv8, wkv)
    _, dgmem = _norm_bwd(mem, dmn, gmem, None, BF16)
    dx, dgpre = _norm_bwd(x, dh, gpre, dxo, F32)
    return dx, dgpre, dgmem, dgpost, dwq, dwkv, dwo


def _log_sigmoid(z):
    return jnp.minimum(z, 0.0) - jnp.log1p(jnp.exp(-jnp.abs(z)))


def _lane_scan_steps():
    return (1, 2, 4, 8, 16, 32, 64)


def _fox_cum(frow, bfb):
    s = frow.shape[1]

    def kern(f_ref, b_ref, o_ref):
        lane = lax.broadcasted_iota(jnp.int32, (FOX_H, LANE), 1)
        carry = jnp.zeros((FOX_H, 1), F32)
        for c in range(s // LANE):
            sl = slice(c * LANE, (c + 1) * LANE)
            lf = _log_sigmoid(f_ref[:, sl] + b_ref[...])
            v = lf
            for d in _lane_scan_steps():
                v = v + jnp.where(lane >= d, pltpu.roll(v, d, 1), 0.0)
            o_ref[:, sl] = v + carry
            carry = carry + jnp.sum(lf, axis=1, keepdims=True)

    return _call(kern, name="fox_cum", out_shape=jax.ShapeDtypeStruct((FOX_H, s), F32),
                          compiler_params=pltpu.CompilerParams(vmem_limit_bytes=VMEM_LIMIT))(frow, bfb)


def _fox_dlogf(dcq, dck, frow, bfb):
    s = frow.shape[1]

    def kern(q_ref, d_ref, f_ref, b_ref, df_ref, db_ref):
        lane = lax.broadcasted_iota(jnp.int32, (FOX_H, LANE), 1)
        carry = jnp.zeros((FOX_H, 1), F32)
        dbf = jnp.zeros((FOX_H, 1), F32)
        for c in reversed(range(s // LANE)):
            sl = slice(c * LANE, (c + 1) * LANE)
            dc = q_ref[:, sl] - d_ref[:, sl]
            v = dc
            for d in _lane_scan_steps():
                v = v + jnp.where(lane < LANE - d, pltpu.roll(v, LANE - d, 1), 0.0)
            v = v + carry
            carry = carry + jnp.sum(dc, axis=1, keepdims=True)
            df = v * jax.nn.sigmoid(-(f_ref[:, sl] + b_ref[...]))
            df_ref[:, sl] = df
            dbf = dbf + jnp.sum(df, axis=1, keepdims=True)
        db_ref[...] = jnp.broadcast_to(dbf, (FOX_H, LANE))

    return _call(kern, name="fox_dlogf",
                          out_shape=[jax.ShapeDtypeStruct((FOX_H, s), F32), jax.ShapeDtypeStruct((FOX_H, LANE), F32)],
                          compiler_params=pltpu.CompilerParams(vmem_limit_bytes=VMEM_LIMIT))(dcq, dck, frow, bfb)


FOX_TQ = 512
Q_COL, K_COL, V_COL = 0, FOX_W // LANE, 2 * FOX_W // LANE
B_COL = 3 * FOX_W // LANE
C_COL = B_COL + SC_W // LANE
U_COL = C_COL + SC_W // LANE


def _bf16_terms(c):
    hi = c.astype(BF16).astype(F32)
    mid = (c - hi).astype(BF16).astype(F32)
    return hi, mid, (c - hi - mid).astype(BF16).astype(F32)


def _fox_operands(qv, kv, cq, ck, lane, hh, scale):
    sel = (lane < FOX_HD) if hh == 0 else (lane >= FOX_HD)
    b0 = FOX_HD if hh == 0 else 0
    qa = jnp.where(sel, qv * scale, 0.0)
    ka = jnp.where(sel, kv, 0.0)
    for n, (tq_, tk_) in enumerate(zip(_bf16_terms(cq), _bf16_terms(ck))):
        qa = jnp.where(lane == b0 + n, tq_, jnp.where(lane == b0 + 3 + n, 1.0, qa))
        ka = jnp.where(lane == b0 + n, 1.0, jnp.where(lane == b0 + 3 + n, -tk_, ka))
    return sel, qa.astype(BF16), ka.astype(BF16)


def _fox_logits(qa, ka, causal):
    sc = lax.dot_general(qa, ka, (NT, ((), ())), preferred_element_type=F32)
    return sc if causal is None else jnp.where(causal, sc, NEG)


def _fox_prep(proj, cumc):
    s = proj.shape[0]
    tp = _tile(s, 512)
    scale = FOX_HD ** -0.5

    def kern(q_ref, k_ref, c_ref, qa_ref, ka_ref):
        lane = lax.broadcasted_iota(jnp.int32, (tp, LANE), 1)
        for hh in range(2):
            _, qa_ref[hh], ka_ref[hh] = _fox_operands(q_ref[...], k_ref[...], c_ref[hh], c_ref[hh], lane, hh, scale)

    pair = BS((2, tp, LANE), lambda hp, i: (hp, i, 0))
    shp = jax.ShapeDtypeStruct((FOX_H, s, LANE), BF16)
    return _call(kern, name="fox_prep", grid=(4, s // tp),
                 in_specs=[BS((tp, LANE), lambda hp, i: (i, Q_COL + hp)), BS((tp, LANE), lambda hp, i: (i, K_COL + hp)), pair],
                 out_specs=[pair, pair], out_shape=[shp, shp], compiler_params=_params(2))(proj, proj, cumc)


def _fox_fwd_call(proj, qa, ka):
    s = proj.shape[0]
    tq = _tile(s, FOX_TQ)
    nq = s // tq

    def kern(qa_ref, ka_ref, v_ref, o_ref, lse_ref, m_s, l_s, acc_s):
        i = pl.program_id(1)
        j = pl.program_id(2)
        lane = lax.broadcasted_iota(jnp.int32, (tq, LANE), 1)

        @pl.when(j == 0)
        def _():
            m_s[...] = jnp.full(m_s.shape, NEG, F32)
            l_s[...] = jnp.zeros(l_s.shape, F32)
            acc_s[...] = jnp.zeros(acc_s.shape, F32)

        def step(diagonal):
            vb = v_ref[...].astype(BF16)
            causal = (lax.broadcasted_iota(jnp.int32, (tq, tq), 0) >= lax.broadcasted_iota(jnp.int32, (tq, tq), 1)
                      if diagonal else None)
            for hh in range(2):
                sc = _fox_logits(qa_ref[hh], ka_ref[hh], causal)
                m_prev = m_s[hh]
                m_new = jnp.maximum(m_prev, jnp.max(sc, axis=-1, keepdims=True))
                alpha = jnp.exp(m_prev - m_new)
                p = jnp.exp(sc - m_new)
                l_s[hh] = alpha * l_s[hh] + jnp.sum(p, axis=-1, keepdims=True)
                acc_s[hh] = alpha * acc_s[hh] + jnp.dot(p.astype(BF16), vb, preferred_element_type=F32)
                m_s[hh] = m_new

        @pl.when(j < i)
        def _():
            step(False)

        @pl.when(j == i)
        def _():
            step(True)
            o_ref[...] = jnp.where(lane < FOX_HD, acc_s[0] / l_s[0], acc_s[1] / l_s[1])
            for hh in range(2):
                lse_ref[hh] = jnp.broadcast_to(m_s[hh] + jnp.log(l_s[hh]), (tq, LANE))

    kvi = lambda hp, i, j: jnp.minimum(j, i)
    return _call(
        kern, name="fox_fwd", grid=(4, nq, nq),
        in_specs=[BS((2, tq, LANE), lambda hp, i, j: (hp, i, 0)),
                  BS((2, tq, LANE), lambda hp, i, j: (hp, kvi(hp, i, j), 0)),
                  BS((tq, LANE), lambda hp, i, j: (kvi(hp, i, j), V_COL + hp))],
        out_specs=[BS((tq, LANE), lambda hp, i, j: (i, hp)), BS((2, tq, LANE), lambda hp, i, j: (hp, i, 0))],
        out_shape=[jax.ShapeDtypeStruct((s, FOX_W), F32), jax.ShapeDtypeStruct((FOX_H, s, LANE), F32)],
        scratch_shapes=[pltpu.VMEM((2, tq, 1), F32), pltpu.VMEM((2, tq, 1), F32), pltpu.VMEM((2, tq, LANE), F32)],
        compiler_params=_params(3))(qa, ka, proj)


ROWSUM_M = 16


def _fox_bwd_call(proj, o, lse, dcat, qa, ka):
    s = proj.shape[0]
    tq = _tile(s, FOX_TQ)
    nq = s // tq
    reps = tq // LANE
    scale = FOX_HD ** -0.5

    def kern(qa_ref, ka_ref, v_ref, do_ref, o_ref, lse_ref, dq_ref, dk_ref, dv_ref, dck_ref, dcq_ref):
        j = pl.program_id(1)
        i = pl.program_id(2)
        lane = lax.broadcasted_iota(jnp.int32, (tq, LANE), 1)
        ones = jnp.ones((ROWSUM_M, tq), BF16)

        @pl.when((j == 0) & (i == 0))
        def _():
            dq_ref[...] = jnp.zeros(dq_ref.shape, F32)
            dcq_ref[...] = jnp.zeros(dcq_ref.shape, F32)

        @pl.when(i == j)
        def _():
            dk_ref[...] = jnp.zeros(dk_ref.shape, F32)
            dv_ref[...] = jnp.zeros(dv_ref.shape, F32)
            dck_ref[...] = jnp.zeros(dck_ref.shape, F32)

        def step(diagonal):
            dov = do_ref[...]
            ov = o_ref[...]
            vb = v_ref[...].astype(BF16)
            causal = (lax.broadcasted_iota(jnp.int32, (tq, tq), 0) >= lax.broadcasted_iota(jnp.int32, (tq, tq), 1)
                      if diagonal else None)
            dq_t = jnp.zeros((tq, LANE), F32)
            dk_t = jnp.zeros((tq, LANE), F32)
            dv_t = jnp.zeros((tq, LANE), F32)
            for hh in range(2):
                sel = (lane < FOX_HD) if hh == 0 else (lane >= FOX_HD)
                qa, ka = qa_ref[hh], ka_ref[hh]
                dom32 = jnp.where(sel, dov, 0.0)
                dom = dom32.astype(BF16)
                sc = _fox_logits(qa, ka, causal)
                p = jnp.exp(sc - jnp.tile(lse_ref[hh], (1, reps)))
                dp = lax.dot_general(dom, vb, (NT, ((), ())), preferred_element_type=F32)
                delta = jnp.sum(dom32 * ov, axis=-1, keepdims=True)
                ds = p * (dp - delta)
                dsb = ds.astype(BF16)
                dq_t = jnp.where(sel, jnp.dot(dsb, ka, preferred_element_type=F32) * scale, dq_t)
                dk_t = jnp.where(sel, lax.dot_general(dsb, qa, (TN, ((), ())), preferred_element_type=F32), dk_t)
                dv_t = dv_t + lax.dot_general(p.astype(BF16), dom, (TN, ((), ())), preferred_element_type=F32)
                dck_ref[hh] += jnp.sum(ds, axis=0, keepdims=True)
                ds_lo = (ds - dsb.astype(F32)).astype(BF16)
                dcq_ref[hh, i] += (lax.dot_general(ones, dsb, (NT, ((), ())), preferred_element_type=F32)
                                   + lax.dot_general(ones, ds_lo, (NT, ((), ())), preferred_element_type=F32))
            rows = pl.ds(pl.multiple_of(i * tq, tq), tq)
            dq_ref[rows, :] += dq_t
            dk_ref[...] += dk_t
            dv_ref[...] += dv_t

        @pl.when(i > j)
        def _():
            step(False)

        @pl.when(i == j)
        def _():
            step(True)

    qi = lambda hp, j, i: jnp.maximum(i, j)
    return _call(
        kern, name="fox_bwd", grid=(4, nq, nq),
        in_specs=[BS((2, tq, LANE), lambda hp, j, i: (hp, qi(hp, j, i), 0)),
                  BS((2, tq, LANE), lambda hp, j, i: (hp, j, 0)),
                  BS((tq, LANE), lambda hp, j, i: (j, V_COL + hp)),
                  BS((tq, LANE), lambda hp, j, i: (qi(hp, j, i), hp)),
                  BS((tq, LANE), lambda hp, j, i: (qi(hp, j, i), hp)),
                  BS((2, tq, LANE), lambda hp, j, i: (hp, qi(hp, j, i), 0))],
        out_specs=[BS((s, LANE), lambda hp, j, i: (0, hp)), BS((tq, LANE), lambda hp, j, i: (j, hp)),
                   BS((tq, LANE), lambda hp, j, i: (j, hp)), BS((2, 1, tq), lambda hp, j, i: (hp, 0, j)),
                   BS((2, nq, ROWSUM_M, tq), lambda hp, j, i: (hp, 0, 0, 0))],
        out_shape=[jax.ShapeDtypeStruct((s, FOX_W), F32), jax.ShapeDtypeStruct((s, FOX_W), F32),
                   jax.ShapeDtypeStruct((s, FOX_W), F32), jax.ShapeDtypeStruct((FOX_H, 1, s), F32),
                   jax.ShapeDtypeStruct((FOX_H, nq, ROWSUM_M, tq), F32)],
        compiler_params=_params(3))(qa, ka, proj, dcat, o, lse)


def _shift_down(v, d, row):
    return jnp.where(row >= d, pltpu.roll(v, d, 0), 0.0)


def _shift_up(v, d, row, n):
    return jnp.where(row < n - d, pltpu.roll(v, n - d, 0), 0.0)


def _sconv_fwd(proj, convw):
    s = proj.shape[0]

    def kern(b_ref, c_ref, u_ref, w_ref, y_ref):
        row = lax.broadcasted_iota(jnp.int32, (s, LANE), 0)
        z = c_ref[...] * u_ref[...]
        conv = w_ref[2:3, :] * z + w_ref[1:2, :] * _shift_down(z, 1, row) + w_ref[0:1, :] * _shift_down(z, 2, row)
        y_ref[...] = (b_ref[...] * conv).astype(BF16)

    col = lambda base: BS((s, LANE), lambda cb: (0, base + cb))
    return _call(kern, name="sconv_fwd", grid=(SC_W // LANE,),
                          in_specs=[col(B_COL), col(C_COL), col(U_COL), BS((SC_K, LANE), lambda cb: (0, cb))],
                          out_specs=BS((s, LANE), lambda cb: (0, cb)),
                          out_shape=jax.ShapeDtypeStruct((s, SC_W), BF16), compiler_params=_params(1))(proj, proj, proj, convw)


def _sconv_bwd(proj, convw, dcat):
    s = proj.shape[0]

    def kern(b_ref, c_ref, u_ref, w_ref, dy_ref, db_ref, dc_ref, du_ref, dw_ref):
        row = lax.broadcasted_iota(jnp.int32, (s, LANE), 0)
        cv, uv, dyv = c_ref[...], u_ref[...], dy_ref[...]
        z = cv * uv
        z1 = _shift_down(z, 1, row)
        z2 = _shift_down(z, 2, row)
        conv = w_ref[2:3, :] * z + w_ref[1:2, :] * z1 + w_ref[0:1, :] * z2
        db_ref[...] = dyv * conv
        dcv = dyv * b_ref[...]
        dz = w_ref[2:3, :] * dcv + w_ref[1:2, :] * _shift_up(dcv, 1, row, s) + w_ref[0:1, :] * _shift_up(dcv, 2, row, s)
        dc_ref[...] = dz * uv
        du_ref[...] = dz * cv
        dw_ref[0:1, :] = jnp.sum(dcv * z2, axis=0, keepdims=True)
        dw_ref[1:2, :] = jnp.sum(dcv * z1, axis=0, keepdims=True)
        dw_ref[2:3, :] = jnp.sum(dcv * z, axis=0, keepdims=True)

    col = lambda base: BS((s, LANE), lambda cb: (0, base + cb))
    out = BS((s, LANE), lambda cb: (0, cb))
    wspec = BS((SC_K, LANE), lambda cb: (0, cb))
    act = jax.ShapeDtypeStruct((s, SC_W), F32)
    return _call(kern, name="sconv_bwd", grid=(SC_W // LANE,),
                          in_specs=[col(B_COL), col(C_COL), col(U_COL), wspec, col(FOX_W // LANE)],
                          out_specs=[out, out, out, wspec],
                          out_shape=[act, act, act, jax.ShapeDtypeStruct((SC_K, SC_W), F32)],
                          compiler_params=_params(1))(proj, proj, proj, convw, dcat)


def _fox_layer_fwd(x, gpre, gpost, wall, bfb, convw, wout, after=None):
    s = x.shape[0]
    h = _norm_fwd(x, gpre, after)
    proj = _mm_nt_cols("fox_proj", h, wall, AB_PAD // 5)
    frow = proj[:, 3 * FOX_W + 3 * SC_W:3 * FOX_W + 3 * SC_W + FOX_H].T
    cumr = _fox_cum(frow, bfb)
    qa, ka = _fox_prep(proj, jnp.broadcast_to(cumr[:, :, None], (FOX_H, s, LANE)))
    o, lse = _fox_fwd_call(proj, qa, ka)
    yb = _sconv_fwd(proj, convw)
    cat = jnp.concatenate([o.astype(BF16), yb], axis=1)
    y = _mm_nn("fox_out", cat, wout)
    return _norm_res(x, y, gpost), (x, h, proj, frow, qa, ka, o, lse, cat, y)


def _fox_layer_bwd(dxo, saved, gpre, gpost, wall, bfb, convw, wout, after=None):
    x, h, proj, frow, qa, ka, o, lse, cat, y = saved
    s = x.shape[0]
    dy, dgpost = _norm_bwd(y, dxo, gpost, None, BF16, after)
    dcat = _mm_nt("fox_dcat", dy, wout)
    dwout = _mm_tn("fox_dwout", cat, dy)
    db, dc, du, dconvw = _sconv_bwd(proj, convw, dcat)
    dq, dk, dv, dck, dcq = _fox_bwd_call(proj, o, lse, dcat, qa, ka)
    dfrow, dbf = _fox_dlogf(dcq[:, :, 0, :].reshape(FOX_H, s), dck.reshape(FOX_H, s), frow, bfb)
    dfcol = jnp.pad(dfrow.T, ((0, 0), (0, LANE - FOX_H)))
    dproj = jnp.concatenate([dq, dk, dv, db, dc, du, dfcol], axis=1).astype(BF16)
    dwall = _mm_tn_rows("fox_dwall", dproj, h, AB_PAD // 5)
    dh = _mm_nn("fox_dh", dproj, wall, vmem=VMEM_BIG)
    dx, dgpre = _norm_bwd(x, dh, gpre, dxo, F32)
    return dx, dgpre, dgpost, dwall, dbf[:, 0], dconvw, dwout


def _ab_pack(wt):
    nf = 3 * FOX_W
    return jnp.concatenate([wt[:nf], wt[nf + FOX_H:], wt[nf:nf + FOX_H],
                            jnp.zeros((AB_PAD - AB_IN, wt.shape[1]), wt.dtype)], axis=0)


def _ab_unpack(wt):
    nf = 3 * FOX_W
    nbcu = 3 * SC_W
    return jnp.concatenate([wt[:nf], wt[nf + nbcu:nf + nbcu + FOX_H], wt[nf:nf + nbcu]], axis=0)


NCH = DM // LANE
CH_PER_BLK = LRU_BW // LANE


def _chunk_spec(s, lead=0):
    return BS((None, s, LANE), lambda ch: (lead + ch // CH_PER_BLK, 0, ch % CH_PER_BLK))


def _vec_chunk(rows):
    return BS((rows, LANE), lambda ch: (0, ch))


def _neg_expm1(x):
    series = -x * (1.0 + x * (1 / 2) * (1.0 + x * (1 / 3) * (1.0 + x * (1 / 4) * (1.0 + x * (1 / 5) * (
        1.0 + x * (1 / 6) * (1.0 + x * (1 / 7)))))))
    return jnp.where(x > -0.25, series, 1.0 - jnp.exp(x))


def _softplus(z):
    return jnp.maximum(z, 0.0) + jnp.log1p(jnp.exp(-jnp.abs(z)))


GELU_C = math.sqrt(2.0 / math.pi)
GELU_A = 0.044715


def _gelu(x):
    return 0.5 * x * (1.0 + jnp.tanh(GELU_C * (x + GELU_A * x * x * x)))


def _gelu_grad(x):
    t = jnp.tanh(GELU_C * (x + GELU_A * x * x * x))
    return 0.5 * (1.0 + t) + 0.5 * x * (1.0 - t * t) * GELU_C * (1.0 + 3.0 * GELU_A * x * x)


def _lru_conv_fwd(gu, convw, convb):
    s = gu.shape[1]

    def kern(x_ref, w_ref, b_ref, u_ref):
        row = lax.broadcasted_iota(jnp.int32, (s, LANE), 0)
        xv = x_ref[...]
        u_ref[...] = (b_ref[...] + w_ref[3:4, :] * xv + w_ref[2:3, :] * _shift_down(xv, 1, row)
                      + w_ref[1:2, :] * _shift_down(xv, 2, row) + w_ref[0:1, :] * _shift_down(xv, 3, row))

    return _call(kern, name="lru_conv_fwd", grid=(NCH,),
                          in_specs=[_chunk_spec(s, LRU_NB), _vec_chunk(RG_K), _vec_chunk(1)], out_specs=_chunk_spec(s),
                          out_shape=jax.ShapeDtypeStruct((LRU_NB, s, LRU_BW), F32), compiler_params=_params(1))(gu, convw, convb)


def _lru_conv_bwd(dud, dug, gu, convw):
    s = gu.shape[1]

    def kern(d1_ref, d2_ref, x_ref, w_ref, dx_ref, dw_ref, db_ref):
        row = lax.broadcasted_iota(jnp.int32, (s, LANE), 0)
        du = d1_ref[...] + d2_ref[...]
        xv = x_ref[...]
        dx_ref[...] = (w_ref[3:4, :] * du + w_ref[2:3, :] * _shift_up(du, 1, row, s) + w_ref[1:2, :] * _shift_up(du, 2, row, s)
                       + w_ref[0:1, :] * _shift_up(du, 3, row, s)).astype(BF16)
        dw_ref[3:4, :] = jnp.sum(du * xv, axis=0, keepdims=True)
        for k in range(1, RG_K):
            dw_ref[3 - k:4 - k, :] = jnp.sum(du * _shift_down(xv, k, row), axis=0, keepdims=True)
        db_ref[...] = jnp.sum(du, axis=0, keepdims=True)

    return _call(kern, name="lru_conv_bwd", grid=(NCH,),
                          in_specs=[_chunk_spec(s), _chunk_spec(s), _chunk_spec(s, LRU_NB), _vec_chunk(RG_K)],
                          out_specs=[_chunk_spec(s), _vec_chunk(RG_K), _vec_chunk(1)],
                          out_shape=[jax.ShapeDtypeStruct((LRU_NB, s, LRU_BW), BF16),
                                     jax.ShapeDtypeStruct((RG_K, DM), F32), jax.ShapeDtypeStruct((1, DM), F32)],
                          compiler_params=_params(1))(dud, dug, gu, convw)


def _lru_gates(z_ref, bai_ref, lam_ref, uv):
    r = jax.nn.sigmoid(z_ref[0] + bai_ref[0:1, :])
    ig = jax.nn.sigmoid(z_ref[1] + bai_ref[1:2, :])
    sp = _softplus(-lam_ref[...])
    la = -RG_C * r * sp
    a = jnp.exp(la)
    sq = jnp.sqrt(_neg_expm1(2.0 * la))
    return r, ig, sp, a, sq


def _scan_steps(n):
    d, out = 1, []
    while d < n:
        out.append(d)
        d *= 2
    return out


def _lru_scan_fwd(z, bai, lam, u, gu):
    s = u.shape[1]
    zspec = BS((2, None, s, LANE), lambda ch: (0, ch // CH_PER_BLK, 0, ch % CH_PER_BLK))

    def kern(z_ref, bai_ref, lam_ref, u_ref, g_ref, hs_ref, y_ref):
        row = lax.broadcasted_iota(jnp.int32, (s, LANE), 0)
        uv = u_ref[...]
        _, ig, _, a, sq = _lru_gates(z_ref, bai_ref, lam_ref, uv)
        b = sq * (ig * uv)
        for d in _scan_steps(s):
            a_sh = jnp.where(row >= d, pltpu.roll(a, d, 0), 1.0)
            b = a * _shift_down(b, d, row) + b
            a = a * a_sh
        hs_ref[...] = b
        y_ref[...] = (_gelu(g_ref[...]) * b).astype(BF16)

    return _call(kern, name="lru_scan_fwd", grid=(NCH,),
                          in_specs=[zspec, _vec_chunk(2), _vec_chunk(1), _chunk_spec(s), _chunk_spec(s)],
                          out_specs=[_chunk_spec(s), BS((s, LANE), lambda ch: (0, ch))],
                          out_shape=[jax.ShapeDtypeStruct((LRU_NB, s, LRU_BW), F32), jax.ShapeDtypeStruct((s, DM), BF16)],
                          compiler_params=_params(1, VMEM_BIG))(z, bai, lam, u, gu)


def _lru_scan_bwd(dyp, z, bai, lam, u, gu, hs):
    s = u.shape[1]
    zspec = BS((2, None, s, LANE), lambda ch: (0, ch // CH_PER_BLK, 0, ch % CH_PER_BLK))

    def kern(dy_ref, z_ref, bai_ref, lam_ref, u_ref, g_ref, hs_ref, dg_ref, dz_ref, du_ref, dbai_ref, dlam_ref):
        row = lax.broadcasted_iota(jnp.int32, (s, LANE), 0)
        uv, gv, hv, dyv = u_ref[...], g_ref[...], hs_ref[...], dy_ref[...]
        r, ig, sp, a, sq = _lru_gates(z_ref, bai_ref, lam_ref, uv)
        dg_ref[...] = (dyv * hv * _gelu_grad(gv)).astype(BF16)
        g = dyv * _gelu(gv)
        an = _shift_up(a, 1, row, s)
        for d in _scan_steps(s):
            an_sh = jnp.where(row < s - d, pltpu.roll(an, s - d, 0), 1.0)
            g = an * _shift_up(g, d, row, s) + g
            an = an * an_sh
        da = g * _shift_down(hv, 1, row)
        dsq = g * (ig * uv)
        di = g * sq * uv
        du_ref[...] = g * sq * ig
        dla = da * a - dsq * (a * a / sq)
        dzr = dla * (-RG_C * sp) * r * (1.0 - r)
        dzi = di * ig * (1.0 - ig)
        dz_ref[0] = dzr.astype(BF16)
        dz_ref[1] = dzi.astype(BF16)
        dbai_ref[0:1, :] = jnp.sum(dzr, axis=0, keepdims=True)
        dbai_ref[1:2, :] = jnp.sum(dzi, axis=0, keepdims=True)
        dlam_ref[...] = jnp.sum(dla * r, axis=0, keepdims=True) * (RG_C * jax.nn.sigmoid(-lam_ref[...]))

    return _call(
        kern, name="lru_scan_bwd", grid=(NCH,),
        in_specs=[BS((s, LANE), lambda ch: (0, ch)), zspec, _vec_chunk(2), _vec_chunk(1), _chunk_spec(s), _chunk_spec(s),
                  _chunk_spec(s)],
        out_specs=[_chunk_spec(s), zspec, _chunk_spec(s), _vec_chunk(2), _vec_chunk(1)],
        out_shape=[jax.ShapeDtypeStruct((LRU_NB, s, LRU_BW), BF16), jax.ShapeDtypeStruct((2, LRU_NB, s, LRU_BW), BF16),
                   jax.ShapeDtypeStruct((LRU_NB, s, LRU_BW), F32), jax.ShapeDtypeStruct((2, DM), F32),
                   jax.ShapeDtypeStruct((1, DM), F32)],
        compiler_params=_params(1, VMEM_BIG))(dyp, z, bai, lam, u, gu, hs)


def _lru_layer_fwd(x, gpre, gpost, win, convw, convb, wai, bai, lam, wout, after=None):
    s = x.shape[0]
    tm = _tile(s, MM_TM)
    h = _norm_fwd(x, gpre, after)
    gu = _bmm_nn("lru_in", h, win)
    u = _lru_conv_fwd(gu, convw, convb)
    z = _mm("lru_gate", u, wai, grid=(2, LRU_NB, s // tm, 1),
            a_spec=BS((None, tm, LRU_BW), lambda k, n, i, r: (n, i, 0)),
            b_spec=BS((None, None, LRU_BW, LRU_BW), lambda k, n, i, r: (k, n, 0, 0)),
            o_spec=BS((None, None, tm, LRU_BW), lambda k, n, i, r: (k, n, i, 0)),
            out_shape=(2, LRU_NB, s, LRU_BW), dn=NN)
    hs, yp = _lru_scan_fwd(z, bai, lam, u, gu)
    y = _mm_nn("lru_out", yp, wout)
    return _norm_res(x, y, gpost), (x, h, gu, u, z, hs, yp, y)


def _lru_layer_bwd(dxo, saved, gpre, gpost, win, convw, convb, wai, bai, lam, wout, after=None):
    x, h, gu, u, z, hs, yp, y = saved
    s = x.shape[0]
    tm = _tile(s, MM_TM)
    dy, dgpost = _norm_bwd(y, dxo, gpost, None, BF16, after)
    dyp = _mm_nt("lru_dyp", dy, wout)
    dwout = _mm_tn("lru_dwout", yp, dy)
    dgate, dz, dud, dbai, dlam = _lru_scan_bwd(dyp, z, bai, lam, u, gu, hs)
    dwai = _mm("lru_dwai", u, dz, grid=(2, LRU_NB, s // tm),
               a_spec=BS((None, tm, LRU_BW), lambda k, n, r: (n, r, 0)),
               b_spec=BS((None, None, tm, LRU_BW), lambda k, n, r: (k, n, r, 0)),
               o_spec=BS((None, None, LRU_BW, LRU_BW), lambda k, n, r: (k, n, 0, 0)),
               out_shape=(2, LRU_NB, LRU_BW, LRU_BW), dn=TN)
    dug = _mm("lru_dug", dz, wai, grid=(LRU_NB, s // tm, 2),
              a_spec=BS((None, None, tm, LRU_BW), lambda n, i, k: (k, n, i, 0)),
              b_spec=BS((None, None, LRU_BW, LRU_BW), lambda n, i, k: (k, n, 0, 0)),
              o_spec=BS((None, tm, LRU_BW), lambda n, i, k: (n, i, 0)),
              out_shape=(LRU_NB, s, LRU_BW), dn=NT)
    duraw, dconvw, dconvb = _lru_conv_bwd(dud, dug, gu, convw)
    dgu = jnp.concatenate([dgate, duraw], axis=0)
    dwin = _bmm_tn("lru_dwin", h, dgu)
    dh = _bmm_nt_sum("lru_dh", dgu, win)
    dx, dgpre = _norm_bwd(x, dh, gpre, dxo, F32)
    return dx, dgpre, dgpost, dwin, dconvw, dconvb, dwai, dbai, dlam, dwout


CHIP_FLIPS = ((1, 0), (0, 1), (1, 1))


def _place():
    return lax.axis_index("x"), lax.axis_index("y"), lax.axis_index("c")


def _flip(v, f):
    return 1 - v if f else v


def _comm_params():
    return pltpu.CompilerParams(vmem_limit_bytes=VMEM_LIMIT)


def _small_gather(v):
    def body(v_ref, o_ref, send_sems, recv_sems, local_sem):
        x, y, c = _place()
        mine = 4 * x + 2 * y + c
        local = pltpu.make_async_copy(v_ref, o_ref.at[mine], local_sem)
        local.start()
        sends = []
        for k in range(1, NDEV):
            fx, fy, fc = (k >> 2) & 1, (k >> 1) & 1, k & 1
            sends.append(pltpu.make_async_remote_copy(
                src_ref=v_ref, dst_ref=o_ref.at[mine], send_sem=send_sems.at[k - 1], recv_sem=recv_sems.at[k - 1],
                device_id=(_flip(x, fx), _flip(y, fy), _flip(c, fc)), device_id_type=MESH))
        for cp in sends:
            cp.start()
        for k in range(1, NDEV):
            fx, fy, fc = (k >> 2) & 1, (k >> 1) & 1, k & 1
            src = 4 * _flip(x, fx) + 2 * _flip(y, fy) + _flip(c, fc)
            pltpu.make_async_remote_copy(src_ref=v_ref, dst_ref=o_ref.at[src], send_sem=send_sems.at[k - 1],
                                         recv_sem=recv_sems.at[k - 1], device_id=(x, y, c), device_id_type=MESH).wait_recv()
        for cp in sends:
            cp.wait_send()
        local.wait()

    return pl.pallas_call(body, name="small_gather", in_specs=[ANY], out_specs=ANY,
                          out_shape=jax.ShapeDtypeStruct((NDEV,) + v.shape, v.dtype),
                          scratch_shapes=[pltpu.SemaphoreType.DMA((NDEV - 1,)), pltpu.SemaphoreType.DMA((NDEV - 1,)),
                                          pltpu.SemaphoreType.DMA],
                          compiler_params=_comm_params())(v)


REL_CHIPS = ((0, 0),) + CHIP_FLIPS


def _rs_d2d(g5s, after=None):
    n = len(g5s)
    extra = () if after is None else (after,)

    def body(*refs):
        ins, gots = refs[:n], refs[n + len(extra):2 * n + len(extra)]
        send_sems, recv_sems = refs[2 * n + len(extra):]
        x, y, c = _place()
        copies = []
        for t in range(n):
            for f, (fx, fy) in enumerate(REL_CHIPS):
                copies.append(pltpu.make_async_remote_copy(
                    src_ref=ins[t].at[_flip(x, fx), _flip(y, fy), 1 - c], dst_ref=gots[t].at[f],
                    send_sem=send_sems.at[4 * t + f], recv_sem=recv_sems.at[4 * t + f], device_id=(x, y, 1 - c),
                    device_id_type=MESH))
        for cp in copies:
            cp.start()
        for cp in copies:
            cp.wait()

    out = [jax.ShapeDtypeStruct((4,) + g.shape[3:], F32) for g in g5s]
    return pl.pallas_call(body, name="rs_d2d", in_specs=[ANY] * (n + len(extra)), out_specs=[ANY] * n, out_shape=out,
                          scratch_shapes=[pltpu.SemaphoreType.DMA((4 * n,)), pltpu.SemaphoreType.DMA((4 * n,))],
                          compiler_params=_comm_params())(*g5s, *extra)


HBM = pl.BlockSpec(memory_space=pltpu.HBM)
SEM = pl.BlockSpec(memory_space=pltpu.SEMAPHORE)
EFFECT = pltpu.SideEffectType.DATAFLOW_SIDE_EFFECTING


def _in_hbm(a):
    return pltpu.with_memory_space_constraint(a, pltpu.HBM)


def _rs_ici_copies(ins, lands, send_sems, recv_sems):
    x, y, c = _place()
    return [pltpu.make_async_remote_copy(
        src_ref=ins[t].at[f], dst_ref=lands[t].at[f], send_sem=send_sems.at[3 * t + f], recv_sem=recv_sems.at[3 * t + f],
        device_id=(_flip(x, fx), _flip(y, fy), c), device_id_type=MESH)
        for t in range(len(ins)) for f, (fx, fy) in enumerate(CHIP_FLIPS)]


def _rs_ici_start(parts, name):
    n = len(parts)

    def body(*refs):
        ins, lands = refs[:n], refs[n:2 * n]
        send_sems, recv_sems = refs[2 * n], refs[2 * n + 1]
        token = refs[-1]
        for cp in _rs_ici_copies(ins, lands, send_sems, recv_sems):
            cp.start()
        token[...] = jnp.zeros(token.shape, token.dtype)

    thru = [pltpu.HBM(p.shape, p.dtype) for p in parts]
    res = pl.pallas_call(
        body, name=name, in_specs=[HBM] * (2 * n),
        out_shape=(pltpu.SemaphoreType.DMA((3 * n,)), pltpu.SemaphoreType.DMA((3 * n,)), *thru, *thru,
                   jax.ShapeDtypeStruct((8, LANE), F32)),
        out_specs=(SEM, SEM, *([HBM] * (2 * n)), pl.BlockSpec(memory_space=pltpu.VMEM)),
        input_output_aliases={i: 2 + i for i in range(2 * n)},
        compiler_params=pltpu.CompilerParams(has_side_effects=EFFECT),
    )(*[_in_hbm(p) for p in parts], *[_in_hbm(lax.empty(p.shape, p.dtype)) for p in parts])
    return res[:-1], res[-1]


def _rs_ici_wait(state, after, name):
    n = (len(state) - 2) // 2

    def body(*refs):
        send_sems, recv_sems = refs[0], refs[1]
        ins, lands = refs[2:2 + n], refs[2 + n:2 + 2 * n]
        for cp in _rs_ici_copies(ins, lands, send_sems, recv_sems):
            cp.wait_send()
            cp.wait_recv()

    thru = [pltpu.HBM(s.shape, s.dtype) for s in state[2:]]
    res = pl.pallas_call(
        body, name=name, in_specs=[SEM, SEM] + [HBM] * (2 * n) + [ANY], out_shape=tuple(thru),
        out_specs=tuple([HBM] * (2 * n)), input_output_aliases={2 + i: i for i in range(2 * n)},
        compiler_params=pltpu.CompilerParams(has_side_effects=EFFECT),
    )(*state, after)
    return list(res[n:])


def _ag_copies(shards, lands, send_sems, recv_sems):
    x, y, c = _place()
    mine = 4 * x + 2 * y + c
    peers = [(x, y, 1 - c)] + [(_flip(x, fx), _flip(y, fy), c) for fx, fy in CHIP_FLIPS]
    return [pltpu.make_async_remote_copy(
        src_ref=shards[t], dst_ref=lands[t].at[mine], send_sem=send_sems.at[4 * t + k], recv_sem=recv_sems.at[4 * t + k],
        device_id=peer, device_id_type=MESH) for t in range(len(shards)) for k, peer in enumerate(peers)]


def _ag_start(shards, after, name):
    n = len(shards)

    def body(*refs):
        ins, lands = refs[:n], refs[n:2 * n]
        send_sems, recv_sems = refs[2 * n + 1], refs[2 * n + 2]
        token = refs[-1]
        for cp in _ag_copies(ins, lands, send_sems, recv_sems):
            cp.start()
        token[...] = jnp.zeros(token.shape, token.dtype)

    thru = [pltpu.HBM(s.shape, s.dtype) for s in shards]
    land = [pltpu.HBM((NDEV,) + s.shape, s.dtype) for s in shards]
    res = pl.pallas_call(
        body, name=name, in_specs=[HBM] * (2 * n) + [ANY],
        out_shape=(pltpu.SemaphoreType.DMA((4 * n,)), pltpu.SemaphoreType.DMA((4 * n,)), *thru, *land,
                   jax.ShapeDtypeStruct((8, LANE), F32)),
        out_specs=(SEM, SEM, *([HBM] * (2 * n)), pl.BlockSpec(memory_space=pltpu.VMEM)),
        input_output_aliases={i: 2 + i for i in range(2 * n)},
        compiler_params=pltpu.CompilerParams(has_side_effects=EFFECT),
    )(*[_in_hbm(s) for s in shards], *[_in_hbm(lax.empty((NDEV,) + s.shape, s.dtype)) for s in shards], after)
    return res[:-1], res[-1]


def _ag_wait(state, after, name):
    n = (len(state) - 2) // 2

    def body(*refs):
        send_sems, recv_sems = refs[0], refs[1]
        ins, lands = refs[2:2 + n], refs[2 + n:2 + 2 * n]
        for cp in _ag_copies(ins, lands, send_sems, recv_sems):
            cp.wait_send()
            cp.wait_recv()

    thru = [pltpu.HBM(s.shape, s.dtype) for s in state[2:]]
    res = pl.pallas_call(
        body, name=name, in_specs=[SEM, SEM] + [HBM] * (2 * n) + [ANY], out_shape=tuple(thru),
        out_specs=tuple([HBM] * (2 * n)), input_output_aliases={2 + i: i for i in range(2 * n)},
        compiler_params=pltpu.CompilerParams(has_side_effects=EFFECT),
    )(*state, after)
    return list(res[:n]), list(res[n:])


def _ag_finish(shards, lands):
    n = len(shards)

    def body(*refs):
        ins, outs, stage = refs[:n], refs[2 * n:3 * n], refs[3 * n:4 * n]
        send_sems, recv_sems, local_sems = refs[4 * n:]
        x, y, c = _place()
        chips = [(_flip(x, fx), _flip(y, fy)) for fx, fy in CHIP_FLIPS]

        def passing(t, j, core, to):
            blk = outs[t].at[4 * chips[j][0] + 2 * chips[j][1] + core]
            return pltpu.make_async_remote_copy(src_ref=blk, dst_ref=blk, send_sem=send_sems.at[3 * t + j],
                                                recv_sem=recv_sems.at[3 * t + j], device_id=to, device_id_type=MESH)

        sends = [passing(t, j, c, (x, y, 1 - c)) for t in range(n) for j in range(3)]
        for cp in sends:
            cp.start()
        load = [pltpu.make_async_copy(ins[t], stage[t], local_sems.at[t]) for t in range(n)]
        mine = [pltpu.make_async_copy(stage[t], outs[t].at[4 * x + 2 * y + c], local_sems.at[t]) for t in range(n)]
        for cp in load:
            cp.start()
        for t in range(n):
            load[t].wait()
            mine[t].start()
        for t in range(n):
            for j in range(3):
                passing(t, j, 1 - c, (x, y, c)).wait_recv()
        for cp in sends:
            cp.wait_send()
        for cp in mine:
            cp.wait()

    return pl.pallas_call(
        body, name="ag_finish", in_specs=[ANY] * (2 * n), out_specs=[ANY] * n,
        out_shape=[jax.ShapeDtypeStruct(l.shape, l.dtype) for l in lands],
        input_output_aliases={n + i: i for i in range(n)},
        scratch_shapes=[pltpu.VMEM(s.shape, s.dtype) for s in shards]
        + [pltpu.SemaphoreType.DMA((3 * n,)), pltpu.SemaphoreType.DMA((3 * n,)), pltpu.SemaphoreType.DMA((n,))],
        compiler_params=_comm_params())(*shards, *lands)


def _row_tile(rows, largest=256):
    for t in (1024, 512, 256, 128, 64, 32, 16, 8):
        if t > largest:
            continue
        if rows % t == 0:
            return t
    return rows


def _rs_chip_sum(pos, g5, got):
    a, b = g5.shape[3:]
    ta = _row_tile(a, 1024)

    def kern(pos_ref, o_ref, g_ref, p_ref):
        p_ref[...] = (o_ref[...] + g_ref[...]).astype(BF16)

    def mine(f, i, pos_ref):
        return (pos_ref[0] ^ ((f + 1) & 1), pos_ref[1] ^ ((f + 1) >> 1), pos_ref[2], i, 0)

    spec = pltpu.PrefetchScalarGridSpec(
        num_scalar_prefetch=1, grid=(3, a // ta),
        in_specs=[BS((None, None, None, ta, b), mine), BS((None, ta, b), lambda f, i, pos_ref: (f + 1, i, 0))],
        out_specs=BS((None, ta, b), lambda f, i, pos_ref: (f, i, 0)))
    return _call(kern, name="rs_chip_sum", grid_spec=spec, out_shape=jax.ShapeDtypeStruct((3, a, b), BF16),
                          compiler_params=_params(2))(pos, g5, got)


def _rs_final_sum(pos, g5, got, recv):
    a, b = g5.shape[3:]
    ta = _row_tile(a, 1024)

    def kern(pos_ref, o_ref, g_ref, r_ref, s_ref):
        acc = o_ref[...] + g_ref[...]
        for f in range(3):
            acc = acc + r_ref[f].astype(F32)
        s_ref[...] = acc

    spec = pltpu.PrefetchScalarGridSpec(
        num_scalar_prefetch=1, grid=(a // ta,),
        in_specs=[BS((None, None, None, ta, b), lambda i, pos_ref: (pos_ref[0], pos_ref[1], pos_ref[2], i, 0)),
                  BS((None, ta, b), lambda i, pos_ref: (0, i, 0)), BS((3, ta, b), lambda i, pos_ref: (0, i, 0))],
        out_specs=BS((ta, b), lambda i, pos_ref: (i, 0)))
    return _call(kern, name="rs_final_sum", grid_spec=spec, out_shape=jax.ShapeDtypeStruct((a, b), F32),
                          compiler_params=_params(1))(pos, g5, got, recv)


def _rs_d2d_copies(ins, lands, send_sems, recv_sems):
    x, y, c = _place()
    return [pltpu.make_async_remote_copy(
        src_ref=ins[t].at[_flip(x, fx), _flip(y, fy), 1 - c], dst_ref=lands[t].at[f], send_sem=send_sems.at[4 * t + f],
        recv_sem=recv_sems.at[4 * t + f], device_id=(x, y, 1 - c), device_id_type=MESH)
        for t in range(len(ins)) for f, (fx, fy) in enumerate(REL_CHIPS)]


def _rs_d2d_start(g5s, name):
    n = len(g5s)

    def body(*refs):
        ins, lands = refs[:n], refs[n:2 * n]
        for cp in _rs_d2d_copies(ins, lands, refs[2 * n], refs[2 * n + 1]):
            cp.start()
        refs[-1][...] = jnp.zeros(refs[-1].shape, F32)

    thru = [pltpu.HBM(g.shape, g.dtype) for g in g5s]
    land = [pltpu.HBM((4,) + g.shape[3:], F32) for g in g5s]
    res = pl.pallas_call(
        body, name=name, in_specs=[HBM] * (2 * n),
        out_shape=(pltpu.SemaphoreType.DMA((4 * n,)), pltpu.SemaphoreType.DMA((4 * n,)), *thru, *land,
                   jax.ShapeDtypeStruct((8, LANE), F32)),
        out_specs=(SEM, SEM, *([HBM] * (2 * n)), pl.BlockSpec(memory_space=pltpu.VMEM)),
        input_output_aliases={i: 2 + i for i in range(2 * n)},
        compiler_params=pltpu.CompilerParams(has_side_effects=EFFECT),
    )(*[_in_hbm(g) for g in g5s], *[_in_hbm(lax.empty((4,) + g.shape[3:], F32)) for g in g5s])
    return res[:-1], res[-1]


def _rs_d2d_wait(state, after, name):
    n = (len(state) - 2) // 2

    def body(*refs):
        ins, lands = refs[2:2 + n], refs[2 + n:2 + 2 * n]
        for cp in _rs_d2d_copies(ins, lands, refs[0], refs[1]):
            cp.wait_send()
            cp.wait_recv()

    thru = [pltpu.HBM(s.shape, s.dtype) for s in state[2:]]
    res = pl.pallas_call(
        body, name=name, in_specs=[SEM, SEM] + [HBM] * (2 * n) + [ANY], out_shape=tuple(thru),
        out_specs=tuple([HBM] * (2 * n)), input_output_aliases={2 + i: i for i in range(2 * n)},
        compiler_params=pltpu.CompilerParams(has_side_effects=EFFECT),
    )(*state, after)
    return list(res[:n]), list(res[n:])


def _as_g5(grads):
    return [g.reshape((2, 2, 2) + g.shape[1:]) for g in grads]


def _rs_mid(g5s, gots, pos, tag):
    parts = [_rs_chip_sum(pos, g, got) for g, got in zip(g5s, gots)]
    state, token = _rs_ici_start(parts, "rs_ici_start_" + tag)
    return (g5s, gots, state, tag), token


def _rs_begin(grads, pos, tag, after=None):
    g5s = _as_g5(grads)
    return _rs_mid(g5s, _rs_d2d(g5s, after), pos, tag)


def _rs_end(pending, after, pos):
    g5s, gots, state, tag = pending
    recvs = _rs_ici_wait(state, after, "rs_ici_wait_" + tag)
    return [_rs_final_sum(pos, g, got, r) for g, got, r in zip(g5s, gots, recvs)]


def _sum_devices(v):
    _, r, _ = v.shape

    def kern(v_ref, o_ref):
        acc = v_ref[0]
        for d in range(1, NDEV):
            acc = acc + v_ref[d]
        o_ref[...] = acc

    return _call(kern, name="sum_devices", out_shape=jax.ShapeDtypeStruct((r, LANE), F32),
                          compiler_params=_comm_params())(v)


def _loss_head(xf, target):
    s = xf.shape[0]
    tm = _tile(s, 512)

    def kern(x_ref, t_ref, dx_ref, l_ref):
        err = x_ref[...] - t_ref[...]
        dx_ref[...] = err * (1.0 / DM)
        part = jnp.broadcast_to(0.5 * jnp.sum(jnp.mean(err * err, axis=-1, keepdims=True), axis=0, keepdims=True), (8, LANE))

        @pl.when(pl.program_id(0) == 0)
        def _():
            l_ref[...] = part

        @pl.when(pl.program_id(0) > 0)
        def _():
            l_ref[...] += part

    row = BS((tm, DM), lambda i: (i, 0))
    return _call(kern, name="loss_head", grid=(s // tm,), in_specs=[row, row],
                          out_specs=[row, BS((8, LANE), lambda i: (0, 0))],
                          out_shape=[jax.ShapeDtypeStruct((s, DM), F32), jax.ShapeDtypeStruct((8, LANE), F32)],
                          compiler_params=_params(1))(xf, target)


def _adamw(w, g, m, v, after=None):
    rows, cols = w.shape
    tr = _row_tile(rows)
    extra = () if after is None else (after,)

    def kern(w_ref, g_ref, m_ref, v_ref, *rest):
        d_ref, nm_ref, nv_ref = rest[-3:]
        gv = g_ref[...]
        nm = ADAM_B1 * m_ref[...] + (1.0 - ADAM_B1) * gv
        nv = ADAM_B2 * v_ref[...] + (1.0 - ADAM_B2) * (gv * gv)
        m_hat = nm / (1.0 - ADAM_B1 ** ADAM_STEP)
        v_hat = nv / (1.0 - ADAM_B2 ** ADAM_STEP)
        d_ref[...] = -ADAM_LR * (m_hat / (jnp.sqrt(v_hat) + ADAM_EPS) + ADAM_WD * w_ref[...])
        nm_ref[...] = nm
        nv_ref[...] = nv

    blk = BS((tr, cols), lambda i: (i, 0))
    shp = jax.ShapeDtypeStruct((rows, cols), F32)
    return _call(kern, name="adamw", grid=(rows // tr,), in_specs=[blk] * 4 + [ANY] * len(extra),
                          out_specs=[blk] * 3, out_shape=[shp] * 3, compiler_params=_params(1))(w, g, m, v, *extra)


def _adamw_nd(w, g, m, v, after=None):
    shape = w.shape
    two = (math.prod(shape[:-1]), shape[-1])
    return tuple(o.reshape(shape)
                 for o in _adamw(w.reshape(two), g.reshape(two), m.reshape(two), v.reshape(two), after))


def _pack_small(parts):
    flat = jnp.concatenate([p.reshape(-1) for p in parts])
    pad = (-flat.shape[0]) % (8 * LANE)
    return jnp.pad(flat, (0, pad)).reshape(-1, LANE)


def _unpack_small(packed, shapes, lead=()):
    flat = packed.reshape(lead + (-1,))
    out, off = [], 0
    for shp in shapes:
        n = math.prod(shp)
        out.append(flat[..., off:off + n].reshape(lead + tuple(shp)))
        off += n
    return out


WEIGHT_NAMES = ('g_mix_pre', 'g_mix_post', 'g_cross_pre', 'g_mem', 'g_cross_post', 'g_ffn_pre', 'g_ffn_post', 'w_xq',
                'w_xkv', 'w_xo', 'w_ffn_gu', 'w_ffn_down', 'ab_w_in', 'ab_b_f', 'ab_conv_w', 'ab_w_out', 'c_w_in',
                'c_conv_w', 'c_conv_b', 'c_w_a', 'c_b_a', 'c_w_i', 'c_b_i', 'c_lam', 'c_w_out')
BIG = ('w_xq', 'w_xkv', 'w_xo', 'w_ffn_gu', 'w_ffn_down', 'ab_w_in', 'ab_w_out', 'c_w_in', 'c_w_a', 'c_w_i', 'c_w_out')
SMALL_SHARDED = ('ab_conv_w', 'c_conv_w', 'c_conv_b', 'c_b_a', 'c_b_i', 'c_lam')
REPLICATED = ('g_mix_pre', 'g_mix_post', 'g_cross_pre', 'g_mem', 'g_cross_post', 'g_ffn_pre', 'g_ffn_post', 'ab_b_f')


def _small_full(name, gathered):
    nd = gathered.ndim
    return jnp.moveaxis(gathered, 0, nd - 2).reshape(gathered.shape[1:-1] + (NDEV * gathered.shape[-1],))


def _small_shard(full, dev):
    c = full.shape[-1] // NDEV
    return lax.dynamic_slice_in_dim(full, dev * c, c, axis=full.ndim - 1)


def kernel(x, mem, g_mix_pre, g_mix_post, g_cross_pre, g_mem, g_cross_post, g_ffn_pre, g_ffn_post, w_xq, w_xkv, w_xo, w_ffn_gu, w_ffn_down, ab_w_in, ab_b_f, ab_conv_w, ab_w_out, c_w_in, c_conv_w, c_conv_b, c_w_a, c_b_a, c_w_i, c_b_i, c_lam, c_w_out, loss_target, m_g_mix_pre, m_g_mix_post, m_g_cross_pre, m_g_mem, m_g_cross_post, m_g_ffn_pre, m_g_ffn_post, m_w_xq, m_w_xkv, m_w_xo, m_w_ffn_gu, m_w_ffn_down, m_ab_w_in, m_ab_b_f, m_ab_conv_w, m_ab_w_out, m_c_w_in, m_c_conv_w, m_c_conv_b, m_c_w_a, m_c_b_a, m_c_w_i, m_c_b_i, m_c_lam, m_c_w_out, v_g_mix_pre, v_g_mix_post, v_g_cross_pre, v_g_mem, v_g_cross_post, v_g_ffn_pre, v_g_ffn_post, v_w_xq, v_w_xkv, v_w_xo, v_w_ffn_gu, v_w_ffn_down, v_ab_w_in, v_ab_b_f, v_ab_conv_w, v_ab_w_out, v_c_w_in, v_c_conv_w, v_c_conv_b, v_c_w_a, v_c_b_a, v_c_w_i, v_c_b_i, v_c_lam, v_c_w_out):
    args = locals()
    w = {n: args[n] for n in WEIGHT_NAMES}
    mom = {n: args["m_" + n] for n in WEIGHT_NAMES}
    var = {n: args["v_" + n] for n in WEIGHT_NAMES}
    for t in (w, mom, var):
        t['w_ffn_gu'] = t['w_ffn_gu'].transpose(0, 2, 1)
    ab_t = [t['ab_w_in'].transpose(2, 0, 1) for t in (w, mom, var)]
    pos = jnp.stack([lax.axis_index("x"), lax.axis_index("y"), lax.axis_index("c")]).astype(jnp.int32)
    dev = 4 * pos[0] + 2 * pos[1] + pos[2]
    xs, mems, target = x[0], mem[0], loss_target[0]
    n_even, n_odd = (DEPTH + 1) // 2, DEPTH // 2

    small_shapes = [w[n].shape for n in SMALL_SHARDED]
    small_w_all = _small_gather(_pack_small([w[n] for n in SMALL_SHARDED]))
    gathered_small = _unpack_small(small_w_all, small_shapes, (NDEV,))
    small = {n: _small_full(n, g) for n, g in zip(SMALL_SHARDED, gathered_small)}
    ab_bfb = jnp.broadcast_to(ab_b_f[:, :, None], (n_even, FOX_H, LANE))
    c_bai = jnp.stack([small['c_b_a'].reshape(n_odd, DM), small['c_b_i'].reshape(n_odd, DM)], axis=1)
    row = lambda a, l: a[l][None]

    REST = ('w_xq', 'w_xkv', 'w_xo', 'w_ffn_gu', 'w_ffn_down')

    def mixer_names(l):
        return ('ab_w_in', 'ab_w_out') if l % 2 == 0 else ('c_w_in', 'c_w_a', 'c_w_i', 'c_w_out')

    def shards_of(l, names):
        out = []
        for n in names:
            if n == 'ab_w_in':
                s = ab_t[0][:, l // 2].astype(BF16)
            else:
                s = w[n][l if w[n].shape[0] == DEPTH else l // 2].astype(BF16)
            out.append(s.reshape(-1, s.shape[-1]))
        return out

    def mixer_weights(l, full):
        if l % 2 == 0:
            e = l // 2
            return (row(g_mix_pre, l), row(g_mix_post, l), _ab_pack(full['ab_w_in'].reshape(AB_IN, DM)), ab_bfb[e],
                    small['ab_conv_w'][e], full['ab_w_out'].reshape(DM, DM))
        o = l // 2
        gate_w = lambda g: g.reshape(NDEV, LRU_NB, LRU_BW // NDEV, LRU_BW).transpose(1, 0, 2, 3).reshape(
            LRU_NB, LRU_BW, LRU_BW)
        return (row(g_mix_pre, l), row(g_mix_post, l), full['c_w_in'], small['c_conv_w'][o], row(small['c_conv_b'], o),
                jnp.stack([gate_w(full['c_w_a']), gate_w(full['c_w_i'])]), c_bai[o], row(small['c_lam'], o),
                full['c_w_out'].reshape(DM, DM))

    def rest_weights(l, full):
        cross = (row(g_cross_pre, l), row(g_mem, l), row(g_cross_post, l), full['w_xq'].reshape(DM, DM), full['w_xkv'],
                 full['w_xo'].reshape(DM, DM))
        ffn = (row(g_ffn_pre, l), row(g_ffn_post, l), full['w_ffn_gu'], full['w_ffn_down'].reshape(D_FF, DM))
        return cross, ffn

    def gathered(state, names, after, tag):
        shards, lands = _ag_wait(state, after, "ag_wait_" + tag)
        full = _ag_finish(shards, lands)
        return dict(zip(names, full)), full[0]

    saved, weights = [], []
    h = xs
    names_of = lambda l: mixer_names(l) + REST
    states = {}
    st_m, _ = _ag_start(shards_of(0, mixer_names(0)), small_w_all, "ag_start_0m")
    st_r, _ = _ag_start(shards_of(0, REST), st_m[2], "ag_start_0r")
    states[1], token = _ag_start(shards_of(1, names_of(1)), st_r[2], "ag_start_1")
    full_m, _ = gathered(st_m, mixer_names(0), xs, "0m")
    for l in range(DEPTH):
        if l > 0:
            full, done = gathered(states[l], names_of(l), h, str(l))
            full_m = full_r = full
            token = None
            if l + 2 < DEPTH:
                states[l + 2], token = _ag_start(shards_of(l + 2, names_of(l + 2)), done, "ag_start_%d" % (l + 2))
        mixer = mixer_weights(l, full_m)
        h, s_mix = (_fox_layer_fwd if l % 2 == 0 else _lru_layer_fwd)(h, *mixer, after=token)
        token = None
        if l == 0:
            full_r, done = gathered(st_r, REST, h, "0r")
            states[2], token = _ag_start(shards_of(2, names_of(2)), done, "ag_start_2")
        cross, ffn = rest_weights(l, full_r)
        h, s_cross = _cross_fwd(h, mems, *cross, after=token)
        h, s_ffn = _ffn_fwd(h, *ffn)
        saved.append((s_mix, s_cross, s_ffn))
        weights.append((mixer, cross, ffn))
    mixer_args = lambda l: weights[l][0]
    cross_args = lambda l: weights[l][1]
    ffn_args = lambda l: weights[l][2]
    dx, loss_rep = _loss_head(h, target)
    loss = lax.psum(loss_rep[0, 0], ("x", "y", "c"))

    grads = {n: [None] * w[n].shape[0] for n in BIG}
    partial = {n: [None] * w[n].shape[0] for n in REPLICATED + SMALL_SHARDED}
    def finish(pending, after):
        state, names, where = pending
        for n, g in zip(names, _rs_end(state, after, pos)):
            grads[n][where[n]] = g

    def unit(layer, names):
        return [layer[n][1] for n in names], names, {n: layer[n][0] for n in names}

    d2d = ici = None
    token = None
    for l in reversed(range(DEPTH)):
        s_mix, s_cross, s_ffn = saved[l]
        dx, partial['g_ffn_pre'][l], partial['g_ffn_post'][l], dwgu, dwd = _ffn_bwd(dx, s_ffn, *ffn_args(l), after=token)
        token = None
        if d2d is not None:
            g5s, gots = _rs_d2d_wait(d2d[0], dx, "rs_d2d_wait_%d" % (l + 1))
            state, token = _rs_mid(g5s, gots, pos, str(l + 1))
            ici, d2d = (state,) + d2d[1:], None
        (dx, partial['g_cross_pre'][l], partial['g_mem'][l], partial['g_cross_post'][l], dwq, dwkv, dwo) = _cross_bwd(
            dx, s_cross, mems, *cross_args(l), after=token)
        token = None
        layer = {'w_xq': (l, dwq.reshape(NDEV, DM // NDEV, DM)), 'w_xkv': (l, dwkv), 'w_xo': (l, dwo.reshape(NDEV, DM // NDEV, DM)),
                 'w_ffn_gu': (l, dwgu), 'w_ffn_down': (l, dwd.reshape(NDEV, D_FF // NDEV, DM))}
        if l == 0:
            gs, names, where = unit(layer, REST)
            state, token = _rs_begin(gs, pos, "0r")
            ici_rest = (state, names, where)
        if l % 2 == 0:
            e = l // 2
            (dx, partial['g_mix_pre'][l], partial['g_mix_post'][l], dwall, partial['ab_b_f'][e], partial['ab_conv_w'][e],
             dwout) = _fox_layer_bwd(dx, s_mix, *mixer_args(l), after=token)
            layer['ab_w_in'] = (e, _ab_unpack(dwall).reshape(NDEV, AB_IN // NDEV, DM))
            layer['ab_w_out'] = (e, dwout.reshape(NDEV, DM // NDEV, DM))
        else:
            o = l // 2
            (dx, partial['g_mix_pre'][l], partial['g_mix_post'][l], dwin, partial['c_conv_w'][o], dconvb, dwai, dbai, dlam,
             dwout) = _lru_layer_bwd(dx, s_mix, *mixer_args(l), after=token)
            partial['c_conv_b'][o], partial['c_lam'][o] = dconvb[0], dlam[0]
            partial['c_b_a'][o], partial['c_b_i'][o] = dbai[0].reshape(LRU_NB, LRU_BW), dbai[1].reshape(LRU_NB, LRU_BW)
            rows = LRU_BW // NDEV
            by_dev = lambda d: d.reshape(LRU_NB, NDEV, rows, LRU_BW).transpose(1, 0, 2, 3).reshape(NDEV, LRU_NB * rows, LRU_BW)
            layer['c_w_in'] = (o, dwin)
            layer['c_w_a'] = (o, by_dev(dwai[0]))
            layer['c_w_i'] = (o, by_dev(dwai[1]))
            layer['c_w_out'] = (o, dwout.reshape(NDEV, DM // NDEV, DM))
        token = None
        if ici is not None:
            finish(ici, dx)
            ici = None
        if l > 0:
            gs, names, where = unit(layer, list(layer))
            state, token = _rs_d2d_start(_as_g5(gs), "rs_d2d_start_%d" % l)
            d2d = (state, names, where)
    small_names = REPLICATED + SMALL_SHARDED
    small_parts = [jnp.stack([p.reshape(w[n].shape[1:] if n in REPLICATED else small[n].shape[1:]) for p in partial[n]])
                   for n in small_names]
    small_all = _small_gather(_pack_small(small_parts))
    reduced = _unpack_small(_sum_devices(small_all), [p.shape for p in small_parts])
    grad = {}
    for n, g in zip(small_names, reduced):
        grad[n] = g if n in REPLICATED else _small_shard(g, dev)

    gs, names, where = unit(layer, mixer_names(0))
    state, token = _rs_begin(gs, pos, "0m", after=small_all)
    ici_mixer = (state, names, where)
    finish(ici_rest, dx)

    delta, new_m, new_v = {}, {}, {}
    last = mixer_names(0)
    for n in BIG:
        if n not in last:
            grad[n] = jnp.stack(grads[n]).reshape(w[n].shape)
            delta[n], new_m[n], new_v[n] = _adamw_nd(w[n], grad[n], mom[n], var[n], token)
            token = delta[n]
    shapes = [w[n].shape for n in small_names]
    packed = [_pack_small([t[n] for n in small_names]) for t in (w, grad, mom, var)]
    res_small = _adamw(*packed, after=token)
    for res, out in zip(res_small, (delta, new_m, new_v)):
        for n, val in zip(small_names, _unpack_small(res, shapes)):
            out[n] = val
    finish(ici_mixer, res_small[0])
    for n in last:
        if n == 'ab_w_in':
            g_t = jnp.stack(grads[n], axis=1)
            res = (g_t,) + _adamw_nd(ab_t[0], g_t, ab_t[1], ab_t[2])
            grad[n], delta[n], new_m[n], new_v[n] = (r.transpose(1, 2, 0) for r in res)
            continue
        grad[n] = jnp.stack(grads[n]).reshape(w[n].shape)
        delta[n], new_m[n], new_v[n] = _adamw_nd(w[n], grad[n], mom[n], var[n])

    for t in (grad, delta, new_m, new_v):
        t['w_ffn_gu'] = t['w_ffn_gu'].transpose(0, 2, 1)
    return (loss, dx[None], *[grad[n] for n in WEIGHT_NAMES], *[delta[n] for n in WEIGHT_NAMES],
            *[new_m[n] for n in WEIGHT_NAMES], *[new_v[n] for n in WEIGHT_NAMES])
```

```python
import math

import jax
import jax.numpy as jnp
from jax import lax
from jax.experimental import pallas as pl
from jax.experimental.pallas import tpu as pltpu

F32 = jnp.float32
BF16 = jnp.bfloat16
BS = pl.BlockSpec
ANY = pl.BlockSpec(memory_space=pl.ANY)
MESH = pl.DeviceIdType.MESH

DM = 1024
DEPTH = 4
EPS = 1e-6
NEG = -1e30
FOX_W = 512
FOX_HD = 64
FOX_H = 8
SC_W = 512
SC_K = 3
AB_IN = 3 * FOX_W + FOX_H + 3 * SC_W
AB_PAD = 3200
LRU_BW = 256
LRU_NB = 4
RG_K = 4
RG_C = 8.0
MEM_H = 4
MEM_HD = 256
D_FF = 2816
NDEV = 8
FFB = 2 * D_FF // NDEV
ADAM_LR, ADAM_B1, ADAM_B2, ADAM_EPS, ADAM_WD, ADAM_STEP = 0.001, 0.9, 0.999, 1e-08, 0.01, 10

LANE = 128
VMEM_LIMIT = 16 * 1024 * 1024
VMEM_MM = 32 * 1024 * 1024
VMEM_BIG = 40 * 1024 * 1024


def _params(ngrid, vmem=None):
    return pltpu.CompilerParams(dimension_semantics=("arbitrary",) * ngrid, vmem_limit_bytes=vmem or VMEM_LIMIT)


def _call(kern, **kwargs):
    return pl.pallas_call(kern, **kwargs)


TK_RED = 2048
TM_SUM = 512
MM_TM = 2048
FFN_TM = 1024


def _tile(n, t):
    return t if n % t == 0 else n


def _mm(name, a, b, *, grid, a_spec, b_spec, o_spec, out_shape, dn, out_dtype=F32, vmem=None):
    nred = grid[-1]
    ngrid = len(grid)

    def kern(a_ref, b_ref, o_ref, *scratch):
        p = lax.dot_general(a_ref[...].astype(BF16), b_ref[...].astype(BF16), (dn, ((), ())),
                            preferred_element_type=F32)
        if nred == 1:
            o_ref[...] = p.astype(o_ref.dtype)
            return
        acc = scratch[0] if scratch else o_ref
        r = pl.program_id(ngrid - 1)

        @pl.when(r == 0)
        def _():
            acc[...] = p

        @pl.when(r > 0)
        def _():
            acc[...] += p

        if scratch:
            @pl.when(r == nred - 1)
            def _():
                o_ref[...] = acc[...].astype(o_ref.dtype)

    blk = tuple(d for d in o_spec.block_shape if d is not None)
    scratch = [pltpu.VMEM(blk, F32)] if (nred > 1 and out_dtype != F32) else []
    return _call(kern, name=name, grid=grid, in_specs=[a_spec, b_spec], out_specs=o_spec,
                          out_shape=jax.ShapeDtypeStruct(out_shape, out_dtype), scratch_shapes=scratch,
                          compiler_params=_params(ngrid, vmem or (VMEM_LIMIT if dn == TN else VMEM_MM)))(a, b)


NN = ((1,), (0,))
NT = ((1,), (1,))
TN = ((0,), (0,))


def _mm_nn(name, a, w, out_dtype=F32, tn=None, vmem=None, tm=MM_TM):
    m, k = a.shape
    n = w.shape[1]
    tm = _tile(m, tm)
    tn = n if tn is None else tn
    return _mm(name, a, w, grid=(m // tm, n // tn, 1), a_spec=BS((tm, k), lambda i, j, r: (i, 0)),
               b_spec=BS((k, tn), lambda i, j, r: (0, j)), o_spec=BS((tm, tn), lambda i, j, r: (i, j)),
               out_shape=(m, n), dn=NN, out_dtype=out_dtype, vmem=vmem)


def _mm_nt_cols(name, a, wt, tn):
    m, k = a.shape
    n = wt.shape[0]
    tm = _tile(m, MM_TM)
    return _mm(name, a, wt, grid=(m // tm, n // tn, 1), a_spec=BS((tm, k), lambda i, j, r: (i, 0)),
               b_spec=BS((tn, k), lambda i, j, r: (j, 0)), o_spec=BS((tm, tn), lambda i, j, r: (i, j)),
               out_shape=(m, n), dn=NT)


def _mm_tn_rows(name, a, b, tk):
    m, k = a.shape
    n = b.shape[1]
    tm = _tile(m, TK_RED)
    return _mm(name, a, b, grid=(k // tk, m // tm), a_spec=BS((tm, tk), lambda j, r: (r, j)),
               b_spec=BS((tm, n), lambda j, r: (r, 0)), o_spec=BS((tk, n), lambda j, r: (j, 0)),
               out_shape=(k, n), dn=TN)


def _mm_nt(name, a, w, out_dtype=F32, tn=None):
    m, n = a.shape
    k = w.shape[0]
    tm = _tile(m, MM_TM)
    tn = n if tn is None else tn
    return _mm(name, a, w, grid=(m // tm, n // tn), a_spec=BS((tm, tn), lambda i, r: (i, r)),
               b_spec=BS((k, tn), lambda i, r: (0, r)), o_spec=BS((tm, k), lambda i, r: (i, 0)),
               out_shape=(m, k), dn=NT, out_dtype=out_dtype)


def _mm_tn(name, a, b, tn=None):
    m, k = a.shape
    n = b.shape[1]
    tm = _tile(m, TK_RED)
    tn = n if tn is None else tn
    return _mm(name, a, b, grid=(n // tn, m // tm), a_spec=BS((tm, k), lambda j, r: (r, 0)),
               b_spec=BS((tm, tn), lambda j, r: (r, j)), o_spec=BS((k, tn), lambda j, r: (0, j)),
               out_shape=(k, n), dn=TN)


def _bmm_nn(name, a, w, out_dtype=F32):
    m, k = a.shape
    g, _, n = w.shape
    tm = _tile(m, MM_TM)
    return _mm(name, a, w, grid=(g, m // tm, 1), a_spec=BS((tm, k), lambda q, i, r: (i, 0)),
               b_spec=BS((None, k, n), lambda q, i, r: (q, 0, 0)), o_spec=BS((None, tm, n), lambda q, i, r: (q, i, 0)),
               out_shape=(g, m, n), dn=NN, out_dtype=out_dtype)


def _bmm_tn(name, a, b):
    m, k = a.shape
    g, _, n = b.shape
    tm = _tile(m, TK_RED)
    return _mm(name, a, b, grid=(g, m // tm), a_spec=BS((tm, k), lambda q, r: (r, 0)),
               b_spec=BS((None, tm, n), lambda q, r: (q, r, 0)), o_spec=BS((None, k, n), lambda q, r: (q, 0, 0)),
               out_shape=(g, k, n), dn=TN)


def _block_sum(name, a, w, dn, out_cols):
    g, m, ac = a.shape
    tm = _tile(m, TM_SUM)

    def kern(a_ref, w_ref, o_ref):
        acc = None
        for q in range(g):
            p = lax.dot_general(a_ref[q].astype(BF16), w_ref[q].astype(BF16), (dn, ((), ())), preferred_element_type=F32)
            acc = p if acc is None else acc + p
        o_ref[...] = acc

    return _call(kern, name=name, grid=(m // tm,),
                 in_specs=[BS((g, tm, ac), lambda i: (0, i, 0)), BS(w.shape, lambda i: (0, 0, 0))],
                 out_specs=BS((tm, out_cols), lambda i: (i, 0)), out_shape=jax.ShapeDtypeStruct((m, out_cols), F32),
                 compiler_params=_params(1, VMEM_BIG))(a, w)


def _bmm_nt_sum(name, a, w):
    return _block_sum(name, a, w, NT, w.shape[1])


def _bmm_nn_sum(name, a, w):
    return _block_sum(name, a, w, NN, w.shape[2])


def _rstd(x):
    return lax.rsqrt(jnp.mean(x * x, axis=-1, keepdims=True) + EPS)


def _norm_fwd(x, g, after=None):
    rows = x.shape[0]
    tm = _tile(rows, 512)

    def kern(x_ref, g_ref, *rest):
        xv = x_ref[...]
        rest[-1][...] = ((xv * _rstd(xv)) * g_ref[...]).astype(BF16)

    extra = () if after is None else (after,)
    return _call(kern, name="norm_fwd", grid=(rows // tm,),
                          in_specs=[BS((tm, DM), lambda i: (i, 0)), BS((1, DM), lambda i: (0, 0))] + [ANY] * len(extra),
                          out_specs=BS((tm, DM), lambda i: (i, 0)),
                          out_shape=jax.ShapeDtypeStruct((rows, DM), BF16), compiler_params=_params(1))(x, g, *extra)


def _norm_res(x, y, g):
    rows = x.shape[0]
    tm = _tile(rows, 512)

    def kern(x_ref, y_ref, g_ref, o_ref):
        yv = y_ref[...]
        o_ref[...] = x_ref[...] + (yv * _rstd(yv)) * g_ref[...]

    row = BS((tm, DM), lambda i: (i, 0))
    return _call(kern, name="norm_res", grid=(rows // tm,),
                          in_specs=[row, row, BS((1, DM), lambda i: (0, 0))], out_specs=row,
                          out_shape=jax.ShapeDtypeStruct((rows, DM), F32), compiler_params=_params(1))(x, y, g)


def _norm_bwd(z, dout, g, resid, out_dtype, after=None):
    rows = z.shape[0]
    tm = _tile(rows, 512)
    has_res = resid is not None

    def kern(*refs):
        z_ref, d_ref, g_ref = refs[:3]
        r_ref = refs[3] if has_res else None
        dz_ref, dg_ref = refs[-2:]
        zv = z_ref[...]
        dv = d_ref[...].astype(F32)
        r = _rstd(zv)
        zh = zv * r
        dzh = dv * g_ref[...]
        dz = r * (dzh - zh * jnp.mean(dzh * zh, axis=-1, keepdims=True))
        if has_res:
            dz = dz + r_ref[...]
        dz_ref[...] = dz.astype(dz_ref.dtype)
        part = jnp.sum(dv * zh, axis=0, keepdims=True)

        @pl.when(pl.program_id(0) == 0)
        def _():
            dg_ref[...] = part

        @pl.when(pl.program_id(0) > 0)
        def _():
            dg_ref[...] += part

    row = BS((tm, DM), lambda i: (i, 0))
    vec = BS((1, DM), lambda i: (0, 0))
    ins = [row, row, vec] + ([row] if has_res else []) + ([ANY] if after is not None else [])
    args = (z, dout, g) + ((resid,) if has_res else ()) + ((after,) if after is not None else ())
    return _call(kern, name="norm_bwd_res" if has_res else "norm_bwd", grid=(rows // tm,), in_specs=ins,
                          out_specs=[row, vec],
                          out_shape=[jax.ShapeDtypeStruct((rows, DM), out_dtype), jax.ShapeDtypeStruct((1, DM), F32)],
                          compiler_params=_params(1))(*args)


def _ffn_up(h, wgu4):
    s = h.shape[0]
    tm = _tile(s, FFN_TM)

    def kern(h_ref, w_ref, gu_ref, a_ref):
        hv = h_ref[...]
        gate = lax.dot_general(hv, w_ref[0], (NT, ((), ())), preferred_element_type=F32)
        up = lax.dot_general(hv, w_ref[1], (NT, ((), ())), preferred_element_type=F32)
        gu_ref[0] = gate.astype(BF16)
        gu_ref[1] = up.astype(BF16)
        a_ref[...] = (gate * jax.nn.sigmoid(gate) * up).astype(BF16)

    return _call(
        kern, name="ffn_up", grid=(4, s // tm),
        in_specs=[BS((tm, DM), lambda j, i: (i, 0)), BS((2, None, FFB, DM), lambda j, i: (0, j, 0, 0))],
        out_specs=[BS((2, None, tm, FFB), lambda j, i: (0, j, i, 0)), BS((None, tm, FFB), lambda j, i: (j, i, 0))],
        out_shape=[jax.ShapeDtypeStruct((2, 4, s, FFB), BF16), jax.ShapeDtypeStruct((4, s, FFB), BF16)],
        compiler_params=_params(2, VMEM_BIG))(h, wgu4)


def _ffn_da(dy, wd4, gu):
    s = dy.shape[0]
    tm = _tile(s, FFN_TM)

    def kern(dy_ref, w_ref, gu_ref, o_ref):
        da = lax.dot_general(dy_ref[...], w_ref[...], (NT, ((), ())), preferred_element_type=F32)
        gate = gu_ref[0].astype(F32)
        up = gu_ref[1].astype(F32)
        sg = jax.nn.sigmoid(gate)
        o_ref[0] = (da * up * (sg * (1.0 + gate * (1.0 - sg)))).astype(BF16)
        o_ref[1] = (da * (gate * sg)).astype(BF16)

    blk = BS((2, None, tm, FFB), lambda j, i: (0, j, i, 0))
    return _call(
        kern, name="ffn_da", grid=(4, s // tm),
        in_specs=[BS((tm, DM), lambda j, i: (i, 0)), BS((None, FFB, DM), lambda j, i: (j, 0, 0)), blk],
        out_specs=blk, out_shape=jax.ShapeDtypeStruct((2, 4, s, FFB), BF16), compiler_params=_params(2, VMEM_BIG))(dy, wd4, gu)


def _ffn_fwd(x, gpre, gpost, wgu, wd):
    h = _norm_fwd(x, gpre)
    gu, a = _ffn_up(h, wgu.reshape(2, 4, FFB, DM))
    y = _bmm_nn_sum("ffn_down", a, wd.reshape(4, FFB, DM))
    return _norm_res(x, y, gpost), (x, h, gu, a, y)


def _ffn_bwd(dxo, saved, gpre, gpost, wgu, wd, after=None):
    x, h, gu, a, y = saved
    s = x.shape[0]
    dy, dgpost = _norm_bwd(y, dxo, gpost, None, BF16, after)
    dgu = _ffn_da(dy, wd.reshape(4, FFB, DM), gu).reshape(8, s, FFB)
    dwd = _bmm_tn_a3("ffn_dwd", a, dy)
    dwgu = _bmm_tn_a3("ffn_dwgu", dgu, h)
    dh = _bmm_nn_sum("ffn_dh", dgu, wgu)
    dx, dgpre = _norm_bwd(x, dh, gpre, dxo, F32)
    return dx, dgpre, dgpost, dwgu, dwd.reshape(D_FF, DM)


def _bmm_tn_a3(name, a, b):
    g, m, k = a.shape
    n = b.shape[1]
    tm = _tile(m, TK_RED)
    return _mm(name, a, b, grid=(g, m // tm), a_spec=BS((None, tm, k), lambda q, r: (q, r, 0)),
               b_spec=BS((tm, n), lambda q, r: (r, 0)), o_spec=BS((None, k, n), lambda q, r: (q, 0, 0)),
               out_shape=(g, k, n), dn=TN)


XATTN_TM = 2048


def _softmax_rows(s):
    m = jnp.max(s, axis=-1, keepdims=True)
    p = jnp.exp(s - m)
    return p / jnp.sum(p, axis=-1, keepdims=True)


def _xattn_fwd_call(h, wq, kv):
    s = h.shape[0]
    mlen = kv.shape[1]
    tm = _tile(s, XATTN_TM)
    scale = MEM_HD ** -0.5

    def kern(h_ref, w_ref, k_ref, v_ref, q_ref, o_ref):
        q = jnp.dot(h_ref[...], w_ref[...], preferred_element_type=F32).astype(BF16)
        q_ref[...] = q
        sc = lax.dot_general(q, k_ref[...], (NT, ((), ())), preferred_element_type=F32) * scale
        p = _softmax_rows(sc)
        o_ref[...] = jnp.dot(p.astype(BF16), v_ref[...], preferred_element_type=F32).astype(BF16)

    blk = BS((tm, MEM_HD), lambda i, hd: (i, hd))
    return _call(
        kern, name="xattn_fwd", grid=(s // tm, MEM_H),
        in_specs=[BS((tm, DM), lambda i, hd: (i, 0)), BS((DM, MEM_HD), lambda i, hd: (0, hd)),
                  BS((None, mlen, MEM_HD), lambda i, hd: (hd, 0, 0)),
                  BS((None, mlen, MEM_HD), lambda i, hd: (MEM_H + hd, 0, 0))],
        out_specs=[blk, blk],
        out_shape=[jax.ShapeDtypeStruct((s, DM), BF16), jax.ShapeDtypeStruct((s, DM), BF16)],
        compiler_params=_params(2, VMEM_MM))(h, wq, kv, kv)


def _xattn_bwd_call(q, kv, do):
    s = q.shape[0]
    mlen = kv.shape[1]
    tm = _tile(s, XATTN_TM)
    scale = MEM_HD ** -0.5

    def kern(q_ref, k_ref, v_ref, do_ref, dq_ref, dkv_ref):
        qv, kvv, vv, dov = q_ref[...], k_ref[...], v_ref[...], do_ref[...]
        sc = lax.dot_general(qv, kvv, (NT, ((), ())), preferred_element_type=F32) * scale
        p = _softmax_rows(sc)
        dp = lax.dot_general(dov, vv, (NT, ((), ())), preferred_element_type=F32)
        ds = (p * (dp - jnp.sum(dp * p, axis=-1, keepdims=True)) * scale).astype(BF16)
        dq_ref[...] = jnp.dot(ds, kvv, preferred_element_type=F32).astype(BF16)
        dk = lax.dot_general(ds, qv, (TN, ((), ())), preferred_element_type=F32)
        dv = lax.dot_general(p.astype(BF16), dov, (TN, ((), ())), preferred_element_type=F32)

        @pl.when(pl.program_id(1) == 0)
        def _():
            dkv_ref[0] = dk
            dkv_ref[1] = dv

        @pl.when(pl.program_id(1) > 0)
        def _():
            dkv_ref[0] += dk
            dkv_ref[1] += dv

    blk = BS((tm, MEM_HD), lambda hd, i: (i, hd))
    return _call(
        kern, name="xattn_bwd", grid=(MEM_H, s // tm),
        in_specs=[blk, BS((None, mlen, MEM_HD), lambda hd, i: (hd, 0, 0)),
                  BS((None, mlen, MEM_HD), lambda hd, i: (MEM_H + hd, 0, 0)), blk],
        out_specs=[blk, BS((2, None, mlen, MEM_HD), lambda hd, i: (0, hd, 0, 0))],
        out_shape=[jax.ShapeDtypeStruct((s, DM), BF16), jax.ShapeDtypeStruct((2, MEM_H, mlen, MEM_HD), F32)],
        compiler_params=_params(2, VMEM_MM))(q, kv, kv, do)


def _cross_fwd(x, mem, gpre, gmem, gpost, wq, wkv, wo, after=None):
    h = _norm_fwd(x, gpre, after)
    mn = _norm_fwd(mem, gmem)
    kv = _bmm_nn("xattn_kv", mn, wkv, BF16)
    q, o = _xattn_fwd_call(h, wq, kv)
    y = _mm_nn("xattn_out", o, wo)
    return _norm_res(x, y, gpost), (x, h, mn, kv, q, o, y)


def _cross_bwd(dxo, saved, mem, gpre, gmem, gpost, wq, wkv, wo, after=None):
    x, h, mn, kv, q, o, y = saved
    mlen = mem.shape[0]
    dy, dgpost = _norm_bwd(y, dxo, gpost, None, BF16, after)
    do = _mm_nt("xattn_do", dy, wo, BF16)
    dwo = _mm_tn("xattn_dwo", o, dy)
    dq, dkv = _xattn_bwd_call(q, kv, do)
    dwq = _mm_tn("xattn_dwq", h, dq)
    dh = _mm_nt("xattn_dh", dq, wq)
    dkv8 = dkv.reshape(8, mlen, MEM_HD)
    dwkv = _bmm_tn("xattn_dwkv", mn, dkv8)
    dmn = _bmm_nt_sum("xattn_dmn", dkv8, wkv)
    _, dgmem = _norm_bwd(mem, dmn, gmem, None, BF16)
    dx, dgpre = _norm_bwd(x, dh, gpre, dxo, F32)
    return dx, dgpre, dgmem, dgpost, dwq, dwkv, dwo


def _log_sigmoid(z):
    return jnp.minimum(z, 0.0) - jnp.log1p(jnp.exp(-jnp.abs(z)))


def _lane_scan_steps():
    return (1, 2, 4, 8, 16, 32, 64)


def _fox_cum(frow, bfb):
    s = frow.shape[1]

    def kern(f_ref, b_ref, o_ref):
        lane = lax.broadcasted_iota(jnp.int32, (FOX_H, LANE), 1)
        carry = jnp.zeros((FOX_H, 1), F32)
        for c in range(s // LANE):
            sl = slice(c * LANE, (c + 1) * LANE)
            lf = _log_sigmoid(f_ref[:, sl] + b_ref[...])
            v = lf
            for d in _lane_scan_steps():
                v = v + jnp.where(lane >= d, pltpu.roll(v, d, 1), 0.0)
            o_ref[:, sl] = v + carry
            carry = carry + jnp.sum(lf, axis=1, keepdims=True)

    return _call(kern, name="fox_cum", out_shape=jax.ShapeDtypeStruct((FOX_H, s), F32),
                          compiler_params=pltpu.CompilerParams(vmem_limit_bytes=VMEM_LIMIT))(frow, bfb)


def _fox_dlogf(dcq, dck, frow, bfb):
    s = frow.shape[1]

    def kern(q_ref, d_ref, f_ref, b_ref, df_ref, db_ref):
        lane = lax.broadcasted_iota(jnp.int32, (FOX_H, LANE), 1)
        carry = jnp.zeros((FOX_H, 1), F32)
        dbf = jnp.zeros((FOX_H, 1), F32)
        for c in reversed(range(s // LANE)):
            sl = slice(c * LANE, (c + 1) * LANE)
            dc = q_ref[:, sl] - d_ref[:, sl]
            v = dc
            for d in _lane_scan_steps():
                v = v + jnp.where(lane < LANE - d, pltpu.roll(v, LANE - d, 1), 0.0)
            v = v + carry
            carry = carry + jnp.sum(dc, axis=1, keepdims=True)
            df = v * jax.nn.sigmoid(-(f_ref[:, sl] + b_ref[...]))
            df_ref[:, sl] = df
            dbf = dbf + jnp.sum(df, axis=1, keepdims=True)
        db_ref[...] = jnp.broadcast_to(dbf, (FOX_H, LANE))

    return _call(kern, name="fox_dlogf",
                          out_shape=[jax.ShapeDtypeStruct((FOX_H, s), F32), jax.ShapeDtypeStruct((FOX_H, LANE), F32)],
                          compiler_params=pltpu.CompilerParams(vmem_limit_bytes=VMEM_LIMIT))(dcq, dck, frow, bfb)


FOX_TQ = 512
Q_COL, K_COL, V_COL = 0, FOX_W // LANE, 2 * FOX_W // LANE
B_COL = 3 * FOX_W // LANE
C_COL = B_COL + SC_W // LANE
U_COL = C_COL + SC_W // LANE


def _bf16_terms(c):
    hi = c.astype(BF16).astype(F32)
    mid = (c - hi).astype(BF16).astype(F32)
    return hi, mid, (c - hi - mid).astype(BF16).astype(F32)


def _fox_operands(qv, kv, cq, ck, lane, hh, scale):
    sel = (lane < FOX_HD) if hh == 0 else (lane >= FOX_HD)
    b0 = FOX_HD if hh == 0 else 0
    qa = jnp.where(sel, qv * scale, 0.0)
    ka = jnp.where(sel, kv, 0.0)
    for n, (tq_, tk_) in enumerate(zip(_bf16_terms(cq), _bf16_terms(ck))):
        qa = jnp.where(lane == b0 + n, tq_, jnp.where(lane == b0 + 3 + n, 1.0, qa))
        ka = jnp.where(lane == b0 + n, 1.0, jnp.where(lane == b0 + 3 + n, -tk_, ka))
    return sel, qa.astype(BF16), ka.astype(BF16)


def _fox_logits(qa, ka, causal):
    sc = lax.dot_general(qa, ka, (NT, ((), ())), preferred_element_type=F32)
    return sc if causal is None else jnp.where(causal, sc, NEG)


def _fox_prep(proj, cumc):
    s = proj.shape[0]
    tp = _tile(s, 512)
    scale = FOX_HD ** -0.5

    def kern(q_ref, k_ref, c_ref, qa_ref, ka_ref):
        lane = lax.broadcasted_iota(jnp.int32, (tp, LANE), 1)
        for hh in range(2):
            _, qa_ref[hh], ka_ref[hh] = _fox_operands(q_ref[...], k_ref[...], c_ref[hh], c_ref[hh], lane, hh, scale)

    pair = BS((2, tp, LANE), lambda hp, i: (hp, i, 0))
    shp = jax.ShapeDtypeStruct((FOX_H, s, LANE), BF16)
    return _call(kern, name="fox_prep", grid=(4, s // tp),
                 in_specs=[BS((tp, LANE), lambda hp, i: (i, Q_COL + hp)), BS((tp, LANE), lambda hp, i: (i, K_COL + hp)), pair],
                 out_specs=[pair, pair], out_shape=[shp, shp], compiler_params=_params(2))(proj, proj, cumc)


def _fox_fwd_call(proj, qa, ka):
    s = proj.shape[0]
    tq = _tile(s, FOX_TQ)
    nq = s // tq

    def kern(qa_ref, ka_ref, v_ref, o_ref, lse_ref, m_s, l_s, acc_s):
        i = pl.program_id(1)
        j = pl.program_id(2)
        lane = lax.broadcasted_iota(jnp.int32, (tq, LANE), 1)

        @pl.when(j == 0)
        def _():
            m_s[...] = jnp.full(m_s.shape, NEG, F32)
            l_s[...] = jnp.zeros(l_s.shape, F32)
            acc_s[...] = jnp.zeros(acc_s.shape, F32)

        def step(diagonal):
            vb = v_ref[...].astype(BF16)
            causal = (lax.broadcasted_iota(jnp.int32, (tq, tq), 0) >= lax.broadcasted_iota(jnp.int32, (tq, tq), 1)
                      if diagonal else None)
            for hh in range(2):
                sc = _fox_logits(qa_ref[hh], ka_ref[hh], causal)
                m_prev = m_s[hh]
                m_new = jnp.maximum(m_prev, jnp.max(sc, axis=-1, keepdims=True))
                alpha = jnp.exp(m_prev - m_new)
                p = jnp.exp(sc - m_new)
                l_s[hh] = alpha * l_s[hh] + jnp.sum(p, axis=-1, keepdims=True)
                acc_s[hh] = alpha * acc_s[hh] + jnp.dot(p.astype(BF16), vb, preferred_element_type=F32)
                m_s[hh] = m_new

        @pl.when(j < i)
        def _():
            step(False)

        @pl.when(j == i)
        def _():
            step(True)
            o_ref[...] = jnp.where(lane < FOX_HD, acc_s[0] / l_s[0], acc_s[1] / l_s[1])
            for hh in range(2):
                lse_ref[hh] = jnp.broadcast_to(m_s[hh] + jnp.log(l_s[hh]), (tq, LANE))

    kvi = lambda hp, i, j: jnp.minimum(j, i)
    return _call(
        kern, name="fox_fwd", grid=(4, nq, nq),
        in_specs=[BS((2, tq, LANE), lambda hp, i, j: (hp, i, 0)),
                  BS((2, tq, LANE), lambda hp, i, j: (hp, kvi(hp, i, j), 0)),
                  BS((tq, LANE), lambda hp, i, j: (kvi(hp, i, j), V_COL + hp))],
        out_specs=[BS((tq, LANE), lambda hp, i, j: (i, hp)), BS((2, tq, LANE), lambda hp, i, j: (hp, i, 0))],
        out_shape=[jax.ShapeDtypeStruct((s, FOX_W), F32), jax.ShapeDtypeStruct((FOX_H, s, LANE), F32)],
        scratch_shapes=[pltpu.VMEM((2, tq, 1), F32), pltpu.VMEM((2, tq, 1), F32), pltpu.VMEM((2, tq, LANE), F32)],
        compiler_params=_params(3))(qa, ka, proj)


ROWSUM_M = 16


def _fox_bwd_call(proj, o, lse, dcat, qa, ka):
    s = proj.shape[0]
    tq = _tile(s, FOX_TQ)
    nq = s // tq
    reps = tq // LANE
    scale = FOX_HD ** -0.5

    def kern(qa_ref, ka_ref, v_ref, do_ref, o_ref, lse_ref, dq_ref, dk_ref, dv_ref, dck_ref, dcq_ref):
        j = pl.program_id(1)
        i = pl.program_id(2)
        lane = lax.broadcasted_iota(jnp.int32, (tq, LANE), 1)
        ones = jnp.ones((ROWSUM_M, tq), BF16)

        @pl.when((j == 0) & (i == 0))
        def _():
            dq_ref[...] = jnp.zeros(dq_ref.shape, F32)
            dcq_ref[...] = jnp.zeros(dcq_ref.shape, F32)

        @pl.when(i == j)
        def _():
            dk_ref[...] = jnp.zeros(dk_ref.shape, F32)
            dv_ref[...] = jnp.zeros(dv_ref.shape, F32)
            dck_ref[...] = jnp.zeros(dck_ref.shape, F32)

        def step(diagonal):
            dov = do_ref[...]
            ov = o_ref[...]
            vb = v_ref[...].astype(BF16)
            causal = (lax.broadcasted_iota(jnp.int32, (tq, tq), 0) >= lax.broadcasted_iota(jnp.int32, (tq, tq), 1)
                      if diagonal else None)
            dq_t = jnp.zeros((tq, LANE), F32)
            dk_t = jnp.zeros((tq, LANE), F32)
            dv_t = jnp.zeros((tq, LANE), F32)
            for hh in range(2):
                sel = (lane < FOX_HD) if hh == 0 else (lane >= FOX_HD)
                qa, ka = qa_ref[hh], ka_ref[hh]
                dom32 = jnp.where(sel, dov, 0.0)
                dom = dom32.astype(BF16)
                sc = _fox_logits(qa, ka, causal)
                p = jnp.exp(sc - jnp.tile(lse_ref[hh], (1, reps)))
                dp = lax.dot_general(dom, vb, (NT, ((), ())), preferred_element_type=F32)
                delta = jnp.sum(dom32 * ov, axis=-1, keepdims=True)
                ds = p * (dp - delta)
                dsb = ds.astype(BF16)
                dq_t = jnp.where(sel, jnp.dot(dsb, ka, preferred_element_type=F32) * scale, dq_t)
                dk_t = jnp.where(sel, lax.dot_general(dsb, qa, (TN, ((), ())), preferred_element_type=F32), dk_t)
                dv_t = dv_t + lax.dot_general(p.astype(BF16), dom, (TN, ((), ())), preferred_element_type=F32)
                dck_ref[hh] += jnp.sum(ds, axis=0, keepdims=True)
                ds_lo = (ds - dsb.astype(F32)).astype(BF16)
                dcq_ref[hh, i] += (lax.dot_general(ones, dsb, (NT, ((), ())), preferred_element_type=F32)
                                   + lax.dot_general(ones, ds_lo, (NT, ((), ())), preferred_element_type=F32))
            rows = pl.ds(pl.multiple_of(i * tq, tq), tq)
            dq_ref[rows, :] += dq_t
            dk_ref[...] += dk_t
            dv_ref[...] += dv_t

        @pl.when(i > j)
        def _():
            step(False)

        @pl.when(i == j)
        def _():
            step(True)

    qi = lambda hp, j, i: jnp.maximum(i, j)
    return _call(
        kern, name="fox_bwd", grid=(4, nq, nq),
        in_specs=[BS((2, tq, LANE), lambda hp, j, i: (hp, qi(hp, j, i), 0)),
                  BS((2, tq, LANE), lambda hp, j, i: (hp, j, 0)),
                  BS((tq, LANE), lambda hp, j, i: (j, V_COL + hp)),
                  BS((tq, LANE), lambda hp, j, i: (qi(hp, j, i), hp)),
                  BS((tq, LANE), lambda hp, j, i: (qi(hp, j, i), hp)),
                  BS((2, tq, LANE), lambda hp, j, i: (hp, qi(hp, j, i), 0))],
        out_specs=[BS((s, LANE), lambda hp, j, i: (0, hp)), BS((tq, LANE), lambda hp, j, i: (j, hp)),
                   BS((tq, LANE), lambda hp, j, i: (j, hp)), BS((2, 1, tq), lambda hp, j, i: (hp, 0, j)),
                   BS((2, nq, ROWSUM_M, tq), lambda hp, j, i: (hp, 0, 0, 0))],
        out_shape=[jax.ShapeDtypeStruct((s, FOX_W), F32), jax.ShapeDtypeStruct((s, FOX_W), F32),
                   jax.ShapeDtypeStruct((s, FOX_W), F32), jax.ShapeDtypeStruct((FOX_H, 1, s), F32),
                   jax.ShapeDtypeStruct((FOX_H, nq, ROWSUM_M, tq), F32)],
        compiler_params=_params(3))(qa, ka, proj, dcat, o, lse)


def _shift_down(v, d, row):
    return jnp.where(row >= d, pltpu.roll(v, d, 0), 0.0)


def _shift_up(v, d, row, n):
    return jnp.where(row < n - d, pltpu.roll(v, n - d, 0), 0.0)


def _sconv_fwd(proj, convw):
    s = proj.shape[0]

    def kern(b_ref, c_ref, u_ref, w_ref, y_ref):
        row = lax.broadcasted_iota(jnp.int32, (s, LANE), 0)
        z = c_ref[...] * u_ref[...]
        conv = w_ref[2:3, :] * z + w_ref[1:2, :] * _shift_down(z, 1, row) + w_ref[0:1, :] * _shift_down(z, 2, row)
        y_ref[...] = (b_ref[...] * conv).astype(BF16)

    col = lambda base: BS((s, LANE), lambda cb: (0, base + cb))
    return _call(kern, name="sconv_fwd", grid=(SC_W // LANE,),
                          in_specs=[col(B_COL), col(C_COL), col(U_COL), BS((SC_K, LANE), lambda cb: (0, cb))],
                          out_specs=BS((s, LANE), lambda cb: (0, cb)),
                          out_shape=jax.ShapeDtypeStruct((s, SC_W), BF16), compiler_params=_params(1))(proj, proj, proj, convw)


def _sconv_bwd(proj, convw, dcat):
    s = proj.shape[0]

    def kern(b_ref, c_ref, u_ref, w_ref, dy_ref, db_ref, dc_ref, du_ref, dw_ref):
        row = lax.broadcasted_iota(jnp.int32, (s, LANE), 0)
        cv, uv, dyv = c_ref[...], u_ref[...], dy_ref[...]
        z = cv * uv
        z1 = _shift_down(z, 1, row)
        z2 = _shift_down(z, 2, row)
        conv = w_ref[2:3, :] * z + w_ref[1:2, :] * z1 + w_ref[0:1, :] * z2
        db_ref[...] = dyv * conv
        dcv = dyv * b_ref[...]
        dz = w_ref[2:3, :] * dcv + w_ref[1:2, :] * _shift_up(dcv, 1, row, s) + w_ref[0:1, :] * _shift_up(dcv, 2, row, s)
        dc_ref[...] = dz * uv
        du_ref[...] = dz * cv
        dw_ref[0:1, :] = jnp.sum(dcv * z2, axis=0, keepdims=True)
        dw_ref[1:2, :] = jnp.sum(dcv * z1, axis=0, keepdims=True)
        dw_ref[2:3, :] = jnp.sum(dcv * z, axis=0, keepdims=True)

    col = lambda base: BS((s, LANE), lambda cb: (0, base + cb))
    out = BS((s, LANE), lambda cb: (0, cb))
    wspec = BS((SC_K, LANE), lambda cb: (0, cb))
    act = jax.ShapeDtypeStruct((s, SC_W), F32)
    return _call(kern, name="sconv_bwd", grid=(SC_W // LANE,),
                          in_specs=[col(B_COL), col(C_COL), col(U_COL), wspec, col(FOX_W // LANE)],
                          out_specs=[out, out, out, wspec],
                          out_shape=[act, act, act, jax.ShapeDtypeStruct((SC_K, SC_W), F32)],
                          compiler_params=_params(1))(proj, proj, proj, convw, dcat)


def _fox_layer_fwd(x, gpre, gpost, wall, bfb, convw, wout, after=None):
    s = x.shape[0]
    h = _norm_fwd(x, gpre, after)
    proj = _mm_nt_cols("fox_proj", h, wall, AB_PAD // 5)
    frow = proj[:, 3 * FOX_W + 3 * SC_W:3 * FOX_W + 3 * SC_W + FOX_H].T
    cumr = _fox_cum(frow, bfb)
    qa, ka = _fox_prep(proj, jnp.broadcast_to(cumr[:, :, None], (FOX_H, s, LANE)))
    o, lse = _fox_fwd_call(proj, qa, ka)
    yb = _sconv_fwd(proj, convw)
    cat = jnp.concatenate([o.astype(BF16), yb], axis=1)
    y = _mm_nn("fox_out", cat, wout)
    return _norm_res(x, y, gpost), (x, h, proj, frow, qa, ka, o, lse, cat, y)


def _fox_layer_bwd(dxo, saved, gpre, gpost, wall, bfb, convw, wout, after=None):
    x, h, proj, frow, qa, ka, o, lse, cat, y = saved
    s = x.shape[0]
    dy, dgpost = _norm_bwd(y, dxo, gpost, None, BF16, after)
    dcat = _mm_nt("fox_dcat", dy, wout)
    dwout = _mm_tn("fox_dwout", cat, dy)
    db, dc, du, dconvw = _sconv_bwd(proj, convw, dcat)
    dq, dk, dv, dck, dcq = _fox_bwd_call(proj, o, lse, dcat, qa, ka)
    dfrow, dbf = _fox_dlogf(dcq[:, :, 0, :].reshape(FOX_H, s), dck.reshape(FOX_H, s), frow, bfb)
    dfcol = jnp.pad(dfrow.T, ((0, 0), (0, LANE - FOX_H)))
    dproj = jnp.concatenate([dq, dk, dv, db, dc, du, dfcol], axis=1).astype(BF16)
    dwall = _mm_tn_rows("fox_dwall", dproj, h, AB_PAD // 5)
    dh = _mm_nn("fox_dh", dproj, wall, vmem=VMEM_BIG, tm=FFN_TM)
    dx, dgpre = _norm_bwd(x, dh, gpre, dxo, F32)
    return dx, dgpre, dgpost, dwall, dbf[:, 0], dconvw, dwout


def _ab_pack(wt):
    nf = 3 * FOX_W
    return jnp.concatenate([wt[:nf], wt[nf + FOX_H:], wt[nf:nf + FOX_H],
                            jnp.zeros((AB_PAD - AB_IN, wt.shape[1]), wt.dtype)], axis=0)


def _ab_unpack(wt):
    nf = 3 * FOX_W
    nbcu = 3 * SC_W
    return jnp.concatenate([wt[:nf], wt[nf + nbcu:nf + nbcu + FOX_H], wt[nf:nf + nbcu]], axis=0)


NCH = DM // LANE
CH_PER_BLK = LRU_BW // LANE


def _chunk_spec(s, lead=0):
    return BS((None, s, LANE), lambda ch: (lead + ch // CH_PER_BLK, 0, ch % CH_PER_BLK))


def _vec_chunk(rows):
    return BS((rows, LANE), lambda ch: (0, ch))


def _neg_expm1(x):
    series = -x * (1.0 + x * (1 / 2) * (1.0 + x * (1 / 3) * (1.0 + x * (1 / 4) * (1.0 + x * (1 / 5) * (
        1.0 + x * (1 / 6) * (1.0 + x * (1 / 7)))))))
    return jnp.where(x > -0.25, series, 1.0 - jnp.exp(x))


def _softplus(z):
    return jnp.maximum(z, 0.0) + jnp.log1p(jnp.exp(-jnp.abs(z)))


GELU_C = math.sqrt(2.0 / math.pi)
GELU_A = 0.044715


def _gelu(x):
    return 0.5 * x * (1.0 + jnp.tanh(GELU_C * (x + GELU_A * x * x * x)))


def _gelu_grad(x):
    t = jnp.tanh(GELU_C * (x + GELU_A * x * x * x))
    return 0.5 * (1.0 + t) + 0.5 * x * (1.0 - t * t) * GELU_C * (1.0 + 3.0 * GELU_A * x * x)


def _lru_conv_fwd(gu, convw, convb):
    s = gu.shape[1]

    def kern(x_ref, w_ref, b_ref, u_ref):
        row = lax.broadcasted_iota(jnp.int32, (s, LANE), 0)
        xv = x_ref[...]
        u_ref[...] = (b_ref[...] + w_ref[3:4, :] * xv + w_ref[2:3, :] * _shift_down(xv, 1, row)
                      + w_ref[1:2, :] * _shift_down(xv, 2, row) + w_ref[0:1, :] * _shift_down(xv, 3, row))

    return _call(kern, name="lru_conv_fwd", grid=(NCH,),
                          in_specs=[_chunk_spec(s, LRU_NB), _vec_chunk(RG_K), _vec_chunk(1)], out_specs=_chunk_spec(s),
                          out_shape=jax.ShapeDtypeStruct((LRU_NB, s, LRU_BW), F32), compiler_params=_params(1))(gu, convw, convb)


def _lru_conv_bwd(dud, dug, gu, convw):
    s = gu.shape[1]

    def kern(d1_ref, d2_ref, x_ref, w_ref, dx_ref, dw_ref, db_ref):
        row = lax.broadcasted_iota(jnp.int32, (s, LANE), 0)
        du = d1_ref[...] + d2_ref[...]
        xv = x_ref[...]
        dx_ref[...] = (w_ref[3:4, :] * du + w_ref[2:3, :] * _shift_up(du, 1, row, s) + w_ref[1:2, :] * _shift_up(du, 2, row, s)
                       + w_ref[0:1, :] * _shift_up(du, 3, row, s)).astype(BF16)
        dw_ref[3:4, :] = jnp.sum(du * xv, axis=0, keepdims=True)
        for k in range(1, RG_K):
            dw_ref[3 - k:4 - k, :] = jnp.sum(du * _shift_down(xv, k, row), axis=0, keepdims=True)
        db_ref[...] = jnp.sum(du, axis=0, keepdims=True)

    return _call(kern, name="lru_conv_bwd", grid=(NCH,),
                          in_specs=[_chunk_spec(s), _chunk_spec(s), _chunk_spec(s, LRU_NB), _vec_chunk(RG_K)],
                          out_specs=[_chunk_spec(s), _vec_chunk(RG_K), _vec_chunk(1)],
                          out_shape=[jax.ShapeDtypeStruct((LRU_NB, s, LRU_BW), BF16),
                                     jax.ShapeDtypeStruct((RG_K, DM), F32), jax.ShapeDtypeStruct((1, DM), F32)],
                          compiler_params=_params(1))(dud, dug, gu, convw)


def _lru_gates(z_ref, bai_ref, lam_ref, uv):
    r = jax.nn.sigmoid(z_ref[0] + bai_ref[0:1, :])
    ig = jax.nn.sigmoid(z_ref[1] + bai_ref[1:2, :])
    sp = _softplus(-lam_ref[...])
    la = -RG_C * r * sp
    a = jnp.exp(la)
    sq = jnp.sqrt(_neg_expm1(2.0 * la))
    return r, ig, sp, a, sq


def _scan_steps(n):
    d, out = 1, []
    while d < n:
        out.append(d)
        d *= 2
    return out


def _lru_scan_fwd(z, bai, lam, u, gu):
    s = u.shape[1]
    zspec = BS((2, None, s, LANE), lambda ch: (0, ch // CH_PER_BLK, 0, ch % CH_PER_BLK))

    def kern(z_ref, bai_ref, lam_ref, u_ref, g_ref, hs_ref, y_ref):
        row = lax.broadcasted_iota(jnp.int32, (s, LANE), 0)
        uv = u_ref[...]
        _, ig, _, a, sq = _lru_gates(z_ref, bai_ref, lam_ref, uv)
        b = sq * (ig * uv)
        for d in _scan_steps(s):
            a_sh = jnp.where(row >= d, pltpu.roll(a, d, 0), 1.0)
            b = a * _shift_down(b, d, row) + b
            a = a * a_sh
        hs_ref[...] = b
        y_ref[...] = (_gelu(g_ref[...]) * b).astype(BF16)

    return _call(kern, name="lru_scan_fwd", grid=(NCH,),
                          in_specs=[zspec, _vec_chunk(2), _vec_chunk(1), _chunk_spec(s), _chunk_spec(s)],
                          out_specs=[_chunk_spec(s), BS((s, LANE), lambda ch: (0, ch))],
                          out_shape=[jax.ShapeDtypeStruct((LRU_NB, s, LRU_BW), F32), jax.ShapeDtypeStruct((s, DM), BF16)],
                          compiler_params=_params(1, VMEM_BIG))(z, bai, lam, u, gu)


def _lru_scan_bwd(dyp, z, bai, lam, u, gu, hs):
    s = u.shape[1]
    zspec = BS((2, None, s, LANE), lambda ch: (0, ch // CH_PER_BLK, 0, ch % CH_PER_BLK))

    def kern(dy_ref, z_ref, bai_ref, lam_ref, u_ref, g_ref, hs_ref, dg_ref, dz_ref, du_ref, dbai_ref, dlam_ref):
        row = lax.broadcasted_iota(jnp.int32, (s, LANE), 0)
        uv, gv, hv, dyv = u_ref[...], g_ref[...], hs_ref[...], dy_ref[...]
        r, ig, sp, a, sq = _lru_gates(z_ref, bai_ref, lam_ref, uv)
        dg_ref[...] = (dyv * hv * _gelu_grad(gv)).astype(BF16)
        g = dyv * _gelu(gv)
        an = _shift_up(a, 1, row, s)
        for d in _scan_steps(s):
            an_sh = jnp.where(row < s - d, pltpu.roll(an, s - d, 0), 1.0)
            g = an * _shift_up(g, d, row, s) + g
            an = an * an_sh
        da = g * _shift_down(hv, 1, row)
        dsq = g * (ig * uv)
        di = g * sq * uv
        du_ref[...] = g * sq * ig
        dla = da * a - dsq * (a * a / sq)
        dzr = dla * (-RG_C * sp) * r * (1.0 - r)
        dzi = di * ig * (1.0 - ig)
        dz_ref[0] = dzr.astype(BF16)
        dz_ref[1] = dzi.astype(BF16)
        dbai_ref[0:1, :] = jnp.sum(dzr, axis=0, keepdims=True)
        dbai_ref[1:2, :] = jnp.sum(dzi, axis=0, keepdims=True)
        dlam_ref[...] = jnp.sum(dla * r, axis=0, keepdims=True) * (RG_C * jax.nn.sigmoid(-lam_ref[...]))

    return _call(
        kern, name="lru_scan_bwd", grid=(NCH,),
        in_specs=[BS((s, LANE), lambda ch: (0, ch)), zspec, _vec_chunk(2), _vec_chunk(1), _chunk_spec(s), _chunk_spec(s),
                  _chunk_spec(s)],
        out_specs=[_chunk_spec(s), zspec, _chunk_spec(s), _vec_chunk(2), _vec_chunk(1)],
        out_shape=[jax.ShapeDtypeStruct((LRU_NB, s, LRU_BW), BF16), jax.ShapeDtypeStruct((2, LRU_NB, s, LRU_BW), BF16),
                   jax.ShapeDtypeStruct((LRU_NB, s, LRU_BW), F32), jax.ShapeDtypeStruct((2, DM), F32),
                   jax.ShapeDtypeStruct((1, DM), F32)],
        compiler_params=_params(1, VMEM_BIG))(dyp, z, bai, lam, u, gu, hs)


def _lru_layer_fwd(x, gpre, gpost, win, convw, convb, wai, bai, lam, wout, after=None):
    s = x.shape[0]
    tm = _tile(s, MM_TM)
    h = _norm_fwd(x, gpre, after)
    gu = _bmm_nn("lru_in", h, win)
    u = _lru_conv_fwd(gu, convw, convb)
    z = _mm("lru_gate", u, wai, grid=(2, LRU_NB, s // tm, 1),
            a_spec=BS((None, tm, LRU_BW), lambda k, n, i, r: (n, i, 0)),
            b_spec=BS((None, None, LRU_BW, LRU_BW), lambda k, n, i, r: (k, n, 0, 0)),
            o_spec=BS((None, None, tm, LRU_BW), lambda k, n, i, r: (k, n, i, 0)),
            out_shape=(2, LRU_NB, s, LRU_BW), dn=NN)
    hs, yp = _lru_scan_fwd(z, bai, lam, u, gu)
    y = _mm_nn("lru_out", yp, wout)
    return _norm_res(x, y, gpost), (x, h, gu, u, z, hs, yp, y)


def _lru_layer_bwd(dxo, saved, gpre, gpost, win, convw, convb, wai, bai, lam, wout, after=None):
    x, h, gu, u, z, hs, yp, y = saved
    s = x.shape[0]
    tm = _tile(s, MM_TM)
    dy, dgpost = _norm_bwd(y, dxo, gpost, None, BF16, after)
    dyp = _mm_nt("lru_dyp", dy, wout)
    dwout = _mm_tn("lru_dwout", yp, dy)
    dgate, dz, dud, dbai, dlam = _lru_scan_bwd(dyp, z, bai, lam, u, gu, hs)
    dwai = _mm("lru_dwai", u, dz, grid=(2, LRU_NB, s // tm),
               a_spec=BS((None, tm, LRU_BW), lambda k, n, r: (n, r, 0)),
               b_spec=BS((None, None, tm, LRU_BW), lambda k, n, r: (k, n, r, 0)),
               o_spec=BS((None, None, LRU_BW, LRU_BW), lambda k, n, r: (k, n, 0, 0)),
               out_shape=(2, LRU_NB, LRU_BW, LRU_BW), dn=TN)
    dug = _mm("lru_dug", dz, wai, grid=(LRU_NB, s // tm, 2),
              a_spec=BS((None, None, tm, LRU_BW), lambda n, i, k: (k, n, i, 0)),
              b_spec=BS((None, None, LRU_BW, LRU_BW), lambda n, i, k: (k, n, 0, 0)),
              o_spec=BS((None, tm, LRU_BW), lambda n, i, k: (n, i, 0)),
              out_shape=(LRU_NB, s, LRU_BW), dn=NT)
    duraw, dconvw, dconvb = _lru_conv_bwd(dud, dug, gu, convw)
    dgu = jnp.concatenate([dgate, duraw], axis=0)
    dwin = _bmm_tn("lru_dwin", h, dgu)
    dh = _bmm_nt_sum("lru_dh", dgu, win)
    dx, dgpre = _norm_bwd(x, dh, gpre, dxo, F32)
    return dx, dgpre, dgpost, dwin, dconvw, dconvb, dwai, dbai, dlam, dwout


CHIP_FLIPS = ((1, 0), (0, 1), (1, 1))


def _place():
    return lax.axis_index("x"), lax.axis_index("y"), lax.axis_index("c")


def _flip(v, f):
    return 1 - v if f else v


def _comm_params():
    return pltpu.CompilerParams(vmem_limit_bytes=VMEM_LIMIT)


def _small_gather(v):
    def body(v_ref, o_ref, send_sems, recv_sems, local_sem):
        x, y, c = _place()
        mine = 4 * x + 2 * y + c
        local = pltpu.make_async_copy(v_ref, o_ref.at[mine], local_sem)
        local.start()
        sends = []
        for k in range(1, NDEV):
            fx, fy, fc = (k >> 2) & 1, (k >> 1) & 1, k & 1
            sends.append(pltpu.make_async_remote_copy(
                src_ref=v_ref, dst_ref=o_ref.at[mine], send_sem=send_sems.at[k - 1], recv_sem=recv_sems.at[k - 1],
                device_id=(_flip(x, fx), _flip(y, fy), _flip(c, fc)), device_id_type=MESH))
        for cp in sends:
            cp.start()
        for k in range(1, NDEV):
            fx, fy, fc = (k >> 2) & 1, (k >> 1) & 1, k & 1
            src = 4 * _flip(x, fx) + 2 * _flip(y, fy) + _flip(c, fc)
            pltpu.make_async_remote_copy(src_ref=v_ref, dst_ref=o_ref.at[src], send_sem=send_sems.at[k - 1],
                                         recv_sem=recv_sems.at[k - 1], device_id=(x, y, c), device_id_type=MESH).wait_recv()
        for cp in sends:
            cp.wait_send()
        local.wait()

    return pl.pallas_call(body, name="small_gather", in_specs=[ANY], out_specs=ANY,
                          out_shape=jax.ShapeDtypeStruct((NDEV,) + v.shape, v.dtype),
                          scratch_shapes=[pltpu.SemaphoreType.DMA((NDEV - 1,)), pltpu.SemaphoreType.DMA((NDEV - 1,)),
                                          pltpu.SemaphoreType.DMA],
                          compiler_params=_comm_params())(v)


REL_CHIPS = ((0, 0),) + CHIP_FLIPS


def _rs_d2d(g5s, after=None):
    n = len(g5s)
    extra = () if after is None else (after,)

    def body(*refs):
        ins, gots = refs[:n], refs[n + len(extra):2 * n + len(extra)]
        send_sems, recv_sems = refs[2 * n + len(extra):]
        x, y, c = _place()
        copies = []
        for t in range(n):
            for f, (fx, fy) in enumerate(REL_CHIPS):
                copies.append(pltpu.make_async_remote_copy(
                    src_ref=ins[t].at[_flip(x, fx), _flip(y, fy), 1 - c], dst_ref=gots[t].at[f],
                    send_sem=send_sems.at[4 * t + f], recv_sem=recv_sems.at[4 * t + f], device_id=(x, y, 1 - c),
                    device_id_type=MESH))
        for cp in copies:
            cp.start()
        for cp in copies:
            cp.wait()

    out = [jax.ShapeDtypeStruct((4,) + g.shape[3:], F32) for g in g5s]
    return pl.pallas_call(body, name="rs_d2d", in_specs=[ANY] * (n + len(extra)), out_specs=[ANY] * n, out_shape=out,
                          scratch_shapes=[pltpu.SemaphoreType.DMA((4 * n,)), pltpu.SemaphoreType.DMA((4 * n,))],
                          compiler_params=_comm_params())(*g5s, *extra)


HBM = pl.BlockSpec(memory_space=pltpu.HBM)
SEM = pl.BlockSpec(memory_space=pltpu.SEMAPHORE)
EFFECT = pltpu.SideEffectType.DATAFLOW_SIDE_EFFECTING


def _in_hbm(a):
    return pltpu.with_memory_space_constraint(a, pltpu.HBM)


def _rs_ici_copies(ins, lands, send_sems, recv_sems):
    x, y, c = _place()
    return [pltpu.make_async_remote_copy(
        src_ref=ins[t].at[f], dst_ref=lands[t].at[f], send_sem=send_sems.at[3 * t + f], recv_sem=recv_sems.at[3 * t + f],
        device_id=(_flip(x, fx), _flip(y, fy), c), device_id_type=MESH)
        for t in range(len(ins)) for f, (fx, fy) in enumerate(CHIP_FLIPS)]


def _rs_ici_start(parts, name):
    n = len(parts)

    def body(*refs):
        ins, lands = refs[:n], refs[n:2 * n]
        send_sems, recv_sems = refs[2 * n], refs[2 * n + 1]
        token = refs[-1]
        for cp in _rs_ici_copies(ins, lands, send_sems, recv_sems):
            cp.start()
        token[...] = jnp.zeros(token.shape, token.dtype)

    thru = [pltpu.HBM(p.shape, p.dtype) for p in parts]
    res = pl.pallas_call(
        body, name=name, in_specs=[HBM] * (2 * n),
        out_shape=(pltpu.SemaphoreType.DMA((3 * n,)), pltpu.SemaphoreType.DMA((3 * n,)), *thru, *thru,
                   jax.ShapeDtypeStruct((8, LANE), F32)),
        out_specs=(SEM, SEM, *([HBM] * (2 * n)), pl.BlockSpec(memory_space=pltpu.VMEM)),
        input_output_aliases={i: 2 + i for i in range(2 * n)},
        compiler_params=pltpu.CompilerParams(has_side_effects=EFFECT),
    )(*[_in_hbm(p) for p in parts], *[_in_hbm(lax.empty(p.shape, p.dtype)) for p in parts])
    return res[:-1], res[-1]


def _rs_ici_wait(state, after, name):
    n = (len(state) - 2) // 2

    def body(*refs):
        send_sems, recv_sems = refs[0], refs[1]
        ins, lands = refs[2:2 + n], refs[2 + n:2 + 2 * n]
        for cp in _rs_ici_copies(ins, lands, send_sems, recv_sems):
            cp.wait_send()
            cp.wait_recv()

    thru = [pltpu.HBM(s.shape, s.dtype) for s in state[2:]]
    res = pl.pallas_call(
        body, name=name, in_specs=[SEM, SEM] + [HBM] * (2 * n) + [ANY], out_shape=tuple(thru),
        out_specs=tuple([HBM] * (2 * n)), input_output_aliases={2 + i: i for i in range(2 * n)},
        compiler_params=pltpu.CompilerParams(has_side_effects=EFFECT),
    )(*state, after)
    return list(res[n:])


def _ag_copies(shards, lands, send_sems, recv_sems):
    x, y, c = _place()
    mine = 4 * x + 2 * y + c
    peers = [(x, y, 1 - c)] + [(_flip(x, fx), _flip(y, fy), c) for fx, fy in CHIP_FLIPS]
    return [pltpu.make_async_remote_copy(
        src_ref=shards[t], dst_ref=lands[t].at[mine], send_sem=send_sems.at[4 * t + k], recv_sem=recv_sems.at[4 * t + k],
        device_id=peer, device_id_type=MESH) for t in range(len(shards)) for k, peer in enumerate(peers)]


def _ag_start(shards, after, name):
    n = len(shards)

    def body(*refs):
        ins, lands = refs[:n], refs[n:2 * n]
        send_sems, recv_sems = refs[2 * n + 1], refs[2 * n + 2]
        token = refs[-1]
        for cp in _ag_copies(ins, lands, send_sems, recv_sems):
            cp.start()
        token[...] = jnp.zeros(token.shape, token.dtype)

    thru = [pltpu.HBM(s.shape, s.dtype) for s in shards]
    land = [pltpu.HBM((NDEV,) + s.shape, s.dtype) for s in shards]
    res = pl.pallas_call(
        body, name=name, in_specs=[HBM] * (2 * n) + [ANY],
        out_shape=(pltpu.SemaphoreType.DMA((4 * n,)), pltpu.SemaphoreType.DMA((4 * n,)), *thru, *land,
                   jax.ShapeDtypeStruct((8, LANE), F32)),
        out_specs=(SEM, SEM, *([HBM] * (2 * n)), pl.BlockSpec(memory_space=pltpu.VMEM)),
        input_output_aliases={i: 2 + i for i in range(2 * n)},
        compiler_params=pltpu.CompilerParams(has_side_effects=EFFECT),
    )(*[_in_hbm(s) for s in shards], *[_in_hbm(lax.empty((NDEV,) + s.shape, s.dtype)) for s in shards], after)
    return res[:-1], res[-1]


def _ag_wait(state, after, name):
    n = (len(state) - 2) // 2

    def body(*refs):
        send_sems, recv_sems = refs[0], refs[1]
        ins, lands = refs[2:2 + n], refs[2 + n:2 + 2 * n]
        for cp in _ag_copies(ins, lands, send_sems, recv_sems):
            cp.wait_send()
            cp.wait_recv()

    thru = [pltpu.HBM(s.shape, s.dtype) for s in state[2:]]
    res = pl.pallas_call(
        body, name=name, in_specs=[SEM, SEM] + [HBM] * (2 * n) + [ANY], out_shape=tuple(thru),
        out_specs=tuple([HBM] * (2 * n)), input_output_aliases={2 + i: i for i in range(2 * n)},
        compiler_params=pltpu.CompilerParams(has_side_effects=EFFECT),
    )(*state, after)
    return list(res[:n]), list(res[n:])


def _ag_finish(shards, lands):
    n = len(shards)

    def body(*refs):
        ins, outs, stage = refs[:n], refs[2 * n:3 * n], refs[3 * n:4 * n]
        send_sems, recv_sems, local_sems = refs[4 * n:]
        x, y, c = _place()
        chips = [(_flip(x, fx), _flip(y, fy)) for fx, fy in CHIP_FLIPS]

        def passing(t, j, core, to):
            blk = outs[t].at[4 * chips[j][0] + 2 * chips[j][1] + core]
            return pltpu.make_async_remote_copy(src_ref=blk, dst_ref=blk, send_sem=send_sems.at[3 * t + j],
                                                recv_sem=recv_sems.at[3 * t + j], device_id=to, device_id_type=MESH)

        sends = [passing(t, j, c, (x, y, 1 - c)) for t in range(n) for j in range(3)]
        for cp in sends:
            cp.start()
        load = [pltpu.make_async_copy(ins[t], stage[t], local_sems.at[t]) for t in range(n)]
        mine = [pltpu.make_async_copy(stage[t], outs[t].at[4 * x + 2 * y + c], local_sems.at[t]) for t in range(n)]
        for cp in load:
            cp.start()
        for t in range(n):
            load[t].wait()
            mine[t].start()
        for t in range(n):
            for j in range(3):
                passing(t, j, 1 - c, (x, y, c)).wait_recv()
        for cp in sends:
            cp.wait_send()
        for cp in mine:
            cp.wait()

    return pl.pallas_call(
        body, name="ag_finish", in_specs=[ANY] * (2 * n), out_specs=[ANY] * n,
        out_shape=[jax.ShapeDtypeStruct(l.shape, l.dtype) for l in lands],
        input_output_aliases={n + i: i for i in range(n)},
        scratch_shapes=[pltpu.VMEM(s.shape, s.dtype) for s in shards]
        + [pltpu.SemaphoreType.DMA((3 * n,)), pltpu.SemaphoreType.DMA((3 * n,)), pltpu.SemaphoreType.DMA((n,))],
        compiler_params=_comm_params())(*shards, *lands)


def _row_tile(rows, largest=256):
    for t in (1024, 512, 256, 128, 64, 32, 16, 8):
        if t > largest:
            continue
        if rows % t == 0:
            return t
    return rows


def _rs_chip_sum(pos, g5, got):
    a, b = g5.shape[3:]
    ta = _row_tile(a, 1024)

    def kern(pos_ref, o_ref, g_ref, p_ref):
        p_ref[...] = (o_ref[...] + g_ref[...]).astype(BF16)

    def mine(f, i, pos_ref):
        return (pos_ref[0] ^ ((f + 1) & 1), pos_ref[1] ^ ((f + 1) >> 1), pos_ref[2], i, 0)

    spec = pltpu.PrefetchScalarGridSpec(
        num_scalar_prefetch=1, grid=(3, a // ta),
        in_specs=[BS((None, None, None, ta, b), mine), BS((None, ta, b), lambda f, i, pos_ref: (f + 1, i, 0))],
        out_specs=BS((None, ta, b), lambda f, i, pos_ref: (f, i, 0)))
    return _call(kern, name="rs_chip_sum", grid_spec=spec, out_shape=jax.ShapeDtypeStruct((3, a, b), BF16),
                          compiler_params=_params(2))(pos, g5, got)


def _rs_final_sum(pos, g5, got, recv):
    a, b = g5.shape[3:]
    ta = _row_tile(a, 1024)

    def kern(pos_ref, o_ref, g_ref, r_ref, s_ref):
        acc = o_ref[...] + g_ref[...]
        for f in range(3):
            acc = acc + r_ref[f].astype(F32)
        s_ref[...] = acc

    spec = pltpu.PrefetchScalarGridSpec(
        num_scalar_prefetch=1, grid=(a // ta,),
        in_specs=[BS((None, None, None, ta, b), lambda i, pos_ref: (pos_ref[0], pos_ref[1], pos_ref[2], i, 0)),
                  BS((None, ta, b), lambda i, pos_ref: (0, i, 0)), BS((3, ta, b), lambda i, pos_ref: (0, i, 0))],
        out_specs=BS((ta, b), lambda i, pos_ref: (i, 0)))
    return _call(kern, name="rs_final_sum", grid_spec=spec, out_shape=jax.ShapeDtypeStruct((a, b), F32),
                          compiler_params=_params(1))(pos, g5, got, recv)


def _rs_d2d_copies(ins, lands, send_sems, recv_sems):
    x, y, c = _place()
    return [pltpu.make_async_remote_copy(
        src_ref=ins[t].at[_flip(x, fx), _flip(y, fy), 1 - c], dst_ref=lands[t].at[f], send_sem=send_sems.at[4 * t + f],
        recv_sem=recv_sems.at[4 * t + f], device_id=(x, y, 1 - c), device_id_type=MESH)
        for t in range(len(ins)) for f, (fx, fy) in enumerate(REL_CHIPS)]


def _rs_d2d_start(g5s, name):
    n = len(g5s)

    def body(*refs):
        ins, lands = refs[:n], refs[n:2 * n]
        for cp in _rs_d2d_copies(ins, lands, refs[2 * n], refs[2 * n + 1]):
            cp.start()
        refs[-1][...] = jnp.zeros(refs[-1].shape, F32)

    thru = [pltpu.HBM(g.shape, g.dtype) for g in g5s]
    land = [pltpu.HBM((4,) + g.shape[3:], F32) for g in g5s]
    res = pl.pallas_call(
        body, name=name, in_specs=[HBM] * (2 * n),
        out_shape=(pltpu.SemaphoreType.DMA((4 * n,)), pltpu.SemaphoreType.DMA((4 * n,)), *thru, *land,
                   jax.ShapeDtypeStruct((8, LANE), F32)),
        out_specs=(SEM, SEM, *([HBM] * (2 * n)), pl.BlockSpec(memory_space=pltpu.VMEM)),
        input_output_aliases={i: 2 + i for i in range(2 * n)},
        compiler_params=pltpu.CompilerParams(has_side_effects=EFFECT),
    )(*[_in_hbm(g) for g in g5s], *[_in_hbm(lax.empty((4,) + g.shape[3:], F32)) for g in g5s])
    return res[:-1], res[-1]


def _rs_d2d_wait(state, after, name):
    n = (len(state) - 2) // 2

    def body(*refs):
        ins, lands = refs[2:2 + n], refs[2 + n:2 + 2 * n]
        for cp in _rs_d2d_copies(ins, lands, refs[0], refs[1]):
            cp.wait_send()
            cp.wait_recv()

    thru = [pltpu.HBM(s.shape, s.dtype) for s in state[2:]]
    res = pl.pallas_call(
        body, name=name, in_specs=[SEM, SEM] + [HBM] * (2 * n) + [ANY], out_shape=tuple(thru),
        out_specs=tuple([HBM] * (2 * n)), input_output_aliases={2 + i: i for i in range(2 * n)},
        compiler_params=pltpu.CompilerParams(has_side_effects=EFFECT),
    )(*state, after)
    return list(res[:n]), list(res[n:])


def _as_g5(grads):
    return [g.reshape((2, 2, 2) + g.shape[1:]) for g in grads]


def _rs_mid(g5s, gots, pos, tag):
    parts = [_rs_chip_sum(pos, g, got) for g, got in zip(g5s, gots)]
    state, token = _rs_ici_start(parts, "rs_ici_start_" + tag)
    return (g5s, gots, state, tag), token


def _rs_begin(grads, pos, tag, after=None):
    g5s = _as_g5(grads)
    return _rs_mid(g5s, _rs_d2d(g5s, after), pos, tag)


def _rs_end(pending, after, pos):
    g5s, gots, state, tag = pending
    recvs = _rs_ici_wait(state, after, "rs_ici_wait_" + tag)
    return [_rs_final_sum(pos, g, got, r) for g, got, r in zip(g5s, gots, recvs)]


def _sum_devices(v):
    _, r, _ = v.shape

    def kern(v_ref, o_ref):
        acc = v_ref[0]
        for d in range(1, NDEV):
            acc = acc + v_ref[d]
        o_ref[...] = acc

    return _call(kern, name="sum_devices", out_shape=jax.ShapeDtypeStruct((r, LANE), F32),
                          compiler_params=_comm_params())(v)


def _loss_head(xf, target):
    s = xf.shape[0]
    tm = _tile(s, 512)

    def kern(x_ref, t_ref, dx_ref, l_ref):
        err = x_ref[...] - t_ref[...]
        dx_ref[...] = err * (1.0 / DM)
        part = jnp.broadcast_to(0.5 * jnp.sum(jnp.mean(err * err, axis=-1, keepdims=True), axis=0, keepdims=True), (8, LANE))

        @pl.when(pl.program_id(0) == 0)
        def _():
            l_ref[...] = part

        @pl.when(pl.program_id(0) > 0)
        def _():
            l_ref[...] += part

    row = BS((tm, DM), lambda i: (i, 0))
    return _call(kern, name="loss_head", grid=(s // tm,), in_specs=[row, row],
                          out_specs=[row, BS((8, LANE), lambda i: (0, 0))],
                          out_shape=[jax.ShapeDtypeStruct((s, DM), F32), jax.ShapeDtypeStruct((8, LANE), F32)],
                          compiler_params=_params(1))(xf, target)


def _adamw(w, g, m, v, after=None):
    rows, cols = w.shape
    tr = _row_tile(rows)
    extra = () if after is None else (after,)

    def kern(w_ref, g_ref, m_ref, v_ref, *rest):
        d_ref, nm_ref, nv_ref = rest[-3:]
        gv = g_ref[...]
        nm = ADAM_B1 * m_ref[...] + (1.0 - ADAM_B1) * gv
        nv = ADAM_B2 * v_ref[...] + (1.0 - ADAM_B2) * (gv * gv)
        m_hat = nm / (1.0 - ADAM_B1 ** ADAM_STEP)
        v_hat = nv / (1.0 - ADAM_B2 ** ADAM_STEP)
        d_ref[...] = -ADAM_LR * (m_hat / (jnp.sqrt(v_hat) + ADAM_EPS) + ADAM_WD * w_ref[...])
        nm_ref[...] = nm
        nv_ref[...] = nv

    blk = BS((tr, cols), lambda i: (i, 0))
    shp = jax.ShapeDtypeStruct((rows, cols), F32)
    return _call(kern, name="adamw", grid=(rows // tr,), in_specs=[blk] * 4 + [ANY] * len(extra),
                          out_specs=[blk] * 3, out_shape=[shp] * 3, compiler_params=_params(1))(w, g, m, v, *extra)


def _adamw_nd(w, g, m, v, after=None):
    shape = w.shape
    two = (math.prod(shape[:-1]), shape[-1])
    return tuple(o.reshape(shape)
                 for o in _adamw(w.reshape(two), g.reshape(two), m.reshape(two), v.reshape(two), after))


def _pack_small(parts):
    flat = jnp.concatenate([p.reshape(-1) for p in parts])
    pad = (-flat.shape[0]) % (8 * LANE)
    return jnp.pad(flat, (0, pad)).reshape(-1, LANE)


def _unpack_small(packed, shapes, lead=()):
    flat = packed.reshape(lead + (-1,))
    out, off = [], 0
    for shp in shapes:
        n = math.prod(shp)
        out.append(flat[..., off:off + n].reshape(lead + tuple(shp)))
        off += n
    return out


WEIGHT_NAMES = ('g_mix_pre', 'g_mix_post', 'g_cross_pre', 'g_mem', 'g_cross_post', 'g_ffn_pre', 'g_ffn_post', 'w_xq',
                'w_xkv', 'w_xo', 'w_ffn_gu', 'w_ffn_down', 'ab_w_in', 'ab_b_f', 'ab_conv_w', 'ab_w_out', 'c_w_in',
                'c_conv_w', 'c_conv_b', 'c_w_a', 'c_b_a', 'c_w_i', 'c_b_i', 'c_lam', 'c_w_out')
BIG = ('w_xq', 'w_xkv', 'w_xo', 'w_ffn_gu', 'w_ffn_down', 'ab_w_in', 'ab_w_out', 'c_w_in', 'c_w_a', 'c_w_i', 'c_w_out')
SMALL_SHARDED = ('ab_conv_w', 'c_conv_w', 'c_conv_b', 'c_b_a', 'c_b_i', 'c_lam')
REPLICATED = ('g_mix_pre', 'g_mix_post', 'g_cross_pre', 'g_mem', 'g_cross_post', 'g_ffn_pre', 'g_ffn_post', 'ab_b_f')


def _small_full(name, gathered):
    nd = gathered.ndim
    return jnp.moveaxis(gathered, 0, nd - 2).reshape(gathered.shape[1:-1] + (NDEV * gathered.shape[-1],))


def _small_shard(full, dev):
    c = full.shape[-1] // NDEV
    return lax.dynamic_slice_in_dim(full, dev * c, c, axis=full.ndim - 1)


def kernel(x, mem, g_mix_pre, g_mix_post, g_cross_pre, g_mem, g_cross_post, g_ffn_pre, g_ffn_post, w_xq, w_xkv, w_xo, w_ffn_gu, w_ffn_down, ab_w_in, ab_b_f, ab_conv_w, ab_w_out, c_w_in, c_conv_w, c_conv_b, c_w_a, c_b_a, c_w_i, c_b_i, c_lam, c_w_out, loss_target, m_g_mix_pre, m_g_mix_post, m_g_cross_pre, m_g_mem, m_g_cross_post, m_g_ffn_pre, m_g_ffn_post, m_w_xq, m_w_xkv, m_w_xo, m_w_ffn_gu, m_w_ffn_down, m_ab_w_in, m_ab_b_f, m_ab_conv_w, m_ab_w_out, m_c_w_in, m_c_conv_w, m_c_conv_b, m_c_w_a, m_c_b_a, m_c_w_i, m_c_b_i, m_c_lam, m_c_w_out, v_g_mix_pre, v_g_mix_post, v_g_cross_pre, v_g_mem, v_g_cross_post, v_g_ffn_pre, v_g_ffn_post, v_w_xq, v_w_xkv, v_w_xo, v_w_ffn_gu, v_w_ffn_down, v_ab_w_in, v_ab_b_f, v_ab_conv_w, v_ab_w_out, v_c_w_in, v_c_conv_w, v_c_conv_b, v_c_w_a, v_c_b_a, v_c_w_i, v_c_b_i, v_c_lam, v_c_w_out):
    args = locals()
    w = {n: args[n] for n in WEIGHT_NAMES}
    mom = {n: args["m_" + n] for n in WEIGHT_NAMES}
    var = {n: args["v_" + n] for n in WEIGHT_NAMES}
    for t in (w, mom, var):
        t['w_ffn_gu'] = t['w_ffn_gu'].transpose(0, 2, 1)
    ab_t = [t['ab_w_in'].transpose(2, 0, 1) for t in (w, mom, var)]
    pos = jnp.stack([lax.axis_index("x"), lax.axis_index("y"), lax.axis_index("c")]).astype(jnp.int32)
    dev = 4 * pos[0] + 2 * pos[1] + pos[2]
    xs, mems, target = x[0], mem[0], loss_target[0]
    n_even, n_odd = (DEPTH + 1) // 2, DEPTH // 2

    small_shapes = [w[n].shape for n in SMALL_SHARDED]
    small_w_all = _small_gather(_pack_small([w[n] for n in SMALL_SHARDED]))
    gathered_small = _unpack_small(small_w_all, small_shapes, (NDEV,))
    small = {n: _small_full(n, g) for n, g in zip(SMALL_SHARDED, gathered_small)}
    ab_bfb = jnp.broadcast_to(ab_b_f[:, :, None], (n_even, FOX_H, LANE))
    c_bai = jnp.stack([small['c_b_a'].reshape(n_odd, DM), small['c_b_i'].reshape(n_odd, DM)], axis=1)
    row = lambda a, l: a[l][None]

    REST = ('w_xq', 'w_xkv', 'w_xo', 'w_ffn_gu', 'w_ffn_down')

    def mixer_names(l):
        return ('ab_w_in', 'ab_w_out') if l % 2 == 0 else ('c_w_in', 'c_w_a', 'c_w_i', 'c_w_out')

    def shards_of(l, names):
        out = []
        for n in names:
            if n == 'ab_w_in':
                s = ab_t[0][:, l // 2].astype(BF16)
            else:
                s = w[n][l if w[n].shape[0] == DEPTH else l // 2].astype(BF16)
            out.append(s.reshape(-1, s.shape[-1]))
        return out

    def mixer_weights(l, full):
        if l % 2 == 0:
            e = l // 2
            return (row(g_mix_pre, l), row(g_mix_post, l), _ab_pack(full['ab_w_in'].reshape(AB_IN, DM)), ab_bfb[e],
                    small['ab_conv_w'][e], full['ab_w_out'].reshape(DM, DM))
        o = l // 2
        gate_w = lambda g: g.reshape(NDEV, LRU_NB, LRU_BW // NDEV, LRU_BW).transpose(1, 0, 2, 3).reshape(
            LRU_NB, LRU_BW, LRU_BW)
        return (row(g_mix_pre, l), row(g_mix_post, l), full['c_w_in'], small['c_conv_w'][o], row(small['c_conv_b'], o),
                jnp.stack([gate_w(full['c_w_a']), gate_w(full['c_w_i'])]), c_bai[o], row(small['c_lam'], o),
                full['c_w_out'].reshape(DM, DM))

    def rest_weights(l, full):
        cross = (row(g_cross_pre, l), row(g_mem, l), row(g_cross_post, l), full['w_xq'].reshape(DM, DM), full['w_xkv'],
                 full['w_xo'].reshape(DM, DM))
        ffn = (row(g_ffn_pre, l), row(g_ffn_post, l), full['w_ffn_gu'], full['w_ffn_down'].reshape(D_FF, DM))
        return cross, ffn

    def gathered(state, names, after, tag):
        shards, lands = _ag_wait(state, after, "ag_wait_" + tag)
        full = _ag_finish(shards, lands)
        return dict(zip(names, full)), full[0]

    saved, weights = [], []
    h = xs
    names_of = lambda l: mixer_names(l) + REST
    states = {}
    st_m, _ = _ag_start(shards_of(0, mixer_names(0)), small_w_all, "ag_start_0m")
    st_r, _ = _ag_start(shards_of(0, REST), st_m[2], "ag_start_0r")
    states[1], token = _ag_start(shards_of(1, names_of(1)), st_r[2], "ag_start_1")
    full_m, _ = gathered(st_m, mixer_names(0), xs, "0m")
    for l in range(DEPTH):
        if l > 0:
            full, done = gathered(states[l], names_of(l), h, str(l))
            full_m = full_r = full
            token = None
            if l + 2 < DEPTH:
                states[l + 2], token = _ag_start(shards_of(l + 2, names_of(l + 2)), done, "ag_start_%d" % (l + 2))
        mixer = mixer_weights(l, full_m)
        h, s_mix = (_fox_layer_fwd if l % 2 == 0 else _lru_layer_fwd)(h, *mixer, after=token)
        token = None
        if l == 0:
            full_r, done = gathered(st_r, REST, h, "0r")
            states[2], token = _ag_start(shards_of(2, names_of(2)), done, "ag_start_2")
        cross, ffn = rest_weights(l, full_r)
        h, s_cross = _cross_fwd(h, mems, *cross, after=token)
        h, s_ffn = _ffn_fwd(h, *ffn)
        saved.append((s_mix, s_cross, s_ffn))
        weights.append((mixer, cross, ffn))
    mixer_args = lambda l: weights[l][0]
    cross_args = lambda l: weights[l][1]
    ffn_args = lambda l: weights[l][2]
    dx, loss_rep = _loss_head(h, target)
    loss = lax.psum(loss_rep[0, 0], ("x", "y", "c"))

    grads = {n: [None] * w[n].shape[0] for n in BIG}
    partial = {n: [None] * w[n].shape[0] for n in REPLICATED + SMALL_SHARDED}
    def finish(pending, after):
        state, names, where = pending
        for n, g in zip(names, _rs_end(state, after, pos)):
            grads[n][where[n]] = g

    def unit(layer, names):
        return [layer[n][1] for n in names], names, {n: layer[n][0] for n in names}

    d2d = ici = None
    token = None
    for l in reversed(range(DEPTH)):
        s_mix, s_cross, s_ffn = saved[l]
        dx, partial['g_ffn_pre'][l], partial['g_ffn_post'][l], dwgu, dwd = _ffn_bwd(dx, s_ffn, *ffn_args(l), after=token)
        token = None
        if d2d is not None:
            g5s, gots = _rs_d2d_wait(d2d[0], dx, "rs_d2d_wait_%d" % (l + 1))
            state, token = _rs_mid(g5s, gots, pos, str(l + 1))
            ici, d2d = (state,) + d2d[1:], None
        (dx, partial['g_cross_pre'][l], partial['g_mem'][l], partial['g_cross_post'][l], dwq, dwkv, dwo) = _cross_bwd(
            dx, s_cross, mems, *cross_args(l), after=token)
        token = None
        layer = {'w_xq': (l, dwq.reshape(NDEV, DM // NDEV, DM)), 'w_xkv': (l, dwkv), 'w_xo': (l, dwo.reshape(NDEV, DM // NDEV, DM)),
                 'w_ffn_gu': (l, dwgu), 'w_ffn_down': (l, dwd.reshape(NDEV, D_FF // NDEV, DM))}
        if l == 0:
            gs, names, where = unit(layer, REST)
            state, token = _rs_begin(gs, pos, "0r")
            ici_rest = (state, names, where)
        if l % 2 == 0:
            e = l // 2
            (dx, partial['g_mix_pre'][l], partial['g_mix_post'][l], dwall, partial['ab_b_f'][e], partial['ab_conv_w'][e],
             dwout) = _fox_layer_bwd(dx, s_mix, *mixer_args(l), after=token)
            layer['ab_w_in'] = (e, _ab_unpack(dwall).reshape(NDEV, AB_IN // NDEV, DM))
            layer['ab_w_out'] = (e, dwout.reshape(NDEV, DM // NDEV, DM))
        else:
            o = l // 2
            (dx, partial['g_mix_pre'][l], partial['g_mix_post'][l], dwin, partial['c_conv_w'][o], dconvb, dwai, dbai, dlam,
             dwout) = _lru_layer_bwd(dx, s_mix, *mixer_args(l), after=token)
            partial['c_conv_b'][o], partial['c_lam'][o] = dconvb[0], dlam[0]
            partial['c_b_a'][o], partial['c_b_i'][o] = dbai[0].reshape(LRU_NB, LRU_BW), dbai[1].reshape(LRU_NB, LRU_BW)
            rows = LRU_BW // NDEV
            by_dev = lambda d: d.reshape(LRU_NB, NDEV, rows, LRU_BW).transpose(1, 0, 2, 3).reshape(NDEV, LRU_NB * rows, LRU_BW)
            layer['c_w_in'] = (o, dwin)
            layer['c_w_a'] = (o, by_dev(dwai[0]))
            layer['c_w_i'] = (o, by_dev(dwai[1]))
            layer['c_w_out'] = (o, dwout.reshape(NDEV, DM // NDEV, DM))
        token = None
        if ici is not None:
            finish(ici, dx)
            ici = None
        if l > 0:
            gs, names, where = unit(layer, list(layer))
            state, token = _rs_d2d_start(_as_g5(gs), "rs_d2d_start_%d" % l)
            d2d = (state, names, where)
    small_names = REPLICATED + SMALL_SHARDED
    small_parts = [jnp.stack([p.reshape(w[n].shape[1:] if n in REPLICATED else small[n].shape[1:]) for p in partial[n]])
                   for n in small_names]
    small_all = _small_gather(_pack_small(small_parts))
    reduced = _unpack_small(_sum_devices(small_all), [p.shape for p in small_parts])
    grad = {}
    for n, g in zip(small_names, reduced):
        grad[n] = g if n in REPLICATED else _small_shard(g, dev)

    gs, names, where = unit(layer, mixer_names(0))
    state, token = _rs_begin(gs, pos, "0m", after=small_all)
    ici_mixer = (state, names, where)
    finish(ici_rest, dx)

    delta, new_m, new_v = {}, {}, {}
    last = mixer_names(0)
    for n in BIG:
        if n not in last:
            grad[n] = jnp.stack(grads[n]).reshape(w[n].shape)
            delta[n], new_m[n], new_v[n] = _adamw_nd(w[n], grad[n], mom[n], var[n], token)
            token = delta[n]
    shapes = [w[n].shape for n in small_names]
    packed = [_pack_small([t[n] for n in small_names]) for t in (w, grad, mom, var)]
    res_small = _adamw(*packed, after=token)
    for res, out in zip(res_small, (delta, new_m, new_v)):
        for n, val in zip(small_names, _unpack_small(res, shapes)):
            out[n] = val
    finish(ici_mixer, res_small[0])
    for n in last:
        if n == 'ab_w_in':
            g_t = jnp.stack(grads[n], axis=1)
            res = (g_t,) + _adamw_nd(ab_t[0], g_t, ab_t[1], ab_t[2])
            grad[n], delta[n], new_m[n], new_v[n] = (r.transpose(1, 2, 0) for r in res)
            continue
        grad[n] = jnp.stack(grads[n]).reshape(w[n].shape)
        delta[n], new_m[n], new_v[n] = _adamw_nd(w[n], grad[n], mom[n], var[n])

    for t in (grad, delta, new_m, new_v):
        t['w_ffn_gu'] = t['w_ffn_gu'].transpose(0, 2, 1)
    return (loss, dx[None], *[grad[n] for n in WEIGHT_NAMES], *[delta[n] for n in WEIGHT_NAMES],
            *[new_m[n] for n in WEIGHT_NAMES], *[new_v[n] for n in WEIGHT_NAMES])
```

```python
import math

import jax
import jax.numpy as jnp
from jax import lax
from jax.experimental import pallas as pl
from jax.experimental.pallas import tpu as pltpu

F32 = jnp.float32
BF16 = jnp.bfloat16
BS = pl.BlockSpec
ANY = pl.BlockSpec(memory_space=pl.ANY)
MESH = pl.DeviceIdType.MESH

DM = 1024
DEPTH = 4
EPS = 1e-6
NEG = -1e30
FOX_W = 512
FOX_HD = 64
FOX_H = 8
SC_W = 512
SC_K = 3
AB_IN = 3 * FOX_W + FOX_H + 3 * SC_W
AB_PAD = 3200
LRU_BW = 256
LRU_NB = 4
RG_K = 4
RG_C = 8.0
MEM_H = 4
MEM_HD = 256
D_FF = 2816
NDEV = 8
FFB = 2 * D_FF // NDEV
ADAM_LR, ADAM_B1, ADAM_B2, ADAM_EPS, ADAM_WD, ADAM_STEP = 0.001, 0.9, 0.999, 1e-08, 0.01, 10

LANE = 128
VMEM_LIMIT = 16 * 1024 * 1024
VMEM_MM = 32 * 1024 * 1024
VMEM_BIG = 40 * 1024 * 1024


def _params(ngrid, vmem=None):
    return pltpu.CompilerParams(dimension_semantics=("arbitrary",) * ngrid, vmem_limit_bytes=vmem or VMEM_LIMIT)


def _call(kern, **kwargs):
    return pl.pallas_call(kern, **kwargs)


TK_RED = 2048
TM_SUM = 512
MM_TM = 2048
FFN_TM = 1024


def _tile(n, t):
    return t if n % t == 0 else n


def _mm(name, a, b, *, grid, a_spec, b_spec, o_spec, out_shape, dn, out_dtype=F32, vmem=None):
    nred = grid[-1]
    ngrid = len(grid)

    def kern(a_ref, b_ref, o_ref, *scratch):
        p = lax.dot_general(a_ref[...].astype(BF16), b_ref[...].astype(BF16), (dn, ((), ())),
                            preferred_element_type=F32)
        if nred == 1:
            o_ref[...] = p.astype(o_ref.dtype)
            return
        acc = scratch[0] if scratch else o_ref
        r = pl.program_id(ngrid - 1)

        @pl.when(r == 0)
        def _():
            acc[...] = p

        @pl.when(r > 0)
        def _():
            acc[...] += p

        if scratch:
            @pl.when(r == nred - 1)
            def _():
                o_ref[...] = acc[...].astype(o_ref.dtype)

    blk = tuple(d for d in o_spec.block_shape if d is not None)
    scratch = [pltpu.VMEM(blk, F32)] if (nred > 1 and out_dtype != F32) else []
    return _call(kern, name=name, grid=grid, in_specs=[a_spec, b_spec], out_specs=o_spec,
                          out_shape=jax.ShapeDtypeStruct(out_shape, out_dtype), scratch_shapes=scratch,
                          compiler_params=_params(ngrid, vmem or (VMEM_LIMIT if dn == TN else VMEM_MM)))(a, b)


NN = ((1,), (0,))
NT = ((1,), (1,))
TN = ((0,), (0,))


def _mm_nn(name, a, w, out_dtype=F32, tn=None, vmem=None, tm=MM_TM):
    m, k = a.shape
    n = w.shape[1]
    tm = _tile(m, tm)
    tn = n if tn is None else tn
    return _mm(name, a, w, grid=(m // tm, n // tn, 1), a_spec=BS((tm, k), lambda i, j, r: (i, 0)),
               b_spec=BS((k, tn), lambda i, j, r: (0, j)), o_spec=BS((tm, tn), lambda i, j, r: (i, j)),
               out_shape=(m, n), dn=NN, out_dtype=out_dtype, vmem=vmem)


def _mm_nt_cols(name, a, wt, tn):
    m, k = a.shape
    n = wt.shape[0]
    tm = _tile(m, MM_TM)
    return _mm(name, a, wt, grid=(m // tm, n // tn, 1), a_spec=BS((tm, k), lambda i, j, r: (i, 0)),
               b_spec=BS((tn, k), lambda i, j, r: (j, 0)), o_spec=BS((tm, tn), lambda i, j, r: (i, j)),
               out_shape=(m, n), dn=NT)


def _mm_tn_rows(name, a, b, tk):
    m, k = a.shape
    n = b.shape[1]
    tm = _tile(m, TK_RED)
    return _mm(name, a, b, grid=(k // tk, m // tm), a_spec=BS((tm, tk), lambda j, r: (r, j)),
               b_spec=BS((tm, n), lambda j, r: (r, 0)), o_spec=BS((tk, n), lambda j, r: (j, 0)),
               out_shape=(k, n), dn=TN)


def _mm_nt(name, a, w, out_dtype=F32, tn=None):
    m, n = a.shape
    k = w.shape[0]
    tm = _tile(m, MM_TM)
    tn = n if tn is None else tn
    return _mm(name, a, w, grid=(m // tm, n // tn), a_spec=BS((tm, tn), lambda i, r: (i, r)),
               b_spec=BS((k, tn), lambda i, r: (0, r)), o_spec=BS((tm, k), lambda i, r: (i, 0)),
               out_shape=(m, k), dn=NT, out_dtype=out_dtype)


def _mm_tn(name, a, b, tn=None):
    m, k = a.shape
    n = b.shape[1]
    tm = _tile(m, TK_RED)
    tn = n if tn is None else tn
    return _mm(name, a, b, grid=(n // tn, m // tm), a_spec=BS((tm, k), lambda j, r: (r, 0)),
               b_spec=BS((tm, tn), lambda j, r: (r, j)), o_spec=BS((k, tn), lambda j, r: (0, j)),
               out_shape=(k, n), dn=TN)


def _bmm_nn(name, a, w, out_dtype=F32):
    m, k = a.shape
    g, _, n = w.shape
    tm = _tile(m, MM_TM)
    return _mm(name, a, w, grid=(g, m // tm, 1), a_spec=BS((tm, k), lambda q, i, r: (i, 0)),
               b_spec=BS((None, k, n), lambda q, i, r: (q, 0, 0)), o_spec=BS((None, tm, n), lambda q, i, r: (q, i, 0)),
               out_shape=(g, m, n), dn=NN, out_dtype=out_dtype)


def _bmm_tn(name, a, b):
    m, k = a.shape
    g, _, n = b.shape
    tm = _tile(m, TK_RED)
    return _mm(name, a, b, grid=(g, m // tm), a_spec=BS((tm, k), lambda q, r: (r, 0)),
               b_spec=BS((None, tm, n), lambda q, r: (q, r, 0)), o_spec=BS((None, k, n), lambda q, r: (q, 0, 0)),
               out_shape=(g, k, n), dn=TN)


def _block_sum(name, a, w, dn, out_cols):
    g, m, ac = a.shape
    tm = _tile(m, TM_SUM)

    def kern(a_ref, w_ref, o_ref):
        acc = None
        for q in range(g):
            p = lax.dot_general(a_ref[q].astype(BF16), w_ref[q].astype(BF16), (dn, ((), ())), preferred_element_type=F32)
            acc = p if acc is None else acc + p
        o_ref[...] = acc

    return _call(kern, name=name, grid=(m // tm,),
                 in_specs=[BS((g, tm, ac), lambda i: (0, i, 0)), BS(w.shape, lambda i: (0, 0, 0))],
                 out_specs=BS((tm, out_cols), lambda i: (i, 0)), out_shape=jax.ShapeDtypeStruct((m, out_cols), F32),
                 compiler_params=_params(1, VMEM_BIG))(a, w)


def _bmm_nt_sum(name, a, w):
    return _block_sum(name, a, w, NT, w.shape[1])


def _bmm_nn_sum(name, a, w):
    return _block_sum(name, a, w, NN, w.shape[2])


def _rstd(x):
    return lax.rsqrt(jnp.mean(x * x, axis=-1, keepdims=True) + EPS)


def _norm_fwd(x, g, after=None):
    rows = x.shape[0]
    tm = _tile(rows, 512)

    def kern(x_ref, g_ref, *rest):
        xv = x_ref[...]
        rest[-1][...] = ((xv * _rstd(xv)) * g_ref[...]).astype(BF16)

    extra = () if after is None else (after,)
    return _call(kern, name="norm_fwd", grid=(rows // tm,),
                          in_specs=[BS((tm, DM), lambda i: (i, 0)), BS((1, DM), lambda i: (0, 0))] + [ANY] * len(extra),
                          out_specs=BS((tm, DM), lambda i: (i, 0)),
                          out_shape=jax.ShapeDtypeStruct((rows, DM), BF16), compiler_params=_params(1))(x, g, *extra)


def _norm_res(x, y, g):
    rows = x.shape[0]
    tm = _tile(rows, 512)

    def kern(x_ref, y_ref, g_ref, o_ref):
        yv = y_ref[...]
        o_ref[...] = x_ref[...] + (yv * _rstd(yv)) * g_ref[...]

    row = BS((tm, DM), lambda i: (i, 0))
    return _call(kern, name="norm_res", grid=(rows // tm,),
                          in_specs=[row, row, BS((1, DM), lambda i: (0, 0))], out_specs=row,
                          out_shape=jax.ShapeDtypeStruct((rows, DM), F32), compiler_params=_params(1))(x, y, g)


def _norm_bwd(z, dout, g, resid, out_dtype, after=None):
    rows = z.shape[0]
    tm = _tile(rows, 512)
    has_res = resid is not None

    def kern(*refs):
        z_ref, d_ref, g_ref = refs[:3]
        r_ref = refs[3] if has_res else None
        dz_ref, dg_ref = refs[-2:]
        zv = z_ref[...]
        dv = d_ref[...].astype(F32)
        r = _rstd(zv)
        zh = zv * r
        dzh = dv * g_ref[...]
        dz = r * (dzh - zh * jnp.mean(dzh * zh, axis=-1, keepdims=True))
        if has_res:
            dz = dz + r_ref[...]
        dz_ref[...] = dz.astype(dz_ref.dtype)
        part = jnp.sum(dv * zh, axis=0, keepdims=True)

        @pl.when(pl.program_id(0) == 0)
        def _():
            dg_ref[...] = part

        @pl.when(pl.program_id(0) > 0)
        def _():
            dg_ref[...] += part

    row = BS((tm, DM), lambda i: (i, 0))
    vec = BS((1, DM), lambda i: (0, 0))
    ins = [row, row, vec] + ([row] if has_res else []) + ([ANY] if after is not None else [])
    args = (z, dout, g) + ((resid,) if has_res else ()) + ((after,) if after is not None else ())
    return _call(kern, name="norm_bwd_res" if has_res else "norm_bwd", grid=(rows // tm,), in_specs=ins,
                          out_specs=[row, vec],
                          out_shape=[jax.ShapeDtypeStruct((rows, DM), out_dtype), jax.ShapeDtypeStruct((1, DM), F32)],
                          compiler_params=_params(1))(*args)


def _ffn_up(h, wgu4):
    s = h.shape[0]
    tm = _tile(s, FFN_TM)

    def kern(h_ref, w_ref, gu_ref, a_ref):
        hv = h_ref[...]
        gate = lax.dot_general(hv, w_ref[0], (NT, ((), ())), preferred_element_type=F32)
        up = lax.dot_general(hv, w_ref[1], (NT, ((), ())), preferred_element_type=F32)
        gu_ref[0] = gate.astype(BF16)
        gu_ref[1] = up.astype(BF16)
        a_ref[...] = (gate * jax.nn.sigmoid(gate) * up).astype(BF16)

    return _call(
        kern, name="ffn_up", grid=(4, s // tm),
        in_specs=[BS((tm, DM), lambda j, i: (i, 0)), BS((2, None, FFB, DM), lambda j, i: (0, j, 0, 0))],
        out_specs=[BS((2, None, tm, FFB), lambda j, i: (0, j, i, 0)), BS((None, tm, FFB), lambda j, i: (j, i, 0))],
        out_shape=[jax.ShapeDtypeStruct((2, 4, s, FFB), BF16), jax.ShapeDtypeStruct((4, s, FFB), BF16)],
        compiler_params=_params(2, VMEM_BIG))(h, wgu4)


def _ffn_da(dy, wd4, gu):
    s = dy.shape[0]
    tm = _tile(s, FFN_TM)

    def kern(dy_ref, w_ref, gu_ref, o_ref):
        da = lax.dot_general(dy_ref[...], w_ref[...], (NT, ((), ())), preferred_element_type=F32)
        gate = gu_ref[0].astype(F32)
        up = gu_ref[1].astype(F32)
        sg = jax.nn.sigmoid(gate)
        o_ref[0] = (da * up * (sg * (1.0 + gate * (1.0 - sg)))).astype(BF16)
        o_ref[1] = (da * (gate * sg)).astype(BF16)

    blk = BS((2, None, tm, FFB), lambda j, i: (0, j, i, 0))
    return _call(
        kern, name="ffn_da", grid=(4, s // tm),
        in_specs=[BS((tm, DM), lambda j, i: (i, 0)), BS((None, FFB, DM), lambda j, i: (j, 0, 0)), blk],
        out_specs=blk, out_shape=jax.ShapeDtypeStruct((2, 4, s, FFB), BF16), compiler_params=_params(2, VMEM_BIG))(dy, wd4, gu)


def _ffn_fwd(x, gpre, gpost, wgu, wd):
    h = _norm_fwd(x, gpre)
    gu, a = _ffn_up(h, wgu.reshape(2, 4, FFB, DM))
    y = _bmm_nn_sum("ffn_down", a, wd.reshape(4, FFB, DM))
    return _norm_res(x, y, gpost), (x, h, gu, a, y)


def _ffn_bwd(dxo, saved, gpre, gpost, wgu, wd, after=None):
    x, h, gu, a, y = saved
    s = x.shape[0]
    dy, dgpost = _norm_bwd(y, dxo, gpost, None, BF16, after)
    dgu = _ffn_da(dy, wd.reshape(4, FFB, DM), gu).reshape(8, s, FFB)
    dwd = _bmm_tn_a3("ffn_dwd", a, dy)
    dwgu = _bmm_tn_a3("ffn_dwgu", dgu, h)
    dh = _bmm_nn_sum("ffn_dh", dgu, wgu)
    dx, dgpre = _norm_bwd(x, dh, gpre, dxo, F32)
    return dx, dgpre, dgpost, dwgu, dwd.reshape(D_FF, DM)


def _bmm_tn_a3(name, a, b):
    g, m, k = a.shape
    n = b.shape[1]
    tm = _tile(m, TK_RED)
    return _mm(name, a, b, grid=(g, m // tm), a_spec=BS((None, tm, k), lambda q, r: (q, r, 0)),
               b_spec=BS((tm, n), lambda q, r: (r, 0)), o_spec=BS((None, k, n), lambda q, r: (q, 0, 0)),
               out_shape=(g, k, n), dn=TN)


XATTN_TM = 2048


def _softmax_rows(s):
    m = jnp.max(s, axis=-1, keepdims=True)
    p = jnp.exp(s - m)
    return p / jnp.sum(p, axis=-1, keepdims=True)


def _xattn_fwd_call(h, wq, kv):
    s = h.shape[0]
    mlen = kv.shape[1]
    tm = _tile(s, XATTN_TM)
    scale = MEM_HD ** -0.5

    def kern(h_ref, w_ref, k_ref, v_ref, q_ref, o_ref):
        q = jnp.dot(h_ref[...], w_ref[...], preferred_element_type=F32).astype(BF16)
        q_ref[...] = q
        sc = lax.dot_general(q, k_ref[...], (NT, ((), ())), preferred_element_type=F32) * scale
        p = _softmax_rows(sc)
        o_ref[...] = jnp.dot(p.astype(BF16), v_ref[...], preferred_element_type=F32).astype(BF16)

    blk = BS((tm, MEM_HD), lambda i, hd: (i, hd))
    return _call(
        kern, name="xattn_fwd", grid=(s // tm, MEM_H),
        in_specs=[BS((tm, DM), lambda i, hd: (i, 0)), BS((DM, MEM_HD), lambda i, hd: (0, hd)),
                  BS((None, mlen, MEM_HD), lambda i, hd: (hd, 0, 0)),
                  BS((None, mlen, MEM_HD), lambda i, hd: (MEM_H + hd, 0, 0))],
        out_specs=[blk, blk],
        out_shape=[jax.ShapeDtypeStruct((s, DM), BF16), jax.ShapeDtypeStruct((s, DM), BF16)],
        compiler_params=_params(2, VMEM_MM))(h, wq, kv, kv)


def _xattn_bwd_call(q, kv, do):
    s = q.shape[0]
    mlen = kv.shape[1]
    tm = _tile(s, XATTN_TM)
    scale = MEM_HD ** -0.5

    def kern(q_ref, k_ref, v_ref, do_ref, dq_ref, dkv_ref):
        qv, kvv, vv, dov = q_ref[...], k_ref[...], v_ref[...], do_ref[...]
        sc = lax.dot_general(qv, kvv, (NT, ((), ())), preferred_element_type=F32) * scale
        p = _softmax_rows(sc)
        dp = lax.dot_general(dov, vv, (NT, ((), ())), preferred_element_type=F32)
        ds = (p * (dp - jnp.sum(dp * p, axis=-1, keepdims=True)) * scale).astype(BF16)
        dq_ref[...] = jnp.dot(ds, kvv, preferred_element_type=F32).astype(BF16)
        dk = lax.dot_general(ds, qv, (TN, ((), ())), preferred_element_type=F32)
        dv = lax.dot_general(p.astype(BF16), dov, (TN, ((), ())), preferred_element_type=F32)

        @pl.when(pl.program_id(1) == 0)
        def _():
            dkv_ref[0] = dk
            dkv_ref[1] = dv

        @pl.when(pl.program_id(1) > 0)
        def _():
            dkv_ref[0] += dk
            dkv_ref[1] += dv

    blk = BS((tm, MEM_HD), lambda hd, i: (i, hd))
    return _call(
        kern, name="xattn_bwd", grid=(MEM_H, s // tm),
        in_specs=[blk, BS((None, mlen, MEM_HD), lambda hd, i: (hd, 0, 0)),
                  BS((None, mlen, MEM_HD), lambda hd, i: (MEM_H + hd, 0, 0)), blk],
        out_specs=[blk, BS((2, None, mlen, MEM_HD), lambda hd, i: (0, hd, 0, 0))],
        out_shape=[jax.ShapeDtypeStruct((s, DM), BF16), jax.ShapeDtypeStruct((2, MEM_H, mlen, MEM_HD), F32)],
        compiler_params=_params(2, VMEM_MM))(q, kv, kv, do)


def _cross_fwd(x, mem, gpre, gmem, gpost, wq, wkv, wo, after=None):
    h = _norm_fwd(x, gpre, after)
    mn = _norm_fwd(mem, gmem)
    kv = _bmm_nn("xattn_kv", mn, wkv, BF16)
    q, o = _xattn_fwd_call(h, wq, kv)
    y = _mm_nn("xattn_out", o, wo)
    return _norm_res(x, y, gpost), (x, h, mn, kv, q, o, y)


def _cross_bwd(dxo, saved, mem, gpre, gmem, gpost, wq, wkv, wo, after=None):
    x, h, mn, kv, q, o, y = saved
    mlen = mem.shape[0]
    dy, dgpost = _norm_bwd(y, dxo, gpost, None, BF16, after)
    do = _mm_nt("xattn_do", dy, wo, BF16)
    dwo = _mm_tn("xattn_dwo", o, dy)
    dq, dkv = _xattn_bwd_call(q, kv, do)
    dwq = _mm_tn("xattn_dwq", h, dq)
    dh = _mm_nt("xattn_dh", dq, wq)
    dkv8 = dkv.reshape(8, mlen, MEM_HD)
    dwkv = _bmm_tn("xattn_dwkv", mn, dkv8)
    dmn = _bmm_nt_sum("xattn_dmn", dkv8, wkv)
    _, dgmem = _norm_bwd(mem, dmn, gmem, None, BF16)
    dx, dgpre = _norm_bwd(x, dh, gpre, dxo, F32)
    return dx, dgpre, dgmem, dgpost, dwq, dwkv, dwo


def _log_sigmoid(z):
    return jnp.minimum(z, 0.0) - jnp.log1p(jnp.exp(-jnp.abs(z)))


def _lane_scan_steps():
    return (1, 2, 4, 8, 16, 32, 64)


def _fox_cum(frow, bfb):
    s = frow.shape[1]

    def kern(f_ref, b_ref, o_ref):
        lane = lax.broadcasted_iota(jnp.int32, (FOX_H, LANE), 1)
        carry = jnp.zeros((FOX_H, 1), F32)
        for c in range(s // LANE):
            sl = slice(c * LANE, (c + 1) * LANE)
            lf = _log_sigmoid(f_ref[:, sl] + b_ref[...])
            v = lf
            for d in _lane_scan_steps():
                v = v + jnp.where(lane >= d, pltpu.roll(v, d, 1), 0.0)
            o_ref[:, sl] = v + carry
            carry = carry + jnp.sum(lf, axis=1, keepdims=True)

    return _call(kern, name="fox_cum", out_shape=jax.ShapeDtypeStruct((FOX_H, s), F32),
                          compiler_params=pltpu.CompilerParams(vmem_limit_bytes=VMEM_LIMIT))(frow, bfb)


def _fox_dlogf(dcq, dck, frow, bfb):
    s = frow.shape[1]

    def kern(q_ref, d_ref, f_ref, b_ref, df_ref, db_ref):
        lane = lax.broadcasted_iota(jnp.int32, (FOX_H, LANE), 1)
        carry = jnp.zeros((FOX_H, 1), F32)
        dbf = jnp.zeros((FOX_H, 1), F32)
        for c in reversed(range(s // LANE)):
            sl = slice(c * LANE, (c + 1) * LANE)
            dc = q_ref[:, sl] - d_ref[:, sl]
            v = dc
            for d in _lane_scan_steps():
                v = v + jnp.where(lane < LANE - d, pltpu.roll(v, LANE - d, 1), 0.0)
            v = v + carry
            carry = carry + jnp.sum(dc, axis=1, keepdims=True)
            df = v * jax.nn.sigmoid(-(f_ref[:, sl] + b_ref[...]))
            df_ref[:, sl] = df
            dbf = dbf + jnp.sum(df, axis=1, keepdims=True)
        db_ref[...] = jnp.broadcast_to(dbf, (FOX_H, LANE))

    return _call(kern, name="fox_dlogf",
                          out_shape=[jax.ShapeDtypeStruct((FOX_H, s), F32), jax.ShapeDtypeStruct((FOX_H, LANE), F32)],
                          compiler_params=pltpu.CompilerParams(vmem_limit_bytes=VMEM_LIMIT))(dcq, dck, frow, bfb)


FOX_TQ = 512
Q_COL, K_COL, V_COL = 0, FOX_W // LANE, 2 * FOX_W // LANE
B_COL = 3 * FOX_W // LANE
C_COL = B_COL + SC_W // LANE
U_COL = C_COL + SC_W // LANE


def _bf16_terms(c):
    hi = c.astype(BF16).astype(F32)
    mid = (c - hi).astype(BF16).astype(F32)
    return hi, mid, (c - hi - mid).astype(BF16).astype(F32)


def _fox_operands(qv, kv, cq, ck, lane, hh, scale):
    sel = (lane < FOX_HD) if hh == 0 else (lane >= FOX_HD)
    b0 = FOX_HD if hh == 0 else 0
    qa = jnp.where(sel, qv * scale, 0.0)
    ka = jnp.where(sel, kv, 0.0)
    for n, (tq_, tk_) in enumerate(zip(_bf16_terms(cq), _bf16_terms(ck))):
        qa = jnp.where(lane == b0 + n, tq_, jnp.where(lane == b0 + 3 + n, 1.0, qa))
        ka = jnp.where(lane == b0 + n, 1.0, jnp.where(lane == b0 + 3 + n, -tk_, ka))
    return sel, qa.astype(BF16), ka.astype(BF16)


def _fox_logits(qa, ka, causal):
    sc = lax.dot_general(qa, ka, (NT, ((), ())), preferred_element_type=F32)
    return sc if causal is None else jnp.where(causal, sc, NEG)


def _fox_prep(proj, cumc):
    s = proj.shape[0]
    tp = _tile(s, 512)
    scale = FOX_HD ** -0.5

    def kern(q_ref, k_ref, c_ref, qa_ref, ka_ref):
        lane = lax.broadcasted_iota(jnp.int32, (tp, LANE), 1)
        for hh in range(2):
            _, qa_ref[hh], ka_ref[hh] = _fox_operands(q_ref[...], k_ref[...], c_ref[hh], c_ref[hh], lane, hh, scale)

    pair = BS((2, tp, LANE), lambda hp, i: (hp, i, 0))
    shp = jax.ShapeDtypeStruct((FOX_H, s, LANE), BF16)
    return _call(kern, name="fox_prep", grid=(4, s // tp),
                 in_specs=[BS((tp, LANE), lambda hp, i: (i, Q_COL + hp)), BS((tp, LANE), lambda hp, i: (i, K_COL + hp)), pair],
                 out_specs=[pair, pair], out_shape=[shp, shp], compiler_params=_params(2))(proj, proj, cumc)


def _fox_fwd_call(proj, qa, ka):
    s = proj.shape[0]
    tq = _tile(s, FOX_TQ)
    nq = s // tq

    def kern(qa_ref, ka_ref, v_ref, o_ref, lse_ref, m_s, l_s, acc_s):
        i = pl.program_id(1)
        j = pl.program_id(2)
        lane = lax.broadcasted_iota(jnp.int32, (tq, LANE), 1)

        @pl.when(j == 0)
        def _():
            m_s[...] = jnp.full(m_s.shape, NEG, F32)
            l_s[...] = jnp.zeros(l_s.shape, F32)
            acc_s[...] = jnp.zeros(acc_s.shape, F32)

        def step(diagonal):
            vb = v_ref[...].astype(BF16)
            causal = (lax.broadcasted_iota(jnp.int32, (tq, tq), 0) >= lax.broadcasted_iota(jnp.int32, (tq, tq), 1)
                      if diagonal else None)
            for hh in range(2):
                sc = _fox_logits(qa_ref[hh], ka_ref[hh], causal)
                m_prev = m_s[hh]
                m_new = jnp.maximum(m_prev, jnp.max(sc, axis=-1, keepdims=True))
                alpha = jnp.exp(m_prev - m_new)
                p = jnp.exp(sc - m_new)
                l_s[hh] = alpha * l_s[hh] + jnp.sum(p, axis=-1, keepdims=True)
                acc_s[hh] = alpha * acc_s[hh] + jnp.dot(p.astype(BF16), vb, preferred_element_type=F32)
                m_s[hh] = m_new

        @pl.when(j < i)
        def _():
            step(False)

        @pl.when(j == i)
        def _():
            step(True)
            o_ref[...] = jnp.where(lane < FOX_HD, acc_s[0] / l_s[0], acc_s[1] / l_s[1])
            for hh in range(2):
                lse_ref[hh] = jnp.broadcast_to(m_s[hh] + jnp.log(l_s[hh]), (tq, LANE))

    kvi = lambda hp, i, j: jnp.minimum(j, i)
    return _call(
        kern, name="fox_fwd", grid=(4, nq, nq),
        in_specs=[BS((2, tq, LANE), lambda hp, i, j: (hp, i, 0)),
                  BS((2, tq, LANE), lambda hp, i, j: (hp, kvi(hp, i, j), 0)),
                  BS((tq, LANE), lambda hp, i, j: (kvi(hp, i, j), V_COL + hp))],
        out_specs=[BS((tq, LANE), lambda hp, i, j: (i, hp)), BS((2, tq, LANE), lambda hp, i, j: (hp, i, 0))],
        out_shape=[jax.ShapeDtypeStruct((s, FOX_W), F32), jax.ShapeDtypeStruct((FOX_H, s, LANE), F32)],
        scratch_shapes=[pltpu.VMEM((2, tq, 1), F32), pltpu.VMEM((2, tq, 1), F32), pltpu.VMEM((2, tq, LANE), F32)],
        compiler_params=_params(3))(qa, ka, proj)


ROWSUM_M = 16


def _fox_bwd_call(proj, o, lse, dcat, qa, ka):
    s = proj.shape[0]
    tq = _tile(s, FOX_TQ)
    nq = s // tq
    reps = tq // LANE
    scale = FOX_HD ** -0.5

    def kern(qa_ref, ka_ref, v_ref, do_ref, o_ref, lse_ref, dq_ref, dk_ref, dv_ref, dck_ref, dcq_ref):
        j = pl.program_id(1)
        i = pl.program_id(2)
        lane = lax.broadcasted_iota(jnp.int32, (tq, LANE), 1)
        ones = jnp.ones((ROWSUM_M, tq), BF16)

        @pl.when((j == 0) & (i == 0))
        def _():
            dq_ref[...] = jnp.zeros(dq_ref.shape, F32)
            dcq_ref[...] = jnp.zeros(dcq_ref.shape, F32)

        @pl.when(i == j)
        def _():
            dk_ref[...] = jnp.zeros(dk_ref.shape, F32)
            dv_ref[...] = jnp.zeros(dv_ref.shape, F32)
            dck_ref[...] = jnp.zeros(dck_ref.shape, F32)

        def step(diagonal):
            dov = do_ref[...]
            ov = o_ref[...]
            vb = v_ref[...].astype(BF16)
            causal = (lax.broadcasted_iota(jnp.int32, (tq, tq), 0) >= lax.broadcasted_iota(jnp.int32, (tq, tq), 1)
                      if diagonal else None)
            dq_t = jnp.zeros((tq, LANE), F32)
            dk_t = jnp.zeros((tq, LANE), F32)
            dv_t = jnp.zeros((tq, LANE), F32)
            for hh in range(2):
                sel = (lane < FOX_HD) if hh == 0 else (lane >= FOX_HD)
                qa, ka = qa_ref[hh], ka_ref[hh]
                dom32 = jnp.where(sel, dov, 0.0)
                dom = dom32.astype(BF16)
                sc = _fox_logits(qa, ka, causal)
                p = jnp.exp(sc - jnp.tile(lse_ref[hh], (1, reps)))
                dp = lax.dot_general(dom, vb, (NT, ((), ())), preferred_element_type=F32)
                delta = jnp.sum(dom32 * ov, axis=-1, keepdims=True)
                ds = p * (dp - delta)
                dsb = ds.astype(BF16)
                dq_t = jnp.where(sel, jnp.dot(dsb, ka, preferred_element_type=F32) * scale, dq_t)
                dk_t = jnp.where(sel, lax.dot_general(dsb, qa, (TN, ((), ())), preferred_element_type=F32), dk_t)
                dv_t = dv_t + lax.dot_general(p.astype(BF16), dom, (TN, ((), ())), preferred_element_type=F32)
                dck_ref[hh] += jnp.sum(ds, axis=0, keepdims=True)
                ds_lo = (ds - dsb.astype(F32)).astype(BF16)
                dcq_ref[hh, i] += (lax.dot_general(ones, dsb, (NT, ((), ())), preferred_element_type=F32)
                                   + lax.dot_general(ones, ds_lo, (NT, ((), ())), preferred_element_type=F32))
            rows = pl.ds(pl.multiple_of(i * tq, tq), tq)
            dq_ref[rows, :] += dq_t
            dk_ref[...] += dk_t
            dv_ref[...] += dv_t

        @pl.when(i > j)
        def _():
            step(False)

        @pl.when(i == j)
        def _():
            step(True)

    qi = lambda hp, j, i: jnp.maximum(i, j)
    return _call(
        kern, name="fox_bwd", grid=(4, nq, nq),
        in_specs=[BS((2, tq, LANE), lambda hp, j, i: (hp, qi(hp, j, i), 0)),
                  BS((2, tq, LANE), lambda hp, j, i: (hp, j, 0)),
                  BS((tq, LANE), lambda hp, j, i: (j, V_COL + hp)),
                  BS((tq, LANE), lambda hp, j, i: (qi(hp, j, i), hp)),
                  BS((tq, LANE), lambda hp, j, i: (qi(hp, j, i), hp)),
                  BS((2, tq, LANE), lambda hp, j, i: (hp, qi(hp, j, i), 0))],
        out_specs=[BS((s, LANE), lambda hp, j, i: (0, hp)), BS((tq, LANE), lambda hp, j, i: (j, hp)),
                   BS((tq, LANE), lambda hp, j, i: (j, hp)), BS((2, 1, tq), lambda hp, j, i: (hp, 0, j)),
                   BS((2, nq, ROWSUM_M, tq), lambda hp, j, i: (hp, 0, 0, 0))],
        out_shape=[jax.ShapeDtypeStruct((s, FOX_W), F32), jax.ShapeDtypeStruct((s, FOX_W), F32),
                   jax.ShapeDtypeStruct((s, FOX_W), F32), jax.ShapeDtypeStruct((FOX_H, 1, s), F32),
                   jax.ShapeDtypeStruct((FOX_H, nq, ROWSUM_M, tq), F32)],
        compiler_params=_params(3))(qa, ka, proj, dcat, o, lse)


def _shift_down(v, d, row):
    return jnp.where(row >= d, pltpu.roll(v, d, 0), 0.0)


def _shift_up(v, d, row, n):
    return jnp.where(row < n - d, pltpu.roll(v, n - d, 0), 0.0)


def _sconv_fwd(proj, convw):
    s = proj.shape[0]

    def kern(b_ref, c_ref, u_ref, w_ref, y_ref):
        row = lax.broadcasted_iota(jnp.int32, (s, LANE), 0)
        z = c_ref[...] * u_ref[...]
        conv = w_ref[2:3, :] * z + w_ref[1:2, :] * _shift_down(z, 1, row) + w_ref[0:1, :] * _shift_down(z, 2, row)
        y_ref[...] = (b_ref[...] * conv).astype(BF16)

    col = lambda base: BS((s, LANE), lambda cb: (0, base + cb))
    return _call(kern, name="sconv_fwd", grid=(SC_W // LANE,),
                          in_specs=[col(B_COL), col(C_COL), col(U_COL), BS((SC_K, LANE), lambda cb: (0, cb))],
                          out_specs=BS((s, LANE), lambda cb: (0, cb)),
                          out_shape=jax.ShapeDtypeStruct((s, SC_W), BF16), compiler_params=_params(1))(proj, proj, proj, convw)


def _sconv_bwd(proj, convw, dcat):
    s = proj.shape[0]

    def kern(b_ref, c_ref, u_ref, w_ref, dy_ref, db_ref, dc_ref, du_ref, dw_ref):
        row = lax.broadcasted_iota(jnp.int32, (s, LANE), 0)
        cv, uv, dyv = c_ref[...], u_ref[...], dy_ref[...]
        z = cv * uv
        z1 = _shift_down(z, 1, row)
        z2 = _shift_down(z, 2, row)
        conv = w_ref[2:3, :] * z + w_ref[1:2, :] * z1 + w_ref[0:1, :] * z2
        db_ref[...] = dyv * conv
        dcv = dyv * b_ref[...]
        dz = w_ref[2:3, :] * dcv + w_ref[1:2, :] * _shift_up(dcv, 1, row, s) + w_ref[0:1, :] * _shift_up(dcv, 2, row, s)
        dc_ref[...] = dz * uv
        du_ref[...] = dz * cv
        dw_ref[0:1, :] = jnp.sum(dcv * z2, axis=0, keepdims=True)
        dw_ref[1:2, :] = jnp.sum(dcv * z1, axis=0, keepdims=True)
        dw_ref[2:3, :] = jnp.sum(dcv * z, axis=0, keepdims=True)

    col = lambda base: BS((s, LANE), lambda cb: (0, base + cb))
    out = BS((s, LANE), lambda cb: (0, cb))
    wspec = BS((SC_K, LANE), lambda cb: (0, cb))
    act = jax.ShapeDtypeStruct((s, SC_W), F32)
    return _call(kern, name="sconv_bwd", grid=(SC_W // LANE,),
                          in_specs=[col(B_COL), col(C_COL), col(U_COL), wspec, col(FOX_W // LANE)],
                          out_specs=[out, out, out, wspec],
                          out_shape=[act, act, act, jax.ShapeDtypeStruct((SC_K, SC_W), F32)],
                          compiler_params=_params(1))(proj, proj, proj, convw, dcat)


def _fox_layer_fwd(x, gpre, gpost, wall, bfb, convw, wout, after=None):
    s = x.shape[0]
    h = _norm_fwd(x, gpre, after)
    proj = _mm_nt_cols("fox_proj", h, wall, AB_PAD // 5)
    frow = proj[:, 3 * FOX_W + 3 * SC_W:3 * FOX_W + 3 * SC_W + FOX_H].T
    cumr = _fox_cum(frow, bfb)
    qa, ka = _fox_prep(proj, jnp.broadcast_to(cumr[:, :, None], (FOX_H, s, LANE)))
    o, lse = _fox_fwd_call(proj, qa, ka)
    yb = _sconv_fwd(proj, convw)
    cat = jnp.concatenate([o.astype(BF16), yb], axis=1)
    y = _mm_nn("fox_out", cat, wout)
    return _norm_res(x, y, gpost), (x, h, proj, frow, qa, ka, o, lse, cat, y)


def _fox_layer_bwd(dxo, saved, gpre, gpost, wall, bfb, convw, wout, after=None):
    x, h, proj, frow, qa, ka, o, lse, cat, y = saved
    s = x.shape[0]
    dy, dgpost = _norm_bwd(y, dxo, gpost, None, BF16, after)
    dcat = _mm_nt("fox_dcat", dy, wout)
    dwout = _mm_tn("fox_dwout", cat, dy)
    db, dc, du, dconvw = _sconv_bwd(proj, convw, dcat)
    dq, dk, dv, dck, dcq = _fox_bwd_call(proj, o, lse, dcat, qa, ka)
    dfrow, dbf = _fox_dlogf(dcq[:, :, 0, :].reshape(FOX_H, s), dck.reshape(FOX_H, s), frow, bfb)
    dfcol = jnp.pad(dfrow.T, ((0, 0), (0, LANE - FOX_H)))
    dproj = jnp.concatenate([dq, dk, dv, db, dc, du, dfcol], axis=1).astype(BF16)
    dwall = _mm_tn_rows("fox_dwall", dproj, h, AB_PAD // 5)
    dh = _mm_nn("fox_dh", dproj, wall, vmem=VMEM_BIG, tm=FFN_TM)
    dx, dgpre = _norm_bwd(x, dh, gpre, dxo, F32)
    return dx, dgpre, dgpost, dwall, dbf[:, 0], dconvw, dwout


def _ab_pack(wt):
    nf = 3 * FOX_W
    return jnp.concatenate([wt[:nf], wt[nf + FOX_H:], wt[nf:nf + FOX_H],
                            jnp.zeros((AB_PAD - AB_IN, wt.shape[1]), wt.dtype)], axis=0)


def _ab_unpack(wt):
    nf = 3 * FOX_W
    nbcu = 3 * SC_W
    return jnp.concatenate([wt[:nf], wt[nf + nbcu:nf + nbcu + FOX_H], wt[nf:nf + nbcu]], axis=0)


NCH = DM // LANE
CH_PER_BLK = LRU_BW // LANE


def _chunk_spec(s, lead=0):
    return BS((None, s, LANE), lambda ch: (lead + ch // CH_PER_BLK, 0, ch % CH_PER_BLK))


def _vec_chunk(rows):
    return BS((rows, LANE), lambda ch: (0, ch))


def _neg_expm1(x):
    series = -x * (1.0 + x * (1 / 2) * (1.0 + x * (1 / 3) * (1.0 + x * (1 / 4) * (1.0 + x * (1 / 5) * (
        1.0 + x * (1 / 6) * (1.0 + x * (1 / 7)))))))
    return jnp.where(x > -0.25, series, 1.0 - jnp.exp(x))


def _softplus(z):
    return jnp.maximum(z, 0.0) + jnp.log1p(jnp.exp(-jnp.abs(z)))


GELU_C = math.sqrt(2.0 / math.pi)
GELU_A = 0.044715


def _gelu(x):
    return 0.5 * x * (1.0 + jnp.tanh(GELU_C * (x + GELU_A * x * x * x)))


def _gelu_grad(x):
    t = jnp.tanh(GELU_C * (x + GELU_A * x * x * x))
    return 0.5 * (1.0 + t) + 0.5 * x * (1.0 - t * t) * GELU_C * (1.0 + 3.0 * GELU_A * x * x)


def _lru_conv_fwd(gu, convw, convb):
    s = gu.shape[1]

    def kern(x_ref, w_ref, b_ref, u_ref):
        row = lax.broadcasted_iota(jnp.int32, (s, LANE), 0)
        xv = x_ref[...]
        u_ref[...] = (b_ref[...] + w_ref[3:4, :] * xv + w_ref[2:3, :] * _shift_down(xv, 1, row)
                      + w_ref[1:2, :] * _shift_down(xv, 2, row) + w_ref[0:1, :] * _shift_down(xv, 3, row))

    return _call(kern, name="lru_conv_fwd", grid=(NCH,),
                          in_specs=[_chunk_spec(s, LRU_NB), _vec_chunk(RG_K), _vec_chunk(1)], out_specs=_chunk_spec(s),
                          out_shape=jax.ShapeDtypeStruct((LRU_NB, s, LRU_BW), F32), compiler_params=_params(1))(gu, convw, convb)


def _lru_conv_bwd(dud, dug, gu, convw):
    s = gu.shape[1]

    def kern(d1_ref, d2_ref, x_ref, w_ref, dx_ref, dw_ref, db_ref):
        row = lax.broadcasted_iota(jnp.int32, (s, LANE), 0)
        du = d1_ref[...] + d2_ref[...]
        xv = x_ref[...]
        dx_ref[...] = (w_ref[3:4, :] * du + w_ref[2:3, :] * _shift_up(du, 1, row, s) + w_ref[1:2, :] * _shift_up(du, 2, row, s)
                       + w_ref[0:1, :] * _shift_up(du, 3, row, s)).astype(BF16)
        dw_ref[3:4, :] = jnp.sum(du * xv, axis=0, keepdims=True)
        for k in range(1, RG_K):
            dw_ref[3 - k:4 - k, :] = jnp.sum(du * _shift_down(xv, k, row), axis=0, keepdims=True)
        db_ref[...] = jnp.sum(du, axis=0, keepdims=True)

    return _call(kern, name="lru_conv_bwd", grid=(NCH,),
                          in_specs=[_chunk_spec(s), _chunk_spec(s), _chunk_spec(s, LRU_NB), _vec_chunk(RG_K)],
                          out_specs=[_chunk_spec(s), _vec_chunk(RG_K), _vec_chunk(1)],
                          out_shape=[jax.ShapeDtypeStruct((LRU_NB, s, LRU_BW), BF16),
                                     jax.ShapeDtypeStruct((RG_K, DM), F32), jax.ShapeDtypeStruct((1, DM), F32)],
                          compiler_params=_params(1))(dud, dug, gu, convw)


def _lru_gates(z_ref, bai_ref, lam_ref, uv):
    r = jax.nn.sigmoid(z_ref[0] + bai_ref[0:1, :])
    ig = jax.nn.sigmoid(z_ref[1] + bai_ref[1:2, :])
    sp = _softplus(-lam_ref[...])
    la = -RG_C * r * sp
    a = jnp.exp(la)
    sq = jnp.sqrt(_neg_expm1(2.0 * la))
    return r, ig, sp, a, sq


def _scan_steps(n):
    d, out = 1, []
    while d < n:
        out.append(d)
        d *= 2
    return out


def _lru_scan_fwd(z, bai, lam, u, gu):
    s = u.shape[1]
    zspec = BS((2, None, s, LANE), lambda ch: (0, ch // CH_PER_BLK, 0, ch % CH_PER_BLK))

    def kern(z_ref, bai_ref, lam_ref, u_ref, g_ref, hs_ref, y_ref):
        row = lax.broadcasted_iota(jnp.int32, (s, LANE), 0)
        uv = u_ref[...]
        _, ig, _, a, sq = _lru_gates(z_ref, bai_ref, lam_ref, uv)
        b = sq * (ig * uv)
        for d in _scan_steps(s):
            a_sh = jnp.where(row >= d, pltpu.roll(a, d, 0), 1.0)
            b = a * _shift_down(b, d, row) + b
            a = a * a_sh
        hs_ref[...] = b
        y_ref[...] = (_gelu(g_ref[...]) * b).astype(BF16)

    return _call(kern, name="lru_scan_fwd", grid=(NCH,),
                          in_specs=[zspec, _vec_chunk(2), _vec_chunk(1), _chunk_spec(s), _chunk_spec(s)],
                          out_specs=[_chunk_spec(s), BS((s, LANE), lambda ch: (0, ch))],
                          out_shape=[jax.ShapeDtypeStruct((LRU_NB, s, LRU_BW), F32), jax.ShapeDtypeStruct((s, DM), BF16)],
                          compiler_params=_params(1, VMEM_BIG))(z, bai, lam, u, gu)


def _lru_scan_bwd(dyp, z, bai, lam, u, gu, hs):
    s = u.shape[1]
    zspec = BS((2, None, s, LANE), lambda ch: (0, ch // CH_PER_BLK, 0, ch % CH_PER_BLK))

    def kern(dy_ref, z_ref, bai_ref, lam_ref, u_ref, g_ref, hs_ref, dg_ref, dz_ref, du_ref, dbai_ref, dlam_ref):
        row = lax.broadcasted_iota(jnp.int32, (s, LANE), 0)
        uv, gv, hv, dyv = u_ref[...], g_ref[...], hs_ref[...], dy_ref[...]
        r, ig, sp, a, sq = _lru_gates(z_ref, bai_ref, lam_ref, uv)
        dg_ref[...] = (dyv * hv * _gelu_grad(gv)).astype(BF16)
        g = dyv * _gelu(gv)
        an = _shift_up(a, 1, row, s)
        for d in _scan_steps(s):
            an_sh = jnp.where(row < s - d, pltpu.roll(an, s - d, 0), 1.0)
            g = an * _shift_up(g, d, row, s) + g
            an = an * an_sh
        da = g * _shift_down(hv, 1, row)
        dsq = g * (ig * uv)
        di = g * sq * uv
        du_ref[...] = g * sq * ig
        dla = da * a - dsq * (a * a / sq)
        dzr = dla * (-RG_C * sp) * r * (1.0 - r)
        dzi = di * ig * (1.0 - ig)
        dz_ref[0] = dzr.astype(BF16)
        dz_ref[1] = dzi.astype(BF16)
        dbai_ref[0:1, :] = jnp.sum(dzr, axis=0, keepdims=True)
        dbai_ref[1:2, :] = jnp.sum(dzi, axis=0, keepdims=True)
        dlam_ref[...] = jnp.sum(dla * r, axis=0, keepdims=True) * (RG_C * jax.nn.sigmoid(-lam_ref[...]))

    return _call(
        kern, name="lru_scan_bwd", grid=(NCH,),
        in_specs=[BS((s, LANE), lambda ch: (0, ch)), zspec, _vec_chunk(2), _vec_chunk(1), _chunk_spec(s), _chunk_spec(s),
                  _chunk_spec(s)],
        out_specs=[_chunk_spec(s), zspec, _chunk_spec(s), _vec_chunk(2), _vec_chunk(1)],
        out_shape=[jax.ShapeDtypeStruct((LRU_NB, s, LRU_BW), BF16), jax.ShapeDtypeStruct((2, LRU_NB, s, LRU_BW), BF16),
                   jax.ShapeDtypeStruct((LRU_NB, s, LRU_BW), F32), jax.ShapeDtypeStruct((2, DM), F32),
                   jax.ShapeDtypeStruct((1, DM), F32)],
        compiler_params=_params(1, VMEM_BIG))(dyp, z, bai, lam, u, gu, hs)


def _lru_layer_fwd(x, gpre, gpost, win, convw, convb, wai, bai, lam, wout, after=None):
    s = x.shape[0]
    tm = _tile(s, MM_TM)
    h = _norm_fwd(x, gpre, after)
    gu = _bmm_nn("lru_in", h, win)
    u = _lru_conv_fwd(gu, convw, convb)
    z = _mm("lru_gate", u, wai, grid=(2, LRU_NB, s // tm, 1),
            a_spec=BS((None, tm, LRU_BW), lambda k, n, i, r: (n, i, 0)),
            b_spec=BS((None, None, LRU_BW, LRU_BW), lambda k, n, i, r: (k, n, 0, 0)),
            o_spec=BS((None, None, tm, LRU_BW), lambda k, n, i, r: (k, n, i, 0)),
            out_shape=(2, LRU_NB, s, LRU_BW), dn=NN)
    hs, yp = _lru_scan_fwd(z, bai, lam, u, gu)
    y = _mm_nn("lru_out", yp, wout)
    return _norm_res(x, y, gpost), (x, h, gu, u, z, hs, yp, y)


def _lru_layer_bwd(dxo, saved, gpre, gpost, win, convw, convb, wai, bai, lam, wout, after=None):
    x, h, gu, u, z, hs, yp, y = saved
    s = x.shape[0]
    tm = _tile(s, MM_TM)
    dy, dgpost = _norm_bwd(y, dxo, gpost, None, BF16, after)
    dyp = _mm_nt("lru_dyp", dy, wout)
    dwout = _mm_tn("lru_dwout", yp, dy)
    dgate, dz, dud, dbai, dlam = _lru_scan_bwd(dyp, z, bai, lam, u, gu, hs)
    dwai = _mm("lru_dwai", u, dz, grid=(2, LRU_NB, s // tm),
               a_spec=BS((None, tm, LRU_BW), lambda k, n, r: (n, r, 0)),
               b_spec=BS((None, None, tm, LRU_BW), lambda k, n, r: (k, n, r, 0)),
               o_spec=BS((None, None, LRU_BW, LRU_BW), lambda k, n, r: (k, n, 0, 0)),
               out_shape=(2, LRU_NB, LRU_BW, LRU_BW), dn=TN)
    dug = _mm("lru_dug", dz, wai, grid=(LRU_NB, s // tm, 2),
              a_spec=BS((None, None, tm, LRU_BW), lambda n, i, k: (k, n, i, 0)),
              b_spec=BS((None, None, LRU_BW, LRU_BW), lambda n, i, k: (k, n, 0, 0)),
              o_spec=BS((None, tm, LRU_BW), lambda n, i, k: (n, i, 0)),
              out_shape=(LRU_NB, s, LRU_BW), dn=NT)
    duraw, dconvw, dconvb = _lru_conv_bwd(dud, dug, gu, convw)
    dgu = jnp.concatenate([dgate, duraw], axis=0)
    dwin = _bmm_tn("lru_dwin", h, dgu)
    dh = _bmm_nt_sum("lru_dh", dgu, win)
    dx, dgpre = _norm_bwd(x, dh, gpre, dxo, F32)
    return dx, dgpre, dgpost, dwin, dconvw, dconvb, dwai, dbai, dlam, dwout


CHIP_FLIPS = ((1, 0), (0, 1), (1, 1))


def _place():
    return lax.axis_index("x"), lax.axis_index("y"), lax.axis_index("c")


def _flip(v, f):
    return 1 - v if f else v


def _comm_params():
    return pltpu.CompilerParams(vmem_limit_bytes=VMEM_LIMIT)


def _small_gather(v):
    def body(v_ref, o_ref, send_sems, recv_sems, local_sem):
        x, y, c = _place()
        mine = 4 * x + 2 * y + c
        local = pltpu.make_async_copy(v_ref, o_ref.at[mine], local_sem)
        local.start()
        sends = []
        for k in range(1, NDEV):
            fx, fy, fc = (k >> 2) & 1, (k >> 1) & 1, k & 1
            sends.append(pltpu.make_async_remote_copy(
                src_ref=v_ref, dst_ref=o_ref.at[mine], send_sem=send_sems.at[k - 1], recv_sem=recv_sems.at[k - 1],
                device_id=(_flip(x, fx), _flip(y, fy), _flip(c, fc)), device_id_type=MESH))
        for cp in sends:
            cp.start()
        for k in range(1, NDEV):
            fx, fy, fc = (k >> 2) & 1, (k >> 1) & 1, k & 1
            src = 4 * _flip(x, fx) + 2 * _flip(y, fy) + _flip(c, fc)
            pltpu.make_async_remote_copy(src_ref=v_ref, dst_ref=o_ref.at[src], send_sem=send_sems.at[k - 1],
                                         recv_sem=recv_sems.at[k - 1], device_id=(x, y, c), device_id_type=MESH).wait_recv()
        for cp in sends:
            cp.wait_send()
        local.wait()

    return pl.pallas_call(body, name="small_gather", in_specs=[ANY], out_specs=ANY,
                          out_shape=jax.ShapeDtypeStruct((NDEV,) + v.shape, v.dtype),
                          scratch_shapes=[pltpu.SemaphoreType.DMA((NDEV - 1,)), pltpu.SemaphoreType.DMA((NDEV - 1,)),
                                          pltpu.SemaphoreType.DMA],
                          compiler_params=_comm_params())(v)


REL_CHIPS = ((0, 0),) + CHIP_FLIPS


def _rs_d2d(g5s, after=None):
    n = len(g5s)
    extra = () if after is None else (after,)

    def body(*refs):
        ins, gots = refs[:n], refs[n + len(extra):2 * n + len(extra)]
        send_sems, recv_sems = refs[2 * n + len(extra):]
        x, y, c = _place()
        copies = []
        for t in range(n):
            for f, (fx, fy) in enumerate(REL_CHIPS):
                copies.append(pltpu.make_async_remote_copy(
                    src_ref=ins[t].at[_flip(x, fx), _flip(y, fy), 1 - c], dst_ref=gots[t].at[f],
                    send_sem=send_sems.at[4 * t + f], recv_sem=recv_sems.at[4 * t + f], device_id=(x, y, 1 - c),
                    device_id_type=MESH))
        for cp in copies:
            cp.start()
        for cp in copies:
            cp.wait()

    out = [jax.ShapeDtypeStruct((4,) + g.shape[3:], F32) for g in g5s]
    return pl.pallas_call(body, name="rs_d2d", in_specs=[ANY] * (n + len(extra)), out_specs=[ANY] * n, out_shape=out,
                          scratch_shapes=[pltpu.SemaphoreType.DMA((4 * n,)), pltpu.SemaphoreType.DMA((4 * n,))],
                          compiler_params=_comm_params())(*g5s, *extra)


HBM = pl.BlockSpec(memory_space=pltpu.HBM)
SEM = pl.BlockSpec(memory_space=pltpu.SEMAPHORE)
EFFECT = pltpu.SideEffectType.DATAFLOW_SIDE_EFFECTING


def _in_hbm(a):
    return pltpu.with_memory_space_constraint(a, pltpu.HBM)


def _rs_ici_copies(ins, lands, send_sems, recv_sems):
    x, y, c = _place()
    return [pltpu.make_async_remote_copy(
        src_ref=ins[t].at[f], dst_ref=lands[t].at[f], send_sem=send_sems.at[3 * t + f], recv_sem=recv_sems.at[3 * t + f],
        device_id=(_flip(x, fx), _flip(y, fy), c), device_id_type=MESH)
        for t in range(len(ins)) for f, (fx, fy) in enumerate(CHIP_FLIPS)]


def _rs_ici_start(parts, name):
    n = len(parts)

    def body(*refs):
        ins, lands = refs[:n], refs[n:2 * n]
        send_sems, recv_sems = refs[2 * n], refs[2 * n + 1]
        token = refs[-1]
        for cp in _rs_ici_copies(ins, lands, send_sems, recv_sems):
            cp.start()
        token[...] = jnp.zeros(token.shape, token.dtype)

    thru = [pltpu.HBM(p.shape, p.dtype) for p in parts]
    res = pl.pallas_call(
        body, name=name, in_specs=[HBM] * (2 * n),
        out_shape=(pltpu.SemaphoreType.DMA((3 * n,)), pltpu.SemaphoreType.DMA((3 * n,)), *thru, *thru,
                   jax.ShapeDtypeStruct((8, LANE), F32)),
        out_specs=(SEM, SEM, *([HBM] * (2 * n)), pl.BlockSpec(memory_space=pltpu.VMEM)),
        input_output_aliases={i: 2 + i for i in range(2 * n)},
        compiler_params=pltpu.CompilerParams(has_side_effects=EFFECT),
    )(*[_in_hbm(p) for p in parts], *[_in_hbm(lax.empty(p.shape, p.dtype)) for p in parts])
    return res[:-1], res[-1]


def _rs_ici_wait(state, after, name):
    n = (len(state) - 2) // 2

    def body(*refs):
        send_sems, recv_sems = refs[0], refs[1]
        ins, lands = refs[2:2 + n], refs[2 + n:2 + 2 * n]
        for cp in _rs_ici_copies(ins, lands, send_sems, recv_sems):
            cp.wait_send()
            cp.wait_recv()

    thru = [pltpu.HBM(s.shape, s.dtype) for s in state[2:]]
    res = pl.pallas_call(
        body, name=name, in_specs=[SEM, SEM] + [HBM] * (2 * n) + [ANY], out_shape=tuple(thru),
        out_specs=tuple([HBM] * (2 * n)), input_output_aliases={2 + i: i for i in range(2 * n)},
        compiler_params=pltpu.CompilerParams(has_side_effects=EFFECT),
    )(*state, after)
    return list(res[n:])


def _ag_copies(shards, lands, send_sems, recv_sems):
    x, y, c = _place()
    mine = 4 * x + 2 * y + c
    peers = [(x, y, 1 - c)] + [(_flip(x, fx), _flip(y, fy), c) for fx, fy in CHIP_FLIPS]
    return [pltpu.make_async_remote_copy(
        src_ref=shards[t], dst_ref=lands[t].at[mine], send_sem=send_sems.at[4 * t + k], recv_sem=recv_sems.at[4 * t + k],
        device_id=peer, device_id_type=MESH) for t in range(len(shards)) for k, peer in enumerate(peers)]


def _ag_start(shards, after, name):
    n = len(shards)

    def body(*refs):
        ins, lands = refs[:n], refs[n:2 * n]
        send_sems, recv_sems = refs[2 * n + 1], refs[2 * n + 2]
        token = refs[-1]
        for cp in _ag_copies(ins, lands, send_sems, recv_sems):
            cp.start()
        token[...] = jnp.zeros(token.shape, token.dtype)

    thru = [pltpu.HBM(s.shape, s.dtype) for s in shards]
    land = [pltpu.HBM((NDEV,) + s.shape, s.dtype) for s in shards]
    res = pl.pallas_call(
        body, name=name, in_specs=[HBM] * (2 * n) + [ANY],
        out_shape=(pltpu.SemaphoreType.DMA((4 * n,)), pltpu.SemaphoreType.DMA((4 * n,)), *thru, *land,
                   jax.ShapeDtypeStruct((8, LANE), F32)),
        out_specs=(SEM, SEM, *([HBM] * (2 * n)), pl.BlockSpec(memory_space=pltpu.VMEM)),
        input_output_aliases={i: 2 + i for i in range(2 * n)},
        compiler_params=pltpu.CompilerParams(has_side_effects=EFFECT),
    )(*[_in_hbm(s) for s in shards], *[_in_hbm(lax.empty((NDEV,) + s.shape, s.dtype)) for s in shards], after)
    return res[:-1], res[-1]


def _ag_wait(state, after, name):
    n = (len(state) - 2) // 2

    def body(*refs):
        send_sems, recv_sems = refs[0], refs[1]
        ins, lands = refs[2:2 + n], refs[2 + n:2 + 2 * n]
        for cp in _ag_copies(ins, lands, send_sems, recv_sems):
            cp.wait_send()
            cp.wait_recv()

    thru = [pltpu.HBM(s.shape, s.dtype) for s in state[2:]]
    res = pl.pallas_call(
        body, name=name, in_specs=[SEM, SEM] + [HBM] * (2 * n) + [ANY], out_shape=tuple(thru),
        out_specs=tuple([HBM] * (2 * n)), input_output_aliases={2 + i: i for i in range(2 * n)},
        compiler_params=pltpu.CompilerParams(has_side_effects=EFFECT),
    )(*state, after)
    return list(res[:n]), list(res[n:])


def _ag_finish(shards, lands):
    n = len(shards)

    def body(*refs):
        ins, outs, stage = refs[:n], refs[2 * n:3 * n], refs[3 * n:4 * n]
        send_sems, recv_sems, local_sems = refs[4 * n:]
        x, y, c = _place()
        chips = [(_flip(x, fx), _flip(y, fy)) for fx, fy in CHIP_FLIPS]

        def passing(t, j, core, to):
            blk = outs[t].at[4 * chips[j][0] + 2 * chips[j][1] + core]
            return pltpu.make_async_remote_copy(src_ref=blk, dst_ref=blk, send_sem=send_sems.at[3 * t + j],
                                                recv_sem=recv_sems.at[3 * t + j], device_id=to, device_id_type=MESH)

        sends = [passing(t, j, c, (x, y, 1 - c)) for t in range(n) for j in range(3)]
        for cp in sends:
            cp.start()
        load = [pltpu.make_async_copy(ins[t], stage[t], local_sems.at[t]) for t in range(n)]
        mine = [pltpu.make_async_copy(stage[t], outs[t].at[4 * x + 2 * y + c], local_sems.at[t]) for t in range(n)]
        for cp in load:
            cp.start()
        for t in range(n):
            load[t].wait()
            mine[t].start()
        for t in range(n):
            for j in range(3):
                passing(t, j, 1 - c, (x, y, c)).wait_recv()
        for cp in sends:
            cp.wait_send()
        for cp in mine:
            cp.wait()

    return pl.pallas_call(
        body, name="ag_finish", in_specs=[ANY] * (2 * n), out_specs=[ANY] * n,
        out_shape=[jax.ShapeDtypeStruct(l.shape, l.dtype) for l in lands],
        input_output_aliases={n + i: i for i in range(n)},
        scratch_shapes=[pltpu.VMEM(s.shape, s.dtype) for s in shards]
        + [pltpu.SemaphoreType.DMA((3 * n,)), pltpu.SemaphoreType.DMA((3 * n,)), pltpu.SemaphoreType.DMA((n,))],
        compiler_params=_comm_params())(*shards, *lands)


def _row_tile(rows, largest=256):
    for t in (1024, 512, 256, 128, 64, 32, 16, 8):
        if t > largest:
            continue
        if rows % t == 0:
            return t
    return rows


def _rs_chip_sum(pos, g5, got):
    a, b = g5.shape[3:]
    ta = _row_tile(a, 1024)

    def kern(pos_ref, o_ref, g_ref, p_ref):
        p_ref[...] = (o_ref[...] + g_ref[...]).astype(BF16)

    def mine(f, i, pos_ref):
        return (pos_ref[0] ^ ((f + 1) & 1), pos_ref[1] ^ ((f + 1) >> 1), pos_ref[2], i, 0)

    spec = pltpu.PrefetchScalarGridSpec(
        num_scalar_prefetch=1, grid=(3, a // ta),
        in_specs=[BS((None, None, None, ta, b), mine), BS((None, ta, b), lambda f, i, pos_ref: (f + 1, i, 0))],
        out_specs=BS((None, ta, b), lambda f, i, pos_ref: (f, i, 0)))
    return _call(kern, name="rs_chip_sum", grid_spec=spec, out_shape=jax.ShapeDtypeStruct((3, a, b), BF16),
                          compiler_params=_params(2))(pos, g5, got)


def _rs_final_sum(pos, g5, got, recv):
    a, b = g5.shape[3:]
    ta = _row_tile(a, 1024)

    def kern(pos_ref, o_ref, g_ref, r_ref, s_ref):
        acc = o_ref[...] + g_ref[...]
        for f in range(3):
            acc = acc + r_ref[f].astype(F32)
        s_ref[...] = acc

    spec = pltpu.PrefetchScalarGridSpec(
        num_scalar_prefetch=1, grid=(a // ta,),
        in_specs=[BS((None, None, None, ta, b), lambda i, pos_ref: (pos_ref[0], pos_ref[1], pos_ref[2], i, 0)),
                  BS((None, ta, b), lambda i, pos_ref: (0, i, 0)), BS((3, ta, b), lambda i, pos_ref: (0, i, 0))],
        out_specs=BS((ta, b), lambda i, pos_ref: (i, 0)))
    return _call(kern, name="rs_final_sum", grid_spec=spec, out_shape=jax.ShapeDtypeStruct((a, b), F32),
                          compiler_params=_params(1))(pos, g5, got, recv)


def _rs_d2d_copies(ins, lands, send_sems, recv_sems):
    x, y, c = _place()
    return [pltpu.make_async_remote_copy(
        src_ref=ins[t].at[_flip(x, fx), _flip(y, fy), 1 - c], dst_ref=lands[t].at[f], send_sem=send_sems.at[4 * t + f],
        recv_sem=recv_sems.at[4 * t + f], device_id=(x, y, 1 - c), device_id_type=MESH)
        for t in range(len(ins)) for f, (fx, fy) in enumerate(REL_CHIPS)]


def _rs_d2d_start(g5s, name):
    n = len(g5s)

    def body(*refs):
        ins, lands = refs[:n], refs[n:2 * n]
        for cp in _rs_d2d_copies(ins, lands, refs[2 * n], refs[2 * n + 1]):
            cp.start()
        refs[-1][...] = jnp.zeros(refs[-1].shape, F32)

    thru = [pltpu.HBM(g.shape, g.dtype) for g in g5s]
    land = [pltpu.HBM((4,) + g.shape[3:], F32) for g in g5s]
    res = pl.pallas_call(
        body, name=name, in_specs=[HBM] * (2 * n),
        out_shape=(pltpu.SemaphoreType.DMA((4 * n,)), pltpu.SemaphoreType.DMA((4 * n,)), *thru, *land,
                   jax.ShapeDtypeStruct((8, LANE), F32)),
        out_specs=(SEM, SEM, *([HBM] * (2 * n)), pl.BlockSpec(memory_space=pltpu.VMEM)),
        input_output_aliases={i: 2 + i for i in range(2 * n)},
        compiler_params=pltpu.CompilerParams(has_side_effects=EFFECT),
    )(*[_in_hbm(g) for g in g5s], *[_in_hbm(lax.empty((4,) + g.shape[3:], F32)) for g in g5s])
    return res[:-1], res[-1]


def _rs_d2d_wait(state, after, name):
    n = (len(state) - 2) // 2

    def body(*refs):
        ins, lands = refs[2:2 + n], refs[2 + n:2 + 2 * n]
        for cp in _rs_d2d_copies(ins, lands, refs[0], refs[1]):
            cp.wait_send()
            cp.wait_recv()

    thru = [pltpu.HBM(s.shape, s.dtype) for s in state[2:]]
    res = pl.pallas_call(
        body, name=name, in_specs=[SEM, SEM] + [HBM] * (2 * n) + [ANY], out_shape=tuple(thru),
        out_specs=tuple([HBM] * (2 * n)), input_output_aliases={2 + i: i for i in range(2 * n)},
        compiler_params=pltpu.CompilerParams(has_side_effects=EFFECT),
    )(*state, after)
    return list(res[:n]), list(res[n:])


def _as_g5(grads):
    return [g.reshape((2, 2, 2) + g.shape[1:]) for g in grads]


def _rs_mid(g5s, gots, pos, tag):
    parts = [_rs_chip_sum(pos, g, got) for g, got in zip(g5s, gots)]
    state, token = _rs_ici_start(parts, "rs_ici_start_" + tag)
    return (g5s, gots, state, tag), token


def _rs_begin(grads, pos, tag, after=None):
    g5s = _as_g5(grads)
    return _rs_mid(g5s, _rs_d2d(g5s, after), pos, tag)


def _rs_end(pending, after, pos):
    g5s, gots, state, tag = pending
    recvs = _rs_ici_wait(state, after, "rs_ici_wait_" + tag)
    return [_rs_final_sum(pos, g, got, r) for g, got, r in zip(g5s, gots, recvs)]


def _sum_devices(v):
    _, r, _ = v.shape

    def kern(v_ref, o_ref):
        acc = v_ref[0]
        for d in range(1, NDEV):
            acc = acc + v_ref[d]
        o_ref[...] = acc

    return _call(kern, name="sum_devices", out_shape=jax.ShapeDtypeStruct((r, LANE), F32),
                          compiler_params=_comm_params())(v)


def _loss_head(xf, target):
    s = xf.shape[0]
    tm = _tile(s, 512)

    def kern(x_ref, t_ref, dx_ref, l_ref):
        err = x_ref[...] - t_ref[...]
        dx_ref[...] = err * (1.0 / DM)
        part = jnp.broadcast_to(0.5 * jnp.sum(jnp.mean(err * err, axis=-1, keepdims=True), axis=0, keepdims=True), (8, LANE))

        @pl.when(pl.program_id(0) == 0)
        def _():
            l_ref[...] = part

        @pl.when(pl.program_id(0) > 0)
        def _():
            l_ref[...] += part

    row = BS((tm, DM), lambda i: (i, 0))
    return _call(kern, name="loss_head", grid=(s // tm,), in_specs=[row, row],
                          out_specs=[row, BS((8, LANE), lambda i: (0, 0))],
                          out_shape=[jax.ShapeDtypeStruct((s, DM), F32), jax.ShapeDtypeStruct((8, LANE), F32)],
                          compiler_params=_params(1))(xf, target)


def _adamw(w, g, m, v, after=None):
    rows, cols = w.shape
    tr = _row_tile(rows, 512)
    extra = () if after is None else (after,)

    def kern(w_ref, g_ref, m_ref, v_ref, *rest):
        d_ref, nm_ref, nv_ref = rest[-3:]
        gv = g_ref[...]
        nm = ADAM_B1 * m_ref[...] + (1.0 - ADAM_B1) * gv
        nv = ADAM_B2 * v_ref[...] + (1.0 - ADAM_B2) * (gv * gv)
        m_hat = nm / (1.0 - ADAM_B1 ** ADAM_STEP)
        v_hat = nv / (1.0 - ADAM_B2 ** ADAM_STEP)
        d_ref[...] = -ADAM_LR * (m_hat / (jnp.sqrt(v_hat) + ADAM_EPS) + ADAM_WD * w_ref[...])
        nm_ref[...] = nm
        nv_ref[...] = nv

    blk = BS((tr, cols), lambda i: (i, 0))
    shp = jax.ShapeDtypeStruct((rows, cols), F32)
    return _call(kern, name="adamw", grid=(rows // tr,), in_specs=[blk] * 4 + [ANY] * len(extra),
                          out_specs=[blk] * 3, out_shape=[shp] * 3, compiler_params=_params(1, VMEM_MM))(w, g, m, v, *extra)


def _adamw_nd(w, g, m, v, after=None):
    shape = w.shape
    two = (math.prod(shape[:-1]), shape[-1])
    return tuple(o.reshape(shape)
                 for o in _adamw(w.reshape(two), g.reshape(two), m.reshape(two), v.reshape(two), after))


def _pack_small(parts):
    flat = jnp.concatenate([p.reshape(-1) for p in parts])
    pad = (-flat.shape[0]) % (8 * LANE)
    return jnp.pad(flat, (0, pad)).reshape(-1, LANE)


def _unpack_small(packed, shapes, lead=()):
    flat = packed.reshape(lead + (-1,))
    out, off = [], 0
    for shp in shapes:
        n = math.prod(shp)
        out.append(flat[..., off:off + n].reshape(lead + tuple(shp)))
        off += n
    return out


WEIGHT_NAMES = ('g_mix_pre', 'g_mix_post', 'g_cross_pre', 'g_mem', 'g_cross_post', 'g_ffn_pre', 'g_ffn_post', 'w_xq',
                'w_xkv', 'w_xo', 'w_ffn_gu', 'w_ffn_down', 'ab_w_in', 'ab_b_f', 'ab_conv_w', 'ab_w_out', 'c_w_in',
                'c_conv_w', 'c_conv_b', 'c_w_a', 'c_b_a', 'c_w_i', 'c_b_i', 'c_lam', 'c_w_out')
BIG = ('w_xq', 'w_xkv', 'w_xo', 'w_ffn_gu', 'w_ffn_down', 'ab_w_in', 'ab_w_out', 'c_w_in', 'c_w_a', 'c_w_i', 'c_w_out')
SMALL_SHARDED = ('ab_conv_w', 'c_conv_w', 'c_conv_b', 'c_b_a', 'c_b_i', 'c_lam')
REPLICATED = ('g_mix_pre', 'g_mix_post', 'g_cross_pre', 'g_mem', 'g_cross_post', 'g_ffn_pre', 'g_ffn_post', 'ab_b_f')


def _small_full(name, gathered):
    nd = gathered.ndim
    return jnp.moveaxis(gathered, 0, nd - 2).reshape(gathered.shape[1:-1] + (NDEV * gathered.shape[-1],))


def _small_shard(full, dev):
    c = full.shape[-1] // NDEV
    return lax.dynamic_slice_in_dim(full, dev * c, c, axis=full.ndim - 1)


def kernel(x, mem, g_mix_pre, g_mix_post, g_cross_pre, g_mem, g_cross_post, g_ffn_pre, g_ffn_post, w_xq, w_xkv, w_xo, w_ffn_gu, w_ffn_down, ab_w_in, ab_b_f, ab_conv_w, ab_w_out, c_w_in, c_conv_w, c_conv_b, c_w_a, c_b_a, c_w_i, c_b_i, c_lam, c_w_out, loss_target, m_g_mix_pre, m_g_mix_post, m_g_cross_pre, m_g_mem, m_g_cross_post, m_g_ffn_pre, m_g_ffn_post, m_w_xq, m_w_xkv, m_w_xo, m_w_ffn_gu, m_w_ffn_down, m_ab_w_in, m_ab_b_f, m_ab_conv_w, m_ab_w_out, m_c_w_in, m_c_conv_w, m_c_conv_b, m_c_w_a, m_c_b_a, m_c_w_i, m_c_b_i, m_c_lam, m_c_w_out, v_g_mix_pre, v_g_mix_post, v_g_cross_pre, v_g_mem, v_g_cross_post, v_g_ffn_pre, v_g_ffn_post, v_w_xq, v_w_xkv, v_w_xo, v_w_ffn_gu, v_w_ffn_down, v_ab_w_in, v_ab_b_f, v_ab_conv_w, v_ab_w_out, v_c_w_in, v_c_conv_w, v_c_conv_b, v_c_w_a, v_c_b_a, v_c_w_i, v_c_b_i, v_c_lam, v_c_w_out):
    args = locals()
    w = {n: args[n] for n in WEIGHT_NAMES}
    mom = {n: args["m_" + n] for n in WEIGHT_NAMES}
    var = {n: args["v_" + n] for n in WEIGHT_NAMES}
    for t in (w, mom, var):
        t['w_ffn_gu'] = t['w_ffn_gu'].transpose(0, 2, 1)
    ab_t = [t['ab_w_in'].transpose(2, 0, 1) for t in (w, mom, var)]
    pos = jnp.stack([lax.axis_index("x"), lax.axis_index("y"), lax.axis_index("c")]).astype(jnp.int32)
    dev = 4 * pos[0] + 2 * pos[1] + pos[2]
    xs, mems, target = x[0], mem[0], loss_target[0]
    n_even, n_odd = (DEPTH + 1) // 2, DEPTH // 2

    small_shapes = [w[n].shape for n in SMALL_SHARDED]
    small_w_all = _small_gather(_pack_small([w[n] for n in SMALL_SHARDED]))
    gathered_small = _unpack_small(small_w_all, small_shapes, (NDEV,))
    small = {n: _small_full(n, g) for n, g in zip(SMALL_SHARDED, gathered_small)}
    ab_bfb = jnp.broadcast_to(ab_b_f[:, :, None], (n_even, FOX_H, LANE))
    c_bai = jnp.stack([small['c_b_a'].reshape(n_odd, DM), small['c_b_i'].reshape(n_odd, DM)], axis=1)
    row = lambda a, l: a[l][None]

    REST = ('w_xq', 'w_xkv', 'w_xo', 'w_ffn_gu', 'w_ffn_down')

    def mixer_names(l):
        return ('ab_w_in', 'ab_w_out') if l % 2 == 0 else ('c_w_in', 'c_w_a', 'c_w_i', 'c_w_out')

    def shards_of(l, names):
        out = []
        for n in names:
            if n == 'ab_w_in':
                s = ab_t[0][:, l // 2].astype(BF16)
            else:
                s = w[n][l if w[n].shape[0] == DEPTH else l // 2].astype(BF16)
            out.append(s.reshape(-1, s.shape[-1]))
        return out

    def mixer_weights(l, full):
        if l % 2 == 0:
            e = l // 2
            return (row(g_mix_pre, l), row(g_mix_post, l), _ab_pack(full['ab_w_in'].reshape(AB_IN, DM)), ab_bfb[e],
                    small['ab_conv_w'][e], full['ab_w_out'].reshape(DM, DM))
        o = l // 2
        gate_w = lambda g: g.reshape(NDEV, LRU_NB, LRU_BW // NDEV, LRU_BW).transpose(1, 0, 2, 3).reshape(
            LRU_NB, LRU_BW, LRU_BW)
        return (row(g_mix_pre, l), row(g_mix_post, l), full['c_w_in'], small['c_conv_w'][o], row(small['c_conv_b'], o),
                jnp.stack([gate_w(full['c_w_a']), gate_w(full['c_w_i'])]), c_bai[o], row(small['c_lam'], o),
                full['c_w_out'].reshape(DM, DM))

    def rest_weights(l, full):
        cross = (row(g_cross_pre, l), row(g_mem, l), row(g_cross_post, l), full['w_xq'].reshape(DM, DM), full['w_xkv'],
                 full['w_xo'].reshape(DM, DM))
        ffn = (row(g_ffn_pre, l), row(g_ffn_post, l), full['w_ffn_gu'], full['w_ffn_down'].reshape(D_FF, DM))
        return cross, ffn

    def gathered(state, names, after, tag):
        shards, lands = _ag_wait(state, after, "ag_wait_" + tag)
        full = _ag_finish(shards, lands)
        return dict(zip(names, full)), full[0]

    saved, weights = [], []
    h = xs
    names_of = lambda l: mixer_names(l) + REST
    states = {}
    st_m, _ = _ag_start(shards_of(0, mixer_names(0)), small_w_all, "ag_start_0m")
    st_r, _ = _ag_start(shards_of(0, REST), st_m[2], "ag_start_0r")
    states[1], token = _ag_start(shards_of(1, names_of(1)), st_r[2], "ag_start_1")
    full_m, _ = gathered(st_m, mixer_names(0), xs, "0m")
    for l in range(DEPTH):
        if l > 0:
            full, done = gathered(states[l], names_of(l), h, str(l))
            full_m = full_r = full
            token = None
            if l + 2 < DEPTH:
                states[l + 2], token = _ag_start(shards_of(l + 2, names_of(l + 2)), done, "ag_start_%d" % (l + 2))
        mixer = mixer_weights(l, full_m)
        h, s_mix = (_fox_layer_fwd if l % 2 == 0 else _lru_layer_fwd)(h, *mixer, after=token)
        token = None
        if l == 0:
            full_r, done = gathered(st_r, REST, h, "0r")
            states[2], token = _ag_start(shards_of(2, names_of(2)), done, "ag_start_2")
        cross, ffn = rest_weights(l, full_r)
        h, s_cross = _cross_fwd(h, mems, *cross, after=token)
        h, s_ffn = _ffn_fwd(h, *ffn)
        saved.append((s_mix, s_cross, s_ffn))
        weights.append((mixer, cross, ffn))
    mixer_args = lambda l: weights[l][0]
    cross_args = lambda l: weights[l][1]
    ffn_args = lambda l: weights[l][2]
    dx, loss_rep = _loss_head(h, target)
    loss = lax.psum(loss_rep[0, 0], ("x", "y", "c"))

    grads = {n: [None] * w[n].shape[0] for n in BIG}
    partial = {n: [None] * w[n].shape[0] for n in REPLICATED + SMALL_SHARDED}
    def finish(pending, after):
        state, names, where = pending
        for n, g in zip(names, _rs_end(state, after, pos)):
            grads[n][where[n]] = g

    def unit(layer, names):
        return [layer[n][1] for n in names], names, {n: layer[n][0] for n in names}

    d2d = ici = None
    token = None
    for l in reversed(range(DEPTH)):
        s_mix, s_cross, s_ffn = saved[l]
        dx, partial['g_ffn_pre'][l], partial['g_ffn_post'][l], dwgu, dwd = _ffn_bwd(dx, s_ffn, *ffn_args(l), after=token)
        token = None
        if d2d is not None:
            g5s, gots = _rs_d2d_wait(d2d[0], dx, "rs_d2d_wait_%d" % (l + 1))
            state, token = _rs_mid(g5s, gots, pos, str(l + 1))
            ici, d2d = (state,) + d2d[1:], None
        (dx, partial['g_cross_pre'][l], partial['g_mem'][l], partial['g_cross_post'][l], dwq, dwkv, dwo) = _cross_bwd(
            dx, s_cross, mems, *cross_args(l), after=token)
        token = None
        layer = {'w_xq': (l, dwq.reshape(NDEV, DM // NDEV, DM)), 'w_xkv': (l, dwkv), 'w_xo': (l, dwo.reshape(NDEV, DM // NDEV, DM)),
                 'w_ffn_gu': (l, dwgu), 'w_ffn_down': (l, dwd.reshape(NDEV, D_FF // NDEV, DM))}
        if l == 0:
            gs, names, where = unit(layer, REST)
            state, token = _rs_begin(gs, pos, "0r")
            ici_rest = (state, names, where)
        if l % 2 == 0:
            e = l // 2
            (dx, partial['g_mix_pre'][l], partial['g_mix_post'][l], dwall, partial['ab_b_f'][e], partial['ab_conv_w'][e],
             dwout) = _fox_layer_bwd(dx, s_mix, *mixer_args(l), after=token)
            layer['ab_w_in'] = (e, _ab_unpack(dwall).reshape(NDEV, AB_IN // NDEV, DM))
            layer['ab_w_out'] = (e, dwout.reshape(NDEV, DM // NDEV, DM))
        else:
            o = l // 2
            (dx, partial['g_mix_pre'][l], partial['g_mix_post'][l], dwin, partial['c_conv_w'][o], dconvb, dwai, dbai, dlam,
             dwout) = _lru_layer_bwd(dx, s_mix, *mixer_args(l), after=token)
            partial['c_conv_b'][o], partial['c_lam'][o] = dconvb[0], dlam[0]
            partial['c_b_a'][o], partial['c_b_i'][o] = dbai[0].reshape(LRU_NB, LRU_BW), dbai[1].reshape(LRU_NB, LRU_BW)
            rows = LRU_BW // NDEV
            by_dev = lambda d: d.reshape(LRU_NB, NDEV, rows, LRU_BW).transpose(1, 0, 2, 3).reshape(NDEV, LRU_NB * rows, LRU_BW)
            layer['c_w_in'] = (o, dwin)
            layer['c_w_a'] = (o, by_dev(dwai[0]))
            layer['c_w_i'] = (o, by_dev(dwai[1]))
            layer['c_w_out'] = (o, dwout.reshape(NDEV, DM // NDEV, DM))
        token = None
        if ici is not None:
            finish(ici, dx)
            ici = None
        if l > 0:
            gs, names, where = unit(layer, list(layer))
            state, token = _rs_d2d_start(_as_g5(gs), "rs_d2d_start_%d" % l)
            d2d = (state, names, where)
    small_names = REPLICATED + SMALL_SHARDED
    small_parts = [jnp.stack([p.reshape(w[n].shape[1:] if n in REPLICATED else small[n].shape[1:]) for p in partial[n]])
                   for n in small_names]
    small_all = _small_gather(_pack_small(small_parts))
    reduced = _unpack_small(_sum_devices(small_all), [p.shape for p in small_parts])
    grad = {}
    for n, g in zip(small_names, reduced):
        grad[n] = g if n in REPLICATED else _small_shard(g, dev)

    gs, names, where = unit(layer, mixer_names(0))
    state, token = _rs_begin(gs, pos, "0m", after=small_all)
    ici_mixer = (state, names, where)
    finish(ici_rest, dx)

    delta, new_m, new_v = {}, {}, {}
    last = mixer_names(0)
    for n in BIG:
        if n not in last:
            grad[n] = jnp.stack(grads[n]).reshape(w[n].shape)
            delta[n], new_m[n], new_v[n] = _adamw_nd(w[n], grad[n], mom[n], var[n], token)
            token = delta[n]
    shapes = [w[n].shape for n in small_names]
    packed = [_pack_small([t[n] for n in small_names]) for t in (w, grad, mom, var)]
    res_small = _adamw(*packed, after=token)
    for res, out in zip(res_small, (delta, new_m, new_v)):
        for n, val in zip(small_names, _unpack_small(res, shapes)):
            out[n] = val
    finish(ici_mixer, res_small[0])
    for n in last:
        if n == 'ab_w_in':
            g_t = jnp.stack(grads[n], axis=1)
            res = (g_t,) + _adamw_nd(ab_t[0], g_t, ab_t[1], ab_t[2])
            grad[n], delta[n], new_m[n], new_v[n] = (r.transpose(1, 2, 0) for r in res)
            continue
        grad[n] = jnp.stack(grads[n]).reshape(w[n].shape)
        delta[n], new_m[n], new_v[n] = _adamw_nd(w[n], grad[n], mom[n], var[n])

    for t in (grad, delta, new_m, new_v):
        t['w_ffn_gu'] = t['w_ffn_gu'].transpose(0, 2, 1)
    return (loss, dx[None], *[grad[n] for n in WEIGHT_NAMES], *[delta[n] for n in WEIGHT_NAMES],
            *[new_m[n] for n in WEIGHT_NAMES], *[new_v[n] for n in WEIGHT_NAMES])
```
